```python
import jax
import jax.numpy as jnp
from jax import lax
import numpy as np


D_MODEL = 2048
BATCH = 8
SEQ = 4096
DEPTH = 1

MEM_LEN = 256
BLOCK = 128
EPS = 1e-6
NEG_INF = -1e30

SWA_Q_HEADS = 16
SWA_KV_HEADS = 2
SWA_HEAD_DIM = 64
WINDOW = 128

MLA_HEADS = 4
MLA_Q_RANK = 512
MLA_KV_RANK = 512
MLA_NOPE_DIM = 128
MLA_ROPE_DIM = 64
MLA_V_DIM = 128
ROPE_THETA = 10000.0

MEM_HEADS = 4
MEM_HEAD_DIM = 128

MIX_WIDTH = SWA_Q_HEADS * SWA_HEAD_DIM + MLA_HEADS * MLA_V_DIM + MEM_HEADS * MEM_HEAD_DIM
IN_SIZES = (SWA_Q_HEADS * SWA_HEAD_DIM, SWA_KV_HEADS * SWA_HEAD_DIM, SWA_KV_HEADS * SWA_HEAD_DIM,
            MLA_Q_RANK, MLA_KV_RANK, MLA_ROPE_DIM, MEM_HEADS * MEM_HEAD_DIM)
IN_WIDTH = sum(IN_SIZES)
IN_SPLITS = tuple(int(v) for v in np.cumsum(IN_SIZES)[:-1])

D_FF = ((8 * D_MODEL + 3 * 256 - 1) // (3 * 256)) * 256

kernel_name = "hymba_swa_sink_mla_memory_swiglu"


def rms_norm(x, g):
    xf = x.astype(jnp.float32)
    y = xf * lax.rsqrt(jnp.mean(xf * xf, axis=-1, keepdims=True) + EPS)
    return (y * g.astype(jnp.float32)).astype(x.dtype)


def alibi_slopes(n):
    return 2.0 ** (-8.0 * jnp.arange(1, n + 1, dtype=jnp.float32) / n)


def apply_rope(x, cos, sin):
    x1, x2 = jnp.split(x.astype(jnp.float32), 2, axis=-1)
    return jnp.concatenate([x1 * cos - x2 * sin, x1 * sin + x2 * cos], axis=-1).astype(x.dtype)


def swa_sink_attention(q, k, v, pos, sinks):
    b, s, _, d = q.shape
    nb = s // BLOCK
    g = SWA_Q_HEADS // SWA_KV_HEADS
    qb = q.reshape(b, nb, BLOCK, SWA_KV_HEADS, g, d)

    def with_prev(t):
        tb = t.reshape((b, nb, BLOCK) + t.shape[2:])
        prev = jnp.concatenate([jnp.zeros_like(tb[:, :1]), tb[:, :-1]], axis=1)
        return jnp.concatenate([prev, tb], axis=2)

    kb, vb, pk = with_prev(k), with_prev(v), with_prev(pos)
    pq = pos.reshape(b, nb, BLOCK)
    scores = jnp.einsum('bnqhgd,bnkhd->bnhgqk', qb, kb,
                        preferred_element_type=jnp.float32) * (d ** -0.5)
    dist = jnp.abs(pq[:, :, :, None] - pk[:, :, None, :]).astype(jnp.float32)
    slopes = alibi_slopes(SWA_Q_HEADS).reshape(SWA_KV_HEADS, g)
    scores = scores - slopes[None, None, :, :, None, None] * dist[:, :, None, None]
    qi = jnp.arange(BLOCK)[:, None] + BLOCK
    ki = jnp.arange(2 * BLOCK)[None, :]
    band = (ki <= qi) & (qi - ki < WINDOW)
    not_first = jnp.arange(nb)[:, None, None] > 0
    valid = band[None] & (not_first | (ki >= BLOCK)[None])
    scores = jnp.where(valid[None, :, None, None], scores, NEG_INF)
    sink = sinks.astype(jnp.float32).reshape(SWA_KV_HEADS, g)[None, None, :, :, None, None]
    m = jnp.maximum(jnp.max(scores, axis=-1, keepdims=True), sink)
    p = jnp.exp(scores - m)
    p = p / (jnp.sum(p, axis=-1, keepdims=True) + jnp.exp(sink - m))
    out = jnp.einsum('bnhgqk,bnkhd->bnqhgd', p.astype(v.dtype), vb)
    return out.reshape(b, s, SWA_Q_HEADS * d)


def mla_causal_attention(q_nope, q_rope, k_nope, k_rope, v):
    b, s, h, _ = q_nope.shape
    nb = s // BLOCK
    scale = (MLA_NOPE_DIM + MLA_ROPE_DIM) ** -0.5
    k_idx = jnp.arange(s)

    def to_blocks(t):
        return jnp.moveaxis(t.reshape((b, nb, BLOCK) + t.shape[2:]), 1, 0)

    def one_block(args):
        qn, qr, i = args
        sc = (jnp.einsum('bqhd,bkhd->bhqk', qn, k_nope, preferred_element_type=jnp.float32)
              + jnp.einsum('bqhd,bkd->bhqk', qr, k_rope, preferred_element_type=jnp.float32)) * scale
        q_idx = i * BLOCK + jnp.arange(BLOCK)
        sc = jnp.where(k_idx[None, :] <= q_idx[:, None], sc, NEG_INF)
        p = jax.nn.softmax(sc, axis=-1).astype(v.dtype)
        return jnp.einsum('bhqk,bkhd->bqhd', p, v)

    out = lax.map(one_block, (to_blocks(q_nope), to_blocks(q_rope), jnp.arange(nb)))
    return jnp.moveaxis(out, 0, 1).reshape(b, s, h * MLA_V_DIM)


def memory_cross_attention(q, k, v):
    b, s, h, d = q.shape
    sc = jnp.einsum('bshd,bmhd->bhsm', q, k, preferred_element_type=jnp.float32) * (d ** -0.5)
    p = jax.nn.softmax(sc, axis=-1).astype(v.dtype)
    return jnp.einsum('bhsm,bmhd->bshd', p, v).reshape(b, s, h * d)


def _fwd_setup_inputs(seed: int = 0) -> dict:
    key = jax.random.key(seed)
    ks = iter(jax.random.split(key, 32))

    def nrm(shape, fan_in):
        return jax.random.normal(next(ks), shape, jnp.float32) * (fan_in ** -0.5)

    def gain(n):
        return 1.0 + 0.02 * jax.random.normal(next(ks), (DEPTH, n), jnp.float32)

    x = jax.random.normal(next(ks), (BATCH, SEQ, D_MODEL), jnp.float32)
    mem = jax.random.normal(next(ks), (BATCH, MEM_LEN, D_MODEL), jnp.float32)
    offsets = jax.random.randint(next(ks), (BATCH, 1), 0, 1024, dtype=jnp.int32)
    positions = offsets + jnp.arange(SEQ, dtype=jnp.int32)[None, :]
    return {
        "x": x,
        "mem": mem,
        "positions": positions,
        "attn_norm_g": gain(D_MODEL),
        "w_in": nrm((DEPTH, D_MODEL, IN_WIDTH), D_MODEL),
        "swa_q_norm_g": gain(SWA_HEAD_DIM),
        "swa_k_norm_g": gain(SWA_HEAD_DIM),
        "swa_sinks": 0.5 * jax.random.normal(next(ks), (DEPTH, SWA_Q_HEADS), jnp.float32),
        "mla_cq_norm_g": gain(MLA_Q_RANK),
        "mla_ckv_norm_g": gain(MLA_KV_RANK),
        "w_uq": nrm((DEPTH, MLA_Q_RANK, MLA_HEADS * (MLA_NOPE_DIM + MLA_ROPE_DIM)), MLA_Q_RANK),
        "w_ukv": nrm((DEPTH, MLA_KV_RANK, MLA_HEADS * (MLA_NOPE_DIM + MLA_V_DIM)), MLA_KV_RANK),
        "mla_qn_norm_g": gain(MLA_NOPE_DIM),
        "mla_qr_norm_g": gain(MLA_ROPE_DIM),
        "mla_kn_norm_g": gain(MLA_NOPE_DIM),
        "mla_kr_norm_g": gain(MLA_ROPE_DIM),
        "mem_norm_g": gain(D_MODEL),
        "w_mem_kv": nrm((DEPTH, D_MODEL, 2 * MEM_HEADS * MEM_HEAD_DIM), D_MODEL),
        "mem_q_norm_g": gain(MEM_HEAD_DIM),
        "mem_k_norm_g": gain(MEM_HEAD_DIM),
        "w_out": nrm((DEPTH, MIX_WIDTH, D_MODEL), MIX_WIDTH),
        "ffn_norm_g": gain(D_MODEL),
        "w_gate": nrm((DEPTH, D_MODEL, D_FF), D_MODEL),
        "w_up": nrm((DEPTH, D_MODEL, D_FF), D_MODEL),
        "w_down": nrm((DEPTH, D_FF, D_MODEL), D_FF),
    }


def _fwd_reference(x, mem, positions, attn_norm_g, w_in, swa_q_norm_g, swa_k_norm_g, swa_sinks,
              mla_cq_norm_g, mla_ckv_norm_g, w_uq, w_ukv, mla_qn_norm_g, mla_qr_norm_g,
              mla_kn_norm_g, mla_kr_norm_g, mem_norm_g, w_mem_kv, mem_q_norm_g, mem_k_norm_g,
              w_out, ffn_norm_g, w_gate, w_up, w_down):
    b, s, _ = x.shape
    m_len = mem.shape[1]
    inv_freq = ROPE_THETA ** (-jnp.arange(0, MLA_ROPE_DIM, 2, dtype=jnp.float32) / MLA_ROPE_DIM)
    ang = positions.astype(jnp.float32)[..., None] * inv_freq
    cos, sin = jnp.cos(ang), jnp.sin(ang)
    h = x
    for l in range(DEPTH):
        hn = rms_norm(h, attn_norm_g[l])
        proj = hn @ w_in[l]
        q_a, k_a, v_a, c_q, c_kv, k_r, q_m = jnp.split(proj, IN_SPLITS, axis=-1)

        q_a = rms_norm(q_a.reshape(b, s, SWA_Q_HEADS, SWA_HEAD_DIM), swa_q_norm_g[l])
        k_a = rms_norm(k_a.reshape(b, s, SWA_KV_HEADS, SWA_HEAD_DIM), swa_k_norm_g[l])
        v_a = v_a.reshape(b, s, SWA_KV_HEADS, SWA_HEAD_DIM)
        y_a = swa_sink_attention(q_a, k_a, v_a, positions, swa_sinks[l])

        q_b = (rms_norm(c_q, mla_cq_norm_g[l]) @ w_uq[l]).reshape(
            b, s, MLA_HEADS, MLA_NOPE_DIM + MLA_ROPE_DIM)
        kv_b = (rms_norm(c_kv, mla_ckv_norm_g[l]) @ w_ukv[l]).reshape(
            b, s, MLA_HEADS, MLA_NOPE_DIM + MLA_V_DIM)
        q_nope = rms_norm(q_b[..., :MLA_NOPE_DIM], mla_qn_norm_g[l])
        q_rope = apply_rope(rms_norm(q_b[..., MLA_NOPE_DIM:], mla_qr_norm_g[l]),
                            cos[:, :, None], sin[:, :, None])
        k_nope = rms_norm(kv_b[..., :MLA_NOPE_DIM], mla_kn_norm_g[l])
        v_b = kv_b[..., MLA_NOPE_DIM:]
        k_rope = apply_rope(rms_norm(k_r, mla_kr_norm_g[l]), cos, sin)
        y_b = mla_causal_attention(q_nope, q_rope, k_nope, k_rope, v_b)

        q_m = rms_norm(q_m.reshape(b, s, MEM_HEADS, MEM_HEAD_DIM), mem_q_norm_g[l])
        kv_m = (rms_norm(mem, mem_norm_g[l]) @ w_mem_kv[l]).reshape(
            b, m_len, 2, MEM_HEADS, MEM_HEAD_DIM)
        k_m = rms_norm(kv_m[:, :, 0], mem_k_norm_g[l])
        v_m = kv_m[:, :, 1]
        y_m = memory_cross_attention(q_m, k_m, v_m)

        h = h + jnp.concatenate([y_a, y_b, y_m], axis=-1) @ w_out[l]

        fn = rms_norm(h, ffn_norm_g[l])
        h = h + (jax.nn.silu(fn @ w_gate[l]) * (fn @ w_up[l])) @ w_down[l]
    return h


import jax as _jax
import jax.numpy as _jnp

TWIN_FORMAT = 'train_step'
FWD_PARAMS = ['x', 'mem', 'positions', 'attn_norm_g', 'w_in', 'swa_q_norm_g', 'swa_k_norm_g', 'swa_sinks', 'mla_cq_norm_g', 'mla_ckv_norm_g', 'w_uq', 'w_ukv', 'mla_qn_norm_g', 'mla_qr_norm_g', 'mla_kn_norm_g', 'mla_kr_norm_g', 'mem_norm_g', 'w_mem_kv', 'mem_q_norm_g', 'mem_k_norm_g', 'w_out', 'ffn_norm_g', 'w_gate', 'w_up', 'w_down']
TWIN_WEIGHTS = ['attn_norm_g', 'w_in', 'swa_q_norm_g', 'swa_k_norm_g', 'swa_sinks', 'mla_cq_norm_g', 'mla_ckv_norm_g', 'w_uq', 'w_ukv', 'mla_qn_norm_g', 'mla_qr_norm_g', 'mla_kn_norm_g', 'mla_kr_norm_g', 'mem_norm_g', 'w_mem_kv', 'mem_q_norm_g', 'mem_k_norm_g', 'w_out', 'ffn_norm_g', 'w_gate', 'w_up', 'w_down']
TWIN_DIFF_INPUT = 'x'
TWIN_INPUTS = ['x', 'mem', 'positions', 'attn_norm_g', 'w_in', 'swa_q_norm_g', 'swa_k_norm_g', 'swa_sinks', 'mla_cq_norm_g', 'mla_ckv_norm_g', 'w_uq', 'w_ukv', 'mla_qn_norm_g', 'mla_qr_norm_g', 'mla_kn_norm_g', 'mla_kr_norm_g', 'mem_norm_g', 'w_mem_kv', 'mem_q_norm_g', 'mem_k_norm_g', 'w_out', 'ffn_norm_g', 'w_gate', 'w_up', 'w_down', 'loss_target', 'm_attn_norm_g', 'm_w_in', 'm_swa_q_norm_g', 'm_swa_k_norm_g', 'm_swa_sinks', 'm_mla_cq_norm_g', 'm_mla_ckv_norm_g', 'm_w_uq', 'm_w_ukv', 'm_mla_qn_norm_g', 'm_mla_qr_norm_g', 'm_mla_kn_norm_g', 'm_mla_kr_norm_g', 'm_mem_norm_g', 'm_w_mem_kv', 'm_mem_q_norm_g', 'm_mem_k_norm_g', 'm_w_out', 'm_ffn_norm_g', 'm_w_gate', 'm_w_up', 'm_w_down', 'v_attn_norm_g', 'v_w_in', 'v_swa_q_norm_g', 'v_swa_k_norm_g', 'v_swa_sinks', 'v_mla_cq_norm_g', 'v_mla_ckv_norm_g', 'v_w_uq', 'v_w_ukv', 'v_mla_qn_norm_g', 'v_mla_qr_norm_g', 'v_mla_kn_norm_g', 'v_mla_kr_norm_g', 'v_mem_norm_g', 'v_w_mem_kv', 'v_mem_q_norm_g', 'v_mem_k_norm_g', 'v_w_out', 'v_ffn_norm_g', 'v_w_gate', 'v_w_up', 'v_w_down']
TWIN_OUTPUTS = ['loss', 'grad_x', 'grad_attn_norm_g', 'grad_w_in', 'grad_swa_q_norm_g', 'grad_swa_k_norm_g', 'grad_swa_sinks', 'grad_mla_cq_norm_g', 'grad_mla_ckv_norm_g', 'grad_w_uq', 'grad_w_ukv', 'grad_mla_qn_norm_g', 'grad_mla_qr_norm_g', 'grad_mla_kn_norm_g', 'grad_mla_kr_norm_g', 'grad_mem_norm_g', 'grad_w_mem_kv', 'grad_mem_q_norm_g', 'grad_mem_k_norm_g', 'grad_w_out', 'grad_ffn_norm_g', 'grad_w_gate', 'grad_w_up', 'grad_w_down', 'delta_attn_norm_g', 'delta_w_in', 'delta_swa_q_norm_g', 'delta_swa_k_norm_g', 'delta_swa_sinks', 'delta_mla_cq_norm_g', 'delta_mla_ckv_norm_g', 'delta_w_uq', 'delta_w_ukv', 'delta_mla_qn_norm_g', 'delta_mla_qr_norm_g', 'delta_mla_kn_norm_g', 'delta_mla_kr_norm_g', 'delta_mem_norm_g', 'delta_w_mem_kv', 'delta_mem_q_norm_g', 'delta_mem_k_norm_g', 'delta_w_out', 'delta_ffn_norm_g', 'delta_w_gate', 'delta_w_up', 'delta_w_down', 'new_m_attn_norm_g', 'new_m_w_in', 'new_m_swa_q_norm_g', 'new_m_swa_k_norm_g', 'new_m_swa_sinks', 'new_m_mla_cq_norm_g', 'new_m_mla_ckv_norm_g', 'new_m_w_uq', 'new_m_w_ukv', 'new_m_mla_qn_norm_g', 'new_m_mla_qr_norm_g', 'new_m_mla_kn_norm_g', 'new_m_mla_kr_norm_g', 'new_m_mem_norm_g', 'new_m_w_mem_kv', 'new_m_mem_q_norm_g', 'new_m_mem_k_norm_g', 'new_m_w_out', 'new_m_ffn_norm_g', 'new_m_w_gate', 'new_m_w_up', 'new_m_w_down', 'new_v_attn_norm_g', 'new_v_w_in', 'new_v_swa_q_norm_g', 'new_v_swa_k_norm_g', 'new_v_swa_sinks', 'new_v_mla_cq_norm_g', 'new_v_mla_ckv_norm_g', 'new_v_w_uq', 'new_v_w_ukv', 'new_v_mla_qn_norm_g', 'new_v_mla_qr_norm_g', 'new_v_mla_kn_norm_g', 'new_v_mla_kr_norm_g', 'new_v_mem_norm_g', 'new_v_w_mem_kv', 'new_v_mem_q_norm_g', 'new_v_mem_k_norm_g', 'new_v_w_out', 'new_v_ffn_norm_g', 'new_v_w_gate', 'new_v_w_up', 'new_v_w_down']
TWIN_LEAF_KINDS = {'loss': 'loss', 'grad_x': 'grad_x', 'grad_attn_norm_g': 'grad_w', 'grad_w_in': 'grad_w', 'grad_swa_q_norm_g': 'grad_w', 'grad_swa_k_norm_g': 'grad_w', 'grad_swa_sinks': 'grad_w', 'grad_mla_cq_norm_g': 'grad_w', 'grad_mla_ckv_norm_g': 'grad_w', 'grad_w_uq': 'grad_w', 'grad_w_ukv': 'grad_w', 'grad_mla_qn_norm_g': 'grad_w', 'grad_mla_qr_norm_g': 'grad_w', 'grad_mla_kn_norm_g': 'grad_w', 'grad_mla_kr_norm_g': 'grad_w', 'grad_mem_norm_g': 'grad_w', 'grad_w_mem_kv': 'grad_w', 'grad_mem_q_norm_g': 'grad_w', 'grad_mem_k_norm_g': 'grad_w', 'grad_w_out': 'grad_w', 'grad_ffn_norm_g': 'grad_w', 'grad_w_gate': 'grad_w', 'grad_w_up': 'grad_w', 'grad_w_down': 'grad_w', 'delta_attn_norm_g': 'delta_w', 'delta_w_in': 'delta_w', 'delta_swa_q_norm_g': 'delta_w', 'delta_swa_k_norm_g': 'delta_w', 'delta_swa_sinks': 'delta_w', 'delta_mla_cq_norm_g': 'delta_w', 'delta_mla_ckv_norm_g': 'delta_w', 'delta_w_uq': 'delta_w', 'delta_w_ukv': 'delta_w', 'delta_mla_qn_norm_g': 'delta_w', 'delta_mla_qr_norm_g': 'delta_w', 'delta_mla_kn_norm_g': 'delta_w', 'delta_mla_kr_norm_g': 'delta_w', 'delta_mem_norm_g': 'delta_w', 'delta_w_mem_kv': 'delta_w', 'delta_mem_q_norm_g': 'delta_w', 'delta_mem_k_norm_g': 'delta_w', 'delta_w_out': 'delta_w', 'delta_ffn_norm_g': 'delta_w', 'delta_w_gate': 'delta_w', 'delta_w_up': 'delta_w', 'delta_w_down': 'delta_w', 'new_m_attn_norm_g': 'new_m', 'new_m_w_in': 'new_m', 'new_m_swa_q_norm_g': 'new_m', 'new_m_swa_k_norm_g': 'new_m', 'new_m_swa_sinks': 'new_m', 'new_m_mla_cq_norm_g': 'new_m', 'new_m_mla_ckv_norm_g': 'new_m', 'new_m_w_uq': 'new_m', 'new_m_w_ukv': 'new_m', 'new_m_mla_qn_norm_g': 'new_m', 'new_m_mla_qr_norm_g': 'new_m', 'new_m_mla_kn_norm_g': 'new_m', 'new_m_mla_kr_norm_g': 'new_m', 'new_m_mem_norm_g': 'new_m', 'new_m_w_mem_kv': 'new_m', 'new_m_mem_q_norm_g': 'new_m', 'new_m_mem_k_norm_g': 'new_m', 'new_m_w_out': 'new_m', 'new_m_ffn_norm_g': 'new_m', 'new_m_w_gate': 'new_m', 'new_m_w_up': 'new_m', 'new_m_w_down': 'new_m', 'new_v_attn_norm_g': 'new_v', 'new_v_w_in': 'new_v', 'new_v_swa_q_norm_g': 'new_v', 'new_v_swa_k_norm_g': 'new_v', 'new_v_swa_sinks': 'new_v', 'new_v_mla_cq_norm_g': 'new_v', 'new_v_mla_ckv_norm_g': 'new_v', 'new_v_w_uq': 'new_v', 'new_v_w_ukv': 'new_v', 'new_v_mla_qn_norm_g': 'new_v', 'new_v_mla_qr_norm_g': 'new_v', 'new_v_mla_kn_norm_g': 'new_v', 'new_v_mla_kr_norm_g': 'new_v', 'new_v_mem_norm_g': 'new_v', 'new_v_w_mem_kv': 'new_v', 'new_v_mem_q_norm_g': 'new_v', 'new_v_mem_k_norm_g': 'new_v', 'new_v_w_out': 'new_v', 'new_v_ffn_norm_g': 'new_v', 'new_v_w_gate': 'new_v', 'new_v_w_up': 'new_v', 'new_v_w_down': 'new_v'}


def _forward(args):
    return _fwd_reference(*[args[k] for k in FWD_PARAMS])


def _output_shape():
    def fwd():
        inp = _fwd_setup_inputs(0)
        return _fwd_reference(*[inp[k] for k in FWD_PARAMS])
    out = _jax.eval_shape(fwd)
    return out.shape, out.dtype

N_MICROBATCH = 1
ADAM_LR = 0.001
ADAM_B1 = 0.9
ADAM_B2 = 0.999
ADAM_EPS = 1e-08
ADAM_WD = 0.01
ADAM_STEP = 10
PER_EXAMPLE_BATCH_AXIS = {'x': 0, 'mem': 0, 'positions': 0, 'loss_target': 0}
SHARED_INPUTS = []
_WEIGHT_DTYPES = {'attn_norm_g': _jnp.float32, 'w_in': _jnp.float32, 'swa_q_norm_g': _jnp.float32, 'swa_k_norm_g': _jnp.float32, 'swa_sinks': _jnp.float32, 'mla_cq_norm_g': _jnp.float32, 'mla_ckv_norm_g': _jnp.float32, 'w_uq': _jnp.float32, 'w_ukv': _jnp.float32, 'mla_qn_norm_g': _jnp.float32, 'mla_qr_norm_g': _jnp.float32, 'mla_kn_norm_g': _jnp.float32, 'mla_kr_norm_g': _jnp.float32, 'mem_norm_g': _jnp.float32, 'w_mem_kv': _jnp.float32, 'mem_q_norm_g': _jnp.float32, 'mem_k_norm_g': _jnp.float32, 'w_out': _jnp.float32, 'ffn_norm_g': _jnp.float32, 'w_gate': _jnp.float32, 'w_up': _jnp.float32, 'w_down': _jnp.float32}
MOMENT_SCALE = {'attn_norm_g': 6.621176e-01, 'w_in': 8.254546e-02, 'swa_q_norm_g': 9.955667e+00, 'swa_k_norm_g': 1.000492e+01, 'swa_sinks': 1.706202e+01, 'mla_cq_norm_g': 4.290601e-02, 'mla_ckv_norm_g': 2.006777e-01, 'w_uq': 3.334137e-02, 'w_ukv': 4.501372e-02, 'mla_qn_norm_g': 2.996436e-01, 'mla_qr_norm_g': 2.910148e-01, 'mla_kn_norm_g': 2.997836e-01, 'mla_kr_norm_g': 2.933486e-01, 'mem_norm_g': 4.858964e-02, 'w_mem_kv': 4.094424e-02, 'mem_q_norm_g': 6.109604e-01, 'mem_k_norm_g': 6.115195e-01, 'w_out': 6.406451e-02, 'ffn_norm_g': 1.236951e+01, 'w_gate': 7.296782e-02, 'w_up': 7.414071e-02, 'w_down': 1.119072e-01}


def _to_microbatches(a, axis):
    t = _jnp.moveaxis(a, axis, 0)
    t = t.reshape((N_MICROBATCH, t.shape[0] // N_MICROBATCH) + t.shape[1:])
    return _jnp.moveaxis(t, 1, axis + 1)


def setup_inputs(seed: int = 0) -> dict:
    inp = _fwd_setup_inputs(seed)
    key = _jax.random.fold_in(_jax.random.key(seed), 7919)
    shape, _ = _output_shape()
    out = dict(inp)
    out["loss_target"] = _jax.random.normal(_jax.random.fold_in(key, 0), shape, _jnp.float32)
    for i, name in enumerate(TWIN_WEIGHTS):
        w = inp[name].astype(_jnp.float32)
        if MOMENT_SCALE is None:
            s = _jnp.sqrt(_jnp.mean(_jnp.square(w)) + 1e-30)
        else:
            s = MOMENT_SCALE[name]
        km, kv = _jax.random.split(_jax.random.fold_in(key, i + 1))
        out[name] = w
        out["m_" + name] = s * _jax.random.normal(km, w.shape, _jnp.float32)
        out["v_" + name] = (s * s) * _jax.random.uniform(kv, w.shape, _jnp.float32, 0.5, 1.5)
    if N_MICROBATCH > 1:
        for name, axis in PER_EXAMPLE_BATCH_AXIS.items():
            out[name] = _to_microbatches(out[name], axis)
    return {'x': out['x'], 'mem': out['mem'], 'positions': out['positions'], 'attn_norm_g': out['attn_norm_g'], 'w_in': out['w_in'], 'swa_q_norm_g': out['swa_q_norm_g'], 'swa_k_norm_g': out['swa_k_norm_g'], 'swa_sinks': out['swa_sinks'], 'mla_cq_norm_g': out['mla_cq_norm_g'], 'mla_ckv_norm_g': out['mla_ckv_norm_g'], 'w_uq': out['w_uq'], 'w_ukv': out['w_ukv'], 'mla_qn_norm_g': out['mla_qn_norm_g'], 'mla_qr_norm_g': out['mla_qr_norm_g'], 'mla_kn_norm_g': out['mla_kn_norm_g'], 'mla_kr_norm_g': out['mla_kr_norm_g'], 'mem_norm_g': out['mem_norm_g'], 'w_mem_kv': out['w_mem_kv'], 'mem_q_norm_g': out['mem_q_norm_g'], 'mem_k_norm_g': out['mem_k_norm_g'], 'w_out': out['w_out'], 'ffn_norm_g': out['ffn_norm_g'], 'w_gate': out['w_gate'], 'w_up': out['w_up'], 'w_down': out['w_down'], 'loss_target': out['loss_target'], 'm_attn_norm_g': out['m_attn_norm_g'], 'm_w_in': out['m_w_in'], 'm_swa_q_norm_g': out['m_swa_q_norm_g'], 'm_swa_k_norm_g': out['m_swa_k_norm_g'], 'm_swa_sinks': out['m_swa_sinks'], 'm_mla_cq_norm_g': out['m_mla_cq_norm_g'], 'm_mla_ckv_norm_g': out['m_mla_ckv_norm_g'], 'm_w_uq': out['m_w_uq'], 'm_w_ukv': out['m_w_ukv'], 'm_mla_qn_norm_g': out['m_mla_qn_norm_g'], 'm_mla_qr_norm_g': out['m_mla_qr_norm_g'], 'm_mla_kn_norm_g': out['m_mla_kn_norm_g'], 'm_mla_kr_norm_g': out['m_mla_kr_norm_g'], 'm_mem_norm_g': out['m_mem_norm_g'], 'm_w_mem_kv': out['m_w_mem_kv'], 'm_mem_q_norm_g': out['m_mem_q_norm_g'], 'm_mem_k_norm_g': out['m_mem_k_norm_g'], 'm_w_out': out['m_w_out'], 'm_ffn_norm_g': out['m_ffn_norm_g'], 'm_w_gate': out['m_w_gate'], 'm_w_up': out['m_w_up'], 'm_w_down': out['m_w_down'], 'v_attn_norm_g': out['v_attn_norm_g'], 'v_w_in': out['v_w_in'], 'v_swa_q_norm_g': out['v_swa_q_norm_g'], 'v_swa_k_norm_g': out['v_swa_k_norm_g'], 'v_swa_sinks': out['v_swa_sinks'], 'v_mla_cq_norm_g': out['v_mla_cq_norm_g'], 'v_mla_ckv_norm_g': out['v_mla_ckv_norm_g'], 'v_w_uq': out['v_w_uq'], 'v_w_ukv': out['v_w_ukv'], 'v_mla_qn_norm_g': out['v_mla_qn_norm_g'], 'v_mla_qr_norm_g': out['v_mla_qr_norm_g'], 'v_mla_kn_norm_g': out['v_mla_kn_norm_g'], 'v_mla_kr_norm_g': out['v_mla_kr_norm_g'], 'v_mem_norm_g': out['v_mem_norm_g'], 'v_w_mem_kv': out['v_w_mem_kv'], 'v_mem_q_norm_g': out['v_mem_q_norm_g'], 'v_mem_k_norm_g': out['v_mem_k_norm_g'], 'v_w_out': out['v_w_out'], 'v_ffn_norm_g': out['v_ffn_norm_g'], 'v_w_gate': out['v_w_gate'], 'v_w_up': out['v_w_up'], 'v_w_down': out['v_w_down']}


def _loss(weights, diff, rest, loss_target):
    with _jax.named_scope("forward"):
        args = {**rest, TWIN_DIFF_INPUT: diff, **{k: w.astype(_WEIGHT_DTYPES[k]) for k, w in weights.items()}}
        y = _forward(args)
    with _jax.named_scope("loss_head"):
        err = _jnp.square(y.astype(_jnp.float32) - loss_target)
        return 0.5 * _jnp.sum(_jnp.mean(err, axis=-1)) if err.ndim else 0.5 * err


def _adamw(w, g, m, v):
    m = ADAM_B1 * m + (1.0 - ADAM_B1) * g
    v = ADAM_B2 * v + (1.0 - ADAM_B2) * _jnp.square(g)
    m_hat = m / (1.0 - ADAM_B1 ** ADAM_STEP)
    v_hat = v / (1.0 - ADAM_B2 ** ADAM_STEP)
    delta = -ADAM_LR * (m_hat / (_jnp.sqrt(v_hat) + ADAM_EPS) + ADAM_WD * w)
    return delta, m, v


def reference(x, mem, positions, attn_norm_g, w_in, swa_q_norm_g, swa_k_norm_g, swa_sinks, mla_cq_norm_g, mla_ckv_norm_g, w_uq, w_ukv, mla_qn_norm_g, mla_qr_norm_g, mla_kn_norm_g, mla_kr_norm_g, mem_norm_g, w_mem_kv, mem_q_norm_g, mem_k_norm_g, w_out, ffn_norm_g, w_gate, w_up, w_down, loss_target, m_attn_norm_g, m_w_in, m_swa_q_norm_g, m_swa_k_norm_g, m_swa_sinks, m_mla_cq_norm_g, m_mla_ckv_norm_g, m_w_uq, m_w_ukv, m_mla_qn_norm_g, m_mla_qr_norm_g, m_mla_kn_norm_g, m_mla_kr_norm_g, m_mem_norm_g, m_w_mem_kv, m_mem_q_norm_g, m_mem_k_norm_g, m_w_out, m_ffn_norm_g, m_w_gate, m_w_up, m_w_down, v_attn_norm_g, v_w_in, v_swa_q_norm_g, v_swa_k_norm_g, v_swa_sinks, v_mla_cq_norm_g, v_mla_ckv_norm_g, v_w_uq, v_w_ukv, v_mla_qn_norm_g, v_mla_qr_norm_g, v_mla_kn_norm_g, v_mla_kr_norm_g, v_mem_norm_g, v_w_mem_kv, v_mem_q_norm_g, v_mem_k_norm_g, v_w_out, v_ffn_norm_g, v_w_gate, v_w_up, v_w_down):
    given = dict(x=x, mem=mem, positions=positions, attn_norm_g=attn_norm_g, w_in=w_in, swa_q_norm_g=swa_q_norm_g, swa_k_norm_g=swa_k_norm_g, swa_sinks=swa_sinks, mla_cq_norm_g=mla_cq_norm_g, mla_ckv_norm_g=mla_ckv_norm_g, w_uq=w_uq, w_ukv=w_ukv, mla_qn_norm_g=mla_qn_norm_g, mla_qr_norm_g=mla_qr_norm_g, mla_kn_norm_g=mla_kn_norm_g, mla_kr_norm_g=mla_kr_norm_g, mem_norm_g=mem_norm_g, w_mem_kv=w_mem_kv, mem_q_norm_g=mem_q_norm_g, mem_k_norm_g=mem_k_norm_g, w_out=w_out, ffn_norm_g=ffn_norm_g, w_gate=w_gate, w_up=w_up, w_down=w_down, loss_target=loss_target, m_attn_norm_g=m_attn_norm_g, m_w_in=m_w_in, m_swa_q_norm_g=m_swa_q_norm_g, m_swa_k_norm_g=m_swa_k_norm_g, m_swa_sinks=m_swa_sinks, m_mla_cq_norm_g=m_mla_cq_norm_g, m_mla_ckv_norm_g=m_mla_ckv_norm_g, m_w_uq=m_w_uq, m_w_ukv=m_w_ukv, m_mla_qn_norm_g=m_mla_qn_norm_g, m_mla_qr_norm_g=m_mla_qr_norm_g, m_mla_kn_norm_g=m_mla_kn_norm_g, m_mla_kr_norm_g=m_mla_kr_norm_g, m_mem_norm_g=m_mem_norm_g, m_w_mem_kv=m_w_mem_kv, m_mem_q_norm_g=m_mem_q_norm_g, m_mem_k_norm_g=m_mem_k_norm_g, m_w_out=m_w_out, m_ffn_norm_g=m_ffn_norm_g, m_w_gate=m_w_gate, m_w_up=m_w_up, m_w_down=m_w_down, v_attn_norm_g=v_attn_norm_g, v_w_in=v_w_in, v_swa_q_norm_g=v_swa_q_norm_g, v_swa_k_norm_g=v_swa_k_norm_g, v_swa_sinks=v_swa_sinks, v_mla_cq_norm_g=v_mla_cq_norm_g, v_mla_ckv_norm_g=v_mla_ckv_norm_g, v_w_uq=v_w_uq, v_w_ukv=v_w_ukv, v_mla_qn_norm_g=v_mla_qn_norm_g, v_mla_qr_norm_g=v_mla_qr_norm_g, v_mla_kn_norm_g=v_mla_kn_norm_g, v_mla_kr_norm_g=v_mla_kr_norm_g, v_mem_norm_g=v_mem_norm_g, v_w_mem_kv=v_w_mem_kv, v_mem_q_norm_g=v_mem_q_norm_g, v_mem_k_norm_g=v_mem_k_norm_g, v_w_out=v_w_out, v_ffn_norm_g=v_ffn_norm_g, v_w_gate=v_w_gate, v_w_up=v_w_up, v_w_down=v_w_down)
    weights = {n: given[n] for n in TWIN_WEIGHTS}
    shared = {n: given[n] for n in SHARED_INPUTS}
    per_example = {n: given[n] for n in ['x', 'mem', 'positions']}
    grad_fn = _jax.value_and_grad(_loss, argnums=(0, 1))

    def one_microbatch(ex, loss_target):
        ex = dict(ex)
        diff = ex.pop(TWIN_DIFF_INPUT)
        return grad_fn(weights, diff, {**shared, **ex}, loss_target)

    if N_MICROBATCH == 1:
        loss, (grad_w, grad_x) = one_microbatch(per_example, given["loss_target"])
    else:
        def body(carry, xs):
            loss_sum, grad_sum = carry
            l_k, (gw_k, gx_k) = one_microbatch(xs[0], xs[1])
            with _jax.named_scope("update"):
                return (loss_sum + l_k, _jax.tree.map(_jnp.add, grad_sum, gw_k)), gx_k

        init = (_jnp.zeros((), _jnp.float32), _jax.tree.map(_jnp.zeros_like, weights))
        (loss, grad_w), grad_x = _jax.lax.scan(body, init, (per_example, given["loss_target"]))
    with _jax.named_scope("update"):
        delta_w, new_m, new_v = {}, {}, {}
        for n in TWIN_WEIGHTS:
            delta_w[n], new_m[n], new_v[n] = _adamw(weights[n], grad_w[n], given["m_" + n], given["v_" + n])
    return (loss, grad_x, *[grad_w[n] for n in TWIN_WEIGHTS], *[delta_w[n] for n in TWIN_WEIGHTS],
            *[new_m[n] for n in TWIN_WEIGHTS], *[new_v[n] for n in TWIN_WEIGHTS])
```

```python
import jax
import jax.numpy as jnp
from jax import lax
from jax.experimental import pallas as pl
from jax.experimental.pallas import tpu as pltpu

MXU = jnp.bfloat16
WIRE = jnp.bfloat16
EPS = 1e-6
NEG_INF = -1e30
N_DEV = 8
LANES = 128
ROW_TILE = 256
FFN_TILE = 512
ATT_TILE = 1024
SWA_BLOCK = 128
VMEM_LIMIT = 56 * 1024 * 1024

SWA_Q_HEADS, SWA_KV_HEADS, SWA_DIM = 16, 2, 64
MLA_HEADS, MLA_NOPE, MLA_ROPE, MLA_V = 4, 128, 64, 128
MEM_HEADS, MEM_DIM = 4, 128
ROPE_THETA = 10000.0
ADAM_LR, ADAM_B1, ADAM_B2, ADAM_EPS, ADAM_WD, ADAM_STEP = 0.001, 0.9, 0.999, 1e-08, 0.01, 10

C_QA, C_CQ, C_CKV, C_QM, C_KA, C_VA, C_KR, IN_PAD = 0, 1024, 1536, 2048, 2560, 2688, 2816, 2944


def _pcall(body, *, name, out_shape, in_specs, out_specs, grid=(), scratch=(), sem=None):
    params = pltpu.CompilerParams(dimension_semantics=sem, vmem_limit_bytes=VMEM_LIMIT)
    return pl.pallas_call(body, name=name, grid=grid, in_specs=in_specs, out_specs=out_specs,
                          out_shape=out_shape, scratch_shapes=list(scratch), compiler_params=params)


def _sds(shape, dtype):
    return jax.ShapeDtypeStruct(tuple(shape), dtype)


def _dot(a, b):
    return jnp.dot(a.astype(MXU), b.astype(MXU), preferred_element_type=jnp.float32)


def _dot_nt(a, b):
    return lax.dot_general(a.astype(MXU), b.astype(MXU), (((1,), (1,)), ((), ())),
                           preferred_element_type=jnp.float32)


def _dot_tn(a, b):
    return lax.dot_general(a.astype(MXU), b.astype(MXU), (((0,), (0,)), ((), ())),
                           preferred_element_type=jnp.float32)


def _lo_mask(shape):
    return (lax.broadcasted_iota(jnp.int32, shape, len(shape) - 1) % LANES) < 64


def _norm_fwd(x, g, half=False):
    x2 = x * x
    if half:
        lo = _lo_mask(x.shape)
        s_lo = jnp.sum(jnp.where(lo, x2, 0.0), -1, keepdims=True)
        s_hi = jnp.sum(jnp.where(lo, 0.0, x2), -1, keepdims=True)
        r = jnp.where(lo, lax.rsqrt(s_lo / 64.0 + EPS), lax.rsqrt(s_hi / 64.0 + EPS))
    else:
        r = lax.rsqrt(jnp.mean(x2, -1, keepdims=True) + EPS)
    xn = x * r
    return xn * g, xn, r


def _norm_bwd(xn, r, g, dy, half=False):
    t = dy * g
    tx = t * xn
    if half:
        lo = _lo_mask(xn.shape)
        m_lo = jnp.sum(jnp.where(lo, tx, 0.0), -1, keepdims=True) / 64.0
        m_hi = jnp.sum(jnp.where(lo, 0.0, tx), -1, keepdims=True) / 64.0
        m = jnp.where(lo, m_lo, m_hi)
    else:
        m = jnp.mean(tx, -1, keepdims=True)
    dx = r * (t - xn * m)
    dg = jnp.sum(dy * xn, 0, keepdims=True)
    return dx, dg


def _swap32(x):
    lane = lax.broadcasted_iota(jnp.int32, x.shape, 1)
    return jnp.where((lane % 64) < 32, pltpu.roll(x, 96, 1), pltpu.roll(x, 32, 1))


def _rope(x, cos, sin):
    return x * cos + _swap32(x) * sin


def _rope_bwd(d, cos, sin):
    return d * cos + _swap32(d * sin)


def _my_coords():
    return lax.axis_index("x"), lax.axis_index("y"), lax.axis_index("c")


def _dev_index(px, py, pc):
    return 4 * px + 2 * py + pc


_FLIPS = [(0, 0, 1), (0, 1, 0), (0, 1, 1), (1, 0, 0), (1, 0, 1), (1, 1, 0), (1, 1, 1)]


def _flip(coords, f):
    return tuple((1 - v) if b else v for v, b in zip(coords, f))


def _all_gather(shards):
    n = len(shards)

    def body(*refs):
        ins, outs = refs[:n], refs[n:2 * n]
        send_sems, recv_sems, local_sems = refs[2 * n:]
        x, y, c = _my_coords()
        me, sibling = (x, y, c), (x, y, 1 - c)
        chips = [(1 - x, y), (x, 1 - y), (1 - x, 1 - y)]

        def copy(w, k, block, to, src=None):
            dst = outs[w].at[_dev_index(*block)]
            return pltpu.make_async_remote_copy(
                src_ref=dst if src is None else src, dst_ref=dst,
                send_sem=send_sems.at[w, k], recv_sem=recv_sems.at[w, k],
                device_id=to, device_id_type=pl.DeviceIdType.MESH)

        sends, locals_ = [], []
        for w in range(n):
            mine = pltpu.make_async_copy(ins[w], outs[w].at[_dev_index(*me)], local_sems.at[w])
            mine.start()
            locals_.append(mine)
            first = [copy(w, 0, me, sibling, src=ins[w])]
            first += [copy(w, 1 + j, me, (*chip, c), src=ins[w]) for j, chip in enumerate(chips)]
            for cp in first:
                cp.start()
            sends += first
        for w in range(n):
            for j, chip in enumerate(chips):
                copy(w, 1 + j, (*chip, c), me).wait_recv()
                fwd = copy(w, 4 + j, (*chip, c), sibling)
                fwd.start()
                sends.append(fwd)
        for w in range(n):
            copy(w, 0, sibling, me).wait_recv()
            for j, chip in enumerate(chips):
                copy(w, 4 + j, (*chip, 1 - c), me).wait_recv()
        for cp in sends:
            cp.wait_send()
        for mine in locals_:
            mine.wait()

    any_spec = pl.BlockSpec(memory_space=pl.ANY)
    return _pcall(
        body, name="all_gather_weights",
        out_shape=[_sds((N_DEV,) + s.shape, s.dtype) for s in shards],
        in_specs=[any_spec] * n, out_specs=[any_spec] * n,
        scratch=[pltpu.SemaphoreType.DMA((n, 7)), pltpu.SemaphoreType.DMA((n, 7)),
                 pltpu.SemaphoreType.DMA((n,))])(*shards)


def _exchange_grads(grads):
    n = len(grads)

    def body(*refs):
        ins, outs = refs[:n], refs[n:2 * n]
        send_sems, recv_sems, local_sems = refs[2 * n:]
        me = _my_coords()
        my_idx = _dev_index(*me)
        sends, locals_ = [], []
        for w in range(n):
            mine = pltpu.make_async_copy(ins[w].at[my_idx], outs[w].at[my_idx], local_sems.at[w])
            mine.start()
            locals_.append(mine)
            for k, f in enumerate(_FLIPS):
                peer = _flip(me, f)
                cp = pltpu.make_async_remote_copy(
                    src_ref=ins[w].at[_dev_index(*peer)], dst_ref=outs[w].at[my_idx],
                    send_sem=send_sems.at[w, k], recv_sem=recv_sems.at[w, k],
                    device_id=peer, device_id_type=pl.DeviceIdType.MESH)
                cp.start()
                sends.append(cp)
        for w in range(n):
            for k, f in enumerate(_FLIPS):
                peer = _flip(me, f)
                slot = outs[w].at[_dev_index(*peer)]
                pltpu.make_async_remote_copy(
                    src_ref=slot, dst_ref=slot,
                    send_sem=send_sems.at[w, k], recv_sem=recv_sems.at[w, k],
                    device_id=peer, device_id_type=pl.DeviceIdType.MESH).wait_recv()
        for cp in sends:
            cp.wait_send()
        for mine in locals_:
            mine.wait()

    any_spec = pl.BlockSpec(memory_space=pl.ANY)
    return _pcall(
        body, name="exchange_grads",
        out_shape=[_sds(g.shape, g.dtype) for g in grads],
        in_specs=[any_spec] * n, out_specs=[any_spec] * n,
        scratch=[pltpu.SemaphoreType.DMA((n, 7)), pltpu.SemaphoreType.DMA((n, 7)),
                 pltpu.SemaphoreType.DMA((n,))])(*grads)


def _adam_math(w, g, m, v):
    m = ADAM_B1 * m + (1.0 - ADAM_B1) * g
    v = ADAM_B2 * v + (1.0 - ADAM_B2) * (g * g)
    m_hat = m / (1.0 - ADAM_B1 ** ADAM_STEP)
    v_hat = v / (1.0 - ADAM_B2 ** ADAM_STEP)
    delta = -ADAM_LR * (m_hat / (jnp.sqrt(v_hat) + ADAM_EPS) + ADAM_WD * w)
    return delta, m, v


def _small_allreduce_adam(pg, pw, pm, pv):
    rows = pg.shape[0]

    def body(pg_ref, pw_ref, pm_ref, pv_ref, g_ref, d_ref, m_ref, v_ref, gath, send_sems, recv_sems):
        me = _my_coords()
        my_idx = _dev_index(*me)
        gath[my_idx] = pg_ref[...]
        sends = []
        for k, f in enumerate(_FLIPS):
            peer = _flip(me, f)
            cp = pltpu.make_async_remote_copy(
                src_ref=pg_ref, dst_ref=gath.at[my_idx],
                send_sem=send_sems.at[k], recv_sem=recv_sems.at[k],
                device_id=peer, device_id_type=pl.DeviceIdType.MESH)
            cp.start()
            sends.append(cp)
        for k, f in enumerate(_FLIPS):
            peer = _flip(me, f)
            slot = gath.at[_dev_index(*peer)]
            pltpu.make_async_remote_copy(
                src_ref=slot, dst_ref=slot, send_sem=send_sems.at[k], recv_sem=recv_sems.at[k],
                device_id=peer, device_id_type=pl.DeviceIdType.MESH).wait_recv()
        for cp in sends:
            cp.wait_send()
        g = gath[0]
        for d in range(1, N_DEV):
            g = g + gath[d]
        delta, m, v = _adam_math(pw_ref[...], g, pm_ref[...], pv_ref[...])
        g_ref[...] = g
        d_ref[...] = delta
        m_ref[...] = m
        v_ref[...] = v

    vm = pl.BlockSpec(memory_space=pltpu.VMEM)
    return _pcall(
        body, name="small_allreduce_adam",
        out_shape=[_sds(pg.shape, jnp.float32)] * 4,
        in_specs=[vm] * 4, out_specs=[vm] * 4,
        scratch=[pltpu.VMEM((N_DEV, rows, LANES), jnp.float32),
                 pltpu.SemaphoreType.DMA((7,)), pltpu.SemaphoreType.DMA((7,))])(pg, pw, pm, pv)


def _adam_big(recv, w, m, v, name):
    _, rows, cols = recv.shape
    tr = rows
    while tr * cols > 256 * 1024 and tr % 2 == 0 and (tr // 2) % 16 == 0:
        tr //= 2

    def body(r_ref, w_ref, m_ref, v_ref, g_ref, d_ref, mo_ref, vo_ref):
        g = r_ref[0].astype(jnp.float32)
        for d in range(1, N_DEV):
            g = g + r_ref[d].astype(jnp.float32)
        delta, mn, vn = _adam_math(w_ref[...], g, m_ref[...], v_ref[...])
        g_ref[...] = g
        d_ref[...] = delta
        mo_ref[...] = mn
        vo_ref[...] = vn

    blk = pl.BlockSpec((tr, cols), lambda i: (i, 0))
    return _pcall(
        body, name=name, grid=(rows // tr,),
        out_shape=[_sds((rows, cols), jnp.float32)] * 4,
        in_specs=[pl.BlockSpec((N_DEV, tr, cols), lambda i: (0, i, 0)), blk, blk, blk],
        out_specs=[blk] * 4, sem=("parallel",))(recv, w, m, v)


def _mm(a, b, *, ta=False, tb=False, out_dtype, tm, tk, name):
    (kdim, mdim) = a.shape if ta else a.shape[::-1]
    ndim = b.shape[0] if tb else b.shape[1]
    tm, tk = min(tm, mdim), min(tk, kdim)
    nk = kdim // tk

    def body(a_ref, b_ref, o_ref, acc):
        k = pl.program_id(1)
        if ta:
            part = _dot_tn(a_ref[...], b_ref[...])
        elif tb:
            part = _dot_nt(a_ref[...], b_ref[...])
        else:
            part = _dot(a_ref[...], b_ref[...])

        @pl.when(k == 0)
        def _():
            acc[...] = part

        @pl.when(k > 0)
        def _():
            acc[...] += part

        @pl.when(k == nk - 1)
        def _():
            o_ref[...] = acc[...].astype(o_ref.dtype)

    a_spec = pl.BlockSpec((tk, tm), lambda i, k: (k, i)) if ta else pl.BlockSpec((tm, tk), lambda i, k: (i, k))
    b_spec = pl.BlockSpec((ndim, tk), lambda i, k: (0, k)) if tb else pl.BlockSpec((tk, ndim), lambda i, k: (k, 0))
    return _pcall(
        body, name=name, grid=(mdim // tm, nk), out_shape=_sds((mdim, ndim), out_dtype),
        in_specs=[a_spec, b_spec], out_specs=pl.BlockSpec((tm, ndim), lambda i, k: (i, 0)),
        scratch=[pltpu.VMEM((tm, ndim), jnp.float32)], sem=("parallel", "arbitrary"))(a, b)


def _in_proj(x, g, w):
    s, d = x.shape
    n = w.shape[1]
    tm = min(ROW_TILE, s)

    def body(x_ref, g_ref, w_ref, p_ref, hn_ref):
        hn, _, _ = _norm_fwd(x_ref[...], g_ref[...])
        hn_ref[...] = hn.astype(hn_ref.dtype)
        p_ref[...] = _dot(hn, w_ref[...])

    return _pcall(
        body, name="in_proj", grid=(s // tm,),
        out_shape=[_sds((s, n), jnp.float32), _sds((s, d), MXU)],
        in_specs=[pl.BlockSpec((tm, d), lambda i: (i, 0)), pl.BlockSpec((1, d), lambda i: (0, 0)),
                  pl.BlockSpec((d, n), lambda i: (0, 0))],
        out_specs=[pl.BlockSpec((tm, n), lambda i: (i, 0)), pl.BlockSpec((tm, d), lambda i: (i, 0))],
        sem=("parallel",))(x, g, w)


def _mla_prep(proj, cos, sin, g_cq, g_ckv, w_uq, w_ukv, g_qn, g_qr, g_kn, g_kr):
    s = proj.shape[0]
    tm = min(ROW_TILE, s)
    nh = MLA_HEADS

    def body(cq_ref, ckv_ref, kr_ref, cos_ref, sin_ref, gcq_ref, gckv_ref, wuq_ref, wukv_ref,
             gqn_ref, gqr_ref, gkn_ref, gkr_ref,
             qc_ref, kc_ref, v_ref, qb_ref, kvb_ref, cqn_ref, ckvn_ref):
        cos_t, sin_t = cos_ref[...], sin_ref[...]
        lo = _lo_mask((tm, LANES))
        cqn, _, _ = _norm_fwd(cq_ref[...], gcq_ref[...])
        cqn_ref[...] = cqn.astype(cqn_ref.dtype)
        qb = _dot(cqn, wuq_ref[...])
        qb_ref[...] = qb
        ckvn, _, _ = _norm_fwd(ckv_ref[...], gckv_ref[...])
        ckvn_ref[...] = ckvn.astype(ckvn_ref.dtype)
        kvb = _dot(ckvn, wukv_ref[...])
        kvb_ref[...] = kvb
        kr, _, _ = _norm_fwd(kr_ref[...], gkr_ref[...], half=True)
        kr = _rope(kr, cos_t, sin_t)
        kr2 = jnp.where(lo, kr, pltpu.roll(kr, 64, 1))
        ropes = []
        for j in range(nh // 2):
            xr = qb[:, nh * MLA_NOPE + LANES * j: nh * MLA_NOPE + LANES * (j + 1)]
            qr, _, _ = _norm_fwd(xr, gqr_ref[...], half=True)
            ropes.append(_rope(qr, cos_t, sin_t))
        for h in range(nh):
            qn, _, _ = _norm_fwd(qb[:, MLA_NOPE * h: MLA_NOPE * (h + 1)], gqn_ref[...])
            mask = lo if h % 2 == 0 else jnp.logical_not(lo)
            qr = jnp.where(mask, ropes[h // 2], 0.0)
            qc_ref[h] = jnp.concatenate([qn, qr], axis=1).astype(qc_ref.dtype)
            kn, _, _ = _norm_fwd(kvb[:, 256 * h: 256 * h + MLA_NOPE], gkn_ref[...])
            kc_ref[h] = jnp.concatenate([kn, kr2], axis=1).astype(kc_ref.dtype)
            v_ref[h] = kvb[:, 256 * h + MLA_NOPE: 256 * (h + 1)].astype(v_ref.dtype)

    def col(width, start):
        return pl.BlockSpec((tm, width), lambda i: (i, start // width))

    def full(shape):
        return pl.BlockSpec(shape, lambda i: (0,) * len(shape))

    def row(width):
        return pl.BlockSpec((tm, width), lambda i: (i, 0))

    def heads(width):
        return pl.BlockSpec((nh, tm, width), lambda i: (0, i, 0))

    return _pcall(
        body, name="mla_prep", grid=(s // tm,),
        out_shape=[_sds((nh, s, 256), MXU), _sds((nh, s, 256), MXU), _sds((nh, s, MLA_V), MXU),
                   _sds((s, 768), jnp.float32), _sds((s, 1024), jnp.float32),
                   _sds((s, 512), MXU), _sds((s, 512), MXU)],
        in_specs=[col(512, C_CQ), col(512, C_CKV), col(LANES, C_KR), row(LANES), row(LANES),
                  full((1, 512)), full((1, 512)), full((512, 768)), full((512, 1024)),
                  full((1, LANES)), full((1, LANES)), full((1, LANES)), full((1, LANES))],
        out_specs=[heads(256), heads(256), heads(MLA_V), row(768), row(1024), row(512), row(512)],
        sem=("parallel",))(proj, proj, proj, cos, sin, g_cq, g_ckv, w_uq, w_ukv, g_qn, g_qr, g_kn, g_kr)


def _mla_fwd(qc, kc, v):
    nh, s, _ = qc.shape
    t = min(ATT_TILE, s)
    nb = s // t
    scale = (MLA_NOPE + MLA_ROPE) ** -0.5

    def body(q_ref, k_ref, v_ref, y_ref, lse_ref, m_sc, l_sc, acc):
        qi, ki = pl.program_id(1), pl.program_id(2)

        @pl.when(ki == 0)
        def _():
            m_sc[...] = jnp.full_like(m_sc, NEG_INF)
            l_sc[...] = jnp.zeros_like(l_sc)
            acc[...] = jnp.zeros_like(acc)

        @pl.when(ki <= qi)
        def _():
            sc = _dot_nt(q_ref[0], k_ref[0]) * scale
            r_i = lax.broadcasted_iota(jnp.int32, sc.shape, 0) + qi * t
            c_i = lax.broadcasted_iota(jnp.int32, sc.shape, 1) + ki * t
            sc = jnp.where(c_i <= r_i, sc, NEG_INF)
            m_new = jnp.maximum(m_sc[...], jnp.max(sc, -1, keepdims=True))
            alpha = jnp.exp(m_sc[...] - m_new)
            p = jnp.exp(sc - m_new)
            l_sc[...] = alpha * l_sc[...] + jnp.sum(p, -1, keepdims=True)
            acc[...] = alpha * acc[...] + _dot(p, v_ref[0])
            m_sc[...] = m_new

        @pl.when(ki == qi)
        def _():
            y_ref[...] = acc[...] / l_sc[...]
            lse_ref[0] = m_sc[...] + jnp.log(l_sc[...])

    return _pcall(
        body, name="mla_fwd", grid=(nh, nb, nb),
        out_shape=[_sds((s, nh * MLA_V), jnp.float32), _sds((nh, s, 1), jnp.float32)],
        in_specs=[pl.BlockSpec((1, t, 256), lambda h, i, k: (h, i, 0)),
                  pl.BlockSpec((1, t, 256), lambda h, i, k: (h, jnp.minimum(k, i), 0)),
                  pl.BlockSpec((1, t, MLA_V), lambda h, i, k: (h, jnp.minimum(k, i), 0))],
        out_specs=[pl.BlockSpec((t, MLA_V), lambda h, i, k: (i, h)),
                   pl.BlockSpec((1, t, 1), lambda h, i, k: (h, i, 0))],
        scratch=[pltpu.VMEM((t, 1), jnp.float32), pltpu.VMEM((t, 1), jnp.float32),
                 pltpu.VMEM((t, MLA_V), jnp.float32)],
        sem=("parallel", "parallel", "arbitrary"))(qc, kc, v)


def _memkv_prep(mem, g_mem, w_mkv, g_mk):
    ml, d = mem.shape
    hw = MEM_HEADS * MEM_DIM

    def body(mem_ref, g_ref, w_ref, gk_ref, k_ref, v_ref, kv_ref, mn_ref):
        mn, _, _ = _norm_fwd(mem_ref[...], g_ref[...])
        mn_ref[...] = mn.astype(mn_ref.dtype)
        kv = _dot(mn, w_ref[...])
        kv_ref[...] = kv
        for h in range(MEM_HEADS):
            kn, _, _ = _norm_fwd(kv[:, MEM_DIM * h: MEM_DIM * (h + 1)], gk_ref[...])
            k_ref[:, MEM_DIM * h: MEM_DIM * (h + 1)] = kn.astype(k_ref.dtype)
        v_ref[...] = kv[:, hw:].astype(v_ref.dtype)

    vm = pl.BlockSpec(memory_space=pltpu.VMEM)
    return _pcall(
        body, name="memkv_prep",
        out_shape=[_sds((ml, hw), MXU), _sds((ml, hw), MXU), _sds((ml, 2 * hw), jnp.float32), _sds((ml, d), MXU)],
        in_specs=[vm] * 4, out_specs=[vm] * 4)(mem, g_mem, w_mkv, g_mk)


def _mem_fwd(proj, g_mq, km, vmm):
    s = proj.shape[0]
    ml, hw = km.shape
    tm = min(FFN_TILE, s)
    scale = MEM_DIM ** -0.5

    def body(q_ref, g_ref, k_ref, v_ref, y_ref, lse_ref):
        col = lax.broadcasted_iota(jnp.int32, (tm, MEM_HEADS), 1)
        lse_t = jnp.zeros((tm, MEM_HEADS), jnp.float32)
        for h in range(MEM_HEADS):
            sl = slice(MEM_DIM * h, MEM_DIM * (h + 1))
            qn, _, _ = _norm_fwd(q_ref[:, sl], g_ref[...])
            sc = _dot_nt(qn, k_ref[:, sl]) * scale
            m = jnp.max(sc, -1, keepdims=True)
            p = jnp.exp(sc - m)
            l = jnp.sum(p, -1, keepdims=True)
            y_ref[:, sl] = _dot(p, v_ref[:, sl]) / l
            lse_t = jnp.where(col == h, m + jnp.log(l), lse_t)
        lse_ref[...] = lse_t

    return _pcall(
        body, name="mem_fwd", grid=(s // tm,),
        out_shape=[_sds((s, hw), jnp.float32), _sds((s, MEM_HEADS), jnp.float32)],
        in_specs=[pl.BlockSpec((tm, hw), lambda i: (i, C_QM // hw)), pl.BlockSpec((1, MEM_DIM), lambda i: (0, 0)),
                  pl.BlockSpec((ml, hw), lambda i: (0, 0)), pl.BlockSpec((ml, hw), lambda i: (0, 0))],
        out_specs=[pl.BlockSpec((tm, hw), lambda i: (i, 0)), pl.BlockSpec((tm, MEM_HEADS), lambda i: (i, 0))],
        sem=("parallel",))(proj, g_mq, km, vmm)


def _alibi_slope(h):
    return float(2.0 ** (-8.0 * (h + 1) / SWA_Q_HEADS))


def _swa_common(n, kp, kc, vp, vc, pq, pkp, pkc, gk):
    b = SWA_BLOCK
    k_raw = jnp.concatenate([kp, kc], axis=0)
    kn, kxn, kr = _norm_fwd(k_raw, gk, half=True)
    v = jnp.concatenate([vp, vc], axis=0)
    dist = jnp.abs(pq - jnp.concatenate([pkp, pkc], axis=1))
    r_i = lax.broadcasted_iota(jnp.int32, (b, 2 * b), 0)
    c_i = lax.broadcasted_iota(jnp.int32, (b, 2 * b), 1)
    valid = (c_i > r_i) & (c_i <= r_i + b) & (c_i >= jnp.where(n > 0, 0, b))
    return kn, v, dist, valid


def _swa_specs(s):
    b = SWA_BLOCK
    prev = lambda n: jnp.maximum(n - 1, 0)
    return [
        pl.BlockSpec((b, 1024), lambda n: (n, C_QA // 1024)),
        pl.BlockSpec((b, LANES), lambda n: (prev(n), C_KA // LANES)),
        pl.BlockSpec((b, LANES), lambda n: (n, C_KA // LANES)),
        pl.BlockSpec((b, LANES), lambda n: (prev(n), C_VA // LANES)),
        pl.BlockSpec((b, LANES), lambda n: (n, C_VA // LANES)),
        pl.BlockSpec((b, 1), lambda n: (n, 0)),
        pl.BlockSpec((1, b), lambda n: (0, prev(n))),
        pl.BlockSpec((1, b), lambda n: (0, n)),
        pl.BlockSpec((1, LANES), lambda n: (0, 0)),
        pl.BlockSpec((1, LANES), lambda n: (0, 0)),
        pl.BlockSpec(memory_space=pltpu.SMEM),
    ]


def _swa_fwd(proj, posc, posr, gq, gk, sinks):
    s = proj.shape[0]
    b = SWA_BLOCK
    scale = SWA_DIM ** -0.5

    def body(q_ref, kp_ref, kc_ref, vp_ref, vc_ref, pq_ref, pkp_ref, pkc_ref, gq_ref, gk_ref, sink_ref,
             y_ref, lse_ref):
        n = pl.program_id(0)
        kn, v, dist, valid = _swa_common(n, kp_ref[...], kc_ref[...], vp_ref[...], vc_ref[...],
                                         pq_ref[...], pkp_ref[...], pkc_ref[...], gk_ref[...])
        lo = _lo_mask((b, LANES))
        col = lax.broadcasted_iota(jnp.int32, (b, SWA_Q_HEADS), 1)
        lse_t = jnp.zeros((b, SWA_Q_HEADS), jnp.float32)
        for j in range(SWA_Q_HEADS // 2):
            hk = (2 * j) // (SWA_Q_HEADS // SWA_KV_HEADS)
            kvmask = lo if hk == 0 else jnp.logical_not(lo)
            qn, _, _ = _norm_fwd(q_ref[:, LANES * j: LANES * (j + 1)], gq_ref[...], half=True)
            qsw = pltpu.roll(qn, 64, 1)
            outs = []
            for e in range(2):
                h = 2 * j + e
                qm = jnp.where(kvmask, qn if e == hk else qsw, 0.0)
                sc = _dot_nt(qm, kn) * scale - _alibi_slope(h) * dist
                sc = jnp.where(valid, sc, NEG_INF)
                sk = sink_ref[h]
                m = jnp.maximum(jnp.max(sc, -1, keepdims=True), sk)
                p = jnp.exp(sc - m)
                l = jnp.sum(p, -1, keepdims=True) + jnp.exp(sk - m)
                o = _dot(p, v) / l
                outs.append(o if e == hk else pltpu.roll(o, 64, 1))
                lse_t = jnp.where(col == h, m + jnp.log(l), lse_t)
            y_ref[:, LANES * j: LANES * (j + 1)] = jnp.where(lo, outs[0], outs[1])
        lse_ref[...] = lse_t

    return _pcall(
        body, name="swa_fwd", grid=(s // b,),
        out_shape=[_sds((s, 1024), jnp.float32), _sds((s, SWA_Q_HEADS), jnp.float32)],
        in_specs=_swa_specs(s),
        out_specs=[pl.BlockSpec((b, 1024), lambda n: (n, 0)), pl.BlockSpec((b, SWA_Q_HEADS), lambda n: (n, 0))],
        sem=("parallel",))(proj, proj, proj, proj, proj, posc, posr, posr, gq, gk, sinks)


def _out_proj(y_a, y_b, y_m, x, w_out, g_ffn):
    s, d = x.shape
    tm = min(ROW_TILE, s)

    def body(ya_ref, yb_ref, ym_ref, x_ref, w_ref, g_ref, h1_ref, fn_ref):
        y = jnp.concatenate([ya_ref[...].astype(MXU), yb_ref[...].astype(MXU), ym_ref[...].astype(MXU)], axis=1)
        h1 = x_ref[...] + _dot(y, w_ref[...])
        h1_ref[...] = h1
        fn, _, _ = _norm_fwd(h1, g_ref[...])
        fn_ref[...] = fn.astype(fn_ref.dtype)

    def row(width):
        return pl.BlockSpec((tm, width), lambda i: (i, 0))

    return _pcall(
        body, name="out_proj", grid=(s // tm,),
        out_shape=[_sds((s, d), jnp.float32), _sds((s, d), MXU)],
        in_specs=[row(1024), row(512), row(512), row(d), pl.BlockSpec(w_out.shape, lambda i: (0, 0)),
                  pl.BlockSpec((1, d), lambda i: (0, 0))],
        out_specs=[row(d), row(d)], sem=("parallel",))(y_a, y_b, y_m, x, w_out, g_ffn)


def _ffn_gu(fn, w_gu):
    s, d = fn.shape
    f = w_gu.shape[-1]
    tm = min(FFN_TILE, s)

    def body(fn_ref, w_ref, gu_ref, act_ref):
        x = fn_ref[...]
        g = _dot(x, w_ref[0, 0])
        u = _dot(x, w_ref[0, 1])
        gu_ref[0, 0] = g
        gu_ref[0, 1] = u
        act_ref[0] = (g * jax.nn.sigmoid(g) * u).astype(act_ref.dtype)

    return _pcall(
        body, name="ffn_gate_up", grid=(N_DEV, s // tm),
        out_shape=[_sds((N_DEV, 2, s, f), jnp.float32), _sds((N_DEV, s, f), MXU)],
        in_specs=[pl.BlockSpec((tm, d), lambda j, i: (i, 0)),
                  pl.BlockSpec((1, 2, d, f), lambda j, i: (j, 0, 0, 0))],
        out_specs=[pl.BlockSpec((1, 2, tm, f), lambda j, i: (j, 0, i, 0)),
                   pl.BlockSpec((1, tm, f), lambda j, i: (j, i, 0))],
        sem=("parallel", "parallel"))(fn, w_gu)


def _ffn_down(act, w_d, h1, target):
    _, s, f = act.shape
    d = h1.shape[1]
    tm = min(FFN_TILE, s)

    def body(a_ref, w_ref, h1_ref, t_ref, dout_ref, loss_ref, acc):
        i, j = pl.program_id(0), pl.program_id(1)
        part = _dot(a_ref[0], w_ref[0])

        @pl.when(j == 0)
        def _():
            acc[...] = h1_ref[...] + part

        @pl.when(j > 0)
        def _():
            acc[...] += part

        @pl.when((i == 0) & (j == 0))
        def _():
            loss_ref[...] = jnp.zeros_like(loss_ref)

        @pl.when(j == N_DEV - 1)
        def _():
            diff = acc[...] - t_ref[...]
            dout_ref[...] = diff / d
            loss_ref[...] += 0.5 * jnp.sum(jnp.sum(diff * diff, -1, keepdims=True) / d)

    row = pl.BlockSpec((tm, d), lambda i, j: (i, 0))
    return _pcall(
        body, name="ffn_down", grid=(s // tm, N_DEV),
        out_shape=[_sds((s, d), jnp.float32), _sds((8, LANES), jnp.float32)],
        in_specs=[pl.BlockSpec((1, tm, f), lambda i, j: (j, i, 0)), pl.BlockSpec((1, f, d), lambda i, j: (j, 0, 0)),
                  row, row],
        out_specs=[row, pl.BlockSpec((8, LANES), lambda i, j: (0, 0))],
        scratch=[pltpu.VMEM((tm, d), jnp.float32)], sem=("arbitrary", "arbitrary"))(act, w_d, h1, target)


def _ffn_bwd_act(dout, w_d, gu):
    s, d = dout.shape
    f = w_d.shape[1]
    tm = min(FFN_TILE, s)
    ni = s // tm

    def body(do_ref, w_ref, gu_ref, dgu_ref, dw_ref, acc):
        i = pl.program_id(1)
        do = do_ref[...].astype(MXU)
        d_act = _dot_nt(do, w_ref[0])
        g, u = gu_ref[0, 0], gu_ref[0, 1]
        sig = jax.nn.sigmoid(g)
        silu = g * sig
        dgu_ref[0, 0] = (d_act * u * (sig * (1.0 + g * (1.0 - sig)))).astype(dgu_ref.dtype)
        dgu_ref[0, 1] = (d_act * silu).astype(dgu_ref.dtype)
        part = _dot_tn(silu * u, do)

        @pl.when(i == 0)
        def _():
            acc[...] = part

        @pl.when(i > 0)
        def _():
            acc[...] += part

        @pl.when(i == ni - 1)
        def _():
            dw_ref[0] = acc[...].astype(dw_ref.dtype)

    return _pcall(
        body, name="ffn_bwd_act", grid=(N_DEV, ni),
        out_shape=[_sds((N_DEV, 2, s, f), MXU), _sds((N_DEV, f, d), WIRE)],
        in_specs=[pl.BlockSpec((tm, d), lambda j, i: (i, 0)), pl.BlockSpec((1, f, d), lambda j, i: (j, 0, 0)),
                  pl.BlockSpec((1, 2, tm, f), lambda j, i: (j, 0, i, 0))],
        out_specs=[pl.BlockSpec((1, 2, tm, f), lambda j, i: (j, 0, i, 0)),
                   pl.BlockSpec((1, f, d), lambda j, i: (j, 0, 0))],
        scratch=[pltpu.VMEM((f, d), jnp.float32)], sem=("parallel", "arbitrary"))(dout, w_d, gu)


def _ffn_dw_gu(fn, dgu):
    s, d = fn.shape
    f = dgu.shape[-1]
    tk = min(FFN_TILE, s)
    nk = s // tk

    def body(fn_ref, dgu_ref, dw_ref, acc):
        k = pl.program_id(1)
        x = fn_ref[...]
        pg = _dot_tn(x, dgu_ref[0, 0])
        pu = _dot_tn(x, dgu_ref[0, 1])

        @pl.when(k == 0)
        def _():
            acc[0] = pg
            acc[1] = pu

        @pl.when(k > 0)
        def _():
            acc[0] += pg
            acc[1] += pu

        @pl.when(k == nk - 1)
        def _():
            dw_ref[0] = acc[...].astype(dw_ref.dtype)

    return _pcall(
        body, name="ffn_dw_gate_up", grid=(N_DEV, nk),
        out_shape=_sds((N_DEV, 2, d, f), WIRE),
        in_specs=[pl.BlockSpec((tk, d), lambda j, k: (k, 0)), pl.BlockSpec((1, 2, tk, f), lambda j, k: (j, 0, k, 0))],
        out_specs=pl.BlockSpec((1, 2, d, f), lambda j, k: (j, 0, 0, 0)),
        scratch=[pltpu.VMEM((2, d, f), jnp.float32)], sem=("parallel", "arbitrary"))(fn, dgu)


def _ffn_dfn(dgu, w_gu):
    _, _, s, f = dgu.shape
    d = w_gu.shape[2]
    tm = min(FFN_TILE, s)

    def body(dgu_ref, w_ref, dfn_ref):
        j = pl.program_id(1)
        part = _dot_nt(dgu_ref[0, 0], w_ref[0, 0]) + _dot_nt(dgu_ref[0, 1], w_ref[0, 1])

        @pl.when(j == 0)
        def _():
            dfn_ref[...] = part

        @pl.when(j > 0)
        def _():
            dfn_ref[...] += part

    return _pcall(
        body, name="ffn_dfn", grid=(s // tm, N_DEV),
        out_shape=_sds((s, d), jnp.float32),
        in_specs=[pl.BlockSpec((1, 2, tm, f), lambda i, j: (j, 0, i, 0)),
                  pl.BlockSpec((1, 2, d, f), lambda i, j: (j, 0, 0, 0))],
        out_specs=pl.BlockSpec((tm, d), lambda i, j: (i, 0)),
        sem=("parallel", "arbitrary"))(dgu, w_gu)


def _ffn_norm_bwd(d_fn, dout, h1, g_ffn):
    s, d = h1.shape
    tm = min(ROW_TILE, s)

    def body(dfn_ref, do_ref, h1_ref, g_ref, dh1_ref, dg_ref):
        i = pl.program_id(0)

        @pl.when(i == 0)
        def _():
            dg_ref[...] = jnp.zeros_like(dg_ref)

        _, xn, r = _norm_fwd(h1_ref[...], g_ref[...])
        dx, dg = _norm_bwd(xn, r, g_ref[...], dfn_ref[...])
        dh1_ref[...] = do_ref[...] + dx
        dg_ref[...] += dg

    row = pl.BlockSpec((tm, d), lambda i: (i, 0))
    vec = pl.BlockSpec((1, d), lambda i: (0, 0))
    return _pcall(
        body, name="ffn_norm_bwd", grid=(s // tm,),
        out_shape=[_sds((s, d), jnp.float32), _sds((1, d), jnp.float32)],
        in_specs=[row, row, row, vec], out_specs=[row, vec], sem=("arbitrary",))(d_fn, dout, h1, g_ffn)


def _mem_bwd(proj, g_mq, km, vmm, d_y, y_m, lse):
    s = proj.shape[0]
    ml, hw = km.shape
    tm = min(FFN_TILE, s)
    scale = MEM_DIM ** -0.5

    def body(q_ref, g_ref, k_ref, v_ref, do_ref, y_ref, lse_ref, dq_ref, dk_ref, dv_ref, dg_ref):
        i = pl.program_id(0)

        @pl.when(i == 0)
        def _():
            dk_ref[...] = jnp.zeros_like(dk_ref)
            dv_ref[...] = jnp.zeros_like(dv_ref)
            dg_ref[...] = jnp.zeros_like(dg_ref)

        col = lax.broadcasted_iota(jnp.int32, (tm, MEM_HEADS), 1)
        lse_t = lse_ref[...]
        for h in range(MEM_HEADS):
            sl = slice(MEM_DIM * h, MEM_DIM * (h + 1))
            qn, xn, r = _norm_fwd(q_ref[:, sl], g_ref[...])
            lse_h = jnp.sum(jnp.where(col == h, lse_t, 0.0), -1, keepdims=True)
            p = jnp.exp(_dot_nt(qn, k_ref[:, sl]) * scale - lse_h)
            do = do_ref[:, sl]
            dd = jnp.sum(do * y_ref[:, sl], -1, keepdims=True)
            dp = _dot_nt(do, v_ref[:, sl])
            ds = (p * (dp - dd)).astype(MXU)
            dv_ref[:, sl] += _dot_tn(p, do)
            dk_ref[:, sl] += _dot_tn(ds, qn) * scale
            dx, dg = _norm_bwd(xn, r, g_ref[...], _dot(ds, k_ref[:, sl]) * scale)
            dq_ref[:, sl] = dx.astype(dq_ref.dtype)
            dg_ref[...] += dg

    full = pl.BlockSpec((ml, hw), lambda i: (0, 0))
    return _pcall(
        body, name="mem_bwd", grid=(s // tm,),
        out_shape=[_sds((s, hw), MXU), _sds((ml, hw), jnp.float32), _sds((ml, hw), jnp.float32),
                   _sds((1, MEM_DIM), jnp.float32)],
        in_specs=[pl.BlockSpec((tm, hw), lambda i: (i, C_QM // hw)), pl.BlockSpec((1, MEM_DIM), lambda i: (0, 0)),
                  full, full, pl.BlockSpec((tm, hw), lambda i: (i, 3)), pl.BlockSpec((tm, hw), lambda i: (i, 0)),
                  pl.BlockSpec((tm, MEM_HEADS), lambda i: (i, 0))],
        out_specs=[pl.BlockSpec((tm, hw), lambda i: (i, 0)), full, full,
                   pl.BlockSpec((1, MEM_DIM), lambda i: (0, 0))],
        sem=("arbitrary",))(proj, g_mq, km, vmm, d_y, y_m, lse)


def _memkv_bwd(mem, g_mem, w_mkv, g_mk, kv, memn, dk, dv):
    ml, d = mem.shape
    hw = MEM_HEADS * MEM_DIM

    def body(mem_ref, g_ref, w_ref, gk_ref, kv_ref, mn_ref, dk_ref, dv_ref, dw_ref, dgm_ref, dgk_ref):
        parts = []
        dgk = jnp.zeros((1, MEM_DIM), jnp.float32)
        for h in range(MEM_HEADS):
            sl = slice(MEM_DIM * h, MEM_DIM * (h + 1))
            _, xn, r = _norm_fwd(kv_ref[:, sl], gk_ref[...])
            dx, dg = _norm_bwd(xn, r, gk_ref[...], dk_ref[:, sl])
            parts.append(dx)
            dgk = dgk + dg
        dkv = jnp.concatenate(parts + [dv_ref[...]], axis=1).astype(MXU)
        dgk_ref[...] = dgk
        dw_ref[...] = _dot_tn(mn_ref[...], dkv).astype(dw_ref.dtype)
        d_mn = _dot_nt(dkv, w_ref[...])
        _, xn, _ = _norm_fwd(mem_ref[...], g_ref[...])
        dgm_ref[...] = jnp.sum(d_mn * xn, 0, keepdims=True)

    vm = pl.BlockSpec(memory_space=pltpu.VMEM)
    return _pcall(
        body, name="memkv_bwd",
        out_shape=[_sds((d, 2 * hw), WIRE), _sds((1, d), jnp.float32), _sds((1, MEM_DIM), jnp.float32)],
        in_specs=[vm] * 8, out_specs=[vm] * 3)(mem, g_mem, w_mkv, g_mk, kv, memn, dk, dv)


def _mla_bwd(qc, kc, v, d_y, y_b, lse):
    nh, s, _ = qc.shape
    t = min(ATT_TILE, s)
    nb = s // t
    scale = (MLA_NOPE + MLA_ROPE) ** -0.5

    def body(q_ref, k_ref, v_ref, do_ref, y_ref, lse_ref, dq_ref, dk_ref, dv_ref, dk_acc, dv_acc):
        kj, qi = pl.program_id(1), pl.program_id(2)

        @pl.when((kj == 0) & (qi == 0))
        def _():
            dq_ref[...] = jnp.zeros_like(dq_ref)

        @pl.when(qi == kj)
        def _():
            dk_acc[...] = jnp.zeros_like(dk_acc)
            dv_acc[...] = jnp.zeros_like(dv_acc)

        @pl.when(qi >= kj)
        def _():
            q, k = q_ref[0], k_ref[0]
            sc = _dot_nt(q, k) * scale
            r_i = lax.broadcasted_iota(jnp.int32, sc.shape, 0) + qi * t
            c_i = lax.broadcasted_iota(jnp.int32, sc.shape, 1) + kj * t
            p = jnp.exp(jnp.where(c_i <= r_i, sc, NEG_INF) - lse_ref[0])
            do = do_ref[...]
            dd = jnp.sum(do * y_ref[...], -1, keepdims=True)
            dp = _dot_nt(do, v_ref[0])
            ds = (p * (dp - dd) * scale).astype(MXU)
            dv_acc[...] += _dot_tn(p, do)
            dk_acc[...] += _dot_tn(ds, q)
            rows = pl.ds(pl.multiple_of(qi * t, t), t)
            dq_ref[0, rows, :] += _dot(ds, k)

        @pl.when(qi == nb - 1)
        def _():
            dk_ref[0] = dk_acc[...]
            dv_ref[0] = dv_acc[...]

    qmap = lambda h, j, i: (h, jnp.maximum(i, j), 0)
    return _pcall(
        body, name="mla_bwd", grid=(nh, nb, nb),
        out_shape=[_sds((nh, s, 256), jnp.float32), _sds((nh, s, 256), jnp.float32),
                   _sds((nh, s, MLA_V), jnp.float32)],
        in_specs=[pl.BlockSpec((1, t, 256), qmap),
                  pl.BlockSpec((1, t, 256), lambda h, j, i: (h, j, 0)),
                  pl.BlockSpec((1, t, MLA_V), lambda h, j, i: (h, j, 0)),
                  pl.BlockSpec((t, MLA_V), lambda h, j, i: (jnp.maximum(i, j), 8 + h)),
                  pl.BlockSpec((t, MLA_V), lambda h, j, i: (jnp.maximum(i, j), h)),
                  pl.BlockSpec((1, t, 1), qmap)],
        out_specs=[pl.BlockSpec((1, s, 256), lambda h, j, i: (h, 0, 0)),
                   pl.BlockSpec((1, t, 256), lambda h, j, i: (h, j, 0)),
                   pl.BlockSpec((1, t, MLA_V), lambda h, j, i: (h, j, 0))],
        scratch=[pltpu.VMEM((t, 256), jnp.float32), pltpu.VMEM((t, MLA_V), jnp.float32)],
        sem=("parallel", "arbitrary", "arbitrary"))(qc, kc, v, d_y, y_b, lse)


def _mla_prep_bwd(proj, cos, sin, g_cq, g_ckv, w_uq, w_ukv, g_qn, g_qr, g_kn, g_kr,
                  qb, kvb, cqn, ckvn, dqc, dkc, dv):
    s = proj.shape[0]
    tm = min(ROW_TILE, s)
    nh = MLA_HEADS
    ni = s // tm

    def body(cq_ref, ckv_ref, kr_ref, cos_ref, sin_ref, gcq_ref, gckv_ref, wuq_ref, wukv_ref,
             gqn_ref, gqr_ref, gkn_ref, gkr_ref, qb_ref, kvb_ref, cqn_ref, ckvn_ref, dqc_ref, dkc_ref, dv_ref,
             dcq_ref, dckv_ref, dkr_ref, dwuq_ref, dwukv_ref,
             dgcq_ref, dgckv_ref, dgqn_ref, dgqr_ref, dgkn_ref, dgkr_ref, acc_uq, acc_ukv):
        i = pl.program_id(0)

        @pl.when(i == 0)
        def _():
            acc_uq[...] = jnp.zeros_like(acc_uq)
            acc_ukv[...] = jnp.zeros_like(acc_ukv)
            for ref in (dgcq_ref, dgckv_ref, dgqn_ref, dgqr_ref, dgkn_ref, dgkr_ref):
                ref[...] = jnp.zeros_like(ref)

        cos_t, sin_t = cos_ref[...], sin_ref[...]
        lo = _lo_mask((tm, LANES))
        qb_v, kvb_v = qb_ref[...], kvb_ref[...]
        dq_parts, dgqn = [], jnp.zeros((1, LANES), jnp.float32)
        for h in range(nh):
            _, xn, r = _norm_fwd(qb_v[:, MLA_NOPE * h: MLA_NOPE * (h + 1)], gqn_ref[...])
            dx, dg = _norm_bwd(xn, r, gqn_ref[...], dqc_ref[h][:, :MLA_NOPE])
            dq_parts.append(dx)
            dgqn = dgqn + dg
        dgqn_ref[...] += dgqn
        dgqr = jnp.zeros((1, LANES), jnp.float32)
        for j in range(nh // 2):
            d_rope = jnp.where(lo, dqc_ref[2 * j][:, MLA_NOPE:], dqc_ref[2 * j + 1][:, MLA_NOPE:])
            d_pre = _rope_bwd(d_rope, cos_t, sin_t)
            xr = qb_v[:, nh * MLA_NOPE + LANES * j: nh * MLA_NOPE + LANES * (j + 1)]
            _, xn, r = _norm_fwd(xr, gqr_ref[...], half=True)
            dx, dg = _norm_bwd(xn, r, gqr_ref[...], d_pre, half=True)
            dq_parts.append(dx)
            dgqr = dgqr + dg
        dgqr_ref[...] += dgqr
        dqb = jnp.concatenate(dq_parts, axis=1).astype(MXU)
        acc_uq[...] += _dot_tn(cqn_ref[...], dqb)
        _, xn, r = _norm_fwd(cq_ref[...], gcq_ref[...])
        dx, dg = _norm_bwd(xn, r, gcq_ref[...], _dot_nt(dqb, wuq_ref[...]))
        dcq_ref[...] = dx.astype(dcq_ref.dtype)
        dgcq_ref[...] += dg
        dkv_parts, dgkn = [], jnp.zeros((1, LANES), jnp.float32)
        d_kr2 = jnp.zeros((tm, LANES), jnp.float32)
        for h in range(nh):
            _, xn, r = _norm_fwd(kvb_v[:, 256 * h: 256 * h + MLA_NOPE], gkn_ref[...])
            dx, dg = _norm_bwd(xn, r, gkn_ref[...], dkc_ref[h][:, :MLA_NOPE])
            dkv_parts += [dx, dv_ref[h]]
            dgkn = dgkn + dg
            d_kr2 = d_kr2 + dkc_ref[h][:, MLA_NOPE:]
        dgkn_ref[...] += dgkn
        dkvb = jnp.concatenate(dkv_parts, axis=1).astype(MXU)
        acc_ukv[...] += _dot_tn(ckvn_ref[...], dkvb)
        _, xn, r = _norm_fwd(ckv_ref[...], gckv_ref[...])
        dx, dg = _norm_bwd(xn, r, gckv_ref[...], _dot_nt(dkvb, wukv_ref[...]))
        dckv_ref[...] = dx.astype(dckv_ref.dtype)
        dgckv_ref[...] += dg
        d_kr = jnp.where(lo, d_kr2 + pltpu.roll(d_kr2, 64, 1), 0.0)
        d_pre = _rope_bwd(d_kr, cos_t, sin_t)
        _, xn, r = _norm_fwd(kr_ref[...], gkr_ref[...], half=True)
        dx, dg = _norm_bwd(xn, r, gkr_ref[...], d_pre, half=True)
        dkr_ref[...] = jnp.where(lo, dx, 0.0).astype(dkr_ref.dtype)
        dgkr_ref[...] += jnp.where(_lo_mask((1, LANES)), dg, 0.0)

        @pl.when(i == ni - 1)
        def _():
            dwuq_ref[...] = acc_uq[...].astype(dwuq_ref.dtype)
            dwukv_ref[...] = acc_ukv[...].astype(dwukv_ref.dtype)

    def col(width, start):
        return pl.BlockSpec((tm, width), lambda i: (i, start // width))

    def full(shape):
        return pl.BlockSpec(shape, lambda i: (0,) * len(shape))

    def row(width):
        return pl.BlockSpec((tm, width), lambda i: (i, 0))

    def heads(width):
        return pl.BlockSpec((nh, tm, width), lambda i: (0, i, 0))

    vec = full((1, LANES))
    return _pcall(
        body, name="mla_prep_bwd", grid=(ni,),
        out_shape=[_sds((s, 512), MXU), _sds((s, 512), MXU), _sds((s, LANES), MXU),
                   _sds((512, 768), WIRE), _sds((512, 1024), WIRE),
                   _sds((1, 512), jnp.float32), _sds((1, 512), jnp.float32)] + [_sds((1, LANES), jnp.float32)] * 4,
        in_specs=[col(512, C_CQ), col(512, C_CKV), col(LANES, C_KR), row(LANES), row(LANES),
                  full((1, 512)), full((1, 512)), full((512, 768)), full((512, 1024)), vec, vec, vec, vec,
                  row(768), row(1024), row(512), row(512), heads(256), heads(256), heads(MLA_V)],
        out_specs=[row(512), row(512), row(LANES), full((512, 768)), full((512, 1024)),
                   full((1, 512)), full((1, 512)), vec, vec, vec, vec],
        scratch=[pltpu.VMEM((512, 768), jnp.float32), pltpu.VMEM((512, 1024), jnp.float32)],
        sem=("arbitrary",))(proj, proj, proj, cos, sin, g_cq, g_ckv, w_uq, w_ukv, g_qn, g_qr, g_kn, g_kr,
                            qb, kvb, cqn, ckvn, dqc, dkc, dv)


def _swa_bwd(proj, posc, posr, gq, gk, sinks, d_y, y_a, lse):
    s = proj.shape[0]
    b = SWA_BLOCK
    nb = s // b
    scale = SWA_DIM ** -0.5

    def body(q_ref, kp_ref, kc_ref, vp_ref, vc_ref, pq_ref, pkp_ref, pkc_ref, gq_ref, gk_ref, sink_ref,
             do_ref, y_ref, lse_ref, kfull_ref,
             dq_ref, dk_ref, dv_ref, dgq_ref, dgk_ref, dsink_ref, dk_acc, dv_acc):
        n = pl.program_id(0)

        @pl.when(n == 0)
        def _():
            dk_acc[...] = jnp.zeros_like(dk_acc)
            dv_acc[...] = jnp.zeros_like(dv_acc)
            dgq_ref[...] = jnp.zeros_like(dgq_ref)
            dsink_ref[...] = jnp.zeros_like(dsink_ref)

        kn, v, dist, valid = _swa_common(n, kp_ref[...], kc_ref[...], vp_ref[...], vc_ref[...],
                                         pq_ref[...], pkp_ref[...], pkc_ref[...], gk_ref[...])
        lo = _lo_mask((b, LANES))
        col = lax.broadcasted_iota(jnp.int32, (b, SWA_Q_HEADS), 1)
        col1 = lax.broadcasted_iota(jnp.int32, (1, SWA_Q_HEADS), 1)
        lse_t = lse_ref[...]
        dk_blk = jnp.zeros((2 * b, LANES), jnp.float32)
        dv_blk = jnp.zeros((2 * b, LANES), jnp.float32)
        dgq = jnp.zeros((1, LANES), jnp.float32)
        dsink = jnp.zeros((1, SWA_Q_HEADS), jnp.float32)
        for j in range(SWA_Q_HEADS // 2):
            hk = (2 * j) // (SWA_Q_HEADS // SWA_KV_HEADS)
            kvmask = lo if hk == 0 else jnp.logical_not(lo)
            sl = slice(LANES * j, LANES * (j + 1))
            qn, xn, r = _norm_fwd(q_ref[:, sl], gq_ref[...], half=True)
            qsw = pltpu.roll(qn, 64, 1)
            d2 = do_ref[:, sl]
            d2sw = pltpu.roll(d2, 64, 1)
            prod = d2 * y_ref[:, sl]
            dqs = []
            for e in range(2):
                h = 2 * j + e
                half_e = lo if e == 0 else jnp.logical_not(lo)
                qm = jnp.where(kvmask, qn if e == hk else qsw, 0.0)
                dm = jnp.where(kvmask, d2 if e == hk else d2sw, 0.0)
                sc = _dot_nt(qm, kn) * scale - _alibi_slope(h) * dist
                sc = jnp.where(valid, sc, NEG_INF)
                lse_h = jnp.sum(jnp.where(col == h, lse_t, 0.0), -1, keepdims=True)
                p = jnp.exp(sc - lse_h)
                dd = jnp.sum(jnp.where(half_e, prod, 0.0), -1, keepdims=True)
                dp = _dot_nt(dm, v)
                ds = (p * (dp - dd)).astype(MXU)
                dsink = dsink - jnp.where(col1 == h, jnp.sum(jnp.exp(sink_ref[h] - lse_h) * dd), 0.0)
                dq_m = _dot(ds, kn) * scale
                dk_blk = dk_blk + _dot_tn(ds, qm) * scale
                dv_blk = dv_blk + _dot_tn(p, dm)
                dqs.append(dq_m if e == hk else pltpu.roll(dq_m, 64, 1))
            dx, dg = _norm_bwd(xn, r, gq_ref[...], jnp.where(lo, dqs[0], dqs[1]), half=True)
            dq_ref[:, sl] = dx.astype(dq_ref.dtype)
            dgq = dgq + dg
        dgq_ref[...] += dgq
        dsink_ref[...] += dsink
        prev = pl.ds(pl.multiple_of(jnp.maximum(n - 1, 0) * b, b), b)
        cur = pl.ds(pl.multiple_of(n * b, b), b)
        dk_acc[prev, :] += dk_blk[:b]
        dv_acc[prev, :] += dv_blk[:b]
        dk_acc[cur, :] += dk_blk[b:]
        dv_acc[cur, :] += dv_blk[b:]

        @pl.when(n == nb - 1)
        def _():
            _, kxn, kr = _norm_fwd(kfull_ref[...], gk_ref[...], half=True)
            dx, dg = _norm_bwd(kxn, kr, gk_ref[...], dk_acc[...], half=True)
            dk_ref[...] = dx.astype(dk_ref.dtype)
            dv_ref[...] = dv_acc[...].astype(dv_ref.dtype)
            dgk_ref[...] = dg

    full = pl.BlockSpec((s, LANES), lambda n: (0, 0))
    vec = pl.BlockSpec((1, LANES), lambda n: (0, 0))
    return _pcall(
        body, name="swa_bwd", grid=(nb,),
        out_shape=[_sds((s, 1024), MXU), _sds((s, LANES), MXU), _sds((s, LANES), MXU),
                   _sds((1, LANES), jnp.float32), _sds((1, LANES), jnp.float32),
                   _sds((1, SWA_Q_HEADS), jnp.float32)],
        in_specs=_swa_specs(s) + [pl.BlockSpec((b, 1024), lambda n: (n, 0)), pl.BlockSpec((b, 1024), lambda n: (n, 0)),
                                  pl.BlockSpec((b, SWA_Q_HEADS), lambda n: (n, 0)),
                                  pl.BlockSpec((s, LANES), lambda n: (0, C_KA // LANES))],
        out_specs=[pl.BlockSpec((b, 1024), lambda n: (n, 0)), full, full, vec, vec,
                   pl.BlockSpec((1, SWA_Q_HEADS), lambda n: (0, 0))],
        scratch=[pltpu.VMEM((s, LANES), jnp.float32), pltpu.VMEM((s, LANES), jnp.float32)],
        sem=("arbitrary",))(proj, proj, proj, proj, proj, posc, posr, posr, gq, gk, sinks, d_y, y_a, lse, proj)


def _dx(d_proj, w_in, x, g, d_h1):
    s, d = x.shape
    n = w_in.shape[1]
    tm = min(ROW_TILE, s)

    def body(dp_ref, w_ref, x_ref, g_ref, dh_ref, dx_ref, dg_ref):
        i = pl.program_id(0)

        @pl.when(i == 0)
        def _():
            dg_ref[...] = jnp.zeros_like(dg_ref)

        d_hn = _dot_nt(dp_ref[...], w_ref[...])
        _, xn, r = _norm_fwd(x_ref[...], g_ref[...])
        dx, dg = _norm_bwd(xn, r, g_ref[...], d_hn)
        dx_ref[...] = dh_ref[...] + dx
        dg_ref[...] += dg

    row = pl.BlockSpec((tm, d), lambda i: (i, 0))
    vec = pl.BlockSpec((1, d), lambda i: (0, 0))
    return _pcall(
        body, name="grad_x", grid=(s // tm,),
        out_shape=[_sds((s, d), jnp.float32), _sds((1, d), jnp.float32)],
        in_specs=[pl.BlockSpec((tm, n), lambda i: (i, 0)), pl.BlockSpec((d, n), lambda i: (0, 0)), row, vec, row],
        out_specs=[row, vec], sem=("arbitrary",))(d_proj, w_in, x, g, d_h1)


_SMALL = ["attn_norm_g", "swa_q_norm_g", "swa_k_norm_g", "swa_sinks", "mla_cq_norm_g", "mla_ckv_norm_g",
          "mla_qn_norm_g", "mla_qr_norm_g", "mla_kn_norm_g", "mla_kr_norm_g", "mem_norm_g",
          "mem_q_norm_g", "mem_k_norm_g", "ffn_norm_g"]


def _pack_rows(v):
    n = v.shape[-1]
    rows = -(-n // LANES)
    rows8 = -(-rows // 8) * 8
    flat = jnp.pad(v.reshape(-1), (0, rows8 * LANES - n))
    return flat.reshape(rows8, LANES)


def _pack(parts):
    return jnp.concatenate([_pack_rows(p) for p in parts], axis=0)


def _unpack(packed, sizes):
    out, r = [], 0
    for n in sizes:
        rows = -(-n // LANES)
        rows8 = -(-rows // 8) * 8
        out.append(packed[r:r + rows8].reshape(-1)[:n].reshape(1, n))
        r += rows8
    return out


def _fold64(v):
    return v[:, :64] + v[:, 64:]


def kernel(x, mem, positions, attn_norm_g, w_in, swa_q_norm_g, swa_k_norm_g, swa_sinks, mla_cq_norm_g, mla_ckv_norm_g, w_uq, w_ukv, mla_qn_norm_g, mla_qr_norm_g, mla_kn_norm_g, mla_kr_norm_g, mem_norm_g, w_mem_kv, mem_q_norm_g, mem_k_norm_g, w_out, ffn_norm_g, w_gate, w_up, w_down, loss_target, m_attn_norm_g, m_w_in, m_swa_q_norm_g, m_swa_k_norm_g, m_swa_sinks, m_mla_cq_norm_g, m_mla_ckv_norm_g, m_w_uq, m_w_ukv, m_mla_qn_norm_g, m_mla_qr_norm_g, m_mla_kn_norm_g, m_mla_kr_norm_g, m_mem_norm_g, m_w_mem_kv, m_mem_q_norm_g, m_mem_k_norm_g, m_w_out, m_ffn_norm_g, m_w_gate, m_w_up, m_w_down, v_attn_norm_g, v_w_in, v_swa_q_norm_g, v_swa_k_norm_g, v_swa_sinks, v_mla_cq_norm_g, v_mla_ckv_norm_g, v_w_uq, v_w_ukv, v_mla_qn_norm_g, v_mla_qr_norm_g, v_mla_kn_norm_g, v_mla_kr_norm_g, v_mem_norm_g, v_w_mem_kv, v_mem_q_norm_g, v_mem_k_norm_g, v_w_out, v_ffn_norm_g, v_w_gate, v_w_up, v_w_down):
    args = dict(locals())
    x2, mem2, tgt = x[0], mem[0], loss_target[0]
    s, d = x2.shape
    n_in = w_in.shape[2]
    f = w_gate.shape[2]

    shards = [w_in[0].astype(WIRE), w_uq[0].astype(WIRE), w_ukv[0].astype(WIRE), w_mem_kv[0].astype(WIRE),
              w_out[0].astype(WIRE), jnp.stack([w_gate[0], w_up[0]]).astype(WIRE), w_down[0].astype(WIRE)]
    g_in, g_uq, g_ukv, g_mkv, g_out, w_gu, w_d = _all_gather(shards)
    wi = g_in.transpose(1, 0, 2).reshape(d, N_DEV * n_in)
    wi = jnp.concatenate([wi[:, 0:1024], wi[:, 1280:1792], wi[:, 1792:2304], wi[:, 2368:2880],
                          wi[:, 1024:1152], wi[:, 1152:1280], wi[:, 2304:2368],
                          jnp.zeros((d, IN_PAD - 2880), wi.dtype)], axis=1)
    wq = g_uq.transpose(1, 0, 2).reshape(512, 768)
    wq = jnp.concatenate([wq[:, 192 * h: 192 * h + 128] for h in range(4)]
                         + [wq[:, 192 * h + 128: 192 * (h + 1)] for h in range(4)], axis=1)
    wkv = g_ukv.transpose(1, 0, 2).reshape(512, 1024)
    wmkv = g_mkv.reshape(-1, g_mkv.shape[-1])
    wo = g_out.reshape(-1, d)

    pos = positions[0].astype(jnp.float32)
    inv_freq = ROPE_THETA ** (-jnp.arange(0, MLA_ROPE, 2, dtype=jnp.float32) / MLA_ROPE)
    ang = pos[:, None] * inv_freq
    cos32, sin32 = jnp.cos(ang), jnp.sin(ang)
    cos_t = jnp.tile(cos32, (1, 4))
    sin_t = jnp.tile(jnp.concatenate([-sin32, sin32], axis=1), (1, 2))
    posc, posr = pos.reshape(s, 1), pos.reshape(1, s)
    two = lambda g: jnp.tile(g, (1, 2))
    gq2, gk2, gqr2, gkr2 = two(swa_q_norm_g), two(swa_k_norm_g), two(mla_qr_norm_g), two(mla_kr_norm_g)
    sinks1 = swa_sinks[0]

    proj, hn = _in_proj(x2, attn_norm_g, wi)
    qc, kc, vb, qb, kvb, cqn, ckvn = _mla_prep(proj, cos_t, sin_t, mla_cq_norm_g, mla_ckv_norm_g, wq, wkv,
                                                mla_qn_norm_g, gqr2, mla_kn_norm_g, gkr2)
    y_b, lse_b = _mla_fwd(qc, kc, vb)
    km, vmm, kvm, memn = _memkv_prep(mem2, mem_norm_g, wmkv, mem_k_norm_g)
    y_m, lse_m = _mem_fwd(proj, mem_q_norm_g, km, vmm)
    y_a, lse_a = _swa_fwd(proj, posc, posr, gq2, gk2, sinks1)
    h1, fn = _out_proj(y_a, y_b, y_m, x2, wo, ffn_norm_g)
    gu, act = _ffn_gu(fn, w_gu)
    dout, loss_tile = _ffn_down(act, w_d, h1, tgt)

    dgu, dw_d = _ffn_bwd_act(dout, w_d, gu)
    dw_gu = _ffn_dw_gu(fn, dgu)
    d_h1, dg_ffn = _ffn_norm_bwd(_ffn_dfn(dgu, w_gu), dout, h1, ffn_norm_g)
    d_y = _mm(d_h1, wo, tb=True, out_dtype=jnp.float32, tm=FFN_TILE, tk=512, name="d_mix")
    dw_out = jnp.concatenate([
        _mm(y_a, d_h1, ta=True, out_dtype=WIRE, tm=1024, tk=512, name="dw_out_a"),
        _mm(y_b, d_h1, ta=True, out_dtype=WIRE, tm=1024, tk=512, name="dw_out_b"),
        _mm(y_m, d_h1, ta=True, out_dtype=WIRE, tm=1024, tk=512, name="dw_out_m")], axis=0)
    d_qm, dkm, dvmm, dg_mq = _mem_bwd(proj, mem_q_norm_g, km, vmm, d_y, y_m, lse_m)
    dw_mkv, dg_mem, dg_mk = _memkv_bwd(mem2, mem_norm_g, wmkv, mem_k_norm_g, kvm, memn, dkm, dvmm)
    dqc, dkc, dvb = _mla_bwd(qc, kc, vb, d_y, y_b, lse_b)
    (d_cq, d_ckv, d_kr, dw_uq, dw_ukv, dg_cq, dg_ckv, dg_qn, dg_qr, dg_kn, dg_kr) = _mla_prep_bwd(
        proj, cos_t, sin_t, mla_cq_norm_g, mla_ckv_norm_g, wq, wkv, mla_qn_norm_g, gqr2, mla_kn_norm_g, gkr2,
        qb, kvb, cqn, ckvn, dqc, dkc, dvb)
    d_qa, d_ka, d_va, dg_q, dg_k, d_sinks = _swa_bwd(proj, posc, posr, gq2, gk2, sinks1, d_y, y_a, lse_a)
    d_proj = jnp.concatenate([d_qa, d_cq, d_ckv, d_qm, d_ka, d_va, d_kr], axis=1)
    dw_in = _mm(hn, d_proj, ta=True, out_dtype=WIRE, tm=512, tk=512, name="dw_in")
    grad_x, dg_attn = _dx(d_proj, wi, x2, attn_norm_g, d_h1)

    gi = jnp.concatenate([dw_in[:, C_QA:C_QA + 1024], dw_in[:, C_KA:C_KA + 128], dw_in[:, C_VA:C_VA + 128],
                          dw_in[:, C_CQ:C_CQ + 512], dw_in[:, C_CKV:C_CKV + 512], dw_in[:, C_KR:C_KR + 64],
                          dw_in[:, C_QM:C_QM + 512]], axis=1)
    gi = gi.reshape(d, N_DEV, n_in).transpose(1, 0, 2)
    gq_ = jnp.concatenate(sum([[dw_uq[:, 128 * h: 128 * (h + 1)], dw_uq[:, 512 + 64 * h: 512 + 64 * (h + 1)]]
                               for h in range(4)], []), axis=1)
    gq_ = gq_.reshape(512, N_DEV, 96).transpose(1, 0, 2)
    gkv = dw_ukv.reshape(512, N_DEV, 128).transpose(1, 0, 2)
    grads = [gi, gq_, gkv, dw_mkv.reshape(g_mkv.shape), dw_out.reshape(g_out.shape), dw_gu, dw_d]
    recv = _exchange_grads(grads)

    big = {}
    def adam(name, r, stacked=False):
        w, m, v = args[name][0], args["m_" + name][0], args["v_" + name][0]
        return _adam_big(r.reshape(N_DEV, -1, r.shape[-1]), w, m, v, "adam_" + name)
    for name, r in zip(["w_in", "w_uq", "w_ukv", "w_mem_kv", "w_out"], recv[:5]):
        big[name] = [o[None] for o in adam(name, r)]
    big["w_down"] = [o[None] for o in adam("w_down", recv[6])]
    r_gu = recv[5]
    big["w_gate"] = [o[None] for o in adam("w_gate", r_gu[:, 0])]
    big["w_up"] = [o[None] for o in adam("w_up", r_gu[:, 1])]

    small_g = {
        "attn_norm_g": dg_attn, "swa_q_norm_g": _fold64(dg_q), "swa_k_norm_g": _fold64(dg_k),
        "swa_sinks": d_sinks, "mla_cq_norm_g": dg_cq, "mla_ckv_norm_g": dg_ckv, "mla_qn_norm_g": dg_qn,
        "mla_qr_norm_g": _fold64(dg_qr), "mla_kn_norm_g": dg_kn, "mla_kr_norm_g": _fold64(dg_kr),
        "mem_norm_g": dg_mem, "mem_q_norm_g": dg_mq, "mem_k_norm_g": dg_mk, "ffn_norm_g": dg_ffn}
    sizes = [args[n].shape[-1] for n in _SMALL]
    pg = _pack([small_g[n] for n in _SMALL] + [loss_tile[0:1, 0:1]])
    zero = jnp.zeros((1, 1), jnp.float32)
    pw = _pack([args[n] for n in _SMALL] + [zero])
    pm = _pack([args["m_" + n] for n in _SMALL] + [zero])
    pv = _pack([args["v_" + n] for n in _SMALL] + [zero])
    sg, sd, sm, sv = _small_allreduce_adam(pg, pw, pm, pv)
    small = {n: vals for n, vals in zip(_SMALL, zip(*[_unpack(p, sizes) for p in (sg, sd, sm, sv)]))}
    loss = _unpack(sg, sizes + [1])[-1].reshape(())

    order = ["attn_norm_g", "w_in", "swa_q_norm_g", "swa_k_norm_g", "swa_sinks", "mla_cq_norm_g", "mla_ckv_norm_g",
             "w_uq", "w_ukv", "mla_qn_norm_g", "mla_qr_norm_g", "mla_kn_norm_g", "mla_kr_norm_g", "mem_norm_g",
             "w_mem_kv", "mem_q_norm_g", "mem_k_norm_g", "w_out", "ffn_norm_g", "w_gate", "w_up", "w_down"]
    res = {n: (big[n] if n in big else list(small[n])) for n in order}
    outs = [loss, grad_x[None]]
    for kind in range(4):
        outs += [res[n][kind] for n in order]
    return tuple(outs)
```

```python
import jax
import jax.numpy as jnp
from jax import lax
from jax.experimental import pallas as pl
from jax.experimental.pallas import tpu as pltpu
from jax.experimental.pallas import tpu_sc as plsc

MXU = jnp.bfloat16
WIRE = jnp.bfloat16
EPS = 1e-6
NEG_INF = -1e30
N_DEV = 8
LANES = 128
ROW_TILE = 256
FFN_TILE = 512
ATT_TILE = 1024
SWA_BLOCK = 128
VMEM_LIMIT = 56 * 1024 * 1024

SWA_Q_HEADS, SWA_KV_HEADS, SWA_DIM = 16, 2, 64
MLA_HEADS, MLA_NOPE, MLA_ROPE, MLA_V = 4, 128, 64, 128
MEM_HEADS, MEM_DIM = 4, 128
ROPE_THETA = 10000.0
ADAM_LR, ADAM_B1, ADAM_B2, ADAM_EPS, ADAM_WD, ADAM_STEP = 0.001, 0.9, 0.999, 1e-08, 0.01, 10

C_QA, C_CQ, C_CKV, C_QM, C_KA, C_VA, C_KR, IN_PAD = 0, 1024, 1536, 2048, 2560, 2688, 2816, 2944


def _pcall(body, *, name, out_shape, in_specs, out_specs, grid=(), scratch=(), sem=None):
    params = pltpu.CompilerParams(dimension_semantics=sem, vmem_limit_bytes=VMEM_LIMIT)
    return pl.pallas_call(body, name=name, grid=grid, in_specs=in_specs, out_specs=out_specs,
                          out_shape=out_shape, scratch_shapes=list(scratch), compiler_params=params)


def _sds(shape, dtype):
    return jax.ShapeDtypeStruct(tuple(shape), dtype)


def _dot(a, b):
    return jnp.dot(a.astype(MXU), b.astype(MXU), preferred_element_type=jnp.float32)


def _dot_nt(a, b):
    return lax.dot_general(a.astype(MXU), b.astype(MXU), (((1,), (1,)), ((), ())),
                           preferred_element_type=jnp.float32)


def _dot_tn(a, b):
    return lax.dot_general(a.astype(MXU), b.astype(MXU), (((0,), (0,)), ((), ())),
                           preferred_element_type=jnp.float32)


def _lo_mask(shape):
    return (lax.broadcasted_iota(jnp.int32, shape, len(shape) - 1) % LANES) < 64


def _norm_fwd(x, g, half=False):
    x2 = x * x
    if half:
        lo = _lo_mask(x.shape)
        s_lo = jnp.sum(jnp.where(lo, x2, 0.0), -1, keepdims=True)
        s_hi = jnp.sum(jnp.where(lo, 0.0, x2), -1, keepdims=True)
        r = jnp.where(lo, lax.rsqrt(s_lo / 64.0 + EPS), lax.rsqrt(s_hi / 64.0 + EPS))
    else:
        r = lax.rsqrt(jnp.mean(x2, -1, keepdims=True) + EPS)
    xn = x * r
    return xn * g, xn, r


def _norm_bwd(xn, r, g, dy, half=False):
    t = dy * g
    tx = t * xn
    if half:
        lo = _lo_mask(xn.shape)
        m_lo = jnp.sum(jnp.where(lo, tx, 0.0), -1, keepdims=True) / 64.0
        m_hi = jnp.sum(jnp.where(lo, 0.0, tx), -1, keepdims=True) / 64.0
        m = jnp.where(lo, m_lo, m_hi)
    else:
        m = jnp.mean(tx, -1, keepdims=True)
    dx = r * (t - xn * m)
    dg = jnp.sum(dy * xn, 0, keepdims=True)
    return dx, dg


def _swap32(x):
    lane = lax.broadcasted_iota(jnp.int32, x.shape, 1)
    return jnp.where((lane % 64) < 32, pltpu.roll(x, 96, 1), pltpu.roll(x, 32, 1))


def _rope(x, cos, sin):
    return x * cos + _swap32(x) * sin


def _rope_bwd(d, cos, sin):
    return d * cos + _swap32(d * sin)


def _my_coords():
    return lax.axis_index("x"), lax.axis_index("y"), lax.axis_index("c")


def _dev_index(px, py, pc):
    return 4 * px + 2 * py + pc


_FLIPS = [(0, 0, 1), (0, 1, 0), (0, 1, 1), (1, 0, 0), (1, 0, 1), (1, 1, 0), (1, 1, 1)]


def _flip(coords, f):
    return tuple((1 - v) if b else v for v, b in zip(coords, f))


def _all_gather(shards):
    n = len(shards)

    def body(*refs):
        ins, outs = refs[:n], refs[n:2 * n]
        send_sems, recv_sems, local_sems = refs[2 * n:]
        x, y, c = _my_coords()
        me, sibling = (x, y, c), (x, y, 1 - c)
        chips = [(1 - x, y), (x, 1 - y), (1 - x, 1 - y)]

        def copy(w, k, block, to, src=None):
            dst = outs[w].at[_dev_index(*block)]
            return pltpu.make_async_remote_copy(
                src_ref=dst if src is None else src, dst_ref=dst,
                send_sem=send_sems.at[w, k], recv_sem=recv_sems.at[w, k],
                device_id=to, device_id_type=pl.DeviceIdType.MESH)

        sends, locals_ = [], []
        for w in range(n):
            mine = pltpu.make_async_copy(ins[w], outs[w].at[_dev_index(*me)], local_sems.at[w])
            mine.start()
            locals_.append(mine)
            first = [copy(w, 0, me, sibling, src=ins[w])]
            first += [copy(w, 1 + j, me, (*chip, c), src=ins[w]) for j, chip in enumerate(chips)]
            for cp in first:
                cp.start()
            sends += first
        for w in range(n):
            for j, chip in enumerate(chips):
                copy(w, 1 + j, (*chip, c), me).wait_recv()
                fwd = copy(w, 4 + j, (*chip, c), sibling)
                fwd.start()
                sends.append(fwd)
        for w in range(n):
            copy(w, 0, sibling, me).wait_recv()
            for j, chip in enumerate(chips):
                copy(w, 4 + j, (*chip, 1 - c), me).wait_recv()
        for cp in sends:
            cp.wait_send()
        for mine in locals_:
            mine.wait()

    any_spec = pl.BlockSpec(memory_space=pl.ANY)
    return _pcall(
        body, name="all_gather_weights",
        out_shape=[_sds((N_DEV,) + s.shape, s.dtype) for s in shards],
        in_specs=[any_spec] * n, out_specs=[any_spec] * n,
        scratch=[pltpu.SemaphoreType.DMA((n, 7)), pltpu.SemaphoreType.DMA((n, 7)),
                 pltpu.SemaphoreType.DMA((n,))])(*shards)


def _all_gather_background(shards, collective_id, name):
    n = len(shards)
    src_refs = [jax.new_ref(s, memory_space=pltpu.MemorySpace.HBM) for s in shards]
    out_refs = [jax.empty_ref(_sds((N_DEV,) + s.shape, s.dtype), memory_space=pltpu.MemorySpace.HBM) for s in shards]

    @pl.kernel(mesh=plsc.ScalarSubcoreMesh(axis_name="seq", num_cores=1), name=name,
               scratch_types=(pltpu.SemaphoreType.DMA((n, 7)), pltpu.SemaphoreType.DMA((n, 7)),
                              pltpu.SemaphoreType.DMA((n,))),
               compiler_params=pltpu.CompilerParams(collective_id=collective_id))
    def launch(send_sems, recv_sems, local_sems):
        x, y, c = _my_coords()
        me, sibling = (x, y, c), (x, y, 1 - c)
        chips = [(1 - x, y), (x, 1 - y), (1 - x, 1 - y)]
        barrier = pltpu.get_barrier_semaphore()
        for peer in [sibling] + [(*chip, c) for chip in chips]:
            pl.semaphore_signal(barrier, inc=1, device_id=peer, device_id_type=pl.DeviceIdType.MESH)
        pl.semaphore_wait(barrier, 4)

        def copy(w, k, block, to, src=None):
            dst = out_refs[w].at[_dev_index(*block)]
            return pltpu.make_async_remote_copy(
                src_ref=dst if src is None else src, dst_ref=dst,
                send_sem=send_sems.at[w, k], recv_sem=recv_sems.at[w, k],
                device_id=to, device_id_type=pl.DeviceIdType.MESH)

        sends, locals_ = [], []
        for w in range(n):
            mine = pltpu.make_async_copy(src_refs[w], out_refs[w].at[_dev_index(*me)], local_sems.at[w])
            mine.start()
            locals_.append(mine)
            first = [copy(w, 0, me, sibling, src=src_refs[w])]
            first += [copy(w, 1 + j, me, (*chip, c), src=src_refs[w]) for j, chip in enumerate(chips)]
            for cp in first:
                cp.start()
            sends += first
        for w in range(n):
            for j, chip in enumerate(chips):
                copy(w, 1 + j, (*chip, c), me).wait_recv()
                fwd = copy(w, 4 + j, (*chip, c), sibling)
                fwd.start()
                sends.append(fwd)
        for w in range(n):
            copy(w, 0, sibling, me).wait_recv()
            for j, chip in enumerate(chips):
                copy(w, 4 + j, (*chip, 1 - c), me).wait_recv()
        for cp in sends:
            cp.wait_send()
        for mine in locals_:
            mine.wait()

    launch()
    return [r[...] for r in out_refs]


def _exchange_grads(grads):
    n = len(grads)

    def body(*refs):
        ins, outs = refs[:n], refs[n:2 * n]
        send_sems, recv_sems, local_sems = refs[2 * n:]
        me = _my_coords()
        my_idx = _dev_index(*me)
        sends, locals_ = [], []
        for w in range(n):
            mine = pltpu.make_async_copy(ins[w].at[my_idx], outs[w].at[my_idx], local_sems.at[w])
            mine.start()
            locals_.append(mine)
            for k, f in enumerate(_FLIPS):
                peer = _flip(me, f)
                cp = pltpu.make_async_remote_copy(
                    src_ref=ins[w].at[_dev_index(*peer)], dst_ref=outs[w].at[my_idx],
                    send_sem=send_sems.at[w, k], recv_sem=recv_sems.at[w, k],
                    device_id=peer, device_id_type=pl.DeviceIdType.MESH)
                cp.start()
                sends.append(cp)
        for w in range(n):
            for k, f in enumerate(_FLIPS):
                peer = _flip(me, f)
                slot = outs[w].at[_dev_index(*peer)]
                pltpu.make_async_remote_copy(
                    src_ref=slot, dst_ref=slot,
                    send_sem=send_sems.at[w, k], recv_sem=recv_sems.at[w, k],
                    device_id=peer, device_id_type=pl.DeviceIdType.MESH).wait_recv()
        for cp in sends:
            cp.wait_send()
        for mine in locals_:
            mine.wait()

    any_spec = pl.BlockSpec(memory_space=pl.ANY)
    return _pcall(
        body, name="exchange_grads",
        out_shape=[_sds(g.shape, g.dtype) for g in grads],
        in_specs=[any_spec] * n, out_specs=[any_spec] * n,
        scratch=[pltpu.SemaphoreType.DMA((n, 7)), pltpu.SemaphoreType.DMA((n, 7)),
                 pltpu.SemaphoreType.DMA((n,))])(*grads)


def _adam_math(w, g, m, v):
    m = ADAM_B1 * m + (1.0 - ADAM_B1) * g
    v = ADAM_B2 * v + (1.0 - ADAM_B2) * (g * g)
    m_hat = m / (1.0 - ADAM_B1 ** ADAM_STEP)
    v_hat = v / (1.0 - ADAM_B2 ** ADAM_STEP)
    delta = -ADAM_LR * (m_hat / (jnp.sqrt(v_hat) + ADAM_EPS) + ADAM_WD * w)
    return delta, m, v


def _small_allreduce_adam(pg, pw, pm, pv):
    rows = pg.shape[0]

    def body(pg_ref, pw_ref, pm_ref, pv_ref, g_ref, d_ref, m_ref, v_ref, gath, send_sems, recv_sems):
        me = _my_coords()
        my_idx = _dev_index(*me)
        gath[my_idx] = pg_ref[...]
        sends = []
        for k, f in enumerate(_FLIPS):
            peer = _flip(me, f)
            cp = pltpu.make_async_remote_copy(
                src_ref=pg_ref, dst_ref=gath.at[my_idx],
                send_sem=send_sems.at[k], recv_sem=recv_sems.at[k],
                device_id=peer, device_id_type=pl.DeviceIdType.MESH)
            cp.start()
            sends.append(cp)
        for k, f in enumerate(_FLIPS):
            peer = _flip(me, f)
            slot = gath.at[_dev_index(*peer)]
            pltpu.make_async_remote_copy(
                src_ref=slot, dst_ref=slot, send_sem=send_sems.at[k], recv_sem=recv_sems.at[k],
                device_id=peer, device_id_type=pl.DeviceIdType.MESH).wait_recv()
        for cp in sends:
            cp.wait_send()
        g = gath[0]
        for d in range(1, N_DEV):
            g = g + gath[d]
        delta, m, v = _adam_math(pw_ref[...], g, pm_ref[...], pv_ref[...])
        g_ref[...] = g
        d_ref[...] = delta
        m_ref[...] = m
        v_ref[...] = v

    vm = pl.BlockSpec(memory_space=pltpu.VMEM)
    return _pcall(
        body, name="small_allreduce_adam",
        out_shape=[_sds(pg.shape, jnp.float32)] * 4,
        in_specs=[vm] * 4, out_specs=[vm] * 4,
        scratch=[pltpu.VMEM((N_DEV, rows, LANES), jnp.float32),
                 pltpu.SemaphoreType.DMA((7,)), pltpu.SemaphoreType.DMA((7,))])(pg, pw, pm, pv)


def _adam_big(recv, w, m, v, name):
    _, rows, cols = recv.shape
    tr = rows
    while tr * cols > 256 * 1024 and tr % 2 == 0 and (tr // 2) % 16 == 0:
        tr //= 2

    def body(r_ref, w_ref, m_ref, v_ref, g_ref, d_ref, mo_ref, vo_ref):
        g = r_ref[0].astype(jnp.float32)
        for d in range(1, N_DEV):
            g = g + r_ref[d].astype(jnp.float32)
        delta, mn, vn = _adam_math(w_ref[...], g, m_ref[...], v_ref[...])
        g_ref[...] = g
        d_ref[...] = delta
        mo_ref[...] = mn
        vo_ref[...] = vn

    blk = pl.BlockSpec((tr, cols), lambda i: (i, 0))
    return _pcall(
        body, name=name, grid=(rows // tr,),
        out_shape=[_sds((rows, cols), jnp.float32)] * 4,
        in_specs=[pl.BlockSpec((N_DEV, tr, cols), lambda i: (0, i, 0)), blk, blk, blk],
        out_specs=[blk] * 4, sem=("parallel",))(recv, w, m, v)


def _mm(a, b, *, ta=False, tb=False, out_dtype, tm, tk, name):
    (kdim, mdim) = a.shape if ta else a.shape[::-1]
    ndim = b.shape[0] if tb else b.shape[1]
    tm, tk = min(tm, mdim), min(tk, kdim)
    nk = kdim // tk

    def body(a_ref, b_ref, o_ref, acc):
        k = pl.program_id(1)
        if ta:
            part = _dot_tn(a_ref[...], b_ref[...])
        elif tb:
            part = _dot_nt(a_ref[...], b_ref[...])
        else:
            part = _dot(a_ref[...], b_ref[...])

        @pl.when(k == 0)
        def _():
            acc[...] = part

        @pl.when(k > 0)
        def _():
            acc[...] += part

        @pl.when(k == nk - 1)
        def _():
            o_ref[...] = acc[...].astype(o_ref.dtype)

    a_spec = pl.BlockSpec((tk, tm), lambda i, k: (k, i)) if ta else pl.BlockSpec((tm, tk), lambda i, k: (i, k))
    b_spec = pl.BlockSpec((ndim, tk), lambda i, k: (0, k)) if tb else pl.BlockSpec((tk, ndim), lambda i, k: (k, 0))
    return _pcall(
        body, name=name, grid=(mdim // tm, nk), out_shape=_sds((mdim, ndim), out_dtype),
        in_specs=[a_spec, b_spec], out_specs=pl.BlockSpec((tm, ndim), lambda i, k: (i, 0)),
        scratch=[pltpu.VMEM((tm, ndim), jnp.float32)], sem=("parallel", "arbitrary"))(a, b)


def _in_proj(x, g, w):
    s, d = x.shape
    n = w.shape[1]
    tm = min(ROW_TILE, s)

    def body(x_ref, g_ref, w_ref, p_ref, hn_ref):
        hn, _, _ = _norm_fwd(x_ref[...], g_ref[...])
        hn_ref[...] = hn.astype(hn_ref.dtype)
        p_ref[...] = _dot(hn, w_ref[...])

    return _pcall(
        body, name="in_proj", grid=(s // tm,),
        out_shape=[_sds((s, n), jnp.float32), _sds((s, d), MXU)],
        in_specs=[pl.BlockSpec((tm, d), lambda i: (i, 0)), pl.BlockSpec((1, d), lambda i: (0, 0)),
                  pl.BlockSpec((d, n), lambda i: (0, 0))],
        out_specs=[pl.BlockSpec((tm, n), lambda i: (i, 0)), pl.BlockSpec((tm, d), lambda i: (i, 0))],
        sem=("parallel",))(x, g, w)


def _mla_prep(proj, cos, sin, g_cq, g_ckv, w_uq, w_ukv, g_qn, g_qr, g_kn, g_kr):
    s = proj.shape[0]
    tm = min(ROW_TILE, s)
    nh = MLA_HEADS

    def body(cq_ref, ckv_ref, kr_ref, cos_ref, sin_ref, gcq_ref, gckv_ref, wuq_ref, wukv_ref,
             gqn_ref, gqr_ref, gkn_ref, gkr_ref,
             qc_ref, kc_ref, v_ref, qb_ref, kvb_ref, cqn_ref, ckvn_ref):
        cos_t, sin_t = cos_ref[...], sin_ref[...]
        lo = _lo_mask((tm, LANES))
        cqn, _, _ = _norm_fwd(cq_ref[...], gcq_ref[...])
        cqn_ref[...] = cqn.astype(cqn_ref.dtype)
        qb = _dot(cqn, wuq_ref[...])
        qb_ref[...] = qb
        ckvn, _, _ = _norm_fwd(ckv_ref[...], gckv_ref[...])
        ckvn_ref[...] = ckvn.astype(ckvn_ref.dtype)
        kvb = _dot(ckvn, wukv_ref[...])
        kvb_ref[...] = kvb
        kr, _, _ = _norm_fwd(kr_ref[...], gkr_ref[...], half=True)
        kr = _rope(kr, cos_t, sin_t)
        kr2 = jnp.where(lo, kr, pltpu.roll(kr, 64, 1))
        ropes = []
        for j in range(nh // 2):
            xr = qb[:, nh * MLA_NOPE + LANES * j: nh * MLA_NOPE + LANES * (j + 1)]
            qr, _, _ = _norm_fwd(xr, gqr_ref[...], half=True)
            ropes.append(_rope(qr, cos_t, sin_t))
        for h in range(nh):
            qn, _, _ = _norm_fwd(qb[:, MLA_NOPE * h: MLA_NOPE * (h + 1)], gqn_ref[...])
            mask = lo if h % 2 == 0 else jnp.logical_not(lo)
            qr = jnp.where(mask, ropes[h // 2], 0.0)
            qc_ref[h] = jnp.concatenate([qn, qr], axis=1).astype(qc_ref.dtype)
            kn, _, _ = _norm_fwd(kvb[:, 256 * h: 256 * h + MLA_NOPE], gkn_ref[...])
            kc_ref[h] = jnp.concatenate([kn, kr2], axis=1).astype(kc_ref.dtype)
            v_ref[h] = kvb[:, 256 * h + MLA_NOPE: 256 * (h + 1)].astype(v_ref.dtype)

    def col(width, start):
        return pl.BlockSpec((tm, width), lambda i: (i, start // width))

    def full(shape):
        return pl.BlockSpec(shape, lambda i: (0,) * len(shape))

    def row(width):
        return pl.BlockSpec((tm, width), lambda i: (i, 0))

    def heads(width):
        return pl.BlockSpec((nh, tm, width), lambda i: (0, i, 0))

    return _pcall(
        body, name="mla_prep", grid=(s // tm,),
        out_shape=[_sds((nh, s, 256), MXU), _sds((nh, s, 256), MXU), _sds((nh, s, MLA_V), MXU),
                   _sds((s, 768), jnp.float32), _sds((s, 1024), jnp.float32),
                   _sds((s, 512), MXU), _sds((s, 512), MXU)],
        in_specs=[col(512, C_CQ), col(512, C_CKV), col(LANES, C_KR), row(LANES), row(LANES),
                  full((1, 512)), full((1, 512)), full((512, 768)), full((512, 1024)),
                  full((1, LANES)), full((1, LANES)), full((1, LANES)), full((1, LANES))],
        out_specs=[heads(256), heads(256), heads(MLA_V), row(768), row(1024), row(512), row(512)],
        sem=("parallel",))(proj, proj, proj, cos, sin, g_cq, g_ckv, w_uq, w_ukv, g_qn, g_qr, g_kn, g_kr)


def _mla_fwd(qc, kc, v):
    nh, s, _ = qc.shape
    t = min(ATT_TILE, s)
    nb = s // t
    scale = (MLA_NOPE + MLA_ROPE) ** -0.5

    def body(q_ref, k_ref, v_ref, y_ref, lse_ref, m_sc, l_sc, acc):
        qi, ki = pl.program_id(1), pl.program_id(2)

        @pl.when(ki == 0)
        def _():
            m_sc[...] = jnp.full_like(m_sc, NEG_INF)
            l_sc[...] = jnp.zeros_like(l_sc)
            acc[...] = jnp.zeros_like(acc)

        @pl.when(ki <= qi)
        def _():
            sc = _dot_nt(q_ref[0], k_ref[0]) * scale
            r_i = lax.broadcasted_iota(jnp.int32, sc.shape, 0) + qi * t
            c_i = lax.broadcasted_iota(jnp.int32, sc.shape, 1) + ki * t
            sc = jnp.where(c_i <= r_i, sc, NEG_INF)
            m_new = jnp.maximum(m_sc[...], jnp.max(sc, -1, keepdims=True))
            alpha = jnp.exp(m_sc[...] - m_new)
            p = jnp.exp(sc - m_new)
            l_sc[...] = alpha * l_sc[...] + jnp.sum(p, -1, keepdims=True)
            acc[...] = alpha * acc[...] + _dot(p, v_ref[0])
            m_sc[...] = m_new

        @pl.when(ki == qi)
        def _():
            y_ref[...] = acc[...] / l_sc[...]
            lse_ref[0] = m_sc[...] + jnp.log(l_sc[...])

    return _pcall(
        body, name="mla_fwd", grid=(nh, nb, nb),
        out_shape=[_sds((s, nh * MLA_V), jnp.float32), _sds((nh, s, 1), jnp.float32)],
        in_specs=[pl.BlockSpec((1, t, 256), lambda h, i, k: (h, i, 0)),
                  pl.BlockSpec((1, t, 256), lambda h, i, k: (h, jnp.minimum(k, i), 0)),
                  pl.BlockSpec((1, t, MLA_V), lambda h, i, k: (h, jnp.minimum(k, i), 0))],
        out_specs=[pl.BlockSpec((t, MLA_V), lambda h, i, k: (i, h)),
                   pl.BlockSpec((1, t, 1), lambda h, i, k: (h, i, 0))],
        scratch=[pltpu.VMEM((t, 1), jnp.float32), pltpu.VMEM((t, 1), jnp.float32),
                 pltpu.VMEM((t, MLA_V), jnp.float32)],
        sem=("parallel", "parallel", "arbitrary"))(qc, kc, v)


def _memkv_prep(mem, g_mem, w_mkv, g_mk):
    ml, d = mem.shape
    hw = MEM_HEADS * MEM_DIM

    def body(mem_ref, g_ref, w_ref, gk_ref, k_ref, v_ref, kv_ref, mn_ref):
        mn, _, _ = _norm_fwd(mem_ref[...], g_ref[...])
        mn_ref[...] = mn.astype(mn_ref.dtype)
        kv = _dot(mn, w_ref[...])
        kv_ref[...] = kv
        for h in range(MEM_HEADS):
            kn, _, _ = _norm_fwd(kv[:, MEM_DIM * h: MEM_DIM * (h + 1)], gk_ref[...])
            k_ref[:, MEM_DIM * h: MEM_DIM * (h + 1)] = kn.astype(k_ref.dtype)
        v_ref[...] = kv[:, hw:].astype(v_ref.dtype)

    vm = pl.BlockSpec(memory_space=pltpu.VMEM)
    return _pcall(
        body, name="memkv_prep",
        out_shape=[_sds((ml, hw), MXU), _sds((ml, hw), MXU), _sds((ml, 2 * hw), jnp.float32), _sds((ml, d), MXU)],
        in_specs=[vm] * 4, out_specs=[vm] * 4)(mem, g_mem, w_mkv, g_mk)


def _mem_fwd(proj, g_mq, km, vmm):
    s = proj.shape[0]
    ml, hw = km.shape
    tm = min(FFN_TILE, s)
    scale = MEM_DIM ** -0.5

    def body(q_ref, g_ref, k_ref, v_ref, y_ref, lse_ref):
        col = lax.broadcasted_iota(jnp.int32, (tm, MEM_HEADS), 1)
        lse_t = jnp.zeros((tm, MEM_HEADS), jnp.float32)
        for h in range(MEM_HEADS):
            sl = slice(MEM_DIM * h, MEM_DIM * (h + 1))
            qn, _, _ = _norm_fwd(q_ref[:, sl], g_ref[...])
            sc = _dot_nt(qn, k_ref[:, sl]) * scale
            m = jnp.max(sc, -1, keepdims=True)
            p = jnp.exp(sc - m)
            l = jnp.sum(p, -1, keepdims=True)
            y_ref[:, sl] = _dot(p, v_ref[:, sl]) / l
            lse_t = jnp.where(col == h, m + jnp.log(l), lse_t)
        lse_ref[...] = lse_t

    return _pcall(
        body, name="mem_fwd", grid=(s // tm,),
        out_shape=[_sds((s, hw), jnp.float32), _sds((s, MEM_HEADS), jnp.float32)],
        in_specs=[pl.BlockSpec((tm, hw), lambda i: (i, C_QM // hw)), pl.BlockSpec((1, MEM_DIM), lambda i: (0, 0)),
                  pl.BlockSpec((ml, hw), lambda i: (0, 0)), pl.BlockSpec((ml, hw), lambda i: (0, 0))],
        out_specs=[pl.BlockSpec((tm, hw), lambda i: (i, 0)), pl.BlockSpec((tm, MEM_HEADS), lambda i: (i, 0))],
        sem=("parallel",))(proj, g_mq, km, vmm)


def _alibi_slope(h):
    return float(2.0 ** (-8.0 * (h + 1) / SWA_Q_HEADS))


def _swa_common(n, kp, kc, vp, vc, pq, pkp, pkc, gk):
    b = SWA_BLOCK
    k_raw = jnp.concatenate([kp, kc], axis=0)
    kn, kxn, kr = _norm_fwd(k_raw, gk, half=True)
    v = jnp.concatenate([vp, vc], axis=0)
    dist = jnp.abs(pq - jnp.concatenate([pkp, pkc], axis=1))
    r_i = lax.broadcasted_iota(jnp.int32, (b, 2 * b), 0)
    c_i = lax.broadcasted_iota(jnp.int32, (b, 2 * b), 1)
    valid = (c_i > r_i) & (c_i <= r_i + b) & (c_i >= jnp.where(n > 0, 0, b))
    return kn, v, dist, valid


def _swa_specs(s):
    b = SWA_BLOCK
    prev = lambda n: jnp.maximum(n - 1, 0)
    return [
        pl.BlockSpec((b, 1024), lambda n: (n, C_QA // 1024)),
        pl.BlockSpec((b, LANES), lambda n: (prev(n), C_KA // LANES)),
        pl.BlockSpec((b, LANES), lambda n: (n, C_KA // LANES)),
        pl.BlockSpec((b, LANES), lambda n: (prev(n), C_VA // LANES)),
        pl.BlockSpec((b, LANES), lambda n: (n, C_VA // LANES)),
        pl.BlockSpec((b, 1), lambda n: (n, 0)),
        pl.BlockSpec((1, b), lambda n: (0, prev(n))),
        pl.BlockSpec((1, b), lambda n: (0, n)),
        pl.BlockSpec((1, LANES), lambda n: (0, 0)),
        pl.BlockSpec((1, LANES), lambda n: (0, 0)),
        pl.BlockSpec(memory_space=pltpu.SMEM),
    ]


def _swa_fwd(proj, posc, posr, gq, gk, sinks):
    s = proj.shape[0]
    b = SWA_BLOCK
    scale = SWA_DIM ** -0.5

    def body(q_ref, kp_ref, kc_ref, vp_ref, vc_ref, pq_ref, pkp_ref, pkc_ref, gq_ref, gk_ref, sink_ref,
             y_ref, lse_ref):
        n = pl.program_id(0)
        kn, v, dist, valid = _swa_common(n, kp_ref[...], kc_ref[...], vp_ref[...], vc_ref[...],
                                         pq_ref[...], pkp_ref[...], pkc_ref[...], gk_ref[...])
        lo = _lo_mask((b, LANES))
        col = lax.broadcasted_iota(jnp.int32, (b, SWA_Q_HEADS), 1)
        lse_t = jnp.zeros((b, SWA_Q_HEADS), jnp.float32)
        for j in range(SWA_Q_HEADS // 2):
            hk = (2 * j) // (SWA_Q_HEADS // SWA_KV_HEADS)
            kvmask = lo if hk == 0 else jnp.logical_not(lo)
            qn, _, _ = _norm_fwd(q_ref[:, LANES * j: LANES * (j + 1)], gq_ref[...], half=True)
            qsw = pltpu.roll(qn, 64, 1)
            outs = []
            for e in range(2):
                h = 2 * j + e
                qm = jnp.where(kvmask, qn if e == hk else qsw, 0.0)
                sc = _dot_nt(qm, kn) * scale - _alibi_slope(h) * dist
                sc = jnp.where(valid, sc, NEG_INF)
                sk = sink_ref[h]
                m = jnp.maximum(jnp.max(sc, -1, keepdims=True), sk)
                p = jnp.exp(sc - m)
                l = jnp.sum(p, -1, keepdims=True) + jnp.exp(sk - m)
                o = _dot(p, v) / l
                outs.append(o if e == hk else pltpu.roll(o, 64, 1))
                lse_t = jnp.where(col == h, m + jnp.log(l), lse_t)
            y_ref[:, LANES * j: LANES * (j + 1)] = jnp.where(lo, outs[0], outs[1])
        lse_ref[...] = lse_t

    return _pcall(
        body, name="swa_fwd", grid=(s // b,),
        out_shape=[_sds((s, 1024), jnp.float32), _sds((s, SWA_Q_HEADS), jnp.float32)],
        in_specs=_swa_specs(s),
        out_specs=[pl.BlockSpec((b, 1024), lambda n: (n, 0)), pl.BlockSpec((b, SWA_Q_HEADS), lambda n: (n, 0))],
        sem=("parallel",))(proj, proj, proj, proj, proj, posc, posr, posr, gq, gk, sinks)


def _out_proj(y_a, y_b, y_m, x, w_out, g_ffn):
    s, d = x.shape
    tm = min(ROW_TILE, s)

    def body(ya_ref, yb_ref, ym_ref, x_ref, w_ref, g_ref, h1_ref, fn_ref):
        y = jnp.concatenate([ya_ref[...].astype(MXU), yb_ref[...].astype(MXU), ym_ref[...].astype(MXU)], axis=1)
        h1 = x_ref[...] + _dot(y, w_ref[...])
        h1_ref[...] = h1
        fn, _, _ = _norm_fwd(h1, g_ref[...])
        fn_ref[...] = fn.astype(fn_ref.dtype)

    def row(width):
        return pl.BlockSpec((tm, width), lambda i: (i, 0))

    return _pcall(
        body, name="out_proj", grid=(s // tm,),
        out_shape=[_sds((s, d), jnp.float32), _sds((s, d), MXU)],
        in_specs=[row(1024), row(512), row(512), row(d), pl.BlockSpec(w_out.shape, lambda i: (0, 0)),
                  pl.BlockSpec((1, d), lambda i: (0, 0))],
        out_specs=[row(d), row(d)], sem=("parallel",))(y_a, y_b, y_m, x, w_out, g_ffn)


def _ffn_gu(fn, w_gu):
    s, d = fn.shape
    f = w_gu.shape[-1]
    tm = min(FFN_TILE, s)

    def body(fn_ref, w_ref, gu_ref, act_ref):
        x = fn_ref[...]
        g = _dot(x, w_ref[0, 0])
        u = _dot(x, w_ref[0, 1])
        gu_ref[0, 0] = g
        gu_ref[0, 1] = u
        act_ref[0] = (g * jax.nn.sigmoid(g) * u).astype(act_ref.dtype)

    return _pcall(
        body, name="ffn_gate_up", grid=(N_DEV, s // tm),
        out_shape=[_sds((N_DEV, 2, s, f), jnp.float32), _sds((N_DEV, s, f), MXU)],
        in_specs=[pl.BlockSpec((tm, d), lambda j, i: (i, 0)),
                  pl.BlockSpec((1, 2, d, f), lambda j, i: (j, 0, 0, 0))],
        out_specs=[pl.BlockSpec((1, 2, tm, f), lambda j, i: (j, 0, i, 0)),
                   pl.BlockSpec((1, tm, f), lambda j, i: (j, i, 0))],
        sem=("parallel", "parallel"))(fn, w_gu)


def _ffn_down(act, w_d, h1, target):
    _, s, f = act.shape
    d = h1.shape[1]
    tm = min(FFN_TILE, s)

    def body(a_ref, w_ref, h1_ref, t_ref, dout_ref, loss_ref, acc):
        i, j = pl.program_id(0), pl.program_id(1)
        part = _dot(a_ref[0], w_ref[0])

        @pl.when(j == 0)
        def _():
            acc[...] = h1_ref[...] + part

        @pl.when(j > 0)
        def _():
            acc[...] += part

        @pl.when((i == 0) & (j == 0))
        def _():
            loss_ref[...] = jnp.zeros_like(loss_ref)

        @pl.when(j == N_DEV - 1)
        def _():
            diff = acc[...] - t_ref[...]
            dout_ref[...] = diff / d
            loss_ref[...] += 0.5 * jnp.sum(jnp.sum(diff * diff, -1, keepdims=True) / d)

    row = pl.BlockSpec((tm, d), lambda i, j: (i, 0))
    return _pcall(
        body, name="ffn_down", grid=(s // tm, N_DEV),
        out_shape=[_sds((s, d), jnp.float32), _sds((8, LANES), jnp.float32)],
        in_specs=[pl.BlockSpec((1, tm, f), lambda i, j: (j, i, 0)), pl.BlockSpec((1, f, d), lambda i, j: (j, 0, 0)),
                  row, row],
        out_specs=[row, pl.BlockSpec((8, LANES), lambda i, j: (0, 0))],
        scratch=[pltpu.VMEM((tm, d), jnp.float32)], sem=("arbitrary", "arbitrary"))(act, w_d, h1, target)


def _ffn_bwd_act(dout, w_d, gu):
    s, d = dout.shape
    f = w_d.shape[1]
    tm = min(FFN_TILE, s)
    ni = s // tm

    def body(do_ref, w_ref, gu_ref, dgu_ref, dw_ref, acc):
        i = pl.program_id(1)
        do = do_ref[...].astype(MXU)
        d_act = _dot_nt(do, w_ref[0])
        g, u = gu_ref[0, 0], gu_ref[0, 1]
        sig = jax.nn.sigmoid(g)
        silu = g * sig
        dgu_ref[0, 0] = (d_act * u * (sig * (1.0 + g * (1.0 - sig)))).astype(dgu_ref.dtype)
        dgu_ref[0, 1] = (d_act * silu).astype(dgu_ref.dtype)
        part = _dot_tn(silu * u, do)

        @pl.when(i == 0)
        def _():
            acc[...] = part

        @pl.when(i > 0)
        def _():
            acc[...] += part

        @pl.when(i == ni - 1)
        def _():
            dw_ref[0] = acc[...].astype(dw_ref.dtype)

    return _pcall(
        body, name="ffn_bwd_act", grid=(N_DEV, ni),
        out_shape=[_sds((N_DEV, 2, s, f), MXU), _sds((N_DEV, f, d), WIRE)],
        in_specs=[pl.BlockSpec((tm, d), lambda j, i: (i, 0)), pl.BlockSpec((1, f, d), lambda j, i: (j, 0, 0)),
                  pl.BlockSpec((1, 2, tm, f), lambda j, i: (j, 0, i, 0))],
        out_specs=[pl.BlockSpec((1, 2, tm, f), lambda j, i: (j, 0, i, 0)),
                   pl.BlockSpec((1, f, d), lambda j, i: (j, 0, 0))],
        scratch=[pltpu.VMEM((f, d), jnp.float32)], sem=("parallel", "arbitrary"))(dout, w_d, gu)


def _ffn_dw_gu(fn, dgu):
    s, d = fn.shape
    f = dgu.shape[-1]
    tk = min(FFN_TILE, s)
    nk = s // tk

    def body(fn_ref, dgu_ref, dw_ref, acc):
        k = pl.program_id(1)
        x = fn_ref[...]
        pg = _dot_tn(x, dgu_ref[0, 0])
        pu = _dot_tn(x, dgu_ref[0, 1])

        @pl.when(k == 0)
        def _():
            acc[0] = pg
            acc[1] = pu

        @pl.when(k > 0)
        def _():
            acc[0] += pg
            acc[1] += pu

        @pl.when(k == nk - 1)
        def _():
            dw_ref[0] = acc[...].astype(dw_ref.dtype)

    return _pcall(
        body, name="ffn_dw_gate_up", grid=(N_DEV, nk),
        out_shape=_sds((N_DEV, 2, d, f), WIRE),
        in_specs=[pl.BlockSpec((tk, d), lambda j, k: (k, 0)), pl.BlockSpec((1, 2, tk, f), lambda j, k: (j, 0, k, 0))],
        out_specs=pl.BlockSpec((1, 2, d, f), lambda j, k: (j, 0, 0, 0)),
        scratch=[pltpu.VMEM((2, d, f), jnp.float32)], sem=("parallel", "arbitrary"))(fn, dgu)


def _ffn_dfn(dgu, w_gu):
    _, _, s, f = dgu.shape
    d = w_gu.shape[2]
    tm = min(FFN_TILE, s)

    def body(dgu_ref, w_ref, dfn_ref):
        j = pl.program_id(1)
        part = _dot_nt(dgu_ref[0, 0], w_ref[0, 0]) + _dot_nt(dgu_ref[0, 1], w_ref[0, 1])

        @pl.when(j == 0)
        def _():
            dfn_ref[...] = part

        @pl.when(j > 0)
        def _():
            dfn_ref[...] += part

    return _pcall(
        body, name="ffn_dfn", grid=(s // tm, N_DEV),
        out_shape=_sds((s, d), jnp.float32),
        in_specs=[pl.BlockSpec((1, 2, tm, f), lambda i, j: (j, 0, i, 0)),
                  pl.BlockSpec((1, 2, d, f), lambda i, j: (j, 0, 0, 0))],
        out_specs=pl.BlockSpec((tm, d), lambda i, j: (i, 0)),
        sem=("parallel", "arbitrary"))(dgu, w_gu)


def _ffn_norm_bwd(d_fn, dout, h1, g_ffn):
    s, d = h1.shape
    tm = min(ROW_TILE, s)

    def body(dfn_ref, do_ref, h1_ref, g_ref, dh1_ref, dg_ref):
        i = pl.program_id(0)

        @pl.when(i == 0)
        def _():
            dg_ref[...] = jnp.zeros_like(dg_ref)

        _, xn, r = _norm_fwd(h1_ref[...], g_ref[...])
        dx, dg = _norm_bwd(xn, r, g_ref[...], dfn_ref[...])
        dh1_ref[...] = do_ref[...] + dx
        dg_ref[...] += dg

    row = pl.BlockSpec((tm, d), lambda i: (i, 0))
    vec = pl.BlockSpec((1, d), lambda i: (0, 0))
    return _pcall(
        body, name="ffn_norm_bwd", grid=(s // tm,),
        out_shape=[_sds((s, d), jnp.float32), _sds((1, d), jnp.float32)],
        in_specs=[row, row, row, vec], out_specs=[row, vec], sem=("arbitrary",))(d_fn, dout, h1, g_ffn)


def _mem_bwd(proj, g_mq, km, vmm, d_y, y_m, lse):
    s = proj.shape[0]
    ml, hw = km.shape
    tm = min(FFN_TILE, s)
    scale = MEM_DIM ** -0.5

    def body(q_ref, g_ref, k_ref, v_ref, do_ref, y_ref, lse_ref, dq_ref, dk_ref, dv_ref, dg_ref):
        i = pl.program_id(0)

        @pl.when(i == 0)
        def _():
            dk_ref[...] = jnp.zeros_like(dk_ref)
            dv_ref[...] = jnp.zeros_like(dv_ref)
            dg_ref[...] = jnp.zeros_like(dg_ref)

        col = lax.broadcasted_iota(jnp.int32, (tm, MEM_HEADS), 1)
        lse_t = lse_ref[...]
        for h in range(MEM_HEADS):
            sl = slice(MEM_DIM * h, MEM_DIM * (h + 1))
            qn, xn, r = _norm_fwd(q_ref[:, sl], g_ref[...])
            lse_h = jnp.sum(jnp.where(col == h, lse_t, 0.0), -1, keepdims=True)
            p = jnp.exp(_dot_nt(qn, k_ref[:, sl]) * scale - lse_h)
            do = do_ref[:, sl]
            dd = jnp.sum(do * y_ref[:, sl], -1, keepdims=True)
            dp = _dot_nt(do, v_ref[:, sl])
            ds = (p * (dp - dd)).astype(MXU)
            dv_ref[:, sl] += _dot_tn(p, do)
            dk_ref[:, sl] += _dot_tn(ds, qn) * scale
            dx, dg = _norm_bwd(xn, r, g_ref[...], _dot(ds, k_ref[:, sl]) * scale)
            dq_ref[:, sl] = dx.astype(dq_ref.dtype)
            dg_ref[...] += dg

    full = pl.BlockSpec((ml, hw), lambda i: (0, 0))
    return _pcall(
        body, name="mem_bwd", grid=(s // tm,),
        out_shape=[_sds((s, hw), MXU), _sds((ml, hw), jnp.float32), _sds((ml, hw), jnp.float32),
                   _sds((1, MEM_DIM), jnp.float32)],
        in_specs=[pl.BlockSpec((tm, hw), lambda i: (i, C_QM // hw)), pl.BlockSpec((1, MEM_DIM), lambda i: (0, 0)),
                  full, full, pl.BlockSpec((tm, hw), lambda i: (i, 3)), pl.BlockSpec((tm, hw), lambda i: (i, 0)),
                  pl.BlockSpec((tm, MEM_HEADS), lambda i: (i, 0))],
        out_specs=[pl.BlockSpec((tm, hw), lambda i: (i, 0)), full, full,
                   pl.BlockSpec((1, MEM_DIM), lambda i: (0, 0))],
        sem=("arbitrary",))(proj, g_mq, km, vmm, d_y, y_m, lse)


def _memkv_bwd(mem, g_mem, w_mkv, g_mk, kv, memn, dk, dv):
    ml, d = mem.shape
    hw = MEM_HEADS * MEM_DIM

    def body(mem_ref, g_ref, w_ref, gk_ref, kv_ref, mn_ref, dk_ref, dv_ref, dw_ref, dgm_ref, dgk_ref):
        parts = []
        dgk = jnp.zeros((1, MEM_DIM), jnp.float32)
        for h in range(MEM_HEADS):
            sl = slice(MEM_DIM * h, MEM_DIM * (h + 1))
            _, xn, r = _norm_fwd(kv_ref[:, sl], gk_ref[...])
            dx, dg = _norm_bwd(xn, r, gk_ref[...], dk_ref[:, sl])
            parts.append(dx)
            dgk = dgk + dg
        dkv = jnp.concatenate(parts + [dv_ref[...]], axis=1).astype(MXU)
        dgk_ref[...] = dgk
        dw_ref[...] = _dot_tn(mn_ref[...], dkv).astype(dw_ref.dtype)
        d_mn = _dot_nt(dkv, w_ref[...])
        _, xn, _ = _norm_fwd(mem_ref[...], g_ref[...])
        dgm_ref[...] = jnp.sum(d_mn * xn, 0, keepdims=True)

    vm = pl.BlockSpec(memory_space=pltpu.VMEM)
    return _pcall(
        body, name="memkv_bwd",
        out_shape=[_sds((d, 2 * hw), WIRE), _sds((1, d), jnp.float32), _sds((1, MEM_DIM), jnp.float32)],
        in_specs=[vm] * 8, out_specs=[vm] * 3)(mem, g_mem, w_mkv, g_mk, kv, memn, dk, dv)


def _mla_bwd(qc, kc, v, d_y, y_b, lse):
    nh, s, _ = qc.shape
    t = min(ATT_TILE, s)
    nb = s // t
    scale = (MLA_NOPE + MLA_ROPE) ** -0.5

    def body(q_ref, k_ref, v_ref, do_ref, y_ref, lse_ref, dq_ref, dk_ref, dv_ref, dk_acc, dv_acc):
        kj, qi = pl.program_id(1), pl.program_id(2)

        @pl.when((kj == 0) & (qi == 0))
        def _():
            dq_ref[...] = jnp.zeros_like(dq_ref)

        @pl.when(qi == kj)
        def _():
            dk_acc[...] = jnp.zeros_like(dk_acc)
            dv_acc[...] = jnp.zeros_like(dv_acc)

        @pl.when(qi >= kj)
        def _():
            q, k = q_ref[0], k_ref[0]
            sc = _dot_nt(q, k) * scale
            r_i = lax.broadcasted_iota(jnp.int32, sc.shape, 0) + qi * t
            c_i = lax.broadcasted_iota(jnp.int32, sc.shape, 1) + kj * t
            p = jnp.exp(jnp.where(c_i <= r_i, sc, NEG_INF) - lse_ref[0])
            do = do_ref[...]
            dd = jnp.sum(do * y_ref[...], -1, keepdims=True)
            dp = _dot_nt(do, v_ref[0])
            ds = (p * (dp - dd) * scale).astype(MXU)
            dv_acc[...] += _dot_tn(p, do)
            dk_acc[...] += _dot_tn(ds, q)
            rows = pl.ds(pl.multiple_of(qi * t, t), t)
            dq_ref[0, rows, :] += _dot(ds, k)

        @pl.when(qi == nb - 1)
        def _():
            dk_ref[0] = dk_acc[...]
            dv_ref[0] = dv_acc[...]

    qmap = lambda h, j, i: (h, jnp.maximum(i, j), 0)
    return _pcall(
        body, name="mla_bwd", grid=(nh, nb, nb),
        out_shape=[_sds((nh, s, 256), jnp.float32), _sds((nh, s, 256), jnp.float32),
                   _sds((nh, s, MLA_V), jnp.float32)],
        in_specs=[pl.BlockSpec((1, t, 256), qmap),
                  pl.BlockSpec((1, t, 256), lambda h, j, i: (h, j, 0)),
                  pl.BlockSpec((1, t, MLA_V), lambda h, j, i: (h, j, 0)),
                  pl.BlockSpec((t, MLA_V), lambda h, j, i: (jnp.maximum(i, j), 8 + h)),
                  pl.BlockSpec((t, MLA_V), lambda h, j, i: (jnp.maximum(i, j), h)),
                  pl.BlockSpec((1, t, 1), qmap)],
        out_specs=[pl.BlockSpec((1, s, 256), lambda h, j, i: (h, 0, 0)),
                   pl.BlockSpec((1, t, 256), lambda h, j, i: (h, j, 0)),
                   pl.BlockSpec((1, t, MLA_V), lambda h, j, i: (h, j, 0))],
        scratch=[pltpu.VMEM((t, 256), jnp.float32), pltpu.VMEM((t, MLA_V), jnp.float32)],
        sem=("parallel", "arbitrary", "arbitrary"))(qc, kc, v, d_y, y_b, lse)


def _mla_prep_bwd(proj, cos, sin, g_cq, g_ckv, w_uq, w_ukv, g_qn, g_qr, g_kn, g_kr,
                  qb, kvb, cqn, ckvn, dqc, dkc, dv):
    s = proj.shape[0]
    tm = min(ROW_TILE, s)
    nh = MLA_HEADS
    ni = s // tm

    def body(cq_ref, ckv_ref, kr_ref, cos_ref, sin_ref, gcq_ref, gckv_ref, wuq_ref, wukv_ref,
             gqn_ref, gqr_ref, gkn_ref, gkr_ref, qb_ref, kvb_ref, cqn_ref, ckvn_ref, dqc_ref, dkc_ref, dv_ref,
             dcq_ref, dckv_ref, dkr_ref, dwuq_ref, dwukv_ref,
             dgcq_ref, dgckv_ref, dgqn_ref, dgqr_ref, dgkn_ref, dgkr_ref, acc_uq, acc_ukv):
        i = pl.program_id(0)

        @pl.when(i == 0)
        def _():
            acc_uq[...] = jnp.zeros_like(acc_uq)
            acc_ukv[...] = jnp.zeros_like(acc_ukv)
            for ref in (dgcq_ref, dgckv_ref, dgqn_ref, dgqr_ref, dgkn_ref, dgkr_ref):
                ref[...] = jnp.zeros_like(ref)

        cos_t, sin_t = cos_ref[...], sin_ref[...]
        lo = _lo_mask((tm, LANES))
        qb_v, kvb_v = qb_ref[...], kvb_ref[...]
        dq_parts, dgqn = [], jnp.zeros((1, LANES), jnp.float32)
        for h in range(nh):
            _, xn, r = _norm_fwd(qb_v[:, MLA_NOPE * h: MLA_NOPE * (h + 1)], gqn_ref[...])
            dx, dg = _norm_bwd(xn, r, gqn_ref[...], dqc_ref[h][:, :MLA_NOPE])
            dq_parts.append(dx)
            dgqn = dgqn + dg
        dgqn_ref[...] += dgqn
        dgqr = jnp.zeros((1, LANES), jnp.float32)
        for j in range(nh // 2):
            d_rope = jnp.where(lo, dqc_ref[2 * j][:, MLA_NOPE:], dqc_ref[2 * j + 1][:, MLA_NOPE:])
            d_pre = _rope_bwd(d_rope, cos_t, sin_t)
            xr = qb_v[:, nh * MLA_NOPE + LANES * j: nh * MLA_NOPE + LANES * (j + 1)]
            _, xn, r = _norm_fwd(xr, gqr_ref[...], half=True)
            dx, dg = _norm_bwd(xn, r, gqr_ref[...], d_pre, half=True)
            dq_parts.append(dx)
            dgqr = dgqr + dg
        dgqr_ref[...] += dgqr
        dqb = jnp.concatenate(dq_parts, axis=1).astype(MXU)
        acc_uq[...] += _dot_tn(cqn_ref[...], dqb)
        _, xn, r = _norm_fwd(cq_ref[...], gcq_ref[...])
        dx, dg = _norm_bwd(xn, r, gcq_ref[...], _dot_nt(dqb, wuq_ref[...]))
        dcq_ref[...] = dx.astype(dcq_ref.dtype)
        dgcq_ref[...] += dg
        dkv_parts, dgkn = [], jnp.zeros((1, LANES), jnp.float32)
        d_kr2 = jnp.zeros((tm, LANES), jnp.float32)
        for h in range(nh):
            _, xn, r = _norm_fwd(kvb_v[:, 256 * h: 256 * h + MLA_NOPE], gkn_ref[...])
            dx, dg = _norm_bwd(xn, r, gkn_ref[...], dkc_ref[h][:, :MLA_NOPE])
            dkv_parts += [dx, dv_ref[h]]
            dgkn = dgkn + dg
            d_kr2 = d_kr2 + dkc_ref[h][:, MLA_NOPE:]
        dgkn_ref[...] += dgkn
        dkvb = jnp.concatenate(dkv_parts, axis=1).astype(MXU)
        acc_ukv[...] += _dot_tn(ckvn_ref[...], dkvb)
        _, xn, r = _norm_fwd(ckv_ref[...], gckv_ref[...])
        dx, dg = _norm_bwd(xn, r, gckv_ref[...], _dot_nt(dkvb, wukv_ref[...]))
        dckv_ref[...] = dx.astype(dckv_ref.dtype)
        dgckv_ref[...] += dg
        d_kr = jnp.where(lo, d_kr2 + pltpu.roll(d_kr2, 64, 1), 0.0)
        d_pre = _rope_bwd(d_kr, cos_t, sin_t)
        _, xn, r = _norm_fwd(kr_ref[...], gkr_ref[...], half=True)
        dx, dg = _norm_bwd(xn, r, gkr_ref[...], d_pre, half=True)
        dkr_ref[...] = jnp.where(lo, dx, 0.0).astype(dkr_ref.dtype)
        dgkr_ref[...] += jnp.where(_lo_mask((1, LANES)), dg, 0.0)

        @pl.when(i == ni - 1)
        def _():
            dwuq_ref[...] = acc_uq[...].astype(dwuq_ref.dtype)
            dwukv_ref[...] = acc_ukv[...].astype(dwukv_ref.dtype)

    def col(width, start):
        return pl.BlockSpec((tm, width), lambda i: (i, start // width))

    def full(shape):
        return pl.BlockSpec(shape, lambda i: (0,) * len(shape))

    def row(width):
        return pl.BlockSpec((tm, width), lambda i: (i, 0))

    def heads(width):
        return pl.BlockSpec((nh, tm, width), lambda i: (0, i, 0))

    vec = full((1, LANES))
    return _pcall(
        body, name="mla_prep_bwd", grid=(ni,),
        out_shape=[_sds((s, 512), MXU), _sds((s, 512), MXU), _sds((s, LANES), MXU),
                   _sds((512, 768), WIRE), _sds((512, 1024), WIRE),
                   _sds((1, 512), jnp.float32), _sds((1, 512), jnp.float32)] + [_sds((1, LANES), jnp.float32)] * 4,
        in_specs=[col(512, C_CQ), col(512, C_CKV), col(LANES, C_KR), row(LANES), row(LANES),
                  full((1, 512)), full((1, 512)), full((512, 768)), full((512, 1024)), vec, vec, vec, vec,
                  row(768), row(1024), row(512), row(512), heads(256), heads(256), heads(MLA_V)],
        out_specs=[row(512), row(512), row(LANES), full((512, 768)), full((512, 1024)),
                   full((1, 512)), full((1, 512)), vec, vec, vec, vec],
        scratch=[pltpu.VMEM((512, 768), jnp.float32), pltpu.VMEM((512, 1024), jnp.float32)],
        sem=("arbitrary",))(proj, proj, proj, cos, sin, g_cq, g_ckv, w_uq, w_ukv, g_qn, g_qr, g_kn, g_kr,
                            qb, kvb, cqn, ckvn, dqc, dkc, dv)


def _swa_bwd(proj, posc, posr, gq, gk, sinks, d_y, y_a, lse):
    s = proj.shape[0]
    b = SWA_BLOCK
    nb = s // b
    scale = SWA_DIM ** -0.5

    def body(q_ref, kp_ref, kc_ref, vp_ref, vc_ref, pq_ref, pkp_ref, pkc_ref, gq_ref, gk_ref, sink_ref,
             do_ref, y_ref, lse_ref, kfull_ref,
             dq_ref, dk_ref, dv_ref, dgq_ref, dgk_ref, dsink_ref, dk_acc, dv_acc):
        n = pl.program_id(0)

        @pl.when(n == 0)
        def _():
            dk_acc[...] = jnp.zeros_like(dk_acc)
            dv_acc[...] = jnp.zeros_like(dv_acc)
            dgq_ref[...] = jnp.zeros_like(dgq_ref)
            dsink_ref[...] = jnp.zeros_like(dsink_ref)

        kn, v, dist, valid = _swa_common(n, kp_ref[...], kc_ref[...], vp_ref[...], vc_ref[...],
                                         pq_ref[...], pkp_ref[...], pkc_ref[...], gk_ref[...])
        lo = _lo_mask((b, LANES))
        col = lax.broadcasted_iota(jnp.int32, (b, SWA_Q_HEADS), 1)
        col1 = lax.broadcasted_iota(jnp.int32, (1, SWA_Q_HEADS), 1)
        lse_t = lse_ref[...]
        dk_blk = jnp.zeros((2 * b, LANES), jnp.float32)
        dv_blk = jnp.zeros((2 * b, LANES), jnp.float32)
        dgq = jnp.zeros((1, LANES), jnp.float32)
        dsink = jnp.zeros((1, SWA_Q_HEADS), jnp.float32)
        for j in range(SWA_Q_HEADS // 2):
            hk = (2 * j) // (SWA_Q_HEADS // SWA_KV_HEADS)
            kvmask = lo if hk == 0 else jnp.logical_not(lo)
            sl = slice(LANES * j, LANES * (j + 1))
            qn, xn, r = _norm_fwd(q_ref[:, sl], gq_ref[...], half=True)
            qsw = pltpu.roll(qn, 64, 1)
            d2 = do_ref[:, sl]
            d2sw = pltpu.roll(d2, 64, 1)
            prod = d2 * y_ref[:, sl]
            dqs = []
            for e in range(2):
                h = 2 * j + e
                half_e = lo if e == 0 else jnp.logical_not(lo)
                qm = jnp.where(kvmask, qn if e == hk else qsw, 0.0)
                dm = jnp.where(kvmask, d2 if e == hk else d2sw, 0.0)
                sc = _dot_nt(qm, kn) * scale - _alibi_slope(h) * dist
                sc = jnp.where(valid, sc, NEG_INF)
                lse_h = jnp.sum(jnp.where(col == h, lse_t, 0.0), -1, keepdims=True)
                p = jnp.exp(sc - lse_h)
                dd = jnp.sum(jnp.where(half_e, prod, 0.0), -1, keepdims=True)
                dp = _dot_nt(dm, v)
                ds = (p * (dp - dd)).astype(MXU)
                dsink = dsink - jnp.where(col1 == h, jnp.sum(jnp.exp(sink_ref[h] - lse_h) * dd), 0.0)
                dq_m = _dot(ds, kn) * scale
                dk_blk = dk_blk + _dot_tn(ds, qm) * scale
                dv_blk = dv_blk + _dot_tn(p, dm)
                dqs.append(dq_m if e == hk else pltpu.roll(dq_m, 64, 1))
            dx, dg = _norm_bwd(xn, r, gq_ref[...], jnp.where(lo, dqs[0], dqs[1]), half=True)
            dq_ref[:, sl] = dx.astype(dq_ref.dtype)
            dgq = dgq + dg
        dgq_ref[...] += dgq
        dsink_ref[...] += dsink
        prev = pl.ds(pl.multiple_of(jnp.maximum(n - 1, 0) * b, b), b)
        cur = pl.ds(pl.multiple_of(n * b, b), b)
        dk_acc[prev, :] += dk_blk[:b]
        dv_acc[prev, :] += dv_blk[:b]
        dk_acc[cur, :] += dk_blk[b:]
        dv_acc[cur, :] += dv_blk[b:]

        @pl.when(n == nb - 1)
        def _():
            _, kxn, kr = _norm_fwd(kfull_ref[...], gk_ref[...], half=True)
            dx, dg = _norm_bwd(kxn, kr, gk_ref[...], dk_acc[...], half=True)
            dk_ref[...] = dx.astype(dk_ref.dtype)
            dv_ref[...] = dv_acc[...].astype(dv_ref.dtype)
            dgk_ref[...] = dg

    full = pl.BlockSpec((s, LANES), lambda n: (0, 0))
    vec = pl.BlockSpec((1, LANES), lambda n: (0, 0))
    return _pcall(
        body, name="swa_bwd", grid=(nb,),
        out_shape=[_sds((s, 1024), MXU), _sds((s, LANES), MXU), _sds((s, LANES), MXU),
                   _sds((1, LANES), jnp.float32), _sds((1, LANES), jnp.float32),
                   _sds((1, SWA_Q_HEADS), jnp.float32)],
        in_specs=_swa_specs(s) + [pl.BlockSpec((b, 1024), lambda n: (n, 0)), pl.BlockSpec((b, 1024), lambda n: (n, 0)),
                                  pl.BlockSpec((b, SWA_Q_HEADS), lambda n: (n, 0)),
                                  pl.BlockSpec((s, LANES), lambda n: (0, C_KA // LANES))],
        out_specs=[pl.BlockSpec((b, 1024), lambda n: (n, 0)), full, full, vec, vec,
                   pl.BlockSpec((1, SWA_Q_HEADS), lambda n: (0, 0))],
        scratch=[pltpu.VMEM((s, LANES), jnp.float32), pltpu.VMEM((s, LANES), jnp.float32)],
        sem=("arbitrary",))(proj, proj, proj, proj, proj, posc, posr, posr, gq, gk, sinks, d_y, y_a, lse, proj)


def _dx(d_proj, w_in, x, g, d_h1):
    s, d = x.shape
    n = w_in.shape[1]
    tm = min(ROW_TILE, s)

    def body(dp_ref, w_ref, x_ref, g_ref, dh_ref, dx_ref, dg_ref):
        i = pl.program_id(0)

        @pl.when(i == 0)
        def _():
            dg_ref[...] = jnp.zeros_like(dg_ref)

        d_hn = _dot_nt(dp_ref[...], w_ref[...])
        _, xn, r = _norm_fwd(x_ref[...], g_ref[...])
        dx, dg = _norm_bwd(xn, r, g_ref[...], d_hn)
        dx_ref[...] = dh_ref[...] + dx
        dg_ref[...] += dg

    row = pl.BlockSpec((tm, d), lambda i: (i, 0))
    vec = pl.BlockSpec((1, d), lambda i: (0, 0))
    return _pcall(
        body, name="grad_x", grid=(s // tm,),
        out_shape=[_sds((s, d), jnp.float32), _sds((1, d), jnp.float32)],
        in_specs=[pl.BlockSpec((tm, n), lambda i: (i, 0)), pl.BlockSpec((d, n), lambda i: (0, 0)), row, vec, row],
        out_specs=[row, vec], sem=("arbitrary",))(d_proj, w_in, x, g, d_h1)


_SMALL = ["attn_norm_g", "swa_q_norm_g", "swa_k_norm_g", "swa_sinks", "mla_cq_norm_g", "mla_ckv_norm_g",
          "mla_qn_norm_g", "mla_qr_norm_g", "mla_kn_norm_g", "mla_kr_norm_g", "mem_norm_g",
          "mem_q_norm_g", "mem_k_norm_g", "ffn_norm_g"]


def _pack_rows(v):
    n = v.shape[-1]
    rows = -(-n // LANES)
    rows8 = -(-rows // 8) * 8
    flat = jnp.pad(v.reshape(-1), (0, rows8 * LANES - n))
    return flat.reshape(rows8, LANES)


def _pack(parts):
    return jnp.concatenate([_pack_rows(p) for p in parts], axis=0)


def _unpack(packed, sizes):
    out, r = [], 0
    for n in sizes:
        rows = -(-n // LANES)
        rows8 = -(-rows // 8) * 8
        out.append(packed[r:r + rows8].reshape(-1)[:n].reshape(1, n))
        r += rows8
    return out


def _fold64(v):
    return v[:, :64] + v[:, 64:]


def kernel(x, mem, positions, attn_norm_g, w_in, swa_q_norm_g, swa_k_norm_g, swa_sinks, mla_cq_norm_g, mla_ckv_norm_g, w_uq, w_ukv, mla_qn_norm_g, mla_qr_norm_g, mla_kn_norm_g, mla_kr_norm_g, mem_norm_g, w_mem_kv, mem_q_norm_g, mem_k_norm_g, w_out, ffn_norm_g, w_gate, w_up, w_down, loss_target, m_attn_norm_g, m_w_in, m_swa_q_norm_g, m_swa_k_norm_g, m_swa_sinks, m_mla_cq_norm_g, m_mla_ckv_norm_g, m_w_uq, m_w_ukv, m_mla_qn_norm_g, m_mla_qr_norm_g, m_mla_kn_norm_g, m_mla_kr_norm_g, m_mem_norm_g, m_w_mem_kv, m_mem_q_norm_g, m_mem_k_norm_g, m_w_out, m_ffn_norm_g, m_w_gate, m_w_up, m_w_down, v_attn_norm_g, v_w_in, v_swa_q_norm_g, v_swa_k_norm_g, v_swa_sinks, v_mla_cq_norm_g, v_mla_ckv_norm_g, v_w_uq, v_w_ukv, v_mla_qn_norm_g, v_mla_qr_norm_g, v_mla_kn_norm_g, v_mla_kr_norm_g, v_mem_norm_g, v_w_mem_kv, v_mem_q_norm_g, v_mem_k_norm_g, v_w_out, v_ffn_norm_g, v_w_gate, v_w_up, v_w_down):
    args = dict(locals())
    x2, mem2, tgt = x[0], mem[0], loss_target[0]
    s, d = x2.shape
    n_in = w_in.shape[2]
    f = w_gate.shape[2]

    shards = [w_in[0].astype(WIRE), w_uq[0].astype(WIRE), w_ukv[0].astype(WIRE), w_mem_kv[0].astype(WIRE),
              w_out[0].astype(WIRE), jnp.stack([w_gate[0], w_up[0]]).astype(WIRE), w_down[0].astype(WIRE)]
    g_in, g_uq, g_ukv, g_mkv, g_out = _all_gather(shards[:5])
    w_gu, w_d = _all_gather_background(shards[5:], 1, "all_gather_ffn_weights")
    wi = g_in.transpose(1, 0, 2).reshape(d, N_DEV * n_in)
    wi = jnp.concatenate([wi[:, 0:1024], wi[:, 1280:1792], wi[:, 1792:2304], wi[:, 2368:2880],
                          wi[:, 1024:1152], wi[:, 1152:1280], wi[:, 2304:2368],
                          jnp.zeros((d, IN_PAD - 2880), wi.dtype)], axis=1)
    wq = g_uq.transpose(1, 0, 2).reshape(512, 768)
    wq = jnp.concatenate([wq[:, 192 * h: 192 * h + 128] for h in range(4)]
                         + [wq[:, 192 * h + 128: 192 * (h + 1)] for h in range(4)], axis=1)
    wkv = g_ukv.transpose(1, 0, 2).reshape(512, 1024)
    wmkv = g_mkv.reshape(-1, g_mkv.shape[-1])
    wo = g_out.reshape(-1, d)

    pos = positions[0].astype(jnp.float32)
    inv_freq = ROPE_THETA ** (-jnp.arange(0, MLA_ROPE, 2, dtype=jnp.float32) / MLA_ROPE)
    ang = pos[:, None] * inv_freq
    cos32, sin32 = jnp.cos(ang), jnp.sin(ang)
    cos_t = jnp.tile(cos32, (1, 4))
    sin_t = jnp.tile(jnp.concatenate([-sin32, sin32], axis=1), (1, 2))
    posc, posr = pos.reshape(s, 1), pos.reshape(1, s)
    two = lambda g: jnp.tile(g, (1, 2))
    gq2, gk2, gqr2, gkr2 = two(swa_q_norm_g), two(swa_k_norm_g), two(mla_qr_norm_g), two(mla_kr_norm_g)
    sinks1 = swa_sinks[0]

    proj, hn = _in_proj(x2, attn_norm_g, wi)
    qc, kc, vb, qb, kvb, cqn, ckvn = _mla_prep(proj, cos_t, sin_t, mla_cq_norm_g, mla_ckv_norm_g, wq, wkv,
                                                mla_qn_norm_g, gqr2, mla_kn_norm_g, gkr2)
    y_b, lse_b = _mla_fwd(qc, kc, vb)
    km, vmm, kvm, memn = _memkv_prep(mem2, mem_norm_g, wmkv, mem_k_norm_g)
    y_m, lse_m = _mem_fwd(proj, mem_q_norm_g, km, vmm)
    y_a, lse_a = _swa_fwd(proj, posc, posr, gq2, gk2, sinks1)
    h1, fn = _out_proj(y_a, y_b, y_m, x2, wo, ffn_norm_g)
    gu, act = _ffn_gu(fn, w_gu)
    dout, loss_tile = _ffn_down(act, w_d, h1, tgt)

    dgu, dw_d = _ffn_bwd_act(dout, w_d, gu)
    dw_gu = _ffn_dw_gu(fn, dgu)
    d_h1, dg_ffn = _ffn_norm_bwd(_ffn_dfn(dgu, w_gu), dout, h1, ffn_norm_g)
    d_y = _mm(d_h1, wo, tb=True, out_dtype=jnp.float32, tm=FFN_TILE, tk=512, name="d_mix")
    dw_out = jnp.concatenate([
        _mm(y_a, d_h1, ta=True, out_dtype=WIRE, tm=1024, tk=512, name="dw_out_a"),
        _mm(y_b, d_h1, ta=True, out_dtype=WIRE, tm=1024, tk=512, name="dw_out_b"),
        _mm(y_m, d_h1, ta=True, out_dtype=WIRE, tm=1024, tk=512, name="dw_out_m")], axis=0)
    d_qm, dkm, dvmm, dg_mq = _mem_bwd(proj, mem_q_norm_g, km, vmm, d_y, y_m, lse_m)
    dw_mkv, dg_mem, dg_mk = _memkv_bwd(mem2, mem_norm_g, wmkv, mem_k_norm_g, kvm, memn, dkm, dvmm)
    dqc, dkc, dvb = _mla_bwd(qc, kc, vb, d_y, y_b, lse_b)
    (d_cq, d_ckv, d_kr, dw_uq, dw_ukv, dg_cq, dg_ckv, dg_qn, dg_qr, dg_kn, dg_kr) = _mla_prep_bwd(
        proj, cos_t, sin_t, mla_cq_norm_g, mla_ckv_norm_g, wq, wkv, mla_qn_norm_g, gqr2, mla_kn_norm_g, gkr2,
        qb, kvb, cqn, ckvn, dqc, dkc, dvb)
    d_qa, d_ka, d_va, dg_q, dg_k, d_sinks = _swa_bwd(proj, posc, posr, gq2, gk2, sinks1, d_y, y_a, lse_a)
    d_proj = jnp.concatenate([d_qa, d_cq, d_ckv, d_qm, d_ka, d_va, d_kr], axis=1)
    dw_in = _mm(hn, d_proj, ta=True, out_dtype=WIRE, tm=512, tk=512, name="dw_in")
    grad_x, dg_attn = _dx(d_proj, wi, x2, attn_norm_g, d_h1)

    gi = jnp.concatenate([dw_in[:, C_QA:C_QA + 1024], dw_in[:, C_KA:C_KA + 128], dw_in[:, C_VA:C_VA + 128],
                          dw_in[:, C_CQ:C_CQ + 512], dw_in[:, C_CKV:C_CKV + 512], dw_in[:, C_KR:C_KR + 64],
                          dw_in[:, C_QM:C_QM + 512]], axis=1)
    gi = gi.reshape(d, N_DEV, n_in).transpose(1, 0, 2)
    gq_ = jnp.concatenate(sum([[dw_uq[:, 128 * h: 128 * (h + 1)], dw_uq[:, 512 + 64 * h: 512 + 64 * (h + 1)]]
                               for h in range(4)], []), axis=1)
    gq_ = gq_.reshape(512, N_DEV, 96).transpose(1, 0, 2)
    gkv = dw_ukv.reshape(512, N_DEV, 128).transpose(1, 0, 2)
    grads = [gi, gq_, gkv, dw_mkv.reshape(g_mkv.shape), dw_out.reshape(g_out.shape), dw_gu, dw_d]
    recv = _exchange_grads(grads)

    big = {}
    def adam(name, r, stacked=False):
        w, m, v = args[name][0], args["m_" + name][0], args["v_" + name][0]
        return _adam_big(r.reshape(N_DEV, -1, r.shape[-1]), w, m, v, "adam_" + name)
    for name, r in zip(["w_in", "w_uq", "w_ukv", "w_mem_kv", "w_out"], recv[:5]):
        big[name] = [o[None] for o in adam(name, r)]
    big["w_down"] = [o[None] for o in adam("w_down", recv[6])]
    r_gu = recv[5]
    big["w_gate"] = [o[None] for o in adam("w_gate", r_gu[:, 0])]
    big["w_up"] = [o[None] for o in adam("w_up", r_gu[:, 1])]

    small_g = {
        "attn_norm_g": dg_attn, "swa_q_norm_g": _fold64(dg_q), "swa_k_norm_g": _fold64(dg_k),
        "swa_sinks": d_sinks, "mla_cq_norm_g": dg_cq, "mla_ckv_norm_g": dg_ckv, "mla_qn_norm_g": dg_qn,
        "mla_qr_norm_g": _fold64(dg_qr), "mla_kn_norm_g": dg_kn, "mla_kr_norm_g": _fold64(dg_kr),
        "mem_norm_g": dg_mem, "mem_q_norm_g": dg_mq, "mem_k_norm_g": dg_mk, "ffn_norm_g": dg_ffn}
    sizes = [args[n].shape[-1] for n in _SMALL]
    pg = _pack([small_g[n] for n in _SMALL] + [loss_tile[0:1, 0:1]])
    zero = jnp.zeros((1, 1), jnp.float32)
    pw = _pack([args[n] for n in _SMALL] + [zero])
    pm = _pack([args["m_" + n] for n in _SMALL] + [zero])
    pv = _pack([args["v_" + n] for n in _SMALL] + [zero])
    sg, sd, sm, sv = _small_allreduce_adam(pg, pw, pm, pv)
    small = {n: vals for n, vals in zip(_SMALL, zip(*[_unpack(p, sizes) for p in (sg, sd, sm, sv)]))}
    loss = _unpack(sg, sizes + [1])[-1].reshape(())

    order = ["attn_norm_g", "w_in", "swa_q_norm_g", "swa_k_norm_g", "swa_sinks", "mla_cq_norm_g", "mla_ckv_norm_g",
             "w_uq", "w_ukv", "mla_qn_norm_g", "mla_qr_norm_g", "mla_kn_norm_g", "mla_kr_norm_g", "mem_norm_g",
             "w_mem_kv", "mem_q_norm_g", "mem_k_norm_g", "w_out", "ffn_norm_g", "w_gate", "w_up", "w_down"]
    res = {n: (big[n] if n in big else list(small[n])) for n in order}
    outs = [loss, grad_x[None]]
    for kind in range(4):
        outs += [res[n][kind] for n in order]
    return tuple(outs)
```

```python
import jax
import jax.numpy as jnp
from jax import lax
from jax.experimental import pallas as pl
from jax.experimental.pallas import tpu as pltpu
from jax.experimental.pallas import tpu_sc as plsc

MXU = jnp.bfloat16
WIRE = jnp.bfloat16
EPS = 1e-6
NEG_INF = -1e30
N_DEV = 8
LANES = 128
ROW_TILE = 256
FFN_TILE = 512
ATT_TILE = 1024
SWA_BLOCK = 128
VMEM_LIMIT = 56 * 1024 * 1024

SWA_Q_HEADS, SWA_KV_HEADS, SWA_DIM = 16, 2, 64
MLA_HEADS, MLA_NOPE, MLA_ROPE, MLA_V = 4, 128, 64, 128
MEM_HEADS, MEM_DIM = 4, 128
ROPE_THETA = 10000.0
ADAM_LR, ADAM_B1, ADAM_B2, ADAM_EPS, ADAM_WD, ADAM_STEP = 0.001, 0.9, 0.999, 1e-08, 0.01, 10

C_QA, C_CQ, C_CKV, C_QM, C_KA, C_VA, C_KR, IN_PAD = 0, 1024, 1536, 2048, 2560, 2688, 2816, 2944


def _pcall(body, *, name, out_shape, in_specs, out_specs, grid=(), scratch=(), sem=None, after=None):
    params = pltpu.CompilerParams(dimension_semantics=sem, vmem_limit_bytes=VMEM_LIMIT)
    if after is not None:
        n_in, inner = len(in_specs), body

        def body(*refs):
            inner(*refs[:n_in], *refs[n_in + 1:])

        in_specs = list(in_specs) + [pl.BlockSpec(memory_space=pl.ANY)]
    call = pl.pallas_call(body, name=name, grid=grid, in_specs=in_specs, out_specs=out_specs,
                          out_shape=out_shape, scratch_shapes=list(scratch), compiler_params=params)
    return call if after is None else (lambda *ops: call(*ops, after))


def _sds(shape, dtype):
    return jax.ShapeDtypeStruct(tuple(shape), dtype)


def _dot(a, b):
    return jnp.dot(a.astype(MXU), b.astype(MXU), preferred_element_type=jnp.float32)


def _dot_nt(a, b):
    return lax.dot_general(a.astype(MXU), b.astype(MXU), (((1,), (1,)), ((), ())),
                           preferred_element_type=jnp.float32)


def _dot_tn(a, b):
    return lax.dot_general(a.astype(MXU), b.astype(MXU), (((0,), (0,)), ((), ())),
                           preferred_element_type=jnp.float32)


def _lo_mask(shape):
    return (lax.broadcasted_iota(jnp.int32, shape, len(shape) - 1) % LANES) < 64


def _norm_fwd(x, g, half=False):
    x2 = x * x
    if half:
        lo = _lo_mask(x.shape)
        s_lo = jnp.sum(jnp.where(lo, x2, 0.0), -1, keepdims=True)
        s_hi = jnp.sum(jnp.where(lo, 0.0, x2), -1, keepdims=True)
        r = jnp.where(lo, lax.rsqrt(s_lo / 64.0 + EPS), lax.rsqrt(s_hi / 64.0 + EPS))
    else:
        r = lax.rsqrt(jnp.mean(x2, -1, keepdims=True) + EPS)
    xn = x * r
    return xn * g, xn, r


def _norm_bwd(xn, r, g, dy, half=False):
    t = dy * g
    tx = t * xn
    if half:
        lo = _lo_mask(xn.shape)
        m_lo = jnp.sum(jnp.where(lo, tx, 0.0), -1, keepdims=True) / 64.0
        m_hi = jnp.sum(jnp.where(lo, 0.0, tx), -1, keepdims=True) / 64.0
        m = jnp.where(lo, m_lo, m_hi)
    else:
        m = jnp.mean(tx, -1, keepdims=True)
    dx = r * (t - xn * m)
    dg = jnp.sum(dy * xn, 0, keepdims=True)
    return dx, dg


def _swap32(x):
    lane = lax.broadcasted_iota(jnp.int32, x.shape, 1)
    return jnp.where((lane % 64) < 32, pltpu.roll(x, 96, 1), pltpu.roll(x, 32, 1))


def _rope(x, cos, sin):
    return x * cos + _swap32(x) * sin


def _rope_bwd(d, cos, sin):
    return d * cos + _swap32(d * sin)


def _my_coords():
    return lax.axis_index("x"), lax.axis_index("y"), lax.axis_index("c")


def _dev_index(px, py, pc):
    return 4 * px + 2 * py + pc


_FLIPS = [(0, 0, 1), (0, 1, 0), (0, 1, 1), (1, 0, 0), (1, 0, 1), (1, 1, 0), (1, 1, 1)]


def _flip(coords, f):
    return tuple((1 - v) if b else v for v, b in zip(coords, f))


def _all_gather(shards):
    n = len(shards)

    def body(*refs):
        ins, outs = refs[:n], refs[n:2 * n]
        send_sems, recv_sems, local_sems = refs[2 * n:]
        x, y, c = _my_coords()
        me, sibling = (x, y, c), (x, y, 1 - c)
        chips = [(1 - x, y), (x, 1 - y), (1 - x, 1 - y)]

        def copy(w, k, block, to, src=None):
            dst = outs[w].at[_dev_index(*block)]
            return pltpu.make_async_remote_copy(
                src_ref=dst if src is None else src, dst_ref=dst,
                send_sem=send_sems.at[w, k], recv_sem=recv_sems.at[w, k],
                device_id=to, device_id_type=pl.DeviceIdType.MESH)

        sends, locals_ = [], []
        for w in range(n):
            mine = pltpu.make_async_copy(ins[w], outs[w].at[_dev_index(*me)], local_sems.at[w])
            mine.start()
            locals_.append(mine)
            first = [copy(w, 0, me, sibling, src=ins[w])]
            first += [copy(w, 1 + j, me, (*chip, c), src=ins[w]) for j, chip in enumerate(chips)]
            for cp in first:
                cp.start()
            sends += first
        for w in range(n):
            for j, chip in enumerate(chips):
                copy(w, 1 + j, (*chip, c), me).wait_recv()
                fwd = copy(w, 4 + j, (*chip, c), sibling)
                fwd.start()
                sends.append(fwd)
        for w in range(n):
            copy(w, 0, sibling, me).wait_recv()
            for j, chip in enumerate(chips):
                copy(w, 4 + j, (*chip, 1 - c), me).wait_recv()
        for cp in sends:
            cp.wait_send()
        for mine in locals_:
            mine.wait()

    any_spec = pl.BlockSpec(memory_space=pl.ANY)
    return _pcall(
        body, name="all_gather_weights",
        out_shape=[_sds((N_DEV,) + s.shape, s.dtype) for s in shards],
        in_specs=[any_spec] * n, out_specs=[any_spec] * n,
        scratch=[pltpu.SemaphoreType.DMA((n, 7)), pltpu.SemaphoreType.DMA((n, 7)),
                 pltpu.SemaphoreType.DMA((n,))])(*shards)


def _all_gather_background(shards, collective_id, name):
    n = len(shards)
    src_refs = [jax.new_ref(s, memory_space=pltpu.MemorySpace.HBM) for s in shards]
    out_refs = [jax.empty_ref(_sds((N_DEV,) + s.shape, s.dtype), memory_space=pltpu.MemorySpace.HBM) for s in shards]

    @pl.kernel(mesh=plsc.ScalarSubcoreMesh(axis_name="seq", num_cores=1), name=name,
               scratch_types=(pltpu.SemaphoreType.DMA((n, 7)), pltpu.SemaphoreType.DMA((n, 7)),
                              pltpu.SemaphoreType.DMA((n,))),
               compiler_params=pltpu.CompilerParams(collective_id=collective_id))
    def launch(send_sems, recv_sems, local_sems):
        x, y, c = _my_coords()
        me, sibling = (x, y, c), (x, y, 1 - c)
        chips = [(1 - x, y), (x, 1 - y), (1 - x, 1 - y)]
        barrier = pltpu.get_barrier_semaphore()
        for peer in [sibling] + [(*chip, c) for chip in chips]:
            pl.semaphore_signal(barrier, inc=1, device_id=peer, device_id_type=pl.DeviceIdType.MESH)
        pl.semaphore_wait(barrier, 4)

        def copy(w, k, block, to, src=None):
            dst = out_refs[w].at[_dev_index(*block)]
            return pltpu.make_async_remote_copy(
                src_ref=dst if src is None else src, dst_ref=dst,
                send_sem=send_sems.at[w, k], recv_sem=recv_sems.at[w, k],
                device_id=to, device_id_type=pl.DeviceIdType.MESH)

        sends, locals_ = [], []
        for w in range(n):
            mine = pltpu.make_async_copy(src_refs[w], out_refs[w].at[_dev_index(*me)], local_sems.at[w])
            mine.start()
            locals_.append(mine)
            first = [copy(w, 0, me, sibling, src=src_refs[w])]
            first += [copy(w, 1 + j, me, (*chip, c), src=src_refs[w]) for j, chip in enumerate(chips)]
            for cp in first:
                cp.start()
            sends += first
        for w in range(n):
            for j, chip in enumerate(chips):
                copy(w, 1 + j, (*chip, c), me).wait_recv()
                fwd = copy(w, 4 + j, (*chip, c), sibling)
                fwd.start()
                sends.append(fwd)
        for w in range(n):
            copy(w, 0, sibling, me).wait_recv()
            for j, chip in enumerate(chips):
                copy(w, 4 + j, (*chip, 1 - c), me).wait_recv()
        for cp in sends:
            cp.wait_send()
        for mine in locals_:
            mine.wait()

    launch()
    return [r[...] for r in out_refs]


def _exchange_grads(grads):
    n = len(grads)

    def body(*refs):
        ins, outs = refs[:n], refs[n:2 * n]
        send_sems, recv_sems, local_sems = refs[2 * n:]
        me = _my_coords()
        my_idx = _dev_index(*me)
        sends, locals_ = [], []
        for w in range(n):
            mine = pltpu.make_async_copy(ins[w].at[my_idx], outs[w].at[my_idx], local_sems.at[w])
            mine.start()
            locals_.append(mine)
            for k, f in enumerate(_FLIPS):
                peer = _flip(me, f)
                cp = pltpu.make_async_remote_copy(
                    src_ref=ins[w].at[_dev_index(*peer)], dst_ref=outs[w].at[my_idx],
                    send_sem=send_sems.at[w, k], recv_sem=recv_sems.at[w, k],
                    device_id=peer, device_id_type=pl.DeviceIdType.MESH)
                cp.start()
                sends.append(cp)
        for w in range(n):
            for k, f in enumerate(_FLIPS):
                peer = _flip(me, f)
                slot = outs[w].at[_dev_index(*peer)]
                pltpu.make_async_remote_copy(
                    src_ref=slot, dst_ref=slot,
                    send_sem=send_sems.at[w, k], recv_sem=recv_sems.at[w, k],
                    device_id=peer, device_id_type=pl.DeviceIdType.MESH).wait_recv()
        for cp in sends:
            cp.wait_send()
        for mine in locals_:
            mine.wait()

    any_spec = pl.BlockSpec(memory_space=pl.ANY)
    return _pcall(
        body, name="exchange_grads",
        out_shape=[_sds(g.shape, g.dtype) for g in grads],
        in_specs=[any_spec] * n, out_specs=[any_spec] * n,
        scratch=[pltpu.SemaphoreType.DMA((n, 7)), pltpu.SemaphoreType.DMA((n, 7)),
                 pltpu.SemaphoreType.DMA((n,))])(*grads)


def _exchange_grads_background(grads, collective_id, name):
    n = len(grads)
    src_refs = [jax.new_ref(g, memory_space=pltpu.MemorySpace.HBM) for g in grads]
    out_refs = [jax.empty_ref(_sds(g.shape, g.dtype), memory_space=pltpu.MemorySpace.HBM) for g in grads]

    @pl.kernel(mesh=plsc.ScalarSubcoreMesh(axis_name="seq", num_cores=1), name=name,
               scratch_types=(pltpu.SemaphoreType.DMA((n, 7)), pltpu.SemaphoreType.DMA((n, 7)),
                              pltpu.SemaphoreType.DMA((n,))),
               compiler_params=pltpu.CompilerParams(collective_id=collective_id))
    def launch(send_sems, recv_sems, local_sems):
        me = _my_coords()
        my_idx = _dev_index(*me)
        peers = [_flip(me, f) for f in _FLIPS]
        barrier = pltpu.get_barrier_semaphore()
        for peer in peers:
            pl.semaphore_signal(barrier, inc=1, device_id=peer, device_id_type=pl.DeviceIdType.MESH)
        pl.semaphore_wait(barrier, len(peers))
        sends, locals_ = [], []
        for w in range(n):
            mine = pltpu.make_async_copy(src_refs[w].at[my_idx], out_refs[w].at[my_idx], local_sems.at[w])
            mine.start()
            locals_.append(mine)
            for k, peer in enumerate(peers):
                cp = pltpu.make_async_remote_copy(
                    src_ref=src_refs[w].at[_dev_index(*peer)], dst_ref=out_refs[w].at[my_idx],
                    send_sem=send_sems.at[w, k], recv_sem=recv_sems.at[w, k],
                    device_id=peer, device_id_type=pl.DeviceIdType.MESH)
                cp.start()
                sends.append(cp)
        for w in range(n):
            for k, peer in enumerate(peers):
                slot = out_refs[w].at[_dev_index(*peer)]
                pltpu.make_async_remote_copy(
                    src_ref=slot, dst_ref=slot, send_sem=send_sems.at[w, k], recv_sem=recv_sems.at[w, k],
                    device_id=peer, device_id_type=pl.DeviceIdType.MESH).wait_recv()
        for cp in sends:
            cp.wait_send()
        for mine in locals_:
            mine.wait()

    launch()
    return [r[...] for r in out_refs]


def _adam_math(w, g, m, v):
    m = ADAM_B1 * m + (1.0 - ADAM_B1) * g
    v = ADAM_B2 * v + (1.0 - ADAM_B2) * (g * g)
    m_hat = m / (1.0 - ADAM_B1 ** ADAM_STEP)
    v_hat = v / (1.0 - ADAM_B2 ** ADAM_STEP)
    delta = -ADAM_LR * (m_hat / (jnp.sqrt(v_hat) + ADAM_EPS) + ADAM_WD * w)
    return delta, m, v


def _small_allreduce_adam(pg, pw, pm, pv):
    rows = pg.shape[0]

    def body(pg_ref, pw_ref, pm_ref, pv_ref, g_ref, d_ref, m_ref, v_ref, gath, send_sems, recv_sems):
        me = _my_coords()
        my_idx = _dev_index(*me)
        gath[my_idx] = pg_ref[...]
        sends = []
        for k, f in enumerate(_FLIPS):
            peer = _flip(me, f)
            cp = pltpu.make_async_remote_copy(
                src_ref=pg_ref, dst_ref=gath.at[my_idx],
                send_sem=send_sems.at[k], recv_sem=recv_sems.at[k],
                device_id=peer, device_id_type=pl.DeviceIdType.MESH)
            cp.start()
            sends.append(cp)
        for k, f in enumerate(_FLIPS):
            peer = _flip(me, f)
            slot = gath.at[_dev_index(*peer)]
            pltpu.make_async_remote_copy(
                src_ref=slot, dst_ref=slot, send_sem=send_sems.at[k], recv_sem=recv_sems.at[k],
                device_id=peer, device_id_type=pl.DeviceIdType.MESH).wait_recv()
        for cp in sends:
            cp.wait_send()
        g = gath[0]
        for d in range(1, N_DEV):
            g = g + gath[d]
        delta, m, v = _adam_math(pw_ref[...], g, pm_ref[...], pv_ref[...])
        g_ref[...] = g
        d_ref[...] = delta
        m_ref[...] = m
        v_ref[...] = v

    vm = pl.BlockSpec(memory_space=pltpu.VMEM)
    return _pcall(
        body, name="small_allreduce_adam",
        out_shape=[_sds(pg.shape, jnp.float32)] * 4,
        in_specs=[vm] * 4, out_specs=[vm] * 4,
        scratch=[pltpu.VMEM((N_DEV, rows, LANES), jnp.float32),
                 pltpu.SemaphoreType.DMA((7,)), pltpu.SemaphoreType.DMA((7,))])(pg, pw, pm, pv)


def _adam_big(recv, w, m, v, name):
    _, rows, cols = recv.shape
    tr = rows
    while tr * cols > 256 * 1024 and tr % 2 == 0 and (tr // 2) % 16 == 0:
        tr //= 2

    def body(r_ref, w_ref, m_ref, v_ref, g_ref, d_ref, mo_ref, vo_ref):
        g = r_ref[0].astype(jnp.float32)
        for d in range(1, N_DEV):
            g = g + r_ref[d].astype(jnp.float32)
        delta, mn, vn = _adam_math(w_ref[...], g, m_ref[...], v_ref[...])
        g_ref[...] = g
        d_ref[...] = delta
        mo_ref[...] = mn
        vo_ref[...] = vn

    blk = pl.BlockSpec((tr, cols), lambda i: (i, 0))
    return _pcall(
        body, name=name, grid=(rows // tr,),
        out_shape=[_sds((rows, cols), jnp.float32)] * 4,
        in_specs=[pl.BlockSpec((N_DEV, tr, cols), lambda i: (0, i, 0)), blk, blk, blk],
        out_specs=[blk] * 4, sem=("parallel",))(recv, w, m, v)


def _mm(a, b, *, ta=False, tb=False, out_dtype, tm, tk, name):
    (kdim, mdim) = a.shape if ta else a.shape[::-1]
    ndim = b.shape[0] if tb else b.shape[1]
    tm, tk = min(tm, mdim), min(tk, kdim)
    nk = kdim // tk

    def body(a_ref, b_ref, o_ref, acc):
        k = pl.program_id(1)
        if ta:
            part = _dot_tn(a_ref[...], b_ref[...])
        elif tb:
            part = _dot_nt(a_ref[...], b_ref[...])
        else:
            part = _dot(a_ref[...], b_ref[...])

        @pl.when(k == 0)
        def _():
            acc[...] = part

        @pl.when(k > 0)
        def _():
            acc[...] += part

        @pl.when(k == nk - 1)
        def _():
            o_ref[...] = acc[...].astype(o_ref.dtype)

    a_spec = pl.BlockSpec((tk, tm), lambda i, k: (k, i)) if ta else pl.BlockSpec((tm, tk), lambda i, k: (i, k))
    b_spec = pl.BlockSpec((ndim, tk), lambda i, k: (0, k)) if tb else pl.BlockSpec((tk, ndim), lambda i, k: (k, 0))
    return _pcall(
        body, name=name, grid=(mdim // tm, nk), out_shape=_sds((mdim, ndim), out_dtype),
        in_specs=[a_spec, b_spec], out_specs=pl.BlockSpec((tm, ndim), lambda i, k: (i, 0)),
        scratch=[pltpu.VMEM((tm, ndim), jnp.float32)], sem=("parallel", "arbitrary"))(a, b)


def _in_proj(x, g, w):
    s, d = x.shape
    n = w.shape[1]
    tm = min(ROW_TILE, s)

    def body(x_ref, g_ref, w_ref, p_ref, hn_ref):
        hn, _, _ = _norm_fwd(x_ref[...], g_ref[...])
        hn_ref[...] = hn.astype(hn_ref.dtype)
        p_ref[...] = _dot(hn, w_ref[...])

    return _pcall(
        body, name="in_proj", grid=(s // tm,),
        out_shape=[_sds((s, n), jnp.float32), _sds((s, d), MXU)],
        in_specs=[pl.BlockSpec((tm, d), lambda i: (i, 0)), pl.BlockSpec((1, d), lambda i: (0, 0)),
                  pl.BlockSpec((d, n), lambda i: (0, 0))],
        out_specs=[pl.BlockSpec((tm, n), lambda i: (i, 0)), pl.BlockSpec((tm, d), lambda i: (i, 0))],
        sem=("parallel",))(x, g, w)


def _mla_prep(proj, cos, sin, g_cq, g_ckv, w_uq, w_ukv, g_qn, g_qr, g_kn, g_kr):
    s = proj.shape[0]
    tm = min(ROW_TILE, s)
    nh = MLA_HEADS

    def body(cq_ref, ckv_ref, kr_ref, cos_ref, sin_ref, gcq_ref, gckv_ref, wuq_ref, wukv_ref,
             gqn_ref, gqr_ref, gkn_ref, gkr_ref,
             qc_ref, kc_ref, v_ref, qb_ref, kvb_ref, cqn_ref, ckvn_ref):
        cos_t, sin_t = cos_ref[...], sin_ref[...]
        lo = _lo_mask((tm, LANES))
        cqn, _, _ = _norm_fwd(cq_ref[...], gcq_ref[...])
        cqn_ref[...] = cqn.astype(cqn_ref.dtype)
        qb = _dot(cqn, wuq_ref[...])
        qb_ref[...] = qb
        ckvn, _, _ = _norm_fwd(ckv_ref[...], gckv_ref[...])
        ckvn_ref[...] = ckvn.astype(ckvn_ref.dtype)
        kvb = _dot(ckvn, wukv_ref[...])
        kvb_ref[...] = kvb
        kr, _, _ = _norm_fwd(kr_ref[...], gkr_ref[...], half=True)
        kr = _rope(kr, cos_t, sin_t)
        kr2 = jnp.where(lo, kr, pltpu.roll(kr, 64, 1))
        ropes = []
        for j in range(nh // 2):
            xr = qb[:, nh * MLA_NOPE + LANES * j: nh * MLA_NOPE + LANES * (j + 1)]
            qr, _, _ = _norm_fwd(xr, gqr_ref[...], half=True)
            ropes.append(_rope(qr, cos_t, sin_t))
        for h in range(nh):
            qn, _, _ = _norm_fwd(qb[:, MLA_NOPE * h: MLA_NOPE * (h + 1)], gqn_ref[...])
            mask = lo if h % 2 == 0 else jnp.logical_not(lo)
            qr = jnp.where(mask, ropes[h // 2], 0.0)
            qc_ref[h] = jnp.concatenate([qn, qr], axis=1).astype(qc_ref.dtype)
            kn, _, _ = _norm_fwd(kvb[:, 256 * h: 256 * h + MLA_NOPE], gkn_ref[...])
            kc_ref[h] = jnp.concatenate([kn, kr2], axis=1).astype(kc_ref.dtype)
            v_ref[h] = kvb[:, 256 * h + MLA_NOPE: 256 * (h + 1)].astype(v_ref.dtype)

    def col(width, start):
        return pl.BlockSpec((tm, width), lambda i: (i, start // width))

    def full(shape):
        return pl.BlockSpec(shape, lambda i: (0,) * len(shape))

    def row(width):
        return pl.BlockSpec((tm, width), lambda i: (i, 0))

    def heads(width):
        return pl.BlockSpec((nh, tm, width), lambda i: (0, i, 0))

    return _pcall(
        body, name="mla_prep", grid=(s // tm,),
        out_shape=[_sds((nh, s, 256), MXU), _sds((nh, s, 256), MXU), _sds((nh, s, MLA_V), MXU),
                   _sds((s, 768), jnp.float32), _sds((s, 1024), jnp.float32),
                   _sds((s, 512), MXU), _sds((s, 512), MXU)],
        in_specs=[col(512, C_CQ), col(512, C_CKV), col(LANES, C_KR), row(LANES), row(LANES),
                  full((1, 512)), full((1, 512)), full((512, 768)), full((512, 1024)),
                  full((1, LANES)), full((1, LANES)), full((1, LANES)), full((1, LANES))],
        out_specs=[heads(256), heads(256), heads(MLA_V), row(768), row(1024), row(512), row(512)],
        sem=("parallel",))(proj, proj, proj, cos, sin, g_cq, g_ckv, w_uq, w_ukv, g_qn, g_qr, g_kn, g_kr)


def _mla_fwd(qc, kc, v):
    nh, s, _ = qc.shape
    t = min(ATT_TILE, s)
    nb = s // t
    scale = (MLA_NOPE + MLA_ROPE) ** -0.5

    def body(q_ref, k_ref, v_ref, y_ref, lse_ref, m_sc, l_sc, acc):
        qi, ki = pl.program_id(1), pl.program_id(2)

        @pl.when(ki == 0)
        def _():
            m_sc[...] = jnp.full_like(m_sc, NEG_INF)
            l_sc[...] = jnp.zeros_like(l_sc)
            acc[...] = jnp.zeros_like(acc)

        @pl.when(ki <= qi)
        def _():
            sc = _dot_nt(q_ref[0], k_ref[0]) * scale
            r_i = lax.broadcasted_iota(jnp.int32, sc.shape, 0) + qi * t
            c_i = lax.broadcasted_iota(jnp.int32, sc.shape, 1) + ki * t
            sc = jnp.where(c_i <= r_i, sc, NEG_INF)
            m_new = jnp.maximum(m_sc[...], jnp.max(sc, -1, keepdims=True))
            alpha = jnp.exp(m_sc[...] - m_new)
            p = jnp.exp(sc - m_new)
            l_sc[...] = alpha * l_sc[...] + jnp.sum(p, -1, keepdims=True)
            acc[...] = alpha * acc[...] + _dot(p, v_ref[0])
            m_sc[...] = m_new

        @pl.when(ki == qi)
        def _():
            y_ref[...] = acc[...] / l_sc[...]
            lse_ref[0] = m_sc[...] + jnp.log(l_sc[...])

    return _pcall(
        body, name="mla_fwd", grid=(nh, nb, nb),
        out_shape=[_sds((s, nh * MLA_V), jnp.float32), _sds((nh, s, 1), jnp.float32)],
        in_specs=[pl.BlockSpec((1, t, 256), lambda h, i, k: (h, i, 0)),
                  pl.BlockSpec((1, t, 256), lambda h, i, k: (h, jnp.minimum(k, i), 0)),
                  pl.BlockSpec((1, t, MLA_V), lambda h, i, k: (h, jnp.minimum(k, i), 0))],
        out_specs=[pl.BlockSpec((t, MLA_V), lambda h, i, k: (i, h)),
                   pl.BlockSpec((1, t, 1), lambda h, i, k: (h, i, 0))],
        scratch=[pltpu.VMEM((t, 1), jnp.float32), pltpu.VMEM((t, 1), jnp.float32),
                 pltpu.VMEM((t, MLA_V), jnp.float32)],
        sem=("parallel", "parallel", "arbitrary"))(qc, kc, v)


def _memkv_prep(mem, g_mem, w_mkv, g_mk):
    ml, d = mem.shape
    hw = MEM_HEADS * MEM_DIM

    def body(mem_ref, g_ref, w_ref, gk_ref, k_ref, v_ref, kv_ref, mn_ref):
        mn, _, _ = _norm_fwd(mem_ref[...], g_ref[...])
        mn_ref[...] = mn.astype(mn_ref.dtype)
        kv = _dot(mn, w_ref[...])
        kv_ref[...] = kv
        for h in range(MEM_HEADS):
            kn, _, _ = _norm_fwd(kv[:, MEM_DIM * h: MEM_DIM * (h + 1)], gk_ref[...])
            k_ref[:, MEM_DIM * h: MEM_DIM * (h + 1)] = kn.astype(k_ref.dtype)
        v_ref[...] = kv[:, hw:].astype(v_ref.dtype)

    vm = pl.BlockSpec(memory_space=pltpu.VMEM)
    return _pcall(
        body, name="memkv_prep",
        out_shape=[_sds((ml, hw), MXU), _sds((ml, hw), MXU), _sds((ml, 2 * hw), jnp.float32), _sds((ml, d), MXU)],
        in_specs=[vm] * 4, out_specs=[vm] * 4)(mem, g_mem, w_mkv, g_mk)


def _mem_fwd(proj, g_mq, km, vmm):
    s = proj.shape[0]
    ml, hw = km.shape
    tm = min(FFN_TILE, s)
    scale = MEM_DIM ** -0.5

    def body(q_ref, g_ref, k_ref, v_ref, y_ref, lse_ref):
        col = lax.broadcasted_iota(jnp.int32, (tm, MEM_HEADS), 1)
        lse_t = jnp.zeros((tm, MEM_HEADS), jnp.float32)
        for h in range(MEM_HEADS):
            sl = slice(MEM_DIM * h, MEM_DIM * (h + 1))
            qn, _, _ = _norm_fwd(q_ref[:, sl], g_ref[...])
            sc = _dot_nt(qn, k_ref[:, sl]) * scale
            m = jnp.max(sc, -1, keepdims=True)
            p = jnp.exp(sc - m)
            l = jnp.sum(p, -1, keepdims=True)
            y_ref[:, sl] = _dot(p, v_ref[:, sl]) / l
            lse_t = jnp.where(col == h, m + jnp.log(l), lse_t)
        lse_ref[...] = lse_t

    return _pcall(
        body, name="mem_fwd", grid=(s // tm,),
        out_shape=[_sds((s, hw), jnp.float32), _sds((s, MEM_HEADS), jnp.float32)],
        in_specs=[pl.BlockSpec((tm, hw), lambda i: (i, C_QM // hw)), pl.BlockSpec((1, MEM_DIM), lambda i: (0, 0)),
                  pl.BlockSpec((ml, hw), lambda i: (0, 0)), pl.BlockSpec((ml, hw), lambda i: (0, 0))],
        out_specs=[pl.BlockSpec((tm, hw), lambda i: (i, 0)), pl.BlockSpec((tm, MEM_HEADS), lambda i: (i, 0))],
        sem=("parallel",))(proj, g_mq, km, vmm)


def _alibi_slope(h):
    return float(2.0 ** (-8.0 * (h + 1) / SWA_Q_HEADS))


def _swa_common(n, kp, kc, vp, vc, pq, pkp, pkc, gk):
    b = SWA_BLOCK
    k_raw = jnp.concatenate([kp, kc], axis=0)
    kn, kxn, kr = _norm_fwd(k_raw, gk, half=True)
    v = jnp.concatenate([vp, vc], axis=0)
    dist = jnp.abs(pq - jnp.concatenate([pkp, pkc], axis=1))
    r_i = lax.broadcasted_iota(jnp.int32, (b, 2 * b), 0)
    c_i = lax.broadcasted_iota(jnp.int32, (b, 2 * b), 1)
    valid = (c_i > r_i) & (c_i <= r_i + b) & (c_i >= jnp.where(n > 0, 0, b))
    return kn, v, dist, valid


def _swa_specs(s):
    b = SWA_BLOCK
    prev = lambda n: jnp.maximum(n - 1, 0)
    return [
        pl.BlockSpec((b, 1024), lambda n: (n, C_QA // 1024)),
        pl.BlockSpec((b, LANES), lambda n: (prev(n), C_KA // LANES)),
        pl.BlockSpec((b, LANES), lambda n: (n, C_KA // LANES)),
        pl.BlockSpec((b, LANES), lambda n: (prev(n), C_VA // LANES)),
        pl.BlockSpec((b, LANES), lambda n: (n, C_VA // LANES)),
        pl.BlockSpec((b, 1), lambda n: (n, 0)),
        pl.BlockSpec((1, b), lambda n: (0, prev(n))),
        pl.BlockSpec((1, b), lambda n: (0, n)),
        pl.BlockSpec((1, LANES), lambda n: (0, 0)),
        pl.BlockSpec((1, LANES), lambda n: (0, 0)),
        pl.BlockSpec(memory_space=pltpu.SMEM),
    ]


def _swa_fwd(proj, posc, posr, gq, gk, sinks):
    s = proj.shape[0]
    b = SWA_BLOCK
    scale = SWA_DIM ** -0.5

    def body(q_ref, kp_ref, kc_ref, vp_ref, vc_ref, pq_ref, pkp_ref, pkc_ref, gq_ref, gk_ref, sink_ref,
             y_ref, lse_ref):
        n = pl.program_id(0)
        kn, v, dist, valid = _swa_common(n, kp_ref[...], kc_ref[...], vp_ref[...], vc_ref[...],
                                         pq_ref[...], pkp_ref[...], pkc_ref[...], gk_ref[...])
        lo = _lo_mask((b, LANES))
        col = lax.broadcasted_iota(jnp.int32, (b, SWA_Q_HEADS), 1)
        lse_t = jnp.zeros((b, SWA_Q_HEADS), jnp.float32)
        for j in range(SWA_Q_HEADS // 2):
            hk = (2 * j) // (SWA_Q_HEADS // SWA_KV_HEADS)
            kvmask = lo if hk == 0 else jnp.logical_not(lo)
            qn, _, _ = _norm_fwd(q_ref[:, LANES * j: LANES * (j + 1)], gq_ref[...], half=True)
            qsw = pltpu.roll(qn, 64, 1)
            outs = []
            for e in range(2):
                h = 2 * j + e
                qm = jnp.where(kvmask, qn if e == hk else qsw, 0.0)
                sc = _dot_nt(qm, kn) * scale - _alibi_slope(h) * dist
                sc = jnp.where(valid, sc, NEG_INF)
                sk = sink_ref[h]
                m = jnp.maximum(jnp.max(sc, -1, keepdims=True), sk)
                p = jnp.exp(sc - m)
                l = jnp.sum(p, -1, keepdims=True) + jnp.exp(sk - m)
                o = _dot(p, v) / l
                outs.append(o if e == hk else pltpu.roll(o, 64, 1))
                lse_t = jnp.where(col == h, m + jnp.log(l), lse_t)
            y_ref[:, LANES * j: LANES * (j + 1)] = jnp.where(lo, outs[0], outs[1])
        lse_ref[...] = lse_t

    return _pcall(
        body, name="swa_fwd", grid=(s // b,),
        out_shape=[_sds((s, 1024), jnp.float32), _sds((s, SWA_Q_HEADS), jnp.float32)],
        in_specs=_swa_specs(s),
        out_specs=[pl.BlockSpec((b, 1024), lambda n: (n, 0)), pl.BlockSpec((b, SWA_Q_HEADS), lambda n: (n, 0))],
        sem=("parallel",))(proj, proj, proj, proj, proj, posc, posr, posr, gq, gk, sinks)


def _out_proj(y_a, y_b, y_m, x, w_out, g_ffn):
    s, d = x.shape
    tm = min(ROW_TILE, s)

    def body(ya_ref, yb_ref, ym_ref, x_ref, w_ref, g_ref, h1_ref, fn_ref):
        y = jnp.concatenate([ya_ref[...].astype(MXU), yb_ref[...].astype(MXU), ym_ref[...].astype(MXU)], axis=1)
        h1 = x_ref[...] + _dot(y, w_ref[...])
        h1_ref[...] = h1
        fn, _, _ = _norm_fwd(h1, g_ref[...])
        fn_ref[...] = fn.astype(fn_ref.dtype)

    def row(width):
        return pl.BlockSpec((tm, width), lambda i: (i, 0))

    return _pcall(
        body, name="out_proj", grid=(s // tm,),
        out_shape=[_sds((s, d), jnp.float32), _sds((s, d), MXU)],
        in_specs=[row(1024), row(512), row(512), row(d), pl.BlockSpec(w_out.shape, lambda i: (0, 0)),
                  pl.BlockSpec((1, d), lambda i: (0, 0))],
        out_specs=[row(d), row(d)], sem=("parallel",))(y_a, y_b, y_m, x, w_out, g_ffn)


def _ffn_gu(fn, w_gu):
    s, d = fn.shape
    f = w_gu.shape[-1]
    tm = min(FFN_TILE, s)

    def body(fn_ref, w_ref, gu_ref, act_ref):
        x = fn_ref[...]
        g = _dot(x, w_ref[0, 0])
        u = _dot(x, w_ref[0, 1])
        gu_ref[0, 0] = g
        gu_ref[0, 1] = u
        act_ref[0] = (g * jax.nn.sigmoid(g) * u).astype(act_ref.dtype)

    return _pcall(
        body, name="ffn_gate_up", grid=(N_DEV, s // tm),
        out_shape=[_sds((N_DEV, 2, s, f), jnp.float32), _sds((N_DEV, s, f), MXU)],
        in_specs=[pl.BlockSpec((tm, d), lambda j, i: (i, 0)),
                  pl.BlockSpec((1, 2, d, f), lambda j, i: (j, 0, 0, 0))],
        out_specs=[pl.BlockSpec((1, 2, tm, f), lambda j, i: (j, 0, i, 0)),
                   pl.BlockSpec((1, tm, f), lambda j, i: (j, i, 0))],
        sem=("parallel", "parallel"))(fn, w_gu)


def _ffn_down(act, w_d, h1, target):
    _, s, f = act.shape
    d = h1.shape[1]
    tm = min(FFN_TILE, s)

    def body(a_ref, w_ref, h1_ref, t_ref, dout_ref, loss_ref, acc):
        i, j = pl.program_id(0), pl.program_id(1)
        part = _dot(a_ref[0], w_ref[0])

        @pl.when(j == 0)
        def _():
            acc[...] = h1_ref[...] + part

        @pl.when(j > 0)
        def _():
            acc[...] += part

        @pl.when((i == 0) & (j == 0))
        def _():
            loss_ref[...] = jnp.zeros_like(loss_ref)

        @pl.when(j == N_DEV - 1)
        def _():
            diff = acc[...] - t_ref[...]
            dout_ref[...] = diff / d
            loss_ref[...] += 0.5 * jnp.sum(jnp.sum(diff * diff, -1, keepdims=True) / d)

    row = pl.BlockSpec((tm, d), lambda i, j: (i, 0))
    return _pcall(
        body, name="ffn_down", grid=(s // tm, N_DEV),
        out_shape=[_sds((s, d), jnp.float32), _sds((8, LANES), jnp.float32)],
        in_specs=[pl.BlockSpec((1, tm, f), lambda i, j: (j, i, 0)), pl.BlockSpec((1, f, d), lambda i, j: (j, 0, 0)),
                  row, row],
        out_specs=[row, pl.BlockSpec((8, LANES), lambda i, j: (0, 0))],
        scratch=[pltpu.VMEM((tm, d), jnp.float32)], sem=("arbitrary", "arbitrary"))(act, w_d, h1, target)


def _ffn_bwd_act(dout, w_d, gu):
    s, d = dout.shape
    f = w_d.shape[1]
    tm = min(FFN_TILE, s)
    ni = s // tm

    def body(do_ref, w_ref, gu_ref, dgu_ref, dw_ref, acc):
        i = pl.program_id(1)
        do = do_ref[...].astype(MXU)
        d_act = _dot_nt(do, w_ref[0])
        g, u = gu_ref[0, 0], gu_ref[0, 1]
        sig = jax.nn.sigmoid(g)
        silu = g * sig
        dgu_ref[0, 0] = (d_act * u * (sig * (1.0 + g * (1.0 - sig)))).astype(dgu_ref.dtype)
        dgu_ref[0, 1] = (d_act * silu).astype(dgu_ref.dtype)
        part = _dot_tn(silu * u, do)

        @pl.when(i == 0)
        def _():
            acc[...] = part

        @pl.when(i > 0)
        def _():
            acc[...] += part

        @pl.when(i == ni - 1)
        def _():
            dw_ref[0] = acc[...].astype(dw_ref.dtype)

    return _pcall(
        body, name="ffn_bwd_act", grid=(N_DEV, ni),
        out_shape=[_sds((N_DEV, 2, s, f), MXU), _sds((N_DEV, f, d), WIRE)],
        in_specs=[pl.BlockSpec((tm, d), lambda j, i: (i, 0)), pl.BlockSpec((1, f, d), lambda j, i: (j, 0, 0)),
                  pl.BlockSpec((1, 2, tm, f), lambda j, i: (j, 0, i, 0))],
        out_specs=[pl.BlockSpec((1, 2, tm, f), lambda j, i: (j, 0, i, 0)),
                   pl.BlockSpec((1, f, d), lambda j, i: (j, 0, 0))],
        scratch=[pltpu.VMEM((f, d), jnp.float32)], sem=("parallel", "arbitrary"))(dout, w_d, gu)


def _ffn_dw_gu(fn, dgu):
    s, d = fn.shape
    f = dgu.shape[-1]
    tk = min(FFN_TILE, s)
    nk = s // tk

    def body(fn_ref, dgu_ref, dw_ref, acc):
        k = pl.program_id(1)
        x = fn_ref[...]
        pg = _dot_tn(x, dgu_ref[0, 0])
        pu = _dot_tn(x, dgu_ref[0, 1])

        @pl.when(k == 0)
        def _():
            acc[0] = pg
            acc[1] = pu

        @pl.when(k > 0)
        def _():
            acc[0] += pg
            acc[1] += pu

        @pl.when(k == nk - 1)
        def _():
            dw_ref[0] = acc[...].astype(dw_ref.dtype)

    return _pcall(
        body, name="ffn_dw_gate_up", grid=(N_DEV, nk),
        out_shape=_sds((N_DEV, 2, d, f), WIRE),
        in_specs=[pl.BlockSpec((tk, d), lambda j, k: (k, 0)), pl.BlockSpec((1, 2, tk, f), lambda j, k: (j, 0, k, 0))],
        out_specs=pl.BlockSpec((1, 2, d, f), lambda j, k: (j, 0, 0, 0)),
        scratch=[pltpu.VMEM((2, d, f), jnp.float32)], sem=("parallel", "arbitrary"))(fn, dgu)


def _ffn_dfn(dgu, w_gu, after):
    _, _, s, f = dgu.shape
    d = w_gu.shape[2]
    tm = min(FFN_TILE, s)

    def body(dgu_ref, w_ref, dfn_ref):
        j = pl.program_id(1)
        part = _dot_nt(dgu_ref[0, 0], w_ref[0, 0]) + _dot_nt(dgu_ref[0, 1], w_ref[0, 1])

        @pl.when(j == 0)
        def _():
            dfn_ref[...] = part

        @pl.when(j > 0)
        def _():
            dfn_ref[...] += part

    return _pcall(
        body, name="ffn_dfn", grid=(s // tm, N_DEV),
        out_shape=_sds((s, d), jnp.float32),
        in_specs=[pl.BlockSpec((1, 2, tm, f), lambda i, j: (j, 0, i, 0)),
                  pl.BlockSpec((1, 2, d, f), lambda i, j: (j, 0, 0, 0))],
        out_specs=pl.BlockSpec((tm, d), lambda i, j: (i, 0)),
        sem=("parallel", "arbitrary"), after=after)(dgu, w_gu)


def _ffn_norm_bwd(d_fn, dout, h1, g_ffn):
    s, d = h1.shape
    tm = min(ROW_TILE, s)

    def body(dfn_ref, do_ref, h1_ref, g_ref, dh1_ref, dg_ref):
        i = pl.program_id(0)

        @pl.when(i == 0)
        def _():
            dg_ref[...] = jnp.zeros_like(dg_ref)

        _, xn, r = _norm_fwd(h1_ref[...], g_ref[...])
        dx, dg = _norm_bwd(xn, r, g_ref[...], dfn_ref[...])
        dh1_ref[...] = do_ref[...] + dx
        dg_ref[...] += dg

    row = pl.BlockSpec((tm, d), lambda i: (i, 0))
    vec = pl.BlockSpec((1, d), lambda i: (0, 0))
    return _pcall(
        body, name="ffn_norm_bwd", grid=(s // tm,),
        out_shape=[_sds((s, d), jnp.float32), _sds((1, d), jnp.float32)],
        in_specs=[row, row, row, vec], out_specs=[row, vec], sem=("arbitrary",))(d_fn, dout, h1, g_ffn)


def _mem_bwd(proj, g_mq, km, vmm, d_y, y_m, lse):
    s = proj.shape[0]
    ml, hw = km.shape
    tm = min(FFN_TILE, s)
    scale = MEM_DIM ** -0.5

    def body(q_ref, g_ref, k_ref, v_ref, do_ref, y_ref, lse_ref, dq_ref, dk_ref, dv_ref, dg_ref):
        i = pl.program_id(0)

        @pl.when(i == 0)
        def _():
            dk_ref[...] = jnp.zeros_like(dk_ref)
            dv_ref[...] = jnp.zeros_like(dv_ref)
            dg_ref[...] = jnp.zeros_like(dg_ref)

        col = lax.broadcasted_iota(jnp.int32, (tm, MEM_HEADS), 1)
        lse_t = lse_ref[...]
        for h in range(MEM_HEADS):
            sl = slice(MEM_DIM * h, MEM_DIM * (h + 1))
            qn, xn, r = _norm_fwd(q_ref[:, sl], g_ref[...])
            lse_h = jnp.sum(jnp.where(col == h, lse_t, 0.0), -1, keepdims=True)
            p = jnp.exp(_dot_nt(qn, k_ref[:, sl]) * scale - lse_h)
            do = do_ref[:, sl]
            dd = jnp.sum(do * y_ref[:, sl], -1, keepdims=True)
            dp = _dot_nt(do, v_ref[:, sl])
            ds = (p * (dp - dd)).astype(MXU)
            dv_ref[:, sl] += _dot_tn(p, do)
            dk_ref[:, sl] += _dot_tn(ds, qn) * scale
            dx, dg = _norm_bwd(xn, r, g_ref[...], _dot(ds, k_ref[:, sl]) * scale)
            dq_ref[:, sl] = dx.astype(dq_ref.dtype)
            dg_ref[...] += dg

    full = pl.BlockSpec((ml, hw), lambda i: (0, 0))
    return _pcall(
        body, name="mem_bwd", grid=(s // tm,),
        out_shape=[_sds((s, hw), MXU), _sds((ml, hw), jnp.float32), _sds((ml, hw), jnp.float32),
                   _sds((1, MEM_DIM), jnp.float32)],
        in_specs=[pl.BlockSpec((tm, hw), lambda i: (i, C_QM // hw)), pl.BlockSpec((1, MEM_DIM), lambda i: (0, 0)),
                  full, full, pl.BlockSpec((tm, hw), lambda i: (i, 3)), pl.BlockSpec((tm, hw), lambda i: (i, 0)),
                  pl.BlockSpec((tm, MEM_HEADS), lambda i: (i, 0))],
        out_specs=[pl.BlockSpec((tm, hw), lambda i: (i, 0)), full, full,
                   pl.BlockSpec((1, MEM_DIM), lambda i: (0, 0))],
        sem=("arbitrary",))(proj, g_mq, km, vmm, d_y, y_m, lse)


def _memkv_bwd(mem, g_mem, w_mkv, g_mk, kv, memn, dk, dv):
    ml, d = mem.shape
    hw = MEM_HEADS * MEM_DIM

    def body(mem_ref, g_ref, w_ref, gk_ref, kv_ref, mn_ref, dk_ref, dv_ref, dw_ref, dgm_ref, dgk_ref):
        parts = []
        dgk = jnp.zeros((1, MEM_DIM), jnp.float32)
        for h in range(MEM_HEADS):
            sl = slice(MEM_DIM * h, MEM_DIM * (h + 1))
            _, xn, r = _norm_fwd(kv_ref[:, sl], gk_ref[...])
            dx, dg = _norm_bwd(xn, r, gk_ref[...], dk_ref[:, sl])
            parts.append(dx)
            dgk = dgk + dg
        dkv = jnp.concatenate(parts + [dv_ref[...]], axis=1).astype(MXU)
        dgk_ref[...] = dgk
        dw_ref[...] = _dot_tn(mn_ref[...], dkv).astype(dw_ref.dtype)
        d_mn = _dot_nt(dkv, w_ref[...])
        _, xn, _ = _norm_fwd(mem_ref[...], g_ref[...])
        dgm_ref[...] = jnp.sum(d_mn * xn, 0, keepdims=True)

    vm = pl.BlockSpec(memory_space=pltpu.VMEM)
    return _pcall(
        body, name="memkv_bwd",
        out_shape=[_sds((d, 2 * hw), WIRE), _sds((1, d), jnp.float32), _sds((1, MEM_DIM), jnp.float32)],
        in_specs=[vm] * 8, out_specs=[vm] * 3)(mem, g_mem, w_mkv, g_mk, kv, memn, dk, dv)


def _mla_bwd(qc, kc, v, d_y, y_b, lse, after):
    nh, s, _ = qc.shape
    t = min(ATT_TILE, s)
    nb = s // t
    scale = (MLA_NOPE + MLA_ROPE) ** -0.5

    def body(q_ref, k_ref, v_ref, do_ref, y_ref, lse_ref, dq_ref, dk_ref, dv_ref, dk_acc, dv_acc):
        kj, qi = pl.program_id(1), pl.program_id(2)

        @pl.when((kj == 0) & (qi == 0))
        def _():
            dq_ref[...] = jnp.zeros_like(dq_ref)

        @pl.when(qi == kj)
        def _():
            dk_acc[...] = jnp.zeros_like(dk_acc)
            dv_acc[...] = jnp.zeros_like(dv_acc)

        @pl.when(qi >= kj)
        def _():
            q, k = q_ref[0], k_ref[0]
            sc = _dot_nt(q, k) * scale
            r_i = lax.broadcasted_iota(jnp.int32, sc.shape, 0) + qi * t
            c_i = lax.broadcasted_iota(jnp.int32, sc.shape, 1) + kj * t
            p = jnp.exp(jnp.where(c_i <= r_i, sc, NEG_INF) - lse_ref[0])
            do = do_ref[...]
            dd = jnp.sum(do * y_ref[...], -1, keepdims=True)
            dp = _dot_nt(do, v_ref[0])
            ds = (p * (dp - dd) * scale).astype(MXU)
            dv_acc[...] += _dot_tn(p, do)
            dk_acc[...] += _dot_tn(ds, q)
            rows = pl.ds(pl.multiple_of(qi * t, t), t)
            dq_ref[0, rows, :] += _dot(ds, k)

        @pl.when(qi == nb - 1)
        def _():
            dk_ref[0] = dk_acc[...]
            dv_ref[0] = dv_acc[...]

    qmap = lambda h, j, i: (h, jnp.maximum(i, j), 0)
    return _pcall(
        body, name="mla_bwd", grid=(nh, nb, nb),
        out_shape=[_sds((nh, s, 256), jnp.float32), _sds((nh, s, 256), jnp.float32),
                   _sds((nh, s, MLA_V), jnp.float32)],
        in_specs=[pl.BlockSpec((1, t, 256), qmap),
                  pl.BlockSpec((1, t, 256), lambda h, j, i: (h, j, 0)),
                  pl.BlockSpec((1, t, MLA_V), lambda h, j, i: (h, j, 0)),
                  pl.BlockSpec((t, MLA_V), lambda h, j, i: (jnp.maximum(i, j), 8 + h)),
                  pl.BlockSpec((t, MLA_V), lambda h, j, i: (jnp.maximum(i, j), h)),
                  pl.BlockSpec((1, t, 1), qmap)],
        out_specs=[pl.BlockSpec((1, s, 256), lambda h, j, i: (h, 0, 0)),
                   pl.BlockSpec((1, t, 256), lambda h, j, i: (h, j, 0)),
                   pl.BlockSpec((1, t, MLA_V), lambda h, j, i: (h, j, 0))],
        scratch=[pltpu.VMEM((t, 256), jnp.float32), pltpu.VMEM((t, MLA_V), jnp.float32)],
        sem=("parallel", "arbitrary", "arbitrary"), after=after)(qc, kc, v, d_y, y_b, lse)


def _mla_prep_bwd(proj, cos, sin, g_cq, g_ckv, w_uq, w_ukv, g_qn, g_qr, g_kn, g_kr,
                  qb, kvb, cqn, ckvn, dqc, dkc, dv):
    s = proj.shape[0]
    tm = min(ROW_TILE, s)
    nh = MLA_HEADS
    ni = s // tm

    def body(cq_ref, ckv_ref, kr_ref, cos_ref, sin_ref, gcq_ref, gckv_ref, wuq_ref, wukv_ref,
             gqn_ref, gqr_ref, gkn_ref, gkr_ref, qb_ref, kvb_ref, cqn_ref, ckvn_ref, dqc_ref, dkc_ref, dv_ref,
             dcq_ref, dckv_ref, dkr_ref, dwuq_ref, dwukv_ref,
             dgcq_ref, dgckv_ref, dgqn_ref, dgqr_ref, dgkn_ref, dgkr_ref, acc_uq, acc_ukv):
        i = pl.program_id(0)

        @pl.when(i == 0)
        def _():
            acc_uq[...] = jnp.zeros_like(acc_uq)
            acc_ukv[...] = jnp.zeros_like(acc_ukv)
            for ref in (dgcq_ref, dgckv_ref, dgqn_ref, dgqr_ref, dgkn_ref, dgkr_ref):
                ref[...] = jnp.zeros_like(ref)

        cos_t, sin_t = cos_ref[...], sin_ref[...]
        lo = _lo_mask((tm, LANES))
        qb_v, kvb_v = qb_ref[...], kvb_ref[...]
        dq_parts, dgqn = [], jnp.zeros((1, LANES), jnp.float32)
        for h in range(nh):
            _, xn, r = _norm_fwd(qb_v[:, MLA_NOPE * h: MLA_NOPE * (h + 1)], gqn_ref[...])
            dx, dg = _norm_bwd(xn, r, gqn_ref[...], dqc_ref[h][:, :MLA_NOPE])
            dq_parts.append(dx)
            dgqn = dgqn + dg
        dgqn_ref[...] += dgqn
        dgqr = jnp.zeros((1, LANES), jnp.float32)
        for j in range(nh // 2):
            d_rope = jnp.where(lo, dqc_ref[2 * j][:, MLA_NOPE:], dqc_ref[2 * j + 1][:, MLA_NOPE:])
            d_pre = _rope_bwd(d_rope, cos_t, sin_t)
            xr = qb_v[:, nh * MLA_NOPE + LANES * j: nh * MLA_NOPE + LANES * (j + 1)]
            _, xn, r = _norm_fwd(xr, gqr_ref[...], half=True)
            dx, dg = _norm_bwd(xn, r, gqr_ref[...], d_pre, half=True)
            dq_parts.append(dx)
            dgqr = dgqr + dg
        dgqr_ref[...] += dgqr
        dqb = jnp.concatenate(dq_parts, axis=1).astype(MXU)
        acc_uq[...] += _dot_tn(cqn_ref[...], dqb)
        _, xn, r = _norm_fwd(cq_ref[...], gcq_ref[...])
        dx, dg = _norm_bwd(xn, r, gcq_ref[...], _dot_nt(dqb, wuq_ref[...]))
        dcq_ref[...] = dx.astype(dcq_ref.dtype)
        dgcq_ref[...] += dg
        dkv_parts, dgkn = [], jnp.zeros((1, LANES), jnp.float32)
        d_kr2 = jnp.zeros((tm, LANES), jnp.float32)
        for h in range(nh):
            _, xn, r = _norm_fwd(kvb_v[:, 256 * h: 256 * h + MLA_NOPE], gkn_ref[...])
            dx, dg = _norm_bwd(xn, r, gkn_ref[...], dkc_ref[h][:, :MLA_NOPE])
            dkv_parts += [dx, dv_ref[h]]
            dgkn = dgkn + dg
            d_kr2 = d_kr2 + dkc_ref[h][:, MLA_NOPE:]
        dgkn_ref[...] += dgkn
        dkvb = jnp.concatenate(dkv_parts, axis=1).astype(MXU)
        acc_ukv[...] += _dot_tn(ckvn_ref[...], dkvb)
        _, xn, r = _norm_fwd(ckv_ref[...], gckv_ref[...])
        dx, dg = _norm_bwd(xn, r, gckv_ref[...], _dot_nt(dkvb, wukv_ref[...]))
        dckv_ref[...] = dx.astype(dckv_ref.dtype)
        dgckv_ref[...] += dg
        d_kr = jnp.where(lo, d_kr2 + pltpu.roll(d_kr2, 64, 1), 0.0)
        d_pre = _rope_bwd(d_kr, cos_t, sin_t)
        _, xn, r = _norm_fwd(kr_ref[...], gkr_ref[...], half=True)
        dx, dg = _norm_bwd(xn, r, gkr_ref[...], d_pre, half=True)
        dkr_ref[...] = jnp.where(lo, dx, 0.0).astype(dkr_ref.dtype)
        dgkr_ref[...] += jnp.where(_lo_mask((1, LANES)), dg, 0.0)

        @pl.when(i == ni - 1)
        def _():
            dwuq_ref[...] = acc_uq[...].astype(dwuq_ref.dtype)
            dwukv_ref[...] = acc_ukv[...].astype(dwukv_ref.dtype)

    def col(width, start):
        return pl.BlockSpec((tm, width), lambda i: (i, start // width))

    def full(shape):
        return pl.BlockSpec(shape, lambda i: (0,) * len(shape))

    def row(width):
        return pl.BlockSpec((tm, width), lambda i: (i, 0))

    def heads(width):
        return pl.BlockSpec((nh, tm, width), lambda i: (0, i, 0))

    vec = full((1, LANES))
    return _pcall(
        body, name="mla_prep_bwd", grid=(ni,),
        out_shape=[_sds((s, 512), MXU), _sds((s, 512), MXU), _sds((s, LANES), MXU),
                   _sds((512, 768), WIRE), _sds((512, 1024), WIRE),
                   _sds((1, 512), jnp.float32), _sds((1, 512), jnp.float32)] + [_sds((1, LANES), jnp.float32)] * 4,
        in_specs=[col(512, C_CQ), col(512, C_CKV), col(LANES, C_KR), row(LANES), row(LANES),
                  full((1, 512)), full((1, 512)), full((512, 768)), full((512, 1024)), vec, vec, vec, vec,
                  row(768), row(1024), row(512), row(512), heads(256), heads(256), heads(MLA_V)],
        out_specs=[row(512), row(512), row(LANES), full((512, 768)), full((512, 1024)),
                   full((1, 512)), full((1, 512)), vec, vec, vec, vec],
        scratch=[pltpu.VMEM((512, 768), jnp.float32), pltpu.VMEM((512, 1024), jnp.float32)],
        sem=("arbitrary",))(proj, proj, proj, cos, sin, g_cq, g_ckv, w_uq, w_ukv, g_qn, g_qr, g_kn, g_kr,
                            qb, kvb, cqn, ckvn, dqc, dkc, dv)


def _swa_bwd(proj, posc, posr, gq, gk, sinks, d_y, y_a, lse, after):
    s = proj.shape[0]
    b = SWA_BLOCK
    nb = s // b
    scale = SWA_DIM ** -0.5

    def body(q_ref, kp_ref, kc_ref, vp_ref, vc_ref, pq_ref, pkp_ref, pkc_ref, gq_ref, gk_ref, sink_ref,
             do_ref, y_ref, lse_ref, kfull_ref,
             dq_ref, dk_ref, dv_ref, dgq_ref, dgk_ref, dsink_ref, dk_acc, dv_acc):
        n = pl.program_id(0)

        @pl.when(n == 0)
        def _():
            dk_acc[...] = jnp.zeros_like(dk_acc)
            dv_acc[...] = jnp.zeros_like(dv_acc)
            dgq_ref[...] = jnp.zeros_like(dgq_ref)
            dsink_ref[...] = jnp.zeros_like(dsink_ref)

        kn, v, dist, valid = _swa_common(n, kp_ref[...], kc_ref[...], vp_ref[...], vc_ref[...],
                                         pq_ref[...], pkp_ref[...], pkc_ref[...], gk_ref[...])
        lo = _lo_mask((b, LANES))
        col = lax.broadcasted_iota(jnp.int32, (b, SWA_Q_HEADS), 1)
        col1 = lax.broadcasted_iota(jnp.int32, (1, SWA_Q_HEADS), 1)
        lse_t = lse_ref[...]
        dk_blk = jnp.zeros((2 * b, LANES), jnp.float32)
        dv_blk = jnp.zeros((2 * b, LANES), jnp.float32)
        dgq = jnp.zeros((1, LANES), jnp.float32)
        dsink = jnp.zeros((1, SWA_Q_HEADS), jnp.float32)
        for j in range(SWA_Q_HEADS // 2):
            hk = (2 * j) // (SWA_Q_HEADS // SWA_KV_HEADS)
            kvmask = lo if hk == 0 else jnp.logical_not(lo)
            sl = slice(LANES * j, LANES * (j + 1))
            qn, xn, r = _norm_fwd(q_ref[:, sl], gq_ref[...], half=True)
            qsw = pltpu.roll(qn, 64, 1)
            d2 = do_ref[:, sl]
            d2sw = pltpu.roll(d2, 64, 1)
            prod = d2 * y_ref[:, sl]
            dqs = []
            for e in range(2):
                h = 2 * j + e
                half_e = lo if e == 0 else jnp.logical_not(lo)
                qm = jnp.where(kvmask, qn if e == hk else qsw, 0.0)
                dm = jnp.where(kvmask, d2 if e == hk else d2sw, 0.0)
                sc = _dot_nt(qm, kn) * scale - _alibi_slope(h) * dist
                sc = jnp.where(valid, sc, NEG_INF)
                lse_h = jnp.sum(jnp.where(col == h, lse_t, 0.0), -1, keepdims=True)
                p = jnp.exp(sc - lse_h)
                dd = jnp.sum(jnp.where(half_e, prod, 0.0), -1, keepdims=True)
                dp = _dot_nt(dm, v)
                ds = (p * (dp - dd)).astype(MXU)
                dsink = dsink - jnp.where(col1 == h, jnp.sum(jnp.exp(sink_ref[h] - lse_h) * dd), 0.0)
                dq_m = _dot(ds, kn) * scale
                dk_blk = dk_blk + _dot_tn(ds, qm) * scale
                dv_blk = dv_blk + _dot_tn(p, dm)
                dqs.append(dq_m if e == hk else pltpu.roll(dq_m, 64, 1))
            dx, dg = _norm_bwd(xn, r, gq_ref[...], jnp.where(lo, dqs[0], dqs[1]), half=True)
            dq_ref[:, sl] = dx.astype(dq_ref.dtype)
            dgq = dgq + dg
        dgq_ref[...] += dgq
        dsink_ref[...] += dsink
        prev = pl.ds(pl.multiple_of(jnp.maximum(n - 1, 0) * b, b), b)
        cur = pl.ds(pl.multiple_of(n * b, b), b)
        dk_acc[prev, :] += dk_blk[:b]
        dv_acc[prev, :] += dv_blk[:b]
        dk_acc[cur, :] += dk_blk[b:]
        dv_acc[cur, :] += dv_blk[b:]

        @pl.when(n == nb - 1)
        def _():
            _, kxn, kr = _norm_fwd(kfull_ref[...], gk_ref[...], half=True)
            dx, dg = _norm_bwd(kxn, kr, gk_ref[...], dk_acc[...], half=True)
            dk_ref[...] = dx.astype(dk_ref.dtype)
            dv_ref[...] = dv_acc[...].astype(dv_ref.dtype)
            dgk_ref[...] = dg

    full = pl.BlockSpec((s, LANES), lambda n: (0, 0))
    vec = pl.BlockSpec((1, LANES), lambda n: (0, 0))
    return _pcall(
        body, name="swa_bwd", grid=(nb,),
        out_shape=[_sds((s, 1024), MXU), _sds((s, LANES), MXU), _sds((s, LANES), MXU),
                   _sds((1, LANES), jnp.float32), _sds((1, LANES), jnp.float32),
                   _sds((1, SWA_Q_HEADS), jnp.float32)],
        in_specs=_swa_specs(s) + [pl.BlockSpec((b, 1024), lambda n: (n, 0)), pl.BlockSpec((b, 1024), lambda n: (n, 0)),
                                  pl.BlockSpec((b, SWA_Q_HEADS), lambda n: (n, 0)),
                                  pl.BlockSpec((s, LANES), lambda n: (0, C_KA // LANES))],
        out_specs=[pl.BlockSpec((b, 1024), lambda n: (n, 0)), full, full, vec, vec,
                   pl.BlockSpec((1, SWA_Q_HEADS), lambda n: (0, 0))],
        scratch=[pltpu.VMEM((s, LANES), jnp.float32), pltpu.VMEM((s, LANES), jnp.float32)],
        sem=("arbitrary",), after=after)(proj, proj, proj, proj, proj, posc, posr, posr, gq, gk, sinks, d_y, y_a, lse,
                                         proj)


def _dx(d_proj, w_in, x, g, d_h1, after):
    s, d = x.shape
    n = w_in.shape[1]
    tm = min(ROW_TILE, s)

    def body(dp_ref, w_ref, x_ref, g_ref, dh_ref, dx_ref, dg_ref):
        i = pl.program_id(0)

        @pl.when(i == 0)
        def _():
            dg_ref[...] = jnp.zeros_like(dg_ref)

        d_hn = _dot_nt(dp_ref[...], w_ref[...])
        _, xn, r = _norm_fwd(x_ref[...], g_ref[...])
        dx, dg = _norm_bwd(xn, r, g_ref[...], d_hn)
        dx_ref[...] = dh_ref[...] + dx
        dg_ref[...] += dg

    row = pl.BlockSpec((tm, d), lambda i: (i, 0))
    vec = pl.BlockSpec((1, d), lambda i: (0, 0))
    return _pcall(
        body, name="grad_x", grid=(s // tm,),
        out_shape=[_sds((s, d), jnp.float32), _sds((1, d), jnp.float32)],
        in_specs=[pl.BlockSpec((tm, n), lambda i: (i, 0)), pl.BlockSpec((d, n), lambda i: (0, 0)), row, vec, row],
        out_specs=[row, vec], sem=("arbitrary",), after=after)(d_proj, w_in, x, g, d_h1)


_SMALL = ["attn_norm_g", "swa_q_norm_g", "swa_k_norm_g", "swa_sinks", "mla_cq_norm_g", "mla_ckv_norm_g",
          "mla_qn_norm_g", "mla_qr_norm_g", "mla_kn_norm_g", "mla_kr_norm_g", "mem_norm_g",
          "mem_q_norm_g", "mem_k_norm_g", "ffn_norm_g"]


def _pack_rows(v):
    n = v.shape[-1]
    rows = -(-n // LANES)
    rows8 = -(-rows // 8) * 8
    flat = jnp.pad(v.reshape(-1), (0, rows8 * LANES - n))
    return flat.reshape(rows8, LANES)


def _pack(parts):
    return jnp.concatenate([_pack_rows(p) for p in parts], axis=0)


def _unpack(packed, sizes):
    out, r = [], 0
    for n in sizes:
        rows = -(-n // LANES)
        rows8 = -(-rows // 8) * 8
        out.append(packed[r:r + rows8].reshape(-1)[:n].reshape(1, n))
        r += rows8
    return out


def _fold64(v):
    return v[:, :64] + v[:, 64:]


def kernel(x, mem, positions, attn_norm_g, w_in, swa_q_norm_g, swa_k_norm_g, swa_sinks, mla_cq_norm_g, mla_ckv_norm_g, w_uq, w_ukv, mla_qn_norm_g, mla_qr_norm_g, mla_kn_norm_g, mla_kr_norm_g, mem_norm_g, w_mem_kv, mem_q_norm_g, mem_k_norm_g, w_out, ffn_norm_g, w_gate, w_up, w_down, loss_target, m_attn_norm_g, m_w_in, m_swa_q_norm_g, m_swa_k_norm_g, m_swa_sinks, m_mla_cq_norm_g, m_mla_ckv_norm_g, m_w_uq, m_w_ukv, m_mla_qn_norm_g, m_mla_qr_norm_g, m_mla_kn_norm_g, m_mla_kr_norm_g, m_mem_norm_g, m_w_mem_kv, m_mem_q_norm_g, m_mem_k_norm_g, m_w_out, m_ffn_norm_g, m_w_gate, m_w_up, m_w_down, v_attn_norm_g, v_w_in, v_swa_q_norm_g, v_swa_k_norm_g, v_swa_sinks, v_mla_cq_norm_g, v_mla_ckv_norm_g, v_w_uq, v_w_ukv, v_mla_qn_norm_g, v_mla_qr_norm_g, v_mla_kn_norm_g, v_mla_kr_norm_g, v_mem_norm_g, v_w_mem_kv, v_mem_q_norm_g, v_mem_k_norm_g, v_w_out, v_ffn_norm_g, v_w_gate, v_w_up, v_w_down):
    args = dict(locals())
    x2, mem2, tgt = x[0], mem[0], loss_target[0]
    s, d = x2.shape
    n_in = w_in.shape[2]
    f = w_gate.shape[2]

    shards = [w_in[0].astype(WIRE), w_uq[0].astype(WIRE), w_ukv[0].astype(WIRE), w_mem_kv[0].astype(WIRE),
              w_out[0].astype(WIRE), jnp.stack([w_gate[0], w_up[0]]).astype(WIRE), w_down[0].astype(WIRE)]
    g_in, g_uq, g_ukv, g_mkv, g_out = _all_gather(shards[:5])
    w_gu, w_d = _all_gather_background(shards[5:], 1, "all_gather_ffn_weights")
    wi = g_in.transpose(1, 0, 2).reshape(d, N_DEV * n_in)
    wi = jnp.concatenate([wi[:, 0:1024], wi[:, 1280:1792], wi[:, 1792:2304], wi[:, 2368:2880],
                          wi[:, 1024:1152], wi[:, 1152:1280], wi[:, 2304:2368],
                          jnp.zeros((d, IN_PAD - 2880), wi.dtype)], axis=1)
    wq = g_uq.transpose(1, 0, 2).reshape(512, 768)
    wq = jnp.concatenate([wq[:, 192 * h: 192 * h + 128] for h in range(4)]
                         + [wq[:, 192 * h + 128: 192 * (h + 1)] for h in range(4)], axis=1)
    wkv = g_ukv.transpose(1, 0, 2).reshape(512, 1024)
    wmkv = g_mkv.reshape(-1, g_mkv.shape[-1])
    wo = g_out.reshape(-1, d)

    pos = positions[0].astype(jnp.float32)
    inv_freq = ROPE_THETA ** (-jnp.arange(0, MLA_ROPE, 2, dtype=jnp.float32) / MLA_ROPE)
    ang = pos[:, None] * inv_freq
    cos32, sin32 = jnp.cos(ang), jnp.sin(ang)
    cos_t = jnp.tile(cos32, (1, 4))
    sin_t = jnp.tile(jnp.concatenate([-sin32, sin32], axis=1), (1, 2))
    posc, posr = pos.reshape(s, 1), pos.reshape(1, s)
    two = lambda g: jnp.tile(g, (1, 2))
    gq2, gk2, gqr2, gkr2 = two(swa_q_norm_g), two(swa_k_norm_g), two(mla_qr_norm_g), two(mla_kr_norm_g)
    sinks1 = swa_sinks[0]

    proj, hn = _in_proj(x2, attn_norm_g, wi)
    qc, kc, vb, qb, kvb, cqn, ckvn = _mla_prep(proj, cos_t, sin_t, mla_cq_norm_g, mla_ckv_norm_g, wq, wkv,
                                                mla_qn_norm_g, gqr2, mla_kn_norm_g, gkr2)
    y_b, lse_b = _mla_fwd(qc, kc, vb)
    km, vmm, kvm, memn = _memkv_prep(mem2, mem_norm_g, wmkv, mem_k_norm_g)
    y_m, lse_m = _mem_fwd(proj, mem_q_norm_g, km, vmm)
    y_a, lse_a = _swa_fwd(proj, posc, posr, gq2, gk2, sinks1)
    h1, fn = _out_proj(y_a, y_b, y_m, x2, wo, ffn_norm_g)
    gu, act = _ffn_gu(fn, w_gu)
    dout, loss_tile = _ffn_down(act, w_d, h1, tgt)

    dgu, dw_d = _ffn_bwd_act(dout, w_d, gu)
    dw_gu = _ffn_dw_gu(fn, dgu)
    r_gu, r_d = _exchange_grads_background([dw_gu, dw_d], 2, "exchange_ffn_grads")
    d_h1, dg_ffn = _ffn_norm_bwd(_ffn_dfn(dgu, w_gu, dw_gu), dout, h1, ffn_norm_g)
    d_y = _mm(d_h1, wo, tb=True, out_dtype=jnp.float32, tm=FFN_TILE, tk=512, name="d_mix")
    dw_out = jnp.concatenate([
        _mm(y_a, d_h1, ta=True, out_dtype=WIRE, tm=1024, tk=512, name="dw_out_a"),
        _mm(y_b, d_h1, ta=True, out_dtype=WIRE, tm=1024, tk=512, name="dw_out_b"),
        _mm(y_m, d_h1, ta=True, out_dtype=WIRE, tm=1024, tk=512, name="dw_out_m")], axis=0)
    d_qm, dkm, dvmm, dg_mq = _mem_bwd(proj, mem_q_norm_g, km, vmm, d_y, y_m, lse_m)
    dw_mkv, dg_mem, dg_mk = _memkv_bwd(mem2, mem_norm_g, wmkv, mem_k_norm_g, kvm, memn, dkm, dvmm)
    r_mkv, r_out = _exchange_grads_background([dw_mkv.reshape(g_mkv.shape), dw_out.reshape(g_out.shape)], 3,
                                              "exchange_mix_grads")
    dqc, dkc, dvb = _mla_bwd(qc, kc, vb, d_y, y_b, lse_b, dw_mkv)
    (d_cq, d_ckv, d_kr, dw_uq, dw_ukv, dg_cq, dg_ckv, dg_qn, dg_qr, dg_kn, dg_kr) = _mla_prep_bwd(
        proj, cos_t, sin_t, mla_cq_norm_g, mla_ckv_norm_g, wq, wkv, mla_qn_norm_g, gqr2, mla_kn_norm_g, gkr2,
        qb, kvb, cqn, ckvn, dqc, dkc, dvb)
    d_qa, d_ka, d_va, dg_q, dg_k, d_sinks = _swa_bwd(proj, posc, posr, gq2, gk2, sinks1, d_y, y_a, lse_a, dw_out)
    d_proj = jnp.concatenate([d_qa, d_cq, d_ckv, d_qm, d_ka, d_va, d_kr], axis=1)
    dw_in = _mm(hn, d_proj, ta=True, out_dtype=WIRE, tm=512, tk=512, name="dw_in")

    gi = jnp.concatenate([dw_in[:, C_QA:C_QA + 1024], dw_in[:, C_KA:C_KA + 128], dw_in[:, C_VA:C_VA + 128],
                          dw_in[:, C_CQ:C_CQ + 512], dw_in[:, C_CKV:C_CKV + 512], dw_in[:, C_KR:C_KR + 64],
                          dw_in[:, C_QM:C_QM + 512]], axis=1)
    gi = gi.reshape(d, N_DEV, n_in).transpose(1, 0, 2)
    gq_ = jnp.concatenate(sum([[dw_uq[:, 128 * h: 128 * (h + 1)], dw_uq[:, 512 + 64 * h: 512 + 64 * (h + 1)]]
                               for h in range(4)], []), axis=1)
    gq_ = gq_.reshape(512, N_DEV, 96).transpose(1, 0, 2)
    gkv = dw_ukv.reshape(512, N_DEV, 128).transpose(1, 0, 2)
    r_in, r_uq, r_ukv = _exchange_grads_background([gi, gq_, gkv], 4, "exchange_in_grads")
    grad_x, dg_attn = _dx(d_proj, wi, x2, attn_norm_g, d_h1, gi)
    recv = [r_in, r_uq, r_ukv, r_mkv, r_out, r_gu, r_d]

    big = {}
    def adam(name, r, stacked=False):
        w, m, v = args[name][0], args["m_" + name][0], args["v_" + name][0]
        return _adam_big(r.reshape(N_DEV, -1, r.shape[-1]), w, m, v, "adam_" + name)
    for name, r in zip(["w_in", "w_uq", "w_ukv", "w_mem_kv", "w_out"], recv[:5]):
        big[name] = [o[None] for o in adam(name, r)]
    big["w_down"] = [o[None] for o in adam("w_down", recv[6])]
    r_gu = recv[5]
    big["w_gate"] = [o[None] for o in adam("w_gate", r_gu[:, 0])]
    big["w_up"] = [o[None] for o in adam("w_up", r_gu[:, 1])]

    small_g = {
        "attn_norm_g": dg_attn, "swa_q_norm_g": _fold64(dg_q), "swa_k_norm_g": _fold64(dg_k),
        "swa_sinks": d_sinks, "mla_cq_norm_g": dg_cq, "mla_ckv_norm_g": dg_ckv, "mla_qn_norm_g": dg_qn,
        "mla_qr_norm_g": _fold64(dg_qr), "mla_kn_norm_g": dg_kn, "mla_kr_norm_g": _fold64(dg_kr),
        "mem_norm_g": dg_mem, "mem_q_norm_g": dg_mq, "mem_k_norm_g": dg_mk, "ffn_norm_g": dg_ffn}
    sizes = [args[n].shape[-1] for n in _SMALL]
    pg = _pack([small_g[n] for n in _SMALL] + [loss_tile[0:1, 0:1]])
    zero = jnp.zeros((1, 1), jnp.float32)
    pw = _pack([args[n] for n in _SMALL] + [zero])
    pm = _pack([args["m_" + n] for n in _SMALL] + [zero])
    pv = _pack([args["v_" + n] for n in _SMALL] + [zero])
    sg, sd, sm, sv = _small_allreduce_adam(pg, pw, pm, pv)
    small = {n: vals for n, vals in zip(_SMALL, zip(*[_unpack(p, sizes) for p in (sg, sd, sm, sv)]))}
    loss = _unpack(sg, sizes + [1])[-1].reshape(())

    order = ["attn_norm_g", "w_in", "swa_q_norm_g", "swa_k_norm_g", "swa_sinks", "mla_cq_norm_g", "mla_ckv_norm_g",
             "w_uq", "w_ukv", "mla_qn_norm_g", "mla_qr_norm_g", "mla_kn_norm_g", "mla_kr_norm_g", "mem_norm_g",
             "w_mem_kv", "mem_q_norm_g", "mem_k_norm_g", "w_out", "ffn_norm_g", "w_gate", "w_up", "w_down"]
    res = {n: (big[n] if n in big else list(small[n])) for n in order}
    outs = [loss, grad_x[None]]
    for kind in range(4):
        outs += [res[n][kind] for n in order]
    return tuple(outs)
```

```python
import jax
import jax.numpy as jnp
from jax import lax
from jax.experimental import pallas as pl
from jax.experimental.pallas import tpu as pltpu
from jax.experimental.pallas import tpu_sc as plsc

MXU = jnp.bfloat16
WIRE = jnp.bfloat16
EPS = 1e-6
NEG_INF = -1e30
N_DEV = 8
LANES = 128
ROW_TILE = 256
FFN_TILE = 512
ATT_TILE = 1024
SWA_BLOCK = 128
VMEM_LIMIT = 56 * 1024 * 1024

SWA_Q_HEADS, SWA_KV_HEADS, SWA_DIM = 16, 2, 64
MLA_HEADS, MLA_NOPE, MLA_ROPE, MLA_V = 4, 128, 64, 128
MEM_HEADS, MEM_DIM = 4, 128
ROPE_THETA = 10000.0
ADAM_LR, ADAM_B1, ADAM_B2, ADAM_EPS, ADAM_WD, ADAM_STEP = 0.001, 0.9, 0.999, 1e-08, 0.01, 10

C_QA, C_CQ, C_CKV, C_QM, C_KA, C_VA, C_KR, IN_PAD = 0, 1024, 1536, 2048, 2560, 2688, 2816, 2944


def _pcall(body, *, name, out_shape, in_specs, out_specs, grid=(), scratch=(), sem=None, after=None):
    params = pltpu.CompilerParams(dimension_semantics=sem, vmem_limit_bytes=VMEM_LIMIT)
    if after is not None:
        n_in, inner = len(in_specs), body

        def body(*refs):
            inner(*refs[:n_in], *refs[n_in + 1:])

        in_specs = list(in_specs) + [pl.BlockSpec(memory_space=pl.ANY)]
    call = pl.pallas_call(body, name=name, grid=grid, in_specs=in_specs, out_specs=out_specs,
                          out_shape=out_shape, scratch_shapes=list(scratch), compiler_params=params)
    return call if after is None else (lambda *ops: call(*ops, after))


def _sds(shape, dtype):
    return jax.ShapeDtypeStruct(tuple(shape), dtype)


def _dot(a, b):
    return jnp.dot(a.astype(MXU), b.astype(MXU), preferred_element_type=jnp.float32)


def _dot_nt(a, b):
    return lax.dot_general(a.astype(MXU), b.astype(MXU), (((1,), (1,)), ((), ())),
                           preferred_element_type=jnp.float32)


def _dot_tn(a, b):
    return lax.dot_general(a.astype(MXU), b.astype(MXU), (((0,), (0,)), ((), ())),
                           preferred_element_type=jnp.float32)


def _lo_mask(shape):
    return (lax.broadcasted_iota(jnp.int32, shape, len(shape) - 1) % LANES) < 64


def _norm_fwd(x, g, half=False):
    x2 = x * x
    if half:
        lo = _lo_mask(x.shape)
        s_lo = jnp.sum(jnp.where(lo, x2, 0.0), -1, keepdims=True)
        s_hi = jnp.sum(jnp.where(lo, 0.0, x2), -1, keepdims=True)
        r = jnp.where(lo, lax.rsqrt(s_lo / 64.0 + EPS), lax.rsqrt(s_hi / 64.0 + EPS))
    else:
        r = lax.rsqrt(jnp.mean(x2, -1, keepdims=True) + EPS)
    xn = x * r
    return xn * g, xn, r


def _norm_bwd(xn, r, g, dy, half=False):
    t = dy * g
    tx = t * xn
    if half:
        lo = _lo_mask(xn.shape)
        m_lo = jnp.sum(jnp.where(lo, tx, 0.0), -1, keepdims=True) / 64.0
        m_hi = jnp.sum(jnp.where(lo, 0.0, tx), -1, keepdims=True) / 64.0
        m = jnp.where(lo, m_lo, m_hi)
    else:
        m = jnp.mean(tx, -1, keepdims=True)
    dx = r * (t - xn * m)
    dg = jnp.sum(dy * xn, 0, keepdims=True)
    return dx, dg


def _swap32(x):
    lane = lax.broadcasted_iota(jnp.int32, x.shape, 1)
    return jnp.where((lane % 64) < 32, pltpu.roll(x, 96, 1), pltpu.roll(x, 32, 1))


def _rope(x, cos, sin):
    return x * cos + _swap32(x) * sin


def _rope_bwd(d, cos, sin):
    return d * cos + _swap32(d * sin)


def _my_coords():
    return lax.axis_index("x"), lax.axis_index("y"), lax.axis_index("c")


def _dev_index(px, py, pc):
    return 4 * px + 2 * py + pc


_FLIPS = [(0, 0, 1), (0, 1, 0), (0, 1, 1), (1, 0, 0), (1, 0, 1), (1, 1, 0), (1, 1, 1)]


def _flip(coords, f):
    return tuple((1 - v) if b else v for v, b in zip(coords, f))


def _all_gather(shards):
    n = len(shards)

    def body(*refs):
        ins, outs = refs[:n], refs[n:2 * n]
        send_sems, recv_sems, local_sems = refs[2 * n:]
        x, y, c = _my_coords()
        me, sibling = (x, y, c), (x, y, 1 - c)
        chips = [(1 - x, y), (x, 1 - y), (1 - x, 1 - y)]

        def copy(w, k, block, to, src=None):
            dst = outs[w].at[_dev_index(*block)]
            return pltpu.make_async_remote_copy(
                src_ref=dst if src is None else src, dst_ref=dst,
                send_sem=send_sems.at[w, k], recv_sem=recv_sems.at[w, k],
                device_id=to, device_id_type=pl.DeviceIdType.MESH)

        sends, locals_ = [], []
        for w in range(n):
            mine = pltpu.make_async_copy(ins[w], outs[w].at[_dev_index(*me)], local_sems.at[w])
            mine.start()
            locals_.append(mine)
            first = [copy(w, 0, me, sibling, src=ins[w])]
            first += [copy(w, 1 + j, me, (*chip, c), src=ins[w]) for j, chip in enumerate(chips)]
            for cp in first:
                cp.start()
            sends += first
        for w in range(n):
            for j, chip in enumerate(chips):
                copy(w, 1 + j, (*chip, c), me).wait_recv()
                fwd = copy(w, 4 + j, (*chip, c), sibling)
                fwd.start()
                sends.append(fwd)
        for w in range(n):
            copy(w, 0, sibling, me).wait_recv()
            for j, chip in enumerate(chips):
                copy(w, 4 + j, (*chip, 1 - c), me).wait_recv()
        for cp in sends:
            cp.wait_send()
        for mine in locals_:
            mine.wait()

    any_spec = pl.BlockSpec(memory_space=pl.ANY)
    return _pcall(
        body, name="all_gather_weights",
        out_shape=[_sds((N_DEV,) + s.shape, s.dtype) for s in shards],
        in_specs=[any_spec] * n, out_specs=[any_spec] * n,
        scratch=[pltpu.SemaphoreType.DMA((n, 7)), pltpu.SemaphoreType.DMA((n, 7)),
                 pltpu.SemaphoreType.DMA((n,))])(*shards)


def _wire_cost(arrays):
    nbytes = sum(a.size * a.dtype.itemsize for a in arrays)
    return pl.CostEstimate(flops=0, transcendentals=0, bytes_accessed=40 * nbytes)


def _all_gather_background(shards, collective_id, name):
    n = len(shards)
    src_refs = [jax.new_ref(s, memory_space=pltpu.MemorySpace.HBM) for s in shards]
    out_refs = [jax.empty_ref(_sds((N_DEV,) + s.shape, s.dtype), memory_space=pltpu.MemorySpace.HBM) for s in shards]

    @pl.kernel(mesh=plsc.ScalarSubcoreMesh(axis_name="seq", num_cores=1), name=name,
               scratch_types=(pltpu.SemaphoreType.DMA((n, 7)), pltpu.SemaphoreType.DMA((n, 7)),
                              pltpu.SemaphoreType.DMA((n,))),
               compiler_params=pltpu.CompilerParams(collective_id=collective_id))
    def launch(send_sems, recv_sems, local_sems):
        x, y, c = _my_coords()
        me, sibling = (x, y, c), (x, y, 1 - c)
        chips = [(1 - x, y), (x, 1 - y), (1 - x, 1 - y)]
        barrier = pltpu.get_barrier_semaphore()
        for peer in [sibling] + [(*chip, c) for chip in chips]:
            pl.semaphore_signal(barrier, inc=1, device_id=peer, device_id_type=pl.DeviceIdType.MESH)
        pl.semaphore_wait(barrier, 4)

        def copy(w, k, block, to, src=None):
            dst = out_refs[w].at[_dev_index(*block)]
            return pltpu.make_async_remote_copy(
                src_ref=dst if src is None else src, dst_ref=dst,
                send_sem=send_sems.at[w, k], recv_sem=recv_sems.at[w, k],
                device_id=to, device_id_type=pl.DeviceIdType.MESH)

        sends, locals_ = [], []
        for w in range(n):
            mine = pltpu.make_async_copy(src_refs[w], out_refs[w].at[_dev_index(*me)], local_sems.at[w])
            mine.start()
            locals_.append(mine)
            first = [copy(w, 0, me, sibling, src=src_refs[w])]
            first += [copy(w, 1 + j, me, (*chip, c), src=src_refs[w]) for j, chip in enumerate(chips)]
            for cp in first:
                cp.start()
            sends += first
        for w in range(n):
            for j, chip in enumerate(chips):
                copy(w, 1 + j, (*chip, c), me).wait_recv()
                fwd = copy(w, 4 + j, (*chip, c), sibling)
                fwd.start()
                sends.append(fwd)
        for w in range(n):
            copy(w, 0, sibling, me).wait_recv()
            for j, chip in enumerate(chips):
                copy(w, 4 + j, (*chip, 1 - c), me).wait_recv()
        for cp in sends:
            cp.wait_send()
        for mine in locals_:
            mine.wait()

    launch()
    return [r[...] for r in out_refs]


def _exchange_grads(grads):
    n = len(grads)

    def body(*refs):
        ins, outs = refs[:n], refs[n:2 * n]
        send_sems, recv_sems, local_sems = refs[2 * n:]
        me = _my_coords()
        my_idx = _dev_index(*me)
        sends, locals_ = [], []
        for w in range(n):
            mine = pltpu.make_async_copy(ins[w].at[my_idx], outs[w].at[my_idx], local_sems.at[w])
            mine.start()
            locals_.append(mine)
            for k, f in enumerate(_FLIPS):
                peer = _flip(me, f)
                cp = pltpu.make_async_remote_copy(
                    src_ref=ins[w].at[_dev_index(*peer)], dst_ref=outs[w].at[my_idx],
                    send_sem=send_sems.at[w, k], recv_sem=recv_sems.at[w, k],
                    device_id=peer, device_id_type=pl.DeviceIdType.MESH)
                cp.start()
                sends.append(cp)
        for w in range(n):
            for k, f in enumerate(_FLIPS):
                peer = _flip(me, f)
                slot = outs[w].at[_dev_index(*peer)]
                pltpu.make_async_remote_copy(
                    src_ref=slot, dst_ref=slot,
                    send_sem=send_sems.at[w, k], recv_sem=recv_sems.at[w, k],
                    device_id=peer, device_id_type=pl.DeviceIdType.MESH).wait_recv()
        for cp in sends:
            cp.wait_send()
        for mine in locals_:
            mine.wait()

    any_spec = pl.BlockSpec(memory_space=pl.ANY)
    return _pcall(
        body, name="exchange_grads",
        out_shape=[_sds(g.shape, g.dtype) for g in grads],
        in_specs=[any_spec] * n, out_specs=[any_spec] * n,
        scratch=[pltpu.SemaphoreType.DMA((n, 7)), pltpu.SemaphoreType.DMA((n, 7)),
                 pltpu.SemaphoreType.DMA((n,))])(*grads)


def _exchange_grads_background(grads, collective_id, name):
    n = len(grads)
    src_refs = [jax.new_ref(g, memory_space=pltpu.MemorySpace.HBM) for g in grads]
    out_refs = [jax.empty_ref(_sds(g.shape, g.dtype), memory_space=pltpu.MemorySpace.HBM) for g in grads]

    @pl.kernel(mesh=plsc.ScalarSubcoreMesh(axis_name="seq", num_cores=1), name=name,
               scratch_types=(pltpu.SemaphoreType.DMA((n, 7)), pltpu.SemaphoreType.DMA((n, 7)),
                              pltpu.SemaphoreType.DMA((n,))),
               cost_estimate=_wire_cost(grads),
               compiler_params=pltpu.CompilerParams(collective_id=collective_id))
    def launch(send_sems, recv_sems, local_sems):
        me = _my_coords()
        my_idx = _dev_index(*me)
        peers = [_flip(me, f) for f in _FLIPS]
        barrier = pltpu.get_barrier_semaphore()
        for peer in peers:
            pl.semaphore_signal(barrier, inc=1, device_id=peer, device_id_type=pl.DeviceIdType.MESH)
        pl.semaphore_wait(barrier, len(peers))
        sends, locals_ = [], []
        for w in range(n):
            mine = pltpu.make_async_copy(src_refs[w].at[my_idx], out_refs[w].at[my_idx], local_sems.at[w])
            mine.start()
            locals_.append(mine)
            for k, peer in enumerate(peers):
                cp = pltpu.make_async_remote_copy(
                    src_ref=src_refs[w].at[_dev_index(*peer)], dst_ref=out_refs[w].at[my_idx],
                    send_sem=send_sems.at[w, k], recv_sem=recv_sems.at[w, k],
                    device_id=peer, device_id_type=pl.DeviceIdType.MESH)
                cp.start()
                sends.append(cp)
        for w in range(n):
            for k, peer in enumerate(peers):
                slot = out_refs[w].at[_dev_index(*peer)]
                pltpu.make_async_remote_copy(
                    src_ref=slot, dst_ref=slot, send_sem=send_sems.at[w, k], recv_sem=recv_sems.at[w, k],
                    device_id=peer, device_id_type=pl.DeviceIdType.MESH).wait_recv()
        for cp in sends:
            cp.wait_send()
        for mine in locals_:
            mine.wait()

    launch()
    return [r[...] for r in out_refs]


def _adam_math(w, g, m, v):
    m = ADAM_B1 * m + (1.0 - ADAM_B1) * g
    v = ADAM_B2 * v + (1.0 - ADAM_B2) * (g * g)
    m_hat = m / (1.0 - ADAM_B1 ** ADAM_STEP)
    v_hat = v / (1.0 - ADAM_B2 ** ADAM_STEP)
    delta = -ADAM_LR * (m_hat / (jnp.sqrt(v_hat) + ADAM_EPS) + ADAM_WD * w)
    return delta, m, v


def _small_allreduce_adam(pg, pw, pm, pv):
    rows = pg.shape[0]

    def body(pg_ref, pw_ref, pm_ref, pv_ref, g_ref, d_ref, m_ref, v_ref, gath, send_sems, recv_sems):
        me = _my_coords()
        my_idx = _dev_index(*me)
        gath[my_idx] = pg_ref[...]
        sends = []
        for k, f in enumerate(_FLIPS):
            peer = _flip(me, f)
            cp = pltpu.make_async_remote_copy(
                src_ref=pg_ref, dst_ref=gath.at[my_idx],
                send_sem=send_sems.at[k], recv_sem=recv_sems.at[k],
                device_id=peer, device_id_type=pl.DeviceIdType.MESH)
            cp.start()
            sends.append(cp)
        for k, f in enumerate(_FLIPS):
            peer = _flip(me, f)
            slot = gath.at[_dev_index(*peer)]
            pltpu.make_async_remote_copy(
                src_ref=slot, dst_ref=slot, send_sem=send_sems.at[k], recv_sem=recv_sems.at[k],
                device_id=peer, device_id_type=pl.DeviceIdType.MESH).wait_recv()
        for cp in sends:
            cp.wait_send()
        g = gath[0]
        for d in range(1, N_DEV):
            g = g + gath[d]
        delta, m, v = _adam_math(pw_ref[...], g, pm_ref[...], pv_ref[...])
        g_ref[...] = g
        d_ref[...] = delta
        m_ref[...] = m
        v_ref[...] = v

    vm = pl.BlockSpec(memory_space=pltpu.VMEM)
    return _pcall(
        body, name="small_allreduce_adam",
        out_shape=[_sds(pg.shape, jnp.float32)] * 4,
        in_specs=[vm] * 4, out_specs=[vm] * 4,
        scratch=[pltpu.VMEM((N_DEV, rows, LANES), jnp.float32),
                 pltpu.SemaphoreType.DMA((7,)), pltpu.SemaphoreType.DMA((7,))])(pg, pw, pm, pv)


def _adam_big(recv, w, m, v, name):
    _, rows, cols = recv.shape
    tr = rows
    while tr * cols > 256 * 1024 and tr % 2 == 0 and (tr // 2) % 16 == 0:
        tr //= 2

    def body(r_ref, w_ref, m_ref, v_ref, g_ref, d_ref, mo_ref, vo_ref):
        g = r_ref[0].astype(jnp.float32)
        for d in range(1, N_DEV):
            g = g + r_ref[d].astype(jnp.float32)
        delta, mn, vn = _adam_math(w_ref[...], g, m_ref[...], v_ref[...])
        g_ref[...] = g
        d_ref[...] = delta
        mo_ref[...] = mn
        vo_ref[...] = vn

    blk = pl.BlockSpec((tr, cols), lambda i: (i, 0))
    return _pcall(
        body, name=name, grid=(rows // tr,),
        out_shape=[_sds((rows, cols), jnp.float32)] * 4,
        in_specs=[pl.BlockSpec((N_DEV, tr, cols), lambda i: (0, i, 0)), blk, blk, blk],
        out_specs=[blk] * 4, sem=("parallel",))(recv, w, m, v)


def _mm(a, b, *, ta=False, tb=False, out_dtype, tm, tk, name):
    (kdim, mdim) = a.shape if ta else a.shape[::-1]
    ndim = b.shape[0] if tb else b.shape[1]
    tm, tk = min(tm, mdim), min(tk, kdim)
    nk = kdim // tk

    def body(a_ref, b_ref, o_ref, acc):
        k = pl.program_id(1)
        if ta:
            part = _dot_tn(a_ref[...], b_ref[...])
        elif tb:
            part = _dot_nt(a_ref[...], b_ref[...])
        else:
            part = _dot(a_ref[...], b_ref[...])

        @pl.when(k == 0)
        def _():
            acc[...] = part

        @pl.when(k > 0)
        def _():
            acc[...] += part

        @pl.when(k == nk - 1)
        def _():
            o_ref[...] = acc[...].astype(o_ref.dtype)

    a_spec = pl.BlockSpec((tk, tm), lambda i, k: (k, i)) if ta else pl.BlockSpec((tm, tk), lambda i, k: (i, k))
    b_spec = pl.BlockSpec((ndim, tk), lambda i, k: (0, k)) if tb else pl.BlockSpec((tk, ndim), lambda i, k: (k, 0))
    return _pcall(
        body, name=name, grid=(mdim // tm, nk), out_shape=_sds((mdim, ndim), out_dtype),
        in_specs=[a_spec, b_spec], out_specs=pl.BlockSpec((tm, ndim), lambda i, k: (i, 0)),
        scratch=[pltpu.VMEM((tm, ndim), jnp.float32)], sem=("parallel", "arbitrary"))(a, b)


def _ref_col_pieces(start, stop):
    ref_starts = [0, 1024, 1152, 1280, 1792, 2304, 2368, 2880]
    perm_starts = [C_QA, C_KA, C_VA, C_CQ, C_CKV, C_KR, C_QM]
    out = []
    for p in range(7):
        lo, hi = max(start, ref_starts[p]), min(stop, ref_starts[p + 1])
        if lo < hi:
            out.append((lo - start, perm_starts[p] + lo - ref_starts[p], hi - lo))
    return out


def _dw_in(hn, d_proj, n_shard):
    s, d = hn.shape
    n = d_proj.shape[1]
    tm, tk = min(512, d), min(512, s)
    nk = s // tk

    def body(a_ref, b_ref, o_ref, acc):
        k = pl.program_id(1)
        part = _dot_tn(a_ref[...], b_ref[...])

        @pl.when(k == 0)
        def _():
            acc[...] = part

        @pl.when(k > 0)
        def _():
            acc[...] += part

        @pl.when(k == nk - 1)
        def _():
            for j in range(N_DEV):
                for dst, src, width in _ref_col_pieces(j * n_shard, (j + 1) * n_shard):
                    o_ref[j, :, dst:dst + width] = acc[:, src:src + width].astype(o_ref.dtype)

    return _pcall(
        body, name="dw_in", grid=(d // tm, nk), out_shape=_sds((N_DEV, d, n_shard), WIRE),
        in_specs=[pl.BlockSpec((tk, tm), lambda i, k: (k, i)), pl.BlockSpec((tk, n), lambda i, k: (k, 0))],
        out_specs=pl.BlockSpec((N_DEV, tm, n_shard), lambda i, k: (0, i, 0)),
        scratch=[pltpu.VMEM((tm, n), jnp.float32)], sem=("parallel", "arbitrary"))(hn, d_proj)


def _in_proj(x, g, w):
    s, d = x.shape
    n = w.shape[1]
    tm = min(ROW_TILE, s)

    def body(x_ref, g_ref, w_ref, p_ref, hn_ref):
        hn, _, _ = _norm_fwd(x_ref[...], g_ref[...])
        hn_ref[...] = hn.astype(hn_ref.dtype)
        p_ref[...] = _dot(hn, w_ref[...])

    return _pcall(
        body, name="in_proj", grid=(s // tm,),
        out_shape=[_sds((s, n), jnp.float32), _sds((s, d), MXU)],
        in_specs=[pl.BlockSpec((tm, d), lambda i: (i, 0)), pl.BlockSpec((1, d), lambda i: (0, 0)),
                  pl.BlockSpec((d, n), lambda i: (0, 0))],
        out_specs=[pl.BlockSpec((tm, n), lambda i: (i, 0)), pl.BlockSpec((tm, d), lambda i: (i, 0))],
        sem=("parallel",))(x, g, w)


def _mla_prep(proj, cos, sin, g_cq, g_ckv, w_uq, w_ukv, g_qn, g_qr, g_kn, g_kr):
    s = proj.shape[0]
    tm = min(ROW_TILE, s)
    nh = MLA_HEADS

    def body(cq_ref, ckv_ref, kr_ref, cos_ref, sin_ref, gcq_ref, gckv_ref, wuq_ref, wukv_ref,
             gqn_ref, gqr_ref, gkn_ref, gkr_ref,
             qc_ref, kc_ref, v_ref, qb_ref, kvb_ref, cqn_ref, ckvn_ref):
        cos_t, sin_t = cos_ref[...], sin_ref[...]
        lo = _lo_mask((tm, LANES))
        cqn, _, _ = _norm_fwd(cq_ref[...], gcq_ref[...])
        cqn_ref[...] = cqn.astype(cqn_ref.dtype)
        qb = _dot(cqn, wuq_ref[...])
        qb_ref[...] = qb
        ckvn, _, _ = _norm_fwd(ckv_ref[...], gckv_ref[...])
        ckvn_ref[...] = ckvn.astype(ckvn_ref.dtype)
        kvb = _dot(ckvn, wukv_ref[...])
        kvb_ref[...] = kvb
        kr, _, _ = _norm_fwd(kr_ref[...], gkr_ref[...], half=True)
        kr = _rope(kr, cos_t, sin_t)
        kr2 = jnp.where(lo, kr, pltpu.roll(kr, 64, 1))
        ropes = []
        for j in range(nh // 2):
            xr = qb[:, nh * MLA_NOPE + LANES * j: nh * MLA_NOPE + LANES * (j + 1)]
            qr, _, _ = _norm_fwd(xr, gqr_ref[...], half=True)
            ropes.append(_rope(qr, cos_t, sin_t))
        for h in range(nh):
            qn, _, _ = _norm_fwd(qb[:, MLA_NOPE * h: MLA_NOPE * (h + 1)], gqn_ref[...])
            mask = lo if h % 2 == 0 else jnp.logical_not(lo)
            qr = jnp.where(mask, ropes[h // 2], 0.0)
            qc_ref[h] = jnp.concatenate([qn, qr], axis=1).astype(qc_ref.dtype)
            kn, _, _ = _norm_fwd(kvb[:, 256 * h: 256 * h + MLA_NOPE], gkn_ref[...])
            kc_ref[h] = jnp.concatenate([kn, kr2], axis=1).astype(kc_ref.dtype)
            v_ref[h] = kvb[:, 256 * h + MLA_NOPE: 256 * (h + 1)].astype(v_ref.dtype)

    def col(width, start):
        return pl.BlockSpec((tm, width), lambda i: (i, start // width))

    def full(shape):
        return pl.BlockSpec(shape, lambda i: (0,) * len(shape))

    def row(width):
        return pl.BlockSpec((tm, width), lambda i: (i, 0))

    def heads(width):
        return pl.BlockSpec((nh, tm, width), lambda i: (0, i, 0))

    return _pcall(
        body, name="mla_prep", grid=(s // tm,),
        out_shape=[_sds((nh, s, 256), MXU), _sds((nh, s, 256), MXU), _sds((nh, s, MLA_V), MXU),
                   _sds((s, 768), jnp.float32), _sds((s, 1024), jnp.float32),
                   _sds((s, 512), MXU), _sds((s, 512), MXU)],
        in_specs=[col(512, C_CQ), col(512, C_CKV), col(LANES, C_KR), row(LANES), row(LANES),
                  full((1, 512)), full((1, 512)), full((512, 768)), full((512, 1024)),
                  full((1, LANES)), full((1, LANES)), full((1, LANES)), full((1, LANES))],
        out_specs=[heads(256), heads(256), heads(MLA_V), row(768), row(1024), row(512), row(512)],
        sem=("parallel",))(proj, proj, proj, cos, sin, g_cq, g_ckv, w_uq, w_ukv, g_qn, g_qr, g_kn, g_kr)


def _mla_fwd(qc, kc, v):
    nh, s, _ = qc.shape
    t = min(ATT_TILE, s)
    nb = s // t
    scale = (MLA_NOPE + MLA_ROPE) ** -0.5

    def body(q_ref, k_ref, v_ref, y_ref, lse_ref, m_sc, l_sc, acc):
        qi, ki = pl.program_id(1), pl.program_id(2)

        @pl.when(ki == 0)
        def _():
            m_sc[...] = jnp.full_like(m_sc, NEG_INF)
            l_sc[...] = jnp.zeros_like(l_sc)
            acc[...] = jnp.zeros_like(acc)

        @pl.when(ki <= qi)
        def _():
            sc = _dot_nt(q_ref[0], k_ref[0]) * scale
            r_i = lax.broadcasted_iota(jnp.int32, sc.shape, 0) + qi * t
            c_i = lax.broadcasted_iota(jnp.int32, sc.shape, 1) + ki * t
            sc = jnp.where(c_i <= r_i, sc, NEG_INF)
            m_new = jnp.maximum(m_sc[...], jnp.max(sc, -1, keepdims=True))
            alpha = jnp.exp(m_sc[...] - m_new)
            p = jnp.exp(sc - m_new)
            l_sc[...] = alpha * l_sc[...] + jnp.sum(p, -1, keepdims=True)
            acc[...] = alpha * acc[...] + _dot(p, v_ref[0])
            m_sc[...] = m_new

        @pl.when(ki == qi)
        def _():
            y_ref[...] = acc[...] / l_sc[...]
            lse_ref[0] = m_sc[...] + jnp.log(l_sc[...])

    return _pcall(
        body, name="mla_fwd", grid=(nh, nb, nb),
        out_shape=[_sds((s, nh * MLA_V), jnp.float32), _sds((nh, s, 1), jnp.float32)],
        in_specs=[pl.BlockSpec((1, t, 256), lambda h, i, k: (h, i, 0)),
                  pl.BlockSpec((1, t, 256), lambda h, i, k: (h, jnp.minimum(k, i), 0)),
                  pl.BlockSpec((1, t, MLA_V), lambda h, i, k: (h, jnp.minimum(k, i), 0))],
        out_specs=[pl.BlockSpec((t, MLA_V), lambda h, i, k: (i, h)),
                   pl.BlockSpec((1, t, 1), lambda h, i, k: (h, i, 0))],
        scratch=[pltpu.VMEM((t, 1), jnp.float32), pltpu.VMEM((t, 1), jnp.float32),
                 pltpu.VMEM((t, MLA_V), jnp.float32)],
        sem=("parallel", "parallel", "arbitrary"))(qc, kc, v)


def _memkv_prep(mem, g_mem, w_mkv, g_mk):
    ml, d = mem.shape
    hw = MEM_HEADS * MEM_DIM

    def body(mem_ref, g_ref, w_ref, gk_ref, k_ref, v_ref, kv_ref, mn_ref):
        mn, _, _ = _norm_fwd(mem_ref[...], g_ref[...])
        mn_ref[...] = mn.astype(mn_ref.dtype)
        kv = _dot(mn, w_ref[...])
        kv_ref[...] = kv
        for h in range(MEM_HEADS):
            kn, _, _ = _norm_fwd(kv[:, MEM_DIM * h: MEM_DIM * (h + 1)], gk_ref[...])
            k_ref[:, MEM_DIM * h: MEM_DIM * (h + 1)] = kn.astype(k_ref.dtype)
        v_ref[...] = kv[:, hw:].astype(v_ref.dtype)

    vm = pl.BlockSpec(memory_space=pltpu.VMEM)
    return _pcall(
        body, name="memkv_prep",
        out_shape=[_sds((ml, hw), MXU), _sds((ml, hw), MXU), _sds((ml, 2 * hw), jnp.float32), _sds((ml, d), MXU)],
        in_specs=[vm] * 4, out_specs=[vm] * 4)(mem, g_mem, w_mkv, g_mk)


def _mem_fwd(proj, g_mq, km, vmm):
    s = proj.shape[0]
    ml, hw = km.shape
    tm = min(FFN_TILE, s)
    scale = MEM_DIM ** -0.5

    def body(q_ref, g_ref, k_ref, v_ref, y_ref, lse_ref):
        col = lax.broadcasted_iota(jnp.int32, (tm, MEM_HEADS), 1)
        lse_t = jnp.zeros((tm, MEM_HEADS), jnp.float32)
        for h in range(MEM_HEADS):
            sl = slice(MEM_DIM * h, MEM_DIM * (h + 1))
            qn, _, _ = _norm_fwd(q_ref[:, sl], g_ref[...])
            sc = _dot_nt(qn, k_ref[:, sl]) * scale
            m = jnp.max(sc, -1, keepdims=True)
            p = jnp.exp(sc - m)
            l = jnp.sum(p, -1, keepdims=True)
            y_ref[:, sl] = _dot(p, v_ref[:, sl]) / l
            lse_t = jnp.where(col == h, m + jnp.log(l), lse_t)
        lse_ref[...] = lse_t

    return _pcall(
        body, name="mem_fwd", grid=(s // tm,),
        out_shape=[_sds((s, hw), jnp.float32), _sds((s, MEM_HEADS), jnp.float32)],
        in_specs=[pl.BlockSpec((tm, hw), lambda i: (i, C_QM // hw)), pl.BlockSpec((1, MEM_DIM), lambda i: (0, 0)),
                  pl.BlockSpec((ml, hw), lambda i: (0, 0)), pl.BlockSpec((ml, hw), lambda i: (0, 0))],
        out_specs=[pl.BlockSpec((tm, hw), lambda i: (i, 0)), pl.BlockSpec((tm, MEM_HEADS), lambda i: (i, 0))],
        sem=("parallel",))(proj, g_mq, km, vmm)


def _alibi_slope(h):
    return float(2.0 ** (-8.0 * (h + 1) / SWA_Q_HEADS))


def _swa_common(n, kp, kc, vp, vc, pq, pkp, pkc, gk):
    b = SWA_BLOCK
    k_raw = jnp.concatenate([kp, kc], axis=0)
    kn, kxn, kr = _norm_fwd(k_raw, gk, half=True)
    v = jnp.concatenate([vp, vc], axis=0)
    dist = jnp.abs(pq - jnp.concatenate([pkp, pkc], axis=1))
    r_i = lax.broadcasted_iota(jnp.int32, (b, 2 * b), 0)
    c_i = lax.broadcasted_iota(jnp.int32, (b, 2 * b), 1)
    valid = (c_i > r_i) & (c_i <= r_i + b) & (c_i >= jnp.where(n > 0, 0, b))
    return kn, v, dist, valid


def _swa_specs(s):
    b = SWA_BLOCK
    prev = lambda n: jnp.maximum(n - 1, 0)
    return [
        pl.BlockSpec((b, 1024), lambda n: (n, C_QA // 1024)),
        pl.BlockSpec((b, LANES), lambda n: (prev(n), C_KA // LANES)),
        pl.BlockSpec((b, LANES), lambda n: (n, C_KA // LANES)),
        pl.BlockSpec((b, LANES), lambda n: (prev(n), C_VA // LANES)),
        pl.BlockSpec((b, LANES), lambda n: (n, C_VA // LANES)),
        pl.BlockSpec((b, 1), lambda n: (n, 0)),
        pl.BlockSpec((1, b), lambda n: (0, prev(n))),
        pl.BlockSpec((1, b), lambda n: (0, n)),
        pl.BlockSpec((1, LANES), lambda n: (0, 0)),
        pl.BlockSpec((1, LANES), lambda n: (0, 0)),
        pl.BlockSpec(memory_space=pltpu.SMEM),
    ]


def _swa_fwd(proj, posc, posr, gq, gk, sinks):
    s = proj.shape[0]
    b = SWA_BLOCK
    scale = SWA_DIM ** -0.5

    def body(q_ref, kp_ref, kc_ref, vp_ref, vc_ref, pq_ref, pkp_ref, pkc_ref, gq_ref, gk_ref, sink_ref,
             y_ref, lse_ref):
        n = pl.program_id(0)
        kn, v, dist, valid = _swa_common(n, kp_ref[...], kc_ref[...], vp_ref[...], vc_ref[...],
                                         pq_ref[...], pkp_ref[...], pkc_ref[...], gk_ref[...])
        lo = _lo_mask((b, LANES))
        col = lax.broadcasted_iota(jnp.int32, (b, SWA_Q_HEADS), 1)
        lse_t = jnp.zeros((b, SWA_Q_HEADS), jnp.float32)
        for j in range(SWA_Q_HEADS // 2):
            hk = (2 * j) // (SWA_Q_HEADS // SWA_KV_HEADS)
            kvmask = lo if hk == 0 else jnp.logical_not(lo)
            qn, _, _ = _norm_fwd(q_ref[:, LANES * j: LANES * (j + 1)], gq_ref[...], half=True)
            qsw = pltpu.roll(qn, 64, 1)
            outs = []
            for e in range(2):
                h = 2 * j + e
                qm = jnp.where(kvmask, qn if e == hk else qsw, 0.0)
                sc = _dot_nt(qm, kn) * scale - _alibi_slope(h) * dist
                sc = jnp.where(valid, sc, NEG_INF)
                sk = sink_ref[h]
                m = jnp.maximum(jnp.max(sc, -1, keepdims=True), sk)
                p = jnp.exp(sc - m)
                l = jnp.sum(p, -1, keepdims=True) + jnp.exp(sk - m)
                o = _dot(p, v) / l
                outs.append(o if e == hk else pltpu.roll(o, 64, 1))
                lse_t = jnp.where(col == h, m + jnp.log(l), lse_t)
            y_ref[:, LANES * j: LANES * (j + 1)] = jnp.where(lo, outs[0], outs[1])
        lse_ref[...] = lse_t

    return _pcall(
        body, name="swa_fwd", grid=(s // b,),
        out_shape=[_sds((s, 1024), jnp.float32), _sds((s, SWA_Q_HEADS), jnp.float32)],
        in_specs=_swa_specs(s),
        out_specs=[pl.BlockSpec((b, 1024), lambda n: (n, 0)), pl.BlockSpec((b, SWA_Q_HEADS), lambda n: (n, 0))],
        sem=("parallel",))(proj, proj, proj, proj, proj, posc, posr, posr, gq, gk, sinks)


def _out_proj(y_a, y_b, y_m, x, w_out, g_ffn):
    s, d = x.shape
    tm = min(ROW_TILE, s)

    def body(ya_ref, yb_ref, ym_ref, x_ref, w_ref, g_ref, h1_ref, fn_ref):
        y = jnp.concatenate([ya_ref[...].astype(MXU), yb_ref[...].astype(MXU), ym_ref[...].astype(MXU)], axis=1)
        h1 = x_ref[...] + _dot(y, w_ref[...])
        h1_ref[...] = h1
        fn, _, _ = _norm_fwd(h1, g_ref[...])
        fn_ref[...] = fn.astype(fn_ref.dtype)

    def row(width):
        return pl.BlockSpec((tm, width), lambda i: (i, 0))

    return _pcall(
        body, name="out_proj", grid=(s // tm,),
        out_shape=[_sds((s, d), jnp.float32), _sds((s, d), MXU)],
        in_specs=[row(1024), row(512), row(512), row(d), pl.BlockSpec(w_out.shape, lambda i: (0, 0)),
                  pl.BlockSpec((1, d), lambda i: (0, 0))],
        out_specs=[row(d), row(d)], sem=("parallel",))(y_a, y_b, y_m, x, w_out, g_ffn)


def _ffn_gu(fn, w_gu):
    s, d = fn.shape
    f = w_gu.shape[-1]
    tm = min(FFN_TILE, s)

    def body(fn_ref, w_ref, gu_ref, act_ref):
        x = fn_ref[...]
        g = _dot(x, w_ref[0, 0])
        u = _dot(x, w_ref[0, 1])
        gu_ref[0, 0] = g
        gu_ref[0, 1] = u
        act_ref[0] = (g * jax.nn.sigmoid(g) * u).astype(act_ref.dtype)

    return _pcall(
        body, name="ffn_gate_up", grid=(N_DEV, s // tm),
        out_shape=[_sds((N_DEV, 2, s, f), jnp.float32), _sds((N_DEV, s, f), MXU)],
        in_specs=[pl.BlockSpec((tm, d), lambda j, i: (i, 0)),
                  pl.BlockSpec((1, 2, d, f), lambda j, i: (j, 0, 0, 0))],
        out_specs=[pl.BlockSpec((1, 2, tm, f), lambda j, i: (j, 0, i, 0)),
                   pl.BlockSpec((1, tm, f), lambda j, i: (j, i, 0))],
        sem=("parallel", "parallel"))(fn, w_gu)


def _ffn_down(act, w_d, h1, target):
    _, s, f = act.shape
    d = h1.shape[1]
    tm = min(FFN_TILE, s)

    def body(a_ref, w_ref, h1_ref, t_ref, dout_ref, loss_ref, acc):
        i, j = pl.program_id(0), pl.program_id(1)
        part = _dot(a_ref[0], w_ref[0])

        @pl.when(j == 0)
        def _():
            acc[...] = h1_ref[...] + part

        @pl.when(j > 0)
        def _():
            acc[...] += part

        @pl.when((i == 0) & (j == 0))
        def _():
            loss_ref[...] = jnp.zeros_like(loss_ref)

        @pl.when(j == N_DEV - 1)
        def _():
            diff = acc[...] - t_ref[...]
            dout_ref[...] = diff / d
            loss_ref[...] += 0.5 * jnp.sum(jnp.sum(diff * diff, -1, keepdims=True) / d)

    row = pl.BlockSpec((tm, d), lambda i, j: (i, 0))
    return _pcall(
        body, name="ffn_down", grid=(s // tm, N_DEV),
        out_shape=[_sds((s, d), jnp.float32), _sds((8, LANES), jnp.float32)],
        in_specs=[pl.BlockSpec((1, tm, f), lambda i, j: (j, i, 0)), pl.BlockSpec((1, f, d), lambda i, j: (j, 0, 0)),
                  row, row],
        out_specs=[row, pl.BlockSpec((8, LANES), lambda i, j: (0, 0))],
        scratch=[pltpu.VMEM((tm, d), jnp.float32)], sem=("arbitrary", "arbitrary"))(act, w_d, h1, target)


def _ffn_bwd_act(dout, w_d, gu):
    s, d = dout.shape
    f = w_d.shape[1]
    tm = min(FFN_TILE, s)
    ni = s // tm

    def body(do_ref, w_ref, gu_ref, dgu_ref, dw_ref, acc):
        i = pl.program_id(1)
        do = do_ref[...].astype(MXU)
        d_act = _dot_nt(do, w_ref[0])
        g, u = gu_ref[0, 0], gu_ref[0, 1]
        sig = jax.nn.sigmoid(g)
        silu = g * sig
        dgu_ref[0, 0] = (d_act * u * (sig * (1.0 + g * (1.0 - sig)))).astype(dgu_ref.dtype)
        dgu_ref[0, 1] = (d_act * silu).astype(dgu_ref.dtype)
        part = _dot_tn(silu * u, do)

        @pl.when(i == 0)
        def _():
            acc[...] = part

        @pl.when(i > 0)
        def _():
            acc[...] += part

        @pl.when(i == ni - 1)
        def _():
            dw_ref[0] = acc[...].astype(dw_ref.dtype)

    return _pcall(
        body, name="ffn_bwd_act", grid=(N_DEV, ni),
        out_shape=[_sds((N_DEV, 2, s, f), MXU), _sds((N_DEV, f, d), WIRE)],
        in_specs=[pl.BlockSpec((tm, d), lambda j, i: (i, 0)), pl.BlockSpec((1, f, d), lambda j, i: (j, 0, 0)),
                  pl.BlockSpec((1, 2, tm, f), lambda j, i: (j, 0, i, 0))],
        out_specs=[pl.BlockSpec((1, 2, tm, f), lambda j, i: (j, 0, i, 0)),
                   pl.BlockSpec((1, f, d), lambda j, i: (j, 0, 0))],
        scratch=[pltpu.VMEM((f, d), jnp.float32)], sem=("parallel", "arbitrary"))(dout, w_d, gu)


def _ffn_dw_gu(fn, dgu):
    s, d = fn.shape
    f = dgu.shape[-1]
    tk = min(FFN_TILE, s)
    nk = s // tk

    def body(fn_ref, dgu_ref, dw_ref, acc):
        k = pl.program_id(1)
        x = fn_ref[...]
        pg = _dot_tn(x, dgu_ref[0, 0])
        pu = _dot_tn(x, dgu_ref[0, 1])

        @pl.when(k == 0)
        def _():
            acc[0] = pg
            acc[1] = pu

        @pl.when(k > 0)
        def _():
            acc[0] += pg
            acc[1] += pu

        @pl.when(k == nk - 1)
        def _():
            dw_ref[0] = acc[...].astype(dw_ref.dtype)

    return _pcall(
        body, name="ffn_dw_gate_up", grid=(N_DEV, nk),
        out_shape=_sds((N_DEV, 2, d, f), WIRE),
        in_specs=[pl.BlockSpec((tk, d), lambda j, k: (k, 0)), pl.BlockSpec((1, 2, tk, f), lambda j, k: (j, 0, k, 0))],
        out_specs=pl.BlockSpec((1, 2, d, f), lambda j, k: (j, 0, 0, 0)),
        scratch=[pltpu.VMEM((2, d, f), jnp.float32)], sem=("parallel", "arbitrary"))(fn, dgu)


def _ffn_dfn(dgu, w_gu, after):
    _, _, s, f = dgu.shape
    d = w_gu.shape[2]
    tm = min(FFN_TILE, s)

    def body(dgu_ref, w_ref, dfn_ref):
        j = pl.program_id(1)
        part = _dot_nt(dgu_ref[0, 0], w_ref[0, 0]) + _dot_nt(dgu_ref[0, 1], w_ref[0, 1])

        @pl.when(j == 0)
        def _():
            dfn_ref[...] = part

        @pl.when(j > 0)
        def _():
            dfn_ref[...] += part

    return _pcall(
        body, name="ffn_dfn", grid=(s // tm, N_DEV),
        out_shape=_sds((s, d), jnp.float32),
        in_specs=[pl.BlockSpec((1, 2, tm, f), lambda i, j: (j, 0, i, 0)),
                  pl.BlockSpec((1, 2, d, f), lambda i, j: (j, 0, 0, 0))],
        out_specs=pl.BlockSpec((tm, d), lambda i, j: (i, 0)),
        sem=("parallel", "arbitrary"), after=after)(dgu, w_gu)


def _ffn_norm_bwd(d_fn, dout, h1, g_ffn):
    s, d = h1.shape
    tm = min(ROW_TILE, s)

    def body(dfn_ref, do_ref, h1_ref, g_ref, dh1_ref, dg_ref):
        i = pl.program_id(0)

        @pl.when(i == 0)
        def _():
            dg_ref[...] = jnp.zeros_like(dg_ref)

        _, xn, r = _norm_fwd(h1_ref[...], g_ref[...])
        dx, dg = _norm_bwd(xn, r, g_ref[...], dfn_ref[...])
        dh1_ref[...] = do_ref[...] + dx
        dg_ref[...] += dg

    row = pl.BlockSpec((tm, d), lambda i: (i, 0))
    vec = pl.BlockSpec((1, d), lambda i: (0, 0))
    return _pcall(
        body, name="ffn_norm_bwd", grid=(s // tm,),
        out_shape=[_sds((s, d), jnp.float32), _sds((1, d), jnp.float32)],
        in_specs=[row, row, row, vec], out_specs=[row, vec], sem=("arbitrary",))(d_fn, dout, h1, g_ffn)


def _mem_bwd(proj, g_mq, km, vmm, d_y, y_m, lse):
    s = proj.shape[0]
    ml, hw = km.shape
    tm = min(FFN_TILE, s)
    scale = MEM_DIM ** -0.5

    def body(q_ref, g_ref, k_ref, v_ref, do_ref, y_ref, lse_ref, dq_ref, dk_ref, dv_ref, dg_ref):
        i = pl.program_id(0)

        @pl.when(i == 0)
        def _():
            dk_ref[...] = jnp.zeros_like(dk_ref)
            dv_ref[...] = jnp.zeros_like(dv_ref)
            dg_ref[...] = jnp.zeros_like(dg_ref)

        col = lax.broadcasted_iota(jnp.int32, (tm, MEM_HEADS), 1)
        lse_t = lse_ref[...]
        for h in range(MEM_HEADS):
            sl = slice(MEM_DIM * h, MEM_DIM * (h + 1))
            qn, xn, r = _norm_fwd(q_ref[:, sl], g_ref[...])
            lse_h = jnp.sum(jnp.where(col == h, lse_t, 0.0), -1, keepdims=True)
            p = jnp.exp(_dot_nt(qn, k_ref[:, sl]) * scale - lse_h)
            do = do_ref[:, sl]
            dd = jnp.sum(do * y_ref[:, sl], -1, keepdims=True)
            dp = _dot_nt(do, v_ref[:, sl])
            ds = (p * (dp - dd)).astype(MXU)
            dv_ref[:, sl] += _dot_tn(p, do)
            dk_ref[:, sl] += _dot_tn(ds, qn) * scale
            dx, dg = _norm_bwd(xn, r, g_ref[...], _dot(ds, k_ref[:, sl]) * scale)
            dq_ref[:, sl] = dx.astype(dq_ref.dtype)
            dg_ref[...] += dg

    full = pl.BlockSpec((ml, hw), lambda i: (0, 0))
    return _pcall(
        body, name="mem_bwd", grid=(s // tm,),
        out_shape=[_sds((s, hw), MXU), _sds((ml, hw), jnp.float32), _sds((ml, hw), jnp.float32),
                   _sds((1, MEM_DIM), jnp.float32)],
        in_specs=[pl.BlockSpec((tm, hw), lambda i: (i, C_QM // hw)), pl.BlockSpec((1, MEM_DIM), lambda i: (0, 0)),
                  full, full, pl.BlockSpec((tm, hw), lambda i: (i, 3)), pl.BlockSpec((tm, hw), lambda i: (i, 0)),
                  pl.BlockSpec((tm, MEM_HEADS), lambda i: (i, 0))],
        out_specs=[pl.BlockSpec((tm, hw), lambda i: (i, 0)), full, full,
                   pl.BlockSpec((1, MEM_DIM), lambda i: (0, 0))],
        sem=("arbitrary",))(proj, g_mq, km, vmm, d_y, y_m, lse)


def _memkv_bwd(mem, g_mem, w_mkv, g_mk, kv, memn, dk, dv):
    ml, d = mem.shape
    hw = MEM_HEADS * MEM_DIM

    def body(mem_ref, g_ref, w_ref, gk_ref, kv_ref, mn_ref, dk_ref, dv_ref, dw_ref, dgm_ref, dgk_ref):
        parts = []
        dgk = jnp.zeros((1, MEM_DIM), jnp.float32)
        for h in range(MEM_HEADS):
            sl = slice(MEM_DIM * h, MEM_DIM * (h + 1))
            _, xn, r = _norm_fwd(kv_ref[:, sl], gk_ref[...])
            dx, dg = _norm_bwd(xn, r, gk_ref[...], dk_ref[:, sl])
            parts.append(dx)
            dgk = dgk + dg
        dkv = jnp.concatenate(parts + [dv_ref[...]], axis=1).astype(MXU)
        dgk_ref[...] = dgk
        dw_ref[...] = _dot_tn(mn_ref[...], dkv).astype(dw_ref.dtype)
        d_mn = _dot_nt(dkv, w_ref[...])
        _, xn, _ = _norm_fwd(mem_ref[...], g_ref[...])
        dgm_ref[...] = jnp.sum(d_mn * xn, 0, keepdims=True)

    vm = pl.BlockSpec(memory_space=pltpu.VMEM)
    return _pcall(
        body, name="memkv_bwd",
        out_shape=[_sds((d, 2 * hw), WIRE), _sds((1, d), jnp.float32), _sds((1, MEM_DIM), jnp.float32)],
        in_specs=[vm] * 8, out_specs=[vm] * 3)(mem, g_mem, w_mkv, g_mk, kv, memn, dk, dv)


def _mla_bwd(qc, kc, v, d_y, y_b, lse, after):
    nh, s, _ = qc.shape
    t = min(ATT_TILE, s)
    nb = s // t
    scale = (MLA_NOPE + MLA_ROPE) ** -0.5

    def body(q_ref, k_ref, v_ref, do_ref, y_ref, lse_ref, dq_ref, dk_ref, dv_ref, dk_acc, dv_acc):
        kj, qi = pl.program_id(1), pl.program_id(2)

        @pl.when((kj == 0) & (qi == 0))
        def _():
            dq_ref[...] = jnp.zeros_like(dq_ref)

        @pl.when(qi == kj)
        def _():
            dk_acc[...] = jnp.zeros_like(dk_acc)
            dv_acc[...] = jnp.zeros_like(dv_acc)

        @pl.when(qi >= kj)
        def _():
            q, k = q_ref[0], k_ref[0]
            sc = _dot_nt(q, k) * scale
            r_i = lax.broadcasted_iota(jnp.int32, sc.shape, 0) + qi * t
            c_i = lax.broadcasted_iota(jnp.int32, sc.shape, 1) + kj * t
            p = jnp.exp(jnp.where(c_i <= r_i, sc, NEG_INF) - lse_ref[0])
            do = do_ref[...]
            dd = jnp.sum(do * y_ref[...], -1, keepdims=True)
            dp = _dot_nt(do, v_ref[0])
            ds = (p * (dp - dd) * scale).astype(MXU)
            dv_acc[...] += _dot_tn(p, do)
            dk_acc[...] += _dot_tn(ds, q)
            rows = pl.ds(pl.multiple_of(qi * t, t), t)
            dq_ref[0, rows, :] += _dot(ds, k)

        @pl.when(qi == nb - 1)
        def _():
            dk_ref[0] = dk_acc[...]
            dv_ref[0] = dv_acc[...]

    qmap = lambda h, j, i: (h, jnp.maximum(i, j), 0)
    return _pcall(
        body, name="mla_bwd", grid=(nh, nb, nb),
        out_shape=[_sds((nh, s, 256), jnp.float32), _sds((nh, s, 256), jnp.float32),
                   _sds((nh, s, MLA_V), jnp.float32)],
        in_specs=[pl.BlockSpec((1, t, 256), qmap),
                  pl.BlockSpec((1, t, 256), lambda h, j, i: (h, j, 0)),
                  pl.BlockSpec((1, t, MLA_V), lambda h, j, i: (h, j, 0)),
                  pl.BlockSpec((t, MLA_V), lambda h, j, i: (jnp.maximum(i, j), 8 + h)),
                  pl.BlockSpec((t, MLA_V), lambda h, j, i: (jnp.maximum(i, j), h)),
                  pl.BlockSpec((1, t, 1), qmap)],
        out_specs=[pl.BlockSpec((1, s, 256), lambda h, j, i: (h, 0, 0)),
                   pl.BlockSpec((1, t, 256), lambda h, j, i: (h, j, 0)),
                   pl.BlockSpec((1, t, MLA_V), lambda h, j, i: (h, j, 0))],
        scratch=[pltpu.VMEM((t, 256), jnp.float32), pltpu.VMEM((t, MLA_V), jnp.float32)],
        sem=("parallel", "arbitrary", "arbitrary"), after=after)(qc, kc, v, d_y, y_b, lse)


def _mla_prep_bwd(proj, cos, sin, g_cq, g_ckv, w_uq, w_ukv, g_qn, g_qr, g_kn, g_kr,
                  qb, kvb, cqn, ckvn, dqc, dkc, dv):
    s = proj.shape[0]
    tm = min(ROW_TILE, s)
    nh = MLA_HEADS
    ni = s // tm

    def body(cq_ref, ckv_ref, kr_ref, cos_ref, sin_ref, gcq_ref, gckv_ref, wuq_ref, wukv_ref,
             gqn_ref, gqr_ref, gkn_ref, gkr_ref, qb_ref, kvb_ref, cqn_ref, ckvn_ref, dqc_ref, dkc_ref, dv_ref,
             dcq_ref, dckv_ref, dkr_ref, dwuq_ref, dwukv_ref,
             dgcq_ref, dgckv_ref, dgqn_ref, dgqr_ref, dgkn_ref, dgkr_ref, acc_uq, acc_ukv):
        i = pl.program_id(0)

        @pl.when(i == 0)
        def _():
            acc_uq[...] = jnp.zeros_like(acc_uq)
            acc_ukv[...] = jnp.zeros_like(acc_ukv)
            for ref in (dgcq_ref, dgckv_ref, dgqn_ref, dgqr_ref, dgkn_ref, dgkr_ref):
                ref[...] = jnp.zeros_like(ref)

        cos_t, sin_t = cos_ref[...], sin_ref[...]
        lo = _lo_mask((tm, LANES))
        qb_v, kvb_v = qb_ref[...], kvb_ref[...]
        dq_parts, dgqn = [], jnp.zeros((1, LANES), jnp.float32)
        for h in range(nh):
            _, xn, r = _norm_fwd(qb_v[:, MLA_NOPE * h: MLA_NOPE * (h + 1)], gqn_ref[...])
            dx, dg = _norm_bwd(xn, r, gqn_ref[...], dqc_ref[h][:, :MLA_NOPE])
            dq_parts.append(dx)
            dgqn = dgqn + dg
        dgqn_ref[...] += dgqn
        dgqr = jnp.zeros((1, LANES), jnp.float32)
        for j in range(nh // 2):
            d_rope = jnp.where(lo, dqc_ref[2 * j][:, MLA_NOPE:], dqc_ref[2 * j + 1][:, MLA_NOPE:])
            d_pre = _rope_bwd(d_rope, cos_t, sin_t)
            xr = qb_v[:, nh * MLA_NOPE + LANES * j: nh * MLA_NOPE + LANES * (j + 1)]
            _, xn, r = _norm_fwd(xr, gqr_ref[...], half=True)
            dx, dg = _norm_bwd(xn, r, gqr_ref[...], d_pre, half=True)
            dq_parts.append(dx)
            dgqr = dgqr + dg
        dgqr_ref[...] += dgqr
        dqb = jnp.concatenate(dq_parts, axis=1).astype(MXU)
        acc_uq[...] += _dot_tn(cqn_ref[...], dqb)
        _, xn, r = _norm_fwd(cq_ref[...], gcq_ref[...])
        dx, dg = _norm_bwd(xn, r, gcq_ref[...], _dot_nt(dqb, wuq_ref[...]))
        dcq_ref[...] = dx.astype(dcq_ref.dtype)
        dgcq_ref[...] += dg
        dkv_parts, dgkn = [], jnp.zeros((1, LANES), jnp.float32)
        d_kr2 = jnp.zeros((tm, LANES), jnp.float32)
        for h in range(nh):
            _, xn, r = _norm_fwd(kvb_v[:, 256 * h: 256 * h + MLA_NOPE], gkn_ref[...])
            dx, dg = _norm_bwd(xn, r, gkn_ref[...], dkc_ref[h][:, :MLA_NOPE])
            dkv_parts += [dx, dv_ref[h]]
            dgkn = dgkn + dg
            d_kr2 = d_kr2 + dkc_ref[h][:, MLA_NOPE:]
        dgkn_ref[...] += dgkn
        dkvb = jnp.concatenate(dkv_parts, axis=1).astype(MXU)
        acc_ukv[...] += _dot_tn(ckvn_ref[...], dkvb)
        _, xn, r = _norm_fwd(ckv_ref[...], gckv_ref[...])
        dx, dg = _norm_bwd(xn, r, gckv_ref[...], _dot_nt(dkvb, wukv_ref[...]))
        dckv_ref[...] = dx.astype(dckv_ref.dtype)
        dgckv_ref[...] += dg
        d_kr = jnp.where(lo, d_kr2 + pltpu.roll(d_kr2, 64, 1), 0.0)
        d_pre = _rope_bwd(d_kr, cos_t, sin_t)
        _, xn, r = _norm_fwd(kr_ref[...], gkr_ref[...], half=True)
        dx, dg = _norm_bwd(xn, r, gkr_ref[...], d_pre, half=True)
        dkr_ref[...] = jnp.where(lo, dx, 0.0).astype(dkr_ref.dtype)
        dgkr_ref[...] += jnp.where(_lo_mask((1, LANES)), dg, 0.0)

        @pl.when(i == ni - 1)
        def _():
            dwuq_ref[...] = acc_uq[...].astype(dwuq_ref.dtype)
            dwukv_ref[...] = acc_ukv[...].astype(dwukv_ref.dtype)

    def col(width, start):
        return pl.BlockSpec((tm, width), lambda i: (i, start // width))

    def full(shape):
        return pl.BlockSpec(shape, lambda i: (0,) * len(shape))

    def row(width):
        return pl.BlockSpec((tm, width), lambda i: (i, 0))

    def heads(width):
        return pl.BlockSpec((nh, tm, width), lambda i: (0, i, 0))

    vec = full((1, LANES))
    return _pcall(
        body, name="mla_prep_bwd", grid=(ni,),
        out_shape=[_sds((s, 512), MXU), _sds((s, 512), MXU), _sds((s, LANES), MXU),
                   _sds((512, 768), WIRE), _sds((512, 1024), WIRE),
                   _sds((1, 512), jnp.float32), _sds((1, 512), jnp.float32)] + [_sds((1, LANES), jnp.float32)] * 4,
        in_specs=[col(512, C_CQ), col(512, C_CKV), col(LANES, C_KR), row(LANES), row(LANES),
                  full((1, 512)), full((1, 512)), full((512, 768)), full((512, 1024)), vec, vec, vec, vec,
                  row(768), row(1024), row(512), row(512), heads(256), heads(256), heads(MLA_V)],
        out_specs=[row(512), row(512), row(LANES), full((512, 768)), full((512, 1024)),
                   full((1, 512)), full((1, 512)), vec, vec, vec, vec],
        scratch=[pltpu.VMEM((512, 768), jnp.float32), pltpu.VMEM((512, 1024), jnp.float32)],
        sem=("arbitrary",))(proj, proj, proj, cos, sin, g_cq, g_ckv, w_uq, w_ukv, g_qn, g_qr, g_kn, g_kr,
                            qb, kvb, cqn, ckvn, dqc, dkc, dv)


def _swa_bwd(proj, posc, posr, gq, gk, sinks, d_y, y_a, lse, after):
    s = proj.shape[0]
    b = SWA_BLOCK
    nb = s // b
    scale = SWA_DIM ** -0.5

    def body(q_ref, kp_ref, kc_ref, vp_ref, vc_ref, pq_ref, pkp_ref, pkc_ref, gq_ref, gk_ref, sink_ref,
             do_ref, y_ref, lse_ref, kfull_ref,
             dq_ref, dk_ref, dv_ref, dgq_ref, dgk_ref, dsink_ref, dk_acc, dv_acc):
        n = pl.program_id(0)

        @pl.when(n == 0)
        def _():
            dk_acc[...] = jnp.zeros_like(dk_acc)
            dv_acc[...] = jnp.zeros_like(dv_acc)
            dgq_ref[...] = jnp.zeros_like(dgq_ref)
            dsink_ref[...] = jnp.zeros_like(dsink_ref)

        kn, v, dist, valid = _swa_common(n, kp_ref[...], kc_ref[...], vp_ref[...], vc_ref[...],
                                         pq_ref[...], pkp_ref[...], pkc_ref[...], gk_ref[...])
        lo = _lo_mask((b, LANES))
        col = lax.broadcasted_iota(jnp.int32, (b, SWA_Q_HEADS), 1)
        col1 = lax.broadcasted_iota(jnp.int32, (1, SWA_Q_HEADS), 1)
        lse_t = lse_ref[...]
        dk_blk = jnp.zeros((2 * b, LANES), jnp.float32)
        dv_blk = jnp.zeros((2 * b, LANES), jnp.float32)
        dgq = jnp.zeros((1, LANES), jnp.float32)
        dsink = jnp.zeros((1, SWA_Q_HEADS), jnp.float32)
        for j in range(SWA_Q_HEADS // 2):
            hk = (2 * j) // (SWA_Q_HEADS // SWA_KV_HEADS)
            kvmask = lo if hk == 0 else jnp.logical_not(lo)
            sl = slice(LANES * j, LANES * (j + 1))
            qn, xn, r = _norm_fwd(q_ref[:, sl], gq_ref[...], half=True)
            qsw = pltpu.roll(qn, 64, 1)
            d2 = do_ref[:, sl]
            d2sw = pltpu.roll(d2, 64, 1)
            prod = d2 * y_ref[:, sl]
            dqs = []
            for e in range(2):
                h = 2 * j + e
                half_e = lo if e == 0 else jnp.logical_not(lo)
                qm = jnp.where(kvmask, qn if e == hk else qsw, 0.0)
                dm = jnp.where(kvmask, d2 if e == hk else d2sw, 0.0)
                sc = _dot_nt(qm, kn) * scale - _alibi_slope(h) * dist
                sc = jnp.where(valid, sc, NEG_INF)
                lse_h = jnp.sum(jnp.where(col == h, lse_t, 0.0), -1, keepdims=True)
                p = jnp.exp(sc - lse_h)
                dd = jnp.sum(jnp.where(half_e, prod, 0.0), -1, keepdims=True)
                dp = _dot_nt(dm, v)
                ds = (p * (dp - dd)).astype(MXU)
                dsink = dsink - jnp.where(col1 == h, jnp.sum(jnp.exp(sink_ref[h] - lse_h) * dd), 0.0)
                dq_m = _dot(ds, kn) * scale
                dk_blk = dk_blk + _dot_tn(ds, qm) * scale
                dv_blk = dv_blk + _dot_tn(p, dm)
                dqs.append(dq_m if e == hk else pltpu.roll(dq_m, 64, 1))
            dx, dg = _norm_bwd(xn, r, gq_ref[...], jnp.where(lo, dqs[0], dqs[1]), half=True)
            dq_ref[:, sl] = dx.astype(dq_ref.dtype)
            dgq = dgq + dg
        dgq_ref[...] += dgq
        dsink_ref[...] += dsink
        prev = pl.ds(pl.multiple_of(jnp.maximum(n - 1, 0) * b, b), b)
        cur = pl.ds(pl.multiple_of(n * b, b), b)
        dk_acc[prev, :] += dk_blk[:b]
        dv_acc[prev, :] += dv_blk[:b]
        dk_acc[cur, :] += dk_blk[b:]
        dv_acc[cur, :] += dv_blk[b:]

        @pl.when(n == nb - 1)
        def _():
            _, kxn, kr = _norm_fwd(kfull_ref[...], gk_ref[...], half=True)
            dx, dg = _norm_bwd(kxn, kr, gk_ref[...], dk_acc[...], half=True)
            dk_ref[...] = dx.astype(dk_ref.dtype)
            dv_ref[...] = dv_acc[...].astype(dv_ref.dtype)
            dgk_ref[...] = dg

    full = pl.BlockSpec((s, LANES), lambda n: (0, 0))
    vec = pl.BlockSpec((1, LANES), lambda n: (0, 0))
    return _pcall(
        body, name="swa_bwd", grid=(nb,),
        out_shape=[_sds((s, 1024), MXU), _sds((s, LANES), MXU), _sds((s, LANES), MXU),
                   _sds((1, LANES), jnp.float32), _sds((1, LANES), jnp.float32),
                   _sds((1, SWA_Q_HEADS), jnp.float32)],
        in_specs=_swa_specs(s) + [pl.BlockSpec((b, 1024), lambda n: (n, 0)), pl.BlockSpec((b, 1024), lambda n: (n, 0)),
                                  pl.BlockSpec((b, SWA_Q_HEADS), lambda n: (n, 0)),
                                  pl.BlockSpec((s, LANES), lambda n: (0, C_KA // LANES))],
        out_specs=[pl.BlockSpec((b, 1024), lambda n: (n, 0)), full, full, vec, vec,
                   pl.BlockSpec((1, SWA_Q_HEADS), lambda n: (0, 0))],
        scratch=[pltpu.VMEM((s, LANES), jnp.float32), pltpu.VMEM((s, LANES), jnp.float32)],
        sem=("arbitrary",), after=after)(proj, proj, proj, proj, proj, posc, posr, posr, gq, gk, sinks, d_y, y_a, lse,
                                         proj)


def _dx(d_proj, w_in, x, g, d_h1, after):
    s, d = x.shape
    n = w_in.shape[1]
    tm = min(ROW_TILE, s)

    def body(dp_ref, w_ref, x_ref, g_ref, dh_ref, dx_ref, dg_ref):
        i = pl.program_id(0)

        @pl.when(i == 0)
        def _():
            dg_ref[...] = jnp.zeros_like(dg_ref)

        d_hn = _dot_nt(dp_ref[...], w_ref[...])
        _, xn, r = _norm_fwd(x_ref[...], g_ref[...])
        dx, dg = _norm_bwd(xn, r, g_ref[...], d_hn)
        dx_ref[...] = dh_ref[...] + dx
        dg_ref[...] += dg

    row = pl.BlockSpec((tm, d), lambda i: (i, 0))
    vec = pl.BlockSpec((1, d), lambda i: (0, 0))
    return _pcall(
        body, name="grad_x", grid=(s // tm,),
        out_shape=[_sds((s, d), jnp.float32), _sds((1, d), jnp.float32)],
        in_specs=[pl.BlockSpec((tm, n), lambda i: (i, 0)), pl.BlockSpec((d, n), lambda i: (0, 0)), row, vec, row],
        out_specs=[row, vec], sem=("arbitrary",), after=after)(d_proj, w_in, x, g, d_h1)


_SMALL = ["attn_norm_g", "swa_q_norm_g", "swa_k_norm_g", "swa_sinks", "mla_cq_norm_g", "mla_ckv_norm_g",
          "mla_qn_norm_g", "mla_qr_norm_g", "mla_kn_norm_g", "mla_kr_norm_g", "mem_norm_g",
          "mem_q_norm_g", "mem_k_norm_g", "ffn_norm_g"]


def _pack_rows(v):
    n = v.shape[-1]
    rows = -(-n // LANES)
    rows8 = -(-rows // 8) * 8
    flat = jnp.pad(v.reshape(-1), (0, rows8 * LANES - n))
    return flat.reshape(rows8, LANES)


def _pack(parts):
    return jnp.concatenate([_pack_rows(p) for p in parts], axis=0)


def _unpack(packed, sizes):
    out, r = [], 0
    for n in sizes:
        rows = -(-n // LANES)
        rows8 = -(-rows // 8) * 8
        out.append(packed[r:r + rows8].reshape(-1)[:n].reshape(1, n))
        r += rows8
    return out


def _fold64(v):
    return v[:, :64] + v[:, 64:]


def kernel(x, mem, positions, attn_norm_g, w_in, swa_q_norm_g, swa_k_norm_g, swa_sinks, mla_cq_norm_g, mla_ckv_norm_g, w_uq, w_ukv, mla_qn_norm_g, mla_qr_norm_g, mla_kn_norm_g, mla_kr_norm_g, mem_norm_g, w_mem_kv, mem_q_norm_g, mem_k_norm_g, w_out, ffn_norm_g, w_gate, w_up, w_down, loss_target, m_attn_norm_g, m_w_in, m_swa_q_norm_g, m_swa_k_norm_g, m_swa_sinks, m_mla_cq_norm_g, m_mla_ckv_norm_g, m_w_uq, m_w_ukv, m_mla_qn_norm_g, m_mla_qr_norm_g, m_mla_kn_norm_g, m_mla_kr_norm_g, m_mem_norm_g, m_w_mem_kv, m_mem_q_norm_g, m_mem_k_norm_g, m_w_out, m_ffn_norm_g, m_w_gate, m_w_up, m_w_down, v_attn_norm_g, v_w_in, v_swa_q_norm_g, v_swa_k_norm_g, v_swa_sinks, v_mla_cq_norm_g, v_mla_ckv_norm_g, v_w_uq, v_w_ukv, v_mla_qn_norm_g, v_mla_qr_norm_g, v_mla_kn_norm_g, v_mla_kr_norm_g, v_mem_norm_g, v_w_mem_kv, v_mem_q_norm_g, v_mem_k_norm_g, v_w_out, v_ffn_norm_g, v_w_gate, v_w_up, v_w_down):
    args = dict(locals())
    x2, mem2, tgt = x[0], mem[0], loss_target[0]
    s, d = x2.shape
    n_in = w_in.shape[2]
    f = w_gate.shape[2]

    shards = [w_in[0].astype(WIRE), w_uq[0].astype(WIRE), w_ukv[0].astype(WIRE), w_mem_kv[0].astype(WIRE),
              w_out[0].astype(WIRE), jnp.stack([w_gate[0], w_up[0]]).astype(WIRE), w_down[0].astype(WIRE)]
    g_in, g_uq, g_ukv, g_mkv, g_out = _all_gather(shards[:5])
    w_gu, w_d = _all_gather_background(shards[5:], 1, "all_gather_ffn_weights")
    wi = g_in.transpose(1, 0, 2).reshape(d, N_DEV * n_in)
    wi = jnp.concatenate([wi[:, 0:1024], wi[:, 1280:1792], wi[:, 1792:2304], wi[:, 2368:2880],
                          wi[:, 1024:1152], wi[:, 1152:1280], wi[:, 2304:2368],
                          jnp.zeros((d, IN_PAD - 2880), wi.dtype)], axis=1)
    wq = g_uq.transpose(1, 0, 2).reshape(512, 768)
    wq = jnp.concatenate([wq[:, 192 * h: 192 * h + 128] for h in range(4)]
                         + [wq[:, 192 * h + 128: 192 * (h + 1)] for h in range(4)], axis=1)
    wkv = g_ukv.transpose(1, 0, 2).reshape(512, 1024)
    wmkv = g_mkv.reshape(-1, g_mkv.shape[-1])
    wo = g_out.reshape(-1, d)

    pos = positions[0].astype(jnp.float32)
    inv_freq = ROPE_THETA ** (-jnp.arange(0, MLA_ROPE, 2, dtype=jnp.float32) / MLA_ROPE)
    ang = pos[:, None] * inv_freq
    cos32, sin32 = jnp.cos(ang), jnp.sin(ang)
    cos_t = jnp.tile(cos32, (1, 4))
    sin_t = jnp.tile(jnp.concatenate([-sin32, sin32], axis=1), (1, 2))
    posc, posr = pos.reshape(s, 1), pos.reshape(1, s)
    two = lambda g: jnp.tile(g, (1, 2))
    gq2, gk2, gqr2, gkr2 = two(swa_q_norm_g), two(swa_k_norm_g), two(mla_qr_norm_g), two(mla_kr_norm_g)
    sinks1 = swa_sinks[0]

    proj, hn = _in_proj(x2, attn_norm_g, wi)
    qc, kc, vb, qb, kvb, cqn, ckvn = _mla_prep(proj, cos_t, sin_t, mla_cq_norm_g, mla_ckv_norm_g, wq, wkv,
                                                mla_qn_norm_g, gqr2, mla_kn_norm_g, gkr2)
    y_b, lse_b = _mla_fwd(qc, kc, vb)
    km, vmm, kvm, memn = _memkv_prep(mem2, mem_norm_g, wmkv, mem_k_norm_g)
    y_m, lse_m = _mem_fwd(proj, mem_q_norm_g, km, vmm)
    y_a, lse_a = _swa_fwd(proj, posc, posr, gq2, gk2, sinks1)
    h1, fn = _out_proj(y_a, y_b, y_m, x2, wo, ffn_norm_g)
    gu, act = _ffn_gu(fn, w_gu)
    dout, loss_tile = _ffn_down(act, w_d, h1, tgt)

    dgu, dw_d = _ffn_bwd_act(dout, w_d, gu)
    dw_gu = _ffn_dw_gu(fn, dgu)
    r_gu, r_d = _exchange_grads_background([dw_gu, dw_d], 2, "exchange_ffn_grads")
    d_h1, dg_ffn = _ffn_norm_bwd(_ffn_dfn(dgu, w_gu, dw_gu), dout, h1, ffn_norm_g)
    d_y = _mm(d_h1, wo, tb=True, out_dtype=jnp.float32, tm=FFN_TILE, tk=512, name="d_mix")
    dw_out = jnp.concatenate([
        _mm(y_a, d_h1, ta=True, out_dtype=WIRE, tm=1024, tk=512, name="dw_out_a"),
        _mm(y_b, d_h1, ta=True, out_dtype=WIRE, tm=1024, tk=512, name="dw_out_b"),
        _mm(y_m, d_h1, ta=True, out_dtype=WIRE, tm=1024, tk=512, name="dw_out_m")], axis=0)
    d_qm, dkm, dvmm, dg_mq = _mem_bwd(proj, mem_q_norm_g, km, vmm, d_y, y_m, lse_m)
    dw_mkv, dg_mem, dg_mk = _memkv_bwd(mem2, mem_norm_g, wmkv, mem_k_norm_g, kvm, memn, dkm, dvmm)
    r_mkv, r_out = _exchange_grads_background([dw_mkv.reshape(g_mkv.shape), dw_out.reshape(g_out.shape)], 3,
                                              "exchange_mix_grads")
    dqc, dkc, dvb = _mla_bwd(qc, kc, vb, d_y, y_b, lse_b, dw_mkv)
    (d_cq, d_ckv, d_kr, dw_uq, dw_ukv, dg_cq, dg_ckv, dg_qn, dg_qr, dg_kn, dg_kr) = _mla_prep_bwd(
        proj, cos_t, sin_t, mla_cq_norm_g, mla_ckv_norm_g, wq, wkv, mla_qn_norm_g, gqr2, mla_kn_norm_g, gkr2,
        qb, kvb, cqn, ckvn, dqc, dkc, dvb)
    d_qa, d_ka, d_va, dg_q, dg_k, d_sinks = _swa_bwd(proj, posc, posr, gq2, gk2, sinks1, d_y, y_a, lse_a, dw_out)
    d_proj = jnp.concatenate([d_qa, d_cq, d_ckv, d_qm, d_ka, d_va, d_kr], axis=1)
    gi = _dw_in(hn, d_proj, n_in)

    gq_ = jnp.concatenate(sum([[dw_uq[:, 128 * h: 128 * (h + 1)], dw_uq[:, 512 + 64 * h: 512 + 64 * (h + 1)]]
                               for h in range(4)], []), axis=1)
    gq_ = gq_.reshape(512, N_DEV, 96).transpose(1, 0, 2)
    gkv = dw_ukv.reshape(512, N_DEV, 128).transpose(1, 0, 2)
    r_in, r_uq, r_ukv = _exchange_grads_background([gi, gq_, gkv], 4, "exchange_in_grads")
    grad_x, dg_attn = _dx(d_proj, wi, x2, attn_norm_g, d_h1, gi)
    recv = [r_in, r_uq, r_ukv, r_mkv, r_out, r_gu, r_d]

    big = {}
    def adam(name, r, stacked=False):
        w, m, v = args[name][0], args["m_" + name][0], args["v_" + name][0]
        return _adam_big(r.reshape(N_DEV, -1, r.shape[-1]), w, m, v, "adam_" + name)
    for name, r in zip(["w_in", "w_uq", "w_ukv", "w_mem_kv", "w_out"], recv[:5]):
        big[name] = [o[None] for o in adam(name, r)]
    big["w_down"] = [o[None] for o in adam("w_down", recv[6])]
    r_gu = recv[5]
    big["w_gate"] = [o[None] for o in adam("w_gate", r_gu[:, 0])]
    big["w_up"] = [o[None] for o in adam("w_up", r_gu[:, 1])]

    small_g = {
        "attn_norm_g": dg_attn, "swa_q_norm_g": _fold64(dg_q), "swa_k_norm_g": _fold64(dg_k),
        "swa_sinks": d_sinks, "mla_cq_norm_g": dg_cq, "mla_ckv_norm_g": dg_ckv, "mla_qn_norm_g": dg_qn,
        "mla_qr_norm_g": _fold64(dg_qr), "mla_kn_norm_g": dg_kn, "mla_kr_norm_g": _fold64(dg_kr),
        "mem_norm_g": dg_mem, "mem_q_norm_g": dg_mq, "mem_k_norm_g": dg_mk, "ffn_norm_g": dg_ffn}
    sizes = [args[n].shape[-1] for n in _SMALL]
    pg = _pack([small_g[n] for n in _SMALL] + [loss_tile[0:1, 0:1]])
    zero = jnp.zeros((1, 1), jnp.float32)
    pw = _pack([args[n] for n in _SMALL] + [zero])
    pm = _pack([args["m_" + n] for n in _SMALL] + [zero])
    pv = _pack([args["v_" + n] for n in _SMALL] + [zero])
    sg, sd, sm, sv = _small_allreduce_adam(pg, pw, pm, pv)
    small = {n: vals for n, vals in zip(_SMALL, zip(*[_unpack(p, sizes) for p in (sg, sd, sm, sv)]))}
    loss = _unpack(sg, sizes + [1])[-1].reshape(())

    order = ["attn_norm_g", "w_in", "swa_q_norm_g", "swa_k_norm_g", "swa_sinks", "mla_cq_norm_g", "mla_ckv_norm_g",
             "w_uq", "w_ukv", "mla_qn_norm_g", "mla_qr_norm_g", "mla_kn_norm_g", "mla_kr_norm_g", "mem_norm_g",
             "w_mem_kv", "mem_q_norm_g", "mem_k_norm_g", "w_out", "ffn_norm_g", "w_gate", "w_up", "w_down"]
    res = {n: (big[n] if n in big else list(small[n])) for n in order}
    outs = [loss, grad_x[None]]
    for kind in range(4):
        outs += [res[n][kind] for n in order]
    return tuple(outs)
```

```python
import jax
import jax.numpy as jnp
from jax import lax
from jax.experimental import pallas as pl
from jax.experimental.pallas import tpu as pltpu
from jax.experimental.pallas import tpu_sc as plsc

MXU = jnp.bfloat16
WIRE = jnp.bfloat16
EPS = 1e-6
NEG_INF = -1e30
N_DEV = 8
LANES = 128
ROW_TILE = 256
FFN_TILE = 512
ATT_TILE = 1024
SWA_BLOCK = 128
VMEM_LIMIT = 56 * 1024 * 1024

SWA_Q_HEADS, SWA_KV_HEADS, SWA_DIM = 16, 2, 64
MLA_HEADS, MLA_NOPE, MLA_ROPE, MLA_V = 4, 128, 64, 128
MEM_HEADS, MEM_DIM = 4, 128
ROPE_THETA = 10000.0
ADAM_LR, ADAM_B1, ADAM_B2, ADAM_EPS, ADAM_WD, ADAM_STEP = 0.001, 0.9, 0.999, 1e-08, 0.01, 10

C_QA, C_CQ, C_CKV, C_QM, C_KA, C_VA, C_KR, IN_PAD = 0, 1024, 1536, 2048, 2560, 2688, 2816, 2944


def _pcall(body, *, name, out_shape, in_specs, out_specs, grid=(), scratch=(), sem=None, after=None):
    params = pltpu.CompilerParams(dimension_semantics=sem, vmem_limit_bytes=VMEM_LIMIT)
    if after is not None:
        n_in, inner = len(in_specs), body

        def body(*refs):
            inner(*refs[:n_in], *refs[n_in + 1:])

        in_specs = list(in_specs) + [pl.BlockSpec(memory_space=pl.ANY)]
    call = pl.pallas_call(body, name=name, grid=grid, in_specs=in_specs, out_specs=out_specs,
                          out_shape=out_shape, scratch_shapes=list(scratch), compiler_params=params)
    return call if after is None else (lambda *ops: call(*ops, after))


def _sds(shape, dtype):
    return jax.ShapeDtypeStruct(tuple(shape), dtype)


def _dot(a, b):
    return jnp.dot(a.astype(MXU), b.astype(MXU), preferred_element_type=jnp.float32)


def _dot_nt(a, b):
    return lax.dot_general(a.astype(MXU), b.astype(MXU), (((1,), (1,)), ((), ())),
                           preferred_element_type=jnp.float32)


def _dot_tn(a, b):
    return lax.dot_general(a.astype(MXU), b.astype(MXU), (((0,), (0,)), ((), ())),
                           preferred_element_type=jnp.float32)


def _lo_mask(shape):
    return (lax.broadcasted_iota(jnp.int32, shape, len(shape) - 1) % LANES) < 64


def _norm_fwd(x, g, half=False):
    x2 = x * x
    if half:
        lo = _lo_mask(x.shape)
        s_lo = jnp.sum(jnp.where(lo, x2, 0.0), -1, keepdims=True)
        s_hi = jnp.sum(jnp.where(lo, 0.0, x2), -1, keepdims=True)
        r = jnp.where(lo, lax.rsqrt(s_lo / 64.0 + EPS), lax.rsqrt(s_hi / 64.0 + EPS))
    else:
        r = lax.rsqrt(jnp.mean(x2, -1, keepdims=True) + EPS)
    xn = x * r
    return xn * g, xn, r


def _norm_bwd(xn, r, g, dy, half=False):
    t = dy * g
    tx = t * xn
    if half:
        lo = _lo_mask(xn.shape)
        m_lo = jnp.sum(jnp.where(lo, tx, 0.0), -1, keepdims=True) / 64.0
        m_hi = jnp.sum(jnp.where(lo, 0.0, tx), -1, keepdims=True) / 64.0
        m = jnp.where(lo, m_lo, m_hi)
    else:
        m = jnp.mean(tx, -1, keepdims=True)
    dx = r * (t - xn * m)
    dg = jnp.sum(dy * xn, 0, keepdims=True)
    return dx, dg


def _swap32(x):
    lane = lax.broadcasted_iota(jnp.int32, x.shape, 1)
    return jnp.where((lane % 64) < 32, pltpu.roll(x, 96, 1), pltpu.roll(x, 32, 1))


def _rope(x, cos, sin):
    return x * cos + _swap32(x) * sin


def _rope_bwd(d, cos, sin):
    return d * cos + _swap32(d * sin)


def _my_coords():
    return lax.axis_index("x"), lax.axis_index("y"), lax.axis_index("c")


def _dev_index(px, py, pc):
    return 4 * px + 2 * py + pc


_FLIPS = [(0, 0, 1), (0, 1, 0), (0, 1, 1), (1, 0, 0), (1, 0, 1), (1, 1, 0), (1, 1, 1)]


def _flip(coords, f):
    return tuple((1 - v) if b else v for v, b in zip(coords, f))


def _all_gather(shards):
    n = len(shards)

    def body(*refs):
        ins, outs = refs[:n], refs[n:2 * n]
        send_sems, recv_sems, local_sems = refs[2 * n:]
        x, y, c = _my_coords()
        me, sibling = (x, y, c), (x, y, 1 - c)
        chips = [(1 - x, y), (x, 1 - y), (1 - x, 1 - y)]

        def copy(w, k, block, to, src=None):
            dst = outs[w].at[_dev_index(*block)]
            return pltpu.make_async_remote_copy(
                src_ref=dst if src is None else src, dst_ref=dst,
                send_sem=send_sems.at[w, k], recv_sem=recv_sems.at[w, k],
                device_id=to, device_id_type=pl.DeviceIdType.MESH)

        sends, locals_ = [], []
        for w in range(n):
            mine = pltpu.make_async_copy(ins[w], outs[w].at[_dev_index(*me)], local_sems.at[w])
            mine.start()
            locals_.append(mine)
            first = [copy(w, 0, me, sibling, src=ins[w])]
            first += [copy(w, 1 + j, me, (*chip, c), src=ins[w]) for j, chip in enumerate(chips)]
            for cp in first:
                cp.start()
            sends += first
        for w in range(n):
            for j, chip in enumerate(chips):
                copy(w, 1 + j, (*chip, c), me).wait_recv()
                fwd = copy(w, 4 + j, (*chip, c), sibling)
                fwd.start()
                sends.append(fwd)
        for w in range(n):
            copy(w, 0, sibling, me).wait_recv()
            for j, chip in enumerate(chips):
                copy(w, 4 + j, (*chip, 1 - c), me).wait_recv()
        for cp in sends:
            cp.wait_send()
        for mine in locals_:
            mine.wait()

    any_spec = pl.BlockSpec(memory_space=pl.ANY)
    return _pcall(
        body, name="all_gather_weights",
        out_shape=[_sds((N_DEV,) + s.shape, s.dtype) for s in shards],
        in_specs=[any_spec] * n, out_specs=[any_spec] * n,
        scratch=[pltpu.SemaphoreType.DMA((n, 7)), pltpu.SemaphoreType.DMA((n, 7)),
                 pltpu.SemaphoreType.DMA((n,))])(*shards)


def _wire_cost(arrays):
    nbytes = sum(a.size * a.dtype.itemsize for a in arrays)
    return pl.CostEstimate(flops=0, transcendentals=0, bytes_accessed=40 * nbytes)


def _all_gather_background(shards, collective_id, name):
    n = len(shards)
    src_refs = [jax.new_ref(s, memory_space=pltpu.MemorySpace.HBM) for s in shards]
    out_refs = [jax.empty_ref(_sds((N_DEV,) + s.shape, s.dtype), memory_space=pltpu.MemorySpace.HBM) for s in shards]

    @pl.kernel(mesh=plsc.ScalarSubcoreMesh(axis_name="seq", num_cores=1), name=name,
               scratch_types=(pltpu.SemaphoreType.DMA((n, 7)), pltpu.SemaphoreType.DMA((n, 7)),
                              pltpu.SemaphoreType.DMA((n,))),
               compiler_params=pltpu.CompilerParams(collective_id=collective_id))
    def launch(send_sems, recv_sems, local_sems):
        x, y, c = _my_coords()
        me, sibling = (x, y, c), (x, y, 1 - c)
        chips = [(1 - x, y), (x, 1 - y), (1 - x, 1 - y)]
        barrier = pltpu.get_barrier_semaphore()
        for peer in [sibling] + [(*chip, c) for chip in chips]:
            pl.semaphore_signal(barrier, inc=1, device_id=peer, device_id_type=pl.DeviceIdType.MESH)
        pl.semaphore_wait(barrier, 4)

        def copy(w, k, block, to, src=None):
            dst = out_refs[w].at[_dev_index(*block)]
            return pltpu.make_async_remote_copy(
                src_ref=dst if src is None else src, dst_ref=dst,
                send_sem=send_sems.at[w, k], recv_sem=recv_sems.at[w, k],
                device_id=to, device_id_type=pl.DeviceIdType.MESH)

        sends, locals_ = [], []
        for w in range(n):
            mine = pltpu.make_async_copy(src_refs[w], out_refs[w].at[_dev_index(*me)], local_sems.at[w])
            mine.start()
            locals_.append(mine)
            first = [copy(w, 0, me, sibling, src=src_refs[w])]
            first += [copy(w, 1 + j, me, (*chip, c), src=src_refs[w]) for j, chip in enumerate(chips)]
            for cp in first:
                cp.start()
            sends += first
        for w in range(n):
            for j, chip in enumerate(chips):
                copy(w, 1 + j, (*chip, c), me).wait_recv()
                fwd = copy(w, 4 + j, (*chip, c), sibling)
                fwd.start()
                sends.append(fwd)
        for w in range(n):
            copy(w, 0, sibling, me).wait_recv()
            for j, chip in enumerate(chips):
                copy(w, 4 + j, (*chip, 1 - c), me).wait_recv()
        for cp in sends:
            cp.wait_send()
        for mine in locals_:
            mine.wait()

    launch()
    return [r[...] for r in out_refs]


def _exchange_grads(grads):
    n = len(grads)

    def body(*refs):
        ins, outs = refs[:n], refs[n:2 * n]
        send_sems, recv_sems, local_sems = refs[2 * n:]
        me = _my_coords()
        my_idx = _dev_index(*me)
        sends, locals_ = [], []
        for w in range(n):
            mine = pltpu.make_async_copy(ins[w].at[my_idx], outs[w].at[my_idx], local_sems.at[w])
            mine.start()
            locals_.append(mine)
            for k, f in enumerate(_FLIPS):
                peer = _flip(me, f)
                cp = pltpu.make_async_remote_copy(
                    src_ref=ins[w].at[_dev_index(*peer)], dst_ref=outs[w].at[my_idx],
                    send_sem=send_sems.at[w, k], recv_sem=recv_sems.at[w, k],
                    device_id=peer, device_id_type=pl.DeviceIdType.MESH)
                cp.start()
                sends.append(cp)
        for w in range(n):
            for k, f in enumerate(_FLIPS):
                peer = _flip(me, f)
                slot = outs[w].at[_dev_index(*peer)]
                pltpu.make_async_remote_copy(
                    src_ref=slot, dst_ref=slot,
                    send_sem=send_sems.at[w, k], recv_sem=recv_sems.at[w, k],
                    device_id=peer, device_id_type=pl.DeviceIdType.MESH).wait_recv()
        for cp in sends:
            cp.wait_send()
        for mine in locals_:
            mine.wait()

    any_spec = pl.BlockSpec(memory_space=pl.ANY)
    return _pcall(
        body, name="exchange_grads",
        out_shape=[_sds(g.shape, g.dtype) for g in grads],
        in_specs=[any_spec] * n, out_specs=[any_spec] * n,
        scratch=[pltpu.SemaphoreType.DMA((n, 7)), pltpu.SemaphoreType.DMA((n, 7)),
                 pltpu.SemaphoreType.DMA((n,))])(*grads)


def _exchange_grads_background(grads, collective_id, name):
    n = len(grads)
    src_refs = [jax.new_ref(g, memory_space=pltpu.MemorySpace.HBM) for g in grads]
    out_refs = [jax.empty_ref(_sds(g.shape, g.dtype), memory_space=pltpu.MemorySpace.HBM) for g in grads]

    @pl.kernel(mesh=plsc.ScalarSubcoreMesh(axis_name="seq", num_cores=1), name=name,
               scratch_types=(pltpu.SemaphoreType.DMA((n, 7)), pltpu.SemaphoreType.DMA((n, 7)),
                              pltpu.SemaphoreType.DMA((n,))),
               cost_estimate=_wire_cost(grads),
               compiler_params=pltpu.CompilerParams(collective_id=collective_id))
    def launch(send_sems, recv_sems, local_sems):
        me = _my_coords()
        my_idx = _dev_index(*me)
        peers = [_flip(me, f) for f in _FLIPS]
        barrier = pltpu.get_barrier_semaphore()
        for peer in peers:
            pl.semaphore_signal(barrier, inc=1, device_id=peer, device_id_type=pl.DeviceIdType.MESH)
        pl.semaphore_wait(barrier, len(peers))
        sends, locals_ = [], []
        for w in range(n):
            mine = pltpu.make_async_copy(src_refs[w].at[my_idx], out_refs[w].at[my_idx], local_sems.at[w])
            mine.start()
            locals_.append(mine)
            for k, peer in enumerate(peers):
                cp = pltpu.make_async_remote_copy(
                    src_ref=src_refs[w].at[_dev_index(*peer)], dst_ref=out_refs[w].at[my_idx],
                    send_sem=send_sems.at[w, k], recv_sem=recv_sems.at[w, k],
                    device_id=peer, device_id_type=pl.DeviceIdType.MESH)
                cp.start()
                sends.append(cp)
        for w in range(n):
            for k, peer in enumerate(peers):
                slot = out_refs[w].at[_dev_index(*peer)]
                pltpu.make_async_remote_copy(
                    src_ref=slot, dst_ref=slot, send_sem=send_sems.at[w, k], recv_sem=recv_sems.at[w, k],
                    device_id=peer, device_id_type=pl.DeviceIdType.MESH).wait_recv()
        for cp in sends:
            cp.wait_send()
        for mine in locals_:
            mine.wait()

    launch()
    return [r[...] for r in out_refs]


def _adam_math(w, g, m, v):
    m = ADAM_B1 * m + (1.0 - ADAM_B1) * g
    v = ADAM_B2 * v + (1.0 - ADAM_B2) * (g * g)
    m_hat = m / (1.0 - ADAM_B1 ** ADAM_STEP)
    v_hat = v / (1.0 - ADAM_B2 ** ADAM_STEP)
    delta = -ADAM_LR * (m_hat / (jnp.sqrt(v_hat) + ADAM_EPS) + ADAM_WD * w)
    return delta, m, v


def _small_allreduce_adam(pg, pw, pm, pv):
    rows = pg.shape[0]

    def body(pg_ref, pw_ref, pm_ref, pv_ref, g_ref, d_ref, m_ref, v_ref, gath, send_sems, recv_sems):
        me = _my_coords()
        my_idx = _dev_index(*me)
        gath[my_idx] = pg_ref[...]
        sends = []
        for k, f in enumerate(_FLIPS):
            peer = _flip(me, f)
            cp = pltpu.make_async_remote_copy(
                src_ref=pg_ref, dst_ref=gath.at[my_idx],
                send_sem=send_sems.at[k], recv_sem=recv_sems.at[k],
                device_id=peer, device_id_type=pl.DeviceIdType.MESH)
            cp.start()
            sends.append(cp)
        for k, f in enumerate(_FLIPS):
            peer = _flip(me, f)
            slot = gath.at[_dev_index(*peer)]
            pltpu.make_async_remote_copy(
                src_ref=slot, dst_ref=slot, send_sem=send_sems.at[k], recv_sem=recv_sems.at[k],
                device_id=peer, device_id_type=pl.DeviceIdType.MESH).wait_recv()
        for cp in sends:
            cp.wait_send()
        g = gath[0]
        for d in range(1, N_DEV):
            g = g + gath[d]
        delta, m, v = _adam_math(pw_ref[...], g, pm_ref[...], pv_ref[...])
        g_ref[...] = g
        d_ref[...] = delta
        m_ref[...] = m
        v_ref[...] = v

    vm = pl.BlockSpec(memory_space=pltpu.VMEM)
    return _pcall(
        body, name="small_allreduce_adam",
        out_shape=[_sds(pg.shape, jnp.float32)] * 4,
        in_specs=[vm] * 4, out_specs=[vm] * 4,
        scratch=[pltpu.VMEM((N_DEV, rows, LANES), jnp.float32),
                 pltpu.SemaphoreType.DMA((7,)), pltpu.SemaphoreType.DMA((7,))])(pg, pw, pm, pv)


def _adam_big(recv, w, m, v, name):
    _, rows, cols = recv.shape
    tc = 512 if cols % 512 == 0 else cols
    tr = rows
    while tr * tc > 256 * 1024 and tr % 2 == 0 and (tr // 2) % 16 == 0:
        tr //= 2

    def body(r_ref, w_ref, m_ref, v_ref, g_ref, d_ref, mo_ref, vo_ref):
        g = r_ref[0].astype(jnp.float32)
        for d in range(1, N_DEV):
            g = g + r_ref[d].astype(jnp.float32)
        delta, mn, vn = _adam_math(w_ref[...], g, m_ref[...], v_ref[...])
        g_ref[...] = g
        d_ref[...] = delta
        mo_ref[...] = mn
        vo_ref[...] = vn

    blk = pl.BlockSpec((tr, tc), lambda i, j: (i, j))
    return _pcall(
        body, name=name, grid=(rows // tr, cols // tc),
        out_shape=[_sds((rows, cols), jnp.float32)] * 4,
        in_specs=[pl.BlockSpec((N_DEV, tr, tc), lambda i, j: (0, i, j)), blk, blk, blk],
        out_specs=[blk] * 4, sem=("parallel", "parallel"))(recv, w, m, v)


def _mm(a, b, *, ta=False, tb=False, out_dtype, tm, tk, name):
    (kdim, mdim) = a.shape if ta else a.shape[::-1]
    ndim = b.shape[0] if tb else b.shape[1]
    tm, tk = min(tm, mdim), min(tk, kdim)
    nk = kdim // tk

    def body(a_ref, b_ref, o_ref, acc):
        k = pl.program_id(1)
        if ta:
            part = _dot_tn(a_ref[...], b_ref[...])
        elif tb:
            part = _dot_nt(a_ref[...], b_ref[...])
        else:
            part = _dot(a_ref[...], b_ref[...])

        @pl.when(k == 0)
        def _():
            acc[...] = part

        @pl.when(k > 0)
        def _():
            acc[...] += part

        @pl.when(k == nk - 1)
        def _():
            o_ref[...] = acc[...].astype(o_ref.dtype)

    a_spec = pl.BlockSpec((tk, tm), lambda i, k: (k, i)) if ta else pl.BlockSpec((tm, tk), lambda i, k: (i, k))
    b_spec = pl.BlockSpec((ndim, tk), lambda i, k: (0, k)) if tb else pl.BlockSpec((tk, ndim), lambda i, k: (k, 0))
    return _pcall(
        body, name=name, grid=(mdim // tm, nk), out_shape=_sds((mdim, ndim), out_dtype),
        in_specs=[a_spec, b_spec], out_specs=pl.BlockSpec((tm, ndim), lambda i, k: (i, 0)),
        scratch=[pltpu.VMEM((tm, ndim), jnp.float32)], sem=("parallel", "arbitrary"))(a, b)


def _ref_col_pieces(start, stop):
    ref_starts = [0, 1024, 1152, 1280, 1792, 2304, 2368, 2880]
    perm_starts = [C_QA, C_KA, C_VA, C_CQ, C_CKV, C_KR, C_QM]
    out = []
    for p in range(7):
        lo, hi = max(start, ref_starts[p]), min(stop, ref_starts[p + 1])
        if lo < hi:
            out.append((lo - start, perm_starts[p] + lo - ref_starts[p], hi - lo))
    return out


def _dw_in(hn, d_proj, n_shard):
    s, d = hn.shape
    n = d_proj.shape[1]
    tm, tk = min(512, d), min(512, s)
    nk = s // tk

    def body(a_ref, b_ref, o_ref, acc):
        k = pl.program_id(1)
        part = _dot_tn(a_ref[...], b_ref[...])

        @pl.when(k == 0)
        def _():
            acc[...] = part

        @pl.when(k > 0)
        def _():
            acc[...] += part

        @pl.when(k == nk - 1)
        def _():
            t = acc[...].T
            for j in range(N_DEV):
                rows = [t[src:src + width] for _, src, width in _ref_col_pieces(j * n_shard, (j + 1) * n_shard)]
                o_ref[j] = jnp.concatenate(rows, axis=0).astype(o_ref.dtype)

    return _pcall(
        body, name="dw_in", grid=(d // tm, nk), out_shape=_sds((N_DEV, n_shard, d), WIRE),
        in_specs=[pl.BlockSpec((tk, tm), lambda i, k: (k, i)), pl.BlockSpec((tk, n), lambda i, k: (k, 0))],
        out_specs=pl.BlockSpec((N_DEV, n_shard, tm), lambda i, k: (0, 0, i)),
        scratch=[pltpu.VMEM((tm, n), jnp.float32)], sem=("parallel", "arbitrary"))(hn, d_proj)


def _in_proj(x, g, w):
    s, d = x.shape
    n = w.shape[0]
    tm = min(ROW_TILE, s)

    def body(x_ref, g_ref, w_ref, p_ref, hn_ref):
        hn, _, _ = _norm_fwd(x_ref[...], g_ref[...])
        hn_ref[...] = hn.astype(hn_ref.dtype)
        p_ref[...] = _dot_nt(hn, w_ref[...])

    return _pcall(
        body, name="in_proj", grid=(s // tm,),
        out_shape=[_sds((s, n), jnp.float32), _sds((s, d), MXU)],
        in_specs=[pl.BlockSpec((tm, d), lambda i: (i, 0)), pl.BlockSpec((1, d), lambda i: (0, 0)),
                  pl.BlockSpec((n, d), lambda i: (0, 0))],
        out_specs=[pl.BlockSpec((tm, n), lambda i: (i, 0)), pl.BlockSpec((tm, d), lambda i: (i, 0))],
        sem=("parallel",))(x, g, w)


def _mla_prep(proj, cos, sin, g_cq, g_ckv, w_uq, w_ukv, g_qn, g_qr, g_kn, g_kr):
    s = proj.shape[0]
    tm = min(ROW_TILE, s)
    nh = MLA_HEADS

    def body(cq_ref, ckv_ref, kr_ref, cos_ref, sin_ref, gcq_ref, gckv_ref, wuq_ref, wukv_ref,
             gqn_ref, gqr_ref, gkn_ref, gkr_ref,
             qc_ref, kc_ref, v_ref, qb_ref, kvb_ref, cqn_ref, ckvn_ref):
        cos_t, sin_t = cos_ref[...], sin_ref[...]
        lo = _lo_mask((tm, LANES))
        cqn, _, _ = _norm_fwd(cq_ref[...], gcq_ref[...])
        cqn_ref[...] = cqn.astype(cqn_ref.dtype)
        qb = _dot_nt(cqn, wuq_ref[...])
        qb_ref[...] = qb
        ckvn, _, _ = _norm_fwd(ckv_ref[...], gckv_ref[...])
        ckvn_ref[...] = ckvn.astype(ckvn_ref.dtype)
        kvb = jnp.concatenate([_dot(ckvn, wukv_ref[dev]) for dev in range(N_DEV)], axis=1)
        kvb_ref[...] = kvb
        kr, _, _ = _norm_fwd(kr_ref[...], gkr_ref[...], half=True)
        kr = _rope(kr, cos_t, sin_t)
        kr2 = jnp.where(lo, kr, pltpu.roll(kr, 64, 1))
        ropes = []
        for j in range(nh // 2):
            xr = qb[:, nh * MLA_NOPE + LANES * j: nh * MLA_NOPE + LANES * (j + 1)]
            qr, _, _ = _norm_fwd(xr, gqr_ref[...], half=True)
            ropes.append(_rope(qr, cos_t, sin_t))
        for h in range(nh):
            qn, _, _ = _norm_fwd(qb[:, MLA_NOPE * h: MLA_NOPE * (h + 1)], gqn_ref[...])
            mask = lo if h % 2 == 0 else jnp.logical_not(lo)
            qr = jnp.where(mask, ropes[h // 2], 0.0)
            qc_ref[h] = jnp.concatenate([qn, qr], axis=1).astype(qc_ref.dtype)
            kn, _, _ = _norm_fwd(kvb[:, 256 * h: 256 * h + MLA_NOPE], gkn_ref[...])
            kc_ref[h] = jnp.concatenate([kn, kr2], axis=1).astype(kc_ref.dtype)
            v_ref[h] = kvb[:, 256 * h + MLA_NOPE: 256 * (h + 1)].astype(v_ref.dtype)

    def col(width, start):
        return pl.BlockSpec((tm, width), lambda i: (i, start // width))

    def full(shape):
        return pl.BlockSpec(shape, lambda i: (0,) * len(shape))

    def row(width):
        return pl.BlockSpec((tm, width), lambda i: (i, 0))

    def heads(width):
        return pl.BlockSpec((nh, tm, width), lambda i: (0, i, 0))

    return _pcall(
        body, name="mla_prep", grid=(s // tm,),
        out_shape=[_sds((nh, s, 256), MXU), _sds((nh, s, 256), MXU), _sds((nh, s, MLA_V), MXU),
                   _sds((s, 768), jnp.float32), _sds((s, 1024), jnp.float32),
                   _sds((s, 512), MXU), _sds((s, 512), MXU)],
        in_specs=[col(512, C_CQ), col(512, C_CKV), col(LANES, C_KR), row(LANES), row(LANES),
                  full((1, 512)), full((1, 512)), full((768, 512)), full((N_DEV, 512, LANES)),
                  full((1, LANES)), full((1, LANES)), full((1, LANES)), full((1, LANES))],
        out_specs=[heads(256), heads(256), heads(MLA_V), row(768), row(1024), row(512), row(512)],
        sem=("parallel",))(proj, proj, proj, cos, sin, g_cq, g_ckv, w_uq, w_ukv, g_qn, g_qr, g_kn, g_kr)


def _mla_fwd(qc, kc, v):
    nh, s, _ = qc.shape
    t = min(ATT_TILE, s)
    nb = s // t
    scale = (MLA_NOPE + MLA_ROPE) ** -0.5

    def body(q_ref, k_ref, v_ref, y_ref, lse_ref, m_sc, l_sc, acc):
        qi, ki = pl.program_id(1), pl.program_id(2)

        @pl.when(ki == 0)
        def _():
            m_sc[...] = jnp.full_like(m_sc, NEG_INF)
            l_sc[...] = jnp.zeros_like(l_sc)
            acc[...] = jnp.zeros_like(acc)

        @pl.when(ki <= qi)
        def _():
            sc = _dot_nt(q_ref[0], k_ref[0]) * scale
            r_i = lax.broadcasted_iota(jnp.int32, sc.shape, 0) + qi * t
            c_i = lax.broadcasted_iota(jnp.int32, sc.shape, 1) + ki * t
            sc = jnp.where(c_i <= r_i, sc, NEG_INF)
            m_new = jnp.maximum(m_sc[...], jnp.max(sc, -1, keepdims=True))
            alpha = jnp.exp(m_sc[...] - m_new)
            p = jnp.exp(sc - m_new)
            l_sc[...] = alpha * l_sc[...] + jnp.sum(p, -1, keepdims=True)
            acc[...] = alpha * acc[...] + _dot(p, v_ref[0])
            m_sc[...] = m_new

        @pl.when(ki == qi)
        def _():
            y_ref[...] = acc[...] / l_sc[...]
            lse_ref[0] = m_sc[...] + jnp.log(l_sc[...])

    return _pcall(
        body, name="mla_fwd", grid=(nh, nb, nb),
        out_shape=[_sds((s, nh * MLA_V), jnp.float32), _sds((nh, s, 1), jnp.float32)],
        in_specs=[pl.BlockSpec((1, t, 256), lambda h, i, k: (h, i, 0)),
                  pl.BlockSpec((1, t, 256), lambda h, i, k: (h, jnp.minimum(k, i), 0)),
                  pl.BlockSpec((1, t, MLA_V), lambda h, i, k: (h, jnp.minimum(k, i), 0))],
        out_specs=[pl.BlockSpec((t, MLA_V), lambda h, i, k: (i, h)),
                   pl.BlockSpec((1, t, 1), lambda h, i, k: (h, i, 0))],
        scratch=[pltpu.VMEM((t, 1), jnp.float32), pltpu.VMEM((t, 1), jnp.float32),
                 pltpu.VMEM((t, MLA_V), jnp.float32)],
        sem=("parallel", "parallel", "arbitrary"))(qc, kc, v)


def _memkv_prep(mem, g_mem, w_mkv, g_mk):
    ml, d = mem.shape
    hw = MEM_HEADS * MEM_DIM

    def body(mem_ref, g_ref, w_ref, gk_ref, k_ref, v_ref, kv_ref, mn_ref):
        mn, _, _ = _norm_fwd(mem_ref[...], g_ref[...])
        mn_ref[...] = mn.astype(mn_ref.dtype)
        kv = _dot(mn, w_ref[...])
        kv_ref[...] = kv
        for h in range(MEM_HEADS):
            kn, _, _ = _norm_fwd(kv[:, MEM_DIM * h: MEM_DIM * (h + 1)], gk_ref[...])
            k_ref[:, MEM_DIM * h: MEM_DIM * (h + 1)] = kn.astype(k_ref.dtype)
        v_ref[...] = kv[:, hw:].astype(v_ref.dtype)

    vm = pl.BlockSpec(memory_space=pltpu.VMEM)
    return _pcall(
        body, name="memkv_prep",
        out_shape=[_sds((ml, hw), MXU), _sds((ml, hw), MXU), _sds((ml, 2 * hw), jnp.float32), _sds((ml, d), MXU)],
        in_specs=[vm] * 4, out_specs=[vm] * 4)(mem, g_mem, w_mkv, g_mk)


def _mem_fwd(proj, g_mq, km, vmm):
    s = proj.shape[0]
    ml, hw = km.shape
    tm = min(FFN_TILE, s)
    scale = MEM_DIM ** -0.5

    def body(q_ref, g_ref, k_ref, v_ref, y_ref, lse_ref):
        col = lax.broadcasted_iota(jnp.int32, (tm, MEM_HEADS), 1)
        lse_t = jnp.zeros((tm, MEM_HEADS), jnp.float32)
        for h in range(MEM_HEADS):
            sl = slice(MEM_DIM * h, MEM_DIM * (h + 1))
            qn, _, _ = _norm_fwd(q_ref[:, sl], g_ref[...])
            sc = _dot_nt(qn, k_ref[:, sl]) * scale
            m = jnp.max(sc, -1, keepdims=True)
            p = jnp.exp(sc - m)
            l = jnp.sum(p, -1, keepdims=True)
            y_ref[:, sl] = _dot(p, v_ref[:, sl]) / l
            lse_t = jnp.where(col == h, m + jnp.log(l), lse_t)
        lse_ref[...] = lse_t

    return _pcall(
        body, name="mem_fwd", grid=(s // tm,),
        out_shape=[_sds((s, hw), jnp.float32), _sds((s, MEM_HEADS), jnp.float32)],
        in_specs=[pl.BlockSpec((tm, hw), lambda i: (i, C_QM // hw)), pl.BlockSpec((1, MEM_DIM), lambda i: (0, 0)),
                  pl.BlockSpec((ml, hw), lambda i: (0, 0)), pl.BlockSpec((ml, hw), lambda i: (0, 0))],
        out_specs=[pl.BlockSpec((tm, hw), lambda i: (i, 0)), pl.BlockSpec((tm, MEM_HEADS), lambda i: (i, 0))],
        sem=("parallel",))(proj, g_mq, km, vmm)


def _alibi_slope(h):
    return float(2.0 ** (-8.0 * (h + 1) / SWA_Q_HEADS))


def _swa_common(n, kp, kc, vp, vc, pq, pkp, pkc, gk):
    b = SWA_BLOCK
    k_raw = jnp.concatenate([kp, kc], axis=0)
    kn, kxn, kr = _norm_fwd(k_raw, gk, half=True)
    v = jnp.concatenate([vp, vc], axis=0)
    dist = jnp.abs(pq - jnp.concatenate([pkp, pkc], axis=1))
    r_i = lax.broadcasted_iota(jnp.int32, (b, 2 * b), 0)
    c_i = lax.broadcasted_iota(jnp.int32, (b, 2 * b), 1)
    valid = (c_i > r_i) & (c_i <= r_i + b) & (c_i >= jnp.where(n > 0, 0, b))
    return kn, v, dist, valid


def _swa_specs(s):
    b = SWA_BLOCK
    prev = lambda n: jnp.maximum(n - 1, 0)
    return [
        pl.BlockSpec((b, 1024), lambda n: (n, C_QA // 1024)),
        pl.BlockSpec((b, LANES), lambda n: (prev(n), C_KA // LANES)),
        pl.BlockSpec((b, LANES), lambda n: (n, C_KA // LANES)),
        pl.BlockSpec((b, LANES), lambda n: (prev(n), C_VA // LANES)),
        pl.BlockSpec((b, LANES), lambda n: (n, C_VA // LANES)),
        pl.BlockSpec((b, 1), lambda n: (n, 0)),
        pl.BlockSpec((1, b), lambda n: (0, prev(n))),
        pl.BlockSpec((1, b), lambda n: (0, n)),
        pl.BlockSpec((1, LANES), lambda n: (0, 0)),
        pl.BlockSpec((1, LANES), lambda n: (0, 0)),
        pl.BlockSpec(memory_space=pltpu.SMEM),
    ]


def _swa_fwd(proj, posc, posr, gq, gk, sinks):
    s = proj.shape[0]
    b = SWA_BLOCK
    scale = SWA_DIM ** -0.5

    def body(q_ref, kp_ref, kc_ref, vp_ref, vc_ref, pq_ref, pkp_ref, pkc_ref, gq_ref, gk_ref, sink_ref,
             y_ref, lse_ref):
        n = pl.program_id(0)
        kn, v, dist, valid = _swa_common(n, kp_ref[...], kc_ref[...], vp_ref[...], vc_ref[...],
                                         pq_ref[...], pkp_ref[...], pkc_ref[...], gk_ref[...])
        lo = _lo_mask((b, LANES))
        col = lax.broadcasted_iota(jnp.int32, (b, SWA_Q_HEADS), 1)
        lse_t = jnp.zeros((b, SWA_Q_HEADS), jnp.float32)
        for j in range(SWA_Q_HEADS // 2):
            hk = (2 * j) // (SWA_Q_HEADS // SWA_KV_HEADS)
            kvmask = lo if hk == 0 else jnp.logical_not(lo)
            qn, _, _ = _norm_fwd(q_ref[:, LANES * j: LANES * (j + 1)], gq_ref[...], half=True)
            qsw = pltpu.roll(qn, 64, 1)
            outs = []
            for e in range(2):
                h = 2 * j + e
                qm = jnp.where(kvmask, qn if e == hk else qsw, 0.0)
                sc = _dot_nt(qm, kn) * scale - _alibi_slope(h) * dist
                sc = jnp.where(valid, sc, NEG_INF)
                sk = sink_ref[h]
                m = jnp.maximum(jnp.max(sc, -1, keepdims=True), sk)
                p = jnp.exp(sc - m)
                l = jnp.sum(p, -1, keepdims=True) + jnp.exp(sk - m)
                o = _dot(p, v) / l
                outs.append(o if e == hk else pltpu.roll(o, 64, 1))
                lse_t = jnp.where(col == h, m + jnp.log(l), lse_t)
            y_ref[:, LANES * j: LANES * (j + 1)] = jnp.where(lo, outs[0], outs[1])
        lse_ref[...] = lse_t

    return _pcall(
        body, name="swa_fwd", grid=(s // b,),
        out_shape=[_sds((s, 1024), jnp.float32), _sds((s, SWA_Q_HEADS), jnp.float32)],
        in_specs=_swa_specs(s),
        out_specs=[pl.BlockSpec((b, 1024), lambda n: (n, 0)), pl.BlockSpec((b, SWA_Q_HEADS), lambda n: (n, 0))],
        sem=("parallel",))(proj, proj, proj, proj, proj, posc, posr, posr, gq, gk, sinks)


def _out_proj(y_a, y_b, y_m, x, w_out, g_ffn):
    s, d = x.shape
    tm = min(ROW_TILE, s)

    def body(ya_ref, yb_ref, ym_ref, x_ref, w_ref, g_ref, h1_ref, fn_ref):
        y = jnp.concatenate([ya_ref[...].astype(MXU), yb_ref[...].astype(MXU), ym_ref[...].astype(MXU)], axis=1)
        h1 = x_ref[...] + _dot(y, w_ref[...])
        h1_ref[...] = h1
        fn, _, _ = _norm_fwd(h1, g_ref[...])
        fn_ref[...] = fn.astype(fn_ref.dtype)

    def row(width):
        return pl.BlockSpec((tm, width), lambda i: (i, 0))

    return _pcall(
        body, name="out_proj", grid=(s // tm,),
        out_shape=[_sds((s, d), jnp.float32), _sds((s, d), MXU)],
        in_specs=[row(1024), row(512), row(512), row(d), pl.BlockSpec(w_out.shape, lambda i: (0, 0)),
                  pl.BlockSpec((1, d), lambda i: (0, 0))],
        out_specs=[row(d), row(d)], sem=("parallel",))(y_a, y_b, y_m, x, w_out, g_ffn)


def _ffn_gu(fn, w_gu):
    s, d = fn.shape
    f = w_gu.shape[2]
    tm = min(FFN_TILE, s)

    def body(fn_ref, w_ref, gu_ref, act_ref):
        x = fn_ref[...]
        g = _dot_nt(x, w_ref[0, 0])
        u = _dot_nt(x, w_ref[0, 1])
        gu_ref[0, 0] = g
        gu_ref[0, 1] = u
        act_ref[0] = (g * jax.nn.sigmoid(g) * u).astype(act_ref.dtype)

    return _pcall(
        body, name="ffn_gate_up", grid=(N_DEV, s // tm),
        out_shape=[_sds((N_DEV, 2, s, f), jnp.float32), _sds((N_DEV, s, f), MXU)],
        in_specs=[pl.BlockSpec((tm, d), lambda j, i: (i, 0)),
                  pl.BlockSpec((1, 2, f, d), lambda j, i: (j, 0, 0, 0))],
        out_specs=[pl.BlockSpec((1, 2, tm, f), lambda j, i: (j, 0, i, 0)),
                   pl.BlockSpec((1, tm, f), lambda j, i: (j, i, 0))],
        sem=("parallel", "parallel"))(fn, w_gu)


def _ffn_down(act, w_d, h1, target):
    _, s, f = act.shape
    d = h1.shape[1]
    tm = min(FFN_TILE, s)

    def body(a_ref, w_ref, h1_ref, t_ref, dout_ref, loss_ref, acc):
        i, j = pl.program_id(0), pl.program_id(1)
        part = _dot(a_ref[0], w_ref[0]) + _dot(a_ref[1], w_ref[1])

        @pl.when(j == 0)
        def _():
            acc[...] = h1_ref[...] + part

        @pl.when(j > 0)
        def _():
            acc[...] += part

        @pl.when((i == 0) & (j == 0))
        def _():
            loss_ref[...] = jnp.zeros_like(loss_ref)

        @pl.when(j == N_DEV // 2 - 1)
        def _():
            diff = acc[...] - t_ref[...]
            dout_ref[...] = diff / d
            loss_ref[...] += 0.5 * jnp.sum(jnp.sum(diff * diff, -1, keepdims=True) / d)

    row = pl.BlockSpec((tm, d), lambda i, j: (i, 0))
    return _pcall(
        body, name="ffn_down", grid=(s // tm, N_DEV // 2),
        out_shape=[_sds((s, d), jnp.float32), _sds((8, LANES), jnp.float32)],
        in_specs=[pl.BlockSpec((2, tm, f), lambda i, j: (j, i, 0)), pl.BlockSpec((2, f, d), lambda i, j: (j, 0, 0)),
                  row, row],
        out_specs=[row, pl.BlockSpec((8, LANES), lambda i, j: (0, 0))],
        scratch=[pltpu.VMEM((tm, d), jnp.float32)], sem=("arbitrary", "arbitrary"))(act, w_d, h1, target)


def _ffn_bwd_act(dout, w_d, gu):
    s, d = dout.shape
    f = w_d.shape[1]
    tm = min(FFN_TILE, s)
    ni = s // tm

    def body(do_ref, w_ref, gu_ref, dgu_ref, dw_ref, acc):
        i = pl.program_id(1)
        do = do_ref[...].astype(MXU)
        d_act = _dot_nt(do, w_ref[0])
        g, u = gu_ref[0, 0], gu_ref[0, 1]
        sig = jax.nn.sigmoid(g)
        silu = g * sig
        dgu_ref[0, 0] = (d_act * u * (sig * (1.0 + g * (1.0 - sig)))).astype(dgu_ref.dtype)
        dgu_ref[0, 1] = (d_act * silu).astype(dgu_ref.dtype)
        part = _dot_tn(silu * u, do)

        @pl.when(i == 0)
        def _():
            acc[...] = part

        @pl.when(i > 0)
        def _():
            acc[...] += part

        @pl.when(i == ni - 1)
        def _():
            dw_ref[0] = acc[...].astype(dw_ref.dtype)

    return _pcall(
        body, name="ffn_bwd_act", grid=(N_DEV, ni),
        out_shape=[_sds((N_DEV, 2, s, f), MXU), _sds((N_DEV, f, d), WIRE)],
        in_specs=[pl.BlockSpec((tm, d), lambda j, i: (i, 0)), pl.BlockSpec((1, f, d), lambda j, i: (j, 0, 0)),
                  pl.BlockSpec((1, 2, tm, f), lambda j, i: (j, 0, i, 0))],
        out_specs=[pl.BlockSpec((1, 2, tm, f), lambda j, i: (j, 0, i, 0)),
                   pl.BlockSpec((1, f, d), lambda j, i: (j, 0, 0))],
        scratch=[pltpu.VMEM((f, d), jnp.float32)], sem=("parallel", "arbitrary"))(dout, w_d, gu)


def _ffn_dw_gu(fn, dgu):
    s, d = fn.shape
    f = dgu.shape[-1]
    tk = min(2 * FFN_TILE, s)
    nk = s // tk

    def body(fn_ref, dgu_ref, dw_ref, acc):
        k = pl.program_id(1)
        x = fn_ref[...]
        pg = _dot_tn(dgu_ref[0, 0], x)
        pu = _dot_tn(dgu_ref[0, 1], x)

        @pl.when(k == 0)
        def _():
            acc[0] = pg
            acc[1] = pu

        @pl.when(k > 0)
        def _():
            acc[0] += pg
            acc[1] += pu

        @pl.when(k == nk - 1)
        def _():
            dw_ref[0] = acc[...].astype(dw_ref.dtype)

    return _pcall(
        body, name="ffn_dw_gate_up", grid=(N_DEV, nk),
        out_shape=_sds((N_DEV, 2, f, d), WIRE),
        in_specs=[pl.BlockSpec((tk, d), lambda j, k: (k, 0)), pl.BlockSpec((1, 2, tk, f), lambda j, k: (j, 0, k, 0))],
        out_specs=pl.BlockSpec((1, 2, f, d), lambda j, k: (j, 0, 0, 0)),
        scratch=[pltpu.VMEM((2, f, d), jnp.float32)], sem=("parallel", "arbitrary"))(fn, dgu)


def _ffn_dfn(dgu, w_gu, after):
    _, _, s, f = dgu.shape
    d = w_gu.shape[3]
    tm = min(FFN_TILE, s)

    def body(dgu_ref, w_ref, dfn_ref):
        j = pl.program_id(1)
        part = (_dot(dgu_ref[0, 0], w_ref[0, 0]) + _dot(dgu_ref[0, 1], w_ref[0, 1])
                + _dot(dgu_ref[1, 0], w_ref[1, 0]) + _dot(dgu_ref[1, 1], w_ref[1, 1]))

        @pl.when(j == 0)
        def _():
            dfn_ref[...] = part

        @pl.when(j > 0)
        def _():
            dfn_ref[...] += part

    return _pcall(
        body, name="ffn_dfn", grid=(s // tm, N_DEV // 2),
        out_shape=_sds((s, d), jnp.float32),
        in_specs=[pl.BlockSpec((2, 2, tm, f), lambda i, j: (j, 0, i, 0)),
                  pl.BlockSpec((2, 2, f, d), lambda i, j: (j, 0, 0, 0))],
        out_specs=pl.BlockSpec((tm, d), lambda i, j: (i, 0)),
        sem=("parallel", "arbitrary"), after=after)(dgu, w_gu)


def _ffn_norm_bwd(d_fn, dout, h1, g_ffn):
    s, d = h1.shape
    tm = min(ROW_TILE, s)

    def body(dfn_ref, do_ref, h1_ref, g_ref, dh1_ref, dg_ref):
        i = pl.program_id(0)

        @pl.when(i == 0)
        def _():
            dg_ref[...] = jnp.zeros_like(dg_ref)

        _, xn, r = _norm_fwd(h1_ref[...], g_ref[...])
        dx, dg = _norm_bwd(xn, r, g_ref[...], dfn_ref[...])
        dh1_ref[...] = do_ref[...] + dx
        dg_ref[...] += dg

    row = pl.BlockSpec((tm, d), lambda i: (i, 0))
    vec = pl.BlockSpec((1, d), lambda i: (0, 0))
    return _pcall(
        body, name="ffn_norm_bwd", grid=(s // tm,),
        out_shape=[_sds((s, d), jnp.float32), _sds((1, d), jnp.float32)],
        in_specs=[row, row, row, vec], out_specs=[row, vec], sem=("arbitrary",))(d_fn, dout, h1, g_ffn)


def _mem_bwd(proj, g_mq, km, vmm, d_y, y_m, lse):
    s = proj.shape[0]
    ml, hw = km.shape
    tm = min(FFN_TILE, s)
    scale = MEM_DIM ** -0.5

    def body(q_ref, g_ref, k_ref, v_ref, do_ref, y_ref, lse_ref, dq_ref, dk_ref, dv_ref, dg_ref):
        i = pl.program_id(0)

        @pl.when(i == 0)
        def _():
            dk_ref[...] = jnp.zeros_like(dk_ref)
            dv_ref[...] = jnp.zeros_like(dv_ref)
            dg_ref[...] = jnp.zeros_like(dg_ref)

        col = lax.broadcasted_iota(jnp.int32, (tm, MEM_HEADS), 1)
        lse_t = lse_ref[...]
        for h in range(MEM_HEADS):
            sl = slice(MEM_DIM * h, MEM_DIM * (h + 1))
            qn, xn, r = _norm_fwd(q_ref[:, sl], g_ref[...])
            lse_h = jnp.sum(jnp.where(col == h, lse_t, 0.0), -1, keepdims=True)
            p = jnp.exp(_dot_nt(qn, k_ref[:, sl]) * scale - lse_h)
            do = do_ref[:, sl]
            dd = jnp.sum(do * y_ref[:, sl], -1, keepdims=True)
            dp = _dot_nt(do, v_ref[:, sl])
            ds = (p * (dp - dd)).astype(MXU)
            dv_ref[:, sl] += _dot_tn(p, do)
            dk_ref[:, sl] += _dot_tn(ds, qn) * scale
            dx, dg = _norm_bwd(xn, r, g_ref[...], _dot(ds, k_ref[:, sl]) * scale)
            dq_ref[:, sl] = dx.astype(dq_ref.dtype)
            dg_ref[...] += dg

    full = pl.BlockSpec((ml, hw), lambda i: (0, 0))
    return _pcall(
        body, name="mem_bwd", grid=(s // tm,),
        out_shape=[_sds((s, hw), MXU), _sds((ml, hw), jnp.float32), _sds((ml, hw), jnp.float32),
                   _sds((1, MEM_DIM), jnp.float32)],
        in_specs=[pl.BlockSpec((tm, hw), lambda i: (i, C_QM // hw)), pl.BlockSpec((1, MEM_DIM), lambda i: (0, 0)),
                  full, full, pl.BlockSpec((tm, hw), lambda i: (i, 3)), pl.BlockSpec((tm, hw), lambda i: (i, 0)),
                  pl.BlockSpec((tm, MEM_HEADS), lambda i: (i, 0))],
        out_specs=[pl.BlockSpec((tm, hw), lambda i: (i, 0)), full, full,
                   pl.BlockSpec((1, MEM_DIM), lambda i: (0, 0))],
        sem=("arbitrary",))(proj, g_mq, km, vmm, d_y, y_m, lse)


def _memkv_bwd(mem, g_mem, w_mkv, g_mk, kv, memn, dk, dv):
    ml, d = mem.shape
    hw = MEM_HEADS * MEM_DIM

    def body(mem_ref, g_ref, w_ref, gk_ref, kv_ref, mn_ref, dk_ref, dv_ref, dw_ref, dgm_ref, dgk_ref):
        parts = []
        dgk = jnp.zeros((1, MEM_DIM), jnp.float32)
        for h in range(MEM_HEADS):
            sl = slice(MEM_DIM * h, MEM_DIM * (h + 1))
            _, xn, r = _norm_fwd(kv_ref[:, sl], gk_ref[...])
            dx, dg = _norm_bwd(xn, r, gk_ref[...], dk_ref[:, sl])
            parts.append(dx)
            dgk = dgk + dg
        dkv = jnp.concatenate(parts + [dv_ref[...]], axis=1).astype(MXU)
        dgk_ref[...] = dgk
        dw_ref[...] = _dot_tn(mn_ref[...], dkv).astype(dw_ref.dtype)
        d_mn = _dot_nt(dkv, w_ref[...])
        _, xn, _ = _norm_fwd(mem_ref[...], g_ref[...])
        dgm_ref[...] = jnp.sum(d_mn * xn, 0, keepdims=True)

    vm = pl.BlockSpec(memory_space=pltpu.VMEM)
    return _pcall(
        body, name="memkv_bwd",
        out_shape=[_sds((d, 2 * hw), WIRE), _sds((1, d), jnp.float32), _sds((1, MEM_DIM), jnp.float32)],
        in_specs=[vm] * 8, out_specs=[vm] * 3)(mem, g_mem, w_mkv, g_mk, kv, memn, dk, dv)


def _mla_bwd(qc, kc, v, d_y, y_b, lse, after):
    nh, s, _ = qc.shape
    t = min(ATT_TILE, s)
    nb = s // t
    scale = (MLA_NOPE + MLA_ROPE) ** -0.5

    def body(q_ref, k_ref, v_ref, do_ref, y_ref, lse_ref, dq_ref, dk_ref, dv_ref, dk_acc, dv_acc):
        kj, qi = pl.program_id(1), pl.program_id(2)

        @pl.when((kj == 0) & (qi == 0))
        def _():
            dq_ref[...] = jnp.zeros_like(dq_ref)

        @pl.when(qi == kj)
        def _():
            dk_acc[...] = jnp.zeros_like(dk_acc)
            dv_acc[...] = jnp.zeros_like(dv_acc)

        @pl.when(qi >= kj)
        def _():
            q, k = q_ref[0], k_ref[0]
            sc = _dot_nt(q, k) * scale
            r_i = lax.broadcasted_iota(jnp.int32, sc.shape, 0) + qi * t
            c_i = lax.broadcasted_iota(jnp.int32, sc.shape, 1) + kj * t
            p = jnp.exp(jnp.where(c_i <= r_i, sc, NEG_INF) - lse_ref[0])
            do = do_ref[...]
            dd = jnp.sum(do * y_ref[...], -1, keepdims=True)
            dp = _dot_nt(do, v_ref[0])
            ds = (p * (dp - dd) * scale).astype(MXU)
            dv_acc[...] += _dot_tn(p, do)
            dk_acc[...] += _dot_tn(ds, q)
            rows = pl.ds(pl.multiple_of(qi * t, t), t)
            dq_ref[0, rows, :] += _dot(ds, k)

        @pl.when(qi == nb - 1)
        def _():
            dk_ref[0] = dk_acc[...]
            dv_ref[0] = dv_acc[...]

    qmap = lambda h, j, i: (h, jnp.maximum(i, j), 0)
    return _pcall(
        body, name="mla_bwd", grid=(nh, nb, nb),
        out_shape=[_sds((nh, s, 256), jnp.float32), _sds((nh, s, 256), jnp.float32),
                   _sds((nh, s, MLA_V), jnp.float32)],
        in_specs=[pl.BlockSpec((1, t, 256), qmap),
                  pl.BlockSpec((1, t, 256), lambda h, j, i: (h, j, 0)),
                  pl.BlockSpec((1, t, MLA_V), lambda h, j, i: (h, j, 0)),
                  pl.BlockSpec((t, MLA_V), lambda h, j, i: (jnp.maximum(i, j), 8 + h)),
                  pl.BlockSpec((t, MLA_V), lambda h, j, i: (jnp.maximum(i, j), h)),
                  pl.BlockSpec((1, t, 1), qmap)],
        out_specs=[pl.BlockSpec((1, s, 256), lambda h, j, i: (h, 0, 0)),
                   pl.BlockSpec((1, t, 256), lambda h, j, i: (h, j, 0)),
                   pl.BlockSpec((1, t, MLA_V), lambda h, j, i: (h, j, 0))],
        scratch=[pltpu.VMEM((t, 256), jnp.float32), pltpu.VMEM((t, MLA_V), jnp.float32)],
        sem=("parallel", "arbitrary", "arbitrary"), after=after)(qc, kc, v, d_y, y_b, lse)


def _mla_prep_bwd(proj, cos, sin, g_cq, g_ckv, w_uq, w_ukv, g_qn, g_qr, g_kn, g_kr,
                  qb, kvb, cqn, ckvn, dqc, dkc, dv):
    s = proj.shape[0]
    tm = min(ROW_TILE, s)
    nh = MLA_HEADS
    ni = s // tm

    def body(cq_ref, ckv_ref, kr_ref, cos_ref, sin_ref, gcq_ref, gckv_ref, wuq_ref, wukv_ref,
             gqn_ref, gqr_ref, gkn_ref, gkr_ref, qb_ref, kvb_ref, cqn_ref, ckvn_ref, dqc_ref, dkc_ref, dv_ref,
             dcq_ref, dckv_ref, dkr_ref, dwuq_ref, dwukv_ref,
             dgcq_ref, dgckv_ref, dgqn_ref, dgqr_ref, dgkn_ref, dgkr_ref, acc_uq, acc_ukv):
        i = pl.program_id(0)

        @pl.when(i == 0)
        def _():
            acc_uq[...] = jnp.zeros_like(acc_uq)
            acc_ukv[...] = jnp.zeros_like(acc_ukv)
            for ref in (dgcq_ref, dgckv_ref, dgqn_ref, dgqr_ref, dgkn_ref, dgkr_ref):
                ref[...] = jnp.zeros_like(ref)

        cos_t, sin_t = cos_ref[...], sin_ref[...]
        lo = _lo_mask((tm, LANES))
        qb_v, kvb_v = qb_ref[...], kvb_ref[...]
        dq_parts, dgqn = [], jnp.zeros((1, LANES), jnp.float32)
        for h in range(nh):
            _, xn, r = _norm_fwd(qb_v[:, MLA_NOPE * h: MLA_NOPE * (h + 1)], gqn_ref[...])
            dx, dg = _norm_bwd(xn, r, gqn_ref[...], dqc_ref[h][:, :MLA_NOPE])
            dq_parts.append(dx)
            dgqn = dgqn + dg
        dgqn_ref[...] += dgqn
        dgqr = jnp.zeros((1, LANES), jnp.float32)
        for j in range(nh // 2):
            d_rope = jnp.where(lo, dqc_ref[2 * j][:, MLA_NOPE:], dqc_ref[2 * j + 1][:, MLA_NOPE:])
            d_pre = _rope_bwd(d_rope, cos_t, sin_t)
            xr = qb_v[:, nh * MLA_NOPE + LANES * j: nh * MLA_NOPE + LANES * (j + 1)]
            _, xn, r = _norm_fwd(xr, gqr_ref[...], half=True)
            dx, dg = _norm_bwd(xn, r, gqr_ref[...], d_pre, half=True)
            dq_parts.append(dx)
            dgqr = dgqr + dg
        dgqr_ref[...] += dgqr
        dqb = jnp.concatenate(dq_parts, axis=1).astype(MXU)
        acc_uq[...] += _dot_tn(dqb, cqn_ref[...])
        _, xn, r = _norm_fwd(cq_ref[...], gcq_ref[...])
        dx, dg = _norm_bwd(xn, r, gcq_ref[...], _dot(dqb, wuq_ref[...]))
        dcq_ref[...] = dx.astype(dcq_ref.dtype)
        dgcq_ref[...] += dg
        dkv_parts, dgkn = [], jnp.zeros((1, LANES), jnp.float32)
        d_kr2 = jnp.zeros((tm, LANES), jnp.float32)
        for h in range(nh):
            _, xn, r = _norm_fwd(kvb_v[:, 256 * h: 256 * h + MLA_NOPE], gkn_ref[...])
            dx, dg = _norm_bwd(xn, r, gkn_ref[...], dkc_ref[h][:, :MLA_NOPE])
            dkv_parts += [dx, dv_ref[h]]
            dgkn = dgkn + dg
            d_kr2 = d_kr2 + dkc_ref[h][:, MLA_NOPE:]
        dgkn_ref[...] += dgkn
        dkvb = jnp.concatenate(dkv_parts, axis=1).astype(MXU)
        d_ckvn = jnp.zeros((tm, 512), jnp.float32)
        for dev in range(N_DEV):
            piece = dkvb[:, LANES * dev: LANES * (dev + 1)]
            acc_ukv[dev] += _dot_tn(ckvn_ref[...], piece)
            d_ckvn = d_ckvn + _dot_nt(piece, wukv_ref[dev])
        _, xn, r = _norm_fwd(ckv_ref[...], gckv_ref[...])
        dx, dg = _norm_bwd(xn, r, gckv_ref[...], d_ckvn)
        dckv_ref[...] = dx.astype(dckv_ref.dtype)
        dgckv_ref[...] += dg
        d_kr = jnp.where(lo, d_kr2 + pltpu.roll(d_kr2, 64, 1), 0.0)
        d_pre = _rope_bwd(d_kr, cos_t, sin_t)
        _, xn, r = _norm_fwd(kr_ref[...], gkr_ref[...], half=True)
        dx, dg = _norm_bwd(xn, r, gkr_ref[...], d_pre, half=True)
        dkr_ref[...] = jnp.where(lo, dx, 0.0).astype(dkr_ref.dtype)
        dgkr_ref[...] += jnp.where(_lo_mask((1, LANES)), dg, 0.0)

        @pl.when(i == ni - 1)
        def _():
            dwuq_ref[...] = acc_uq[...].astype(dwuq_ref.dtype)
            dwukv_ref[...] = acc_ukv[...].astype(dwukv_ref.dtype)

    def col(width, start):
        return pl.BlockSpec((tm, width), lambda i: (i, start // width))

    def full(shape):
        return pl.BlockSpec(shape, lambda i: (0,) * len(shape))

    def row(width):
        return pl.BlockSpec((tm, width), lambda i: (i, 0))

    def heads(width):
        return pl.BlockSpec((nh, tm, width), lambda i: (0, i, 0))

    vec = full((1, LANES))
    return _pcall(
        body, name="mla_prep_bwd", grid=(ni,),
        out_shape=[_sds((s, 512), MXU), _sds((s, 512), MXU), _sds((s, LANES), MXU),
                   _sds((768, 512), WIRE), _sds((N_DEV, 512, LANES), WIRE),
                   _sds((1, 512), jnp.float32), _sds((1, 512), jnp.float32)] + [_sds((1, LANES), jnp.float32)] * 4,
        in_specs=[col(512, C_CQ), col(512, C_CKV), col(LANES, C_KR), row(LANES), row(LANES),
                  full((1, 512)), full((1, 512)), full((768, 512)), full((N_DEV, 512, LANES)), vec, vec, vec, vec,
                  row(768), row(1024), row(512), row(512), heads(256), heads(256), heads(MLA_V)],
        out_specs=[row(512), row(512), row(LANES), full((768, 512)), full((N_DEV, 512, LANES)),
                   full((1, 512)), full((1, 512)), vec, vec, vec, vec],
        scratch=[pltpu.VMEM((768, 512), jnp.float32), pltpu.VMEM((N_DEV, 512, LANES), jnp.float32)],
        sem=("arbitrary",))(proj, proj, proj, cos, sin, g_cq, g_ckv, w_uq, w_ukv, g_qn, g_qr, g_kn, g_kr,
                            qb, kvb, cqn, ckvn, dqc, dkc, dv)


def _swa_bwd(proj, posc, posr, gq, gk, sinks, d_y, y_a, lse, after):
    s = proj.shape[0]
    b = SWA_BLOCK
    nb = s // b
    scale = SWA_DIM ** -0.5

    def body(q_ref, kp_ref, kc_ref, vp_ref, vc_ref, pq_ref, pkp_ref, pkc_ref, gq_ref, gk_ref, sink_ref,
             do_ref, y_ref, lse_ref, kfull_ref,
             dq_ref, dk_ref, dv_ref, dgq_ref, dgk_ref, dsink_ref, dk_acc, dv_acc):
        n = pl.program_id(0)

        @pl.when(n == 0)
        def _():
            dk_acc[...] = jnp.zeros_like(dk_acc)
            dv_acc[...] = jnp.zeros_like(dv_acc)
            dgq_ref[...] = jnp.zeros_like(dgq_ref)
            dsink_ref[...] = jnp.zeros_like(dsink_ref)

        kn, v, dist, valid = _swa_common(n, kp_ref[...], kc_ref[...], vp_ref[...], vc_ref[...],
                                         pq_ref[...], pkp_ref[...], pkc_ref[...], gk_ref[...])
        lo = _lo_mask((b, LANES))
        col = lax.broadcasted_iota(jnp.int32, (b, SWA_Q_HEADS), 1)
        col1 = lax.broadcasted_iota(jnp.int32, (1, SWA_Q_HEADS), 1)
        lse_t = lse_ref[...]
        dk_blk = jnp.zeros((2 * b, LANES), jnp.float32)
        dv_blk = jnp.zeros((2 * b, LANES), jnp.float32)
        dgq = jnp.zeros((1, LANES), jnp.float32)
        dsink = jnp.zeros((1, SWA_Q_HEADS), jnp.float32)
        for j in range(SWA_Q_HEADS // 2):
            hk = (2 * j) // (SWA_Q_HEADS // SWA_KV_HEADS)
            kvmask = lo if hk == 0 else jnp.logical_not(lo)
            sl = slice(LANES * j, LANES * (j + 1))
            qn, xn, r = _norm_fwd(q_ref[:, sl], gq_ref[...], half=True)
            qsw = pltpu.roll(qn, 64, 1)
            d2 = do_ref[:, sl]
            d2sw = pltpu.roll(d2, 64, 1)
            prod = d2 * y_ref[:, sl]
            dqs = []
            for e in range(2):
                h = 2 * j + e
                half_e = lo if e == 0 else jnp.logical_not(lo)
                qm = jnp.where(kvmask, qn if e == hk else qsw, 0.0)
                dm = jnp.where(kvmask, d2 if e == hk else d2sw, 0.0)
                sc = _dot_nt(qm, kn) * scale - _alibi_slope(h) * dist
                sc = jnp.where(valid, sc, NEG_INF)
                lse_h = jnp.sum(jnp.where(col == h, lse_t, 0.0), -1, keepdims=True)
                p = jnp.exp(sc - lse_h)
                dd = jnp.sum(jnp.where(half_e, prod, 0.0), -1, keepdims=True)
                dp = _dot_nt(dm, v)
                ds = (p * (dp - dd)).astype(MXU)
                dsink = dsink - jnp.where(col1 == h, jnp.sum(jnp.exp(sink_ref[h] - lse_h) * dd), 0.0)
                dq_m = _dot(ds, kn) * scale
                dk_blk = dk_blk + _dot_tn(ds, qm) * scale
                dv_blk = dv_blk + _dot_tn(p, dm)
                dqs.append(dq_m if e == hk else pltpu.roll(dq_m, 64, 1))
            dx, dg = _norm_bwd(xn, r, gq_ref[...], jnp.where(lo, dqs[0], dqs[1]), half=True)
            dq_ref[:, sl] = dx.astype(dq_ref.dtype)
            dgq = dgq + dg
        dgq_ref[...] += dgq
        dsink_ref[...] += dsink
        prev = pl.ds(pl.multiple_of(jnp.maximum(n - 1, 0) * b, b), b)
        cur = pl.ds(pl.multiple_of(n * b, b), b)
        dk_acc[prev, :] += dk_blk[:b]
        dv_acc[prev, :] += dv_blk[:b]
        dk_acc[cur, :] += dk_blk[b:]
        dv_acc[cur, :] += dv_blk[b:]

        @pl.when(n == nb - 1)
        def _():
            _, kxn, kr = _norm_fwd(kfull_ref[...], gk_ref[...], half=True)
            dx, dg = _norm_bwd(kxn, kr, gk_ref[...], dk_acc[...], half=True)
            dk_ref[...] = dx.astype(dk_ref.dtype)
            dv_ref[...] = dv_acc[...].astype(dv_ref.dtype)
            dgk_ref[...] = dg

    full = pl.BlockSpec((s, LANES), lambda n: (0, 0))
    vec = pl.BlockSpec((1, LANES), lambda n: (0, 0))
    return _pcall(
        body, name="swa_bwd", grid=(nb,),
        out_shape=[_sds((s, 1024), MXU), _sds((s, LANES), MXU), _sds((s, LANES), MXU),
                   _sds((1, LANES), jnp.float32), _sds((1, LANES), jnp.float32),
                   _sds((1, SWA_Q_HEADS), jnp.float32)],
        in_specs=_swa_specs(s) + [pl.BlockSpec((b, 1024), lambda n: (n, 0)), pl.BlockSpec((b, 1024), lambda n: (n, 0)),
                                  pl.BlockSpec((b, SWA_Q_HEADS), lambda n: (n, 0)),
                                  pl.BlockSpec((s, LANES), lambda n: (0, C_KA // LANES))],
        out_specs=[pl.BlockSpec((b, 1024), lambda n: (n, 0)), full, full, vec, vec,
                   pl.BlockSpec((1, SWA_Q_HEADS), lambda n: (0, 0))],
        scratch=[pltpu.VMEM((s, LANES), jnp.float32), pltpu.VMEM((s, LANES), jnp.float32)],
        sem=("arbitrary",), after=after)(proj, proj, proj, proj, proj, posc, posr, posr, gq, gk, sinks, d_y, y_a, lse,
                                         proj)


def _dx(d_proj, w_in, x, g, d_h1, after):
    s, d = x.shape
    n = w_in.shape[0]
    tm = min(ROW_TILE, s)

    def body(dp_ref, w_ref, x_ref, g_ref, dh_ref, dx_ref, dg_ref):
        i = pl.program_id(0)

        @pl.when(i == 0)
        def _():
            dg_ref[...] = jnp.zeros_like(dg_ref)

        d_hn = _dot(dp_ref[...], w_ref[...])
        _, xn, r = _norm_fwd(x_ref[...], g_ref[...])
        dx, dg = _norm_bwd(xn, r, g_ref[...], d_hn)
        dx_ref[...] = dh_ref[...] + dx
        dg_ref[...] += dg

    row = pl.BlockSpec((tm, d), lambda i: (i, 0))
    vec = pl.BlockSpec((1, d), lambda i: (0, 0))
    return _pcall(
        body, name="grad_x", grid=(s // tm,),
        out_shape=[_sds((s, d), jnp.float32), _sds((1, d), jnp.float32)],
        in_specs=[pl.BlockSpec((tm, n), lambda i: (i, 0)), pl.BlockSpec((n, d), lambda i: (0, 0)), row, vec, row],
        out_specs=[row, vec], sem=("arbitrary",), after=after)(d_proj, w_in, x, g, d_h1)


_SMALL = ["attn_norm_g", "swa_q_norm_g", "swa_k_norm_g", "swa_sinks", "mla_cq_norm_g", "mla_ckv_norm_g",
          "mla_qn_norm_g", "mla_qr_norm_g", "mla_kn_norm_g", "mla_kr_norm_g", "mem_norm_g",
          "mem_q_norm_g", "mem_k_norm_g", "ffn_norm_g"]


def _pack_rows(v):
    n = v.shape[-1]
    rows = -(-n // LANES)
    rows8 = -(-rows // 8) * 8
    flat = jnp.pad(v.reshape(-1), (0, rows8 * LANES - n))
    return flat.reshape(rows8, LANES)


def _pack(parts):
    return jnp.concatenate([_pack_rows(p) for p in parts], axis=0)


def _unpack(packed, sizes):
    out, r = [], 0
    for n in sizes:
        rows = -(-n // LANES)
        rows8 = -(-rows // 8) * 8
        out.append(packed[r:r + rows8].reshape(-1)[:n].reshape(1, n))
        r += rows8
    return out


def _fold64(v):
    return v[:, :64] + v[:, 64:]


def kernel(x, mem, positions, attn_norm_g, w_in, swa_q_norm_g, swa_k_norm_g, swa_sinks, mla_cq_norm_g, mla_ckv_norm_g, w_uq, w_ukv, mla_qn_norm_g, mla_qr_norm_g, mla_kn_norm_g, mla_kr_norm_g, mem_norm_g, w_mem_kv, mem_q_norm_g, mem_k_norm_g, w_out, ffn_norm_g, w_gate, w_up, w_down, loss_target, m_attn_norm_g, m_w_in, m_swa_q_norm_g, m_swa_k_norm_g, m_swa_sinks, m_mla_cq_norm_g, m_mla_ckv_norm_g, m_w_uq, m_w_ukv, m_mla_qn_norm_g, m_mla_qr_norm_g, m_mla_kn_norm_g, m_mla_kr_norm_g, m_mem_norm_g, m_w_mem_kv, m_mem_q_norm_g, m_mem_k_norm_g, m_w_out, m_ffn_norm_g, m_w_gate, m_w_up, m_w_down, v_attn_norm_g, v_w_in, v_swa_q_norm_g, v_swa_k_norm_g, v_swa_sinks, v_mla_cq_norm_g, v_mla_ckv_norm_g, v_w_uq, v_w_ukv, v_mla_qn_norm_g, v_mla_qr_norm_g, v_mla_kn_norm_g, v_mla_kr_norm_g, v_mem_norm_g, v_w_mem_kv, v_mem_q_norm_g, v_mem_k_norm_g, v_w_out, v_ffn_norm_g, v_w_gate, v_w_up, v_w_down):
    args = dict(locals())
    x2, mem2, tgt = x[0], mem[0], loss_target[0]
    s, d = x2.shape
    n_in = w_in.shape[2]
    f = w_gate.shape[2]

    shards = [w_in[0].T.astype(WIRE), w_uq[0].T.astype(WIRE), w_ukv[0].astype(WIRE), w_mem_kv[0].astype(WIRE),
              w_out[0].astype(WIRE), jnp.stack([w_gate[0].T, w_up[0].T]).astype(WIRE), w_down[0].astype(WIRE)]
    g_in, g_uq, wkv, g_mkv, g_out = _all_gather(shards[:5])
    w_gu, w_d = _all_gather_background(shards[5:], 1, "all_gather_ffn_weights")
    wi = g_in.reshape(N_DEV * n_in, d)
    wi = jnp.concatenate([wi[0:1024], wi[1280:1792], wi[1792:2304], wi[2368:2880],
                          wi[1024:1152], wi[1152:1280], wi[2304:2368],
                          jnp.zeros((IN_PAD - 2880, d), wi.dtype)], axis=0)
    wq = g_uq.reshape(768, 512)
    wq = jnp.concatenate([wq[192 * h: 192 * h + 128] for h in range(4)]
                         + [wq[192 * h + 128: 192 * (h + 1)] for h in range(4)], axis=0)
    wmkv = g_mkv.reshape(-1, g_mkv.shape[-1])
    wo = g_out.reshape(-1, d)

    pos = positions[0].astype(jnp.float32)
    inv_freq = ROPE_THETA ** (-jnp.arange(0, MLA_ROPE, 2, dtype=jnp.float32) / MLA_ROPE)
    ang = pos[:, None] * inv_freq
    cos32, sin32 = jnp.cos(ang), jnp.sin(ang)
    cos_t = jnp.tile(cos32, (1, 4))
    sin_t = jnp.tile(jnp.concatenate([-sin32, sin32], axis=1), (1, 2))
    posc, posr = pos.reshape(s, 1), pos.reshape(1, s)
    two = lambda g: jnp.tile(g, (1, 2))
    gq2, gk2, gqr2, gkr2 = two(swa_q_norm_g), two(swa_k_norm_g), two(mla_qr_norm_g), two(mla_kr_norm_g)
    sinks1 = swa_sinks[0]

    proj, hn = _in_proj(x2, attn_norm_g, wi)
    qc, kc, vb, qb, kvb, cqn, ckvn = _mla_prep(proj, cos_t, sin_t, mla_cq_norm_g, mla_ckv_norm_g, wq, wkv,
                                                mla_qn_norm_g, gqr2, mla_kn_norm_g, gkr2)
    y_b, lse_b = _mla_fwd(qc, kc, vb)
    km, vmm, kvm, memn = _memkv_prep(mem2, mem_norm_g, wmkv, mem_k_norm_g)
    y_m, lse_m = _mem_fwd(proj, mem_q_norm_g, km, vmm)
    y_a, lse_a = _swa_fwd(proj, posc, posr, gq2, gk2, sinks1)
    h1, fn = _out_proj(y_a, y_b, y_m, x2, wo, ffn_norm_g)
    gu, act = _ffn_gu(fn, w_gu)
    dout, loss_tile = _ffn_down(act, w_d, h1, tgt)

    dgu, dw_d = _ffn_bwd_act(dout, w_d, gu)
    dw_gu = _ffn_dw_gu(fn, dgu)
    r_gu, r_d = _exchange_grads_background([dw_gu, dw_d], 2, "exchange_ffn_grads")
    d_h1, dg_ffn = _ffn_norm_bwd(_ffn_dfn(dgu, w_gu, dw_gu), dout, h1, ffn_norm_g)
    d_y = _mm(d_h1, wo, tb=True, out_dtype=jnp.float32, tm=FFN_TILE, tk=512, name="d_mix")
    dw_out = jnp.concatenate([
        _mm(y_a, d_h1, ta=True, out_dtype=WIRE, tm=1024, tk=512, name="dw_out_a"),
        _mm(y_b, d_h1, ta=True, out_dtype=WIRE, tm=1024, tk=512, name="dw_out_b"),
        _mm(y_m, d_h1, ta=True, out_dtype=WIRE, tm=1024, tk=512, name="dw_out_m")], axis=0)
    d_qm, dkm, dvmm, dg_mq = _mem_bwd(proj, mem_q_norm_g, km, vmm, d_y, y_m, lse_m)
    dw_mkv, dg_mem, dg_mk = _memkv_bwd(mem2, mem_norm_g, wmkv, mem_k_norm_g, kvm, memn, dkm, dvmm)
    r_mkv, r_out = _exchange_grads_background([dw_mkv.reshape(g_mkv.shape), dw_out.reshape(g_out.shape)], 3,
                                              "exchange_mix_grads")
    dqc, dkc, dvb = _mla_bwd(qc, kc, vb, d_y, y_b, lse_b, dw_mkv)
    (d_cq, d_ckv, d_kr, dw_uq, dw_ukv, dg_cq, dg_ckv, dg_qn, dg_qr, dg_kn, dg_kr) = _mla_prep_bwd(
        proj, cos_t, sin_t, mla_cq_norm_g, mla_ckv_norm_g, wq, wkv, mla_qn_norm_g, gqr2, mla_kn_norm_g, gkr2,
        qb, kvb, cqn, ckvn, dqc, dkc, dvb)
    d_qa, d_ka, d_va, dg_q, dg_k, d_sinks = _swa_bwd(proj, posc, posr, gq2, gk2, sinks1, d_y, y_a, lse_a, dw_out)
    d_proj = jnp.concatenate([d_qa, d_cq, d_ckv, d_qm, d_ka, d_va, d_kr], axis=1)
    gi = _dw_in(hn, d_proj, n_in)

    gq_ = jnp.concatenate(sum([[dw_uq[128 * h: 128 * (h + 1)], dw_uq[512 + 64 * h: 512 + 64 * (h + 1)]]
                               for h in range(4)], []), axis=0)
    gq_ = gq_.reshape(N_DEV, 96, 512)
    r_in, r_uq, r_ukv = _exchange_grads_background([gi, gq_, dw_ukv], 4, "exchange_in_grads")
    grad_x, dg_attn = _dx(d_proj, wi, x2, attn_norm_g, d_h1, gi)

    big = {}
    def adam(name, r, transposed=False):
        w, m, v = args[name][0], args["m_" + name][0], args["v_" + name][0]
        if transposed:
            outs = _adam_big(r, w.T, m.T, v.T, "adam_" + name)
            return [o.T[None] for o in outs]
        return [o[None] for o in _adam_big(r, w, m, v, "adam_" + name)]
    big["w_in"] = adam("w_in", r_in, True)
    big["w_uq"] = adam("w_uq", r_uq, True)
    big["w_ukv"] = adam("w_ukv", r_ukv)
    big["w_mem_kv"] = adam("w_mem_kv", r_mkv)
    big["w_out"] = adam("w_out", r_out)
    big["w_down"] = adam("w_down", r_d)
    big["w_gate"] = adam("w_gate", r_gu[:, 0], True)
    big["w_up"] = adam("w_up", r_gu[:, 1], True)

    small_g = {
        "attn_norm_g": dg_attn, "swa_q_norm_g": _fold64(dg_q), "swa_k_norm_g": _fold64(dg_k),
        "swa_sinks": d_sinks, "mla_cq_norm_g": dg_cq, "mla_ckv_norm_g": dg_ckv, "mla_qn_norm_g": dg_qn,
        "mla_qr_norm_g": _fold64(dg_qr), "mla_kn_norm_g": dg_kn, "mla_kr_norm_g": _fold64(dg_kr),
        "mem_norm_g": dg_mem, "mem_q_norm_g": dg_mq, "mem_k_norm_g": dg_mk, "ffn_norm_g": dg_ffn}
    sizes = [args[n].shape[-1] for n in _SMALL]
    pg = _pack([small_g[n] for n in _SMALL] + [loss_tile[0:1, 0:1]])
    zero = jnp.zeros((1, 1), jnp.float32)
    pw = _pack([args[n] for n in _SMALL] + [zero])
    pm = _pack([args["m_" + n] for n in _SMALL] + [zero])
    pv = _pack([args["v_" + n] for n in _SMALL] + [zero])
    sg, sd, sm, sv = _small_allreduce_adam(pg, pw, pm, pv)
    small = {n: vals for n, vals in zip(_SMALL, zip(*[_unpack(p, sizes) for p in (sg, sd, sm, sv)]))}
    loss = _unpack(sg, sizes + [1])[-1].reshape(())

    order = ["attn_norm_g", "w_in", "swa_q_norm_g", "swa_k_norm_g", "swa_sinks", "mla_cq_norm_g", "mla_ckv_norm_g",
             "w_uq", "w_ukv", "mla_qn_norm_g", "mla_qr_norm_g", "mla_kn_norm_g", "mla_kr_norm_g", "mem_norm_g",
             "w_mem_kv", "mem_q_norm_g", "mem_k_norm_g", "w_out", "ffn_norm_g", "w_gate", "w_up", "w_down"]
    res = {n: (big[n] if n in big else list(small[n])) for n in order}
    outs = [loss, grad_x[None]]
    for kind in range(4):
        outs += [res[n][kind] for n in order]
    return tuple(outs)
```

```python
import jax
import jax.numpy as jnp
from jax import lax
from jax.experimental import pallas as pl
from jax.experimental.pallas import tpu as pltpu
from jax.experimental.pallas import tpu_sc as plsc

MXU = jnp.bfloat16
WIRE = jnp.bfloat16
EPS = 1e-6
NEG_INF = -1e30
N_DEV = 8
LANES = 128
ROW_TILE = 256
FFN_TILE = 512
ATT_TILE = 1024
SWA_BLOCK = 128
VMEM_LIMIT = 56 * 1024 * 1024

SWA_Q_HEADS, SWA_KV_HEADS, SWA_DIM = 16, 2, 64
MLA_HEADS, MLA_NOPE, MLA_ROPE, MLA_V = 4, 128, 64, 128
MEM_HEADS, MEM_DIM = 4, 128
ROPE_THETA = 10000.0
ADAM_LR, ADAM_B1, ADAM_B2, ADAM_EPS, ADAM_WD, ADAM_STEP = 0.001, 0.9, 0.999, 1e-08, 0.01, 10

C_QA, C_CQ, C_CKV, C_QM, C_KA, C_VA, C_KR, IN_PAD = 0, 1024, 1536, 2048, 2560, 2688, 2816, 2944


def _pcall(body, *, name, out_shape, in_specs, out_specs, grid=(), scratch=(), sem=None, after=None):
    params = pltpu.CompilerParams(dimension_semantics=sem, vmem_limit_bytes=VMEM_LIMIT)
    if after is not None:
        n_in, inner = len(in_specs), body

        def body(*refs):
            inner(*refs[:n_in], *refs[n_in + 1:])

        in_specs = list(in_specs) + [pl.BlockSpec(memory_space=pl.ANY)]
    call = pl.pallas_call(body, name=name, grid=grid, in_specs=in_specs, out_specs=out_specs,
                          out_shape=out_shape, scratch_shapes=list(scratch), compiler_params=params)
    return call if after is None else (lambda *ops: call(*ops, after))


def _sds(shape, dtype):
    return jax.ShapeDtypeStruct(tuple(shape), dtype)


def _dot(a, b):
    return jnp.dot(a.astype(MXU), b.astype(MXU), preferred_element_type=jnp.float32)


def _dot_nt(a, b):
    return lax.dot_general(a.astype(MXU), b.astype(MXU), (((1,), (1,)), ((), ())),
                           preferred_element_type=jnp.float32)


def _dot_tn(a, b):
    return lax.dot_general(a.astype(MXU), b.astype(MXU), (((0,), (0,)), ((), ())),
                           preferred_element_type=jnp.float32)


def _lo_mask(shape):
    return (lax.broadcasted_iota(jnp.int32, shape, len(shape) - 1) % LANES) < 64


def _norm_fwd(x, g, half=False):
    x2 = x * x
    if half:
        lo = _lo_mask(x.shape)
        s_lo = jnp.sum(jnp.where(lo, x2, 0.0), -1, keepdims=True)
        s_hi = jnp.sum(jnp.where(lo, 0.0, x2), -1, keepdims=True)
        r = jnp.where(lo, lax.rsqrt(s_lo / 64.0 + EPS), lax.rsqrt(s_hi / 64.0 + EPS))
    else:
        r = lax.rsqrt(jnp.mean(x2, -1, keepdims=True) + EPS)
    xn = x * r
    return xn * g, xn, r


def _norm_bwd(xn, r, g, dy, half=False):
    t = dy * g
    tx = t * xn
    if half:
        lo = _lo_mask(xn.shape)
        m_lo = jnp.sum(jnp.where(lo, tx, 0.0), -1, keepdims=True) / 64.0
        m_hi = jnp.sum(jnp.where(lo, 0.0, tx), -1, keepdims=True) / 64.0
        m = jnp.where(lo, m_lo, m_hi)
    else:
        m = jnp.mean(tx, -1, keepdims=True)
    dx = r * (t - xn * m)
    dg = jnp.sum(dy * xn, 0, keepdims=True)
    return dx, dg


def _swap32(x):
    lane = lax.broadcasted_iota(jnp.int32, x.shape, 1)
    return jnp.where((lane % 64) < 32, pltpu.roll(x, 96, 1), pltpu.roll(x, 32, 1))


def _rope(x, cos, sin):
    return x * cos + _swap32(x) * sin


def _rope_bwd(d, cos, sin):
    return d * cos + _swap32(d * sin)


def _my_coords():
    return lax.axis_index("x"), lax.axis_index("y"), lax.axis_index("c")


def _dev_index(px, py, pc):
    return 4 * px + 2 * py + pc


_FLIPS = [(0, 0, 1), (0, 1, 0), (0, 1, 1), (1, 0, 0), (1, 0, 1), (1, 1, 0), (1, 1, 1)]


def _flip(coords, f):
    return tuple((1 - v) if b else v for v, b in zip(coords, f))


def _all_gather(shards):
    n = len(shards)

    def body(*refs):
        ins, outs = refs[:n], refs[n:2 * n]
        send_sems, recv_sems, local_sems = refs[2 * n:]
        x, y, c = _my_coords()
        me, sibling = (x, y, c), (x, y, 1 - c)
        chips = [(1 - x, y), (x, 1 - y), (1 - x, 1 - y)]

        def copy(w, k, block, to, src=None):
            dst = outs[w].at[_dev_index(*block)]
            return pltpu.make_async_remote_copy(
                src_ref=dst if src is None else src, dst_ref=dst,
                send_sem=send_sems.at[w, k], recv_sem=recv_sems.at[w, k],
                device_id=to, device_id_type=pl.DeviceIdType.MESH)

        sends, locals_ = [], []
        for w in range(n):
            mine = pltpu.make_async_copy(ins[w], outs[w].at[_dev_index(*me)], local_sems.at[w])
            mine.start()
            locals_.append(mine)
            first = [copy(w, 0, me, sibling, src=ins[w])]
            first += [copy(w, 1 + j, me, (*chip, c), src=ins[w]) for j, chip in enumerate(chips)]
            for cp in first:
                cp.start()
            sends += first
        for w in range(n):
            for j, chip in enumerate(chips):
                copy(w, 1 + j, (*chip, c), me).wait_recv()
                fwd = copy(w, 4 + j, (*chip, c), sibling)
                fwd.start()
                sends.append(fwd)
        for w in range(n):
            copy(w, 0, sibling, me).wait_recv()
            for j, chip in enumerate(chips):
                copy(w, 4 + j, (*chip, 1 - c), me).wait_recv()
        for cp in sends:
            cp.wait_send()
        for mine in locals_:
            mine.wait()

    any_spec = pl.BlockSpec(memory_space=pl.ANY)
    return _pcall(
        body, name="all_gather_weights",
        out_shape=[_sds((N_DEV,) + s.shape, s.dtype) for s in shards],
        in_specs=[any_spec] * n, out_specs=[any_spec] * n,
        scratch=[pltpu.SemaphoreType.DMA((n, 7)), pltpu.SemaphoreType.DMA((n, 7)),
                 pltpu.SemaphoreType.DMA((n,))])(*shards)


def _wire_cost(arrays):
    nbytes = sum(a.size * a.dtype.itemsize for a in arrays)
    return pl.CostEstimate(flops=0, transcendentals=0, bytes_accessed=40 * nbytes)


def _all_gather_background(shards, collective_id, name):
    n = len(shards)
    src_refs = [jax.new_ref(s, memory_space=pltpu.MemorySpace.HBM) for s in shards]
    out_refs = [jax.empty_ref(_sds((N_DEV,) + s.shape, s.dtype), memory_space=pltpu.MemorySpace.HBM) for s in shards]

    @pl.kernel(mesh=plsc.ScalarSubcoreMesh(axis_name="seq", num_cores=1), name=name,
               scratch_types=(pltpu.SemaphoreType.DMA((n, 7)), pltpu.SemaphoreType.DMA((n, 7)),
                              pltpu.SemaphoreType.DMA((n,))),
               compiler_params=pltpu.CompilerParams(collective_id=collective_id))
    def launch(send_sems, recv_sems, local_sems):
        x, y, c = _my_coords()
        me, sibling = (x, y, c), (x, y, 1 - c)
        chips = [(1 - x, y), (x, 1 - y), (1 - x, 1 - y)]
        barrier = pltpu.get_barrier_semaphore()
        for peer in [sibling] + [(*chip, c) for chip in chips]:
            pl.semaphore_signal(barrier, inc=1, device_id=peer, device_id_type=pl.DeviceIdType.MESH)
        pl.semaphore_wait(barrier, 4)

        def copy(w, k, block, to, src=None):
            dst = out_refs[w].at[_dev_index(*block)]
            return pltpu.make_async_remote_copy(
                src_ref=dst if src is None else src, dst_ref=dst,
                send_sem=send_sems.at[w, k], recv_sem=recv_sems.at[w, k],
                device_id=to, device_id_type=pl.DeviceIdType.MESH)

        sends, locals_ = [], []
        for w in range(n):
            mine = pltpu.make_async_copy(src_refs[w], out_refs[w].at[_dev_index(*me)], local_sems.at[w])
            mine.start()
            locals_.append(mine)
            first = [copy(w, 0, me, sibling, src=src_refs[w])]
            first += [copy(w, 1 + j, me, (*chip, c), src=src_refs[w]) for j, chip in enumerate(chips)]
            for cp in first:
                cp.start()
            sends += first
        for w in range(n):
            for j, chip in enumerate(chips):
                copy(w, 1 + j, (*chip, c), me).wait_recv()
                fwd = copy(w, 4 + j, (*chip, c), sibling)
                fwd.start()
                sends.append(fwd)
        for w in range(n):
            copy(w, 0, sibling, me).wait_recv()
            for j, chip in enumerate(chips):
                copy(w, 4 + j, (*chip, 1 - c), me).wait_recv()
        for cp in sends:
            cp.wait_send()
        for mine in locals_:
            mine.wait()

    launch()
    return [r[...] for r in out_refs]


def _exchange_grads(grads):
    n = len(grads)

    def body(*refs):
        ins, outs = refs[:n], refs[n:2 * n]
        send_sems, recv_sems, local_sems = refs[2 * n:]
        me = _my_coords()
        my_idx = _dev_index(*me)
        sends, locals_ = [], []
        for w in range(n):
            mine = pltpu.make_async_copy(ins[w].at[my_idx], outs[w].at[my_idx], local_sems.at[w])
            mine.start()
            locals_.append(mine)
            for k, f in enumerate(_FLIPS):
                peer = _flip(me, f)
                cp = pltpu.make_async_remote_copy(
                    src_ref=ins[w].at[_dev_index(*peer)], dst_ref=outs[w].at[my_idx],
                    send_sem=send_sems.at[w, k], recv_sem=recv_sems.at[w, k],
                    device_id=peer, device_id_type=pl.DeviceIdType.MESH)
                cp.start()
                sends.append(cp)
        for w in range(n):
            for k, f in enumerate(_FLIPS):
                peer = _flip(me, f)
                slot = outs[w].at[_dev_index(*peer)]
                pltpu.make_async_remote_copy(
                    src_ref=slot, dst_ref=slot,
                    send_sem=send_sems.at[w, k], recv_sem=recv_sems.at[w, k],
                    device_id=peer, device_id_type=pl.DeviceIdType.MESH).wait_recv()
        for cp in sends:
            cp.wait_send()
        for mine in locals_:
            mine.wait()

    any_spec = pl.BlockSpec(memory_space=pl.ANY)
    return _pcall(
        body, name="exchange_grads",
        out_shape=[_sds(g.shape, g.dtype) for g in grads],
        in_specs=[any_spec] * n, out_specs=[any_spec] * n,
        scratch=[pltpu.SemaphoreType.DMA((n, 7)), pltpu.SemaphoreType.DMA((n, 7)),
                 pltpu.SemaphoreType.DMA((n,))])(*grads)


def _exchange_grads_background(grads, collective_id, name):
    n = len(grads)
    src_refs = [jax.new_ref(g, memory_space=pltpu.MemorySpace.HBM) for g in grads]
    out_refs = [jax.empty_ref(_sds(g.shape, g.dtype), memory_space=pltpu.MemorySpace.HBM) for g in grads]

    @pl.kernel(mesh=plsc.ScalarSubcoreMesh(axis_name="seq", num_cores=1), name=name,
               scratch_types=(pltpu.SemaphoreType.DMA((n, 7)), pltpu.SemaphoreType.DMA((n, 7)),
                              pltpu.SemaphoreType.DMA((n,))),
               cost_estimate=_wire_cost(grads),
               compiler_params=pltpu.CompilerParams(collective_id=collective_id))
    def launch(send_sems, recv_sems, local_sems):
        me = _my_coords()
        my_idx = _dev_index(*me)
        peers = [_flip(me, f) for f in _FLIPS]
        barrier = pltpu.get_barrier_semaphore()
        for peer in peers:
            pl.semaphore_signal(barrier, inc=1, device_id=peer, device_id_type=pl.DeviceIdType.MESH)
        pl.semaphore_wait(barrier, len(peers))
        sends, locals_ = [], []
        for w in range(n):
            mine = pltpu.make_async_copy(src_refs[w].at[my_idx], out_refs[w].at[my_idx], local_sems.at[w])
            mine.start()
            locals_.append(mine)
            for k, peer in enumerate(peers):
                cp = pltpu.make_async_remote_copy(
                    src_ref=src_refs[w].at[_dev_index(*peer)], dst_ref=out_refs[w].at[my_idx],
                    send_sem=send_sems.at[w, k], recv_sem=recv_sems.at[w, k],
                    device_id=peer, device_id_type=pl.DeviceIdType.MESH)
                cp.start()
                sends.append(cp)
        for w in range(n):
            for k, peer in enumerate(peers):
                slot = out_refs[w].at[_dev_index(*peer)]
                pltpu.make_async_remote_copy(
                    src_ref=slot, dst_ref=slot, send_sem=send_sems.at[w, k], recv_sem=recv_sems.at[w, k],
                    device_id=peer, device_id_type=pl.DeviceIdType.MESH).wait_recv()
        for cp in sends:
            cp.wait_send()
        for mine in locals_:
            mine.wait()

    launch()
    return [r[...] for r in out_refs]


def _to_wire(parts, after, name):
    n = len(parts)
    rows, cols = parts[0].shape
    tr = rows // 2 if rows % 32 == 0 else rows

    def body(*refs):
        for k in range(n):
            refs[n][k] = refs[k][...].astype(WIRE)

    blk = pl.BlockSpec((tr, cols), lambda i: (i, 0))
    return _pcall(
        body, name=name, grid=(rows // tr,), out_shape=_sds((n, rows, cols), WIRE),
        in_specs=[blk] * n, out_specs=pl.BlockSpec((n, tr, cols), lambda i: (0, i, 0)),
        sem=("parallel",), after=after)(*parts)


def _adam_math(w, g, m, v):
    m = ADAM_B1 * m + (1.0 - ADAM_B1) * g
    v = ADAM_B2 * v + (1.0 - ADAM_B2) * (g * g)
    m_hat = m / (1.0 - ADAM_B1 ** ADAM_STEP)
    v_hat = v / (1.0 - ADAM_B2 ** ADAM_STEP)
    delta = -ADAM_LR * (m_hat / (jnp.sqrt(v_hat) + ADAM_EPS) + ADAM_WD * w)
    return delta, m, v


def _small_allreduce_adam(pg, pw, pm, pv):
    rows = pg.shape[0]

    def body(pg_ref, pw_ref, pm_ref, pv_ref, g_ref, d_ref, m_ref, v_ref, gath, send_sems, recv_sems):
        me = _my_coords()
        my_idx = _dev_index(*me)
        gath[my_idx] = pg_ref[...]
        sends = []
        for k, f in enumerate(_FLIPS):
            peer = _flip(me, f)
            cp = pltpu.make_async_remote_copy(
                src_ref=pg_ref, dst_ref=gath.at[my_idx],
                send_sem=send_sems.at[k], recv_sem=recv_sems.at[k],
                device_id=peer, device_id_type=pl.DeviceIdType.MESH)
            cp.start()
            sends.append(cp)
        for k, f in enumerate(_FLIPS):
            peer = _flip(me, f)
            slot = gath.at[_dev_index(*peer)]
            pltpu.make_async_remote_copy(
                src_ref=slot, dst_ref=slot, send_sem=send_sems.at[k], recv_sem=recv_sems.at[k],
                device_id=peer, device_id_type=pl.DeviceIdType.MESH).wait_recv()
        for cp in sends:
            cp.wait_send()
        g = gath[0]
        for d in range(1, N_DEV):
            g = g + gath[d]
        delta, m, v = _adam_math(pw_ref[...], g, pm_ref[...], pv_ref[...])
        g_ref[...] = g
        d_ref[...] = delta
        m_ref[...] = m
        v_ref[...] = v

    vm = pl.BlockSpec(memory_space=pltpu.VMEM)
    return _pcall(
        body, name="small_allreduce_adam",
        out_shape=[_sds(pg.shape, jnp.float32)] * 4,
        in_specs=[vm] * 4, out_specs=[vm] * 4,
        scratch=[pltpu.VMEM((N_DEV, rows, LANES), jnp.float32),
                 pltpu.SemaphoreType.DMA((7,)), pltpu.SemaphoreType.DMA((7,))])(pg, pw, pm, pv)


def _adam_big(recv, w, m, v, name, after=None):
    _, rows, cols = recv.shape
    tc = 512 if cols % 512 == 0 else cols
    tr = rows
    while tr * tc > 256 * 1024 and tr % 2 == 0 and (tr // 2) % 16 == 0:
        tr //= 2

    def body(r_ref, w_ref, m_ref, v_ref, g_ref, d_ref, mo_ref, vo_ref):
        g = r_ref[0].astype(jnp.float32)
        for d in range(1, N_DEV):
            g = g + r_ref[d].astype(jnp.float32)
        delta, mn, vn = _adam_math(w_ref[...], g, m_ref[...], v_ref[...])
        g_ref[...] = g
        d_ref[...] = delta
        mo_ref[...] = mn
        vo_ref[...] = vn

    blk = pl.BlockSpec((tr, tc), lambda i, j: (i, j))
    return _pcall(
        body, name=name, grid=(rows // tr, cols // tc),
        out_shape=[_sds((rows, cols), jnp.float32)] * 4,
        in_specs=[pl.BlockSpec((N_DEV, tr, tc), lambda i, j: (0, i, j)), blk, blk, blk],
        out_specs=[blk] * 4, sem=("parallel", "parallel"), after=after)(recv, w, m, v)


def _mm(a, b, *, ta=False, tb=False, out_dtype, tm, tk, name):
    (kdim, mdim) = a.shape if ta else a.shape[::-1]
    ndim = b.shape[0] if tb else b.shape[1]
    tm, tk = min(tm, mdim), min(tk, kdim)
    nk = kdim // tk

    def body(a_ref, b_ref, o_ref, acc):
        k = pl.program_id(1)
        if ta:
            part = _dot_tn(a_ref[...], b_ref[...])
        elif tb:
            part = _dot_nt(a_ref[...], b_ref[...])
        else:
            part = _dot(a_ref[...], b_ref[...])

        @pl.when(k == 0)
        def _():
            acc[...] = part

        @pl.when(k > 0)
        def _():
            acc[...] += part

        @pl.when(k == nk - 1)
        def _():
            o_ref[...] = acc[...].astype(o_ref.dtype)

    a_spec = pl.BlockSpec((tk, tm), lambda i, k: (k, i)) if ta else pl.BlockSpec((tm, tk), lambda i, k: (i, k))
    b_spec = pl.BlockSpec((ndim, tk), lambda i, k: (0, k)) if tb else pl.BlockSpec((tk, ndim), lambda i, k: (k, 0))
    return _pcall(
        body, name=name, grid=(mdim // tm, nk), out_shape=_sds((mdim, ndim), out_dtype),
        in_specs=[a_spec, b_spec], out_specs=pl.BlockSpec((tm, ndim), lambda i, k: (i, 0)),
        scratch=[pltpu.VMEM((tm, ndim), jnp.float32)], sem=("parallel", "arbitrary"))(a, b)


def _ref_col_pieces(start, stop):
    ref_starts = [0, 1024, 1152, 1280, 1792, 2304, 2368, 2880]
    perm_starts = [C_QA, C_KA, C_VA, C_CQ, C_CKV, C_KR, C_QM]
    out = []
    for p in range(7):
        lo, hi = max(start, ref_starts[p]), min(stop, ref_starts[p + 1])
        if lo < hi:
            out.append((lo - start, perm_starts[p] + lo - ref_starts[p], hi - lo))
    return out


def _dw_in(hn, d_proj, n_shard):
    s, d = hn.shape
    n = d_proj.shape[1]
    tm, tk = min(512, d), min(512, s)
    nk = s // tk

    def body(a_ref, b_ref, o_ref, acc):
        k = pl.program_id(1)
        part = _dot_tn(a_ref[...], b_ref[...])

        @pl.when(k == 0)
        def _():
            acc[...] = part

        @pl.when(k > 0)
        def _():
            acc[...] += part

        @pl.when(k == nk - 1)
        def _():
            t = acc[...].T
            for j in range(N_DEV):
                rows = [t[src:src + width] for _, src, width in _ref_col_pieces(j * n_shard, (j + 1) * n_shard)]
                o_ref[j] = jnp.concatenate(rows, axis=0).astype(o_ref.dtype)

    return _pcall(
        body, name="dw_in", grid=(d // tm, nk), out_shape=_sds((N_DEV, n_shard, d), WIRE),
        in_specs=[pl.BlockSpec((tk, tm), lambda i, k: (k, i)), pl.BlockSpec((tk, n), lambda i, k: (k, 0))],
        out_specs=pl.BlockSpec((N_DEV, n_shard, tm), lambda i, k: (0, 0, i)),
        scratch=[pltpu.VMEM((tm, n), jnp.float32)], sem=("parallel", "arbitrary"))(hn, d_proj)


def _in_proj(x, g, w):
    s, d = x.shape
    n = w.shape[0]
    tm = min(ROW_TILE, s)

    def body(x_ref, g_ref, w_ref, p_ref, hn_ref):
        hn, _, _ = _norm_fwd(x_ref[...], g_ref[...])
        hn_ref[...] = hn.astype(hn_ref.dtype)
        p_ref[...] = _dot_nt(hn, w_ref[...])

    return _pcall(
        body, name="in_proj", grid=(s // tm,),
        out_shape=[_sds((s, n), jnp.float32), _sds((s, d), MXU)],
        in_specs=[pl.BlockSpec((tm, d), lambda i: (i, 0)), pl.BlockSpec((1, d), lambda i: (0, 0)),
                  pl.BlockSpec((n, d), lambda i: (0, 0))],
        out_specs=[pl.BlockSpec((tm, n), lambda i: (i, 0)), pl.BlockSpec((tm, d), lambda i: (i, 0))],
        sem=("parallel",))(x, g, w)


def _mla_prep(proj, cos, sin, g_cq, g_ckv, w_uq, w_ukv, g_qn, g_qr, g_kn, g_kr):
    s = proj.shape[0]
    tm = min(ROW_TILE, s)
    nh = MLA_HEADS

    def body(cq_ref, ckv_ref, kr_ref, cos_ref, sin_ref, gcq_ref, gckv_ref, wuq_ref, wukv_ref,
             gqn_ref, gqr_ref, gkn_ref, gkr_ref,
             qc_ref, kc_ref, v_ref, qb_ref, kvb_ref, cqn_ref, ckvn_ref):
        cos_t, sin_t = cos_ref[...], sin_ref[...]
        lo = _lo_mask((tm, LANES))
        cqn, _, _ = _norm_fwd(cq_ref[...], gcq_ref[...])
        cqn_ref[...] = cqn.astype(cqn_ref.dtype)
        qb = _dot_nt(cqn, wuq_ref[...])
        qb_ref[...] = qb
        ckvn, _, _ = _norm_fwd(ckv_ref[...], gckv_ref[...])
        ckvn_ref[...] = ckvn.astype(ckvn_ref.dtype)
        kvb = jnp.concatenate([_dot(ckvn, wukv_ref[dev]) for dev in range(N_DEV)], axis=1)
        kvb_ref[...] = kvb
        kr, _, _ = _norm_fwd(kr_ref[...], gkr_ref[...], half=True)
        kr = _rope(kr, cos_t, sin_t)
        kr2 = jnp.where(lo, kr, pltpu.roll(kr, 64, 1))
        ropes = []
        for j in range(nh // 2):
            xr = qb[:, nh * MLA_NOPE + LANES * j: nh * MLA_NOPE + LANES * (j + 1)]
            qr, _, _ = _norm_fwd(xr, gqr_ref[...], half=True)
            ropes.append(_rope(qr, cos_t, sin_t))
        for h in range(nh):
            qn, _, _ = _norm_fwd(qb[:, MLA_NOPE * h: MLA_NOPE * (h + 1)], gqn_ref[...])
            mask = lo if h % 2 == 0 else jnp.logical_not(lo)
            qr = jnp.where(mask, ropes[h // 2], 0.0)
            qc_ref[h] = jnp.concatenate([qn, qr], axis=1).astype(qc_ref.dtype)
            kn, _, _ = _norm_fwd(kvb[:, 256 * h: 256 * h + MLA_NOPE], gkn_ref[...])
            kc_ref[h] = jnp.concatenate([kn, kr2], axis=1).astype(kc_ref.dtype)
            v_ref[h] = kvb[:, 256 * h + MLA_NOPE: 256 * (h + 1)].astype(v_ref.dtype)

    def col(width, start):
        return pl.BlockSpec((tm, width), lambda i: (i, start // width))

    def full(shape):
        return pl.BlockSpec(shape, lambda i: (0,) * len(shape))

    def row(width):
        return pl.BlockSpec((tm, width), lambda i: (i, 0))

    def heads(width):
        return pl.BlockSpec((nh, tm, width), lambda i: (0, i, 0))

    return _pcall(
        body, name="mla_prep", grid=(s // tm,),
        out_shape=[_sds((nh, s, 256), MXU), _sds((nh, s, 256), MXU), _sds((nh, s, MLA_V), MXU),
                   _sds((s, 768), jnp.float32), _sds((s, 1024), jnp.float32),
                   _sds((s, 512), MXU), _sds((s, 512), MXU)],
        in_specs=[col(512, C_CQ), col(512, C_CKV), col(LANES, C_KR), row(LANES), row(LANES),
                  full((1, 512)), full((1, 512)), full((768, 512)), full((N_DEV, 512, LANES)),
                  full((1, LANES)), full((1, LANES)), full((1, LANES)), full((1, LANES))],
        out_specs=[heads(256), heads(256), heads(MLA_V), row(768), row(1024), row(512), row(512)],
        sem=("parallel",))(proj, proj, proj, cos, sin, g_cq, g_ckv, w_uq, w_ukv, g_qn, g_qr, g_kn, g_kr)


def _mla_fwd(qc, kc, v):
    nh, s, _ = qc.shape
    t = min(ATT_TILE, s)
    nb = s // t
    scale = (MLA_NOPE + MLA_ROPE) ** -0.5

    def body(q_ref, k_ref, v_ref, y_ref, lse_ref, m_sc, l_sc, acc):
        qi, ki = pl.program_id(1), pl.program_id(2)

        @pl.when(ki == 0)
        def _():
            m_sc[...] = jnp.full_like(m_sc, NEG_INF)
            l_sc[...] = jnp.zeros_like(l_sc)
            acc[...] = jnp.zeros_like(acc)

        @pl.when(ki <= qi)
        def _():
            sc = _dot_nt(q_ref[0], k_ref[0]) * scale
            r_i = lax.broadcasted_iota(jnp.int32, sc.shape, 0) + qi * t
            c_i = lax.broadcasted_iota(jnp.int32, sc.shape, 1) + ki * t
            sc = jnp.where(c_i <= r_i, sc, NEG_INF)
            m_new = jnp.maximum(m_sc[...], jnp.max(sc, -1, keepdims=True))
            alpha = jnp.exp(m_sc[...] - m_new)
            p = jnp.exp(sc - m_new)
            l_sc[...] = alpha * l_sc[...] + jnp.sum(p, -1, keepdims=True)
            acc[...] = alpha * acc[...] + _dot(p, v_ref[0])
            m_sc[...] = m_new

        @pl.when(ki == qi)
        def _():
            y_ref[...] = acc[...] / l_sc[...]
            lse_ref[0] = m_sc[...] + jnp.log(l_sc[...])

    return _pcall(
        body, name="mla_fwd", grid=(nh, nb, nb),
        out_shape=[_sds((s, nh * MLA_V), jnp.float32), _sds((nh, s, 1), jnp.float32)],
        in_specs=[pl.BlockSpec((1, t, 256), lambda h, i, k: (h, i, 0)),
                  pl.BlockSpec((1, t, 256), lambda h, i, k: (h, jnp.minimum(k, i), 0)),
                  pl.BlockSpec((1, t, MLA_V), lambda h, i, k: (h, jnp.minimum(k, i), 0))],
        out_specs=[pl.BlockSpec((t, MLA_V), lambda h, i, k: (i, h)),
                   pl.BlockSpec((1, t, 1), lambda h, i, k: (h, i, 0))],
        scratch=[pltpu.VMEM((t, 1), jnp.float32), pltpu.VMEM((t, 1), jnp.float32),
                 pltpu.VMEM((t, MLA_V), jnp.float32)],
        sem=("parallel", "parallel", "arbitrary"))(qc, kc, v)


def _memkv_prep(mem, g_mem, w_mkv, g_mk):
    ml, d = mem.shape
    hw = MEM_HEADS * MEM_DIM

    def body(mem_ref, g_ref, w_ref, gk_ref, k_ref, v_ref, kv_ref, mn_ref):
        mn, _, _ = _norm_fwd(mem_ref[...], g_ref[...])
        mn_ref[...] = mn.astype(mn_ref.dtype)
        kv = _dot(mn, w_ref[...])
        kv_ref[...] = kv
        for h in range(MEM_HEADS):
            kn, _, _ = _norm_fwd(kv[:, MEM_DIM * h: MEM_DIM * (h + 1)], gk_ref[...])
            k_ref[:, MEM_DIM * h: MEM_DIM * (h + 1)] = kn.astype(k_ref.dtype)
        v_ref[...] = kv[:, hw:].astype(v_ref.dtype)

    vm = pl.BlockSpec(memory_space=pltpu.VMEM)
    return _pcall(
        body, name="memkv_prep",
        out_shape=[_sds((ml, hw), MXU), _sds((ml, hw), MXU), _sds((ml, 2 * hw), jnp.float32), _sds((ml, d), MXU)],
        in_specs=[vm] * 4, out_specs=[vm] * 4)(mem, g_mem, w_mkv, g_mk)


def _mem_fwd(proj, g_mq, km, vmm):
    s = proj.shape[0]
    ml, hw = km.shape
    tm = min(FFN_TILE, s)
    scale = MEM_DIM ** -0.5

    def body(q_ref, g_ref, k_ref, v_ref, y_ref, lse_ref):
        col = lax.broadcasted_iota(jnp.int32, (tm, MEM_HEADS), 1)
        lse_t = jnp.zeros((tm, MEM_HEADS), jnp.float32)
        for h in range(MEM_HEADS):
            sl = slice(MEM_DIM * h, MEM_DIM * (h + 1))
            qn, _, _ = _norm_fwd(q_ref[:, sl], g_ref[...])
            sc = _dot_nt(qn, k_ref[:, sl]) * scale
            m = jnp.max(sc, -1, keepdims=True)
            p = jnp.exp(sc - m)
            l = jnp.sum(p, -1, keepdims=True)
            y_ref[:, sl] = _dot(p, v_ref[:, sl]) / l
            lse_t = jnp.where(col == h, m + jnp.log(l), lse_t)
        lse_ref[...] = lse_t

    return _pcall(
        body, name="mem_fwd", grid=(s // tm,),
        out_shape=[_sds((s, hw), jnp.float32), _sds((s, MEM_HEADS), jnp.float32)],
        in_specs=[pl.BlockSpec((tm, hw), lambda i: (i, C_QM // hw)), pl.BlockSpec((1, MEM_DIM), lambda i: (0, 0)),
                  pl.BlockSpec((ml, hw), lambda i: (0, 0)), pl.BlockSpec((ml, hw), lambda i: (0, 0))],
        out_specs=[pl.BlockSpec((tm, hw), lambda i: (i, 0)), pl.BlockSpec((tm, MEM_HEADS), lambda i: (i, 0))],
        sem=("parallel",))(proj, g_mq, km, vmm)


def _alibi_slope(h):
    return float(2.0 ** (-8.0 * (h + 1) / SWA_Q_HEADS))


def _swa_common(n, kp, kc, vp, vc, pq, pkp, pkc, gk):
    b = SWA_BLOCK
    k_raw = jnp.concatenate([kp, kc], axis=0)
    kn, kxn, kr = _norm_fwd(k_raw, gk, half=True)
    v = jnp.concatenate([vp, vc], axis=0)
    dist = jnp.abs(pq - jnp.concatenate([pkp, pkc], axis=1))
    r_i = lax.broadcasted_iota(jnp.int32, (b, 2 * b), 0)
    c_i = lax.broadcasted_iota(jnp.int32, (b, 2 * b), 1)
    valid = (c_i > r_i) & (c_i <= r_i + b) & (c_i >= jnp.where(n > 0, 0, b))
    return kn, v, dist, valid


def _swa_specs(s):
    b = SWA_BLOCK
    prev = lambda n: jnp.maximum(n - 1, 0)
    return [
        pl.BlockSpec((b, 1024), lambda n: (n, C_QA // 1024)),
        pl.BlockSpec((b, LANES), lambda n: (prev(n), C_KA // LANES)),
        pl.BlockSpec((b, LANES), lambda n: (n, C_KA // LANES)),
        pl.BlockSpec((b, LANES), lambda n: (prev(n), C_VA // LANES)),
        pl.BlockSpec((b, LANES), lambda n: (n, C_VA // LANES)),
        pl.BlockSpec((b, 1), lambda n: (n, 0)),
        pl.BlockSpec((1, b), lambda n: (0, prev(n))),
        pl.BlockSpec((1, b), lambda n: (0, n)),
        pl.BlockSpec((1, LANES), lambda n: (0, 0)),
        pl.BlockSpec((1, LANES), lambda n: (0, 0)),
        pl.BlockSpec(memory_space=pltpu.SMEM),
    ]


def _swa_fwd(proj, posc, posr, gq, gk, sinks):
    s = proj.shape[0]
    b = SWA_BLOCK
    scale = SWA_DIM ** -0.5

    def body(q_ref, kp_ref, kc_ref, vp_ref, vc_ref, pq_ref, pkp_ref, pkc_ref, gq_ref, gk_ref, sink_ref,
             y_ref, lse_ref):
        n = pl.program_id(0)
        kn, v, dist, valid = _swa_common(n, kp_ref[...], kc_ref[...], vp_ref[...], vc_ref[...],
                                         pq_ref[...], pkp_ref[...], pkc_ref[...], gk_ref[...])
        lo = _lo_mask((b, LANES))
        col = lax.broadcasted_iota(jnp.int32, (b, SWA_Q_HEADS), 1)
        lse_t = jnp.zeros((b, SWA_Q_HEADS), jnp.float32)
        for j in range(SWA_Q_HEADS // 2):
            hk = (2 * j) // (SWA_Q_HEADS // SWA_KV_HEADS)
            kvmask = lo if hk == 0 else jnp.logical_not(lo)
            qn, _, _ = _norm_fwd(q_ref[:, LANES * j: LANES * (j + 1)], gq_ref[...], half=True)
            qsw = pltpu.roll(qn, 64, 1)
            outs = []
            for e in range(2):
                h = 2 * j + e
                qm = jnp.where(kvmask, qn if e == hk else qsw, 0.0)
                sc = _dot_nt(qm, kn) * scale - _alibi_slope(h) * dist
                sc = jnp.where(valid, sc, NEG_INF)
                sk = sink_ref[h]
                m = jnp.maximum(jnp.max(sc, -1, keepdims=True), sk)
                p = jnp.exp(sc - m)
                l = jnp.sum(p, -1, keepdims=True) + jnp.exp(sk - m)
                o = _dot(p, v) / l
                outs.append(o if e == hk else pltpu.roll(o, 64, 1))
                lse_t = jnp.where(col == h, m + jnp.log(l), lse_t)
            y_ref[:, LANES * j: LANES * (j + 1)] = jnp.where(lo, outs[0], outs[1])
        lse_ref[...] = lse_t

    return _pcall(
        body, name="swa_fwd", grid=(s // b,),
        out_shape=[_sds((s, 1024), jnp.float32), _sds((s, SWA_Q_HEADS), jnp.float32)],
        in_specs=_swa_specs(s),
        out_specs=[pl.BlockSpec((b, 1024), lambda n: (n, 0)), pl.BlockSpec((b, SWA_Q_HEADS), lambda n: (n, 0))],
        sem=("parallel",))(proj, proj, proj, proj, proj, posc, posr, posr, gq, gk, sinks)


def _out_proj(y_a, y_b, y_m, x, w_out, g_ffn):
    s, d = x.shape
    tm = min(ROW_TILE, s)

    def body(ya_ref, yb_ref, ym_ref, x_ref, w_ref, g_ref, h1_ref, fn_ref):
        y = jnp.concatenate([ya_ref[...].astype(MXU), yb_ref[...].astype(MXU), ym_ref[...].astype(MXU)], axis=1)
        h1 = x_ref[...] + _dot(y, w_ref[...])
        h1_ref[...] = h1
        fn, _, _ = _norm_fwd(h1, g_ref[...])
        fn_ref[...] = fn.astype(fn_ref.dtype)

    def row(width):
        return pl.BlockSpec((tm, width), lambda i: (i, 0))

    return _pcall(
        body, name="out_proj", grid=(s // tm,),
        out_shape=[_sds((s, d), jnp.float32), _sds((s, d), MXU)],
        in_specs=[row(1024), row(512), row(512), row(d), pl.BlockSpec(w_out.shape, lambda i: (0, 0)),
                  pl.BlockSpec((1, d), lambda i: (0, 0))],
        out_specs=[row(d), row(d)], sem=("parallel",))(y_a, y_b, y_m, x, w_out, g_ffn)


def _ffn_gu(fn, w_gu):
    s, d = fn.shape
    f = w_gu.shape[2]
    tm = min(FFN_TILE, s)

    def body(fn_ref, w_ref, gu_ref, act_ref):
        x = fn_ref[...]
        g = _dot_nt(x, w_ref[0, 0])
        u = _dot_nt(x, w_ref[0, 1])
        gu_ref[0, 0] = g
        gu_ref[0, 1] = u
        act_ref[0] = (g * jax.nn.sigmoid(g) * u).astype(act_ref.dtype)

    return _pcall(
        body, name="ffn_gate_up", grid=(N_DEV, s // tm),
        out_shape=[_sds((N_DEV, 2, s, f), jnp.float32), _sds((N_DEV, s, f), MXU)],
        in_specs=[pl.BlockSpec((tm, d), lambda j, i: (i, 0)),
                  pl.BlockSpec((1, 2, f, d), lambda j, i: (j, 0, 0, 0))],
        out_specs=[pl.BlockSpec((1, 2, tm, f), lambda j, i: (j, 0, i, 0)),
                   pl.BlockSpec((1, tm, f), lambda j, i: (j, i, 0))],
        sem=("parallel", "parallel"))(fn, w_gu)


def _ffn_down(act, w_d, h1, target):
    _, s, f = act.shape
    d = h1.shape[1]
    tm = min(FFN_TILE, s)

    def body(a_ref, w_ref, h1_ref, t_ref, dout_ref, loss_ref, acc):
        i, j = pl.program_id(0), pl.program_id(1)
        part = _dot(a_ref[0], w_ref[0]) + _dot(a_ref[1], w_ref[1])

        @pl.when(j == 0)
        def _():
            acc[...] = h1_ref[...] + part

        @pl.when(j > 0)
        def _():
            acc[...] += part

        @pl.when((i == 0) & (j == 0))
        def _():
            loss_ref[...] = jnp.zeros_like(loss_ref)

        @pl.when(j == N_DEV // 2 - 1)
        def _():
            diff = acc[...] - t_ref[...]
            dout_ref[...] = diff / d
            loss_ref[...] += 0.5 * jnp.sum(jnp.sum(diff * diff, -1, keepdims=True) / d)

    row = pl.BlockSpec((tm, d), lambda i, j: (i, 0))
    return _pcall(
        body, name="ffn_down", grid=(s // tm, N_DEV // 2),
        out_shape=[_sds((s, d), jnp.float32), _sds((8, LANES), jnp.float32)],
        in_specs=[pl.BlockSpec((2, tm, f), lambda i, j: (j, i, 0)), pl.BlockSpec((2, f, d), lambda i, j: (j, 0, 0)),
                  row, row],
        out_specs=[row, pl.BlockSpec((8, LANES), lambda i, j: (0, 0))],
        scratch=[pltpu.VMEM((tm, d), jnp.float32)], sem=("arbitrary", "arbitrary"))(act, w_d, h1, target)


def _ffn_bwd_act(dout, w_d, gu):
    s, d = dout.shape
    f = w_d.shape[1]
    tm = min(FFN_TILE, s)
    ni = s // tm

    def body(do_ref, w_ref, gu_ref, dgu_ref, dw_ref, acc):
        i = pl.program_id(1)
        do = do_ref[...].astype(MXU)
        d_act = _dot_nt(do, w_ref[0])
        g, u = gu_ref[0, 0], gu_ref[0, 1]
        sig = jax.nn.sigmoid(g)
        silu = g * sig
        dgu_ref[0, 0] = (d_act * u * (sig * (1.0 + g * (1.0 - sig)))).astype(dgu_ref.dtype)
        dgu_ref[0, 1] = (d_act * silu).astype(dgu_ref.dtype)
        part = _dot_tn(silu * u, do)

        @pl.when(i == 0)
        def _():
            acc[...] = part

        @pl.when(i > 0)
        def _():
            acc[...] += part

        @pl.when(i == ni - 1)
        def _():
            dw_ref[0] = acc[...].astype(dw_ref.dtype)

    return _pcall(
        body, name="ffn_bwd_act", grid=(N_DEV, ni),
        out_shape=[_sds((N_DEV, 2, s, f), MXU), _sds((N_DEV, f, d), WIRE)],
        in_specs=[pl.BlockSpec((tm, d), lambda j, i: (i, 0)), pl.BlockSpec((1, f, d), lambda j, i: (j, 0, 0)),
                  pl.BlockSpec((1, 2, tm, f), lambda j, i: (j, 0, i, 0))],
        out_specs=[pl.BlockSpec((1, 2, tm, f), lambda j, i: (j, 0, i, 0)),
                   pl.BlockSpec((1, f, d), lambda j, i: (j, 0, 0))],
        scratch=[pltpu.VMEM((f, d), jnp.float32)], sem=("parallel", "arbitrary"))(dout, w_d, gu)


def _ffn_dw_gu(fn, dgu):
    s, d = fn.shape
    f = dgu.shape[-1]
    tk = min(2 * FFN_TILE, s)
    nk = s // tk

    def body(fn_ref, dgu_ref, dw_ref, acc):
        k = pl.program_id(1)
        x = fn_ref[...]
        pg = _dot_tn(dgu_ref[0, 0], x)
        pu = _dot_tn(dgu_ref[0, 1], x)

        @pl.when(k == 0)
        def _():
            acc[0] = pg
            acc[1] = pu

        @pl.when(k > 0)
        def _():
            acc[0] += pg
            acc[1] += pu

        @pl.when(k == nk - 1)
        def _():
            dw_ref[0] = acc[...].astype(dw_ref.dtype)

    return _pcall(
        body, name="ffn_dw_gate_up", grid=(N_DEV, nk),
        out_shape=_sds((N_DEV, 2, f, d), WIRE),
        in_specs=[pl.BlockSpec((tk, d), lambda j, k: (k, 0)), pl.BlockSpec((1, 2, tk, f), lambda j, k: (j, 0, k, 0))],
        out_specs=pl.BlockSpec((1, 2, f, d), lambda j, k: (j, 0, 0, 0)),
        scratch=[pltpu.VMEM((2, f, d), jnp.float32)], sem=("parallel", "arbitrary"))(fn, dgu)


def _ffn_dfn(dgu, w_gu, after):
    _, _, s, f = dgu.shape
    d = w_gu.shape[3]
    tm = min(FFN_TILE, s)

    def body(dgu_ref, w_ref, dfn_ref):
        j = pl.program_id(1)
        part = (_dot(dgu_ref[0, 0], w_ref[0, 0]) + _dot(dgu_ref[0, 1], w_ref[0, 1])
                + _dot(dgu_ref[1, 0], w_ref[1, 0]) + _dot(dgu_ref[1, 1], w_ref[1, 1]))

        @pl.when(j == 0)
        def _():
            dfn_ref[...] = part

        @pl.when(j > 0)
        def _():
            dfn_ref[...] += part

    return _pcall(
        body, name="ffn_dfn", grid=(s // tm, N_DEV // 2),
        out_shape=_sds((s, d), jnp.float32),
        in_specs=[pl.BlockSpec((2, 2, tm, f), lambda i, j: (j, 0, i, 0)),
                  pl.BlockSpec((2, 2, f, d), lambda i, j: (j, 0, 0, 0))],
        out_specs=pl.BlockSpec((tm, d), lambda i, j: (i, 0)),
        sem=("parallel", "arbitrary"), after=after)(dgu, w_gu)


def _ffn_norm_bwd(d_fn, dout, h1, g_ffn):
    s, d = h1.shape
    tm = min(ROW_TILE, s)

    def body(dfn_ref, do_ref, h1_ref, g_ref, dh1_ref, dg_ref):
        i = pl.program_id(0)

        @pl.when(i == 0)
        def _():
            dg_ref[...] = jnp.zeros_like(dg_ref)

        _, xn, r = _norm_fwd(h1_ref[...], g_ref[...])
        dx, dg = _norm_bwd(xn, r, g_ref[...], dfn_ref[...])
        dh1_ref[...] = do_ref[...] + dx
        dg_ref[...] += dg

    row = pl.BlockSpec((tm, d), lambda i: (i, 0))
    vec = pl.BlockSpec((1, d), lambda i: (0, 0))
    return _pcall(
        body, name="ffn_norm_bwd", grid=(s // tm,),
        out_shape=[_sds((s, d), jnp.float32), _sds((1, d), jnp.float32)],
        in_specs=[row, row, row, vec], out_specs=[row, vec], sem=("arbitrary",))(d_fn, dout, h1, g_ffn)


def _mem_bwd(proj, g_mq, km, vmm, d_y, y_m, lse):
    s = proj.shape[0]
    ml, hw = km.shape
    tm = min(FFN_TILE, s)
    scale = MEM_DIM ** -0.5

    def body(q_ref, g_ref, k_ref, v_ref, do_ref, y_ref, lse_ref, dq_ref, dk_ref, dv_ref, dg_ref):
        i = pl.program_id(0)

        @pl.when(i == 0)
        def _():
            dk_ref[...] = jnp.zeros_like(dk_ref)
            dv_ref[...] = jnp.zeros_like(dv_ref)
            dg_ref[...] = jnp.zeros_like(dg_ref)

        col = lax.broadcasted_iota(jnp.int32, (tm, MEM_HEADS), 1)
        lse_t = lse_ref[...]
        for h in range(MEM_HEADS):
            sl = slice(MEM_DIM * h, MEM_DIM * (h + 1))
            qn, xn, r = _norm_fwd(q_ref[:, sl], g_ref[...])
            lse_h = jnp.sum(jnp.where(col == h, lse_t, 0.0), -1, keepdims=True)
            p = jnp.exp(_dot_nt(qn, k_ref[:, sl]) * scale - lse_h)
            do = do_ref[:, sl]
            dd = jnp.sum(do * y_ref[:, sl], -1, keepdims=True)
            dp = _dot_nt(do, v_ref[:, sl])
            ds = (p * (dp - dd)).astype(MXU)
            dv_ref[:, sl] += _dot_tn(p, do)
            dk_ref[:, sl] += _dot_tn(ds, qn) * scale
            dx, dg = _norm_bwd(xn, r, g_ref[...], _dot(ds, k_ref[:, sl]) * scale)
            dq_ref[:, sl] = dx.astype(dq_ref.dtype)
            dg_ref[...] += dg

    full = pl.BlockSpec((ml, hw), lambda i: (0, 0))
    return _pcall(
        body, name="mem_bwd", grid=(s // tm,),
        out_shape=[_sds((s, hw), MXU), _sds((ml, hw), jnp.float32), _sds((ml, hw), jnp.float32),
                   _sds((1, MEM_DIM), jnp.float32)],
        in_specs=[pl.BlockSpec((tm, hw), lambda i: (i, C_QM // hw)), pl.BlockSpec((1, MEM_DIM), lambda i: (0, 0)),
                  full, full, pl.BlockSpec((tm, hw), lambda i: (i, 3)), pl.BlockSpec((tm, hw), lambda i: (i, 0)),
                  pl.BlockSpec((tm, MEM_HEADS), lambda i: (i, 0))],
        out_specs=[pl.BlockSpec((tm, hw), lambda i: (i, 0)), full, full,
                   pl.BlockSpec((1, MEM_DIM), lambda i: (0, 0))],
        sem=("arbitrary",))(proj, g_mq, km, vmm, d_y, y_m, lse)


def _memkv_bwd(mem, g_mem, w_mkv, g_mk, kv, memn, dk, dv):
    ml, d = mem.shape
    hw = MEM_HEADS * MEM_DIM

    def body(mem_ref, g_ref, w_ref, gk_ref, kv_ref, mn_ref, dk_ref, dv_ref, dw_ref, dgm_ref, dgk_ref):
        parts = []
        dgk = jnp.zeros((1, MEM_DIM), jnp.float32)
        for h in range(MEM_HEADS):
            sl = slice(MEM_DIM * h, MEM_DIM * (h + 1))
            _, xn, r = _norm_fwd(kv_ref[:, sl], gk_ref[...])
            dx, dg = _norm_bwd(xn, r, gk_ref[...], dk_ref[:, sl])
            parts.append(dx)
            dgk = dgk + dg
        dkv = jnp.concatenate(parts + [dv_ref[...]], axis=1).astype(MXU)
        dgk_ref[...] = dgk
        dw_ref[...] = _dot_tn(mn_ref[...], dkv).astype(dw_ref.dtype)
        d_mn = _dot_nt(dkv, w_ref[...])
        _, xn, _ = _norm_fwd(mem_ref[...], g_ref[...])
        dgm_ref[...] = jnp.sum(d_mn * xn, 0, keepdims=True)

    vm = pl.BlockSpec(memory_space=pltpu.VMEM)
    return _pcall(
        body, name="memkv_bwd",
        out_shape=[_sds((d, 2 * hw), WIRE), _sds((1, d), jnp.float32), _sds((1, MEM_DIM), jnp.float32)],
        in_specs=[vm] * 8, out_specs=[vm] * 3)(mem, g_mem, w_mkv, g_mk, kv, memn, dk, dv)


def _mla_bwd(qc, kc, v, d_y, y_b, lse, after):
    nh, s, _ = qc.shape
    t = min(ATT_TILE, s)
    nb = s // t
    scale = (MLA_NOPE + MLA_ROPE) ** -0.5

    def body(q_ref, k_ref, v_ref, do_ref, y_ref, lse_ref, dq_ref, dk_ref, dv_ref, dk_acc, dv_acc):
        kj, qi = pl.program_id(1), pl.program_id(2)

        @pl.when((kj == 0) & (qi == 0))
        def _():
            dq_ref[...] = jnp.zeros_like(dq_ref)

        @pl.when(qi == kj)
        def _():
            dk_acc[...] = jnp.zeros_like(dk_acc)
            dv_acc[...] = jnp.zeros_like(dv_acc)

        @pl.when(qi >= kj)
        def _():
            q, k = q_ref[0], k_ref[0]
            sc = _dot_nt(q, k) * scale
            r_i = lax.broadcasted_iota(jnp.int32, sc.shape, 0) + qi * t
            c_i = lax.broadcasted_iota(jnp.int32, sc.shape, 1) + kj * t
            p = jnp.exp(jnp.where(c_i <= r_i, sc, NEG_INF) - lse_ref[0])
            do = do_ref[...]
            dd = jnp.sum(do * y_ref[...], -1, keepdims=True)
            dp = _dot_nt(do, v_ref[0])
            ds = (p * (dp - dd) * scale).astype(MXU)
            dv_acc[...] += _dot_tn(p, do)
            dk_acc[...] += _dot_tn(ds, q)
            rows = pl.ds(pl.multiple_of(qi * t, t), t)
            dq_ref[0, rows, :] += _dot(ds, k)

        @pl.when(qi == nb - 1)
        def _():
            dk_ref[0] = dk_acc[...]
            dv_ref[0] = dv_acc[...]

    qmap = lambda h, j, i: (h, jnp.maximum(i, j), 0)
    return _pcall(
        body, name="mla_bwd", grid=(nh, nb, nb),
        out_shape=[_sds((nh, s, 256), jnp.float32), _sds((nh, s, 256), jnp.float32),
                   _sds((nh, s, MLA_V), jnp.float32)],
        in_specs=[pl.BlockSpec((1, t, 256), qmap),
                  pl.BlockSpec((1, t, 256), lambda h, j, i: (h, j, 0)),
                  pl.BlockSpec((1, t, MLA_V), lambda h, j, i: (h, j, 0)),
                  pl.BlockSpec((t, MLA_V), lambda h, j, i: (jnp.maximum(i, j), 8 + h)),
                  pl.BlockSpec((t, MLA_V), lambda h, j, i: (jnp.maximum(i, j), h)),
                  pl.BlockSpec((1, t, 1), qmap)],
        out_specs=[pl.BlockSpec((1, s, 256), lambda h, j, i: (h, 0, 0)),
                   pl.BlockSpec((1, t, 256), lambda h, j, i: (h, j, 0)),
                   pl.BlockSpec((1, t, MLA_V), lambda h, j, i: (h, j, 0))],
        scratch=[pltpu.VMEM((t, 256), jnp.float32), pltpu.VMEM((t, MLA_V), jnp.float32)],
        sem=("parallel", "arbitrary", "arbitrary"), after=after)(qc, kc, v, d_y, y_b, lse)


def _mla_prep_bwd(proj, cos, sin, g_cq, g_ckv, w_uq, w_ukv, g_qn, g_qr, g_kn, g_kr,
                  qb, kvb, cqn, ckvn, dqc, dkc, dv):
    s = proj.shape[0]
    tm = min(ROW_TILE, s)
    nh = MLA_HEADS
    ni = s // tm

    def body(cq_ref, ckv_ref, kr_ref, cos_ref, sin_ref, gcq_ref, gckv_ref, wuq_ref, wukv_ref,
             gqn_ref, gqr_ref, gkn_ref, gkr_ref, qb_ref, kvb_ref, cqn_ref, ckvn_ref, dqc_ref, dkc_ref, dv_ref,
             dcq_ref, dckv_ref, dkr_ref, dwuq_ref, dwukv_ref,
             dgcq_ref, dgckv_ref, dgqn_ref, dgqr_ref, dgkn_ref, dgkr_ref, acc_uq, acc_ukv):
        i = pl.program_id(0)

        @pl.when(i == 0)
        def _():
            acc_uq[...] = jnp.zeros_like(acc_uq)
            acc_ukv[...] = jnp.zeros_like(acc_ukv)
            for ref in (dgcq_ref, dgckv_ref, dgqn_ref, dgqr_ref, dgkn_ref, dgkr_ref):
                ref[...] = jnp.zeros_like(ref)

        cos_t, sin_t = cos_ref[...], sin_ref[...]
        lo = _lo_mask((tm, LANES))
        qb_v, kvb_v = qb_ref[...], kvb_ref[...]
        dq_parts, dgqn = [], jnp.zeros((1, LANES), jnp.float32)
        for h in range(nh):
            _, xn, r = _norm_fwd(qb_v[:, MLA_NOPE * h: MLA_NOPE * (h + 1)], gqn_ref[...])
            dx, dg = _norm_bwd(xn, r, gqn_ref[...], dqc_ref[h][:, :MLA_NOPE])
            dq_parts.append(dx)
            dgqn = dgqn + dg
        dgqn_ref[...] += dgqn
        dgqr = jnp.zeros((1, LANES), jnp.float32)
        for j in range(nh // 2):
            d_rope = jnp.where(lo, dqc_ref[2 * j][:, MLA_NOPE:], dqc_ref[2 * j + 1][:, MLA_NOPE:])
            d_pre = _rope_bwd(d_rope, cos_t, sin_t)
            xr = qb_v[:, nh * MLA_NOPE + LANES * j: nh * MLA_NOPE + LANES * (j + 1)]
            _, xn, r = _norm_fwd(xr, gqr_ref[...], half=True)
            dx, dg = _norm_bwd(xn, r, gqr_ref[...], d_pre, half=True)
            dq_parts.append(dx)
            dgqr = dgqr + dg
        dgqr_ref[...] += dgqr
        dqb = jnp.concatenate(dq_parts, axis=1).astype(MXU)
        acc_uq[...] += _dot_tn(dqb, cqn_ref[...])
        _, xn, r = _norm_fwd(cq_ref[...], gcq_ref[...])
        dx, dg = _norm_bwd(xn, r, gcq_ref[...], _dot(dqb, wuq_ref[...]))
        dcq_ref[...] = dx.astype(dcq_ref.dtype)
        dgcq_ref[...] += dg
        dkv_parts, dgkn = [], jnp.zeros((1, LANES), jnp.float32)
        d_kr2 = jnp.zeros((tm, LANES), jnp.float32)
        for h in range(nh):
            _, xn, r = _norm_fwd(kvb_v[:, 256 * h: 256 * h + MLA_NOPE], gkn_ref[...])
            dx, dg = _norm_bwd(xn, r, gkn_ref[...], dkc_ref[h][:, :MLA_NOPE])
            dkv_parts += [dx, dv_ref[h]]
            dgkn = dgkn + dg
            d_kr2 = d_kr2 + dkc_ref[h][:, MLA_NOPE:]
        dgkn_ref[...] += dgkn
        dkvb = jnp.concatenate(dkv_parts, axis=1).astype(MXU)
        d_ckvn = jnp.zeros((tm, 512), jnp.float32)
        for dev in range(N_DEV):
            piece = dkvb[:, LANES * dev: LANES * (dev + 1)]
            acc_ukv[dev] += _dot_tn(ckvn_ref[...], piece)
            d_ckvn = d_ckvn + _dot_nt(piece, wukv_ref[dev])
        _, xn, r = _norm_fwd(ckv_ref[...], gckv_ref[...])
        dx, dg = _norm_bwd(xn, r, gckv_ref[...], d_ckvn)
        dckv_ref[...] = dx.astype(dckv_ref.dtype)
        dgckv_ref[...] += dg
        d_kr = jnp.where(lo, d_kr2 + pltpu.roll(d_kr2, 64, 1), 0.0)
        d_pre = _rope_bwd(d_kr, cos_t, sin_t)
        _, xn, r = _norm_fwd(kr_ref[...], gkr_ref[...], half=True)
        dx, dg = _norm_bwd(xn, r, gkr_ref[...], d_pre, half=True)
        dkr_ref[...] = jnp.where(lo, dx, 0.0).astype(dkr_ref.dtype)
        dgkr_ref[...] += jnp.where(_lo_mask((1, LANES)), dg, 0.0)

        @pl.when(i == ni - 1)
        def _():
            dwuq_ref[...] = acc_uq[...].astype(dwuq_ref.dtype)
            dwukv_ref[...] = acc_ukv[...].astype(dwukv_ref.dtype)

    def col(width, start):
        return pl.BlockSpec((tm, width), lambda i: (i, start // width))

    def full(shape):
        return pl.BlockSpec(shape, lambda i: (0,) * len(shape))

    def row(width):
        return pl.BlockSpec((tm, width), lambda i: (i, 0))

    def heads(width):
        return pl.BlockSpec((nh, tm, width), lambda i: (0, i, 0))

    vec = full((1, LANES))
    return _pcall(
        body, name="mla_prep_bwd", grid=(ni,),
        out_shape=[_sds((s, 512), MXU), _sds((s, 512), MXU), _sds((s, LANES), MXU),
                   _sds((768, 512), WIRE), _sds((N_DEV, 512, LANES), WIRE),
                   _sds((1, 512), jnp.float32), _sds((1, 512), jnp.float32)] + [_sds((1, LANES), jnp.float32)] * 4,
        in_specs=[col(512, C_CQ), col(512, C_CKV), col(LANES, C_KR), row(LANES), row(LANES),
                  full((1, 512)), full((1, 512)), full((768, 512)), full((N_DEV, 512, LANES)), vec, vec, vec, vec,
                  row(768), row(1024), row(512), row(512), heads(256), heads(256), heads(MLA_V)],
        out_specs=[row(512), row(512), row(LANES), full((768, 512)), full((N_DEV, 512, LANES)),
                   full((1, 512)), full((1, 512)), vec, vec, vec, vec],
        scratch=[pltpu.VMEM((768, 512), jnp.float32), pltpu.VMEM((N_DEV, 512, LANES), jnp.float32)],
        sem=("arbitrary",))(proj, proj, proj, cos, sin, g_cq, g_ckv, w_uq, w_ukv, g_qn, g_qr, g_kn, g_kr,
                            qb, kvb, cqn, ckvn, dqc, dkc, dv)


def _swa_bwd(proj, posc, posr, gq, gk, sinks, d_y, y_a, lse, after):
    s = proj.shape[0]
    b = SWA_BLOCK
    nb = s // b
    scale = SWA_DIM ** -0.5

    def body(q_ref, kp_ref, kc_ref, vp_ref, vc_ref, pq_ref, pkp_ref, pkc_ref, gq_ref, gk_ref, sink_ref,
             do_ref, y_ref, lse_ref, kfull_ref,
             dq_ref, dk_ref, dv_ref, dgq_ref, dgk_ref, dsink_ref, dk_acc, dv_acc):
        n = pl.program_id(0)

        @pl.when(n == 0)
        def _():
            dk_acc[...] = jnp.zeros_like(dk_acc)
            dv_acc[...] = jnp.zeros_like(dv_acc)
            dgq_ref[...] = jnp.zeros_like(dgq_ref)
            dsink_ref[...] = jnp.zeros_like(dsink_ref)

        kn, v, dist, valid = _swa_common(n, kp_ref[...], kc_ref[...], vp_ref[...], vc_ref[...],
                                         pq_ref[...], pkp_ref[...], pkc_ref[...], gk_ref[...])
        lo = _lo_mask((b, LANES))
        col = lax.broadcasted_iota(jnp.int32, (b, SWA_Q_HEADS), 1)
        col1 = lax.broadcasted_iota(jnp.int32, (1, SWA_Q_HEADS), 1)
        lse_t = lse_ref[...]
        dk_blk = jnp.zeros((2 * b, LANES), jnp.float32)
        dv_blk = jnp.zeros((2 * b, LANES), jnp.float32)
        dgq = jnp.zeros((1, LANES), jnp.float32)
        dsink = jnp.zeros((1, SWA_Q_HEADS), jnp.float32)
        for j in range(SWA_Q_HEADS // 2):
            hk = (2 * j) // (SWA_Q_HEADS // SWA_KV_HEADS)
            kvmask = lo if hk == 0 else jnp.logical_not(lo)
            sl = slice(LANES * j, LANES * (j + 1))
            qn, xn, r = _norm_fwd(q_ref[:, sl], gq_ref[...], half=True)
            qsw = pltpu.roll(qn, 64, 1)
            d2 = do_ref[:, sl]
            d2sw = pltpu.roll(d2, 64, 1)
            prod = d2 * y_ref[:, sl]
            dqs = []
            for e in range(2):
                h = 2 * j + e
                half_e = lo if e == 0 else jnp.logical_not(lo)
                qm = jnp.where(kvmask, qn if e == hk else qsw, 0.0)
                dm = jnp.where(kvmask, d2 if e == hk else d2sw, 0.0)
                sc = _dot_nt(qm, kn) * scale - _alibi_slope(h) * dist
                sc = jnp.where(valid, sc, NEG_INF)
                lse_h = jnp.sum(jnp.where(col == h, lse_t, 0.0), -1, keepdims=True)
                p = jnp.exp(sc - lse_h)
                dd = jnp.sum(jnp.where(half_e, prod, 0.0), -1, keepdims=True)
                dp = _dot_nt(dm, v)
                ds = (p * (dp - dd)).astype(MXU)
                dsink = dsink - jnp.where(col1 == h, jnp.sum(jnp.exp(sink_ref[h] - lse_h) * dd), 0.0)
                dq_m = _dot(ds, kn) * scale
                dk_blk = dk_blk + _dot_tn(ds, qm) * scale
                dv_blk = dv_blk + _dot_tn(p, dm)
                dqs.append(dq_m if e == hk else pltpu.roll(dq_m, 64, 1))
            dx, dg = _norm_bwd(xn, r, gq_ref[...], jnp.where(lo, dqs[0], dqs[1]), half=True)
            dq_ref[:, sl] = dx.astype(dq_ref.dtype)
            dgq = dgq + dg
        dgq_ref[...] += dgq
        dsink_ref[...] += dsink
        prev = pl.ds(pl.multiple_of(jnp.maximum(n - 1, 0) * b, b), b)
        cur = pl.ds(pl.multiple_of(n * b, b), b)
        dk_acc[prev, :] += dk_blk[:b]
        dv_acc[prev, :] += dv_blk[:b]
        dk_acc[cur, :] += dk_blk[b:]
        dv_acc[cur, :] += dv_blk[b:]

        @pl.when(n == nb - 1)
        def _():
            _, kxn, kr = _norm_fwd(kfull_ref[...], gk_ref[...], half=True)
            dx, dg = _norm_bwd(kxn, kr, gk_ref[...], dk_acc[...], half=True)
            dk_ref[...] = dx.astype(dk_ref.dtype)
            dv_ref[...] = dv_acc[...].astype(dv_ref.dtype)
            dgk_ref[...] = dg

    full = pl.BlockSpec((s, LANES), lambda n: (0, 0))
    vec = pl.BlockSpec((1, LANES), lambda n: (0, 0))
    return _pcall(
        body, name="swa_bwd", grid=(nb,),
        out_shape=[_sds((s, 1024), MXU), _sds((s, LANES), MXU), _sds((s, LANES), MXU),
                   _sds((1, LANES), jnp.float32), _sds((1, LANES), jnp.float32),
                   _sds((1, SWA_Q_HEADS), jnp.float32)],
        in_specs=_swa_specs(s) + [pl.BlockSpec((b, 1024), lambda n: (n, 0)), pl.BlockSpec((b, 1024), lambda n: (n, 0)),
                                  pl.BlockSpec((b, SWA_Q_HEADS), lambda n: (n, 0)),
                                  pl.BlockSpec((s, LANES), lambda n: (0, C_KA // LANES))],
        out_specs=[pl.BlockSpec((b, 1024), lambda n: (n, 0)), full, full, vec, vec,
                   pl.BlockSpec((1, SWA_Q_HEADS), lambda n: (0, 0))],
        scratch=[pltpu.VMEM((s, LANES), jnp.float32), pltpu.VMEM((s, LANES), jnp.float32)],
        sem=("arbitrary",), after=after)(proj, proj, proj, proj, proj, posc, posr, posr, gq, gk, sinks, d_y, y_a, lse,
                                         proj)


def _dx(d_proj, w_in, x, g, d_h1, after):
    s, d = x.shape
    n = w_in.shape[0]
    tm = min(ROW_TILE, s)

    def body(dp_ref, w_ref, x_ref, g_ref, dh_ref, dx_ref, dg_ref):
        i = pl.program_id(0)

        @pl.when(i == 0)
        def _():
            dg_ref[...] = jnp.zeros_like(dg_ref)

        d_hn = _dot(dp_ref[...], w_ref[...])
        _, xn, r = _norm_fwd(x_ref[...], g_ref[...])
        dx, dg = _norm_bwd(xn, r, g_ref[...], d_hn)
        dx_ref[...] = dh_ref[...] + dx
        dg_ref[...] += dg

    row = pl.BlockSpec((tm, d), lambda i: (i, 0))
    vec = pl.BlockSpec((1, d), lambda i: (0, 0))
    return _pcall(
        body, name="grad_x", grid=(s // tm,),
        out_shape=[_sds((s, d), jnp.float32), _sds((1, d), jnp.float32)],
        in_specs=[pl.BlockSpec((tm, n), lambda i: (i, 0)), pl.BlockSpec((n, d), lambda i: (0, 0)), row, vec, row],
        out_specs=[row, vec], sem=("arbitrary",), after=after)(d_proj, w_in, x, g, d_h1)


_SMALL = ["attn_norm_g", "swa_q_norm_g", "swa_k_norm_g", "swa_sinks", "mla_cq_norm_g", "mla_ckv_norm_g",
          "mla_qn_norm_g", "mla_qr_norm_g", "mla_kn_norm_g", "mla_kr_norm_g", "mem_norm_g",
          "mem_q_norm_g", "mem_k_norm_g", "ffn_norm_g"]


def _pack_rows(v):
    n = v.shape[-1]
    rows = -(-n // LANES)
    rows8 = -(-rows // 8) * 8
    flat = jnp.pad(v.reshape(-1), (0, rows8 * LANES - n))
    return flat.reshape(rows8, LANES)


def _pack(parts):
    return jnp.concatenate([_pack_rows(p) for p in parts], axis=0)


def _unpack(packed, sizes):
    out, r = [], 0
    for n in sizes:
        rows = -(-n // LANES)
        rows8 = -(-rows // 8) * 8
        out.append(packed[r:r + rows8].reshape(-1)[:n].reshape(1, n))
        r += rows8
    return out


def _fold64(v):
    return v[:, :64] + v[:, 64:]


def kernel(x, mem, positions, attn_norm_g, w_in, swa_q_norm_g, swa_k_norm_g, swa_sinks, mla_cq_norm_g, mla_ckv_norm_g, w_uq, w_ukv, mla_qn_norm_g, mla_qr_norm_g, mla_kn_norm_g, mla_kr_norm_g, mem_norm_g, w_mem_kv, mem_q_norm_g, mem_k_norm_g, w_out, ffn_norm_g, w_gate, w_up, w_down, loss_target, m_attn_norm_g, m_w_in, m_swa_q_norm_g, m_swa_k_norm_g, m_swa_sinks, m_mla_cq_norm_g, m_mla_ckv_norm_g, m_w_uq, m_w_ukv, m_mla_qn_norm_g, m_mla_qr_norm_g, m_mla_kn_norm_g, m_mla_kr_norm_g, m_mem_norm_g, m_w_mem_kv, m_mem_q_norm_g, m_mem_k_norm_g, m_w_out, m_ffn_norm_g, m_w_gate, m_w_up, m_w_down, v_attn_norm_g, v_w_in, v_swa_q_norm_g, v_swa_k_norm_g, v_swa_sinks, v_mla_cq_norm_g, v_mla_ckv_norm_g, v_w_uq, v_w_ukv, v_mla_qn_norm_g, v_mla_qr_norm_g, v_mla_kn_norm_g, v_mla_kr_norm_g, v_mem_norm_g, v_w_mem_kv, v_mem_q_norm_g, v_mem_k_norm_g, v_w_out, v_ffn_norm_g, v_w_gate, v_w_up, v_w_down):
    args = dict(locals())
    x2, mem2, tgt = x[0], mem[0], loss_target[0]
    s, d = x2.shape
    n_in = w_in.shape[2]
    f = w_gate.shape[2]

    shards = [w_in[0].T.astype(WIRE), w_uq[0].T.astype(WIRE), w_ukv[0].astype(WIRE), w_mem_kv[0].astype(WIRE),
              w_out[0].astype(WIRE)]
    g_in, g_uq, wkv, g_mkv, g_out = _all_gather(shards)
    ffn_shards = [_to_wire([w_gate[0].T, w_up[0].T], g_in, "wire_gate_up"),
                  _to_wire([w_down[0]], g_in, "wire_down")[0]]
    w_gu, w_d = _all_gather_background(ffn_shards, 1, "all_gather_ffn_weights")
    wi = g_in.reshape(N_DEV * n_in, d)
    wi = jnp.concatenate([wi[0:1024], wi[1280:1792], wi[1792:2304], wi[2368:2880],
                          wi[1024:1152], wi[1152:1280], wi[2304:2368],
                          jnp.zeros((IN_PAD - 2880, d), wi.dtype)], axis=0)
    wq = g_uq.reshape(768, 512)
    wq = jnp.concatenate([wq[192 * h: 192 * h + 128] for h in range(4)]
                         + [wq[192 * h + 128: 192 * (h + 1)] for h in range(4)], axis=0)
    wmkv = g_mkv.reshape(-1, g_mkv.shape[-1])
    wo = g_out.reshape(-1, d)

    pos = positions[0].astype(jnp.float32)
    inv_freq = ROPE_THETA ** (-jnp.arange(0, MLA_ROPE, 2, dtype=jnp.float32) / MLA_ROPE)
    ang = pos[:, None] * inv_freq
    cos32, sin32 = jnp.cos(ang), jnp.sin(ang)
    cos_t = jnp.tile(cos32, (1, 4))
    sin_t = jnp.tile(jnp.concatenate([-sin32, sin32], axis=1), (1, 2))
    posc, posr = pos.reshape(s, 1), pos.reshape(1, s)
    two = lambda g: jnp.tile(g, (1, 2))
    gq2, gk2, gqr2, gkr2 = two(swa_q_norm_g), two(swa_k_norm_g), two(mla_qr_norm_g), two(mla_kr_norm_g)
    sinks1 = swa_sinks[0]

    proj, hn = _in_proj(x2, attn_norm_g, wi)
    qc, kc, vb, qb, kvb, cqn, ckvn = _mla_prep(proj, cos_t, sin_t, mla_cq_norm_g, mla_ckv_norm_g, wq, wkv,
                                                mla_qn_norm_g, gqr2, mla_kn_norm_g, gkr2)
    y_b, lse_b = _mla_fwd(qc, kc, vb)
    km, vmm, kvm, memn = _memkv_prep(mem2, mem_norm_g, wmkv, mem_k_norm_g)
    y_m, lse_m = _mem_fwd(proj, mem_q_norm_g, km, vmm)
    y_a, lse_a = _swa_fwd(proj, posc, posr, gq2, gk2, sinks1)
    h1, fn = _out_proj(y_a, y_b, y_m, x2, wo, ffn_norm_g)
    gu, act = _ffn_gu(fn, w_gu)
    dout, loss_tile = _ffn_down(act, w_d, h1, tgt)

    dgu, dw_d = _ffn_bwd_act(dout, w_d, gu)
    dw_gu = _ffn_dw_gu(fn, dgu)
    r_gu, r_d = _exchange_grads_background([dw_gu, dw_d], 2, "exchange_ffn_grads")
    d_h1, dg_ffn = _ffn_norm_bwd(_ffn_dfn(dgu, w_gu, dw_gu), dout, h1, ffn_norm_g)
    d_y = _mm(d_h1, wo, tb=True, out_dtype=jnp.float32, tm=FFN_TILE, tk=512, name="d_mix")
    dw_out = jnp.concatenate([
        _mm(y_a, d_h1, ta=True, out_dtype=WIRE, tm=1024, tk=512, name="dw_out_a"),
        _mm(y_b, d_h1, ta=True, out_dtype=WIRE, tm=1024, tk=512, name="dw_out_b"),
        _mm(y_m, d_h1, ta=True, out_dtype=WIRE, tm=1024, tk=512, name="dw_out_m")], axis=0)
    d_qm, dkm, dvmm, dg_mq = _mem_bwd(proj, mem_q_norm_g, km, vmm, d_y, y_m, lse_m)
    dw_mkv, dg_mem, dg_mk = _memkv_bwd(mem2, mem_norm_g, wmkv, mem_k_norm_g, kvm, memn, dkm, dvmm)
    r_mkv, r_out = _exchange_grads_background([dw_mkv.reshape(g_mkv.shape), dw_out.reshape(g_out.shape)], 3,
                                              "exchange_mix_grads")
    dqc, dkc, dvb = _mla_bwd(qc, kc, vb, d_y, y_b, lse_b, dw_mkv)
    (d_cq, d_ckv, d_kr, dw_uq, dw_ukv, dg_cq, dg_ckv, dg_qn, dg_qr, dg_kn, dg_kr) = _mla_prep_bwd(
        proj, cos_t, sin_t, mla_cq_norm_g, mla_ckv_norm_g, wq, wkv, mla_qn_norm_g, gqr2, mla_kn_norm_g, gkr2,
        qb, kvb, cqn, ckvn, dqc, dkc, dvb)
    d_qa, d_ka, d_va, dg_q, dg_k, d_sinks = _swa_bwd(proj, posc, posr, gq2, gk2, sinks1, d_y, y_a, lse_a, dw_out)
    d_proj = jnp.concatenate([d_qa, d_cq, d_ckv, d_qm, d_ka, d_va, d_kr], axis=1)
    gi = _dw_in(hn, d_proj, n_in)

    gq_ = jnp.concatenate(sum([[dw_uq[128 * h: 128 * (h + 1)], dw_uq[512 + 64 * h: 512 + 64 * (h + 1)]]
                               for h in range(4)], []), axis=0)
    gq_ = gq_.reshape(N_DEV, 96, 512)
    r_in, r_uq, r_ukv = _exchange_grads_background([gi, gq_, dw_ukv], 4, "exchange_in_grads")
    grad_x, dg_attn = _dx(d_proj, wi, x2, attn_norm_g, d_h1, gi)

    big = {}
    def adam(name, r, transposed=False, after=None):
        w, m, v = args[name][0], args["m_" + name][0], args["v_" + name][0]
        if transposed:
            outs = _adam_big(r, w.T, m.T, v.T, "adam_" + name, after)
            return [o.T[None] for o in outs]
        return [o[None] for o in _adam_big(r, w, m, v, "adam_" + name, after)]
    big["w_gate"] = adam("w_gate", r_gu[:, 0], True)
    big["w_up"] = adam("w_up", r_gu[:, 1], True, after=big["w_gate"][0])
    big["w_down"] = adam("w_down", r_d, after=big["w_up"][0])
    big["w_out"] = adam("w_out", r_out, after=big["w_down"][0])
    big["w_mem_kv"] = adam("w_mem_kv", r_mkv, after=big["w_out"][0])
    big["w_in"] = adam("w_in", r_in, True, after=big["w_mem_kv"][0])
    big["w_uq"] = adam("w_uq", r_uq, True, after=big["w_in"][0])
    big["w_ukv"] = adam("w_ukv", r_ukv, after=big["w_uq"][0])

    small_g = {
        "attn_norm_g": dg_attn, "swa_q_norm_g": _fold64(dg_q), "swa_k_norm_g": _fold64(dg_k),
        "swa_sinks": d_sinks, "mla_cq_norm_g": dg_cq, "mla_ckv_norm_g": dg_ckv, "mla_qn_norm_g": dg_qn,
        "mla_qr_norm_g": _fold64(dg_qr), "mla_kn_norm_g": dg_kn, "mla_kr_norm_g": _fold64(dg_kr),
        "mem_norm_g": dg_mem, "mem_q_norm_g": dg_mq, "mem_k_norm_g": dg_mk, "ffn_norm_g": dg_ffn}
    sizes = [args[n].shape[-1] for n in _SMALL]
    pg = _pack([small_g[n] for n in _SMALL] + [loss_tile[0:1, 0:1]])
    zero = jnp.zeros((1, 1), jnp.float32)
    pw = _pack([args[n] for n in _SMALL] + [zero])
    pm = _pack([args["m_" + n] for n in _SMALL] + [zero])
    pv = _pack([args["v_" + n] for n in _SMALL] + [zero])
    sg, sd, sm, sv = _small_allreduce_adam(pg, pw, pm, pv)
    small = {n: vals for n, vals in zip(_SMALL, zip(*[_unpack(p, sizes) for p in (sg, sd, sm, sv)]))}
    loss = _unpack(sg, sizes + [1])[-1].reshape(())

    order = ["attn_norm_g", "w_in", "swa_q_norm_g", "swa_k_norm_g", "swa_sinks", "mla_cq_norm_g", "mla_ckv_norm_g",
             "w_uq", "w_ukv", "mla_qn_norm_g", "mla_qr_norm_g", "mla_kn_norm_g", "mla_kr_norm_g", "mem_norm_g",
             "w_mem_kv", "mem_q_norm_g", "mem_k_norm_g", "w_out", "ffn_norm_g", "w_gate", "w_up", "w_down"]
    res = {n: (big[n] if n in big else list(small[n])) for n in order}
    outs = [loss, grad_x[None]]
    for kind in range(4):
        outs += [res[n][kind] for n in order]
    return tuple(outs)
```

```python
import jax
import jax.numpy as jnp
from jax import lax
from jax.experimental import pallas as pl
from jax.experimental.pallas import tpu as pltpu
from jax.experimental.pallas import tpu_sc as plsc

MXU = jnp.bfloat16
WIRE = jnp.bfloat16
EPS = 1e-6
NEG_INF = -1e30
N_DEV = 8
LANES = 128
ROW_TILE = 256
FFN_TILE = 512
ATT_TILE = 1024
SWA_BLOCK = 128
VMEM_LIMIT = 56 * 1024 * 1024

SWA_Q_HEADS, SWA_KV_HEADS, SWA_DIM = 16, 2, 64
MLA_HEADS, MLA_NOPE, MLA_ROPE, MLA_V = 4, 128, 64, 128
MEM_HEADS, MEM_DIM = 4, 128
ROPE_THETA = 10000.0
ADAM_LR, ADAM_B1, ADAM_B2, ADAM_EPS, ADAM_WD, ADAM_STEP = 0.001, 0.9, 0.999, 1e-08, 0.01, 10

C_QA, C_CQ, C_CKV, C_QM, C_KA, C_VA, C_KR, IN_PAD = 0, 1024, 1536, 2048, 2560, 2688, 2816, 2944


def _pcall(body, *, name, out_shape, in_specs, out_specs, grid=(), scratch=(), sem=None, after=None):
    params = pltpu.CompilerParams(dimension_semantics=sem, vmem_limit_bytes=VMEM_LIMIT)
    if after is not None:
        n_in, inner = len(in_specs), body

        def body(*refs):
            inner(*refs[:n_in], *refs[n_in + 1:])

        in_specs = list(in_specs) + [pl.BlockSpec(memory_space=pl.ANY)]
    call = pl.pallas_call(body, name=name, grid=grid, in_specs=in_specs, out_specs=out_specs,
                          out_shape=out_shape, scratch_shapes=list(scratch), compiler_params=params)
    return call if after is None else (lambda *ops: call(*ops, after))


def _sds(shape, dtype):
    return jax.ShapeDtypeStruct(tuple(shape), dtype)


def _dot(a, b):
    return jnp.dot(a.astype(MXU), b.astype(MXU), preferred_element_type=jnp.float32)


def _dot_nt(a, b):
    return lax.dot_general(a.astype(MXU), b.astype(MXU), (((1,), (1,)), ((), ())),
                           preferred_element_type=jnp.float32)


def _dot_tn(a, b):
    return lax.dot_general(a.astype(MXU), b.astype(MXU), (((0,), (0,)), ((), ())),
                           preferred_element_type=jnp.float32)


def _lo_mask(shape):
    return (lax.broadcasted_iota(jnp.int32, shape, len(shape) - 1) % LANES) < 64


def _norm_fwd(x, g, half=False):
    x2 = x * x
    if half:
        lo = _lo_mask(x.shape)
        s_lo = jnp.sum(jnp.where(lo, x2, 0.0), -1, keepdims=True)
        s_hi = jnp.sum(jnp.where(lo, 0.0, x2), -1, keepdims=True)
        r = jnp.where(lo, lax.rsqrt(s_lo / 64.0 + EPS), lax.rsqrt(s_hi / 64.0 + EPS))
    else:
        r = lax.rsqrt(jnp.mean(x2, -1, keepdims=True) + EPS)
    xn = x * r
    return xn * g, xn, r


def _norm_bwd(xn, r, g, dy, half=False):
    t = dy * g
    tx = t * xn
    if half:
        lo = _lo_mask(xn.shape)
        m_lo = jnp.sum(jnp.where(lo, tx, 0.0), -1, keepdims=True) / 64.0
        m_hi = jnp.sum(jnp.where(lo, 0.0, tx), -1, keepdims=True) / 64.0
        m = jnp.where(lo, m_lo, m_hi)
    else:
        m = jnp.mean(tx, -1, keepdims=True)
    dx = r * (t - xn * m)
    dg = jnp.sum(dy * xn, 0, keepdims=True)
    return dx, dg


def _swap32(x):
    lane = lax.broadcasted_iota(jnp.int32, x.shape, 1)
    return jnp.where((lane % 64) < 32, pltpu.roll(x, 96, 1), pltpu.roll(x, 32, 1))


def _rope(x, cos, sin):
    return x * cos + _swap32(x) * sin


def _rope_bwd(d, cos, sin):
    return d * cos + _swap32(d * sin)


def _my_coords():
    return lax.axis_index("x"), lax.axis_index("y"), lax.axis_index("c")


def _dev_index(px, py, pc):
    return 4 * px + 2 * py + pc


_FLIPS = [(0, 0, 1), (0, 1, 0), (0, 1, 1), (1, 0, 0), (1, 0, 1), (1, 1, 0), (1, 1, 1)]


def _flip(coords, f):
    return tuple((1 - v) if b else v for v, b in zip(coords, f))


def _all_gather(shards):
    n = len(shards)

    def body(*refs):
        ins, outs = refs[:n], refs[n:2 * n]
        send_sems, recv_sems, local_sems = refs[2 * n:]
        x, y, c = _my_coords()
        me, sibling = (x, y, c), (x, y, 1 - c)
        chips = [(1 - x, y), (x, 1 - y), (1 - x, 1 - y)]

        def copy(w, k, block, to, src=None):
            dst = outs[w].at[_dev_index(*block)]
            return pltpu.make_async_remote_copy(
                src_ref=dst if src is None else src, dst_ref=dst,
                send_sem=send_sems.at[w, k], recv_sem=recv_sems.at[w, k],
                device_id=to, device_id_type=pl.DeviceIdType.MESH)

        sends, locals_ = [], []
        for w in range(n):
            mine = pltpu.make_async_copy(ins[w], outs[w].at[_dev_index(*me)], local_sems.at[w])
            mine.start()
            locals_.append(mine)
            first = [copy(w, 0, me, sibling, src=ins[w])]
            first += [copy(w, 1 + j, me, (*chip, c), src=ins[w]) for j, chip in enumerate(chips)]
            for cp in first:
                cp.start()
            sends += first
        for w in range(n):
            for j, chip in enumerate(chips):
                copy(w, 1 + j, (*chip, c), me).wait_recv()
                fwd = copy(w, 4 + j, (*chip, c), sibling)
                fwd.start()
                sends.append(fwd)
        for w in range(n):
            copy(w, 0, sibling, me).wait_recv()
            for j, chip in enumerate(chips):
                copy(w, 4 + j, (*chip, 1 - c), me).wait_recv()
        for cp in sends:
            cp.wait_send()
        for mine in locals_:
            mine.wait()

    any_spec = pl.BlockSpec(memory_space=pl.ANY)
    return _pcall(
        body, name="all_gather_weights",
        out_shape=[_sds((N_DEV,) + s.shape, s.dtype) for s in shards],
        in_specs=[any_spec] * n, out_specs=[any_spec] * n,
        scratch=[pltpu.SemaphoreType.DMA((n, 7)), pltpu.SemaphoreType.DMA((n, 7)),
                 pltpu.SemaphoreType.DMA((n,))])(*shards)


def _wire_cost(arrays):
    nbytes = sum(a.size * a.dtype.itemsize for a in arrays)
    return pl.CostEstimate(flops=0, transcendentals=0, bytes_accessed=40 * nbytes)


def _all_gather_background(shards, collective_id, name):
    n = len(shards)
    src_refs = [jax.new_ref(s, memory_space=pltpu.MemorySpace.HBM) for s in shards]
    out_refs = [jax.empty_ref(_sds((N_DEV,) + s.shape, s.dtype), memory_space=pltpu.MemorySpace.HBM) for s in shards]

    @pl.kernel(mesh=plsc.ScalarSubcoreMesh(axis_name="seq", num_cores=1), name=name,
               scratch_types=(pltpu.SemaphoreType.DMA((n, 7)), pltpu.SemaphoreType.DMA((n, 7)),
                              pltpu.SemaphoreType.DMA((n,))),
               compiler_params=pltpu.CompilerParams(collective_id=collective_id))
    def launch(send_sems, recv_sems, local_sems):
        x, y, c = _my_coords()
        me, sibling = (x, y, c), (x, y, 1 - c)
        chips = [(1 - x, y), (x, 1 - y), (1 - x, 1 - y)]
        barrier = pltpu.get_barrier_semaphore()
        for peer in [sibling] + [(*chip, c) for chip in chips]:
            pl.semaphore_signal(barrier, inc=1, device_id=peer, device_id_type=pl.DeviceIdType.MESH)
        pl.semaphore_wait(barrier, 4)

        def copy(w, k, block, to, src=None):
            dst = out_refs[w].at[_dev_index(*block)]
            return pltpu.make_async_remote_copy(
                src_ref=dst if src is None else src, dst_ref=dst,
                send_sem=send_sems.at[w, k], recv_sem=recv_sems.at[w, k],
                device_id=to, device_id_type=pl.DeviceIdType.MESH)

        sends, locals_ = [], []
        for w in range(n):
            mine = pltpu.make_async_copy(src_refs[w], out_refs[w].at[_dev_index(*me)], local_sems.at[w])
            mine.start()
            locals_.append(mine)
            first = [copy(w, 0, me, sibling, src=src_refs[w])]
            first += [copy(w, 1 + j, me, (*chip, c), src=src_refs[w]) for j, chip in enumerate(chips)]
            for cp in first:
                cp.start()
            sends += first
        for w in range(n):
            for j, chip in enumerate(chips):
                copy(w, 1 + j, (*chip, c), me).wait_recv()
                fwd = copy(w, 4 + j, (*chip, c), sibling)
                fwd.start()
                sends.append(fwd)
        for w in range(n):
            copy(w, 0, sibling, me).wait_recv()
            for j, chip in enumerate(chips):
                copy(w, 4 + j, (*chip, 1 - c), me).wait_recv()
        for cp in sends:
            cp.wait_send()
        for mine in locals_:
            mine.wait()

    launch()
    return [r[...] for r in out_refs]


def _exchange_grads(grads):
    n = len(grads)

    def body(*refs):
        ins, outs = refs[:n], refs[n:2 * n]
        send_sems, recv_sems, local_sems = refs[2 * n:]
        me = _my_coords()
        my_idx = _dev_index(*me)
        sends, locals_ = [], []
        for w in range(n):
            mine = pltpu.make_async_copy(ins[w].at[my_idx], outs[w].at[my_idx], local_sems.at[w])
            mine.start()
            locals_.append(mine)
            for k, f in enumerate(_FLIPS):
                peer = _flip(me, f)
                cp = pltpu.make_async_remote_copy(
                    src_ref=ins[w].at[_dev_index(*peer)], dst_ref=outs[w].at[my_idx],
                    send_sem=send_sems.at[w, k], recv_sem=recv_sems.at[w, k],
                    device_id=peer, device_id_type=pl.DeviceIdType.MESH)
                cp.start()
                sends.append(cp)
        for w in range(n):
            for k, f in enumerate(_FLIPS):
                peer = _flip(me, f)
                slot = outs[w].at[_dev_index(*peer)]
                pltpu.make_async_remote_copy(
                    src_ref=slot, dst_ref=slot,
                    send_sem=send_sems.at[w, k], recv_sem=recv_sems.at[w, k],
                    device_id=peer, device_id_type=pl.DeviceIdType.MESH).wait_recv()
        for cp in sends:
            cp.wait_send()
        for mine in locals_:
            mine.wait()

    any_spec = pl.BlockSpec(memory_space=pl.ANY)
    return _pcall(
        body, name="exchange_grads",
        out_shape=[_sds(g.shape, g.dtype) for g in grads],
        in_specs=[any_spec] * n, out_specs=[any_spec] * n,
        scratch=[pltpu.SemaphoreType.DMA((n, 7)), pltpu.SemaphoreType.DMA((n, 7)),
                 pltpu.SemaphoreType.DMA((n,))])(*grads)


def _exchange_grads_background(grads, collective_id, name):
    n = len(grads)
    src_refs = [jax.new_ref(g, memory_space=pltpu.MemorySpace.HBM) for g in grads]
    out_refs = [jax.empty_ref(_sds(g.shape, g.dtype), memory_space=pltpu.MemorySpace.HBM) for g in grads]

    @pl.kernel(mesh=plsc.ScalarSubcoreMesh(axis_name="seq", num_cores=1), name=name,
               scratch_types=(pltpu.SemaphoreType.DMA((n, 7)), pltpu.SemaphoreType.DMA((n, 7)),
                              pltpu.SemaphoreType.DMA((n,))),
               cost_estimate=_wire_cost(grads),
               compiler_params=pltpu.CompilerParams(collective_id=collective_id))
    def launch(send_sems, recv_sems, local_sems):
        me = _my_coords()
        my_idx = _dev_index(*me)
        peers = [_flip(me, f) for f in _FLIPS]
        barrier = pltpu.get_barrier_semaphore()
        for peer in peers:
            pl.semaphore_signal(barrier, inc=1, device_id=peer, device_id_type=pl.DeviceIdType.MESH)
        pl.semaphore_wait(barrier, len(peers))
        sends, locals_ = [], []
        for w in range(n):
            mine = pltpu.make_async_copy(src_refs[w].at[my_idx], out_refs[w].at[my_idx], local_sems.at[w])
            mine.start()
            locals_.append(mine)
            for k, peer in enumerate(peers):
                cp = pltpu.make_async_remote_copy(
                    src_ref=src_refs[w].at[_dev_index(*peer)], dst_ref=out_refs[w].at[my_idx],
                    send_sem=send_sems.at[w, k], recv_sem=recv_sems.at[w, k],
                    device_id=peer, device_id_type=pl.DeviceIdType.MESH)
                cp.start()
                sends.append(cp)
        for w in range(n):
            for k, peer in enumerate(peers):
                slot = out_refs[w].at[_dev_index(*peer)]
                pltpu.make_async_remote_copy(
                    src_ref=slot, dst_ref=slot, send_sem=send_sems.at[w, k], recv_sem=recv_sems.at[w, k],
                    device_id=peer, device_id_type=pl.DeviceIdType.MESH).wait_recv()
        for cp in sends:
            cp.wait_send()
        for mine in locals_:
            mine.wait()

    launch()
    return [r[...] for r in out_refs]


def _to_wire(parts, after, name):
    n = len(parts)
    rows, cols = parts[0].shape
    tr = rows // 2 if rows % 32 == 0 else rows

    def body(*refs):
        for k in range(n):
            refs[n][k] = refs[k][...].astype(WIRE)

    blk = pl.BlockSpec((tr, cols), lambda i: (i, 0))
    return _pcall(
        body, name=name, grid=(rows // tr,), out_shape=_sds((n, rows, cols), WIRE),
        in_specs=[blk] * n, out_specs=pl.BlockSpec((n, tr, cols), lambda i: (0, i, 0)),
        sem=("parallel",), after=after)(*parts)


def _adam_math(w, g, m, v):
    m = ADAM_B1 * m + (1.0 - ADAM_B1) * g
    v = ADAM_B2 * v + (1.0 - ADAM_B2) * (g * g)
    m_hat = m / (1.0 - ADAM_B1 ** ADAM_STEP)
    v_hat = v / (1.0 - ADAM_B2 ** ADAM_STEP)
    delta = -ADAM_LR * (m_hat / (jnp.sqrt(v_hat) + ADAM_EPS) + ADAM_WD * w)
    return delta, m, v


def _small_allreduce_adam(pg, pw, pm, pv):
    rows = pg.shape[0]

    def body(pg_ref, pw_ref, pm_ref, pv_ref, g_ref, d_ref, m_ref, v_ref, gath, send_sems, recv_sems):
        me = _my_coords()
        my_idx = _dev_index(*me)
        gath[my_idx] = pg_ref[...]
        sends = []
        for k, f in enumerate(_FLIPS):
            peer = _flip(me, f)
            cp = pltpu.make_async_remote_copy(
                src_ref=pg_ref, dst_ref=gath.at[my_idx],
                send_sem=send_sems.at[k], recv_sem=recv_sems.at[k],
                device_id=peer, device_id_type=pl.DeviceIdType.MESH)
            cp.start()
            sends.append(cp)
        for k, f in enumerate(_FLIPS):
            peer = _flip(me, f)
            slot = gath.at[_dev_index(*peer)]
            pltpu.make_async_remote_copy(
                src_ref=slot, dst_ref=slot, send_sem=send_sems.at[k], recv_sem=recv_sems.at[k],
                device_id=peer, device_id_type=pl.DeviceIdType.MESH).wait_recv()
        for cp in sends:
            cp.wait_send()
        g = gath[0]
        for d in range(1, N_DEV):
            g = g + gath[d]
        delta, m, v = _adam_math(pw_ref[...], g, pm_ref[...], pv_ref[...])
        g_ref[...] = g
        d_ref[...] = delta
        m_ref[...] = m
        v_ref[...] = v

    vm = pl.BlockSpec(memory_space=pltpu.VMEM)
    return _pcall(
        body, name="small_allreduce_adam",
        out_shape=[_sds(pg.shape, jnp.float32)] * 4,
        in_specs=[vm] * 4, out_specs=[vm] * 4,
        scratch=[pltpu.VMEM((N_DEV, rows, LANES), jnp.float32),
                 pltpu.SemaphoreType.DMA((7,)), pltpu.SemaphoreType.DMA((7,))])(pg, pw, pm, pv)


def _adam_big(recv, w, m, v, name, after=None, which=None):
    rows, cols = recv.shape[-2:]
    tc = 512 if cols % 512 == 0 else cols
    tr = rows
    while tr * tc > 256 * 1024 and tr % 2 == 0 and (tr // 2) % 16 == 0:
        tr //= 2

    def body(r_ref, w_ref, m_ref, v_ref, g_ref, d_ref, mo_ref, vo_ref):
        g = r_ref[0].astype(jnp.float32)
        for d in range(1, N_DEV):
            g = g + r_ref[d].astype(jnp.float32)
        delta, mn, vn = _adam_math(w_ref[...], g, m_ref[...], v_ref[...])
        g_ref[...] = g
        d_ref[...] = delta
        mo_ref[...] = mn
        vo_ref[...] = vn

    blk = pl.BlockSpec((tr, tc), lambda i, j: (i, j))
    if which is None:
        r_spec = pl.BlockSpec((N_DEV, tr, tc), lambda i, j: (0, i, j))
    else:
        r_spec = pl.BlockSpec((N_DEV, None, tr, tc), lambda i, j: (0, which, i, j))
    return _pcall(
        body, name=name, grid=(rows // tr, cols // tc),
        out_shape=[_sds((rows, cols), jnp.float32)] * 4,
        in_specs=[r_spec, blk, blk, blk],
        out_specs=[blk] * 4, sem=("parallel", "parallel"), after=after)(recv, w, m, v)


def _mm(a, b, *, ta=False, tb=False, out_dtype, tm, tk, name):
    (kdim, mdim) = a.shape if ta else a.shape[::-1]
    ndim = b.shape[0] if tb else b.shape[1]
    tm, tk = min(tm, mdim), min(tk, kdim)
    nk = kdim // tk

    def body(a_ref, b_ref, o_ref, acc):
        k = pl.program_id(1)
        if ta:
            part = _dot_tn(a_ref[...], b_ref[...])
        elif tb:
            part = _dot_nt(a_ref[...], b_ref[...])
        else:
            part = _dot(a_ref[...], b_ref[...])

        @pl.when(k == 0)
        def _():
            acc[...] = part

        @pl.when(k > 0)
        def _():
            acc[...] += part

        @pl.when(k == nk - 1)
        def _():
            o_ref[...] = acc[...].astype(o_ref.dtype)

    a_spec = pl.BlockSpec((tk, tm), lambda i, k: (k, i)) if ta else pl.BlockSpec((tm, tk), lambda i, k: (i, k))
    b_spec = pl.BlockSpec((ndim, tk), lambda i, k: (0, k)) if tb else pl.BlockSpec((tk, ndim), lambda i, k: (k, 0))
    return _pcall(
        body, name=name, grid=(mdim // tm, nk), out_shape=_sds((mdim, ndim), out_dtype),
        in_specs=[a_spec, b_spec], out_specs=pl.BlockSpec((tm, ndim), lambda i, k: (i, 0)),
        scratch=[pltpu.VMEM((tm, ndim), jnp.float32)], sem=("parallel", "arbitrary"))(a, b)


def _ref_col_pieces(start, stop):
    ref_starts = [0, 1024, 1152, 1280, 1792, 2304, 2368, 2880]
    perm_starts = [C_QA, C_KA, C_VA, C_CQ, C_CKV, C_KR, C_QM]
    out = []
    for p in range(7):
        lo, hi = max(start, ref_starts[p]), min(stop, ref_starts[p + 1])
        if lo < hi:
            out.append((lo - start, perm_starts[p] + lo - ref_starts[p], hi - lo))
    return out


def _dw_in(hn, d_proj, n_shard):
    s, d = hn.shape
    n = d_proj.shape[1]
    tm, tk = min(512, d), min(512, s)
    nk = s // tk

    def body(a_ref, b_ref, o_ref, acc):
        k = pl.program_id(1)
        part = _dot_tn(a_ref[...], b_ref[...])

        @pl.when(k == 0)
        def _():
            acc[...] = part

        @pl.when(k > 0)
        def _():
            acc[...] += part

        @pl.when(k == nk - 1)
        def _():
            t = acc[...].T
            for j in range(N_DEV):
                rows = [t[src:src + width] for _, src, width in _ref_col_pieces(j * n_shard, (j + 1) * n_shard)]
                o_ref[j] = jnp.concatenate(rows, axis=0).astype(o_ref.dtype)

    return _pcall(
        body, name="dw_in", grid=(d // tm, nk), out_shape=_sds((N_DEV, n_shard, d), WIRE),
        in_specs=[pl.BlockSpec((tk, tm), lambda i, k: (k, i)), pl.BlockSpec((tk, n), lambda i, k: (k, 0))],
        out_specs=pl.BlockSpec((N_DEV, n_shard, tm), lambda i, k: (0, 0, i)),
        scratch=[pltpu.VMEM((tm, n), jnp.float32)], sem=("parallel", "arbitrary"))(hn, d_proj)


def _in_proj(x, g, w):
    s, d = x.shape
    n = w.shape[0]
    tm = min(ROW_TILE, s)

    def body(x_ref, g_ref, w_ref, p_ref, hn_ref):
        hn, _, _ = _norm_fwd(x_ref[...], g_ref[...])
        hn_ref[...] = hn.astype(hn_ref.dtype)
        p_ref[...] = _dot_nt(hn, w_ref[...])

    return _pcall(
        body, name="in_proj", grid=(s // tm,),
        out_shape=[_sds((s, n), jnp.float32), _sds((s, d), MXU)],
        in_specs=[pl.BlockSpec((tm, d), lambda i: (i, 0)), pl.BlockSpec((1, d), lambda i: (0, 0)),
                  pl.BlockSpec((n, d), lambda i: (0, 0))],
        out_specs=[pl.BlockSpec((tm, n), lambda i: (i, 0)), pl.BlockSpec((tm, d), lambda i: (i, 0))],
        sem=("parallel",))(x, g, w)


def _mla_prep(proj, cos, sin, g_cq, g_ckv, w_uq, w_ukv, g_qn, g_qr, g_kn, g_kr):
    s = proj.shape[0]
    tm = min(ROW_TILE, s)
    nh = MLA_HEADS

    def body(cq_ref, ckv_ref, kr_ref, cos_ref, sin_ref, gcq_ref, gckv_ref, wuq_ref, wukv_ref,
             gqn_ref, gqr_ref, gkn_ref, gkr_ref,
             qc_ref, kc_ref, v_ref, qb_ref, kvb_ref, cqn_ref, ckvn_ref):
        cos_t, sin_t = cos_ref[...], sin_ref[...]
        lo = _lo_mask((tm, LANES))
        cqn, _, _ = _norm_fwd(cq_ref[...], gcq_ref[...])
        cqn_ref[...] = cqn.astype(cqn_ref.dtype)
        qb = _dot_nt(cqn, wuq_ref[...])
        qb_ref[...] = qb
        ckvn, _, _ = _norm_fwd(ckv_ref[...], gckv_ref[...])
        ckvn_ref[...] = ckvn.astype(ckvn_ref.dtype)
        kvb = jnp.concatenate([_dot(ckvn, wukv_ref[dev]) for dev in range(N_DEV)], axis=1)
        kvb_ref[...] = kvb
        kr, _, _ = _norm_fwd(kr_ref[...], gkr_ref[...], half=True)
        kr = _rope(kr, cos_t, sin_t)
        kr2 = jnp.where(lo, kr, pltpu.roll(kr, 64, 1))
        ropes = []
        for j in range(nh // 2):
            xr = qb[:, nh * MLA_NOPE + LANES * j: nh * MLA_NOPE + LANES * (j + 1)]
            qr, _, _ = _norm_fwd(xr, gqr_ref[...], half=True)
            ropes.append(_rope(qr, cos_t, sin_t))
        for h in range(nh):
            qn, _, _ = _norm_fwd(qb[:, MLA_NOPE * h: MLA_NOPE * (h + 1)], gqn_ref[...])
            mask = lo if h % 2 == 0 else jnp.logical_not(lo)
            qr = jnp.where(mask, ropes[h // 2], 0.0)
            qc_ref[h] = jnp.concatenate([qn, qr], axis=1).astype(qc_ref.dtype)
            kn, _, _ = _norm_fwd(kvb[:, 256 * h: 256 * h + MLA_NOPE], gkn_ref[...])
            kc_ref[h] = jnp.concatenate([kn, kr2], axis=1).astype(kc_ref.dtype)
            v_ref[h] = kvb[:, 256 * h + MLA_NOPE: 256 * (h + 1)].astype(v_ref.dtype)

    def col(width, start):
        return pl.BlockSpec((tm, width), lambda i: (i, start // width))

    def full(shape):
        return pl.BlockSpec(shape, lambda i: (0,) * len(shape))

    def row(width):
        return pl.BlockSpec((tm, width), lambda i: (i, 0))

    def heads(width):
        return pl.BlockSpec((nh, tm, width), lambda i: (0, i, 0))

    return _pcall(
        body, name="mla_prep", grid=(s // tm,),
        out_shape=[_sds((nh, s, 256), MXU), _sds((nh, s, 256), MXU), _sds((nh, s, MLA_V), MXU),
                   _sds((s, 768), jnp.float32), _sds((s, 1024), jnp.float32),
                   _sds((s, 512), MXU), _sds((s, 512), MXU)],
        in_specs=[col(512, C_CQ), col(512, C_CKV), col(LANES, C_KR), row(LANES), row(LANES),
                  full((1, 512)), full((1, 512)), full((768, 512)), full((N_DEV, 512, LANES)),
                  full((1, LANES)), full((1, LANES)), full((1, LANES)), full((1, LANES))],
        out_specs=[heads(256), heads(256), heads(MLA_V), row(768), row(1024), row(512), row(512)],
        sem=("parallel",))(proj, proj, proj, cos, sin, g_cq, g_ckv, w_uq, w_ukv, g_qn, g_qr, g_kn, g_kr)


def _mla_fwd(qc, kc, v):
    nh, s, _ = qc.shape
    t = min(ATT_TILE, s)
    nb = s // t
    scale = (MLA_NOPE + MLA_ROPE) ** -0.5

    def body(q_ref, k_ref, v_ref, y_ref, lse_ref, m_sc, l_sc, acc):
        qi, ki = pl.program_id(1), pl.program_id(2)

        @pl.when(ki == 0)
        def _():
            m_sc[...] = jnp.full_like(m_sc, NEG_INF)
            l_sc[...] = jnp.zeros_like(l_sc)
            acc[...] = jnp.zeros_like(acc)

        @pl.when(ki <= qi)
        def _():
            sc = _dot_nt(q_ref[0], k_ref[0]) * scale
            r_i = lax.broadcasted_iota(jnp.int32, sc.shape, 0) + qi * t
            c_i = lax.broadcasted_iota(jnp.int32, sc.shape, 1) + ki * t
            sc = jnp.where(c_i <= r_i, sc, NEG_INF)
            m_new = jnp.maximum(m_sc[...], jnp.max(sc, -1, keepdims=True))
            alpha = jnp.exp(m_sc[...] - m_new)
            p = jnp.exp(sc - m_new)
            l_sc[...] = alpha * l_sc[...] + jnp.sum(p, -1, keepdims=True)
            acc[...] = alpha * acc[...] + _dot(p, v_ref[0])
            m_sc[...] = m_new

        @pl.when(ki == qi)
        def _():
            y_ref[...] = acc[...] / l_sc[...]
            lse_ref[0] = m_sc[...] + jnp.log(l_sc[...])

    return _pcall(
        body, name="mla_fwd", grid=(nh, nb, nb),
        out_shape=[_sds((s, nh * MLA_V), jnp.float32), _sds((nh, s, 1), jnp.float32)],
        in_specs=[pl.BlockSpec((1, t, 256), lambda h, i, k: (h, i, 0)),
                  pl.BlockSpec((1, t, 256), lambda h, i, k: (h, jnp.minimum(k, i), 0)),
                  pl.BlockSpec((1, t, MLA_V), lambda h, i, k: (h, jnp.minimum(k, i), 0))],
        out_specs=[pl.BlockSpec((t, MLA_V), lambda h, i, k: (i, h)),
                   pl.BlockSpec((1, t, 1), lambda h, i, k: (h, i, 0))],
        scratch=[pltpu.VMEM((t, 1), jnp.float32), pltpu.VMEM((t, 1), jnp.float32),
                 pltpu.VMEM((t, MLA_V), jnp.float32)],
        sem=("parallel", "parallel", "arbitrary"))(qc, kc, v)


def _memkv_prep(mem, g_mem, w_mkv, g_mk):
    ml, d = mem.shape
    hw = MEM_HEADS * MEM_DIM

    def body(mem_ref, g_ref, w_ref, gk_ref, k_ref, v_ref, kv_ref, mn_ref):
        mn, _, _ = _norm_fwd(mem_ref[...], g_ref[...])
        mn_ref[...] = mn.astype(mn_ref.dtype)
        kv = _dot(mn, w_ref[...])
        kv_ref[...] = kv
        for h in range(MEM_HEADS):
            kn, _, _ = _norm_fwd(kv[:, MEM_DIM * h: MEM_DIM * (h + 1)], gk_ref[...])
            k_ref[:, MEM_DIM * h: MEM_DIM * (h + 1)] = kn.astype(k_ref.dtype)
        v_ref[...] = kv[:, hw:].astype(v_ref.dtype)

    vm = pl.BlockSpec(memory_space=pltpu.VMEM)
    return _pcall(
        body, name="memkv_prep",
        out_shape=[_sds((ml, hw), MXU), _sds((ml, hw), MXU), _sds((ml, 2 * hw), jnp.float32), _sds((ml, d), MXU)],
        in_specs=[vm] * 4, out_specs=[vm] * 4)(mem, g_mem, w_mkv, g_mk)


def _mem_fwd(proj, g_mq, km, vmm):
    s = proj.shape[0]
    ml, hw = km.shape
    tm = min(FFN_TILE, s)
    scale = MEM_DIM ** -0.5

    def body(q_ref, g_ref, k_ref, v_ref, y_ref, lse_ref):
        col = lax.broadcasted_iota(jnp.int32, (tm, MEM_HEADS), 1)
        lse_t = jnp.zeros((tm, MEM_HEADS), jnp.float32)
        for h in range(MEM_HEADS):
            sl = slice(MEM_DIM * h, MEM_DIM * (h + 1))
            qn, _, _ = _norm_fwd(q_ref[:, sl], g_ref[...])
            sc = _dot_nt(qn, k_ref[:, sl]) * scale
            m = jnp.max(sc, -1, keepdims=True)
            p = jnp.exp(sc - m)
            l = jnp.sum(p, -1, keepdims=True)
            y_ref[:, sl] = _dot(p, v_ref[:, sl]) / l
            lse_t = jnp.where(col == h, m + jnp.log(l), lse_t)
        lse_ref[...] = lse_t

    return _pcall(
        body, name="mem_fwd", grid=(s // tm,),
        out_shape=[_sds((s, hw), jnp.float32), _sds((s, MEM_HEADS), jnp.float32)],
        in_specs=[pl.BlockSpec((tm, hw), lambda i: (i, C_QM // hw)), pl.BlockSpec((1, MEM_DIM), lambda i: (0, 0)),
                  pl.BlockSpec((ml, hw), lambda i: (0, 0)), pl.BlockSpec((ml, hw), lambda i: (0, 0))],
        out_specs=[pl.BlockSpec((tm, hw), lambda i: (i, 0)), pl.BlockSpec((tm, MEM_HEADS), lambda i: (i, 0))],
        sem=("parallel",))(proj, g_mq, km, vmm)


def _alibi_slope(h):
    return float(2.0 ** (-8.0 * (h + 1) / SWA_Q_HEADS))


def _swa_common(n, kp, kc, vp, vc, pq, pkp, pkc, gk):
    b = SWA_BLOCK
    k_raw = jnp.concatenate([kp, kc], axis=0)
    kn, kxn, kr = _norm_fwd(k_raw, gk, half=True)
    v = jnp.concatenate([vp, vc], axis=0)
    dist = jnp.abs(pq - jnp.concatenate([pkp, pkc], axis=1))
    r_i = lax.broadcasted_iota(jnp.int32, (b, 2 * b), 0)
    c_i = lax.broadcasted_iota(jnp.int32, (b, 2 * b), 1)
    valid = (c_i > r_i) & (c_i <= r_i + b) & (c_i >= jnp.where(n > 0, 0, b))
    return kn, v, dist, valid


def _swa_specs(s):
    b = SWA_BLOCK
    prev = lambda n: jnp.maximum(n - 1, 0)
    return [
        pl.BlockSpec((b, 1024), lambda n: (n, C_QA // 1024)),
        pl.BlockSpec((b, LANES), lambda n: (prev(n), C_KA // LANES)),
        pl.BlockSpec((b, LANES), lambda n: (n, C_KA // LANES)),
        pl.BlockSpec((b, LANES), lambda n: (prev(n), C_VA // LANES)),
        pl.BlockSpec((b, LANES), lambda n: (n, C_VA // LANES)),
        pl.BlockSpec((b, 1), lambda n: (n, 0)),
        pl.BlockSpec((1, b), lambda n: (0, prev(n))),
        pl.BlockSpec((1, b), lambda n: (0, n)),
        pl.BlockSpec((1, LANES), lambda n: (0, 0)),
        pl.BlockSpec((1, LANES), lambda n: (0, 0)),
        pl.BlockSpec(memory_space=pltpu.SMEM),
    ]


def _swa_fwd(proj, posc, posr, gq, gk, sinks):
    s = proj.shape[0]
    b = SWA_BLOCK
    scale = SWA_DIM ** -0.5

    def body(q_ref, kp_ref, kc_ref, vp_ref, vc_ref, pq_ref, pkp_ref, pkc_ref, gq_ref, gk_ref, sink_ref,
             y_ref, lse_ref):
        n = pl.program_id(0)
        kn, v, dist, valid = _swa_common(n, kp_ref[...], kc_ref[...], vp_ref[...], vc_ref[...],
                                         pq_ref[...], pkp_ref[...], pkc_ref[...], gk_ref[...])
        lo = _lo_mask((b, LANES))
        col = lax.broadcasted_iota(jnp.int32, (b, SWA_Q_HEADS), 1)
        lse_t = jnp.zeros((b, SWA_Q_HEADS), jnp.float32)
        for j in range(SWA_Q_HEADS // 2):
            hk = (2 * j) // (SWA_Q_HEADS // SWA_KV_HEADS)
            kvmask = lo if hk == 0 else jnp.logical_not(lo)
            qn, _, _ = _norm_fwd(q_ref[:, LANES * j: LANES * (j + 1)], gq_ref[...], half=True)
            qsw = pltpu.roll(qn, 64, 1)
            outs = []
            for e in range(2):
                h = 2 * j + e
                qm = jnp.where(kvmask, qn if e == hk else qsw, 0.0)
                sc = _dot_nt(qm, kn) * scale - _alibi_slope(h) * dist
                sc = jnp.where(valid, sc, NEG_INF)
                sk = sink_ref[h]
                m = jnp.maximum(jnp.max(sc, -1, keepdims=True), sk)
                p = jnp.exp(sc - m)
                l = jnp.sum(p, -1, keepdims=True) + jnp.exp(sk - m)
                o = _dot(p, v) / l
                outs.append(o if e == hk else pltpu.roll(o, 64, 1))
                lse_t = jnp.where(col == h, m + jnp.log(l), lse_t)
            y_ref[:, LANES * j: LANES * (j + 1)] = jnp.where(lo, outs[0], outs[1])
        lse_ref[...] = lse_t

    return _pcall(
        body, name="swa_fwd", grid=(s // b,),
        out_shape=[_sds((s, 1024), jnp.float32), _sds((s, SWA_Q_HEADS), jnp.float32)],
        in_specs=_swa_specs(s),
        out_specs=[pl.BlockSpec((b, 1024), lambda n: (n, 0)), pl.BlockSpec((b, SWA_Q_HEADS), lambda n: (n, 0))],
        sem=("parallel",))(proj, proj, proj, proj, proj, posc, posr, posr, gq, gk, sinks)


def _out_proj(y_a, y_b, y_m, x, w_out, g_ffn):
    s, d = x.shape
    tm = min(ROW_TILE, s)

    def body(ya_ref, yb_ref, ym_ref, x_ref, w_ref, g_ref, h1_ref, fn_ref):
        y = jnp.concatenate([ya_ref[...].astype(MXU), yb_ref[...].astype(MXU), ym_ref[...].astype(MXU)], axis=1)
        h1 = x_ref[...] + _dot(y, w_ref[...])
        h1_ref[...] = h1
        fn, _, _ = _norm_fwd(h1, g_ref[...])
        fn_ref[...] = fn.astype(fn_ref.dtype)

    def row(width):
        return pl.BlockSpec((tm, width), lambda i: (i, 0))

    return _pcall(
        body, name="out_proj", grid=(s // tm,),
        out_shape=[_sds((s, d), jnp.float32), _sds((s, d), MXU)],
        in_specs=[row(1024), row(512), row(512), row(d), pl.BlockSpec(w_out.shape, lambda i: (0, 0)),
                  pl.BlockSpec((1, d), lambda i: (0, 0))],
        out_specs=[row(d), row(d)], sem=("parallel",))(y_a, y_b, y_m, x, w_out, g_ffn)


def _ffn_gu(fn, w_gu):
    s, d = fn.shape
    f = w_gu.shape[2]
    tm = min(FFN_TILE, s)

    def body(fn_ref, w_ref, gu_ref, act_ref):
        x = fn_ref[...]
        g = _dot_nt(x, w_ref[0, 0])
        u = _dot_nt(x, w_ref[0, 1])
        gu_ref[0, 0] = g
        gu_ref[0, 1] = u
        act_ref[0] = (g * jax.nn.sigmoid(g) * u).astype(act_ref.dtype)

    return _pcall(
        body, name="ffn_gate_up", grid=(N_DEV, s // tm),
        out_shape=[_sds((N_DEV, 2, s, f), jnp.float32), _sds((N_DEV, s, f), MXU)],
        in_specs=[pl.BlockSpec((tm, d), lambda j, i: (i, 0)),
                  pl.BlockSpec((1, 2, f, d), lambda j, i: (j, 0, 0, 0))],
        out_specs=[pl.BlockSpec((1, 2, tm, f), lambda j, i: (j, 0, i, 0)),
                   pl.BlockSpec((1, tm, f), lambda j, i: (j, i, 0))],
        sem=("parallel", "parallel"))(fn, w_gu)


def _ffn_down(act, w_d, h1, target):
    _, s, f = act.shape
    d = h1.shape[1]
    tm = min(FFN_TILE, s)

    def body(a_ref, w_ref, h1_ref, t_ref, dout_ref, loss_ref, acc):
        i, j = pl.program_id(0), pl.program_id(1)
        part = _dot(a_ref[0], w_ref[0]) + _dot(a_ref[1], w_ref[1])

        @pl.when(j == 0)
        def _():
            acc[...] = h1_ref[...] + part

        @pl.when(j > 0)
        def _():
            acc[...] += part

        @pl.when((i == 0) & (j == 0))
        def _():
            loss_ref[...] = jnp.zeros_like(loss_ref)

        @pl.when(j == N_DEV // 2 - 1)
        def _():
            diff = acc[...] - t_ref[...]
            dout_ref[...] = diff / d
            loss_ref[...] += 0.5 * jnp.sum(jnp.sum(diff * diff, -1, keepdims=True) / d)

    row = pl.BlockSpec((tm, d), lambda i, j: (i, 0))
    return _pcall(
        body, name="ffn_down", grid=(s // tm, N_DEV // 2),
        out_shape=[_sds((s, d), jnp.float32), _sds((8, LANES), jnp.float32)],
        in_specs=[pl.BlockSpec((2, tm, f), lambda i, j: (j, i, 0)), pl.BlockSpec((2, f, d), lambda i, j: (j, 0, 0)),
                  row, row],
        out_specs=[row, pl.BlockSpec((8, LANES), lambda i, j: (0, 0))],
        scratch=[pltpu.VMEM((tm, d), jnp.float32)], sem=("arbitrary", "arbitrary"))(act, w_d, h1, target)


def _ffn_bwd_act(dout, w_d, gu):
    s, d = dout.shape
    f = w_d.shape[1]
    tm = min(FFN_TILE, s)
    ni = s // tm

    def body(do_ref, w_ref, gu_ref, dgu_ref, dw_ref, acc):
        i = pl.program_id(1)
        do = do_ref[...].astype(MXU)
        d_act = _dot_nt(do, w_ref[0])
        g, u = gu_ref[0, 0], gu_ref[0, 1]
        sig = jax.nn.sigmoid(g)
        silu = g * sig
        dgu_ref[0, 0] = (d_act * u * (sig * (1.0 + g * (1.0 - sig)))).astype(dgu_ref.dtype)
        dgu_ref[0, 1] = (d_act * silu).astype(dgu_ref.dtype)
        part = _dot_tn(silu * u, do)

        @pl.when(i == 0)
        def _():
            acc[...] = part

        @pl.when(i > 0)
        def _():
            acc[...] += part

        @pl.when(i == ni - 1)
        def _():
            dw_ref[0] = acc[...].astype(dw_ref.dtype)

    return _pcall(
        body, name="ffn_bwd_act", grid=(N_DEV, ni),
        out_shape=[_sds((N_DEV, 2, s, f), MXU), _sds((N_DEV, f, d), WIRE)],
        in_specs=[pl.BlockSpec((tm, d), lambda j, i: (i, 0)), pl.BlockSpec((1, f, d), lambda j, i: (j, 0, 0)),
                  pl.BlockSpec((1, 2, tm, f), lambda j, i: (j, 0, i, 0))],
        out_specs=[pl.BlockSpec((1, 2, tm, f), lambda j, i: (j, 0, i, 0)),
                   pl.BlockSpec((1, f, d), lambda j, i: (j, 0, 0))],
        scratch=[pltpu.VMEM((f, d), jnp.float32)], sem=("parallel", "arbitrary"))(dout, w_d, gu)


def _ffn_dw_gu(fn, dgu):
    s, d = fn.shape
    f = dgu.shape[-1]
    tk = min(2 * FFN_TILE, s)
    nk = s // tk

    def body(fn_ref, dgu_ref, dw_ref, acc):
        k = pl.program_id(1)
        x = fn_ref[...]
        pg = _dot_tn(dgu_ref[0, 0], x)
        pu = _dot_tn(dgu_ref[0, 1], x)

        @pl.when(k == 0)
        def _():
            acc[0] = pg
            acc[1] = pu

        @pl.when(k > 0)
        def _():
            acc[0] += pg
            acc[1] += pu

        @pl.when(k == nk - 1)
        def _():
            dw_ref[0] = acc[...].astype(dw_ref.dtype)

    return _pcall(
        body, name="ffn_dw_gate_up", grid=(N_DEV, nk),
        out_shape=_sds((N_DEV, 2, f, d), WIRE),
        in_specs=[pl.BlockSpec((tk, d), lambda j, k: (k, 0)), pl.BlockSpec((1, 2, tk, f), lambda j, k: (j, 0, k, 0))],
        out_specs=pl.BlockSpec((1, 2, f, d), lambda j, k: (j, 0, 0, 0)),
        scratch=[pltpu.VMEM((2, f, d), jnp.float32)], sem=("parallel", "arbitrary"))(fn, dgu)


def _ffn_dfn(dgu, w_gu, after):
    _, _, s, f = dgu.shape
    d = w_gu.shape[3]
    tm = min(FFN_TILE, s)

    def body(dgu_ref, w_ref, dfn_ref):
        j = pl.program_id(1)
        part = (_dot(dgu_ref[0, 0], w_ref[0, 0]) + _dot(dgu_ref[0, 1], w_ref[0, 1])
                + _dot(dgu_ref[1, 0], w_ref[1, 0]) + _dot(dgu_ref[1, 1], w_ref[1, 1]))

        @pl.when(j == 0)
        def _():
            dfn_ref[...] = part

        @pl.when(j > 0)
        def _():
            dfn_ref[...] += part

    return _pcall(
        body, name="ffn_dfn", grid=(s // tm, N_DEV // 2),
        out_shape=_sds((s, d), jnp.float32),
        in_specs=[pl.BlockSpec((2, 2, tm, f), lambda i, j: (j, 0, i, 0)),
                  pl.BlockSpec((2, 2, f, d), lambda i, j: (j, 0, 0, 0))],
        out_specs=pl.BlockSpec((tm, d), lambda i, j: (i, 0)),
        sem=("parallel", "arbitrary"), after=after)(dgu, w_gu)


def _ffn_norm_bwd(d_fn, dout, h1, g_ffn):
    s, d = h1.shape
    tm = min(ROW_TILE, s)

    def body(dfn_ref, do_ref, h1_ref, g_ref, dh1_ref, dg_ref):
        i = pl.program_id(0)

        @pl.when(i == 0)
        def _():
            dg_ref[...] = jnp.zeros_like(dg_ref)

        _, xn, r = _norm_fwd(h1_ref[...], g_ref[...])
        dx, dg = _norm_bwd(xn, r, g_ref[...], dfn_ref[...])
        dh1_ref[...] = do_ref[...] + dx
        dg_ref[...] += dg

    row = pl.BlockSpec((tm, d), lambda i: (i, 0))
    vec = pl.BlockSpec((1, d), lambda i: (0, 0))
    return _pcall(
        body, name="ffn_norm_bwd", grid=(s // tm,),
        out_shape=[_sds((s, d), jnp.float32), _sds((1, d), jnp.float32)],
        in_specs=[row, row, row, vec], out_specs=[row, vec], sem=("arbitrary",))(d_fn, dout, h1, g_ffn)


def _mem_bwd(proj, g_mq, km, vmm, d_y, y_m, lse):
    s = proj.shape[0]
    ml, hw = km.shape
    tm = min(FFN_TILE, s)
    scale = MEM_DIM ** -0.5

    def body(q_ref, g_ref, k_ref, v_ref, do_ref, y_ref, lse_ref, dq_ref, dk_ref, dv_ref, dg_ref):
        i = pl.program_id(0)

        @pl.when(i == 0)
        def _():
            dk_ref[...] = jnp.zeros_like(dk_ref)
            dv_ref[...] = jnp.zeros_like(dv_ref)
            dg_ref[...] = jnp.zeros_like(dg_ref)

        col = lax.broadcasted_iota(jnp.int32, (tm, MEM_HEADS), 1)
        lse_t = lse_ref[...]
        for h in range(MEM_HEADS):
            sl = slice(MEM_DIM * h, MEM_DIM * (h + 1))
            qn, xn, r = _norm_fwd(q_ref[:, sl], g_ref[...])
            lse_h = jnp.sum(jnp.where(col == h, lse_t, 0.0), -1, keepdims=True)
            p = jnp.exp(_dot_nt(qn, k_ref[:, sl]) * scale - lse_h)
            do = do_ref[:, sl]
            dd = jnp.sum(do * y_ref[:, sl], -1, keepdims=True)
            dp = _dot_nt(do, v_ref[:, sl])
            ds = (p * (dp - dd)).astype(MXU)
            dv_ref[:, sl] += _dot_tn(p, do)
            dk_ref[:, sl] += _dot_tn(ds, qn) * scale
            dx, dg = _norm_bwd(xn, r, g_ref[...], _dot(ds, k_ref[:, sl]) * scale)
            dq_ref[:, sl] = dx.astype(dq_ref.dtype)
            dg_ref[...] += dg

    full = pl.BlockSpec((ml, hw), lambda i: (0, 0))
    return _pcall(
        body, name="mem_bwd", grid=(s // tm,),
        out_shape=[_sds((s, hw), MXU), _sds((ml, hw), jnp.float32), _sds((ml, hw), jnp.float32),
                   _sds((1, MEM_DIM), jnp.float32)],
        in_specs=[pl.BlockSpec((tm, hw), lambda i: (i, C_QM // hw)), pl.BlockSpec((1, MEM_DIM), lambda i: (0, 0)),
                  full, full, pl.BlockSpec((tm, hw), lambda i: (i, 3)), pl.BlockSpec((tm, hw), lambda i: (i, 0)),
                  pl.BlockSpec((tm, MEM_HEADS), lambda i: (i, 0))],
        out_specs=[pl.BlockSpec((tm, hw), lambda i: (i, 0)), full, full,
                   pl.BlockSpec((1, MEM_DIM), lambda i: (0, 0))],
        sem=("arbitrary",))(proj, g_mq, km, vmm, d_y, y_m, lse)


def _memkv_bwd(mem, g_mem, w_mkv, g_mk, kv, memn, dk, dv):
    ml, d = mem.shape
    hw = MEM_HEADS * MEM_DIM

    def body(mem_ref, g_ref, w_ref, gk_ref, kv_ref, mn_ref, dk_ref, dv_ref, dw_ref, dgm_ref, dgk_ref):
        parts = []
        dgk = jnp.zeros((1, MEM_DIM), jnp.float32)
        for h in range(MEM_HEADS):
            sl = slice(MEM_DIM * h, MEM_DIM * (h + 1))
            _, xn, r = _norm_fwd(kv_ref[:, sl], gk_ref[...])
            dx, dg = _norm_bwd(xn, r, gk_ref[...], dk_ref[:, sl])
            parts.append(dx)
            dgk = dgk + dg
        dkv = jnp.concatenate(parts + [dv_ref[...]], axis=1).astype(MXU)
        dgk_ref[...] = dgk
        dw_ref[...] = _dot_tn(mn_ref[...], dkv).astype(dw_ref.dtype)
        d_mn = _dot_nt(dkv, w_ref[...])
        _, xn, _ = _norm_fwd(mem_ref[...], g_ref[...])
        dgm_ref[...] = jnp.sum(d_mn * xn, 0, keepdims=True)

    vm = pl.BlockSpec(memory_space=pltpu.VMEM)
    return _pcall(
        body, name="memkv_bwd",
        out_shape=[_sds((d, 2 * hw), WIRE), _sds((1, d), jnp.float32), _sds((1, MEM_DIM), jnp.float32)],
        in_specs=[vm] * 8, out_specs=[vm] * 3)(mem, g_mem, w_mkv, g_mk, kv, memn, dk, dv)


def _mla_bwd(qc, kc, v, d_y, y_b, lse, after):
    nh, s, _ = qc.shape
    t = min(ATT_TILE, s)
    nb = s // t
    scale = (MLA_NOPE + MLA_ROPE) ** -0.5

    def body(q_ref, k_ref, v_ref, do_ref, y_ref, lse_ref, dq_ref, dk_ref, dv_ref, dk_acc, dv_acc):
        kj, qi = pl.program_id(1), pl.program_id(2)

        @pl.when((kj == 0) & (qi == 0))
        def _():
            dq_ref[...] = jnp.zeros_like(dq_ref)

        @pl.when(qi == kj)
        def _():
            dk_acc[...] = jnp.zeros_like(dk_acc)
            dv_acc[...] = jnp.zeros_like(dv_acc)

        @pl.when(qi >= kj)
        def _():
            q, k = q_ref[0], k_ref[0]
            sc = _dot_nt(q, k) * scale
            r_i = lax.broadcasted_iota(jnp.int32, sc.shape, 0) + qi * t
            c_i = lax.broadcasted_iota(jnp.int32, sc.shape, 1) + kj * t
            p = jnp.exp(jnp.where(c_i <= r_i, sc, NEG_INF) - lse_ref[0])
            do = do_ref[...]
            dd = jnp.sum(do * y_ref[...], -1, keepdims=True)
            dp = _dot_nt(do, v_ref[0])
            ds = (p * (dp - dd) * scale).astype(MXU)
            dv_acc[...] += _dot_tn(p, do)
            dk_acc[...] += _dot_tn(ds, q)
            rows = pl.ds(pl.multiple_of(qi * t, t), t)
            dq_ref[0, rows, :] += _dot(ds, k)

        @pl.when(qi == nb - 1)
        def _():
            dk_ref[0] = dk_acc[...]
            dv_ref[0] = dv_acc[...]

    qmap = lambda h, j, i: (h, jnp.maximum(i, j), 0)
    return _pcall(
        body, name="mla_bwd", grid=(nh, nb, nb),
        out_shape=[_sds((nh, s, 256), jnp.float32), _sds((nh, s, 256), jnp.float32),
                   _sds((nh, s, MLA_V), jnp.float32)],
        in_specs=[pl.BlockSpec((1, t, 256), qmap),
                  pl.BlockSpec((1, t, 256), lambda h, j, i: (h, j, 0)),
                  pl.BlockSpec((1, t, MLA_V), lambda h, j, i: (h, j, 0)),
                  pl.BlockSpec((t, MLA_V), lambda h, j, i: (jnp.maximum(i, j), 8 + h)),
                  pl.BlockSpec((t, MLA_V), lambda h, j, i: (jnp.maximum(i, j), h)),
                  pl.BlockSpec((1, t, 1), qmap)],
        out_specs=[pl.BlockSpec((1, s, 256), lambda h, j, i: (h, 0, 0)),
                   pl.BlockSpec((1, t, 256), lambda h, j, i: (h, j, 0)),
                   pl.BlockSpec((1, t, MLA_V), lambda h, j, i: (h, j, 0))],
        scratch=[pltpu.VMEM((t, 256), jnp.float32), pltpu.VMEM((t, MLA_V), jnp.float32)],
        sem=("parallel", "arbitrary", "arbitrary"), after=after)(qc, kc, v, d_y, y_b, lse)


def _mla_prep_bwd(proj, cos, sin, g_cq, g_ckv, w_uq, w_ukv, g_qn, g_qr, g_kn, g_kr,
                  qb, kvb, cqn, ckvn, dqc, dkc, dv):
    s = proj.shape[0]
    tm = min(ROW_TILE, s)
    nh = MLA_HEADS
    ni = s // tm

    def body(cq_ref, ckv_ref, kr_ref, cos_ref, sin_ref, gcq_ref, gckv_ref, wuq_ref, wukv_ref,
             gqn_ref, gqr_ref, gkn_ref, gkr_ref, qb_ref, kvb_ref, cqn_ref, ckvn_ref, dqc_ref, dkc_ref, dv_ref,
             dcq_ref, dckv_ref, dkr_ref, dwuq_ref, dwukv_ref,
             dgcq_ref, dgckv_ref, dgqn_ref, dgqr_ref, dgkn_ref, dgkr_ref, acc_uq, acc_ukv):
        i = pl.program_id(0)

        @pl.when(i == 0)
        def _():
            acc_uq[...] = jnp.zeros_like(acc_uq)
            acc_ukv[...] = jnp.zeros_like(acc_ukv)
            for ref in (dgcq_ref, dgckv_ref, dgqn_ref, dgqr_ref, dgkn_ref, dgkr_ref):
                ref[...] = jnp.zeros_like(ref)

        cos_t, sin_t = cos_ref[...], sin_ref[...]
        lo = _lo_mask((tm, LANES))
        qb_v, kvb_v = qb_ref[...], kvb_ref[...]
        dq_parts, dgqn = [], jnp.zeros((1, LANES), jnp.float32)
        for h in range(nh):
            _, xn, r = _norm_fwd(qb_v[:, MLA_NOPE * h: MLA_NOPE * (h + 1)], gqn_ref[...])
            dx, dg = _norm_bwd(xn, r, gqn_ref[...], dqc_ref[h][:, :MLA_NOPE])
            dq_parts.append(dx)
            dgqn = dgqn + dg
        dgqn_ref[...] += dgqn
        dgqr = jnp.zeros((1, LANES), jnp.float32)
        for j in range(nh // 2):
            d_rope = jnp.where(lo, dqc_ref[2 * j][:, MLA_NOPE:], dqc_ref[2 * j + 1][:, MLA_NOPE:])
            d_pre = _rope_bwd(d_rope, cos_t, sin_t)
            xr = qb_v[:, nh * MLA_NOPE + LANES * j: nh * MLA_NOPE + LANES * (j + 1)]
            _, xn, r = _norm_fwd(xr, gqr_ref[...], half=True)
            dx, dg = _norm_bwd(xn, r, gqr_ref[...], d_pre, half=True)
            dq_parts.append(dx)
            dgqr = dgqr + dg
        dgqr_ref[...] += dgqr
        dqb = jnp.concatenate(dq_parts, axis=1).astype(MXU)
        acc_uq[...] += _dot_tn(dqb, cqn_ref[...])
        _, xn, r = _norm_fwd(cq_ref[...], gcq_ref[...])
        dx, dg = _norm_bwd(xn, r, gcq_ref[...], _dot(dqb, wuq_ref[...]))
        dcq_ref[...] = dx.astype(dcq_ref.dtype)
        dgcq_ref[...] += dg
        dkv_parts, dgkn = [], jnp.zeros((1, LANES), jnp.float32)
        d_kr2 = jnp.zeros((tm, LANES), jnp.float32)
        for h in range(nh):
            _, xn, r = _norm_fwd(kvb_v[:, 256 * h: 256 * h + MLA_NOPE], gkn_ref[...])
            dx, dg = _norm_bwd(xn, r, gkn_ref[...], dkc_ref[h][:, :MLA_NOPE])
            dkv_parts += [dx, dv_ref[h]]
            dgkn = dgkn + dg
            d_kr2 = d_kr2 + dkc_ref[h][:, MLA_NOPE:]
        dgkn_ref[...] += dgkn
        dkvb = jnp.concatenate(dkv_parts, axis=1).astype(MXU)
        d_ckvn = jnp.zeros((tm, 512), jnp.float32)
        for dev in range(N_DEV):
            piece = dkvb[:, LANES * dev: LANES * (dev + 1)]
            acc_ukv[dev] += _dot_tn(ckvn_ref[...], piece)
            d_ckvn = d_ckvn + _dot_nt(piece, wukv_ref[dev])
        _, xn, r = _norm_fwd(ckv_ref[...], gckv_ref[...])
        dx, dg = _norm_bwd(xn, r, gckv_ref[...], d_ckvn)
        dckv_ref[...] = dx.astype(dckv_ref.dtype)
        dgckv_ref[...] += dg
        d_kr = jnp.where(lo, d_kr2 + pltpu.roll(d_kr2, 64, 1), 0.0)
        d_pre = _rope_bwd(d_kr, cos_t, sin_t)
        _, xn, r = _norm_fwd(kr_ref[...], gkr_ref[...], half=True)
        dx, dg = _norm_bwd(xn, r, gkr_ref[...], d_pre, half=True)
        dkr_ref[...] = jnp.where(lo, dx, 0.0).astype(dkr_ref.dtype)
        dgkr_ref[...] += jnp.where(_lo_mask((1, LANES)), dg, 0.0)

        @pl.when(i == ni - 1)
        def _():
            dwuq_ref[...] = acc_uq[...].astype(dwuq_ref.dtype)
            dwukv_ref[...] = acc_ukv[...].astype(dwukv_ref.dtype)

    def col(width, start):
        return pl.BlockSpec((tm, width), lambda i: (i, start // width))

    def full(shape):
        return pl.BlockSpec(shape, lambda i: (0,) * len(shape))

    def row(width):
        return pl.BlockSpec((tm, width), lambda i: (i, 0))

    def heads(width):
        return pl.BlockSpec((nh, tm, width), lambda i: (0, i, 0))

    vec = full((1, LANES))
    return _pcall(
        body, name="mla_prep_bwd", grid=(ni,),
        out_shape=[_sds((s, 512), MXU), _sds((s, 512), MXU), _sds((s, LANES), MXU),
                   _sds((768, 512), WIRE), _sds((N_DEV, 512, LANES), WIRE),
                   _sds((1, 512), jnp.float32), _sds((1, 512), jnp.float32)] + [_sds((1, LANES), jnp.float32)] * 4,
        in_specs=[col(512, C_CQ), col(512, C_CKV), col(LANES, C_KR), row(LANES), row(LANES),
                  full((1, 512)), full((1, 512)), full((768, 512)), full((N_DEV, 512, LANES)), vec, vec, vec, vec,
                  row(768), row(1024), row(512), row(512), heads(256), heads(256), heads(MLA_V)],
        out_specs=[row(512), row(512), row(LANES), full((768, 512)), full((N_DEV, 512, LANES)),
                   full((1, 512)), full((1, 512)), vec, vec, vec, vec],
        scratch=[pltpu.VMEM((768, 512), jnp.float32), pltpu.VMEM((N_DEV, 512, LANES), jnp.float32)],
        sem=("arbitrary",))(proj, proj, proj, cos, sin, g_cq, g_ckv, w_uq, w_ukv, g_qn, g_qr, g_kn, g_kr,
                            qb, kvb, cqn, ckvn, dqc, dkc, dv)


def _swa_bwd(proj, posc, posr, gq, gk, sinks, d_y, y_a, lse, after):
    s = proj.shape[0]
    b = SWA_BLOCK
    nb = s // b
    scale = SWA_DIM ** -0.5

    def body(q_ref, kp_ref, kc_ref, vp_ref, vc_ref, pq_ref, pkp_ref, pkc_ref, gq_ref, gk_ref, sink_ref,
             do_ref, y_ref, lse_ref, kfull_ref,
             dq_ref, dk_ref, dv_ref, dgq_ref, dgk_ref, dsink_ref, dk_acc, dv_acc):
        n = pl.program_id(0)

        @pl.when(n == 0)
        def _():
            dk_acc[...] = jnp.zeros_like(dk_acc)
            dv_acc[...] = jnp.zeros_like(dv_acc)
            dgq_ref[...] = jnp.zeros_like(dgq_ref)
            dsink_ref[...] = jnp.zeros_like(dsink_ref)

        kn, v, dist, valid = _swa_common(n, kp_ref[...], kc_ref[...], vp_ref[...], vc_ref[...],
                                         pq_ref[...], pkp_ref[...], pkc_ref[...], gk_ref[...])
        lo = _lo_mask((b, LANES))
        col = lax.broadcasted_iota(jnp.int32, (b, SWA_Q_HEADS), 1)
        col1 = lax.broadcasted_iota(jnp.int32, (1, SWA_Q_HEADS), 1)
        lse_t = lse_ref[...]
        dk_blk = jnp.zeros((2 * b, LANES), jnp.float32)
        dv_blk = jnp.zeros((2 * b, LANES), jnp.float32)
        dgq = jnp.zeros((1, LANES), jnp.float32)
        dsink = jnp.zeros((1, SWA_Q_HEADS), jnp.float32)
        for j in range(SWA_Q_HEADS // 2):
            hk = (2 * j) // (SWA_Q_HEADS // SWA_KV_HEADS)
            kvmask = lo if hk == 0 else jnp.logical_not(lo)
            sl = slice(LANES * j, LANES * (j + 1))
            qn, xn, r = _norm_fwd(q_ref[:, sl], gq_ref[...], half=True)
            qsw = pltpu.roll(qn, 64, 1)
            d2 = do_ref[:, sl]
            d2sw = pltpu.roll(d2, 64, 1)
            prod = d2 * y_ref[:, sl]
            dqs = []
            for e in range(2):
                h = 2 * j + e
                half_e = lo if e == 0 else jnp.logical_not(lo)
                qm = jnp.where(kvmask, qn if e == hk else qsw, 0.0)
                dm = jnp.where(kvmask, d2 if e == hk else d2sw, 0.0)
                sc = _dot_nt(qm, kn) * scale - _alibi_slope(h) * dist
                sc = jnp.where(valid, sc, NEG_INF)
                lse_h = jnp.sum(jnp.where(col == h, lse_t, 0.0), -1, keepdims=True)
                p = jnp.exp(sc - lse_h)
                dd = jnp.sum(jnp.where(half_e, prod, 0.0), -1, keepdims=True)
                dp = _dot_nt(dm, v)
                ds = (p * (dp - dd)).astype(MXU)
                dsink = dsink - jnp.where(col1 == h, jnp.sum(jnp.exp(sink_ref[h] - lse_h) * dd), 0.0)
                dq_m = _dot(ds, kn) * scale
                dk_blk = dk_blk + _dot_tn(ds, qm) * scale
                dv_blk = dv_blk + _dot_tn(p, dm)
                dqs.append(dq_m if e == hk else pltpu.roll(dq_m, 64, 1))
            dx, dg = _norm_bwd(xn, r, gq_ref[...], jnp.where(lo, dqs[0], dqs[1]), half=True)
            dq_ref[:, sl] = dx.astype(dq_ref.dtype)
            dgq = dgq + dg
        dgq_ref[...] += dgq
        dsink_ref[...] += dsink
        prev = pl.ds(pl.multiple_of(jnp.maximum(n - 1, 0) * b, b), b)
        cur = pl.ds(pl.multiple_of(n * b, b), b)
        dk_acc[prev, :] += dk_blk[:b]
        dv_acc[prev, :] += dv_blk[:b]
        dk_acc[cur, :] += dk_blk[b:]
        dv_acc[cur, :] += dv_blk[b:]

        @pl.when(n == nb - 1)
        def _():
            _, kxn, kr = _norm_fwd(kfull_ref[...], gk_ref[...], half=True)
            dx, dg = _norm_bwd(kxn, kr, gk_ref[...], dk_acc[...], half=True)
            dk_ref[...] = dx.astype(dk_ref.dtype)
            dv_ref[...] = dv_acc[...].astype(dv_ref.dtype)
            dgk_ref[...] = dg

    full = pl.BlockSpec((s, LANES), lambda n: (0, 0))
    vec = pl.BlockSpec((1, LANES), lambda n: (0, 0))
    return _pcall(
        body, name="swa_bwd", grid=(nb,),
        out_shape=[_sds((s, 1024), MXU), _sds((s, LANES), MXU), _sds((s, LANES), MXU),
                   _sds((1, LANES), jnp.float32), _sds((1, LANES), jnp.float32),
                   _sds((1, SWA_Q_HEADS), jnp.float32)],
        in_specs=_swa_specs(s) + [pl.BlockSpec((b, 1024), lambda n: (n, 0)), pl.BlockSpec((b, 1024), lambda n: (n, 0)),
                                  pl.BlockSpec((b, SWA_Q_HEADS), lambda n: (n, 0)),
                                  pl.BlockSpec((s, LANES), lambda n: (0, C_KA // LANES))],
        out_specs=[pl.BlockSpec((b, 1024), lambda n: (n, 0)), full, full, vec, vec,
                   pl.BlockSpec((1, SWA_Q_HEADS), lambda n: (0, 0))],
        scratch=[pltpu.VMEM((s, LANES), jnp.float32), pltpu.VMEM((s, LANES), jnp.float32)],
        sem=("arbitrary",), after=after)(proj, proj, proj, proj, proj, posc, posr, posr, gq, gk, sinks, d_y, y_a, lse,
                                         proj)


def _dx(d_proj, w_in, x, g, d_h1, after):
    s, d = x.shape
    n = w_in.shape[0]
    tm = min(ROW_TILE, s)

    def body(dp_ref, w_ref, x_ref, g_ref, dh_ref, dx_ref, dg_ref):
        i = pl.program_id(0)

        @pl.when(i == 0)
        def _():
            dg_ref[...] = jnp.zeros_like(dg_ref)

        d_hn = _dot(dp_ref[...], w_ref[...])
        _, xn, r = _norm_fwd(x_ref[...], g_ref[...])
        dx, dg = _norm_bwd(xn, r, g_ref[...], d_hn)
        dx_ref[...] = dh_ref[...] + dx
        dg_ref[...] += dg

    row = pl.BlockSpec((tm, d), lambda i: (i, 0))
    vec = pl.BlockSpec((1, d), lambda i: (0, 0))
    return _pcall(
        body, name="grad_x", grid=(s // tm,),
        out_shape=[_sds((s, d), jnp.float32), _sds((1, d), jnp.float32)],
        in_specs=[pl.BlockSpec((tm, n), lambda i: (i, 0)), pl.BlockSpec((n, d), lambda i: (0, 0)), row, vec, row],
        out_specs=[row, vec], sem=("arbitrary",), after=after)(d_proj, w_in, x, g, d_h1)


_SMALL = ["attn_norm_g", "swa_q_norm_g", "swa_k_norm_g", "swa_sinks", "mla_cq_norm_g", "mla_ckv_norm_g",
          "mla_qn_norm_g", "mla_qr_norm_g", "mla_kn_norm_g", "mla_kr_norm_g", "mem_norm_g",
          "mem_q_norm_g", "mem_k_norm_g", "ffn_norm_g"]


def _pack_rows(v):
    n = v.shape[-1]
    rows = -(-n // LANES)
    rows8 = -(-rows // 8) * 8
    flat = jnp.pad(v.reshape(-1), (0, rows8 * LANES - n))
    return flat.reshape(rows8, LANES)


def _pack(parts):
    return jnp.concatenate([_pack_rows(p) for p in parts], axis=0)


def _unpack(packed, sizes):
    out, r = [], 0
    for n in sizes:
        rows = -(-n // LANES)
        rows8 = -(-rows // 8) * 8
        out.append(packed[r:r + rows8].reshape(-1)[:n].reshape(1, n))
        r += rows8
    return out


def _fold64(v):
    return v[:, :64] + v[:, 64:]


def kernel(x, mem, positions, attn_norm_g, w_in, swa_q_norm_g, swa_k_norm_g, swa_sinks, mla_cq_norm_g, mla_ckv_norm_g, w_uq, w_ukv, mla_qn_norm_g, mla_qr_norm_g, mla_kn_norm_g, mla_kr_norm_g, mem_norm_g, w_mem_kv, mem_q_norm_g, mem_k_norm_g, w_out, ffn_norm_g, w_gate, w_up, w_down, loss_target, m_attn_norm_g, m_w_in, m_swa_q_norm_g, m_swa_k_norm_g, m_swa_sinks, m_mla_cq_norm_g, m_mla_ckv_norm_g, m_w_uq, m_w_ukv, m_mla_qn_norm_g, m_mla_qr_norm_g, m_mla_kn_norm_g, m_mla_kr_norm_g, m_mem_norm_g, m_w_mem_kv, m_mem_q_norm_g, m_mem_k_norm_g, m_w_out, m_ffn_norm_g, m_w_gate, m_w_up, m_w_down, v_attn_norm_g, v_w_in, v_swa_q_norm_g, v_swa_k_norm_g, v_swa_sinks, v_mla_cq_norm_g, v_mla_ckv_norm_g, v_w_uq, v_w_ukv, v_mla_qn_norm_g, v_mla_qr_norm_g, v_mla_kn_norm_g, v_mla_kr_norm_g, v_mem_norm_g, v_w_mem_kv, v_mem_q_norm_g, v_mem_k_norm_g, v_w_out, v_ffn_norm_g, v_w_gate, v_w_up, v_w_down):
    args = dict(locals())
    x2, mem2, tgt = x[0], mem[0], loss_target[0]
    s, d = x2.shape
    n_in = w_in.shape[2]
    f = w_gate.shape[2]

    (g_in,) = _all_gather([w_in[0].T.astype(WIRE)])
    mix_shards = [w_uq[0].T.astype(WIRE), w_ukv[0].astype(WIRE), w_mem_kv[0].astype(WIRE),
                  _to_wire([w_out[0]], g_in, "wire_out")[0]]
    g_uq, wkv, g_mkv, g_out = _all_gather_background(mix_shards, 5, "all_gather_mix_weights")
    ffn_shards = [_to_wire([w_gate[0].T, w_up[0].T], g_in, "wire_gate_up"),
                  _to_wire([w_down[0]], g_in, "wire_down")[0]]
    w_gu, w_d = _all_gather_background(ffn_shards, 1, "all_gather_ffn_weights")
    wi = g_in.reshape(N_DEV * n_in, d)
    wi = jnp.concatenate([wi[0:1024], wi[1280:1792], wi[1792:2304], wi[2368:2880],
                          wi[1024:1152], wi[1152:1280], wi[2304:2368],
                          jnp.zeros((IN_PAD - 2880, d), wi.dtype)], axis=0)
    wq = g_uq.reshape(768, 512)
    wq = jnp.concatenate([wq[192 * h: 192 * h + 128] for h in range(4)]
                         + [wq[192 * h + 128: 192 * (h + 1)] for h in range(4)], axis=0)
    wmkv = g_mkv.reshape(-1, g_mkv.shape[-1])
    wo = g_out.reshape(-1, d)

    pos = positions[0].astype(jnp.float32)
    inv_freq = ROPE_THETA ** (-jnp.arange(0, MLA_ROPE, 2, dtype=jnp.float32) / MLA_ROPE)
    ang = pos[:, None] * inv_freq
    cos32, sin32 = jnp.cos(ang), jnp.sin(ang)
    cos_t = jnp.tile(cos32, (1, 4))
    sin_t = jnp.tile(jnp.concatenate([-sin32, sin32], axis=1), (1, 2))
    posc, posr = pos.reshape(s, 1), pos.reshape(1, s)
    two = lambda g: jnp.tile(g, (1, 2))
    gq2, gk2, gqr2, gkr2 = two(swa_q_norm_g), two(swa_k_norm_g), two(mla_qr_norm_g), two(mla_kr_norm_g)
    sinks1 = swa_sinks[0]

    proj, hn = _in_proj(x2, attn_norm_g, wi)
    qc, kc, vb, qb, kvb, cqn, ckvn = _mla_prep(proj, cos_t, sin_t, mla_cq_norm_g, mla_ckv_norm_g, wq, wkv,
                                                mla_qn_norm_g, gqr2, mla_kn_norm_g, gkr2)
    y_b, lse_b = _mla_fwd(qc, kc, vb)
    km, vmm, kvm, memn = _memkv_prep(mem2, mem_norm_g, wmkv, mem_k_norm_g)
    y_m, lse_m = _mem_fwd(proj, mem_q_norm_g, km, vmm)
    y_a, lse_a = _swa_fwd(proj, posc, posr, gq2, gk2, sinks1)
    h1, fn = _out_proj(y_a, y_b, y_m, x2, wo, ffn_norm_g)
    gu, act = _ffn_gu(fn, w_gu)
    dout, loss_tile = _ffn_down(act, w_d, h1, tgt)

    dgu, dw_d = _ffn_bwd_act(dout, w_d, gu)
    dw_gu = _ffn_dw_gu(fn, dgu)
    r_gu, r_d = _exchange_grads_background([dw_gu, dw_d], 2, "exchange_ffn_grads")
    d_h1, dg_ffn = _ffn_norm_bwd(_ffn_dfn(dgu, w_gu, dw_gu), dout, h1, ffn_norm_g)
    d_y = _mm(d_h1, wo, tb=True, out_dtype=jnp.float32, tm=FFN_TILE, tk=2048, name="d_mix")
    dw_out = jnp.concatenate([
        _mm(y_a, d_h1, ta=True, out_dtype=WIRE, tm=1024, tk=1024, name="dw_out_a"),
        _mm(y_b, d_h1, ta=True, out_dtype=WIRE, tm=1024, tk=1024, name="dw_out_b"),
        _mm(y_m, d_h1, ta=True, out_dtype=WIRE, tm=1024, tk=1024, name="dw_out_m")], axis=0)
    d_qm, dkm, dvmm, dg_mq = _mem_bwd(proj, mem_q_norm_g, km, vmm, d_y, y_m, lse_m)
    dw_mkv, dg_mem, dg_mk = _memkv_bwd(mem2, mem_norm_g, wmkv, mem_k_norm_g, kvm, memn, dkm, dvmm)
    r_mkv, r_out = _exchange_grads_background([dw_mkv.reshape(g_mkv.shape), dw_out.reshape(g_out.shape)], 3,
                                              "exchange_mix_grads")
    dqc, dkc, dvb = _mla_bwd(qc, kc, vb, d_y, y_b, lse_b, dw_mkv)
    (d_cq, d_ckv, d_kr, dw_uq, dw_ukv, dg_cq, dg_ckv, dg_qn, dg_qr, dg_kn, dg_kr) = _mla_prep_bwd(
        proj, cos_t, sin_t, mla_cq_norm_g, mla_ckv_norm_g, wq, wkv, mla_qn_norm_g, gqr2, mla_kn_norm_g, gkr2,
        qb, kvb, cqn, ckvn, dqc, dkc, dvb)
    d_qa, d_ka, d_va, dg_q, dg_k, d_sinks = _swa_bwd(proj, posc, posr, gq2, gk2, sinks1, d_y, y_a, lse_a, dw_out)
    d_proj = jnp.concatenate([d_qa, d_cq, d_ckv, d_qm, d_ka, d_va, d_kr], axis=1)
    gi = _dw_in(hn, d_proj, n_in)

    gq_ = jnp.concatenate(sum([[dw_uq[128 * h: 128 * (h + 1)], dw_uq[512 + 64 * h: 512 + 64 * (h + 1)]]
                               for h in range(4)], []), axis=0)
    gq_ = gq_.reshape(N_DEV, 96, 512)
    r_in, r_uq, r_ukv = _exchange_grads_background([gi, gq_, dw_ukv], 4, "exchange_in_grads")
    grad_x, dg_attn = _dx(d_proj, wi, x2, attn_norm_g, d_h1, gi)

    big = {}
    def adam(name, r, transposed=False, after=None, which=None):
        w, m, v = args[name][0], args["m_" + name][0], args["v_" + name][0]
        if transposed:
            outs = _adam_big(r, w.T, m.T, v.T, "adam_" + name, after, which)
            return [o.T[None] for o in outs]
        return [o[None] for o in _adam_big(r, w, m, v, "adam_" + name, after)]
    big["w_gate"] = adam("w_gate", r_gu, True, which=0)
    big["w_up"] = adam("w_up", r_gu, True, after=big["w_gate"][0], which=1)
    big["w_down"] = adam("w_down", r_d, after=big["w_up"][0])
    big["w_out"] = adam("w_out", r_out, after=big["w_down"][0])
    big["w_mem_kv"] = adam("w_mem_kv", r_mkv, after=big["w_out"][0])
    big["w_in"] = adam("w_in", r_in, True, after=big["w_mem_kv"][0])
    big["w_uq"] = adam("w_uq", r_uq, True, after=big["w_in"][0])
    big["w_ukv"] = adam("w_ukv", r_ukv, after=big["w_uq"][0])

    small_g = {
        "attn_norm_g": dg_attn, "swa_q_norm_g": _fold64(dg_q), "swa_k_norm_g": _fold64(dg_k),
        "swa_sinks": d_sinks, "mla_cq_norm_g": dg_cq, "mla_ckv_norm_g": dg_ckv, "mla_qn_norm_g": dg_qn,
        "mla_qr_norm_g": _fold64(dg_qr), "mla_kn_norm_g": dg_kn, "mla_kr_norm_g": _fold64(dg_kr),
        "mem_norm_g": dg_mem, "mem_q_norm_g": dg_mq, "mem_k_norm_g": dg_mk, "ffn_norm_g": dg_ffn}
    sizes = [args[n].shape[-1] for n in _SMALL]
    pg = _pack([small_g[n] for n in _SMALL] + [loss_tile[0:1, 0:1]])
    zero = jnp.zeros((1, 1), jnp.float32)
    pw = _pack([args[n] for n in _SMALL] + [zero])
    pm = _pack([args["m_" + n] for n in _SMALL] + [zero])
    pv = _pack([args["v_" + n] for n in _SMALL] + [zero])
    sg, sd, sm, sv = _small_allreduce_adam(pg, pw, pm, pv)
    small = {n: vals for n, vals in zip(_SMALL, zip(*[_unpack(p, sizes) for p in (sg, sd, sm, sv)]))}
    loss = _unpack(sg, sizes + [1])[-1].reshape(())

    order = ["attn_norm_g", "w_in", "swa_q_norm_g", "swa_k_norm_g", "swa_sinks", "mla_cq_norm_g", "mla_ckv_norm_g",
             "w_uq", "w_ukv", "mla_qn_norm_g", "mla_qr_norm_g", "mla_kn_norm_g", "mla_kr_norm_g", "mem_norm_g",
             "w_mem_kv", "mem_q_norm_g", "mem_k_norm_g", "w_out", "ffn_norm_g", "w_gate", "w_up", "w_down"]
    res = {n: (big[n] if n in big else list(small[n])) for n in order}
    outs = [loss, grad_x[None]]
    for kind in range(4):
        outs += [res[n][kind] for n in order]
    return tuple(outs)
```

```python
import jax
import jax.numpy as jnp
from jax import lax
from jax.experimental import pallas as pl
from jax.experimental.pallas import tpu as pltpu
from jax.experimental.pallas import tpu_sc as plsc

MXU = jnp.bfloat16
WIRE = jnp.bfloat16
EPS = 1e-6
NEG_INF = -1e30
N_DEV = 8
LANES = 128
ROW_TILE = 256
FFN_TILE = 512
ATT_TILE = 1024
SWA_BLOCK = 128
VMEM_LIMIT = 56 * 1024 * 1024

SWA_Q_HEADS, SWA_KV_HEADS, SWA_DIM = 16, 2, 64
MLA_HEADS, MLA_NOPE, MLA_ROPE, MLA_V = 4, 128, 64, 128
MEM_HEADS, MEM_DIM = 4, 128
ROPE_THETA = 10000.0
ADAM_LR, ADAM_B1, ADAM_B2, ADAM_EPS, ADAM_WD, ADAM_STEP = 0.001, 0.9, 0.999, 1e-08, 0.01, 10

C_QA, C_CQ, C_CKV, C_QM, C_KA, C_VA, C_KR, IN_PAD = 0, 1024, 1536, 2048, 2560, 2688, 2816, 2944


def _pcall(body, *, name, out_shape, in_specs, out_specs, grid=(), scratch=(), sem=None, after=None):
    params = pltpu.CompilerParams(dimension_semantics=sem, vmem_limit_bytes=VMEM_LIMIT)
    if after is not None:
        n_in, inner = len(in_specs), body

        def body(*refs):
            inner(*refs[:n_in], *refs[n_in + 1:])

        in_specs = list(in_specs) + [pl.BlockSpec(memory_space=pl.ANY)]
    call = pl.pallas_call(body, name=name, grid=grid, in_specs=in_specs, out_specs=out_specs,
                          out_shape=out_shape, scratch_shapes=list(scratch), compiler_params=params)
    return call if after is None else (lambda *ops: call(*ops, after))


def _sds(shape, dtype):
    return jax.ShapeDtypeStruct(tuple(shape), dtype)


def _dot(a, b):
    return jnp.dot(a.astype(MXU), b.astype(MXU), preferred_element_type=jnp.float32)


def _dot_nt(a, b):
    return lax.dot_general(a.astype(MXU), b.astype(MXU), (((1,), (1,)), ((), ())),
                           preferred_element_type=jnp.float32)


def _dot_tn(a, b):
    return lax.dot_general(a.astype(MXU), b.astype(MXU), (((0,), (0,)), ((), ())),
                           preferred_element_type=jnp.float32)


def _lo_mask(shape):
    return (lax.broadcasted_iota(jnp.int32, shape, len(shape) - 1) % LANES) < 64


def _norm_fwd(x, g, half=False):
    x2 = x * x
    if half:
        lo = _lo_mask(x.shape)
        s_lo = jnp.sum(jnp.where(lo, x2, 0.0), -1, keepdims=True)
        s_hi = jnp.sum(jnp.where(lo, 0.0, x2), -1, keepdims=True)
        r = jnp.where(lo, lax.rsqrt(s_lo / 64.0 + EPS), lax.rsqrt(s_hi / 64.0 + EPS))
    else:
        r = lax.rsqrt(jnp.mean(x2, -1, keepdims=True) + EPS)
    xn = x * r
    return xn * g, xn, r


def _norm_bwd(xn, r, g, dy, half=False):
    t = dy * g
    tx = t * xn
    if half:
        lo = _lo_mask(xn.shape)
        m_lo = jnp.sum(jnp.where(lo, tx, 0.0), -1, keepdims=True) / 64.0
        m_hi = jnp.sum(jnp.where(lo, 0.0, tx), -1, keepdims=True) / 64.0
        m = jnp.where(lo, m_lo, m_hi)
    else:
        m = jnp.mean(tx, -1, keepdims=True)
    dx = r * (t - xn * m)
    dg = jnp.sum(dy * xn, 0, keepdims=True)
    return dx, dg


def _swap32(x):
    lane = lax.broadcasted_iota(jnp.int32, x.shape, 1)
    return jnp.where((lane % 64) < 32, pltpu.roll(x, 96, 1), pltpu.roll(x, 32, 1))


def _rope(x, cos, sin):
    return x * cos + _swap32(x) * sin


def _rope_bwd(d, cos, sin):
    return d * cos + _swap32(d * sin)


def _my_coords():
    return lax.axis_index("x"), lax.axis_index("y"), lax.axis_index("c")


def _dev_index(px, py, pc):
    return 4 * px + 2 * py + pc


_FLIPS = [(0, 0, 1), (0, 1, 0), (0, 1, 1), (1, 0, 0), (1, 0, 1), (1, 1, 0), (1, 1, 1)]


def _flip(coords, f):
    return tuple((1 - v) if b else v for v, b in zip(coords, f))


def _all_gather(shards):
    n = len(shards)

    def body(*refs):
        ins, outs = refs[:n], refs[n:2 * n]
        send_sems, recv_sems, local_sems = refs[2 * n:]
        x, y, c = _my_coords()
        me, sibling = (x, y, c), (x, y, 1 - c)
        chips = [(1 - x, y), (x, 1 - y), (1 - x, 1 - y)]

        def copy(w, k, block, to, src=None):
            dst = outs[w].at[_dev_index(*block)]
            return pltpu.make_async_remote_copy(
                src_ref=dst if src is None else src, dst_ref=dst,
                send_sem=send_sems.at[w, k], recv_sem=recv_sems.at[w, k],
                device_id=to, device_id_type=pl.DeviceIdType.MESH)

        sends, locals_ = [], []
        for w in range(n):
            mine = pltpu.make_async_copy(ins[w], outs[w].at[_dev_index(*me)], local_sems.at[w])
            mine.start()
            locals_.append(mine)
            first = [copy(w, 0, me, sibling, src=ins[w])]
            first += [copy(w, 1 + j, me, (*chip, c), src=ins[w]) for j, chip in enumerate(chips)]
            for cp in first:
                cp.start()
            sends += first
        for w in range(n):
            for j, chip in enumerate(chips):
                copy(w, 1 + j, (*chip, c), me).wait_recv()
                fwd = copy(w, 4 + j, (*chip, c), sibling)
                fwd.start()
                sends.append(fwd)
        for w in range(n):
            copy(w, 0, sibling, me).wait_recv()
            for j, chip in enumerate(chips):
                copy(w, 4 + j, (*chip, 1 - c), me).wait_recv()
        for cp in sends:
            cp.wait_send()
        for mine in locals_:
            mine.wait()

    any_spec = pl.BlockSpec(memory_space=pl.ANY)
    return _pcall(
        body, name="all_gather_weights",
        out_shape=[_sds((N_DEV,) + s.shape, s.dtype) for s in shards],
        in_specs=[any_spec] * n, out_specs=[any_spec] * n,
        scratch=[pltpu.SemaphoreType.DMA((n, 7)), pltpu.SemaphoreType.DMA((n, 7)),
                 pltpu.SemaphoreType.DMA((n,))])(*shards)


def _wire_cost(arrays):
    nbytes = sum(a.size * a.dtype.itemsize for a in arrays)
    return pl.CostEstimate(flops=0, transcendentals=0, bytes_accessed=40 * nbytes)


def _all_gather_background(shards, collective_id, name):
    n = len(shards)
    src_refs = [jax.new_ref(s, memory_space=pltpu.MemorySpace.HBM) for s in shards]
    out_refs = [jax.empty_ref(_sds((N_DEV,) + s.shape, s.dtype), memory_space=pltpu.MemorySpace.HBM) for s in shards]

    @pl.kernel(mesh=plsc.ScalarSubcoreMesh(axis_name="seq", num_cores=1), name=name,
               scratch_types=(pltpu.SemaphoreType.DMA((n, 7)), pltpu.SemaphoreType.DMA((n, 7)),
                              pltpu.SemaphoreType.DMA((n,))),
               compiler_params=pltpu.CompilerParams(collective_id=collective_id))
    def launch(send_sems, recv_sems, local_sems):
        x, y, c = _my_coords()
        me, sibling = (x, y, c), (x, y, 1 - c)
        chips = [(1 - x, y), (x, 1 - y), (1 - x, 1 - y)]
        barrier = pltpu.get_barrier_semaphore()
        for peer in [sibling] + [(*chip, c) for chip in chips]:
            pl.semaphore_signal(barrier, inc=1, device_id=peer, device_id_type=pl.DeviceIdType.MESH)
        pl.semaphore_wait(barrier, 4)

        def copy(w, k, block, to, src=None):
            dst = out_refs[w].at[_dev_index(*block)]
            return pltpu.make_async_remote_copy(
                src_ref=dst if src is None else src, dst_ref=dst,
                send_sem=send_sems.at[w, k], recv_sem=recv_sems.at[w, k],
                device_id=to, device_id_type=pl.DeviceIdType.MESH)

        sends, locals_ = [], []
        for w in range(n):
            mine = pltpu.make_async_copy(src_refs[w], out_refs[w].at[_dev_index(*me)], local_sems.at[w])
            mine.start()
            locals_.append(mine)
            first = [copy(w, 0, me, sibling, src=src_refs[w])]
            first += [copy(w, 1 + j, me, (*chip, c), src=src_refs[w]) for j, chip in enumerate(chips)]
            for cp in first:
                cp.start()
            sends += first
        for w in range(n):
            for j, chip in enumerate(chips):
                copy(w, 1 + j, (*chip, c), me).wait_recv()
                fwd = copy(w, 4 + j, (*chip, c), sibling)
                fwd.start()
                sends.append(fwd)
        for w in range(n):
            copy(w, 0, sibling, me).wait_recv()
            for j, chip in enumerate(chips):
                copy(w, 4 + j, (*chip, 1 - c), me).wait_recv()
        for cp in sends:
            cp.wait_send()
        for mine in locals_:
            mine.wait()

    launch()
    return [r[...] for r in out_refs]


def _exchange_grads(grads):
    n = len(grads)

    def body(*refs):
        ins, outs = refs[:n], refs[n:2 * n]
        send_sems, recv_sems, local_sems = refs[2 * n:]
        me = _my_coords()
        my_idx = _dev_index(*me)
        sends, locals_ = [], []
        for w in range(n):
            mine = pltpu.make_async_copy(ins[w].at[my_idx], outs[w].at[my_idx], local_sems.at[w])
            mine.start()
            locals_.append(mine)
            for k, f in enumerate(_FLIPS):
                peer = _flip(me, f)
                cp = pltpu.make_async_remote_copy(
                    src_ref=ins[w].at[_dev_index(*peer)], dst_ref=outs[w].at[my_idx],
                    send_sem=send_sems.at[w, k], recv_sem=recv_sems.at[w, k],
                    device_id=peer, device_id_type=pl.DeviceIdType.MESH)
                cp.start()
                sends.append(cp)
        for w in range(n):
            for k, f in enumerate(_FLIPS):
                peer = _flip(me, f)
                slot = outs[w].at[_dev_index(*peer)]
                pltpu.make_async_remote_copy(
                    src_ref=slot, dst_ref=slot,
                    send_sem=send_sems.at[w, k], recv_sem=recv_sems.at[w, k],
                    device_id=peer, device_id_type=pl.DeviceIdType.MESH).wait_recv()
        for cp in sends:
            cp.wait_send()
        for mine in locals_:
            mine.wait()

    any_spec = pl.BlockSpec(memory_space=pl.ANY)
    return _pcall(
        body, name="exchange_grads",
        out_shape=[_sds(g.shape, g.dtype) for g in grads],
        in_specs=[any_spec] * n, out_specs=[any_spec] * n,
        scratch=[pltpu.SemaphoreType.DMA((n, 7)), pltpu.SemaphoreType.DMA((n, 7)),
                 pltpu.SemaphoreType.DMA((n,))])(*grads)


def _exchange_grads_background(grads, collective_id, name):
    n = len(grads)
    src_refs = [jax.new_ref(g, memory_space=pltpu.MemorySpace.HBM) for g in grads]
    out_refs = [jax.empty_ref(_sds(g.shape, g.dtype), memory_space=pltpu.MemorySpace.HBM) for g in grads]

    @pl.kernel(mesh=plsc.ScalarSubcoreMesh(axis_name="seq", num_cores=1), name=name,
               scratch_types=(pltpu.SemaphoreType.DMA((n, 7)), pltpu.SemaphoreType.DMA((n, 7)),
                              pltpu.SemaphoreType.DMA((n,))),
               cost_estimate=_wire_cost(grads),
               compiler_params=pltpu.CompilerParams(collective_id=collective_id))
    def launch(send_sems, recv_sems, local_sems):
        me = _my_coords()
        my_idx = _dev_index(*me)
        peers = [_flip(me, f) for f in _FLIPS]
        barrier = pltpu.get_barrier_semaphore()
        for peer in peers:
            pl.semaphore_signal(barrier, inc=1, device_id=peer, device_id_type=pl.DeviceIdType.MESH)
        pl.semaphore_wait(barrier, len(peers))
        sends, locals_ = [], []
        for w in range(n):
            mine = pltpu.make_async_copy(src_refs[w].at[my_idx], out_refs[w].at[my_idx], local_sems.at[w])
            mine.start()
            locals_.append(mine)
            for k, peer in enumerate(peers):
                cp = pltpu.make_async_remote_copy(
                    src_ref=src_refs[w].at[_dev_index(*peer)], dst_ref=out_refs[w].at[my_idx],
                    send_sem=send_sems.at[w, k], recv_sem=recv_sems.at[w, k],
                    device_id=peer, device_id_type=pl.DeviceIdType.MESH)
                cp.start()
                sends.append(cp)
        for w in range(n):
            for k, peer in enumerate(peers):
                slot = out_refs[w].at[_dev_index(*peer)]
                pltpu.make_async_remote_copy(
                    src_ref=slot, dst_ref=slot, send_sem=send_sems.at[w, k], recv_sem=recv_sems.at[w, k],
                    device_id=peer, device_id_type=pl.DeviceIdType.MESH).wait_recv()
        for cp in sends:
            cp.wait_send()
        for mine in locals_:
            mine.wait()

    launch()
    return [r[...] for r in out_refs]


def _to_wire(parts, after, name):
    n = len(parts)
    rows, cols = parts[0].shape
    tr = rows // 2 if rows % 32 == 0 else rows

    def body(*refs):
        for k in range(n):
            refs[n][k] = refs[k][...].astype(WIRE)

    blk = pl.BlockSpec((tr, cols), lambda i: (i, 0))
    return _pcall(
        body, name=name, grid=(rows // tr,), out_shape=_sds((n, rows, cols), WIRE),
        in_specs=[blk] * n, out_specs=pl.BlockSpec((n, tr, cols), lambda i: (0, i, 0)),
        sem=("parallel",), after=after)(*parts)


def _adam_math(w, g, m, v):
    m = ADAM_B1 * m + (1.0 - ADAM_B1) * g
    v = ADAM_B2 * v + (1.0 - ADAM_B2) * (g * g)
    m_hat = m / (1.0 - ADAM_B1 ** ADAM_STEP)
    v_hat = v / (1.0 - ADAM_B2 ** ADAM_STEP)
    delta = -ADAM_LR * (m_hat / (jnp.sqrt(v_hat) + ADAM_EPS) + ADAM_WD * w)
    return delta, m, v


def _small_allreduce_adam(grads, loss_tile, ws, ms, vs):
    sizes = [w.shape[-1] for w in ws]
    n_par = len(ws)
    row0, r = [], 0
    for n in sizes:
        row0.append(r)
        r += -(-n // LANES)
    loss_row = r
    rows = -(-(r + 1) // 8) * 8

    def pieces(n):
        return [(k, min(LANES, n - LANES * k)) for k in range(-(-n // LANES))]

    def body(*refs):
        g_refs = refs[:n_par]
        loss_in = refs[n_par]
        w_refs = refs[n_par + 1: 2 * n_par + 1]
        m_refs = refs[2 * n_par + 1: 3 * n_par + 1]
        v_refs = refs[3 * n_par + 1: 4 * n_par + 1]
        loss_out = refs[4 * n_par + 1]
        out_refs = refs[4 * n_par + 2: 8 * n_par + 2]
        pack, gath, res, send_sems, recv_sems = refs[8 * n_par + 2:]
        me = _my_coords()
        my_idx = _dev_index(*me)

        def fill(slot, srcs):
            pack[slot] = jnp.zeros((rows, LANES), jnp.float32)
            for p, n in enumerate(sizes):
                val = srcs[p][...]
                if val.shape[-1] == LANES and n == 64:
                    pack[slot, row0[p]:row0[p] + 1, :] = val + pltpu.roll(val, 64, 1)
                    continue
                for k, width in pieces(n):
                    pack[slot, row0[p] + k:row0[p] + k + 1, 0:width] = srcs[p][:, LANES * k:LANES * k + width]

        fill(0, g_refs)
        pack[0, loss_row:loss_row + 1, :] = loss_in[0:1, :]
        gath[my_idx] = pack[0]
        sends = []
        for k, f in enumerate(_FLIPS):
            peer = _flip(me, f)
            cp = pltpu.make_async_remote_copy(
                src_ref=pack.at[0], dst_ref=gath.at[my_idx],
                send_sem=send_sems.at[k], recv_sem=recv_sems.at[k],
                device_id=peer, device_id_type=pl.DeviceIdType.MESH)
            cp.start()
            sends.append(cp)
        fill(1, w_refs)
        fill(2, m_refs)
        fill(3, v_refs)
        for k, f in enumerate(_FLIPS):
            peer = _flip(me, f)
            slot = gath.at[_dev_index(*peer)]
            pltpu.make_async_remote_copy(
                src_ref=slot, dst_ref=slot, send_sem=send_sems.at[k], recv_sem=recv_sems.at[k],
                device_id=peer, device_id_type=pl.DeviceIdType.MESH).wait_recv()
        for cp in sends:
            cp.wait_send()
        g = gath[0]
        for d in range(1, N_DEV):
            g = g + gath[d]
        delta, m, v = _adam_math(pack[1], g, pack[2], pack[3])
        res[0], res[1], res[2], res[3] = g, delta, m, v
        loss_out[...] = res[0, loss_row:loss_row + 1, 0:1]
        for p, n in enumerate(sizes):
            for kind in range(4):
                for k, width in pieces(n):
                    out_refs[4 * p + kind][:, LANES * k:LANES * k + width] = (
                        res[kind, row0[p] + k:row0[p] + k + 1, 0:width])

    vm = pl.BlockSpec(memory_space=pltpu.VMEM)
    out_shape = [_sds((1, 1), jnp.float32)]
    for n in sizes:
        out_shape += [_sds((1, n), jnp.float32)] * 4
    outs = _pcall(
        body, name="small_allreduce_adam", out_shape=out_shape,
        in_specs=[vm] * (4 * n_par + 1), out_specs=[vm] * len(out_shape),
        scratch=[pltpu.VMEM((4, rows, LANES), jnp.float32), pltpu.VMEM((N_DEV, rows, LANES), jnp.float32),
                 pltpu.VMEM((4, rows, LANES), jnp.float32),
                 pltpu.SemaphoreType.DMA((7,)), pltpu.SemaphoreType.DMA((7,))])(*grads, loss_tile, *ws, *ms, *vs)
    return outs[0], [outs[1 + 4 * p: 5 + 4 * p] for p in range(n_par)]


def _adam_big(recv, w, m, v, name, after=None, which=None):
    rows, cols = recv.shape[-2:]
    tc = 512 if cols % 512 == 0 else cols
    tr = rows
    while tr * tc > 256 * 1024 and tr % 2 == 0 and (tr // 2) % 16 == 0:
        tr //= 2

    def body(r_ref, w_ref, m_ref, v_ref, g_ref, d_ref, mo_ref, vo_ref):
        g = r_ref[0].astype(jnp.float32)
        for d in range(1, N_DEV):
            g = g + r_ref[d].astype(jnp.float32)
        delta, mn, vn = _adam_math(w_ref[...], g, m_ref[...], v_ref[...])
        g_ref[...] = g
        d_ref[...] = delta
        mo_ref[...] = mn
        vo_ref[...] = vn

    blk = pl.BlockSpec((tr, tc), lambda i, j: (i, j))
    if which is None:
        r_spec = pl.BlockSpec((N_DEV, tr, tc), lambda i, j: (0, i, j))
    else:
        r_spec = pl.BlockSpec((N_DEV, None, tr, tc), lambda i, j: (0, which, i, j))
    return _pcall(
        body, name=name, grid=(rows // tr, cols // tc),
        out_shape=[_sds((rows, cols), jnp.float32)] * 4,
        in_specs=[r_spec, blk, blk, blk],
        out_specs=[blk] * 4, sem=("parallel", "parallel"), after=after)(recv, w, m, v)


def _mm(a, b, *, ta=False, tb=False, out_dtype, tm, tk, name):
    (kdim, mdim) = a.shape if ta else a.shape[::-1]
    ndim = b.shape[0] if tb else b.shape[1]
    tm, tk = min(tm, mdim), min(tk, kdim)
    nk = kdim // tk

    def body(a_ref, b_ref, o_ref, acc):
        k = pl.program_id(1)
        if ta:
            part = _dot_tn(a_ref[...], b_ref[...])
        elif tb:
            part = _dot_nt(a_ref[...], b_ref[...])
        else:
            part = _dot(a_ref[...], b_ref[...])

        @pl.when(k == 0)
        def _():
            acc[...] = part

        @pl.when(k > 0)
        def _():
            acc[...] += part

        @pl.when(k == nk - 1)
        def _():
            o_ref[...] = acc[...].astype(o_ref.dtype)

    a_spec = pl.BlockSpec((tk, tm), lambda i, k: (k, i)) if ta else pl.BlockSpec((tm, tk), lambda i, k: (i, k))
    b_spec = pl.BlockSpec((ndim, tk), lambda i, k: (0, k)) if tb else pl.BlockSpec((tk, ndim), lambda i, k: (k, 0))
    return _pcall(
        body, name=name, grid=(mdim // tm, nk), out_shape=_sds((mdim, ndim), out_dtype),
        in_specs=[a_spec, b_spec], out_specs=pl.BlockSpec((tm, ndim), lambda i, k: (i, 0)),
        scratch=[pltpu.VMEM((tm, ndim), jnp.float32)], sem=("parallel", "arbitrary"))(a, b)


def _ref_col_pieces(start, stop):
    ref_starts = [0, 1024, 1152, 1280, 1792, 2304, 2368, 2880]
    perm_starts = [C_QA, C_KA, C_VA, C_CQ, C_CKV, C_KR, C_QM]
    out = []
    for p in range(7):
        lo, hi = max(start, ref_starts[p]), min(stop, ref_starts[p + 1])
        if lo < hi:
            out.append((lo - start, perm_starts[p] + lo - ref_starts[p], hi - lo))
    return out


def _dw_in(hn, d_proj, n_shard):
    s, d = hn.shape
    n = d_proj.shape[1]
    tm, tk = min(512, d), min(1024, s)
    nk = s // tk

    def body(a_ref, b_ref, o_ref, acc):
        k = pl.program_id(1)
        part = _dot_tn(a_ref[...], b_ref[...])

        @pl.when(k == 0)
        def _():
            acc[...] = part

        @pl.when(k > 0)
        def _():
            acc[...] += part

        @pl.when(k == nk - 1)
        def _():
            t = acc[...].T
            for j in range(N_DEV):
                rows = [t[src:src + width] for _, src, width in _ref_col_pieces(j * n_shard, (j + 1) * n_shard)]
                o_ref[j] = jnp.concatenate(rows, axis=0).astype(o_ref.dtype)

    return _pcall(
        body, name="dw_in", grid=(d // tm, nk), out_shape=_sds((N_DEV, n_shard, d), WIRE),
        in_specs=[pl.BlockSpec((tk, tm), lambda i, k: (k, i)), pl.BlockSpec((tk, n), lambda i, k: (k, 0))],
        out_specs=pl.BlockSpec((N_DEV, n_shard, tm), lambda i, k: (0, 0, i)),
        scratch=[pltpu.VMEM((tm, n), jnp.float32)], sem=("parallel", "arbitrary"))(hn, d_proj)


def _in_proj(x, g, w):
    s, d = x.shape
    n = w.shape[0]
    tm = min(ROW_TILE, s)

    def body(x_ref, g_ref, w_ref, p_ref, hn_ref):
        hn, _, _ = _norm_fwd(x_ref[...], g_ref[...])
        hn_ref[...] = hn.astype(hn_ref.dtype)
        p_ref[...] = _dot_nt(hn, w_ref[...])

    return _pcall(
        body, name="in_proj", grid=(s // tm,),
        out_shape=[_sds((s, n), jnp.float32), _sds((s, d), MXU)],
        in_specs=[pl.BlockSpec((tm, d), lambda i: (i, 0)), pl.BlockSpec((1, d), lambda i: (0, 0)),
                  pl.BlockSpec((n, d), lambda i: (0, 0))],
        out_specs=[pl.BlockSpec((tm, n), lambda i: (i, 0)), pl.BlockSpec((tm, d), lambda i: (i, 0))],
        sem=("parallel",))(x, g, w)


def _mla_prep(proj, cos, sin, g_cq, g_ckv, w_uq, w_ukv, g_qn, g_qr, g_kn, g_kr):
    s = proj.shape[0]
    tm = min(ROW_TILE, s)
    nh = MLA_HEADS

    def body(cq_ref, ckv_ref, kr_ref, cos_ref, sin_ref, gcq_ref, gckv_ref, wuq_ref, wukv_ref,
             gqn_ref, gqr_ref, gkn_ref, gkr_ref,
             qc_ref, kc_ref, v_ref, qb_ref, kvb_ref, cqn_ref, ckvn_ref):
        cos_t, sin_t = cos_ref[...], sin_ref[...]
        lo = _lo_mask((tm, LANES))
        cqn, _, _ = _norm_fwd(cq_ref[...], gcq_ref[...])
        cqn_ref[...] = cqn.astype(cqn_ref.dtype)
        qb = _dot_nt(cqn, wuq_ref[...])
        qb_ref[...] = qb
        ckvn, _, _ = _norm_fwd(ckv_ref[...], gckv_ref[...])
        ckvn_ref[...] = ckvn.astype(ckvn_ref.dtype)
        kvb = jnp.concatenate([_dot(ckvn, wukv_ref[dev]) for dev in range(N_DEV)], axis=1)
        kvb_ref[...] = kvb
        kr, _, _ = _norm_fwd(kr_ref[...], gkr_ref[...], half=True)
        kr = _rope(kr, cos_t, sin_t)
        kr2 = jnp.where(lo, kr, pltpu.roll(kr, 64, 1))
        ropes = []
        for j in range(nh // 2):
            xr = qb[:, nh * MLA_NOPE + LANES * j: nh * MLA_NOPE + LANES * (j + 1)]
            qr, _, _ = _norm_fwd(xr, gqr_ref[...], half=True)
            ropes.append(_rope(qr, cos_t, sin_t))
        for h in range(nh):
            qn, _, _ = _norm_fwd(qb[:, MLA_NOPE * h: MLA_NOPE * (h + 1)], gqn_ref[...])
            mask = lo if h % 2 == 0 else jnp.logical_not(lo)
            qr = jnp.where(mask, ropes[h // 2], 0.0)
            qc_ref[h] = jnp.concatenate([qn, qr], axis=1).astype(qc_ref.dtype)
            kn, _, _ = _norm_fwd(kvb[:, 256 * h: 256 * h + MLA_NOPE], gkn_ref[...])
            kc_ref[h] = jnp.concatenate([kn, kr2], axis=1).astype(kc_ref.dtype)
            v_ref[h] = kvb[:, 256 * h + MLA_NOPE: 256 * (h + 1)].astype(v_ref.dtype)

    def col(width, start):
        return pl.BlockSpec((tm, width), lambda i: (i, start // width))

    def full(shape):
        return pl.BlockSpec(shape, lambda i: (0,) * len(shape))

    def row(width):
        return pl.BlockSpec((tm, width), lambda i: (i, 0))

    def heads(width):
        return pl.BlockSpec((nh, tm, width), lambda i: (0, i, 0))

    return _pcall(
        body, name="mla_prep", grid=(s // tm,),
        out_shape=[_sds((nh, s, 256), MXU), _sds((nh, s, 256), MXU), _sds((nh, s, MLA_V), MXU),
                   _sds((s, 768), jnp.float32), _sds((s, 1024), jnp.float32),
                   _sds((s, 512), MXU), _sds((s, 512), MXU)],
        in_specs=[col(512, C_CQ), col(512, C_CKV), col(LANES, C_KR), row(LANES), row(LANES),
                  full((1, 512)), full((1, 512)), full((768, 512)), full((N_DEV, 512, LANES)),
                  full((1, LANES)), full((1, LANES)), full((1, LANES)), full((1, LANES))],
        out_specs=[heads(256), heads(256), heads(MLA_V), row(768), row(1024), row(512), row(512)],
        sem=("parallel",))(proj, proj, proj, cos, sin, g_cq, g_ckv, w_uq, w_ukv, g_qn, g_qr, g_kn, g_kr)


def _mla_fwd(qc, kc, v):
    nh, s, _ = qc.shape
    t = min(ATT_TILE, s)
    nb = s // t
    scale = (MLA_NOPE + MLA_ROPE) ** -0.5

    def body(q_ref, k_ref, v_ref, y_ref, lse_ref, m_sc, l_sc, acc):
        qi, ki = pl.program_id(1), pl.program_id(2)

        @pl.when(ki == 0)
        def _():
            m_sc[...] = jnp.full_like(m_sc, NEG_INF)
            l_sc[...] = jnp.zeros_like(l_sc)
            acc[...] = jnp.zeros_like(acc)

        def step(diagonal):
            sc = _dot_nt(q_ref[0], k_ref[0]) * scale
            if diagonal:
                r_i = lax.broadcasted_iota(jnp.int32, sc.shape, 0)
                c_i = lax.broadcasted_iota(jnp.int32, sc.shape, 1)
                sc = jnp.where(c_i <= r_i, sc, NEG_INF)
            m_new = jnp.maximum(m_sc[...], jnp.max(sc, -1, keepdims=True))
            alpha = jnp.exp(m_sc[...] - m_new)
            p = jnp.exp(sc - m_new)
            l_sc[...] = alpha * l_sc[...] + jnp.sum(p, -1, keepdims=True)
            acc[...] = alpha * acc[...] + _dot(p, v_ref[0])
            m_sc[...] = m_new

        @pl.when(ki < qi)
        def _():
            step(False)

        @pl.when(ki == qi)
        def _():
            step(True)

        @pl.when(ki == qi)
        def _():
            y_ref[...] = acc[...] / l_sc[...]
            lse_ref[0] = m_sc[...] + jnp.log(l_sc[...])

    return _pcall(
        body, name="mla_fwd", grid=(nh, nb, nb),
        out_shape=[_sds((s, nh * MLA_V), jnp.float32), _sds((nh, s, 1), jnp.float32)],
        in_specs=[pl.BlockSpec((1, t, 256), lambda h, i, k: (h, i, 0)),
                  pl.BlockSpec((1, t, 256), lambda h, i, k: (h, jnp.minimum(k, i), 0)),
                  pl.BlockSpec((1, t, MLA_V), lambda h, i, k: (h, jnp.minimum(k, i), 0))],
        out_specs=[pl.BlockSpec((t, MLA_V), lambda h, i, k: (i, h)),
                   pl.BlockSpec((1, t, 1), lambda h, i, k: (h, i, 0))],
        scratch=[pltpu.VMEM((t, 1), jnp.float32), pltpu.VMEM((t, 1), jnp.float32),
                 pltpu.VMEM((t, MLA_V), jnp.float32)],
        sem=("parallel", "parallel", "arbitrary"))(qc, kc, v)


def _memkv_prep(mem, g_mem, w_mkv, g_mk):
    ml, d = mem.shape
    hw = MEM_HEADS * MEM_DIM

    def body(mem_ref, g_ref, w_ref, gk_ref, k_ref, v_ref, kv_ref, mn_ref):
        mn, _, _ = _norm_fwd(mem_ref[...], g_ref[...])
        mn_ref[...] = mn.astype(mn_ref.dtype)
        kv = _dot(mn, w_ref[...])
        kv_ref[...] = kv
        for h in range(MEM_HEADS):
            kn, _, _ = _norm_fwd(kv[:, MEM_DIM * h: MEM_DIM * (h + 1)], gk_ref[...])
            k_ref[:, MEM_DIM * h: MEM_DIM * (h + 1)] = kn.astype(k_ref.dtype)
        v_ref[...] = kv[:, hw:].astype(v_ref.dtype)

    vm = pl.BlockSpec(memory_space=pltpu.VMEM)
    return _pcall(
        body, name="memkv_prep",
        out_shape=[_sds((ml, hw), MXU), _sds((ml, hw), MXU), _sds((ml, 2 * hw), jnp.float32), _sds((ml, d), MXU)],
        in_specs=[vm] * 4, out_specs=[vm] * 4)(mem, g_mem, w_mkv, g_mk)


def _mem_fwd(proj, g_mq, km, vmm):
    s = proj.shape[0]
    ml, hw = km.shape
    tm = min(FFN_TILE, s)
    scale = MEM_DIM ** -0.5

    def body(q_ref, g_ref, k_ref, v_ref, y_ref, lse_ref):
        col = lax.broadcasted_iota(jnp.int32, (tm, MEM_HEADS), 1)
        lse_t = jnp.zeros((tm, MEM_HEADS), jnp.float32)
        for h in range(MEM_HEADS):
            sl = slice(MEM_DIM * h, MEM_DIM * (h + 1))
            qn, _, _ = _norm_fwd(q_ref[:, sl], g_ref[...])
            sc = _dot_nt(qn, k_ref[:, sl]) * scale
            m = jnp.max(sc, -1, keepdims=True)
            p = jnp.exp(sc - m)
            l = jnp.sum(p, -1, keepdims=True)
            y_ref[:, sl] = _dot(p, v_ref[:, sl]) / l
            lse_t = jnp.where(col == h, m + jnp.log(l), lse_t)
        lse_ref[...] = lse_t

    return _pcall(
        body, name="mem_fwd", grid=(s // tm,),
        out_shape=[_sds((s, hw), jnp.float32), _sds((s, MEM_HEADS), jnp.float32)],
        in_specs=[pl.BlockSpec((tm, hw), lambda i: (i, C_QM // hw)), pl.BlockSpec((1, MEM_DIM), lambda i: (0, 0)),
                  pl.BlockSpec((ml, hw), lambda i: (0, 0)), pl.BlockSpec((ml, hw), lambda i: (0, 0))],
        out_specs=[pl.BlockSpec((tm, hw), lambda i: (i, 0)), pl.BlockSpec((tm, MEM_HEADS), lambda i: (i, 0))],
        sem=("parallel",))(proj, g_mq, km, vmm)


def _alibi_slope(h):
    return float(2.0 ** (-8.0 * (h + 1) / SWA_Q_HEADS))


def _swa_common(n, kp, kc, vp, vc, pq, pkp, pkc, gk):
    b = SWA_BLOCK
    k_raw = jnp.concatenate([kp, kc], axis=0)
    kn, kxn, kr = _norm_fwd(k_raw, gk, half=True)
    v = jnp.concatenate([vp, vc], axis=0)
    dist = jnp.abs(pq - jnp.concatenate([pkp, pkc], axis=1))
    r_i = lax.broadcasted_iota(jnp.int32, (b, 2 * b), 0)
    c_i = lax.broadcasted_iota(jnp.int32, (b, 2 * b), 1)
    valid = (c_i > r_i) & (c_i <= r_i + b) & (c_i >= jnp.where(n > 0, 0, b))
    return kn, v, dist, valid


def _swa_specs(s):
    b = SWA_BLOCK
    prev = lambda n: jnp.maximum(n - 1, 0)
    return [
        pl.BlockSpec((b, 1024), lambda n: (n, C_QA // 1024)),
        pl.BlockSpec((b, LANES), lambda n: (prev(n), C_KA // LANES)),
        pl.BlockSpec((b, LANES), lambda n: (n, C_KA // LANES)),
        pl.BlockSpec((b, LANES), lambda n: (prev(n), C_VA // LANES)),
        pl.BlockSpec((b, LANES), lambda n: (n, C_VA // LANES)),
        pl.BlockSpec((b, 1), lambda n: (n, 0)),
        pl.BlockSpec((1, b), lambda n: (0, prev(n))),
        pl.BlockSpec((1, b), lambda n: (0, n)),
        pl.BlockSpec((1, LANES), lambda n: (0, 0)),
        pl.BlockSpec((1, LANES), lambda n: (0, 0)),
        pl.BlockSpec(memory_space=pltpu.SMEM),
    ]


def _swa_fwd(proj, posc, posr, gq, gk, sinks):
    s = proj.shape[0]
    b = SWA_BLOCK
    scale = SWA_DIM ** -0.5

    def body(q_ref, kp_ref, kc_ref, vp_ref, vc_ref, pq_ref, pkp_ref, pkc_ref, gq_ref, gk_ref, sink_ref,
             y_ref, lse_ref):
        n = pl.program_id(0)
        kn, v, dist, valid = _swa_common(n, kp_ref[...], kc_ref[...], vp_ref[...], vc_ref[...],
                                         pq_ref[...], pkp_ref[...], pkc_ref[...], gk_ref[...])
        lo = _lo_mask((b, LANES))
        col = lax.broadcasted_iota(jnp.int32, (b, SWA_Q_HEADS), 1)
        lse_t = jnp.zeros((b, SWA_Q_HEADS), jnp.float32)
        for j in range(SWA_Q_HEADS // 2):
            hk = (2 * j) // (SWA_Q_HEADS // SWA_KV_HEADS)
            kvmask = lo if hk == 0 else jnp.logical_not(lo)
            qn, _, _ = _norm_fwd(q_ref[:, LANES * j: LANES * (j + 1)], gq_ref[...], half=True)
            qsw = pltpu.roll(qn, 64, 1)
            outs = []
            for e in range(2):
                h = 2 * j + e
                qm = jnp.where(kvmask, qn if e == hk else qsw, 0.0)
                sc = _dot_nt(qm, kn) * scale - _alibi_slope(h) * dist
                sc = jnp.where(valid, sc, NEG_INF)
                sk = sink_ref[h]
                m = jnp.maximum(jnp.max(sc, -1, keepdims=True), sk)
                p = jnp.exp(sc - m)
                l = jnp.sum(p, -1, keepdims=True) + jnp.exp(sk - m)
                o = _dot(p, v) / l
                outs.append(o if e == hk else pltpu.roll(o, 64, 1))
                lse_t = jnp.where(col == h, m + jnp.log(l), lse_t)
            y_ref[:, LANES * j: LANES * (j + 1)] = jnp.where(lo, outs[0], outs[1])
        lse_ref[...] = lse_t

    return _pcall(
        body, name="swa_fwd", grid=(s // b,),
        out_shape=[_sds((s, 1024), jnp.float32), _sds((s, SWA_Q_HEADS), jnp.float32)],
        in_specs=_swa_specs(s),
        out_specs=[pl.BlockSpec((b, 1024), lambda n: (n, 0)), pl.BlockSpec((b, SWA_Q_HEADS), lambda n: (n, 0))],
        sem=("parallel",))(proj, proj, proj, proj, proj, posc, posr, posr, gq, gk, sinks)


def _out_proj(y_a, y_b, y_m, x, w_out, g_ffn):
    s, d = x.shape
    tm = min(ROW_TILE, s)

    def body(ya_ref, yb_ref, ym_ref, x_ref, w_ref, g_ref, h1_ref, fn_ref):
        y = jnp.concatenate([ya_ref[...].astype(MXU), yb_ref[...].astype(MXU), ym_ref[...].astype(MXU)], axis=1)
        h1 = x_ref[...] + _dot(y, w_ref[...])
        h1_ref[...] = h1
        fn, _, _ = _norm_fwd(h1, g_ref[...])
        fn_ref[...] = fn.astype(fn_ref.dtype)

    def row(width):
        return pl.BlockSpec((tm, width), lambda i: (i, 0))

    return _pcall(
        body, name="out_proj", grid=(s // tm,),
        out_shape=[_sds((s, d), jnp.float32), _sds((s, d), MXU)],
        in_specs=[row(1024), row(512), row(512), row(d), pl.BlockSpec(w_out.shape, lambda i: (0, 0)),
                  pl.BlockSpec((1, d), lambda i: (0, 0))],
        out_specs=[row(d), row(d)], sem=("parallel",))(y_a, y_b, y_m, x, w_out, g_ffn)


def _ffn_gu(fn, w_gu):
    s, d = fn.shape
    f = w_gu.shape[2]
    tm = min(FFN_TILE, s)

    def body(fn_ref, w_ref, gu_ref, act_ref):
        x = fn_ref[...]
        g = _dot_nt(x, w_ref[0, 0])
        u = _dot_nt(x, w_ref[0, 1])
        gu_ref[0, 0] = g
        gu_ref[0, 1] = u
        act_ref[0] = (g * jax.nn.sigmoid(g) * u).astype(act_ref.dtype)

    return _pcall(
        body, name="ffn_gate_up", grid=(N_DEV, s // tm),
        out_shape=[_sds((N_DEV, 2, s, f), jnp.float32), _sds((N_DEV, s, f), MXU)],
        in_specs=[pl.BlockSpec((tm, d), lambda j, i: (i, 0)),
                  pl.BlockSpec((1, 2, f, d), lambda j, i: (j, 0, 0, 0))],
        out_specs=[pl.BlockSpec((1, 2, tm, f), lambda j, i: (j, 0, i, 0)),
                   pl.BlockSpec((1, tm, f), lambda j, i: (j, i, 0))],
        sem=("parallel", "parallel"))(fn, w_gu)


def _ffn_down(act, w_d, h1, target):
    _, s, f = act.shape
    d = h1.shape[1]
    tm = min(FFN_TILE, s)

    def body(a_ref, w_ref, h1_ref, t_ref, dout_ref, loss_ref, acc):
        i, j = pl.program_id(0), pl.program_id(1)
        part = _dot(a_ref[0], w_ref[0]) + _dot(a_ref[1], w_ref[1])

        @pl.when(j == 0)
        def _():
            acc[...] = h1_ref[...] + part

        @pl.when(j > 0)
        def _():
            acc[...] += part

        @pl.when((i == 0) & (j == 0))
        def _():
            loss_ref[...] = jnp.zeros_like(loss_ref)

        @pl.when(j == N_DEV // 2 - 1)
        def _():
            diff = acc[...] - t_ref[...]
            dout_ref[...] = diff / d
            loss_ref[...] += 0.5 * jnp.sum(jnp.sum(diff * diff, -1, keepdims=True) / d)

    row = pl.BlockSpec((tm, d), lambda i, j: (i, 0))
    return _pcall(
        body, name="ffn_down", grid=(s // tm, N_DEV // 2),
        out_shape=[_sds((s, d), jnp.float32), _sds((8, LANES), jnp.float32)],
        in_specs=[pl.BlockSpec((2, tm, f), lambda i, j: (j, i, 0)), pl.BlockSpec((2, f, d), lambda i, j: (j, 0, 0)),
                  row, row],
        out_specs=[row, pl.BlockSpec((8, LANES), lambda i, j: (0, 0))],
        scratch=[pltpu.VMEM((tm, d), jnp.float32)], sem=("arbitrary", "arbitrary"))(act, w_d, h1, target)


def _ffn_bwd_act(dout, w_d, gu):
    s, d = dout.shape
    f = w_d.shape[1]
    tm = min(FFN_TILE, s)
    ni = s // tm

    def body(do_ref, w_ref, gu_ref, dgu_ref, dw_ref, acc):
        i = pl.program_id(1)
        do = do_ref[...].astype(MXU)
        d_act = _dot_nt(do, w_ref[0])
        g, u = gu_ref[0, 0], gu_ref[0, 1]
        sig = jax.nn.sigmoid(g)
        silu = g * sig
        dgu_ref[0, 0] = (d_act * u * (sig * (1.0 + g * (1.0 - sig)))).astype(dgu_ref.dtype)
        dgu_ref[0, 1] = (d_act * silu).astype(dgu_ref.dtype)
        part = _dot_tn(silu * u, do)

        @pl.when(i == 0)
        def _():
            acc[...] = part

        @pl.when(i > 0)
        def _():
            acc[...] += part

        @pl.when(i == ni - 1)
        def _():
            dw_ref[0] = acc[...].astype(dw_ref.dtype)

    return _pcall(
        body, name="ffn_bwd_act", grid=(N_DEV, ni),
        out_shape=[_sds((N_DEV, 2, s, f), MXU), _sds((N_DEV, f, d), WIRE)],
        in_specs=[pl.BlockSpec((tm, d), lambda j, i: (i, 0)), pl.BlockSpec((1, f, d), lambda j, i: (j, 0, 0)),
                  pl.BlockSpec((1, 2, tm, f), lambda j, i: (j, 0, i, 0))],
        out_specs=[pl.BlockSpec((1, 2, tm, f), lambda j, i: (j, 0, i, 0)),
                   pl.BlockSpec((1, f, d), lambda j, i: (j, 0, 0))],
        scratch=[pltpu.VMEM((f, d), jnp.float32)], sem=("parallel", "arbitrary"))(dout, w_d, gu)


def _ffn_dw_gu(fn, dgu):
    s, d = fn.shape
    f = dgu.shape[-1]
    tk = min(2 * FFN_TILE, s)
    nk = s // tk

    def body(fn_ref, dgu_ref, dw_ref, acc):
        k = pl.program_id(1)
        x = fn_ref[...]
        pg = _dot_tn(dgu_ref[0, 0], x)
        pu = _dot_tn(dgu_ref[0, 1], x)

        @pl.when(k == 0)
        def _():
            acc[0] = pg
            acc[1] = pu

        @pl.when(k > 0)
        def _():
            acc[0] += pg
            acc[1] += pu

        @pl.when(k == nk - 1)
        def _():
            dw_ref[0] = acc[...].astype(dw_ref.dtype)

    return _pcall(
        body, name="ffn_dw_gate_up", grid=(N_DEV, nk),
        out_shape=_sds((N_DEV, 2, f, d), WIRE),
        in_specs=[pl.BlockSpec((tk, d), lambda j, k: (k, 0)), pl.BlockSpec((1, 2, tk, f), lambda j, k: (j, 0, k, 0))],
        out_specs=pl.BlockSpec((1, 2, f, d), lambda j, k: (j, 0, 0, 0)),
        scratch=[pltpu.VMEM((2, f, d), jnp.float32)], sem=("parallel", "arbitrary"))(fn, dgu)


def _ffn_dfn(dgu, w_gu, after):
    _, _, s, f = dgu.shape
    d = w_gu.shape[3]
    tm = min(FFN_TILE, s)

    def body(dgu_ref, w_ref, dfn_ref):
        j = pl.program_id(1)
        part = (_dot(dgu_ref[0, 0], w_ref[0, 0]) + _dot(dgu_ref[0, 1], w_ref[0, 1])
                + _dot(dgu_ref[1, 0], w_ref[1, 0]) + _dot(dgu_ref[1, 1], w_ref[1, 1]))

        @pl.when(j == 0)
        def _():
            dfn_ref[...] = part

        @pl.when(j > 0)
        def _():
            dfn_ref[...] += part

    return _pcall(
        body, name="ffn_dfn", grid=(s // tm, N_DEV // 2),
        out_shape=_sds((s, d), jnp.float32),
        in_specs=[pl.BlockSpec((2, 2, tm, f), lambda i, j: (j, 0, i, 0)),
                  pl.BlockSpec((2, 2, f, d), lambda i, j: (j, 0, 0, 0))],
        out_specs=pl.BlockSpec((tm, d), lambda i, j: (i, 0)),
        sem=("parallel", "arbitrary"), after=after)(dgu, w_gu)


def _ffn_norm_bwd(d_fn, dout, h1, g_ffn):
    s, d = h1.shape
    tm = min(ROW_TILE, s)

    def body(dfn_ref, do_ref, h1_ref, g_ref, dh1_ref, dg_ref):
        i = pl.program_id(0)

        @pl.when(i == 0)
        def _():
            dg_ref[...] = jnp.zeros_like(dg_ref)

        _, xn, r = _norm_fwd(h1_ref[...], g_ref[...])
        dx, dg = _norm_bwd(xn, r, g_ref[...], dfn_ref[...])
        dh1_ref[...] = do_ref[...] + dx
        dg_ref[...] += dg

    row = pl.BlockSpec((tm, d), lambda i: (i, 0))
    vec = pl.BlockSpec((1, d), lambda i: (0, 0))
    return _pcall(
        body, name="ffn_norm_bwd", grid=(s // tm,),
        out_shape=[_sds((s, d), jnp.float32), _sds((1, d), jnp.float32)],
        in_specs=[row, row, row, vec], out_specs=[row, vec], sem=("arbitrary",))(d_fn, dout, h1, g_ffn)


def _mem_bwd(proj, g_mq, km, vmm, d_y, y_m, lse):
    s = proj.shape[0]
    ml, hw = km.shape
    tm = min(FFN_TILE, s)
    scale = MEM_DIM ** -0.5

    def body(q_ref, g_ref, k_ref, v_ref, do_ref, y_ref, lse_ref, dq_ref, dk_ref, dv_ref, dg_ref):
        i = pl.program_id(0)

        @pl.when(i == 0)
        def _():
            dk_ref[...] = jnp.zeros_like(dk_ref)
            dv_ref[...] = jnp.zeros_like(dv_ref)
            dg_ref[...] = jnp.zeros_like(dg_ref)

        col = lax.broadcasted_iota(jnp.int32, (tm, MEM_HEADS), 1)
        lse_t = lse_ref[...]
        for h in range(MEM_HEADS):
            sl = slice(MEM_DIM * h, MEM_DIM * (h + 1))
            qn, xn, r = _norm_fwd(q_ref[:, sl], g_ref[...])
            lse_h = jnp.sum(jnp.where(col == h, lse_t, 0.0), -1, keepdims=True)
            p = jnp.exp(_dot_nt(qn, k_ref[:, sl]) * scale - lse_h)
            do = do_ref[:, sl]
            dd = jnp.sum(do * y_ref[:, sl], -1, keepdims=True)
            dp = _dot_nt(do, v_ref[:, sl])
            ds = (p * (dp - dd)).astype(MXU)
            dv_ref[:, sl] += _dot_tn(p, do)
            dk_ref[:, sl] += _dot_tn(ds, qn) * scale
            dx, dg = _norm_bwd(xn, r, g_ref[...], _dot(ds, k_ref[:, sl]) * scale)
            dq_ref[:, sl] = dx.astype(dq_ref.dtype)
            dg_ref[...] += dg

    full = pl.BlockSpec((ml, hw), lambda i: (0, 0))
    return _pcall(
        body, name="mem_bwd", grid=(s // tm,),
        out_shape=[_sds((s, hw), MXU), _sds((ml, hw), jnp.float32), _sds((ml, hw), jnp.float32),
                   _sds((1, MEM_DIM), jnp.float32)],
        in_specs=[pl.BlockSpec((tm, hw), lambda i: (i, C_QM // hw)), pl.BlockSpec((1, MEM_DIM), lambda i: (0, 0)),
                  full, full, pl.BlockSpec((tm, hw), lambda i: (i, 3)), pl.BlockSpec((tm, hw), lambda i: (i, 0)),
                  pl.BlockSpec((tm, MEM_HEADS), lambda i: (i, 0))],
        out_specs=[pl.BlockSpec((tm, hw), lambda i: (i, 0)), full, full,
                   pl.BlockSpec((1, MEM_DIM), lambda i: (0, 0))],
        sem=("arbitrary",))(proj, g_mq, km, vmm, d_y, y_m, lse)


def _memkv_bwd(mem, g_mem, w_mkv, g_mk, kv, memn, dk, dv):
    ml, d = mem.shape
    hw = MEM_HEADS * MEM_DIM

    def body(mem_ref, g_ref, w_ref, gk_ref, kv_ref, mn_ref, dk_ref, dv_ref, dw_ref, dgm_ref, dgk_ref):
        parts = []
        dgk = jnp.zeros((1, MEM_DIM), jnp.float32)
        for h in range(MEM_HEADS):
            sl = slice(MEM_DIM * h, MEM_DIM * (h + 1))
            _, xn, r = _norm_fwd(kv_ref[:, sl], gk_ref[...])
            dx, dg = _norm_bwd(xn, r, gk_ref[...], dk_ref[:, sl])
            parts.append(dx)
            dgk = dgk + dg
        dkv = jnp.concatenate(parts + [dv_ref[...]], axis=1).astype(MXU)
        dgk_ref[...] = dgk
        dw_ref[...] = _dot_tn(mn_ref[...], dkv).astype(dw_ref.dtype)
        d_mn = _dot_nt(dkv, w_ref[...])
        _, xn, _ = _norm_fwd(mem_ref[...], g_ref[...])
        dgm_ref[...] = jnp.sum(d_mn * xn, 0, keepdims=True)

    vm = pl.BlockSpec(memory_space=pltpu.VMEM)
    return _pcall(
        body, name="memkv_bwd",
        out_shape=[_sds((d, 2 * hw), WIRE), _sds((1, d), jnp.float32), _sds((1, MEM_DIM), jnp.float32)],
        in_specs=[vm] * 8, out_specs=[vm] * 3)(mem, g_mem, w_mkv, g_mk, kv, memn, dk, dv)


def _mla_bwd(qc, kc, v, d_y, y_b, lse, after):
    nh, s, _ = qc.shape
    t = min(ATT_TILE, s)
    nb = s // t
    scale = (MLA_NOPE + MLA_ROPE) ** -0.5

    def body(q_ref, k_ref, v_ref, do_ref, y_ref, lse_ref, dq_ref, dk_ref, dv_ref, dk_acc, dv_acc):
        kj, qi = pl.program_id(1), pl.program_id(2)

        @pl.when((kj == 0) & (qi == 0))
        def _():
            dq_ref[...] = jnp.zeros_like(dq_ref)

        @pl.when(qi == kj)
        def _():
            dk_acc[...] = jnp.zeros_like(dk_acc)
            dv_acc[...] = jnp.zeros_like(dv_acc)

        def step(diagonal):
            q, k = q_ref[0], k_ref[0]
            sc = _dot_nt(q, k) * scale
            if diagonal:
                r_i = lax.broadcasted_iota(jnp.int32, sc.shape, 0)
                c_i = lax.broadcasted_iota(jnp.int32, sc.shape, 1)
                sc = jnp.where(c_i <= r_i, sc, NEG_INF)
            p = jnp.exp(sc - lse_ref[0])
            do = do_ref[...]
            dd = jnp.sum(do * y_ref[...], -1, keepdims=True)
            dp = _dot_nt(do, v_ref[0])
            ds = (p * (dp - dd) * scale).astype(MXU)
            dv_acc[...] += _dot_tn(p, do)
            dk_acc[...] += _dot_tn(ds, q)
            rows = pl.ds(pl.multiple_of(qi * t, t), t)
            dq_ref[0, rows, :] += _dot(ds, k)

        @pl.when(qi > kj)
        def _():
            step(False)

        @pl.when(qi == kj)
        def _():
            step(True)

        @pl.when(qi == nb - 1)
        def _():
            dk_ref[0] = dk_acc[...]
            dv_ref[0] = dv_acc[...]

    qmap = lambda h, j, i: (h, jnp.maximum(i, j), 0)
    return _pcall(
        body, name="mla_bwd", grid=(nh, nb, nb),
        out_shape=[_sds((nh, s, 256), jnp.float32), _sds((nh, s, 256), jnp.float32),
                   _sds((nh, s, MLA_V), jnp.float32)],
        in_specs=[pl.BlockSpec((1, t, 256), qmap),
                  pl.BlockSpec((1, t, 256), lambda h, j, i: (h, j, 0)),
                  pl.BlockSpec((1, t, MLA_V), lambda h, j, i: (h, j, 0)),
                  pl.BlockSpec((t, MLA_V), lambda h, j, i: (jnp.maximum(i, j), 8 + h)),
                  pl.BlockSpec((t, MLA_V), lambda h, j, i: (jnp.maximum(i, j), h)),
                  pl.BlockSpec((1, t, 1), qmap)],
        out_specs=[pl.BlockSpec((1, s, 256), lambda h, j, i: (h, 0, 0)),
                   pl.BlockSpec((1, t, 256), lambda h, j, i: (h, j, 0)),
                   pl.BlockSpec((1, t, MLA_V), lambda h, j, i: (h, j, 0))],
        scratch=[pltpu.VMEM((t, 256), jnp.float32), pltpu.VMEM((t, MLA_V), jnp.float32)],
        sem=("parallel", "arbitrary", "arbitrary"), after=after)(qc, kc, v, d_y, y_b, lse)


def _mla_prep_bwd(proj, cos, sin, g_cq, g_ckv, w_uq, w_ukv, g_qn, g_qr, g_kn, g_kr,
                  qb, kvb, cqn, ckvn, dqc, dkc, dv):
    s = proj.shape[0]
    tm = min(ROW_TILE, s)
    nh = MLA_HEADS
    ni = s // tm

    def body(cq_ref, ckv_ref, kr_ref, cos_ref, sin_ref, gcq_ref, gckv_ref, wuq_ref, wukv_ref,
             gqn_ref, gqr_ref, gkn_ref, gkr_ref, qb_ref, kvb_ref, cqn_ref, ckvn_ref, dqc_ref, dkc_ref, dv_ref,
             dcq_ref, dckv_ref, dkr_ref, dwuq_ref, dwukv_ref,
             dgcq_ref, dgckv_ref, dgqn_ref, dgqr_ref, dgkn_ref, dgkr_ref, acc_uq, acc_ukv):
        i = pl.program_id(0)

        @pl.when(i == 0)
        def _():
            acc_uq[...] = jnp.zeros_like(acc_uq)
            acc_ukv[...] = jnp.zeros_like(acc_ukv)
            for ref in (dgcq_ref, dgckv_ref, dgqn_ref, dgqr_ref, dgkn_ref, dgkr_ref):
                ref[...] = jnp.zeros_like(ref)

        cos_t, sin_t = cos_ref[...], sin_ref[...]
        lo = _lo_mask((tm, LANES))
        qb_v, kvb_v = qb_ref[...], kvb_ref[...]
        dq_parts, dgqn = [], jnp.zeros((1, LANES), jnp.float32)
        for h in range(nh):
            _, xn, r = _norm_fwd(qb_v[:, MLA_NOPE * h: MLA_NOPE * (h + 1)], gqn_ref[...])
            dx, dg = _norm_bwd(xn, r, gqn_ref[...], dqc_ref[h][:, :MLA_NOPE])
            dq_parts.append(dx)
            dgqn = dgqn + dg
        dgqn_ref[...] += dgqn
        dgqr = jnp.zeros((1, LANES), jnp.float32)
        for j in range(nh // 2):
            d_rope = jnp.where(lo, dqc_ref[2 * j][:, MLA_NOPE:], dqc_ref[2 * j + 1][:, MLA_NOPE:])
            d_pre = _rope_bwd(d_rope, cos_t, sin_t)
            xr = qb_v[:, nh * MLA_NOPE + LANES * j: nh * MLA_NOPE + LANES * (j + 1)]
            _, xn, r = _norm_fwd(xr, gqr_ref[...], half=True)
            dx, dg = _norm_bwd(xn, r, gqr_ref[...], d_pre, half=True)
            dq_parts.append(dx)
            dgqr = dgqr + dg
        dgqr_ref[...] += dgqr
        dqb = jnp.concatenate(dq_parts, axis=1).astype(MXU)
        acc_uq[...] += _dot_tn(dqb, cqn_ref[...])
        _, xn, r = _norm_fwd(cq_ref[...], gcq_ref[...])
        dx, dg = _norm_bwd(xn, r, gcq_ref[...], _dot(dqb, wuq_ref[...]))
        dcq_ref[...] = dx.astype(dcq_ref.dtype)
        dgcq_ref[...] += dg
        dkv_parts, dgkn = [], jnp.zeros((1, LANES), jnp.float32)
        d_kr2 = jnp.zeros((tm, LANES), jnp.float32)
        for h in range(nh):
            _, xn, r = _norm_fwd(kvb_v[:, 256 * h: 256 * h + MLA_NOPE], gkn_ref[...])
            dx, dg = _norm_bwd(xn, r, gkn_ref[...], dkc_ref[h][:, :MLA_NOPE])
            dkv_parts += [dx, dv_ref[h]]
            dgkn = dgkn + dg
            d_kr2 = d_kr2 + dkc_ref[h][:, MLA_NOPE:]
        dgkn_ref[...] += dgkn
        dkvb = jnp.concatenate(dkv_parts, axis=1).astype(MXU)
        d_ckvn = jnp.zeros((tm, 512), jnp.float32)
        for dev in range(N_DEV):
            piece = dkvb[:, LANES * dev: LANES * (dev + 1)]
            acc_ukv[dev] += _dot_tn(ckvn_ref[...], piece)
            d_ckvn = d_ckvn + _dot_nt(piece, wukv_ref[dev])
        _, xn, r = _norm_fwd(ckv_ref[...], gckv_ref[...])
        dx, dg = _norm_bwd(xn, r, gckv_ref[...], d_ckvn)
        dckv_ref[...] = dx.astype(dckv_ref.dtype)
        dgckv_ref[...] += dg
        d_kr = jnp.where(lo, d_kr2 + pltpu.roll(d_kr2, 64, 1), 0.0)
        d_pre = _rope_bwd(d_kr, cos_t, sin_t)
        _, xn, r = _norm_fwd(kr_ref[...], gkr_ref[...], half=True)
        dx, dg = _norm_bwd(xn, r, gkr_ref[...], d_pre, half=True)
        dkr_ref[...] = jnp.where(lo, dx, 0.0).astype(dkr_ref.dtype)
        dgkr_ref[...] += jnp.where(_lo_mask((1, LANES)), dg, 0.0)

        @pl.when(i == ni - 1)
        def _():
            dwuq_ref[...] = acc_uq[...].astype(dwuq_ref.dtype)
            dwukv_ref[...] = acc_ukv[...].astype(dwukv_ref.dtype)

    def col(width, start):
        return pl.BlockSpec((tm, width), lambda i: (i, start // width))

    def full(shape):
        return pl.BlockSpec(shape, lambda i: (0,) * len(shape))

    def row(width):
        return pl.BlockSpec((tm, width), lambda i: (i, 0))

    def heads(width):
        return pl.BlockSpec((nh, tm, width), lambda i: (0, i, 0))

    vec = full((1, LANES))
    return _pcall(
        body, name="mla_prep_bwd", grid=(ni,),
        out_shape=[_sds((s, 512), MXU), _sds((s, 512), MXU), _sds((s, LANES), MXU),
                   _sds((768, 512), WIRE), _sds((N_DEV, 512, LANES), WIRE),
                   _sds((1, 512), jnp.float32), _sds((1, 512), jnp.float32)] + [_sds((1, LANES), jnp.float32)] * 4,
        in_specs=[col(512, C_CQ), col(512, C_CKV), col(LANES, C_KR), row(LANES), row(LANES),
                  full((1, 512)), full((1, 512)), full((768, 512)), full((N_DEV, 512, LANES)), vec, vec, vec, vec,
                  row(768), row(1024), row(512), row(512), heads(256), heads(256), heads(MLA_V)],
        out_specs=[row(512), row(512), row(LANES), full((768, 512)), full((N_DEV, 512, LANES)),
                   full((1, 512)), full((1, 512)), vec, vec, vec, vec],
        scratch=[pltpu.VMEM((768, 512), jnp.float32), pltpu.VMEM((N_DEV, 512, LANES), jnp.float32)],
        sem=("arbitrary",))(proj, proj, proj, cos, sin, g_cq, g_ckv, w_uq, w_ukv, g_qn, g_qr, g_kn, g_kr,
                            qb, kvb, cqn, ckvn, dqc, dkc, dv)


def _swa_bwd(proj, posc, posr, gq, gk, sinks, d_y, y_a, lse, after):
    s = proj.shape[0]
    b = SWA_BLOCK
    nb = s // b
    scale = SWA_DIM ** -0.5

    def body(q_ref, kp_ref, kc_ref, vp_ref, vc_ref, pq_ref, pkp_ref, pkc_ref, gq_ref, gk_ref, sink_ref,
             do_ref, y_ref, lse_ref, kfull_ref,
             dq_ref, dk_ref, dv_ref, dgq_ref, dgk_ref, dsink_ref, dk_acc, dv_acc):
        n = pl.program_id(0)

        @pl.when(n == 0)
        def _():
            dk_acc[...] = jnp.zeros_like(dk_acc)
            dv_acc[...] = jnp.zeros_like(dv_acc)
            dgq_ref[...] = jnp.zeros_like(dgq_ref)
            dsink_ref[...] = jnp.zeros_like(dsink_ref)

        kn, v, dist, valid = _swa_common(n, kp_ref[...], kc_ref[...], vp_ref[...], vc_ref[...],
                                         pq_ref[...], pkp_ref[...], pkc_ref[...], gk_ref[...])
        lo = _lo_mask((b, LANES))
        col = lax.broadcasted_iota(jnp.int32, (b, SWA_Q_HEADS), 1)
        col1 = lax.broadcasted_iota(jnp.int32, (1, SWA_Q_HEADS), 1)
        lse_t = lse_ref[...]
        dk_blk = jnp.zeros((2 * b, LANES), jnp.float32)
        dv_blk = jnp.zeros((2 * b, LANES), jnp.float32)
        dgq = jnp.zeros((1, LANES), jnp.float32)
        dsink = jnp.zeros((1, SWA_Q_HEADS), jnp.float32)
        for j in range(SWA_Q_HEADS // 2):
            hk = (2 * j) // (SWA_Q_HEADS // SWA_KV_HEADS)
            kvmask = lo if hk == 0 else jnp.logical_not(lo)
            sl = slice(LANES * j, LANES * (j + 1))
            qn, xn, r = _norm_fwd(q_ref[:, sl], gq_ref[...], half=True)
            qsw = pltpu.roll(qn, 64, 1)
            d2 = do_ref[:, sl]
            d2sw = pltpu.roll(d2, 64, 1)
            prod = d2 * y_ref[:, sl]
            dqs = []
            for e in range(2):
                h = 2 * j + e
                half_e = lo if e == 0 else jnp.logical_not(lo)
                qm = jnp.where(kvmask, qn if e == hk else qsw, 0.0)
                dm = jnp.where(kvmask, d2 if e == hk else d2sw, 0.0)
                sc = _dot_nt(qm, kn) * scale - _alibi_slope(h) * dist
                sc = jnp.where(valid, sc, NEG_INF)
                lse_h = jnp.sum(jnp.where(col == h, lse_t, 0.0), -1, keepdims=True)
                p = jnp.exp(sc - lse_h)
                dd = jnp.sum(jnp.where(half_e, prod, 0.0), -1, keepdims=True)
                dp = _dot_nt(dm, v)
                ds = (p * (dp - dd)).astype(MXU)
                dsink = dsink - jnp.where(col1 == h, jnp.sum(jnp.exp(sink_ref[h] - lse_h) * dd), 0.0)
                dq_m = _dot(ds, kn) * scale
                dk_blk = dk_blk + _dot_tn(ds, qm) * scale
                dv_blk = dv_blk + _dot_tn(p, dm)
                dqs.append(dq_m if e == hk else pltpu.roll(dq_m, 64, 1))
            dx, dg = _norm_bwd(xn, r, gq_ref[...], jnp.where(lo, dqs[0], dqs[1]), half=True)
            dq_ref[:, sl] = dx.astype(dq_ref.dtype)
            dgq = dgq + dg
        dgq_ref[...] += dgq
        dsink_ref[...] += dsink
        prev = pl.ds(pl.multiple_of(jnp.maximum(n - 1, 0) * b, b), b)
        cur = pl.ds(pl.multiple_of(n * b, b), b)
        dk_acc[prev, :] += dk_blk[:b]
        dv_acc[prev, :] += dv_blk[:b]
        dk_acc[cur, :] += dk_blk[b:]
        dv_acc[cur, :] += dv_blk[b:]

        @pl.when(n == nb - 1)
        def _():
            _, kxn, kr = _norm_fwd(kfull_ref[...], gk_ref[...], half=True)
            dx, dg = _norm_bwd(kxn, kr, gk_ref[...], dk_acc[...], half=True)
            dk_ref[...] = dx.astype(dk_ref.dtype)
            dv_ref[...] = dv_acc[...].astype(dv_ref.dtype)
            dgk_ref[...] = dg

    full = pl.BlockSpec((s, LANES), lambda n: (0, 0))
    vec = pl.BlockSpec((1, LANES), lambda n: (0, 0))
    return _pcall(
        body, name="swa_bwd", grid=(nb,),
        out_shape=[_sds((s, 1024), MXU), _sds((s, LANES), MXU), _sds((s, LANES), MXU),
                   _sds((1, LANES), jnp.float32), _sds((1, LANES), jnp.float32),
                   _sds((1, SWA_Q_HEADS), jnp.float32)],
        in_specs=_swa_specs(s) + [pl.BlockSpec((b, 1024), lambda n: (n, 0)), pl.BlockSpec((b, 1024), lambda n: (n, 0)),
                                  pl.BlockSpec((b, SWA_Q_HEADS), lambda n: (n, 0)),
                                  pl.BlockSpec((s, LANES), lambda n: (0, C_KA // LANES))],
        out_specs=[pl.BlockSpec((b, 1024), lambda n: (n, 0)), full, full, vec, vec,
                   pl.BlockSpec((1, SWA_Q_HEADS), lambda n: (0, 0))],
        scratch=[pltpu.VMEM((s, LANES), jnp.float32), pltpu.VMEM((s, LANES), jnp.float32)],
        sem=("arbitrary",), after=after)(proj, proj, proj, proj, proj, posc, posr, posr, gq, gk, sinks, d_y, y_a, lse,
                                         proj)


def _dx(d_proj, w_in, x, g, d_h1, after):
    s, d = x.shape
    n = w_in.shape[0]
    tm = min(ROW_TILE, s)

    def body(dp_ref, w_ref, x_ref, g_ref, dh_ref, dx_ref, dg_ref):
        i = pl.program_id(0)

        @pl.when(i == 0)
        def _():
            dg_ref[...] = jnp.zeros_like(dg_ref)

        d_hn = _dot(dp_ref[...], w_ref[...])
        _, xn, r = _norm_fwd(x_ref[...], g_ref[...])
        dx, dg = _norm_bwd(xn, r, g_ref[...], d_hn)
        dx_ref[...] = dh_ref[...] + dx
        dg_ref[...] += dg

    row = pl.BlockSpec((tm, d), lambda i: (i, 0))
    vec = pl.BlockSpec((1, d), lambda i: (0, 0))
    return _pcall(
        body, name="grad_x", grid=(s // tm,),
        out_shape=[_sds((s, d), jnp.float32), _sds((1, d), jnp.float32)],
        in_specs=[pl.BlockSpec((tm, n), lambda i: (i, 0)), pl.BlockSpec((n, d), lambda i: (0, 0)), row, vec, row],
        out_specs=[row, vec], sem=("arbitrary",), after=after)(d_proj, w_in, x, g, d_h1)


_SMALL = ["attn_norm_g", "swa_q_norm_g", "swa_k_norm_g", "swa_sinks", "mla_cq_norm_g", "mla_ckv_norm_g",
          "mla_qn_norm_g", "mla_qr_norm_g", "mla_kn_norm_g", "mla_kr_norm_g", "mem_norm_g",
          "mem_q_norm_g", "mem_k_norm_g", "ffn_norm_g"]


def kernel(x, mem, positions, attn_norm_g, w_in, swa_q_norm_g, swa_k_norm_g, swa_sinks, mla_cq_norm_g, mla_ckv_norm_g, w_uq, w_ukv, mla_qn_norm_g, mla_qr_norm_g, mla_kn_norm_g, mla_kr_norm_g, mem_norm_g, w_mem_kv, mem_q_norm_g, mem_k_norm_g, w_out, ffn_norm_g, w_gate, w_up, w_down, loss_target, m_attn_norm_g, m_w_in, m_swa_q_norm_g, m_swa_k_norm_g, m_swa_sinks, m_mla_cq_norm_g, m_mla_ckv_norm_g, m_w_uq, m_w_ukv, m_mla_qn_norm_g, m_mla_qr_norm_g, m_mla_kn_norm_g, m_mla_kr_norm_g, m_mem_norm_g, m_w_mem_kv, m_mem_q_norm_g, m_mem_k_norm_g, m_w_out, m_ffn_norm_g, m_w_gate, m_w_up, m_w_down, v_attn_norm_g, v_w_in, v_swa_q_norm_g, v_swa_k_norm_g, v_swa_sinks, v_mla_cq_norm_g, v_mla_ckv_norm_g, v_w_uq, v_w_ukv, v_mla_qn_norm_g, v_mla_qr_norm_g, v_mla_kn_norm_g, v_mla_kr_norm_g, v_mem_norm_g, v_w_mem_kv, v_mem_q_norm_g, v_mem_k_norm_g, v_w_out, v_ffn_norm_g, v_w_gate, v_w_up, v_w_down):
    args = dict(locals())
    x2, mem2, tgt = x[0], mem[0], loss_target[0]
    s, d = x2.shape
    n_in = w_in.shape[2]
    f = w_gate.shape[2]

    (g_in,) = _all_gather([w_in[0].T.astype(WIRE)])
    mix_shards = [w_uq[0].T.astype(WIRE), w_ukv[0].astype(WIRE), w_mem_kv[0].astype(WIRE),
                  _to_wire([w_out[0]], g_in, "wire_out")[0]]
    g_uq, wkv, g_mkv, g_out = _all_gather_background(mix_shards, 5, "all_gather_mix_weights")
    ffn_shards = [_to_wire([w_gate[0].T, w_up[0].T], g_in, "wire_gate_up"),
                  _to_wire([w_down[0]], g_in, "wire_down")[0]]
    w_gu, w_d = _all_gather_background(ffn_shards, 1, "all_gather_ffn_weights")
    wi = g_in.reshape(N_DEV * n_in, d)
    wi = jnp.concatenate([wi[0:1024], wi[1280:1792], wi[1792:2304], wi[2368:2880],
                          wi[1024:1152], wi[1152:1280], wi[2304:2368],
                          jnp.zeros((IN_PAD - 2880, d), wi.dtype)], axis=0)
    wq = g_uq.reshape(768, 512)
    wq = jnp.concatenate([wq[192 * h: 192 * h + 128] for h in range(4)]
                         + [wq[192 * h + 128: 192 * (h + 1)] for h in range(4)], axis=0)
    wmkv = g_mkv.reshape(-1, g_mkv.shape[-1])
    wo = g_out.reshape(-1, d)

    pos = positions[0].astype(jnp.float32)
    inv_freq = ROPE_THETA ** (-jnp.arange(0, MLA_ROPE, 2, dtype=jnp.float32) / MLA_ROPE)
    ang = pos[:, None] * inv_freq
    cos32, sin32 = jnp.cos(ang), jnp.sin(ang)
    cos_t = jnp.tile(cos32, (1, 4))
    sin_t = jnp.tile(jnp.concatenate([-sin32, sin32], axis=1), (1, 2))
    posc, posr = pos.reshape(s, 1), pos.reshape(1, s)
    two = lambda g: jnp.tile(g, (1, 2))
    gq2, gk2, gqr2, gkr2 = two(swa_q_norm_g), two(swa_k_norm_g), two(mla_qr_norm_g), two(mla_kr_norm_g)
    sinks1 = swa_sinks[0]

    proj, hn = _in_proj(x2, attn_norm_g, wi)
    qc, kc, vb, qb, kvb, cqn, ckvn = _mla_prep(proj, cos_t, sin_t, mla_cq_norm_g, mla_ckv_norm_g, wq, wkv,
                                                mla_qn_norm_g, gqr2, mla_kn_norm_g, gkr2)
    y_b, lse_b = _mla_fwd(qc, kc, vb)
    km, vmm, kvm, memn = _memkv_prep(mem2, mem_norm_g, wmkv, mem_k_norm_g)
    y_m, lse_m = _mem_fwd(proj, mem_q_norm_g, km, vmm)
    y_a, lse_a = _swa_fwd(proj, posc, posr, gq2, gk2, sinks1)
    h1, fn = _out_proj(y_a, y_b, y_m, x2, wo, ffn_norm_g)
    gu, act = _ffn_gu(fn, w_gu)
    dout, loss_tile = _ffn_down(act, w_d, h1, tgt)

    dgu, dw_d = _ffn_bwd_act(dout, w_d, gu)
    dw_gu = _ffn_dw_gu(fn, dgu)
    r_gu, r_d = _exchange_grads_background([dw_gu, dw_d], 2, "exchange_ffn_grads")
    d_h1, dg_ffn = _ffn_norm_bwd(_ffn_dfn(dgu, w_gu, dw_gu), dout, h1, ffn_norm_g)
    d_y = _mm(d_h1, wo, tb=True, out_dtype=jnp.float32, tm=FFN_TILE, tk=2048, name="d_mix")
    dw_out = jnp.concatenate([
        _mm(y_a, d_h1, ta=True, out_dtype=WIRE, tm=1024, tk=1024, name="dw_out_a"),
        _mm(y_b, d_h1, ta=True, out_dtype=WIRE, tm=1024, tk=1024, name="dw_out_b"),
        _mm(y_m, d_h1, ta=True, out_dtype=WIRE, tm=1024, tk=1024, name="dw_out_m")], axis=0)
    d_qm, dkm, dvmm, dg_mq = _mem_bwd(proj, mem_q_norm_g, km, vmm, d_y, y_m, lse_m)
    dw_mkv, dg_mem, dg_mk = _memkv_bwd(mem2, mem_norm_g, wmkv, mem_k_norm_g, kvm, memn, dkm, dvmm)
    r_mkv, r_out = _exchange_grads_background([dw_mkv.reshape(g_mkv.shape), dw_out.reshape(g_out.shape)], 3,
                                              "exchange_mix_grads")
    dqc, dkc, dvb = _mla_bwd(qc, kc, vb, d_y, y_b, lse_b, dw_mkv)
    (d_cq, d_ckv, d_kr, dw_uq, dw_ukv, dg_cq, dg_ckv, dg_qn, dg_qr, dg_kn, dg_kr) = _mla_prep_bwd(
        proj, cos_t, sin_t, mla_cq_norm_g, mla_ckv_norm_g, wq, wkv, mla_qn_norm_g, gqr2, mla_kn_norm_g, gkr2,
        qb, kvb, cqn, ckvn, dqc, dkc, dvb)
    d_qa, d_ka, d_va, dg_q, dg_k, d_sinks = _swa_bwd(proj, posc, posr, gq2, gk2, sinks1, d_y, y_a, lse_a, dw_out)
    d_proj = jnp.concatenate([d_qa, d_cq, d_ckv, d_qm, d_ka, d_va, d_kr], axis=1)
    gi = _dw_in(hn, d_proj, n_in)

    gq_ = jnp.concatenate(sum([[dw_uq[128 * h: 128 * (h + 1)], dw_uq[512 + 64 * h: 512 + 64 * (h + 1)]]
                               for h in range(4)], []), axis=0)
    gq_ = gq_.reshape(N_DEV, 96, 512)
    r_in, r_uq, r_ukv = _exchange_grads_background([gi, gq_, dw_ukv], 4, "exchange_in_grads")
    grad_x, dg_attn = _dx(d_proj, wi, x2, attn_norm_g, d_h1, gi)

    big = {}
    def adam(name, r, transposed=False, after=None, which=None):
        w, m, v = args[name][0], args["m_" + name][0], args["v_" + name][0]
        if transposed:
            outs = _adam_big(r, w.T, m.T, v.T, "adam_" + name, after, which)
            return [o.T[None] for o in outs]
        return [o[None] for o in _adam_big(r, w, m, v, "adam_" + name, after)]
    big["w_gate"] = adam("w_gate", r_gu, True, which=0)
    big["w_up"] = adam("w_up", r_gu, True, after=big["w_gate"][0], which=1)
    big["w_down"] = adam("w_down", r_d, after=big["w_up"][0])
    big["w_out"] = adam("w_out", r_out, after=big["w_down"][0])
    big["w_mem_kv"] = adam("w_mem_kv", r_mkv, after=big["w_out"][0])
    big["w_in"] = adam("w_in", r_in, True, after=big["w_mem_kv"][0])
    big["w_uq"] = adam("w_uq", r_uq, True, after=big["w_in"][0])
    big["w_ukv"] = adam("w_ukv", r_ukv, after=big["w_uq"][0])

    small_g = {
        "attn_norm_g": dg_attn, "swa_q_norm_g": dg_q, "swa_k_norm_g": dg_k,
        "swa_sinks": d_sinks, "mla_cq_norm_g": dg_cq, "mla_ckv_norm_g": dg_ckv, "mla_qn_norm_g": dg_qn,
        "mla_qr_norm_g": dg_qr, "mla_kn_norm_g": dg_kn, "mla_kr_norm_g": dg_kr,
        "mem_norm_g": dg_mem, "mem_q_norm_g": dg_mq, "mem_k_norm_g": dg_mk, "ffn_norm_g": dg_ffn}
    loss11, small_out = _small_allreduce_adam(
        [small_g[n] for n in _SMALL], loss_tile, [args[n] for n in _SMALL],
        [args["m_" + n] for n in _SMALL], [args["v_" + n] for n in _SMALL])
    small = dict(zip(_SMALL, small_out))
    loss = loss11.reshape(())

    order = ["attn_norm_g", "w_in", "swa_q_norm_g", "swa_k_norm_g", "swa_sinks", "mla_cq_norm_g", "mla_ckv_norm_g",
             "w_uq", "w_ukv", "mla_qn_norm_g", "mla_qr_norm_g", "mla_kn_norm_g", "mla_kr_norm_g", "mem_norm_g",
             "w_mem_kv", "mem_q_norm_g", "mem_k_norm_g", "w_out", "ffn_norm_g", "w_gate", "w_up", "w_down"]
    res = {n: (big[n] if n in big else list(small[n])) for n in order}
    outs = [loss, grad_x[None]]
    for kind in range(4):
        outs += [res[n][kind] for n in order]
    return tuple(outs)
```

```python
import jax
import jax.numpy as jnp
from jax import lax
from jax.experimental import pallas as pl
from jax.experimental.pallas import tpu as pltpu
from jax.experimental.pallas import tpu_sc as plsc

MXU = jnp.bfloat16
WIRE = jnp.bfloat16
EPS = 1e-6
NEG_INF = -1e30
LOG2E = 1.4426950408889634
N_DEV = 8
LANES = 128
ROW_TILE = 256
FFN_TILE = 512
ATT_TILE = 1024
SWA_BLOCK = 128
VMEM_LIMIT = 56 * 1024 * 1024

SWA_Q_HEADS, SWA_KV_HEADS, SWA_DIM = 16, 2, 64
MLA_HEADS, MLA_NOPE, MLA_ROPE, MLA_V = 4, 128, 64, 128
MEM_HEADS, MEM_DIM = 4, 128
ROPE_THETA = 10000.0
ADAM_LR, ADAM_B1, ADAM_B2, ADAM_EPS, ADAM_WD, ADAM_STEP = 0.001, 0.9, 0.999, 1e-08, 0.01, 10

C_QA, C_CQ, C_CKV, C_QM, C_KA, C_VA, C_KR, IN_PAD = 0, 1024, 1536, 2048, 2560, 2688, 2816, 2944


def _pcall(body, *, name, out_shape, in_specs, out_specs, grid=(), scratch=(), sem=None, after=None):
    params = pltpu.CompilerParams(dimension_semantics=sem, vmem_limit_bytes=VMEM_LIMIT)
    if after is not None:
        n_in, inner = len(in_specs), body

        def body(*refs):
            inner(*refs[:n_in], *refs[n_in + 1:])

        in_specs = list(in_specs) + [pl.BlockSpec(memory_space=pl.ANY)]
    call = pl.pallas_call(body, name=name, grid=grid, in_specs=in_specs, out_specs=out_specs,
                          out_shape=out_shape, scratch_shapes=list(scratch), compiler_params=params)
    return call if after is None else (lambda *ops: call(*ops, after))


def _sds(shape, dtype):
    return jax.ShapeDtypeStruct(tuple(shape), dtype)


def _dot(a, b):
    return jnp.dot(a.astype(MXU), b.astype(MXU), preferred_element_type=jnp.float32)


def _dot_nt(a, b):
    return lax.dot_general(a.astype(MXU), b.astype(MXU), (((1,), (1,)), ((), ())),
                           preferred_element_type=jnp.float32)


def _dot_tn(a, b):
    return lax.dot_general(a.astype(MXU), b.astype(MXU), (((0,), (0,)), ((), ())),
                           preferred_element_type=jnp.float32)


def _lo_mask(shape):
    return (lax.broadcasted_iota(jnp.int32, shape, len(shape) - 1) % LANES) < 64


def _norm_fwd(x, g, half=False):
    x2 = x * x
    if half:
        lo = _lo_mask(x.shape)
        s_lo = jnp.sum(jnp.where(lo, x2, 0.0), -1, keepdims=True)
        s_hi = jnp.sum(jnp.where(lo, 0.0, x2), -1, keepdims=True)
        r = jnp.where(lo, lax.rsqrt(s_lo / 64.0 + EPS), lax.rsqrt(s_hi / 64.0 + EPS))
    else:
        r = lax.rsqrt(jnp.mean(x2, -1, keepdims=True) + EPS)
    xn = x * r
    return xn * g, xn, r


def _norm_bwd(xn, r, g, dy, half=False):
    t = dy * g
    tx = t * xn
    if half:
        lo = _lo_mask(xn.shape)
        m_lo = jnp.sum(jnp.where(lo, tx, 0.0), -1, keepdims=True) / 64.0
        m_hi = jnp.sum(jnp.where(lo, 0.0, tx), -1, keepdims=True) / 64.0
        m = jnp.where(lo, m_lo, m_hi)
    else:
        m = jnp.mean(tx, -1, keepdims=True)
    dx = r * (t - xn * m)
    dg = jnp.sum(dy * xn, 0, keepdims=True)
    return dx, dg


def _swap32(x):
    lane = lax.broadcasted_iota(jnp.int32, x.shape, 1)
    return jnp.where((lane % 64) < 32, pltpu.roll(x, 96, 1), pltpu.roll(x, 32, 1))


def _rope(x, cos, sin):
    return x * cos + _swap32(x) * sin


def _rope_bwd(d, cos, sin):
    return d * cos + _swap32(d * sin)


def _my_coords():
    return lax.axis_index("x"), lax.axis_index("y"), lax.axis_index("c")


def _dev_index(px, py, pc):
    return 4 * px + 2 * py + pc


_FLIPS = [(0, 0, 1), (0, 1, 0), (0, 1, 1), (1, 0, 0), (1, 0, 1), (1, 1, 0), (1, 1, 1)]


def _flip(coords, f):
    return tuple((1 - v) if b else v for v, b in zip(coords, f))


def _all_gather(shards):
    n = len(shards)

    def body(*refs):
        ins, outs = refs[:n], refs[n:2 * n]
        send_sems, recv_sems, local_sems = refs[2 * n:]
        x, y, c = _my_coords()
        me, sibling = (x, y, c), (x, y, 1 - c)
        chips = [(1 - x, y), (x, 1 - y), (1 - x, 1 - y)]

        def copy(w, k, block, to, src=None):
            dst = outs[w].at[_dev_index(*block)]
            return pltpu.make_async_remote_copy(
                src_ref=dst if src is None else src, dst_ref=dst,
                send_sem=send_sems.at[w, k], recv_sem=recv_sems.at[w, k],
                device_id=to, device_id_type=pl.DeviceIdType.MESH)

        sends, locals_ = [], []
        for w in range(n):
            mine = pltpu.make_async_copy(ins[w], outs[w].at[_dev_index(*me)], local_sems.at[w])
            mine.start()
            locals_.append(mine)
            first = [copy(w, 0, me, sibling, src=ins[w])]
            first += [copy(w, 1 + j, me, (*chip, c), src=ins[w]) for j, chip in enumerate(chips)]
            for cp in first:
                cp.start()
            sends += first
        for w in range(n):
            for j, chip in enumerate(chips):
                copy(w, 1 + j, (*chip, c), me).wait_recv()
                fwd = copy(w, 4 + j, (*chip, c), sibling)
                fwd.start()
                sends.append(fwd)
        for w in range(n):
            copy(w, 0, sibling, me).wait_recv()
            for j, chip in enumerate(chips):
                copy(w, 4 + j, (*chip, 1 - c), me).wait_recv()
        for cp in sends:
            cp.wait_send()
        for mine in locals_:
            mine.wait()

    any_spec = pl.BlockSpec(memory_space=pl.ANY)
    return _pcall(
        body, name="all_gather_weights",
        out_shape=[_sds((N_DEV,) + s.shape, s.dtype) for s in shards],
        in_specs=[any_spec] * n, out_specs=[any_spec] * n,
        scratch=[pltpu.SemaphoreType.DMA((n, 7)), pltpu.SemaphoreType.DMA((n, 7)),
                 pltpu.SemaphoreType.DMA((n,))])(*shards)


def _wire_cost(arrays):
    nbytes = sum(a.size * a.dtype.itemsize for a in arrays)
    return pl.CostEstimate(flops=0, transcendentals=0, bytes_accessed=40 * nbytes)


def _all_gather_background(shards, collective_id, name):
    n = len(shards)
    src_refs = [jax.new_ref(s, memory_space=pltpu.MemorySpace.HBM) for s in shards]
    out_refs = [jax.empty_ref(_sds((N_DEV,) + s.shape, s.dtype), memory_space=pltpu.MemorySpace.HBM) for s in shards]

    @pl.kernel(mesh=plsc.ScalarSubcoreMesh(axis_name="seq", num_cores=1), name=name,
               scratch_types=(pltpu.SemaphoreType.DMA((n, 7)), pltpu.SemaphoreType.DMA((n, 7)),
                              pltpu.SemaphoreType.DMA((n,))),
               compiler_params=pltpu.CompilerParams(collective_id=collective_id))
    def launch(send_sems, recv_sems, local_sems):
        x, y, c = _my_coords()
        me, sibling = (x, y, c), (x, y, 1 - c)
        chips = [(1 - x, y), (x, 1 - y), (1 - x, 1 - y)]
        barrier = pltpu.get_barrier_semaphore()
        for peer in [sibling] + [(*chip, c) for chip in chips]:
            pl.semaphore_signal(barrier, inc=1, device_id=peer, device_id_type=pl.DeviceIdType.MESH)
        pl.semaphore_wait(barrier, 4)

        def copy(w, k, block, to, src=None):
            dst = out_refs[w].at[_dev_index(*block)]
            return pltpu.make_async_remote_copy(
                src_ref=dst if src is None else src, dst_ref=dst,
                send_sem=send_sems.at[w, k], recv_sem=recv_sems.at[w, k],
                device_id=to, device_id_type=pl.DeviceIdType.MESH)

        sends, locals_ = [], []
        for w in range(n):
            mine = pltpu.make_async_copy(src_refs[w], out_refs[w].at[_dev_index(*me)], local_sems.at[w])
            mine.start()
            locals_.append(mine)
            first = [copy(w, 0, me, sibling, src=src_refs[w])]
            first += [copy(w, 1 + j, me, (*chip, c), src=src_refs[w]) for j, chip in enumerate(chips)]
            for cp in first:
                cp.start()
            sends += first
        for w in range(n):
            for j, chip in enumerate(chips):
                copy(w, 1 + j, (*chip, c), me).wait_recv()
                fwd = copy(w, 4 + j, (*chip, c), sibling)
                fwd.start()
                sends.append(fwd)
        for w in range(n):
            copy(w, 0, sibling, me).wait_recv()
            for j, chip in enumerate(chips):
                copy(w, 4 + j, (*chip, 1 - c), me).wait_recv()
        for cp in sends:
            cp.wait_send()
        for mine in locals_:
            mine.wait()

    launch()
    return [r[...] for r in out_refs]


def _exchange_grads(grads):
    n = len(grads)

    def body(*refs):
        ins, outs = refs[:n], refs[n:2 * n]
        send_sems, recv_sems, local_sems = refs[2 * n:]
        me = _my_coords()
        my_idx = _dev_index(*me)
        sends, locals_ = [], []
        for w in range(n):
            mine = pltpu.make_async_copy(ins[w].at[my_idx], outs[w].at[my_idx], local_sems.at[w])
            mine.start()
            locals_.append(mine)
            for k, f in enumerate(_FLIPS):
                peer = _flip(me, f)
                cp = pltpu.make_async_remote_copy(
                    src_ref=ins[w].at[_dev_index(*peer)], dst_ref=outs[w].at[my_idx],
                    send_sem=send_sems.at[w, k], recv_sem=recv_sems.at[w, k],
                    device_id=peer, device_id_type=pl.DeviceIdType.MESH)
                cp.start()
                sends.append(cp)
        for w in range(n):
            for k, f in enumerate(_FLIPS):
                peer = _flip(me, f)
                slot = outs[w].at[_dev_index(*peer)]
                pltpu.make_async_remote_copy(
                    src_ref=slot, dst_ref=slot,
                    send_sem=send_sems.at[w, k], recv_sem=recv_sems.at[w, k],
                    device_id=peer, device_id_type=pl.DeviceIdType.MESH).wait_recv()
        for cp in sends:
            cp.wait_send()
        for mine in locals_:
            mine.wait()

    any_spec = pl.BlockSpec(memory_space=pl.ANY)
    return _pcall(
        body, name="exchange_grads",
        out_shape=[_sds(g.shape, g.dtype) for g in grads],
        in_specs=[any_spec] * n, out_specs=[any_spec] * n,
        scratch=[pltpu.SemaphoreType.DMA((n, 7)), pltpu.SemaphoreType.DMA((n, 7)),
                 pltpu.SemaphoreType.DMA((n,))])(*grads)


def _exchange_grads_background(grads, collective_id, name):
    n = len(grads)
    src_refs = [jax.new_ref(g, memory_space=pltpu.MemorySpace.HBM) for g in grads]
    out_refs = [jax.empty_ref(_sds(g.shape, g.dtype), memory_space=pltpu.MemorySpace.HBM) for g in grads]

    @pl.kernel(mesh=plsc.ScalarSubcoreMesh(axis_name="seq", num_cores=1), name=name,
               scratch_types=(pltpu.SemaphoreType.DMA((n, 7)), pltpu.SemaphoreType.DMA((n, 7)),
                              pltpu.SemaphoreType.DMA((n,))),
               cost_estimate=_wire_cost(grads),
               compiler_params=pltpu.CompilerParams(collective_id=collective_id))
    def launch(send_sems, recv_sems, local_sems):
        me = _my_coords()
        my_idx = _dev_index(*me)
        peers = [_flip(me, f) for f in _FLIPS]
        barrier = pltpu.get_barrier_semaphore()
        for peer in peers:
            pl.semaphore_signal(barrier, inc=1, device_id=peer, device_id_type=pl.DeviceIdType.MESH)
        pl.semaphore_wait(barrier, len(peers))
        sends, locals_ = [], []
        for w in range(n):
            mine = pltpu.make_async_copy(src_refs[w].at[my_idx], out_refs[w].at[my_idx], local_sems.at[w])
            mine.start()
            locals_.append(mine)
            for k, peer in enumerate(peers):
                cp = pltpu.make_async_remote_copy(
                    src_ref=src_refs[w].at[_dev_index(*peer)], dst_ref=out_refs[w].at[my_idx],
                    send_sem=send_sems.at[w, k], recv_sem=recv_sems.at[w, k],
                    device_id=peer, device_id_type=pl.DeviceIdType.MESH)
                cp.start()
                sends.append(cp)
        for w in range(n):
            for k, peer in enumerate(peers):
                slot = out_refs[w].at[_dev_index(*peer)]
                pltpu.make_async_remote_copy(
                    src_ref=slot, dst_ref=slot, send_sem=send_sems.at[w, k], recv_sem=recv_sems.at[w, k],
                    device_id=peer, device_id_type=pl.DeviceIdType.MESH).wait_recv()
        for cp in sends:
            cp.wait_send()
        for mine in locals_:
            mine.wait()

    launch()
    return [r[...] for r in out_refs]


def _to_wire(parts, after, name):
    n = len(parts)
    rows, cols = parts[0].shape
    tr = rows // 2 if rows % 32 == 0 else rows

    def body(*refs):
        for k in range(n):
            refs[n][k] = refs[k][...].astype(WIRE)

    blk = pl.BlockSpec((tr, cols), lambda i: (i, 0))
    return _pcall(
        body, name=name, grid=(rows // tr,), out_shape=_sds((n, rows, cols), WIRE),
        in_specs=[blk] * n, out_specs=pl.BlockSpec((n, tr, cols), lambda i: (0, i, 0)),
        sem=("parallel",), after=after)(*parts)


def _adam_math(w, g, m, v):
    m = ADAM_B1 * m + (1.0 - ADAM_B1) * g
    v = ADAM_B2 * v + (1.0 - ADAM_B2) * (g * g)
    m_hat = m / (1.0 - ADAM_B1 ** ADAM_STEP)
    v_hat = v / (1.0 - ADAM_B2 ** ADAM_STEP)
    delta = -ADAM_LR * (m_hat / (jnp.sqrt(v_hat) + ADAM_EPS) + ADAM_WD * w)
    return delta, m, v


def _small_allreduce_adam(grads, loss_tile, ws, ms, vs):
    sizes = [w.shape[-1] for w in ws]
    n_par = len(ws)
    row0, r = [], 0
    for n in sizes:
        row0.append(r)
        r += -(-n // LANES)
    loss_row = r
    rows = -(-(r + 1) // 8) * 8

    def pieces(n):
        return [(k, min(LANES, n - LANES * k)) for k in range(-(-n // LANES))]

    def body(*refs):
        g_refs = refs[:n_par]
        loss_in = refs[n_par]
        w_refs = refs[n_par + 1: 2 * n_par + 1]
        m_refs = refs[2 * n_par + 1: 3 * n_par + 1]
        v_refs = refs[3 * n_par + 1: 4 * n_par + 1]
        loss_out = refs[4 * n_par + 1]
        out_refs = refs[4 * n_par + 2: 8 * n_par + 2]
        pack, gath, res, send_sems, recv_sems = refs[8 * n_par + 2:]
        me = _my_coords()
        my_idx = _dev_index(*me)

        def fill(slot, srcs):
            pack[slot] = jnp.zeros((rows, LANES), jnp.float32)
            for p, n in enumerate(sizes):
                val = srcs[p][...]
                if val.shape[-1] == LANES and n == 64:
                    pack[slot, row0[p]:row0[p] + 1, :] = val + pltpu.roll(val, 64, 1)
                    continue
                for k, width in pieces(n):
                    pack[slot, row0[p] + k:row0[p] + k + 1, 0:width] = srcs[p][:, LANES * k:LANES * k + width]

        fill(0, g_refs)
        pack[0, loss_row:loss_row + 1, :] = loss_in[0:1, :]
        gath[my_idx] = pack[0]
        sends = []
        for k, f in enumerate(_FLIPS):
            peer = _flip(me, f)
            cp = pltpu.make_async_remote_copy(
                src_ref=pack.at[0], dst_ref=gath.at[my_idx],
                send_sem=send_sems.at[k], recv_sem=recv_sems.at[k],
                device_id=peer, device_id_type=pl.DeviceIdType.MESH)
            cp.start()
            sends.append(cp)
        fill(1, w_refs)
        fill(2, m_refs)
        fill(3, v_refs)
        for k, f in enumerate(_FLIPS):
            peer = _flip(me, f)
            slot = gath.at[_dev_index(*peer)]
            pltpu.make_async_remote_copy(
                src_ref=slot, dst_ref=slot, send_sem=send_sems.at[k], recv_sem=recv_sems.at[k],
                device_id=peer, device_id_type=pl.DeviceIdType.MESH).wait_recv()
        for cp in sends:
            cp.wait_send()
        g = gath[0]
        for d in range(1, N_DEV):
            g = g + gath[d]
        delta, m, v = _adam_math(pack[1], g, pack[2], pack[3])
        res[0], res[1], res[2], res[3] = g, delta, m, v
        loss_out[...] = res[0, loss_row:loss_row + 1, 0:1]
        for p, n in enumerate(sizes):
            for kind in range(4):
                for k, width in pieces(n):
                    out_refs[4 * p + kind][:, LANES * k:LANES * k + width] = (
                        res[kind, row0[p] + k:row0[p] + k + 1, 0:width])

    vm = pl.BlockSpec(memory_space=pltpu.VMEM)
    out_shape = [_sds((1, 1), jnp.float32)]
    for n in sizes:
        out_shape += [_sds((1, n), jnp.float32)] * 4
    outs = _pcall(
        body, name="small_allreduce_adam", out_shape=out_shape,
        in_specs=[vm] * (4 * n_par + 1), out_specs=[vm] * len(out_shape),
        scratch=[pltpu.VMEM((4, rows, LANES), jnp.float32), pltpu.VMEM((N_DEV, rows, LANES), jnp.float32),
                 pltpu.VMEM((4, rows, LANES), jnp.float32),
                 pltpu.SemaphoreType.DMA((7,)), pltpu.SemaphoreType.DMA((7,))])(*grads, loss_tile, *ws, *ms, *vs)
    return outs[0], [outs[1 + 4 * p: 5 + 4 * p] for p in range(n_par)]


def _adam_big(recv, w, m, v, name, after=None, which=None):
    rows, cols = recv.shape[-2:]
    row_tiles = [t for t in range(16, rows + 1, 16) if rows % t == 0 and t * cols <= 400 * 1024]
    tr, tc = (max(row_tiles), cols) if row_tiles else (rows, 512 if cols % 512 == 0 else cols)

    def body(r_ref, w_ref, m_ref, v_ref, g_ref, d_ref, mo_ref, vo_ref):
        g = r_ref[0].astype(jnp.float32)
        for d in range(1, N_DEV):
            g = g + r_ref[d].astype(jnp.float32)
        delta, mn, vn = _adam_math(w_ref[...], g, m_ref[...], v_ref[...])
        g_ref[...] = g
        d_ref[...] = delta
        mo_ref[...] = mn
        vo_ref[...] = vn

    blk = pl.BlockSpec((tr, tc), lambda i, j: (i, j))
    if which is None:
        r_spec = pl.BlockSpec((N_DEV, tr, tc), lambda i, j: (0, i, j))
    else:
        r_spec = pl.BlockSpec((N_DEV, None, tr, tc), lambda i, j: (0, which, i, j))
    return _pcall(
        body, name=name, grid=(rows // tr, cols // tc),
        out_shape=[_sds((rows, cols), jnp.float32)] * 4,
        in_specs=[r_spec, blk, blk, blk],
        out_specs=[blk] * 4, sem=("parallel", "parallel"), after=after)(recv, w, m, v)


def _mm(a, b, *, ta=False, tb=False, out_dtype, tm, tk, name):
    (kdim, mdim) = a.shape if ta else a.shape[::-1]
    ndim = b.shape[0] if tb else b.shape[1]
    tm, tk = min(tm, mdim), min(tk, kdim)
    nk = kdim // tk

    def body(a_ref, b_ref, o_ref, acc):
        k = pl.program_id(1)
        if ta:
            part = _dot_tn(a_ref[...], b_ref[...])
        elif tb:
            part = _dot_nt(a_ref[...], b_ref[...])
        else:
            part = _dot(a_ref[...], b_ref[...])

        @pl.when(k == 0)
        def _():
            acc[...] = part

        @pl.when(k > 0)
        def _():
            acc[...] += part

        @pl.when(k == nk - 1)
        def _():
            o_ref[...] = acc[...].astype(o_ref.dtype)

    a_spec = pl.BlockSpec((tk, tm), lambda i, k: (k, i)) if ta else pl.BlockSpec((tm, tk), lambda i, k: (i, k))
    b_spec = pl.BlockSpec((ndim, tk), lambda i, k: (0, k)) if tb else pl.BlockSpec((tk, ndim), lambda i, k: (k, 0))
    return _pcall(
        body, name=name, grid=(mdim // tm, nk), out_shape=_sds((mdim, ndim), out_dtype),
        in_specs=[a_spec, b_spec], out_specs=pl.BlockSpec((tm, ndim), lambda i, k: (i, 0)),
        scratch=[pltpu.VMEM((tm, ndim), jnp.float32)], sem=("parallel", "arbitrary"))(a, b)


def _ref_col_pieces(start, stop):
    ref_starts = [0, 1024, 1152, 1280, 1792, 2304, 2368, 2880]
    perm_starts = [C_QA, C_KA, C_VA, C_CQ, C_CKV, C_KR, C_QM]
    out = []
    for p in range(7):
        lo, hi = max(start, ref_starts[p]), min(stop, ref_starts[p + 1])
        if lo < hi:
            out.append((lo - start, perm_starts[p] + lo - ref_starts[p], hi - lo))
    return out


def _dw_in(hn, d_proj, n_shard):
    s, d = hn.shape
    n = d_proj.shape[1]
    tm, tk = min(512, d), min(1024, s)
    nk = s // tk

    def body(a_ref, b_ref, o_ref, acc):
        k = pl.program_id(1)
        part = _dot_tn(a_ref[...], b_ref[...])

        @pl.when(k == 0)
        def _():
            acc[...] = part

        @pl.when(k > 0)
        def _():
            acc[...] += part

        @pl.when(k == nk - 1)
        def _():
            t = acc[...].T
            for j in range(N_DEV):
                rows = [t[src:src + width] for _, src, width in _ref_col_pieces(j * n_shard, (j + 1) * n_shard)]
                o_ref[j] = jnp.concatenate(rows, axis=0).astype(o_ref.dtype)

    return _pcall(
        body, name="dw_in", grid=(d // tm, nk), out_shape=_sds((N_DEV, n_shard, d), WIRE),
        in_specs=[pl.BlockSpec((tk, tm), lambda i, k: (k, i)), pl.BlockSpec((tk, n), lambda i, k: (k, 0))],
        out_specs=pl.BlockSpec((N_DEV, n_shard, tm), lambda i, k: (0, 0, i)),
        scratch=[pltpu.VMEM((tm, n), jnp.float32)], sem=("parallel", "arbitrary"))(hn, d_proj)


def _in_proj(x, g, w):
    s, d = x.shape
    n = w.shape[0]
    tm = min(ROW_TILE, s)

    def body(x_ref, g_ref, w_ref, p_ref, hn_ref):
        hn, _, _ = _norm_fwd(x_ref[...], g_ref[...])
        hn_ref[...] = hn.astype(hn_ref.dtype)
        p_ref[...] = _dot_nt(hn, w_ref[...])

    return _pcall(
        body, name="in_proj", grid=(s // tm,),
        out_shape=[_sds((s, n), jnp.float32), _sds((s, d), MXU)],
        in_specs=[pl.BlockSpec((tm, d), lambda i: (i, 0)), pl.BlockSpec((1, d), lambda i: (0, 0)),
                  pl.BlockSpec((n, d), lambda i: (0, 0))],
        out_specs=[pl.BlockSpec((tm, n), lambda i: (i, 0)), pl.BlockSpec((tm, d), lambda i: (i, 0))],
        sem=("parallel",))(x, g, w)


def _mla_prep(proj, cos, sin, g_cq, g_ckv, w_uq, w_ukv, g_qn, g_qr, g_kn, g_kr):
    s = proj.shape[0]
    tm = min(ROW_TILE, s)
    nh = MLA_HEADS

    def body(cq_ref, ckv_ref, kr_ref, cos_ref, sin_ref, gcq_ref, gckv_ref, wuq_ref, wukv_ref,
             gqn_ref, gqr_ref, gkn_ref, gkr_ref,
             qc_ref, kc_ref, v_ref, qb_ref, kvb_ref, cqn_ref, ckvn_ref):
        cos_t, sin_t = cos_ref[...], sin_ref[...]
        lo = _lo_mask((tm, LANES))
        cqn, _, _ = _norm_fwd(cq_ref[...], gcq_ref[...])
        cqn_ref[...] = cqn.astype(cqn_ref.dtype)
        qb = _dot_nt(cqn, wuq_ref[...])
        qb_ref[...] = qb
        ckvn, _, _ = _norm_fwd(ckv_ref[...], gckv_ref[...])
        ckvn_ref[...] = ckvn.astype(ckvn_ref.dtype)
        kvb = jnp.concatenate([_dot(ckvn, wukv_ref[dev]) for dev in range(N_DEV)], axis=1)
        kvb_ref[...] = kvb
        kr, _, _ = _norm_fwd(kr_ref[...], gkr_ref[...], half=True)
        kr = _rope(kr, cos_t, sin_t)
        kr2 = jnp.where(lo, kr, pltpu.roll(kr, 64, 1))
        ropes = []
        for j in range(nh // 2):
            xr = qb[:, nh * MLA_NOPE + LANES * j: nh * MLA_NOPE + LANES * (j + 1)]
            qr, _, _ = _norm_fwd(xr, gqr_ref[...], half=True)
            ropes.append(_rope(qr, cos_t, sin_t))
        for h in range(nh):
            qn, _, _ = _norm_fwd(qb[:, MLA_NOPE * h: MLA_NOPE * (h + 1)], gqn_ref[...])
            mask = lo if h % 2 == 0 else jnp.logical_not(lo)
            qr = jnp.where(mask, ropes[h // 2], 0.0)
            qc_ref[h] = jnp.concatenate([qn, qr], axis=1).astype(qc_ref.dtype)
            kn, _, _ = _norm_fwd(kvb[:, 256 * h: 256 * h + MLA_NOPE], gkn_ref[...])
            kc_ref[h] = jnp.concatenate([kn, kr2], axis=1).astype(kc_ref.dtype)
            v_ref[h] = kvb[:, 256 * h + MLA_NOPE: 256 * (h + 1)].astype(v_ref.dtype)

    def col(width, start):
        return pl.BlockSpec((tm, width), lambda i: (i, start // width))

    def full(shape):
        return pl.BlockSpec(shape, lambda i: (0,) * len(shape))

    def row(width):
        return pl.BlockSpec((tm, width), lambda i: (i, 0))

    def heads(width):
        return pl.BlockSpec((nh, tm, width), lambda i: (0, i, 0))

    return _pcall(
        body, name="mla_prep", grid=(s // tm,),
        out_shape=[_sds((nh, s, 256), MXU), _sds((nh, s, 256), MXU), _sds((nh, s, MLA_V), MXU),
                   _sds((s, 768), jnp.float32), _sds((s, 1024), jnp.float32),
                   _sds((s, 512), MXU), _sds((s, 512), MXU)],
        in_specs=[col(512, C_CQ), col(512, C_CKV), col(LANES, C_KR), row(LANES), row(LANES),
                  full((1, 512)), full((1, 512)), full((768, 512)), full((N_DEV, 512, LANES)),
                  full((1, LANES)), full((1, LANES)), full((1, LANES)), full((1, LANES))],
        out_specs=[heads(256), heads(256), heads(MLA_V), row(768), row(1024), row(512), row(512)],
        sem=("parallel",))(proj, proj, proj, cos, sin, g_cq, g_ckv, w_uq, w_ukv, g_qn, g_qr, g_kn, g_kr)


def _mla_fwd(qc, kc, v):
    nh, s, _ = qc.shape
    t = min(ATT_TILE, s)
    nb = s // t
    scale = (MLA_NOPE + MLA_ROPE) ** -0.5

    def body(q_ref, k_ref, v_ref, y_ref, lse_ref, m_sc, l_sc, acc):
        qi, ki = pl.program_id(1), pl.program_id(2)

        @pl.when(ki == 0)
        def _():
            m_sc[...] = jnp.full_like(m_sc, NEG_INF)
            l_sc[...] = jnp.zeros_like(l_sc)
            acc[...] = jnp.zeros_like(acc)

        def step(diagonal):
            sc = _dot_nt(q_ref[0], k_ref[0]) * (scale * LOG2E)
            if diagonal:
                r_i = lax.broadcasted_iota(jnp.int32, sc.shape, 0)
                c_i = lax.broadcasted_iota(jnp.int32, sc.shape, 1)
                sc = jnp.where(c_i <= r_i, sc, NEG_INF)
            m_new = jnp.maximum(m_sc[...], jnp.max(sc, -1, keepdims=True))
            alpha = jnp.exp2(m_sc[...] - m_new)
            p = jnp.exp2(sc - m_new)
            l_sc[...] = alpha * l_sc[...] + jnp.sum(p, -1, keepdims=True)
            acc[...] = alpha * acc[...] + _dot(p, v_ref[0])
            m_sc[...] = m_new

        @pl.when(ki < qi)
        def _():
            step(False)

        @pl.when(ki == qi)
        def _():
            step(True)

        @pl.when(ki == qi)
        def _():
            y_ref[...] = acc[...] / l_sc[...]
            lse_ref[0] = m_sc[...] + jnp.log2(l_sc[...])

    return _pcall(
        body, name="mla_fwd", grid=(nh, nb, nb),
        out_shape=[_sds((s, nh * MLA_V), jnp.float32), _sds((nh, s, 1), jnp.float32)],
        in_specs=[pl.BlockSpec((1, t, 256), lambda h, i, k: (h, i, 0)),
                  pl.BlockSpec((1, t, 256), lambda h, i, k: (h, jnp.minimum(k, i), 0)),
                  pl.BlockSpec((1, t, MLA_V), lambda h, i, k: (h, jnp.minimum(k, i), 0))],
        out_specs=[pl.BlockSpec((t, MLA_V), lambda h, i, k: (i, h)),
                   pl.BlockSpec((1, t, 1), lambda h, i, k: (h, i, 0))],
        scratch=[pltpu.VMEM((t, 1), jnp.float32), pltpu.VMEM((t, 1), jnp.float32),
                 pltpu.VMEM((t, MLA_V), jnp.float32)],
        sem=("parallel", "parallel", "arbitrary"))(qc, kc, v)


def _memkv_prep(mem, g_mem, w_mkv, g_mk):
    ml, d = mem.shape
    hw = MEM_HEADS * MEM_DIM

    def body(mem_ref, g_ref, w_ref, gk_ref, k_ref, v_ref, kv_ref, mn_ref):
        mn, _, _ = _norm_fwd(mem_ref[...], g_ref[...])
        mn_ref[...] = mn.astype(mn_ref.dtype)
        kv = _dot(mn, w_ref[...])
        kv_ref[...] = kv
        for h in range(MEM_HEADS):
            kn, _, _ = _norm_fwd(kv[:, MEM_DIM * h: MEM_DIM * (h + 1)], gk_ref[...])
            k_ref[:, MEM_DIM * h: MEM_DIM * (h + 1)] = kn.astype(k_ref.dtype)
        v_ref[...] = kv[:, hw:].astype(v_ref.dtype)

    vm = pl.BlockSpec(memory_space=pltpu.VMEM)
    return _pcall(
        body, name="memkv_prep",
        out_shape=[_sds((ml, hw), MXU), _sds((ml, hw), MXU), _sds((ml, 2 * hw), jnp.float32), _sds((ml, d), MXU)],
        in_specs=[vm] * 4, out_specs=[vm] * 4)(mem, g_mem, w_mkv, g_mk)


def _mem_fwd(proj, g_mq, km, vmm):
    s = proj.shape[0]
    ml, hw = km.shape
    tm = min(FFN_TILE, s)
    scale = MEM_DIM ** -0.5

    def body(q_ref, g_ref, k_ref, v_ref, y_ref, lse_ref):
        col = lax.broadcasted_iota(jnp.int32, (tm, MEM_HEADS), 1)
        lse_t = jnp.zeros((tm, MEM_HEADS), jnp.float32)
        for h in range(MEM_HEADS):
            sl = slice(MEM_DIM * h, MEM_DIM * (h + 1))
            qn, _, _ = _norm_fwd(q_ref[:, sl], g_ref[...])
            sc = _dot_nt(qn, k_ref[:, sl]) * scale
            m = jnp.max(sc, -1, keepdims=True)
            p = jnp.exp(sc - m)
            l = jnp.sum(p, -1, keepdims=True)
            y_ref[:, sl] = _dot(p, v_ref[:, sl]) / l
            lse_t = jnp.where(col == h, m + jnp.log(l), lse_t)
        lse_ref[...] = lse_t

    return _pcall(
        body, name="mem_fwd", grid=(s // tm,),
        out_shape=[_sds((s, hw), jnp.float32), _sds((s, MEM_HEADS), jnp.float32)],
        in_specs=[pl.BlockSpec((tm, hw), lambda i: (i, C_QM // hw)), pl.BlockSpec((1, MEM_DIM), lambda i: (0, 0)),
                  pl.BlockSpec((ml, hw), lambda i: (0, 0)), pl.BlockSpec((ml, hw), lambda i: (0, 0))],
        out_specs=[pl.BlockSpec((tm, hw), lambda i: (i, 0)), pl.BlockSpec((tm, MEM_HEADS), lambda i: (i, 0))],
        sem=("parallel",))(proj, g_mq, km, vmm)


def _alibi_slope(h):
    return float(2.0 ** (-8.0 * (h + 1) / SWA_Q_HEADS))


def _swa_common(n, kp, kc, vp, vc, pq, pkp, pkc, gk):
    b = SWA_BLOCK
    k_raw = jnp.concatenate([kp, kc], axis=0)
    kn, kxn, kr = _norm_fwd(k_raw, gk, half=True)
    v = jnp.concatenate([vp, vc], axis=0)
    dist = jnp.abs(pq - jnp.concatenate([pkp, pkc], axis=1))
    r_i = lax.broadcasted_iota(jnp.int32, (b, 2 * b), 0)
    c_i = lax.broadcasted_iota(jnp.int32, (b, 2 * b), 1)
    valid = (c_i > r_i) & (c_i <= r_i + b) & (c_i >= jnp.where(n > 0, 0, b))
    bias = jnp.where(valid, -dist, NEG_INF)
    return kn, v, bias


def _swa_specs(s):
    b = SWA_BLOCK
    prev = lambda n: jnp.maximum(n - 1, 0)
    return [
        pl.BlockSpec((b, 1024), lambda n: (n, C_QA // 1024)),
        pl.BlockSpec((b, LANES), lambda n: (prev(n), C_KA // LANES)),
        pl.BlockSpec((b, LANES), lambda n: (n, C_KA // LANES)),
        pl.BlockSpec((b, LANES), lambda n: (prev(n), C_VA // LANES)),
        pl.BlockSpec((b, LANES), lambda n: (n, C_VA // LANES)),
        pl.BlockSpec((b, 1), lambda n: (n, 0)),
        pl.BlockSpec((1, b), lambda n: (0, prev(n))),
        pl.BlockSpec((1, b), lambda n: (0, n)),
        pl.BlockSpec((1, LANES), lambda n: (0, 0)),
        pl.BlockSpec((1, LANES), lambda n: (0, 0)),
        pl.BlockSpec(memory_space=pltpu.SMEM),
    ]


def _swa_fwd(proj, posc, posr, gq, gk, sinks):
    s = proj.shape[0]
    b = SWA_BLOCK
    scale = SWA_DIM ** -0.5

    def body(q_ref, kp_ref, kc_ref, vp_ref, vc_ref, pq_ref, pkp_ref, pkc_ref, gq_ref, gk_ref, sink_ref,
             y_ref, lse_ref):
        n = pl.program_id(0)
        kn, v, bias = _swa_common(n, kp_ref[...], kc_ref[...], vp_ref[...], vc_ref[...],
                                  pq_ref[...], pkp_ref[...], pkc_ref[...], gk_ref[...])
        lo = _lo_mask((b, LANES))
        col = lax.broadcasted_iota(jnp.int32, (b, SWA_Q_HEADS), 1)
        lse_t = jnp.zeros((b, SWA_Q_HEADS), jnp.float32)
        for j in range(SWA_Q_HEADS // 2):
            hk = (2 * j) // (SWA_Q_HEADS // SWA_KV_HEADS)
            kvmask = lo if hk == 0 else jnp.logical_not(lo)
            qn, _, _ = _norm_fwd(q_ref[:, LANES * j: LANES * (j + 1)], gq_ref[...], half=True)
            qn = qn * scale
            qsw = pltpu.roll(qn, 64, 1)
            outs = []
            for e in range(2):
                h = 2 * j + e
                qm = jnp.where(kvmask, qn if e == hk else qsw, 0.0)
                sc = _dot_nt(qm, kn) + _alibi_slope(h) * bias
                sk = sink_ref[h]
                m = jnp.maximum(jnp.max(sc, -1, keepdims=True), sk)
                p = jnp.exp(sc - m)
                l = jnp.sum(p, -1, keepdims=True) + jnp.exp(sk - m)
                o = _dot(p, v) / l
                outs.append(o if e == hk else pltpu.roll(o, 64, 1))
                lse_t = jnp.where(col == h, m + jnp.log(l), lse_t)
            y_ref[:, LANES * j: LANES * (j + 1)] = jnp.where(lo, outs[0], outs[1])
        lse_ref[...] = lse_t

    return _pcall(
        body, name="swa_fwd", grid=(s // b,),
        out_shape=[_sds((s, 1024), jnp.float32), _sds((s, SWA_Q_HEADS), jnp.float32)],
        in_specs=_swa_specs(s),
        out_specs=[pl.BlockSpec((b, 1024), lambda n: (n, 0)), pl.BlockSpec((b, SWA_Q_HEADS), lambda n: (n, 0))],
        sem=("parallel",))(proj, proj, proj, proj, proj, posc, posr, posr, gq, gk, sinks)


def _out_proj(y_a, y_b, y_m, x, w_out, g_ffn):
    s, d = x.shape
    tm = min(ROW_TILE, s)

    def body(ya_ref, yb_ref, ym_ref, x_ref, w_ref, g_ref, h1_ref, fn_ref):
        y = jnp.concatenate([ya_ref[...].astype(MXU), yb_ref[...].astype(MXU), ym_ref[...].astype(MXU)], axis=1)
        h1 = x_ref[...] + _dot(y, w_ref[...])
        h1_ref[...] = h1
        fn, _, _ = _norm_fwd(h1, g_ref[...])
        fn_ref[...] = fn.astype(fn_ref.dtype)

    def row(width):
        return pl.BlockSpec((tm, width), lambda i: (i, 0))

    return _pcall(
        body, name="out_proj", grid=(s // tm,),
        out_shape=[_sds((s, d), jnp.float32), _sds((s, d), MXU)],
        in_specs=[row(1024), row(512), row(512), row(d), pl.BlockSpec(w_out.shape, lambda i: (0, 0)),
                  pl.BlockSpec((1, d), lambda i: (0, 0))],
        out_specs=[row(d), row(d)], sem=("parallel",))(y_a, y_b, y_m, x, w_out, g_ffn)


def _ffn_gu(fn, w_gu):
    s, d = fn.shape
    f = w_gu.shape[2]
    tm = min(FFN_TILE, s)

    def body(fn_ref, w_ref, gu_ref, act_ref):
        x = fn_ref[...]
        g = _dot_nt(x, w_ref[0, 0])
        u = _dot_nt(x, w_ref[0, 1])
        gu_ref[0, 0] = g
        gu_ref[0, 1] = u
        act_ref[0] = (g * jax.nn.sigmoid(g) * u).astype(act_ref.dtype)

    return _pcall(
        body, name="ffn_gate_up", grid=(N_DEV, s // tm),
        out_shape=[_sds((N_DEV, 2, s, f), jnp.float32), _sds((N_DEV, s, f), MXU)],
        in_specs=[pl.BlockSpec((tm, d), lambda j, i: (i, 0)),
                  pl.BlockSpec((1, 2, f, d), lambda j, i: (j, 0, 0, 0))],
        out_specs=[pl.BlockSpec((1, 2, tm, f), lambda j, i: (j, 0, i, 0)),
                   pl.BlockSpec((1, tm, f), lambda j, i: (j, i, 0))],
        sem=("parallel", "parallel"))(fn, w_gu)


def _ffn_down(act, w_d, h1, target):
    _, s, f = act.shape
    d = h1.shape[1]
    tm = min(FFN_TILE, s)

    def body(a_ref, w_ref, h1_ref, t_ref, dout_ref, loss_ref, acc):
        i, j = pl.program_id(0), pl.program_id(1)
        part = _dot(a_ref[0], w_ref[0]) + _dot(a_ref[1], w_ref[1])

        @pl.when(j == 0)
        def _():
            acc[...] = h1_ref[...] + part

        @pl.when(j > 0)
        def _():
            acc[...] += part

        @pl.when((i == 0) & (j == 0))
        def _():
            loss_ref[...] = jnp.zeros_like(loss_ref)

        @pl.when(j == N_DEV // 2 - 1)
        def _():
            diff = acc[...] - t_ref[...]
            dout_ref[...] = diff / d
            loss_ref[...] += 0.5 * jnp.sum(jnp.sum(diff * diff, -1, keepdims=True) / d)

    row = pl.BlockSpec((tm, d), lambda i, j: (i, 0))
    return _pcall(
        body, name="ffn_down", grid=(s // tm, N_DEV // 2),
        out_shape=[_sds((s, d), jnp.float32), _sds((8, LANES), jnp.float32)],
        in_specs=[pl.BlockSpec((2, tm, f), lambda i, j: (j, i, 0)), pl.BlockSpec((2, f, d), lambda i, j: (j, 0, 0)),
                  row, row],
        out_specs=[row, pl.BlockSpec((8, LANES), lambda i, j: (0, 0))],
        scratch=[pltpu.VMEM((tm, d), jnp.float32)], sem=("arbitrary", "arbitrary"))(act, w_d, h1, target)


def _ffn_bwd_act(dout, w_d, gu):
    s, d = dout.shape
    f = w_d.shape[1]
    tm = min(FFN_TILE, s)

    def body(do_ref, w_ref, gu_ref, dgu_ref):
        d_act = _dot_nt(do_ref[...], w_ref[0])
        g, u = gu_ref[0, 0], gu_ref[0, 1]
        sig = jax.nn.sigmoid(g)
        dgu_ref[0, 0] = (d_act * u * (sig * (1.0 + g * (1.0 - sig)))).astype(dgu_ref.dtype)
        dgu_ref[0, 1] = (d_act * (g * sig)).astype(dgu_ref.dtype)

    return _pcall(
        body, name="ffn_bwd_act", grid=(N_DEV, s // tm),
        out_shape=_sds((N_DEV, 2, s, f), MXU),
        in_specs=[pl.BlockSpec((tm, d), lambda j, i: (i, 0)), pl.BlockSpec((1, f, d), lambda j, i: (j, 0, 0)),
                  pl.BlockSpec((1, 2, tm, f), lambda j, i: (j, 0, i, 0))],
        out_specs=pl.BlockSpec((1, 2, tm, f), lambda j, i: (j, 0, i, 0)),
        sem=("parallel", "parallel"))(dout, w_d, gu)


def _ffn_dw_down(act, dout):
    _, s, f = act.shape
    d = dout.shape[1]
    tk = min(2 * FFN_TILE, s)
    nk = s // tk

    def body(a_ref, do_ref, dw_ref, acc):
        k = pl.program_id(1)
        part = _dot_tn(a_ref[0], do_ref[...])

        @pl.when(k == 0)
        def _():
            acc[...] = part

        @pl.when(k > 0)
        def _():
            acc[...] += part

        @pl.when(k == nk - 1)
        def _():
            dw_ref[0] = acc[...].astype(dw_ref.dtype)

    return _pcall(
        body, name="ffn_dw_down", grid=(N_DEV, nk),
        out_shape=_sds((N_DEV, f, d), WIRE),
        in_specs=[pl.BlockSpec((1, tk, f), lambda j, k: (j, k, 0)), pl.BlockSpec((tk, d), lambda j, k: (k, 0))],
        out_specs=pl.BlockSpec((1, f, d), lambda j, k: (j, 0, 0)),
        scratch=[pltpu.VMEM((f, d), jnp.float32)], sem=("parallel", "arbitrary"))(act, dout)


def _ffn_dw_gu(fn, dgu):
    s, d = fn.shape
    f = dgu.shape[-1]
    tk = min(2 * FFN_TILE, s)
    nk = s // tk

    def body(fn_ref, dgu_ref, dw_ref, acc):
        k = pl.program_id(1)
        x = fn_ref[...]
        pg = _dot_tn(dgu_ref[0, 0], x)
        pu = _dot_tn(dgu_ref[0, 1], x)

        @pl.when(k == 0)
        def _():
            acc[0] = pg
            acc[1] = pu

        @pl.when(k > 0)
        def _():
            acc[0] += pg
            acc[1] += pu

        @pl.when(k == nk - 1)
        def _():
            dw_ref[0] = acc[...].astype(dw_ref.dtype)

    return _pcall(
        body, name="ffn_dw_gate_up", grid=(N_DEV, nk),
        out_shape=_sds((N_DEV, 2, f, d), WIRE),
        in_specs=[pl.BlockSpec((tk, d), lambda j, k: (k, 0)), pl.BlockSpec((1, 2, tk, f), lambda j, k: (j, 0, k, 0))],
        out_specs=pl.BlockSpec((1, 2, f, d), lambda j, k: (j, 0, 0, 0)),
        scratch=[pltpu.VMEM((2, f, d), jnp.float32)], sem=("parallel", "arbitrary"))(fn, dgu)


def _ffn_dfn(dgu, w_gu, after):
    _, _, s, f = dgu.shape
    d = w_gu.shape[3]
    tm = min(FFN_TILE, s)

    def body(dgu_ref, w_ref, dfn_ref):
        j = pl.program_id(1)
        part = (_dot(dgu_ref[0, 0], w_ref[0, 0]) + _dot(dgu_ref[0, 1], w_ref[0, 1])
                + _dot(dgu_ref[1, 0], w_ref[1, 0]) + _dot(dgu_ref[1, 1], w_ref[1, 1]))

        @pl.when(j == 0)
        def _():
            dfn_ref[...] = part

        @pl.when(j > 0)
        def _():
            dfn_ref[...] += part

    return _pcall(
        body, name="ffn_dfn", grid=(s // tm, N_DEV // 2),
        out_shape=_sds((s, d), jnp.float32),
        in_specs=[pl.BlockSpec((2, 2, tm, f), lambda i, j: (j, 0, i, 0)),
                  pl.BlockSpec((2, 2, f, d), lambda i, j: (j, 0, 0, 0))],
        out_specs=pl.BlockSpec((tm, d), lambda i, j: (i, 0)),
        sem=("parallel", "arbitrary"), after=after)(dgu, w_gu)


def _ffn_norm_bwd(d_fn, dout, h1, g_ffn):
    s, d = h1.shape
    tm = min(ROW_TILE, s)

    def body(dfn_ref, do_ref, h1_ref, g_ref, dh1_ref, dg_ref):
        i = pl.program_id(0)

        @pl.when(i == 0)
        def _():
            dg_ref[...] = jnp.zeros_like(dg_ref)

        _, xn, r = _norm_fwd(h1_ref[...], g_ref[...])
        dx, dg = _norm_bwd(xn, r, g_ref[...], dfn_ref[...])
        dh1_ref[...] = do_ref[...] + dx
        dg_ref[...] += dg

    row = pl.BlockSpec((tm, d), lambda i: (i, 0))
    vec = pl.BlockSpec((1, d), lambda i: (0, 0))
    return _pcall(
        body, name="ffn_norm_bwd", grid=(s // tm,),
        out_shape=[_sds((s, d), jnp.float32), _sds((1, d), jnp.float32)],
        in_specs=[row, row, row, vec], out_specs=[row, vec], sem=("arbitrary",))(d_fn, dout, h1, g_ffn)


def _mem_bwd(proj, g_mq, km, vmm, d_y, y_m, lse):
    s = proj.shape[0]
    ml, hw = km.shape
    tm = min(FFN_TILE, s)
    scale = MEM_DIM ** -0.5

    def body(q_ref, g_ref, k_ref, v_ref, do_ref, y_ref, lse_ref, dq_ref, dk_ref, dv_ref, dg_ref):
        i = pl.program_id(0)

        @pl.when(i == 0)
        def _():
            dk_ref[...] = jnp.zeros_like(dk_ref)
            dv_ref[...] = jnp.zeros_like(dv_ref)
            dg_ref[...] = jnp.zeros_like(dg_ref)

        col = lax.broadcasted_iota(jnp.int32, (tm, MEM_HEADS), 1)
        lse_t = lse_ref[...]
        for h in range(MEM_HEADS):
            sl = slice(MEM_DIM * h, MEM_DIM * (h + 1))
            qn, xn, r = _norm_fwd(q_ref[:, sl], g_ref[...])
            lse_h = jnp.sum(jnp.where(col == h, lse_t, 0.0), -1, keepdims=True)
            p = jnp.exp(_dot_nt(qn, k_ref[:, sl]) * scale - lse_h)
            do = do_ref[:, sl]
            dd = jnp.sum(do * y_ref[:, sl], -1, keepdims=True)
            dp = _dot_nt(do, v_ref[:, sl])
            ds = (p * (dp - dd)).astype(MXU)
            dv_ref[:, sl] += _dot_tn(p, do)
            dk_ref[:, sl] += _dot_tn(ds, qn) * scale
            dx, dg = _norm_bwd(xn, r, g_ref[...], _dot(ds, k_ref[:, sl]) * scale)
            dq_ref[:, sl] = dx.astype(dq_ref.dtype)
            dg_ref[...] += dg

    full = pl.BlockSpec((ml, hw), lambda i: (0, 0))
    return _pcall(
        body, name="mem_bwd", grid=(s // tm,),
        out_shape=[_sds((s, hw), MXU), _sds((ml, hw), jnp.float32), _sds((ml, hw), jnp.float32),
                   _sds((1, MEM_DIM), jnp.float32)],
        in_specs=[pl.BlockSpec((tm, hw), lambda i: (i, C_QM // hw)), pl.BlockSpec((1, MEM_DIM), lambda i: (0, 0)),
                  full, full, pl.BlockSpec((tm, hw), lambda i: (i, 3)), pl.BlockSpec((tm, hw), lambda i: (i, 0)),
                  pl.BlockSpec((tm, MEM_HEADS), lambda i: (i, 0))],
        out_specs=[pl.BlockSpec((tm, hw), lambda i: (i, 0)), full, full,
                   pl.BlockSpec((1, MEM_DIM), lambda i: (0, 0))],
        sem=("arbitrary",))(proj, g_mq, km, vmm, d_y, y_m, lse)


def _memkv_bwd(mem, g_mem, w_mkv, g_mk, kv, memn, dk, dv):
    ml, d = mem.shape
    hw = MEM_HEADS * MEM_DIM

    def body(mem_ref, g_ref, w_ref, gk_ref, kv_ref, mn_ref, dk_ref, dv_ref, dw_ref, dgm_ref, dgk_ref):
        parts = []
        dgk = jnp.zeros((1, MEM_DIM), jnp.float32)
        for h in range(MEM_HEADS):
            sl = slice(MEM_DIM * h, MEM_DIM * (h + 1))
            _, xn, r = _norm_fwd(kv_ref[:, sl], gk_ref[...])
            dx, dg = _norm_bwd(xn, r, gk_ref[...], dk_ref[:, sl])
            parts.append(dx)
            dgk = dgk + dg
        dkv = jnp.concatenate(parts + [dv_ref[...]], axis=1).astype(MXU)
        dgk_ref[...] = dgk
        dw_ref[...] = _dot_tn(mn_ref[...], dkv).astype(dw_ref.dtype)
        d_mn = _dot_nt(dkv, w_ref[...])
        _, xn, _ = _norm_fwd(mem_ref[...], g_ref[...])
        dgm_ref[...] = jnp.sum(d_mn * xn, 0, keepdims=True)

    vm = pl.BlockSpec(memory_space=pltpu.VMEM)
    return _pcall(
        body, name="memkv_bwd",
        out_shape=[_sds((d, 2 * hw), WIRE), _sds((1, d), jnp.float32), _sds((1, MEM_DIM), jnp.float32)],
        in_specs=[vm] * 8, out_specs=[vm] * 3)(mem, g_mem, w_mkv, g_mk, kv, memn, dk, dv)


def _mla_bwd(qc, kc, v, d_y, y_b, lse, after):
    nh, s, _ = qc.shape
    t = min(ATT_TILE, s)
    nb = s // t
    scale = (MLA_NOPE + MLA_ROPE) ** -0.5

    def body(q_ref, k_ref, v_ref, do_ref, y_ref, lse_ref, dq_ref, dk_ref, dv_ref, dk_acc, dv_acc):
        kj, qi = pl.program_id(1), pl.program_id(2)

        @pl.when((kj == 0) & (qi == 0))
        def _():
            dq_ref[...] = jnp.zeros_like(dq_ref)

        @pl.when(qi == kj)
        def _():
            dk_acc[...] = jnp.zeros_like(dk_acc)
            dv_acc[...] = jnp.zeros_like(dv_acc)

        def step(diagonal):
            q, k = q_ref[0], k_ref[0]
            sc = _dot_nt(q, k) * (scale * LOG2E)
            if diagonal:
                r_i = lax.broadcasted_iota(jnp.int32, sc.shape, 0)
                c_i = lax.broadcasted_iota(jnp.int32, sc.shape, 1)
                sc = jnp.where(c_i <= r_i, sc, NEG_INF)
            p = jnp.exp2(sc - lse_ref[0])
            do = do_ref[...]
            dd = jnp.sum(do * y_ref[...], -1, keepdims=True)
            dp = _dot_nt(do, v_ref[0])
            ds = (p * (dp - dd) * scale).astype(MXU)
            dv_acc[...] += _dot_tn(p, do)
            dk_acc[...] += _dot_tn(ds, q)
            rows = pl.ds(pl.multiple_of(qi * t, t), t)
            dq_ref[0, rows, :] += _dot(ds, k)

        @pl.when(qi > kj)
        def _():
            step(False)

        @pl.when(qi == kj)
        def _():
            step(True)

        @pl.when(qi == nb - 1)
        def _():
            dk_ref[0] = dk_acc[...]
            dv_ref[0] = dv_acc[...]

    qmap = lambda h, j, i: (h, jnp.maximum(i, j), 0)
    return _pcall(
        body, name="mla_bwd", grid=(nh, nb, nb),
        out_shape=[_sds((nh, s, 256), jnp.float32), _sds((nh, s, 256), jnp.float32),
                   _sds((nh, s, MLA_V), jnp.float32)],
        in_specs=[pl.BlockSpec((1, t, 256), qmap),
                  pl.BlockSpec((1, t, 256), lambda h, j, i: (h, j, 0)),
                  pl.BlockSpec((1, t, MLA_V), lambda h, j, i: (h, j, 0)),
                  pl.BlockSpec((t, MLA_V), lambda h, j, i: (jnp.maximum(i, j), 8 + h)),
                  pl.BlockSpec((t, MLA_V), lambda h, j, i: (jnp.maximum(i, j), h)),
                  pl.BlockSpec((1, t, 1), qmap)],
        out_specs=[pl.BlockSpec((1, s, 256), lambda h, j, i: (h, 0, 0)),
                   pl.BlockSpec((1, t, 256), lambda h, j, i: (h, j, 0)),
                   pl.BlockSpec((1, t, MLA_V), lambda h, j, i: (h, j, 0))],
        scratch=[pltpu.VMEM((t, 256), jnp.float32), pltpu.VMEM((t, MLA_V), jnp.float32)],
        sem=("parallel", "arbitrary", "arbitrary"), after=after)(qc, kc, v, d_y, y_b, lse)


def _mla_prep_bwd(proj, cos, sin, g_cq, g_ckv, w_uq, w_ukv, g_qn, g_qr, g_kn, g_kr,
                  qb, kvb, cqn, ckvn, dqc, dkc, dv):
    s = proj.shape[0]
    tm = min(ROW_TILE, s)
    nh = MLA_HEADS
    ni = s // tm

    def body(cq_ref, ckv_ref, kr_ref, cos_ref, sin_ref, gcq_ref, gckv_ref, wuq_ref, wukv_ref,
             gqn_ref, gqr_ref, gkn_ref, gkr_ref, qb_ref, kvb_ref, cqn_ref, ckvn_ref, dqc_ref, dkc_ref, dv_ref,
             dcq_ref, dckv_ref, dkr_ref, dwuq_ref, dwukv_ref,
             dgcq_ref, dgckv_ref, dgqn_ref, dgqr_ref, dgkn_ref, dgkr_ref, acc_uq, acc_ukv):
        i = pl.program_id(0)

        @pl.when(i == 0)
        def _():
            acc_uq[...] = jnp.zeros_like(acc_uq)
            acc_ukv[...] = jnp.zeros_like(acc_ukv)
            for ref in (dgcq_ref, dgckv_ref, dgqn_ref, dgqr_ref, dgkn_ref, dgkr_ref):
                ref[...] = jnp.zeros_like(ref)

        cos_t, sin_t = cos_ref[...], sin_ref[...]
        lo = _lo_mask((tm, LANES))
        qb_v, kvb_v = qb_ref[...], kvb_ref[...]
        dq_parts, dgqn = [], jnp.zeros((1, LANES), jnp.float32)
        for h in range(nh):
            _, xn, r = _norm_fwd(qb_v[:, MLA_NOPE * h: MLA_NOPE * (h + 1)], gqn_ref[...])
            dx, dg = _norm_bwd(xn, r, gqn_ref[...], dqc_ref[h][:, :MLA_NOPE])
            dq_parts.append(dx)
            dgqn = dgqn + dg
        dgqn_ref[...] += dgqn
        dgqr = jnp.zeros((1, LANES), jnp.float32)
        for j in range(nh // 2):
            d_rope = jnp.where(lo, dqc_ref[2 * j][:, MLA_NOPE:], dqc_ref[2 * j + 1][:, MLA_NOPE:])
            d_pre = _rope_bwd(d_rope, cos_t, sin_t)
            xr = qb_v[:, nh * MLA_NOPE + LANES * j: nh * MLA_NOPE + LANES * (j + 1)]
            _, xn, r = _norm_fwd(xr, gqr_ref[...], half=True)
            dx, dg = _norm_bwd(xn, r, gqr_ref[...], d_pre, half=True)
            dq_parts.append(dx)
            dgqr = dgqr + dg
        dgqr_ref[...] += dgqr
        dqb = jnp.concatenate(dq_parts, axis=1).astype(MXU)
        acc_uq[...] += _dot_tn(dqb, cqn_ref[...])
        _, xn, r = _norm_fwd(cq_ref[...], gcq_ref[...])
        dx, dg = _norm_bwd(xn, r, gcq_ref[...], _dot(dqb, wuq_ref[...]))
        dcq_ref[...] = dx.astype(dcq_ref.dtype)
        dgcq_ref[...] += dg
        dkv_parts, dgkn = [], jnp.zeros((1, LANES), jnp.float32)
        d_kr2 = jnp.zeros((tm, LANES), jnp.float32)
        for h in range(nh):
            _, xn, r = _norm_fwd(kvb_v[:, 256 * h: 256 * h + MLA_NOPE], gkn_ref[...])
            dx, dg = _norm_bwd(xn, r, gkn_ref[...], dkc_ref[h][:, :MLA_NOPE])
            dkv_parts += [dx, dv_ref[h]]
            dgkn = dgkn + dg
            d_kr2 = d_kr2 + dkc_ref[h][:, MLA_NOPE:]
        dgkn_ref[...] += dgkn
        dkvb = jnp.concatenate(dkv_parts, axis=1).astype(MXU)
        d_ckvn = jnp.zeros((tm, 512), jnp.float32)
        for dev in range(N_DEV):
            piece = dkvb[:, LANES * dev: LANES * (dev + 1)]
            acc_ukv[dev] += _dot_tn(ckvn_ref[...], piece)
            d_ckvn = d_ckvn + _dot_nt(piece, wukv_ref[dev])
        _, xn, r = _norm_fwd(ckv_ref[...], gckv_ref[...])
        dx, dg = _norm_bwd(xn, r, gckv_ref[...], d_ckvn)
        dckv_ref[...] = dx.astype(dckv_ref.dtype)
        dgckv_ref[...] += dg
        d_kr = jnp.where(lo, d_kr2 + pltpu.roll(d_kr2, 64, 1), 0.0)
        d_pre = _rope_bwd(d_kr, cos_t, sin_t)
        _, xn, r = _norm_fwd(kr_ref[...], gkr_ref[...], half=True)
        dx, dg = _norm_bwd(xn, r, gkr_ref[...], d_pre, half=True)
        dkr_ref[...] = jnp.where(lo, dx, 0.0).astype(dkr_ref.dtype)
        dgkr_ref[...] += jnp.where(_lo_mask((1, LANES)), dg, 0.0)

        @pl.when(i == ni - 1)
        def _():
            dwuq_ref[...] = acc_uq[...].astype(dwuq_ref.dtype)
            dwukv_ref[...] = acc_ukv[...].astype(dwukv_ref.dtype)

    def col(width, start):
        return pl.BlockSpec((tm, width), lambda i: (i, start // width))

    def full(shape):
        return pl.BlockSpec(shape, lambda i: (0,) * len(shape))

    def row(width):
        return pl.BlockSpec((tm, width), lambda i: (i, 0))

    def heads(width):
        return pl.BlockSpec((nh, tm, width), lambda i: (0, i, 0))

    vec = full((1, LANES))
    return _pcall(
        body, name="mla_prep_bwd", grid=(ni,),
        out_shape=[_sds((s, 512), MXU), _sds((s, 512), MXU), _sds((s, LANES), MXU),
                   _sds((768, 512), WIRE), _sds((N_DEV, 512, LANES), WIRE),
                   _sds((1, 512), jnp.float32), _sds((1, 512), jnp.float32)] + [_sds((1, LANES), jnp.float32)] * 4,
        in_specs=[col(512, C_CQ), col(512, C_CKV), col(LANES, C_KR), row(LANES), row(LANES),
                  full((1, 512)), full((1, 512)), full((768, 512)), full((N_DEV, 512, LANES)), vec, vec, vec, vec,
                  row(768), row(1024), row(512), row(512), heads(256), heads(256), heads(MLA_V)],
        out_specs=[row(512), row(512), row(LANES), full((768, 512)), full((N_DEV, 512, LANES)),
                   full((1, 512)), full((1, 512)), vec, vec, vec, vec],
        scratch=[pltpu.VMEM((768, 512), jnp.float32), pltpu.VMEM((N_DEV, 512, LANES), jnp.float32)],
        sem=("arbitrary",))(proj, proj, proj, cos, sin, g_cq, g_ckv, w_uq, w_ukv, g_qn, g_qr, g_kn, g_kr,
                            qb, kvb, cqn, ckvn, dqc, dkc, dv)


def _swa_bwd(proj, posc, posr, gq, gk, sinks, d_y, y_a, lse, after):
    s = proj.shape[0]
    b = SWA_BLOCK
    nb = s // b
    scale = SWA_DIM ** -0.5

    def body(q_ref, kp_ref, kc_ref, vp_ref, vc_ref, pq_ref, pkp_ref, pkc_ref, gq_ref, gk_ref, sink_ref,
             do_ref, y_ref, lse_ref, kfull_ref,
             dq_ref, dk_ref, dv_ref, dgq_ref, dgk_ref, dsink_ref, dk_acc, dv_acc):
        n = pl.program_id(0)

        @pl.when(n == 0)
        def _():
            dk_acc[...] = jnp.zeros_like(dk_acc)
            dv_acc[...] = jnp.zeros_like(dv_acc)
            dgq_ref[...] = jnp.zeros_like(dgq_ref)
            dsink_ref[...] = jnp.zeros_like(dsink_ref)

        kn, v, bias = _swa_common(n, kp_ref[...], kc_ref[...], vp_ref[...], vc_ref[...],
                                  pq_ref[...], pkp_ref[...], pkc_ref[...], gk_ref[...])
        lo = _lo_mask((b, LANES))
        col = lax.broadcasted_iota(jnp.int32, (b, SWA_Q_HEADS), 1)
        col1 = lax.broadcasted_iota(jnp.int32, (1, SWA_Q_HEADS), 1)
        lse_t = lse_ref[...]
        dk_blk = jnp.zeros((2 * b, LANES), jnp.float32)
        dv_blk = jnp.zeros((2 * b, LANES), jnp.float32)
        dgq = jnp.zeros((1, LANES), jnp.float32)
        dsink = jnp.zeros((1, SWA_Q_HEADS), jnp.float32)
        for j in range(SWA_Q_HEADS // 2):
            hk = (2 * j) // (SWA_Q_HEADS // SWA_KV_HEADS)
            kvmask = lo if hk == 0 else jnp.logical_not(lo)
            sl = slice(LANES * j, LANES * (j + 1))
            qn, xn, r = _norm_fwd(q_ref[:, sl], gq_ref[...], half=True)
            qn = qn * scale
            qsw = pltpu.roll(qn, 64, 1)
            d2 = do_ref[:, sl]
            d2sw = pltpu.roll(d2, 64, 1)
            prod = d2 * y_ref[:, sl]
            dqs = []
            for e in range(2):
                h = 2 * j + e
                half_e = lo if e == 0 else jnp.logical_not(lo)
                qm = jnp.where(kvmask, qn if e == hk else qsw, 0.0)
                dm = jnp.where(kvmask, d2 if e == hk else d2sw, 0.0)
                sc = _dot_nt(qm, kn) + _alibi_slope(h) * bias
                lse_h = jnp.sum(jnp.where(col == h, lse_t, 0.0), -1, keepdims=True)
                p = jnp.exp(sc - lse_h)
                dd = jnp.sum(jnp.where(half_e, prod, 0.0), -1, keepdims=True)
                dp = _dot_nt(dm, v)
                ds = (p * (dp - dd)).astype(MXU)
                dsink = dsink - jnp.where(col1 == h, jnp.sum(jnp.exp(sink_ref[h] - lse_h) * dd), 0.0)
                dq_m = _dot(ds, kn) * scale
                dk_blk = dk_blk + _dot_tn(ds, qm)
                dv_blk = dv_blk + _dot_tn(p, dm)
                dqs.append(dq_m if e == hk else pltpu.roll(dq_m, 64, 1))
            dx, dg = _norm_bwd(xn, r, gq_ref[...], jnp.where(lo, dqs[0], dqs[1]), half=True)
            dq_ref[:, sl] = dx.astype(dq_ref.dtype)
            dgq = dgq + dg
        dgq_ref[...] += dgq
        dsink_ref[...] += dsink
        prev = pl.ds(pl.multiple_of(jnp.maximum(n - 1, 0) * b, b), b)
        cur = pl.ds(pl.multiple_of(n * b, b), b)
        dk_acc[prev, :] += dk_blk[:b]
        dv_acc[prev, :] += dv_blk[:b]
        dk_acc[cur, :] += dk_blk[b:]
        dv_acc[cur, :] += dv_blk[b:]

        @pl.when(n == nb - 1)
        def _():
            _, kxn, kr = _norm_fwd(kfull_ref[...], gk_ref[...], half=True)
            dx, dg = _norm_bwd(kxn, kr, gk_ref[...], dk_acc[...], half=True)
            dk_ref[...] = dx.astype(dk_ref.dtype)
            dv_ref[...] = dv_acc[...].astype(dv_ref.dtype)
            dgk_ref[...] = dg

    full = pl.BlockSpec((s, LANES), lambda n: (0, 0))
    vec = pl.BlockSpec((1, LANES), lambda n: (0, 0))
    return _pcall(
        body, name="swa_bwd", grid=(nb,),
        out_shape=[_sds((s, 1024), MXU), _sds((s, LANES), MXU), _sds((s, LANES), MXU),
                   _sds((1, LANES), jnp.float32), _sds((1, LANES), jnp.float32),
                   _sds((1, SWA_Q_HEADS), jnp.float32)],
        in_specs=_swa_specs(s) + [pl.BlockSpec((b, 1024), lambda n: (n, 0)), pl.BlockSpec((b, 1024), lambda n: (n, 0)),
                                  pl.BlockSpec((b, SWA_Q_HEADS), lambda n: (n, 0)),
                                  pl.BlockSpec((s, LANES), lambda n: (0, C_KA // LANES))],
        out_specs=[pl.BlockSpec((b, 1024), lambda n: (n, 0)), full, full, vec, vec,
                   pl.BlockSpec((1, SWA_Q_HEADS), lambda n: (0, 0))],
        scratch=[pltpu.VMEM((s, LANES), jnp.float32), pltpu.VMEM((s, LANES), jnp.float32)],
        sem=("arbitrary",), after=after)(proj, proj, proj, proj, proj, posc, posr, posr, gq, gk, sinks, d_y, y_a, lse,
                                         proj)


def _dx(d_proj, w_in, x, g, d_h1, after):
    s, d = x.shape
    n = w_in.shape[0]
    tm = min(ROW_TILE, s)

    def body(dp_ref, w_ref, x_ref, g_ref, dh_ref, dx_ref, dg_ref):
        i = pl.program_id(0)

        @pl.when(i == 0)
        def _():
            dg_ref[...] = jnp.zeros_like(dg_ref)

        d_hn = _dot(dp_ref[...], w_ref[...])
        _, xn, r = _norm_fwd(x_ref[...], g_ref[...])
        dx, dg = _norm_bwd(xn, r, g_ref[...], d_hn)
        dx_ref[...] = dh_ref[...] + dx
        dg_ref[...] += dg

    row = pl.BlockSpec((tm, d), lambda i: (i, 0))
    vec = pl.BlockSpec((1, d), lambda i: (0, 0))
    return _pcall(
        body, name="grad_x", grid=(s // tm,),
        out_shape=[_sds((s, d), jnp.float32), _sds((1, d), jnp.float32)],
        in_specs=[pl.BlockSpec((tm, n), lambda i: (i, 0)), pl.BlockSpec((n, d), lambda i: (0, 0)), row, vec, row],
        out_specs=[row, vec], sem=("arbitrary",), after=after)(d_proj, w_in, x, g, d_h1)


_SMALL = ["attn_norm_g", "swa_q_norm_g", "swa_k_norm_g", "swa_sinks", "mla_cq_norm_g", "mla_ckv_norm_g",
          "mla_qn_norm_g", "mla_qr_norm_g", "mla_kn_norm_g", "mla_kr_norm_g", "mem_norm_g",
          "mem_q_norm_g", "mem_k_norm_g", "ffn_norm_g"]


def kernel(x, mem, positions, attn_norm_g, w_in, swa_q_norm_g, swa_k_norm_g, swa_sinks, mla_cq_norm_g, mla_ckv_norm_g, w_uq, w_ukv, mla_qn_norm_g, mla_qr_norm_g, mla_kn_norm_g, mla_kr_norm_g, mem_norm_g, w_mem_kv, mem_q_norm_g, mem_k_norm_g, w_out, ffn_norm_g, w_gate, w_up, w_down, loss_target, m_attn_norm_g, m_w_in, m_swa_q_norm_g, m_swa_k_norm_g, m_swa_sinks, m_mla_cq_norm_g, m_mla_ckv_norm_g, m_w_uq, m_w_ukv, m_mla_qn_norm_g, m_mla_qr_norm_g, m_mla_kn_norm_g, m_mla_kr_norm_g, m_mem_norm_g, m_w_mem_kv, m_mem_q_norm_g, m_mem_k_norm_g, m_w_out, m_ffn_norm_g, m_w_gate, m_w_up, m_w_down, v_attn_norm_g, v_w_in, v_swa_q_norm_g, v_swa_k_norm_g, v_swa_sinks, v_mla_cq_norm_g, v_mla_ckv_norm_g, v_w_uq, v_w_ukv, v_mla_qn_norm_g, v_mla_qr_norm_g, v_mla_kn_norm_g, v_mla_kr_norm_g, v_mem_norm_g, v_w_mem_kv, v_mem_q_norm_g, v_mem_k_norm_g, v_w_out, v_ffn_norm_g, v_w_gate, v_w_up, v_w_down):
    args = dict(locals())
    x2, mem2, tgt = x[0], mem[0], loss_target[0]
    s, d = x2.shape
    n_in = w_in.shape[2]
    f = w_gate.shape[2]

    (g_in,) = _all_gather([w_in[0].T.astype(WIRE)])
    mix_shards = [w_uq[0].T.astype(WIRE), w_ukv[0].astype(WIRE), w_mem_kv[0].astype(WIRE),
                  _to_wire([w_out[0]], g_in, "wire_out")[0]]
    g_uq, wkv, g_mkv, g_out = _all_gather_background(mix_shards, 5, "all_gather_mix_weights")
    ffn_shards = [_to_wire([w_gate[0].T, w_up[0].T], g_in, "wire_gate_up"),
                  _to_wire([w_down[0]], g_in, "wire_down")[0]]
    w_gu, w_d = _all_gather_background(ffn_shards, 1, "all_gather_ffn_weights")
    wi = g_in.reshape(N_DEV * n_in, d)
    wi = jnp.concatenate([wi[0:1024], wi[1280:1792], wi[1792:2304], wi[2368:2880],
                          wi[1024:1152], wi[1152:1280], wi[2304:2368],
                          jnp.zeros((IN_PAD - 2880, d), wi.dtype)], axis=0)
    wq = g_uq.reshape(768, 512)
    wq = jnp.concatenate([wq[192 * h: 192 * h + 128] for h in range(4)]
                         + [wq[192 * h + 128: 192 * (h + 1)] for h in range(4)], axis=0)
    wmkv = g_mkv.reshape(-1, g_mkv.shape[-1])
    wo = g_out.reshape(-1, d)

    pos = positions[0].astype(jnp.float32)
    inv_freq = ROPE_THETA ** (-jnp.arange(0, MLA_ROPE, 2, dtype=jnp.float32) / MLA_ROPE)
    ang = pos[:, None] * inv_freq
    cos32, sin32 = jnp.cos(ang), jnp.sin(ang)
    cos_t = jnp.tile(cos32, (1, 4))
    sin_t = jnp.tile(jnp.concatenate([-sin32, sin32], axis=1), (1, 2))
    posc, posr = pos.reshape(s, 1), pos.reshape(1, s)
    two = lambda g: jnp.tile(g, (1, 2))
    gq2, gk2, gqr2, gkr2 = two(swa_q_norm_g), two(swa_k_norm_g), two(mla_qr_norm_g), two(mla_kr_norm_g)
    sinks1 = swa_sinks[0]

    proj, hn = _in_proj(x2, attn_norm_g, wi)
    qc, kc, vb, qb, kvb, cqn, ckvn = _mla_prep(proj, cos_t, sin_t, mla_cq_norm_g, mla_ckv_norm_g, wq, wkv,
                                                mla_qn_norm_g, gqr2, mla_kn_norm_g, gkr2)
    y_b, lse_b = _mla_fwd(qc, kc, vb)
    km, vmm, kvm, memn = _memkv_prep(mem2, mem_norm_g, wmkv, mem_k_norm_g)
    y_m, lse_m = _mem_fwd(proj, mem_q_norm_g, km, vmm)
    y_a, lse_a = _swa_fwd(proj, posc, posr, gq2, gk2, sinks1)
    h1, fn = _out_proj(y_a, y_b, y_m, x2, wo, ffn_norm_g)
    gu, act = _ffn_gu(fn, w_gu)
    dout, loss_tile = _ffn_down(act, w_d, h1, tgt)

    dw_d = _ffn_dw_down(act, dout)
    dgu = _ffn_bwd_act(dout, w_d, gu)
    dw_gu = _ffn_dw_gu(fn, dgu)
    r_gu, r_d = _exchange_grads_background([dw_gu, dw_d], 2, "exchange_ffn_grads")
    d_h1, dg_ffn = _ffn_norm_bwd(_ffn_dfn(dgu, w_gu, dw_gu), dout, h1, ffn_norm_g)
    d_y = _mm(d_h1, wo, tb=True, out_dtype=jnp.float32, tm=FFN_TILE, tk=2048, name="d_mix")
    dw_out = jnp.concatenate([
        _mm(y_a, d_h1, ta=True, out_dtype=WIRE, tm=1024, tk=1024, name="dw_out_a"),
        _mm(y_b, d_h1, ta=True, out_dtype=WIRE, tm=1024, tk=1024, name="dw_out_b"),
        _mm(y_m, d_h1, ta=True, out_dtype=WIRE, tm=1024, tk=1024, name="dw_out_m")], axis=0)
    d_qm, dkm, dvmm, dg_mq = _mem_bwd(proj, mem_q_norm_g, km, vmm, d_y, y_m, lse_m)
    dw_mkv, dg_mem, dg_mk = _memkv_bwd(mem2, mem_norm_g, wmkv, mem_k_norm_g, kvm, memn, dkm, dvmm)
    r_mkv, r_out = _exchange_grads_background([dw_mkv.reshape(g_mkv.shape), dw_out.reshape(g_out.shape)], 3,
                                              "exchange_mix_grads")
    dqc, dkc, dvb = _mla_bwd(qc, kc, vb, d_y, y_b, lse_b, dw_mkv)
    (d_cq, d_ckv, d_kr, dw_uq, dw_ukv, dg_cq, dg_ckv, dg_qn, dg_qr, dg_kn, dg_kr) = _mla_prep_bwd(
        proj, cos_t, sin_t, mla_cq_norm_g, mla_ckv_norm_g, wq, wkv, mla_qn_norm_g, gqr2, mla_kn_norm_g, gkr2,
        qb, kvb, cqn, ckvn, dqc, dkc, dvb)
    d_qa, d_ka, d_va, dg_q, dg_k, d_sinks = _swa_bwd(proj, posc, posr, gq2, gk2, sinks1, d_y, y_a, lse_a, dw_out)
    d_proj = jnp.concatenate([d_qa, d_cq, d_ckv, d_qm, d_ka, d_va, d_kr], axis=1)
    gi = _dw_in(hn, d_proj, n_in)

    gq_ = jnp.concatenate(sum([[dw_uq[128 * h: 128 * (h + 1)], dw_uq[512 + 64 * h: 512 + 64 * (h + 1)]]
                               for h in range(4)], []), axis=0)
    gq_ = gq_.reshape(N_DEV, 96, 512)
    r_in, r_uq, r_ukv = _exchange_grads_background([gi, gq_, dw_ukv], 4, "exchange_in_grads")
    grad_x, dg_attn = _dx(d_proj, wi, x2, attn_norm_g, d_h1, gi)

    big = {}
    def adam(name, r, transposed=False, after=None, which=None):
        w, m, v = args[name][0], args["m_" + name][0], args["v_" + name][0]
        if transposed:
            outs = _adam_big(r, w.T, m.T, v.T, "adam_" + name, after, which)
            return [o.T[None] for o in outs]
        return [o[None] for o in _adam_big(r, w, m, v, "adam_" + name, after)]
    big["w_gate"] = adam("w_gate", r_gu, True, which=0)
    big["w_up"] = adam("w_up", r_gu, True, after=big["w_gate"][0], which=1)
    big["w_down"] = adam("w_down", r_d, after=big["w_up"][0])
    big["w_out"] = adam("w_out", r_out, after=big["w_down"][0])
    big["w_mem_kv"] = adam("w_mem_kv", r_mkv, after=big["w_out"][0])
    big["w_in"] = adam("w_in", r_in, True, after=big["w_mem_kv"][0])
    big["w_uq"] = adam("w_uq", r_uq, True, after=big["w_in"][0])
    big["w_ukv"] = adam("w_ukv", r_ukv, after=big["w_uq"][0])

    small_g = {
        "attn_norm_g": dg_attn, "swa_q_norm_g": dg_q, "swa_k_norm_g": dg_k,
        "swa_sinks": d_sinks, "mla_cq_norm_g": dg_cq, "mla_ckv_norm_g": dg_ckv, "mla_qn_norm_g": dg_qn,
        "mla_qr_norm_g": dg_qr, "mla_kn_norm_g": dg_kn, "mla_kr_norm_g": dg_kr,
        "mem_norm_g": dg_mem, "mem_q_norm_g": dg_mq, "mem_k_norm_g": dg_mk, "ffn_norm_g": dg_ffn}
    loss11, small_out = _small_allreduce_adam(
        [small_g[n] for n in _SMALL], loss_tile, [args[n] for n in _SMALL],
        [args["m_" + n] for n in _SMALL], [args["v_" + n] for n in _SMALL])
    small = dict(zip(_SMALL, small_out))
    loss = loss11.reshape(())

    order = ["attn_norm_g", "w_in", "swa_q_norm_g", "swa_k_norm_g", "swa_sinks", "mla_cq_norm_g", "mla_ckv_norm_g",
             "w_uq", "w_ukv", "mla_qn_norm_g", "mla_qr_norm_g", "mla_kn_norm_g", "mla_kr_norm_g", "mem_norm_g",
             "w_mem_kv", "mem_q_norm_g", "mem_k_norm_g", "w_out", "ffn_norm_g", "w_gate", "w_up", "w_down"]
    res = {n: (big[n] if n in big else list(small[n])) for n in order}
    outs = [loss, grad_x[None]]
    for kind in range(4):
        outs += [res[n][kind] for n in order]
    return tuple(outs)
```

```python
import jax
import jax.numpy as jnp
from jax import lax
from jax.experimental import pallas as pl
from jax.experimental.pallas import tpu as pltpu
from jax.experimental.pallas import tpu_sc as plsc

MXU = jnp.bfloat16
WIRE = jnp.bfloat16
EPS = 1e-6
NEG_INF = -1e30
LOG2E = 1.4426950408889634
N_DEV = 8
LANES = 128
ROW_TILE = 256
FFN_TILE = 512
ATT_TILE = 1024
SWA_BLOCK = 128
VMEM_LIMIT = 56 * 1024 * 1024

SWA_Q_HEADS, SWA_KV_HEADS, SWA_DIM = 16, 2, 64
MLA_HEADS, MLA_NOPE, MLA_ROPE, MLA_V = 4, 128, 64, 128
MEM_HEADS, MEM_DIM = 4, 128
ROPE_THETA = 10000.0
ADAM_LR, ADAM_B1, ADAM_B2, ADAM_EPS, ADAM_WD, ADAM_STEP = 0.001, 0.9, 0.999, 1e-08, 0.01, 10

C_QA, C_CQ, C_CKV, C_QM, C_KA, C_VA, C_KR, IN_PAD = 0, 1024, 1536, 2048, 2560, 2688, 2816, 2944


def _pcall(body, *, name, out_shape, in_specs, out_specs, grid=(), scratch=(), sem=None, after=None):
    params = pltpu.CompilerParams(dimension_semantics=sem, vmem_limit_bytes=VMEM_LIMIT)
    if after is not None:
        n_in, inner = len(in_specs), body

        def body(*refs):
            inner(*refs[:n_in], *refs[n_in + 1:])

        in_specs = list(in_specs) + [pl.BlockSpec(memory_space=pl.ANY)]
    call = pl.pallas_call(body, name=name, grid=grid, in_specs=in_specs, out_specs=out_specs,
                          out_shape=out_shape, scratch_shapes=list(scratch), compiler_params=params)
    return call if after is None else (lambda *ops: call(*ops, after))


def _sds(shape, dtype):
    return jax.ShapeDtypeStruct(tuple(shape), dtype)


def _dot(a, b):
    return jnp.dot(a.astype(MXU), b.astype(MXU), preferred_element_type=jnp.float32)


def _dot_nt(a, b):
    return lax.dot_general(a.astype(MXU), b.astype(MXU), (((1,), (1,)), ((), ())),
                           preferred_element_type=jnp.float32)


def _dot_tn(a, b):
    return lax.dot_general(a.astype(MXU), b.astype(MXU), (((0,), (0,)), ((), ())),
                           preferred_element_type=jnp.float32)


def _lo_mask(shape):
    return (lax.broadcasted_iota(jnp.int32, shape, len(shape) - 1) % LANES) < 64


def _norm_fwd(x, g, half=False):
    x2 = x * x
    if half:
        lo = _lo_mask(x.shape)
        s_lo = jnp.sum(jnp.where(lo, x2, 0.0), -1, keepdims=True)
        s_hi = jnp.sum(jnp.where(lo, 0.0, x2), -1, keepdims=True)
        r = jnp.where(lo, lax.rsqrt(s_lo / 64.0 + EPS), lax.rsqrt(s_hi / 64.0 + EPS))
    else:
        r = lax.rsqrt(jnp.mean(x2, -1, keepdims=True) + EPS)
    xn = x * r
    return xn * g, xn, r


def _norm_bwd(xn, r, g, dy, half=False):
    t = dy * g
    tx = t * xn
    if half:
        lo = _lo_mask(xn.shape)
        m_lo = jnp.sum(jnp.where(lo, tx, 0.0), -1, keepdims=True) / 64.0
        m_hi = jnp.sum(jnp.where(lo, 0.0, tx), -1, keepdims=True) / 64.0
        m = jnp.where(lo, m_lo, m_hi)
    else:
        m = jnp.mean(tx, -1, keepdims=True)
    dx = r * (t - xn * m)
    dg = jnp.sum(dy * xn, 0, keepdims=True)
    return dx, dg


def _swap32(x):
    lane = lax.broadcasted_iota(jnp.int32, x.shape, 1)
    return jnp.where((lane % 64) < 32, pltpu.roll(x, 96, 1), pltpu.roll(x, 32, 1))


def _rope(x, cos, sin):
    return x * cos + _swap32(x) * sin


def _rope_bwd(d, cos, sin):
    return d * cos + _swap32(d * sin)


def _my_coords():
    return lax.axis_index("x"), lax.axis_index("y"), lax.axis_index("c")


def _dev_index(px, py, pc):
    return 4 * px + 2 * py + pc


_FLIPS = [(0, 0, 1), (0, 1, 0), (0, 1, 1), (1, 0, 0), (1, 0, 1), (1, 1, 0), (1, 1, 1)]


def _flip(coords, f):
    return tuple((1 - v) if b else v for v, b in zip(coords, f))


def _all_gather(shards):
    n = len(shards)

    def body(*refs):
        ins, outs = refs[:n], refs[n:2 * n]
        send_sems, recv_sems, local_sems = refs[2 * n:]
        x, y, c = _my_coords()
        me, sibling = (x, y, c), (x, y, 1 - c)
        chips = [(1 - x, y), (x, 1 - y), (1 - x, 1 - y)]

        def copy(w, k, block, to, src=None):
            dst = outs[w].at[_dev_index(*block)]
            return pltpu.make_async_remote_copy(
                src_ref=dst if src is None else src, dst_ref=dst,
                send_sem=send_sems.at[w, k], recv_sem=recv_sems.at[w, k],
                device_id=to, device_id_type=pl.DeviceIdType.MESH)

        sends, locals_ = [], []
        for w in range(n):
            mine = pltpu.make_async_copy(ins[w], outs[w].at[_dev_index(*me)], local_sems.at[w])
            mine.start()
            locals_.append(mine)
            first = [copy(w, 0, me, sibling, src=ins[w])]
            first += [copy(w, 1 + j, me, (*chip, c), src=ins[w]) for j, chip in enumerate(chips)]
            for cp in first:
                cp.start()
            sends += first
        for w in range(n):
            for j, chip in enumerate(chips):
                copy(w, 1 + j, (*chip, c), me).wait_recv()
                fwd = copy(w, 4 + j, (*chip, c), sibling)
                fwd.start()
                sends.append(fwd)
        for w in range(n):
            copy(w, 0, sibling, me).wait_recv()
            for j, chip in enumerate(chips):
                copy(w, 4 + j, (*chip, 1 - c), me).wait_recv()
        for cp in sends:
            cp.wait_send()
        for mine in locals_:
            mine.wait()

    any_spec = pl.BlockSpec(memory_space=pl.ANY)
    return _pcall(
        body, name="all_gather_weights",
        out_shape=[_sds((N_DEV,) + s.shape, s.dtype) for s in shards],
        in_specs=[any_spec] * n, out_specs=[any_spec] * n,
        scratch=[pltpu.SemaphoreType.DMA((n, 7)), pltpu.SemaphoreType.DMA((n, 7)),
                 pltpu.SemaphoreType.DMA((n,))])(*shards)


def _wire_cost(arrays):
    nbytes = sum(a.size * a.dtype.itemsize for a in arrays)
    return pl.CostEstimate(flops=0, transcendentals=0, bytes_accessed=40 * nbytes)


def _all_gather_background(shards, collective_id, name):
    n = len(shards)
    src_refs = [jax.new_ref(s, memory_space=pltpu.MemorySpace.HBM) for s in shards]
    out_refs = [jax.empty_ref(_sds((N_DEV,) + s.shape, s.dtype), memory_space=pltpu.MemorySpace.HBM) for s in shards]

    @pl.kernel(mesh=plsc.ScalarSubcoreMesh(axis_name="seq", num_cores=1), name=name,
               scratch_types=(pltpu.SemaphoreType.DMA((n, 7)), pltpu.SemaphoreType.DMA((n, 7)),
                              pltpu.SemaphoreType.DMA((n,))),
               compiler_params=pltpu.CompilerParams(collective_id=collective_id))
    def launch(send_sems, recv_sems, local_sems):
        x, y, c = _my_coords()
        me, sibling = (x, y, c), (x, y, 1 - c)
        chips = [(1 - x, y), (x, 1 - y), (1 - x, 1 - y)]
        barrier = pltpu.get_barrier_semaphore()
        for peer in [sibling] + [(*chip, c) for chip in chips]:
            pl.semaphore_signal(barrier, inc=1, device_id=peer, device_id_type=pl.DeviceIdType.MESH)
        pl.semaphore_wait(barrier, 4)

        def copy(w, k, block, to, src=None):
            dst = out_refs[w].at[_dev_index(*block)]
            return pltpu.make_async_remote_copy(
                src_ref=dst if src is None else src, dst_ref=dst,
                send_sem=send_sems.at[w, k], recv_sem=recv_sems.at[w, k],
                device_id=to, device_id_type=pl.DeviceIdType.MESH)

        sends, locals_ = [], []
        for w in range(n):
            mine = pltpu.make_async_copy(src_refs[w], out_refs[w].at[_dev_index(*me)], local_sems.at[w])
            mine.start()
            locals_.append(mine)
            first = [copy(w, 0, me, sibling, src=src_refs[w])]
            first += [copy(w, 1 + j, me, (*chip, c), src=src_refs[w]) for j, chip in enumerate(chips)]
            for cp in first:
                cp.start()
            sends += first
        for w in range(n):
            for j, chip in enumerate(chips):
                copy(w, 1 + j, (*chip, c), me).wait_recv()
                fwd = copy(w, 4 + j, (*chip, c), sibling)
                fwd.start()
                sends.append(fwd)
        for w in range(n):
            copy(w, 0, sibling, me).wait_recv()
            for j, chip in enumerate(chips):
                copy(w, 4 + j, (*chip, 1 - c), me).wait_recv()
        for cp in sends:
            cp.wait_send()
        for mine in locals_:
            mine.wait()

    launch()
    return [r[...] for r in out_refs]


def _exchange_grads(grads):
    n = len(grads)

    def body(*refs):
        ins, outs = refs[:n], refs[n:2 * n]
        send_sems, recv_sems, local_sems = refs[2 * n:]
        me = _my_coords()
        my_idx = _dev_index(*me)
        sends, locals_ = [], []
        for w in range(n):
            mine = pltpu.make_async_copy(ins[w].at[my_idx], outs[w].at[my_idx], local_sems.at[w])
            mine.start()
            locals_.append(mine)
            for k, f in enumerate(_FLIPS):
                peer = _flip(me, f)
                cp = pltpu.make_async_remote_copy(
                    src_ref=ins[w].at[_dev_index(*peer)], dst_ref=outs[w].at[my_idx],
                    send_sem=send_sems.at[w, k], recv_sem=recv_sems.at[w, k],
                    device_id=peer, device_id_type=pl.DeviceIdType.MESH)
                cp.start()
                sends.append(cp)
        for w in range(n):
            for k, f in enumerate(_FLIPS):
                peer = _flip(me, f)
                slot = outs[w].at[_dev_index(*peer)]
                pltpu.make_async_remote_copy(
                    src_ref=slot, dst_ref=slot,
                    send_sem=send_sems.at[w, k], recv_sem=recv_sems.at[w, k],
                    device_id=peer, device_id_type=pl.DeviceIdType.MESH).wait_recv()
        for cp in sends:
            cp.wait_send()
        for mine in locals_:
            mine.wait()

    any_spec = pl.BlockSpec(memory_space=pl.ANY)
    return _pcall(
        body, name="exchange_grads",
        out_shape=[_sds(g.shape, g.dtype) for g in grads],
        in_specs=[any_spec] * n, out_specs=[any_spec] * n,
        scratch=[pltpu.SemaphoreType.DMA((n, 7)), pltpu.SemaphoreType.DMA((n, 7)),
                 pltpu.SemaphoreType.DMA((n,))])(*grads)


def _exchange_grads_background(grads, collective_id, name):
    n = len(grads)
    src_refs = [jax.new_ref(g, memory_space=pltpu.MemorySpace.HBM) for g in grads]
    out_refs = [jax.empty_ref(_sds(g.shape, g.dtype), memory_space=pltpu.MemorySpace.HBM) for g in grads]

    @pl.kernel(mesh=plsc.ScalarSubcoreMesh(axis_name="seq", num_cores=1), name=name,
               scratch_types=(pltpu.SemaphoreType.DMA((n, 7)), pltpu.SemaphoreType.DMA((n, 7)),
                              pltpu.SemaphoreType.DMA((n,))),
               cost_estimate=_wire_cost(grads),
               compiler_params=pltpu.CompilerParams(collective_id=collective_id))
    def launch(send_sems, recv_sems, local_sems):
        me = _my_coords()
        my_idx = _dev_index(*me)
        peers = [_flip(me, f) for f in _FLIPS]
        barrier = pltpu.get_barrier_semaphore()
        for peer in peers:
            pl.semaphore_signal(barrier, inc=1, device_id=peer, device_id_type=pl.DeviceIdType.MESH)
        pl.semaphore_wait(barrier, len(peers))
        sends, locals_ = [], []
        for w in range(n):
            mine = pltpu.make_async_copy(src_refs[w].at[my_idx], out_refs[w].at[my_idx], local_sems.at[w])
            mine.start()
            locals_.append(mine)
            for k, peer in enumerate(peers):
                cp = pltpu.make_async_remote_copy(
                    src_ref=src_refs[w].at[_dev_index(*peer)], dst_ref=out_refs[w].at[my_idx],
                    send_sem=send_sems.at[w, k], recv_sem=recv_sems.at[w, k],
                    device_id=peer, device_id_type=pl.DeviceIdType.MESH)
                cp.start()
                sends.append(cp)
        for w in range(n):
            for k, peer in enumerate(peers):
                slot = out_refs[w].at[_dev_index(*peer)]
                pltpu.make_async_remote_copy(
                    src_ref=slot, dst_ref=slot, send_sem=send_sems.at[w, k], recv_sem=recv_sems.at[w, k],
                    device_id=peer, device_id_type=pl.DeviceIdType.MESH).wait_recv()
        for cp in sends:
            cp.wait_send()
        for mine in locals_:
            mine.wait()

    launch()
    return [r[...] for r in out_refs]


def _to_wire(parts, after, name):
    n = len(parts)
    rows, cols = parts[0].shape
    tr = rows // 2 if rows % 32 == 0 else rows

    def body(*refs):
        for k in range(n):
            refs[n][k] = refs[k][...].astype(WIRE)

    blk = pl.BlockSpec((tr, cols), lambda i: (i, 0))
    return _pcall(
        body, name=name, grid=(rows // tr,), out_shape=_sds((n, rows, cols), WIRE),
        in_specs=[blk] * n, out_specs=pl.BlockSpec((n, tr, cols), lambda i: (0, i, 0)),
        sem=("parallel",), after=after)(*parts)


def _adam_math(w, g, m, v):
    m = ADAM_B1 * m + (1.0 - ADAM_B1) * g
    v = ADAM_B2 * v + (1.0 - ADAM_B2) * (g * g)
    m_hat = m / (1.0 - ADAM_B1 ** ADAM_STEP)
    v_hat = v / (1.0 - ADAM_B2 ** ADAM_STEP)
    delta = -ADAM_LR * (m_hat / (jnp.sqrt(v_hat) + ADAM_EPS) + ADAM_WD * w)
    return delta, m, v


def _small_allreduce_adam(grads, loss_tile, ws, ms, vs):
    sizes = [w.shape[-1] for w in ws]
    n_par = len(ws)
    row0, r = [], 0
    for n in sizes:
        row0.append(r)
        r += -(-n // LANES)
    loss_row = r
    rows = -(-(r + 1) // 8) * 8

    def pieces(n):
        return [(k, min(LANES, n - LANES * k)) for k in range(-(-n // LANES))]

    def body(*refs):
        g_refs = refs[:n_par]
        loss_in = refs[n_par]
        w_refs = refs[n_par + 1: 2 * n_par + 1]
        m_refs = refs[2 * n_par + 1: 3 * n_par + 1]
        v_refs = refs[3 * n_par + 1: 4 * n_par + 1]
        loss_out = refs[4 * n_par + 1]
        out_refs = refs[4 * n_par + 2: 8 * n_par + 2]
        pack, gath, res, send_sems, recv_sems = refs[8 * n_par + 2:]
        me = _my_coords()
        my_idx = _dev_index(*me)

        def fill(slot, srcs):
            pack[slot] = jnp.zeros((rows, LANES), jnp.float32)
            for p, n in enumerate(sizes):
                val = srcs[p][...]
                if val.shape[-1] == LANES and n == 64:
                    pack[slot, row0[p]:row0[p] + 1, :] = val + pltpu.roll(val, 64, 1)
                    continue
                for k, width in pieces(n):
                    pack[slot, row0[p] + k:row0[p] + k + 1, 0:width] = srcs[p][:, LANES * k:LANES * k + width]

        fill(0, g_refs)
        pack[0, loss_row:loss_row + 1, :] = loss_in[0:1, :]
        gath[my_idx] = pack[0]
        sends = []
        for k, f in enumerate(_FLIPS):
            peer = _flip(me, f)
            cp = pltpu.make_async_remote_copy(
                src_ref=pack.at[0], dst_ref=gath.at[my_idx],
                send_sem=send_sems.at[k], recv_sem=recv_sems.at[k],
                device_id=peer, device_id_type=pl.DeviceIdType.MESH)
            cp.start()
            sends.append(cp)
        fill(1, w_refs)
        fill(2, m_refs)
        fill(3, v_refs)
        for k, f in enumerate(_FLIPS):
            peer = _flip(me, f)
            slot = gath.at[_dev_index(*peer)]
            pltpu.make_async_remote_copy(
                src_ref=slot, dst_ref=slot, send_sem=send_sems.at[k], recv_sem=recv_sems.at[k],
                device_id=peer, device_id_type=pl.DeviceIdType.MESH).wait_recv()
        for cp in sends:
            cp.wait_send()
        g = gath[0]
        for d in range(1, N_DEV):
            g = g + gath[d]
        delta, m, v = _adam_math(pack[1], g, pack[2], pack[3])
        res[0], res[1], res[2], res[3] = g, delta, m, v
        loss_out[...] = res[0, loss_row:loss_row + 1, 0:1]
        for p, n in enumerate(sizes):
            for kind in range(4):
                for k, width in pieces(n):
                    out_refs[4 * p + kind][:, LANES * k:LANES * k + width] = (
                        res[kind, row0[p] + k:row0[p] + k + 1, 0:width])

    vm = pl.BlockSpec(memory_space=pltpu.VMEM)
    out_shape = [_sds((1, 1), jnp.float32)]
    for n in sizes:
        out_shape += [_sds((1, n), jnp.float32)] * 4
    outs = _pcall(
        body, name="small_allreduce_adam", out_shape=out_shape,
        in_specs=[vm] * (4 * n_par + 1), out_specs=[vm] * len(out_shape),
        scratch=[pltpu.VMEM((4, rows, LANES), jnp.float32), pltpu.VMEM((N_DEV, rows, LANES), jnp.float32),
                 pltpu.VMEM((4, rows, LANES), jnp.float32),
                 pltpu.SemaphoreType.DMA((7,)), pltpu.SemaphoreType.DMA((7,))])(*grads, loss_tile, *ws, *ms, *vs)
    return outs[0], [outs[1 + 4 * p: 5 + 4 * p] for p in range(n_par)]


def _adam_big(recv, w, m, v, name, after=None, which=None):
    rows, cols = recv.shape[-2:]
    row_tiles = [t for t in range(16, rows + 1, 16) if rows % t == 0 and t * cols <= 400 * 1024]
    tr, tc = (max(row_tiles), cols) if row_tiles else (rows, 512 if cols % 512 == 0 else cols)

    def body(r_ref, w_ref, m_ref, v_ref, g_ref, d_ref, mo_ref, vo_ref):
        g = r_ref[0].astype(jnp.float32)
        for d in range(1, N_DEV):
            g = g + r_ref[d].astype(jnp.float32)
        delta, mn, vn = _adam_math(w_ref[...], g, m_ref[...], v_ref[...])
        g_ref[...] = g
        d_ref[...] = delta
        mo_ref[...] = mn
        vo_ref[...] = vn

    blk = pl.BlockSpec((tr, tc), lambda i, j: (i, j))
    if which is None:
        r_spec = pl.BlockSpec((N_DEV, tr, tc), lambda i, j: (0, i, j))
    else:
        r_spec = pl.BlockSpec((N_DEV, None, tr, tc), lambda i, j: (0, which, i, j))
    return _pcall(
        body, name=name, grid=(rows // tr, cols // tc),
        out_shape=[_sds((rows, cols), jnp.float32)] * 4,
        in_specs=[r_spec, blk, blk, blk],
        out_specs=[blk] * 4, sem=("parallel", "parallel"), after=after)(recv, w, m, v)


def _mm(a, b, *, ta=False, tb=False, out_dtype, tm, tk, name):
    (kdim, mdim) = a.shape if ta else a.shape[::-1]
    ndim = b.shape[0] if tb else b.shape[1]
    tm, tk = min(tm, mdim), min(tk, kdim)
    nk = kdim // tk

    def body(a_ref, b_ref, o_ref, acc):
        k = pl.program_id(1)
        if ta:
            part = _dot_tn(a_ref[...], b_ref[...])
        elif tb:
            part = _dot_nt(a_ref[...], b_ref[...])
        else:
            part = _dot(a_ref[...], b_ref[...])

        @pl.when(k == 0)
        def _():
            acc[...] = part

        @pl.when(k > 0)
        def _():
            acc[...] += part

        @pl.when(k == nk - 1)
        def _():
            o_ref[...] = acc[...].astype(o_ref.dtype)

    a_spec = pl.BlockSpec((tk, tm), lambda i, k: (k, i)) if ta else pl.BlockSpec((tm, tk), lambda i, k: (i, k))
    b_spec = pl.BlockSpec((ndim, tk), lambda i, k: (0, k)) if tb else pl.BlockSpec((tk, ndim), lambda i, k: (k, 0))
    return _pcall(
        body, name=name, grid=(mdim // tm, nk), out_shape=_sds((mdim, ndim), out_dtype),
        in_specs=[a_spec, b_spec], out_specs=pl.BlockSpec((tm, ndim), lambda i, k: (i, 0)),
        scratch=[pltpu.VMEM((tm, ndim), jnp.float32)], sem=("parallel", "arbitrary"))(a, b)


def _ref_col_pieces(start, stop):
    ref_starts = [0, 1024, 1152, 1280, 1792, 2304, 2368, 2880]
    perm_starts = [C_QA, C_KA, C_VA, C_CQ, C_CKV, C_KR, C_QM]
    out = []
    for p in range(7):
        lo, hi = max(start, ref_starts[p]), min(stop, ref_starts[p + 1])
        if lo < hi:
            out.append((lo - start, perm_starts[p] + lo - ref_starts[p], hi - lo))
    return out


def _dw_in(hn, d_proj, n_shard):
    s, d = hn.shape
    n = d_proj.shape[1]
    tm, tk = min(512, d), min(1024, s)
    nk = s // tk

    def body(a_ref, b_ref, o_ref, acc):
        k = pl.program_id(1)
        part = _dot_tn(a_ref[...], b_ref[...])

        @pl.when(k == 0)
        def _():
            acc[...] = part

        @pl.when(k > 0)
        def _():
            acc[...] += part

        @pl.when(k == nk - 1)
        def _():
            t = acc[...].T
            for j in range(N_DEV):
                rows = [t[src:src + width] for _, src, width in _ref_col_pieces(j * n_shard, (j + 1) * n_shard)]
                o_ref[j] = jnp.concatenate(rows, axis=0).astype(o_ref.dtype)

    return _pcall(
        body, name="dw_in", grid=(d // tm, nk), out_shape=_sds((N_DEV, n_shard, d), WIRE),
        in_specs=[pl.BlockSpec((tk, tm), lambda i, k: (k, i)), pl.BlockSpec((tk, n), lambda i, k: (k, 0))],
        out_specs=pl.BlockSpec((N_DEV, n_shard, tm), lambda i, k: (0, 0, i)),
        scratch=[pltpu.VMEM((tm, n), jnp.float32)], sem=("parallel", "arbitrary"))(hn, d_proj)


def _in_proj(x, g, w):
    s, d = x.shape
    n = w.shape[0]
    tm = min(ROW_TILE, s)

    def body(x_ref, g_ref, w_ref, p_ref, hn_ref):
        hn, _, _ = _norm_fwd(x_ref[...], g_ref[...])
        hn_ref[...] = hn.astype(hn_ref.dtype)
        p_ref[...] = _dot_nt(hn, w_ref[...])

    return _pcall(
        body, name="in_proj", grid=(s // tm,),
        out_shape=[_sds((s, n), jnp.float32), _sds((s, d), MXU)],
        in_specs=[pl.BlockSpec((tm, d), lambda i: (i, 0)), pl.BlockSpec((1, d), lambda i: (0, 0)),
                  pl.BlockSpec((n, d), lambda i: (0, 0))],
        out_specs=[pl.BlockSpec((tm, n), lambda i: (i, 0)), pl.BlockSpec((tm, d), lambda i: (i, 0))],
        sem=("parallel",))(x, g, w)


def _mla_prep(proj, cos, sin, g_cq, g_ckv, w_uq, w_ukv, g_qn, g_qr, g_kn, g_kr):
    s = proj.shape[0]
    tm = min(ROW_TILE, s)
    nh = MLA_HEADS

    def body(cq_ref, ckv_ref, kr_ref, cos_ref, sin_ref, gcq_ref, gckv_ref, wuq_ref, wukv_ref,
             gqn_ref, gqr_ref, gkn_ref, gkr_ref,
             qc_ref, kc_ref, v_ref, qb_ref, kvb_ref, cqn_ref, ckvn_ref):
        cos_t, sin_t = cos_ref[...], sin_ref[...]
        lo = _lo_mask((tm, LANES))
        cqn, _, _ = _norm_fwd(cq_ref[...], gcq_ref[...])
        cqn_ref[...] = cqn.astype(cqn_ref.dtype)
        qb = _dot_nt(cqn, wuq_ref[...])
        qb_ref[...] = qb
        ckvn, _, _ = _norm_fwd(ckv_ref[...], gckv_ref[...])
        ckvn_ref[...] = ckvn.astype(ckvn_ref.dtype)
        kvb = jnp.concatenate([_dot(ckvn, wukv_ref[dev]) for dev in range(N_DEV)], axis=1)
        kvb_ref[...] = kvb
        kr, _, _ = _norm_fwd(kr_ref[...], gkr_ref[...], half=True)
        kr = _rope(kr, cos_t, sin_t)
        kr2 = jnp.where(lo, kr, pltpu.roll(kr, 64, 1))
        ropes = []
        for j in range(nh // 2):
            xr = qb[:, nh * MLA_NOPE + LANES * j: nh * MLA_NOPE + LANES * (j + 1)]
            qr, _, _ = _norm_fwd(xr, gqr_ref[...], half=True)
            ropes.append(_rope(qr, cos_t, sin_t))
        for h in range(nh):
            qn, _, _ = _norm_fwd(qb[:, MLA_NOPE * h: MLA_NOPE * (h + 1)], gqn_ref[...])
            mask = lo if h % 2 == 0 else jnp.logical_not(lo)
            qr = jnp.where(mask, ropes[h // 2], 0.0)
            qc_ref[h] = jnp.concatenate([qn, qr], axis=1).astype(qc_ref.dtype)
            kn, _, _ = _norm_fwd(kvb[:, 256 * h: 256 * h + MLA_NOPE], gkn_ref[...])
            kc_ref[h] = jnp.concatenate([kn, kr2], axis=1).astype(kc_ref.dtype)
            v_ref[h] = kvb[:, 256 * h + MLA_NOPE: 256 * (h + 1)].astype(v_ref.dtype)

    def col(width, start):
        return pl.BlockSpec((tm, width), lambda i: (i, start // width))

    def full(shape):
        return pl.BlockSpec(shape, lambda i: (0,) * len(shape))

    def row(width):
        return pl.BlockSpec((tm, width), lambda i: (i, 0))

    def heads(width):
        return pl.BlockSpec((nh, tm, width), lambda i: (0, i, 0))

    return _pcall(
        body, name="mla_prep", grid=(s // tm,),
        out_shape=[_sds((nh, s, 256), MXU), _sds((nh, s, 256), MXU), _sds((nh, s, MLA_V), MXU),
                   _sds((s, 768), jnp.float32), _sds((s, 1024), jnp.float32),
                   _sds((s, 512), MXU), _sds((s, 512), MXU)],
        in_specs=[col(512, C_CQ), col(512, C_CKV), col(LANES, C_KR), row(LANES), row(LANES),
                  full((1, 512)), full((1, 512)), full((768, 512)), full((N_DEV, 512, LANES)),
                  full((1, LANES)), full((1, LANES)), full((1, LANES)), full((1, LANES))],
        out_specs=[heads(256), heads(256), heads(MLA_V), row(768), row(1024), row(512), row(512)],
        sem=("parallel",))(proj, proj, proj, cos, sin, g_cq, g_ckv, w_uq, w_ukv, g_qn, g_qr, g_kn, g_kr)


def _mla_fwd(qc, kc, v):
    nh, s, _ = qc.shape
    t = min(ATT_TILE, s)
    nb = s // t
    scale = (MLA_NOPE + MLA_ROPE) ** -0.5

    def body(q_ref, k_ref, v_ref, y_ref, lse_ref, m_sc, l_sc, acc):
        qi, ki = pl.program_id(1), pl.program_id(2)

        @pl.when(ki == 0)
        def _():
            m_sc[...] = jnp.full_like(m_sc, NEG_INF)
            l_sc[...] = jnp.zeros_like(l_sc)
            acc[...] = jnp.zeros_like(acc)

        def step(diagonal):
            sc = _dot_nt(q_ref[0], k_ref[0]) * (scale * LOG2E)
            if diagonal:
                r_i = lax.broadcasted_iota(jnp.int32, sc.shape, 0)
                c_i = lax.broadcasted_iota(jnp.int32, sc.shape, 1)
                sc = jnp.where(c_i <= r_i, sc, NEG_INF)
            m_new = jnp.maximum(m_sc[...], jnp.max(sc, -1, keepdims=True))
            alpha = jnp.exp2(m_sc[...] - m_new)
            p = jnp.exp2(sc - m_new)
            l_sc[...] = alpha * l_sc[...] + jnp.sum(p, -1, keepdims=True)
            acc[...] = alpha * acc[...] + _dot(p, v_ref[0])
            m_sc[...] = m_new

        @pl.when(ki < qi)
        def _():
            step(False)

        @pl.when(ki == qi)
        def _():
            step(True)

        @pl.when(ki == qi)
        def _():
            y_ref[...] = acc[...] / l_sc[...]
            lse_ref[0] = m_sc[...] + jnp.log2(l_sc[...])

    return _pcall(
        body, name="mla_fwd", grid=(nh, nb, nb),
        out_shape=[_sds((s, nh * MLA_V), jnp.float32), _sds((nh, s, 1), jnp.float32)],
        in_specs=[pl.BlockSpec((1, t, 256), lambda h, i, k: (h, i, 0)),
                  pl.BlockSpec((1, t, 256), lambda h, i, k: (h, jnp.minimum(k, i), 0)),
                  pl.BlockSpec((1, t, MLA_V), lambda h, i, k: (h, jnp.minimum(k, i), 0))],
        out_specs=[pl.BlockSpec((t, MLA_V), lambda h, i, k: (i, h)),
                   pl.BlockSpec((1, t, 1), lambda h, i, k: (h, i, 0))],
        scratch=[pltpu.VMEM((t, 1), jnp.float32), pltpu.VMEM((t, 1), jnp.float32),
                 pltpu.VMEM((t, MLA_V), jnp.float32)],
        sem=("parallel", "parallel", "arbitrary"))(qc, kc, v)


def _memkv_prep(mem, g_mem, w_mkv, g_mk):
    ml, d = mem.shape
    hw = MEM_HEADS * MEM_DIM

    def body(mem_ref, g_ref, w_ref, gk_ref, k_ref, v_ref, kv_ref, mn_ref):
        mn, _, _ = _norm_fwd(mem_ref[...], g_ref[...])
        mn_ref[...] = mn.astype(mn_ref.dtype)
        kv = _dot(mn, w_ref[...])
        kv_ref[...] = kv
        for h in range(MEM_HEADS):
            kn, _, _ = _norm_fwd(kv[:, MEM_DIM * h: MEM_DIM * (h + 1)], gk_ref[...])
            k_ref[:, MEM_DIM * h: MEM_DIM * (h + 1)] = kn.astype(k_ref.dtype)
        v_ref[...] = kv[:, hw:].astype(v_ref.dtype)

    vm = pl.BlockSpec(memory_space=pltpu.VMEM)
    return _pcall(
        body, name="memkv_prep",
        out_shape=[_sds((ml, hw), MXU), _sds((ml, hw), MXU), _sds((ml, 2 * hw), jnp.float32), _sds((ml, d), MXU)],
        in_specs=[vm] * 4, out_specs=[vm] * 4)(mem, g_mem, w_mkv, g_mk)


def _mem_fwd(proj, g_mq, km, vmm):
    s = proj.shape[0]
    ml, hw = km.shape
    tm = min(FFN_TILE, s)
    scale = MEM_DIM ** -0.5

    def body(q_ref, g_ref, k_ref, v_ref, y_ref, lse_ref):
        col = lax.broadcasted_iota(jnp.int32, (tm, MEM_HEADS), 1)
        lse_t = jnp.zeros((tm, MEM_HEADS), jnp.float32)
        for h in range(MEM_HEADS):
            sl = slice(MEM_DIM * h, MEM_DIM * (h + 1))
            qn, _, _ = _norm_fwd(q_ref[:, sl], g_ref[...])
            sc = _dot_nt(qn, k_ref[:, sl]) * scale
            m = jnp.max(sc, -1, keepdims=True)
            p = jnp.exp(sc - m)
            l = jnp.sum(p, -1, keepdims=True)
            y_ref[:, sl] = _dot(p, v_ref[:, sl]) / l
            lse_t = jnp.where(col == h, m + jnp.log(l), lse_t)
        lse_ref[...] = lse_t

    return _pcall(
        body, name="mem_fwd", grid=(s // tm,),
        out_shape=[_sds((s, hw), jnp.float32), _sds((s, MEM_HEADS), jnp.float32)],
        in_specs=[pl.BlockSpec((tm, hw), lambda i: (i, C_QM // hw)), pl.BlockSpec((1, MEM_DIM), lambda i: (0, 0)),
                  pl.BlockSpec((ml, hw), lambda i: (0, 0)), pl.BlockSpec((ml, hw), lambda i: (0, 0))],
        out_specs=[pl.BlockSpec((tm, hw), lambda i: (i, 0)), pl.BlockSpec((tm, MEM_HEADS), lambda i: (i, 0))],
        sem=("parallel",))(proj, g_mq, km, vmm)


def _alibi_slope(h):
    return float(2.0 ** (-8.0 * (h + 1) / SWA_Q_HEADS))


def _swa_common(n, kp, kc, vp, vc, pq, pkp, pkc, gk):
    b = SWA_BLOCK
    k_raw = jnp.concatenate([kp, kc], axis=0)
    kn, kxn, kr = _norm_fwd(k_raw, gk, half=True)
    v = jnp.concatenate([vp, vc], axis=0)
    dist = jnp.abs(pq - jnp.concatenate([pkp, pkc], axis=1))
    r_i = lax.broadcasted_iota(jnp.int32, (b, 2 * b), 0)
    c_i = lax.broadcasted_iota(jnp.int32, (b, 2 * b), 1)
    valid = (c_i > r_i) & (c_i <= r_i + b) & (c_i >= jnp.where(n > 0, 0, b))
    bias = jnp.where(valid, -dist, NEG_INF)
    return kn, v, bias


def _swa_specs(s):
    b = SWA_BLOCK
    prev = lambda n: jnp.maximum(n - 1, 0)
    return [
        pl.BlockSpec((b, 1024), lambda n: (n, C_QA // 1024)),
        pl.BlockSpec((b, LANES), lambda n: (prev(n), C_KA // LANES)),
        pl.BlockSpec((b, LANES), lambda n: (n, C_KA // LANES)),
        pl.BlockSpec((b, LANES), lambda n: (prev(n), C_VA // LANES)),
        pl.BlockSpec((b, LANES), lambda n: (n, C_VA // LANES)),
        pl.BlockSpec((b, 1), lambda n: (n, 0)),
        pl.BlockSpec((1, b), lambda n: (0, prev(n))),
        pl.BlockSpec((1, b), lambda n: (0, n)),
        pl.BlockSpec((1, LANES), lambda n: (0, 0)),
        pl.BlockSpec((1, LANES), lambda n: (0, 0)),
        pl.BlockSpec(memory_space=pltpu.SMEM),
    ]


def _swa_fwd(proj, posc, posr, gq, gk, sinks):
    s = proj.shape[0]
    b = SWA_BLOCK
    scale = SWA_DIM ** -0.5

    def body(q_ref, kp_ref, kc_ref, vp_ref, vc_ref, pq_ref, pkp_ref, pkc_ref, gq_ref, gk_ref, sink_ref,
             y_ref, lse_ref):
        n = pl.program_id(0)
        kn, v, bias = _swa_common(n, kp_ref[...], kc_ref[...], vp_ref[...], vc_ref[...],
                                  pq_ref[...], pkp_ref[...], pkc_ref[...], gk_ref[...])
        lo = _lo_mask((b, LANES))
        col = lax.broadcasted_iota(jnp.int32, (b, SWA_Q_HEADS), 1)
        lse_t = jnp.zeros((b, SWA_Q_HEADS), jnp.float32)
        for j in range(SWA_Q_HEADS // 2):
            hk = (2 * j) // (SWA_Q_HEADS // SWA_KV_HEADS)
            kvmask = lo if hk == 0 else jnp.logical_not(lo)
            qn, _, _ = _norm_fwd(q_ref[:, LANES * j: LANES * (j + 1)], gq_ref[...], half=True)
            qn = qn * scale
            qsw = pltpu.roll(qn, 64, 1)
            outs = []
            for e in range(2):
                h = 2 * j + e
                qm = jnp.where(kvmask, qn if e == hk else qsw, 0.0)
                sc = _dot_nt(qm, kn) + _alibi_slope(h) * bias
                sk = sink_ref[h]
                m = jnp.maximum(jnp.max(sc, -1, keepdims=True), sk)
                p = jnp.exp(sc - m)
                l = jnp.sum(p, -1, keepdims=True) + jnp.exp(sk - m)
                o = _dot(p, v) / l
                outs.append(o if e == hk else pltpu.roll(o, 64, 1))
                lse_t = jnp.where(col == h, m + jnp.log(l), lse_t)
            y_ref[:, LANES * j: LANES * (j + 1)] = jnp.where(lo, outs[0], outs[1])
        lse_ref[...] = lse_t

    return _pcall(
        body, name="swa_fwd", grid=(s // b,),
        out_shape=[_sds((s, 1024), jnp.float32), _sds((s, SWA_Q_HEADS), jnp.float32)],
        in_specs=_swa_specs(s),
        out_specs=[pl.BlockSpec((b, 1024), lambda n: (n, 0)), pl.BlockSpec((b, SWA_Q_HEADS), lambda n: (n, 0))],
        sem=("parallel",))(proj, proj, proj, proj, proj, posc, posr, posr, gq, gk, sinks)


def _out_proj(y_a, y_b, y_m, x, w_out, g_ffn):
    s, d = x.shape
    tm = min(ROW_TILE, s)

    def body(ya_ref, yb_ref, ym_ref, x_ref, w_ref, g_ref, h1_ref, fn_ref):
        y = jnp.concatenate([ya_ref[...].astype(MXU), yb_ref[...].astype(MXU), ym_ref[...].astype(MXU)], axis=1)
        h1 = x_ref[...] + _dot(y, w_ref[...])
        h1_ref[...] = h1
        fn, _, _ = _norm_fwd(h1, g_ref[...])
        fn_ref[...] = fn.astype(fn_ref.dtype)

    def row(width):
        return pl.BlockSpec((tm, width), lambda i: (i, 0))

    return _pcall(
        body, name="out_proj", grid=(s // tm,),
        out_shape=[_sds((s, d), jnp.float32), _sds((s, d), MXU)],
        in_specs=[row(1024), row(512), row(512), row(d), pl.BlockSpec(w_out.shape, lambda i: (0, 0)),
                  pl.BlockSpec((1, d), lambda i: (0, 0))],
        out_specs=[row(d), row(d)], sem=("parallel",))(y_a, y_b, y_m, x, w_out, g_ffn)


def _ffn_gu(fn, w_gu):
    s, d = fn.shape
    f = w_gu.shape[2]
    tm = min(FFN_TILE, s)

    def body(fn_ref, w_ref, gu_ref, act_ref):
        x = fn_ref[...]
        g = _dot_nt(x, w_ref[0, 0])
        u = _dot_nt(x, w_ref[0, 1])
        gu_ref[0, 0] = g
        gu_ref[0, 1] = u
        act_ref[0] = (g * jax.nn.sigmoid(g) * u).astype(act_ref.dtype)

    return _pcall(
        body, name="ffn_gate_up", grid=(N_DEV, s // tm),
        out_shape=[_sds((N_DEV, 2, s, f), jnp.float32), _sds((N_DEV, s, f), MXU)],
        in_specs=[pl.BlockSpec((tm, d), lambda j, i: (i, 0)),
                  pl.BlockSpec((1, 2, f, d), lambda j, i: (j, 0, 0, 0))],
        out_specs=[pl.BlockSpec((1, 2, tm, f), lambda j, i: (j, 0, i, 0)),
                   pl.BlockSpec((1, tm, f), lambda j, i: (j, i, 0))],
        sem=("parallel", "parallel"))(fn, w_gu)


def _ffn_down(act, w_d, h1, target):
    _, s, f = act.shape
    d = h1.shape[1]
    tm = min(FFN_TILE, s)

    def body(a_ref, w_ref, h1_ref, t_ref, dout_ref, doutb_ref, loss_ref, acc):
        i, j = pl.program_id(0), pl.program_id(1)
        part = _dot(a_ref[0], w_ref[0]) + _dot(a_ref[1], w_ref[1])

        @pl.when(j == 0)
        def _():
            acc[...] = h1_ref[...] + part

        @pl.when(j > 0)
        def _():
            acc[...] += part

        @pl.when((i == 0) & (j == 0))
        def _():
            loss_ref[...] = jnp.zeros_like(loss_ref)

        @pl.when(j == N_DEV // 2 - 1)
        def _():
            diff = acc[...] - t_ref[...]
            dout_ref[...] = diff / d
            doutb_ref[...] = (diff / d).astype(doutb_ref.dtype)
            loss_ref[...] += 0.5 * jnp.sum(jnp.sum(diff * diff, -1, keepdims=True) / d)

    row = pl.BlockSpec((tm, d), lambda i, j: (i, 0))
    return _pcall(
        body, name="ffn_down", grid=(s // tm, N_DEV // 2),
        out_shape=[_sds((s, d), jnp.float32), _sds((s, d), MXU), _sds((8, LANES), jnp.float32)],
        in_specs=[pl.BlockSpec((2, tm, f), lambda i, j: (j, i, 0)), pl.BlockSpec((2, f, d), lambda i, j: (j, 0, 0)),
                  row, row],
        out_specs=[row, row, pl.BlockSpec((8, LANES), lambda i, j: (0, 0))],
        scratch=[pltpu.VMEM((tm, d), jnp.float32)], sem=("arbitrary", "arbitrary"))(act, w_d, h1, target)


def _ffn_bwd_act(dout, w_d, gu):
    s, d = dout.shape
    f = w_d.shape[1]
    tm = min(FFN_TILE, s)
    ni = s // tm

    def body(do_ref, w_ref, gu_ref, dgu_ref, dw_ref, acc):
        i = pl.program_id(1)
        do = do_ref[...]
        d_act = _dot_nt(do, w_ref[0])
        g, u = gu_ref[0, 0], gu_ref[0, 1]
        sig = jax.nn.sigmoid(g)
        silu = g * sig
        dgu_ref[0, 0] = (d_act * u * (sig * (1.0 + g * (1.0 - sig)))).astype(dgu_ref.dtype)
        dgu_ref[0, 1] = (d_act * silu).astype(dgu_ref.dtype)
        part = _dot_tn(silu * u, do)

        @pl.when(i == 0)
        def _():
            acc[...] = part

        @pl.when(i > 0)
        def _():
            acc[...] += part

        @pl.when(i == ni - 1)
        def _():
            dw_ref[0] = acc[...].astype(dw_ref.dtype)

    return _pcall(
        body, name="ffn_bwd_act", grid=(N_DEV, ni),
        out_shape=[_sds((N_DEV, 2, s, f), MXU), _sds((N_DEV, f, d), WIRE)],
        in_specs=[pl.BlockSpec((tm, d), lambda j, i: (i, 0)), pl.BlockSpec((1, f, d), lambda j, i: (j, 0, 0)),
                  pl.BlockSpec((1, 2, tm, f), lambda j, i: (j, 0, i, 0))],
        out_specs=[pl.BlockSpec((1, 2, tm, f), lambda j, i: (j, 0, i, 0)),
                   pl.BlockSpec((1, f, d), lambda j, i: (j, 0, 0))],
        scratch=[pltpu.VMEM((f, d), jnp.float32)], sem=("parallel", "arbitrary"))(dout, w_d, gu)


def _ffn_dw_gu(fn, dgu):
    s, d = fn.shape
    f = dgu.shape[-1]
    tk = min(2 * FFN_TILE, s)
    nk = s // tk

    def body(fn_ref, dgu_ref, dw_ref, acc):
        k = pl.program_id(1)
        x = fn_ref[...]
        pg = _dot_tn(dgu_ref[0, 0], x)
        pu = _dot_tn(dgu_ref[0, 1], x)

        @pl.when(k == 0)
        def _():
            acc[0] = pg
            acc[1] = pu

        @pl.when(k > 0)
        def _():
            acc[0] += pg
            acc[1] += pu

        @pl.when(k == nk - 1)
        def _():
            dw_ref[0] = acc[...].astype(dw_ref.dtype)

    return _pcall(
        body, name="ffn_dw_gate_up", grid=(N_DEV, nk),
        out_shape=_sds((N_DEV, 2, f, d), WIRE),
        in_specs=[pl.BlockSpec((tk, d), lambda j, k: (k, 0)), pl.BlockSpec((1, 2, tk, f), lambda j, k: (j, 0, k, 0))],
        out_specs=pl.BlockSpec((1, 2, f, d), lambda j, k: (j, 0, 0, 0)),
        scratch=[pltpu.VMEM((2, f, d), jnp.float32)], sem=("parallel", "arbitrary"))(fn, dgu)


def _ffn_dfn(dgu, w_gu, after):
    _, _, s, f = dgu.shape
    d = w_gu.shape[3]
    tm = min(FFN_TILE, s)

    def body(dgu_ref, w_ref, dfn_ref):
        j = pl.program_id(1)
        part = (_dot(dgu_ref[0, 0], w_ref[0, 0]) + _dot(dgu_ref[0, 1], w_ref[0, 1])
                + _dot(dgu_ref[1, 0], w_ref[1, 0]) + _dot(dgu_ref[1, 1], w_ref[1, 1]))

        @pl.when(j == 0)
        def _():
            dfn_ref[...] = part

        @pl.when(j > 0)
        def _():
            dfn_ref[...] += part

    return _pcall(
        body, name="ffn_dfn", grid=(s // tm, N_DEV // 2),
        out_shape=_sds((s, d), jnp.float32),
        in_specs=[pl.BlockSpec((2, 2, tm, f), lambda i, j: (j, 0, i, 0)),
                  pl.BlockSpec((2, 2, f, d), lambda i, j: (j, 0, 0, 0))],
        out_specs=pl.BlockSpec((tm, d), lambda i, j: (i, 0)),
        sem=("parallel", "arbitrary"), after=after)(dgu, w_gu)


def _ffn_norm_bwd(d_fn, dout, h1, g_ffn):
    s, d = h1.shape
    tm = min(ROW_TILE, s)

    def body(dfn_ref, do_ref, h1_ref, g_ref, dh1_ref, dg_ref):
        i = pl.program_id(0)

        @pl.when(i == 0)
        def _():
            dg_ref[...] = jnp.zeros_like(dg_ref)

        _, xn, r = _norm_fwd(h1_ref[...], g_ref[...])
        dx, dg = _norm_bwd(xn, r, g_ref[...], dfn_ref[...])
        dh1_ref[...] = do_ref[...] + dx
        dg_ref[...] += dg

    row = pl.BlockSpec((tm, d), lambda i: (i, 0))
    vec = pl.BlockSpec((1, d), lambda i: (0, 0))
    return _pcall(
        body, name="ffn_norm_bwd", grid=(s // tm,),
        out_shape=[_sds((s, d), jnp.float32), _sds((1, d), jnp.float32)],
        in_specs=[row, row, row, vec], out_specs=[row, vec], sem=("arbitrary",))(d_fn, dout, h1, g_ffn)


def _mem_bwd(proj, g_mq, km, vmm, d_y, y_m, lse):
    s = proj.shape[0]
    ml, hw = km.shape
    tm = min(FFN_TILE, s)
    scale = MEM_DIM ** -0.5

    def body(q_ref, g_ref, k_ref, v_ref, do_ref, y_ref, lse_ref, dq_ref, dk_ref, dv_ref, dg_ref):
        i = pl.program_id(0)

        @pl.when(i == 0)
        def _():
            dk_ref[...] = jnp.zeros_like(dk_ref)
            dv_ref[...] = jnp.zeros_like(dv_ref)
            dg_ref[...] = jnp.zeros_like(dg_ref)

        col = lax.broadcasted_iota(jnp.int32, (tm, MEM_HEADS), 1)
        lse_t = lse_ref[...]
        for h in range(MEM_HEADS):
            sl = slice(MEM_DIM * h, MEM_DIM * (h + 1))
            qn, xn, r = _norm_fwd(q_ref[:, sl], g_ref[...])
            lse_h = jnp.sum(jnp.where(col == h, lse_t, 0.0), -1, keepdims=True)
            p = jnp.exp(_dot_nt(qn, k_ref[:, sl]) * scale - lse_h)
            do = do_ref[:, sl]
            dd = jnp.sum(do * y_ref[:, sl], -1, keepdims=True)
            dp = _dot_nt(do, v_ref[:, sl])
            ds = (p * (dp - dd)).astype(MXU)
            dv_ref[:, sl] += _dot_tn(p, do)
            dk_ref[:, sl] += _dot_tn(ds, qn) * scale
            dx, dg = _norm_bwd(xn, r, g_ref[...], _dot(ds, k_ref[:, sl]) * scale)
            dq_ref[:, sl] = dx.astype(dq_ref.dtype)
            dg_ref[...] += dg

    full = pl.BlockSpec((ml, hw), lambda i: (0, 0))
    return _pcall(
        body, name="mem_bwd", grid=(s // tm,),
        out_shape=[_sds((s, hw), MXU), _sds((ml, hw), jnp.float32), _sds((ml, hw), jnp.float32),
                   _sds((1, MEM_DIM), jnp.float32)],
        in_specs=[pl.BlockSpec((tm, hw), lambda i: (i, C_QM // hw)), pl.BlockSpec((1, MEM_DIM), lambda i: (0, 0)),
                  full, full, pl.BlockSpec((tm, hw), lambda i: (i, 3)), pl.BlockSpec((tm, hw), lambda i: (i, 0)),
                  pl.BlockSpec((tm, MEM_HEADS), lambda i: (i, 0))],
        out_specs=[pl.BlockSpec((tm, hw), lambda i: (i, 0)), full, full,
                   pl.BlockSpec((1, MEM_DIM), lambda i: (0, 0))],
        sem=("arbitrary",))(proj, g_mq, km, vmm, d_y, y_m, lse)


def _memkv_bwd(mem, g_mem, w_mkv, g_mk, kv, memn, dk, dv):
    ml, d = mem.shape
    hw = MEM_HEADS * MEM_DIM

    def body(mem_ref, g_ref, w_ref, gk_ref, kv_ref, mn_ref, dk_ref, dv_ref, dw_ref, dgm_ref, dgk_ref):
        parts = []
        dgk = jnp.zeros((1, MEM_DIM), jnp.float32)
        for h in range(MEM_HEADS):
            sl = slice(MEM_DIM * h, MEM_DIM * (h + 1))
            _, xn, r = _norm_fwd(kv_ref[:, sl], gk_ref[...])
            dx, dg = _norm_bwd(xn, r, gk_ref[...], dk_ref[:, sl])
            parts.append(dx)
            dgk = dgk + dg
        dkv = jnp.concatenate(parts + [dv_ref[...]], axis=1).astype(MXU)
        dgk_ref[...] = dgk
        dw_ref[...] = _dot_tn(mn_ref[...], dkv).astype(dw_ref.dtype)
        d_mn = _dot_nt(dkv, w_ref[...])
        _, xn, _ = _norm_fwd(mem_ref[...], g_ref[...])
        dgm_ref[...] = jnp.sum(d_mn * xn, 0, keepdims=True)

    vm = pl.BlockSpec(memory_space=pltpu.VMEM)
    return _pcall(
        body, name="memkv_bwd",
        out_shape=[_sds((d, 2 * hw), WIRE), _sds((1, d), jnp.float32), _sds((1, MEM_DIM), jnp.float32)],
        in_specs=[vm] * 8, out_specs=[vm] * 3)(mem, g_mem, w_mkv, g_mk, kv, memn, dk, dv)


def _mla_bwd(qc, kc, v, d_y, y_b, lse, after):
    nh, s, _ = qc.shape
    t = min(ATT_TILE, s)
    nb = s // t
    scale = (MLA_NOPE + MLA_ROPE) ** -0.5

    def body(q_ref, k_ref, v_ref, do_ref, y_ref, lse_ref, dq_ref, dk_ref, dv_ref, dk_acc, dv_acc):
        kj, qi = pl.program_id(1), pl.program_id(2)

        @pl.when((kj == 0) & (qi == 0))
        def _():
            dq_ref[...] = jnp.zeros_like(dq_ref)

        @pl.when(qi == kj)
        def _():
            dk_acc[...] = jnp.zeros_like(dk_acc)
            dv_acc[...] = jnp.zeros_like(dv_acc)

        def step(diagonal):
            q, k = q_ref[0], k_ref[0]
            sc = _dot_nt(q, k) * (scale * LOG2E)
            if diagonal:
                r_i = lax.broadcasted_iota(jnp.int32, sc.shape, 0)
                c_i = lax.broadcasted_iota(jnp.int32, sc.shape, 1)
                sc = jnp.where(c_i <= r_i, sc, NEG_INF)
            p = jnp.exp2(sc - lse_ref[0])
            do = do_ref[...]
            dd = jnp.sum(do * y_ref[...], -1, keepdims=True)
            dp = _dot_nt(do, v_ref[0])
            ds = (p * (dp - dd) * scale).astype(MXU)
            dv_acc[...] += _dot_tn(p, do)
            dk_acc[...] += _dot_tn(ds, q)
            rows = pl.ds(pl.multiple_of(qi * t, t), t)
            dq_ref[0, rows, :] += _dot(ds, k)

        @pl.when(qi > kj)
        def _():
            step(False)

        @pl.when(qi == kj)
        def _():
            step(True)

        @pl.when(qi == nb - 1)
        def _():
            dk_ref[0] = dk_acc[...]
            dv_ref[0] = dv_acc[...]

    qmap = lambda h, j, i: (h, jnp.maximum(i, j), 0)
    return _pcall(
        body, name="mla_bwd", grid=(nh, nb, nb),
        out_shape=[_sds((nh, s, 256), jnp.float32), _sds((nh, s, 256), jnp.float32),
                   _sds((nh, s, MLA_V), jnp.float32)],
        in_specs=[pl.BlockSpec((1, t, 256), qmap),
                  pl.BlockSpec((1, t, 256), lambda h, j, i: (h, j, 0)),
                  pl.BlockSpec((1, t, MLA_V), lambda h, j, i: (h, j, 0)),
                  pl.BlockSpec((t, MLA_V), lambda h, j, i: (jnp.maximum(i, j), 8 + h)),
                  pl.BlockSpec((t, MLA_V), lambda h, j, i: (jnp.maximum(i, j), h)),
                  pl.BlockSpec((1, t, 1), qmap)],
        out_specs=[pl.BlockSpec((1, s, 256), lambda h, j, i: (h, 0, 0)),
                   pl.BlockSpec((1, t, 256), lambda h, j, i: (h, j, 0)),
                   pl.BlockSpec((1, t, MLA_V), lambda h, j, i: (h, j, 0))],
        scratch=[pltpu.VMEM((t, 256), jnp.float32), pltpu.VMEM((t, MLA_V), jnp.float32)],
        sem=("parallel", "arbitrary", "arbitrary"), after=after)(qc, kc, v, d_y, y_b, lse)


def _mla_prep_bwd(proj, cos, sin, g_cq, g_ckv, w_uq, w_ukv, g_qn, g_qr, g_kn, g_kr,
                  qb, kvb, cqn, ckvn, dqc, dkc, dv):
    s = proj.shape[0]
    tm = min(ROW_TILE, s)
    nh = MLA_HEADS
    ni = s // tm

    def body(cq_ref, ckv_ref, kr_ref, cos_ref, sin_ref, gcq_ref, gckv_ref, wuq_ref, wukv_ref,
             gqn_ref, gqr_ref, gkn_ref, gkr_ref, qb_ref, kvb_ref, cqn_ref, ckvn_ref, dqc_ref, dkc_ref, dv_ref,
             dcq_ref, dckv_ref, dkr_ref, dwuq_ref, dwukv_ref,
             dgcq_ref, dgckv_ref, dgqn_ref, dgqr_ref, dgkn_ref, dgkr_ref, acc_uq, acc_ukv):
        i = pl.program_id(0)

        @pl.when(i == 0)
        def _():
            acc_uq[...] = jnp.zeros_like(acc_uq)
            acc_ukv[...] = jnp.zeros_like(acc_ukv)
            for ref in (dgcq_ref, dgckv_ref, dgqn_ref, dgqr_ref, dgkn_ref, dgkr_ref):
                ref[...] = jnp.zeros_like(ref)

        cos_t, sin_t = cos_ref[...], sin_ref[...]
        lo = _lo_mask((tm, LANES))
        qb_v, kvb_v = qb_ref[...], kvb_ref[...]
        dq_parts, dgqn = [], jnp.zeros((1, LANES), jnp.float32)
        for h in range(nh):
            _, xn, r = _norm_fwd(qb_v[:, MLA_NOPE * h: MLA_NOPE * (h + 1)], gqn_ref[...])
            dx, dg = _norm_bwd(xn, r, gqn_ref[...], dqc_ref[h][:, :MLA_NOPE])
            dq_parts.append(dx)
            dgqn = dgqn + dg
        dgqn_ref[...] += dgqn
        dgqr = jnp.zeros((1, LANES), jnp.float32)
        for j in range(nh // 2):
            d_rope = jnp.where(lo, dqc_ref[2 * j][:, MLA_NOPE:], dqc_ref[2 * j + 1][:, MLA_NOPE:])
            d_pre = _rope_bwd(d_rope, cos_t, sin_t)
            xr = qb_v[:, nh * MLA_NOPE + LANES * j: nh * MLA_NOPE + LANES * (j + 1)]
            _, xn, r = _norm_fwd(xr, gqr_ref[...], half=True)
            dx, dg = _norm_bwd(xn, r, gqr_ref[...], d_pre, half=True)
            dq_parts.append(dx)
            dgqr = dgqr + dg
        dgqr_ref[...] += dgqr
        dqb = jnp.concatenate(dq_parts, axis=1).astype(MXU)
        acc_uq[...] += _dot_tn(dqb, cqn_ref[...])
        _, xn, r = _norm_fwd(cq_ref[...], gcq_ref[...])
        dx, dg = _norm_bwd(xn, r, gcq_ref[...], _dot(dqb, wuq_ref[...]))
        dcq_ref[...] = dx.astype(dcq_ref.dtype)
        dgcq_ref[...] += dg
        dkv_parts, dgkn = [], jnp.zeros((1, LANES), jnp.float32)
        d_kr2 = jnp.zeros((tm, LANES), jnp.float32)
        for h in range(nh):
            _, xn, r = _norm_fwd(kvb_v[:, 256 * h: 256 * h + MLA_NOPE], gkn_ref[...])
            dx, dg = _norm_bwd(xn, r, gkn_ref[...], dkc_ref[h][:, :MLA_NOPE])
            dkv_parts += [dx, dv_ref[h]]
            dgkn = dgkn + dg
            d_kr2 = d_kr2 + dkc_ref[h][:, MLA_NOPE:]
        dgkn_ref[...] += dgkn
        dkvb = jnp.concatenate(dkv_parts, axis=1).astype(MXU)
        d_ckvn = jnp.zeros((tm, 512), jnp.float32)
        for dev in range(N_DEV):
            piece = dkvb[:, LANES * dev: LANES * (dev + 1)]
            acc_ukv[dev] += _dot_tn(ckvn_ref[...], piece)
            d_ckvn = d_ckvn + _dot_nt(piece, wukv_ref[dev])
        _, xn, r = _norm_fwd(ckv_ref[...], gckv_ref[...])
        dx, dg = _norm_bwd(xn, r, gckv_ref[...], d_ckvn)
        dckv_ref[...] = dx.astype(dckv_ref.dtype)
        dgckv_ref[...] += dg
        d_kr = jnp.where(lo, d_kr2 + pltpu.roll(d_kr2, 64, 1), 0.0)
        d_pre = _rope_bwd(d_kr, cos_t, sin_t)
        _, xn, r = _norm_fwd(kr_ref[...], gkr_ref[...], half=True)
        dx, dg = _norm_bwd(xn, r, gkr_ref[...], d_pre, half=True)
        dkr_ref[...] = jnp.where(lo, dx, 0.0).astype(dkr_ref.dtype)
        dgkr_ref[...] += jnp.where(_lo_mask((1, LANES)), dg, 0.0)

        @pl.when(i == ni - 1)
        def _():
            dwuq_ref[...] = acc_uq[...].astype(dwuq_ref.dtype)
            dwukv_ref[...] = acc_ukv[...].astype(dwukv_ref.dtype)

    def col(width, start):
        return pl.BlockSpec((tm, width), lambda i: (i, start // width))

    def full(shape):
        return pl.BlockSpec(shape, lambda i: (0,) * len(shape))

    def row(width):
        return pl.BlockSpec((tm, width), lambda i: (i, 0))

    def heads(width):
        return pl.BlockSpec((nh, tm, width), lambda i: (0, i, 0))

    vec = full((1, LANES))
    return _pcall(
        body, name="mla_prep_bwd", grid=(ni,),
        out_shape=[_sds((s, 512), MXU), _sds((s, 512), MXU), _sds((s, LANES), MXU),
                   _sds((768, 512), WIRE), _sds((N_DEV, 512, LANES), WIRE),
                   _sds((1, 512), jnp.float32), _sds((1, 512), jnp.float32)] + [_sds((1, LANES), jnp.float32)] * 4,
        in_specs=[col(512, C_CQ), col(512, C_CKV), col(LANES, C_KR), row(LANES), row(LANES),
                  full((1, 512)), full((1, 512)), full((768, 512)), full((N_DEV, 512, LANES)), vec, vec, vec, vec,
                  row(768), row(1024), row(512), row(512), heads(256), heads(256), heads(MLA_V)],
        out_specs=[row(512), row(512), row(LANES), full((768, 512)), full((N_DEV, 512, LANES)),
                   full((1, 512)), full((1, 512)), vec, vec, vec, vec],
        scratch=[pltpu.VMEM((768, 512), jnp.float32), pltpu.VMEM((N_DEV, 512, LANES), jnp.float32)],
        sem=("arbitrary",))(proj, proj, proj, cos, sin, g_cq, g_ckv, w_uq, w_ukv, g_qn, g_qr, g_kn, g_kr,
                            qb, kvb, cqn, ckvn, dqc, dkc, dv)


def _swa_bwd(proj, posc, posr, gq, gk, sinks, d_y, y_a, lse, after):
    s = proj.shape[0]
    b = SWA_BLOCK
    nb = s // b
    scale = SWA_DIM ** -0.5

    def body(q_ref, kp_ref, kc_ref, vp_ref, vc_ref, pq_ref, pkp_ref, pkc_ref, gq_ref, gk_ref, sink_ref,
             do_ref, y_ref, lse_ref, kfull_ref,
             dq_ref, dk_ref, dv_ref, dgq_ref, dgk_ref, dsink_ref, dk_acc, dv_acc):
        n = pl.program_id(0)

        @pl.when(n == 0)
        def _():
            dk_acc[...] = jnp.zeros_like(dk_acc)
            dv_acc[...] = jnp.zeros_like(dv_acc)
            dgq_ref[...] = jnp.zeros_like(dgq_ref)
            dsink_ref[...] = jnp.zeros_like(dsink_ref)

        kn, v, bias = _swa_common(n, kp_ref[...], kc_ref[...], vp_ref[...], vc_ref[...],
                                  pq_ref[...], pkp_ref[...], pkc_ref[...], gk_ref[...])
        lo = _lo_mask((b, LANES))
        col = lax.broadcasted_iota(jnp.int32, (b, SWA_Q_HEADS), 1)
        col1 = lax.broadcasted_iota(jnp.int32, (1, SWA_Q_HEADS), 1)
        lse_t = lse_ref[...]
        dk_blk = jnp.zeros((2 * b, LANES), jnp.float32)
        dv_blk = jnp.zeros((2 * b, LANES), jnp.float32)
        dgq = jnp.zeros((1, LANES), jnp.float32)
        dsink = jnp.zeros((1, SWA_Q_HEADS), jnp.float32)
        for j in range(SWA_Q_HEADS // 2):
            hk = (2 * j) // (SWA_Q_HEADS // SWA_KV_HEADS)
            kvmask = lo if hk == 0 else jnp.logical_not(lo)
            sl = slice(LANES * j, LANES * (j + 1))
            qn, xn, r = _norm_fwd(q_ref[:, sl], gq_ref[...], half=True)
            qn = qn * scale
            qsw = pltpu.roll(qn, 64, 1)
            d2 = do_ref[:, sl]
            d2sw = pltpu.roll(d2, 64, 1)
            prod = d2 * y_ref[:, sl]
            dqs = []
            for e in range(2):
                h = 2 * j + e
                half_e = lo if e == 0 else jnp.logical_not(lo)
                qm = jnp.where(kvmask, qn if e == hk else qsw, 0.0)
                dm = jnp.where(kvmask, d2 if e == hk else d2sw, 0.0)
                sc = _dot_nt(qm, kn) + _alibi_slope(h) * bias
                lse_h = jnp.sum(jnp.where(col == h, lse_t, 0.0), -1, keepdims=True)
                p = jnp.exp(sc - lse_h)
                dd = jnp.sum(jnp.where(half_e, prod, 0.0), -1, keepdims=True)
                dp = _dot_nt(dm, v)
                ds = (p * (dp - dd)).astype(MXU)
                dsink = dsink - jnp.where(col1 == h, jnp.sum(jnp.exp(sink_ref[h] - lse_h) * dd), 0.0)
                dq_m = _dot(ds, kn) * scale
                dk_blk = dk_blk + _dot_tn(ds, qm)
                dv_blk = dv_blk + _dot_tn(p, dm)
                dqs.append(dq_m if e == hk else pltpu.roll(dq_m, 64, 1))
            dx, dg = _norm_bwd(xn, r, gq_ref[...], jnp.where(lo, dqs[0], dqs[1]), half=True)
            dq_ref[:, sl] = dx.astype(dq_ref.dtype)
            dgq = dgq + dg
        dgq_ref[...] += dgq
        dsink_ref[...] += dsink
        prev = pl.ds(pl.multiple_of(jnp.maximum(n - 1, 0) * b, b), b)
        cur = pl.ds(pl.multiple_of(n * b, b), b)
        dk_acc[prev, :] += dk_blk[:b]
        dv_acc[prev, :] += dv_blk[:b]
        dk_acc[cur, :] += dk_blk[b:]
        dv_acc[cur, :] += dv_blk[b:]

        @pl.when(n == nb - 1)
        def _():
            _, kxn, kr = _norm_fwd(kfull_ref[...], gk_ref[...], half=True)
            dx, dg = _norm_bwd(kxn, kr, gk_ref[...], dk_acc[...], half=True)
            dk_ref[...] = dx.astype(dk_ref.dtype)
            dv_ref[...] = dv_acc[...].astype(dv_ref.dtype)
            dgk_ref[...] = dg

    full = pl.BlockSpec((s, LANES), lambda n: (0, 0))
    vec = pl.BlockSpec((1, LANES), lambda n: (0, 0))
    return _pcall(
        body, name="swa_bwd", grid=(nb,),
        out_shape=[_sds((s, 1024), MXU), _sds((s, LANES), MXU), _sds((s, LANES), MXU),
                   _sds((1, LANES), jnp.float32), _sds((1, LANES), jnp.float32),
                   _sds((1, SWA_Q_HEADS), jnp.float32)],
        in_specs=_swa_specs(s) + [pl.BlockSpec((b, 1024), lambda n: (n, 0)), pl.BlockSpec((b, 1024), lambda n: (n, 0)),
                                  pl.BlockSpec((b, SWA_Q_HEADS), lambda n: (n, 0)),
                                  pl.BlockSpec((s, LANES), lambda n: (0, C_KA // LANES))],
        out_specs=[pl.BlockSpec((b, 1024), lambda n: (n, 0)), full, full, vec, vec,
                   pl.BlockSpec((1, SWA_Q_HEADS), lambda n: (0, 0))],
        scratch=[pltpu.VMEM((s, LANES), jnp.float32), pltpu.VMEM((s, LANES), jnp.float32)],
        sem=("arbitrary",), after=after)(proj, proj, proj, proj, proj, posc, posr, posr, gq, gk, sinks, d_y, y_a, lse,
                                         proj)


def _dx(d_proj, w_in, x, g, d_h1, after):
    s, d = x.shape
    n = w_in.shape[0]
    tm = min(ROW_TILE, s)

    def body(dp_ref, w_ref, x_ref, g_ref, dh_ref, dx_ref, dg_ref):
        i = pl.program_id(0)

        @pl.when(i == 0)
        def _():
            dg_ref[...] = jnp.zeros_like(dg_ref)

        d_hn = _dot(dp_ref[...], w_ref[...])
        _, xn, r = _norm_fwd(x_ref[...], g_ref[...])
        dx, dg = _norm_bwd(xn, r, g_ref[...], d_hn)
        dx_ref[...] = dh_ref[...] + dx
        dg_ref[...] += dg

    row = pl.BlockSpec((tm, d), lambda i: (i, 0))
    vec = pl.BlockSpec((1, d), lambda i: (0, 0))
    return _pcall(
        body, name="grad_x", grid=(s // tm,),
        out_shape=[_sds((s, d), jnp.float32), _sds((1, d), jnp.float32)],
        in_specs=[pl.BlockSpec((tm, n), lambda i: (i, 0)), pl.BlockSpec((n, d), lambda i: (0, 0)), row, vec, row],
        out_specs=[row, vec], sem=("arbitrary",), after=after)(d_proj, w_in, x, g, d_h1)


_SMALL = ["attn_norm_g", "swa_q_norm_g", "swa_k_norm_g", "swa_sinks", "mla_cq_norm_g", "mla_ckv_norm_g",
          "mla_qn_norm_g", "mla_qr_norm_g", "mla_kn_norm_g", "mla_kr_norm_g", "mem_norm_g",
          "mem_q_norm_g", "mem_k_norm_g", "ffn_norm_g"]


def kernel(x, mem, positions, attn_norm_g, w_in, swa_q_norm_g, swa_k_norm_g, swa_sinks, mla_cq_norm_g, mla_ckv_norm_g, w_uq, w_ukv, mla_qn_norm_g, mla_qr_norm_g, mla_kn_norm_g, mla_kr_norm_g, mem_norm_g, w_mem_kv, mem_q_norm_g, mem_k_norm_g, w_out, ffn_norm_g, w_gate, w_up, w_down, loss_target, m_attn_norm_g, m_w_in, m_swa_q_norm_g, m_swa_k_norm_g, m_swa_sinks, m_mla_cq_norm_g, m_mla_ckv_norm_g, m_w_uq, m_w_ukv, m_mla_qn_norm_g, m_mla_qr_norm_g, m_mla_kn_norm_g, m_mla_kr_norm_g, m_mem_norm_g, m_w_mem_kv, m_mem_q_norm_g, m_mem_k_norm_g, m_w_out, m_ffn_norm_g, m_w_gate, m_w_up, m_w_down, v_attn_norm_g, v_w_in, v_swa_q_norm_g, v_swa_k_norm_g, v_swa_sinks, v_mla_cq_norm_g, v_mla_ckv_norm_g, v_w_uq, v_w_ukv, v_mla_qn_norm_g, v_mla_qr_norm_g, v_mla_kn_norm_g, v_mla_kr_norm_g, v_mem_norm_g, v_w_mem_kv, v_mem_q_norm_g, v_mem_k_norm_g, v_w_out, v_ffn_norm_g, v_w_gate, v_w_up, v_w_down):
    args = dict(locals())
    x2, mem2, tgt = x[0], mem[0], loss_target[0]
    s, d = x2.shape
    n_in = w_in.shape[2]
    f = w_gate.shape[2]

    (g_in,) = _all_gather([w_in[0].T.astype(WIRE)])
    mix_shards = [w_uq[0].T.astype(WIRE), w_ukv[0].astype(WIRE), w_mem_kv[0].astype(WIRE),
                  _to_wire([w_out[0]], g_in, "wire_out")[0]]
    g_uq, wkv, g_mkv, g_out = _all_gather_background(mix_shards, 5, "all_gather_mix_weights")
    ffn_shards = [_to_wire([w_gate[0].T, w_up[0].T], g_in, "wire_gate_up"),
                  _to_wire([w_down[0]], g_in, "wire_down")[0]]
    w_gu, w_d = _all_gather_background(ffn_shards, 1, "all_gather_ffn_weights")
    wi = g_in.reshape(N_DEV * n_in, d)
    wi = jnp.concatenate([wi[0:1024], wi[1280:1792], wi[1792:2304], wi[2368:2880],
                          wi[1024:1152], wi[1152:1280], wi[2304:2368],
                          jnp.zeros((IN_PAD - 2880, d), wi.dtype)], axis=0)
    wq = g_uq.reshape(768, 512)
    wq = jnp.concatenate([wq[192 * h: 192 * h + 128] for h in range(4)]
                         + [wq[192 * h + 128: 192 * (h + 1)] for h in range(4)], axis=0)
    wmkv = g_mkv.reshape(-1, g_mkv.shape[-1])
    wo = g_out.reshape(-1, d)

    pos = positions[0].astype(jnp.float32)
    inv_freq = ROPE_THETA ** (-jnp.arange(0, MLA_ROPE, 2, dtype=jnp.float32) / MLA_ROPE)
    ang = pos[:, None] * inv_freq
    cos32, sin32 = jnp.cos(ang), jnp.sin(ang)
    cos_t = jnp.tile(cos32, (1, 4))
    sin_t = jnp.tile(jnp.concatenate([-sin32, sin32], axis=1), (1, 2))
    posc, posr = pos.reshape(s, 1), pos.reshape(1, s)
    two = lambda g: jnp.tile(g, (1, 2))
    gq2, gk2, gqr2, gkr2 = two(swa_q_norm_g), two(swa_k_norm_g), two(mla_qr_norm_g), two(mla_kr_norm_g)
    sinks1 = swa_sinks[0]

    proj, hn = _in_proj(x2, attn_norm_g, wi)
    qc, kc, vb, qb, kvb, cqn, ckvn = _mla_prep(proj, cos_t, sin_t, mla_cq_norm_g, mla_ckv_norm_g, wq, wkv,
                                                mla_qn_norm_g, gqr2, mla_kn_norm_g, gkr2)
    y_b, lse_b = _mla_fwd(qc, kc, vb)
    km, vmm, kvm, memn = _memkv_prep(mem2, mem_norm_g, wmkv, mem_k_norm_g)
    y_m, lse_m = _mem_fwd(proj, mem_q_norm_g, km, vmm)
    y_a, lse_a = _swa_fwd(proj, posc, posr, gq2, gk2, sinks1)
    h1, fn = _out_proj(y_a, y_b, y_m, x2, wo, ffn_norm_g)
    gu, act = _ffn_gu(fn, w_gu)
    dout, dout_b, loss_tile = _ffn_down(act, w_d, h1, tgt)

    dgu, dw_d = _ffn_bwd_act(dout_b, w_d, gu)
    dw_gu = _ffn_dw_gu(fn, dgu)
    r_gu, r_d = _exchange_grads_background([dw_gu, dw_d], 2, "exchange_ffn_grads")
    d_h1, dg_ffn = _ffn_norm_bwd(_ffn_dfn(dgu, w_gu, dw_gu), dout, h1, ffn_norm_g)
    d_y = _mm(d_h1, wo, tb=True, out_dtype=jnp.float32, tm=FFN_TILE, tk=2048, name="d_mix")
    dw_out = jnp.concatenate([
        _mm(y_a, d_h1, ta=True, out_dtype=WIRE, tm=1024, tk=1024, name="dw_out_a"),
        _mm(y_b, d_h1, ta=True, out_dtype=WIRE, tm=1024, tk=1024, name="dw_out_b"),
        _mm(y_m, d_h1, ta=True, out_dtype=WIRE, tm=1024, tk=1024, name="dw_out_m")], axis=0)
    d_qm, dkm, dvmm, dg_mq = _mem_bwd(proj, mem_q_norm_g, km, vmm, d_y, y_m, lse_m)
    dw_mkv, dg_mem, dg_mk = _memkv_bwd(mem2, mem_norm_g, wmkv, mem_k_norm_g, kvm, memn, dkm, dvmm)
    r_mkv, r_out = _exchange_grads_background([dw_mkv.reshape(g_mkv.shape), dw_out.reshape(g_out.shape)], 3,
                                              "exchange_mix_grads")
    dqc, dkc, dvb = _mla_bwd(qc, kc, vb, d_y, y_b, lse_b, dw_mkv)
    (d_cq, d_ckv, d_kr, dw_uq, dw_ukv, dg_cq, dg_ckv, dg_qn, dg_qr, dg_kn, dg_kr) = _mla_prep_bwd(
        proj, cos_t, sin_t, mla_cq_norm_g, mla_ckv_norm_g, wq, wkv, mla_qn_norm_g, gqr2, mla_kn_norm_g, gkr2,
        qb, kvb, cqn, ckvn, dqc, dkc, dvb)
    d_qa, d_ka, d_va, dg_q, dg_k, d_sinks = _swa_bwd(proj, posc, posr, gq2, gk2, sinks1, d_y, y_a, lse_a, dw_out)
    d_proj = jnp.concatenate([d_qa, d_cq, d_ckv, d_qm, d_ka, d_va, d_kr], axis=1)
    gi = _dw_in(hn, d_proj, n_in)

    gq_ = jnp.concatenate(sum([[dw_uq[128 * h: 128 * (h + 1)], dw_uq[512 + 64 * h: 512 + 64 * (h + 1)]]
                               for h in range(4)], []), axis=0)
    gq_ = gq_.reshape(N_DEV, 96, 512)
    r_in, r_uq, r_ukv = _exchange_grads_background([gi, gq_, dw_ukv], 4, "exchange_in_grads")
    grad_x, dg_attn = _dx(d_proj, wi, x2, attn_norm_g, d_h1, gi)

    big = {}
    def adam(name, r, transposed=False, after=None, which=None):
        w, m, v = args[name][0], args["m_" + name][0], args["v_" + name][0]
        if transposed:
            outs = _adam_big(r, w.T, m.T, v.T, "adam_" + name, after, which)
            return [o.T[None] for o in outs]
        return [o[None] for o in _adam_big(r, w, m, v, "adam_" + name, after)]
    big["w_gate"] = adam("w_gate", r_gu, True, which=0)
    big["w_up"] = adam("w_up", r_gu, True, after=big["w_gate"][0], which=1)
    big["w_down"] = adam("w_down", r_d, after=big["w_up"][0])
    big["w_out"] = adam("w_out", r_out, after=big["w_down"][0])
    big["w_mem_kv"] = adam("w_mem_kv", r_mkv, after=big["w_out"][0])
    big["w_in"] = adam("w_in", r_in, True, after=big["w_mem_kv"][0])
    big["w_uq"] = adam("w_uq", r_uq, True, after=big["w_in"][0])
    big["w_ukv"] = adam("w_ukv", r_ukv, after=big["w_uq"][0])

    small_g = {
        "attn_norm_g": dg_attn, "swa_q_norm_g": dg_q, "swa_k_norm_g": dg_k,
        "swa_sinks": d_sinks, "mla_cq_norm_g": dg_cq, "mla_ckv_norm_g": dg_ckv, "mla_qn_norm_g": dg_qn,
        "mla_qr_norm_g": dg_qr, "mla_kn_norm_g": dg_kn, "mla_kr_norm_g": dg_kr,
        "mem_norm_g": dg_mem, "mem_q_norm_g": dg_mq, "mem_k_norm_g": dg_mk, "ffn_norm_g": dg_ffn}
    loss11, small_out = _small_allreduce_adam(
        [small_g[n] for n in _SMALL], loss_tile, [args[n] for n in _SMALL],
        [args["m_" + n] for n in _SMALL], [args["v_" + n] for n in _SMALL])
    small = dict(zip(_SMALL, small_out))
    loss = loss11.reshape(())

    order = ["attn_norm_g", "w_in", "swa_q_norm_g", "swa_k_norm_g", "swa_sinks", "mla_cq_norm_g", "mla_ckv_norm_g",
             "w_uq", "w_ukv", "mla_qn_norm_g", "mla_qr_norm_g", "mla_kn_norm_g", "mla_kr_norm_g", "mem_norm_g",
             "w_mem_kv", "mem_q_norm_g", "mem_k_norm_g", "w_out", "ffn_norm_g", "w_gate", "w_up", "w_down"]
    res = {n: (big[n] if n in big else list(small[n])) for n in order}
    outs = [loss, grad_x[None]]
    for kind in range(4):
        outs += [res[n][kind] for n in order]
    return tuple(outs)
```

```python
import jax
import jax.numpy as jnp
from jax import lax
from jax.experimental import pallas as pl
from jax.experimental.pallas import tpu as pltpu
from jax.experimental.pallas import tpu_sc as plsc

MXU = jnp.bfloat16
WIRE = jnp.bfloat16
EPS = 1e-6
NEG_INF = -1e30
LOG2E = 1.4426950408889634
N_DEV = 8
LANES = 128
ROW_TILE = 256
FFN_TILE = 512
ATT_TILE = 1024
SWA_BLOCK = 128
VMEM_LIMIT = 56 * 1024 * 1024

SWA_Q_HEADS, SWA_KV_HEADS, SWA_DIM = 16, 2, 64
MLA_HEADS, MLA_NOPE, MLA_ROPE, MLA_V = 4, 128, 64, 128
MEM_HEADS, MEM_DIM = 4, 128
ROPE_THETA = 10000.0
ADAM_LR, ADAM_B1, ADAM_B2, ADAM_EPS, ADAM_WD, ADAM_STEP = 0.001, 0.9, 0.999, 1e-08, 0.01, 10

C_QA, C_CQ, C_CKV, C_QM, C_KA, C_VA, C_KR, IN_PAD = 0, 1024, 1536, 2048, 2560, 2688, 2816, 2944


def _pcall(body, *, name, out_shape, in_specs, out_specs, grid=(), scratch=(), sem=None, after=None):
    params = pltpu.CompilerParams(dimension_semantics=sem, vmem_limit_bytes=VMEM_LIMIT)
    if after is not None:
        n_in, inner = len(in_specs), body

        def body(*refs):
            inner(*refs[:n_in], *refs[n_in + 1:])

        in_specs = list(in_specs) + [pl.BlockSpec(memory_space=pl.ANY)]
    call = pl.pallas_call(body, name=name, grid=grid, in_specs=in_specs, out_specs=out_specs,
                          out_shape=out_shape, scratch_shapes=list(scratch), compiler_params=params)
    return call if after is None else (lambda *ops: call(*ops, after))


def _sds(shape, dtype):
    return jax.ShapeDtypeStruct(tuple(shape), dtype)


def _dot(a, b):
    return jnp.dot(a.astype(MXU), b.astype(MXU), preferred_element_type=jnp.float32)


def _dot_nt(a, b):
    return lax.dot_general(a.astype(MXU), b.astype(MXU), (((1,), (1,)), ((), ())),
                           preferred_element_type=jnp.float32)


def _dot_tn(a, b):
    return lax.dot_general(a.astype(MXU), b.astype(MXU), (((0,), (0,)), ((), ())),
                           preferred_element_type=jnp.float32)


def _lo_mask(shape):
    return (lax.broadcasted_iota(jnp.int32, shape, len(shape) - 1) % LANES) < 64


def _norm_fwd(x, g, half=False):
    x2 = x * x
    if half:
        lo = _lo_mask(x.shape)
        s_lo = jnp.sum(jnp.where(lo, x2, 0.0), -1, keepdims=True)
        s_hi = jnp.sum(jnp.where(lo, 0.0, x2), -1, keepdims=True)
        r = jnp.where(lo, lax.rsqrt(s_lo / 64.0 + EPS), lax.rsqrt(s_hi / 64.0 + EPS))
    else:
        r = lax.rsqrt(jnp.mean(x2, -1, keepdims=True) + EPS)
    xn = x * r
    return xn * g, xn, r


def _norm_bwd(xn, r, g, dy, half=False):
    t = dy * g
    tx = t * xn
    if half:
        lo = _lo_mask(xn.shape)
        m_lo = jnp.sum(jnp.where(lo, tx, 0.0), -1, keepdims=True) / 64.0
        m_hi = jnp.sum(jnp.where(lo, 0.0, tx), -1, keepdims=True) / 64.0
        m = jnp.where(lo, m_lo, m_hi)
    else:
        m = jnp.mean(tx, -1, keepdims=True)
    dx = r * (t - xn * m)
    dg = jnp.sum(dy * xn, 0, keepdims=True)
    return dx, dg


def _swap32(x):
    lane = lax.broadcasted_iota(jnp.int32, x.shape, 1)
    return jnp.where((lane % 64) < 32, pltpu.roll(x, 96, 1), pltpu.roll(x, 32, 1))


def _rope(x, cos, sin):
    return x * cos + _swap32(x) * sin


def _rope_bwd(d, cos, sin):
    return d * cos + _swap32(d * sin)


def _my_coords():
    return lax.axis_index("x"), lax.axis_index("y"), lax.axis_index("c")


def _dev_index(px, py, pc):
    return 4 * px + 2 * py + pc


_FLIPS = [(0, 0, 1), (0, 1, 0), (0, 1, 1), (1, 0, 0), (1, 0, 1), (1, 1, 0), (1, 1, 1)]


def _flip(coords, f):
    return tuple((1 - v) if b else v for v, b in zip(coords, f))


def _all_gather(shards):
    n = len(shards)

    def body(*refs):
        ins, outs = refs[:n], refs[n:2 * n]
        send_sems, recv_sems, local_sems = refs[2 * n:]
        x, y, c = _my_coords()
        me, sibling = (x, y, c), (x, y, 1 - c)
        chips = [(1 - x, y), (x, 1 - y), (1 - x, 1 - y)]

        def copy(w, k, block, to, src=None):
            dst = outs[w].at[_dev_index(*block)]
            return pltpu.make_async_remote_copy(
                src_ref=dst if src is None else src, dst_ref=dst,
                send_sem=send_sems.at[w, k], recv_sem=recv_sems.at[w, k],
                device_id=to, device_id_type=pl.DeviceIdType.MESH)

        sends, locals_ = [], []
        for w in range(n):
            mine = pltpu.make_async_copy(ins[w], outs[w].at[_dev_index(*me)], local_sems.at[w])
            mine.start()
            locals_.append(mine)
            first = [copy(w, 0, me, sibling, src=ins[w])]
            first += [copy(w, 1 + j, me, (*chip, c), src=ins[w]) for j, chip in enumerate(chips)]
            for cp in first:
                cp.start()
            sends += first
        for w in range(n):
            for j, chip in enumerate(chips):
                copy(w, 1 + j, (*chip, c), me).wait_recv()
                fwd = copy(w, 4 + j, (*chip, c), sibling)
                fwd.start()
                sends.append(fwd)
        for w in range(n):
            copy(w, 0, sibling, me).wait_recv()
            for j, chip in enumerate(chips):
                copy(w, 4 + j, (*chip, 1 - c), me).wait_recv()
        for cp in sends:
            cp.wait_send()
        for mine in locals_:
            mine.wait()

    any_spec = pl.BlockSpec(memory_space=pl.ANY)
    return _pcall(
        body, name="all_gather_weights",
        out_shape=[_sds((N_DEV,) + s.shape, s.dtype) for s in shards],
        in_specs=[any_spec] * n, out_specs=[any_spec] * n,
        scratch=[pltpu.SemaphoreType.DMA((n, 7)), pltpu.SemaphoreType.DMA((n, 7)),
                 pltpu.SemaphoreType.DMA((n,))])(*shards)


def _wire_cost(arrays):
    nbytes = sum(a.size * a.dtype.itemsize for a in arrays)
    return pl.CostEstimate(flops=0, transcendentals=0, bytes_accessed=40 * nbytes)


def _all_gather_background(shards, collective_id, name):
    n = len(shards)
    src_refs = [jax.new_ref(s, memory_space=pltpu.MemorySpace.HBM) for s in shards]
    out_refs = [jax.empty_ref(_sds((N_DEV,) + s.shape, s.dtype), memory_space=pltpu.MemorySpace.HBM) for s in shards]

    @pl.kernel(mesh=plsc.ScalarSubcoreMesh(axis_name="seq", num_cores=1), name=name,
               scratch_types=(pltpu.SemaphoreType.DMA((n, 7)), pltpu.SemaphoreType.DMA((n, 7)),
                              pltpu.SemaphoreType.DMA((n,))),
               compiler_params=pltpu.CompilerParams(collective_id=collective_id))
    def launch(send_sems, recv_sems, local_sems):
        x, y, c = _my_coords()
        me, sibling = (x, y, c), (x, y, 1 - c)
        chips = [(1 - x, y), (x, 1 - y), (1 - x, 1 - y)]
        barrier = pltpu.get_barrier_semaphore()
        for peer in [sibling] + [(*chip, c) for chip in chips]:
            pl.semaphore_signal(barrier, inc=1, device_id=peer, device_id_type=pl.DeviceIdType.MESH)
        pl.semaphore_wait(barrier, 4)

        def copy(w, k, block, to, src=None):
            dst = out_refs[w].at[_dev_index(*block)]
            return pltpu.make_async_remote_copy(
                src_ref=dst if src is None else src, dst_ref=dst,
                send_sem=send_sems.at[w, k], recv_sem=recv_sems.at[w, k],
                device_id=to, device_id_type=pl.DeviceIdType.MESH)

        sends, locals_ = [], []
        for w in range(n):
            mine = pltpu.make_async_copy(src_refs[w], out_refs[w].at[_dev_index(*me)], local_sems.at[w])
            mine.start()
            locals_.append(mine)
            first = [copy(w, 0, me, sibling, src=src_refs[w])]
            first += [copy(w, 1 + j, me, (*chip, c), src=src_refs[w]) for j, chip in enumerate(chips)]
            for cp in first:
                cp.start()
            sends += first
        for w in range(n):
            for j, chip in enumerate(chips):
                copy(w, 1 + j, (*chip, c), me).wait_recv()
                fwd = copy(w, 4 + j, (*chip, c), sibling)
                fwd.start()
                sends.append(fwd)
        for w in range(n):
            copy(w, 0, sibling, me).wait_recv()
            for j, chip in enumerate(chips):
                copy(w, 4 + j, (*chip, 1 - c), me).wait_recv()
        for cp in sends:
            cp.wait_send()
        for mine in locals_:
            mine.wait()

    launch()
    return [r[...] for r in out_refs]


def _exchange_grads(grads):
    n = len(grads)

    def body(*refs):
        ins, outs = refs[:n], refs[n:2 * n]
        send_sems, recv_sems, local_sems = refs[2 * n:]
        me = _my_coords()
        my_idx = _dev_index(*me)
        sends, locals_ = [], []
        for w in range(n):
            mine = pltpu.make_async_copy(ins[w].at[my_idx], outs[w].at[my_idx], local_sems.at[w])
            mine.start()
            locals_.append(mine)
            for k, f in enumerate(_FLIPS):
                peer = _flip(me, f)
                cp = pltpu.make_async_remote_copy(
                    src_ref=ins[w].at[_dev_index(*peer)], dst_ref=outs[w].at[my_idx],
                    send_sem=send_sems.at[w, k], recv_sem=recv_sems.at[w, k],
                    device_id=peer, device_id_type=pl.DeviceIdType.MESH)
                cp.start()
                sends.append(cp)
        for w in range(n):
            for k, f in enumerate(_FLIPS):
                peer = _flip(me, f)
                slot = outs[w].at[_dev_index(*peer)]
                pltpu.make_async_remote_copy(
                    src_ref=slot, dst_ref=slot,
                    send_sem=send_sems.at[w, k], recv_sem=recv_sems.at[w, k],
                    device_id=peer, device_id_type=pl.DeviceIdType.MESH).wait_recv()
        for cp in sends:
            cp.wait_send()
        for mine in locals_:
            mine.wait()

    any_spec = pl.BlockSpec(memory_space=pl.ANY)
    return _pcall(
        body, name="exchange_grads",
        out_shape=[_sds(g.shape, g.dtype) for g in grads],
        in_specs=[any_spec] * n, out_specs=[any_spec] * n,
        scratch=[pltpu.SemaphoreType.DMA((n, 7)), pltpu.SemaphoreType.DMA((n, 7)),
                 pltpu.SemaphoreType.DMA((n,))])(*grads)


def _exchange_grads_background(grads, collective_id, name):
    n = len(grads)
    src_refs = [jax.new_ref(g, memory_space=pltpu.MemorySpace.HBM) for g in grads]
    out_refs = [jax.empty_ref(_sds(g.shape, g.dtype), memory_space=pltpu.MemorySpace.HBM) for g in grads]

    @pl.kernel(mesh=plsc.ScalarSubcoreMesh(axis_name="seq", num_cores=1), name=name,
               scratch_types=(pltpu.SemaphoreType.DMA((n, 7)), pltpu.SemaphoreType.DMA((n, 7)),
                              pltpu.SemaphoreType.DMA((n,))),
               cost_estimate=_wire_cost(grads),
               compiler_params=pltpu.CompilerParams(collective_id=collective_id))
    def launch(send_sems, recv_sems, local_sems):
        me = _my_coords()
        my_idx = _dev_index(*me)
        peers = [_flip(me, f) for f in _FLIPS]
        barrier = pltpu.get_barrier_semaphore()
        for peer in peers:
            pl.semaphore_signal(barrier, inc=1, device_id=peer, device_id_type=pl.DeviceIdType.MESH)
        pl.semaphore_wait(barrier, len(peers))
        sends, locals_ = [], []
        for w in range(n):
            mine = pltpu.make_async_copy(src_refs[w].at[my_idx], out_refs[w].at[my_idx], local_sems.at[w])
            mine.start()
            locals_.append(mine)
            for k, peer in enumerate(peers):
                cp = pltpu.make_async_remote_copy(
                    src_ref=src_refs[w].at[_dev_index(*peer)], dst_ref=out_refs[w].at[my_idx],
                    send_sem=send_sems.at[w, k], recv_sem=recv_sems.at[w, k],
                    device_id=peer, device_id_type=pl.DeviceIdType.MESH)
                cp.start()
                sends.append(cp)
        for w in range(n):
            for k, peer in enumerate(peers):
                slot = out_refs[w].at[_dev_index(*peer)]
                pltpu.make_async_remote_copy(
                    src_ref=slot, dst_ref=slot, send_sem=send_sems.at[w, k], recv_sem=recv_sems.at[w, k],
                    device_id=peer, device_id_type=pl.DeviceIdType.MESH).wait_recv()
        for cp in sends:
            cp.wait_send()
        for mine in locals_:
            mine.wait()

    launch()
    return [r[...] for r in out_refs]


def _to_wire(parts, after, name):
    n = len(parts)
    rows, cols = parts[0].shape
    tr = rows // 2 if rows % 32 == 0 else rows

    def body(*refs):
        for k in range(n):
            refs[n][k] = refs[k][...].astype(WIRE)

    blk = pl.BlockSpec((tr, cols), lambda i: (i, 0))
    return _pcall(
        body, name=name, grid=(rows // tr,), out_shape=_sds((n, rows, cols), WIRE),
        in_specs=[blk] * n, out_specs=pl.BlockSpec((n, tr, cols), lambda i: (0, i, 0)),
        sem=("parallel",), after=after)(*parts)


def _adam_math(w, g, m, v):
    m = ADAM_B1 * m + (1.0 - ADAM_B1) * g
    v = ADAM_B2 * v + (1.0 - ADAM_B2) * (g * g)
    m_hat = m / (1.0 - ADAM_B1 ** ADAM_STEP)
    v_hat = v / (1.0 - ADAM_B2 ** ADAM_STEP)
    delta = -ADAM_LR * (m_hat / (jnp.sqrt(v_hat) + ADAM_EPS) + ADAM_WD * w)
    return delta, m, v


def _small_allreduce_adam(grads, loss_tile, ws, ms, vs):
    sizes = [w.shape[-1] for w in ws]
    n_par = len(ws)
    row0, r = [], 0
    for n in sizes:
        row0.append(r)
        r += -(-n // LANES)
    loss_row = r
    rows = -(-(r + 1) // 8) * 8

    def pieces(n):
        return [(k, min(LANES, n - LANES * k)) for k in range(-(-n // LANES))]

    def body(*refs):
        g_refs = refs[:n_par]
        loss_in = refs[n_par]
        w_refs = refs[n_par + 1: 2 * n_par + 1]
        m_refs = refs[2 * n_par + 1: 3 * n_par + 1]
        v_refs = refs[3 * n_par + 1: 4 * n_par + 1]
        loss_out = refs[4 * n_par + 1]
        out_refs = refs[4 * n_par + 2: 8 * n_par + 2]
        pack, gath, res, send_sems, recv_sems = refs[8 * n_par + 2:]
        me = _my_coords()
        my_idx = _dev_index(*me)

        def fill(slot, srcs):
            pack[slot] = jnp.zeros((rows, LANES), jnp.float32)
            for p, n in enumerate(sizes):
                val = srcs[p][...]
                if val.shape[-1] == LANES and n == 64:
                    pack[slot, row0[p]:row0[p] + 1, :] = val + pltpu.roll(val, 64, 1)
                    continue
                for k, width in pieces(n):
                    pack[slot, row0[p] + k:row0[p] + k + 1, 0:width] = srcs[p][:, LANES * k:LANES * k + width]

        fill(0, g_refs)
        pack[0, loss_row:loss_row + 1, :] = loss_in[0:1, :]
        gath[my_idx] = pack[0]
        sends = []
        for k, f in enumerate(_FLIPS):
            peer = _flip(me, f)
            cp = pltpu.make_async_remote_copy(
                src_ref=pack.at[0], dst_ref=gath.at[my_idx],
                send_sem=send_sems.at[k], recv_sem=recv_sems.at[k],
                device_id=peer, device_id_type=pl.DeviceIdType.MESH)
            cp.start()
            sends.append(cp)
        fill(1, w_refs)
        fill(2, m_refs)
        fill(3, v_refs)
        for k, f in enumerate(_FLIPS):
            peer = _flip(me, f)
            slot = gath.at[_dev_index(*peer)]
            pltpu.make_async_remote_copy(
                src_ref=slot, dst_ref=slot, send_sem=send_sems.at[k], recv_sem=recv_sems.at[k],
                device_id=peer, device_id_type=pl.DeviceIdType.MESH).wait_recv()
        for cp in sends:
            cp.wait_send()
        g = gath[0]
        for d in range(1, N_DEV):
            g = g + gath[d]
        delta, m, v = _adam_math(pack[1], g, pack[2], pack[3])
        res[0], res[1], res[2], res[3] = g, delta, m, v
        loss_out[...] = res[0, loss_row:loss_row + 1, 0:1]
        for p, n in enumerate(sizes):
            for kind in range(4):
                for k, width in pieces(n):
                    out_refs[4 * p + kind][:, LANES * k:LANES * k + width] = (
                        res[kind, row0[p] + k:row0[p] + k + 1, 0:width])

    vm = pl.BlockSpec(memory_space=pltpu.VMEM)
    out_shape = [_sds((1, 1), jnp.float32)]
    for n in sizes:
        out_shape += [_sds((1, n), jnp.float32)] * 4
    outs = _pcall(
        body, name="small_allreduce_adam", out_shape=out_shape,
        in_specs=[vm] * (4 * n_par + 1), out_specs=[vm] * len(out_shape),
        scratch=[pltpu.VMEM((4, rows, LANES), jnp.float32), pltpu.VMEM((N_DEV, rows, LANES), jnp.float32),
                 pltpu.VMEM((4, rows, LANES), jnp.float32),
                 pltpu.SemaphoreType.DMA((7,)), pltpu.SemaphoreType.DMA((7,))])(*grads, loss_tile, *ws, *ms, *vs)
    return outs[0], [outs[1 + 4 * p: 5 + 4 * p] for p in range(n_par)]


def _adam_big(recv, w, m, v, name, after=None, which=None):
    rows, cols = recv.shape[-2:]
    row_tiles = [t for t in range(16, rows + 1, 16) if rows % t == 0 and t * cols <= 400 * 1024]
    tr, tc = (max(row_tiles), cols) if row_tiles else (rows, 512 if cols % 512 == 0 else cols)

    def body(r_ref, w_ref, m_ref, v_ref, g_ref, d_ref, mo_ref, vo_ref):
        g = r_ref[0].astype(jnp.float32)
        for d in range(1, N_DEV):
            g = g + r_ref[d].astype(jnp.float32)
        delta, mn, vn = _adam_math(w_ref[...], g, m_ref[...], v_ref[...])
        g_ref[...] = g
        d_ref[...] = delta
        mo_ref[...] = mn
        vo_ref[...] = vn

    blk = pl.BlockSpec((tr, tc), lambda i, j: (i, j))
    if which is None:
        r_spec = pl.BlockSpec((N_DEV, tr, tc), lambda i, j: (0, i, j))
    else:
        r_spec = pl.BlockSpec((N_DEV, None, tr, tc), lambda i, j: (0, which, i, j))
    return _pcall(
        body, name=name, grid=(rows // tr, cols // tc),
        out_shape=[_sds((rows, cols), jnp.float32)] * 4,
        in_specs=[r_spec, blk, blk, blk],
        out_specs=[blk] * 4, sem=("parallel", "parallel"), after=after)(recv, w, m, v)


def _mm(a, b, *, ta=False, tb=False, out_dtype, tm, tk, name):
    (kdim, mdim) = a.shape if ta else a.shape[::-1]
    ndim = b.shape[0] if tb else b.shape[1]
    tm, tk = min(tm, mdim), min(tk, kdim)
    nk = kdim // tk

    def body(a_ref, b_ref, o_ref, acc):
        k = pl.program_id(1)
        if ta:
            part = _dot_tn(a_ref[...], b_ref[...])
        elif tb:
            part = _dot_nt(a_ref[...], b_ref[...])
        else:
            part = _dot(a_ref[...], b_ref[...])

        @pl.when(k == 0)
        def _():
            acc[...] = part

        @pl.when(k > 0)
        def _():
            acc[...] += part

        @pl.when(k == nk - 1)
        def _():
            o_ref[...] = acc[...].astype(o_ref.dtype)

    a_spec = pl.BlockSpec((tk, tm), lambda i, k: (k, i)) if ta else pl.BlockSpec((tm, tk), lambda i, k: (i, k))
    b_spec = pl.BlockSpec((ndim, tk), lambda i, k: (0, k)) if tb else pl.BlockSpec((tk, ndim), lambda i, k: (k, 0))
    return _pcall(
        body, name=name, grid=(mdim // tm, nk), out_shape=_sds((mdim, ndim), out_dtype),
        in_specs=[a_spec, b_spec], out_specs=pl.BlockSpec((tm, ndim), lambda i, k: (i, 0)),
        scratch=[pltpu.VMEM((tm, ndim), jnp.float32)], sem=("parallel", "arbitrary"))(a, b)


def _ref_col_pieces(start, stop):
    ref_starts = [0, 1024, 1152, 1280, 1792, 2304, 2368, 2880]
    perm_starts = [C_QA, C_KA, C_VA, C_CQ, C_CKV, C_KR, C_QM]
    out = []
    for p in range(7):
        lo, hi = max(start, ref_starts[p]), min(stop, ref_starts[p + 1])
        if lo < hi:
            out.append((lo - start, perm_starts[p] + lo - ref_starts[p], hi - lo))
    return out


def _dw_in(hn, d_proj, n_shard):
    s, d = hn.shape
    n = d_proj.shape[1]
    tm, tk = min(512, d), min(1024, s)
    nk = s // tk

    def body(a_ref, b_ref, o_ref, acc):
        k = pl.program_id(1)
        part = _dot_tn(a_ref[...], b_ref[...])

        @pl.when(k == 0)
        def _():
            acc[...] = part

        @pl.when(k > 0)
        def _():
            acc[...] += part

        @pl.when(k == nk - 1)
        def _():
            t = acc[...].T
            for j in range(N_DEV):
                rows = [t[src:src + width] for _, src, width in _ref_col_pieces(j * n_shard, (j + 1) * n_shard)]
                o_ref[j] = jnp.concatenate(rows, axis=0).astype(o_ref.dtype)

    return _pcall(
        body, name="dw_in", grid=(d // tm, nk), out_shape=_sds((N_DEV, n_shard, d), WIRE),
        in_specs=[pl.BlockSpec((tk, tm), lambda i, k: (k, i)), pl.BlockSpec((tk, n), lambda i, k: (k, 0))],
        out_specs=pl.BlockSpec((N_DEV, n_shard, tm), lambda i, k: (0, 0, i)),
        scratch=[pltpu.VMEM((tm, n), jnp.float32)], sem=("parallel", "arbitrary"))(hn, d_proj)


def _in_proj(x, g, w):
    s, d = x.shape
    n = w.shape[0]
    tm = min(ROW_TILE, s)

    def body(x_ref, g_ref, w_ref, p_ref, hn_ref):
        hn, _, _ = _norm_fwd(x_ref[...], g_ref[...])
        hn_ref[...] = hn.astype(hn_ref.dtype)
        p_ref[...] = _dot_nt(hn, w_ref[...])

    return _pcall(
        body, name="in_proj", grid=(s // tm,),
        out_shape=[_sds((s, n), jnp.float32), _sds((s, d), MXU)],
        in_specs=[pl.BlockSpec((tm, d), lambda i: (i, 0)), pl.BlockSpec((1, d), lambda i: (0, 0)),
                  pl.BlockSpec((n, d), lambda i: (0, 0))],
        out_specs=[pl.BlockSpec((tm, n), lambda i: (i, 0)), pl.BlockSpec((tm, d), lambda i: (i, 0))],
        sem=("parallel",))(x, g, w)


def _mla_prep(proj, cos, sin, g_cq, g_ckv, w_uq, w_ukv, g_qn, g_qr, g_kn, g_kr):
    s = proj.shape[0]
    tm = min(ROW_TILE, s)
    nh = MLA_HEADS

    def body(cq_ref, ckv_ref, kr_ref, cos_ref, sin_ref, gcq_ref, gckv_ref, wuq_ref, wukv_ref,
             gqn_ref, gqr_ref, gkn_ref, gkr_ref,
             qc_ref, kc_ref, v_ref, qb_ref, kvb_ref, cqn_ref, ckvn_ref):
        cos_t, sin_t = cos_ref[...], sin_ref[...]
        lo = _lo_mask((tm, LANES))
        cqn, _, _ = _norm_fwd(cq_ref[...], gcq_ref[...])
        cqn_ref[...] = cqn.astype(cqn_ref.dtype)
        qb = _dot_nt(cqn, wuq_ref[...])
        qb_ref[...] = qb
        ckvn, _, _ = _norm_fwd(ckv_ref[...], gckv_ref[...])
        ckvn_ref[...] = ckvn.astype(ckvn_ref.dtype)
        kvb = jnp.concatenate([_dot(ckvn, wukv_ref[dev]) for dev in range(N_DEV)], axis=1)
        kvb_ref[...] = kvb
        kr, _, _ = _norm_fwd(kr_ref[...], gkr_ref[...], half=True)
        kr = _rope(kr, cos_t, sin_t)
        kr2 = jnp.where(lo, kr, pltpu.roll(kr, 64, 1))
        ropes = []
        for j in range(nh // 2):
            xr = qb[:, nh * MLA_NOPE + LANES * j: nh * MLA_NOPE + LANES * (j + 1)]
            qr, _, _ = _norm_fwd(xr, gqr_ref[...], half=True)
            ropes.append(_rope(qr, cos_t, sin_t))
        for h in range(nh):
            qn, _, _ = _norm_fwd(qb[:, MLA_NOPE * h: MLA_NOPE * (h + 1)], gqn_ref[...])
            mask = lo if h % 2 == 0 else jnp.logical_not(lo)
            qr = jnp.where(mask, ropes[h // 2], 0.0)
            qc_ref[h] = jnp.concatenate([qn, qr], axis=1).astype(qc_ref.dtype)
            kn, _, _ = _norm_fwd(kvb[:, 256 * h: 256 * h + MLA_NOPE], gkn_ref[...])
            kc_ref[h] = jnp.concatenate([kn, kr2], axis=1).astype(kc_ref.dtype)
            v_ref[h] = kvb[:, 256 * h + MLA_NOPE: 256 * (h + 1)].astype(v_ref.dtype)

    def col(width, start):
        return pl.BlockSpec((tm, width), lambda i: (i, start // width))

    def full(shape):
        return pl.BlockSpec(shape, lambda i: (0,) * len(shape))

    def row(width):
        return pl.BlockSpec((tm, width), lambda i: (i, 0))

    def heads(width):
        return pl.BlockSpec((nh, tm, width), lambda i: (0, i, 0))

    return _pcall(
        body, name="mla_prep", grid=(s // tm,),
        out_shape=[_sds((nh, s, 256), MXU), _sds((nh, s, 256), MXU), _sds((nh, s, MLA_V), MXU),
                   _sds((s, 768), jnp.float32), _sds((s, 1024), jnp.float32),
                   _sds((s, 512), MXU), _sds((s, 512), MXU)],
        in_specs=[col(512, C_CQ), col(512, C_CKV), col(LANES, C_KR), row(LANES), row(LANES),
                  full((1, 512)), full((1, 512)), full((768, 512)), full((N_DEV, 512, LANES)),
                  full((1, LANES)), full((1, LANES)), full((1, LANES)), full((1, LANES))],
        out_specs=[heads(256), heads(256), heads(MLA_V), row(768), row(1024), row(512), row(512)],
        sem=("parallel",))(proj, proj, proj, cos, sin, g_cq, g_ckv, w_uq, w_ukv, g_qn, g_qr, g_kn, g_kr)


def _mla_fwd(qc, kc, v):
    nh, s, _ = qc.shape
    t = min(ATT_TILE, s)
    nb = s // t
    scale = (MLA_NOPE + MLA_ROPE) ** -0.5

    def body(q_ref, k_ref, v_ref, y_ref, lse_ref, m_sc, l_sc, acc):
        qi, ki = pl.program_id(1), pl.program_id(2)

        @pl.when(ki == 0)
        def _():
            m_sc[...] = jnp.full_like(m_sc, NEG_INF)
            l_sc[...] = jnp.zeros_like(l_sc)
            acc[...] = jnp.zeros_like(acc)

        def step(diagonal):
            rc = t // 4 if diagonal else t
            for c in range(t // rc):
                rows = slice(rc * c, rc * (c + 1))
                keys = slice(0, rc * (c + 1))
                sc = _dot_nt(q_ref[0, rows, :], k_ref[0, keys, :]) * (scale * LOG2E)
                if diagonal:
                    r_i = lax.broadcasted_iota(jnp.int32, sc.shape, 0) + rc * c
                    c_i = lax.broadcasted_iota(jnp.int32, sc.shape, 1)
                    sc = jnp.where(c_i <= r_i, sc, NEG_INF)
                m_old = m_sc[rows, :]
                m_new = jnp.maximum(m_old, jnp.max(sc, -1, keepdims=True))
                alpha = jnp.exp2(m_old - m_new)
                p = jnp.exp2(sc - m_new)
                l_sc[rows, :] = alpha * l_sc[rows, :] + jnp.sum(p, -1, keepdims=True)
                acc[rows, :] = alpha * acc[rows, :] + _dot(p, v_ref[0, keys, :])
                m_sc[rows, :] = m_new

        @pl.when(ki < qi)
        def _():
            step(False)

        @pl.when(ki == qi)
        def _():
            step(True)

        @pl.when(ki == qi)
        def _():
            y_ref[...] = acc[...] / l_sc[...]
            lse_ref[0] = m_sc[...] + jnp.log2(l_sc[...])

    return _pcall(
        body, name="mla_fwd", grid=(nh, nb, nb),
        out_shape=[_sds((s, nh * MLA_V), jnp.float32), _sds((nh, s, 1), jnp.float32)],
        in_specs=[pl.BlockSpec((1, t, 256), lambda h, i, k: (h, i, 0)),
                  pl.BlockSpec((1, t, 256), lambda h, i, k: (h, jnp.minimum(k, i), 0)),
                  pl.BlockSpec((1, t, MLA_V), lambda h, i, k: (h, jnp.minimum(k, i), 0))],
        out_specs=[pl.BlockSpec((t, MLA_V), lambda h, i, k: (i, h)),
                   pl.BlockSpec((1, t, 1), lambda h, i, k: (h, i, 0))],
        scratch=[pltpu.VMEM((t, 1), jnp.float32), pltpu.VMEM((t, 1), jnp.float32),
                 pltpu.VMEM((t, MLA_V), jnp.float32)],
        sem=("parallel", "parallel", "arbitrary"))(qc, kc, v)


def _memkv_prep(mem, g_mem, w_mkv, g_mk):
    ml, d = mem.shape
    hw = MEM_HEADS * MEM_DIM

    def body(mem_ref, g_ref, w_ref, gk_ref, k_ref, v_ref, kv_ref, mn_ref):
        mn, _, _ = _norm_fwd(mem_ref[...], g_ref[...])
        mn_ref[...] = mn.astype(mn_ref.dtype)
        kv = _dot(mn, w_ref[...])
        kv_ref[...] = kv
        for h in range(MEM_HEADS):
            kn, _, _ = _norm_fwd(kv[:, MEM_DIM * h: MEM_DIM * (h + 1)], gk_ref[...])
            k_ref[:, MEM_DIM * h: MEM_DIM * (h + 1)] = kn.astype(k_ref.dtype)
        v_ref[...] = kv[:, hw:].astype(v_ref.dtype)

    vm = pl.BlockSpec(memory_space=pltpu.VMEM)
    return _pcall(
        body, name="memkv_prep",
        out_shape=[_sds((ml, hw), MXU), _sds((ml, hw), MXU), _sds((ml, 2 * hw), jnp.float32), _sds((ml, d), MXU)],
        in_specs=[vm] * 4, out_specs=[vm] * 4)(mem, g_mem, w_mkv, g_mk)


def _mem_fwd(proj, g_mq, km, vmm):
    s = proj.shape[0]
    ml, hw = km.shape
    tm = min(FFN_TILE, s)
    scale = MEM_DIM ** -0.5

    def body(q_ref, g_ref, k_ref, v_ref, y_ref, lse_ref):
        col = lax.broadcasted_iota(jnp.int32, (tm, MEM_HEADS), 1)
        lse_t = jnp.zeros((tm, MEM_HEADS), jnp.float32)
        for h in range(MEM_HEADS):
            sl = slice(MEM_DIM * h, MEM_DIM * (h + 1))
            qn, _, _ = _norm_fwd(q_ref[:, sl], g_ref[...])
            sc = _dot_nt(qn, k_ref[:, sl]) * scale
            m = jnp.max(sc, -1, keepdims=True)
            p = jnp.exp(sc - m)
            l = jnp.sum(p, -1, keepdims=True)
            y_ref[:, sl] = _dot(p, v_ref[:, sl]) / l
            lse_t = jnp.where(col == h, m + jnp.log(l), lse_t)
        lse_ref[...] = lse_t

    return _pcall(
        body, name="mem_fwd", grid=(s // tm,),
        out_shape=[_sds((s, hw), jnp.float32), _sds((s, MEM_HEADS), jnp.float32)],
        in_specs=[pl.BlockSpec((tm, hw), lambda i: (i, C_QM // hw)), pl.BlockSpec((1, MEM_DIM), lambda i: (0, 0)),
                  pl.BlockSpec((ml, hw), lambda i: (0, 0)), pl.BlockSpec((ml, hw), lambda i: (0, 0))],
        out_specs=[pl.BlockSpec((tm, hw), lambda i: (i, 0)), pl.BlockSpec((tm, MEM_HEADS), lambda i: (i, 0))],
        sem=("parallel",))(proj, g_mq, km, vmm)


def _alibi_slope(h):
    return float(2.0 ** (-8.0 * (h + 1) / SWA_Q_HEADS))


def _swa_common(n, kp, kc, vp, vc, pq, pkp, pkc, gk):
    b = SWA_BLOCK
    k_raw = jnp.concatenate([kp, kc], axis=0)
    kn, kxn, kr = _norm_fwd(k_raw, gk, half=True)
    v = jnp.concatenate([vp, vc], axis=0)
    dist = jnp.abs(pq - jnp.concatenate([pkp, pkc], axis=1))
    r_i = lax.broadcasted_iota(jnp.int32, (b, 2 * b), 0)
    c_i = lax.broadcasted_iota(jnp.int32, (b, 2 * b), 1)
    valid = (c_i > r_i) & (c_i <= r_i + b) & (c_i >= jnp.where(n > 0, 0, b))
    bias = jnp.where(valid, -dist, NEG_INF)
    return kn, v, bias


def _swa_specs(s):
    b = SWA_BLOCK
    prev = lambda n: jnp.maximum(n - 1, 0)
    return [
        pl.BlockSpec((b, 1024), lambda n: (n, C_QA // 1024)),
        pl.BlockSpec((b, LANES), lambda n: (prev(n), C_KA // LANES)),
        pl.BlockSpec((b, LANES), lambda n: (n, C_KA // LANES)),
        pl.BlockSpec((b, LANES), lambda n: (prev(n), C_VA // LANES)),
        pl.BlockSpec((b, LANES), lambda n: (n, C_VA // LANES)),
        pl.BlockSpec((b, 1), lambda n: (n, 0)),
        pl.BlockSpec((1, b), lambda n: (0, prev(n))),
        pl.BlockSpec((1, b), lambda n: (0, n)),
        pl.BlockSpec((1, LANES), lambda n: (0, 0)),
        pl.BlockSpec((1, LANES), lambda n: (0, 0)),
        pl.BlockSpec(memory_space=pltpu.SMEM),
    ]


def _swa_fwd(proj, posc, posr, gq, gk, sinks):
    s = proj.shape[0]
    b = SWA_BLOCK
    scale = SWA_DIM ** -0.5

    def body(q_ref, kp_ref, kc_ref, vp_ref, vc_ref, pq_ref, pkp_ref, pkc_ref, gq_ref, gk_ref, sink_ref,
             y_ref, lse_ref):
        n = pl.program_id(0)
        kn, v, bias = _swa_common(n, kp_ref[...], kc_ref[...], vp_ref[...], vc_ref[...],
                                  pq_ref[...], pkp_ref[...], pkc_ref[...], gk_ref[...])
        lo = _lo_mask((b, LANES))
        col = lax.broadcasted_iota(jnp.int32, (b, SWA_Q_HEADS), 1)
        lse_t = jnp.zeros((b, SWA_Q_HEADS), jnp.float32)
        for j in range(SWA_Q_HEADS // 2):
            hk = (2 * j) // (SWA_Q_HEADS // SWA_KV_HEADS)
            kvmask = lo if hk == 0 else jnp.logical_not(lo)
            qn, _, _ = _norm_fwd(q_ref[:, LANES * j: LANES * (j + 1)], gq_ref[...], half=True)
            qn = qn * scale
            qsw = pltpu.roll(qn, 64, 1)
            outs = []
            for e in range(2):
                h = 2 * j + e
                qm = jnp.where(kvmask, qn if e == hk else qsw, 0.0)
                sc = _dot_nt(qm, kn) + _alibi_slope(h) * bias
                sk = sink_ref[h]
                m = jnp.maximum(jnp.max(sc, -1, keepdims=True), sk)
                p = jnp.exp(sc - m)
                l = jnp.sum(p, -1, keepdims=True) + jnp.exp(sk - m)
                o = _dot(p, v) / l
                outs.append(o if e == hk else pltpu.roll(o, 64, 1))
                lse_t = jnp.where(col == h, m + jnp.log(l), lse_t)
            y_ref[:, LANES * j: LANES * (j + 1)] = jnp.where(lo, outs[0], outs[1])
        lse_ref[...] = lse_t

    return _pcall(
        body, name="swa_fwd", grid=(s // b,),
        out_shape=[_sds((s, 1024), jnp.float32), _sds((s, SWA_Q_HEADS), jnp.float32)],
        in_specs=_swa_specs(s),
        out_specs=[pl.BlockSpec((b, 1024), lambda n: (n, 0)), pl.BlockSpec((b, SWA_Q_HEADS), lambda n: (n, 0))],
        sem=("parallel",))(proj, proj, proj, proj, proj, posc, posr, posr, gq, gk, sinks)


def _out_proj(y_a, y_b, y_m, x, w_out, g_ffn):
    s, d = x.shape
    tm = min(ROW_TILE, s)

    def body(ya_ref, yb_ref, ym_ref, x_ref, w_ref, g_ref, h1_ref, fn_ref):
        y = jnp.concatenate([ya_ref[...].astype(MXU), yb_ref[...].astype(MXU), ym_ref[...].astype(MXU)], axis=1)
        h1 = x_ref[...] + _dot(y, w_ref[...])
        h1_ref[...] = h1
        fn, _, _ = _norm_fwd(h1, g_ref[...])
        fn_ref[...] = fn.astype(fn_ref.dtype)

    def row(width):
        return pl.BlockSpec((tm, width), lambda i: (i, 0))

    return _pcall(
        body, name="out_proj", grid=(s // tm,),
        out_shape=[_sds((s, d), jnp.float32), _sds((s, d), MXU)],
        in_specs=[row(1024), row(512), row(512), row(d), pl.BlockSpec(w_out.shape, lambda i: (0, 0)),
                  pl.BlockSpec((1, d), lambda i: (0, 0))],
        out_specs=[row(d), row(d)], sem=("parallel",))(y_a, y_b, y_m, x, w_out, g_ffn)


def _ffn_gu(fn, w_gu):
    s, d = fn.shape
    f = w_gu.shape[2]
    tm = min(FFN_TILE, s)

    def body(fn_ref, w_ref, gu_ref, act_ref):
        x = fn_ref[...]
        g = _dot_nt(x, w_ref[0, 0])
        u = _dot_nt(x, w_ref[0, 1])
        gu_ref[0, 0] = g
        gu_ref[0, 1] = u
        act_ref[0] = (g * jax.nn.sigmoid(g) * u).astype(act_ref.dtype)

    return _pcall(
        body, name="ffn_gate_up", grid=(N_DEV, s // tm),
        out_shape=[_sds((N_DEV, 2, s, f), jnp.float32), _sds((N_DEV, s, f), MXU)],
        in_specs=[pl.BlockSpec((tm, d), lambda j, i: (i, 0)),
                  pl.BlockSpec((1, 2, f, d), lambda j, i: (j, 0, 0, 0))],
        out_specs=[pl.BlockSpec((1, 2, tm, f), lambda j, i: (j, 0, i, 0)),
                   pl.BlockSpec((1, tm, f), lambda j, i: (j, i, 0))],
        sem=("parallel", "parallel"))(fn, w_gu)


def _ffn_down(act, w_d, h1, target):
    _, s, f = act.shape
    d = h1.shape[1]
    tm = min(FFN_TILE, s)

    def body(a_ref, w_ref, h1_ref, t_ref, dout_ref, doutb_ref, loss_ref, acc):
        i, j = pl.program_id(0), pl.program_id(1)
        part = _dot(a_ref[0], w_ref[0]) + _dot(a_ref[1], w_ref[1])

        @pl.when(j == 0)
        def _():
            acc[...] = h1_ref[...] + part

        @pl.when(j > 0)
        def _():
            acc[...] += part

        @pl.when((i == 0) & (j == 0))
        def _():
            loss_ref[...] = jnp.zeros_like(loss_ref)

        @pl.when(j == N_DEV // 2 - 1)
        def _():
            diff = acc[...] - t_ref[...]
            dout_ref[...] = diff / d
            doutb_ref[...] = (diff / d).astype(doutb_ref.dtype)
            loss_ref[...] += 0.5 * jnp.sum(jnp.sum(diff * diff, -1, keepdims=True) / d)

    row = pl.BlockSpec((tm, d), lambda i, j: (i, 0))
    return _pcall(
        body, name="ffn_down", grid=(s // tm, N_DEV // 2),
        out_shape=[_sds((s, d), jnp.float32), _sds((s, d), MXU), _sds((8, LANES), jnp.float32)],
        in_specs=[pl.BlockSpec((2, tm, f), lambda i, j: (j, i, 0)), pl.BlockSpec((2, f, d), lambda i, j: (j, 0, 0)),
                  row, row],
        out_specs=[row, row, pl.BlockSpec((8, LANES), lambda i, j: (0, 0))],
        scratch=[pltpu.VMEM((tm, d), jnp.float32)], sem=("arbitrary", "arbitrary"))(act, w_d, h1, target)


def _ffn_bwd_act(dout, w_d, gu):
    s, d = dout.shape
    f = w_d.shape[1]
    tm = min(FFN_TILE, s)
    ni = s // tm

    def body(do_ref, w_ref, gu_ref, dgu_ref, dw_ref, acc):
        i = pl.program_id(1)
        do = do_ref[...]
        d_act = _dot_nt(do, w_ref[0])
        g, u = gu_ref[0, 0], gu_ref[0, 1]
        sig = jax.nn.sigmoid(g)
        silu = g * sig
        dgu_ref[0, 0] = (d_act * u * (sig * (1.0 + g * (1.0 - sig)))).astype(dgu_ref.dtype)
        dgu_ref[0, 1] = (d_act * silu).astype(dgu_ref.dtype)
        part = _dot_tn(silu * u, do)

        @pl.when(i == 0)
        def _():
            acc[...] = part

        @pl.when(i > 0)
        def _():
            acc[...] += part

        @pl.when(i == ni - 1)
        def _():
            dw_ref[0] = acc[...].astype(dw_ref.dtype)

    return _pcall(
        body, name="ffn_bwd_act", grid=(N_DEV, ni),
        out_shape=[_sds((N_DEV, 2, s, f), MXU), _sds((N_DEV, f, d), WIRE)],
        in_specs=[pl.BlockSpec((tm, d), lambda j, i: (i, 0)), pl.BlockSpec((1, f, d), lambda j, i: (j, 0, 0)),
                  pl.BlockSpec((1, 2, tm, f), lambda j, i: (j, 0, i, 0))],
        out_specs=[pl.BlockSpec((1, 2, tm, f), lambda j, i: (j, 0, i, 0)),
                   pl.BlockSpec((1, f, d), lambda j, i: (j, 0, 0))],
        scratch=[pltpu.VMEM((f, d), jnp.float32)], sem=("parallel", "arbitrary"))(dout, w_d, gu)


def _ffn_dw_gu(fn, dgu):
    s, d = fn.shape
    f = dgu.shape[-1]
    tk = min(2 * FFN_TILE, s)
    nk = s // tk

    def body(fn_ref, dgu_ref, dw_ref, acc):
        k = pl.program_id(1)
        x = fn_ref[...]
        pg = _dot_tn(dgu_ref[0, 0], x)
        pu = _dot_tn(dgu_ref[0, 1], x)

        @pl.when(k == 0)
        def _():
            acc[0] = pg
            acc[1] = pu

        @pl.when(k > 0)
        def _():
            acc[0] += pg
            acc[1] += pu

        @pl.when(k == nk - 1)
        def _():
            dw_ref[0] = acc[...].astype(dw_ref.dtype)

    return _pcall(
        body, name="ffn_dw_gate_up", grid=(N_DEV, nk),
        out_shape=_sds((N_DEV, 2, f, d), WIRE),
        in_specs=[pl.BlockSpec((tk, d), lambda j, k: (k, 0)), pl.BlockSpec((1, 2, tk, f), lambda j, k: (j, 0, k, 0))],
        out_specs=pl.BlockSpec((1, 2, f, d), lambda j, k: (j, 0, 0, 0)),
        scratch=[pltpu.VMEM((2, f, d), jnp.float32)], sem=("parallel", "arbitrary"))(fn, dgu)


def _ffn_dfn(dgu, w_gu, after):
    _, _, s, f = dgu.shape
    d = w_gu.shape[3]
    tm = min(FFN_TILE, s)

    def body(dgu_ref, w_ref, dfn_ref):
        j = pl.program_id(1)
        part = (_dot(dgu_ref[0, 0], w_ref[0, 0]) + _dot(dgu_ref[0, 1], w_ref[0, 1])
                + _dot(dgu_ref[1, 0], w_ref[1, 0]) + _dot(dgu_ref[1, 1], w_ref[1, 1]))

        @pl.when(j == 0)
        def _():
            dfn_ref[...] = part

        @pl.when(j > 0)
        def _():
            dfn_ref[...] += part

    return _pcall(
        body, name="ffn_dfn", grid=(s // tm, N_DEV // 2),
        out_shape=_sds((s, d), jnp.float32),
        in_specs=[pl.BlockSpec((2, 2, tm, f), lambda i, j: (j, 0, i, 0)),
                  pl.BlockSpec((2, 2, f, d), lambda i, j: (j, 0, 0, 0))],
        out_specs=pl.BlockSpec((tm, d), lambda i, j: (i, 0)),
        sem=("parallel", "arbitrary"), after=after)(dgu, w_gu)


def _ffn_norm_bwd(d_fn, dout, h1, g_ffn):
    s, d = h1.shape
    tm = min(ROW_TILE, s)

    def body(dfn_ref, do_ref, h1_ref, g_ref, dh1_ref, dg_ref):
        i = pl.program_id(0)

        @pl.when(i == 0)
        def _():
            dg_ref[...] = jnp.zeros_like(dg_ref)

        _, xn, r = _norm_fwd(h1_ref[...], g_ref[...])
        dx, dg = _norm_bwd(xn, r, g_ref[...], dfn_ref[...])
        dh1_ref[...] = do_ref[...] + dx
        dg_ref[...] += dg

    row = pl.BlockSpec((tm, d), lambda i: (i, 0))
    vec = pl.BlockSpec((1, d), lambda i: (0, 0))
    return _pcall(
        body, name="ffn_norm_bwd", grid=(s // tm,),
        out_shape=[_sds((s, d), jnp.float32), _sds((1, d), jnp.float32)],
        in_specs=[row, row, row, vec], out_specs=[row, vec], sem=("arbitrary",))(d_fn, dout, h1, g_ffn)


def _mem_bwd(proj, g_mq, km, vmm, d_y, y_m, lse):
    s = proj.shape[0]
    ml, hw = km.shape
    tm = min(FFN_TILE, s)
    scale = MEM_DIM ** -0.5

    def body(q_ref, g_ref, k_ref, v_ref, do_ref, y_ref, lse_ref, dq_ref, dk_ref, dv_ref, dg_ref):
        i = pl.program_id(0)

        @pl.when(i == 0)
        def _():
            dk_ref[...] = jnp.zeros_like(dk_ref)
            dv_ref[...] = jnp.zeros_like(dv_ref)
            dg_ref[...] = jnp.zeros_like(dg_ref)

        col = lax.broadcasted_iota(jnp.int32, (tm, MEM_HEADS), 1)
        lse_t = lse_ref[...]
        for h in range(MEM_HEADS):
            sl = slice(MEM_DIM * h, MEM_DIM * (h + 1))
            qn, xn, r = _norm_fwd(q_ref[:, sl], g_ref[...])
            lse_h = jnp.sum(jnp.where(col == h, lse_t, 0.0), -1, keepdims=True)
            p = jnp.exp(_dot_nt(qn, k_ref[:, sl]) * scale - lse_h)
            do = do_ref[:, sl]
            dd = jnp.sum(do * y_ref[:, sl], -1, keepdims=True)
            dp = _dot_nt(do, v_ref[:, sl])
            ds = (p * (dp - dd)).astype(MXU)
            dv_ref[:, sl] += _dot_tn(p, do)
            dk_ref[:, sl] += _dot_tn(ds, qn) * scale
            dx, dg = _norm_bwd(xn, r, g_ref[...], _dot(ds, k_ref[:, sl]) * scale)
            dq_ref[:, sl] = dx.astype(dq_ref.dtype)
            dg_ref[...] += dg

    full = pl.BlockSpec((ml, hw), lambda i: (0, 0))
    return _pcall(
        body, name="mem_bwd", grid=(s // tm,),
        out_shape=[_sds((s, hw), MXU), _sds((ml, hw), jnp.float32), _sds((ml, hw), jnp.float32),
                   _sds((1, MEM_DIM), jnp.float32)],
        in_specs=[pl.BlockSpec((tm, hw), lambda i: (i, C_QM // hw)), pl.BlockSpec((1, MEM_DIM), lambda i: (0, 0)),
                  full, full, pl.BlockSpec((tm, hw), lambda i: (i, 3)), pl.BlockSpec((tm, hw), lambda i: (i, 0)),
                  pl.BlockSpec((tm, MEM_HEADS), lambda i: (i, 0))],
        out_specs=[pl.BlockSpec((tm, hw), lambda i: (i, 0)), full, full,
                   pl.BlockSpec((1, MEM_DIM), lambda i: (0, 0))],
        sem=("arbitrary",))(proj, g_mq, km, vmm, d_y, y_m, lse)


def _memkv_bwd(mem, g_mem, w_mkv, g_mk, kv, memn, dk, dv):
    ml, d = mem.shape
    hw = MEM_HEADS * MEM_DIM

    def body(mem_ref, g_ref, w_ref, gk_ref, kv_ref, mn_ref, dk_ref, dv_ref, dw_ref, dgm_ref, dgk_ref):
        parts = []
        dgk = jnp.zeros((1, MEM_DIM), jnp.float32)
        for h in range(MEM_HEADS):
            sl = slice(MEM_DIM * h, MEM_DIM * (h + 1))
            _, xn, r = _norm_fwd(kv_ref[:, sl], gk_ref[...])
            dx, dg = _norm_bwd(xn, r, gk_ref[...], dk_ref[:, sl])
            parts.append(dx)
            dgk = dgk + dg
        dkv = jnp.concatenate(parts + [dv_ref[...]], axis=1).astype(MXU)
        dgk_ref[...] = dgk
        dw_ref[...] = _dot_tn(mn_ref[...], dkv).astype(dw_ref.dtype)
        d_mn = _dot_nt(dkv, w_ref[...])
        _, xn, _ = _norm_fwd(mem_ref[...], g_ref[...])
        dgm_ref[...] = jnp.sum(d_mn * xn, 0, keepdims=True)

    vm = pl.BlockSpec(memory_space=pltpu.VMEM)
    return _pcall(
        body, name="memkv_bwd",
        out_shape=[_sds((d, 2 * hw), WIRE), _sds((1, d), jnp.float32), _sds((1, MEM_DIM), jnp.float32)],
        in_specs=[vm] * 8, out_specs=[vm] * 3)(mem, g_mem, w_mkv, g_mk, kv, memn, dk, dv)


def _mla_bwd(qc, kc, v, d_y, y_b, lse, after):
    nh, s, _ = qc.shape
    t = min(ATT_TILE, s)
    nb = s // t
    scale = (MLA_NOPE + MLA_ROPE) ** -0.5

    def body(q_ref, k_ref, v_ref, do_ref, y_ref, lse_ref, dq_ref, dk_ref, dv_ref, dk_acc, dv_acc):
        kj, qi = pl.program_id(1), pl.program_id(2)

        @pl.when((kj == 0) & (qi == 0))
        def _():
            dq_ref[...] = jnp.zeros_like(dq_ref)

        @pl.when(qi == kj)
        def _():
            dk_acc[...] = jnp.zeros_like(dk_acc)
            dv_acc[...] = jnp.zeros_like(dv_acc)

        def step(diagonal):
            rc = t // 4 if diagonal else t
            for c in range(t // rc):
                rows = slice(rc * c, rc * (c + 1))
                keys = slice(0, rc * (c + 1))
                q, k = q_ref[0, rows, :], k_ref[0, keys, :]
                sc = _dot_nt(q, k) * (scale * LOG2E)
                if diagonal:
                    r_i = lax.broadcasted_iota(jnp.int32, sc.shape, 0) + rc * c
                    c_i = lax.broadcasted_iota(jnp.int32, sc.shape, 1)
                    sc = jnp.where(c_i <= r_i, sc, NEG_INF)
                p = jnp.exp2(sc - lse_ref[0, rows, :])
                do = do_ref[rows, :]
                dd = jnp.sum(do * y_ref[rows, :], -1, keepdims=True)
                dp = _dot_nt(do, v_ref[0, keys, :])
                ds = (p * (dp - dd) * scale).astype(MXU)
                dv_acc[keys, :] += _dot_tn(p, do)
                dk_acc[keys, :] += _dot_tn(ds, q)
                out_rows = pl.ds(pl.multiple_of(qi * t + rc * c, rc), rc)
                dq_ref[0, out_rows, :] += _dot(ds, k)

        @pl.when(qi > kj)
        def _():
            step(False)

        @pl.when(qi == kj)
        def _():
            step(True)

        @pl.when(qi == nb - 1)
        def _():
            dk_ref[0] = dk_acc[...]
            dv_ref[0] = dv_acc[...]

    qmap = lambda h, j, i: (h, jnp.maximum(i, j), 0)
    return _pcall(
        body, name="mla_bwd", grid=(nh, nb, nb),
        out_shape=[_sds((nh, s, 256), jnp.float32), _sds((nh, s, 256), jnp.float32),
                   _sds((nh, s, MLA_V), jnp.float32)],
        in_specs=[pl.BlockSpec((1, t, 256), qmap),
                  pl.BlockSpec((1, t, 256), lambda h, j, i: (h, j, 0)),
                  pl.BlockSpec((1, t, MLA_V), lambda h, j, i: (h, j, 0)),
                  pl.BlockSpec((t, MLA_V), lambda h, j, i: (jnp.maximum(i, j), 8 + h)),
                  pl.BlockSpec((t, MLA_V), lambda h, j, i: (jnp.maximum(i, j), h)),
                  pl.BlockSpec((1, t, 1), qmap)],
        out_specs=[pl.BlockSpec((1, s, 256), lambda h, j, i: (h, 0, 0)),
                   pl.BlockSpec((1, t, 256), lambda h, j, i: (h, j, 0)),
                   pl.BlockSpec((1, t, MLA_V), lambda h, j, i: (h, j, 0))],
        scratch=[pltpu.VMEM((t, 256), jnp.float32), pltpu.VMEM((t, MLA_V), jnp.float32)],
        sem=("parallel", "arbitrary", "arbitrary"), after=after)(qc, kc, v, d_y, y_b, lse)


def _mla_prep_bwd(proj, cos, sin, g_cq, g_ckv, w_uq, w_ukv, g_qn, g_qr, g_kn, g_kr,
                  qb, kvb, cqn, ckvn, dqc, dkc, dv):
    s = proj.shape[0]
    tm = min(ROW_TILE, s)
    nh = MLA_HEADS
    ni = s // tm

    def body(cq_ref, ckv_ref, kr_ref, cos_ref, sin_ref, gcq_ref, gckv_ref, wuq_ref, wukv_ref,
             gqn_ref, gqr_ref, gkn_ref, gkr_ref, qb_ref, kvb_ref, cqn_ref, ckvn_ref, dqc_ref, dkc_ref, dv_ref,
             dcq_ref, dckv_ref, dkr_ref, dwuq_ref, dwukv_ref,
             dgcq_ref, dgckv_ref, dgqn_ref, dgqr_ref, dgkn_ref, dgkr_ref, acc_uq, acc_ukv):
        i = pl.program_id(0)

        @pl.when(i == 0)
        def _():
            acc_uq[...] = jnp.zeros_like(acc_uq)
            acc_ukv[...] = jnp.zeros_like(acc_ukv)
            for ref in (dgcq_ref, dgckv_ref, dgqn_ref, dgqr_ref, dgkn_ref, dgkr_ref):
                ref[...] = jnp.zeros_like(ref)

        cos_t, sin_t = cos_ref[...], sin_ref[...]
        lo = _lo_mask((tm, LANES))
        qb_v, kvb_v = qb_ref[...], kvb_ref[...]
        dq_parts, dgqn = [], jnp.zeros((1, LANES), jnp.float32)
        for h in range(nh):
            _, xn, r = _norm_fwd(qb_v[:, MLA_NOPE * h: MLA_NOPE * (h + 1)], gqn_ref[...])
            dx, dg = _norm_bwd(xn, r, gqn_ref[...], dqc_ref[h][:, :MLA_NOPE])
            dq_parts.append(dx)
            dgqn = dgqn + dg
        dgqn_ref[...] += dgqn
        dgqr = jnp.zeros((1, LANES), jnp.float32)
        for j in range(nh // 2):
            d_rope = jnp.where(lo, dqc_ref[2 * j][:, MLA_NOPE:], dqc_ref[2 * j + 1][:, MLA_NOPE:])
            d_pre = _rope_bwd(d_rope, cos_t, sin_t)
            xr = qb_v[:, nh * MLA_NOPE + LANES * j: nh * MLA_NOPE + LANES * (j + 1)]
            _, xn, r = _norm_fwd(xr, gqr_ref[...], half=True)
            dx, dg = _norm_bwd(xn, r, gqr_ref[...], d_pre, half=True)
            dq_parts.append(dx)
            dgqr = dgqr + dg
        dgqr_ref[...] += dgqr
        dqb = jnp.concatenate(dq_parts, axis=1).astype(MXU)
        acc_uq[...] += _dot_tn(dqb, cqn_ref[...])
        _, xn, r = _norm_fwd(cq_ref[...], gcq_ref[...])
        dx, dg = _norm_bwd(xn, r, gcq_ref[...], _dot(dqb, wuq_ref[...]))
        dcq_ref[...] = dx.astype(dcq_ref.dtype)
        dgcq_ref[...] += dg
        dkv_parts, dgkn = [], jnp.zeros((1, LANES), jnp.float32)
        d_kr2 = jnp.zeros((tm, LANES), jnp.float32)
        for h in range(nh):
            _, xn, r = _norm_fwd(kvb_v[:, 256 * h: 256 * h + MLA_NOPE], gkn_ref[...])
            dx, dg = _norm_bwd(xn, r, gkn_ref[...], dkc_ref[h][:, :MLA_NOPE])
            dkv_parts += [dx, dv_ref[h]]
            dgkn = dgkn + dg
            d_kr2 = d_kr2 + dkc_ref[h][:, MLA_NOPE:]
        dgkn_ref[...] += dgkn
        dkvb = jnp.concatenate(dkv_parts, axis=1).astype(MXU)
        d_ckvn = jnp.zeros((tm, 512), jnp.float32)
        for dev in range(N_DEV):
            piece = dkvb[:, LANES * dev: LANES * (dev + 1)]
            acc_ukv[dev] += _dot_tn(ckvn_ref[...], piece)
            d_ckvn = d_ckvn + _dot_nt(piece, wukv_ref[dev])
        _, xn, r = _norm_fwd(ckv_ref[...], gckv_ref[...])
        dx, dg = _norm_bwd(xn, r, gckv_ref[...], d_ckvn)
        dckv_ref[...] = dx.astype(dckv_ref.dtype)
        dgckv_ref[...] += dg
        d_kr = jnp.where(lo, d_kr2 + pltpu.roll(d_kr2, 64, 1), 0.0)
        d_pre = _rope_bwd(d_kr, cos_t, sin_t)
        _, xn, r = _norm_fwd(kr_ref[...], gkr_ref[...], half=True)
        dx, dg = _norm_bwd(xn, r, gkr_ref[...], d_pre, half=True)
        dkr_ref[...] = jnp.where(lo, dx, 0.0).astype(dkr_ref.dtype)
        dgkr_ref[...] += jnp.where(_lo_mask((1, LANES)), dg, 0.0)

        @pl.when(i == ni - 1)
        def _():
            dwuq_ref[...] = acc_uq[...].astype(dwuq_ref.dtype)
            dwukv_ref[...] = acc_ukv[...].astype(dwukv_ref.dtype)

    def col(width, start):
        return pl.BlockSpec((tm, width), lambda i: (i, start // width))

    def full(shape):
        return pl.BlockSpec(shape, lambda i: (0,) * len(shape))

    def row(width):
        return pl.BlockSpec((tm, width), lambda i: (i, 0))

    def heads(width):
        return pl.BlockSpec((nh, tm, width), lambda i: (0, i, 0))

    vec = full((1, LANES))
    return _pcall(
        body, name="mla_prep_bwd", grid=(ni,),
        out_shape=[_sds((s, 512), MXU), _sds((s, 512), MXU), _sds((s, LANES), MXU),
                   _sds((768, 512), WIRE), _sds((N_DEV, 512, LANES), WIRE),
                   _sds((1, 512), jnp.float32), _sds((1, 512), jnp.float32)] + [_sds((1, LANES), jnp.float32)] * 4,
        in_specs=[col(512, C_CQ), col(512, C_CKV), col(LANES, C_KR), row(LANES), row(LANES),
                  full((1, 512)), full((1, 512)), full((768, 512)), full((N_DEV, 512, LANES)), vec, vec, vec, vec,
                  row(768), row(1024), row(512), row(512), heads(256), heads(256), heads(MLA_V)],
        out_specs=[row(512), row(512), row(LANES), full((768, 512)), full((N_DEV, 512, LANES)),
                   full((1, 512)), full((1, 512)), vec, vec, vec, vec],
        scratch=[pltpu.VMEM((768, 512), jnp.float32), pltpu.VMEM((N_DEV, 512, LANES), jnp.float32)],
        sem=("arbitrary",))(proj, proj, proj, cos, sin, g_cq, g_ckv, w_uq, w_ukv, g_qn, g_qr, g_kn, g_kr,
                            qb, kvb, cqn, ckvn, dqc, dkc, dv)


def _swa_bwd(proj, posc, posr, gq, gk, sinks, d_y, y_a, lse, after):
    s = proj.shape[0]
    b = SWA_BLOCK
    nb = s // b
    scale = SWA_DIM ** -0.5

    def body(q_ref, kp_ref, kc_ref, vp_ref, vc_ref, pq_ref, pkp_ref, pkc_ref, gq_ref, gk_ref, sink_ref,
             do_ref, y_ref, lse_ref, kfull_ref,
             dq_ref, dk_ref, dv_ref, dgq_ref, dgk_ref, dsink_ref, dk_acc, dv_acc):
        n = pl.program_id(0)

        @pl.when(n == 0)
        def _():
            dk_acc[...] = jnp.zeros_like(dk_acc)
            dv_acc[...] = jnp.zeros_like(dv_acc)
            dgq_ref[...] = jnp.zeros_like(dgq_ref)
            dsink_ref[...] = jnp.zeros_like(dsink_ref)

        kn, v, bias = _swa_common(n, kp_ref[...], kc_ref[...], vp_ref[...], vc_ref[...],
                                  pq_ref[...], pkp_ref[...], pkc_ref[...], gk_ref[...])
        lo = _lo_mask((b, LANES))
        col = lax.broadcasted_iota(jnp.int32, (b, SWA_Q_HEADS), 1)
        col1 = lax.broadcasted_iota(jnp.int32, (1, SWA_Q_HEADS), 1)
        lse_t = lse_ref[...]
        dk_blk = jnp.zeros((2 * b, LANES), jnp.float32)
        dv_blk = jnp.zeros((2 * b, LANES), jnp.float32)
        dgq = jnp.zeros((1, LANES), jnp.float32)
        dsink = jnp.zeros((1, SWA_Q_HEADS), jnp.float32)
        for j in range(SWA_Q_HEADS // 2):
            hk = (2 * j) // (SWA_Q_HEADS // SWA_KV_HEADS)
            kvmask = lo if hk == 0 else jnp.logical_not(lo)
            sl = slice(LANES * j, LANES * (j + 1))
            qn, xn, r = _norm_fwd(q_ref[:, sl], gq_ref[...], half=True)
            qn = qn * scale
            qsw = pltpu.roll(qn, 64, 1)
            d2 = do_ref[:, sl]
            d2sw = pltpu.roll(d2, 64, 1)
            prod = d2 * y_ref[:, sl]
            dqs = []
            for e in range(2):
                h = 2 * j + e
                half_e = lo if e == 0 else jnp.logical_not(lo)
                qm = jnp.where(kvmask, qn if e == hk else qsw, 0.0)
                dm = jnp.where(kvmask, d2 if e == hk else d2sw, 0.0)
                sc = _dot_nt(qm, kn) + _alibi_slope(h) * bias
                lse_h = jnp.sum(jnp.where(col == h, lse_t, 0.0), -1, keepdims=True)
                p = jnp.exp(sc - lse_h)
                dd = jnp.sum(jnp.where(half_e, prod, 0.0), -1, keepdims=True)
                dp = _dot_nt(dm, v)
                ds = (p * (dp - dd)).astype(MXU)
                dsink = dsink - jnp.where(col1 == h, jnp.sum(jnp.exp(sink_ref[h] - lse_h) * dd), 0.0)
                dq_m = _dot(ds, kn) * scale
                dk_blk = dk_blk + _dot_tn(ds, qm)
                dv_blk = dv_blk + _dot_tn(p, dm)
                dqs.append(dq_m if e == hk else pltpu.roll(dq_m, 64, 1))
            dx, dg = _norm_bwd(xn, r, gq_ref[...], jnp.where(lo, dqs[0], dqs[1]), half=True)
            dq_ref[:, sl] = dx.astype(dq_ref.dtype)
            dgq = dgq + dg
        dgq_ref[...] += dgq
        dsink_ref[...] += dsink
        prev = pl.ds(pl.multiple_of(jnp.maximum(n - 1, 0) * b, b), b)
        cur = pl.ds(pl.multiple_of(n * b, b), b)
        dk_acc[prev, :] += dk_blk[:b]
        dv_acc[prev, :] += dv_blk[:b]
        dk_acc[cur, :] += dk_blk[b:]
        dv_acc[cur, :] += dv_blk[b:]

        @pl.when(n == nb - 1)
        def _():
            _, kxn, kr = _norm_fwd(kfull_ref[...], gk_ref[...], half=True)
            dx, dg = _norm_bwd(kxn, kr, gk_ref[...], dk_acc[...], half=True)
            dk_ref[...] = dx.astype(dk_ref.dtype)
            dv_ref[...] = dv_acc[...].astype(dv_ref.dtype)
            dgk_ref[...] = dg

    full = pl.BlockSpec((s, LANES), lambda n: (0, 0))
    vec = pl.BlockSpec((1, LANES), lambda n: (0, 0))
    return _pcall(
        body, name="swa_bwd", grid=(nb,),
        out_shape=[_sds((s, 1024), MXU), _sds((s, LANES), MXU), _sds((s, LANES), MXU),
                   _sds((1, LANES), jnp.float32), _sds((1, LANES), jnp.float32),
                   _sds((1, SWA_Q_HEADS), jnp.float32)],
        in_specs=_swa_specs(s) + [pl.BlockSpec((b, 1024), lambda n: (n, 0)), pl.BlockSpec((b, 1024), lambda n: (n, 0)),
                                  pl.BlockSpec((b, SWA_Q_HEADS), lambda n: (n, 0)),
                                  pl.BlockSpec((s, LANES), lambda n: (0, C_KA // LANES))],
        out_specs=[pl.BlockSpec((b, 1024), lambda n: (n, 0)), full, full, vec, vec,
                   pl.BlockSpec((1, SWA_Q_HEADS), lambda n: (0, 0))],
        scratch=[pltpu.VMEM((s, LANES), jnp.float32), pltpu.VMEM((s, LANES), jnp.float32)],
        sem=("arbitrary",), after=after)(proj, proj, proj, proj, proj, posc, posr, posr, gq, gk, sinks, d_y, y_a, lse,
                                         proj)


def _dx(d_proj, w_in, x, g, d_h1, after):
    s, d = x.shape
    n = w_in.shape[0]
    tm = min(ROW_TILE, s)

    def body(dp_ref, w_ref, x_ref, g_ref, dh_ref, dx_ref, dg_ref):
        i = pl.program_id(0)

        @pl.when(i == 0)
        def _():
            dg_ref[...] = jnp.zeros_like(dg_ref)

        d_hn = _dot(dp_ref[...], w_ref[...])
        _, xn, r = _norm_fwd(x_ref[...], g_ref[...])
        dx, dg = _norm_bwd(xn, r, g_ref[...], d_hn)
        dx_ref[...] = dh_ref[...] + dx
        dg_ref[...] += dg

    row = pl.BlockSpec((tm, d), lambda i: (i, 0))
    vec = pl.BlockSpec((1, d), lambda i: (0, 0))
    return _pcall(
        body, name="grad_x", grid=(s // tm,),
        out_shape=[_sds((s, d), jnp.float32), _sds((1, d), jnp.float32)],
        in_specs=[pl.BlockSpec((tm, n), lambda i: (i, 0)), pl.BlockSpec((n, d), lambda i: (0, 0)), row, vec, row],
        out_specs=[row, vec], sem=("arbitrary",), after=after)(d_proj, w_in, x, g, d_h1)


_SMALL = ["attn_norm_g", "swa_q_norm_g", "swa_k_norm_g", "swa_sinks", "mla_cq_norm_g", "mla_ckv_norm_g",
          "mla_qn_norm_g", "mla_qr_norm_g", "mla_kn_norm_g", "mla_kr_norm_g", "mem_norm_g",
          "mem_q_norm_g", "mem_k_norm_g", "ffn_norm_g"]


def kernel(x, mem, positions, attn_norm_g, w_in, swa_q_norm_g, swa_k_norm_g, swa_sinks, mla_cq_norm_g, mla_ckv_norm_g, w_uq, w_ukv, mla_qn_norm_g, mla_qr_norm_g, mla_kn_norm_g, mla_kr_norm_g, mem_norm_g, w_mem_kv, mem_q_norm_g, mem_k_norm_g, w_out, ffn_norm_g, w_gate, w_up, w_down, loss_target, m_attn_norm_g, m_w_in, m_swa_q_norm_g, m_swa_k_norm_g, m_swa_sinks, m_mla_cq_norm_g, m_mla_ckv_norm_g, m_w_uq, m_w_ukv, m_mla_qn_norm_g, m_mla_qr_norm_g, m_mla_kn_norm_g, m_mla_kr_norm_g, m_mem_norm_g, m_w_mem_kv, m_mem_q_norm_g, m_mem_k_norm_g, m_w_out, m_ffn_norm_g, m_w_gate, m_w_up, m_w_down, v_attn_norm_g, v_w_in, v_swa_q_norm_g, v_swa_k_norm_g, v_swa_sinks, v_mla_cq_norm_g, v_mla_ckv_norm_g, v_w_uq, v_w_ukv, v_mla_qn_norm_g, v_mla_qr_norm_g, v_mla_kn_norm_g, v_mla_kr_norm_g, v_mem_norm_g, v_w_mem_kv, v_mem_q_norm_g, v_mem_k_norm_g, v_w_out, v_ffn_norm_g, v_w_gate, v_w_up, v_w_down):
    args = dict(locals())
    x2, mem2, tgt = x[0], mem[0], loss_target[0]
    s, d = x2.shape
    n_in = w_in.shape[2]
    f = w_gate.shape[2]

    (g_in,) = _all_gather([w_in[0].T.astype(WIRE)])
    mix_shards = [w_uq[0].T.astype(WIRE), w_ukv[0].astype(WIRE), w_mem_kv[0].astype(WIRE),
                  _to_wire([w_out[0]], g_in, "wire_out")[0]]
    g_uq, wkv, g_mkv, g_out = _all_gather_background(mix_shards, 5, "all_gather_mix_weights")
    ffn_shards = [_to_wire([w_gate[0].T, w_up[0].T], g_in, "wire_gate_up"),
                  _to_wire([w_down[0]], g_in, "wire_down")[0]]
    w_gu, w_d = _all_gather_background(ffn_shards, 1, "all_gather_ffn_weights")
    wi = g_in.reshape(N_DEV * n_in, d)
    wi = jnp.concatenate([wi[0:1024], wi[1280:1792], wi[1792:2304], wi[2368:2880],
                          wi[1024:1152], wi[1152:1280], wi[2304:2368],
                          jnp.zeros((IN_PAD - 2880, d), wi.dtype)], axis=0)
    wq = g_uq.reshape(768, 512)
    wq = jnp.concatenate([wq[192 * h: 192 * h + 128] for h in range(4)]
                         + [wq[192 * h + 128: 192 * (h + 1)] for h in range(4)], axis=0)
    wmkv = g_mkv.reshape(-1, g_mkv.shape[-1])
    wo = g_out.reshape(-1, d)

    pos = positions[0].astype(jnp.float32)
    inv_freq = ROPE_THETA ** (-jnp.arange(0, MLA_ROPE, 2, dtype=jnp.float32) / MLA_ROPE)
    ang = pos[:, None] * inv_freq
    cos32, sin32 = jnp.cos(ang), jnp.sin(ang)
    cos_t = jnp.tile(cos32, (1, 4))
    sin_t = jnp.tile(jnp.concatenate([-sin32, sin32], axis=1), (1, 2))
    posc, posr = pos.reshape(s, 1), pos.reshape(1, s)
    two = lambda g: jnp.tile(g, (1, 2))
    gq2, gk2, gqr2, gkr2 = two(swa_q_norm_g), two(swa_k_norm_g), two(mla_qr_norm_g), two(mla_kr_norm_g)
    sinks1 = swa_sinks[0]

    proj, hn = _in_proj(x2, attn_norm_g, wi)
    qc, kc, vb, qb, kvb, cqn, ckvn = _mla_prep(proj, cos_t, sin_t, mla_cq_norm_g, mla_ckv_norm_g, wq, wkv,
                                                mla_qn_norm_g, gqr2, mla_kn_norm_g, gkr2)
    y_b, lse_b = _mla_fwd(qc, kc, vb)
    km, vmm, kvm, memn = _memkv_prep(mem2, mem_norm_g, wmkv, mem_k_norm_g)
    y_m, lse_m = _mem_fwd(proj, mem_q_norm_g, km, vmm)
    y_a, lse_a = _swa_fwd(proj, posc, posr, gq2, gk2, sinks1)
    h1, fn = _out_proj(y_a, y_b, y_m, x2, wo, ffn_norm_g)
    gu, act = _ffn_gu(fn, w_gu)
    dout, dout_b, loss_tile = _ffn_down(act, w_d, h1, tgt)

    dgu, dw_d = _ffn_bwd_act(dout_b, w_d, gu)
    dw_gu = _ffn_dw_gu(fn, dgu)
    r_gu, r_d = _exchange_grads_background([dw_gu, dw_d], 2, "exchange_ffn_grads")
    d_h1, dg_ffn = _ffn_norm_bwd(_ffn_dfn(dgu, w_gu, dw_gu), dout, h1, ffn_norm_g)
    d_y = _mm(d_h1, wo, tb=True, out_dtype=jnp.float32, tm=FFN_TILE, tk=2048, name="d_mix")
    dw_out = jnp.concatenate([
        _mm(y_a, d_h1, ta=True, out_dtype=WIRE, tm=1024, tk=1024, name="dw_out_a"),
        _mm(y_b, d_h1, ta=True, out_dtype=WIRE, tm=1024, tk=1024, name="dw_out_b"),
        _mm(y_m, d_h1, ta=True, out_dtype=WIRE, tm=1024, tk=1024, name="dw_out_m")], axis=0)
    d_qm, dkm, dvmm, dg_mq = _mem_bwd(proj, mem_q_norm_g, km, vmm, d_y, y_m, lse_m)
    dw_mkv, dg_mem, dg_mk = _memkv_bwd(mem2, mem_norm_g, wmkv, mem_k_norm_g, kvm, memn, dkm, dvmm)
    r_mkv, r_out = _exchange_grads_background([dw_mkv.reshape(g_mkv.shape), dw_out.reshape(g_out.shape)], 3,
                                              "exchange_mix_grads")
    dqc, dkc, dvb = _mla_bwd(qc, kc, vb, d_y, y_b, lse_b, dw_mkv)
    (d_cq, d_ckv, d_kr, dw_uq, dw_ukv, dg_cq, dg_ckv, dg_qn, dg_qr, dg_kn, dg_kr) = _mla_prep_bwd(
        proj, cos_t, sin_t, mla_cq_norm_g, mla_ckv_norm_g, wq, wkv, mla_qn_norm_g, gqr2, mla_kn_norm_g, gkr2,
        qb, kvb, cqn, ckvn, dqc, dkc, dvb)
    d_qa, d_ka, d_va, dg_q, dg_k, d_sinks = _swa_bwd(proj, posc, posr, gq2, gk2, sinks1, d_y, y_a, lse_a, dw_out)
    d_proj = jnp.concatenate([d_qa, d_cq, d_ckv, d_qm, d_ka, d_va, d_kr], axis=1)
    gi = _dw_in(hn, d_proj, n_in)

    gq_ = jnp.concatenate(sum([[dw_uq[128 * h: 128 * (h + 1)], dw_uq[512 + 64 * h: 512 + 64 * (h + 1)]]
                               for h in range(4)], []), axis=0)
    gq_ = gq_.reshape(N_DEV, 96, 512)
    r_in, r_uq, r_ukv = _exchange_grads_background([gi, gq_, dw_ukv], 4, "exchange_in_grads")
    grad_x, dg_attn = _dx(d_proj, wi, x2, attn_norm_g, d_h1, gi)

    big = {}
    def adam(name, r, transposed=False, after=None, which=None):
        w, m, v = args[name][0], args["m_" + name][0], args["v_" + name][0]
        if transposed:
            outs = _adam_big(r, w.T, m.T, v.T, "adam_" + name, after, which)
            return [o.T[None] for o in outs]
        return [o[None] for o in _adam_big(r, w, m, v, "adam_" + name, after)]
    big["w_gate"] = adam("w_gate", r_gu, True, which=0)
    big["w_up"] = adam("w_up", r_gu, True, after=big["w_gate"][0], which=1)
    big["w_down"] = adam("w_down", r_d, after=big["w_up"][0])
    big["w_out"] = adam("w_out", r_out, after=big["w_down"][0])
    big["w_mem_kv"] = adam("w_mem_kv", r_mkv, after=big["w_out"][0])
    big["w_in"] = adam("w_in", r_in, True, after=big["w_mem_kv"][0])
    big["w_uq"] = adam("w_uq", r_uq, True, after=big["w_in"][0])
    big["w_ukv"] = adam("w_ukv", r_ukv, after=big["w_uq"][0])

    small_g = {
        "attn_norm_g": dg_attn, "swa_q_norm_g": dg_q, "swa_k_norm_g": dg_k,
        "swa_sinks": d_sinks, "mla_cq_norm_g": dg_cq, "mla_ckv_norm_g": dg_ckv, "mla_qn_norm_g": dg_qn,
        "mla_qr_norm_g": dg_qr, "mla_kn_norm_g": dg_kn, "mla_kr_norm_g": dg_kr,
        "mem_norm_g": dg_mem, "mem_q_norm_g": dg_mq, "mem_k_norm_g": dg_mk, "ffn_norm_g": dg_ffn}
    loss11, small_out = _small_allreduce_adam(
        [small_g[n] for n in _SMALL], loss_tile, [args[n] for n in _SMALL],
        [args["m_" + n] for n in _SMALL], [args["v_" + n] for n in _SMALL])
    small = dict(zip(_SMALL, small_out))
    loss = loss11.reshape(())

    order = ["attn_norm_g", "w_in", "swa_q_norm_g", "swa_k_norm_g", "swa_sinks", "mla_cq_norm_g", "mla_ckv_norm_g",
             "w_uq", "w_ukv", "mla_qn_norm_g", "mla_qr_norm_g", "mla_kn_norm_g", "mla_kr_norm_g", "mem_norm_g",
             "w_mem_kv", "mem_q_norm_g", "mem_k_norm_g", "w_out", "ffn_norm_g", "w_gate", "w_up", "w_down"]
    res = {n: (big[n] if n in big else list(small[n])) for n in order}
    outs = [loss, grad_x[None]]
    for kind in range(4):
        outs += [res[n][kind] for n in order]
    return tuple(outs)
```

```python
import jax
import jax.numpy as jnp
from jax import lax
from jax.experimental import pallas as pl
from jax.experimental.pallas import tpu as pltpu
from jax.experimental.pallas import tpu_sc as plsc

MXU = jnp.bfloat16
WIRE = jnp.bfloat16
EPS = 1e-6
NEG_INF = -1e30
LOG2E = 1.4426950408889634
N_DEV = 8
LANES = 128
ROW_TILE = 256
FFN_TILE = 512
ATT_TILE = 1024
SWA_BLOCK = 128
VMEM_LIMIT = 56 * 1024 * 1024

SWA_Q_HEADS, SWA_KV_HEADS, SWA_DIM = 16, 2, 64
MLA_HEADS, MLA_NOPE, MLA_ROPE, MLA_V = 4, 128, 64, 128
MEM_HEADS, MEM_DIM = 4, 128
ROPE_THETA = 10000.0
ADAM_LR, ADAM_B1, ADAM_B2, ADAM_EPS, ADAM_WD, ADAM_STEP = 0.001, 0.9, 0.999, 1e-08, 0.01, 10

C_QA, C_CQ, C_CKV, C_QM, C_KA, C_VA, C_KR, IN_PAD = 0, 1024, 1536, 2048, 2560, 2688, 2816, 2944


def _pcall(body, *, name, out_shape, in_specs, out_specs, grid=(), scratch=(), sem=None, after=None):
    params = pltpu.CompilerParams(dimension_semantics=sem, vmem_limit_bytes=VMEM_LIMIT)
    if after is not None:
        n_in, inner = len(in_specs), body

        def body(*refs):
            inner(*refs[:n_in], *refs[n_in + 1:])

        in_specs = list(in_specs) + [pl.BlockSpec(memory_space=pl.ANY)]
    call = pl.pallas_call(body, name=name, grid=grid, in_specs=in_specs, out_specs=out_specs,
                          out_shape=out_shape, scratch_shapes=list(scratch), compiler_params=params)
    return call if after is None else (lambda *ops: call(*ops, after))


def _sds(shape, dtype):
    return jax.ShapeDtypeStruct(tuple(shape), dtype)


def _dot(a, b):
    return jnp.dot(a.astype(MXU), b.astype(MXU), preferred_element_type=jnp.float32)


def _dot_nt(a, b):
    return lax.dot_general(a.astype(MXU), b.astype(MXU), (((1,), (1,)), ((), ())),
                           preferred_element_type=jnp.float32)


def _dot_tn(a, b):
    return lax.dot_general(a.astype(MXU), b.astype(MXU), (((0,), (0,)), ((), ())),
                           preferred_element_type=jnp.float32)


def _lo_mask(shape):
    return (lax.broadcasted_iota(jnp.int32, shape, len(shape) - 1) % LANES) < 64


def _norm_fwd(x, g, half=False):
    x2 = x * x
    if half:
        lo = _lo_mask(x.shape)
        s_lo = jnp.sum(jnp.where(lo, x2, 0.0), -1, keepdims=True)
        s_hi = jnp.sum(jnp.where(lo, 0.0, x2), -1, keepdims=True)
        r = jnp.where(lo, lax.rsqrt(s_lo / 64.0 + EPS), lax.rsqrt(s_hi / 64.0 + EPS))
    else:
        r = lax.rsqrt(jnp.mean(x2, -1, keepdims=True) + EPS)
    xn = x * r
    return xn * g, xn, r


def _norm_bwd(xn, r, g, dy, half=False):
    t = dy * g
    tx = t * xn
    if half:
        lo = _lo_mask(xn.shape)
        m_lo = jnp.sum(jnp.where(lo, tx, 0.0), -1, keepdims=True) / 64.0
        m_hi = jnp.sum(jnp.where(lo, 0.0, tx), -1, keepdims=True) / 64.0
        m = jnp.where(lo, m_lo, m_hi)
    else:
        m = jnp.mean(tx, -1, keepdims=True)
    dx = r * (t - xn * m)
    dg = jnp.sum(dy * xn, 0, keepdims=True)
    return dx, dg


def _swap32(x):
    lane = lax.broadcasted_iota(jnp.int32, x.shape, 1)
    return jnp.where((lane % 64) < 32, pltpu.roll(x, 96, 1), pltpu.roll(x, 32, 1))


def _rope(x, cos, sin):
    return x * cos + _swap32(x) * sin


def _rope_bwd(d, cos, sin):
    return d * cos + _swap32(d * sin)


def _my_coords():
    return lax.axis_index("x"), lax.axis_index("y"), lax.axis_index("c")


def _dev_index(px, py, pc):
    return 4 * px + 2 * py + pc


_FLIPS = [(0, 0, 1), (0, 1, 0), (0, 1, 1), (1, 0, 0), (1, 0, 1), (1, 1, 0), (1, 1, 1)]


def _flip(coords, f):
    return tuple((1 - v) if b else v for v, b in zip(coords, f))


def _all_gather(shards):
    n = len(shards)

    def body(*refs):
        ins, outs = refs[:n], refs[n:2 * n]
        send_sems, recv_sems, local_sems = refs[2 * n:]
        x, y, c = _my_coords()
        me, sibling = (x, y, c), (x, y, 1 - c)
        chips = [(1 - x, y), (x, 1 - y), (1 - x, 1 - y)]

        def copy(w, k, block, to, src=None):
            dst = outs[w].at[_dev_index(*block)]
            return pltpu.make_async_remote_copy(
                src_ref=dst if src is None else src, dst_ref=dst,
                send_sem=send_sems.at[w, k], recv_sem=recv_sems.at[w, k],
                device_id=to, device_id_type=pl.DeviceIdType.MESH)

        sends, locals_ = [], []
        for w in range(n):
            mine = pltpu.make_async_copy(ins[w], outs[w].at[_dev_index(*me)], local_sems.at[w])
            mine.start()
            locals_.append(mine)
            first = [copy(w, 0, me, sibling, src=ins[w])]
            first += [copy(w, 1 + j, me, (*chip, c), src=ins[w]) for j, chip in enumerate(chips)]
            for cp in first:
                cp.start()
            sends += first
        for w in range(n):
            for j, chip in enumerate(chips):
                copy(w, 1 + j, (*chip, c), me).wait_recv()
                fwd = copy(w, 4 + j, (*chip, c), sibling)
                fwd.start()
                sends.append(fwd)
        for w in range(n):
            copy(w, 0, sibling, me).wait_recv()
            for j, chip in enumerate(chips):
                copy(w, 4 + j, (*chip, 1 - c), me).wait_recv()
        for cp in sends:
            cp.wait_send()
        for mine in locals_:
            mine.wait()

    any_spec = pl.BlockSpec(memory_space=pl.ANY)
    return _pcall(
        body, name="all_gather_weights",
        out_shape=[_sds((N_DEV,) + s.shape, s.dtype) for s in shards],
        in_specs=[any_spec] * n, out_specs=[any_spec] * n,
        scratch=[pltpu.SemaphoreType.DMA((n, 7)), pltpu.SemaphoreType.DMA((n, 7)),
                 pltpu.SemaphoreType.DMA((n,))])(*shards)


def _wire_cost(arrays):
    nbytes = sum(a.size * a.dtype.itemsize for a in arrays)
    return pl.CostEstimate(flops=0, transcendentals=0, bytes_accessed=40 * nbytes)


def _all_gather_background(shards, collective_id, name):
    n = len(shards)
    src_refs = [jax.new_ref(s, memory_space=pltpu.MemorySpace.HBM) for s in shards]
    out_refs = [jax.empty_ref(_sds((N_DEV,) + s.shape, s.dtype), memory_space=pltpu.MemorySpace.HBM) for s in shards]

    @pl.kernel(mesh=plsc.ScalarSubcoreMesh(axis_name="seq", num_cores=1), name=name,
               scratch_types=(pltpu.SemaphoreType.DMA((n, 7)), pltpu.SemaphoreType.DMA((n, 7)),
                              pltpu.SemaphoreType.DMA((n,))),
               compiler_params=pltpu.CompilerParams(collective_id=collective_id))
    def launch(send_sems, recv_sems, local_sems):
        x, y, c = _my_coords()
        me, sibling = (x, y, c), (x, y, 1 - c)
        chips = [(1 - x, y), (x, 1 - y), (1 - x, 1 - y)]
        barrier = pltpu.get_barrier_semaphore()
        for peer in [sibling] + [(*chip, c) for chip in chips]:
            pl.semaphore_signal(barrier, inc=1, device_id=peer, device_id_type=pl.DeviceIdType.MESH)
        pl.semaphore_wait(barrier, 4)

        def copy(w, k, block, to, src=None):
            dst = out_refs[w].at[_dev_index(*block)]
            return pltpu.make_async_remote_copy(
                src_ref=dst if src is None else src, dst_ref=dst,
                send_sem=send_sems.at[w, k], recv_sem=recv_sems.at[w, k],
                device_id=to, device_id_type=pl.DeviceIdType.MESH)

        sends, locals_ = [], []
        for w in range(n):
            mine = pltpu.make_async_copy(src_refs[w], out_refs[w].at[_dev_index(*me)], local_sems.at[w])
            mine.start()
            locals_.append(mine)
            first = [copy(w, 0, me, sibling, src=src_refs[w])]
            first += [copy(w, 1 + j, me, (*chip, c), src=src_refs[w]) for j, chip in enumerate(chips)]
            for cp in first:
                cp.start()
            sends += first
        for w in range(n):
            for j, chip in enumerate(chips):
                copy(w, 1 + j, (*chip, c), me).wait_recv()
                fwd = copy(w, 4 + j, (*chip, c), sibling)
                fwd.start()
                sends.append(fwd)
        for w in range(n):
            copy(w, 0, sibling, me).wait_recv()
            for j, chip in enumerate(chips):
                copy(w, 4 + j, (*chip, 1 - c), me).wait_recv()
        for cp in sends:
            cp.wait_send()
        for mine in locals_:
            mine.wait()

    launch()
    return [r[...] for r in out_refs]


def _exchange_grads(grads):
    n = len(grads)

    def body(*refs):
        ins, outs = refs[:n], refs[n:2 * n]
        send_sems, recv_sems, local_sems = refs[2 * n:]
        me = _my_coords()
        my_idx = _dev_index(*me)
        sends, locals_ = [], []
        for w in range(n):
            mine = pltpu.make_async_copy(ins[w].at[my_idx], outs[w].at[my_idx], local_sems.at[w])
            mine.start()
            locals_.append(mine)
            for k, f in enumerate(_FLIPS):
                peer = _flip(me, f)
                cp = pltpu.make_async_remote_copy(
                    src_ref=ins[w].at[_dev_index(*peer)], dst_ref=outs[w].at[my_idx],
                    send_sem=send_sems.at[w, k], recv_sem=recv_sems.at[w, k],
                    device_id=peer, device_id_type=pl.DeviceIdType.MESH)
                cp.start()
                sends.append(cp)
        for w in range(n):
            for k, f in enumerate(_FLIPS):
                peer = _flip(me, f)
                slot = outs[w].at[_dev_index(*peer)]
                pltpu.make_async_remote_copy(
                    src_ref=slot, dst_ref=slot,
                    send_sem=send_sems.at[w, k], recv_sem=recv_sems.at[w, k],
                    device_id=peer, device_id_type=pl.DeviceIdType.MESH).wait_recv()
        for cp in sends:
            cp.wait_send()
        for mine in locals_:
            mine.wait()

    any_spec = pl.BlockSpec(memory_space=pl.ANY)
    return _pcall(
        body, name="exchange_grads",
        out_shape=[_sds(g.shape, g.dtype) for g in grads],
        in_specs=[any_spec] * n, out_specs=[any_spec] * n,
        scratch=[pltpu.SemaphoreType.DMA((n, 7)), pltpu.SemaphoreType.DMA((n, 7)),
                 pltpu.SemaphoreType.DMA((n,))])(*grads)


def _exchange_grads_background(grads, collective_id, name):
    n = len(grads)
    src_refs = [jax.new_ref(g, memory_space=pltpu.MemorySpace.HBM) for g in grads]
    out_refs = [jax.empty_ref(_sds(g.shape, g.dtype), memory_space=pltpu.MemorySpace.HBM) for g in grads]

    @pl.kernel(mesh=plsc.ScalarSubcoreMesh(axis_name="seq", num_cores=1), name=name,
               scratch_types=(pltpu.SemaphoreType.DMA((n, 7)), pltpu.SemaphoreType.DMA((n, 7)),
                              pltpu.SemaphoreType.DMA((n,))),
               cost_estimate=_wire_cost(grads),
               compiler_params=pltpu.CompilerParams(collective_id=collective_id))
    def launch(send_sems, recv_sems, local_sems):
        me = _my_coords()
        my_idx = _dev_index(*me)
        peers = [_flip(me, f) for f in _FLIPS]
        barrier = pltpu.get_barrier_semaphore()
        for peer in peers:
            pl.semaphore_signal(barrier, inc=1, device_id=peer, device_id_type=pl.DeviceIdType.MESH)
        pl.semaphore_wait(barrier, len(peers))
        sends, locals_ = [], []
        for w in range(n):
            mine = pltpu.make_async_copy(src_refs[w].at[my_idx], out_refs[w].at[my_idx], local_sems.at[w])
            mine.start()
            locals_.append(mine)
            for k, peer in enumerate(peers):
                cp = pltpu.make_async_remote_copy(
                    src_ref=src_refs[w].at[_dev_index(*peer)], dst_ref=out_refs[w].at[my_idx],
                    send_sem=send_sems.at[w, k], recv_sem=recv_sems.at[w, k],
                    device_id=peer, device_id_type=pl.DeviceIdType.MESH)
                cp.start()
                sends.append(cp)
        for w in range(n):
            for k, peer in enumerate(peers):
                slot = out_refs[w].at[_dev_index(*peer)]
                pltpu.make_async_remote_copy(
                    src_ref=slot, dst_ref=slot, send_sem=send_sems.at[w, k], recv_sem=recv_sems.at[w, k],
                    device_id=peer, device_id_type=pl.DeviceIdType.MESH).wait_recv()
        for cp in sends:
            cp.wait_send()
        for mine in locals_:
            mine.wait()

    launch()
    return [r[...] for r in out_refs]


def _to_wire(parts, after, name):
    n = len(parts)
    rows, cols = parts[0].shape
    tr = rows // 2 if rows % 32 == 0 else rows

    def body(*refs):
        for k in range(n):
            refs[n][k] = refs[k][...].astype(WIRE)

    blk = pl.BlockSpec((tr, cols), lambda i: (i, 0))
    return _pcall(
        body, name=name, grid=(rows // tr,), out_shape=_sds((n, rows, cols), WIRE),
        in_specs=[blk] * n, out_specs=pl.BlockSpec((n, tr, cols), lambda i: (0, i, 0)),
        sem=("parallel",), after=after)(*parts)


def _adam_math(w, g, m, v):
    m = ADAM_B1 * m + (1.0 - ADAM_B1) * g
    v = ADAM_B2 * v + (1.0 - ADAM_B2) * (g * g)
    m_hat = m / (1.0 - ADAM_B1 ** ADAM_STEP)
    v_hat = v / (1.0 - ADAM_B2 ** ADAM_STEP)
    delta = -ADAM_LR * (m_hat / (jnp.sqrt(v_hat) + ADAM_EPS) + ADAM_WD * w)
    return delta, m, v


def _small_allreduce_adam(grads, loss_tile, ws, ms, vs):
    sizes = [w.shape[-1] for w in ws]
    n_par = len(ws)
    row0, r = [], 0
    for n in sizes:
        row0.append(r)
        r += -(-n // LANES)
    loss_row = r
    rows = -(-(r + 1) // 8) * 8

    def pieces(n):
        return [(k, min(LANES, n - LANES * k)) for k in range(-(-n // LANES))]

    def body(*refs):
        g_refs = refs[:n_par]
        loss_in = refs[n_par]
        w_refs = refs[n_par + 1: 2 * n_par + 1]
        m_refs = refs[2 * n_par + 1: 3 * n_par + 1]
        v_refs = refs[3 * n_par + 1: 4 * n_par + 1]
        loss_out = refs[4 * n_par + 1]
        out_refs = refs[4 * n_par + 2: 8 * n_par + 2]
        pack, gath, res, send_sems, recv_sems = refs[8 * n_par + 2:]
        me = _my_coords()
        my_idx = _dev_index(*me)

        def fill(slot, srcs):
            pack[slot] = jnp.zeros((rows, LANES), jnp.float32)
            for p, n in enumerate(sizes):
                val = srcs[p][...]
                if val.shape[-1] == LANES and n == 64:
                    pack[slot, row0[p]:row0[p] + 1, :] = val + pltpu.roll(val, 64, 1)
                    continue
                for k, width in pieces(n):
                    pack[slot, row0[p] + k:row0[p] + k + 1, 0:width] = srcs[p][:, LANES * k:LANES * k + width]

        fill(0, g_refs)
        pack[0, loss_row:loss_row + 1, :] = loss_in[0:1, :]
        gath[my_idx] = pack[0]
        sends = []
        for k, f in enumerate(_FLIPS):
            peer = _flip(me, f)
            cp = pltpu.make_async_remote_copy(
                src_ref=pack.at[0], dst_ref=gath.at[my_idx],
                send_sem=send_sems.at[k], recv_sem=recv_sems.at[k],
                device_id=peer, device_id_type=pl.DeviceIdType.MESH)
            cp.start()
            sends.append(cp)
        fill(1, w_refs)
        fill(2, m_refs)
        fill(3, v_refs)
        for k, f in enumerate(_FLIPS):
            peer = _flip(me, f)
            slot = gath.at[_dev_index(*peer)]
            pltpu.make_async_remote_copy(
                src_ref=slot, dst_ref=slot, send_sem=send_sems.at[k], recv_sem=recv_sems.at[k],
                device_id=peer, device_id_type=pl.DeviceIdType.MESH).wait_recv()
        for cp in sends:
            cp.wait_send()
        g = gath[0]
        for d in range(1, N_DEV):
            g = g + gath[d]
        delta, m, v = _adam_math(pack[1], g, pack[2], pack[3])
        res[0], res[1], res[2], res[3] = g, delta, m, v
        loss_out[...] = res[0, loss_row:loss_row + 1, 0:1]
        for p, n in enumerate(sizes):
            for kind in range(4):
                for k, width in pieces(n):
                    out_refs[4 * p + kind][:, LANES * k:LANES * k + width] = (
                        res[kind, row0[p] + k:row0[p] + k + 1, 0:width])

    vm = pl.BlockSpec(memory_space=pltpu.VMEM)
    out_shape = [_sds((1, 1), jnp.float32)]
    for n in sizes:
        out_shape += [_sds((1, n), jnp.float32)] * 4
    outs = _pcall(
        body, name="small_allreduce_adam", out_shape=out_shape,
        in_specs=[vm] * (4 * n_par + 1), out_specs=[vm] * len(out_shape),
        scratch=[pltpu.VMEM((4, rows, LANES), jnp.float32), pltpu.VMEM((N_DEV, rows, LANES), jnp.float32),
                 pltpu.VMEM((4, rows, LANES), jnp.float32),
                 pltpu.SemaphoreType.DMA((7,)), pltpu.SemaphoreType.DMA((7,))])(*grads, loss_tile, *ws, *ms, *vs)
    return outs[0], [outs[1 + 4 * p: 5 + 4 * p] for p in range(n_par)]


def _adam_big(recv, w, m, v, name, after=None, which=None):
    rows, cols = recv.shape[-2:]
    row_tiles = [t for t in range(16, rows + 1, 16) if rows % t == 0 and t * cols <= 400 * 1024]
    tr, tc = (max(row_tiles), cols) if row_tiles else (rows, 512 if cols % 512 == 0 else cols)

    def body(r_ref, w_ref, m_ref, v_ref, g_ref, d_ref, mo_ref, vo_ref):
        g = r_ref[0].astype(jnp.float32)
        for d in range(1, N_DEV):
            g = g + r_ref[d].astype(jnp.float32)
        delta, mn, vn = _adam_math(w_ref[...], g, m_ref[...], v_ref[...])
        g_ref[...] = g
        d_ref[...] = delta
        mo_ref[...] = mn
        vo_ref[...] = vn

    blk = pl.BlockSpec((tr, tc), lambda i, j: (i, j))
    if which is None:
        r_spec = pl.BlockSpec((N_DEV, tr, tc), lambda i, j: (0, i, j))
    else:
        r_spec = pl.BlockSpec((N_DEV, None, tr, tc), lambda i, j: (0, which, i, j))
    return _pcall(
        body, name=name, grid=(rows // tr, cols // tc),
        out_shape=[_sds((rows, cols), jnp.float32)] * 4,
        in_specs=[r_spec, blk, blk, blk],
        out_specs=[blk] * 4, sem=("parallel", "parallel"), after=after)(recv, w, m, v)


def _mm(a, b, *, ta=False, tb=False, out_dtype, tm, tk, name):
    (kdim, mdim) = a.shape if ta else a.shape[::-1]
    ndim = b.shape[0] if tb else b.shape[1]
    tm, tk = min(tm, mdim), min(tk, kdim)
    nk = kdim // tk

    def body(a_ref, b_ref, o_ref, acc):
        k = pl.program_id(1)
        if ta:
            part = _dot_tn(a_ref[...], b_ref[...])
        elif tb:
            part = _dot_nt(a_ref[...], b_ref[...])
        else:
            part = _dot(a_ref[...], b_ref[...])

        @pl.when(k == 0)
        def _():
            acc[...] = part

        @pl.when(k > 0)
        def _():
            acc[...] += part

        @pl.when(k == nk - 1)
        def _():
            o_ref[...] = acc[...].astype(o_ref.dtype)

    a_spec = pl.BlockSpec((tk, tm), lambda i, k: (k, i)) if ta else pl.BlockSpec((tm, tk), lambda i, k: (i, k))
    b_spec = pl.BlockSpec((ndim, tk), lambda i, k: (0, k)) if tb else pl.BlockSpec((tk, ndim), lambda i, k: (k, 0))
    return _pcall(
        body, name=name, grid=(mdim // tm, nk), out_shape=_sds((mdim, ndim), out_dtype),
        in_specs=[a_spec, b_spec], out_specs=pl.BlockSpec((tm, ndim), lambda i, k: (i, 0)),
        scratch=[pltpu.VMEM((tm, ndim), jnp.float32)], sem=("parallel", "arbitrary"))(a, b)


def _ref_col_pieces(start, stop):
    ref_starts = [0, 1024, 1152, 1280, 1792, 2304, 2368, 2880]
    perm_starts = [C_QA, C_KA, C_VA, C_CQ, C_CKV, C_KR, C_QM]
    out = []
    for p in range(7):
        lo, hi = max(start, ref_starts[p]), min(stop, ref_starts[p + 1])
        if lo < hi:
            out.append((lo - start, perm_starts[p] + lo - ref_starts[p], hi - lo))
    return out


def _dw_in(hn, d_proj, n_shard):
    s, d = hn.shape
    n = d_proj.shape[1]
    tm, tk = min(512, d), min(1024, s)
    nk = s // tk

    def body(a_ref, b_ref, o_ref, acc):
        k = pl.program_id(1)
        part = _dot_tn(a_ref[...], b_ref[...])

        @pl.when(k == 0)
        def _():
            acc[...] = part

        @pl.when(k > 0)
        def _():
            acc[...] += part

        @pl.when(k == nk - 1)
        def _():
            t = acc[...].T
            for j in range(N_DEV):
                rows = [t[src:src + width] for _, src, width in _ref_col_pieces(j * n_shard, (j + 1) * n_shard)]
                o_ref[j] = jnp.concatenate(rows, axis=0).astype(o_ref.dtype)

    return _pcall(
        body, name="dw_in", grid=(d // tm, nk), out_shape=_sds((N_DEV, n_shard, d), WIRE),
        in_specs=[pl.BlockSpec((tk, tm), lambda i, k: (k, i)), pl.BlockSpec((tk, n), lambda i, k: (k, 0))],
        out_specs=pl.BlockSpec((N_DEV, n_shard, tm), lambda i, k: (0, 0, i)),
        scratch=[pltpu.VMEM((tm, n), jnp.float32)], sem=("parallel", "arbitrary"))(hn, d_proj)


def _in_proj(x, g, w):
    s, d = x.shape
    n = w.shape[0]
    tm = min(ROW_TILE, s)

    def body(x_ref, g_ref, w_ref, p_ref, hn_ref):
        hn, _, _ = _norm_fwd(x_ref[...], g_ref[...])
        hn_ref[...] = hn.astype(hn_ref.dtype)
        p_ref[...] = _dot_nt(hn, w_ref[...])

    return _pcall(
        body, name="in_proj", grid=(s // tm,),
        out_shape=[_sds((s, n), jnp.float32), _sds((s, d), MXU)],
        in_specs=[pl.BlockSpec((tm, d), lambda i: (i, 0)), pl.BlockSpec((1, d), lambda i: (0, 0)),
                  pl.BlockSpec((n, d), lambda i: (0, 0))],
        out_specs=[pl.BlockSpec((tm, n), lambda i: (i, 0)), pl.BlockSpec((tm, d), lambda i: (i, 0))],
        sem=("parallel",))(x, g, w)


def _mla_prep(proj, cos, sin, g_cq, g_ckv, w_uq, w_ukv, g_qn, g_qr, g_kn, g_kr):
    s = proj.shape[0]
    tm = min(ROW_TILE, s)
    nh = MLA_HEADS

    def body(cq_ref, ckv_ref, kr_ref, cos_ref, sin_ref, gcq_ref, gckv_ref, wuq_ref, wukv_ref,
             gqn_ref, gqr_ref, gkn_ref, gkr_ref,
             qc_ref, kc_ref, v_ref, qb_ref, kvb_ref, cqn_ref, ckvn_ref):
        cos_t, sin_t = cos_ref[...], sin_ref[...]
        lo = _lo_mask((tm, LANES))
        cqn, _, _ = _norm_fwd(cq_ref[...], gcq_ref[...])
        cqn_ref[...] = cqn.astype(cqn_ref.dtype)
        qb = _dot_nt(cqn, wuq_ref[...])
        qb_ref[...] = qb
        ckvn, _, _ = _norm_fwd(ckv_ref[...], gckv_ref[...])
        ckvn_ref[...] = ckvn.astype(ckvn_ref.dtype)
        kvb = jnp.concatenate([_dot(ckvn, wukv_ref[dev]) for dev in range(N_DEV)], axis=1)
        kvb_ref[...] = kvb
        kr, _, _ = _norm_fwd(kr_ref[...], gkr_ref[...], half=True)
        kr = _rope(kr, cos_t, sin_t)
        kr2 = jnp.where(lo, kr, pltpu.roll(kr, 64, 1))
        ropes = []
        for j in range(nh // 2):
            xr = qb[:, nh * MLA_NOPE + LANES * j: nh * MLA_NOPE + LANES * (j + 1)]
            qr, _, _ = _norm_fwd(xr, gqr_ref[...], half=True)
            ropes.append(_rope(qr, cos_t, sin_t))
        for h in range(nh):
            qn, _, _ = _norm_fwd(qb[:, MLA_NOPE * h: MLA_NOPE * (h + 1)], gqn_ref[...])
            mask = lo if h % 2 == 0 else jnp.logical_not(lo)
            qr = jnp.where(mask, ropes[h // 2], 0.0)
            qc_ref[h] = jnp.concatenate([qn, qr], axis=1).astype(qc_ref.dtype)
            kn, _, _ = _norm_fwd(kvb[:, 256 * h: 256 * h + MLA_NOPE], gkn_ref[...])
            kc_ref[h] = jnp.concatenate([kn, kr2], axis=1).astype(kc_ref.dtype)
            v_ref[h] = kvb[:, 256 * h + MLA_NOPE: 256 * (h + 1)].astype(v_ref.dtype)

    def col(width, start):
        return pl.BlockSpec((tm, width), lambda i: (i, start // width))

    def full(shape):
        return pl.BlockSpec(shape, lambda i: (0,) * len(shape))

    def row(width):
        return pl.BlockSpec((tm, width), lambda i: (i, 0))

    def heads(width):
        return pl.BlockSpec((nh, tm, width), lambda i: (0, i, 0))

    return _pcall(
        body, name="mla_prep", grid=(s // tm,),
        out_shape=[_sds((nh, s, 256), MXU), _sds((nh, s, 256), MXU), _sds((nh, s, MLA_V), MXU),
                   _sds((s, 768), jnp.float32), _sds((s, 1024), jnp.float32),
                   _sds((s, 512), MXU), _sds((s, 512), MXU)],
        in_specs=[col(512, C_CQ), col(512, C_CKV), col(LANES, C_KR), row(LANES), row(LANES),
                  full((1, 512)), full((1, 512)), full((768, 512)), full((N_DEV, 512, LANES)),
                  full((1, LANES)), full((1, LANES)), full((1, LANES)), full((1, LANES))],
        out_specs=[heads(256), heads(256), heads(MLA_V), row(768), row(1024), row(512), row(512)],
        sem=("parallel",))(proj, proj, proj, cos, sin, g_cq, g_ckv, w_uq, w_ukv, g_qn, g_qr, g_kn, g_kr)


def _mla_fwd(qc, kc, v):
    nh, s, _ = qc.shape
    t = min(ATT_TILE, s)
    nb = s // t
    scale = (MLA_NOPE + MLA_ROPE) ** -0.5

    def body(q_ref, k_ref, v_ref, y_ref, lse_ref, m_sc, l_sc, acc):
        qi, ki = pl.program_id(1), pl.program_id(2)

        @pl.when(ki == 0)
        def _():
            m_sc[...] = jnp.full_like(m_sc, NEG_INF)
            l_sc[...] = jnp.zeros_like(l_sc)
            acc[...] = jnp.zeros_like(acc)

        def step(diagonal):
            rc = t // 4 if diagonal else t
            for c in range(t // rc):
                rows = slice(rc * c, rc * (c + 1))
                keys = slice(0, rc * (c + 1))
                sc = _dot_nt(q_ref[0, rows, :], k_ref[0, keys, :]) * (scale * LOG2E)
                if diagonal:
                    r_i = lax.broadcasted_iota(jnp.int32, sc.shape, 0) + rc * c
                    c_i = lax.broadcasted_iota(jnp.int32, sc.shape, 1)
                    sc = jnp.where(c_i <= r_i, sc, NEG_INF)
                m_old = m_sc[rows, :]
                m_new = jnp.maximum(m_old, jnp.max(sc, -1, keepdims=True))
                alpha = jnp.exp2(m_old - m_new)
                p = jnp.exp2(sc - m_new)
                l_sc[rows, :] = alpha * l_sc[rows, :] + jnp.sum(p, -1, keepdims=True)
                acc[rows, :] = alpha * acc[rows, :] + _dot(p, v_ref[0, keys, :])
                m_sc[rows, :] = m_new

        @pl.when(ki < qi)
        def _():
            step(False)

        @pl.when(ki == qi)
        def _():
            step(True)

        @pl.when(ki == qi)
        def _():
            y_ref[...] = acc[...] / l_sc[...]
            lse_ref[0] = m_sc[...] + jnp.log2(l_sc[...])

    return _pcall(
        body, name="mla_fwd", grid=(nh, nb, nb),
        out_shape=[_sds((s, nh * MLA_V), jnp.float32), _sds((nh, s, 1), jnp.float32)],
        in_specs=[pl.BlockSpec((1, t, 256), lambda h, i, k: (h, i, 0)),
                  pl.BlockSpec((1, t, 256), lambda h, i, k: (h, jnp.minimum(k, i), 0)),
                  pl.BlockSpec((1, t, MLA_V), lambda h, i, k: (h, jnp.minimum(k, i), 0))],
        out_specs=[pl.BlockSpec((t, MLA_V), lambda h, i, k: (i, h)),
                   pl.BlockSpec((1, t, 1), lambda h, i, k: (h, i, 0))],
        scratch=[pltpu.VMEM((t, 1), jnp.float32), pltpu.VMEM((t, 1), jnp.float32),
                 pltpu.VMEM((t, MLA_V), jnp.float32)],
        sem=("parallel", "parallel", "arbitrary"))(qc, kc, v)


def _memkv_prep(mem, g_mem, w_mkv, g_mk):
    ml, d = mem.shape
    hw = MEM_HEADS * MEM_DIM

    def body(mem_ref, g_ref, w_ref, gk_ref, k_ref, v_ref, kv_ref, mn_ref):
        mn, _, _ = _norm_fwd(mem_ref[...], g_ref[...])
        mn_ref[...] = mn.astype(mn_ref.dtype)
        kv = _dot(mn, w_ref[...])
        kv_ref[...] = kv
        for h in range(MEM_HEADS):
            kn, _, _ = _norm_fwd(kv[:, MEM_DIM * h: MEM_DIM * (h + 1)], gk_ref[...])
            k_ref[:, MEM_DIM * h: MEM_DIM * (h + 1)] = kn.astype(k_ref.dtype)
        v_ref[...] = kv[:, hw:].astype(v_ref.dtype)

    vm = pl.BlockSpec(memory_space=pltpu.VMEM)
    return _pcall(
        body, name="memkv_prep",
        out_shape=[_sds((ml, hw), MXU), _sds((ml, hw), MXU), _sds((ml, 2 * hw), jnp.float32), _sds((ml, d), MXU)],
        in_specs=[vm] * 4, out_specs=[vm] * 4)(mem, g_mem, w_mkv, g_mk)


def _mem_fwd(proj, g_mq, km, vmm):
    s = proj.shape[0]
    ml, hw = km.shape
    tm = min(FFN_TILE, s)
    scale = MEM_DIM ** -0.5

    def body(q_ref, g_ref, k_ref, v_ref, y_ref, lse_ref):
        col = lax.broadcasted_iota(jnp.int32, (tm, MEM_HEADS), 1)
        lse_t = jnp.zeros((tm, MEM_HEADS), jnp.float32)
        for h in range(MEM_HEADS):
            sl = slice(MEM_DIM * h, MEM_DIM * (h + 1))
            qn, _, _ = _norm_fwd(q_ref[:, sl], g_ref[...])
            sc = _dot_nt(qn, k_ref[:, sl]) * scale
            m = jnp.max(sc, -1, keepdims=True)
            p = jnp.exp(sc - m)
            l = jnp.sum(p, -1, keepdims=True)
            y_ref[:, sl] = _dot(p, v_ref[:, sl]) / l
            lse_t = jnp.where(col == h, m + jnp.log(l), lse_t)
        lse_ref[...] = lse_t

    return _pcall(
        body, name="mem_fwd", grid=(s // tm,),
        out_shape=[_sds((s, hw), jnp.float32), _sds((s, MEM_HEADS), jnp.float32)],
        in_specs=[pl.BlockSpec((tm, hw), lambda i: (i, C_QM // hw)), pl.BlockSpec((1, MEM_DIM), lambda i: (0, 0)),
                  pl.BlockSpec((ml, hw), lambda i: (0, 0)), pl.BlockSpec((ml, hw), lambda i: (0, 0))],
        out_specs=[pl.BlockSpec((tm, hw), lambda i: (i, 0)), pl.BlockSpec((tm, MEM_HEADS), lambda i: (i, 0))],
        sem=("parallel",))(proj, g_mq, km, vmm)


def _alibi_slope(h):
    return float(2.0 ** (-8.0 * (h + 1) / SWA_Q_HEADS))


def _swa_common(n, kp, kc, vp, vc, pq, pkp, pkc, gk):
    b = SWA_BLOCK
    k_raw = jnp.concatenate([kp, kc], axis=0)
    kn, kxn, kr = _norm_fwd(k_raw, gk, half=True)
    v = jnp.concatenate([vp, vc], axis=0)
    dist = jnp.abs(pq - jnp.concatenate([pkp, pkc], axis=1))
    r_i = lax.broadcasted_iota(jnp.int32, (b, 2 * b), 0)
    c_i = lax.broadcasted_iota(jnp.int32, (b, 2 * b), 1)
    valid = (c_i > r_i) & (c_i <= r_i + b) & (c_i >= jnp.where(n > 0, 0, b))
    bias = jnp.where(valid, -dist, NEG_INF)
    return kn, v, bias


def _swa_specs(s):
    b = SWA_BLOCK
    prev = lambda n: jnp.maximum(n - 1, 0)
    return [
        pl.BlockSpec((b, 1024), lambda n: (n, C_QA // 1024)),
        pl.BlockSpec((b, LANES), lambda n: (prev(n), C_KA // LANES)),
        pl.BlockSpec((b, LANES), lambda n: (n, C_KA // LANES)),
        pl.BlockSpec((b, LANES), lambda n: (prev(n), C_VA // LANES)),
        pl.BlockSpec((b, LANES), lambda n: (n, C_VA // LANES)),
        pl.BlockSpec((b, 1), lambda n: (n, 0)),
        pl.BlockSpec((1, b), lambda n: (0, prev(n))),
        pl.BlockSpec((1, b), lambda n: (0, n)),
        pl.BlockSpec((1, LANES), lambda n: (0, 0)),
        pl.BlockSpec((1, LANES), lambda n: (0, 0)),
        pl.BlockSpec(memory_space=pltpu.SMEM),
    ]


def _swa_fwd(proj, posc, posr, gq, gk, sinks):
    s = proj.shape[0]
    b = SWA_BLOCK
    scale = SWA_DIM ** -0.5

    def body(q_ref, kp_ref, kc_ref, vp_ref, vc_ref, pq_ref, pkp_ref, pkc_ref, gq_ref, gk_ref, sink_ref,
             y_ref, lse_ref):
        n = pl.program_id(0)
        kn, v, bias = _swa_common(n, kp_ref[...], kc_ref[...], vp_ref[...], vc_ref[...],
                                  pq_ref[...], pkp_ref[...], pkc_ref[...], gk_ref[...])
        lo = _lo_mask((b, LANES))
        col = lax.broadcasted_iota(jnp.int32, (b, SWA_Q_HEADS), 1)
        lse_t = jnp.zeros((b, SWA_Q_HEADS), jnp.float32)
        hpg = SWA_Q_HEADS // SWA_KV_HEADS
        for g in range(SWA_KV_HEADS):
            heads = range(hpg * g, hpg * (g + 1))
            kvmask = lo if g == 0 else jnp.logical_not(lo)
            qs = []
            for j in range(hpg // 2 * g, hpg // 2 * (g + 1)):
                qn, _, _ = _norm_fwd(q_ref[:, LANES * j: LANES * (j + 1)], gq_ref[...], half=True)
                qn = qn * scale
                qsw = pltpu.roll(qn, 64, 1)
                qs += [jnp.where(kvmask, qn if e == g else qsw, 0.0) for e in range(2)]
            sc_st = _dot_nt(jnp.concatenate(qs, axis=0), kn)
            ps, ls = [], []
            for i, h in enumerate(heads):
                sc = sc_st[b * i: b * (i + 1)] + _alibi_slope(h) * bias
                sk = sink_ref[h]
                m = jnp.maximum(jnp.max(sc, -1, keepdims=True), sk)
                p = jnp.exp(sc - m)
                l = jnp.sum(p, -1, keepdims=True) + jnp.exp(sk - m)
                ps.append(p.astype(MXU))
                ls.append(l)
                lse_t = jnp.where(col == h, m + jnp.log(l), lse_t)
            o_st = _dot(jnp.concatenate(ps, axis=0), v)
            for j in range(hpg // 2 * g, hpg // 2 * (g + 1)):
                halves = []
                for e in range(2):
                    i = 2 * j + e - hpg * g
                    o_h = o_st[b * i: b * (i + 1)] / ls[i]
                    halves.append(o_h if e == g else pltpu.roll(o_h, 64, 1))
                y_ref[:, LANES * j: LANES * (j + 1)] = jnp.where(lo, halves[0], halves[1])
        lse_ref[...] = lse_t

    return _pcall(
        body, name="swa_fwd", grid=(s // b,),
        out_shape=[_sds((s, 1024), jnp.float32), _sds((s, SWA_Q_HEADS), jnp.float32)],
        in_specs=_swa_specs(s),
        out_specs=[pl.BlockSpec((b, 1024), lambda n: (n, 0)), pl.BlockSpec((b, SWA_Q_HEADS), lambda n: (n, 0))],
        sem=("parallel",))(proj, proj, proj, proj, proj, posc, posr, posr, gq, gk, sinks)


def _out_proj(y_a, y_b, y_m, x, w_out, g_ffn):
    s, d = x.shape
    tm = min(ROW_TILE, s)

    def body(ya_ref, yb_ref, ym_ref, x_ref, w_ref, g_ref, h1_ref, fn_ref):
        y = jnp.concatenate([ya_ref[...].astype(MXU), yb_ref[...].astype(MXU), ym_ref[...].astype(MXU)], axis=1)
        h1 = x_ref[...] + _dot(y, w_ref[...])
        h1_ref[...] = h1
        fn, _, _ = _norm_fwd(h1, g_ref[...])
        fn_ref[...] = fn.astype(fn_ref.dtype)

    def row(width):
        return pl.BlockSpec((tm, width), lambda i: (i, 0))

    return _pcall(
        body, name="out_proj", grid=(s // tm,),
        out_shape=[_sds((s, d), jnp.float32), _sds((s, d), MXU)],
        in_specs=[row(1024), row(512), row(512), row(d), pl.BlockSpec(w_out.shape, lambda i: (0, 0)),
                  pl.BlockSpec((1, d), lambda i: (0, 0))],
        out_specs=[row(d), row(d)], sem=("parallel",))(y_a, y_b, y_m, x, w_out, g_ffn)


def _ffn_gu(fn, w_gu):
    s, d = fn.shape
    f = w_gu.shape[2]
    tm = min(2 * FFN_TILE, s)

    def body(fn_ref, w_ref, gu_ref, act_ref):
        x = fn_ref[...]
        g = _dot_nt(x, w_ref[0, 0])
        u = _dot_nt(x, w_ref[0, 1])
        gu_ref[0, 0] = g
        gu_ref[0, 1] = u
        act_ref[0] = (g * jax.nn.sigmoid(g) * u).astype(act_ref.dtype)

    return _pcall(
        body, name="ffn_gate_up", grid=(N_DEV, s // tm),
        out_shape=[_sds((N_DEV, 2, s, f), jnp.float32), _sds((N_DEV, s, f), MXU)],
        in_specs=[pl.BlockSpec((tm, d), lambda j, i: (i, 0)),
                  pl.BlockSpec((1, 2, f, d), lambda j, i: (j, 0, 0, 0))],
        out_specs=[pl.BlockSpec((1, 2, tm, f), lambda j, i: (j, 0, i, 0)),
                   pl.BlockSpec((1, tm, f), lambda j, i: (j, i, 0))],
        sem=("parallel", "parallel"))(fn, w_gu)


def _ffn_down(act, w_d, h1, target):
    _, s, f = act.shape
    d = h1.shape[1]
    tm = min(FFN_TILE, s)

    def body(a_ref, w_ref, h1_ref, t_ref, dout_ref, doutb_ref, loss_ref, acc):
        i, j = pl.program_id(0), pl.program_id(1)
        part = _dot(a_ref[0], w_ref[0]) + _dot(a_ref[1], w_ref[1])

        @pl.when(j == 0)
        def _():
            acc[...] = h1_ref[...] + part

        @pl.when(j > 0)
        def _():
            acc[...] += part

        @pl.when((i == 0) & (j == 0))
        def _():
            loss_ref[...] = jnp.zeros_like(loss_ref)

        @pl.when(j == N_DEV // 2 - 1)
        def _():
            diff = acc[...] - t_ref[...]
            dout_ref[...] = diff / d
            doutb_ref[...] = (diff / d).astype(doutb_ref.dtype)
            loss_ref[...] += 0.5 * jnp.sum(jnp.sum(diff * diff, -1, keepdims=True) / d)

    row = pl.BlockSpec((tm, d), lambda i, j: (i, 0))
    return _pcall(
        body, name="ffn_down", grid=(s // tm, N_DEV // 2),
        out_shape=[_sds((s, d), jnp.float32), _sds((s, d), MXU), _sds((8, LANES), jnp.float32)],
        in_specs=[pl.BlockSpec((2, tm, f), lambda i, j: (j, i, 0)), pl.BlockSpec((2, f, d), lambda i, j: (j, 0, 0)),
                  row, row],
        out_specs=[row, row, pl.BlockSpec((8, LANES), lambda i, j: (0, 0))],
        scratch=[pltpu.VMEM((tm, d), jnp.float32)], sem=("arbitrary", "arbitrary"))(act, w_d, h1, target)


def _ffn_bwd_act(dout, w_d, gu):
    s, d = dout.shape
    f = w_d.shape[1]
    tm = min(FFN_TILE, s)
    ni = s // tm

    def body(do_ref, w_ref, gu_ref, dgu_ref, dw_ref, acc):
        i = pl.program_id(1)
        do = do_ref[...]
        d_act = _dot_nt(do, w_ref[0])
        g, u = gu_ref[0, 0], gu_ref[0, 1]
        sig = jax.nn.sigmoid(g)
        silu = g * sig
        dgu_ref[0, 0] = (d_act * u * (sig * (1.0 + g * (1.0 - sig)))).astype(dgu_ref.dtype)
        dgu_ref[0, 1] = (d_act * silu).astype(dgu_ref.dtype)
        part = _dot_tn(silu * u, do)

        @pl.when(i == 0)
        def _():
            acc[...] = part

        @pl.when(i > 0)
        def _():
            acc[...] += part

        @pl.when(i == ni - 1)
        def _():
            dw_ref[0] = acc[...].astype(dw_ref.dtype)

    return _pcall(
        body, name="ffn_bwd_act", grid=(N_DEV, ni),
        out_shape=[_sds((N_DEV, 2, s, f), MXU), _sds((N_DEV, f, d), WIRE)],
        in_specs=[pl.BlockSpec((tm, d), lambda j, i: (i, 0)), pl.BlockSpec((1, f, d), lambda j, i: (j, 0, 0)),
                  pl.BlockSpec((1, 2, tm, f), lambda j, i: (j, 0, i, 0))],
        out_specs=[pl.BlockSpec((1, 2, tm, f), lambda j, i: (j, 0, i, 0)),
                   pl.BlockSpec((1, f, d), lambda j, i: (j, 0, 0))],
        scratch=[pltpu.VMEM((f, d), jnp.float32)], sem=("parallel", "arbitrary"))(dout, w_d, gu)


def _ffn_dw_gu(fn, dgu):
    s, d = fn.shape
    f = dgu.shape[-1]
    tk = min(2 * FFN_TILE, s)
    nk = s // tk

    def body(fn_ref, dgu_ref, dw_ref, acc):
        k = pl.program_id(1)
        x = fn_ref[...]
        pg = _dot_tn(dgu_ref[0, 0], x)
        pu = _dot_tn(dgu_ref[0, 1], x)

        @pl.when(k == 0)
        def _():
            acc[0] = pg
            acc[1] = pu

        @pl.when(k > 0)
        def _():
            acc[0] += pg
            acc[1] += pu

        @pl.when(k == nk - 1)
        def _():
            dw_ref[0] = acc[...].astype(dw_ref.dtype)

    return _pcall(
        body, name="ffn_dw_gate_up", grid=(N_DEV, nk),
        out_shape=_sds((N_DEV, 2, f, d), WIRE),
        in_specs=[pl.BlockSpec((tk, d), lambda j, k: (k, 0)), pl.BlockSpec((1, 2, tk, f), lambda j, k: (j, 0, k, 0))],
        out_specs=pl.BlockSpec((1, 2, f, d), lambda j, k: (j, 0, 0, 0)),
        scratch=[pltpu.VMEM((2, f, d), jnp.float32)], sem=("parallel", "arbitrary"))(fn, dgu)


def _ffn_dfn(dgu, w_gu, after):
    _, _, s, f = dgu.shape
    d = w_gu.shape[3]
    tm = min(FFN_TILE, s)

    def body(dgu_ref, w_ref, dfn_ref):
        j = pl.program_id(1)
        part = (_dot(dgu_ref[0, 0], w_ref[0, 0]) + _dot(dgu_ref[0, 1], w_ref[0, 1])
                + _dot(dgu_ref[1, 0], w_ref[1, 0]) + _dot(dgu_ref[1, 1], w_ref[1, 1]))

        @pl.when(j == 0)
        def _():
            dfn_ref[...] = part

        @pl.when(j > 0)
        def _():
            dfn_ref[...] += part

    return _pcall(
        body, name="ffn_dfn", grid=(s // tm, N_DEV // 2),
        out_shape=_sds((s, d), jnp.float32),
        in_specs=[pl.BlockSpec((2, 2, tm, f), lambda i, j: (j, 0, i, 0)),
                  pl.BlockSpec((2, 2, f, d), lambda i, j: (j, 0, 0, 0))],
        out_specs=pl.BlockSpec((tm, d), lambda i, j: (i, 0)),
        sem=("parallel", "arbitrary"), after=after)(dgu, w_gu)


def _ffn_norm_bwd(d_fn, dout, h1, g_ffn):
    s, d = h1.shape
    tm = min(ROW_TILE, s)

    def body(dfn_ref, do_ref, h1_ref, g_ref, dh1_ref, dg_ref):
        i = pl.program_id(0)

        @pl.when(i == 0)
        def _():
            dg_ref[...] = jnp.zeros_like(dg_ref)

        _, xn, r = _norm_fwd(h1_ref[...], g_ref[...])
        dx, dg = _norm_bwd(xn, r, g_ref[...], dfn_ref[...])
        dh1_ref[...] = do_ref[...] + dx
        dg_ref[...] += dg

    row = pl.BlockSpec((tm, d), lambda i: (i, 0))
    vec = pl.BlockSpec((1, d), lambda i: (0, 0))
    return _pcall(
        body, name="ffn_norm_bwd", grid=(s // tm,),
        out_shape=[_sds((s, d), jnp.float32), _sds((1, d), jnp.float32)],
        in_specs=[row, row, row, vec], out_specs=[row, vec], sem=("arbitrary",))(d_fn, dout, h1, g_ffn)


def _mem_bwd(proj, g_mq, km, vmm, d_y, y_m, lse):
    s = proj.shape[0]
    ml, hw = km.shape
    tm = min(FFN_TILE, s)
    scale = MEM_DIM ** -0.5

    def body(q_ref, g_ref, k_ref, v_ref, do_ref, y_ref, lse_ref, dq_ref, dk_ref, dv_ref, dg_ref):
        i = pl.program_id(0)

        @pl.when(i == 0)
        def _():
            dk_ref[...] = jnp.zeros_like(dk_ref)
            dv_ref[...] = jnp.zeros_like(dv_ref)
            dg_ref[...] = jnp.zeros_like(dg_ref)

        col = lax.broadcasted_iota(jnp.int32, (tm, MEM_HEADS), 1)
        lse_t = lse_ref[...]
        for h in range(MEM_HEADS):
            sl = slice(MEM_DIM * h, MEM_DIM * (h + 1))
            qn, xn, r = _norm_fwd(q_ref[:, sl], g_ref[...])
            lse_h = jnp.sum(jnp.where(col == h, lse_t, 0.0), -1, keepdims=True)
            p = jnp.exp(_dot_nt(qn, k_ref[:, sl]) * scale - lse_h)
            do = do_ref[:, sl]
            dd = jnp.sum(do * y_ref[:, sl], -1, keepdims=True)
            dp = _dot_nt(do, v_ref[:, sl])
            ds = (p * (dp - dd)).astype(MXU)
            dv_ref[:, sl] += _dot_tn(p, do)
            dk_ref[:, sl] += _dot_tn(ds, qn) * scale
            dx, dg = _norm_bwd(xn, r, g_ref[...], _dot(ds, k_ref[:, sl]) * scale)
            dq_ref[:, sl] = dx.astype(dq_ref.dtype)
            dg_ref[...] += dg

    full = pl.BlockSpec((ml, hw), lambda i: (0, 0))
    return _pcall(
        body, name="mem_bwd", grid=(s // tm,),
        out_shape=[_sds((s, hw), MXU), _sds((ml, hw), jnp.float32), _sds((ml, hw), jnp.float32),
                   _sds((1, MEM_DIM), jnp.float32)],
        in_specs=[pl.BlockSpec((tm, hw), lambda i: (i, C_QM // hw)), pl.BlockSpec((1, MEM_DIM), lambda i: (0, 0)),
                  full, full, pl.BlockSpec((tm, hw), lambda i: (i, 3)), pl.BlockSpec((tm, hw), lambda i: (i, 0)),
                  pl.BlockSpec((tm, MEM_HEADS), lambda i: (i, 0))],
        out_specs=[pl.BlockSpec((tm, hw), lambda i: (i, 0)), full, full,
                   pl.BlockSpec((1, MEM_DIM), lambda i: (0, 0))],
        sem=("arbitrary",))(proj, g_mq, km, vmm, d_y, y_m, lse)


def _memkv_bwd(mem, g_mem, w_mkv, g_mk, kv, memn, dk, dv):
    ml, d = mem.shape
    hw = MEM_HEADS * MEM_DIM

    def body(mem_ref, g_ref, w_ref, gk_ref, kv_ref, mn_ref, dk_ref, dv_ref, dw_ref, dgm_ref, dgk_ref):
        parts = []
        dgk = jnp.zeros((1, MEM_DIM), jnp.float32)
        for h in range(MEM_HEADS):
            sl = slice(MEM_DIM * h, MEM_DIM * (h + 1))
            _, xn, r = _norm_fwd(kv_ref[:, sl], gk_ref[...])
            dx, dg = _norm_bwd(xn, r, gk_ref[...], dk_ref[:, sl])
            parts.append(dx)
            dgk = dgk + dg
        dkv = jnp.concatenate(parts + [dv_ref[...]], axis=1).astype(MXU)
        dgk_ref[...] = dgk
        dw_ref[...] = _dot_tn(mn_ref[...], dkv).astype(dw_ref.dtype)
        d_mn = _dot_nt(dkv, w_ref[...])
        _, xn, _ = _norm_fwd(mem_ref[...], g_ref[...])
        dgm_ref[...] = jnp.sum(d_mn * xn, 0, keepdims=True)

    vm = pl.BlockSpec(memory_space=pltpu.VMEM)
    return _pcall(
        body, name="memkv_bwd",
        out_shape=[_sds((d, 2 * hw), WIRE), _sds((1, d), jnp.float32), _sds((1, MEM_DIM), jnp.float32)],
        in_specs=[vm] * 8, out_specs=[vm] * 3)(mem, g_mem, w_mkv, g_mk, kv, memn, dk, dv)


def _mla_bwd(qc, kc, v, d_y, y_b, lse, after):
    nh, s, _ = qc.shape
    t = min(ATT_TILE, s)
    nb = s // t
    scale = (MLA_NOPE + MLA_ROPE) ** -0.5

    def body(q_ref, k_ref, v_ref, do_ref, y_ref, lse_ref, dq_ref, dk_ref, dv_ref, dk_acc, dv_acc):
        kj, qi = pl.program_id(1), pl.program_id(2)

        @pl.when((kj == 0) & (qi == 0))
        def _():
            dq_ref[...] = jnp.zeros_like(dq_ref)

        @pl.when(qi == kj)
        def _():
            dk_acc[...] = jnp.zeros_like(dk_acc)
            dv_acc[...] = jnp.zeros_like(dv_acc)

        def step(diagonal):
            rc = t // 4 if diagonal else t
            for c in range(t // rc):
                rows = slice(rc * c, rc * (c + 1))
                keys = slice(0, rc * (c + 1))
                q, k = q_ref[0, rows, :], k_ref[0, keys, :]
                sc = _dot_nt(q, k) * (scale * LOG2E)
                if diagonal:
                    r_i = lax.broadcasted_iota(jnp.int32, sc.shape, 0) + rc * c
                    c_i = lax.broadcasted_iota(jnp.int32, sc.shape, 1)
                    sc = jnp.where(c_i <= r_i, sc, NEG_INF)
                p = jnp.exp2(sc - lse_ref[0, rows, :])
                do = do_ref[rows, :]
                dd = jnp.sum(do * y_ref[rows, :], -1, keepdims=True)
                dp = _dot_nt(do, v_ref[0, keys, :])
                ds = (p * (dp - dd) * scale).astype(MXU)
                dv_acc[keys, :] += _dot_tn(p, do)
                dk_acc[keys, :] += _dot_tn(ds, q)
                out_rows = pl.ds(pl.multiple_of(qi * t + rc * c, rc), rc)
                dq_ref[0, out_rows, :] += _dot(ds, k)

        @pl.when(qi > kj)
        def _():
            step(False)

        @pl.when(qi == kj)
        def _():
            step(True)

        @pl.when(qi == nb - 1)
        def _():
            dk_ref[0] = dk_acc[...]
            dv_ref[0] = dv_acc[...]

    qmap = lambda h, j, i: (h, jnp.maximum(i, j), 0)
    return _pcall(
        body, name="mla_bwd", grid=(nh, nb, nb),
        out_shape=[_sds((nh, s, 256), jnp.float32), _sds((nh, s, 256), jnp.float32),
                   _sds((nh, s, MLA_V), jnp.float32)],
        in_specs=[pl.BlockSpec((1, t, 256), qmap),
                  pl.BlockSpec((1, t, 256), lambda h, j, i: (h, j, 0)),
                  pl.BlockSpec((1, t, MLA_V), lambda h, j, i: (h, j, 0)),
                  pl.BlockSpec((t, MLA_V), lambda h, j, i: (jnp.maximum(i, j), 8 + h)),
                  pl.BlockSpec((t, MLA_V), lambda h, j, i: (jnp.maximum(i, j), h)),
                  pl.BlockSpec((1, t, 1), qmap)],
        out_specs=[pl.BlockSpec((1, s, 256), lambda h, j, i: (h, 0, 0)),
                   pl.BlockSpec((1, t, 256), lambda h, j, i: (h, j, 0)),
                   pl.BlockSpec((1, t, MLA_V), lambda h, j, i: (h, j, 0))],
        scratch=[pltpu.VMEM((t, 256), jnp.float32), pltpu.VMEM((t, MLA_V), jnp.float32)],
        sem=("parallel", "arbitrary", "arbitrary"), after=after)(qc, kc, v, d_y, y_b, lse)


def _mla_prep_bwd(proj, cos, sin, g_cq, g_ckv, w_uq, w_ukv, g_qn, g_qr, g_kn, g_kr,
                  qb, kvb, cqn, ckvn, dqc, dkc, dv):
    s = proj.shape[0]
    tm = min(ROW_TILE, s)
    nh = MLA_HEADS
    ni = s // tm

    def body(cq_ref, ckv_ref, kr_ref, cos_ref, sin_ref, gcq_ref, gckv_ref, wuq_ref, wukv_ref,
             gqn_ref, gqr_ref, gkn_ref, gkr_ref, qb_ref, kvb_ref, cqn_ref, ckvn_ref, dqc_ref, dkc_ref, dv_ref,
             dcq_ref, dckv_ref, dkr_ref, dwuq_ref, dwukv_ref,
             dgcq_ref, dgckv_ref, dgqn_ref, dgqr_ref, dgkn_ref, dgkr_ref, acc_uq, acc_ukv):
        i = pl.program_id(0)

        @pl.when(i == 0)
        def _():
            acc_uq[...] = jnp.zeros_like(acc_uq)
            acc_ukv[...] = jnp.zeros_like(acc_ukv)
            for ref in (dgcq_ref, dgckv_ref, dgqn_ref, dgqr_ref, dgkn_ref, dgkr_ref):
                ref[...] = jnp.zeros_like(ref)

        cos_t, sin_t = cos_ref[...], sin_ref[...]
        lo = _lo_mask((tm, LANES))
        qb_v, kvb_v = qb_ref[...], kvb_ref[...]
        dq_parts, dgqn = [], jnp.zeros((1, LANES), jnp.float32)
        for h in range(nh):
            _, xn, r = _norm_fwd(qb_v[:, MLA_NOPE * h: MLA_NOPE * (h + 1)], gqn_ref[...])
            dx, dg = _norm_bwd(xn, r, gqn_ref[...], dqc_ref[h][:, :MLA_NOPE])
            dq_parts.append(dx)
            dgqn = dgqn + dg
        dgqn_ref[...] += dgqn
        dgqr = jnp.zeros((1, LANES), jnp.float32)
        for j in range(nh // 2):
            d_rope = jnp.where(lo, dqc_ref[2 * j][:, MLA_NOPE:], dqc_ref[2 * j + 1][:, MLA_NOPE:])
            d_pre = _rope_bwd(d_rope, cos_t, sin_t)
            xr = qb_v[:, nh * MLA_NOPE + LANES * j: nh * MLA_NOPE + LANES * (j + 1)]
            _, xn, r = _norm_fwd(xr, gqr_ref[...], half=True)
            dx, dg = _norm_bwd(xn, r, gqr_ref[...], d_pre, half=True)
            dq_parts.append(dx)
            dgqr = dgqr + dg
        dgqr_ref[...] += dgqr
        dqb = jnp.concatenate(dq_parts, axis=1).astype(MXU)
        acc_uq[...] += _dot_tn(dqb, cqn_ref[...])
        _, xn, r = _norm_fwd(cq_ref[...], gcq_ref[...])
        dx, dg = _norm_bwd(xn, r, gcq_ref[...], _dot(dqb, wuq_ref[...]))
        dcq_ref[...] = dx.astype(dcq_ref.dtype)
        dgcq_ref[...] += dg
        dkv_parts, dgkn = [], jnp.zeros((1, LANES), jnp.float32)
        d_kr2 = jnp.zeros((tm, LANES), jnp.float32)
        for h in range(nh):
            _, xn, r = _norm_fwd(kvb_v[:, 256 * h: 256 * h + MLA_NOPE], gkn_ref[...])
            dx, dg = _norm_bwd(xn, r, gkn_ref[...], dkc_ref[h][:, :MLA_NOPE])
            dkv_parts += [dx, dv_ref[h]]
            dgkn = dgkn + dg
            d_kr2 = d_kr2 + dkc_ref[h][:, MLA_NOPE:]
        dgkn_ref[...] += dgkn
        dkvb = jnp.concatenate(dkv_parts, axis=1).astype(MXU)
        d_ckvn = jnp.zeros((tm, 512), jnp.float32)
        for dev in range(N_DEV):
            piece = dkvb[:, LANES * dev: LANES * (dev + 1)]
            acc_ukv[dev] += _dot_tn(ckvn_ref[...], piece)
            d_ckvn = d_ckvn + _dot_nt(piece, wukv_ref[dev])
        _, xn, r = _norm_fwd(ckv_ref[...], gckv_ref[...])
        dx, dg = _norm_bwd(xn, r, gckv_ref[...], d_ckvn)
        dckv_ref[...] = dx.astype(dckv_ref.dtype)
        dgckv_ref[...] += dg
        d_kr = jnp.where(lo, d_kr2 + pltpu.roll(d_kr2, 64, 1), 0.0)
        d_pre = _rope_bwd(d_kr, cos_t, sin_t)
        _, xn, r = _norm_fwd(kr_ref[...], gkr_ref[...], half=True)
        dx, dg = _norm_bwd(xn, r, gkr_ref[...], d_pre, half=True)
        dkr_ref[...] = jnp.where(lo, dx, 0.0).astype(dkr_ref.dtype)
        dgkr_ref[...] += jnp.where(_lo_mask((1, LANES)), dg, 0.0)

        @pl.when(i == ni - 1)
        def _():
            dwuq_ref[...] = acc_uq[...].astype(dwuq_ref.dtype)
            dwukv_ref[...] = acc_ukv[...].astype(dwukv_ref.dtype)

    def col(width, start):
        return pl.BlockSpec((tm, width), lambda i: (i, start // width))

    def full(shape):
        return pl.BlockSpec(shape, lambda i: (0,) * len(shape))

    def row(width):
        return pl.BlockSpec((tm, width), lambda i: (i, 0))

    def heads(width):
        return pl.BlockSpec((nh, tm, width), lambda i: (0, i, 0))

    vec = full((1, LANES))
    return _pcall(
        body, name="mla_prep_bwd", grid=(ni,),
        out_shape=[_sds((s, 512), MXU), _sds((s, 512), MXU), _sds((s, LANES), MXU),
                   _sds((768, 512), WIRE), _sds((N_DEV, 512, LANES), WIRE),
                   _sds((1, 512), jnp.float32), _sds((1, 512), jnp.float32)] + [_sds((1, LANES), jnp.float32)] * 4,
        in_specs=[col(512, C_CQ), col(512, C_CKV), col(LANES, C_KR), row(LANES), row(LANES),
                  full((1, 512)), full((1, 512)), full((768, 512)), full((N_DEV, 512, LANES)), vec, vec, vec, vec,
                  row(768), row(1024), row(512), row(512), heads(256), heads(256), heads(MLA_V)],
        out_specs=[row(512), row(512), row(LANES), full((768, 512)), full((N_DEV, 512, LANES)),
                   full((1, 512)), full((1, 512)), vec, vec, vec, vec],
        scratch=[pltpu.VMEM((768, 512), jnp.float32), pltpu.VMEM((N_DEV, 512, LANES), jnp.float32)],
        sem=("arbitrary",))(proj, proj, proj, cos, sin, g_cq, g_ckv, w_uq, w_ukv, g_qn, g_qr, g_kn, g_kr,
                            qb, kvb, cqn, ckvn, dqc, dkc, dv)


def _swa_bwd(proj, posc, posr, gq, gk, sinks, d_y, y_a, lse, after):
    s = proj.shape[0]
    b = SWA_BLOCK
    nb = s // b
    scale = SWA_DIM ** -0.5

    def body(q_ref, kp_ref, kc_ref, vp_ref, vc_ref, pq_ref, pkp_ref, pkc_ref, gq_ref, gk_ref, sink_ref,
             do_ref, y_ref, lse_ref, kfull_ref,
             dq_ref, dk_ref, dv_ref, dgq_ref, dgk_ref, dsink_ref, dk_acc, dv_acc):
        n = pl.program_id(0)

        @pl.when(n == 0)
        def _():
            dk_acc[...] = jnp.zeros_like(dk_acc)
            dv_acc[...] = jnp.zeros_like(dv_acc)
            dgq_ref[...] = jnp.zeros_like(dgq_ref)
            dsink_ref[...] = jnp.zeros_like(dsink_ref)

        kn, v, bias = _swa_common(n, kp_ref[...], kc_ref[...], vp_ref[...], vc_ref[...],
                                  pq_ref[...], pkp_ref[...], pkc_ref[...], gk_ref[...])
        lo = _lo_mask((b, LANES))
        col = lax.broadcasted_iota(jnp.int32, (b, SWA_Q_HEADS), 1)
        col1 = lax.broadcasted_iota(jnp.int32, (1, SWA_Q_HEADS), 1)
        lse_t = lse_ref[...]
        dk_blk = jnp.zeros((2 * b, LANES), jnp.float32)
        dv_blk = jnp.zeros((2 * b, LANES), jnp.float32)
        dgq = jnp.zeros((1, LANES), jnp.float32)
        dsink = jnp.zeros((1, SWA_Q_HEADS), jnp.float32)
        for j in range(SWA_Q_HEADS // 2):
            hk = (2 * j) // (SWA_Q_HEADS // SWA_KV_HEADS)
            kvmask = lo if hk == 0 else jnp.logical_not(lo)
            sl = slice(LANES * j, LANES * (j + 1))
            qn, xn, r = _norm_fwd(q_ref[:, sl], gq_ref[...], half=True)
            qn = qn * scale
            qsw = pltpu.roll(qn, 64, 1)
            d2 = do_ref[:, sl]
            d2sw = pltpu.roll(d2, 64, 1)
            prod = d2 * y_ref[:, sl]
            dqs = []
            for e in range(2):
                h = 2 * j + e
                half_e = lo if e == 0 else jnp.logical_not(lo)
                qm = jnp.where(kvmask, qn if e == hk else qsw, 0.0)
                dm = jnp.where(kvmask, d2 if e == hk else d2sw, 0.0)
                sc = _dot_nt(qm, kn) + _alibi_slope(h) * bias
                lse_h = jnp.sum(jnp.where(col == h, lse_t, 0.0), -1, keepdims=True)
                p = jnp.exp(sc - lse_h)
                dd = jnp.sum(jnp.where(half_e, prod, 0.0), -1, keepdims=True)
                dp = _dot_nt(dm, v)
                ds = (p * (dp - dd)).astype(MXU)
                dsink = dsink - jnp.where(col1 == h, jnp.sum(jnp.exp(sink_ref[h] - lse_h) * dd), 0.0)
                dq_m = _dot(ds, kn) * scale
                dk_blk = dk_blk + _dot_tn(ds, qm)
                dv_blk = dv_blk + _dot_tn(p, dm)
                dqs.append(dq_m if e == hk else pltpu.roll(dq_m, 64, 1))
            dx, dg = _norm_bwd(xn, r, gq_ref[...], jnp.where(lo, dqs[0], dqs[1]), half=True)
            dq_ref[:, sl] = dx.astype(dq_ref.dtype)
            dgq = dgq + dg
        dgq_ref[...] += dgq
        dsink_ref[...] += dsink
        prev = pl.ds(pl.multiple_of(jnp.maximum(n - 1, 0) * b, b), b)
        cur = pl.ds(pl.multiple_of(n * b, b), b)
        dk_acc[prev, :] += dk_blk[:b]
        dv_acc[prev, :] += dv_blk[:b]
        dk_acc[cur, :] += dk_blk[b:]
        dv_acc[cur, :] += dv_blk[b:]

        @pl.when(n == nb - 1)
        def _():
            _, kxn, kr = _norm_fwd(kfull_ref[...], gk_ref[...], half=True)
            dx, dg = _norm_bwd(kxn, kr, gk_ref[...], dk_acc[...], half=True)
            dk_ref[...] = dx.astype(dk_ref.dtype)
            dv_ref[...] = dv_acc[...].astype(dv_ref.dtype)
            dgk_ref[...] = dg

    full = pl.BlockSpec((s, LANES), lambda n: (0, 0))
    vec = pl.BlockSpec((1, LANES), lambda n: (0, 0))
    return _pcall(
        body, name="swa_bwd", grid=(nb,),
        out_shape=[_sds((s, 1024), MXU), _sds((s, LANES), MXU), _sds((s, LANES), MXU),
                   _sds((1, LANES), jnp.float32), _sds((1, LANES), jnp.float32),
                   _sds((1, SWA_Q_HEADS), jnp.float32)],
        in_specs=_swa_specs(s) + [pl.BlockSpec((b, 1024), lambda n: (n, 0)), pl.BlockSpec((b, 1024), lambda n: (n, 0)),
                                  pl.BlockSpec((b, SWA_Q_HEADS), lambda n: (n, 0)),
                                  pl.BlockSpec((s, LANES), lambda n: (0, C_KA // LANES))],
        out_specs=[pl.BlockSpec((b, 1024), lambda n: (n, 0)), full, full, vec, vec,
                   pl.BlockSpec((1, SWA_Q_HEADS), lambda n: (0, 0))],
        scratch=[pltpu.VMEM((s, LANES), jnp.float32), pltpu.VMEM((s, LANES), jnp.float32)],
        sem=("arbitrary",), after=after)(proj, proj, proj, proj, proj, posc, posr, posr, gq, gk, sinks, d_y, y_a, lse,
                                         proj)


def _dx(d_proj, w_in, x, g, d_h1, after):
    s, d = x.shape
    n = w_in.shape[0]
    tm = min(ROW_TILE, s)

    def body(dp_ref, w_ref, x_ref, g_ref, dh_ref, dx_ref, dg_ref):
        i = pl.program_id(0)

        @pl.when(i == 0)
        def _():
            dg_ref[...] = jnp.zeros_like(dg_ref)

        d_hn = _dot(dp_ref[...], w_ref[...])
        _, xn, r = _norm_fwd(x_ref[...], g_ref[...])
        dx, dg = _norm_bwd(xn, r, g_ref[...], d_hn)
        dx_ref[...] = dh_ref[...] + dx
        dg_ref[...] += dg

    row = pl.BlockSpec((tm, d), lambda i: (i, 0))
    vec = pl.BlockSpec((1, d), lambda i: (0, 0))
    return _pcall(
        body, name="grad_x", grid=(s // tm,),
        out_shape=[_sds((s, d), jnp.float32), _sds((1, d), jnp.float32)],
        in_specs=[pl.BlockSpec((tm, n), lambda i: (i, 0)), pl.BlockSpec((n, d), lambda i: (0, 0)), row, vec, row],
        out_specs=[row, vec], sem=("arbitrary",), after=after)(d_proj, w_in, x, g, d_h1)


_SMALL = ["attn_norm_g", "swa_q_norm_g", "swa_k_norm_g", "swa_sinks", "mla_cq_norm_g", "mla_ckv_norm_g",
          "mla_qn_norm_g", "mla_qr_norm_g", "mla_kn_norm_g", "mla_kr_norm_g", "mem_norm_g",
          "mem_q_norm_g", "mem_k_norm_g", "ffn_norm_g"]


def kernel(x, mem, positions, attn_norm_g, w_in, swa_q_norm_g, swa_k_norm_g, swa_sinks, mla_cq_norm_g, mla_ckv_norm_g, w_uq, w_ukv, mla_qn_norm_g, mla_qr_norm_g, mla_kn_norm_g, mla_kr_norm_g, mem_norm_g, w_mem_kv, mem_q_norm_g, mem_k_norm_g, w_out, ffn_norm_g, w_gate, w_up, w_down, loss_target, m_attn_norm_g, m_w_in, m_swa_q_norm_g, m_swa_k_norm_g, m_swa_sinks, m_mla_cq_norm_g, m_mla_ckv_norm_g, m_w_uq, m_w_ukv, m_mla_qn_norm_g, m_mla_qr_norm_g, m_mla_kn_norm_g, m_mla_kr_norm_g, m_mem_norm_g, m_w_mem_kv, m_mem_q_norm_g, m_mem_k_norm_g, m_w_out, m_ffn_norm_g, m_w_gate, m_w_up, m_w_down, v_attn_norm_g, v_w_in, v_swa_q_norm_g, v_swa_k_norm_g, v_swa_sinks, v_mla_cq_norm_g, v_mla_ckv_norm_g, v_w_uq, v_w_ukv, v_mla_qn_norm_g, v_mla_qr_norm_g, v_mla_kn_norm_g, v_mla_kr_norm_g, v_mem_norm_g, v_w_mem_kv, v_mem_q_norm_g, v_mem_k_norm_g, v_w_out, v_ffn_norm_g, v_w_gate, v_w_up, v_w_down):
    args = dict(locals())
    x2, mem2, tgt = x[0], mem[0], loss_target[0]
    s, d = x2.shape
    n_in = w_in.shape[2]
    f = w_gate.shape[2]

    (g_in,) = _all_gather([w_in[0].T.astype(WIRE)])
    mix_shards = [w_uq[0].T.astype(WIRE), w_ukv[0].astype(WIRE), w_mem_kv[0].astype(WIRE),
                  _to_wire([w_out[0]], g_in, "wire_out")[0]]
    g_uq, wkv, g_mkv, g_out = _all_gather_background(mix_shards, 5, "all_gather_mix_weights")
    ffn_shards = [_to_wire([w_gate[0].T, w_up[0].T], g_in, "wire_gate_up"),
                  _to_wire([w_down[0]], g_in, "wire_down")[0]]
    w_gu, w_d = _all_gather_background(ffn_shards, 1, "all_gather_ffn_weights")
    wi = g_in.reshape(N_DEV * n_in, d)
    wi = jnp.concatenate([wi[0:1024], wi[1280:1792], wi[1792:2304], wi[2368:2880],
                          wi[1024:1152], wi[1152:1280], wi[2304:2368],
                          jnp.zeros((IN_PAD - 2880, d), wi.dtype)], axis=0)
    wq = g_uq.reshape(768, 512)
    wq = jnp.concatenate([wq[192 * h: 192 * h + 128] for h in range(4)]
                         + [wq[192 * h + 128: 192 * (h + 1)] for h in range(4)], axis=0)
    wmkv = g_mkv.reshape(-1, g_mkv.shape[-1])
    wo = g_out.reshape(-1, d)

    pos = positions[0].astype(jnp.float32)
    inv_freq = ROPE_THETA ** (-jnp.arange(0, MLA_ROPE, 2, dtype=jnp.float32) / MLA_ROPE)
    ang = pos[:, None] * inv_freq
    cos32, sin32 = jnp.cos(ang), jnp.sin(ang)
    cos_t = jnp.tile(cos32, (1, 4))
    sin_t = jnp.tile(jnp.concatenate([-sin32, sin32], axis=1), (1, 2))
    posc, posr = pos.reshape(s, 1), pos.reshape(1, s)
    two = lambda g: jnp.tile(g, (1, 2))
    gq2, gk2, gqr2, gkr2 = two(swa_q_norm_g), two(swa_k_norm_g), two(mla_qr_norm_g), two(mla_kr_norm_g)
    sinks1 = swa_sinks[0]

    proj, hn = _in_proj(x2, attn_norm_g, wi)
    qc, kc, vb, qb, kvb, cqn, ckvn = _mla_prep(proj, cos_t, sin_t, mla_cq_norm_g, mla_ckv_norm_g, wq, wkv,
                                                mla_qn_norm_g, gqr2, mla_kn_norm_g, gkr2)
    y_b, lse_b = _mla_fwd(qc, kc, vb)
    km, vmm, kvm, memn = _memkv_prep(mem2, mem_norm_g, wmkv, mem_k_norm_g)
    y_m, lse_m = _mem_fwd(proj, mem_q_norm_g, km, vmm)
    y_a, lse_a = _swa_fwd(proj, posc, posr, gq2, gk2, sinks1)
    h1, fn = _out_proj(y_a, y_b, y_m, x2, wo, ffn_norm_g)
    gu, act = _ffn_gu(fn, w_gu)
    dout, dout_b, loss_tile = _ffn_down(act, w_d, h1, tgt)

    dgu, dw_d = _ffn_bwd_act(dout_b, w_d, gu)
    dw_gu = _ffn_dw_gu(fn, dgu)
    r_gu, r_d = _exchange_grads_background([dw_gu, dw_d], 2, "exchange_ffn_grads")
    d_h1, dg_ffn = _ffn_norm_bwd(_ffn_dfn(dgu, w_gu, dw_gu), dout, h1, ffn_norm_g)
    d_y = _mm(d_h1, wo, tb=True, out_dtype=jnp.float32, tm=FFN_TILE, tk=2048, name="d_mix")
    dw_out = jnp.concatenate([
        _mm(y_a, d_h1, ta=True, out_dtype=WIRE, tm=1024, tk=1024, name="dw_out_a"),
        _mm(y_b, d_h1, ta=True, out_dtype=WIRE, tm=1024, tk=1024, name="dw_out_b"),
        _mm(y_m, d_h1, ta=True, out_dtype=WIRE, tm=1024, tk=1024, name="dw_out_m")], axis=0)
    d_qm, dkm, dvmm, dg_mq = _mem_bwd(proj, mem_q_norm_g, km, vmm, d_y, y_m, lse_m)
    dw_mkv, dg_mem, dg_mk = _memkv_bwd(mem2, mem_norm_g, wmkv, mem_k_norm_g, kvm, memn, dkm, dvmm)
    r_mkv, r_out = _exchange_grads_background([dw_mkv.reshape(g_mkv.shape), dw_out.reshape(g_out.shape)], 3,
                                              "exchange_mix_grads")
    dqc, dkc, dvb = _mla_bwd(qc, kc, vb, d_y, y_b, lse_b, dw_mkv)
    (d_cq, d_ckv, d_kr, dw_uq, dw_ukv, dg_cq, dg_ckv, dg_qn, dg_qr, dg_kn, dg_kr) = _mla_prep_bwd(
        proj, cos_t, sin_t, mla_cq_norm_g, mla_ckv_norm_g, wq, wkv, mla_qn_norm_g, gqr2, mla_kn_norm_g, gkr2,
        qb, kvb, cqn, ckvn, dqc, dkc, dvb)
    d_qa, d_ka, d_va, dg_q, dg_k, d_sinks = _swa_bwd(proj, posc, posr, gq2, gk2, sinks1, d_y, y_a, lse_a, dw_out)
    d_proj = jnp.concatenate([d_qa, d_cq, d_ckv, d_qm, d_ka, d_va, d_kr], axis=1)
    gi = _dw_in(hn, d_proj, n_in)

    gq_ = jnp.concatenate(sum([[dw_uq[128 * h: 128 * (h + 1)], dw_uq[512 + 64 * h: 512 + 64 * (h + 1)]]
                               for h in range(4)], []), axis=0)
    gq_ = gq_.reshape(N_DEV, 96, 512)
    r_in, r_uq, r_ukv = _exchange_grads_background([gi, gq_, dw_ukv], 4, "exchange_in_grads")
    grad_x, dg_attn = _dx(d_proj, wi, x2, attn_norm_g, d_h1, gi)

    big = {}
    def adam(name, r, transposed=False, after=None, which=None):
        w, m, v = args[name][0], args["m_" + name][0], args["v_" + name][0]
        if transposed:
            outs = _adam_big(r, w.T, m.T, v.T, "adam_" + name, after, which)
            return [o.T[None] for o in outs]
        return [o[None] for o in _adam_big(r, w, m, v, "adam_" + name, after)]
    big["w_gate"] = adam("w_gate", r_gu, True, which=0)
    big["w_up"] = adam("w_up", r_gu, True, after=big["w_gate"][0], which=1)
    big["w_down"] = adam("w_down", r_d, after=big["w_up"][0])
    big["w_out"] = adam("w_out", r_out, after=big["w_down"][0])
    big["w_mem_kv"] = adam("w_mem_kv", r_mkv, after=big["w_out"][0])
    big["w_in"] = adam("w_in", r_in, True, after=big["w_mem_kv"][0])
    big["w_uq"] = adam("w_uq", r_uq, True, after=big["w_in"][0])
    big["w_ukv"] = adam("w_ukv", r_ukv, after=big["w_uq"][0])

    small_g = {
        "attn_norm_g": dg_attn, "swa_q_norm_g": dg_q, "swa_k_norm_g": dg_k,
        "swa_sinks": d_sinks, "mla_cq_norm_g": dg_cq, "mla_ckv_norm_g": dg_ckv, "mla_qn_norm_g": dg_qn,
        "mla_qr_norm_g": dg_qr, "mla_kn_norm_g": dg_kn, "mla_kr_norm_g": dg_kr,
        "mem_norm_g": dg_mem, "mem_q_norm_g": dg_mq, "mem_k_norm_g": dg_mk, "ffn_norm_g": dg_ffn}
    loss11, small_out = _small_allreduce_adam(
        [small_g[n] for n in _SMALL], loss_tile, [args[n] for n in _SMALL],
        [args["m_" + n] for n in _SMALL], [args["v_" + n] for n in _SMALL])
    small = dict(zip(_SMALL, small_out))
    loss = loss11.reshape(())

    order = ["attn_norm_g", "w_in", "swa_q_norm_g", "swa_k_norm_g", "swa_sinks", "mla_cq_norm_g", "mla_ckv_norm_g",
             "w_uq", "w_ukv", "mla_qn_norm_g", "mla_qr_norm_g", "mla_kn_norm_g", "mla_kr_norm_g", "mem_norm_g",
             "w_mem_kv", "mem_q_norm_g", "mem_k_norm_g", "w_out", "ffn_norm_g", "w_gate", "w_up", "w_down"]
    res = {n: (big[n] if n in big else list(small[n])) for n in order}
    outs = [loss, grad_x[None]]
    for kind in range(4):
        outs += [res[n][kind] for n in order]
    return tuple(outs)
```

```python
import jax
import jax.numpy as jnp
from jax import lax
from jax.experimental import pallas as pl
from jax.experimental.pallas import tpu as pltpu
from jax.experimental.pallas import tpu_sc as plsc

MXU = jnp.bfloat16
WIRE = jnp.bfloat16
EPS = 1e-6
NEG_INF = -1e30
LOG2E = 1.4426950408889634
N_DEV = 8
LANES = 128
ROW_TILE = 256
FFN_TILE = 512
ATT_TILE = 1024
SWA_BLOCK = 128
VMEM_LIMIT = 56 * 1024 * 1024

SWA_Q_HEADS, SWA_KV_HEADS, SWA_DIM = 16, 2, 64
MLA_HEADS, MLA_NOPE, MLA_ROPE, MLA_V = 4, 128, 64, 128
MEM_HEADS, MEM_DIM = 4, 128
ROPE_THETA = 10000.0
ADAM_LR, ADAM_B1, ADAM_B2, ADAM_EPS, ADAM_WD, ADAM_STEP = 0.001, 0.9, 0.999, 1e-08, 0.01, 10

C_QA, C_CQ, C_CKV, C_QM, C_KA, C_VA, C_KR, IN_PAD = 0, 1024, 1536, 2048, 2560, 2688, 2816, 2944


def _pcall(body, *, name, out_shape, in_specs, out_specs, grid=(), scratch=(), sem=None, after=None):
    params = pltpu.CompilerParams(dimension_semantics=sem, vmem_limit_bytes=VMEM_LIMIT)
    if after is not None:
        n_in, inner = len(in_specs), body

        def body(*refs):
            inner(*refs[:n_in], *refs[n_in + 1:])

        in_specs = list(in_specs) + [pl.BlockSpec(memory_space=pl.ANY)]
    call = pl.pallas_call(body, name=name, grid=grid, in_specs=in_specs, out_specs=out_specs,
                          out_shape=out_shape, scratch_shapes=list(scratch), compiler_params=params)
    return call if after is None else (lambda *ops: call(*ops, after))


def _sds(shape, dtype):
    return jax.ShapeDtypeStruct(tuple(shape), dtype)


def _dot(a, b):
    return jnp.dot(a.astype(MXU), b.astype(MXU), preferred_element_type=jnp.float32)


def _dot_nt(a, b):
    return lax.dot_general(a.astype(MXU), b.astype(MXU), (((1,), (1,)), ((), ())),
                           preferred_element_type=jnp.float32)


def _dot_tn(a, b):
    return lax.dot_general(a.astype(MXU), b.astype(MXU), (((0,), (0,)), ((), ())),
                           preferred_element_type=jnp.float32)


def _lo_mask(shape):
    return (lax.broadcasted_iota(jnp.int32, shape, len(shape) - 1) % LANES) < 64


def _norm_fwd(x, g, half=False):
    x2 = x * x
    if half:
        lo = _lo_mask(x.shape)
        s_lo = jnp.sum(jnp.where(lo, x2, 0.0), -1, keepdims=True)
        s_hi = jnp.sum(jnp.where(lo, 0.0, x2), -1, keepdims=True)
        r = jnp.where(lo, lax.rsqrt(s_lo / 64.0 + EPS), lax.rsqrt(s_hi / 64.0 + EPS))
    else:
        r = lax.rsqrt(jnp.mean(x2, -1, keepdims=True) + EPS)
    xn = x * r
    return xn * g, xn, r


def _norm_bwd(xn, r, g, dy, half=False):
    t = dy * g
    tx = t * xn
    if half:
        lo = _lo_mask(xn.shape)
        m_lo = jnp.sum(jnp.where(lo, tx, 0.0), -1, keepdims=True) / 64.0
        m_hi = jnp.sum(jnp.where(lo, 0.0, tx), -1, keepdims=True) / 64.0
        m = jnp.where(lo, m_lo, m_hi)
    else:
        m = jnp.mean(tx, -1, keepdims=True)
    dx = r * (t - xn * m)
    dg = jnp.sum(dy * xn, 0, keepdims=True)
    return dx, dg


def _swap32(x):
    lane = lax.broadcasted_iota(jnp.int32, x.shape, 1)
    return jnp.where((lane % 64) < 32, pltpu.roll(x, 96, 1), pltpu.roll(x, 32, 1))


def _rope(x, cos, sin):
    return x * cos + _swap32(x) * sin


def _rope_bwd(d, cos, sin):
    return d * cos + _swap32(d * sin)


def _my_coords():
    return lax.axis_index("x"), lax.axis_index("y"), lax.axis_index("c")


def _dev_index(px, py, pc):
    return 4 * px + 2 * py + pc


_FLIPS = [(0, 0, 1), (0, 1, 0), (0, 1, 1), (1, 0, 0), (1, 0, 1), (1, 1, 0), (1, 1, 1)]


def _flip(coords, f):
    return tuple((1 - v) if b else v for v, b in zip(coords, f))


def _all_gather(shards):
    n = len(shards)

    def body(*refs):
        ins, outs = refs[:n], refs[n:2 * n]
        send_sems, recv_sems, local_sems = refs[2 * n:]
        x, y, c = _my_coords()
        me, sibling = (x, y, c), (x, y, 1 - c)
        chips = [(1 - x, y), (x, 1 - y), (1 - x, 1 - y)]

        def copy(w, k, block, to, src=None):
            dst = outs[w].at[_dev_index(*block)]
            return pltpu.make_async_remote_copy(
                src_ref=dst if src is None else src, dst_ref=dst,
                send_sem=send_sems.at[w, k], recv_sem=recv_sems.at[w, k],
                device_id=to, device_id_type=pl.DeviceIdType.MESH)

        sends, locals_ = [], []
        for w in range(n):
            mine = pltpu.make_async_copy(ins[w], outs[w].at[_dev_index(*me)], local_sems.at[w])
            mine.start()
            locals_.append(mine)
            first = [copy(w, 0, me, sibling, src=ins[w])]
            first += [copy(w, 1 + j, me, (*chip, c), src=ins[w]) for j, chip in enumerate(chips)]
            for cp in first:
                cp.start()
            sends += first
        for w in range(n):
            for j, chip in enumerate(chips):
                copy(w, 1 + j, (*chip, c), me).wait_recv()
                fwd = copy(w, 4 + j, (*chip, c), sibling)
                fwd.start()
                sends.append(fwd)
        for w in range(n):
            copy(w, 0, sibling, me).wait_recv()
            for j, chip in enumerate(chips):
                copy(w, 4 + j, (*chip, 1 - c), me).wait_recv()
        for cp in sends:
            cp.wait_send()
        for mine in locals_:
            mine.wait()

    any_spec = pl.BlockSpec(memory_space=pl.ANY)
    return _pcall(
        body, name="all_gather_weights",
        out_shape=[_sds((N_DEV,) + s.shape, s.dtype) for s in shards],
        in_specs=[any_spec] * n, out_specs=[any_spec] * n,
        scratch=[pltpu.SemaphoreType.DMA((n, 7)), pltpu.SemaphoreType.DMA((n, 7)),
                 pltpu.SemaphoreType.DMA((n,))])(*shards)


def _wire_cost(arrays):
    nbytes = sum(a.size * a.dtype.itemsize for a in arrays)
    return pl.CostEstimate(flops=0, transcendentals=0, bytes_accessed=40 * nbytes)


def _all_gather_background(shards, collective_id, name):
    n = len(shards)
    src_refs = [jax.new_ref(s, memory_space=pltpu.MemorySpace.HBM) for s in shards]
    out_refs = [jax.empty_ref(_sds((N_DEV,) + s.shape, s.dtype), memory_space=pltpu.MemorySpace.HBM) for s in shards]

    @pl.kernel(mesh=plsc.ScalarSubcoreMesh(axis_name="seq", num_cores=1), name=name,
               scratch_types=(pltpu.SemaphoreType.DMA((n, 7)), pltpu.SemaphoreType.DMA((n, 7)),
                              pltpu.SemaphoreType.DMA((n,))),
               compiler_params=pltpu.CompilerParams(collective_id=collective_id))
    def launch(send_sems, recv_sems, local_sems):
        x, y, c = _my_coords()
        me, sibling = (x, y, c), (x, y, 1 - c)
        chips = [(1 - x, y), (x, 1 - y), (1 - x, 1 - y)]
        barrier = pltpu.get_barrier_semaphore()
        for peer in [sibling] + [(*chip, c) for chip in chips]:
            pl.semaphore_signal(barrier, inc=1, device_id=peer, device_id_type=pl.DeviceIdType.MESH)
        pl.semaphore_wait(barrier, 4)

        def copy(w, k, block, to, src=None):
            dst = out_refs[w].at[_dev_index(*block)]
            return pltpu.make_async_remote_copy(
                src_ref=dst if src is None else src, dst_ref=dst,
                send_sem=send_sems.at[w, k], recv_sem=recv_sems.at[w, k],
                device_id=to, device_id_type=pl.DeviceIdType.MESH)

        sends, locals_ = [], []
        for w in range(n):
            mine = pltpu.make_async_copy(src_refs[w], out_refs[w].at[_dev_index(*me)], local_sems.at[w])
            mine.start()
            locals_.append(mine)
            first = [copy(w, 0, me, sibling, src=src_refs[w])]
            first += [copy(w, 1 + j, me, (*chip, c), src=src_refs[w]) for j, chip in enumerate(chips)]
            for cp in first:
                cp.start()
            sends += first
        for w in range(n):
            for j, chip in enumerate(chips):
                copy(w, 1 + j, (*chip, c), me).wait_recv()
                fwd = copy(w, 4 + j, (*chip, c), sibling)
                fwd.start()
                sends.append(fwd)
        for w in range(n):
            copy(w, 0, sibling, me).wait_recv()
            for j, chip in enumerate(chips):
                copy(w, 4 + j, (*chip, 1 - c), me).wait_recv()
        for cp in sends:
            cp.wait_send()
        for mine in locals_:
            mine.wait()

    launch()
    return [r[...] for r in out_refs]


def _exchange_grads(grads):
    n = len(grads)

    def body(*refs):
        ins, outs = refs[:n], refs[n:2 * n]
        send_sems, recv_sems, local_sems = refs[2 * n:]
        me = _my_coords()
        my_idx = _dev_index(*me)
        sends, locals_ = [], []
        for w in range(n):
            mine = pltpu.make_async_copy(ins[w].at[my_idx], outs[w].at[my_idx], local_sems.at[w])
            mine.start()
            locals_.append(mine)
            for k, f in enumerate(_FLIPS):
                peer = _flip(me, f)
                cp = pltpu.make_async_remote_copy(
                    src_ref=ins[w].at[_dev_index(*peer)], dst_ref=outs[w].at[my_idx],
                    send_sem=send_sems.at[w, k], recv_sem=recv_sems.at[w, k],
                    device_id=peer, device_id_type=pl.DeviceIdType.MESH)
                cp.start()
                sends.append(cp)
        for w in range(n):
            for k, f in enumerate(_FLIPS):
                peer = _flip(me, f)
                slot = outs[w].at[_dev_index(*peer)]
                pltpu.make_async_remote_copy(
                    src_ref=slot, dst_ref=slot,
                    send_sem=send_sems.at[w, k], recv_sem=recv_sems.at[w, k],
                    device_id=peer, device_id_type=pl.DeviceIdType.MESH).wait_recv()
        for cp in sends:
            cp.wait_send()
        for mine in locals_:
            mine.wait()

    any_spec = pl.BlockSpec(memory_space=pl.ANY)
    return _pcall(
        body, name="exchange_grads",
        out_shape=[_sds(g.shape, g.dtype) for g in grads],
        in_specs=[any_spec] * n, out_specs=[any_spec] * n,
        scratch=[pltpu.SemaphoreType.DMA((n, 7)), pltpu.SemaphoreType.DMA((n, 7)),
                 pltpu.SemaphoreType.DMA((n,))])(*grads)


def _exchange_grads_background(grads, collective_id, name):
    n = len(grads)
    src_refs = [jax.new_ref(g, memory_space=pltpu.MemorySpace.HBM) for g in grads]
    out_refs = [jax.empty_ref(_sds(g.shape, g.dtype), memory_space=pltpu.MemorySpace.HBM) for g in grads]

    @pl.kernel(mesh=plsc.ScalarSubcoreMesh(axis_name="seq", num_cores=1), name=name,
               scratch_types=(pltpu.SemaphoreType.DMA((n, 7)), pltpu.SemaphoreType.DMA((n, 7)),
                              pltpu.SemaphoreType.DMA((n,))),
               cost_estimate=_wire_cost(grads),
               compiler_params=pltpu.CompilerParams(collective_id=collective_id))
    def launch(send_sems, recv_sems, local_sems):
        me = _my_coords()
        my_idx = _dev_index(*me)
        peers = [_flip(me, f) for f in _FLIPS]
        barrier = pltpu.get_barrier_semaphore()
        for peer in peers:
            pl.semaphore_signal(barrier, inc=1, device_id=peer, device_id_type=pl.DeviceIdType.MESH)
        pl.semaphore_wait(barrier, len(peers))
        sends, locals_ = [], []
        for w in range(n):
            mine = pltpu.make_async_copy(src_refs[w].at[my_idx], out_refs[w].at[my_idx], local_sems.at[w])
            mine.start()
            locals_.append(mine)
            for k, peer in enumerate(peers):
                cp = pltpu.make_async_remote_copy(
                    src_ref=src_refs[w].at[_dev_index(*peer)], dst_ref=out_refs[w].at[my_idx],
                    send_sem=send_sems.at[w, k], recv_sem=recv_sems.at[w, k],
                    device_id=peer, device_id_type=pl.DeviceIdType.MESH)
                cp.start()
                sends.append(cp)
        for w in range(n):
            for k, peer in enumerate(peers):
                slot = out_refs[w].at[_dev_index(*peer)]
                pltpu.make_async_remote_copy(
                    src_ref=slot, dst_ref=slot, send_sem=send_sems.at[w, k], recv_sem=recv_sems.at[w, k],
                    device_id=peer, device_id_type=pl.DeviceIdType.MESH).wait_recv()
        for cp in sends:
            cp.wait_send()
        for mine in locals_:
            mine.wait()

    launch()
    return [r[...] for r in out_refs]


def _to_wire(parts, after, name):
    n = len(parts)
    rows, cols = parts[0].shape
    tr = rows // 2 if rows % 32 == 0 else rows

    def body(*refs):
        for k in range(n):
            refs[n][k] = refs[k][...].astype(WIRE)

    blk = pl.BlockSpec((tr, cols), lambda i: (i, 0))
    return _pcall(
        body, name=name, grid=(rows // tr,), out_shape=_sds((n, rows, cols), WIRE),
        in_specs=[blk] * n, out_specs=pl.BlockSpec((n, tr, cols), lambda i: (0, i, 0)),
        sem=("parallel",), after=after)(*parts)


def _adam_math(w, g, m, v):
    m = ADAM_B1 * m + (1.0 - ADAM_B1) * g
    v = ADAM_B2 * v + (1.0 - ADAM_B2) * (g * g)
    m_hat = m / (1.0 - ADAM_B1 ** ADAM_STEP)
    v_hat = v / (1.0 - ADAM_B2 ** ADAM_STEP)
    delta = -ADAM_LR * (m_hat / (jnp.sqrt(v_hat) + ADAM_EPS) + ADAM_WD * w)
    return delta, m, v


def _small_allreduce_adam(grads, loss_tile, ws, ms, vs):
    sizes = [w.shape[-1] for w in ws]
    n_par = len(ws)
    row0, r = [], 0
    for n in sizes:
        row0.append(r)
        r += -(-n // LANES)
    loss_row = r
    rows = -(-(r + 1) // 8) * 8

    def pieces(n):
        return [(k, min(LANES, n - LANES * k)) for k in range(-(-n // LANES))]

    def body(*refs):
        g_refs = refs[:n_par]
        loss_in = refs[n_par]
        w_refs = refs[n_par + 1: 2 * n_par + 1]
        m_refs = refs[2 * n_par + 1: 3 * n_par + 1]
        v_refs = refs[3 * n_par + 1: 4 * n_par + 1]
        loss_out = refs[4 * n_par + 1]
        out_refs = refs[4 * n_par + 2: 8 * n_par + 2]
        pack, gath, res, send_sems, recv_sems = refs[8 * n_par + 2:]
        me = _my_coords()
        my_idx = _dev_index(*me)

        def fill(slot, srcs):
            pack[slot] = jnp.zeros((rows, LANES), jnp.float32)
            for p, n in enumerate(sizes):
                val = srcs[p][...]
                if val.shape[-1] == LANES and n == 64:
                    pack[slot, row0[p]:row0[p] + 1, :] = val + pltpu.roll(val, 64, 1)
                    continue
                for k, width in pieces(n):
                    pack[slot, row0[p] + k:row0[p] + k + 1, 0:width] = srcs[p][:, LANES * k:LANES * k + width]

        fill(0, g_refs)
        pack[0, loss_row:loss_row + 1, :] = loss_in[0:1, :]
        gath[my_idx] = pack[0]
        sends = []
        for k, f in enumerate(_FLIPS):
            peer = _flip(me, f)
            cp = pltpu.make_async_remote_copy(
                src_ref=pack.at[0], dst_ref=gath.at[my_idx],
                send_sem=send_sems.at[k], recv_sem=recv_sems.at[k],
                device_id=peer, device_id_type=pl.DeviceIdType.MESH)
            cp.start()
            sends.append(cp)
        fill(1, w_refs)
        fill(2, m_refs)
        fill(3, v_refs)
        for k, f in enumerate(_FLIPS):
            peer = _flip(me, f)
            slot = gath.at[_dev_index(*peer)]
            pltpu.make_async_remote_copy(
                src_ref=slot, dst_ref=slot, send_sem=send_sems.at[k], recv_sem=recv_sems.at[k],
                device_id=peer, device_id_type=pl.DeviceIdType.MESH).wait_recv()
        for cp in sends:
            cp.wait_send()
        g = gath[0]
        for d in range(1, N_DEV):
            g = g + gath[d]
        delta, m, v = _adam_math(pack[1], g, pack[2], pack[3])
        res[0], res[1], res[2], res[3] = g, delta, m, v
        loss_out[...] = res[0, loss_row:loss_row + 1, 0:1]
        for p, n in enumerate(sizes):
            for kind in range(4):
                for k, width in pieces(n):
                    out_refs[4 * p + kind][:, LANES * k:LANES * k + width] = (
                        res[kind, row0[p] + k:row0[p] + k + 1, 0:width])

    vm = pl.BlockSpec(memory_space=pltpu.VMEM)
    out_shape = [_sds((1, 1), jnp.float32)]
    for n in sizes:
        out_shape += [_sds((1, n), jnp.float32)] * 4
    outs = _pcall(
        body, name="small_allreduce_adam", out_shape=out_shape,
        in_specs=[vm] * (4 * n_par + 1), out_specs=[vm] * len(out_shape),
        scratch=[pltpu.VMEM((4, rows, LANES), jnp.float32), pltpu.VMEM((N_DEV, rows, LANES), jnp.float32),
                 pltpu.VMEM((4, rows, LANES), jnp.float32),
                 pltpu.SemaphoreType.DMA((7,)), pltpu.SemaphoreType.DMA((7,))])(*grads, loss_tile, *ws, *ms, *vs)
    return outs[0], [outs[1 + 4 * p: 5 + 4 * p] for p in range(n_par)]


def _adam_big(recv, w, m, v, name, after=None, which=None):
    rows, cols = recv.shape[-2:]
    row_tiles = [t for t in range(16, rows + 1, 16) if rows % t == 0 and t * cols <= 400 * 1024]
    tr, tc = (max(row_tiles), cols) if row_tiles else (rows, 512 if cols % 512 == 0 else cols)

    def body(r_ref, w_ref, m_ref, v_ref, g_ref, d_ref, mo_ref, vo_ref):
        g = r_ref[0].astype(jnp.float32)
        for d in range(1, N_DEV):
            g = g + r_ref[d].astype(jnp.float32)
        delta, mn, vn = _adam_math(w_ref[...], g, m_ref[...], v_ref[...])
        g_ref[...] = g
        d_ref[...] = delta
        mo_ref[...] = mn
        vo_ref[...] = vn

    blk = pl.BlockSpec((tr, tc), lambda i, j: (i, j))
    if which is None:
        r_spec = pl.BlockSpec((N_DEV, tr, tc), lambda i, j: (0, i, j))
    else:
        r_spec = pl.BlockSpec((N_DEV, None, tr, tc), lambda i, j: (0, which, i, j))
    return _pcall(
        body, name=name, grid=(rows // tr, cols // tc),
        out_shape=[_sds((rows, cols), jnp.float32)] * 4,
        in_specs=[r_spec, blk, blk, blk],
        out_specs=[blk] * 4, sem=("parallel", "parallel"), after=after)(recv, w, m, v)


def _mm(a, b, *, ta=False, tb=False, out_dtype, tm, tk, name):
    (kdim, mdim) = a.shape if ta else a.shape[::-1]
    ndim = b.shape[0] if tb else b.shape[1]
    tm, tk = min(tm, mdim), min(tk, kdim)
    nk = kdim // tk

    def body(a_ref, b_ref, o_ref, acc):
        k = pl.program_id(1)
        if ta:
            part = _dot_tn(a_ref[...], b_ref[...])
        elif tb:
            part = _dot_nt(a_ref[...], b_ref[...])
        else:
            part = _dot(a_ref[...], b_ref[...])

        @pl.when(k == 0)
        def _():
            acc[...] = part

        @pl.when(k > 0)
        def _():
            acc[...] += part

        @pl.when(k == nk - 1)
        def _():
            o_ref[...] = acc[...].astype(o_ref.dtype)

    a_spec = pl.BlockSpec((tk, tm), lambda i, k: (k, i)) if ta else pl.BlockSpec((tm, tk), lambda i, k: (i, k))
    b_spec = pl.BlockSpec((ndim, tk), lambda i, k: (0, k)) if tb else pl.BlockSpec((tk, ndim), lambda i, k: (k, 0))
    return _pcall(
        body, name=name, grid=(mdim // tm, nk), out_shape=_sds((mdim, ndim), out_dtype),
        in_specs=[a_spec, b_spec], out_specs=pl.BlockSpec((tm, ndim), lambda i, k: (i, 0)),
        scratch=[pltpu.VMEM((tm, ndim), jnp.float32)], sem=("parallel", "arbitrary"))(a, b)


def _ref_col_pieces(start, stop):
    ref_starts = [0, 1024, 1152, 1280, 1792, 2304, 2368, 2880]
    perm_starts = [C_QA, C_KA, C_VA, C_CQ, C_CKV, C_KR, C_QM]
    out = []
    for p in range(7):
        lo, hi = max(start, ref_starts[p]), min(stop, ref_starts[p + 1])
        if lo < hi:
            out.append((lo - start, perm_starts[p] + lo - ref_starts[p], hi - lo))
    return out


def _dw_in(hn, d_proj, n_shard):
    s, d = hn.shape
    n = d_proj.shape[1]
    tm, tk = min(512, d), min(1024, s)
    nk = s // tk

    def body(a_ref, b_ref, o_ref, acc):
        k = pl.program_id(1)
        part = _dot_tn(a_ref[...], b_ref[...])

        @pl.when(k == 0)
        def _():
            acc[...] = part

        @pl.when(k > 0)
        def _():
            acc[...] += part

        @pl.when(k == nk - 1)
        def _():
            t = acc[...].T
            for j in range(N_DEV):
                rows = [t[src:src + width] for _, src, width in _ref_col_pieces(j * n_shard, (j + 1) * n_shard)]
                o_ref[j] = jnp.concatenate(rows, axis=0).astype(o_ref.dtype)

    return _pcall(
        body, name="dw_in", grid=(d // tm, nk), out_shape=_sds((N_DEV, n_shard, d), WIRE),
        in_specs=[pl.BlockSpec((tk, tm), lambda i, k: (k, i)), pl.BlockSpec((tk, n), lambda i, k: (k, 0))],
        out_specs=pl.BlockSpec((N_DEV, n_shard, tm), lambda i, k: (0, 0, i)),
        scratch=[pltpu.VMEM((tm, n), jnp.float32)], sem=("parallel", "arbitrary"))(hn, d_proj)


def _in_proj(x, g, w):
    s, d = x.shape
    n = w.shape[0]
    tm = min(ROW_TILE, s)

    def body(x_ref, g_ref, w_ref, p_ref, hn_ref):
        hn, _, _ = _norm_fwd(x_ref[...], g_ref[...])
        hn_ref[...] = hn.astype(hn_ref.dtype)
        p_ref[...] = _dot_nt(hn, w_ref[...])

    return _pcall(
        body, name="in_proj", grid=(s // tm,),
        out_shape=[_sds((s, n), jnp.float32), _sds((s, d), MXU)],
        in_specs=[pl.BlockSpec((tm, d), lambda i: (i, 0)), pl.BlockSpec((1, d), lambda i: (0, 0)),
                  pl.BlockSpec((n, d), lambda i: (0, 0))],
        out_specs=[pl.BlockSpec((tm, n), lambda i: (i, 0)), pl.BlockSpec((tm, d), lambda i: (i, 0))],
        sem=("parallel",))(x, g, w)


def _mla_prep(proj, cos, sin, g_cq, g_ckv, w_uq, w_ukv, g_qn, g_qr, g_kn, g_kr):
    s = proj.shape[0]
    tm = min(ROW_TILE, s)
    nh = MLA_HEADS

    def body(cq_ref, ckv_ref, kr_ref, cos_ref, sin_ref, gcq_ref, gckv_ref, wuq_ref, wukv_ref,
             gqn_ref, gqr_ref, gkn_ref, gkr_ref,
             qc_ref, kc_ref, v_ref, qb_ref, kvb_ref, cqn_ref, ckvn_ref):
        cos_t, sin_t = cos_ref[...], sin_ref[...]
        lo = _lo_mask((tm, LANES))
        cqn, _, _ = _norm_fwd(cq_ref[...], gcq_ref[...])
        cqn_ref[...] = cqn.astype(cqn_ref.dtype)
        qb = _dot_nt(cqn, wuq_ref[...])
        qb_ref[...] = qb
        ckvn, _, _ = _norm_fwd(ckv_ref[...], gckv_ref[...])
        ckvn_ref[...] = ckvn.astype(ckvn_ref.dtype)
        kvb = jnp.concatenate([_dot(ckvn, wukv_ref[dev]) for dev in range(N_DEV)], axis=1)
        kvb_ref[...] = kvb
        kr, _, _ = _norm_fwd(kr_ref[...], gkr_ref[...], half=True)
        kr = _rope(kr, cos_t, sin_t)
        kr2 = jnp.where(lo, kr, pltpu.roll(kr, 64, 1))
        ropes = []
        for j in range(nh // 2):
            xr = qb[:, nh * MLA_NOPE + LANES * j: nh * MLA_NOPE + LANES * (j + 1)]
            qr, _, _ = _norm_fwd(xr, gqr_ref[...], half=True)
            ropes.append(_rope(qr, cos_t, sin_t))
        for h in range(nh):
            qn, _, _ = _norm_fwd(qb[:, MLA_NOPE * h: MLA_NOPE * (h + 1)], gqn_ref[...])
            mask = lo if h % 2 == 0 else jnp.logical_not(lo)
            qr = jnp.where(mask, ropes[h // 2], 0.0)
            qc_ref[h] = jnp.concatenate([qn, qr], axis=1).astype(qc_ref.dtype)
            kn, _, _ = _norm_fwd(kvb[:, 256 * h: 256 * h + MLA_NOPE], gkn_ref[...])
            kc_ref[h] = jnp.concatenate([kn, kr2], axis=1).astype(kc_ref.dtype)
            v_ref[h] = kvb[:, 256 * h + MLA_NOPE: 256 * (h + 1)].astype(v_ref.dtype)

    def col(width, start):
        return pl.BlockSpec((tm, width), lambda i: (i, start // width))

    def full(shape):
        return pl.BlockSpec(shape, lambda i: (0,) * len(shape))

    def row(width):
        return pl.BlockSpec((tm, width), lambda i: (i, 0))

    def heads(width):
        return pl.BlockSpec((nh, tm, width), lambda i: (0, i, 0))

    return _pcall(
        body, name="mla_prep", grid=(s // tm,),
        out_shape=[_sds((nh, s, 256), MXU), _sds((nh, s, 256), MXU), _sds((nh, s, MLA_V), MXU),
                   _sds((s, 768), jnp.float32), _sds((s, 1024), jnp.float32),
                   _sds((s, 512), MXU), _sds((s, 512), MXU)],
        in_specs=[col(512, C_CQ), col(512, C_CKV), col(LANES, C_KR), row(LANES), row(LANES),
                  full((1, 512)), full((1, 512)), full((768, 512)), full((N_DEV, 512, LANES)),
                  full((1, LANES)), full((1, LANES)), full((1, LANES)), full((1, LANES))],
        out_specs=[heads(256), heads(256), heads(MLA_V), row(768), row(1024), row(512), row(512)],
        sem=("parallel",))(proj, proj, proj, cos, sin, g_cq, g_ckv, w_uq, w_ukv, g_qn, g_qr, g_kn, g_kr)


def _mla_fwd(qc, kc, v):
    nh, s, _ = qc.shape
    t = min(ATT_TILE, s)
    nb = s // t
    scale = (MLA_NOPE + MLA_ROPE) ** -0.5

    def body(q_ref, k_ref, v_ref, y_ref, lse_ref, m_sc, l_sc, acc):
        qi, ki = pl.program_id(1), pl.program_id(2)

        @pl.when(ki == 0)
        def _():
            m_sc[...] = jnp.full_like(m_sc, NEG_INF)
            l_sc[...] = jnp.zeros_like(l_sc)
            acc[...] = jnp.zeros_like(acc)

        def step(diagonal):
            rc = t // 4 if diagonal else t
            for c in range(t // rc):
                rows = slice(rc * c, rc * (c + 1))
                keys = slice(0, rc * (c + 1))
                sc = _dot_nt(q_ref[0, rows, :], k_ref[0, keys, :]) * (scale * LOG2E)
                if diagonal:
                    r_i = lax.broadcasted_iota(jnp.int32, sc.shape, 0) + rc * c
                    c_i = lax.broadcasted_iota(jnp.int32, sc.shape, 1)
                    sc = jnp.where(c_i <= r_i, sc, NEG_INF)
                m_old = m_sc[rows, :]
                m_new = jnp.maximum(m_old, jnp.max(sc, -1, keepdims=True))
                alpha = jnp.exp2(m_old - m_new)
                p = jnp.exp2(sc - m_new)
                l_sc[rows, :] = alpha * l_sc[rows, :] + jnp.sum(p, -1, keepdims=True)
                acc[rows, :] = alpha * acc[rows, :] + _dot(p, v_ref[0, keys, :])
                m_sc[rows, :] = m_new

        @pl.when(ki < qi)
        def _():
            step(False)

        @pl.when(ki == qi)
        def _():
            step(True)

        @pl.when(ki == qi)
        def _():
            y_ref[...] = acc[...] / l_sc[...]
            lse_ref[0] = m_sc[...] + jnp.log2(l_sc[...])

    return _pcall(
        body, name="mla_fwd", grid=(nh, nb, nb),
        out_shape=[_sds((s, nh * MLA_V), jnp.float32), _sds((nh, s, 1), jnp.float32)],
        in_specs=[pl.BlockSpec((1, t, 256), lambda h, i, k: (h, i, 0)),
                  pl.BlockSpec((1, t, 256), lambda h, i, k: (h, jnp.minimum(k, i), 0)),
                  pl.BlockSpec((1, t, MLA_V), lambda h, i, k: (h, jnp.minimum(k, i), 0))],
        out_specs=[pl.BlockSpec((t, MLA_V), lambda h, i, k: (i, h)),
                   pl.BlockSpec((1, t, 1), lambda h, i, k: (h, i, 0))],
        scratch=[pltpu.VMEM((t, 1), jnp.float32), pltpu.VMEM((t, 1), jnp.float32),
                 pltpu.VMEM((t, MLA_V), jnp.float32)],
        sem=("parallel", "parallel", "arbitrary"))(qc, kc, v)


def _memkv_prep(mem, g_mem, w_mkv, g_mk):
    ml, d = mem.shape
    hw = MEM_HEADS * MEM_DIM

    def body(mem_ref, g_ref, w_ref, gk_ref, k_ref, v_ref, kv_ref, mn_ref):
        mn, _, _ = _norm_fwd(mem_ref[...], g_ref[...])
        mn_ref[...] = mn.astype(mn_ref.dtype)
        kv = _dot(mn, w_ref[...])
        kv_ref[...] = kv
        for h in range(MEM_HEADS):
            kn, _, _ = _norm_fwd(kv[:, MEM_DIM * h: MEM_DIM * (h + 1)], gk_ref[...])
            k_ref[:, MEM_DIM * h: MEM_DIM * (h + 1)] = kn.astype(k_ref.dtype)
        v_ref[...] = kv[:, hw:].astype(v_ref.dtype)

    vm = pl.BlockSpec(memory_space=pltpu.VMEM)
    return _pcall(
        body, name="memkv_prep",
        out_shape=[_sds((ml, hw), MXU), _sds((ml, hw), MXU), _sds((ml, 2 * hw), jnp.float32), _sds((ml, d), MXU)],
        in_specs=[vm] * 4, out_specs=[vm] * 4)(mem, g_mem, w_mkv, g_mk)


def _mem_fwd(proj, g_mq, km, vmm):
    s = proj.shape[0]
    ml, hw = km.shape
    tm = min(FFN_TILE, s)
    scale = MEM_DIM ** -0.5

    def body(q_ref, g_ref, k_ref, v_ref, y_ref, lse_ref):
        col = lax.broadcasted_iota(jnp.int32, (tm, MEM_HEADS), 1)
        lse_t = jnp.zeros((tm, MEM_HEADS), jnp.float32)
        for h in range(MEM_HEADS):
            sl = slice(MEM_DIM * h, MEM_DIM * (h + 1))
            qn, _, _ = _norm_fwd(q_ref[:, sl], g_ref[...])
            sc = _dot_nt(qn, k_ref[:, sl]) * scale
            m = jnp.max(sc, -1, keepdims=True)
            p = jnp.exp(sc - m)
            l = jnp.sum(p, -1, keepdims=True)
            y_ref[:, sl] = _dot(p, v_ref[:, sl]) / l
            lse_t = jnp.where(col == h, m + jnp.log(l), lse_t)
        lse_ref[...] = lse_t

    return _pcall(
        body, name="mem_fwd", grid=(s // tm,),
        out_shape=[_sds((s, hw), jnp.float32), _sds((s, MEM_HEADS), jnp.float32)],
        in_specs=[pl.BlockSpec((tm, hw), lambda i: (i, C_QM // hw)), pl.BlockSpec((1, MEM_DIM), lambda i: (0, 0)),
                  pl.BlockSpec((ml, hw), lambda i: (0, 0)), pl.BlockSpec((ml, hw), lambda i: (0, 0))],
        out_specs=[pl.BlockSpec((tm, hw), lambda i: (i, 0)), pl.BlockSpec((tm, MEM_HEADS), lambda i: (i, 0))],
        sem=("parallel",))(proj, g_mq, km, vmm)


def _alibi_slope(h):
    return float(2.0 ** (-8.0 * (h + 1) / SWA_Q_HEADS))


def _swa_common(n, kp, kc, vp, vc, pq, pkp, pkc, gk):
    b = SWA_BLOCK
    k_raw = jnp.concatenate([kp, kc], axis=0)
    kn, kxn, kr = _norm_fwd(k_raw, gk, half=True)
    v = jnp.concatenate([vp, vc], axis=0)
    dist = jnp.abs(pq - jnp.concatenate([pkp, pkc], axis=1))
    r_i = lax.broadcasted_iota(jnp.int32, (b, 2 * b), 0)
    c_i = lax.broadcasted_iota(jnp.int32, (b, 2 * b), 1)
    valid = (c_i > r_i) & (c_i <= r_i + b) & (c_i >= jnp.where(n > 0, 0, b))
    bias = jnp.where(valid, -dist, NEG_INF)
    return kn, v, bias


def _swa_specs(s):
    b = SWA_BLOCK
    prev = lambda n: jnp.maximum(n - 1, 0)
    return [
        pl.BlockSpec((b, 1024), lambda n: (n, C_QA // 1024)),
        pl.BlockSpec((b, LANES), lambda n: (prev(n), C_KA // LANES)),
        pl.BlockSpec((b, LANES), lambda n: (n, C_KA // LANES)),
        pl.BlockSpec((b, LANES), lambda n: (prev(n), C_VA // LANES)),
        pl.BlockSpec((b, LANES), lambda n: (n, C_VA // LANES)),
        pl.BlockSpec((b, 1), lambda n: (n, 0)),
        pl.BlockSpec((1, b), lambda n: (0, prev(n))),
        pl.BlockSpec((1, b), lambda n: (0, n)),
        pl.BlockSpec((1, LANES), lambda n: (0, 0)),
        pl.BlockSpec((1, LANES), lambda n: (0, 0)),
        pl.BlockSpec(memory_space=pltpu.SMEM),
    ]


def _swa_fwd(proj, posc, posr, gq, gk, sinks):
    s = proj.shape[0]
    b = SWA_BLOCK
    scale = SWA_DIM ** -0.5

    def body(q_ref, kp_ref, kc_ref, vp_ref, vc_ref, pq_ref, pkp_ref, pkc_ref, gq_ref, gk_ref, sink_ref,
             y_ref, lse_ref):
        n = pl.program_id(0)
        kn, v, bias = _swa_common(n, kp_ref[...], kc_ref[...], vp_ref[...], vc_ref[...],
                                  pq_ref[...], pkp_ref[...], pkc_ref[...], gk_ref[...])
        lo = _lo_mask((b, LANES))
        col = lax.broadcasted_iota(jnp.int32, (b, SWA_Q_HEADS), 1)
        lse_t = jnp.zeros((b, SWA_Q_HEADS), jnp.float32)
        hpg = SWA_Q_HEADS // SWA_KV_HEADS
        for g in range(SWA_KV_HEADS):
            heads = range(hpg * g, hpg * (g + 1))
            kvmask = lo if g == 0 else jnp.logical_not(lo)
            qs = []
            for j in range(hpg // 2 * g, hpg // 2 * (g + 1)):
                qn, _, _ = _norm_fwd(q_ref[:, LANES * j: LANES * (j + 1)], gq_ref[...], half=True)
                qn = qn * scale
                qsw = pltpu.roll(qn, 64, 1)
                qs += [jnp.where(kvmask, qn if e == g else qsw, 0.0) for e in range(2)]
            sc_st = _dot_nt(jnp.concatenate(qs, axis=0), kn)
            ps, ls = [], []
            for i, h in enumerate(heads):
                sc = sc_st[b * i: b * (i + 1)] + _alibi_slope(h) * bias
                sk = sink_ref[h]
                m = jnp.maximum(jnp.max(sc, -1, keepdims=True), sk)
                p = jnp.exp(sc - m)
                l = jnp.sum(p, -1, keepdims=True) + jnp.exp(sk - m)
                ps.append(p.astype(MXU))
                ls.append(l)
                lse_t = jnp.where(col == h, m + jnp.log(l), lse_t)
            o_st = _dot(jnp.concatenate(ps, axis=0), v)
            for j in range(hpg // 2 * g, hpg // 2 * (g + 1)):
                halves = []
                for e in range(2):
                    i = 2 * j + e - hpg * g
                    o_h = o_st[b * i: b * (i + 1)] / ls[i]
                    halves.append(o_h if e == g else pltpu.roll(o_h, 64, 1))
                y_ref[:, LANES * j: LANES * (j + 1)] = jnp.where(lo, halves[0], halves[1])
        lse_ref[...] = lse_t

    return _pcall(
        body, name="swa_fwd", grid=(s // b,),
        out_shape=[_sds((s, 1024), jnp.float32), _sds((s, SWA_Q_HEADS), jnp.float32)],
        in_specs=_swa_specs(s),
        out_specs=[pl.BlockSpec((b, 1024), lambda n: (n, 0)), pl.BlockSpec((b, SWA_Q_HEADS), lambda n: (n, 0))],
        sem=("parallel",))(proj, proj, proj, proj, proj, posc, posr, posr, gq, gk, sinks)


def _out_proj(y_a, y_b, y_m, x, w_out, g_ffn):
    s, d = x.shape
    tm = min(ROW_TILE, s)

    def body(ya_ref, yb_ref, ym_ref, x_ref, w_ref, g_ref, h1_ref, fn_ref):
        y = jnp.concatenate([ya_ref[...].astype(MXU), yb_ref[...].astype(MXU), ym_ref[...].astype(MXU)], axis=1)
        h1 = x_ref[...] + _dot(y, w_ref[...])
        h1_ref[...] = h1
        fn, _, _ = _norm_fwd(h1, g_ref[...])
        fn_ref[...] = fn.astype(fn_ref.dtype)

    def row(width):
        return pl.BlockSpec((tm, width), lambda i: (i, 0))

    return _pcall(
        body, name="out_proj", grid=(s // tm,),
        out_shape=[_sds((s, d), jnp.float32), _sds((s, d), MXU)],
        in_specs=[row(1024), row(512), row(512), row(d), pl.BlockSpec(w_out.shape, lambda i: (0, 0)),
                  pl.BlockSpec((1, d), lambda i: (0, 0))],
        out_specs=[row(d), row(d)], sem=("parallel",))(y_a, y_b, y_m, x, w_out, g_ffn)


def _ffn_gu(fn, w_gu):
    s, d = fn.shape
    f = w_gu.shape[2]
    tm = min(2 * FFN_TILE, s)

    def body(fn_ref, w_ref, gu_ref, act_ref):
        x = fn_ref[...]
        g = _dot_nt(x, w_ref[0, 0])
        u = _dot_nt(x, w_ref[0, 1])
        gu_ref[0, 0] = g
        gu_ref[0, 1] = u
        act_ref[0] = (g * jax.nn.sigmoid(g) * u).astype(act_ref.dtype)

    return _pcall(
        body, name="ffn_gate_up", grid=(N_DEV, s // tm),
        out_shape=[_sds((N_DEV, 2, s, f), jnp.float32), _sds((N_DEV, s, f), MXU)],
        in_specs=[pl.BlockSpec((tm, d), lambda j, i: (i, 0)),
                  pl.BlockSpec((1, 2, f, d), lambda j, i: (j, 0, 0, 0))],
        out_specs=[pl.BlockSpec((1, 2, tm, f), lambda j, i: (j, 0, i, 0)),
                   pl.BlockSpec((1, tm, f), lambda j, i: (j, i, 0))],
        sem=("parallel", "parallel"))(fn, w_gu)


def _ffn_down(act, w_d, h1, target):
    _, s, f = act.shape
    d = h1.shape[1]
    tm = min(FFN_TILE, s)

    def body(a_ref, w_ref, h1_ref, t_ref, dout_ref, doutb_ref, loss_ref, acc):
        i, j = pl.program_id(0), pl.program_id(1)
        part = _dot(a_ref[0], w_ref[0]) + _dot(a_ref[1], w_ref[1])

        @pl.when(j == 0)
        def _():
            acc[...] = h1_ref[...] + part

        @pl.when(j > 0)
        def _():
            acc[...] += part

        @pl.when((i == 0) & (j == 0))
        def _():
            loss_ref[...] = jnp.zeros_like(loss_ref)

        @pl.when(j == N_DEV // 2 - 1)
        def _():
            diff = acc[...] - t_ref[...]
            dout_ref[...] = diff / d
            doutb_ref[...] = (diff / d).astype(doutb_ref.dtype)
            loss_ref[...] += 0.5 * jnp.sum(jnp.sum(diff * diff, -1, keepdims=True) / d)

    row = pl.BlockSpec((tm, d), lambda i, j: (i, 0))
    return _pcall(
        body, name="ffn_down", grid=(s // tm, N_DEV // 2),
        out_shape=[_sds((s, d), jnp.float32), _sds((s, d), MXU), _sds((8, LANES), jnp.float32)],
        in_specs=[pl.BlockSpec((2, tm, f), lambda i, j: (j, i, 0)), pl.BlockSpec((2, f, d), lambda i, j: (j, 0, 0)),
                  row, row],
        out_specs=[row, row, pl.BlockSpec((8, LANES), lambda i, j: (0, 0))],
        scratch=[pltpu.VMEM((tm, d), jnp.float32)], sem=("arbitrary", "arbitrary"))(act, w_d, h1, target)


def _ffn_bwd_act(dout, w_d, gu):
    s, d = dout.shape
    f = w_d.shape[1]
    tm = min(2 * FFN_TILE, s)
    ni = s // tm

    def body(do_ref, w_ref, gu_ref, dgu_ref, dw_ref, acc):
        i = pl.program_id(1)
        do = do_ref[...]
        d_act = _dot_nt(do, w_ref[0])
        g, u = gu_ref[0, 0], gu_ref[0, 1]
        sig = jax.nn.sigmoid(g)
        silu = g * sig
        dgu_ref[0, 0] = (d_act * u * (sig * (1.0 + g * (1.0 - sig)))).astype(dgu_ref.dtype)
        dgu_ref[0, 1] = (d_act * silu).astype(dgu_ref.dtype)
        part = _dot_tn(silu * u, do)

        @pl.when(i == 0)
        def _():
            acc[...] = part

        @pl.when(i > 0)
        def _():
            acc[...] += part

        @pl.when(i == ni - 1)
        def _():
            dw_ref[0] = acc[...].astype(dw_ref.dtype)

    return _pcall(
        body, name="ffn_bwd_act", grid=(N_DEV, ni),
        out_shape=[_sds((N_DEV, 2, s, f), MXU), _sds((N_DEV, f, d), WIRE)],
        in_specs=[pl.BlockSpec((tm, d), lambda j, i: (i, 0)), pl.BlockSpec((1, f, d), lambda j, i: (j, 0, 0)),
                  pl.BlockSpec((1, 2, tm, f), lambda j, i: (j, 0, i, 0))],
        out_specs=[pl.BlockSpec((1, 2, tm, f), lambda j, i: (j, 0, i, 0)),
                   pl.BlockSpec((1, f, d), lambda j, i: (j, 0, 0))],
        scratch=[pltpu.VMEM((f, d), jnp.float32)], sem=("parallel", "arbitrary"))(dout, w_d, gu)


def _ffn_dw_gu(fn, dgu):
    s, d = fn.shape
    f = dgu.shape[-1]
    tk = min(2 * FFN_TILE, s)
    nk = s // tk

    def body(fn_ref, dgu_ref, dw_ref, acc):
        k = pl.program_id(1)
        x = fn_ref[...]
        pg = _dot_tn(dgu_ref[0, 0], x)
        pu = _dot_tn(dgu_ref[0, 1], x)

        @pl.when(k == 0)
        def _():
            acc[0] = pg
            acc[1] = pu

        @pl.when(k > 0)
        def _():
            acc[0] += pg
            acc[1] += pu

        @pl.when(k == nk - 1)
        def _():
            dw_ref[0] = acc[...].astype(dw_ref.dtype)

    return _pcall(
        body, name="ffn_dw_gate_up", grid=(N_DEV, nk),
        out_shape=_sds((N_DEV, 2, f, d), WIRE),
        in_specs=[pl.BlockSpec((tk, d), lambda j, k: (k, 0)), pl.BlockSpec((1, 2, tk, f), lambda j, k: (j, 0, k, 0))],
        out_specs=pl.BlockSpec((1, 2, f, d), lambda j, k: (j, 0, 0, 0)),
        scratch=[pltpu.VMEM((2, f, d), jnp.float32)], sem=("parallel", "arbitrary"))(fn, dgu)


def _ffn_dfn(dgu, w_gu, after):
    _, _, s, f = dgu.shape
    d = w_gu.shape[3]
    tm = min(2 * FFN_TILE, s)

    def body(dgu_ref, w_ref, dfn_ref):
        j = pl.program_id(1)
        part = _dot(dgu_ref[0, 0], w_ref[0, 0]) + _dot(dgu_ref[0, 1], w_ref[0, 1])

        @pl.when(j == 0)
        def _():
            dfn_ref[...] = part

        @pl.when(j > 0)
        def _():
            dfn_ref[...] += part

    return _pcall(
        body, name="ffn_dfn", grid=(s // tm, N_DEV),
        out_shape=_sds((s, d), jnp.float32),
        in_specs=[pl.BlockSpec((1, 2, tm, f), lambda i, j: (j, 0, i, 0)),
                  pl.BlockSpec((1, 2, f, d), lambda i, j: (j, 0, 0, 0))],
        out_specs=pl.BlockSpec((tm, d), lambda i, j: (i, 0)),
        sem=("parallel", "arbitrary"), after=after)(dgu, w_gu)


def _ffn_norm_bwd(d_fn, dout, h1, g_ffn):
    s, d = h1.shape
    tm = min(ROW_TILE, s)

    def body(dfn_ref, do_ref, h1_ref, g_ref, dh1_ref, dg_ref):
        i = pl.program_id(0)

        @pl.when(i == 0)
        def _():
            dg_ref[...] = jnp.zeros_like(dg_ref)

        _, xn, r = _norm_fwd(h1_ref[...], g_ref[...])
        dx, dg = _norm_bwd(xn, r, g_ref[...], dfn_ref[...])
        dh1_ref[...] = do_ref[...] + dx
        dg_ref[...] += dg

    row = pl.BlockSpec((tm, d), lambda i: (i, 0))
    vec = pl.BlockSpec((1, d), lambda i: (0, 0))
    return _pcall(
        body, name="ffn_norm_bwd", grid=(s // tm,),
        out_shape=[_sds((s, d), jnp.float32), _sds((1, d), jnp.float32)],
        in_specs=[row, row, row, vec], out_specs=[row, vec], sem=("arbitrary",))(d_fn, dout, h1, g_ffn)


def _mem_bwd(proj, g_mq, km, vmm, d_y, y_m, lse):
    s = proj.shape[0]
    ml, hw = km.shape
    tm = min(FFN_TILE, s)
    scale = MEM_DIM ** -0.5

    def body(q_ref, g_ref, k_ref, v_ref, do_ref, y_ref, lse_ref, dq_ref, dk_ref, dv_ref, dg_ref):
        i = pl.program_id(0)

        @pl.when(i == 0)
        def _():
            dk_ref[...] = jnp.zeros_like(dk_ref)
            dv_ref[...] = jnp.zeros_like(dv_ref)
            dg_ref[...] = jnp.zeros_like(dg_ref)

        col = lax.broadcasted_iota(jnp.int32, (tm, MEM_HEADS), 1)
        lse_t = lse_ref[...]
        for h in range(MEM_HEADS):
            sl = slice(MEM_DIM * h, MEM_DIM * (h + 1))
            qn, xn, r = _norm_fwd(q_ref[:, sl], g_ref[...])
            lse_h = jnp.sum(jnp.where(col == h, lse_t, 0.0), -1, keepdims=True)
            p = jnp.exp(_dot_nt(qn, k_ref[:, sl]) * scale - lse_h)
            do = do_ref[:, sl]
            dd = jnp.sum(do * y_ref[:, sl], -1, keepdims=True)
            dp = _dot_nt(do, v_ref[:, sl])
            ds = (p * (dp - dd)).astype(MXU)
            dv_ref[:, sl] += _dot_tn(p, do)
            dk_ref[:, sl] += _dot_tn(ds, qn) * scale
            dx, dg = _norm_bwd(xn, r, g_ref[...], _dot(ds, k_ref[:, sl]) * scale)
            dq_ref[:, sl] = dx.astype(dq_ref.dtype)
            dg_ref[...] += dg

    full = pl.BlockSpec((ml, hw), lambda i: (0, 0))
    return _pcall(
        body, name="mem_bwd", grid=(s // tm,),
        out_shape=[_sds((s, hw), MXU), _sds((ml, hw), jnp.float32), _sds((ml, hw), jnp.float32),
                   _sds((1, MEM_DIM), jnp.float32)],
        in_specs=[pl.BlockSpec((tm, hw), lambda i: (i, C_QM // hw)), pl.BlockSpec((1, MEM_DIM), lambda i: (0, 0)),
                  full, full, pl.BlockSpec((tm, hw), lambda i: (i, 3)), pl.BlockSpec((tm, hw), lambda i: (i, 0)),
                  pl.BlockSpec((tm, MEM_HEADS), lambda i: (i, 0))],
        out_specs=[pl.BlockSpec((tm, hw), lambda i: (i, 0)), full, full,
                   pl.BlockSpec((1, MEM_DIM), lambda i: (0, 0))],
        sem=("arbitrary",))(proj, g_mq, km, vmm, d_y, y_m, lse)


def _memkv_bwd(mem, g_mem, w_mkv, g_mk, kv, memn, dk, dv):
    ml, d = mem.shape
    hw = MEM_HEADS * MEM_DIM

    def body(mem_ref, g_ref, w_ref, gk_ref, kv_ref, mn_ref, dk_ref, dv_ref, dw_ref, dgm_ref, dgk_ref):
        parts = []
        dgk = jnp.zeros((1, MEM_DIM), jnp.float32)
        for h in range(MEM_HEADS):
            sl = slice(MEM_DIM * h, MEM_DIM * (h + 1))
            _, xn, r = _norm_fwd(kv_ref[:, sl], gk_ref[...])
            dx, dg = _norm_bwd(xn, r, gk_ref[...], dk_ref[:, sl])
            parts.append(dx)
            dgk = dgk + dg
        dkv = jnp.concatenate(parts + [dv_ref[...]], axis=1).astype(MXU)
        dgk_ref[...] = dgk
        dw_ref[...] = _dot_tn(mn_ref[...], dkv).astype(dw_ref.dtype)
        d_mn = _dot_nt(dkv, w_ref[...])
        _, xn, _ = _norm_fwd(mem_ref[...], g_ref[...])
        dgm_ref[...] = jnp.sum(d_mn * xn, 0, keepdims=True)

    vm = pl.BlockSpec(memory_space=pltpu.VMEM)
    return _pcall(
        body, name="memkv_bwd",
        out_shape=[_sds((d, 2 * hw), WIRE), _sds((1, d), jnp.float32), _sds((1, MEM_DIM), jnp.float32)],
        in_specs=[vm] * 8, out_specs=[vm] * 3)(mem, g_mem, w_mkv, g_mk, kv, memn, dk, dv)


def _mla_bwd(qc, kc, v, d_y, y_b, lse, after):
    nh, s, _ = qc.shape
    t = min(ATT_TILE, s)
    nb = s // t
    scale = (MLA_NOPE + MLA_ROPE) ** -0.5

    def body(q_ref, k_ref, v_ref, do_ref, y_ref, lse_ref, dq_ref, dk_ref, dv_ref, dk_acc, dv_acc):
        kj, qi = pl.program_id(1), pl.program_id(2)

        @pl.when((kj == 0) & (qi == 0))
        def _():
            dq_ref[...] = jnp.zeros_like(dq_ref)

        @pl.when(qi == kj)
        def _():
            dk_acc[...] = jnp.zeros_like(dk_acc)
            dv_acc[...] = jnp.zeros_like(dv_acc)

        def step(diagonal):
            rc = t // 4 if diagonal else t
            for c in range(t // rc):
                rows = slice(rc * c, rc * (c + 1))
                keys = slice(0, rc * (c + 1))
                q, k = q_ref[0, rows, :], k_ref[0, keys, :]
                sc = _dot_nt(q, k) * (scale * LOG2E)
                if diagonal:
                    r_i = lax.broadcasted_iota(jnp.int32, sc.shape, 0) + rc * c
                    c_i = lax.broadcasted_iota(jnp.int32, sc.shape, 1)
                    sc = jnp.where(c_i <= r_i, sc, NEG_INF)
                p = jnp.exp2(sc - lse_ref[0, rows, :])
                do = do_ref[rows, :]
                dd = jnp.sum(do * y_ref[rows, :], -1, keepdims=True)
                dp = _dot_nt(do, v_ref[0, keys, :])
                ds = (p * (dp - dd) * scale).astype(MXU)
                dv_acc[keys, :] += _dot_tn(p, do)
                dk_acc[keys, :] += _dot_tn(ds, q)
                out_rows = pl.ds(pl.multiple_of(qi * t + rc * c, rc), rc)
                dq_ref[0, out_rows, :] += _dot(ds, k)

        @pl.when(qi > kj)
        def _():
            step(False)

        @pl.when(qi == kj)
        def _():
            step(True)

        @pl.when(qi == nb - 1)
        def _():
            dk_ref[0] = dk_acc[...]
            dv_ref[0] = dv_acc[...]

    qmap = lambda h, j, i: (h, jnp.maximum(i, j), 0)
    return _pcall(
        body, name="mla_bwd", grid=(nh, nb, nb),
        out_shape=[_sds((nh, s, 256), jnp.float32), _sds((nh, s, 256), jnp.float32),
                   _sds((nh, s, MLA_V), jnp.float32)],
        in_specs=[pl.BlockSpec((1, t, 256), qmap),
                  pl.BlockSpec((1, t, 256), lambda h, j, i: (h, j, 0)),
                  pl.BlockSpec((1, t, MLA_V), lambda h, j, i: (h, j, 0)),
                  pl.BlockSpec((t, MLA_V), lambda h, j, i: (jnp.maximum(i, j), 8 + h)),
                  pl.BlockSpec((t, MLA_V), lambda h, j, i: (jnp.maximum(i, j), h)),
                  pl.BlockSpec((1, t, 1), qmap)],
        out_specs=[pl.BlockSpec((1, s, 256), lambda h, j, i: (h, 0, 0)),
                   pl.BlockSpec((1, t, 256), lambda h, j, i: (h, j, 0)),
                   pl.BlockSpec((1, t, MLA_V), lambda h, j, i: (h, j, 0))],
        scratch=[pltpu.VMEM((t, 256), jnp.float32), pltpu.VMEM((t, MLA_V), jnp.float32)],
        sem=("parallel", "arbitrary", "arbitrary"), after=after)(qc, kc, v, d_y, y_b, lse)


def _mla_prep_bwd(proj, cos, sin, g_cq, g_ckv, w_uq, w_ukv, g_qn, g_qr, g_kn, g_kr,
                  qb, kvb, cqn, ckvn, dqc, dkc, dv):
    s = proj.shape[0]
    tm = min(ROW_TILE, s)
    nh = MLA_HEADS
    ni = s // tm

    def body(cq_ref, ckv_ref, kr_ref, cos_ref, sin_ref, gcq_ref, gckv_ref, wuq_ref, wukv_ref,
             gqn_ref, gqr_ref, gkn_ref, gkr_ref, qb_ref, kvb_ref, cqn_ref, ckvn_ref, dqc_ref, dkc_ref, dv_ref,
             dcq_ref, dckv_ref, dkr_ref, dwuq_ref, dwukv_ref,
             dgcq_ref, dgckv_ref, dgqn_ref, dgqr_ref, dgkn_ref, dgkr_ref, acc_uq, acc_ukv):
        i = pl.program_id(0)

        @pl.when(i == 0)
        def _():
            acc_uq[...] = jnp.zeros_like(acc_uq)
            acc_ukv[...] = jnp.zeros_like(acc_ukv)
            for ref in (dgcq_ref, dgckv_ref, dgqn_ref, dgqr_ref, dgkn_ref, dgkr_ref):
                ref[...] = jnp.zeros_like(ref)

        cos_t, sin_t = cos_ref[...], sin_ref[...]
        lo = _lo_mask((tm, LANES))
        qb_v, kvb_v = qb_ref[...], kvb_ref[...]
        dq_parts, dgqn = [], jnp.zeros((1, LANES), jnp.float32)
        for h in range(nh):
            _, xn, r = _norm_fwd(qb_v[:, MLA_NOPE * h: MLA_NOPE * (h + 1)], gqn_ref[...])
            dx, dg = _norm_bwd(xn, r, gqn_ref[...], dqc_ref[h][:, :MLA_NOPE])
            dq_parts.append(dx)
            dgqn = dgqn + dg
        dgqn_ref[...] += dgqn
        dgqr = jnp.zeros((1, LANES), jnp.float32)
        for j in range(nh // 2):
            d_rope = jnp.where(lo, dqc_ref[2 * j][:, MLA_NOPE:], dqc_ref[2 * j + 1][:, MLA_NOPE:])
            d_pre = _rope_bwd(d_rope, cos_t, sin_t)
            xr = qb_v[:, nh * MLA_NOPE + LANES * j: nh * MLA_NOPE + LANES * (j + 1)]
            _, xn, r = _norm_fwd(xr, gqr_ref[...], half=True)
            dx, dg = _norm_bwd(xn, r, gqr_ref[...], d_pre, half=True)
            dq_parts.append(dx)
            dgqr = dgqr + dg
        dgqr_ref[...] += dgqr
        dqb = jnp.concatenate(dq_parts, axis=1).astype(MXU)
        acc_uq[...] += _dot_tn(dqb, cqn_ref[...])
        _, xn, r = _norm_fwd(cq_ref[...], gcq_ref[...])
        dx, dg = _norm_bwd(xn, r, gcq_ref[...], _dot(dqb, wuq_ref[...]))
        dcq_ref[...] = dx.astype(dcq_ref.dtype)
        dgcq_ref[...] += dg
        dkv_parts, dgkn = [], jnp.zeros((1, LANES), jnp.float32)
        d_kr2 = jnp.zeros((tm, LANES), jnp.float32)
        for h in range(nh):
            _, xn, r = _norm_fwd(kvb_v[:, 256 * h: 256 * h + MLA_NOPE], gkn_ref[...])
            dx, dg = _norm_bwd(xn, r, gkn_ref[...], dkc_ref[h][:, :MLA_NOPE])
            dkv_parts += [dx, dv_ref[h]]
            dgkn = dgkn + dg
            d_kr2 = d_kr2 + dkc_ref[h][:, MLA_NOPE:]
        dgkn_ref[...] += dgkn
        dkvb = jnp.concatenate(dkv_parts, axis=1).astype(MXU)
        d_ckvn = jnp.zeros((tm, 512), jnp.float32)
        for dev in range(N_DEV):
            piece = dkvb[:, LANES * dev: LANES * (dev + 1)]
            acc_ukv[dev] += _dot_tn(ckvn_ref[...], piece)
            d_ckvn = d_ckvn + _dot_nt(piece, wukv_ref[dev])
        _, xn, r = _norm_fwd(ckv_ref[...], gckv_ref[...])
        dx, dg = _norm_bwd(xn, r, gckv_ref[...], d_ckvn)
        dckv_ref[...] = dx.astype(dckv_ref.dtype)
        dgckv_ref[...] += dg
        d_kr = jnp.where(lo, d_kr2 + pltpu.roll(d_kr2, 64, 1), 0.0)
        d_pre = _rope_bwd(d_kr, cos_t, sin_t)
        _, xn, r = _norm_fwd(kr_ref[...], gkr_ref[...], half=True)
        dx, dg = _norm_bwd(xn, r, gkr_ref[...], d_pre, half=True)
        dkr_ref[...] = jnp.where(lo, dx, 0.0).astype(dkr_ref.dtype)
        dgkr_ref[...] += jnp.where(_lo_mask((1, LANES)), dg, 0.0)

        @pl.when(i == ni - 1)
        def _():
            dwuq_ref[...] = acc_uq[...].astype(dwuq_ref.dtype)
            dwukv_ref[...] = acc_ukv[...].astype(dwukv_ref.dtype)

    def col(width, start):
        return pl.BlockSpec((tm, width), lambda i: (i, start // width))

    def full(shape):
        return pl.BlockSpec(shape, lambda i: (0,) * len(shape))

    def row(width):
        return pl.BlockSpec((tm, width), lambda i: (i, 0))

    def heads(width):
        return pl.BlockSpec((nh, tm, width), lambda i: (0, i, 0))

    vec = full((1, LANES))
    return _pcall(
        body, name="mla_prep_bwd", grid=(ni,),
        out_shape=[_sds((s, 512), MXU), _sds((s, 512), MXU), _sds((s, LANES), MXU),
                   _sds((768, 512), WIRE), _sds((N_DEV, 512, LANES), WIRE),
                   _sds((1, 512), jnp.float32), _sds((1, 512), jnp.float32)] + [_sds((1, LANES), jnp.float32)] * 4,
        in_specs=[col(512, C_CQ), col(512, C_CKV), col(LANES, C_KR), row(LANES), row(LANES),
                  full((1, 512)), full((1, 512)), full((768, 512)), full((N_DEV, 512, LANES)), vec, vec, vec, vec,
                  row(768), row(1024), row(512), row(512), heads(256), heads(256), heads(MLA_V)],
        out_specs=[row(512), row(512), row(LANES), full((768, 512)), full((N_DEV, 512, LANES)),
                   full((1, 512)), full((1, 512)), vec, vec, vec, vec],
        scratch=[pltpu.VMEM((768, 512), jnp.float32), pltpu.VMEM((N_DEV, 512, LANES), jnp.float32)],
        sem=("arbitrary",))(proj, proj, proj, cos, sin, g_cq, g_ckv, w_uq, w_ukv, g_qn, g_qr, g_kn, g_kr,
                            qb, kvb, cqn, ckvn, dqc, dkc, dv)


def _swa_bwd(proj, posc, posr, gq, gk, sinks, d_y, y_a, lse, after):
    s = proj.shape[0]
    b = SWA_BLOCK
    nb = s // b
    scale = SWA_DIM ** -0.5

    def body(q_ref, kp_ref, kc_ref, vp_ref, vc_ref, pq_ref, pkp_ref, pkc_ref, gq_ref, gk_ref, sink_ref,
             do_ref, y_ref, lse_ref, kfull_ref,
             dq_ref, dk_ref, dv_ref, dgq_ref, dgk_ref, dsink_ref, dk_acc, dv_acc):
        n = pl.program_id(0)

        @pl.when(n == 0)
        def _():
            dk_acc[...] = jnp.zeros_like(dk_acc)
            dv_acc[...] = jnp.zeros_like(dv_acc)
            dgq_ref[...] = jnp.zeros_like(dgq_ref)
            dsink_ref[...] = jnp.zeros_like(dsink_ref)

        kn, v, bias = _swa_common(n, kp_ref[...], kc_ref[...], vp_ref[...], vc_ref[...],
                                  pq_ref[...], pkp_ref[...], pkc_ref[...], gk_ref[...])
        lo = _lo_mask((b, LANES))
        col = lax.broadcasted_iota(jnp.int32, (b, SWA_Q_HEADS), 1)
        col1 = lax.broadcasted_iota(jnp.int32, (1, SWA_Q_HEADS), 1)
        lse_t = lse_ref[...]
        dk_blk = jnp.zeros((2 * b, LANES), jnp.float32)
        dv_blk = jnp.zeros((2 * b, LANES), jnp.float32)
        dgq = jnp.zeros((1, LANES), jnp.float32)
        dsink = jnp.zeros((1, SWA_Q_HEADS), jnp.float32)
        for j in range(SWA_Q_HEADS // 2):
            hk = (2 * j) // (SWA_Q_HEADS // SWA_KV_HEADS)
            kvmask = lo if hk == 0 else jnp.logical_not(lo)
            sl = slice(LANES * j, LANES * (j + 1))
            qn, xn, r = _norm_fwd(q_ref[:, sl], gq_ref[...], half=True)
            qn = qn * scale
            qsw = pltpu.roll(qn, 64, 1)
            d2 = do_ref[:, sl]
            d2sw = pltpu.roll(d2, 64, 1)
            prod = d2 * y_ref[:, sl]
            dqs = []
            for e in range(2):
                h = 2 * j + e
                half_e = lo if e == 0 else jnp.logical_not(lo)
                qm = jnp.where(kvmask, qn if e == hk else qsw, 0.0)
                dm = jnp.where(kvmask, d2 if e == hk else d2sw, 0.0)
                sc = _dot_nt(qm, kn) + _alibi_slope(h) * bias
                lse_h = jnp.sum(jnp.where(col == h, lse_t, 0.0), -1, keepdims=True)
                p = jnp.exp(sc - lse_h)
                dd = jnp.sum(jnp.where(half_e, prod, 0.0), -1, keepdims=True)
                dp = _dot_nt(dm, v)
                ds = (p * (dp - dd)).astype(MXU)
                dsink = dsink - jnp.where(col1 == h, jnp.sum(jnp.exp(sink_ref[h] - lse_h) * dd), 0.0)
                dq_m = _dot(ds, kn) * scale
                dk_blk = dk_blk + _dot_tn(ds, qm)
                dv_blk = dv_blk + _dot_tn(p, dm)
                dqs.append(dq_m if e == hk else pltpu.roll(dq_m, 64, 1))
            dx, dg = _norm_bwd(xn, r, gq_ref[...], jnp.where(lo, dqs[0], dqs[1]), half=True)
            dq_ref[:, sl] = dx.astype(dq_ref.dtype)
            dgq = dgq + dg
        dgq_ref[...] += dgq
        dsink_ref[...] += dsink
        prev = pl.ds(pl.multiple_of(jnp.maximum(n - 1, 0) * b, b), b)
        cur = pl.ds(pl.multiple_of(n * b, b), b)
        dk_acc[prev, :] += dk_blk[:b]
        dv_acc[prev, :] += dv_blk[:b]
        dk_acc[cur, :] += dk_blk[b:]
        dv_acc[cur, :] += dv_blk[b:]

        @pl.when(n == nb - 1)
        def _():
            _, kxn, kr = _norm_fwd(kfull_ref[...], gk_ref[...], half=True)
            dx, dg = _norm_bwd(kxn, kr, gk_ref[...], dk_acc[...], half=True)
            dk_ref[...] = dx.astype(dk_ref.dtype)
            dv_ref[...] = dv_acc[...].astype(dv_ref.dtype)
            dgk_ref[...] = dg

    full = pl.BlockSpec((s, LANES), lambda n: (0, 0))
    vec = pl.BlockSpec((1, LANES), lambda n: (0, 0))
    return _pcall(
        body, name="swa_bwd", grid=(nb,),
        out_shape=[_sds((s, 1024), MXU), _sds((s, LANES), MXU), _sds((s, LANES), MXU),
                   _sds((1, LANES), jnp.float32), _sds((1, LANES), jnp.float32),
                   _sds((1, SWA_Q_HEADS), jnp.float32)],
        in_specs=_swa_specs(s) + [pl.BlockSpec((b, 1024), lambda n: (n, 0)), pl.BlockSpec((b, 1024), lambda n: (n, 0)),
                                  pl.BlockSpec((b, SWA_Q_HEADS), lambda n: (n, 0)),
                                  pl.BlockSpec((s, LANES), lambda n: (0, C_KA // LANES))],
        out_specs=[pl.BlockSpec((b, 1024), lambda n: (n, 0)), full, full, vec, vec,
                   pl.BlockSpec((1, SWA_Q_HEADS), lambda n: (0, 0))],
        scratch=[pltpu.VMEM((s, LANES), jnp.float32), pltpu.VMEM((s, LANES), jnp.float32)],
        sem=("arbitrary",), after=after)(proj, proj, proj, proj, proj, posc, posr, posr, gq, gk, sinks, d_y, y_a, lse,
                                         proj)


def _dx(d_proj, w_in, x, g, d_h1, after):
    s, d = x.shape
    n = w_in.shape[0]
    tm = min(ROW_TILE, s)

    def body(dp_ref, w_ref, x_ref, g_ref, dh_ref, dx_ref, dg_ref):
        i = pl.program_id(0)

        @pl.when(i == 0)
        def _():
            dg_ref[...] = jnp.zeros_like(dg_ref)

        d_hn = _dot(dp_ref[...], w_ref[...])
        _, xn, r = _norm_fwd(x_ref[...], g_ref[...])
        dx, dg = _norm_bwd(xn, r, g_ref[...], d_hn)
        dx_ref[...] = dh_ref[...] + dx
        dg_ref[...] += dg

    row = pl.BlockSpec((tm, d), lambda i: (i, 0))
    vec = pl.BlockSpec((1, d), lambda i: (0, 0))
    return _pcall(
        body, name="grad_x", grid=(s // tm,),
        out_shape=[_sds((s, d), jnp.float32), _sds((1, d), jnp.float32)],
        in_specs=[pl.BlockSpec((tm, n), lambda i: (i, 0)), pl.BlockSpec((n, d), lambda i: (0, 0)), row, vec, row],
        out_specs=[row, vec], sem=("arbitrary",), after=after)(d_proj, w_in, x, g, d_h1)


_SMALL = ["attn_norm_g", "swa_q_norm_g", "swa_k_norm_g", "swa_sinks", "mla_cq_norm_g", "mla_ckv_norm_g",
          "mla_qn_norm_g", "mla_qr_norm_g", "mla_kn_norm_g", "mla_kr_norm_g", "mem_norm_g",
          "mem_q_norm_g", "mem_k_norm_g", "ffn_norm_g"]


def kernel(x, mem, positions, attn_norm_g, w_in, swa_q_norm_g, swa_k_norm_g, swa_sinks, mla_cq_norm_g, mla_ckv_norm_g, w_uq, w_ukv, mla_qn_norm_g, mla_qr_norm_g, mla_kn_norm_g, mla_kr_norm_g, mem_norm_g, w_mem_kv, mem_q_norm_g, mem_k_norm_g, w_out, ffn_norm_g, w_gate, w_up, w_down, loss_target, m_attn_norm_g, m_w_in, m_swa_q_norm_g, m_swa_k_norm_g, m_swa_sinks, m_mla_cq_norm_g, m_mla_ckv_norm_g, m_w_uq, m_w_ukv, m_mla_qn_norm_g, m_mla_qr_norm_g, m_mla_kn_norm_g, m_mla_kr_norm_g, m_mem_norm_g, m_w_mem_kv, m_mem_q_norm_g, m_mem_k_norm_g, m_w_out, m_ffn_norm_g, m_w_gate, m_w_up, m_w_down, v_attn_norm_g, v_w_in, v_swa_q_norm_g, v_swa_k_norm_g, v_swa_sinks, v_mla_cq_norm_g, v_mla_ckv_norm_g, v_w_uq, v_w_ukv, v_mla_qn_norm_g, v_mla_qr_norm_g, v_mla_kn_norm_g, v_mla_kr_norm_g, v_mem_norm_g, v_w_mem_kv, v_mem_q_norm_g, v_mem_k_norm_g, v_w_out, v_ffn_norm_g, v_w_gate, v_w_up, v_w_down):
    args = dict(locals())
    x2, mem2, tgt = x[0], mem[0], loss_target[0]
    s, d = x2.shape
    n_in = w_in.shape[2]
    f = w_gate.shape[2]

    (g_in,) = _all_gather([w_in[0].T.astype(WIRE)])
    mix_shards = [w_uq[0].T.astype(WIRE), w_ukv[0].astype(WIRE), w_mem_kv[0].astype(WIRE),
                  _to_wire([w_out[0]], g_in, "wire_out")[0]]
    g_uq, wkv, g_mkv, g_out = _all_gather_background(mix_shards, 5, "all_gather_mix_weights")
    ffn_shards = [_to_wire([w_gate[0].T, w_up[0].T], g_in, "wire_gate_up"),
                  _to_wire([w_down[0]], g_in, "wire_down")[0]]
    w_gu, w_d = _all_gather_background(ffn_shards, 1, "all_gather_ffn_weights")
    wi = g_in.reshape(N_DEV * n_in, d)
    wi = jnp.concatenate([wi[0:1024], wi[1280:1792], wi[1792:2304], wi[2368:2880],
                          wi[1024:1152], wi[1152:1280], wi[2304:2368],
                          jnp.zeros((IN_PAD - 2880, d), wi.dtype)], axis=0)
    wq = g_uq.reshape(768, 512)
    wq = jnp.concatenate([wq[192 * h: 192 * h + 128] for h in range(4)]
                         + [wq[192 * h + 128: 192 * (h + 1)] for h in range(4)], axis=0)
    wmkv = g_mkv.reshape(-1, g_mkv.shape[-1])
    wo = g_out.reshape(-1, d)

    pos = positions[0].astype(jnp.float32)
    inv_freq = ROPE_THETA ** (-jnp.arange(0, MLA_ROPE, 2, dtype=jnp.float32) / MLA_ROPE)
    ang = pos[:, None] * inv_freq
    cos32, sin32 = jnp.cos(ang), jnp.sin(ang)
    cos_t = jnp.tile(cos32, (1, 4))
    sin_t = jnp.tile(jnp.concatenate([-sin32, sin32], axis=1), (1, 2))
    posc, posr = pos.reshape(s, 1), pos.reshape(1, s)
    two = lambda g: jnp.tile(g, (1, 2))
    gq2, gk2, gqr2, gkr2 = two(swa_q_norm_g), two(swa_k_norm_g), two(mla_qr_norm_g), two(mla_kr_norm_g)
    sinks1 = swa_sinks[0]

    proj, hn = _in_proj(x2, attn_norm_g, wi)
    qc, kc, vb, qb, kvb, cqn, ckvn = _mla_prep(proj, cos_t, sin_t, mla_cq_norm_g, mla_ckv_norm_g, wq, wkv,
                                                mla_qn_norm_g, gqr2, mla_kn_norm_g, gkr2)
    y_b, lse_b = _mla_fwd(qc, kc, vb)
    km, vmm, kvm, memn = _memkv_prep(mem2, mem_norm_g, wmkv, mem_k_norm_g)
    y_m, lse_m = _mem_fwd(proj, mem_q_norm_g, km, vmm)
    y_a, lse_a = _swa_fwd(proj, posc, posr, gq2, gk2, sinks1)
    h1, fn = _out_proj(y_a, y_b, y_m, x2, wo, ffn_norm_g)
    gu, act = _ffn_gu(fn, w_gu)
    dout, dout_b, loss_tile = _ffn_down(act, w_d, h1, tgt)

    dgu, dw_d = _ffn_bwd_act(dout_b, w_d, gu)
    dw_gu = _ffn_dw_gu(fn, dgu)
    r_gu, r_d = _exchange_grads_background([dw_gu, dw_d], 2, "exchange_ffn_grads")
    d_h1, dg_ffn = _ffn_norm_bwd(_ffn_dfn(dgu, w_gu, dw_gu), dout, h1, ffn_norm_g)
    d_y = _mm(d_h1, wo, tb=True, out_dtype=jnp.float32, tm=FFN_TILE, tk=2048, name="d_mix")
    dw_out = jnp.concatenate([
        _mm(y_a, d_h1, ta=True, out_dtype=WIRE, tm=1024, tk=1024, name="dw_out_a"),
        _mm(y_b, d_h1, ta=True, out_dtype=WIRE, tm=1024, tk=1024, name="dw_out_b"),
        _mm(y_m, d_h1, ta=True, out_dtype=WIRE, tm=1024, tk=1024, name="dw_out_m")], axis=0)
    d_qm, dkm, dvmm, dg_mq = _mem_bwd(proj, mem_q_norm_g, km, vmm, d_y, y_m, lse_m)
    dw_mkv, dg_mem, dg_mk = _memkv_bwd(mem2, mem_norm_g, wmkv, mem_k_norm_g, kvm, memn, dkm, dvmm)
    r_mkv, r_out = _exchange_grads_background([dw_mkv.reshape(g_mkv.shape), dw_out.reshape(g_out.shape)], 3,
                                              "exchange_mix_grads")
    dqc, dkc, dvb = _mla_bwd(qc, kc, vb, d_y, y_b, lse_b, dw_mkv)
    (d_cq, d_ckv, d_kr, dw_uq, dw_ukv, dg_cq, dg_ckv, dg_qn, dg_qr, dg_kn, dg_kr) = _mla_prep_bwd(
        proj, cos_t, sin_t, mla_cq_norm_g, mla_ckv_norm_g, wq, wkv, mla_qn_norm_g, gqr2, mla_kn_norm_g, gkr2,
        qb, kvb, cqn, ckvn, dqc, dkc, dvb)
    d_qa, d_ka, d_va, dg_q, dg_k, d_sinks = _swa_bwd(proj, posc, posr, gq2, gk2, sinks1, d_y, y_a, lse_a, dw_out)
    d_proj = jnp.concatenate([d_qa, d_cq, d_ckv, d_qm, d_ka, d_va, d_kr], axis=1)
    gi = _dw_in(hn, d_proj, n_in)

    gq_ = jnp.concatenate(sum([[dw_uq[128 * h: 128 * (h + 1)], dw_uq[512 + 64 * h: 512 + 64 * (h + 1)]]
                               for h in range(4)], []), axis=0)
    gq_ = gq_.reshape(N_DEV, 96, 512)
    r_in, r_uq, r_ukv = _exchange_grads_background([gi, gq_, dw_ukv], 4, "exchange_in_grads")
    grad_x, dg_attn = _dx(d_proj, wi, x2, attn_norm_g, d_h1, gi)

    big = {}
    def adam(name, r, transposed=False, after=None, which=None):
        w, m, v = args[name][0], args["m_" + name][0], args["v_" + name][0]
        if transposed:
            outs = _adam_big(r, w.T, m.T, v.T, "adam_" + name, after, which)
            return [o.T[None] for o in outs]
        return [o[None] for o in _adam_big(r, w, m, v, "adam_" + name, after)]
    big["w_gate"] = adam("w_gate", r_gu, True, which=0)
    big["w_up"] = adam("w_up", r_gu, True, after=big["w_gate"][0], which=1)
    big["w_down"] = adam("w_down", r_d, after=big["w_up"][0])
    big["w_out"] = adam("w_out", r_out, after=big["w_down"][0])
    big["w_mem_kv"] = adam("w_mem_kv", r_mkv, after=big["w_out"][0])
    big["w_in"] = adam("w_in", r_in, True, after=big["w_mem_kv"][0])
    big["w_uq"] = adam("w_uq", r_uq, True, after=big["w_in"][0])
    big["w_ukv"] = adam("w_ukv", r_ukv, after=big["w_uq"][0])

    small_g = {
        "attn_norm_g": dg_attn, "swa_q_norm_g": dg_q, "swa_k_norm_g": dg_k,
        "swa_sinks": d_sinks, "mla_cq_norm_g": dg_cq, "mla_ckv_norm_g": dg_ckv, "mla_qn_norm_g": dg_qn,
        "mla_qr_norm_g": dg_qr, "mla_kn_norm_g": dg_kn, "mla_kr_norm_g": dg_kr,
        "mem_norm_g": dg_mem, "mem_q_norm_g": dg_mq, "mem_k_norm_g": dg_mk, "ffn_norm_g": dg_ffn}
    loss11, small_out = _small_allreduce_adam(
        [small_g[n] for n in _SMALL], loss_tile, [args[n] for n in _SMALL],
        [args["m_" + n] for n in _SMALL], [args["v_" + n] for n in _SMALL])
    small = dict(zip(_SMALL, small_out))
    loss = loss11.reshape(())

    order = ["attn_norm_g", "w_in", "swa_q_norm_g", "swa_k_norm_g", "swa_sinks", "mla_cq_norm_g", "mla_ckv_norm_g",
             "w_uq", "w_ukv", "mla_qn_norm_g", "mla_qr_norm_g", "mla_kn_norm_g", "mla_kr_norm_g", "mem_norm_g",
             "w_mem_kv", "mem_q_norm_g", "mem_k_norm_g", "w_out", "ffn_norm_g", "w_gate", "w_up", "w_down"]
    res = {n: (big[n] if n in big else list(small[n])) for n in order}
    outs = [loss, grad_x[None]]
    for kind in range(4):
        outs += [res[n][kind] for n in order]
    return tuple(outs)
```

```python
import jax
import jax.numpy as jnp
from jax import lax
from jax.experimental import pallas as pl
from jax.experimental.pallas import tpu as pltpu
from jax.experimental.pallas import tpu_sc as plsc

MXU = jnp.bfloat16
WIRE = jnp.bfloat16
EPS = 1e-6
NEG_INF = -1e30
LOG2E = 1.4426950408889634
N_DEV = 8
LANES = 128
ROW_TILE = 256
FFN_TILE = 512
ATT_TILE = 1024
SWA_BLOCK = 128
VMEM_LIMIT = 56 * 1024 * 1024

SWA_Q_HEADS, SWA_KV_HEADS, SWA_DIM = 16, 2, 64
MLA_HEADS, MLA_NOPE, MLA_ROPE, MLA_V = 4, 128, 64, 128
MEM_HEADS, MEM_DIM = 4, 128
ROPE_THETA = 10000.0
ADAM_LR, ADAM_B1, ADAM_B2, ADAM_EPS, ADAM_WD, ADAM_STEP = 0.001, 0.9, 0.999, 1e-08, 0.01, 10

C_QA, C_CQ, C_CKV, C_QM, C_KA, C_VA, C_KR, IN_PAD = 0, 1024, 1536, 2048, 2560, 2688, 2816, 2944


def _pcall(body, *, name, out_shape, in_specs, out_specs, grid=(), scratch=(), sem=None, after=None):
    params = pltpu.CompilerParams(dimension_semantics=sem, vmem_limit_bytes=VMEM_LIMIT)
    if after is not None:
        n_in, inner = len(in_specs), body

        def body(*refs):
            inner(*refs[:n_in], *refs[n_in + 1:])

        in_specs = list(in_specs) + [pl.BlockSpec(memory_space=pl.ANY)]
    call = pl.pallas_call(body, name=name, grid=grid, in_specs=in_specs, out_specs=out_specs,
                          out_shape=out_shape, scratch_shapes=list(scratch), compiler_params=params)
    return call if after is None else (lambda *ops: call(*ops, after))


def _sds(shape, dtype):
    return jax.ShapeDtypeStruct(tuple(shape), dtype)


def _dot(a, b):
    return jnp.dot(a.astype(MXU), b.astype(MXU), preferred_element_type=jnp.float32)


def _dot_nt(a, b):
    return lax.dot_general(a.astype(MXU), b.astype(MXU), (((1,), (1,)), ((), ())),
                           preferred_element_type=jnp.float32)


def _dot_tn(a, b):
    return lax.dot_general(a.astype(MXU), b.astype(MXU), (((0,), (0,)), ((), ())),
                           preferred_element_type=jnp.float32)


def _lo_mask(shape):
    return (lax.broadcasted_iota(jnp.int32, shape, len(shape) - 1) % LANES) < 64


def _norm_fwd(x, g, half=False):
    x2 = x * x
    if half:
        lo = _lo_mask(x.shape)
        s_lo = jnp.sum(jnp.where(lo, x2, 0.0), -1, keepdims=True)
        s_hi = jnp.sum(jnp.where(lo, 0.0, x2), -1, keepdims=True)
        r = jnp.where(lo, lax.rsqrt(s_lo / 64.0 + EPS), lax.rsqrt(s_hi / 64.0 + EPS))
    else:
        r = lax.rsqrt(jnp.mean(x2, -1, keepdims=True) + EPS)
    xn = x * r
    return xn * g, xn, r


def _norm_bwd(xn, r, g, dy, half=False):
    t = dy * g
    tx = t * xn
    if half:
        lo = _lo_mask(xn.shape)
        m_lo = jnp.sum(jnp.where(lo, tx, 0.0), -1, keepdims=True) / 64.0
        m_hi = jnp.sum(jnp.where(lo, 0.0, tx), -1, keepdims=True) / 64.0
        m = jnp.where(lo, m_lo, m_hi)
    else:
        m = jnp.mean(tx, -1, keepdims=True)
    dx = r * (t - xn * m)
    dg = jnp.sum(dy * xn, 0, keepdims=True)
    return dx, dg


def _swap32(x):
    lane = lax.broadcasted_iota(jnp.int32, x.shape, 1)
    return jnp.where((lane % 64) < 32, pltpu.roll(x, 96, 1), pltpu.roll(x, 32, 1))


def _rope(x, cos, sin):
    return x * cos + _swap32(x) * sin


def _rope_bwd(d, cos, sin):
    return d * cos + _swap32(d * sin)


def _my_coords():
    return lax.axis_index("x"), lax.axis_index("y"), lax.axis_index("c")


def _dev_index(px, py, pc):
    return 4 * px + 2 * py + pc


_FLIPS = [(0, 0, 1), (0, 1, 0), (0, 1, 1), (1, 0, 0), (1, 0, 1), (1, 1, 0), (1, 1, 1)]


def _flip(coords, f):
    return tuple((1 - v) if b else v for v, b in zip(coords, f))


def _all_gather(shards):
    n = len(shards)

    def body(*refs):
        ins, outs = refs[:n], refs[n:2 * n]
        send_sems, recv_sems, local_sems = refs[2 * n:]
        x, y, c = _my_coords()
        me, sibling = (x, y, c), (x, y, 1 - c)
        chips = [(1 - x, y), (x, 1 - y), (1 - x, 1 - y)]

        def copy(w, k, block, to, src=None):
            dst = outs[w].at[_dev_index(*block)]
            return pltpu.make_async_remote_copy(
                src_ref=dst if src is None else src, dst_ref=dst,
                send_sem=send_sems.at[w, k], recv_sem=recv_sems.at[w, k],
                device_id=to, device_id_type=pl.DeviceIdType.MESH)

        sends, locals_ = [], []
        for w in range(n):
            mine = pltpu.make_async_copy(ins[w], outs[w].at[_dev_index(*me)], local_sems.at[w])
            mine.start()
            locals_.append(mine)
            first = [copy(w, 0, me, sibling, src=ins[w])]
            first += [copy(w, 1 + j, me, (*chip, c), src=ins[w]) for j, chip in enumerate(chips)]
            for cp in first:
                cp.start()
            sends += first
        for w in range(n):
            for j, chip in enumerate(chips):
                copy(w, 1 + j, (*chip, c), me).wait_recv()
                fwd = copy(w, 4 + j, (*chip, c), sibling)
                fwd.start()
                sends.append(fwd)
        for w in range(n):
            copy(w, 0, sibling, me).wait_recv()
            for j, chip in enumerate(chips):
                copy(w, 4 + j, (*chip, 1 - c), me).wait_recv()
        for cp in sends:
            cp.wait_send()
        for mine in locals_:
            mine.wait()

    any_spec = pl.BlockSpec(memory_space=pl.ANY)
    return _pcall(
        body, name="all_gather_weights",
        out_shape=[_sds((N_DEV,) + s.shape, s.dtype) for s in shards],
        in_specs=[any_spec] * n, out_specs=[any_spec] * n,
        scratch=[pltpu.SemaphoreType.DMA((n, 7)), pltpu.SemaphoreType.DMA((n, 7)),
                 pltpu.SemaphoreType.DMA((n,))])(*shards)


def _wire_cost(arrays):
    nbytes = sum(a.size * a.dtype.itemsize for a in arrays)
    return pl.CostEstimate(flops=0, transcendentals=0, bytes_accessed=40 * nbytes)


def _all_gather_background(shards, collective_id, name):
    n = len(shards)
    src_refs = [jax.new_ref(s, memory_space=pltpu.MemorySpace.HBM) for s in shards]
    out_refs = [jax.empty_ref(_sds((N_DEV,) + s.shape, s.dtype), memory_space=pltpu.MemorySpace.HBM) for s in shards]

    @pl.kernel(mesh=plsc.ScalarSubcoreMesh(axis_name="seq", num_cores=1), name=name,
               scratch_types=(pltpu.SemaphoreType.DMA((n, 7)), pltpu.SemaphoreType.DMA((n, 7)),
                              pltpu.SemaphoreType.DMA((n,))),
               compiler_params=pltpu.CompilerParams(collective_id=collective_id))
    def launch(send_sems, recv_sems, local_sems):
        x, y, c = _my_coords()
        me, sibling = (x, y, c), (x, y, 1 - c)
        chips = [(1 - x, y), (x, 1 - y), (1 - x, 1 - y)]
        barrier = pltpu.get_barrier_semaphore()
        for peer in [sibling] + [(*chip, c) for chip in chips]:
            pl.semaphore_signal(barrier, inc=1, device_id=peer, device_id_type=pl.DeviceIdType.MESH)
        pl.semaphore_wait(barrier, 4)

        def copy(w, k, block, to, src=None):
            dst = out_refs[w].at[_dev_index(*block)]
            return pltpu.make_async_remote_copy(
                src_ref=dst if src is None else src, dst_ref=dst,
                send_sem=send_sems.at[w, k], recv_sem=recv_sems.at[w, k],
                device_id=to, device_id_type=pl.DeviceIdType.MESH)

        sends, locals_ = [], []
        for w in range(n):
            mine = pltpu.make_async_copy(src_refs[w], out_refs[w].at[_dev_index(*me)], local_sems.at[w])
            mine.start()
            locals_.append(mine)
            first = [copy(w, 0, me, sibling, src=src_refs[w])]
            first += [copy(w, 1 + j, me, (*chip, c), src=src_refs[w]) for j, chip in enumerate(chips)]
            for cp in first:
                cp.start()
            sends += first
        for w in range(n):
            for j, chip in enumerate(chips):
                copy(w, 1 + j, (*chip, c), me).wait_recv()
                fwd = copy(w, 4 + j, (*chip, c), sibling)
                fwd.start()
                sends.append(fwd)
        for w in range(n):
            copy(w, 0, sibling, me).wait_recv()
            for j, chip in enumerate(chips):
                copy(w, 4 + j, (*chip, 1 - c), me).wait_recv()
        for cp in sends:
            cp.wait_send()
        for mine in locals_:
            mine.wait()

    launch()
    return [r[...] for r in out_refs]


def _exchange_grads(grads):
    n = len(grads)

    def body(*refs):
        ins, outs = refs[:n], refs[n:2 * n]
        send_sems, recv_sems, local_sems = refs[2 * n:]
        me = _my_coords()
        my_idx = _dev_index(*me)
        sends, locals_ = [], []
        for w in range(n):
            mine = pltpu.make_async_copy(ins[w].at[my_idx], outs[w].at[my_idx], local_sems.at[w])
            mine.start()
            locals_.append(mine)
            for k, f in enumerate(_FLIPS):
                peer = _flip(me, f)
                cp = pltpu.make_async_remote_copy(
                    src_ref=ins[w].at[_dev_index(*peer)], dst_ref=outs[w].at[my_idx],
                    send_sem=send_sems.at[w, k], recv_sem=recv_sems.at[w, k],
                    device_id=peer, device_id_type=pl.DeviceIdType.MESH)
                cp.start()
                sends.append(cp)
        for w in range(n):
            for k, f in enumerate(_FLIPS):
                peer = _flip(me, f)
                slot = outs[w].at[_dev_index(*peer)]
                pltpu.make_async_remote_copy(
                    src_ref=slot, dst_ref=slot,
                    send_sem=send_sems.at[w, k], recv_sem=recv_sems.at[w, k],
                    device_id=peer, device_id_type=pl.DeviceIdType.MESH).wait_recv()
        for cp in sends:
            cp.wait_send()
        for mine in locals_:
            mine.wait()

    any_spec = pl.BlockSpec(memory_space=pl.ANY)
    return _pcall(
        body, name="exchange_grads",
        out_shape=[_sds(g.shape, g.dtype) for g in grads],
        in_specs=[any_spec] * n, out_specs=[any_spec] * n,
        scratch=[pltpu.SemaphoreType.DMA((n, 7)), pltpu.SemaphoreType.DMA((n, 7)),
                 pltpu.SemaphoreType.DMA((n,))])(*grads)


def _exchange_grads_background(grads, collective_id, name):
    n = len(grads)
    src_refs = [jax.new_ref(g, memory_space=pltpu.MemorySpace.HBM) for g in grads]
    out_refs = [jax.empty_ref(_sds(g.shape, g.dtype), memory_space=pltpu.MemorySpace.HBM) for g in grads]

    @pl.kernel(mesh=plsc.ScalarSubcoreMesh(axis_name="seq", num_cores=1), name=name,
               scratch_types=(pltpu.SemaphoreType.DMA((n, 7)), pltpu.SemaphoreType.DMA((n, 7)),
                              pltpu.SemaphoreType.DMA((n,))),
               cost_estimate=_wire_cost(grads),
               compiler_params=pltpu.CompilerParams(collective_id=collective_id))
    def launch(send_sems, recv_sems, local_sems):
        me = _my_coords()
        my_idx = _dev_index(*me)
        peers = [_flip(me, f) for f in _FLIPS]
        barrier = pltpu.get_barrier_semaphore()
        for peer in peers:
            pl.semaphore_signal(barrier, inc=1, device_id=peer, device_id_type=pl.DeviceIdType.MESH)
        pl.semaphore_wait(barrier, len(peers))
        sends, locals_ = [], []
        for w in range(n):
            mine = pltpu.make_async_copy(src_refs[w].at[my_idx], out_refs[w].at[my_idx], local_sems.at[w])
            mine.start()
            locals_.append(mine)
            for k, peer in enumerate(peers):
                cp = pltpu.make_async_remote_copy(
                    src_ref=src_refs[w].at[_dev_index(*peer)], dst_ref=out_refs[w].at[my_idx],
                    send_sem=send_sems.at[w, k], recv_sem=recv_sems.at[w, k],
                    device_id=peer, device_id_type=pl.DeviceIdType.MESH)
                cp.start()
                sends.append(cp)
        for w in range(n):
            for k, peer in enumerate(peers):
                slot = out_refs[w].at[_dev_index(*peer)]
                pltpu.make_async_remote_copy(
                    src_ref=slot, dst_ref=slot, send_sem=send_sems.at[w, k], recv_sem=recv_sems.at[w, k],
                    device_id=peer, device_id_type=pl.DeviceIdType.MESH).wait_recv()
        for cp in sends:
            cp.wait_send()
        for mine in locals_:
            mine.wait()

    launch()
    return [r[...] for r in out_refs]


def _to_wire(parts, after, name):
    n = len(parts)
    rows, cols = parts[0].shape
    tr = rows // 2 if rows % 32 == 0 else rows

    def body(*refs):
        for k in range(n):
            refs[n][k] = refs[k][...].astype(WIRE)

    blk = pl.BlockSpec((tr, cols), lambda i: (i, 0))
    return _pcall(
        body, name=name, grid=(rows // tr,), out_shape=_sds((n, rows, cols), WIRE),
        in_specs=[blk] * n, out_specs=pl.BlockSpec((n, tr, cols), lambda i: (0, i, 0)),
        sem=("parallel",), after=after)(*parts)


def _adam_math(w, g, m, v):
    m = ADAM_B1 * m + (1.0 - ADAM_B1) * g
    v = ADAM_B2 * v + (1.0 - ADAM_B2) * (g * g)
    m_hat = m / (1.0 - ADAM_B1 ** ADAM_STEP)
    v_hat = v / (1.0 - ADAM_B2 ** ADAM_STEP)
    delta = -ADAM_LR * (m_hat / (jnp.sqrt(v_hat) + ADAM_EPS) + ADAM_WD * w)
    return delta, m, v


def _small_allreduce_adam(grads, loss_tile, ws, ms, vs):
    sizes = [w.shape[-1] for w in ws]
    n_par = len(ws)
    row0, r = [], 0
    for n in sizes:
        row0.append(r)
        r += -(-n // LANES)
    loss_row = r
    rows = -(-(r + 1) // 8) * 8

    def pieces(n):
        return [(k, min(LANES, n - LANES * k)) for k in range(-(-n // LANES))]

    def body(*refs):
        g_refs = refs[:n_par]
        loss_in = refs[n_par]
        w_refs = refs[n_par + 1: 2 * n_par + 1]
        m_refs = refs[2 * n_par + 1: 3 * n_par + 1]
        v_refs = refs[3 * n_par + 1: 4 * n_par + 1]
        loss_out = refs[4 * n_par + 1]
        out_refs = refs[4 * n_par + 2: 8 * n_par + 2]
        pack, gath, res, send_sems, recv_sems = refs[8 * n_par + 2:]
        me = _my_coords()
        my_idx = _dev_index(*me)

        def fill(slot, srcs):
            pack[slot] = jnp.zeros((rows, LANES), jnp.float32)
            for p, n in enumerate(sizes):
                val = srcs[p][...]
                if val.shape[-1] == LANES and n == 64:
                    pack[slot, row0[p]:row0[p] + 1, :] = val + pltpu.roll(val, 64, 1)
                    continue
                for k, width in pieces(n):
                    pack[slot, row0[p] + k:row0[p] + k + 1, 0:width] = srcs[p][:, LANES * k:LANES * k + width]

        fill(0, g_refs)
        pack[0, loss_row:loss_row + 1, :] = loss_in[0:1, :]
        gath[my_idx] = pack[0]
        sends = []
        for k, f in enumerate(_FLIPS):
            peer = _flip(me, f)
            cp = pltpu.make_async_remote_copy(
                src_ref=pack.at[0], dst_ref=gath.at[my_idx],
                send_sem=send_sems.at[k], recv_sem=recv_sems.at[k],
                device_id=peer, device_id_type=pl.DeviceIdType.MESH)
            cp.start()
            sends.append(cp)
        fill(1, w_refs)
        fill(2, m_refs)
        fill(3, v_refs)
        for k, f in enumerate(_FLIPS):
            peer = _flip(me, f)
            slot = gath.at[_dev_index(*peer)]
            pltpu.make_async_remote_copy(
                src_ref=slot, dst_ref=slot, send_sem=send_sems.at[k], recv_sem=recv_sems.at[k],
                device_id=peer, device_id_type=pl.DeviceIdType.MESH).wait_recv()
        for cp in sends:
            cp.wait_send()
        g = gath[0]
        for d in range(1, N_DEV):
            g = g + gath[d]
        delta, m, v = _adam_math(pack[1], g, pack[2], pack[3])
        res[0], res[1], res[2], res[3] = g, delta, m, v
        loss_out[...] = res[0, loss_row:loss_row + 1, 0:1]
        for p, n in enumerate(sizes):
            for kind in range(4):
                for k, width in pieces(n):
                    out_refs[4 * p + kind][:, LANES * k:LANES * k + width] = (
                        res[kind, row0[p] + k:row0[p] + k + 1, 0:width])

    vm = pl.BlockSpec(memory_space=pltpu.VMEM)
    out_shape = [_sds((1, 1), jnp.float32)]
    for n in sizes:
        out_shape += [_sds((1, n), jnp.float32)] * 4
    outs = _pcall(
        body, name="small_allreduce_adam", out_shape=out_shape,
        in_specs=[vm] * (4 * n_par + 1), out_specs=[vm] * len(out_shape),
        scratch=[pltpu.VMEM((4, rows, LANES), jnp.float32), pltpu.VMEM((N_DEV, rows, LANES), jnp.float32),
                 pltpu.VMEM((4, rows, LANES), jnp.float32),
                 pltpu.SemaphoreType.DMA((7,)), pltpu.SemaphoreType.DMA((7,))])(*grads, loss_tile, *ws, *ms, *vs)
    return outs[0], [outs[1 + 4 * p: 5 + 4 * p] for p in range(n_par)]


def _adam_big(recv, w, m, v, name, after=None, which=None):
    rows, cols = recv.shape[-2:]
    row_tiles = [t for t in range(16, rows + 1, 16) if rows % t == 0 and t * cols <= 400 * 1024]
    tr, tc = (max(row_tiles), cols) if row_tiles else (rows, 512 if cols % 512 == 0 else cols)

    def body(r_ref, w_ref, m_ref, v_ref, g_ref, d_ref, mo_ref, vo_ref):
        g = r_ref[0].astype(jnp.float32)
        for d in range(1, N_DEV):
            g = g + r_ref[d].astype(jnp.float32)
        delta, mn, vn = _adam_math(w_ref[...], g, m_ref[...], v_ref[...])
        g_ref[...] = g
        d_ref[...] = delta
        mo_ref[...] = mn
        vo_ref[...] = vn

    blk = pl.BlockSpec((tr, tc), lambda i, j: (i, j))
    if which is None:
        r_spec = pl.BlockSpec((N_DEV, tr, tc), lambda i, j: (0, i, j))
    else:
        r_spec = pl.BlockSpec((N_DEV, None, tr, tc), lambda i, j: (0, which, i, j))
    return _pcall(
        body, name=name, grid=(rows // tr, cols // tc),
        out_shape=[_sds((rows, cols), jnp.float32)] * 4,
        in_specs=[r_spec, blk, blk, blk],
        out_specs=[blk] * 4, sem=("parallel", "parallel"), after=after)(recv, w, m, v)


def _mm(a, b, *, ta=False, tb=False, out_dtype, tm, tk, name):
    (kdim, mdim) = a.shape if ta else a.shape[::-1]
    ndim = b.shape[0] if tb else b.shape[1]
    tm, tk = min(tm, mdim), min(tk, kdim)
    nk = kdim // tk

    def body(a_ref, b_ref, o_ref, acc):
        k = pl.program_id(1)
        if ta:
            part = _dot_tn(a_ref[...], b_ref[...])
        elif tb:
            part = _dot_nt(a_ref[...], b_ref[...])
        else:
            part = _dot(a_ref[...], b_ref[...])

        @pl.when(k == 0)
        def _():
            acc[...] = part

        @pl.when(k > 0)
        def _():
            acc[...] += part

        @pl.when(k == nk - 1)
        def _():
            o_ref[...] = acc[...].astype(o_ref.dtype)

    a_spec = pl.BlockSpec((tk, tm), lambda i, k: (k, i)) if ta else pl.BlockSpec((tm, tk), lambda i, k: (i, k))
    b_spec = pl.BlockSpec((ndim, tk), lambda i, k: (0, k)) if tb else pl.BlockSpec((tk, ndim), lambda i, k: (k, 0))
    return _pcall(
        body, name=name, grid=(mdim // tm, nk), out_shape=_sds((mdim, ndim), out_dtype),
        in_specs=[a_spec, b_spec], out_specs=pl.BlockSpec((tm, ndim), lambda i, k: (i, 0)),
        scratch=[pltpu.VMEM((tm, ndim), jnp.float32)], sem=("parallel", "arbitrary"))(a, b)


def _ref_col_pieces(start, stop):
    ref_starts = [0, 1024, 1152, 1280, 1792, 2304, 2368, 2880]
    perm_starts = [C_QA, C_KA, C_VA, C_CQ, C_CKV, C_KR, C_QM]
    out = []
    for p in range(7):
        lo, hi = max(start, ref_starts[p]), min(stop, ref_starts[p + 1])
        if lo < hi:
            out.append((lo - start, perm_starts[p] + lo - ref_starts[p], hi - lo))
    return out


def _dw_in(hn, d_proj, n_shard):
    s, d = hn.shape
    n = d_proj.shape[1]
    tm, tk = min(512, d), min(1024, s)
    nk = s // tk

    def body(a_ref, b_ref, o_ref, acc):
        k = pl.program_id(1)
        part = _dot_tn(a_ref[...], b_ref[...])

        @pl.when(k == 0)
        def _():
            acc[...] = part

        @pl.when(k > 0)
        def _():
            acc[...] += part

        @pl.when(k == nk - 1)
        def _():
            t = acc[...].T
            for j in range(N_DEV):
                rows = [t[src:src + width] for _, src, width in _ref_col_pieces(j * n_shard, (j + 1) * n_shard)]
                o_ref[j] = jnp.concatenate(rows, axis=0).astype(o_ref.dtype)

    return _pcall(
        body, name="dw_in", grid=(d // tm, nk), out_shape=_sds((N_DEV, n_shard, d), WIRE),
        in_specs=[pl.BlockSpec((tk, tm), lambda i, k: (k, i)), pl.BlockSpec((tk, n), lambda i, k: (k, 0))],
        out_specs=pl.BlockSpec((N_DEV, n_shard, tm), lambda i, k: (0, 0, i)),
        scratch=[pltpu.VMEM((tm, n), jnp.float32)], sem=("parallel", "arbitrary"))(hn, d_proj)


def _in_proj(x, g, w):
    s, d = x.shape
    n = w.shape[0]
    tm = min(ROW_TILE, s)

    def body(x_ref, g_ref, w_ref, p_ref, hn_ref):
        hn, _, _ = _norm_fwd(x_ref[...], g_ref[...])
        hn_ref[...] = hn.astype(hn_ref.dtype)
        p_ref[...] = _dot_nt(hn, w_ref[...])

    return _pcall(
        body, name="in_proj", grid=(s // tm,),
        out_shape=[_sds((s, n), jnp.float32), _sds((s, d), MXU)],
        in_specs=[pl.BlockSpec((tm, d), lambda i: (i, 0)), pl.BlockSpec((1, d), lambda i: (0, 0)),
                  pl.BlockSpec((n, d), lambda i: (0, 0))],
        out_specs=[pl.BlockSpec((tm, n), lambda i: (i, 0)), pl.BlockSpec((tm, d), lambda i: (i, 0))],
        sem=("parallel",))(x, g, w)


def _mla_prep(proj, cos, sin, g_cq, g_ckv, w_uq, w_ukv, g_qn, g_qr, g_kn, g_kr):
    s = proj.shape[0]
    tm = min(ROW_TILE, s)
    nh = MLA_HEADS

    def body(cq_ref, ckv_ref, kr_ref, cos_ref, sin_ref, gcq_ref, gckv_ref, wuq_ref, wukv_ref,
             gqn_ref, gqr_ref, gkn_ref, gkr_ref,
             qc_ref, kc_ref, v_ref, qb_ref, kvb_ref, cqn_ref, ckvn_ref):
        cos_t, sin_t = cos_ref[...], sin_ref[...]
        lo = _lo_mask((tm, LANES))
        cqn, _, _ = _norm_fwd(cq_ref[...], gcq_ref[...])
        cqn_ref[...] = cqn.astype(cqn_ref.dtype)
        qb = _dot_nt(cqn, wuq_ref[...])
        qb_ref[...] = qb
        ckvn, _, _ = _norm_fwd(ckv_ref[...], gckv_ref[...])
        ckvn_ref[...] = ckvn.astype(ckvn_ref.dtype)
        kvb = jnp.concatenate([_dot(ckvn, wukv_ref[dev]) for dev in range(N_DEV)], axis=1)
        kvb_ref[...] = kvb
        kr, _, _ = _norm_fwd(kr_ref[...], gkr_ref[...], half=True)
        kr = _rope(kr, cos_t, sin_t)
        kr2 = jnp.where(lo, kr, pltpu.roll(kr, 64, 1))
        ropes = []
        for j in range(nh // 2):
            xr = qb[:, nh * MLA_NOPE + LANES * j: nh * MLA_NOPE + LANES * (j + 1)]
            qr, _, _ = _norm_fwd(xr, gqr_ref[...], half=True)
            ropes.append(_rope(qr, cos_t, sin_t))
        for h in range(nh):
            qn, _, _ = _norm_fwd(qb[:, MLA_NOPE * h: MLA_NOPE * (h + 1)], gqn_ref[...])
            mask = lo if h % 2 == 0 else jnp.logical_not(lo)
            qr = jnp.where(mask, ropes[h // 2], 0.0)
            qc_ref[h] = jnp.concatenate([qn, qr], axis=1).astype(qc_ref.dtype)
            kn, _, _ = _norm_fwd(kvb[:, 256 * h: 256 * h + MLA_NOPE], gkn_ref[...])
            kc_ref[h] = jnp.concatenate([kn, kr2], axis=1).astype(kc_ref.dtype)
            v_ref[h] = kvb[:, 256 * h + MLA_NOPE: 256 * (h + 1)].astype(v_ref.dtype)

    def col(width, start):
        return pl.BlockSpec((tm, width), lambda i: (i, start // width))

    def full(shape):
        return pl.BlockSpec(shape, lambda i: (0,) * len(shape))

    def row(width):
        return pl.BlockSpec((tm, width), lambda i: (i, 0))

    def heads(width):
        return pl.BlockSpec((nh, tm, width), lambda i: (0, i, 0))

    return _pcall(
        body, name="mla_prep", grid=(s // tm,),
        out_shape=[_sds((nh, s, 256), MXU), _sds((nh, s, 256), MXU), _sds((nh, s, MLA_V), MXU),
                   _sds((s, 768), jnp.float32), _sds((s, 1024), jnp.float32),
                   _sds((s, 512), MXU), _sds((s, 512), MXU)],
        in_specs=[col(512, C_CQ), col(512, C_CKV), col(LANES, C_KR), row(LANES), row(LANES),
                  full((1, 512)), full((1, 512)), full((768, 512)), full((N_DEV, 512, LANES)),
                  full((1, LANES)), full((1, LANES)), full((1, LANES)), full((1, LANES))],
        out_specs=[heads(256), heads(256), heads(MLA_V), row(768), row(1024), row(512), row(512)],
        sem=("parallel",))(proj, proj, proj, cos, sin, g_cq, g_ckv, w_uq, w_ukv, g_qn, g_qr, g_kn, g_kr)


def _mla_fwd(qc, kc, v):
    nh, s, _ = qc.shape
    t = min(ATT_TILE, s)
    nb = s // t
    scale = (MLA_NOPE + MLA_ROPE) ** -0.5

    def body(q_ref, k_ref, v_ref, y_ref, lse_ref, m_sc, l_sc, acc):
        qi, ki = pl.program_id(1), pl.program_id(2)

        @pl.when(ki == 0)
        def _():
            m_sc[...] = jnp.full_like(m_sc, NEG_INF)
            l_sc[...] = jnp.zeros_like(l_sc)
            acc[...] = jnp.zeros_like(acc)

        def step(diagonal):
            rc = t // 4 if diagonal else t
            for c in range(t // rc):
                rows = slice(rc * c, rc * (c + 1))
                keys = slice(0, rc * (c + 1))
                sc = _dot_nt(q_ref[0, rows, :], k_ref[0, keys, :]) * (scale * LOG2E)
                if diagonal:
                    r_i = lax.broadcasted_iota(jnp.int32, sc.shape, 0) + rc * c
                    c_i = lax.broadcasted_iota(jnp.int32, sc.shape, 1)
                    sc = jnp.where(c_i <= r_i, sc, NEG_INF)
                m_old = m_sc[rows, :]
                m_new = jnp.maximum(m_old, jnp.max(sc, -1, keepdims=True))
                alpha = jnp.exp2(m_old - m_new)
                p = jnp.exp2(sc - m_new)
                l_sc[rows, :] = alpha * l_sc[rows, :] + jnp.sum(p, -1, keepdims=True)
                acc[rows, :] = alpha * acc[rows, :] + _dot(p, v_ref[0, keys, :])
                m_sc[rows, :] = m_new

        @pl.when(ki < qi)
        def _():
            step(False)

        @pl.when(ki == qi)
        def _():
            step(True)

        @pl.when(ki == qi)
        def _():
            y_ref[...] = acc[...] / l_sc[...]
            lse_ref[0] = m_sc[...] + jnp.log2(l_sc[...])

    return _pcall(
        body, name="mla_fwd", grid=(nh, nb, nb),
        out_shape=[_sds((s, nh * MLA_V), jnp.float32), _sds((nh, s, 1), jnp.float32)],
        in_specs=[pl.BlockSpec((1, t, 256), lambda h, i, k: (h, i, 0)),
                  pl.BlockSpec((1, t, 256), lambda h, i, k: (h, jnp.minimum(k, i), 0)),
                  pl.BlockSpec((1, t, MLA_V), lambda h, i, k: (h, jnp.minimum(k, i), 0))],
        out_specs=[pl.BlockSpec((t, MLA_V), lambda h, i, k: (i, h)),
                   pl.BlockSpec((1, t, 1), lambda h, i, k: (h, i, 0))],
        scratch=[pltpu.VMEM((t, 1), jnp.float32), pltpu.VMEM((t, 1), jnp.float32),
                 pltpu.VMEM((t, MLA_V), jnp.float32)],
        sem=("parallel", "parallel", "arbitrary"))(qc, kc, v)


def _memkv_prep(mem, g_mem, w_mkv, g_mk):
    ml, d = mem.shape
    hw = MEM_HEADS * MEM_DIM

    def body(mem_ref, g_ref, w_ref, gk_ref, k_ref, v_ref, kv_ref, mn_ref):
        mn, _, _ = _norm_fwd(mem_ref[...], g_ref[...])
        mn_ref[...] = mn.astype(mn_ref.dtype)
        kv = _dot(mn, w_ref[...])
        kv_ref[...] = kv
        for h in range(MEM_HEADS):
            kn, _, _ = _norm_fwd(kv[:, MEM_DIM * h: MEM_DIM * (h + 1)], gk_ref[...])
            k_ref[:, MEM_DIM * h: MEM_DIM * (h + 1)] = kn.astype(k_ref.dtype)
        v_ref[...] = kv[:, hw:].astype(v_ref.dtype)

    vm = pl.BlockSpec(memory_space=pltpu.VMEM)
    return _pcall(
        body, name="memkv_prep",
        out_shape=[_sds((ml, hw), MXU), _sds((ml, hw), MXU), _sds((ml, 2 * hw), jnp.float32), _sds((ml, d), MXU)],
        in_specs=[vm] * 4, out_specs=[vm] * 4)(mem, g_mem, w_mkv, g_mk)


def _mem_fwd(proj, g_mq, km, vmm):
    s = proj.shape[0]
    ml, hw = km.shape
    tm = min(FFN_TILE, s)
    scale = MEM_DIM ** -0.5

    def body(q_ref, g_ref, k_ref, v_ref, y_ref, lse_ref):
        col = lax.broadcasted_iota(jnp.int32, (tm, MEM_HEADS), 1)
        lse_t = jnp.zeros((tm, MEM_HEADS), jnp.float32)
        for h in range(MEM_HEADS):
            sl = slice(MEM_DIM * h, MEM_DIM * (h + 1))
            qn, _, _ = _norm_fwd(q_ref[:, sl], g_ref[...])
            sc = _dot_nt(qn, k_ref[:, sl]) * scale
            m = jnp.max(sc, -1, keepdims=True)
            p = jnp.exp(sc - m)
            l = jnp.sum(p, -1, keepdims=True)
            y_ref[:, sl] = _dot(p, v_ref[:, sl]) / l
            lse_t = jnp.where(col == h, m + jnp.log(l), lse_t)
        lse_ref[...] = lse_t

    return _pcall(
        body, name="mem_fwd", grid=(s // tm,),
        out_shape=[_sds((s, hw), jnp.float32), _sds((s, MEM_HEADS), jnp.float32)],
        in_specs=[pl.BlockSpec((tm, hw), lambda i: (i, C_QM // hw)), pl.BlockSpec((1, MEM_DIM), lambda i: (0, 0)),
                  pl.BlockSpec((ml, hw), lambda i: (0, 0)), pl.BlockSpec((ml, hw), lambda i: (0, 0))],
        out_specs=[pl.BlockSpec((tm, hw), lambda i: (i, 0)), pl.BlockSpec((tm, MEM_HEADS), lambda i: (i, 0))],
        sem=("parallel",))(proj, g_mq, km, vmm)


def _alibi_slope(h):
    return float(2.0 ** (-8.0 * (h + 1) / SWA_Q_HEADS))


def _swa_common(n, kp, kc, vp, vc, pq, pkp, pkc, gk):
    b = SWA_BLOCK
    k_raw = jnp.concatenate([kp, kc], axis=0)
    kn, kxn, kr = _norm_fwd(k_raw, gk, half=True)
    v = jnp.concatenate([vp, vc], axis=0)
    dist = jnp.abs(pq - jnp.concatenate([pkp, pkc], axis=1))
    r_i = lax.broadcasted_iota(jnp.int32, (b, 2 * b), 0)
    c_i = lax.broadcasted_iota(jnp.int32, (b, 2 * b), 1)
    valid = (c_i > r_i) & (c_i <= r_i + b) & (c_i >= jnp.where(n > 0, 0, b))
    bias = jnp.where(valid, -dist, NEG_INF)
    return kn, v, bias


def _swa_specs(s):
    b = SWA_BLOCK
    prev = lambda n: jnp.maximum(n - 1, 0)
    return [
        pl.BlockSpec((b, 1024), lambda n: (n, C_QA // 1024)),
        pl.BlockSpec((b, LANES), lambda n: (prev(n), C_KA // LANES)),
        pl.BlockSpec((b, LANES), lambda n: (n, C_KA // LANES)),
        pl.BlockSpec((b, LANES), lambda n: (prev(n), C_VA // LANES)),
        pl.BlockSpec((b, LANES), lambda n: (n, C_VA // LANES)),
        pl.BlockSpec((b, 1), lambda n: (n, 0)),
        pl.BlockSpec((1, b), lambda n: (0, prev(n))),
        pl.BlockSpec((1, b), lambda n: (0, n)),
        pl.BlockSpec((1, LANES), lambda n: (0, 0)),
        pl.BlockSpec((1, LANES), lambda n: (0, 0)),
        pl.BlockSpec(memory_space=pltpu.SMEM),
    ]


def _swa_fwd(proj, posc, posr, gq, gk, sinks):
    s = proj.shape[0]
    b = SWA_BLOCK
    scale = SWA_DIM ** -0.5

    def body(q_ref, kp_ref, kc_ref, vp_ref, vc_ref, pq_ref, pkp_ref, pkc_ref, gq_ref, gk_ref, sink_ref,
             y_ref, lse_ref):
        n = pl.program_id(0)
        kn, v, bias = _swa_common(n, kp_ref[...], kc_ref[...], vp_ref[...], vc_ref[...],
                                  pq_ref[...], pkp_ref[...], pkc_ref[...], gk_ref[...])
        lo = _lo_mask((b, LANES))
        col = lax.broadcasted_iota(jnp.int32, (b, SWA_Q_HEADS), 1)
        lse_t = jnp.zeros((b, SWA_Q_HEADS), jnp.float32)
        hpg = SWA_Q_HEADS // SWA_KV_HEADS
        for g in range(SWA_KV_HEADS):
            heads = range(hpg * g, hpg * (g + 1))
            kvmask = lo if g == 0 else jnp.logical_not(lo)
            qs = []
            for j in range(hpg // 2 * g, hpg // 2 * (g + 1)):
                qn, _, _ = _norm_fwd(q_ref[:, LANES * j: LANES * (j + 1)], gq_ref[...], half=True)
                qn = qn * scale
                qsw = pltpu.roll(qn, 64, 1)
                qs += [jnp.where(kvmask, qn if e == g else qsw, 0.0) for e in range(2)]
            sc_st = _dot_nt(jnp.concatenate(qs, axis=0), kn)
            ps, ls = [], []
            for i, h in enumerate(heads):
                sc = sc_st[b * i: b * (i + 1)] + _alibi_slope(h) * bias
                sk = sink_ref[h]
                m = jnp.maximum(jnp.max(sc, -1, keepdims=True), sk)
                p = jnp.exp(sc - m)
                l = jnp.sum(p, -1, keepdims=True) + jnp.exp(sk - m)
                ps.append(p.astype(MXU))
                ls.append(l)
                lse_t = jnp.where(col == h, m + jnp.log(l), lse_t)
            o_st = _dot(jnp.concatenate(ps, axis=0), v)
            for j in range(hpg // 2 * g, hpg // 2 * (g + 1)):
                halves = []
                for e in range(2):
                    i = 2 * j + e - hpg * g
                    o_h = o_st[b * i: b * (i + 1)] / ls[i]
                    halves.append(o_h if e == g else pltpu.roll(o_h, 64, 1))
                y_ref[:, LANES * j: LANES * (j + 1)] = jnp.where(lo, halves[0], halves[1])
        lse_ref[...] = lse_t

    return _pcall(
        body, name="swa_fwd", grid=(s // b,),
        out_shape=[_sds((s, 1024), jnp.float32), _sds((s, SWA_Q_HEADS), jnp.float32)],
        in_specs=_swa_specs(s),
        out_specs=[pl.BlockSpec((b, 1024), lambda n: (n, 0)), pl.BlockSpec((b, SWA_Q_HEADS), lambda n: (n, 0))],
        sem=("parallel",))(proj, proj, proj, proj, proj, posc, posr, posr, gq, gk, sinks)


def _out_proj(y_a, y_b, y_m, x, w_out, g_ffn):
    s, d = x.shape
    tm = min(ROW_TILE, s)

    def body(ya_ref, yb_ref, ym_ref, x_ref, w_ref, g_ref, h1_ref, fn_ref):
        y = jnp.concatenate([ya_ref[...].astype(MXU), yb_ref[...].astype(MXU), ym_ref[...].astype(MXU)], axis=1)
        h1 = x_ref[...] + _dot(y, w_ref[...])
        h1_ref[...] = h1
        fn, _, _ = _norm_fwd(h1, g_ref[...])
        fn_ref[...] = fn.astype(fn_ref.dtype)

    def row(width):
        return pl.BlockSpec((tm, width), lambda i: (i, 0))

    return _pcall(
        body, name="out_proj", grid=(s // tm,),
        out_shape=[_sds((s, d), jnp.float32), _sds((s, d), MXU)],
        in_specs=[row(1024), row(512), row(512), row(d), pl.BlockSpec(w_out.shape, lambda i: (0, 0)),
                  pl.BlockSpec((1, d), lambda i: (0, 0))],
        out_specs=[row(d), row(d)], sem=("parallel",))(y_a, y_b, y_m, x, w_out, g_ffn)


def _ffn_gu(fn, w_gu):
    s, d = fn.shape
    f = w_gu.shape[2]
    tm = min(2 * FFN_TILE, s)

    def body(fn_ref, w_ref, gu_ref, act_ref):
        x = fn_ref[...]
        g = _dot_nt(x, w_ref[0, 0])
        u = _dot_nt(x, w_ref[0, 1])
        gu_ref[0, 0] = g
        gu_ref[0, 1] = u
        act_ref[0] = (g * jax.nn.sigmoid(g) * u).astype(act_ref.dtype)

    return _pcall(
        body, name="ffn_gate_up", grid=(N_DEV, s // tm),
        out_shape=[_sds((N_DEV, 2, s, f), jnp.float32), _sds((N_DEV, s, f), MXU)],
        in_specs=[pl.BlockSpec((tm, d), lambda j, i: (i, 0)),
                  pl.BlockSpec((1, 2, f, d), lambda j, i: (j, 0, 0, 0))],
        out_specs=[pl.BlockSpec((1, 2, tm, f), lambda j, i: (j, 0, i, 0)),
                   pl.BlockSpec((1, tm, f), lambda j, i: (j, i, 0))],
        sem=("parallel", "parallel"))(fn, w_gu)


def _ffn_down(act, w_d, h1, target):
    _, s, f = act.shape
    d = h1.shape[1]
    tm = min(FFN_TILE, s)

    def body(a_ref, w_ref, h1_ref, t_ref, dout_ref, doutb_ref, loss_ref, acc):
        i, j = pl.program_id(0), pl.program_id(1)
        part = _dot(a_ref[0], w_ref[0]) + _dot(a_ref[1], w_ref[1])

        @pl.when(j == 0)
        def _():
            acc[...] = h1_ref[...] + part

        @pl.when(j > 0)
        def _():
            acc[...] += part

        @pl.when((i == 0) & (j == 0))
        def _():
            loss_ref[...] = jnp.zeros_like(loss_ref)

        @pl.when(j == N_DEV // 2 - 1)
        def _():
            diff = acc[...] - t_ref[...]
            dout_ref[...] = diff / d
            doutb_ref[...] = (diff / d).astype(doutb_ref.dtype)
            loss_ref[...] += 0.5 * jnp.sum(jnp.sum(diff * diff, -1, keepdims=True) / d)

    row = pl.BlockSpec((tm, d), lambda i, j: (i, 0))
    return _pcall(
        body, name="ffn_down", grid=(s // tm, N_DEV // 2),
        out_shape=[_sds((s, d), jnp.float32), _sds((s, d), MXU), _sds((8, LANES), jnp.float32)],
        in_specs=[pl.BlockSpec((2, tm, f), lambda i, j: (j, i, 0)), pl.BlockSpec((2, f, d), lambda i, j: (j, 0, 0)),
                  row, row],
        out_specs=[row, row, pl.BlockSpec((8, LANES), lambda i, j: (0, 0))],
        scratch=[pltpu.VMEM((tm, d), jnp.float32)], sem=("arbitrary", "arbitrary"))(act, w_d, h1, target)


def _ffn_bwd_act(dout, w_d, gu):
    s, d = dout.shape
    f = w_d.shape[1]
    tm = min(2 * FFN_TILE, s)
    ni = s // tm

    def body(do_ref, w_ref, gu_ref, dgu_ref, dw_ref, acc):
        i = pl.program_id(1)
        do = do_ref[...]
        d_act = _dot_nt(do, w_ref[0])
        g, u = gu_ref[0, 0], gu_ref[0, 1]
        sig = jax.nn.sigmoid(g)
        silu = g * sig
        dgu_ref[0, 0] = (d_act * u * (sig * (1.0 + g * (1.0 - sig)))).astype(dgu_ref.dtype)
        dgu_ref[0, 1] = (d_act * silu).astype(dgu_ref.dtype)
        part = _dot_tn(silu * u, do)

        @pl.when(i == 0)
        def _():
            acc[...] = part

        @pl.when(i > 0)
        def _():
            acc[...] += part

        @pl.when(i == ni - 1)
        def _():
            dw_ref[0] = acc[...].astype(dw_ref.dtype)

    return _pcall(
        body, name="ffn_bwd_act", grid=(N_DEV, ni),
        out_shape=[_sds((N_DEV, 2, s, f), MXU), _sds((N_DEV, f, d), WIRE)],
        in_specs=[pl.BlockSpec((tm, d), lambda j, i: (i, 0)), pl.BlockSpec((1, f, d), lambda j, i: (j, 0, 0)),
                  pl.BlockSpec((1, 2, tm, f), lambda j, i: (j, 0, i, 0))],
        out_specs=[pl.BlockSpec((1, 2, tm, f), lambda j, i: (j, 0, i, 0)),
                   pl.BlockSpec((1, f, d), lambda j, i: (j, 0, 0))],
        scratch=[pltpu.VMEM((f, d), jnp.float32)], sem=("parallel", "arbitrary"))(dout, w_d, gu)


def _ffn_dw_gu(fn, dgu):
    s, d = fn.shape
    f = dgu.shape[-1]
    tk = min(4 * FFN_TILE, s)
    nk = s // tk

    def body(fn_ref, dgu_ref, dw_ref, acc):
        k = pl.program_id(2)
        part = _dot_tn(dgu_ref[0, 0], fn_ref[...])

        @pl.when(k == 0)
        def _():
            acc[...] = part

        @pl.when(k > 0)
        def _():
            acc[...] += part

        @pl.when(k == nk - 1)
        def _():
            dw_ref[0, 0] = acc[...].astype(dw_ref.dtype)

    return _pcall(
        body, name="ffn_dw_gate_up", grid=(N_DEV, 2, nk),
        out_shape=_sds((N_DEV, 2, f, d), WIRE),
        in_specs=[pl.BlockSpec((tk, d), lambda j, w, k: (k, 0)),
                  pl.BlockSpec((1, 1, tk, f), lambda j, w, k: (j, w, k, 0))],
        out_specs=pl.BlockSpec((1, 1, f, d), lambda j, w, k: (j, w, 0, 0)),
        scratch=[pltpu.VMEM((f, d), jnp.float32)], sem=("parallel", "parallel", "arbitrary"))(fn, dgu)


def _ffn_dfn(dgu, w_gu, after):
    _, _, s, f = dgu.shape
    d = w_gu.shape[3]
    tm = min(FFN_TILE, s)

    def body(dgu_ref, w_ref, dfn_ref):
        j = pl.program_id(1)
        part = (_dot(dgu_ref[0, 0], w_ref[0, 0]) + _dot(dgu_ref[0, 1], w_ref[0, 1])
                + _dot(dgu_ref[1, 0], w_ref[1, 0]) + _dot(dgu_ref[1, 1], w_ref[1, 1]))

        @pl.when(j == 0)
        def _():
            dfn_ref[...] = part

        @pl.when(j > 0)
        def _():
            dfn_ref[...] += part

    return _pcall(
        body, name="ffn_dfn", grid=(s // tm, N_DEV // 2),
        out_shape=_sds((s, d), jnp.float32),
        in_specs=[pl.BlockSpec((2, 2, tm, f), lambda i, j: (j, 0, i, 0)),
                  pl.BlockSpec((2, 2, f, d), lambda i, j: (j, 0, 0, 0))],
        out_specs=pl.BlockSpec((tm, d), lambda i, j: (i, 0)),
        sem=("parallel", "arbitrary"), after=after)(dgu, w_gu)


def _ffn_norm_bwd(d_fn, dout, h1, g_ffn):
    s, d = h1.shape
    tm = min(ROW_TILE, s)

    def body(dfn_ref, do_ref, h1_ref, g_ref, dh1_ref, dg_ref):
        i = pl.program_id(0)

        @pl.when(i == 0)
        def _():
            dg_ref[...] = jnp.zeros_like(dg_ref)

        _, xn, r = _norm_fwd(h1_ref[...], g_ref[...])
        dx, dg = _norm_bwd(xn, r, g_ref[...], dfn_ref[...])
        dh1_ref[...] = do_ref[...] + dx
        dg_ref[...] += dg

    row = pl.BlockSpec((tm, d), lambda i: (i, 0))
    vec = pl.BlockSpec((1, d), lambda i: (0, 0))
    return _pcall(
        body, name="ffn_norm_bwd", grid=(s // tm,),
        out_shape=[_sds((s, d), jnp.float32), _sds((1, d), jnp.float32)],
        in_specs=[row, row, row, vec], out_specs=[row, vec], sem=("arbitrary",))(d_fn, dout, h1, g_ffn)


def _mem_bwd(proj, g_mq, km, vmm, d_y, y_m, lse):
    s = proj.shape[0]
    ml, hw = km.shape
    tm = min(FFN_TILE, s)
    scale = MEM_DIM ** -0.5

    def body(q_ref, g_ref, k_ref, v_ref, do_ref, y_ref, lse_ref, dq_ref, dk_ref, dv_ref, dg_ref):
        i = pl.program_id(0)

        @pl.when(i == 0)
        def _():
            dk_ref[...] = jnp.zeros_like(dk_ref)
            dv_ref[...] = jnp.zeros_like(dv_ref)
            dg_ref[...] = jnp.zeros_like(dg_ref)

        col = lax.broadcasted_iota(jnp.int32, (tm, MEM_HEADS), 1)
        lse_t = lse_ref[...]
        for h in range(MEM_HEADS):
            sl = slice(MEM_DIM * h, MEM_DIM * (h + 1))
            qn, xn, r = _norm_fwd(q_ref[:, sl], g_ref[...])
            lse_h = jnp.sum(jnp.where(col == h, lse_t, 0.0), -1, keepdims=True)
            p = jnp.exp(_dot_nt(qn, k_ref[:, sl]) * scale - lse_h)
            do = do_ref[:, sl]
            dd = jnp.sum(do * y_ref[:, sl], -1, keepdims=True)
            dp = _dot_nt(do, v_ref[:, sl])
            ds = (p * (dp - dd)).astype(MXU)
            dv_ref[:, sl] += _dot_tn(p, do)
            dk_ref[:, sl] += _dot_tn(ds, qn) * scale
            dx, dg = _norm_bwd(xn, r, g_ref[...], _dot(ds, k_ref[:, sl]) * scale)
            dq_ref[:, sl] = dx.astype(dq_ref.dtype)
            dg_ref[...] += dg

    full = pl.BlockSpec((ml, hw), lambda i: (0, 0))
    return _pcall(
        body, name="mem_bwd", grid=(s // tm,),
        out_shape=[_sds((s, hw), MXU), _sds((ml, hw), jnp.float32), _sds((ml, hw), jnp.float32),
                   _sds((1, MEM_DIM), jnp.float32)],
        in_specs=[pl.BlockSpec((tm, hw), lambda i: (i, C_QM // hw)), pl.BlockSpec((1, MEM_DIM), lambda i: (0, 0)),
                  full, full, pl.BlockSpec((tm, hw), lambda i: (i, 3)), pl.BlockSpec((tm, hw), lambda i: (i, 0)),
                  pl.BlockSpec((tm, MEM_HEADS), lambda i: (i, 0))],
        out_specs=[pl.BlockSpec((tm, hw), lambda i: (i, 0)), full, full,
                   pl.BlockSpec((1, MEM_DIM), lambda i: (0, 0))],
        sem=("arbitrary",))(proj, g_mq, km, vmm, d_y, y_m, lse)


def _memkv_bwd(mem, g_mem, w_mkv, g_mk, kv, memn, dk, dv):
    ml, d = mem.shape
    hw = MEM_HEADS * MEM_DIM

    def body(mem_ref, g_ref, w_ref, gk_ref, kv_ref, mn_ref, dk_ref, dv_ref, dw_ref, dgm_ref, dgk_ref):
        parts = []
        dgk = jnp.zeros((1, MEM_DIM), jnp.float32)
        for h in range(MEM_HEADS):
            sl = slice(MEM_DIM * h, MEM_DIM * (h + 1))
            _, xn, r = _norm_fwd(kv_ref[:, sl], gk_ref[...])
            dx, dg = _norm_bwd(xn, r, gk_ref[...], dk_ref[:, sl])
            parts.append(dx)
            dgk = dgk + dg
        dkv = jnp.concatenate(parts + [dv_ref[...]], axis=1).astype(MXU)
        dgk_ref[...] = dgk
        dw_ref[...] = _dot_tn(mn_ref[...], dkv).astype(dw_ref.dtype)
        d_mn = _dot_nt(dkv, w_ref[...])
        _, xn, _ = _norm_fwd(mem_ref[...], g_ref[...])
        dgm_ref[...] = jnp.sum(d_mn * xn, 0, keepdims=True)

    vm = pl.BlockSpec(memory_space=pltpu.VMEM)
    return _pcall(
        body, name="memkv_bwd",
        out_shape=[_sds((d, 2 * hw), WIRE), _sds((1, d), jnp.float32), _sds((1, MEM_DIM), jnp.float32)],
        in_specs=[vm] * 8, out_specs=[vm] * 3)(mem, g_mem, w_mkv, g_mk, kv, memn, dk, dv)


def _mla_bwd(qc, kc, v, d_y, y_b, lse, after):
    nh, s, _ = qc.shape
    t = min(ATT_TILE, s)
    nb = s // t
    scale = (MLA_NOPE + MLA_ROPE) ** -0.5

    def body(q_ref, k_ref, v_ref, do_ref, y_ref, lse_ref, dq_ref, dk_ref, dv_ref, dk_acc, dv_acc):
        kj, qi = pl.program_id(1), pl.program_id(2)

        @pl.when((kj == 0) & (qi == 0))
        def _():
            dq_ref[...] = jnp.zeros_like(dq_ref)

        @pl.when(qi == kj)
        def _():
            dk_acc[...] = jnp.zeros_like(dk_acc)
            dv_acc[...] = jnp.zeros_like(dv_acc)

        def step(diagonal):
            rc = t // 4 if diagonal else t
            for c in range(t // rc):
                rows = slice(rc * c, rc * (c + 1))
                keys = slice(0, rc * (c + 1))
                q, k = q_ref[0, rows, :], k_ref[0, keys, :]
                sc = _dot_nt(q, k) * (scale * LOG2E)
                if diagonal:
                    r_i = lax.broadcasted_iota(jnp.int32, sc.shape, 0) + rc * c
                    c_i = lax.broadcasted_iota(jnp.int32, sc.shape, 1)
                    sc = jnp.where(c_i <= r_i, sc, NEG_INF)
                p = jnp.exp2(sc - lse_ref[0, rows, :])
                do = do_ref[rows, :]
                dd = jnp.sum(do * y_ref[rows, :], -1, keepdims=True)
                dp = _dot_nt(do, v_ref[0, keys, :])
                ds = (p * (dp - dd) * scale).astype(MXU)
                dv_acc[keys, :] += _dot_tn(p, do)
                dk_acc[keys, :] += _dot_tn(ds, q)
                out_rows = pl.ds(pl.multiple_of(qi * t + rc * c, rc), rc)
                dq_ref[0, out_rows, :] += _dot(ds, k)

        @pl.when(qi > kj)
        def _():
            step(False)

        @pl.when(qi == kj)
        def _():
            step(True)

        @pl.when(qi == nb - 1)
        def _():
            dk_ref[0] = dk_acc[...]
            dv_ref[0] = dv_acc[...]

    qmap = lambda h, j, i: (h, jnp.maximum(i, j), 0)
    return _pcall(
        body, name="mla_bwd", grid=(nh, nb, nb),
        out_shape=[_sds((nh, s, 256), jnp.float32), _sds((nh, s, 256), jnp.float32),
                   _sds((nh, s, MLA_V), jnp.float32)],
        in_specs=[pl.BlockSpec((1, t, 256), qmap),
                  pl.BlockSpec((1, t, 256), lambda h, j, i: (h, j, 0)),
                  pl.BlockSpec((1, t, MLA_V), lambda h, j, i: (h, j, 0)),
                  pl.BlockSpec((t, MLA_V), lambda h, j, i: (jnp.maximum(i, j), 8 + h)),
                  pl.BlockSpec((t, MLA_V), lambda h, j, i: (jnp.maximum(i, j), h)),
                  pl.BlockSpec((1, t, 1), qmap)],
        out_specs=[pl.BlockSpec((1, s, 256), lambda h, j, i: (h, 0, 0)),
                   pl.BlockSpec((1, t, 256), lambda h, j, i: (h, j, 0)),
                   pl.BlockSpec((1, t, MLA_V), lambda h, j, i: (h, j, 0))],
        scratch=[pltpu.VMEM((t, 256), jnp.float32), pltpu.VMEM((t, MLA_V), jnp.float32)],
        sem=("parallel", "arbitrary", "arbitrary"), after=after)(qc, kc, v, d_y, y_b, lse)


def _mla_prep_bwd(proj, cos, sin, g_cq, g_ckv, w_uq, w_ukv, g_qn, g_qr, g_kn, g_kr,
                  qb, kvb, cqn, ckvn, dqc, dkc, dv):
    s = proj.shape[0]
    tm = min(ROW_TILE, s)
    nh = MLA_HEADS
    ni = s // tm

    def body(cq_ref, ckv_ref, kr_ref, cos_ref, sin_ref, gcq_ref, gckv_ref, wuq_ref, wukv_ref,
             gqn_ref, gqr_ref, gkn_ref, gkr_ref, qb_ref, kvb_ref, cqn_ref, ckvn_ref, dqc_ref, dkc_ref, dv_ref,
             dcq_ref, dckv_ref, dkr_ref, dwuq_ref, dwukv_ref,
             dgcq_ref, dgckv_ref, dgqn_ref, dgqr_ref, dgkn_ref, dgkr_ref, acc_uq, acc_ukv):
        i = pl.program_id(0)

        @pl.when(i == 0)
        def _():
            acc_uq[...] = jnp.zeros_like(acc_uq)
            acc_ukv[...] = jnp.zeros_like(acc_ukv)
            for ref in (dgcq_ref, dgckv_ref, dgqn_ref, dgqr_ref, dgkn_ref, dgkr_ref):
                ref[...] = jnp.zeros_like(ref)

        cos_t, sin_t = cos_ref[...], sin_ref[...]
        lo = _lo_mask((tm, LANES))
        qb_v, kvb_v = qb_ref[...], kvb_ref[...]
        dq_parts, dgqn = [], jnp.zeros((1, LANES), jnp.float32)
        for h in range(nh):
            _, xn, r = _norm_fwd(qb_v[:, MLA_NOPE * h: MLA_NOPE * (h + 1)], gqn_ref[...])
            dx, dg = _norm_bwd(xn, r, gqn_ref[...], dqc_ref[h][:, :MLA_NOPE])
            dq_parts.append(dx)
            dgqn = dgqn + dg
        dgqn_ref[...] += dgqn
        dgqr = jnp.zeros((1, LANES), jnp.float32)
        for j in range(nh // 2):
            d_rope = jnp.where(lo, dqc_ref[2 * j][:, MLA_NOPE:], dqc_ref[2 * j + 1][:, MLA_NOPE:])
            d_pre = _rope_bwd(d_rope, cos_t, sin_t)
            xr = qb_v[:, nh * MLA_NOPE + LANES * j: nh * MLA_NOPE + LANES * (j + 1)]
            _, xn, r = _norm_fwd(xr, gqr_ref[...], half=True)
            dx, dg = _norm_bwd(xn, r, gqr_ref[...], d_pre, half=True)
            dq_parts.append(dx)
            dgqr = dgqr + dg
        dgqr_ref[...] += dgqr
        dqb = jnp.concatenate(dq_parts, axis=1).astype(MXU)
        acc_uq[...] += _dot_tn(dqb, cqn_ref[...])
        _, xn, r = _norm_fwd(cq_ref[...], gcq_ref[...])
        dx, dg = _norm_bwd(xn, r, gcq_ref[...], _dot(dqb, wuq_ref[...]))
        dcq_ref[...] = dx.astype(dcq_ref.dtype)
        dgcq_ref[...] += dg
        dkv_parts, dgkn = [], jnp.zeros((1, LANES), jnp.float32)
        d_kr2 = jnp.zeros((tm, LANES), jnp.float32)
        for h in range(nh):
            _, xn, r = _norm_fwd(kvb_v[:, 256 * h: 256 * h + MLA_NOPE], gkn_ref[...])
            dx, dg = _norm_bwd(xn, r, gkn_ref[...], dkc_ref[h][:, :MLA_NOPE])
            dkv_parts += [dx, dv_ref[h]]
            dgkn = dgkn + dg
            d_kr2 = d_kr2 + dkc_ref[h][:, MLA_NOPE:]
        dgkn_ref[...] += dgkn
        dkvb = jnp.concatenate(dkv_parts, axis=1).astype(MXU)
        d_ckvn = jnp.zeros((tm, 512), jnp.float32)
        for dev in range(N_DEV):
            piece = dkvb[:, LANES * dev: LANES * (dev + 1)]
            acc_ukv[dev] += _dot_tn(ckvn_ref[...], piece)
            d_ckvn = d_ckvn + _dot_nt(piece, wukv_ref[dev])
        _, xn, r = _norm_fwd(ckv_ref[...], gckv_ref[...])
        dx, dg = _norm_bwd(xn, r, gckv_ref[...], d_ckvn)
        dckv_ref[...] = dx.astype(dckv_ref.dtype)
        dgckv_ref[...] += dg
        d_kr = jnp.where(lo, d_kr2 + pltpu.roll(d_kr2, 64, 1), 0.0)
        d_pre = _rope_bwd(d_kr, cos_t, sin_t)
        _, xn, r = _norm_fwd(kr_ref[...], gkr_ref[...], half=True)
        dx, dg = _norm_bwd(xn, r, gkr_ref[...], d_pre, half=True)
        dkr_ref[...] = jnp.where(lo, dx, 0.0).astype(dkr_ref.dtype)
        dgkr_ref[...] += jnp.where(_lo_mask((1, LANES)), dg, 0.0)

        @pl.when(i == ni - 1)
        def _():
            dwuq_ref[...] = acc_uq[...].astype(dwuq_ref.dtype)
            dwukv_ref[...] = acc_ukv[...].astype(dwukv_ref.dtype)

    def col(width, start):
        return pl.BlockSpec((tm, width), lambda i: (i, start // width))

    def full(shape):
        return pl.BlockSpec(shape, lambda i: (0,) * len(shape))

    def row(width):
        return pl.BlockSpec((tm, width), lambda i: (i, 0))

    def heads(width):
        return pl.BlockSpec((nh, tm, width), lambda i: (0, i, 0))

    vec = full((1, LANES))
    return _pcall(
        body, name="mla_prep_bwd", grid=(ni,),
        out_shape=[_sds((s, 512), MXU), _sds((s, 512), MXU), _sds((s, LANES), MXU),
                   _sds((768, 512), WIRE), _sds((N_DEV, 512, LANES), WIRE),
                   _sds((1, 512), jnp.float32), _sds((1, 512), jnp.float32)] + [_sds((1, LANES), jnp.float32)] * 4,
        in_specs=[col(512, C_CQ), col(512, C_CKV), col(LANES, C_KR), row(LANES), row(LANES),
                  full((1, 512)), full((1, 512)), full((768, 512)), full((N_DEV, 512, LANES)), vec, vec, vec, vec,
                  row(768), row(1024), row(512), row(512), heads(256), heads(256), heads(MLA_V)],
        out_specs=[row(512), row(512), row(LANES), full((768, 512)), full((N_DEV, 512, LANES)),
                   full((1, 512)), full((1, 512)), vec, vec, vec, vec],
        scratch=[pltpu.VMEM((768, 512), jnp.float32), pltpu.VMEM((N_DEV, 512, LANES), jnp.float32)],
        sem=("arbitrary",))(proj, proj, proj, cos, sin, g_cq, g_ckv, w_uq, w_ukv, g_qn, g_qr, g_kn, g_kr,
                            qb, kvb, cqn, ckvn, dqc, dkc, dv)


def _swa_bwd(proj, posc, posr, gq, gk, sinks, d_y, y_a, lse, after):
    s = proj.shape[0]
    b = SWA_BLOCK
    nb = s // b
    scale = SWA_DIM ** -0.5

    def body(q_ref, kp_ref, kc_ref, vp_ref, vc_ref, pq_ref, pkp_ref, pkc_ref, gq_ref, gk_ref, sink_ref,
             do_ref, y_ref, lse_ref, kfull_ref,
             dq_ref, dk_ref, dv_ref, dgq_ref, dgk_ref, dsink_ref, dk_acc, dv_acc):
        n = pl.program_id(0)

        @pl.when(n == 0)
        def _():
            dk_acc[...] = jnp.zeros_like(dk_acc)
            dv_acc[...] = jnp.zeros_like(dv_acc)
            dgq_ref[...] = jnp.zeros_like(dgq_ref)
            dsink_ref[...] = jnp.zeros_like(dsink_ref)

        kn, v, bias = _swa_common(n, kp_ref[...], kc_ref[...], vp_ref[...], vc_ref[...],
                                  pq_ref[...], pkp_ref[...], pkc_ref[...], gk_ref[...])
        lo = _lo_mask((b, LANES))
        col = lax.broadcasted_iota(jnp.int32, (b, SWA_Q_HEADS), 1)
        col1 = lax.broadcasted_iota(jnp.int32, (1, SWA_Q_HEADS), 1)
        lse_t = lse_ref[...]
        dk_blk = jnp.zeros((2 * b, LANES), jnp.float32)
        dv_blk = jnp.zeros((2 * b, LANES), jnp.float32)
        dgq = jnp.zeros((1, LANES), jnp.float32)
        dsink = jnp.zeros((1, SWA_Q_HEADS), jnp.float32)
        for j in range(SWA_Q_HEADS // 2):
            hk = (2 * j) // (SWA_Q_HEADS // SWA_KV_HEADS)
            kvmask = lo if hk == 0 else jnp.logical_not(lo)
            sl = slice(LANES * j, LANES * (j + 1))
            qn, xn, r = _norm_fwd(q_ref[:, sl], gq_ref[...], half=True)
            qn = qn * scale
            qsw = pltpu.roll(qn, 64, 1)
            d2 = do_ref[:, sl]
            d2sw = pltpu.roll(d2, 64, 1)
            prod = d2 * y_ref[:, sl]
            dqs = []
            for e in range(2):
                h = 2 * j + e
                half_e = lo if e == 0 else jnp.logical_not(lo)
                qm = jnp.where(kvmask, qn if e == hk else qsw, 0.0)
                dm = jnp.where(kvmask, d2 if e == hk else d2sw, 0.0)
                sc = _dot_nt(qm, kn) + _alibi_slope(h) * bias
                lse_h = jnp.sum(jnp.where(col == h, lse_t, 0.0), -1, keepdims=True)
                p = jnp.exp(sc - lse_h)
                dd = jnp.sum(jnp.where(half_e, prod, 0.0), -1, keepdims=True)
                dp = _dot_nt(dm, v)
                ds = (p * (dp - dd)).astype(MXU)
                dsink = dsink - jnp.where(col1 == h, jnp.sum(jnp.exp(sink_ref[h] - lse_h) * dd), 0.0)
                dq_m = _dot(ds, kn) * scale
                dk_blk = dk_blk + _dot_tn(ds, qm)
                dv_blk = dv_blk + _dot_tn(p, dm)
                dqs.append(dq_m if e == hk else pltpu.roll(dq_m, 64, 1))
            dx, dg = _norm_bwd(xn, r, gq_ref[...], jnp.where(lo, dqs[0], dqs[1]), half=True)
            dq_ref[:, sl] = dx.astype(dq_ref.dtype)
            dgq = dgq + dg
        dgq_ref[...] += dgq
        dsink_ref[...] += dsink
        prev = pl.ds(pl.multiple_of(jnp.maximum(n - 1, 0) * b, b), b)
        cur = pl.ds(pl.multiple_of(n * b, b), b)
        dk_acc[prev, :] += dk_blk[:b]
        dv_acc[prev, :] += dv_blk[:b]
        dk_acc[cur, :] += dk_blk[b:]
        dv_acc[cur, :] += dv_blk[b:]

        @pl.when(n == nb - 1)
        def _():
            _, kxn, kr = _norm_fwd(kfull_ref[...], gk_ref[...], half=True)
            dx, dg = _norm_bwd(kxn, kr, gk_ref[...], dk_acc[...], half=True)
            dk_ref[...] = dx.astype(dk_ref.dtype)
            dv_ref[...] = dv_acc[...].astype(dv_ref.dtype)
            dgk_ref[...] = dg

    full = pl.BlockSpec((s, LANES), lambda n: (0, 0))
    vec = pl.BlockSpec((1, LANES), lambda n: (0, 0))
    return _pcall(
        body, name="swa_bwd", grid=(nb,),
        out_shape=[_sds((s, 1024), MXU), _sds((s, LANES), MXU), _sds((s, LANES), MXU),
                   _sds((1, LANES), jnp.float32), _sds((1, LANES), jnp.float32),
                   _sds((1, SWA_Q_HEADS), jnp.float32)],
        in_specs=_swa_specs(s) + [pl.BlockSpec((b, 1024), lambda n: (n, 0)), pl.BlockSpec((b, 1024), lambda n: (n, 0)),
                                  pl.BlockSpec((b, SWA_Q_HEADS), lambda n: (n, 0)),
                                  pl.BlockSpec((s, LANES), lambda n: (0, C_KA // LANES))],
        out_specs=[pl.BlockSpec((b, 1024), lambda n: (n, 0)), full, full, vec, vec,
                   pl.BlockSpec((1, SWA_Q_HEADS), lambda n: (0, 0))],
        scratch=[pltpu.VMEM((s, LANES), jnp.float32), pltpu.VMEM((s, LANES), jnp.float32)],
        sem=("arbitrary",), after=after)(proj, proj, proj, proj, proj, posc, posr, posr, gq, gk, sinks, d_y, y_a, lse,
                                         proj)


def _dx(d_proj, w_in, x, g, d_h1, after):
    s, d = x.shape
    n = w_in.shape[0]
    tm = min(ROW_TILE, s)

    def body(dp_ref, w_ref, x_ref, g_ref, dh_ref, dx_ref, dg_ref):
        i = pl.program_id(0)

        @pl.when(i == 0)
        def _():
            dg_ref[...] = jnp.zeros_like(dg_ref)

        d_hn = _dot(dp_ref[...], w_ref[...])
        _, xn, r = _norm_fwd(x_ref[...], g_ref[...])
        dx, dg = _norm_bwd(xn, r, g_ref[...], d_hn)
        dx_ref[...] = dh_ref[...] + dx
        dg_ref[...] += dg

    row = pl.BlockSpec((tm, d), lambda i: (i, 0))
    vec = pl.BlockSpec((1, d), lambda i: (0, 0))
    return _pcall(
        body, name="grad_x", grid=(s // tm,),
        out_shape=[_sds((s, d), jnp.float32), _sds((1, d), jnp.float32)],
        in_specs=[pl.BlockSpec((tm, n), lambda i: (i, 0)), pl.BlockSpec((n, d), lambda i: (0, 0)), row, vec, row],
        out_specs=[row, vec], sem=("arbitrary",), after=after)(d_proj, w_in, x, g, d_h1)


_SMALL = ["attn_norm_g", "swa_q_norm_g", "swa_k_norm_g", "swa_sinks", "mla_cq_norm_g", "mla_ckv_norm_g",
          "mla_qn_norm_g", "mla_qr_norm_g", "mla_kn_norm_g", "mla_kr_norm_g", "mem_norm_g",
          "mem_q_norm_g", "mem_k_norm_g", "ffn_norm_g"]


def kernel(x, mem, positions, attn_norm_g, w_in, swa_q_norm_g, swa_k_norm_g, swa_sinks, mla_cq_norm_g, mla_ckv_norm_g, w_uq, w_ukv, mla_qn_norm_g, mla_qr_norm_g, mla_kn_norm_g, mla_kr_norm_g, mem_norm_g, w_mem_kv, mem_q_norm_g, mem_k_norm_g, w_out, ffn_norm_g, w_gate, w_up, w_down, loss_target, m_attn_norm_g, m_w_in, m_swa_q_norm_g, m_swa_k_norm_g, m_swa_sinks, m_mla_cq_norm_g, m_mla_ckv_norm_g, m_w_uq, m_w_ukv, m_mla_qn_norm_g, m_mla_qr_norm_g, m_mla_kn_norm_g, m_mla_kr_norm_g, m_mem_norm_g, m_w_mem_kv, m_mem_q_norm_g, m_mem_k_norm_g, m_w_out, m_ffn_norm_g, m_w_gate, m_w_up, m_w_down, v_attn_norm_g, v_w_in, v_swa_q_norm_g, v_swa_k_norm_g, v_swa_sinks, v_mla_cq_norm_g, v_mla_ckv_norm_g, v_w_uq, v_w_ukv, v_mla_qn_norm_g, v_mla_qr_norm_g, v_mla_kn_norm_g, v_mla_kr_norm_g, v_mem_norm_g, v_w_mem_kv, v_mem_q_norm_g, v_mem_k_norm_g, v_w_out, v_ffn_norm_g, v_w_gate, v_w_up, v_w_down):
    args = dict(locals())
    x2, mem2, tgt = x[0], mem[0], loss_target[0]
    s, d = x2.shape
    n_in = w_in.shape[2]
    f = w_gate.shape[2]

    (g_in,) = _all_gather([w_in[0].T.astype(WIRE)])
    mix_shards = [w_uq[0].T.astype(WIRE), w_ukv[0].astype(WIRE), w_mem_kv[0].astype(WIRE),
                  _to_wire([w_out[0]], g_in, "wire_out")[0]]
    g_uq, wkv, g_mkv, g_out = _all_gather_background(mix_shards, 5, "all_gather_mix_weights")
    ffn_shards = [_to_wire([w_gate[0].T, w_up[0].T], g_in, "wire_gate_up"),
                  _to_wire([w_down[0]], g_in, "wire_down")[0]]
    w_gu, w_d = _all_gather_background(ffn_shards, 1, "all_gather_ffn_weights")
    wi = g_in.reshape(N_DEV * n_in, d)
    wi = jnp.concatenate([wi[0:1024], wi[1280:1792], wi[1792:2304], wi[2368:2880],
                          wi[1024:1152], wi[1152:1280], wi[2304:2368],
                          jnp.zeros((IN_PAD - 2880, d), wi.dtype)], axis=0)
    wq = g_uq.reshape(768, 512)
    wq = jnp.concatenate([wq[192 * h: 192 * h + 128] for h in range(4)]
                         + [wq[192 * h + 128: 192 * (h + 1)] for h in range(4)], axis=0)
    wmkv = g_mkv.reshape(-1, g_mkv.shape[-1])
    wo = g_out.reshape(-1, d)

    pos = positions[0].astype(jnp.float32)
    inv_freq = ROPE_THETA ** (-jnp.arange(0, MLA_ROPE, 2, dtype=jnp.float32) / MLA_ROPE)
    ang = pos[:, None] * inv_freq
    cos32, sin32 = jnp.cos(ang), jnp.sin(ang)
    cos_t = jnp.tile(cos32, (1, 4))
    sin_t = jnp.tile(jnp.concatenate([-sin32, sin32], axis=1), (1, 2))
    posc, posr = pos.reshape(s, 1), pos.reshape(1, s)
    two = lambda g: jnp.tile(g, (1, 2))
    gq2, gk2, gqr2, gkr2 = two(swa_q_norm_g), two(swa_k_norm_g), two(mla_qr_norm_g), two(mla_kr_norm_g)
    sinks1 = swa_sinks[0]

    proj, hn = _in_proj(x2, attn_norm_g, wi)
    qc, kc, vb, qb, kvb, cqn, ckvn = _mla_prep(proj, cos_t, sin_t, mla_cq_norm_g, mla_ckv_norm_g, wq, wkv,
                                                mla_qn_norm_g, gqr2, mla_kn_norm_g, gkr2)
    y_b, lse_b = _mla_fwd(qc, kc, vb)
    km, vmm, kvm, memn = _memkv_prep(mem2, mem_norm_g, wmkv, mem_k_norm_g)
    y_m, lse_m = _mem_fwd(proj, mem_q_norm_g, km, vmm)
    y_a, lse_a = _swa_fwd(proj, posc, posr, gq2, gk2, sinks1)
    h1, fn = _out_proj(y_a, y_b, y_m, x2, wo, ffn_norm_g)
    gu, act = _ffn_gu(fn, w_gu)
    dout, dout_b, loss_tile = _ffn_down(act, w_d, h1, tgt)

    dgu, dw_d = _ffn_bwd_act(dout_b, w_d, gu)
    dw_gu = _ffn_dw_gu(fn, dgu)
    r_gu, r_d = _exchange_grads_background([dw_gu, dw_d], 2, "exchange_ffn_grads")
    d_h1, dg_ffn = _ffn_norm_bwd(_ffn_dfn(dgu, w_gu, dw_gu), dout, h1, ffn_norm_g)
    d_y = _mm(d_h1, wo, tb=True, out_dtype=jnp.float32, tm=FFN_TILE, tk=2048, name="d_mix")
    dw_out = jnp.concatenate([
        _mm(y_a, d_h1, ta=True, out_dtype=WIRE, tm=1024, tk=1024, name="dw_out_a"),
        _mm(y_b, d_h1, ta=True, out_dtype=WIRE, tm=1024, tk=1024, name="dw_out_b"),
        _mm(y_m, d_h1, ta=True, out_dtype=WIRE, tm=1024, tk=1024, name="dw_out_m")], axis=0)
    d_qm, dkm, dvmm, dg_mq = _mem_bwd(proj, mem_q_norm_g, km, vmm, d_y, y_m, lse_m)
    dw_mkv, dg_mem, dg_mk = _memkv_bwd(mem2, mem_norm_g, wmkv, mem_k_norm_g, kvm, memn, dkm, dvmm)
    r_mkv, r_out = _exchange_grads_background([dw_mkv.reshape(g_mkv.shape), dw_out.reshape(g_out.shape)], 3,
                                              "exchange_mix_grads")
    dqc, dkc, dvb = _mla_bwd(qc, kc, vb, d_y, y_b, lse_b, dw_mkv)
    (d_cq, d_ckv, d_kr, dw_uq, dw_ukv, dg_cq, dg_ckv, dg_qn, dg_qr, dg_kn, dg_kr) = _mla_prep_bwd(
        proj, cos_t, sin_t, mla_cq_norm_g, mla_ckv_norm_g, wq, wkv, mla_qn_norm_g, gqr2, mla_kn_norm_g, gkr2,
        qb, kvb, cqn, ckvn, dqc, dkc, dvb)
    d_qa, d_ka, d_va, dg_q, dg_k, d_sinks = _swa_bwd(proj, posc, posr, gq2, gk2, sinks1, d_y, y_a, lse_a, dw_out)
    d_proj = jnp.concatenate([d_qa, d_cq, d_ckv, d_qm, d_ka, d_va, d_kr], axis=1)
    gi = _dw_in(hn, d_proj, n_in)

    gq_ = jnp.concatenate(sum([[dw_uq[128 * h: 128 * (h + 1)], dw_uq[512 + 64 * h: 512 + 64 * (h + 1)]]
                               for h in range(4)], []), axis=0)
    gq_ = gq_.reshape(N_DEV, 96, 512)
    r_in, r_uq, r_ukv = _exchange_grads_background([gi, gq_, dw_ukv], 4, "exchange_in_grads")
    grad_x, dg_attn = _dx(d_proj, wi, x2, attn_norm_g, d_h1, gi)

    big = {}
    def adam(name, r, transposed=False, after=None, which=None):
        w, m, v = args[name][0], args["m_" + name][0], args["v_" + name][0]
        if transposed:
            outs = _adam_big(r, w.T, m.T, v.T, "adam_" + name, after, which)
            return [o.T[None] for o in outs]
        return [o[None] for o in _adam_big(r, w, m, v, "adam_" + name, after)]
    big["w_gate"] = adam("w_gate", r_gu, True, which=0)
    big["w_up"] = adam("w_up", r_gu, True, after=big["w_gate"][0], which=1)
    big["w_down"] = adam("w_down", r_d, after=big["w_up"][0])
    big["w_out"] = adam("w_out", r_out, after=big["w_down"][0])
    big["w_mem_kv"] = adam("w_mem_kv", r_mkv, after=big["w_out"][0])
    big["w_in"] = adam("w_in", r_in, True, after=big["w_mem_kv"][0])
    big["w_uq"] = adam("w_uq", r_uq, True, after=big["w_in"][0])
    big["w_ukv"] = adam("w_ukv", r_ukv, after=big["w_uq"][0])

    small_g = {
        "attn_norm_g": dg_attn, "swa_q_norm_g": dg_q, "swa_k_norm_g": dg_k,
        "swa_sinks": d_sinks, "mla_cq_norm_g": dg_cq, "mla_ckv_norm_g": dg_ckv, "mla_qn_norm_g": dg_qn,
        "mla_qr_norm_g": dg_qr, "mla_kn_norm_g": dg_kn, "mla_kr_norm_g": dg_kr,
        "mem_norm_g": dg_mem, "mem_q_norm_g": dg_mq, "mem_k_norm_g": dg_mk, "ffn_norm_g": dg_ffn}
    loss11, small_out = _small_allreduce_adam(
        [small_g[n] for n in _SMALL], loss_tile, [args[n] for n in _SMALL],
        [args["m_" + n] for n in _SMALL], [args["v_" + n] for n in _SMALL])
    small = dict(zip(_SMALL, small_out))
    loss = loss11.reshape(())

    order = ["attn_norm_g", "w_in", "swa_q_norm_g", "swa_k_norm_g", "swa_sinks", "mla_cq_norm_g", "mla_ckv_norm_g",
             "w_uq", "w_ukv", "mla_qn_norm_g", "mla_qr_norm_g", "mla_kn_norm_g", "mla_kr_norm_g", "mem_norm_g",
             "w_mem_kv", "mem_q_norm_g", "mem_k_norm_g", "w_out", "ffn_norm_g", "w_gate", "w_up", "w_down"]
    res = {n: (big[n] if n in big else list(small[n])) for n in order}
    outs = [loss, grad_x[None]]
    for kind in range(4):
        outs += [res[n][kind] for n in order]
    return tuple(outs)
```

```python
import jax
import jax.numpy as jnp
from jax import lax
from jax.experimental import pallas as pl
from jax.experimental.pallas import tpu as pltpu
from jax.experimental.pallas import tpu_sc as plsc

MXU = jnp.bfloat16
WIRE = jnp.bfloat16
EPS = 1e-6
NEG_INF = -1e30
LOG2E = 1.4426950408889634
N_DEV = 8
LANES = 128
ROW_TILE = 256
FFN_TILE = 512
ATT_TILE = 1024
SWA_BLOCK = 128
VMEM_LIMIT = 56 * 1024 * 1024

SWA_Q_HEADS, SWA_KV_HEADS, SWA_DIM = 16, 2, 64
MLA_HEADS, MLA_NOPE, MLA_ROPE, MLA_V = 4, 128, 64, 128
MEM_HEADS, MEM_DIM = 4, 128
ROPE_THETA = 10000.0
ADAM_LR, ADAM_B1, ADAM_B2, ADAM_EPS, ADAM_WD, ADAM_STEP = 0.001, 0.9, 0.999, 1e-08, 0.01, 10

C_QA, C_CQ, C_CKV, C_QM, C_KA, C_VA, C_KR, IN_PAD = 0, 1024, 1536, 2048, 2560, 2688, 2816, 2944


def _pcall(body, *, name, out_shape, in_specs, out_specs, grid=(), scratch=(), sem=None, after=None):
    params = pltpu.CompilerParams(dimension_semantics=sem, vmem_limit_bytes=VMEM_LIMIT)
    if after is not None:
        n_in, inner = len(in_specs), body

        def body(*refs):
            inner(*refs[:n_in], *refs[n_in + 1:])

        in_specs = list(in_specs) + [pl.BlockSpec(memory_space=pl.ANY)]
    call = pl.pallas_call(body, name=name, grid=grid, in_specs=in_specs, out_specs=out_specs,
                          out_shape=out_shape, scratch_shapes=list(scratch), compiler_params=params)
    return call if after is None else (lambda *ops: call(*ops, after))


def _sds(shape, dtype):
    return jax.ShapeDtypeStruct(tuple(shape), dtype)


def _dot(a, b):
    return jnp.dot(a.astype(MXU), b.astype(MXU), preferred_element_type=jnp.float32)


def _dot_nt(a, b):
    return lax.dot_general(a.astype(MXU), b.astype(MXU), (((1,), (1,)), ((), ())),
                           preferred_element_type=jnp.float32)


def _dot_tn(a, b):
    return lax.dot_general(a.astype(MXU), b.astype(MXU), (((0,), (0,)), ((), ())),
                           preferred_element_type=jnp.float32)


def _lo_mask(shape):
    return (lax.broadcasted_iota(jnp.int32, shape, len(shape) - 1) % LANES) < 64


def _norm_fwd(x, g, half=False):
    x2 = x * x
    if half:
        lo = _lo_mask(x.shape)
        s_lo = jnp.sum(jnp.where(lo, x2, 0.0), -1, keepdims=True)
        s_hi = jnp.sum(jnp.where(lo, 0.0, x2), -1, keepdims=True)
        r = jnp.where(lo, lax.rsqrt(s_lo / 64.0 + EPS), lax.rsqrt(s_hi / 64.0 + EPS))
    else:
        r = lax.rsqrt(jnp.mean(x2, -1, keepdims=True) + EPS)
    xn = x * r
    return xn * g, xn, r


def _norm_bwd(xn, r, g, dy, half=False):
    t = dy * g
    tx = t * xn
    if half:
        lo = _lo_mask(xn.shape)
        m_lo = jnp.sum(jnp.where(lo, tx, 0.0), -1, keepdims=True) / 64.0
        m_hi = jnp.sum(jnp.where(lo, 0.0, tx), -1, keepdims=True) / 64.0
        m = jnp.where(lo, m_lo, m_hi)
    else:
        m = jnp.mean(tx, -1, keepdims=True)
    dx = r * (t - xn * m)
    dg = jnp.sum(dy * xn, 0, keepdims=True)
    return dx, dg


def _swap32(x):
    lane = lax.broadcasted_iota(jnp.int32, x.shape, 1)
    return jnp.where((lane % 64) < 32, pltpu.roll(x, 96, 1), pltpu.roll(x, 32, 1))


def _rope(x, cos, sin):
    return x * cos + _swap32(x) * sin


def _rope_bwd(d, cos, sin):
    return d * cos + _swap32(d * sin)


def _my_coords():
    return lax.axis_index("x"), lax.axis_index("y"), lax.axis_index("c")


def _dev_index(px, py, pc):
    return 4 * px + 2 * py + pc


_FLIPS = [(0, 0, 1), (0, 1, 0), (0, 1, 1), (1, 0, 0), (1, 0, 1), (1, 1, 0), (1, 1, 1)]


def _flip(coords, f):
    return tuple((1 - v) if b else v for v, b in zip(coords, f))


def _all_gather(shards):
    n = len(shards)

    def body(*refs):
        ins, outs = refs[:n], refs[n:2 * n]
        send_sems, recv_sems, local_sems = refs[2 * n:]
        x, y, c = _my_coords()
        me, sibling = (x, y, c), (x, y, 1 - c)
        chips = [(1 - x, y), (x, 1 - y), (1 - x, 1 - y)]

        def copy(w, k, block, to, src=None):
            dst = outs[w].at[_dev_index(*block)]
            return pltpu.make_async_remote_copy(
                src_ref=dst if src is None else src, dst_ref=dst,
                send_sem=send_sems.at[w, k], recv_sem=recv_sems.at[w, k],
                device_id=to, device_id_type=pl.DeviceIdType.MESH)

        sends, locals_ = [], []
        for w in range(n):
            mine = pltpu.make_async_copy(ins[w], outs[w].at[_dev_index(*me)], local_sems.at[w])
            mine.start()
            locals_.append(mine)
            first = [copy(w, 0, me, sibling, src=ins[w])]
            first += [copy(w, 1 + j, me, (*chip, c), src=ins[w]) for j, chip in enumerate(chips)]
            for cp in first:
                cp.start()
            sends += first
        for w in range(n):
            for j, chip in enumerate(chips):
                copy(w, 1 + j, (*chip, c), me).wait_recv()
                fwd = copy(w, 4 + j, (*chip, c), sibling)
                fwd.start()
                sends.append(fwd)
        for w in range(n):
            copy(w, 0, sibling, me).wait_recv()
            for j, chip in enumerate(chips):
                copy(w, 4 + j, (*chip, 1 - c), me).wait_recv()
        for cp in sends:
            cp.wait_send()
        for mine in locals_:
            mine.wait()

    any_spec = pl.BlockSpec(memory_space=pl.ANY)
    return _pcall(
        body, name="all_gather_weights",
        out_shape=[_sds((N_DEV,) + s.shape, s.dtype) for s in shards],
        in_specs=[any_spec] * n, out_specs=[any_spec] * n,
        scratch=[pltpu.SemaphoreType.DMA((n, 7)), pltpu.SemaphoreType.DMA((n, 7)),
                 pltpu.SemaphoreType.DMA((n,))])(*shards)


def _wire_cost(arrays):
    nbytes = sum(a.size * a.dtype.itemsize for a in arrays)
    return pl.CostEstimate(flops=0, transcendentals=0, bytes_accessed=40 * nbytes)


def _all_gather_background(shards, collective_id, name):
    n = len(shards)
    src_refs = [jax.new_ref(s, memory_space=pltpu.MemorySpace.HBM) for s in shards]
    out_refs = [jax.empty_ref(_sds((N_DEV,) + s.shape, s.dtype), memory_space=pltpu.MemorySpace.HBM) for s in shards]

    @pl.kernel(mesh=plsc.ScalarSubcoreMesh(axis_name="seq", num_cores=1), name=name,
               scratch_types=(pltpu.SemaphoreType.DMA((n, 7)), pltpu.SemaphoreType.DMA((n, 7)),
                              pltpu.SemaphoreType.DMA((n,))),
               compiler_params=pltpu.CompilerParams(collective_id=collective_id))
    def launch(send_sems, recv_sems, local_sems):
        x, y, c = _my_coords()
        me, sibling = (x, y, c), (x, y, 1 - c)
        chips = [(1 - x, y), (x, 1 - y), (1 - x, 1 - y)]
        barrier = pltpu.get_barrier_semaphore()
        for peer in [sibling] + [(*chip, c) for chip in chips]:
            pl.semaphore_signal(barrier, inc=1, device_id=peer, device_id_type=pl.DeviceIdType.MESH)
        pl.semaphore_wait(barrier, 4)

        def copy(w, k, block, to, src=None):
            dst = out_refs[w].at[_dev_index(*block)]
            return pltpu.make_async_remote_copy(
                src_ref=dst if src is None else src, dst_ref=dst,
                send_sem=send_sems.at[w, k], recv_sem=recv_sems.at[w, k],
                device_id=to, device_id_type=pl.DeviceIdType.MESH)

        sends, locals_ = [], []
        for w in range(n):
            mine = pltpu.make_async_copy(src_refs[w], out_refs[w].at[_dev_index(*me)], local_sems.at[w])
            mine.start()
            locals_.append(mine)
            first = [copy(w, 0, me, sibling, src=src_refs[w])]
            first += [copy(w, 1 + j, me, (*chip, c), src=src_refs[w]) for j, chip in enumerate(chips)]
            for cp in first:
                cp.start()
            sends += first
        for w in range(n):
            for j, chip in enumerate(chips):
                copy(w, 1 + j, (*chip, c), me).wait_recv()
                fwd = copy(w, 4 + j, (*chip, c), sibling)
                fwd.start()
                sends.append(fwd)
        for w in range(n):
            copy(w, 0, sibling, me).wait_recv()
            for j, chip in enumerate(chips):
                copy(w, 4 + j, (*chip, 1 - c), me).wait_recv()
        for cp in sends:
            cp.wait_send()
        for mine in locals_:
            mine.wait()

    launch()
    return [r[...] for r in out_refs]


def _exchange_grads(grads):
    n = len(grads)

    def body(*refs):
        ins, outs = refs[:n], refs[n:2 * n]
        send_sems, recv_sems, local_sems = refs[2 * n:]
        me = _my_coords()
        my_idx = _dev_index(*me)
        sends, locals_ = [], []
        for w in range(n):
            mine = pltpu.make_async_copy(ins[w].at[my_idx], outs[w].at[my_idx], local_sems.at[w])
            mine.start()
            locals_.append(mine)
            for k, f in enumerate(_FLIPS):
                peer = _flip(me, f)
                cp = pltpu.make_async_remote_copy(
                    src_ref=ins[w].at[_dev_index(*peer)], dst_ref=outs[w].at[my_idx],
                    send_sem=send_sems.at[w, k], recv_sem=recv_sems.at[w, k],
                    device_id=peer, device_id_type=pl.DeviceIdType.MESH)
                cp.start()
                sends.append(cp)
        for w in range(n):
            for k, f in enumerate(_FLIPS):
                peer = _flip(me, f)
                slot = outs[w].at[_dev_index(*peer)]
                pltpu.make_async_remote_copy(
                    src_ref=slot, dst_ref=slot,
                    send_sem=send_sems.at[w, k], recv_sem=recv_sems.at[w, k],
                    device_id=peer, device_id_type=pl.DeviceIdType.MESH).wait_recv()
        for cp in sends:
            cp.wait_send()
        for mine in locals_:
            mine.wait()

    any_spec = pl.BlockSpec(memory_space=pl.ANY)
    return _pcall(
        body, name="exchange_grads",
        out_shape=[_sds(g.shape, g.dtype) for g in grads],
        in_specs=[any_spec] * n, out_specs=[any_spec] * n,
        scratch=[pltpu.SemaphoreType.DMA((n, 7)), pltpu.SemaphoreType.DMA((n, 7)),
                 pltpu.SemaphoreType.DMA((n,))])(*grads)


def _exchange_grads_background(grads, collective_id, name):
    n = len(grads)
    src_refs = [jax.new_ref(g, memory_space=pltpu.MemorySpace.HBM) for g in grads]
    out_refs = [jax.empty_ref(_sds(g.shape, g.dtype), memory_space=pltpu.MemorySpace.HBM) for g in grads]

    @pl.kernel(mesh=plsc.ScalarSubcoreMesh(axis_name="seq", num_cores=1), name=name,
               scratch_types=(pltpu.SemaphoreType.DMA((n, 7)), pltpu.SemaphoreType.DMA((n, 7)),
                              pltpu.SemaphoreType.DMA((n,))),
               cost_estimate=_wire_cost(grads),
               compiler_params=pltpu.CompilerParams(collective_id=collective_id))
    def launch(send_sems, recv_sems, local_sems):
        me = _my_coords()
        my_idx = _dev_index(*me)
        peers = [_flip(me, f) for f in _FLIPS]
        barrier = pltpu.get_barrier_semaphore()
        for peer in peers:
            pl.semaphore_signal(barrier, inc=1, device_id=peer, device_id_type=pl.DeviceIdType.MESH)
        pl.semaphore_wait(barrier, len(peers))
        sends, locals_ = [], []
        for w in range(n):
            mine = pltpu.make_async_copy(src_refs[w].at[my_idx], out_refs[w].at[my_idx], local_sems.at[w])
            mine.start()
            locals_.append(mine)
            for k, peer in enumerate(peers):
                cp = pltpu.make_async_remote_copy(
                    src_ref=src_refs[w].at[_dev_index(*peer)], dst_ref=out_refs[w].at[my_idx],
                    send_sem=send_sems.at[w, k], recv_sem=recv_sems.at[w, k],
                    device_id=peer, device_id_type=pl.DeviceIdType.MESH)
                cp.start()
                sends.append(cp)
        for w in range(n):
            for k, peer in enumerate(peers):
                slot = out_refs[w].at[_dev_index(*peer)]
                pltpu.make_async_remote_copy(
                    src_ref=slot, dst_ref=slot, send_sem=send_sems.at[w, k], recv_sem=recv_sems.at[w, k],
                    device_id=peer, device_id_type=pl.DeviceIdType.MESH).wait_recv()
        for cp in sends:
            cp.wait_send()
        for mine in locals_:
            mine.wait()

    launch()
    return [r[...] for r in out_refs]


def _to_wire(parts, after, name):
    n = len(parts)
    rows, cols = parts[0].shape
    tr = rows // 2 if rows % 32 == 0 else rows

    def body(*refs):
        for k in range(n):
            refs[n][k] = refs[k][...].astype(WIRE)

    blk = pl.BlockSpec((tr, cols), lambda i: (i, 0))
    return _pcall(
        body, name=name, grid=(rows // tr,), out_shape=_sds((n, rows, cols), WIRE),
        in_specs=[blk] * n, out_specs=pl.BlockSpec((n, tr, cols), lambda i: (0, i, 0)),
        sem=("parallel",), after=after)(*parts)


def _adam_math(w, g, m, v):
    m = ADAM_B1 * m + (1.0 - ADAM_B1) * g
    v = ADAM_B2 * v + (1.0 - ADAM_B2) * (g * g)
    m_hat = m / (1.0 - ADAM_B1 ** ADAM_STEP)
    v_hat = v / (1.0 - ADAM_B2 ** ADAM_STEP)
    delta = -ADAM_LR * (m_hat / (jnp.sqrt(v_hat) + ADAM_EPS) + ADAM_WD * w)
    return delta, m, v


def _small_allreduce_adam(grads, loss_tile, ws, ms, vs):
    sizes = [w.shape[-1] for w in ws]
    n_par = len(ws)
    row0, r = [], 0
    for n in sizes:
        row0.append(r)
        r += -(-n // LANES)
    loss_row = r
    rows = -(-(r + 1) // 8) * 8

    def pieces(n):
        return [(k, min(LANES, n - LANES * k)) for k in range(-(-n // LANES))]

    def body(*refs):
        g_refs = refs[:n_par]
        loss_in = refs[n_par]
        w_refs = refs[n_par + 1: 2 * n_par + 1]
        m_refs = refs[2 * n_par + 1: 3 * n_par + 1]
        v_refs = refs[3 * n_par + 1: 4 * n_par + 1]
        loss_out = refs[4 * n_par + 1]
        out_refs = refs[4 * n_par + 2: 8 * n_par + 2]
        pack, gath, res, send_sems, recv_sems = refs[8 * n_par + 2:]
        me = _my_coords()
        my_idx = _dev_index(*me)

        def fill(slot, srcs):
            pack[slot] = jnp.zeros((rows, LANES), jnp.float32)
            for p, n in enumerate(sizes):
                val = srcs[p][...]
                if val.shape[-1] == LANES and n == 64:
                    pack[slot, row0[p]:row0[p] + 1, :] = val + pltpu.roll(val, 64, 1)
                    continue
                for k, width in pieces(n):
                    pack[slot, row0[p] + k:row0[p] + k + 1, 0:width] = srcs[p][:, LANES * k:LANES * k + width]

        fill(0, g_refs)
        pack[0, loss_row:loss_row + 1, :] = loss_in[0:1, :]
        gath[my_idx] = pack[0]
        sends = []
        for k, f in enumerate(_FLIPS):
            peer = _flip(me, f)
            cp = pltpu.make_async_remote_copy(
                src_ref=pack.at[0], dst_ref=gath.at[my_idx],
                send_sem=send_sems.at[k], recv_sem=recv_sems.at[k],
                device_id=peer, device_id_type=pl.DeviceIdType.MESH)
            cp.start()
            sends.append(cp)
        fill(1, w_refs)
        fill(2, m_refs)
        fill(3, v_refs)
        for k, f in enumerate(_FLIPS):
            peer = _flip(me, f)
            slot = gath.at[_dev_index(*peer)]
            pltpu.make_async_remote_copy(
                src_ref=slot, dst_ref=slot, send_sem=send_sems.at[k], recv_sem=recv_sems.at[k],
                device_id=peer, device_id_type=pl.DeviceIdType.MESH).wait_recv()
        for cp in sends:
            cp.wait_send()
        g = gath[0]
        for d in range(1, N_DEV):
            g = g + gath[d]
        delta, m, v = _adam_math(pack[1], g, pack[2], pack[3])
        res[0], res[1], res[2], res[3] = g, delta, m, v
        loss_out[...] = res[0, loss_row:loss_row + 1, 0:1]
        for p, n in enumerate(sizes):
            for kind in range(4):
                for k, width in pieces(n):
                    out_refs[4 * p + kind][:, LANES * k:LANES * k + width] = (
                        res[kind, row0[p] + k:row0[p] + k + 1, 0:width])

    vm = pl.BlockSpec(memory_space=pltpu.VMEM)
    out_shape = [_sds((1, 1), jnp.float32)]
    for n in sizes:
        out_shape += [_sds((1, n), jnp.float32)] * 4
    outs = _pcall(
        body, name="small_allreduce_adam", out_shape=out_shape,
        in_specs=[vm] * (4 * n_par + 1), out_specs=[vm] * len(out_shape),
        scratch=[pltpu.VMEM((4, rows, LANES), jnp.float32), pltpu.VMEM((N_DEV, rows, LANES), jnp.float32),
                 pltpu.VMEM((4, rows, LANES), jnp.float32),
                 pltpu.SemaphoreType.DMA((7,)), pltpu.SemaphoreType.DMA((7,))])(*grads, loss_tile, *ws, *ms, *vs)
    return outs[0], [outs[1 + 4 * p: 5 + 4 * p] for p in range(n_par)]


def _adam_big(recv, w, m, v, name, after=None, which=None):
    rows, cols = recv.shape[-2:]
    row_tiles = [t for t in range(16, rows + 1, 16) if rows % t == 0 and t * cols <= 400 * 1024]
    tr, tc = (max(row_tiles), cols) if row_tiles else (rows, 512 if cols % 512 == 0 else cols)

    def body(r_ref, w_ref, m_ref, v_ref, g_ref, d_ref, mo_ref, vo_ref):
        g = r_ref[0].astype(jnp.float32)
        for d in range(1, N_DEV):
            g = g + r_ref[d].astype(jnp.float32)
        delta, mn, vn = _adam_math(w_ref[...], g, m_ref[...], v_ref[...])
        g_ref[...] = g
        d_ref[...] = delta
        mo_ref[...] = mn
        vo_ref[...] = vn

    blk = pl.BlockSpec((tr, tc), lambda i, j: (i, j))
    if which is None:
        r_spec = pl.BlockSpec((N_DEV, tr, tc), lambda i, j: (0, i, j))
    else:
        r_spec = pl.BlockSpec((N_DEV, None, tr, tc), lambda i, j: (0, which, i, j))
    return _pcall(
        body, name=name, grid=(rows // tr, cols // tc),
        out_shape=[_sds((rows, cols), jnp.float32)] * 4,
        in_specs=[r_spec, blk, blk, blk],
        out_specs=[blk] * 4, sem=("parallel", "parallel"), after=after)(recv, w, m, v)


def _mm(a, b, *, ta=False, tb=False, out_dtype, tm, tk, name):
    (kdim, mdim) = a.shape if ta else a.shape[::-1]
    ndim = b.shape[0] if tb else b.shape[1]
    tm, tk = min(tm, mdim), min(tk, kdim)
    nk = kdim // tk

    def body(a_ref, b_ref, o_ref, acc):
        k = pl.program_id(1)
        if ta:
            part = _dot_tn(a_ref[...], b_ref[...])
        elif tb:
            part = _dot_nt(a_ref[...], b_ref[...])
        else:
            part = _dot(a_ref[...], b_ref[...])

        @pl.when(k == 0)
        def _():
            acc[...] = part

        @pl.when(k > 0)
        def _():
            acc[...] += part

        @pl.when(k == nk - 1)
        def _():
            o_ref[...] = acc[...].astype(o_ref.dtype)

    a_spec = pl.BlockSpec((tk, tm), lambda i, k: (k, i)) if ta else pl.BlockSpec((tm, tk), lambda i, k: (i, k))
    b_spec = pl.BlockSpec((ndim, tk), lambda i, k: (0, k)) if tb else pl.BlockSpec((tk, ndim), lambda i, k: (k, 0))
    return _pcall(
        body, name=name, grid=(mdim // tm, nk), out_shape=_sds((mdim, ndim), out_dtype),
        in_specs=[a_spec, b_spec], out_specs=pl.BlockSpec((tm, ndim), lambda i, k: (i, 0)),
        scratch=[pltpu.VMEM((tm, ndim), jnp.float32)], sem=("parallel", "arbitrary"))(a, b)


def _ref_col_pieces(start, stop):
    ref_starts = [0, 1024, 1152, 1280, 1792, 2304, 2368, 2880]
    perm_starts = [C_QA, C_KA, C_VA, C_CQ, C_CKV, C_KR, C_QM]
    out = []
    for p in range(7):
        lo, hi = max(start, ref_starts[p]), min(stop, ref_starts[p + 1])
        if lo < hi:
            out.append((lo - start, perm_starts[p] + lo - ref_starts[p], hi - lo))
    return out


def _dw_in(hn, d_proj, n_shard):
    s, d = hn.shape
    n = d_proj.shape[1]
    tm, tk = min(512, d), min(1024, s)
    nk = s // tk

    def body(a_ref, b_ref, o_ref, acc):
        k = pl.program_id(1)
        part = _dot_tn(a_ref[...], b_ref[...])

        @pl.when(k == 0)
        def _():
            acc[...] = part

        @pl.when(k > 0)
        def _():
            acc[...] += part

        @pl.when(k == nk - 1)
        def _():
            t = acc[...].T
            for j in range(N_DEV):
                rows = [t[src:src + width] for _, src, width in _ref_col_pieces(j * n_shard, (j + 1) * n_shard)]
                o_ref[j] = jnp.concatenate(rows, axis=0).astype(o_ref.dtype)

    return _pcall(
        body, name="dw_in", grid=(d // tm, nk), out_shape=_sds((N_DEV, n_shard, d), WIRE),
        in_specs=[pl.BlockSpec((tk, tm), lambda i, k: (k, i)), pl.BlockSpec((tk, n), lambda i, k: (k, 0))],
        out_specs=pl.BlockSpec((N_DEV, n_shard, tm), lambda i, k: (0, 0, i)),
        scratch=[pltpu.VMEM((tm, n), jnp.float32)], sem=("parallel", "arbitrary"))(hn, d_proj)


def _in_proj(x, g, w):
    s, d = x.shape
    n = w.shape[0]
    tm = min(2 * ROW_TILE, s)

    def body(x_ref, g_ref, w_ref, p_ref, hn_ref):
        hn, _, _ = _norm_fwd(x_ref[...], g_ref[...])
        hn_ref[...] = hn.astype(hn_ref.dtype)
        p_ref[...] = _dot_nt(hn, w_ref[...])

    return _pcall(
        body, name="in_proj", grid=(s // tm,),
        out_shape=[_sds((s, n), jnp.float32), _sds((s, d), MXU)],
        in_specs=[pl.BlockSpec((tm, d), lambda i: (i, 0)), pl.BlockSpec((1, d), lambda i: (0, 0)),
                  pl.BlockSpec((n, d), lambda i: (0, 0), pipeline_mode=pl.Buffered(1))],
        out_specs=[pl.BlockSpec((tm, n), lambda i: (i, 0)), pl.BlockSpec((tm, d), lambda i: (i, 0))],
        sem=("parallel",))(x, g, w)


def _mla_prep(proj, cos, sin, g_cq, g_ckv, w_uq, w_ukv, g_qn, g_qr, g_kn, g_kr):
    s = proj.shape[0]
    tm = min(ROW_TILE, s)
    nh = MLA_HEADS

    def body(cq_ref, ckv_ref, kr_ref, cos_ref, sin_ref, gcq_ref, gckv_ref, wuq_ref, wukv_ref,
             gqn_ref, gqr_ref, gkn_ref, gkr_ref,
             qc_ref, kc_ref, v_ref, qb_ref, kvb_ref, cqn_ref, ckvn_ref):
        cos_t, sin_t = cos_ref[...], sin_ref[...]
        lo = _lo_mask((tm, LANES))
        cqn, _, _ = _norm_fwd(cq_ref[...], gcq_ref[...])
        cqn_ref[...] = cqn.astype(cqn_ref.dtype)
        qb = _dot_nt(cqn, wuq_ref[...])
        qb_ref[...] = qb
        ckvn, _, _ = _norm_fwd(ckv_ref[...], gckv_ref[...])
        ckvn_ref[...] = ckvn.astype(ckvn_ref.dtype)
        kvb = jnp.concatenate([_dot(ckvn, wukv_ref[dev]) for dev in range(N_DEV)], axis=1)
        kvb_ref[...] = kvb
        kr, _, _ = _norm_fwd(kr_ref[...], gkr_ref[...], half=True)
        kr = _rope(kr, cos_t, sin_t)
        kr2 = jnp.where(lo, kr, pltpu.roll(kr, 64, 1))
        ropes = []
        for j in range(nh // 2):
            xr = qb[:, nh * MLA_NOPE + LANES * j: nh * MLA_NOPE + LANES * (j + 1)]
            qr, _, _ = _norm_fwd(xr, gqr_ref[...], half=True)
            ropes.append(_rope(qr, cos_t, sin_t))
        for h in range(nh):
            qn, _, _ = _norm_fwd(qb[:, MLA_NOPE * h: MLA_NOPE * (h + 1)], gqn_ref[...])
            mask = lo if h % 2 == 0 else jnp.logical_not(lo)
            qr = jnp.where(mask, ropes[h // 2], 0.0)
            qc_ref[h] = jnp.concatenate([qn, qr], axis=1).astype(qc_ref.dtype)
            kn, _, _ = _norm_fwd(kvb[:, 256 * h: 256 * h + MLA_NOPE], gkn_ref[...])
            kc_ref[h] = jnp.concatenate([kn, kr2], axis=1).astype(kc_ref.dtype)
            v_ref[h] = kvb[:, 256 * h + MLA_NOPE: 256 * (h + 1)].astype(v_ref.dtype)

    def col(width, start):
        return pl.BlockSpec((tm, width), lambda i: (i, start // width))

    def full(shape):
        return pl.BlockSpec(shape, lambda i: (0,) * len(shape))

    def row(width):
        return pl.BlockSpec((tm, width), lambda i: (i, 0))

    def heads(width):
        return pl.BlockSpec((nh, tm, width), lambda i: (0, i, 0))

    return _pcall(
        body, name="mla_prep", grid=(s // tm,),
        out_shape=[_sds((nh, s, 256), MXU), _sds((nh, s, 256), MXU), _sds((nh, s, MLA_V), MXU),
                   _sds((s, 768), jnp.float32), _sds((s, 1024), jnp.float32),
                   _sds((s, 512), MXU), _sds((s, 512), MXU)],
        in_specs=[col(512, C_CQ), col(512, C_CKV), col(LANES, C_KR), row(LANES), row(LANES),
                  full((1, 512)), full((1, 512)), full((768, 512)), full((N_DEV, 512, LANES)),
                  full((1, LANES)), full((1, LANES)), full((1, LANES)), full((1, LANES))],
        out_specs=[heads(256), heads(256), heads(MLA_V), row(768), row(1024), row(512), row(512)],
        sem=("parallel",))(proj, proj, proj, cos, sin, g_cq, g_ckv, w_uq, w_ukv, g_qn, g_qr, g_kn, g_kr)


def _mla_fwd(qc, kc, v):
    nh, s, _ = qc.shape
    t = min(ATT_TILE, s)
    nb = s // t
    scale = (MLA_NOPE + MLA_ROPE) ** -0.5

    def body(q_ref, k_ref, v_ref, y_ref, lse_ref, m_sc, l_sc, acc):
        qi, ki = pl.program_id(1), pl.program_id(2)

        @pl.when(ki == 0)
        def _():
            m_sc[...] = jnp.full_like(m_sc, NEG_INF)
            l_sc[...] = jnp.zeros_like(l_sc)
            acc[...] = jnp.zeros_like(acc)

        def step(diagonal):
            rc = t // 4 if diagonal else t
            for c in range(t // rc):
                rows = slice(rc * c, rc * (c + 1))
                keys = slice(0, rc * (c + 1))
                sc = _dot_nt(q_ref[0, rows, :], k_ref[0, keys, :]) * (scale * LOG2E)
                if diagonal:
                    r_i = lax.broadcasted_iota(jnp.int32, sc.shape, 0) + rc * c
                    c_i = lax.broadcasted_iota(jnp.int32, sc.shape, 1)
                    sc = jnp.where(c_i <= r_i, sc, NEG_INF)
                m_old = m_sc[rows, :]
                m_new = jnp.maximum(m_old, jnp.max(sc, -1, keepdims=True))
                alpha = jnp.exp2(m_old - m_new)
                p = jnp.exp2(sc - m_new)
                l_sc[rows, :] = alpha * l_sc[rows, :] + jnp.sum(p, -1, keepdims=True)
                acc[rows, :] = alpha * acc[rows, :] + _dot(p, v_ref[0, keys, :])
                m_sc[rows, :] = m_new

        @pl.when(ki < qi)
        def _():
            step(False)

        @pl.when(ki == qi)
        def _():
            step(True)

        @pl.when(ki == qi)
        def _():
            y_ref[...] = acc[...] / l_sc[...]
            lse_ref[0] = m_sc[...] + jnp.log2(l_sc[...])

    return _pcall(
        body, name="mla_fwd", grid=(nh, nb, nb),
        out_shape=[_sds((s, nh * MLA_V), jnp.float32), _sds((nh, s, 1), jnp.float32)],
        in_specs=[pl.BlockSpec((1, t, 256), lambda h, i, k: (h, i, 0)),
                  pl.BlockSpec((1, t, 256), lambda h, i, k: (h, jnp.minimum(k, i), 0)),
                  pl.BlockSpec((1, t, MLA_V), lambda h, i, k: (h, jnp.minimum(k, i), 0))],
        out_specs=[pl.BlockSpec((t, MLA_V), lambda h, i, k: (i, h)),
                   pl.BlockSpec((1, t, 1), lambda h, i, k: (h, i, 0))],
        scratch=[pltpu.VMEM((t, 1), jnp.float32), pltpu.VMEM((t, 1), jnp.float32),
                 pltpu.VMEM((t, MLA_V), jnp.float32)],
        sem=("parallel", "parallel", "arbitrary"))(qc, kc, v)


def _memkv_prep(mem, g_mem, w_mkv, g_mk):
    ml, d = mem.shape
    hw = MEM_HEADS * MEM_DIM

    def body(mem_ref, g_ref, w_ref, gk_ref, k_ref, v_ref, kv_ref, mn_ref):
        mn, _, _ = _norm_fwd(mem_ref[...], g_ref[...])
        mn_ref[...] = mn.astype(mn_ref.dtype)
        kv = _dot(mn, w_ref[...])
        kv_ref[...] = kv
        for h in range(MEM_HEADS):
            kn, _, _ = _norm_fwd(kv[:, MEM_DIM * h: MEM_DIM * (h + 1)], gk_ref[...])
            k_ref[:, MEM_DIM * h: MEM_DIM * (h + 1)] = kn.astype(k_ref.dtype)
        v_ref[...] = kv[:, hw:].astype(v_ref.dtype)

    vm = pl.BlockSpec(memory_space=pltpu.VMEM)
    return _pcall(
        body, name="memkv_prep",
        out_shape=[_sds((ml, hw), MXU), _sds((ml, hw), MXU), _sds((ml, 2 * hw), jnp.float32), _sds((ml, d), MXU)],
        in_specs=[vm] * 4, out_specs=[vm] * 4)(mem, g_mem, w_mkv, g_mk)


def _mem_fwd(proj, g_mq, km, vmm):
    s = proj.shape[0]
    ml, hw = km.shape
    tm = min(FFN_TILE, s)
    scale = MEM_DIM ** -0.5

    def body(q_ref, g_ref, k_ref, v_ref, y_ref, lse_ref):
        col = lax.broadcasted_iota(jnp.int32, (tm, MEM_HEADS), 1)
        lse_t = jnp.zeros((tm, MEM_HEADS), jnp.float32)
        for h in range(MEM_HEADS):
            sl = slice(MEM_DIM * h, MEM_DIM * (h + 1))
            qn, _, _ = _norm_fwd(q_ref[:, sl], g_ref[...])
            sc = _dot_nt(qn, k_ref[:, sl]) * scale
            m = jnp.max(sc, -1, keepdims=True)
            p = jnp.exp(sc - m)
            l = jnp.sum(p, -1, keepdims=True)
            y_ref[:, sl] = _dot(p, v_ref[:, sl]) / l
            lse_t = jnp.where(col == h, m + jnp.log(l), lse_t)
        lse_ref[...] = lse_t

    return _pcall(
        body, name="mem_fwd", grid=(s // tm,),
        out_shape=[_sds((s, hw), jnp.float32), _sds((s, MEM_HEADS), jnp.float32)],
        in_specs=[pl.BlockSpec((tm, hw), lambda i: (i, C_QM // hw)), pl.BlockSpec((1, MEM_DIM), lambda i: (0, 0)),
                  pl.BlockSpec((ml, hw), lambda i: (0, 0)), pl.BlockSpec((ml, hw), lambda i: (0, 0))],
        out_specs=[pl.BlockSpec((tm, hw), lambda i: (i, 0)), pl.BlockSpec((tm, MEM_HEADS), lambda i: (i, 0))],
        sem=("parallel",))(proj, g_mq, km, vmm)


def _alibi_slope(h):
    return float(2.0 ** (-8.0 * (h + 1) / SWA_Q_HEADS))


def _swa_common(n, kp, kc, vp, vc, pq, pkp, pkc, gk):
    b = SWA_BLOCK
    k_raw = jnp.concatenate([kp, kc], axis=0)
    kn, kxn, kr = _norm_fwd(k_raw, gk, half=True)
    v = jnp.concatenate([vp, vc], axis=0)
    dist = jnp.abs(pq - jnp.concatenate([pkp, pkc], axis=1))
    r_i = lax.broadcasted_iota(jnp.int32, (b, 2 * b), 0)
    c_i = lax.broadcasted_iota(jnp.int32, (b, 2 * b), 1)
    valid = (c_i > r_i) & (c_i <= r_i + b) & (c_i >= jnp.where(n > 0, 0, b))
    bias = jnp.where(valid, -dist, NEG_INF)
    return kn, v, bias


def _swa_specs(s):
    b = SWA_BLOCK
    prev = lambda n: jnp.maximum(n - 1, 0)
    return [
        pl.BlockSpec((b, 1024), lambda n: (n, C_QA // 1024)),
        pl.BlockSpec((b, LANES), lambda n: (prev(n), C_KA // LANES)),
        pl.BlockSpec((b, LANES), lambda n: (n, C_KA // LANES)),
        pl.BlockSpec((b, LANES), lambda n: (prev(n), C_VA // LANES)),
        pl.BlockSpec((b, LANES), lambda n: (n, C_VA // LANES)),
        pl.BlockSpec((b, 1), lambda n: (n, 0)),
        pl.BlockSpec((1, b), lambda n: (0, prev(n))),
        pl.BlockSpec((1, b), lambda n: (0, n)),
        pl.BlockSpec((1, LANES), lambda n: (0, 0)),
        pl.BlockSpec((1, LANES), lambda n: (0, 0)),
        pl.BlockSpec(memory_space=pltpu.SMEM),
    ]


def _swa_fwd(proj, posc, posr, gq, gk, sinks):
    s = proj.shape[0]
    b = SWA_BLOCK
    scale = SWA_DIM ** -0.5

    def body(q_ref, kp_ref, kc_ref, vp_ref, vc_ref, pq_ref, pkp_ref, pkc_ref, gq_ref, gk_ref, sink_ref,
             y_ref, lse_ref):
        n = pl.program_id(0)
        kn, v, bias = _swa_common(n, kp_ref[...], kc_ref[...], vp_ref[...], vc_ref[...],
                                  pq_ref[...], pkp_ref[...], pkc_ref[...], gk_ref[...])
        lo = _lo_mask((b, LANES))
        col = lax.broadcasted_iota(jnp.int32, (b, SWA_Q_HEADS), 1)
        lse_t = jnp.zeros((b, SWA_Q_HEADS), jnp.float32)
        hpg = SWA_Q_HEADS // SWA_KV_HEADS
        for g in range(SWA_KV_HEADS):
            heads = range(hpg * g, hpg * (g + 1))
            kvmask = lo if g == 0 else jnp.logical_not(lo)
            qs = []
            for j in range(hpg // 2 * g, hpg // 2 * (g + 1)):
                qn, _, _ = _norm_fwd(q_ref[:, LANES * j: LANES * (j + 1)], gq_ref[...], half=True)
                qn = qn * scale
                qsw = pltpu.roll(qn, 64, 1)
                qs += [jnp.where(kvmask, qn if e == g else qsw, 0.0) for e in range(2)]
            sc_st = _dot_nt(jnp.concatenate(qs, axis=0), kn)
            ps, ls = [], []
            for i, h in enumerate(heads):
                sc = sc_st[b * i: b * (i + 1)] + _alibi_slope(h) * bias
                sk = sink_ref[h]
                m = jnp.maximum(jnp.max(sc, -1, keepdims=True), sk)
                p = jnp.exp(sc - m)
                l = jnp.sum(p, -1, keepdims=True) + jnp.exp(sk - m)
                ps.append(p.astype(MXU))
                ls.append(l)
                lse_t = jnp.where(col == h, m + jnp.log(l), lse_t)
            o_st = _dot(jnp.concatenate(ps, axis=0), v)
            for j in range(hpg // 2 * g, hpg // 2 * (g + 1)):
                halves = []
                for e in range(2):
                    i = 2 * j + e - hpg * g
                    o_h = o_st[b * i: b * (i + 1)] / ls[i]
                    halves.append(o_h if e == g else pltpu.roll(o_h, 64, 1))
                y_ref[:, LANES * j: LANES * (j + 1)] = jnp.where(lo, halves[0], halves[1])
        lse_ref[...] = lse_t

    return _pcall(
        body, name="swa_fwd", grid=(s // b,),
        out_shape=[_sds((s, 1024), jnp.float32), _sds((s, SWA_Q_HEADS), jnp.float32)],
        in_specs=_swa_specs(s),
        out_specs=[pl.BlockSpec((b, 1024), lambda n: (n, 0)), pl.BlockSpec((b, SWA_Q_HEADS), lambda n: (n, 0))],
        sem=("parallel",))(proj, proj, proj, proj, proj, posc, posr, posr, gq, gk, sinks)


def _out_proj(y_a, y_b, y_m, x, w_out, g_ffn):
    s, d = x.shape
    tm = min(2 * ROW_TILE, s)

    def body(ya_ref, yb_ref, ym_ref, x_ref, w_ref, g_ref, h1_ref, fn_ref):
        y = jnp.concatenate([ya_ref[...].astype(MXU), yb_ref[...].astype(MXU), ym_ref[...].astype(MXU)], axis=1)
        h1 = x_ref[...] + _dot(y, w_ref[...])
        h1_ref[...] = h1
        fn, _, _ = _norm_fwd(h1, g_ref[...])
        fn_ref[...] = fn.astype(fn_ref.dtype)

    def row(width):
        return pl.BlockSpec((tm, width), lambda i: (i, 0))

    return _pcall(
        body, name="out_proj", grid=(s // tm,),
        out_shape=[_sds((s, d), jnp.float32), _sds((s, d), MXU)],
        in_specs=[row(1024), row(512), row(512), row(d),
                  pl.BlockSpec(w_out.shape, lambda i: (0, 0), pipeline_mode=pl.Buffered(1)),
                  pl.BlockSpec((1, d), lambda i: (0, 0))],
        out_specs=[row(d), row(d)], sem=("parallel",))(y_a, y_b, y_m, x, w_out, g_ffn)


def _ffn_gu(fn, w_gu):
    s, d = fn.shape
    f = w_gu.shape[2]
    tm = min(2 * FFN_TILE, s)

    def body(fn_ref, w_ref, gu_ref, act_ref):
        x = fn_ref[...]
        g = _dot_nt(x, w_ref[0, 0])
        u = _dot_nt(x, w_ref[0, 1])
        gu_ref[0, 0] = g
        gu_ref[0, 1] = u
        act_ref[0] = (g * jax.nn.sigmoid(g) * u).astype(act_ref.dtype)

    return _pcall(
        body, name="ffn_gate_up", grid=(N_DEV, s // tm),
        out_shape=[_sds((N_DEV, 2, s, f), jnp.float32), _sds((N_DEV, s, f), MXU)],
        in_specs=[pl.BlockSpec((tm, d), lambda j, i: (i, 0)),
                  pl.BlockSpec((1, 2, f, d), lambda j, i: (j, 0, 0, 0))],
        out_specs=[pl.BlockSpec((1, 2, tm, f), lambda j, i: (j, 0, i, 0)),
                   pl.BlockSpec((1, tm, f), lambda j, i: (j, i, 0))],
        sem=("parallel", "parallel"))(fn, w_gu)


def _ffn_down(act, w_d, h1, target):
    _, s, f = act.shape
    d = h1.shape[1]
    tm = min(FFN_TILE, s)

    def body(a_ref, w_ref, h1_ref, t_ref, dout_ref, doutb_ref, loss_ref, acc):
        i, j = pl.program_id(0), pl.program_id(1)
        part = _dot(a_ref[0], w_ref[0]) + _dot(a_ref[1], w_ref[1])

        @pl.when(j == 0)
        def _():
            acc[...] = h1_ref[...] + part

        @pl.when(j > 0)
        def _():
            acc[...] += part

        @pl.when((i == 0) & (j == 0))
        def _():
            loss_ref[...] = jnp.zeros_like(loss_ref)

        @pl.when(j == N_DEV // 2 - 1)
        def _():
            diff = acc[...] - t_ref[...]
            dout_ref[...] = diff / d
            doutb_ref[...] = (diff / d).astype(doutb_ref.dtype)
            loss_ref[...] += 0.5 * jnp.sum(jnp.sum(diff * diff, -1, keepdims=True) / d)

    row = pl.BlockSpec((tm, d), lambda i, j: (i, 0))
    return _pcall(
        body, name="ffn_down", grid=(s // tm, N_DEV // 2),
        out_shape=[_sds((s, d), jnp.float32), _sds((s, d), MXU), _sds((8, LANES), jnp.float32)],
        in_specs=[pl.BlockSpec((2, tm, f), lambda i, j: (j, i, 0)), pl.BlockSpec((2, f, d), lambda i, j: (j, 0, 0)),
                  row, row],
        out_specs=[row, row, pl.BlockSpec((8, LANES), lambda i, j: (0, 0))],
        scratch=[pltpu.VMEM((tm, d), jnp.float32)], sem=("arbitrary", "arbitrary"))(act, w_d, h1, target)


def _ffn_bwd_act(dout, w_d, gu):
    s, d = dout.shape
    f = w_d.shape[1]
    tm = min(2 * FFN_TILE, s)
    ni = s // tm

    def body(do_ref, w_ref, gu_ref, dgu_ref, dw_ref, acc):
        i = pl.program_id(1)
        do = do_ref[...]
        d_act = _dot_nt(do, w_ref[0])
        g, u = gu_ref[0, 0], gu_ref[0, 1]
        sig = jax.nn.sigmoid(g)
        silu = g * sig
        dgu_ref[0, 0] = (d_act * u * (sig * (1.0 + g * (1.0 - sig)))).astype(dgu_ref.dtype)
        dgu_ref[0, 1] = (d_act * silu).astype(dgu_ref.dtype)
        part = _dot_tn(silu * u, do)

        @pl.when(i == 0)
        def _():
            acc[...] = part

        @pl.when(i > 0)
        def _():
            acc[...] += part

        @pl.when(i == ni - 1)
        def _():
            dw_ref[0] = acc[...].astype(dw_ref.dtype)

    return _pcall(
        body, name="ffn_bwd_act", grid=(N_DEV, ni),
        out_shape=[_sds((N_DEV, 2, s, f), MXU), _sds((N_DEV, f, d), WIRE)],
        in_specs=[pl.BlockSpec((tm, d), lambda j, i: (i, 0)), pl.BlockSpec((1, f, d), lambda j, i: (j, 0, 0)),
                  pl.BlockSpec((1, 2, tm, f), lambda j, i: (j, 0, i, 0))],
        out_specs=[pl.BlockSpec((1, 2, tm, f), lambda j, i: (j, 0, i, 0)),
                   pl.BlockSpec((1, f, d), lambda j, i: (j, 0, 0))],
        scratch=[pltpu.VMEM((f, d), jnp.float32)], sem=("parallel", "arbitrary"))(dout, w_d, gu)


def _ffn_dw_gu(fn, dgu):
    s, d = fn.shape
    f = dgu.shape[-1]
    tk = min(4 * FFN_TILE, s)
    nk = s // tk

    def body(fn_ref, dgu_ref, dw_ref, acc):
        k = pl.program_id(2)
        part = _dot_tn(dgu_ref[0, 0], fn_ref[...])

        @pl.when(k == 0)
        def _():
            acc[...] = part

        @pl.when(k > 0)
        def _():
            acc[...] += part

        @pl.when(k == nk - 1)
        def _():
            dw_ref[0, 0] = acc[...].astype(dw_ref.dtype)

    return _pcall(
        body, name="ffn_dw_gate_up", grid=(N_DEV, 2, nk),
        out_shape=_sds((N_DEV, 2, f, d), WIRE),
        in_specs=[pl.BlockSpec((tk, d), lambda j, w, k: (k, 0)),
                  pl.BlockSpec((1, 1, tk, f), lambda j, w, k: (j, w, k, 0))],
        out_specs=pl.BlockSpec((1, 1, f, d), lambda j, w, k: (j, w, 0, 0)),
        scratch=[pltpu.VMEM((f, d), jnp.float32)], sem=("parallel", "parallel", "arbitrary"))(fn, dgu)


def _ffn_dfn(dgu, w_gu, after):
    _, _, s, f = dgu.shape
    d = w_gu.shape[3]
    tm = min(FFN_TILE, s)

    def body(dgu_ref, w_ref, dfn_ref):
        j = pl.program_id(1)
        part = (_dot(dgu_ref[0, 0], w_ref[0, 0]) + _dot(dgu_ref[0, 1], w_ref[0, 1])
                + _dot(dgu_ref[1, 0], w_ref[1, 0]) + _dot(dgu_ref[1, 1], w_ref[1, 1]))

        @pl.when(j == 0)
        def _():
            dfn_ref[...] = part

        @pl.when(j > 0)
        def _():
            dfn_ref[...] += part

    return _pcall(
        body, name="ffn_dfn", grid=(s // tm, N_DEV // 2),
        out_shape=_sds((s, d), jnp.float32),
        in_specs=[pl.BlockSpec((2, 2, tm, f), lambda i, j: (j, 0, i, 0)),
                  pl.BlockSpec((2, 2, f, d), lambda i, j: (j, 0, 0, 0))],
        out_specs=pl.BlockSpec((tm, d), lambda i, j: (i, 0)),
        sem=("parallel", "arbitrary"), after=after)(dgu, w_gu)


def _ffn_norm_bwd(d_fn, dout, h1, g_ffn):
    s, d = h1.shape
    tm = min(2 * ROW_TILE, s)

    def body(dfn_ref, do_ref, h1_ref, g_ref, dh1_ref, dg_ref):
        i = pl.program_id(0)

        @pl.when(i == 0)
        def _():
            dg_ref[...] = jnp.zeros_like(dg_ref)

        _, xn, r = _norm_fwd(h1_ref[...], g_ref[...])
        dx, dg = _norm_bwd(xn, r, g_ref[...], dfn_ref[...])
        dh1_ref[...] = do_ref[...] + dx
        dg_ref[...] += dg

    row = pl.BlockSpec((tm, d), lambda i: (i, 0))
    vec = pl.BlockSpec((1, d), lambda i: (0, 0))
    return _pcall(
        body, name="ffn_norm_bwd", grid=(s // tm,),
        out_shape=[_sds((s, d), jnp.float32), _sds((1, d), jnp.float32)],
        in_specs=[row, row, row, vec], out_specs=[row, vec], sem=("arbitrary",))(d_fn, dout, h1, g_ffn)


def _mem_bwd(proj, g_mq, km, vmm, d_y, y_m, lse):
    s = proj.shape[0]
    ml, hw = km.shape
    tm = min(FFN_TILE, s)
    scale = MEM_DIM ** -0.5

    def body(q_ref, g_ref, k_ref, v_ref, do_ref, y_ref, lse_ref, dq_ref, dk_ref, dv_ref, dg_ref):
        i = pl.program_id(0)

        @pl.when(i == 0)
        def _():
            dk_ref[...] = jnp.zeros_like(dk_ref)
            dv_ref[...] = jnp.zeros_like(dv_ref)
            dg_ref[...] = jnp.zeros_like(dg_ref)

        col = lax.broadcasted_iota(jnp.int32, (tm, MEM_HEADS), 1)
        lse_t = lse_ref[...]
        for h in range(MEM_HEADS):
            sl = slice(MEM_DIM * h, MEM_DIM * (h + 1))
            qn, xn, r = _norm_fwd(q_ref[:, sl], g_ref[...])
            lse_h = jnp.sum(jnp.where(col == h, lse_t, 0.0), -1, keepdims=True)
            p = jnp.exp(_dot_nt(qn, k_ref[:, sl]) * scale - lse_h)
            do = do_ref[:, sl]
            dd = jnp.sum(do * y_ref[:, sl], -1, keepdims=True)
            dp = _dot_nt(do, v_ref[:, sl])
            ds = (p * (dp - dd)).astype(MXU)
            dv_ref[:, sl] += _dot_tn(p, do)
            dk_ref[:, sl] += _dot_tn(ds, qn) * scale
            dx, dg = _norm_bwd(xn, r, g_ref[...], _dot(ds, k_ref[:, sl]) * scale)
            dq_ref[:, sl] = dx.astype(dq_ref.dtype)
            dg_ref[...] += dg

    full = pl.BlockSpec((ml, hw), lambda i: (0, 0))
    return _pcall(
        body, name="mem_bwd", grid=(s // tm,),
        out_shape=[_sds((s, hw), MXU), _sds((ml, hw), jnp.float32), _sds((ml, hw), jnp.float32),
                   _sds((1, MEM_DIM), jnp.float32)],
        in_specs=[pl.BlockSpec((tm, hw), lambda i: (i, C_QM // hw)), pl.BlockSpec((1, MEM_DIM), lambda i: (0, 0)),
                  full, full, pl.BlockSpec((tm, hw), lambda i: (i, 3)), pl.BlockSpec((tm, hw), lambda i: (i, 0)),
                  pl.BlockSpec((tm, MEM_HEADS), lambda i: (i, 0))],
        out_specs=[pl.BlockSpec((tm, hw), lambda i: (i, 0)), full, full,
                   pl.BlockSpec((1, MEM_DIM), lambda i: (0, 0))],
        sem=("arbitrary",))(proj, g_mq, km, vmm, d_y, y_m, lse)


def _memkv_bwd(mem, g_mem, w_mkv, g_mk, kv, memn, dk, dv):
    ml, d = mem.shape
    hw = MEM_HEADS * MEM_DIM

    def body(mem_ref, g_ref, w_ref, gk_ref, kv_ref, mn_ref, dk_ref, dv_ref, dw_ref, dgm_ref, dgk_ref):
        parts = []
        dgk = jnp.zeros((1, MEM_DIM), jnp.float32)
        for h in range(MEM_HEADS):
            sl = slice(MEM_DIM * h, MEM_DIM * (h + 1))
            _, xn, r = _norm_fwd(kv_ref[:, sl], gk_ref[...])
            dx, dg = _norm_bwd(xn, r, gk_ref[...], dk_ref[:, sl])
            parts.append(dx)
            dgk = dgk + dg
        dkv = jnp.concatenate(parts + [dv_ref[...]], axis=1).astype(MXU)
        dgk_ref[...] = dgk
        dw_ref[...] = _dot_tn(mn_ref[...], dkv).astype(dw_ref.dtype)
        d_mn = _dot_nt(dkv, w_ref[...])
        _, xn, _ = _norm_fwd(mem_ref[...], g_ref[...])
        dgm_ref[...] = jnp.sum(d_mn * xn, 0, keepdims=True)

    vm = pl.BlockSpec(memory_space=pltpu.VMEM)
    return _pcall(
        body, name="memkv_bwd",
        out_shape=[_sds((d, 2 * hw), WIRE), _sds((1, d), jnp.float32), _sds((1, MEM_DIM), jnp.float32)],
        in_specs=[vm] * 8, out_specs=[vm] * 3)(mem, g_mem, w_mkv, g_mk, kv, memn, dk, dv)


def _mla_bwd(qc, kc, v, d_y, y_b, lse, after):
    nh, s, _ = qc.shape
    t = min(ATT_TILE, s)
    nb = s // t
    scale = (MLA_NOPE + MLA_ROPE) ** -0.5

    def body(q_ref, k_ref, v_ref, do_ref, y_ref, lse_ref, dq_ref, dk_ref, dv_ref, dk_acc, dv_acc):
        kj, qi = pl.program_id(1), pl.program_id(2)

        @pl.when((kj == 0) & (qi == 0))
        def _():
            dq_ref[...] = jnp.zeros_like(dq_ref)

        @pl.when(qi == kj)
        def _():
            dk_acc[...] = jnp.zeros_like(dk_acc)
            dv_acc[...] = jnp.zeros_like(dv_acc)

        def step(diagonal):
            rc = t // 4 if diagonal else t
            for c in range(t // rc):
                rows = slice(rc * c, rc * (c + 1))
                keys = slice(0, rc * (c + 1))
                q, k = q_ref[0, rows, :], k_ref[0, keys, :]
                sc = _dot_nt(q, k) * (scale * LOG2E)
                if diagonal:
                    r_i = lax.broadcasted_iota(jnp.int32, sc.shape, 0) + rc * c
                    c_i = lax.broadcasted_iota(jnp.int32, sc.shape, 1)
                    sc = jnp.where(c_i <= r_i, sc, NEG_INF)
                p = jnp.exp2(sc - lse_ref[0, rows, :])
                do = do_ref[rows, :]
                dd = jnp.sum(do * y_ref[rows, :], -1, keepdims=True)
                dp = _dot_nt(do, v_ref[0, keys, :])
                ds = (p * (dp - dd) * scale).astype(MXU)
                dv_acc[keys, :] += _dot_tn(p, do)
                dk_acc[keys, :] += _dot_tn(ds, q)
                out_rows = pl.ds(pl.multiple_of(qi * t + rc * c, rc), rc)
                dq_ref[0, out_rows, :] += _dot(ds, k)

        @pl.when(qi > kj)
        def _():
            step(False)

        @pl.when(qi == kj)
        def _():
            step(True)

        @pl.when(qi == nb - 1)
        def _():
            dk_ref[0] = dk_acc[...]
            dv_ref[0] = dv_acc[...]

    qmap = lambda h, j, i: (h, jnp.maximum(i, j), 0)
    return _pcall(
        body, name="mla_bwd", grid=(nh, nb, nb),
        out_shape=[_sds((nh, s, 256), jnp.float32), _sds((nh, s, 256), jnp.float32),
                   _sds((nh, s, MLA_V), jnp.float32)],
        in_specs=[pl.BlockSpec((1, t, 256), qmap),
                  pl.BlockSpec((1, t, 256), lambda h, j, i: (h, j, 0)),
                  pl.BlockSpec((1, t, MLA_V), lambda h, j, i: (h, j, 0)),
                  pl.BlockSpec((t, MLA_V), lambda h, j, i: (jnp.maximum(i, j), 8 + h)),
                  pl.BlockSpec((t, MLA_V), lambda h, j, i: (jnp.maximum(i, j), h)),
                  pl.BlockSpec((1, t, 1), qmap)],
        out_specs=[pl.BlockSpec((1, s, 256), lambda h, j, i: (h, 0, 0)),
                   pl.BlockSpec((1, t, 256), lambda h, j, i: (h, j, 0)),
                   pl.BlockSpec((1, t, MLA_V), lambda h, j, i: (h, j, 0))],
        scratch=[pltpu.VMEM((t, 256), jnp.float32), pltpu.VMEM((t, MLA_V), jnp.float32)],
        sem=("parallel", "arbitrary", "arbitrary"), after=after)(qc, kc, v, d_y, y_b, lse)


def _mla_prep_bwd(proj, cos, sin, g_cq, g_ckv, w_uq, w_ukv, g_qn, g_qr, g_kn, g_kr,
                  qb, kvb, cqn, ckvn, dqc, dkc, dv):
    s = proj.shape[0]
    tm = min(ROW_TILE, s)
    nh = MLA_HEADS
    ni = s // tm

    def body(cq_ref, ckv_ref, kr_ref, cos_ref, sin_ref, gcq_ref, gckv_ref, wuq_ref, wukv_ref,
             gqn_ref, gqr_ref, gkn_ref, gkr_ref, qb_ref, kvb_ref, cqn_ref, ckvn_ref, dqc_ref, dkc_ref, dv_ref,
             dcq_ref, dckv_ref, dkr_ref, dwuq_ref, dwukv_ref,
             dgcq_ref, dgckv_ref, dgqn_ref, dgqr_ref, dgkn_ref, dgkr_ref, acc_uq, acc_ukv):
        i = pl.program_id(0)

        @pl.when(i == 0)
        def _():
            acc_uq[...] = jnp.zeros_like(acc_uq)
            acc_ukv[...] = jnp.zeros_like(acc_ukv)
            for ref in (dgcq_ref, dgckv_ref, dgqn_ref, dgqr_ref, dgkn_ref, dgkr_ref):
                ref[...] = jnp.zeros_like(ref)

        cos_t, sin_t = cos_ref[...], sin_ref[...]
        lo = _lo_mask((tm, LANES))
        qb_v, kvb_v = qb_ref[...], kvb_ref[...]
        dq_parts, dgqn = [], jnp.zeros((1, LANES), jnp.float32)
        for h in range(nh):
            _, xn, r = _norm_fwd(qb_v[:, MLA_NOPE * h: MLA_NOPE * (h + 1)], gqn_ref[...])
            dx, dg = _norm_bwd(xn, r, gqn_ref[...], dqc_ref[h][:, :MLA_NOPE])
            dq_parts.append(dx)
            dgqn = dgqn + dg
        dgqn_ref[...] += dgqn
        dgqr = jnp.zeros((1, LANES), jnp.float32)
        for j in range(nh // 2):
            d_rope = jnp.where(lo, dqc_ref[2 * j][:, MLA_NOPE:], dqc_ref[2 * j + 1][:, MLA_NOPE:])
            d_pre = _rope_bwd(d_rope, cos_t, sin_t)
            xr = qb_v[:, nh * MLA_NOPE + LANES * j: nh * MLA_NOPE + LANES * (j + 1)]
            _, xn, r = _norm_fwd(xr, gqr_ref[...], half=True)
            dx, dg = _norm_bwd(xn, r, gqr_ref[...], d_pre, half=True)
            dq_parts.append(dx)
            dgqr = dgqr + dg
        dgqr_ref[...] += dgqr
        dqb = jnp.concatenate(dq_parts, axis=1).astype(MXU)
        acc_uq[...] += _dot_tn(dqb, cqn_ref[...])
        _, xn, r = _norm_fwd(cq_ref[...], gcq_ref[...])
        dx, dg = _norm_bwd(xn, r, gcq_ref[...], _dot(dqb, wuq_ref[...]))
        dcq_ref[...] = dx.astype(dcq_ref.dtype)
        dgcq_ref[...] += dg
        dkv_parts, dgkn = [], jnp.zeros((1, LANES), jnp.float32)
        d_kr2 = jnp.zeros((tm, LANES), jnp.float32)
        for h in range(nh):
            _, xn, r = _norm_fwd(kvb_v[:, 256 * h: 256 * h + MLA_NOPE], gkn_ref[...])
            dx, dg = _norm_bwd(xn, r, gkn_ref[...], dkc_ref[h][:, :MLA_NOPE])
            dkv_parts += [dx, dv_ref[h]]
            dgkn = dgkn + dg
            d_kr2 = d_kr2 + dkc_ref[h][:, MLA_NOPE:]
        dgkn_ref[...] += dgkn
        dkvb = jnp.concatenate(dkv_parts, axis=1).astype(MXU)
        d_ckvn = jnp.zeros((tm, 512), jnp.float32)
        for dev in range(N_DEV):
            piece = dkvb[:, LANES * dev: LANES * (dev + 1)]
            acc_ukv[dev] += _dot_tn(ckvn_ref[...], piece)
            d_ckvn = d_ckvn + _dot_nt(piece, wukv_ref[dev])
        _, xn, r = _norm_fwd(ckv_ref[...], gckv_ref[...])
        dx, dg = _norm_bwd(xn, r, gckv_ref[...], d_ckvn)
        dckv_ref[...] = dx.astype(dckv_ref.dtype)
        dgckv_ref[...] += dg
        d_kr = jnp.where(lo, d_kr2 + pltpu.roll(d_kr2, 64, 1), 0.0)
        d_pre = _rope_bwd(d_kr, cos_t, sin_t)
        _, xn, r = _norm_fwd(kr_ref[...], gkr_ref[...], half=True)
        dx, dg = _norm_bwd(xn, r, gkr_ref[...], d_pre, half=True)
        dkr_ref[...] = jnp.where(lo, dx, 0.0).astype(dkr_ref.dtype)
        dgkr_ref[...] += jnp.where(_lo_mask((1, LANES)), dg, 0.0)

        @pl.when(i == ni - 1)
        def _():
            dwuq_ref[...] = acc_uq[...].astype(dwuq_ref.dtype)
            dwukv_ref[...] = acc_ukv[...].astype(dwukv_ref.dtype)

    def col(width, start):
        return pl.BlockSpec((tm, width), lambda i: (i, start // width))

    def full(shape):
        return pl.BlockSpec(shape, lambda i: (0,) * len(shape))

    def row(width):
        return pl.BlockSpec((tm, width), lambda i: (i, 0))

    def heads(width):
        return pl.BlockSpec((nh, tm, width), lambda i: (0, i, 0))

    vec = full((1, LANES))
    return _pcall(
        body, name="mla_prep_bwd", grid=(ni,),
        out_shape=[_sds((s, 512), MXU), _sds((s, 512), MXU), _sds((s, LANES), MXU),
                   _sds((768, 512), WIRE), _sds((N_DEV, 512, LANES), WIRE),
                   _sds((1, 512), jnp.float32), _sds((1, 512), jnp.float32)] + [_sds((1, LANES), jnp.float32)] * 4,
        in_specs=[col(512, C_CQ), col(512, C_CKV), col(LANES, C_KR), row(LANES), row(LANES),
                  full((1, 512)), full((1, 512)), full((768, 512)), full((N_DEV, 512, LANES)), vec, vec, vec, vec,
                  row(768), row(1024), row(512), row(512), heads(256), heads(256), heads(MLA_V)],
        out_specs=[row(512), row(512), row(LANES), full((768, 512)), full((N_DEV, 512, LANES)),
                   full((1, 512)), full((1, 512)), vec, vec, vec, vec],
        scratch=[pltpu.VMEM((768, 512), jnp.float32), pltpu.VMEM((N_DEV, 512, LANES), jnp.float32)],
        sem=("arbitrary",))(proj, proj, proj, cos, sin, g_cq, g_ckv, w_uq, w_ukv, g_qn, g_qr, g_kn, g_kr,
                            qb, kvb, cqn, ckvn, dqc, dkc, dv)


def _swa_bwd(proj, posc, posr, gq, gk, sinks, d_y, y_a, lse, after):
    s = proj.shape[0]
    b = SWA_BLOCK
    nb = s // b
    scale = SWA_DIM ** -0.5

    def body(q_ref, kp_ref, kc_ref, vp_ref, vc_ref, pq_ref, pkp_ref, pkc_ref, gq_ref, gk_ref, sink_ref,
             do_ref, y_ref, lse_ref, kfull_ref,
             dq_ref, dk_ref, dv_ref, dgq_ref, dgk_ref, dsink_ref, dk_acc, dv_acc):
        n = pl.program_id(0)

        @pl.when(n == 0)
        def _():
            dk_acc[...] = jnp.zeros_like(dk_acc)
            dv_acc[...] = jnp.zeros_like(dv_acc)
            dgq_ref[...] = jnp.zeros_like(dgq_ref)
            dsink_ref[...] = jnp.zeros_like(dsink_ref)

        kn, v, bias = _swa_common(n, kp_ref[...], kc_ref[...], vp_ref[...], vc_ref[...],
                                  pq_ref[...], pkp_ref[...], pkc_ref[...], gk_ref[...])
        lo = _lo_mask((b, LANES))
        col = lax.broadcasted_iota(jnp.int32, (b, SWA_Q_HEADS), 1)
        col1 = lax.broadcasted_iota(jnp.int32, (1, SWA_Q_HEADS), 1)
        lse_t = lse_ref[...]
        dk_blk = jnp.zeros((2 * b, LANES), jnp.float32)
        dv_blk = jnp.zeros((2 * b, LANES), jnp.float32)
        dgq = jnp.zeros((1, LANES), jnp.float32)
        dsink = jnp.zeros((1, SWA_Q_HEADS), jnp.float32)
        for j in range(SWA_Q_HEADS // 2):
            hk = (2 * j) // (SWA_Q_HEADS // SWA_KV_HEADS)
            kvmask = lo if hk == 0 else jnp.logical_not(lo)
            sl = slice(LANES * j, LANES * (j + 1))
            qn, xn, r = _norm_fwd(q_ref[:, sl], gq_ref[...], half=True)
            qn = qn * scale
            qsw = pltpu.roll(qn, 64, 1)
            d2 = do_ref[:, sl]
            d2sw = pltpu.roll(d2, 64, 1)
            prod = d2 * y_ref[:, sl]
            dqs = []
            for e in range(2):
                h = 2 * j + e
                half_e = lo if e == 0 else jnp.logical_not(lo)
                qm = jnp.where(kvmask, qn if e == hk else qsw, 0.0)
                dm = jnp.where(kvmask, d2 if e == hk else d2sw, 0.0)
                sc = _dot_nt(qm, kn) + _alibi_slope(h) * bias
                lse_h = jnp.sum(jnp.where(col == h, lse_t, 0.0), -1, keepdims=True)
                p = jnp.exp(sc - lse_h)
                dd = jnp.sum(jnp.where(half_e, prod, 0.0), -1, keepdims=True)
                dp = _dot_nt(dm, v)
                ds = (p * (dp - dd)).astype(MXU)
                dsink = dsink - jnp.where(col1 == h, jnp.sum(jnp.exp(sink_ref[h] - lse_h) * dd), 0.0)
                dq_m = _dot(ds, kn) * scale
                dk_blk = dk_blk + _dot_tn(ds, qm)
                dv_blk = dv_blk + _dot_tn(p, dm)
                dqs.append(dq_m if e == hk else pltpu.roll(dq_m, 64, 1))
            dx, dg = _norm_bwd(xn, r, gq_ref[...], jnp.where(lo, dqs[0], dqs[1]), half=True)
            dq_ref[:, sl] = dx.astype(dq_ref.dtype)
            dgq = dgq + dg
        dgq_ref[...] += dgq
        dsink_ref[...] += dsink
        prev = pl.ds(pl.multiple_of(jnp.maximum(n - 1, 0) * b, b), b)
        cur = pl.ds(pl.multiple_of(n * b, b), b)
        dk_acc[prev, :] += dk_blk[:b]
        dv_acc[prev, :] += dv_blk[:b]
        dk_acc[cur, :] += dk_blk[b:]
        dv_acc[cur, :] += dv_blk[b:]

        @pl.when(n == nb - 1)
        def _():
            _, kxn, kr = _norm_fwd(kfull_ref[...], gk_ref[...], half=True)
            dx, dg = _norm_bwd(kxn, kr, gk_ref[...], dk_acc[...], half=True)
            dk_ref[...] = dx.astype(dk_ref.dtype)
            dv_ref[...] = dv_acc[...].astype(dv_ref.dtype)
            dgk_ref[...] = dg

    full = pl.BlockSpec((s, LANES), lambda n: (0, 0))
    vec = pl.BlockSpec((1, LANES), lambda n: (0, 0))
    return _pcall(
        body, name="swa_bwd", grid=(nb,),
        out_shape=[_sds((s, 1024), MXU), _sds((s, LANES), MXU), _sds((s, LANES), MXU),
                   _sds((1, LANES), jnp.float32), _sds((1, LANES), jnp.float32),
                   _sds((1, SWA_Q_HEADS), jnp.float32)],
        in_specs=_swa_specs(s) + [pl.BlockSpec((b, 1024), lambda n: (n, 0)), pl.BlockSpec((b, 1024), lambda n: (n, 0)),
                                  pl.BlockSpec((b, SWA_Q_HEADS), lambda n: (n, 0)),
                                  pl.BlockSpec((s, LANES), lambda n: (0, C_KA // LANES))],
        out_specs=[pl.BlockSpec((b, 1024), lambda n: (n, 0)), full, full, vec, vec,
                   pl.BlockSpec((1, SWA_Q_HEADS), lambda n: (0, 0))],
        scratch=[pltpu.VMEM((s, LANES), jnp.float32), pltpu.VMEM((s, LANES), jnp.float32)],
        sem=("arbitrary",), after=after)(proj, proj, proj, proj, proj, posc, posr, posr, gq, gk, sinks, d_y, y_a, lse,
                                         proj)


def _dx(d_proj, w_in, x, g, d_h1, after):
    s, d = x.shape
    n = w_in.shape[0]
    tm = min(2 * ROW_TILE, s)

    def body(dp_ref, w_ref, x_ref, g_ref, dh_ref, dx_ref, dg_ref):
        i = pl.program_id(0)

        @pl.when(i == 0)
        def _():
            dg_ref[...] = jnp.zeros_like(dg_ref)

        d_hn = _dot(dp_ref[...], w_ref[...])
        _, xn, r = _norm_fwd(x_ref[...], g_ref[...])
        dx, dg = _norm_bwd(xn, r, g_ref[...], d_hn)
        dx_ref[...] = dh_ref[...] + dx
        dg_ref[...] += dg

    row = pl.BlockSpec((tm, d), lambda i: (i, 0))
    vec = pl.BlockSpec((1, d), lambda i: (0, 0))
    return _pcall(
        body, name="grad_x", grid=(s // tm,),
        out_shape=[_sds((s, d), jnp.float32), _sds((1, d), jnp.float32)],
        in_specs=[pl.BlockSpec((tm, n), lambda i: (i, 0)),
                  pl.BlockSpec((n, d), lambda i: (0, 0), pipeline_mode=pl.Buffered(1)), row, vec, row],
        out_specs=[row, vec], sem=("arbitrary",), after=after)(d_proj, w_in, x, g, d_h1)


_SMALL = ["attn_norm_g", "swa_q_norm_g", "swa_k_norm_g", "swa_sinks", "mla_cq_norm_g", "mla_ckv_norm_g",
          "mla_qn_norm_g", "mla_qr_norm_g", "mla_kn_norm_g", "mla_kr_norm_g", "mem_norm_g",
          "mem_q_norm_g", "mem_k_norm_g", "ffn_norm_g"]


def kernel(x, mem, positions, attn_norm_g, w_in, swa_q_norm_g, swa_k_norm_g, swa_sinks, mla_cq_norm_g, mla_ckv_norm_g, w_uq, w_ukv, mla_qn_norm_g, mla_qr_norm_g, mla_kn_norm_g, mla_kr_norm_g, mem_norm_g, w_mem_kv, mem_q_norm_g, mem_k_norm_g, w_out, ffn_norm_g, w_gate, w_up, w_down, loss_target, m_attn_norm_g, m_w_in, m_swa_q_norm_g, m_swa_k_norm_g, m_swa_sinks, m_mla_cq_norm_g, m_mla_ckv_norm_g, m_w_uq, m_w_ukv, m_mla_qn_norm_g, m_mla_qr_norm_g, m_mla_kn_norm_g, m_mla_kr_norm_g, m_mem_norm_g, m_w_mem_kv, m_mem_q_norm_g, m_mem_k_norm_g, m_w_out, m_ffn_norm_g, m_w_gate, m_w_up, m_w_down, v_attn_norm_g, v_w_in, v_swa_q_norm_g, v_swa_k_norm_g, v_swa_sinks, v_mla_cq_norm_g, v_mla_ckv_norm_g, v_w_uq, v_w_ukv, v_mla_qn_norm_g, v_mla_qr_norm_g, v_mla_kn_norm_g, v_mla_kr_norm_g, v_mem_norm_g, v_w_mem_kv, v_mem_q_norm_g, v_mem_k_norm_g, v_w_out, v_ffn_norm_g, v_w_gate, v_w_up, v_w_down):
    args = dict(locals())
    x2, mem2, tgt = x[0], mem[0], loss_target[0]
    s, d = x2.shape
    n_in = w_in.shape[2]
    f = w_gate.shape[2]

    (g_in,) = _all_gather([w_in[0].T.astype(WIRE)])
    mix_shards = [w_uq[0].T.astype(WIRE), w_ukv[0].astype(WIRE), w_mem_kv[0].astype(WIRE),
                  _to_wire([w_out[0]], g_in, "wire_out")[0]]
    g_uq, wkv, g_mkv, g_out = _all_gather_background(mix_shards, 5, "all_gather_mix_weights")
    ffn_shards = [_to_wire([w_gate[0].T, w_up[0].T], g_in, "wire_gate_up"),
                  _to_wire([w_down[0]], g_in, "wire_down")[0]]
    w_gu, w_d = _all_gather_background(ffn_shards, 1, "all_gather_ffn_weights")
    wi = g_in.reshape(N_DEV * n_in, d)
    wi = jnp.concatenate([wi[0:1024], wi[1280:1792], wi[1792:2304], wi[2368:2880],
                          wi[1024:1152], wi[1152:1280], wi[2304:2368],
                          jnp.zeros((IN_PAD - 2880, d), wi.dtype)], axis=0)
    wq = g_uq.reshape(768, 512)
    wq = jnp.concatenate([wq[192 * h: 192 * h + 128] for h in range(4)]
                         + [wq[192 * h + 128: 192 * (h + 1)] for h in range(4)], axis=0)
    wmkv = g_mkv.reshape(-1, g_mkv.shape[-1])
    wo = g_out.reshape(-1, d)

    pos = positions[0].astype(jnp.float32)
    inv_freq = ROPE_THETA ** (-jnp.arange(0, MLA_ROPE, 2, dtype=jnp.float32) / MLA_ROPE)
    ang = pos[:, None] * inv_freq
    cos32, sin32 = jnp.cos(ang), jnp.sin(ang)
    cos_t = jnp.tile(cos32, (1, 4))
    sin_t = jnp.tile(jnp.concatenate([-sin32, sin32], axis=1), (1, 2))
    posc, posr = pos.reshape(s, 1), pos.reshape(1, s)
    two = lambda g: jnp.tile(g, (1, 2))
    gq2, gk2, gqr2, gkr2 = two(swa_q_norm_g), two(swa_k_norm_g), two(mla_qr_norm_g), two(mla_kr_norm_g)
    sinks1 = swa_sinks[0]

    proj, hn = _in_proj(x2, attn_norm_g, wi)
    qc, kc, vb, qb, kvb, cqn, ckvn = _mla_prep(proj, cos_t, sin_t, mla_cq_norm_g, mla_ckv_norm_g, wq, wkv,
                                                mla_qn_norm_g, gqr2, mla_kn_norm_g, gkr2)
    y_b, lse_b = _mla_fwd(qc, kc, vb)
    km, vmm, kvm, memn = _memkv_prep(mem2, mem_norm_g, wmkv, mem_k_norm_g)
    y_m, lse_m = _mem_fwd(proj, mem_q_norm_g, km, vmm)
    y_a, lse_a = _swa_fwd(proj, posc, posr, gq2, gk2, sinks1)
    h1, fn = _out_proj(y_a, y_b, y_m, x2, wo, ffn_norm_g)
    gu, act = _ffn_gu(fn, w_gu)
    dout, dout_b, loss_tile = _ffn_down(act, w_d, h1, tgt)

    dgu, dw_d = _ffn_bwd_act(dout_b, w_d, gu)
    dw_gu = _ffn_dw_gu(fn, dgu)
    r_gu, r_d = _exchange_grads_background([dw_gu, dw_d], 2, "exchange_ffn_grads")
    d_h1, dg_ffn = _ffn_norm_bwd(_ffn_dfn(dgu, w_gu, dw_gu), dout, h1, ffn_norm_g)
    d_y = _mm(d_h1, wo, tb=True, out_dtype=jnp.float32, tm=FFN_TILE, tk=2048, name="d_mix")
    dw_out = jnp.concatenate([
        _mm(y_a, d_h1, ta=True, out_dtype=WIRE, tm=1024, tk=1024, name="dw_out_a"),
        _mm(y_b, d_h1, ta=True, out_dtype=WIRE, tm=1024, tk=1024, name="dw_out_b"),
        _mm(y_m, d_h1, ta=True, out_dtype=WIRE, tm=1024, tk=1024, name="dw_out_m")], axis=0)
    d_qm, dkm, dvmm, dg_mq = _mem_bwd(proj, mem_q_norm_g, km, vmm, d_y, y_m, lse_m)
    dw_mkv, dg_mem, dg_mk = _memkv_bwd(mem2, mem_norm_g, wmkv, mem_k_norm_g, kvm, memn, dkm, dvmm)
    r_mkv, r_out = _exchange_grads_background([dw_mkv.reshape(g_mkv.shape), dw_out.reshape(g_out.shape)], 3,
                                              "exchange_mix_grads")
    dqc, dkc, dvb = _mla_bwd(qc, kc, vb, d_y, y_b, lse_b, dw_mkv)
    (d_cq, d_ckv, d_kr, dw_uq, dw_ukv, dg_cq, dg_ckv, dg_qn, dg_qr, dg_kn, dg_kr) = _mla_prep_bwd(
        proj, cos_t, sin_t, mla_cq_norm_g, mla_ckv_norm_g, wq, wkv, mla_qn_norm_g, gqr2, mla_kn_norm_g, gkr2,
        qb, kvb, cqn, ckvn, dqc, dkc, dvb)
    d_qa, d_ka, d_va, dg_q, dg_k, d_sinks = _swa_bwd(proj, posc, posr, gq2, gk2, sinks1, d_y, y_a, lse_a, dw_out)
    d_proj = jnp.concatenate([d_qa, d_cq, d_ckv, d_qm, d_ka, d_va, d_kr], axis=1)
    gi = _dw_in(hn, d_proj, n_in)

    gq_ = jnp.concatenate(sum([[dw_uq[128 * h: 128 * (h + 1)], dw_uq[512 + 64 * h: 512 + 64 * (h + 1)]]
                               for h in range(4)], []), axis=0)
    gq_ = gq_.reshape(N_DEV, 96, 512)
    r_in, r_uq, r_ukv = _exchange_grads_background([gi, gq_, dw_ukv], 4, "exchange_in_grads")
    grad_x, dg_attn = _dx(d_proj, wi, x2, attn_norm_g, d_h1, gi)

    big = {}
    def adam(name, r, transposed=False, after=None, which=None):
        w, m, v = args[name][0], args["m_" + name][0], args["v_" + name][0]
        if transposed:
            outs = _adam_big(r, w.T, m.T, v.T, "adam_" + name, after, which)
            return [o.T[None] for o in outs]
        return [o[None] for o in _adam_big(r, w, m, v, "adam_" + name, after)]
    big["w_gate"] = adam("w_gate", r_gu, True, which=0)
    big["w_up"] = adam("w_up", r_gu, True, after=big["w_gate"][0], which=1)
    big["w_down"] = adam("w_down", r_d, after=big["w_up"][0])
    big["w_out"] = adam("w_out", r_out, after=big["w_down"][0])
    big["w_mem_kv"] = adam("w_mem_kv", r_mkv, after=big["w_out"][0])
    big["w_in"] = adam("w_in", r_in, True, after=big["w_mem_kv"][0])
    big["w_uq"] = adam("w_uq", r_uq, True, after=big["w_in"][0])
    big["w_ukv"] = adam("w_ukv", r_ukv, after=big["w_uq"][0])

    small_g = {
        "attn_norm_g": dg_attn, "swa_q_norm_g": dg_q, "swa_k_norm_g": dg_k,
        "swa_sinks": d_sinks, "mla_cq_norm_g": dg_cq, "mla_ckv_norm_g": dg_ckv, "mla_qn_norm_g": dg_qn,
        "mla_qr_norm_g": dg_qr, "mla_kn_norm_g": dg_kn, "mla_kr_norm_g": dg_kr,
        "mem_norm_g": dg_mem, "mem_q_norm_g": dg_mq, "mem_k_norm_g": dg_mk, "ffn_norm_g": dg_ffn}
    loss11, small_out = _small_allreduce_adam(
        [small_g[n] for n in _SMALL], loss_tile, [args[n] for n in _SMALL],
        [args["m_" + n] for n in _SMALL], [args["v_" + n] for n in _SMALL])
    small = dict(zip(_SMALL, small_out))
    loss = loss11.reshape(())

    order = ["attn_norm_g", "w_in", "swa_q_norm_g", "swa_k_norm_g", "swa_sinks", "mla_cq_norm_g", "mla_ckv_norm_g",
             "w_uq", "w_ukv", "mla_qn_norm_g", "mla_qr_norm_g", "mla_kn_norm_g", "mla_kr_norm_g", "mem_norm_g",
             "w_mem_kv", "mem_q_norm_g", "mem_k_norm_g", "w_out", "ffn_norm_g", "w_gate", "w_up", "w_down"]
    res = {n: (big[n] if n in big else list(small[n])) for n in order}
    outs = [loss, grad_x[None]]
    for kind in range(4):
        outs += [res[n][kind] for n in order]
    return tuple(outs)
```

```python
import jax
import jax.numpy as jnp
from jax import lax
from jax.experimental import pallas as pl
from jax.experimental.pallas import tpu as pltpu
from jax.experimental.pallas import tpu_sc as plsc

MXU = jnp.bfloat16
WIRE = jnp.bfloat16
EPS = 1e-6
NEG_INF = -1e30
LOG2E = 1.4426950408889634
N_DEV = 8
LANES = 128
ROW_TILE = 256
FFN_TILE = 512
ATT_TILE = 1024
SWA_BLOCK = 128
VMEM_LIMIT = 56 * 1024 * 1024

SWA_Q_HEADS, SWA_KV_HEADS, SWA_DIM = 16, 2, 64
MLA_HEADS, MLA_NOPE, MLA_ROPE, MLA_V = 4, 128, 64, 128
MEM_HEADS, MEM_DIM = 4, 128
ROPE_THETA = 10000.0
ADAM_LR, ADAM_B1, ADAM_B2, ADAM_EPS, ADAM_WD, ADAM_STEP = 0.001, 0.9, 0.999, 1e-08, 0.01, 10

C_QA, C_CQ, C_CKV, C_QM, C_KA, C_VA, C_KR, IN_PAD = 0, 1024, 1536, 2048, 2560, 2688, 2816, 2944


def _pcall(body, *, name, out_shape, in_specs, out_specs, grid=(), scratch=(), sem=None, after=None):
    params = pltpu.CompilerParams(dimension_semantics=sem, vmem_limit_bytes=VMEM_LIMIT)
    if after is not None:
        n_in, inner = len(in_specs), body

        def body(*refs):
            inner(*refs[:n_in], *refs[n_in + 1:])

        in_specs = list(in_specs) + [pl.BlockSpec(memory_space=pl.ANY)]
    call = pl.pallas_call(body, name=name, grid=grid, in_specs=in_specs, out_specs=out_specs,
                          out_shape=out_shape, scratch_shapes=list(scratch), compiler_params=params)
    return call if after is None else (lambda *ops: call(*ops, after))


def _sds(shape, dtype):
    return jax.ShapeDtypeStruct(tuple(shape), dtype)


def _dot(a, b):
    return jnp.dot(a.astype(MXU), b.astype(MXU), preferred_element_type=jnp.float32)


def _dot_nt(a, b):
    return lax.dot_general(a.astype(MXU), b.astype(MXU), (((1,), (1,)), ((), ())),
                           preferred_element_type=jnp.float32)


def _dot_tn(a, b):
    return lax.dot_general(a.astype(MXU), b.astype(MXU), (((0,), (0,)), ((), ())),
                           preferred_element_type=jnp.float32)


def _lo_mask(shape):
    return (lax.broadcasted_iota(jnp.int32, shape, len(shape) - 1) % LANES) < 64


def _norm_fwd(x, g, half=False):
    x2 = x * x
    if half:
        lo = _lo_mask(x.shape)
        s_lo = jnp.sum(jnp.where(lo, x2, 0.0), -1, keepdims=True)
        s_hi = jnp.sum(jnp.where(lo, 0.0, x2), -1, keepdims=True)
        r = jnp.where(lo, lax.rsqrt(s_lo / 64.0 + EPS), lax.rsqrt(s_hi / 64.0 + EPS))
    else:
        r = lax.rsqrt(jnp.mean(x2, -1, keepdims=True) + EPS)
    xn = x * r
    return xn * g, xn, r


def _norm_bwd(xn, r, g, dy, half=False):
    t = dy * g
    tx = t * xn
    if half:
        lo = _lo_mask(xn.shape)
        m_lo = jnp.sum(jnp.where(lo, tx, 0.0), -1, keepdims=True) / 64.0
        m_hi = jnp.sum(jnp.where(lo, 0.0, tx), -1, keepdims=True) / 64.0
        m = jnp.where(lo, m_lo, m_hi)
    else:
        m = jnp.mean(tx, -1, keepdims=True)
    dx = r * (t - xn * m)
    dg = jnp.sum(dy * xn, 0, keepdims=True)
    return dx, dg


def _swap32(x):
    lane = lax.broadcasted_iota(jnp.int32, x.shape, 1)
    return jnp.where((lane % 64) < 32, pltpu.roll(x, 96, 1), pltpu.roll(x, 32, 1))


def _rope(x, cos, sin):
    return x * cos + _swap32(x) * sin


def _rope_bwd(d, cos, sin):
    return d * cos + _swap32(d * sin)


def _my_coords():
    return lax.axis_index("x"), lax.axis_index("y"), lax.axis_index("c")


def _dev_index(px, py, pc):
    return 4 * px + 2 * py + pc


_FLIPS = [(0, 0, 1), (0, 1, 0), (0, 1, 1), (1, 0, 0), (1, 0, 1), (1, 1, 0), (1, 1, 1)]


def _flip(coords, f):
    return tuple((1 - v) if b else v for v, b in zip(coords, f))


def _all_gather(shards):
    n = len(shards)

    def body(*refs):
        ins, outs = refs[:n], refs[n:2 * n]
        send_sems, recv_sems, local_sems = refs[2 * n:]
        x, y, c = _my_coords()
        me, sibling = (x, y, c), (x, y, 1 - c)
        chips = [(1 - x, y), (x, 1 - y), (1 - x, 1 - y)]

        def copy(w, k, block, to, src=None):
            dst = outs[w].at[_dev_index(*block)]
            return pltpu.make_async_remote_copy(
                src_ref=dst if src is None else src, dst_ref=dst,
                send_sem=send_sems.at[w, k], recv_sem=recv_sems.at[w, k],
                device_id=to, device_id_type=pl.DeviceIdType.MESH)

        sends, locals_ = [], []
        for w in range(n):
            mine = pltpu.make_async_copy(ins[w], outs[w].at[_dev_index(*me)], local_sems.at[w])
            mine.start()
            locals_.append(mine)
            first = [copy(w, 0, me, sibling, src=ins[w])]
            first += [copy(w, 1 + j, me, (*chip, c), src=ins[w]) for j, chip in enumerate(chips)]
            for cp in first:
                cp.start()
            sends += first
        for w in range(n):
            for j, chip in enumerate(chips):
                copy(w, 1 + j, (*chip, c), me).wait_recv()
                fwd = copy(w, 4 + j, (*chip, c), sibling)
                fwd.start()
                sends.append(fwd)
        for w in range(n):
            copy(w, 0, sibling, me).wait_recv()
            for j, chip in enumerate(chips):
                copy(w, 4 + j, (*chip, 1 - c), me).wait_recv()
        for cp in sends:
            cp.wait_send()
        for mine in locals_:
            mine.wait()

    any_spec = pl.BlockSpec(memory_space=pl.ANY)
    return _pcall(
        body, name="all_gather_weights",
        out_shape=[_sds((N_DEV,) + s.shape, s.dtype) for s in shards],
        in_specs=[any_spec] * n, out_specs=[any_spec] * n,
        scratch=[pltpu.SemaphoreType.DMA((n, 7)), pltpu.SemaphoreType.DMA((n, 7)),
                 pltpu.SemaphoreType.DMA((n,))])(*shards)


def _wire_cost(arrays):
    nbytes = sum(a.size * a.dtype.itemsize for a in arrays)
    return pl.CostEstimate(flops=0, transcendentals=0, bytes_accessed=40 * nbytes)


def _all_gather_background(shards, collective_id, name):
    n = len(shards)
    src_refs = [jax.new_ref(s, memory_space=pltpu.MemorySpace.HBM) for s in shards]
    out_refs = [jax.empty_ref(_sds((N_DEV,) + s.shape, s.dtype), memory_space=pltpu.MemorySpace.HBM) for s in shards]

    @pl.kernel(mesh=plsc.ScalarSubcoreMesh(axis_name="seq", num_cores=1), name=name,
               scratch_types=(pltpu.SemaphoreType.DMA((n, 7)), pltpu.SemaphoreType.DMA((n, 7)),
                              pltpu.SemaphoreType.DMA((n,))),
               compiler_params=pltpu.CompilerParams(collective_id=collective_id))
    def launch(send_sems, recv_sems, local_sems):
        x, y, c = _my_coords()
        me, sibling = (x, y, c), (x, y, 1 - c)
        chips = [(1 - x, y), (x, 1 - y), (1 - x, 1 - y)]
        barrier = pltpu.get_barrier_semaphore()
        for peer in [sibling] + [(*chip, c) for chip in chips]:
            pl.semaphore_signal(barrier, inc=1, device_id=peer, device_id_type=pl.DeviceIdType.MESH)
        pl.semaphore_wait(barrier, 4)

        def copy(w, k, block, to, src=None):
            dst = out_refs[w].at[_dev_index(*block)]
            return pltpu.make_async_remote_copy(
                src_ref=dst if src is None else src, dst_ref=dst,
                send_sem=send_sems.at[w, k], recv_sem=recv_sems.at[w, k],
                device_id=to, device_id_type=pl.DeviceIdType.MESH)

        sends, locals_ = [], []
        for w in range(n):
            mine = pltpu.make_async_copy(src_refs[w], out_refs[w].at[_dev_index(*me)], local_sems.at[w])
            mine.start()
            locals_.append(mine)
            first = [copy(w, 0, me, sibling, src=src_refs[w])]
            first += [copy(w, 1 + j, me, (*chip, c), src=src_refs[w]) for j, chip in enumerate(chips)]
            for cp in first:
                cp.start()
            sends += first
        for w in range(n):
            for j, chip in enumerate(chips):
                copy(w, 1 + j, (*chip, c), me).wait_recv()
                fwd = copy(w, 4 + j, (*chip, c), sibling)
                fwd.start()
                sends.append(fwd)
        for w in range(n):
            copy(w, 0, sibling, me).wait_recv()
            for j, chip in enumerate(chips):
                copy(w, 4 + j, (*chip, 1 - c), me).wait_recv()
        for cp in sends:
            cp.wait_send()
        for mine in locals_:
            mine.wait()

    launch()
    return [r[...] for r in out_refs]


def _exchange_grads(grads):
    n = len(grads)

    def body(*refs):
        ins, outs = refs[:n], refs[n:2 * n]
        send_sems, recv_sems, local_sems = refs[2 * n:]
        me = _my_coords()
        my_idx = _dev_index(*me)
        sends, locals_ = [], []
        for w in range(n):
            mine = pltpu.make_async_copy(ins[w].at[my_idx], outs[w].at[my_idx], local_sems.at[w])
            mine.start()
            locals_.append(mine)
            for k, f in enumerate(_FLIPS):
                peer = _flip(me, f)
                cp = pltpu.make_async_remote_copy(
                    src_ref=ins[w].at[_dev_index(*peer)], dst_ref=outs[w].at[my_idx],
                    send_sem=send_sems.at[w, k], recv_sem=recv_sems.at[w, k],
                    device_id=peer, device_id_type=pl.DeviceIdType.MESH)
                cp.start()
                sends.append(cp)
        for w in range(n):
            for k, f in enumerate(_FLIPS):
                peer = _flip(me, f)
                slot = outs[w].at[_dev_index(*peer)]
                pltpu.make_async_remote_copy(
                    src_ref=slot, dst_ref=slot,
                    send_sem=send_sems.at[w, k], recv_sem=recv_sems.at[w, k],
                    device_id=peer, device_id_type=pl.DeviceIdType.MESH).wait_recv()
        for cp in sends:
            cp.wait_send()
        for mine in locals_:
            mine.wait()

    any_spec = pl.BlockSpec(memory_space=pl.ANY)
    return _pcall(
        body, name="exchange_grads",
        out_shape=[_sds(g.shape, g.dtype) for g in grads],
        in_specs=[any_spec] * n, out_specs=[any_spec] * n,
        scratch=[pltpu.SemaphoreType.DMA((n, 7)), pltpu.SemaphoreType.DMA((n, 7)),
                 pltpu.SemaphoreType.DMA((n,))])(*grads)


def _exchange_grads_background(grads, collective_id, name):
    n = len(grads)
    src_refs = [jax.new_ref(g, memory_space=pltpu.MemorySpace.HBM) for g in grads]
    out_refs = [jax.empty_ref(_sds(g.shape, g.dtype), memory_space=pltpu.MemorySpace.HBM) for g in grads]

    @pl.kernel(mesh=plsc.ScalarSubcoreMesh(axis_name="seq", num_cores=1), name=name,
               scratch_types=(pltpu.SemaphoreType.DMA((n, 7)), pltpu.SemaphoreType.DMA((n, 7)),
                              pltpu.SemaphoreType.DMA((n,))),
               cost_estimate=_wire_cost(grads),
               compiler_params=pltpu.CompilerParams(collective_id=collective_id))
    def launch(send_sems, recv_sems, local_sems):
        me = _my_coords()
        my_idx = _dev_index(*me)
        peers = [_flip(me, f) for f in _FLIPS]
        barrier = pltpu.get_barrier_semaphore()
        for peer in peers:
            pl.semaphore_signal(barrier, inc=1, device_id=peer, device_id_type=pl.DeviceIdType.MESH)
        pl.semaphore_wait(barrier, len(peers))
        sends, locals_ = [], []
        for w in range(n):
            mine = pltpu.make_async_copy(src_refs[w].at[my_idx], out_refs[w].at[my_idx], local_sems.at[w])
            mine.start()
            locals_.append(mine)
            for k, peer in enumerate(peers):
                cp = pltpu.make_async_remote_copy(
                    src_ref=src_refs[w].at[_dev_index(*peer)], dst_ref=out_refs[w].at[my_idx],
                    send_sem=send_sems.at[w, k], recv_sem=recv_sems.at[w, k],
                    device_id=peer, device_id_type=pl.DeviceIdType.MESH)
                cp.start()
                sends.append(cp)
        for w in range(n):
            for k, peer in enumerate(peers):
                slot = out_refs[w].at[_dev_index(*peer)]
                pltpu.make_async_remote_copy(
                    src_ref=slot, dst_ref=slot, send_sem=send_sems.at[w, k], recv_sem=recv_sems.at[w, k],
                    device_id=peer, device_id_type=pl.DeviceIdType.MESH).wait_recv()
        for cp in sends:
            cp.wait_send()
        for mine in locals_:
            mine.wait()

    launch()
    return [r[...] for r in out_refs]


def _to_wire(parts, after, name):
    n = len(parts)
    rows, cols = parts[0].shape
    tr = rows // 2 if rows % 32 == 0 else rows

    def body(*refs):
        for k in range(n):
            refs[n][k] = refs[k][...].astype(WIRE)

    blk = pl.BlockSpec((tr, cols), lambda i: (i, 0))
    return _pcall(
        body, name=name, grid=(rows // tr,), out_shape=_sds((n, rows, cols), WIRE),
        in_specs=[blk] * n, out_specs=pl.BlockSpec((n, tr, cols), lambda i: (0, i, 0)),
        sem=("parallel",), after=after)(*parts)


def _adam_math(w, g, m, v):
    m = ADAM_B1 * m + (1.0 - ADAM_B1) * g
    v = ADAM_B2 * v + (1.0 - ADAM_B2) * (g * g)
    m_hat = m / (1.0 - ADAM_B1 ** ADAM_STEP)
    v_hat = v / (1.0 - ADAM_B2 ** ADAM_STEP)
    delta = -ADAM_LR * (m_hat / (jnp.sqrt(v_hat) + ADAM_EPS) + ADAM_WD * w)
    return delta, m, v


def _small_layout(sizes):
    row0, r = [], 0
    for n in sizes:
        row0.append(r)
        r += -(-n // LANES)
    return row0, r, -(-(r + 1) // 8) * 8


def _small_pieces(n):
    return [(k, min(LANES, n - LANES * k)) for k in range(-(-n // LANES))]


def _small_fill(pack, slot, srcs, sizes, row0, rows):
    pack[slot] = jnp.zeros((rows, LANES), jnp.float32)
    for p, n in enumerate(sizes):
        val = srcs[p][...]
        if val.shape[-1] == LANES and n == 64:
            pack[slot, row0[p]:row0[p] + 1, :] = val + pltpu.roll(val, 64, 1)
            continue
        for k, width in _small_pieces(n):
            pack[slot, row0[p] + k:row0[p] + k + 1, 0:width] = srcs[p][:, LANES * k:LANES * k + width]


def _small_allreduce(grads, loss_tile, sizes):
    n_par = len(sizes)
    row0, loss_row, rows = _small_layout(sizes)

    def body(*refs):
        g_refs, loss_in, out_ref = refs[:n_par], refs[n_par], refs[n_par + 1]
        pack, gath, send_sems, recv_sems = refs[n_par + 2:]
        me = _my_coords()
        my_idx = _dev_index(*me)
        _small_fill(pack, 0, g_refs, sizes, row0, rows)
        pack[0, loss_row:loss_row + 1, :] = loss_in[0:1, :]
        gath[my_idx] = pack[0]
        sends = []
        for k, f in enumerate(_FLIPS):
            peer = _flip(me, f)
            cp = pltpu.make_async_remote_copy(
                src_ref=pack.at[0], dst_ref=gath.at[my_idx],
                send_sem=send_sems.at[k], recv_sem=recv_sems.at[k],
                device_id=peer, device_id_type=pl.DeviceIdType.MESH)
            cp.start()
            sends.append(cp)
        for k, f in enumerate(_FLIPS):
            peer = _flip(me, f)
            slot = gath.at[_dev_index(*peer)]
            pltpu.make_async_remote_copy(
                src_ref=slot, dst_ref=slot, send_sem=send_sems.at[k], recv_sem=recv_sems.at[k],
                device_id=peer, device_id_type=pl.DeviceIdType.MESH).wait_recv()
        for cp in sends:
            cp.wait_send()
        g = gath[0]
        for d in range(1, N_DEV):
            g = g + gath[d]
        out_ref[...] = g

    vm = pl.BlockSpec(memory_space=pltpu.VMEM)
    return _pcall(
        body, name="small_allreduce", out_shape=_sds((rows, LANES), jnp.float32),
        in_specs=[vm] * (n_par + 1), out_specs=vm,
        scratch=[pltpu.VMEM((1, rows, LANES), jnp.float32), pltpu.VMEM((N_DEV, rows, LANES), jnp.float32),
                 pltpu.SemaphoreType.DMA((7,)), pltpu.SemaphoreType.DMA((7,))])(*grads, loss_tile)


def _small_adam(packed_g, ws, ms, vs):
    sizes = [w.shape[-1] for w in ws]
    n_par = len(ws)
    row0, loss_row, rows = _small_layout(sizes)

    def body(*refs):
        g_ref = refs[0]
        w_refs, m_refs, v_refs = (refs[1 + k * n_par: 1 + (k + 1) * n_par] for k in range(3))
        loss_out = refs[3 * n_par + 1]
        out_refs = refs[3 * n_par + 2: 7 * n_par + 2]
        pack, res = refs[7 * n_par + 2:]
        for slot, srcs in enumerate((w_refs, m_refs, v_refs)):
            _small_fill(pack, slot, srcs, sizes, row0, rows)
        g = g_ref[...]
        delta, m, v = _adam_math(pack[0], g, pack[1], pack[2])
        res[0], res[1], res[2], res[3] = g, delta, m, v
        loss_out[...] = res[0, loss_row:loss_row + 1, 0:1]
        for p, n in enumerate(sizes):
            for kind in range(4):
                for k, width in _small_pieces(n):
                    out_refs[4 * p + kind][:, LANES * k:LANES * k + width] = (
                        res[kind, row0[p] + k:row0[p] + k + 1, 0:width])

    vm = pl.BlockSpec(memory_space=pltpu.VMEM)
    out_shape = [_sds((1, 1), jnp.float32)]
    for n in sizes:
        out_shape += [_sds((1, n), jnp.float32)] * 4
    outs = _pcall(
        body, name="small_adam", out_shape=out_shape,
        in_specs=[vm] * (3 * n_par + 1), out_specs=[vm] * len(out_shape),
        scratch=[pltpu.VMEM((3, rows, LANES), jnp.float32), pltpu.VMEM((4, rows, LANES), jnp.float32)])(
            packed_g, *ws, *ms, *vs)
    return outs[0], [outs[1 + 4 * p: 5 + 4 * p] for p in range(n_par)]


def _adam_big(recv, w, m, v, name, after=None, which=None):
    rows, cols = recv.shape[-2:]
    row_tiles = [t for t in range(16, rows + 1, 16) if rows % t == 0 and t * cols <= 400 * 1024]
    tr, tc = (max(row_tiles), cols) if row_tiles else (rows, 512 if cols % 512 == 0 else cols)

    def body(r_ref, w_ref, m_ref, v_ref, g_ref, d_ref, mo_ref, vo_ref):
        g = r_ref[0].astype(jnp.float32)
        for d in range(1, N_DEV):
            g = g + r_ref[d].astype(jnp.float32)
        delta, mn, vn = _adam_math(w_ref[...], g, m_ref[...], v_ref[...])
        g_ref[...] = g
        d_ref[...] = delta
        mo_ref[...] = mn
        vo_ref[...] = vn

    blk = pl.BlockSpec((tr, tc), lambda i, j: (i, j))
    if which is None:
        r_spec = pl.BlockSpec((N_DEV, tr, tc), lambda i, j: (0, i, j))
    else:
        r_spec = pl.BlockSpec((N_DEV, None, tr, tc), lambda i, j: (0, which, i, j))
    return _pcall(
        body, name=name, grid=(rows // tr, cols // tc),
        out_shape=[_sds((rows, cols), jnp.float32)] * 4,
        in_specs=[r_spec, blk, blk, blk],
        out_specs=[blk] * 4, sem=("parallel", "parallel"), after=after)(recv, w, m, v)


def _mm(a, b, *, ta=False, tb=False, out_dtype, tm, tk, name):
    (kdim, mdim) = a.shape if ta else a.shape[::-1]
    ndim = b.shape[0] if tb else b.shape[1]
    tm, tk = min(tm, mdim), min(tk, kdim)
    nk = kdim // tk

    def body(a_ref, b_ref, o_ref, acc):
        k = pl.program_id(1)
        if ta:
            part = _dot_tn(a_ref[...], b_ref[...])
        elif tb:
            part = _dot_nt(a_ref[...], b_ref[...])
        else:
            part = _dot(a_ref[...], b_ref[...])

        @pl.when(k == 0)
        def _():
            acc[...] = part

        @pl.when(k > 0)
        def _():
            acc[...] += part

        @pl.when(k == nk - 1)
        def _():
            o_ref[...] = acc[...].astype(o_ref.dtype)

    a_spec = pl.BlockSpec((tk, tm), lambda i, k: (k, i)) if ta else pl.BlockSpec((tm, tk), lambda i, k: (i, k))
    b_spec = pl.BlockSpec((ndim, tk), lambda i, k: (0, k)) if tb else pl.BlockSpec((tk, ndim), lambda i, k: (k, 0))
    return _pcall(
        body, name=name, grid=(mdim // tm, nk), out_shape=_sds((mdim, ndim), out_dtype),
        in_specs=[a_spec, b_spec], out_specs=pl.BlockSpec((tm, ndim), lambda i, k: (i, 0)),
        scratch=[pltpu.VMEM((tm, ndim), jnp.float32)], sem=("parallel", "arbitrary"))(a, b)


def _ref_col_pieces(start, stop):
    ref_starts = [0, 1024, 1152, 1280, 1792, 2304, 2368, 2880]
    perm_starts = [C_QA, C_KA, C_VA, C_CQ, C_CKV, C_KR, C_QM]
    out = []
    for p in range(7):
        lo, hi = max(start, ref_starts[p]), min(stop, ref_starts[p + 1])
        if lo < hi:
            out.append((lo - start, perm_starts[p] + lo - ref_starts[p], hi - lo))
    return out


def _dw_in(hn, d_proj, n_shard):
    s, d = hn.shape
    n = d_proj.shape[1]
    tm, tk = min(512, d), min(1024, s)
    nk = s // tk

    def body(a_ref, b_ref, o_ref, acc):
        k = pl.program_id(1)
        part = _dot_tn(a_ref[...], b_ref[...])

        @pl.when(k == 0)
        def _():
            acc[...] = part

        @pl.when(k > 0)
        def _():
            acc[...] += part

        @pl.when(k == nk - 1)
        def _():
            t = acc[...].T
            for j in range(N_DEV):
                rows = [t[src:src + width] for _, src, width in _ref_col_pieces(j * n_shard, (j + 1) * n_shard)]
                o_ref[j] = jnp.concatenate(rows, axis=0).astype(o_ref.dtype)

    return _pcall(
        body, name="dw_in", grid=(d // tm, nk), out_shape=_sds((N_DEV, n_shard, d), WIRE),
        in_specs=[pl.BlockSpec((tk, tm), lambda i, k: (k, i)), pl.BlockSpec((tk, n), lambda i, k: (k, 0))],
        out_specs=pl.BlockSpec((N_DEV, n_shard, tm), lambda i, k: (0, 0, i)),
        scratch=[pltpu.VMEM((tm, n), jnp.float32)], sem=("parallel", "arbitrary"))(hn, d_proj)


def _in_proj(x, g, w):
    s, d = x.shape
    n = w.shape[0]
    tm = min(2 * ROW_TILE, s)

    def body(x_ref, g_ref, w_ref, p_ref, hn_ref):
        hn, _, _ = _norm_fwd(x_ref[...], g_ref[...])
        hn_ref[...] = hn.astype(hn_ref.dtype)
        p_ref[...] = _dot_nt(hn, w_ref[...])

    return _pcall(
        body, name="in_proj", grid=(s // tm,),
        out_shape=[_sds((s, n), jnp.float32), _sds((s, d), MXU)],
        in_specs=[pl.BlockSpec((tm, d), lambda i: (i, 0)), pl.BlockSpec((1, d), lambda i: (0, 0)),
                  pl.BlockSpec((n, d), lambda i: (0, 0), pipeline_mode=pl.Buffered(1))],
        out_specs=[pl.BlockSpec((tm, n), lambda i: (i, 0)), pl.BlockSpec((tm, d), lambda i: (i, 0))],
        sem=("parallel",))(x, g, w)


def _mla_prep(proj, cos, sin, g_cq, g_ckv, w_uq, w_ukv, g_qn, g_qr, g_kn, g_kr):
    s = proj.shape[0]
    tm = min(ROW_TILE, s)
    nh = MLA_HEADS

    def body(cq_ref, ckv_ref, kr_ref, cos_ref, sin_ref, gcq_ref, gckv_ref, wuq_ref, wukv_ref,
             gqn_ref, gqr_ref, gkn_ref, gkr_ref,
             qc_ref, kc_ref, v_ref, qb_ref, kvb_ref, cqn_ref, ckvn_ref):
        cos_t, sin_t = cos_ref[...], sin_ref[...]
        lo = _lo_mask((tm, LANES))
        cqn, _, _ = _norm_fwd(cq_ref[...], gcq_ref[...])
        cqn_ref[...] = cqn.astype(cqn_ref.dtype)
        qb = _dot_nt(cqn, wuq_ref[...])
        qb_ref[...] = qb
        ckvn, _, _ = _norm_fwd(ckv_ref[...], gckv_ref[...])
        ckvn_ref[...] = ckvn.astype(ckvn_ref.dtype)
        kvb = jnp.concatenate([_dot(ckvn, wukv_ref[dev]) for dev in range(N_DEV)], axis=1)
        kvb_ref[...] = kvb
        kr, _, _ = _norm_fwd(kr_ref[...], gkr_ref[...], half=True)
        kr = _rope(kr, cos_t, sin_t)
        kr2 = jnp.where(lo, kr, pltpu.roll(kr, 64, 1))
        ropes = []
        for j in range(nh // 2):
            xr = qb[:, nh * MLA_NOPE + LANES * j: nh * MLA_NOPE + LANES * (j + 1)]
            qr, _, _ = _norm_fwd(xr, gqr_ref[...], half=True)
            ropes.append(_rope(qr, cos_t, sin_t))
        for h in range(nh):
            qn, _, _ = _norm_fwd(qb[:, MLA_NOPE * h: MLA_NOPE * (h + 1)], gqn_ref[...])
            mask = lo if h % 2 == 0 else jnp.logical_not(lo)
            qr = jnp.where(mask, ropes[h // 2], 0.0)
            qc_ref[h] = jnp.concatenate([qn, qr], axis=1).astype(qc_ref.dtype)
            kn, _, _ = _norm_fwd(kvb[:, 256 * h: 256 * h + MLA_NOPE], gkn_ref[...])
            kc_ref[h] = jnp.concatenate([kn, kr2], axis=1).astype(kc_ref.dtype)
            v_ref[h] = kvb[:, 256 * h + MLA_NOPE: 256 * (h + 1)].astype(v_ref.dtype)

    def col(width, start):
        return pl.BlockSpec((tm, width), lambda i: (i, start // width))

    def full(shape):
        return pl.BlockSpec(shape, lambda i: (0,) * len(shape))

    def row(width):
        return pl.BlockSpec((tm, width), lambda i: (i, 0))

    def heads(width):
        return pl.BlockSpec((nh, tm, width), lambda i: (0, i, 0))

    return _pcall(
        body, name="mla_prep", grid=(s // tm,),
        out_shape=[_sds((nh, s, 256), MXU), _sds((nh, s, 256), MXU), _sds((nh, s, MLA_V), MXU),
                   _sds((s, 768), jnp.float32), _sds((s, 1024), jnp.float32),
                   _sds((s, 512), MXU), _sds((s, 512), MXU)],
        in_specs=[col(512, C_CQ), col(512, C_CKV), col(LANES, C_KR), row(LANES), row(LANES),
                  full((1, 512)), full((1, 512)), full((768, 512)), full((N_DEV, 512, LANES)),
                  full((1, LANES)), full((1, LANES)), full((1, LANES)), full((1, LANES))],
        out_specs=[heads(256), heads(256), heads(MLA_V), row(768), row(1024), row(512), row(512)],
        sem=("parallel",))(proj, proj, proj, cos, sin, g_cq, g_ckv, w_uq, w_ukv, g_qn, g_qr, g_kn, g_kr)


def _mla_fwd(qc, kc, v):
    nh, s, _ = qc.shape
    t = min(ATT_TILE, s)
    nb = s // t
    scale = (MLA_NOPE + MLA_ROPE) ** -0.5

    def body(q_ref, k_ref, v_ref, y_ref, lse_ref, m_sc, l_sc, acc):
        qi, ki = pl.program_id(1), pl.program_id(2)

        @pl.when(ki == 0)
        def _():
            m_sc[...] = jnp.full_like(m_sc, NEG_INF)
            l_sc[...] = jnp.zeros_like(l_sc)
            acc[...] = jnp.zeros_like(acc)

        def step(diagonal):
            rc = t // 4 if diagonal else t
            for c in range(t // rc):
                rows = slice(rc * c, rc * (c + 1))
                keys = slice(0, rc * (c + 1))
                sc = _dot_nt(q_ref[0, rows, :], k_ref[0, keys, :]) * (scale * LOG2E)
                if diagonal:
                    r_i = lax.broadcasted_iota(jnp.int32, sc.shape, 0) + rc * c
                    c_i = lax.broadcasted_iota(jnp.int32, sc.shape, 1)
                    sc = jnp.where(c_i <= r_i, sc, NEG_INF)
                m_old = m_sc[rows, :]
                m_new = jnp.maximum(m_old, jnp.max(sc, -1, keepdims=True))
                alpha = jnp.exp2(m_old - m_new)
                p = jnp.exp2(sc - m_new)
                l_sc[rows, :] = alpha * l_sc[rows, :] + jnp.sum(p, -1, keepdims=True)
                acc[rows, :] = alpha * acc[rows, :] + _dot(p, v_ref[0, keys, :])
                m_sc[rows, :] = m_new

        @pl.when(ki < qi)
        def _():
            step(False)

        @pl.when(ki == qi)
        def _():
            step(True)

        @pl.when(ki == qi)
        def _():
            y_ref[...] = acc[...] / l_sc[...]
            lse_ref[0] = m_sc[...] + jnp.log2(l_sc[...])

    return _pcall(
        body, name="mla_fwd", grid=(nh, nb, nb),
        out_shape=[_sds((s, nh * MLA_V), jnp.float32), _sds((nh, s, 1), jnp.float32)],
        in_specs=[pl.BlockSpec((1, t, 256), lambda h, i, k: (h, i, 0)),
                  pl.BlockSpec((1, t, 256), lambda h, i, k: (h, jnp.minimum(k, i), 0)),
                  pl.BlockSpec((1, t, MLA_V), lambda h, i, k: (h, jnp.minimum(k, i), 0))],
        out_specs=[pl.BlockSpec((t, MLA_V), lambda h, i, k: (i, h)),
                   pl.BlockSpec((1, t, 1), lambda h, i, k: (h, i, 0))],
        scratch=[pltpu.VMEM((t, 1), jnp.float32), pltpu.VMEM((t, 1), jnp.float32),
                 pltpu.VMEM((t, MLA_V), jnp.float32)],
        sem=("parallel", "parallel", "arbitrary"))(qc, kc, v)


def _memkv_prep(mem, g_mem, w_mkv, g_mk):
    ml, d = mem.shape
    hw = MEM_HEADS * MEM_DIM

    def body(mem_ref, g_ref, w_ref, gk_ref, k_ref, v_ref, kv_ref, mn_ref):
        mn, _, _ = _norm_fwd(mem_ref[...], g_ref[...])
        mn_ref[...] = mn.astype(mn_ref.dtype)
        kv = _dot(mn, w_ref[...])
        kv_ref[...] = kv
        for h in range(MEM_HEADS):
            kn, _, _ = _norm_fwd(kv[:, MEM_DIM * h: MEM_DIM * (h + 1)], gk_ref[...])
            k_ref[:, MEM_DIM * h: MEM_DIM * (h + 1)] = kn.astype(k_ref.dtype)
        v_ref[...] = kv[:, hw:].astype(v_ref.dtype)

    vm = pl.BlockSpec(memory_space=pltpu.VMEM)
    return _pcall(
        body, name="memkv_prep",
        out_shape=[_sds((ml, hw), MXU), _sds((ml, hw), MXU), _sds((ml, 2 * hw), jnp.float32), _sds((ml, d), MXU)],
        in_specs=[vm] * 4, out_specs=[vm] * 4)(mem, g_mem, w_mkv, g_mk)


def _mem_fwd(proj, g_mq, km, vmm):
    s = proj.shape[0]
    ml, hw = km.shape
    tm = min(FFN_TILE, s)
    scale = MEM_DIM ** -0.5

    def body(q_ref, g_ref, k_ref, v_ref, y_ref, lse_ref):
        col = lax.broadcasted_iota(jnp.int32, (tm, MEM_HEADS), 1)
        lse_t = jnp.zeros((tm, MEM_HEADS), jnp.float32)
        for h in range(MEM_HEADS):
            sl = slice(MEM_DIM * h, MEM_DIM * (h + 1))
            qn, _, _ = _norm_fwd(q_ref[:, sl], g_ref[...])
            sc = _dot_nt(qn, k_ref[:, sl]) * scale
            m = jnp.max(sc, -1, keepdims=True)
            p = jnp.exp(sc - m)
            l = jnp.sum(p, -1, keepdims=True)
            y_ref[:, sl] = _dot(p, v_ref[:, sl]) / l
            lse_t = jnp.where(col == h, m + jnp.log(l), lse_t)
        lse_ref[...] = lse_t

    return _pcall(
        body, name="mem_fwd", grid=(s // tm,),
        out_shape=[_sds((s, hw), jnp.float32), _sds((s, MEM_HEADS), jnp.float32)],
        in_specs=[pl.BlockSpec((tm, hw), lambda i: (i, C_QM // hw)), pl.BlockSpec((1, MEM_DIM), lambda i: (0, 0)),
                  pl.BlockSpec((ml, hw), lambda i: (0, 0)), pl.BlockSpec((ml, hw), lambda i: (0, 0))],
        out_specs=[pl.BlockSpec((tm, hw), lambda i: (i, 0)), pl.BlockSpec((tm, MEM_HEADS), lambda i: (i, 0))],
        sem=("parallel",))(proj, g_mq, km, vmm)


def _alibi_slope(h):
    return float(2.0 ** (-8.0 * (h + 1) / SWA_Q_HEADS))


def _swa_common(n, kp, kc, vp, vc, pq, pkp, pkc, gk):
    b = SWA_BLOCK
    k_raw = jnp.concatenate([kp, kc], axis=0)
    kn, kxn, kr = _norm_fwd(k_raw, gk, half=True)
    v = jnp.concatenate([vp, vc], axis=0)
    dist = jnp.abs(pq - jnp.concatenate([pkp, pkc], axis=1))
    r_i = lax.broadcasted_iota(jnp.int32, (b, 2 * b), 0)
    c_i = lax.broadcasted_iota(jnp.int32, (b, 2 * b), 1)
    valid = (c_i > r_i) & (c_i <= r_i + b) & (c_i >= jnp.where(n > 0, 0, b))
    bias = jnp.where(valid, -dist, NEG_INF)
    return kn, v, bias


def _swa_specs(s):
    b = SWA_BLOCK
    prev = lambda n: jnp.maximum(n - 1, 0)
    return [
        pl.BlockSpec((b, 1024), lambda n: (n, C_QA // 1024)),
        pl.BlockSpec((b, LANES), lambda n: (prev(n), C_KA // LANES)),
        pl.BlockSpec((b, LANES), lambda n: (n, C_KA // LANES)),
        pl.BlockSpec((b, LANES), lambda n: (prev(n), C_VA // LANES)),
        pl.BlockSpec((b, LANES), lambda n: (n, C_VA // LANES)),
        pl.BlockSpec((b, 1), lambda n: (n, 0)),
        pl.BlockSpec((1, b), lambda n: (0, prev(n))),
        pl.BlockSpec((1, b), lambda n: (0, n)),
        pl.BlockSpec((1, LANES), lambda n: (0, 0)),
        pl.BlockSpec((1, LANES), lambda n: (0, 0)),
        pl.BlockSpec(memory_space=pltpu.SMEM),
    ]


def _swa_fwd(proj, posc, posr, gq, gk, sinks):
    s = proj.shape[0]
    b = SWA_BLOCK
    scale = SWA_DIM ** -0.5

    def body(q_ref, kp_ref, kc_ref, vp_ref, vc_ref, pq_ref, pkp_ref, pkc_ref, gq_ref, gk_ref, sink_ref,
             y_ref, lse_ref):
        n = pl.program_id(0)
        kn, v, bias = _swa_common(n, kp_ref[...], kc_ref[...], vp_ref[...], vc_ref[...],
                                  pq_ref[...], pkp_ref[...], pkc_ref[...], gk_ref[...])
        lo = _lo_mask((b, LANES))
        col = lax.broadcasted_iota(jnp.int32, (b, SWA_Q_HEADS), 1)
        lse_t = jnp.zeros((b, SWA_Q_HEADS), jnp.float32)
        hpg = SWA_Q_HEADS // SWA_KV_HEADS
        for g in range(SWA_KV_HEADS):
            heads = range(hpg * g, hpg * (g + 1))
            kvmask = lo if g == 0 else jnp.logical_not(lo)
            qs = []
            for j in range(hpg // 2 * g, hpg // 2 * (g + 1)):
                qn, _, _ = _norm_fwd(q_ref[:, LANES * j: LANES * (j + 1)], gq_ref[...], half=True)
                qn = qn * scale
                qsw = pltpu.roll(qn, 64, 1)
                qs += [jnp.where(kvmask, qn if e == g else qsw, 0.0) for e in range(2)]
            sc_st = _dot_nt(jnp.concatenate(qs, axis=0), kn)
            ps, ls = [], []
            for i, h in enumerate(heads):
                sc = sc_st[b * i: b * (i + 1)] + _alibi_slope(h) * bias
                sk = sink_ref[h]
                m = jnp.maximum(jnp.max(sc, -1, keepdims=True), sk)
                p = jnp.exp(sc - m)
                l = jnp.sum(p, -1, keepdims=True) + jnp.exp(sk - m)
                ps.append(p.astype(MXU))
                ls.append(l)
                lse_t = jnp.where(col == h, m + jnp.log(l), lse_t)
            o_st = _dot(jnp.concatenate(ps, axis=0), v)
            for j in range(hpg // 2 * g, hpg // 2 * (g + 1)):
                halves = []
                for e in range(2):
                    i = 2 * j + e - hpg * g
                    o_h = o_st[b * i: b * (i + 1)] / ls[i]
                    halves.append(o_h if e == g else pltpu.roll(o_h, 64, 1))
                y_ref[:, LANES * j: LANES * (j + 1)] = jnp.where(lo, halves[0], halves[1])
        lse_ref[...] = lse_t

    return _pcall(
        body, name="swa_fwd", grid=(s // b,),
        out_shape=[_sds((s, 1024), jnp.float32), _sds((s, SWA_Q_HEADS), jnp.float32)],
        in_specs=_swa_specs(s),
        out_specs=[pl.BlockSpec((b, 1024), lambda n: (n, 0)), pl.BlockSpec((b, SWA_Q_HEADS), lambda n: (n, 0))],
        sem=("parallel",))(proj, proj, proj, proj, proj, posc, posr, posr, gq, gk, sinks)


def _out_proj(y_a, y_b, y_m, x, w_out, g_ffn):
    s, d = x.shape
    tm = min(2 * ROW_TILE, s)

    def body(ya_ref, yb_ref, ym_ref, x_ref, w_ref, g_ref, h1_ref, fn_ref):
        y = jnp.concatenate([ya_ref[...].astype(MXU), yb_ref[...].astype(MXU), ym_ref[...].astype(MXU)], axis=1)
        h1 = x_ref[...] + _dot(y, w_ref[...])
        h1_ref[...] = h1
        fn, _, _ = _norm_fwd(h1, g_ref[...])
        fn_ref[...] = fn.astype(fn_ref.dtype)

    def row(width):
        return pl.BlockSpec((tm, width), lambda i: (i, 0))

    return _pcall(
        body, name="out_proj", grid=(s // tm,),
        out_shape=[_sds((s, d), jnp.float32), _sds((s, d), MXU)],
        in_specs=[row(1024), row(512), row(512), row(d),
                  pl.BlockSpec(w_out.shape, lambda i: (0, 0), pipeline_mode=pl.Buffered(1)),
                  pl.BlockSpec((1, d), lambda i: (0, 0))],
        out_specs=[row(d), row(d)], sem=("parallel",))(y_a, y_b, y_m, x, w_out, g_ffn)


def _ffn_gu(fn, w_gu):
    s, d = fn.shape
    f = w_gu.shape[2]
    tm = min(2 * FFN_TILE, s)

    def body(fn_ref, w_ref, gu_ref, act_ref):
        x = fn_ref[...]
        g = _dot_nt(x, w_ref[0, 0])
        u = _dot_nt(x, w_ref[0, 1])
        gu_ref[0, 0] = g
        gu_ref[0, 1] = u
        act_ref[0] = (g * jax.nn.sigmoid(g) * u).astype(act_ref.dtype)

    return _pcall(
        body, name="ffn_gate_up", grid=(N_DEV, s // tm),
        out_shape=[_sds((N_DEV, 2, s, f), jnp.float32), _sds((N_DEV, s, f), MXU)],
        in_specs=[pl.BlockSpec((tm, d), lambda j, i: (i, 0)),
                  pl.BlockSpec((1, 2, f, d), lambda j, i: (j, 0, 0, 0))],
        out_specs=[pl.BlockSpec((1, 2, tm, f), lambda j, i: (j, 0, i, 0)),
                   pl.BlockSpec((1, tm, f), lambda j, i: (j, i, 0))],
        sem=("parallel", "parallel"))(fn, w_gu)


def _ffn_down(act, w_d, h1, target):
    _, s, f = act.shape
    d = h1.shape[1]
    tm = min(FFN_TILE, s)

    def body(a_ref, w_ref, h1_ref, t_ref, dout_ref, doutb_ref, loss_ref, acc):
        i, j = pl.program_id(0), pl.program_id(1)
        part = _dot(a_ref[0], w_ref[0]) + _dot(a_ref[1], w_ref[1])

        @pl.when(j == 0)
        def _():
            acc[...] = h1_ref[...] + part

        @pl.when(j > 0)
        def _():
            acc[...] += part

        @pl.when((i == 0) & (j == 0))
        def _():
            loss_ref[...] = jnp.zeros_like(loss_ref)

        @pl.when(j == N_DEV // 2 - 1)
        def _():
            diff = acc[...] - t_ref[...]
            dout_ref[...] = diff / d
            doutb_ref[...] = (diff / d).astype(doutb_ref.dtype)
            loss_ref[...] += 0.5 * jnp.sum(jnp.sum(diff * diff, -1, keepdims=True) / d)

    row = pl.BlockSpec((tm, d), lambda i, j: (i, 0))
    return _pcall(
        body, name="ffn_down", grid=(s // tm, N_DEV // 2),
        out_shape=[_sds((s, d), jnp.float32), _sds((s, d), MXU), _sds((8, LANES), jnp.float32)],
        in_specs=[pl.BlockSpec((2, tm, f), lambda i, j: (j, i, 0)), pl.BlockSpec((2, f, d), lambda i, j: (j, 0, 0)),
                  row, row],
        out_specs=[row, row, pl.BlockSpec((8, LANES), lambda i, j: (0, 0))],
        scratch=[pltpu.VMEM((tm, d), jnp.float32)], sem=("arbitrary", "arbitrary"))(act, w_d, h1, target)


def _ffn_bwd_act(dout, w_d, gu):
    s, d = dout.shape
    f = w_d.shape[1]
    tm = min(2 * FFN_TILE, s)
    ni = s // tm

    def body(do_ref, w_ref, gu_ref, dgu_ref, dw_ref, acc):
        i = pl.program_id(1)
        do = do_ref[...]
        d_act = _dot_nt(do, w_ref[0])
        g, u = gu_ref[0, 0], gu_ref[0, 1]
        sig = jax.nn.sigmoid(g)
        silu = g * sig
        dgu_ref[0, 0] = (d_act * u * (sig * (1.0 + g * (1.0 - sig)))).astype(dgu_ref.dtype)
        dgu_ref[0, 1] = (d_act * silu).astype(dgu_ref.dtype)
        part = _dot_tn(silu * u, do)

        @pl.when(i == 0)
        def _():
            acc[...] = part

        @pl.when(i > 0)
        def _():
            acc[...] += part

        @pl.when(i == ni - 1)
        def _():
            dw_ref[0] = acc[...].astype(dw_ref.dtype)

    return _pcall(
        body, name="ffn_bwd_act", grid=(N_DEV, ni),
        out_shape=[_sds((N_DEV, 2, s, f), MXU), _sds((N_DEV, f, d), WIRE)],
        in_specs=[pl.BlockSpec((tm, d), lambda j, i: (i, 0)), pl.BlockSpec((1, f, d), lambda j, i: (j, 0, 0)),
                  pl.BlockSpec((1, 2, tm, f), lambda j, i: (j, 0, i, 0))],
        out_specs=[pl.BlockSpec((1, 2, tm, f), lambda j, i: (j, 0, i, 0)),
                   pl.BlockSpec((1, f, d), lambda j, i: (j, 0, 0))],
        scratch=[pltpu.VMEM((f, d), jnp.float32)], sem=("parallel", "arbitrary"))(dout, w_d, gu)


def _ffn_dw_gu(fn, dgu):
    s, d = fn.shape
    f = dgu.shape[-1]
    tk = min(4 * FFN_TILE, s)
    nk = s // tk

    def body(fn_ref, dgu_ref, dw_ref, acc):
        k = pl.program_id(2)
        part = _dot_tn(dgu_ref[0, 0], fn_ref[...])

        @pl.when(k == 0)
        def _():
            acc[...] = part

        @pl.when(k > 0)
        def _():
            acc[...] += part

        @pl.when(k == nk - 1)
        def _():
            dw_ref[0, 0] = acc[...].astype(dw_ref.dtype)

    return _pcall(
        body, name="ffn_dw_gate_up", grid=(N_DEV, 2, nk),
        out_shape=_sds((N_DEV, 2, f, d), WIRE),
        in_specs=[pl.BlockSpec((tk, d), lambda j, w, k: (k, 0)),
                  pl.BlockSpec((1, 1, tk, f), lambda j, w, k: (j, w, k, 0))],
        out_specs=pl.BlockSpec((1, 1, f, d), lambda j, w, k: (j, w, 0, 0)),
        scratch=[pltpu.VMEM((f, d), jnp.float32)], sem=("parallel", "parallel", "arbitrary"))(fn, dgu)


def _ffn_dfn(dgu, w_gu, after):
    _, _, s, f = dgu.shape
    d = w_gu.shape[3]
    tm = min(FFN_TILE, s)

    def body(dgu_ref, w_ref, dfn_ref):
        j = pl.program_id(1)
        part = (_dot(dgu_ref[0, 0], w_ref[0, 0]) + _dot(dgu_ref[0, 1], w_ref[0, 1])
                + _dot(dgu_ref[1, 0], w_ref[1, 0]) + _dot(dgu_ref[1, 1], w_ref[1, 1]))

        @pl.when(j == 0)
        def _():
            dfn_ref[...] = part

        @pl.when(j > 0)
        def _():
            dfn_ref[...] += part

    return _pcall(
        body, name="ffn_dfn", grid=(s // tm, N_DEV // 2),
        out_shape=_sds((s, d), jnp.float32),
        in_specs=[pl.BlockSpec((2, 2, tm, f), lambda i, j: (j, 0, i, 0)),
                  pl.BlockSpec((2, 2, f, d), lambda i, j: (j, 0, 0, 0))],
        out_specs=pl.BlockSpec((tm, d), lambda i, j: (i, 0)),
        sem=("parallel", "arbitrary"), after=after)(dgu, w_gu)


def _ffn_norm_bwd(d_fn, dout, h1, g_ffn):
    s, d = h1.shape
    tm = min(2 * ROW_TILE, s)

    def body(dfn_ref, do_ref, h1_ref, g_ref, dh1_ref, dg_ref):
        i = pl.program_id(0)

        @pl.when(i == 0)
        def _():
            dg_ref[...] = jnp.zeros_like(dg_ref)

        _, xn, r = _norm_fwd(h1_ref[...], g_ref[...])
        dx, dg = _norm_bwd(xn, r, g_ref[...], dfn_ref[...])
        dh1_ref[...] = do_ref[...] + dx
        dg_ref[...] += dg

    row = pl.BlockSpec((tm, d), lambda i: (i, 0))
    vec = pl.BlockSpec((1, d), lambda i: (0, 0))
    return _pcall(
        body, name="ffn_norm_bwd", grid=(s // tm,),
        out_shape=[_sds((s, d), jnp.float32), _sds((1, d), jnp.float32)],
        in_specs=[row, row, row, vec], out_specs=[row, vec], sem=("arbitrary",))(d_fn, dout, h1, g_ffn)


def _mem_bwd(proj, g_mq, km, vmm, d_y, y_m, lse):
    s = proj.shape[0]
    ml, hw = km.shape
    tm = min(FFN_TILE, s)
    scale = MEM_DIM ** -0.5

    def body(q_ref, g_ref, k_ref, v_ref, do_ref, y_ref, lse_ref, dq_ref, dk_ref, dv_ref, dg_ref):
        i = pl.program_id(0)

        @pl.when(i == 0)
        def _():
            dk_ref[...] = jnp.zeros_like(dk_ref)
            dv_ref[...] = jnp.zeros_like(dv_ref)
            dg_ref[...] = jnp.zeros_like(dg_ref)

        col = lax.broadcasted_iota(jnp.int32, (tm, MEM_HEADS), 1)
        lse_t = lse_ref[...]
        for h in range(MEM_HEADS):
            sl = slice(MEM_DIM * h, MEM_DIM * (h + 1))
            qn, xn, r = _norm_fwd(q_ref[:, sl], g_ref[...])
            lse_h = jnp.sum(jnp.where(col == h, lse_t, 0.0), -1, keepdims=True)
            p = jnp.exp(_dot_nt(qn, k_ref[:, sl]) * scale - lse_h)
            do = do_ref[:, sl]
            dd = jnp.sum(do * y_ref[:, sl], -1, keepdims=True)
            dp = _dot_nt(do, v_ref[:, sl])
            ds = (p * (dp - dd)).astype(MXU)
            dv_ref[:, sl] += _dot_tn(p, do)
            dk_ref[:, sl] += _dot_tn(ds, qn) * scale
            dx, dg = _norm_bwd(xn, r, g_ref[...], _dot(ds, k_ref[:, sl]) * scale)
            dq_ref[:, sl] = dx.astype(dq_ref.dtype)
            dg_ref[...] += dg

    full = pl.BlockSpec((ml, hw), lambda i: (0, 0))
    return _pcall(
        body, name="mem_bwd", grid=(s // tm,),
        out_shape=[_sds((s, hw), MXU), _sds((ml, hw), jnp.float32), _sds((ml, hw), jnp.float32),
                   _sds((1, MEM_DIM), jnp.float32)],
        in_specs=[pl.BlockSpec((tm, hw), lambda i: (i, C_QM // hw)), pl.BlockSpec((1, MEM_DIM), lambda i: (0, 0)),
                  full, full, pl.BlockSpec((tm, hw), lambda i: (i, 3)), pl.BlockSpec((tm, hw), lambda i: (i, 0)),
                  pl.BlockSpec((tm, MEM_HEADS), lambda i: (i, 0))],
        out_specs=[pl.BlockSpec((tm, hw), lambda i: (i, 0)), full, full,
                   pl.BlockSpec((1, MEM_DIM), lambda i: (0, 0))],
        sem=("arbitrary",))(proj, g_mq, km, vmm, d_y, y_m, lse)


def _memkv_bwd(mem, g_mem, w_mkv, g_mk, kv, memn, dk, dv):
    ml, d = mem.shape
    hw = MEM_HEADS * MEM_DIM

    def body(mem_ref, g_ref, w_ref, gk_ref, kv_ref, mn_ref, dk_ref, dv_ref, dw_ref, dgm_ref, dgk_ref):
        parts = []
        dgk = jnp.zeros((1, MEM_DIM), jnp.float32)
        for h in range(MEM_HEADS):
            sl = slice(MEM_DIM * h, MEM_DIM * (h + 1))
            _, xn, r = _norm_fwd(kv_ref[:, sl], gk_ref[...])
            dx, dg = _norm_bwd(xn, r, gk_ref[...], dk_ref[:, sl])
            parts.append(dx)
            dgk = dgk + dg
        dkv = jnp.concatenate(parts + [dv_ref[...]], axis=1).astype(MXU)
        dgk_ref[...] = dgk
        dw_ref[...] = _dot_tn(mn_ref[...], dkv).astype(dw_ref.dtype)
        d_mn = _dot_nt(dkv, w_ref[...])
        _, xn, _ = _norm_fwd(mem_ref[...], g_ref[...])
        dgm_ref[...] = jnp.sum(d_mn * xn, 0, keepdims=True)

    vm = pl.BlockSpec(memory_space=pltpu.VMEM)
    return _pcall(
        body, name="memkv_bwd",
        out_shape=[_sds((d, 2 * hw), WIRE), _sds((1, d), jnp.float32), _sds((1, MEM_DIM), jnp.float32)],
        in_specs=[vm] * 8, out_specs=[vm] * 3)(mem, g_mem, w_mkv, g_mk, kv, memn, dk, dv)


def _mla_bwd(qc, kc, v, d_y, y_b, lse, after):
    nh, s, _ = qc.shape
    t = min(ATT_TILE, s)
    nb = s // t
    scale = (MLA_NOPE + MLA_ROPE) ** -0.5

    def body(q_ref, k_ref, v_ref, do_ref, y_ref, lse_ref, dq_ref, dk_ref, dv_ref, dk_acc, dv_acc):
        kj, qi = pl.program_id(1), pl.program_id(2)

        @pl.when((kj == 0) & (qi == 0))
        def _():
            dq_ref[...] = jnp.zeros_like(dq_ref)

        @pl.when(qi == kj)
        def _():
            dk_acc[...] = jnp.zeros_like(dk_acc)
            dv_acc[...] = jnp.zeros_like(dv_acc)

        def step(diagonal):
            rc = t // 4 if diagonal else t
            for c in range(t // rc):
                rows = slice(rc * c, rc * (c + 1))
                keys = slice(0, rc * (c + 1))
                q, k = q_ref[0, rows, :], k_ref[0, keys, :]
                sc = _dot_nt(q, k) * (scale * LOG2E)
                if diagonal:
                    r_i = lax.broadcasted_iota(jnp.int32, sc.shape, 0) + rc * c
                    c_i = lax.broadcasted_iota(jnp.int32, sc.shape, 1)
                    sc = jnp.where(c_i <= r_i, sc, NEG_INF)
                p = jnp.exp2(sc - lse_ref[0, rows, :])
                do = do_ref[rows, :]
                dd = jnp.sum(do * y_ref[rows, :], -1, keepdims=True)
                dp = _dot_nt(do, v_ref[0, keys, :])
                ds = (p * (dp - dd) * scale).astype(MXU)
                dv_acc[keys, :] += _dot_tn(p, do)
                dk_acc[keys, :] += _dot_tn(ds, q)
                out_rows = pl.ds(pl.multiple_of(qi * t + rc * c, rc), rc)
                dq_ref[0, out_rows, :] += _dot(ds, k)

        @pl.when(qi > kj)
        def _():
            step(False)

        @pl.when(qi == kj)
        def _():
            step(True)

        @pl.when(qi == nb - 1)
        def _():
            dk_ref[0] = dk_acc[...]
            dv_ref[0] = dv_acc[...]

    qmap = lambda h, j, i: (h, jnp.maximum(i, j), 0)
    return _pcall(
        body, name="mla_bwd", grid=(nh, nb, nb),
        out_shape=[_sds((nh, s, 256), jnp.float32), _sds((nh, s, 256), jnp.float32),
                   _sds((nh, s, MLA_V), jnp.float32)],
        in_specs=[pl.BlockSpec((1, t, 256), qmap),
                  pl.BlockSpec((1, t, 256), lambda h, j, i: (h, j, 0)),
                  pl.BlockSpec((1, t, MLA_V), lambda h, j, i: (h, j, 0)),
                  pl.BlockSpec((t, MLA_V), lambda h, j, i: (jnp.maximum(i, j), 8 + h)),
                  pl.BlockSpec((t, MLA_V), lambda h, j, i: (jnp.maximum(i, j), h)),
                  pl.BlockSpec((1, t, 1), qmap)],
        out_specs=[pl.BlockSpec((1, s, 256), lambda h, j, i: (h, 0, 0)),
                   pl.BlockSpec((1, t, 256), lambda h, j, i: (h, j, 0)),
                   pl.BlockSpec((1, t, MLA_V), lambda h, j, i: (h, j, 0))],
        scratch=[pltpu.VMEM((t, 256), jnp.float32), pltpu.VMEM((t, MLA_V), jnp.float32)],
        sem=("parallel", "arbitrary", "arbitrary"), after=after)(qc, kc, v, d_y, y_b, lse)


def _mla_prep_bwd(proj, cos, sin, g_cq, g_ckv, w_uq, w_ukv, g_qn, g_qr, g_kn, g_kr,
                  qb, kvb, cqn, ckvn, dqc, dkc, dv):
    s = proj.shape[0]
    tm = min(ROW_TILE, s)
    nh = MLA_HEADS
    ni = s // tm

    def body(cq_ref, ckv_ref, kr_ref, cos_ref, sin_ref, gcq_ref, gckv_ref, wuq_ref, wukv_ref,
             gqn_ref, gqr_ref, gkn_ref, gkr_ref, qb_ref, kvb_ref, cqn_ref, ckvn_ref, dqc_ref, dkc_ref, dv_ref,
             dcq_ref, dckv_ref, dkr_ref, dwuq_ref, dwukv_ref,
             dgcq_ref, dgckv_ref, dgqn_ref, dgqr_ref, dgkn_ref, dgkr_ref, acc_uq, acc_ukv):
        i = pl.program_id(0)

        @pl.when(i == 0)
        def _():
            acc_uq[...] = jnp.zeros_like(acc_uq)
            acc_ukv[...] = jnp.zeros_like(acc_ukv)
            for ref in (dgcq_ref, dgckv_ref, dgqn_ref, dgqr_ref, dgkn_ref, dgkr_ref):
                ref[...] = jnp.zeros_like(ref)

        cos_t, sin_t = cos_ref[...], sin_ref[...]
        lo = _lo_mask((tm, LANES))
        qb_v, kvb_v = qb_ref[...], kvb_ref[...]
        dq_parts, dgqn = [], jnp.zeros((1, LANES), jnp.float32)
        for h in range(nh):
            _, xn, r = _norm_fwd(qb_v[:, MLA_NOPE * h: MLA_NOPE * (h + 1)], gqn_ref[...])
            dx, dg = _norm_bwd(xn, r, gqn_ref[...], dqc_ref[h][:, :MLA_NOPE])
            dq_parts.append(dx)
            dgqn = dgqn + dg
        dgqn_ref[...] += dgqn
        dgqr = jnp.zeros((1, LANES), jnp.float32)
        for j in range(nh // 2):
            d_rope = jnp.where(lo, dqc_ref[2 * j][:, MLA_NOPE:], dqc_ref[2 * j + 1][:, MLA_NOPE:])
            d_pre = _rope_bwd(d_rope, cos_t, sin_t)
            xr = qb_v[:, nh * MLA_NOPE + LANES * j: nh * MLA_NOPE + LANES * (j + 1)]
            _, xn, r = _norm_fwd(xr, gqr_ref[...], half=True)
            dx, dg = _norm_bwd(xn, r, gqr_ref[...], d_pre, half=True)
            dq_parts.append(dx)
            dgqr = dgqr + dg
        dgqr_ref[...] += dgqr
        dqb = jnp.concatenate(dq_parts, axis=1).astype(MXU)
        acc_uq[...] += _dot_tn(dqb, cqn_ref[...])
        _, xn, r = _norm_fwd(cq_ref[...], gcq_ref[...])
        dx, dg = _norm_bwd(xn, r, gcq_ref[...], _dot(dqb, wuq_ref[...]))
        dcq_ref[...] = dx.astype(dcq_ref.dtype)
        dgcq_ref[...] += dg
        dkv_parts, dgkn = [], jnp.zeros((1, LANES), jnp.float32)
        d_kr2 = jnp.zeros((tm, LANES), jnp.float32)
        for h in range(nh):
            _, xn, r = _norm_fwd(kvb_v[:, 256 * h: 256 * h + MLA_NOPE], gkn_ref[...])
            dx, dg = _norm_bwd(xn, r, gkn_ref[...], dkc_ref[h][:, :MLA_NOPE])
            dkv_parts += [dx, dv_ref[h]]
            dgkn = dgkn + dg
            d_kr2 = d_kr2 + dkc_ref[h][:, MLA_NOPE:]
        dgkn_ref[...] += dgkn
        dkvb = jnp.concatenate(dkv_parts, axis=1).astype(MXU)
        d_ckvn = jnp.zeros((tm, 512), jnp.float32)
        for dev in range(N_DEV):
            piece = dkvb[:, LANES * dev: LANES * (dev + 1)]
            acc_ukv[dev] += _dot_tn(ckvn_ref[...], piece)
            d_ckvn = d_ckvn + _dot_nt(piece, wukv_ref[dev])
        _, xn, r = _norm_fwd(ckv_ref[...], gckv_ref[...])
        dx, dg = _norm_bwd(xn, r, gckv_ref[...], d_ckvn)
        dckv_ref[...] = dx.astype(dckv_ref.dtype)
        dgckv_ref[...] += dg
        d_kr = jnp.where(lo, d_kr2 + pltpu.roll(d_kr2, 64, 1), 0.0)
        d_pre = _rope_bwd(d_kr, cos_t, sin_t)
        _, xn, r = _norm_fwd(kr_ref[...], gkr_ref[...], half=True)
        dx, dg = _norm_bwd(xn, r, gkr_ref[...], d_pre, half=True)
        dkr_ref[...] = jnp.where(lo, dx, 0.0).astype(dkr_ref.dtype)
        dgkr_ref[...] += jnp.where(_lo_mask((1, LANES)), dg, 0.0)

        @pl.when(i == ni - 1)
        def _():
            dwuq_ref[...] = acc_uq[...].astype(dwuq_ref.dtype)
            dwukv_ref[...] = acc_ukv[...].astype(dwukv_ref.dtype)

    def col(width, start):
        return pl.BlockSpec((tm, width), lambda i: (i, start // width))

    def full(shape):
        return pl.BlockSpec(shape, lambda i: (0,) * len(shape))

    def row(width):
        return pl.BlockSpec((tm, width), lambda i: (i, 0))

    def heads(width):
        return pl.BlockSpec((nh, tm, width), lambda i: (0, i, 0))

    vec = full((1, LANES))
    return _pcall(
        body, name="mla_prep_bwd", grid=(ni,),
        out_shape=[_sds((s, 512), MXU), _sds((s, 512), MXU), _sds((s, LANES), MXU),
                   _sds((768, 512), WIRE), _sds((N_DEV, 512, LANES), WIRE),
                   _sds((1, 512), jnp.float32), _sds((1, 512), jnp.float32)] + [_sds((1, LANES), jnp.float32)] * 4,
        in_specs=[col(512, C_CQ), col(512, C_CKV), col(LANES, C_KR), row(LANES), row(LANES),
                  full((1, 512)), full((1, 512)), full((768, 512)), full((N_DEV, 512, LANES)), vec, vec, vec, vec,
                  row(768), row(1024), row(512), row(512), heads(256), heads(256), heads(MLA_V)],
        out_specs=[row(512), row(512), row(LANES), full((768, 512)), full((N_DEV, 512, LANES)),
                   full((1, 512)), full((1, 512)), vec, vec, vec, vec],
        scratch=[pltpu.VMEM((768, 512), jnp.float32), pltpu.VMEM((N_DEV, 512, LANES), jnp.float32)],
        sem=("arbitrary",))(proj, proj, proj, cos, sin, g_cq, g_ckv, w_uq, w_ukv, g_qn, g_qr, g_kn, g_kr,
                            qb, kvb, cqn, ckvn, dqc, dkc, dv)


def _swa_bwd(proj, posc, posr, gq, gk, sinks, d_y, y_a, lse, after):
    s = proj.shape[0]
    b = SWA_BLOCK
    nb = s // b
    scale = SWA_DIM ** -0.5

    def body(q_ref, kp_ref, kc_ref, vp_ref, vc_ref, pq_ref, pkp_ref, pkc_ref, gq_ref, gk_ref, sink_ref,
             do_ref, y_ref, lse_ref, kfull_ref,
             dq_ref, dk_ref, dv_ref, dgq_ref, dgk_ref, dsink_ref, dk_acc, dv_acc):
        n = pl.program_id(0)

        @pl.when(n == 0)
        def _():
            dk_acc[...] = jnp.zeros_like(dk_acc)
            dv_acc[...] = jnp.zeros_like(dv_acc)
            dgq_ref[...] = jnp.zeros_like(dgq_ref)
            dsink_ref[...] = jnp.zeros_like(dsink_ref)

        kn, v, bias = _swa_common(n, kp_ref[...], kc_ref[...], vp_ref[...], vc_ref[...],
                                  pq_ref[...], pkp_ref[...], pkc_ref[...], gk_ref[...])
        lo = _lo_mask((b, LANES))
        col = lax.broadcasted_iota(jnp.int32, (b, SWA_Q_HEADS), 1)
        col1 = lax.broadcasted_iota(jnp.int32, (1, SWA_Q_HEADS), 1)
        lse_t = lse_ref[...]
        dk_blk = jnp.zeros((2 * b, LANES), jnp.float32)
        dv_blk = jnp.zeros((2 * b, LANES), jnp.float32)
        dgq = jnp.zeros((1, LANES), jnp.float32)
        dsink = jnp.zeros((1, SWA_Q_HEADS), jnp.float32)
        for j in range(SWA_Q_HEADS // 2):
            hk = (2 * j) // (SWA_Q_HEADS // SWA_KV_HEADS)
            kvmask = lo if hk == 0 else jnp.logical_not(lo)
            sl = slice(LANES * j, LANES * (j + 1))
            qn, xn, r = _norm_fwd(q_ref[:, sl], gq_ref[...], half=True)
            qn = qn * scale
            qsw = pltpu.roll(qn, 64, 1)
            d2 = do_ref[:, sl]
            d2sw = pltpu.roll(d2, 64, 1)
            prod = d2 * y_ref[:, sl]
            dqs = []
            for e in range(2):
                h = 2 * j + e
                half_e = lo if e == 0 else jnp.logical_not(lo)
                qm = jnp.where(kvmask, qn if e == hk else qsw, 0.0)
                dm = jnp.where(kvmask, d2 if e == hk else d2sw, 0.0)
                sc = _dot_nt(qm, kn) + _alibi_slope(h) * bias
                lse_h = jnp.sum(jnp.where(col == h, lse_t, 0.0), -1, keepdims=True)
                p = jnp.exp(sc - lse_h)
                dd = jnp.sum(jnp.where(half_e, prod, 0.0), -1, keepdims=True)
                dp = _dot_nt(dm, v)
                ds = (p * (dp - dd)).astype(MXU)
                dsink = dsink - jnp.where(col1 == h, jnp.sum(jnp.exp(sink_ref[h] - lse_h) * dd), 0.0)
                dq_m = _dot(ds, kn) * scale
                dk_blk = dk_blk + _dot_tn(ds, qm)
                dv_blk = dv_blk + _dot_tn(p, dm)
                dqs.append(dq_m if e == hk else pltpu.roll(dq_m, 64, 1))
            dx, dg = _norm_bwd(xn, r, gq_ref[...], jnp.where(lo, dqs[0], dqs[1]), half=True)
            dq_ref[:, sl] = dx.astype(dq_ref.dtype)
            dgq = dgq + dg
        dgq_ref[...] += dgq
        dsink_ref[...] += dsink
        prev = pl.ds(pl.multiple_of(jnp.maximum(n - 1, 0) * b, b), b)
        cur = pl.ds(pl.multiple_of(n * b, b), b)
        dk_acc[prev, :] += dk_blk[:b]
        dv_acc[prev, :] += dv_blk[:b]
        dk_acc[cur, :] += dk_blk[b:]
        dv_acc[cur, :] += dv_blk[b:]

        @pl.when(n == nb - 1)
        def _():
            _, kxn, kr = _norm_fwd(kfull_ref[...], gk_ref[...], half=True)
            dx, dg = _norm_bwd(kxn, kr, gk_ref[...], dk_acc[...], half=True)
            dk_ref[...] = dx.astype(dk_ref.dtype)
            dv_ref[...] = dv_acc[...].astype(dv_ref.dtype)
            dgk_ref[...] = dg

    full = pl.BlockSpec((s, LANES), lambda n: (0, 0))
    vec = pl.BlockSpec((1, LANES), lambda n: (0, 0))
    return _pcall(
        body, name="swa_bwd", grid=(nb,),
        out_shape=[_sds((s, 1024), MXU), _sds((s, LANES), MXU), _sds((s, LANES), MXU),
                   _sds((1, LANES), jnp.float32), _sds((1, LANES), jnp.float32),
                   _sds((1, SWA_Q_HEADS), jnp.float32)],
        in_specs=_swa_specs(s) + [pl.BlockSpec((b, 1024), lambda n: (n, 0)), pl.BlockSpec((b, 1024), lambda n: (n, 0)),
                                  pl.BlockSpec((b, SWA_Q_HEADS), lambda n: (n, 0)),
                                  pl.BlockSpec((s, LANES), lambda n: (0, C_KA // LANES))],
        out_specs=[pl.BlockSpec((b, 1024), lambda n: (n, 0)), full, full, vec, vec,
                   pl.BlockSpec((1, SWA_Q_HEADS), lambda n: (0, 0))],
        scratch=[pltpu.VMEM((s, LANES), jnp.float32), pltpu.VMEM((s, LANES), jnp.float32)],
        sem=("arbitrary",), after=after)(proj, proj, proj, proj, proj, posc, posr, posr, gq, gk, sinks, d_y, y_a, lse,
                                         proj)


def _dx(d_proj, w_in, x, g, d_h1, after):
    s, d = x.shape
    n = w_in.shape[0]
    tm = min(2 * ROW_TILE, s)

    def body(dp_ref, w_ref, x_ref, g_ref, dh_ref, dx_ref, dg_ref):
        i = pl.program_id(0)

        @pl.when(i == 0)
        def _():
            dg_ref[...] = jnp.zeros_like(dg_ref)

        d_hn = _dot(dp_ref[...], w_ref[...])
        _, xn, r = _norm_fwd(x_ref[...], g_ref[...])
        dx, dg = _norm_bwd(xn, r, g_ref[...], d_hn)
        dx_ref[...] = dh_ref[...] + dx
        dg_ref[...] += dg

    row = pl.BlockSpec((tm, d), lambda i: (i, 0))
    vec = pl.BlockSpec((1, d), lambda i: (0, 0))
    return _pcall(
        body, name="grad_x", grid=(s // tm,),
        out_shape=[_sds((s, d), jnp.float32), _sds((1, d), jnp.float32)],
        in_specs=[pl.BlockSpec((tm, n), lambda i: (i, 0)),
                  pl.BlockSpec((n, d), lambda i: (0, 0), pipeline_mode=pl.Buffered(1)), row, vec, row],
        out_specs=[row, vec], sem=("arbitrary",), after=after)(d_proj, w_in, x, g, d_h1)


_SMALL = ["attn_norm_g", "swa_q_norm_g", "swa_k_norm_g", "swa_sinks", "mla_cq_norm_g", "mla_ckv_norm_g",
          "mla_qn_norm_g", "mla_qr_norm_g", "mla_kn_norm_g", "mla_kr_norm_g", "mem_norm_g",
          "mem_q_norm_g", "mem_k_norm_g", "ffn_norm_g"]


def kernel(x, mem, positions, attn_norm_g, w_in, swa_q_norm_g, swa_k_norm_g, swa_sinks, mla_cq_norm_g, mla_ckv_norm_g, w_uq, w_ukv, mla_qn_norm_g, mla_qr_norm_g, mla_kn_norm_g, mla_kr_norm_g, mem_norm_g, w_mem_kv, mem_q_norm_g, mem_k_norm_g, w_out, ffn_norm_g, w_gate, w_up, w_down, loss_target, m_attn_norm_g, m_w_in, m_swa_q_norm_g, m_swa_k_norm_g, m_swa_sinks, m_mla_cq_norm_g, m_mla_ckv_norm_g, m_w_uq, m_w_ukv, m_mla_qn_norm_g, m_mla_qr_norm_g, m_mla_kn_norm_g, m_mla_kr_norm_g, m_mem_norm_g, m_w_mem_kv, m_mem_q_norm_g, m_mem_k_norm_g, m_w_out, m_ffn_norm_g, m_w_gate, m_w_up, m_w_down, v_attn_norm_g, v_w_in, v_swa_q_norm_g, v_swa_k_norm_g, v_swa_sinks, v_mla_cq_norm_g, v_mla_ckv_norm_g, v_w_uq, v_w_ukv, v_mla_qn_norm_g, v_mla_qr_norm_g, v_mla_kn_norm_g, v_mla_kr_norm_g, v_mem_norm_g, v_w_mem_kv, v_mem_q_norm_g, v_mem_k_norm_g, v_w_out, v_ffn_norm_g, v_w_gate, v_w_up, v_w_down):
    args = dict(locals())
    x2, mem2, tgt = x[0], mem[0], loss_target[0]
    s, d = x2.shape
    n_in = w_in.shape[2]
    f = w_gate.shape[2]

    (g_in,) = _all_gather([w_in[0].T.astype(WIRE)])
    mix_shards = [w_uq[0].T.astype(WIRE), w_ukv[0].astype(WIRE), w_mem_kv[0].astype(WIRE),
                  _to_wire([w_out[0]], g_in, "wire_out")[0]]
    g_uq, wkv, g_mkv, g_out = _all_gather_background(mix_shards, 5, "all_gather_mix_weights")
    ffn_shards = [_to_wire([w_gate[0].T, w_up[0].T], g_in, "wire_gate_up"),
                  _to_wire([w_down[0]], g_in, "wire_down")[0]]
    w_gu, w_d = _all_gather_background(ffn_shards, 1, "all_gather_ffn_weights")
    wi = g_in.reshape(N_DEV * n_in, d)
    wi = jnp.concatenate([wi[0:1024], wi[1280:1792], wi[1792:2304], wi[2368:2880],
                          wi[1024:1152], wi[1152:1280], wi[2304:2368],
                          jnp.zeros((IN_PAD - 2880, d), wi.dtype)], axis=0)
    wq = g_uq.reshape(768, 512)
    wq = jnp.concatenate([wq[192 * h: 192 * h + 128] for h in range(4)]
                         + [wq[192 * h + 128: 192 * (h + 1)] for h in range(4)], axis=0)
    wmkv = g_mkv.reshape(-1, g_mkv.shape[-1])
    wo = g_out.reshape(-1, d)

    pos = positions[0].astype(jnp.float32)
    inv_freq = ROPE_THETA ** (-jnp.arange(0, MLA_ROPE, 2, dtype=jnp.float32) / MLA_ROPE)
    ang = pos[:, None] * inv_freq
    cos32, sin32 = jnp.cos(ang), jnp.sin(ang)
    cos_t = jnp.tile(cos32, (1, 4))
    sin_t = jnp.tile(jnp.concatenate([-sin32, sin32], axis=1), (1, 2))
    posc, posr = pos.reshape(s, 1), pos.reshape(1, s)
    two = lambda g: jnp.tile(g, (1, 2))
    gq2, gk2, gqr2, gkr2 = two(swa_q_norm_g), two(swa_k_norm_g), two(mla_qr_norm_g), two(mla_kr_norm_g)
    sinks1 = swa_sinks[0]

    proj, hn = _in_proj(x2, attn_norm_g, wi)
    qc, kc, vb, qb, kvb, cqn, ckvn = _mla_prep(proj, cos_t, sin_t, mla_cq_norm_g, mla_ckv_norm_g, wq, wkv,
                                                mla_qn_norm_g, gqr2, mla_kn_norm_g, gkr2)
    y_b, lse_b = _mla_fwd(qc, kc, vb)
    km, vmm, kvm, memn = _memkv_prep(mem2, mem_norm_g, wmkv, mem_k_norm_g)
    y_m, lse_m = _mem_fwd(proj, mem_q_norm_g, km, vmm)
    y_a, lse_a = _swa_fwd(proj, posc, posr, gq2, gk2, sinks1)
    h1, fn = _out_proj(y_a, y_b, y_m, x2, wo, ffn_norm_g)
    gu, act = _ffn_gu(fn, w_gu)
    dout, dout_b, loss_tile = _ffn_down(act, w_d, h1, tgt)

    dgu, dw_d = _ffn_bwd_act(dout_b, w_d, gu)
    dw_gu = _ffn_dw_gu(fn, dgu)
    r_gu, r_d = _exchange_grads_background([dw_gu, dw_d], 2, "exchange_ffn_grads")
    d_h1, dg_ffn = _ffn_norm_bwd(_ffn_dfn(dgu, w_gu, dw_gu), dout, h1, ffn_norm_g)
    d_y = _mm(d_h1, wo, tb=True, out_dtype=jnp.float32, tm=FFN_TILE, tk=2048, name="d_mix")
    dw_out = jnp.concatenate([
        _mm(y_a, d_h1, ta=True, out_dtype=WIRE, tm=1024, tk=1024, name="dw_out_a"),
        _mm(y_b, d_h1, ta=True, out_dtype=WIRE, tm=1024, tk=1024, name="dw_out_b"),
        _mm(y_m, d_h1, ta=True, out_dtype=WIRE, tm=1024, tk=1024, name="dw_out_m")], axis=0)
    d_qm, dkm, dvmm, dg_mq = _mem_bwd(proj, mem_q_norm_g, km, vmm, d_y, y_m, lse_m)
    dw_mkv, dg_mem, dg_mk = _memkv_bwd(mem2, mem_norm_g, wmkv, mem_k_norm_g, kvm, memn, dkm, dvmm)
    r_mkv, r_out = _exchange_grads_background([dw_mkv.reshape(g_mkv.shape), dw_out.reshape(g_out.shape)], 3,
                                              "exchange_mix_grads")
    dqc, dkc, dvb = _mla_bwd(qc, kc, vb, d_y, y_b, lse_b, dw_mkv)
    (d_cq, d_ckv, d_kr, dw_uq, dw_ukv, dg_cq, dg_ckv, dg_qn, dg_qr, dg_kn, dg_kr) = _mla_prep_bwd(
        proj, cos_t, sin_t, mla_cq_norm_g, mla_ckv_norm_g, wq, wkv, mla_qn_norm_g, gqr2, mla_kn_norm_g, gkr2,
        qb, kvb, cqn, ckvn, dqc, dkc, dvb)
    d_qa, d_ka, d_va, dg_q, dg_k, d_sinks = _swa_bwd(proj, posc, posr, gq2, gk2, sinks1, d_y, y_a, lse_a, dw_out)
    d_proj = jnp.concatenate([d_qa, d_cq, d_ckv, d_qm, d_ka, d_va, d_kr], axis=1)
    gi = _dw_in(hn, d_proj, n_in)

    gq_ = jnp.concatenate(sum([[dw_uq[128 * h: 128 * (h + 1)], dw_uq[512 + 64 * h: 512 + 64 * (h + 1)]]
                               for h in range(4)], []), axis=0)
    gq_ = gq_.reshape(N_DEV, 96, 512)
    r_in, r_uq, r_ukv = _exchange_grads_background([gi, gq_, dw_ukv], 4, "exchange_in_grads")
    grad_x, dg_attn = _dx(d_proj, wi, x2, attn_norm_g, d_h1, gi)

    big = {}
    last = [None]

    def adam(name, r, transposed=False, which=None):
        w, m, v = args[name][0], args["m_" + name][0], args["v_" + name][0]
        if transposed:
            outs = _adam_big(r, w.T, m.T, v.T, "adam_" + name, last[0], which)
            big[name] = [o.T[None] for o in outs]
        else:
            outs = _adam_big(r, w, m, v, "adam_" + name, last[0])
            big[name] = [o[None] for o in outs]
        last[0] = outs[0]

    adam("w_gate", r_gu, True, which=0)
    adam("w_up", r_gu, True, which=1)
    adam("w_down", r_d)
    adam("w_out", r_out)
    adam("w_mem_kv", r_mkv)
    adam("w_in", r_in, True)
    adam("w_uq", r_uq, True)
    adam("w_ukv", r_ukv)

    small_g = {
        "attn_norm_g": dg_attn, "swa_q_norm_g": dg_q, "swa_k_norm_g": dg_k,
        "swa_sinks": d_sinks, "mla_cq_norm_g": dg_cq, "mla_ckv_norm_g": dg_ckv, "mla_qn_norm_g": dg_qn,
        "mla_qr_norm_g": dg_qr, "mla_kn_norm_g": dg_kn, "mla_kr_norm_g": dg_kr,
        "mem_norm_g": dg_mem, "mem_q_norm_g": dg_mq, "mem_k_norm_g": dg_mk, "ffn_norm_g": dg_ffn}
    packed_g = _small_allreduce([small_g[n] for n in _SMALL], loss_tile, [args[n].shape[-1] for n in _SMALL])
    loss11, small_out = _small_adam(packed_g, [args[n] for n in _SMALL],
                                    [args["m_" + n] for n in _SMALL], [args["v_" + n] for n in _SMALL])
    small = dict(zip(_SMALL, small_out))
    loss = loss11.reshape(())

    order = ["attn_norm_g", "w_in", "swa_q_norm_g", "swa_k_norm_g", "swa_sinks", "mla_cq_norm_g", "mla_ckv_norm_g",
             "w_uq", "w_ukv", "mla_qn_norm_g", "mla_qr_norm_g", "mla_kn_norm_g", "mla_kr_norm_g", "mem_norm_g",
             "w_mem_kv", "mem_q_norm_g", "mem_k_norm_g", "w_out", "ffn_norm_g", "w_gate", "w_up", "w_down"]
    res = {n: (big[n] if n in big else list(small[n])) for n in order}
    outs = [loss, grad_x[None]]
    for kind in range(4):
        outs += [res[n][kind] for n in order]
    return tuple(outs)
```

```python
import jax
import jax.numpy as jnp
from jax import lax
from jax.experimental import pallas as pl
from jax.experimental.pallas import tpu as pltpu
from jax.experimental.pallas import tpu_sc as plsc

MXU = jnp.bfloat16
WIRE = jnp.bfloat16
EPS = 1e-6
NEG_INF = -1e30
LOG2E = 1.4426950408889634
N_DEV = 8
LANES = 128
ROW_TILE = 256
FFN_TILE = 512
ATT_TILE = 1024
SWA_BLOCK = 128
VMEM_LIMIT = 56 * 1024 * 1024

SWA_Q_HEADS, SWA_KV_HEADS, SWA_DIM = 16, 2, 64
MLA_HEADS, MLA_NOPE, MLA_ROPE, MLA_V = 4, 128, 64, 128
MEM_HEADS, MEM_DIM = 4, 128
ROPE_THETA = 10000.0
ADAM_LR, ADAM_B1, ADAM_B2, ADAM_EPS, ADAM_WD, ADAM_STEP = 0.001, 0.9, 0.999, 1e-08, 0.01, 10

C_QA, C_CQ, C_CKV, C_QM, C_KA, C_VA, C_KR, IN_PAD = 0, 1024, 1536, 2048, 2560, 2688, 2816, 2944


def _pcall(body, *, name, out_shape, in_specs, out_specs, grid=(), scratch=(), sem=None, after=None):
    params = pltpu.CompilerParams(dimension_semantics=sem, vmem_limit_bytes=VMEM_LIMIT)
    if after is not None:
        n_in, inner = len(in_specs), body

        def body(*refs):
            inner(*refs[:n_in], *refs[n_in + 1:])

        in_specs = list(in_specs) + [pl.BlockSpec(memory_space=pl.ANY)]
    call = pl.pallas_call(body, name=name, grid=grid, in_specs=in_specs, out_specs=out_specs,
                          out_shape=out_shape, scratch_shapes=list(scratch), compiler_params=params)
    return call if after is None else (lambda *ops: call(*ops, after))


def _sds(shape, dtype):
    return jax.ShapeDtypeStruct(tuple(shape), dtype)


def _dot(a, b):
    return jnp.dot(a.astype(MXU), b.astype(MXU), preferred_element_type=jnp.float32)


def _dot_nt(a, b):
    return lax.dot_general(a.astype(MXU), b.astype(MXU), (((1,), (1,)), ((), ())),
                           preferred_element_type=jnp.float32)


def _dot_tn(a, b):
    return lax.dot_general(a.astype(MXU), b.astype(MXU), (((0,), (0,)), ((), ())),
                           preferred_element_type=jnp.float32)


def _lo_mask(shape):
    return (lax.broadcasted_iota(jnp.int32, shape, len(shape) - 1) % LANES) < 64


def _norm_fwd(x, g, half=False):
    x2 = x * x
    if half:
        lo = _lo_mask(x.shape)
        s_lo = jnp.sum(jnp.where(lo, x2, 0.0), -1, keepdims=True)
        s_hi = jnp.sum(jnp.where(lo, 0.0, x2), -1, keepdims=True)
        r = jnp.where(lo, lax.rsqrt(s_lo / 64.0 + EPS), lax.rsqrt(s_hi / 64.0 + EPS))
    else:
        r = lax.rsqrt(jnp.mean(x2, -1, keepdims=True) + EPS)
    xn = x * r
    return xn * g, xn, r


def _norm_bwd(xn, r, g, dy, half=False):
    t = dy * g
    tx = t * xn
    if half:
        lo = _lo_mask(xn.shape)
        m_lo = jnp.sum(jnp.where(lo, tx, 0.0), -1, keepdims=True) / 64.0
        m_hi = jnp.sum(jnp.where(lo, 0.0, tx), -1, keepdims=True) / 64.0
        m = jnp.where(lo, m_lo, m_hi)
    else:
        m = jnp.mean(tx, -1, keepdims=True)
    dx = r * (t - xn * m)
    dg = jnp.sum(dy * xn, 0, keepdims=True)
    return dx, dg


def _swap32(x):
    lane = lax.broadcasted_iota(jnp.int32, x.shape, 1)
    return jnp.where((lane % 64) < 32, pltpu.roll(x, 96, 1), pltpu.roll(x, 32, 1))


def _rope(x, cos, sin):
    return x * cos + _swap32(x) * sin


def _rope_bwd(d, cos, sin):
    return d * cos + _swap32(d * sin)


def _my_coords():
    return lax.axis_index("x"), lax.axis_index("y"), lax.axis_index("c")


def _dev_index(px, py, pc):
    return 4 * px + 2 * py + pc


_FLIPS = [(0, 0, 1), (0, 1, 0), (0, 1, 1), (1, 0, 0), (1, 0, 1), (1, 1, 0), (1, 1, 1)]


def _flip(coords, f):
    return tuple((1 - v) if b else v for v, b in zip(coords, f))


def _all_gather(shards):
    n = len(shards)

    def body(*refs):
        ins, outs = refs[:n], refs[n:2 * n]
        send_sems, recv_sems, local_sems = refs[2 * n:]
        x, y, c = _my_coords()
        me, sibling = (x, y, c), (x, y, 1 - c)
        chips = [(1 - x, y), (x, 1 - y), (1 - x, 1 - y)]

        def copy(w, k, block, to, src=None):
            dst = outs[w].at[_dev_index(*block)]
            return pltpu.make_async_remote_copy(
                src_ref=dst if src is None else src, dst_ref=dst,
                send_sem=send_sems.at[w, k], recv_sem=recv_sems.at[w, k],
                device_id=to, device_id_type=pl.DeviceIdType.MESH)

        sends, locals_ = [], []
        for w in range(n):
            mine = pltpu.make_async_copy(ins[w], outs[w].at[_dev_index(*me)], local_sems.at[w])
            mine.start()
            locals_.append(mine)
            first = [copy(w, 0, me, sibling, src=ins[w])]
            first += [copy(w, 1 + j, me, (*chip, c), src=ins[w]) for j, chip in enumerate(chips)]
            for cp in first:
                cp.start()
            sends += first
        for w in range(n):
            for j, chip in enumerate(chips):
                copy(w, 1 + j, (*chip, c), me).wait_recv()
                fwd = copy(w, 4 + j, (*chip, c), sibling)
                fwd.start()
                sends.append(fwd)
        for w in range(n):
            copy(w, 0, sibling, me).wait_recv()
            for j, chip in enumerate(chips):
                copy(w, 4 + j, (*chip, 1 - c), me).wait_recv()
        for cp in sends:
            cp.wait_send()
        for mine in locals_:
            mine.wait()

    any_spec = pl.BlockSpec(memory_space=pl.ANY)
    return _pcall(
        body, name="all_gather_weights",
        out_shape=[_sds((N_DEV,) + s.shape, s.dtype) for s in shards],
        in_specs=[any_spec] * n, out_specs=[any_spec] * n,
        scratch=[pltpu.SemaphoreType.DMA((n, 7)), pltpu.SemaphoreType.DMA((n, 7)),
                 pltpu.SemaphoreType.DMA((n,))])(*shards)


def _wire_cost(arrays):
    nbytes = sum(a.size * a.dtype.itemsize for a in arrays)
    return pl.CostEstimate(flops=0, transcendentals=0, bytes_accessed=40 * nbytes)


def _all_gather_background(shards, collective_id, name):
    n = len(shards)
    src_refs = [jax.new_ref(s, memory_space=pltpu.MemorySpace.HBM) for s in shards]
    out_refs = [jax.empty_ref(_sds((N_DEV,) + s.shape, s.dtype), memory_space=pltpu.MemorySpace.HBM) for s in shards]

    @pl.kernel(mesh=plsc.ScalarSubcoreMesh(axis_name="seq", num_cores=1), name=name,
               scratch_types=(pltpu.SemaphoreType.DMA((n, 7)), pltpu.SemaphoreType.DMA((n, 7)),
                              pltpu.SemaphoreType.DMA((n,))),
               compiler_params=pltpu.CompilerParams(collective_id=collective_id))
    def launch(send_sems, recv_sems, local_sems):
        x, y, c = _my_coords()
        me, sibling = (x, y, c), (x, y, 1 - c)
        chips = [(1 - x, y), (x, 1 - y), (1 - x, 1 - y)]
        barrier = pltpu.get_barrier_semaphore()
        for peer in [sibling] + [(*chip, c) for chip in chips]:
            pl.semaphore_signal(barrier, inc=1, device_id=peer, device_id_type=pl.DeviceIdType.MESH)
        pl.semaphore_wait(barrier, 4)

        def copy(w, k, block, to, src=None):
            dst = out_refs[w].at[_dev_index(*block)]
            return pltpu.make_async_remote_copy(
                src_ref=dst if src is None else src, dst_ref=dst,
                send_sem=send_sems.at[w, k], recv_sem=recv_sems.at[w, k],
                device_id=to, device_id_type=pl.DeviceIdType.MESH)

        sends, locals_ = [], []
        for w in range(n):
            mine = pltpu.make_async_copy(src_refs[w], out_refs[w].at[_dev_index(*me)], local_sems.at[w])
            mine.start()
            locals_.append(mine)
            first = [copy(w, 0, me, sibling, src=src_refs[w])]
            first += [copy(w, 1 + j, me, (*chip, c), src=src_refs[w]) for j, chip in enumerate(chips)]
            for cp in first:
                cp.start()
            sends += first
        for w in range(n):
            for j, chip in enumerate(chips):
                copy(w, 1 + j, (*chip, c), me).wait_recv()
                fwd = copy(w, 4 + j, (*chip, c), sibling)
                fwd.start()
                sends.append(fwd)
        for w in range(n):
            copy(w, 0, sibling, me).wait_recv()
            for j, chip in enumerate(chips):
                copy(w, 4 + j, (*chip, 1 - c), me).wait_recv()
        for cp in sends:
            cp.wait_send()
        for mine in locals_:
            mine.wait()

    launch()
    return [r[...] for r in out_refs]


def _exchange_grads(grads):
    n = len(grads)

    def body(*refs):
        ins, outs = refs[:n], refs[n:2 * n]
        send_sems, recv_sems, local_sems = refs[2 * n:]
        me = _my_coords()
        my_idx = _dev_index(*me)
        sends, locals_ = [], []
        for w in range(n):
            mine = pltpu.make_async_copy(ins[w].at[my_idx], outs[w].at[my_idx], local_sems.at[w])
            mine.start()
            locals_.append(mine)
            for k, f in enumerate(_FLIPS):
                peer = _flip(me, f)
                cp = pltpu.make_async_remote_copy(
                    src_ref=ins[w].at[_dev_index(*peer)], dst_ref=outs[w].at[my_idx],
                    send_sem=send_sems.at[w, k], recv_sem=recv_sems.at[w, k],
                    device_id=peer, device_id_type=pl.DeviceIdType.MESH)
                cp.start()
                sends.append(cp)
        for w in range(n):
            for k, f in enumerate(_FLIPS):
                peer = _flip(me, f)
                slot = outs[w].at[_dev_index(*peer)]
                pltpu.make_async_remote_copy(
                    src_ref=slot, dst_ref=slot,
                    send_sem=send_sems.at[w, k], recv_sem=recv_sems.at[w, k],
                    device_id=peer, device_id_type=pl.DeviceIdType.MESH).wait_recv()
        for cp in sends:
            cp.wait_send()
        for mine in locals_:
            mine.wait()

    any_spec = pl.BlockSpec(memory_space=pl.ANY)
    return _pcall(
        body, name="exchange_grads",
        out_shape=[_sds(g.shape, g.dtype) for g in grads],
        in_specs=[any_spec] * n, out_specs=[any_spec] * n,
        scratch=[pltpu.SemaphoreType.DMA((n, 7)), pltpu.SemaphoreType.DMA((n, 7)),
                 pltpu.SemaphoreType.DMA((n,))])(*grads)


def _exchange_grads_background(grads, collective_id, name):
    n = len(grads)
    src_refs = [jax.new_ref(g, memory_space=pltpu.MemorySpace.HBM) for g in grads]
    out_refs = [jax.empty_ref(_sds(g.shape, g.dtype), memory_space=pltpu.MemorySpace.HBM) for g in grads]

    @pl.kernel(mesh=plsc.ScalarSubcoreMesh(axis_name="seq", num_cores=1), name=name,
               scratch_types=(pltpu.SemaphoreType.DMA((n, 7)), pltpu.SemaphoreType.DMA((n, 7)),
                              pltpu.SemaphoreType.DMA((n,))),
               cost_estimate=_wire_cost(grads),
               compiler_params=pltpu.CompilerParams(collective_id=collective_id))
    def launch(send_sems, recv_sems, local_sems):
        me = _my_coords()
        my_idx = _dev_index(*me)
        peers = [_flip(me, f) for f in _FLIPS]
        barrier = pltpu.get_barrier_semaphore()
        for peer in peers:
            pl.semaphore_signal(barrier, inc=1, device_id=peer, device_id_type=pl.DeviceIdType.MESH)
        pl.semaphore_wait(barrier, len(peers))
        sends, locals_ = [], []
        for w in range(n):
            mine = pltpu.make_async_copy(src_refs[w].at[my_idx], out_refs[w].at[my_idx], local_sems.at[w])
            mine.start()
            locals_.append(mine)
            for k, peer in enumerate(peers):
                cp = pltpu.make_async_remote_copy(
                    src_ref=src_refs[w].at[_dev_index(*peer)], dst_ref=out_refs[w].at[my_idx],
                    send_sem=send_sems.at[w, k], recv_sem=recv_sems.at[w, k],
                    device_id=peer, device_id_type=pl.DeviceIdType.MESH)
                cp.start()
                sends.append(cp)
        for w in range(n):
            for k, peer in enumerate(peers):
                slot = out_refs[w].at[_dev_index(*peer)]
                pltpu.make_async_remote_copy(
                    src_ref=slot, dst_ref=slot, send_sem=send_sems.at[w, k], recv_sem=recv_sems.at[w, k],
                    device_id=peer, device_id_type=pl.DeviceIdType.MESH).wait_recv()
        for cp in sends:
            cp.wait_send()
        for mine in locals_:
            mine.wait()

    launch()
    return [r[...] for r in out_refs]


def _to_wire(parts, after, name):
    n = len(parts)
    rows, cols = parts[0].shape
    tr = rows // 2 if rows % 32 == 0 else rows

    def body(*refs):
        for k in range(n):
            refs[n][k] = refs[k][...].astype(WIRE)

    blk = pl.BlockSpec((tr, cols), lambda i: (i, 0))
    return _pcall(
        body, name=name, grid=(rows // tr,), out_shape=_sds((n, rows, cols), WIRE),
        in_specs=[blk] * n, out_specs=pl.BlockSpec((n, tr, cols), lambda i: (0, i, 0)),
        sem=("parallel",), after=after)(*parts)


def _adam_math(w, g, m, v):
    m = ADAM_B1 * m + (1.0 - ADAM_B1) * g
    v = ADAM_B2 * v + (1.0 - ADAM_B2) * (g * g)
    m_hat = m / (1.0 - ADAM_B1 ** ADAM_STEP)
    v_hat = v / (1.0 - ADAM_B2 ** ADAM_STEP)
    delta = -ADAM_LR * (m_hat / (jnp.sqrt(v_hat) + ADAM_EPS) + ADAM_WD * w)
    return delta, m, v


def _small_layout(sizes):
    row0, r = [], 0
    for n in sizes:
        row0.append(r)
        r += -(-n // LANES)
    return row0, r, -(-(r + 1) // 8) * 8


def _small_pieces(n):
    return [(k, min(LANES, n - LANES * k)) for k in range(-(-n // LANES))]


def _small_fill(pack, slot, srcs, sizes, row0, rows):
    pack[slot] = jnp.zeros((rows, LANES), jnp.float32)
    for p, n in enumerate(sizes):
        val = srcs[p][...]
        if val.shape[-1] == LANES and n == 64:
            pack[slot, row0[p]:row0[p] + 1, :] = val + pltpu.roll(val, 64, 1)
            continue
        for k, width in _small_pieces(n):
            pack[slot, row0[p] + k:row0[p] + k + 1, 0:width] = srcs[p][:, LANES * k:LANES * k + width]


def _small_allreduce(grads, loss_tile, sizes):
    n_par = len(sizes)
    row0, loss_row, rows = _small_layout(sizes)

    def body(*refs):
        g_refs, loss_in, out_ref = refs[:n_par], refs[n_par], refs[n_par + 1]
        pack, gath, send_sems, recv_sems = refs[n_par + 2:]
        me = _my_coords()
        my_idx = _dev_index(*me)
        _small_fill(pack, 0, g_refs, sizes, row0, rows)
        pack[0, loss_row:loss_row + 1, :] = loss_in[0:1, :]
        gath[my_idx] = pack[0]
        sends = []
        for k, f in enumerate(_FLIPS):
            peer = _flip(me, f)
            cp = pltpu.make_async_remote_copy(
                src_ref=pack.at[0], dst_ref=gath.at[my_idx],
                send_sem=send_sems.at[k], recv_sem=recv_sems.at[k],
                device_id=peer, device_id_type=pl.DeviceIdType.MESH)
            cp.start()
            sends.append(cp)
        for k, f in enumerate(_FLIPS):
            peer = _flip(me, f)
            slot = gath.at[_dev_index(*peer)]
            pltpu.make_async_remote_copy(
                src_ref=slot, dst_ref=slot, send_sem=send_sems.at[k], recv_sem=recv_sems.at[k],
                device_id=peer, device_id_type=pl.DeviceIdType.MESH).wait_recv()
        for cp in sends:
            cp.wait_send()
        g = gath[0]
        for d in range(1, N_DEV):
            g = g + gath[d]
        out_ref[...] = g

    vm = pl.BlockSpec(memory_space=pltpu.VMEM)
    return _pcall(
        body, name="small_allreduce", out_shape=_sds((rows, LANES), jnp.float32),
        in_specs=[vm] * (n_par + 1), out_specs=vm,
        scratch=[pltpu.VMEM((1, rows, LANES), jnp.float32), pltpu.VMEM((N_DEV, rows, LANES), jnp.float32),
                 pltpu.SemaphoreType.DMA((7,)), pltpu.SemaphoreType.DMA((7,))])(*grads, loss_tile)


def _small_adam(packed_g, ws, ms, vs):
    sizes = [w.shape[-1] for w in ws]
    n_par = len(ws)
    row0, loss_row, rows = _small_layout(sizes)

    def body(*refs):
        g_ref = refs[0]
        w_refs, m_refs, v_refs = (refs[1 + k * n_par: 1 + (k + 1) * n_par] for k in range(3))
        loss_out = refs[3 * n_par + 1]
        out_refs = refs[3 * n_par + 2: 7 * n_par + 2]
        pack, res = refs[7 * n_par + 2:]
        for slot, srcs in enumerate((w_refs, m_refs, v_refs)):
            _small_fill(pack, slot, srcs, sizes, row0, rows)
        g = g_ref[...]
        delta, m, v = _adam_math(pack[0], g, pack[1], pack[2])
        res[0], res[1], res[2], res[3] = g, delta, m, v
        loss_out[...] = res[0, loss_row:loss_row + 1, 0:1]
        for p, n in enumerate(sizes):
            for kind in range(4):
                for k, width in _small_pieces(n):
                    out_refs[4 * p + kind][:, LANES * k:LANES * k + width] = (
                        res[kind, row0[p] + k:row0[p] + k + 1, 0:width])

    vm = pl.BlockSpec(memory_space=pltpu.VMEM)
    out_shape = [_sds((1, 1), jnp.float32)]
    for n in sizes:
        out_shape += [_sds((1, n), jnp.float32)] * 4
    outs = _pcall(
        body, name="small_adam", out_shape=out_shape,
        in_specs=[vm] * (3 * n_par + 1), out_specs=[vm] * len(out_shape),
        scratch=[pltpu.VMEM((3, rows, LANES), jnp.float32), pltpu.VMEM((4, rows, LANES), jnp.float32)])(
            packed_g, *ws, *ms, *vs)
    return outs[0], [outs[1 + 4 * p: 5 + 4 * p] for p in range(n_par)]


def _adam_big(recv, w, m, v, name, after=None, which=None):
    rows, cols = recv.shape[-2:]
    row_tiles = [t for t in range(16, rows + 1, 16) if rows % t == 0 and t * cols <= 400 * 1024]
    tr, tc = (max(row_tiles), cols) if row_tiles else (rows, 512 if cols % 512 == 0 else cols)

    def body(r_ref, w_ref, m_ref, v_ref, g_ref, d_ref, mo_ref, vo_ref):
        g = r_ref[0].astype(jnp.float32)
        for d in range(1, N_DEV):
            g = g + r_ref[d].astype(jnp.float32)
        delta, mn, vn = _adam_math(w_ref[...], g, m_ref[...], v_ref[...])
        g_ref[...] = g
        d_ref[...] = delta
        mo_ref[...] = mn
        vo_ref[...] = vn

    blk = pl.BlockSpec((tr, tc), lambda i, j: (i, j))
    if which is None:
        r_spec = pl.BlockSpec((N_DEV, tr, tc), lambda i, j: (0, i, j))
    else:
        r_spec = pl.BlockSpec((N_DEV, None, tr, tc), lambda i, j: (0, which, i, j))
    return _pcall(
        body, name=name, grid=(rows // tr, cols // tc),
        out_shape=[_sds((rows, cols), jnp.float32)] * 4,
        in_specs=[r_spec, blk, blk, blk],
        out_specs=[blk] * 4, sem=("parallel", "parallel"), after=after)(recv, w, m, v)


def _mm(a, b, *, ta=False, tb=False, out_dtype, tm, tk, name):
    (kdim, mdim) = a.shape if ta else a.shape[::-1]
    ndim = b.shape[0] if tb else b.shape[1]
    tm, tk = min(tm, mdim), min(tk, kdim)
    nk = kdim // tk

    def body(a_ref, b_ref, o_ref, acc):
        k = pl.program_id(1)
        if ta:
            part = _dot_tn(a_ref[...], b_ref[...])
        elif tb:
            part = _dot_nt(a_ref[...], b_ref[...])
        else:
            part = _dot(a_ref[...], b_ref[...])

        @pl.when(k == 0)
        def _():
            acc[...] = part

        @pl.when(k > 0)
        def _():
            acc[...] += part

        @pl.when(k == nk - 1)
        def _():
            o_ref[...] = acc[...].astype(o_ref.dtype)

    a_spec = pl.BlockSpec((tk, tm), lambda i, k: (k, i)) if ta else pl.BlockSpec((tm, tk), lambda i, k: (i, k))
    b_spec = pl.BlockSpec((ndim, tk), lambda i, k: (0, k)) if tb else pl.BlockSpec((tk, ndim), lambda i, k: (k, 0))
    return _pcall(
        body, name=name, grid=(mdim // tm, nk), out_shape=_sds((mdim, ndim), out_dtype),
        in_specs=[a_spec, b_spec], out_specs=pl.BlockSpec((tm, ndim), lambda i, k: (i, 0)),
        scratch=[pltpu.VMEM((tm, ndim), jnp.float32)], sem=("parallel", "arbitrary"))(a, b)


def _ref_col_pieces(start, stop):
    ref_starts = [0, 1024, 1152, 1280, 1792, 2304, 2368, 2880]
    perm_starts = [C_QA, C_KA, C_VA, C_CQ, C_CKV, C_KR, C_QM]
    out = []
    for p in range(7):
        lo, hi = max(start, ref_starts[p]), min(stop, ref_starts[p + 1])
        if lo < hi:
            out.append((lo - start, perm_starts[p] + lo - ref_starts[p], hi - lo))
    return out


def _dw_in(hn, d_proj, n_shard):
    s, d = hn.shape
    n = d_proj.shape[1]
    tm, tk = min(512, d), min(1024, s)
    nk = s // tk

    def body(a_ref, b_ref, o_ref, acc):
        k = pl.program_id(1)
        part = _dot_tn(a_ref[...], b_ref[...])

        @pl.when(k == 0)
        def _():
            acc[...] = part

        @pl.when(k > 0)
        def _():
            acc[...] += part

        @pl.when(k == nk - 1)
        def _():
            t = acc[...].T
            for j in range(N_DEV):
                rows = [t[src:src + width] for _, src, width in _ref_col_pieces(j * n_shard, (j + 1) * n_shard)]
                o_ref[j] = jnp.concatenate(rows, axis=0).astype(o_ref.dtype)

    return _pcall(
        body, name="dw_in", grid=(d // tm, nk), out_shape=_sds((N_DEV, n_shard, d), WIRE),
        in_specs=[pl.BlockSpec((tk, tm), lambda i, k: (k, i)), pl.BlockSpec((tk, n), lambda i, k: (k, 0))],
        out_specs=pl.BlockSpec((N_DEV, n_shard, tm), lambda i, k: (0, 0, i)),
        scratch=[pltpu.VMEM((tm, n), jnp.float32)], sem=("parallel", "arbitrary"))(hn, d_proj)


def _in_proj(x, g, w):
    s, d = x.shape
    n = w.shape[0]
    tm = min(2 * ROW_TILE, s)

    def body(x_ref, g_ref, w_ref, p_ref, hn_ref):
        hn, _, _ = _norm_fwd(x_ref[...], g_ref[...])
        hn_ref[...] = hn.astype(hn_ref.dtype)
        p_ref[...] = _dot_nt(hn, w_ref[...])

    return _pcall(
        body, name="in_proj", grid=(s // tm,),
        out_shape=[_sds((s, n), jnp.float32), _sds((s, d), MXU)],
        in_specs=[pl.BlockSpec((tm, d), lambda i: (i, 0)), pl.BlockSpec((1, d), lambda i: (0, 0)),
                  pl.BlockSpec((n, d), lambda i: (0, 0), pipeline_mode=pl.Buffered(1))],
        out_specs=[pl.BlockSpec((tm, n), lambda i: (i, 0)), pl.BlockSpec((tm, d), lambda i: (i, 0))],
        sem=("parallel",))(x, g, w)


def _mla_prep(proj, cos, sin, g_cq, g_ckv, w_uq, w_ukv, g_qn, g_qr, g_kn, g_kr):
    s = proj.shape[0]
    tm = min(ROW_TILE, s)
    nh = MLA_HEADS

    def body(cq_ref, ckv_ref, kr_ref, cos_ref, sin_ref, gcq_ref, gckv_ref, wuq_ref, wukv_ref,
             gqn_ref, gqr_ref, gkn_ref, gkr_ref,
             qc_ref, kc_ref, v_ref, qb_ref, kvb_ref, cqn_ref, ckvn_ref):
        cos_t, sin_t = cos_ref[...], sin_ref[...]
        lo = _lo_mask((tm, LANES))
        cqn, _, _ = _norm_fwd(cq_ref[...], gcq_ref[...])
        cqn_ref[...] = cqn.astype(cqn_ref.dtype)
        qb = _dot_nt(cqn, wuq_ref[...])
        qb_ref[...] = qb
        ckvn, _, _ = _norm_fwd(ckv_ref[...], gckv_ref[...])
        ckvn_ref[...] = ckvn.astype(ckvn_ref.dtype)
        kvb = jnp.concatenate([_dot(ckvn, wukv_ref[dev]) for dev in range(N_DEV)], axis=1)
        kvb_ref[...] = kvb
        kr, _, _ = _norm_fwd(kr_ref[...], gkr_ref[...], half=True)
        kr = _rope(kr, cos_t, sin_t)
        kr2 = jnp.where(lo, kr, pltpu.roll(kr, 64, 1))
        ropes = []
        for j in range(nh // 2):
            xr = qb[:, nh * MLA_NOPE + LANES * j: nh * MLA_NOPE + LANES * (j + 1)]
            qr, _, _ = _norm_fwd(xr, gqr_ref[...], half=True)
            ropes.append(_rope(qr, cos_t, sin_t))
        for h in range(nh):
            qn, _, _ = _norm_fwd(qb[:, MLA_NOPE * h: MLA_NOPE * (h + 1)], gqn_ref[...])
            mask = lo if h % 2 == 0 else jnp.logical_not(lo)
            qr = jnp.where(mask, ropes[h // 2], 0.0)
            qc_ref[h] = jnp.concatenate([qn, qr], axis=1).astype(qc_ref.dtype)
            kn, _, _ = _norm_fwd(kvb[:, 256 * h: 256 * h + MLA_NOPE], gkn_ref[...])
            kc_ref[h] = jnp.concatenate([kn, kr2], axis=1).astype(kc_ref.dtype)
            v_ref[h] = kvb[:, 256 * h + MLA_NOPE: 256 * (h + 1)].astype(v_ref.dtype)

    def col(width, start):
        return pl.BlockSpec((tm, width), lambda i: (i, start // width))

    def full(shape):
        return pl.BlockSpec(shape, lambda i: (0,) * len(shape))

    def row(width):
        return pl.BlockSpec((tm, width), lambda i: (i, 0))

    def heads(width):
        return pl.BlockSpec((nh, tm, width), lambda i: (0, i, 0))

    return _pcall(
        body, name="mla_prep", grid=(s // tm,),
        out_shape=[_sds((nh, s, 256), MXU), _sds((nh, s, 256), MXU), _sds((nh, s, MLA_V), MXU),
                   _sds((s, 768), jnp.float32), _sds((s, 1024), jnp.float32),
                   _sds((s, 512), MXU), _sds((s, 512), MXU)],
        in_specs=[col(512, C_CQ), col(512, C_CKV), col(LANES, C_KR), row(LANES), row(LANES),
                  full((1, 512)), full((1, 512)), full((768, 512)), full((N_DEV, 512, LANES)),
                  full((1, LANES)), full((1, LANES)), full((1, LANES)), full((1, LANES))],
        out_specs=[heads(256), heads(256), heads(MLA_V), row(768), row(1024), row(512), row(512)],
        sem=("parallel",))(proj, proj, proj, cos, sin, g_cq, g_ckv, w_uq, w_ukv, g_qn, g_qr, g_kn, g_kr)


def _mla_fwd(qc, kc, v):
    nh, s, _ = qc.shape
    t = min(ATT_TILE, s)
    nb = s // t
    scale = (MLA_NOPE + MLA_ROPE) ** -0.5

    def body(q_ref, k_ref, v_ref, y_ref, lse_ref, m_sc, l_sc, acc):
        qi, ki = pl.program_id(1), pl.program_id(2)

        @pl.when(ki == 0)
        def _():
            m_sc[...] = jnp.full_like(m_sc, NEG_INF)
            l_sc[...] = jnp.zeros_like(l_sc)
            acc[...] = jnp.zeros_like(acc)

        def step(diagonal):
            rc = t // 4 if diagonal else t
            for c in range(t // rc):
                rows = slice(rc * c, rc * (c + 1))
                keys = slice(0, rc * (c + 1))
                sc = _dot_nt(q_ref[0, rows, :], k_ref[0, keys, :]) * (scale * LOG2E)
                if diagonal:
                    r_i = lax.broadcasted_iota(jnp.int32, sc.shape, 0) + rc * c
                    c_i = lax.broadcasted_iota(jnp.int32, sc.shape, 1)
                    sc = jnp.where(c_i <= r_i, sc, NEG_INF)
                m_old = m_sc[rows, :]
                m_new = jnp.maximum(m_old, jnp.max(sc, -1, keepdims=True))
                alpha = jnp.exp2(m_old - m_new)
                p = jnp.exp2(sc - m_new)
                l_sc[rows, :] = alpha * l_sc[rows, :] + jnp.sum(p, -1, keepdims=True)
                acc[rows, :] = alpha * acc[rows, :] + _dot(p, v_ref[0, keys, :])
                m_sc[rows, :] = m_new

        @pl.when(ki < qi)
        def _():
            step(False)

        @pl.when(ki == qi)
        def _():
            step(True)

        @pl.when(ki == qi)
        def _():
            y_ref[...] = acc[...] / l_sc[...]
            lse_ref[0] = m_sc[...] + jnp.log2(l_sc[...])

    return _pcall(
        body, name="mla_fwd", grid=(nh, nb, nb),
        out_shape=[_sds((s, nh * MLA_V), jnp.float32), _sds((nh, s, 1), jnp.float32)],
        in_specs=[pl.BlockSpec((1, t, 256), lambda h, i, k: (h, i, 0)),
                  pl.BlockSpec((1, t, 256), lambda h, i, k: (h, jnp.minimum(k, i), 0)),
                  pl.BlockSpec((1, t, MLA_V), lambda h, i, k: (h, jnp.minimum(k, i), 0))],
        out_specs=[pl.BlockSpec((t, MLA_V), lambda h, i, k: (i, h)),
                   pl.BlockSpec((1, t, 1), lambda h, i, k: (h, i, 0))],
        scratch=[pltpu.VMEM((t, 1), jnp.float32), pltpu.VMEM((t, 1), jnp.float32),
                 pltpu.VMEM((t, MLA_V), jnp.float32)],
        sem=("parallel", "parallel", "arbitrary"))(qc, kc, v)


def _memkv_prep(mem, g_mem, w_mkv, g_mk):
    ml, d = mem.shape
    hw = MEM_HEADS * MEM_DIM

    def body(mem_ref, g_ref, w_ref, gk_ref, k_ref, v_ref, kv_ref, mn_ref):
        mn, _, _ = _norm_fwd(mem_ref[...], g_ref[...])
        mn_ref[...] = mn.astype(mn_ref.dtype)
        kv = _dot(mn, w_ref[...])
        kv_ref[...] = kv
        for h in range(MEM_HEADS):
            kn, _, _ = _norm_fwd(kv[:, MEM_DIM * h: MEM_DIM * (h + 1)], gk_ref[...])
            k_ref[:, MEM_DIM * h: MEM_DIM * (h + 1)] = kn.astype(k_ref.dtype)
        v_ref[...] = kv[:, hw:].astype(v_ref.dtype)

    vm = pl.BlockSpec(memory_space=pltpu.VMEM)
    return _pcall(
        body, name="memkv_prep",
        out_shape=[_sds((ml, hw), MXU), _sds((ml, hw), MXU), _sds((ml, 2 * hw), jnp.float32), _sds((ml, d), MXU)],
        in_specs=[vm] * 4, out_specs=[vm] * 4)(mem, g_mem, w_mkv, g_mk)


def _mem_fwd(proj, g_mq, km, vmm):
    s = proj.shape[0]
    ml, hw = km.shape
    tm = min(FFN_TILE, s)
    scale = MEM_DIM ** -0.5

    def body(q_ref, g_ref, k_ref, v_ref, y_ref, lse_ref):
        col = lax.broadcasted_iota(jnp.int32, (tm, MEM_HEADS), 1)
        lse_t = jnp.zeros((tm, MEM_HEADS), jnp.float32)
        for h in range(MEM_HEADS):
            sl = slice(MEM_DIM * h, MEM_DIM * (h + 1))
            qn, _, _ = _norm_fwd(q_ref[:, sl], g_ref[...])
            sc = _dot_nt(qn, k_ref[:, sl]) * scale
            m = jnp.max(sc, -1, keepdims=True)
            p = jnp.exp(sc - m)
            l = jnp.sum(p, -1, keepdims=True)
            y_ref[:, sl] = _dot(p, v_ref[:, sl]) / l
            lse_t = jnp.where(col == h, m + jnp.log(l), lse_t)
        lse_ref[...] = lse_t

    return _pcall(
        body, name="mem_fwd", grid=(s // tm,),
        out_shape=[_sds((s, hw), jnp.float32), _sds((s, MEM_HEADS), jnp.float32)],
        in_specs=[pl.BlockSpec((tm, hw), lambda i: (i, C_QM // hw)), pl.BlockSpec((1, MEM_DIM), lambda i: (0, 0)),
                  pl.BlockSpec((ml, hw), lambda i: (0, 0)), pl.BlockSpec((ml, hw), lambda i: (0, 0))],
        out_specs=[pl.BlockSpec((tm, hw), lambda i: (i, 0)), pl.BlockSpec((tm, MEM_HEADS), lambda i: (i, 0))],
        sem=("parallel",))(proj, g_mq, km, vmm)


def _alibi_slope(h):
    return float(2.0 ** (-8.0 * (h + 1) / SWA_Q_HEADS))


def _swa_common(n, kp, kc, vp, vc, pq, pkp, pkc, gk):
    b = SWA_BLOCK
    k_raw = jnp.concatenate([kp, kc], axis=0)
    kn, kxn, kr = _norm_fwd(k_raw, gk, half=True)
    v = jnp.concatenate([vp, vc], axis=0)
    dist = jnp.abs(pq - jnp.concatenate([pkp, pkc], axis=1))
    r_i = lax.broadcasted_iota(jnp.int32, (b, 2 * b), 0)
    c_i = lax.broadcasted_iota(jnp.int32, (b, 2 * b), 1)
    valid = (c_i > r_i) & (c_i <= r_i + b) & (c_i >= jnp.where(n > 0, 0, b))
    bias = jnp.where(valid, -dist, NEG_INF)
    return kn, v, bias


def _swa_specs(s):
    b = SWA_BLOCK
    prev = lambda n: jnp.maximum(n - 1, 0)
    return [
        pl.BlockSpec((b, 1024), lambda n: (n, C_QA // 1024)),
        pl.BlockSpec((b, LANES), lambda n: (prev(n), C_KA // LANES)),
        pl.BlockSpec((b, LANES), lambda n: (n, C_KA // LANES)),
        pl.BlockSpec((b, LANES), lambda n: (prev(n), C_VA // LANES)),
        pl.BlockSpec((b, LANES), lambda n: (n, C_VA // LANES)),
        pl.BlockSpec((b, 1), lambda n: (n, 0)),
        pl.BlockSpec((1, b), lambda n: (0, prev(n))),
        pl.BlockSpec((1, b), lambda n: (0, n)),
        pl.BlockSpec((1, LANES), lambda n: (0, 0)),
        pl.BlockSpec((1, LANES), lambda n: (0, 0)),
        pl.BlockSpec(memory_space=pltpu.SMEM),
    ]


def _swa_fwd(proj, posc, posr, gq, gk, sinks):
    s = proj.shape[0]
    b = SWA_BLOCK
    scale = SWA_DIM ** -0.5

    def body(q_ref, kp_ref, kc_ref, vp_ref, vc_ref, pq_ref, pkp_ref, pkc_ref, gq_ref, gk_ref, sink_ref,
             y_ref, lse_ref):
        n = pl.program_id(0)
        kn, v, bias = _swa_common(n, kp_ref[...], kc_ref[...], vp_ref[...], vc_ref[...],
                                  pq_ref[...], pkp_ref[...], pkc_ref[...], gk_ref[...])
        lo = _lo_mask((b, LANES))
        col = lax.broadcasted_iota(jnp.int32, (b, SWA_Q_HEADS), 1)
        lse_t = jnp.zeros((b, SWA_Q_HEADS), jnp.float32)
        hpg = SWA_Q_HEADS // SWA_KV_HEADS
        for g in range(SWA_KV_HEADS):
            heads = range(hpg * g, hpg * (g + 1))
            kvmask = lo if g == 0 else jnp.logical_not(lo)
            qs = []
            for j in range(hpg // 2 * g, hpg // 2 * (g + 1)):
                qn, _, _ = _norm_fwd(q_ref[:, LANES * j: LANES * (j + 1)], gq_ref[...], half=True)
                qn = qn * scale
                qsw = pltpu.roll(qn, 64, 1)
                qs += [jnp.where(kvmask, qn if e == g else qsw, 0.0) for e in range(2)]
            sc_st = _dot_nt(jnp.concatenate(qs, axis=0), kn)
            ps, ls = [], []
            for i, h in enumerate(heads):
                sc = sc_st[b * i: b * (i + 1)] + _alibi_slope(h) * bias
                sk = sink_ref[h]
                m = jnp.maximum(jnp.max(sc, -1, keepdims=True), sk)
                p = jnp.exp(sc - m)
                l = jnp.sum(p, -1, keepdims=True) + jnp.exp(sk - m)
                ps.append(p.astype(MXU))
                ls.append(l)
                lse_t = jnp.where(col == h, m + jnp.log(l), lse_t)
            o_st = _dot(jnp.concatenate(ps, axis=0), v)
            for j in range(hpg // 2 * g, hpg // 2 * (g + 1)):
                halves = []
                for e in range(2):
                    i = 2 * j + e - hpg * g
                    o_h = o_st[b * i: b * (i + 1)] / ls[i]
                    halves.append(o_h if e == g else pltpu.roll(o_h, 64, 1))
                y_ref[:, LANES * j: LANES * (j + 1)] = jnp.where(lo, halves[0], halves[1])
        lse_ref[...] = lse_t

    return _pcall(
        body, name="swa_fwd", grid=(s // b,),
        out_shape=[_sds((s, 1024), jnp.float32), _sds((s, SWA_Q_HEADS), jnp.float32)],
        in_specs=_swa_specs(s),
        out_specs=[pl.BlockSpec((b, 1024), lambda n: (n, 0)), pl.BlockSpec((b, SWA_Q_HEADS), lambda n: (n, 0))],
        sem=("parallel",))(proj, proj, proj, proj, proj, posc, posr, posr, gq, gk, sinks)


def _out_proj(y_a, y_b, y_m, x, w_out, g_ffn):
    s, d = x.shape
    tm = min(2 * ROW_TILE, s)

    def body(ya_ref, yb_ref, ym_ref, x_ref, w_ref, g_ref, h1_ref, fn_ref):
        y = jnp.concatenate([ya_ref[...].astype(MXU), yb_ref[...].astype(MXU), ym_ref[...].astype(MXU)], axis=1)
        h1 = x_ref[...] + _dot(y, w_ref[...])
        h1_ref[...] = h1
        fn, _, _ = _norm_fwd(h1, g_ref[...])
        fn_ref[...] = fn.astype(fn_ref.dtype)

    def row(width):
        return pl.BlockSpec((tm, width), lambda i: (i, 0))

    return _pcall(
        body, name="out_proj", grid=(s // tm,),
        out_shape=[_sds((s, d), jnp.float32), _sds((s, d), MXU)],
        in_specs=[row(1024), row(512), row(512), row(d),
                  pl.BlockSpec(w_out.shape, lambda i: (0, 0), pipeline_mode=pl.Buffered(1)),
                  pl.BlockSpec((1, d), lambda i: (0, 0))],
        out_specs=[row(d), row(d)], sem=("parallel",))(y_a, y_b, y_m, x, w_out, g_ffn)


def _ffn_gu(fn, w_gu):
    s, d = fn.shape
    f = w_gu.shape[2]
    tm = min(2 * FFN_TILE, s)

    def body(fn_ref, w_ref, gu_ref, act_ref):
        x = fn_ref[...]
        g = _dot_nt(x, w_ref[0, 0])
        u = _dot_nt(x, w_ref[0, 1])
        gu_ref[0, 0] = g
        gu_ref[0, 1] = u
        act_ref[0] = (g * jax.nn.sigmoid(g) * u).astype(act_ref.dtype)

    return _pcall(
        body, name="ffn_gate_up", grid=(N_DEV, s // tm),
        out_shape=[_sds((N_DEV, 2, s, f), jnp.float32), _sds((N_DEV, s, f), MXU)],
        in_specs=[pl.BlockSpec((tm, d), lambda j, i: (i, 0)),
                  pl.BlockSpec((1, 2, f, d), lambda j, i: (j, 0, 0, 0))],
        out_specs=[pl.BlockSpec((1, 2, tm, f), lambda j, i: (j, 0, i, 0)),
                   pl.BlockSpec((1, tm, f), lambda j, i: (j, i, 0))],
        sem=("parallel", "parallel"))(fn, w_gu)


def _ffn_down(act, w_d, h1, target):
    _, s, f = act.shape
    d = h1.shape[1]
    tm = min(FFN_TILE, s)

    def body(a_ref, w_ref, h1_ref, t_ref, dout_ref, doutb_ref, loss_ref, acc):
        i, j = pl.program_id(0), pl.program_id(1)
        part = _dot(a_ref[0], w_ref[0]) + _dot(a_ref[1], w_ref[1])

        @pl.when(j == 0)
        def _():
            acc[...] = h1_ref[...] + part

        @pl.when(j > 0)
        def _():
            acc[...] += part

        @pl.when((i == 0) & (j == 0))
        def _():
            loss_ref[...] = jnp.zeros_like(loss_ref)

        @pl.when(j == N_DEV // 2 - 1)
        def _():
            diff = acc[...] - t_ref[...]
            dout_ref[...] = diff / d
            doutb_ref[...] = (diff / d).astype(doutb_ref.dtype)
            loss_ref[...] += 0.5 * jnp.sum(jnp.sum(diff * diff, -1, keepdims=True) / d)

    row = pl.BlockSpec((tm, d), lambda i, j: (i, 0))
    return _pcall(
        body, name="ffn_down", grid=(s // tm, N_DEV // 2),
        out_shape=[_sds((s, d), jnp.float32), _sds((s, d), MXU), _sds((8, LANES), jnp.float32)],
        in_specs=[pl.BlockSpec((2, tm, f), lambda i, j: (j, i, 0)), pl.BlockSpec((2, f, d), lambda i, j: (j, 0, 0)),
                  row, row],
        out_specs=[row, row, pl.BlockSpec((8, LANES), lambda i, j: (0, 0))],
        scratch=[pltpu.VMEM((tm, d), jnp.float32)], sem=("arbitrary", "arbitrary"))(act, w_d, h1, target)


def _ffn_bwd_act(dout, w_d, gu):
    s, d = dout.shape
    f = w_d.shape[1]
    tm = min(2 * FFN_TILE, s)
    ni = s // tm

    def body(do_ref, w_ref, gu_ref, dgu_ref, dw_ref, acc):
        i = pl.program_id(1)
        do = do_ref[...]
        d_act = _dot_nt(do, w_ref[0])
        g, u = gu_ref[0, 0], gu_ref[0, 1]
        sig = jax.nn.sigmoid(g)
        silu = g * sig
        dgu_ref[0, 0] = (d_act * u * (sig * (1.0 + g * (1.0 - sig)))).astype(dgu_ref.dtype)
        dgu_ref[0, 1] = (d_act * silu).astype(dgu_ref.dtype)
        part = _dot_tn(silu * u, do)

        @pl.when(i == 0)
        def _():
            acc[...] = part

        @pl.when(i > 0)
        def _():
            acc[...] += part

        @pl.when(i == ni - 1)
        def _():
            dw_ref[0] = acc[...].astype(dw_ref.dtype)

    return _pcall(
        body, name="ffn_bwd_act", grid=(N_DEV, ni),
        out_shape=[_sds((N_DEV, 2, s, f), MXU), _sds((N_DEV, f, d), WIRE)],
        in_specs=[pl.BlockSpec((tm, d), lambda j, i: (i, 0)), pl.BlockSpec((1, f, d), lambda j, i: (j, 0, 0)),
                  pl.BlockSpec((1, 2, tm, f), lambda j, i: (j, 0, i, 0))],
        out_specs=[pl.BlockSpec((1, 2, tm, f), lambda j, i: (j, 0, i, 0)),
                   pl.BlockSpec((1, f, d), lambda j, i: (j, 0, 0))],
        scratch=[pltpu.VMEM((f, d), jnp.float32)], sem=("parallel", "arbitrary"))(dout, w_d, gu)


def _ffn_dw_gu(fn, dgu):
    s, d = fn.shape
    f = dgu.shape[-1]
    tk = min(4 * FFN_TILE, s)
    nk = s // tk

    def body(fn_ref, dgu_ref, dw_ref, acc):
        k = pl.program_id(2)
        part = _dot_tn(dgu_ref[0, 0], fn_ref[...])

        @pl.when(k == 0)
        def _():
            acc[...] = part

        @pl.when(k > 0)
        def _():
            acc[...] += part

        @pl.when(k == nk - 1)
        def _():
            dw_ref[0, 0] = acc[...].astype(dw_ref.dtype)

    return _pcall(
        body, name="ffn_dw_gate_up", grid=(N_DEV, 2, nk),
        out_shape=_sds((N_DEV, 2, f, d), WIRE),
        in_specs=[pl.BlockSpec((tk, d), lambda j, w, k: (k, 0)),
                  pl.BlockSpec((1, 1, tk, f), lambda j, w, k: (j, w, k, 0))],
        out_specs=pl.BlockSpec((1, 1, f, d), lambda j, w, k: (j, w, 0, 0)),
        scratch=[pltpu.VMEM((f, d), jnp.float32)], sem=("parallel", "parallel", "arbitrary"))(fn, dgu)


def _ffn_dfn(dgu, w_gu, after):
    _, _, s, f = dgu.shape
    d = w_gu.shape[3]
    tm = min(FFN_TILE, s)

    def body(dgu_ref, w_ref, dfn_ref):
        j = pl.program_id(1)
        part = (_dot(dgu_ref[0, 0], w_ref[0, 0]) + _dot(dgu_ref[0, 1], w_ref[0, 1])
                + _dot(dgu_ref[1, 0], w_ref[1, 0]) + _dot(dgu_ref[1, 1], w_ref[1, 1]))

        @pl.when(j == 0)
        def _():
            dfn_ref[...] = part

        @pl.when(j > 0)
        def _():
            dfn_ref[...] += part

    return _pcall(
        body, name="ffn_dfn", grid=(s // tm, N_DEV // 2),
        out_shape=_sds((s, d), jnp.float32),
        in_specs=[pl.BlockSpec((2, 2, tm, f), lambda i, j: (j, 0, i, 0)),
                  pl.BlockSpec((2, 2, f, d), lambda i, j: (j, 0, 0, 0))],
        out_specs=pl.BlockSpec((tm, d), lambda i, j: (i, 0)),
        sem=("parallel", "arbitrary"), after=after)(dgu, w_gu)


def _ffn_norm_bwd(d_fn, dout, h1, g_ffn):
    s, d = h1.shape
    tm = min(2 * ROW_TILE, s)

    def body(dfn_ref, do_ref, h1_ref, g_ref, dh1_ref, dg_ref):
        i = pl.program_id(0)

        @pl.when(i == 0)
        def _():
            dg_ref[...] = jnp.zeros_like(dg_ref)

        _, xn, r = _norm_fwd(h1_ref[...], g_ref[...])
        dx, dg = _norm_bwd(xn, r, g_ref[...], dfn_ref[...])
        dh1_ref[...] = do_ref[...] + dx
        dg_ref[...] += dg

    row = pl.BlockSpec((tm, d), lambda i: (i, 0))
    vec = pl.BlockSpec((1, d), lambda i: (0, 0))
    return _pcall(
        body, name="ffn_norm_bwd", grid=(s // tm,),
        out_shape=[_sds((s, d), jnp.float32), _sds((1, d), jnp.float32)],
        in_specs=[row, row, row, vec], out_specs=[row, vec], sem=("arbitrary",))(d_fn, dout, h1, g_ffn)


def _mem_bwd(proj, g_mq, km, vmm, d_y, y_m, lse):
    s = proj.shape[0]
    ml, hw = km.shape
    tm = min(FFN_TILE, s)
    scale = MEM_DIM ** -0.5

    def body(q_ref, g_ref, k_ref, v_ref, do_ref, y_ref, lse_ref, dq_ref, dk_ref, dv_ref, dg_ref):
        i = pl.program_id(0)

        @pl.when(i == 0)
        def _():
            dk_ref[...] = jnp.zeros_like(dk_ref)
            dv_ref[...] = jnp.zeros_like(dv_ref)
            dg_ref[...] = jnp.zeros_like(dg_ref)

        col = lax.broadcasted_iota(jnp.int32, (tm, MEM_HEADS), 1)
        lse_t = lse_ref[...]
        for h in range(MEM_HEADS):
            sl = slice(MEM_DIM * h, MEM_DIM * (h + 1))
            qn, xn, r = _norm_fwd(q_ref[:, sl], g_ref[...])
            lse_h = jnp.sum(jnp.where(col == h, lse_t, 0.0), -1, keepdims=True)
            p = jnp.exp(_dot_nt(qn, k_ref[:, sl]) * scale - lse_h)
            do = do_ref[:, sl]
            dd = jnp.sum(do * y_ref[:, sl], -1, keepdims=True)
            dp = _dot_nt(do, v_ref[:, sl])
            ds = (p * (dp - dd)).astype(MXU)
            dv_ref[:, sl] += _dot_tn(p, do)
            dk_ref[:, sl] += _dot_tn(ds, qn) * scale
            dx, dg = _norm_bwd(xn, r, g_ref[...], _dot(ds, k_ref[:, sl]) * scale)
            dq_ref[:, sl] = dx.astype(dq_ref.dtype)
            dg_ref[...] += dg

    full = pl.BlockSpec((ml, hw), lambda i: (0, 0))
    return _pcall(
        body, name="mem_bwd", grid=(s // tm,),
        out_shape=[_sds((s, hw), MXU), _sds((ml, hw), jnp.float32), _sds((ml, hw), jnp.float32),
                   _sds((1, MEM_DIM), jnp.float32)],
        in_specs=[pl.BlockSpec((tm, hw), lambda i: (i, C_QM // hw)), pl.BlockSpec((1, MEM_DIM), lambda i: (0, 0)),
                  full, full, pl.BlockSpec((tm, hw), lambda i: (i, 3)), pl.BlockSpec((tm, hw), lambda i: (i, 0)),
                  pl.BlockSpec((tm, MEM_HEADS), lambda i: (i, 0))],
        out_specs=[pl.BlockSpec((tm, hw), lambda i: (i, 0)), full, full,
                   pl.BlockSpec((1, MEM_DIM), lambda i: (0, 0))],
        sem=("arbitrary",))(proj, g_mq, km, vmm, d_y, y_m, lse)


def _memkv_bwd(mem, g_mem, w_mkv, g_mk, kv, memn, dk, dv):
    ml, d = mem.shape
    hw = MEM_HEADS * MEM_DIM

    def body(mem_ref, g_ref, w_ref, gk_ref, kv_ref, mn_ref, dk_ref, dv_ref, dw_ref, dgm_ref, dgk_ref):
        parts = []
        dgk = jnp.zeros((1, MEM_DIM), jnp.float32)
        for h in range(MEM_HEADS):
            sl = slice(MEM_DIM * h, MEM_DIM * (h + 1))
            _, xn, r = _norm_fwd(kv_ref[:, sl], gk_ref[...])
            dx, dg = _norm_bwd(xn, r, gk_ref[...], dk_ref[:, sl])
            parts.append(dx)
            dgk = dgk + dg
        dkv = jnp.concatenate(parts + [dv_ref[...]], axis=1).astype(MXU)
        dgk_ref[...] = dgk
        dw_ref[...] = _dot_tn(mn_ref[...], dkv).astype(dw_ref.dtype)
        d_mn = _dot_nt(dkv, w_ref[...])
        _, xn, _ = _norm_fwd(mem_ref[...], g_ref[...])
        dgm_ref[...] = jnp.sum(d_mn * xn, 0, keepdims=True)

    vm = pl.BlockSpec(memory_space=pltpu.VMEM)
    return _pcall(
        body, name="memkv_bwd",
        out_shape=[_sds((d, 2 * hw), WIRE), _sds((1, d), jnp.float32), _sds((1, MEM_DIM), jnp.float32)],
        in_specs=[vm] * 8, out_specs=[vm] * 3)(mem, g_mem, w_mkv, g_mk, kv, memn, dk, dv)


def _mla_bwd(qc, kc, v, d_y, y_b, lse, after):
    nh, s, _ = qc.shape
    t = min(ATT_TILE, s)
    nb = s // t
    scale = (MLA_NOPE + MLA_ROPE) ** -0.5

    def body(q_ref, k_ref, v_ref, do_ref, y_ref, lse_ref, dq_ref, dk_ref, dv_ref, dk_acc, dv_acc):
        kj, qi = pl.program_id(1), pl.program_id(2)

        @pl.when((kj == 0) & (qi == 0))
        def _():
            dq_ref[...] = jnp.zeros_like(dq_ref)

        @pl.when(qi == kj)
        def _():
            dk_acc[...] = jnp.zeros_like(dk_acc)
            dv_acc[...] = jnp.zeros_like(dv_acc)

        def step(diagonal):
            rc = t // 4 if diagonal else t
            for c in range(t // rc):
                rows = slice(rc * c, rc * (c + 1))
                keys = slice(0, rc * (c + 1))
                q, k = q_ref[0, rows, :], k_ref[0, keys, :]
                sc = _dot_nt(q, k) * (scale * LOG2E)
                if diagonal:
                    r_i = lax.broadcasted_iota(jnp.int32, sc.shape, 0) + rc * c
                    c_i = lax.broadcasted_iota(jnp.int32, sc.shape, 1)
                    sc = jnp.where(c_i <= r_i, sc, NEG_INF)
                p = jnp.exp2(sc - lse_ref[0, rows, :])
                do = do_ref[rows, :]
                dd = jnp.sum(do * y_ref[rows, :], -1, keepdims=True)
                dp = _dot_nt(do, v_ref[0, keys, :])
                ds = (p * (dp - dd) * scale).astype(MXU)
                dv_acc[keys, :] += _dot_tn(p, do)
                dk_acc[keys, :] += _dot_tn(ds, q)
                out_rows = pl.ds(pl.multiple_of(qi * t + rc * c, rc), rc)
                dq_ref[0, out_rows, :] += _dot(ds, k)

        @pl.when(qi > kj)
        def _():
            step(False)

        @pl.when(qi == kj)
        def _():
            step(True)

        @pl.when(qi == nb - 1)
        def _():
            dk_ref[0] = dk_acc[...]
            dv_ref[0] = dv_acc[...]

    qmap = lambda h, j, i: (h, jnp.maximum(i, j), 0)
    return _pcall(
        body, name="mla_bwd", grid=(nh, nb, nb),
        out_shape=[_sds((nh, s, 256), jnp.float32), _sds((nh, s, 256), jnp.float32),
                   _sds((nh, s, MLA_V), jnp.float32)],
        in_specs=[pl.BlockSpec((1, t, 256), qmap),
                  pl.BlockSpec((1, t, 256), lambda h, j, i: (h, j, 0)),
                  pl.BlockSpec((1, t, MLA_V), lambda h, j, i: (h, j, 0)),
                  pl.BlockSpec((t, MLA_V), lambda h, j, i: (jnp.maximum(i, j), 8 + h)),
                  pl.BlockSpec((t, MLA_V), lambda h, j, i: (jnp.maximum(i, j), h)),
                  pl.BlockSpec((1, t, 1), qmap)],
        out_specs=[pl.BlockSpec((1, s, 256), lambda h, j, i: (h, 0, 0)),
                   pl.BlockSpec((1, t, 256), lambda h, j, i: (h, j, 0)),
                   pl.BlockSpec((1, t, MLA_V), lambda h, j, i: (h, j, 0))],
        scratch=[pltpu.VMEM((t, 256), jnp.float32), pltpu.VMEM((t, MLA_V), jnp.float32)],
        sem=("parallel", "arbitrary", "arbitrary"), after=after)(qc, kc, v, d_y, y_b, lse)


def _mla_prep_bwd(proj, cos, sin, g_cq, g_ckv, w_uq, w_ukv, g_qn, g_qr, g_kn, g_kr,
                  qb, kvb, cqn, ckvn, dqc, dkc, dv):
    s = proj.shape[0]
    tm = min(ROW_TILE, s)
    nh = MLA_HEADS
    ni = s // tm

    def body(cq_ref, ckv_ref, kr_ref, cos_ref, sin_ref, gcq_ref, gckv_ref, wuq_ref, wukv_ref,
             gqn_ref, gqr_ref, gkn_ref, gkr_ref, qb_ref, kvb_ref, cqn_ref, ckvn_ref, dqc_ref, dkc_ref, dv_ref,
             dcq_ref, dckv_ref, dkr_ref, dwuq_ref, dwukv_ref,
             dgcq_ref, dgckv_ref, dgqn_ref, dgqr_ref, dgkn_ref, dgkr_ref, acc_uq, acc_ukv):
        i = pl.program_id(0)

        @pl.when(i == 0)
        def _():
            acc_uq[...] = jnp.zeros_like(acc_uq)
            acc_ukv[...] = jnp.zeros_like(acc_ukv)
            for ref in (dgcq_ref, dgckv_ref, dgqn_ref, dgqr_ref, dgkn_ref, dgkr_ref):
                ref[...] = jnp.zeros_like(ref)

        cos_t, sin_t = cos_ref[...], sin_ref[...]
        lo = _lo_mask((tm, LANES))
        qb_v, kvb_v = qb_ref[...], kvb_ref[...]
        dq_parts, dgqn = [], jnp.zeros((1, LANES), jnp.float32)
        for h in range(nh):
            _, xn, r = _norm_fwd(qb_v[:, MLA_NOPE * h: MLA_NOPE * (h + 1)], gqn_ref[...])
            dx, dg = _norm_bwd(xn, r, gqn_ref[...], dqc_ref[h][:, :MLA_NOPE])
            dq_parts.append(dx)
            dgqn = dgqn + dg
        dgqn_ref[...] += dgqn
        dgqr = jnp.zeros((1, LANES), jnp.float32)
        for j in range(nh // 2):
            d_rope = jnp.where(lo, dqc_ref[2 * j][:, MLA_NOPE:], dqc_ref[2 * j + 1][:, MLA_NOPE:])
            d_pre = _rope_bwd(d_rope, cos_t, sin_t)
            xr = qb_v[:, nh * MLA_NOPE + LANES * j: nh * MLA_NOPE + LANES * (j + 1)]
            _, xn, r = _norm_fwd(xr, gqr_ref[...], half=True)
            dx, dg = _norm_bwd(xn, r, gqr_ref[...], d_pre, half=True)
            dq_parts.append(dx)
            dgqr = dgqr + dg
        dgqr_ref[...] += dgqr
        dqb = jnp.concatenate(dq_parts, axis=1).astype(MXU)
        acc_uq[...] += _dot_tn(dqb, cqn_ref[...])
        _, xn, r = _norm_fwd(cq_ref[...], gcq_ref[...])
        dx, dg = _norm_bwd(xn, r, gcq_ref[...], _dot(dqb, wuq_ref[...]))
        dcq_ref[...] = dx.astype(dcq_ref.dtype)
        dgcq_ref[...] += dg
        dkv_parts, dgkn = [], jnp.zeros((1, LANES), jnp.float32)
        d_kr2 = jnp.zeros((tm, LANES), jnp.float32)
        for h in range(nh):
            _, xn, r = _norm_fwd(kvb_v[:, 256 * h: 256 * h + MLA_NOPE], gkn_ref[...])
            dx, dg = _norm_bwd(xn, r, gkn_ref[...], dkc_ref[h][:, :MLA_NOPE])
            dkv_parts += [dx, dv_ref[h]]
            dgkn = dgkn + dg
            d_kr2 = d_kr2 + dkc_ref[h][:, MLA_NOPE:]
        dgkn_ref[...] += dgkn
        dkvb = jnp.concatenate(dkv_parts, axis=1).astype(MXU)
        d_ckvn = jnp.zeros((tm, 512), jnp.float32)
        part_ukv = _dot_tn(ckvn_ref[...], dkvb)
        for dev in range(N_DEV):
            cols = slice(LANES * dev, LANES * (dev + 1))
            acc_ukv[dev] += part_ukv[:, cols]
            d_ckvn = d_ckvn + _dot_nt(dkvb[:, cols], wukv_ref[dev])
        _, xn, r = _norm_fwd(ckv_ref[...], gckv_ref[...])
        dx, dg = _norm_bwd(xn, r, gckv_ref[...], d_ckvn)
        dckv_ref[...] = dx.astype(dckv_ref.dtype)
        dgckv_ref[...] += dg
        d_kr = jnp.where(lo, d_kr2 + pltpu.roll(d_kr2, 64, 1), 0.0)
        d_pre = _rope_bwd(d_kr, cos_t, sin_t)
        _, xn, r = _norm_fwd(kr_ref[...], gkr_ref[...], half=True)
        dx, dg = _norm_bwd(xn, r, gkr_ref[...], d_pre, half=True)
        dkr_ref[...] = jnp.where(lo, dx, 0.0).astype(dkr_ref.dtype)
        dgkr_ref[...] += jnp.where(_lo_mask((1, LANES)), dg, 0.0)

        @pl.when(i == ni - 1)
        def _():
            dwuq_ref[...] = acc_uq[...].astype(dwuq_ref.dtype)
            dwukv_ref[...] = acc_ukv[...].astype(dwukv_ref.dtype)

    def col(width, start):
        return pl.BlockSpec((tm, width), lambda i: (i, start // width))

    def full(shape):
        return pl.BlockSpec(shape, lambda i: (0,) * len(shape))

    def row(width):
        return pl.BlockSpec((tm, width), lambda i: (i, 0))

    def heads(width):
        return pl.BlockSpec((nh, tm, width), lambda i: (0, i, 0))

    vec = full((1, LANES))
    return _pcall(
        body, name="mla_prep_bwd", grid=(ni,),
        out_shape=[_sds((s, 512), MXU), _sds((s, 512), MXU), _sds((s, LANES), MXU),
                   _sds((768, 512), WIRE), _sds((N_DEV, 512, LANES), WIRE),
                   _sds((1, 512), jnp.float32), _sds((1, 512), jnp.float32)] + [_sds((1, LANES), jnp.float32)] * 4,
        in_specs=[col(512, C_CQ), col(512, C_CKV), col(LANES, C_KR), row(LANES), row(LANES),
                  full((1, 512)), full((1, 512)), full((768, 512)), full((N_DEV, 512, LANES)), vec, vec, vec, vec,
                  row(768), row(1024), row(512), row(512), heads(256), heads(256), heads(MLA_V)],
        out_specs=[row(512), row(512), row(LANES), full((768, 512)), full((N_DEV, 512, LANES)),
                   full((1, 512)), full((1, 512)), vec, vec, vec, vec],
        scratch=[pltpu.VMEM((768, 512), jnp.float32), pltpu.VMEM((N_DEV, 512, LANES), jnp.float32)],
        sem=("arbitrary",))(proj, proj, proj, cos, sin, g_cq, g_ckv, w_uq, w_ukv, g_qn, g_qr, g_kn, g_kr,
                            qb, kvb, cqn, ckvn, dqc, dkc, dv)


def _swa_bwd(proj, posc, posr, gq, gk, sinks, d_y, y_a, lse, after):
    s = proj.shape[0]
    b = SWA_BLOCK
    nb = s // b
    scale = SWA_DIM ** -0.5

    def body(q_ref, kp_ref, kc_ref, vp_ref, vc_ref, pq_ref, pkp_ref, pkc_ref, gq_ref, gk_ref, sink_ref,
             do_ref, y_ref, lse_ref, kfull_ref,
             dq_ref, dk_ref, dv_ref, dgq_ref, dgk_ref, dsink_ref, dk_acc, dv_acc):
        n = pl.program_id(0)

        @pl.when(n == 0)
        def _():
            dk_acc[...] = jnp.zeros_like(dk_acc)
            dv_acc[...] = jnp.zeros_like(dv_acc)
            dgq_ref[...] = jnp.zeros_like(dgq_ref)
            dsink_ref[...] = jnp.zeros_like(dsink_ref)

        kn, v, bias = _swa_common(n, kp_ref[...], kc_ref[...], vp_ref[...], vc_ref[...],
                                  pq_ref[...], pkp_ref[...], pkc_ref[...], gk_ref[...])
        lo = _lo_mask((b, LANES))
        col = lax.broadcasted_iota(jnp.int32, (b, SWA_Q_HEADS), 1)
        col1 = lax.broadcasted_iota(jnp.int32, (1, SWA_Q_HEADS), 1)
        lse_t = lse_ref[...]
        dk_blk = jnp.zeros((2 * b, LANES), jnp.float32)
        dv_blk = jnp.zeros((2 * b, LANES), jnp.float32)
        dgq = jnp.zeros((1, LANES), jnp.float32)
        dsink = jnp.zeros((1, SWA_Q_HEADS), jnp.float32)
        for j in range(SWA_Q_HEADS // 2):
            hk = (2 * j) // (SWA_Q_HEADS // SWA_KV_HEADS)
            kvmask = lo if hk == 0 else jnp.logical_not(lo)
            sl = slice(LANES * j, LANES * (j + 1))
            qn, xn, r = _norm_fwd(q_ref[:, sl], gq_ref[...], half=True)
            qn = qn * scale
            qsw = pltpu.roll(qn, 64, 1)
            d2 = do_ref[:, sl]
            d2sw = pltpu.roll(d2, 64, 1)
            prod = d2 * y_ref[:, sl]
            dqs = []
            for e in range(2):
                h = 2 * j + e
                half_e = lo if e == 0 else jnp.logical_not(lo)
                qm = jnp.where(kvmask, qn if e == hk else qsw, 0.0)
                dm = jnp.where(kvmask, d2 if e == hk else d2sw, 0.0)
                sc = _dot_nt(qm, kn) + _alibi_slope(h) * bias
                lse_h = jnp.sum(jnp.where(col == h, lse_t, 0.0), -1, keepdims=True)
                p = jnp.exp(sc - lse_h)
                dd = jnp.sum(jnp.where(half_e, prod, 0.0), -1, keepdims=True)
                dp = _dot_nt(dm, v)
                ds = (p * (dp - dd)).astype(MXU)
                dsink = dsink - jnp.where(col1 == h, jnp.sum(jnp.exp(sink_ref[h] - lse_h) * dd), 0.0)
                dq_m = _dot(ds, kn) * scale
                dk_blk = dk_blk + _dot_tn(ds, qm)
                dv_blk = dv_blk + _dot_tn(p, dm)
                dqs.append(dq_m if e == hk else pltpu.roll(dq_m, 64, 1))
            dx, dg = _norm_bwd(xn, r, gq_ref[...], jnp.where(lo, dqs[0], dqs[1]), half=True)
            dq_ref[:, sl] = dx.astype(dq_ref.dtype)
            dgq = dgq + dg
        dgq_ref[...] += dgq
        dsink_ref[...] += dsink
        prev = pl.ds(pl.multiple_of(jnp.maximum(n - 1, 0) * b, b), b)
        cur = pl.ds(pl.multiple_of(n * b, b), b)
        dk_acc[prev, :] += dk_blk[:b]
        dv_acc[prev, :] += dv_blk[:b]
        dk_acc[cur, :] += dk_blk[b:]
        dv_acc[cur, :] += dv_blk[b:]

        @pl.when(n == nb - 1)
        def _():
            _, kxn, kr = _norm_fwd(kfull_ref[...], gk_ref[...], half=True)
            dx, dg = _norm_bwd(kxn, kr, gk_ref[...], dk_acc[...], half=True)
            dk_ref[...] = dx.astype(dk_ref.dtype)
            dv_ref[...] = dv_acc[...].astype(dv_ref.dtype)
            dgk_ref[...] = dg

    full = pl.BlockSpec((s, LANES), lambda n: (0, 0))
    vec = pl.BlockSpec((1, LANES), lambda n: (0, 0))
    return _pcall(
        body, name="swa_bwd", grid=(nb,),
        out_shape=[_sds((s, 1024), MXU), _sds((s, LANES), MXU), _sds((s, LANES), MXU),
                   _sds((1, LANES), jnp.float32), _sds((1, LANES), jnp.float32),
                   _sds((1, SWA_Q_HEADS), jnp.float32)],
        in_specs=_swa_specs(s) + [pl.BlockSpec((b, 1024), lambda n: (n, 0)), pl.BlockSpec((b, 1024), lambda n: (n, 0)),
                                  pl.BlockSpec((b, SWA_Q_HEADS), lambda n: (n, 0)),
                                  pl.BlockSpec((s, LANES), lambda n: (0, C_KA // LANES))],
        out_specs=[pl.BlockSpec((b, 1024), lambda n: (n, 0)), full, full, vec, vec,
                   pl.BlockSpec((1, SWA_Q_HEADS), lambda n: (0, 0))],
        scratch=[pltpu.VMEM((s, LANES), jnp.float32), pltpu.VMEM((s, LANES), jnp.float32)],
        sem=("arbitrary",), after=after)(proj, proj, proj, proj, proj, posc, posr, posr, gq, gk, sinks, d_y, y_a, lse,
                                         proj)


def _dx(d_proj, w_in, x, g, d_h1, after):
    s, d = x.shape
    n = w_in.shape[0]
    tm = min(2 * ROW_TILE, s)

    def body(dp_ref, w_ref, x_ref, g_ref, dh_ref, dx_ref, dg_ref):
        i = pl.program_id(0)

        @pl.when(i == 0)
        def _():
            dg_ref[...] = jnp.zeros_like(dg_ref)

        d_hn = _dot(dp_ref[...], w_ref[...])
        _, xn, r = _norm_fwd(x_ref[...], g_ref[...])
        dx, dg = _norm_bwd(xn, r, g_ref[...], d_hn)
        dx_ref[...] = dh_ref[...] + dx
        dg_ref[...] += dg

    row = pl.BlockSpec((tm, d), lambda i: (i, 0))
    vec = pl.BlockSpec((1, d), lambda i: (0, 0))
    return _pcall(
        body, name="grad_x", grid=(s // tm,),
        out_shape=[_sds((s, d), jnp.float32), _sds((1, d), jnp.float32)],
        in_specs=[pl.BlockSpec((tm, n), lambda i: (i, 0)),
                  pl.BlockSpec((n, d), lambda i: (0, 0), pipeline_mode=pl.Buffered(1)), row, vec, row],
        out_specs=[row, vec], sem=("arbitrary",), after=after)(d_proj, w_in, x, g, d_h1)


_SMALL = ["attn_norm_g", "swa_q_norm_g", "swa_k_norm_g", "swa_sinks", "mla_cq_norm_g", "mla_ckv_norm_g",
          "mla_qn_norm_g", "mla_qr_norm_g", "mla_kn_norm_g", "mla_kr_norm_g", "mem_norm_g",
          "mem_q_norm_g", "mem_k_norm_g", "ffn_norm_g"]


def kernel(x, mem, positions, attn_norm_g, w_in, swa_q_norm_g, swa_k_norm_g, swa_sinks, mla_cq_norm_g, mla_ckv_norm_g, w_uq, w_ukv, mla_qn_norm_g, mla_qr_norm_g, mla_kn_norm_g, mla_kr_norm_g, mem_norm_g, w_mem_kv, mem_q_norm_g, mem_k_norm_g, w_out, ffn_norm_g, w_gate, w_up, w_down, loss_target, m_attn_norm_g, m_w_in, m_swa_q_norm_g, m_swa_k_norm_g, m_swa_sinks, m_mla_cq_norm_g, m_mla_ckv_norm_g, m_w_uq, m_w_ukv, m_mla_qn_norm_g, m_mla_qr_norm_g, m_mla_kn_norm_g, m_mla_kr_norm_g, m_mem_norm_g, m_w_mem_kv, m_mem_q_norm_g, m_mem_k_norm_g, m_w_out, m_ffn_norm_g, m_w_gate, m_w_up, m_w_down, v_attn_norm_g, v_w_in, v_swa_q_norm_g, v_swa_k_norm_g, v_swa_sinks, v_mla_cq_norm_g, v_mla_ckv_norm_g, v_w_uq, v_w_ukv, v_mla_qn_norm_g, v_mla_qr_norm_g, v_mla_kn_norm_g, v_mla_kr_norm_g, v_mem_norm_g, v_w_mem_kv, v_mem_q_norm_g, v_mem_k_norm_g, v_w_out, v_ffn_norm_g, v_w_gate, v_w_up, v_w_down):
    args = dict(locals())
    x2, mem2, tgt = x[0], mem[0], loss_target[0]
    s, d = x2.shape
    n_in = w_in.shape[2]
    f = w_gate.shape[2]

    (g_in,) = _all_gather([w_in[0].T.astype(WIRE)])
    mix_shards = [w_uq[0].T.astype(WIRE), w_ukv[0].astype(WIRE), w_mem_kv[0].astype(WIRE),
                  _to_wire([w_out[0]], g_in, "wire_out")[0]]
    g_uq, wkv, g_mkv, g_out = _all_gather_background(mix_shards, 5, "all_gather_mix_weights")
    ffn_shards = [_to_wire([w_gate[0].T, w_up[0].T], g_in, "wire_gate_up"),
                  _to_wire([w_down[0]], g_in, "wire_down")[0]]
    w_gu, w_d = _all_gather_background(ffn_shards, 1, "all_gather_ffn_weights")
    wi = g_in.reshape(N_DEV * n_in, d)
    wi = jnp.concatenate([wi[0:1024], wi[1280:1792], wi[1792:2304], wi[2368:2880],
                          wi[1024:1152], wi[1152:1280], wi[2304:2368],
                          jnp.zeros((IN_PAD - 2880, d), wi.dtype)], axis=0)
    wq = g_uq.reshape(768, 512)
    wq = jnp.concatenate([wq[192 * h: 192 * h + 128] for h in range(4)]
                         + [wq[192 * h + 128: 192 * (h + 1)] for h in range(4)], axis=0)
    wmkv = g_mkv.reshape(-1, g_mkv.shape[-1])
    wo = g_out.reshape(-1, d)

    pos = positions[0].astype(jnp.float32)
    inv_freq = ROPE_THETA ** (-jnp.arange(0, MLA_ROPE, 2, dtype=jnp.float32) / MLA_ROPE)
    ang = pos[:, None] * inv_freq
    cos32, sin32 = jnp.cos(ang), jnp.sin(ang)
    cos_t = jnp.tile(cos32, (1, 4))
    sin_t = jnp.tile(jnp.concatenate([-sin32, sin32], axis=1), (1, 2))
    posc, posr = pos.reshape(s, 1), pos.reshape(1, s)
    two = lambda g: jnp.tile(g, (1, 2))
    gq2, gk2, gqr2, gkr2 = two(swa_q_norm_g), two(swa_k_norm_g), two(mla_qr_norm_g), two(mla_kr_norm_g)
    sinks1 = swa_sinks[0]

    proj, hn = _in_proj(x2, attn_norm_g, wi)
    qc, kc, vb, qb, kvb, cqn, ckvn = _mla_prep(proj, cos_t, sin_t, mla_cq_norm_g, mla_ckv_norm_g, wq, wkv,
                                                mla_qn_norm_g, gqr2, mla_kn_norm_g, gkr2)
    y_b, lse_b = _mla_fwd(qc, kc, vb)
    km, vmm, kvm, memn = _memkv_prep(mem2, mem_norm_g, wmkv, mem_k_norm_g)
    y_m, lse_m = _mem_fwd(proj, mem_q_norm_g, km, vmm)
    y_a, lse_a = _swa_fwd(proj, posc, posr, gq2, gk2, sinks1)
    h1, fn = _out_proj(y_a, y_b, y_m, x2, wo, ffn_norm_g)
    gu, act = _ffn_gu(fn, w_gu)
    dout, dout_b, loss_tile = _ffn_down(act, w_d, h1, tgt)

    dgu, dw_d = _ffn_bwd_act(dout_b, w_d, gu)
    dw_gu = _ffn_dw_gu(fn, dgu)
    r_gu, r_d = _exchange_grads_background([dw_gu, dw_d], 2, "exchange_ffn_grads")
    d_h1, dg_ffn = _ffn_norm_bwd(_ffn_dfn(dgu, w_gu, dw_gu), dout, h1, ffn_norm_g)
    d_y = _mm(d_h1, wo, tb=True, out_dtype=jnp.float32, tm=FFN_TILE, tk=2048, name="d_mix")
    dw_out = jnp.concatenate([
        _mm(y_a, d_h1, ta=True, out_dtype=WIRE, tm=1024, tk=1024, name="dw_out_a"),
        _mm(y_b, d_h1, ta=True, out_dtype=WIRE, tm=1024, tk=1024, name="dw_out_b"),
        _mm(y_m, d_h1, ta=True, out_dtype=WIRE, tm=1024, tk=1024, name="dw_out_m")], axis=0)
    d_qm, dkm, dvmm, dg_mq = _mem_bwd(proj, mem_q_norm_g, km, vmm, d_y, y_m, lse_m)
    dw_mkv, dg_mem, dg_mk = _memkv_bwd(mem2, mem_norm_g, wmkv, mem_k_norm_g, kvm, memn, dkm, dvmm)
    r_mkv, r_out = _exchange_grads_background([dw_mkv.reshape(g_mkv.shape), dw_out.reshape(g_out.shape)], 3,
                                              "exchange_mix_grads")
    dqc, dkc, dvb = _mla_bwd(qc, kc, vb, d_y, y_b, lse_b, dw_mkv)
    (d_cq, d_ckv, d_kr, dw_uq, dw_ukv, dg_cq, dg_ckv, dg_qn, dg_qr, dg_kn, dg_kr) = _mla_prep_bwd(
        proj, cos_t, sin_t, mla_cq_norm_g, mla_ckv_norm_g, wq, wkv, mla_qn_norm_g, gqr2, mla_kn_norm_g, gkr2,
        qb, kvb, cqn, ckvn, dqc, dkc, dvb)
    d_qa, d_ka, d_va, dg_q, dg_k, d_sinks = _swa_bwd(proj, posc, posr, gq2, gk2, sinks1, d_y, y_a, lse_a, dw_out)
    d_proj = jnp.concatenate([d_qa, d_cq, d_ckv, d_qm, d_ka, d_va, d_kr], axis=1)
    gi = _dw_in(hn, d_proj, n_in)

    gq_ = jnp.concatenate(sum([[dw_uq[128 * h: 128 * (h + 1)], dw_uq[512 + 64 * h: 512 + 64 * (h + 1)]]
                               for h in range(4)], []), axis=0)
    gq_ = gq_.reshape(N_DEV, 96, 512)
    r_in, r_uq, r_ukv = _exchange_grads_background([gi, gq_, dw_ukv], 4, "exchange_in_grads")
    grad_x, dg_attn = _dx(d_proj, wi, x2, attn_norm_g, d_h1, gi)

    big = {}
    last = [None]

    def adam(name, r, transposed=False, which=None):
        w, m, v = args[name][0], args["m_" + name][0], args["v_" + name][0]
        if transposed:
            outs = _adam_big(r, w.T, m.T, v.T, "adam_" + name, last[0], which)
            big[name] = [o.T[None] for o in outs]
        else:
            outs = _adam_big(r, w, m, v, "adam_" + name, last[0])
            big[name] = [o[None] for o in outs]
        last[0] = outs[0]

    adam("w_gate", r_gu, True, which=0)
    adam("w_up", r_gu, True, which=1)
    adam("w_down", r_d)
    adam("w_out", r_out)
    adam("w_mem_kv", r_mkv)
    adam("w_in", r_in, True)
    adam("w_uq", r_uq, True)
    adam("w_ukv", r_ukv)

    small_g = {
        "attn_norm_g": dg_attn, "swa_q_norm_g": dg_q, "swa_k_norm_g": dg_k,
        "swa_sinks": d_sinks, "mla_cq_norm_g": dg_cq, "mla_ckv_norm_g": dg_ckv, "mla_qn_norm_g": dg_qn,
        "mla_qr_norm_g": dg_qr, "mla_kn_norm_g": dg_kn, "mla_kr_norm_g": dg_kr,
        "mem_norm_g": dg_mem, "mem_q_norm_g": dg_mq, "mem_k_norm_g": dg_mk, "ffn_norm_g": dg_ffn}
    packed_g = _small_allreduce([small_g[n] for n in _SMALL], loss_tile, [args[n].shape[-1] for n in _SMALL])
    loss11, small_out = _small_adam(packed_g, [args[n] for n in _SMALL],
                                    [args["m_" + n] for n in _SMALL], [args["v_" + n] for n in _SMALL])
    small = dict(zip(_SMALL, small_out))
    loss = loss11.reshape(())

    order = ["attn_norm_g", "w_in", "swa_q_norm_g", "swa_k_norm_g", "swa_sinks", "mla_cq_norm_g", "mla_ckv_norm_g",
             "w_uq", "w_ukv", "mla_qn_norm_g", "mla_qr_norm_g", "mla_kn_norm_g", "mla_kr_norm_g", "mem_norm_g",
             "w_mem_kv", "mem_q_norm_g", "mem_k_norm_g", "w_out", "ffn_norm_g", "w_gate", "w_up", "w_down"]
    res = {n: (big[n] if n in big else list(small[n])) for n in order}
    outs = [loss, grad_x[None]]
    for kind in range(4):
        outs += [res[n][kind] for n in order]
    return tuple(outs)
```

```python
import jax
import jax.numpy as jnp
from jax import lax
from jax.experimental import pallas as pl
from jax.experimental.pallas import tpu as pltpu
from jax.experimental.pallas import tpu_sc as plsc

MXU = jnp.bfloat16
WIRE = jnp.bfloat16
EPS = 1e-6
NEG_INF = -1e30
LOG2E = 1.4426950408889634
N_DEV = 8
LANES = 128
ROW_TILE = 256
FFN_TILE = 512
ATT_TILE = 1024
SWA_BLOCK = 128
VMEM_LIMIT = 56 * 1024 * 1024

SWA_Q_HEADS, SWA_KV_HEADS, SWA_DIM = 16, 2, 64
MLA_HEADS, MLA_NOPE, MLA_ROPE, MLA_V = 4, 128, 64, 128
MEM_HEADS, MEM_DIM = 4, 128
ROPE_THETA = 10000.0
ADAM_LR, ADAM_B1, ADAM_B2, ADAM_EPS, ADAM_WD, ADAM_STEP = 0.001, 0.9, 0.999, 1e-08, 0.01, 10

C_QA, C_CQ, C_CKV, C_QM, C_KA, C_VA, C_KR, IN_PAD = 0, 1024, 1536, 2048, 2560, 2688, 2816, 2944


def _pcall(body, *, name, out_shape, in_specs, out_specs, grid=(), scratch=(), sem=None, after=None):
    params = pltpu.CompilerParams(dimension_semantics=sem, vmem_limit_bytes=VMEM_LIMIT)
    if after is not None:
        n_in, inner = len(in_specs), body

        def body(*refs):
            inner(*refs[:n_in], *refs[n_in + 1:])

        in_specs = list(in_specs) + [pl.BlockSpec(memory_space=pl.ANY)]
    call = pl.pallas_call(body, name=name, grid=grid, in_specs=in_specs, out_specs=out_specs,
                          out_shape=out_shape, scratch_shapes=list(scratch), compiler_params=params)
    return call if after is None else (lambda *ops: call(*ops, after))


def _sds(shape, dtype):
    return jax.ShapeDtypeStruct(tuple(shape), dtype)


def _dot(a, b):
    return jnp.dot(a.astype(MXU), b.astype(MXU), preferred_element_type=jnp.float32)


def _dot_nt(a, b):
    return lax.dot_general(a.astype(MXU), b.astype(MXU), (((1,), (1,)), ((), ())),
                           preferred_element_type=jnp.float32)


def _dot_tn(a, b):
    return lax.dot_general(a.astype(MXU), b.astype(MXU), (((0,), (0,)), ((), ())),
                           preferred_element_type=jnp.float32)


def _lo_mask(shape):
    return (lax.broadcasted_iota(jnp.int32, shape, len(shape) - 1) % LANES) < 64


def _norm_fwd(x, g, half=False):
    x2 = x * x
    if half:
        lo = _lo_mask(x.shape)
        s_lo = jnp.sum(jnp.where(lo, x2, 0.0), -1, keepdims=True)
        s_hi = jnp.sum(jnp.where(lo, 0.0, x2), -1, keepdims=True)
        r = jnp.where(lo, lax.rsqrt(s_lo / 64.0 + EPS), lax.rsqrt(s_hi / 64.0 + EPS))
    else:
        r = lax.rsqrt(jnp.mean(x2, -1, keepdims=True) + EPS)
    xn = x * r
    return xn * g, xn, r


def _norm_bwd(xn, r, g, dy, half=False):
    t = dy * g
    tx = t * xn
    if half:
        lo = _lo_mask(xn.shape)
        m_lo = jnp.sum(jnp.where(lo, tx, 0.0), -1, keepdims=True) / 64.0
        m_hi = jnp.sum(jnp.where(lo, 0.0, tx), -1, keepdims=True) / 64.0
        m = jnp.where(lo, m_lo, m_hi)
    else:
        m = jnp.mean(tx, -1, keepdims=True)
    dx = r * (t - xn * m)
    dg = jnp.sum(dy * xn, 0, keepdims=True)
    return dx, dg


def _swap32(x):
    lane = lax.broadcasted_iota(jnp.int32, x.shape, 1)
    return jnp.where((lane % 64) < 32, pltpu.roll(x, 96, 1), pltpu.roll(x, 32, 1))


def _rope(x, cos, sin):
    return x * cos + _swap32(x) * sin


def _rope_bwd(d, cos, sin):
    return d * cos + _swap32(d * sin)


def _my_coords():
    return lax.axis_index("x"), lax.axis_index("y"), lax.axis_index("c")


def _dev_index(px, py, pc):
    return 4 * px + 2 * py + pc


_FLIPS = [(0, 0, 1), (0, 1, 0), (0, 1, 1), (1, 0, 0), (1, 0, 1), (1, 1, 0), (1, 1, 1)]


def _flip(coords, f):
    return tuple((1 - v) if b else v for v, b in zip(coords, f))


def _all_gather(shards):
    n = len(shards)

    def body(*refs):
        ins, outs = refs[:n], refs[n:2 * n]
        send_sems, recv_sems, local_sems = refs[2 * n:]
        x, y, c = _my_coords()
        me, sibling = (x, y, c), (x, y, 1 - c)
        chips = [(1 - x, y), (x, 1 - y), (1 - x, 1 - y)]

        def copy(w, k, block, to, src=None):
            dst = outs[w].at[_dev_index(*block)]
            return pltpu.make_async_remote_copy(
                src_ref=dst if src is None else src, dst_ref=dst,
                send_sem=send_sems.at[w, k], recv_sem=recv_sems.at[w, k],
                device_id=to, device_id_type=pl.DeviceIdType.MESH)

        sends, locals_ = [], []
        for w in range(n):
            mine = pltpu.make_async_copy(ins[w], outs[w].at[_dev_index(*me)], local_sems.at[w])
            mine.start()
            locals_.append(mine)
            first = [copy(w, 0, me, sibling, src=ins[w])]
            first += [copy(w, 1 + j, me, (*chip, c), src=ins[w]) for j, chip in enumerate(chips)]
            for cp in first:
                cp.start()
            sends += first
        for w in range(n):
            for j, chip in enumerate(chips):
                copy(w, 1 + j, (*chip, c), me).wait_recv()
                fwd = copy(w, 4 + j, (*chip, c), sibling)
                fwd.start()
                sends.append(fwd)
        for w in range(n):
            copy(w, 0, sibling, me).wait_recv()
            for j, chip in enumerate(chips):
                copy(w, 4 + j, (*chip, 1 - c), me).wait_recv()
        for cp in sends:
            cp.wait_send()
        for mine in locals_:
            mine.wait()

    any_spec = pl.BlockSpec(memory_space=pl.ANY)
    return _pcall(
        body, name="all_gather_weights",
        out_shape=[_sds((N_DEV,) + s.shape, s.dtype) for s in shards],
        in_specs=[any_spec] * n, out_specs=[any_spec] * n,
        scratch=[pltpu.SemaphoreType.DMA((n, 7)), pltpu.SemaphoreType.DMA((n, 7)),
                 pltpu.SemaphoreType.DMA((n,))])(*shards)


def _wire_cost(arrays):
    nbytes = sum(a.size * a.dtype.itemsize for a in arrays)
    return pl.CostEstimate(flops=0, transcendentals=0, bytes_accessed=40 * nbytes)


def _all_gather_background(shards, collective_id, name):
    n = len(shards)
    src_refs = [jax.new_ref(s, memory_space=pltpu.MemorySpace.HBM) for s in shards]
    out_refs = [jax.empty_ref(_sds((N_DEV,) + s.shape, s.dtype), memory_space=pltpu.MemorySpace.HBM) for s in shards]

    @pl.kernel(mesh=plsc.ScalarSubcoreMesh(axis_name="seq", num_cores=1), name=name,
               scratch_types=(pltpu.SemaphoreType.DMA((n, 7)), pltpu.SemaphoreType.DMA((n, 7)),
                              pltpu.SemaphoreType.DMA((n,))),
               compiler_params=pltpu.CompilerParams(collective_id=collective_id))
    def launch(send_sems, recv_sems, local_sems):
        x, y, c = _my_coords()
        me, sibling = (x, y, c), (x, y, 1 - c)
        chips = [(1 - x, y), (x, 1 - y), (1 - x, 1 - y)]
        barrier = pltpu.get_barrier_semaphore()
        for peer in [sibling] + [(*chip, c) for chip in chips]:
            pl.semaphore_signal(barrier, inc=1, device_id=peer, device_id_type=pl.DeviceIdType.MESH)
        pl.semaphore_wait(barrier, 4)

        def copy(w, k, block, to, src=None):
            dst = out_refs[w].at[_dev_index(*block)]
            return pltpu.make_async_remote_copy(
                src_ref=dst if src is None else src, dst_ref=dst,
                send_sem=send_sems.at[w, k], recv_sem=recv_sems.at[w, k],
                device_id=to, device_id_type=pl.DeviceIdType.MESH)

        sends, locals_ = [], []
        for w in range(n):
            mine = pltpu.make_async_copy(src_refs[w], out_refs[w].at[_dev_index(*me)], local_sems.at[w])
            mine.start()
            locals_.append(mine)
            first = [copy(w, 0, me, sibling, src=src_refs[w])]
            first += [copy(w, 1 + j, me, (*chip, c), src=src_refs[w]) for j, chip in enumerate(chips)]
            for cp in first:
                cp.start()
            sends += first
        for w in range(n):
            for j, chip in enumerate(chips):
                copy(w, 1 + j, (*chip, c), me).wait_recv()
                fwd = copy(w, 4 + j, (*chip, c), sibling)
                fwd.start()
                sends.append(fwd)
        for w in range(n):
            copy(w, 0, sibling, me).wait_recv()
            for j, chip in enumerate(chips):
                copy(w, 4 + j, (*chip, 1 - c), me).wait_recv()
        for cp in sends:
            cp.wait_send()
        for mine in locals_:
            mine.wait()

    launch()
    return [r[...] for r in out_refs]


def _exchange_grads(grads):
    n = len(grads)

    def body(*refs):
        ins, outs = refs[:n], refs[n:2 * n]
        send_sems, recv_sems, local_sems = refs[2 * n:]
        me = _my_coords()
        my_idx = _dev_index(*me)
        sends, locals_ = [], []
        for w in range(n):
            mine = pltpu.make_async_copy(ins[w].at[my_idx], outs[w].at[my_idx], local_sems.at[w])
            mine.start()
            locals_.append(mine)
            for k, f in enumerate(_FLIPS):
                peer = _flip(me, f)
                cp = pltpu.make_async_remote_copy(
                    src_ref=ins[w].at[_dev_index(*peer)], dst_ref=outs[w].at[my_idx],
                    send_sem=send_sems.at[w, k], recv_sem=recv_sems.at[w, k],
                    device_id=peer, device_id_type=pl.DeviceIdType.MESH)
                cp.start()
                sends.append(cp)
        for w in range(n):
            for k, f in enumerate(_FLIPS):
                peer = _flip(me, f)
                slot = outs[w].at[_dev_index(*peer)]
                pltpu.make_async_remote_copy(
                    src_ref=slot, dst_ref=slot,
                    send_sem=send_sems.at[w, k], recv_sem=recv_sems.at[w, k],
                    device_id=peer, device_id_type=pl.DeviceIdType.MESH).wait_recv()
        for cp in sends:
            cp.wait_send()
        for mine in locals_:
            mine.wait()

    any_spec = pl.BlockSpec(memory_space=pl.ANY)
    return _pcall(
        body, name="exchange_grads",
        out_shape=[_sds(g.shape, g.dtype) for g in grads],
        in_specs=[any_spec] * n, out_specs=[any_spec] * n,
        scratch=[pltpu.SemaphoreType.DMA((n, 7)), pltpu.SemaphoreType.DMA((n, 7)),
                 pltpu.SemaphoreType.DMA((n,))])(*grads)


def _exchange_grads_background(grads, collective_id, name):
    n = len(grads)
    src_refs = [jax.new_ref(g, memory_space=pltpu.MemorySpace.HBM) for g in grads]
    out_refs = [jax.empty_ref(_sds(g.shape, g.dtype), memory_space=pltpu.MemorySpace.HBM) for g in grads]

    @pl.kernel(mesh=plsc.ScalarSubcoreMesh(axis_name="seq", num_cores=1), name=name,
               scratch_types=(pltpu.SemaphoreType.DMA((n, 7)), pltpu.SemaphoreType.DMA((n, 7)),
                              pltpu.SemaphoreType.DMA((n,))),
               cost_estimate=_wire_cost(grads),
               compiler_params=pltpu.CompilerParams(collective_id=collective_id))
    def launch(send_sems, recv_sems, local_sems):
        me = _my_coords()
        my_idx = _dev_index(*me)
        peers = [_flip(me, f) for f in _FLIPS]
        barrier = pltpu.get_barrier_semaphore()
        for peer in peers:
            pl.semaphore_signal(barrier, inc=1, device_id=peer, device_id_type=pl.DeviceIdType.MESH)
        pl.semaphore_wait(barrier, len(peers))
        sends, locals_ = [], []
        for w in range(n):
            mine = pltpu.make_async_copy(src_refs[w].at[my_idx], out_refs[w].at[my_idx], local_sems.at[w])
            mine.start()
            locals_.append(mine)
            for k, peer in enumerate(peers):
                cp = pltpu.make_async_remote_copy(
                    src_ref=src_refs[w].at[_dev_index(*peer)], dst_ref=out_refs[w].at[my_idx],
                    send_sem=send_sems.at[w, k], recv_sem=recv_sems.at[w, k],
                    device_id=peer, device_id_type=pl.DeviceIdType.MESH)
                cp.start()
                sends.append(cp)
        for w in range(n):
            for k, peer in enumerate(peers):
                slot = out_refs[w].at[_dev_index(*peer)]
                pltpu.make_async_remote_copy(
                    src_ref=slot, dst_ref=slot, send_sem=send_sems.at[w, k], recv_sem=recv_sems.at[w, k],
                    device_id=peer, device_id_type=pl.DeviceIdType.MESH).wait_recv()
        for cp in sends:
            cp.wait_send()
        for mine in locals_:
            mine.wait()

    launch()
    return [r[...] for r in out_refs]


def _to_wire(parts, after, name):
    n = len(parts)
    rows, cols = parts[0].shape
    tr = rows // 2 if rows % 32 == 0 else rows

    def body(*refs):
        for k in range(n):
            refs[n][k] = refs[k][...].astype(WIRE)

    blk = pl.BlockSpec((tr, cols), lambda i: (i, 0))
    return _pcall(
        body, name=name, grid=(rows // tr,), out_shape=_sds((n, rows, cols), WIRE),
        in_specs=[blk] * n, out_specs=pl.BlockSpec((n, tr, cols), lambda i: (0, i, 0)),
        sem=("parallel",), after=after)(*parts)


def _adam_math(w, g, m, v):
    m = ADAM_B1 * m + (1.0 - ADAM_B1) * g
    v = ADAM_B2 * v + (1.0 - ADAM_B2) * (g * g)
    m_hat = m / (1.0 - ADAM_B1 ** ADAM_STEP)
    v_hat = v / (1.0 - ADAM_B2 ** ADAM_STEP)
    delta = -ADAM_LR * (m_hat / (jnp.sqrt(v_hat) + ADAM_EPS) + ADAM_WD * w)
    return delta, m, v


def _small_layout(sizes):
    row0, r = [], 0
    for n in sizes:
        row0.append(r)
        r += -(-n // LANES)
    return row0, r, -(-(r + 1) // 8) * 8


def _small_pieces(n):
    return [(k, min(LANES, n - LANES * k)) for k in range(-(-n // LANES))]


def _small_fill(pack, slot, srcs, sizes, row0, rows):
    pack[slot] = jnp.zeros((rows, LANES), jnp.float32)
    for p, n in enumerate(sizes):
        val = srcs[p][...]
        if val.shape[-1] == LANES and n == 64:
            pack[slot, row0[p]:row0[p] + 1, :] = val + pltpu.roll(val, 64, 1)
            continue
        for k, width in _small_pieces(n):
            pack[slot, row0[p] + k:row0[p] + k + 1, 0:width] = srcs[p][:, LANES * k:LANES * k + width]


def _small_allreduce(grads, loss_tile, sizes):
    n_par = len(sizes)
    row0, loss_row, rows = _small_layout(sizes)

    def body(*refs):
        g_refs, loss_in, out_ref = refs[:n_par], refs[n_par], refs[n_par + 1]
        pack, gath, send_sems, recv_sems = refs[n_par + 2:]
        me = _my_coords()
        my_idx = _dev_index(*me)
        _small_fill(pack, 0, g_refs, sizes, row0, rows)
        pack[0, loss_row:loss_row + 1, :] = loss_in[0:1, :]
        gath[my_idx] = pack[0]
        sends = []
        for k, f in enumerate(_FLIPS):
            peer = _flip(me, f)
            cp = pltpu.make_async_remote_copy(
                src_ref=pack.at[0], dst_ref=gath.at[my_idx],
                send_sem=send_sems.at[k], recv_sem=recv_sems.at[k],
                device_id=peer, device_id_type=pl.DeviceIdType.MESH)
            cp.start()
            sends.append(cp)
        for k, f in enumerate(_FLIPS):
            peer = _flip(me, f)
            slot = gath.at[_dev_index(*peer)]
            pltpu.make_async_remote_copy(
                src_ref=slot, dst_ref=slot, send_sem=send_sems.at[k], recv_sem=recv_sems.at[k],
                device_id=peer, device_id_type=pl.DeviceIdType.MESH).wait_recv()
        for cp in sends:
            cp.wait_send()
        g = gath[0]
        for d in range(1, N_DEV):
            g = g + gath[d]
        out_ref[...] = g

    vm = pl.BlockSpec(memory_space=pltpu.VMEM)
    return _pcall(
        body, name="small_allreduce", out_shape=_sds((rows, LANES), jnp.float32),
        in_specs=[vm] * (n_par + 1), out_specs=vm,
        scratch=[pltpu.VMEM((1, rows, LANES), jnp.float32), pltpu.VMEM((N_DEV, rows, LANES), jnp.float32),
                 pltpu.SemaphoreType.DMA((7,)), pltpu.SemaphoreType.DMA((7,))])(*grads, loss_tile)


def _small_adam(packed_g, ws, ms, vs):
    sizes = [w.shape[-1] for w in ws]
    n_par = len(ws)
    row0, loss_row, rows = _small_layout(sizes)

    def body(*refs):
        g_ref = refs[0]
        w_refs, m_refs, v_refs = (refs[1 + k * n_par: 1 + (k + 1) * n_par] for k in range(3))
        loss_out = refs[3 * n_par + 1]
        out_refs = refs[3 * n_par + 2: 7 * n_par + 2]
        pack, res = refs[7 * n_par + 2:]
        for slot, srcs in enumerate((w_refs, m_refs, v_refs)):
            _small_fill(pack, slot, srcs, sizes, row0, rows)
        g = g_ref[...]
        delta, m, v = _adam_math(pack[0], g, pack[1], pack[2])
        res[0], res[1], res[2], res[3] = g, delta, m, v
        loss_out[...] = res[0, loss_row:loss_row + 1, 0:1]
        for p, n in enumerate(sizes):
            for kind in range(4):
                for k, width in _small_pieces(n):
                    out_refs[4 * p + kind][:, LANES * k:LANES * k + width] = (
                        res[kind, row0[p] + k:row0[p] + k + 1, 0:width])

    vm = pl.BlockSpec(memory_space=pltpu.VMEM)
    out_shape = [_sds((1, 1), jnp.float32)]
    for n in sizes:
        out_shape += [_sds((1, n), jnp.float32)] * 4
    outs = _pcall(
        body, name="small_adam", out_shape=out_shape,
        in_specs=[vm] * (3 * n_par + 1), out_specs=[vm] * len(out_shape),
        scratch=[pltpu.VMEM((3, rows, LANES), jnp.float32), pltpu.VMEM((4, rows, LANES), jnp.float32)])(
            packed_g, *ws, *ms, *vs)
    return outs[0], [outs[1 + 4 * p: 5 + 4 * p] for p in range(n_par)]


def _adam_big(recv, w, m, v, name, after=None, which=None):
    rows, cols = recv.shape[-2:]
    row_tiles = [t for t in range(16, rows + 1, 16) if rows % t == 0 and t * cols <= 400 * 1024]
    tr, tc = (max(row_tiles), cols) if row_tiles else (rows, 512 if cols % 512 == 0 else cols)

    def body(r_ref, w_ref, m_ref, v_ref, g_ref, d_ref, mo_ref, vo_ref):
        g = r_ref[0].astype(jnp.float32)
        for d in range(1, N_DEV):
            g = g + r_ref[d].astype(jnp.float32)
        delta, mn, vn = _adam_math(w_ref[...], g, m_ref[...], v_ref[...])
        g_ref[...] = g
        d_ref[...] = delta
        mo_ref[...] = mn
        vo_ref[...] = vn

    blk = pl.BlockSpec((tr, tc), lambda i, j: (i, j))
    if which is None:
        r_spec = pl.BlockSpec((N_DEV, tr, tc), lambda i, j: (0, i, j))
    else:
        r_spec = pl.BlockSpec((N_DEV, None, tr, tc), lambda i, j: (0, which, i, j))
    return _pcall(
        body, name=name, grid=(rows // tr, cols // tc),
        out_shape=[_sds((rows, cols), jnp.float32)] * 4,
        in_specs=[r_spec, blk, blk, blk],
        out_specs=[blk] * 4, sem=("parallel", "parallel"), after=after)(recv, w, m, v)


def _mm(a, b, *, ta=False, tb=False, out_dtype, tm, tk, name):
    (kdim, mdim) = a.shape if ta else a.shape[::-1]
    ndim = b.shape[0] if tb else b.shape[1]
    tm, tk = min(tm, mdim), min(tk, kdim)
    nk = kdim // tk

    def body(a_ref, b_ref, o_ref, acc):
        k = pl.program_id(1)
        if ta:
            part = _dot_tn(a_ref[...], b_ref[...])
        elif tb:
            part = _dot_nt(a_ref[...], b_ref[...])
        else:
            part = _dot(a_ref[...], b_ref[...])

        @pl.when(k == 0)
        def _():
            acc[...] = part

        @pl.when(k > 0)
        def _():
            acc[...] += part

        @pl.when(k == nk - 1)
        def _():
            o_ref[...] = acc[...].astype(o_ref.dtype)

    a_spec = pl.BlockSpec((tk, tm), lambda i, k: (k, i)) if ta else pl.BlockSpec((tm, tk), lambda i, k: (i, k))
    b_spec = pl.BlockSpec((ndim, tk), lambda i, k: (0, k)) if tb else pl.BlockSpec((tk, ndim), lambda i, k: (k, 0))
    return _pcall(
        body, name=name, grid=(mdim // tm, nk), out_shape=_sds((mdim, ndim), out_dtype),
        in_specs=[a_spec, b_spec], out_specs=pl.BlockSpec((tm, ndim), lambda i, k: (i, 0)),
        scratch=[pltpu.VMEM((tm, ndim), jnp.float32)], sem=("parallel", "arbitrary"))(a, b)


def _ref_col_pieces(start, stop):
    ref_starts = [0, 1024, 1152, 1280, 1792, 2304, 2368, 2880]
    perm_starts = [C_QA, C_KA, C_VA, C_CQ, C_CKV, C_KR, C_QM]
    out = []
    for p in range(7):
        lo, hi = max(start, ref_starts[p]), min(stop, ref_starts[p + 1])
        if lo < hi:
            out.append((lo - start, perm_starts[p] + lo - ref_starts[p], hi - lo))
    return out


def _dw_in(hn, d_proj, n_shard):
    s, d = hn.shape
    n = sum(p.shape[1] for p in d_proj)
    n_pc = len(d_proj)
    tm, tk = min(512, d), min(1024, s)
    nk = s // tk

    def body(a_ref, *refs):
        b_refs, (o_ref, acc) = refs[:n_pc], refs[n_pc:]
        k = pl.program_id(1)
        part = _dot_tn(a_ref[...], jnp.concatenate([r[...] for r in b_refs], axis=1))

        @pl.when(k == 0)
        def _():
            acc[...] = part

        @pl.when(k > 0)
        def _():
            acc[...] += part

        @pl.when(k == nk - 1)
        def _():
            t = acc[...].T
            for j in range(N_DEV):
                rows = [t[src:src + width] for _, src, width in _ref_col_pieces(j * n_shard, (j + 1) * n_shard)]
                o_ref[j] = jnp.concatenate(rows, axis=0).astype(o_ref.dtype)

    return _pcall(
        body, name="dw_in", grid=(d // tm, nk), out_shape=_sds((N_DEV, n_shard, d), WIRE),
        in_specs=[pl.BlockSpec((tk, tm), lambda i, k: (k, i))]
        + [pl.BlockSpec((tk, p.shape[1]), lambda i, k: (k, 0)) for p in d_proj],
        out_specs=pl.BlockSpec((N_DEV, n_shard, tm), lambda i, k: (0, 0, i)),
        scratch=[pltpu.VMEM((tm, n), jnp.float32)], sem=("parallel", "arbitrary"))(hn, *d_proj)


def _in_proj(x, g, w):
    s, d = x.shape
    n = w.shape[0]
    tm = min(2 * ROW_TILE, s)

    def body(x_ref, g_ref, w_ref, p_ref, hn_ref):
        hn, _, _ = _norm_fwd(x_ref[...], g_ref[...])
        hn_ref[...] = hn.astype(hn_ref.dtype)
        p_ref[...] = _dot_nt(hn, w_ref[...])

    return _pcall(
        body, name="in_proj", grid=(s // tm,),
        out_shape=[_sds((s, n), jnp.float32), _sds((s, d), MXU)],
        in_specs=[pl.BlockSpec((tm, d), lambda i: (i, 0)), pl.BlockSpec((1, d), lambda i: (0, 0)),
                  pl.BlockSpec((n, d), lambda i: (0, 0), pipeline_mode=pl.Buffered(1))],
        out_specs=[pl.BlockSpec((tm, n), lambda i: (i, 0)), pl.BlockSpec((tm, d), lambda i: (i, 0))],
        sem=("parallel",))(x, g, w)


def _mla_prep(proj, cos, sin, g_cq, g_ckv, w_uq, w_ukv, g_qn, g_qr, g_kn, g_kr):
    s = proj.shape[0]
    tm = min(ROW_TILE, s)
    nh = MLA_HEADS

    def body(cq_ref, ckv_ref, kr_ref, cos_ref, sin_ref, gcq_ref, gckv_ref, wuq_ref, wukv_ref,
             gqn_ref, gqr_ref, gkn_ref, gkr_ref,
             qc_ref, kc_ref, v_ref, qb_ref, kvb_ref, cqn_ref, ckvn_ref):
        cos_t, sin_t = cos_ref[...], sin_ref[...]
        lo = _lo_mask((tm, LANES))
        cqn, _, _ = _norm_fwd(cq_ref[...], gcq_ref[...])
        cqn_ref[...] = cqn.astype(cqn_ref.dtype)
        qb = _dot_nt(cqn, wuq_ref[...])
        qb_ref[...] = qb
        ckvn, _, _ = _norm_fwd(ckv_ref[...], gckv_ref[...])
        ckvn_ref[...] = ckvn.astype(ckvn_ref.dtype)
        w_ukv_full = jnp.concatenate([wukv_ref[dev] for dev in range(N_DEV)], axis=1)
        kvb = _dot(ckvn, w_ukv_full)
        kvb_ref[...] = kvb
        kr, _, _ = _norm_fwd(kr_ref[...], gkr_ref[...], half=True)
        kr = _rope(kr, cos_t, sin_t)
        kr2 = jnp.where(lo, kr, pltpu.roll(kr, 64, 1))
        ropes = []
        for j in range(nh // 2):
            xr = qb[:, nh * MLA_NOPE + LANES * j: nh * MLA_NOPE + LANES * (j + 1)]
            qr, _, _ = _norm_fwd(xr, gqr_ref[...], half=True)
            ropes.append(_rope(qr, cos_t, sin_t))
        for h in range(nh):
            qn, _, _ = _norm_fwd(qb[:, MLA_NOPE * h: MLA_NOPE * (h + 1)], gqn_ref[...])
            mask = lo if h % 2 == 0 else jnp.logical_not(lo)
            qr = jnp.where(mask, ropes[h // 2], 0.0)
            qc_ref[h] = jnp.concatenate([qn, qr], axis=1).astype(qc_ref.dtype)
            kn, _, _ = _norm_fwd(kvb[:, 256 * h: 256 * h + MLA_NOPE], gkn_ref[...])
            kc_ref[h] = jnp.concatenate([kn, kr2], axis=1).astype(kc_ref.dtype)
            v_ref[h] = kvb[:, 256 * h + MLA_NOPE: 256 * (h + 1)].astype(v_ref.dtype)

    def col(width, start):
        return pl.BlockSpec((tm, width), lambda i: (i, start // width))

    def full(shape):
        return pl.BlockSpec(shape, lambda i: (0,) * len(shape))

    def row(width):
        return pl.BlockSpec((tm, width), lambda i: (i, 0))

    def heads(width):
        return pl.BlockSpec((nh, tm, width), lambda i: (0, i, 0))

    return _pcall(
        body, name="mla_prep", grid=(s // tm,),
        out_shape=[_sds((nh, s, 256), MXU), _sds((nh, s, 256), MXU), _sds((nh, s, MLA_V), MXU),
                   _sds((s, 768), jnp.float32), _sds((s, 1024), jnp.float32),
                   _sds((s, 512), MXU), _sds((s, 512), MXU)],
        in_specs=[col(512, C_CQ), col(512, C_CKV), col(LANES, C_KR), row(LANES), row(LANES),
                  full((1, 512)), full((1, 512)), full((768, 512)), full((N_DEV, 512, LANES)),
                  full((1, LANES)), full((1, LANES)), full((1, LANES)), full((1, LANES))],
        out_specs=[heads(256), heads(256), heads(MLA_V), row(768), row(1024), row(512), row(512)],
        sem=("parallel",))(proj, proj, proj, cos, sin, g_cq, g_ckv, w_uq, w_ukv, g_qn, g_qr, g_kn, g_kr)


def _mla_fwd(qc, kc, v):
    nh, s, _ = qc.shape
    t = min(ATT_TILE, s)
    nb = s // t
    scale = (MLA_NOPE + MLA_ROPE) ** -0.5

    def body(q_ref, k_ref, v_ref, y_ref, lse_ref, m_sc, l_sc, acc):
        qi, ki = pl.program_id(1), pl.program_id(2)

        @pl.when(ki == 0)
        def _():
            m_sc[...] = jnp.full_like(m_sc, NEG_INF)
            l_sc[...] = jnp.zeros_like(l_sc)
            acc[...] = jnp.zeros_like(acc)

        def step(diagonal):
            rc = t // 4 if diagonal else t
            for c in range(t // rc):
                rows = slice(rc * c, rc * (c + 1))
                keys = slice(0, rc * (c + 1))
                sc = _dot_nt(q_ref[0, rows, :], k_ref[0, keys, :]) * (scale * LOG2E)
                if diagonal:
                    r_i = lax.broadcasted_iota(jnp.int32, sc.shape, 0) + rc * c
                    c_i = lax.broadcasted_iota(jnp.int32, sc.shape, 1)
                    sc = jnp.where(c_i <= r_i, sc, NEG_INF)
                m_old = m_sc[rows, :]
                m_new = jnp.maximum(m_old, jnp.max(sc, -1, keepdims=True))
                alpha = jnp.exp2(m_old - m_new)
                p = jnp.exp2(sc - m_new)
                l_sc[rows, :] = alpha * l_sc[rows, :] + jnp.sum(p, -1, keepdims=True)
                acc[rows, :] = alpha * acc[rows, :] + _dot(p, v_ref[0, keys, :])
                m_sc[rows, :] = m_new

        @pl.when(ki < qi)
        def _():
            step(False)

        @pl.when(ki == qi)
        def _():
            step(True)

        @pl.when(ki == qi)
        def _():
            y_ref[...] = acc[...] / l_sc[...]
            lse_ref[0] = m_sc[...] + jnp.log2(l_sc[...])

    return _pcall(
        body, name="mla_fwd", grid=(nh, nb, nb),
        out_shape=[_sds((s, nh * MLA_V), jnp.float32), _sds((nh, s, 1), jnp.float32)],
        in_specs=[pl.BlockSpec((1, t, 256), lambda h, i, k: (h, i, 0)),
                  pl.BlockSpec((1, t, 256), lambda h, i, k: (h, jnp.minimum(k, i), 0)),
                  pl.BlockSpec((1, t, MLA_V), lambda h, i, k: (h, jnp.minimum(k, i), 0))],
        out_specs=[pl.BlockSpec((t, MLA_V), lambda h, i, k: (i, h)),
                   pl.BlockSpec((1, t, 1), lambda h, i, k: (h, i, 0))],
        scratch=[pltpu.VMEM((t, 1), jnp.float32), pltpu.VMEM((t, 1), jnp.float32),
                 pltpu.VMEM((t, MLA_V), jnp.float32)],
        sem=("parallel", "parallel", "arbitrary"))(qc, kc, v)


def _memkv_prep(mem, g_mem, w_mkv, g_mk):
    ml, d = mem.shape
    hw = MEM_HEADS * MEM_DIM

    def body(mem_ref, g_ref, w_ref, gk_ref, k_ref, v_ref, kv_ref, mn_ref):
        mn, _, _ = _norm_fwd(mem_ref[...], g_ref[...])
        mn_ref[...] = mn.astype(mn_ref.dtype)
        kv = _dot(mn, w_ref[...])
        kv_ref[...] = kv
        for h in range(MEM_HEADS):
            kn, _, _ = _norm_fwd(kv[:, MEM_DIM * h: MEM_DIM * (h + 1)], gk_ref[...])
            k_ref[:, MEM_DIM * h: MEM_DIM * (h + 1)] = kn.astype(k_ref.dtype)
        v_ref[...] = kv[:, hw:].astype(v_ref.dtype)

    vm = pl.BlockSpec(memory_space=pltpu.VMEM)
    return _pcall(
        body, name="memkv_prep",
        out_shape=[_sds((ml, hw), MXU), _sds((ml, hw), MXU), _sds((ml, 2 * hw), jnp.float32), _sds((ml, d), MXU)],
        in_specs=[vm] * 4, out_specs=[vm] * 4)(mem, g_mem, w_mkv, g_mk)


def _mem_fwd(proj, g_mq, km, vmm):
    s = proj.shape[0]
    ml, hw = km.shape
    tm = min(FFN_TILE, s)
    scale = MEM_DIM ** -0.5

    def body(q_ref, g_ref, k_ref, v_ref, y_ref, lse_ref):
        col = lax.broadcasted_iota(jnp.int32, (tm, MEM_HEADS), 1)
        lse_t = jnp.zeros((tm, MEM_HEADS), jnp.float32)
        for h in range(MEM_HEADS):
            sl = slice(MEM_DIM * h, MEM_DIM * (h + 1))
            qn, _, _ = _norm_fwd(q_ref[:, sl], g_ref[...])
            sc = _dot_nt(qn, k_ref[:, sl]) * scale
            m = jnp.max(sc, -1, keepdims=True)
            p = jnp.exp(sc - m)
            l = jnp.sum(p, -1, keepdims=True)
            y_ref[:, sl] = _dot(p, v_ref[:, sl]) / l
            lse_t = jnp.where(col == h, m + jnp.log(l), lse_t)
        lse_ref[...] = lse_t

    return _pcall(
        body, name="mem_fwd", grid=(s // tm,),
        out_shape=[_sds((s, hw), jnp.float32), _sds((s, MEM_HEADS), jnp.float32)],
        in_specs=[pl.BlockSpec((tm, hw), lambda i: (i, C_QM // hw)), pl.BlockSpec((1, MEM_DIM), lambda i: (0, 0)),
                  pl.BlockSpec((ml, hw), lambda i: (0, 0)), pl.BlockSpec((ml, hw), lambda i: (0, 0))],
        out_specs=[pl.BlockSpec((tm, hw), lambda i: (i, 0)), pl.BlockSpec((tm, MEM_HEADS), lambda i: (i, 0))],
        sem=("parallel",))(proj, g_mq, km, vmm)


def _alibi_slope(h):
    return float(2.0 ** (-8.0 * (h + 1) / SWA_Q_HEADS))


def _swa_common(n, kp, kc, vp, vc, pq, pkp, pkc, gk):
    b = SWA_BLOCK
    k_raw = jnp.concatenate([kp, kc], axis=0)
    kn, kxn, kr = _norm_fwd(k_raw, gk, half=True)
    v = jnp.concatenate([vp, vc], axis=0)
    dist = jnp.abs(pq - jnp.concatenate([pkp, pkc], axis=1))
    r_i = lax.broadcasted_iota(jnp.int32, (b, 2 * b), 0)
    c_i = lax.broadcasted_iota(jnp.int32, (b, 2 * b), 1)
    valid = (c_i > r_i) & (c_i <= r_i + b) & (c_i >= jnp.where(n > 0, 0, b))
    bias = jnp.where(valid, -dist, NEG_INF)
    return kn, v, bias


def _swa_specs(s):
    b = SWA_BLOCK
    prev = lambda n: jnp.maximum(n - 1, 0)
    return [
        pl.BlockSpec((b, 1024), lambda n: (n, C_QA // 1024)),
        pl.BlockSpec((b, LANES), lambda n: (prev(n), C_KA // LANES)),
        pl.BlockSpec((b, LANES), lambda n: (n, C_KA // LANES)),
        pl.BlockSpec((b, LANES), lambda n: (prev(n), C_VA // LANES)),
        pl.BlockSpec((b, LANES), lambda n: (n, C_VA // LANES)),
        pl.BlockSpec((b, 1), lambda n: (n, 0)),
        pl.BlockSpec((1, b), lambda n: (0, prev(n))),
        pl.BlockSpec((1, b), lambda n: (0, n)),
        pl.BlockSpec((1, LANES), lambda n: (0, 0)),
        pl.BlockSpec((1, LANES), lambda n: (0, 0)),
        pl.BlockSpec(memory_space=pltpu.SMEM),
    ]


def _swa_fwd(proj, posc, posr, gq, gk, sinks):
    s = proj.shape[0]
    b = SWA_BLOCK
    scale = SWA_DIM ** -0.5

    def body(q_ref, kp_ref, kc_ref, vp_ref, vc_ref, pq_ref, pkp_ref, pkc_ref, gq_ref, gk_ref, sink_ref,
             y_ref, lse_ref):
        n = pl.program_id(0)
        kn, v, bias = _swa_common(n, kp_ref[...], kc_ref[...], vp_ref[...], vc_ref[...],
                                  pq_ref[...], pkp_ref[...], pkc_ref[...], gk_ref[...])
        lo = _lo_mask((b, LANES))
        col = lax.broadcasted_iota(jnp.int32, (b, SWA_Q_HEADS), 1)
        lse_t = jnp.zeros((b, SWA_Q_HEADS), jnp.float32)
        hpg = SWA_Q_HEADS // SWA_KV_HEADS
        for g in range(SWA_KV_HEADS):
            heads = range(hpg * g, hpg * (g + 1))
            kvmask = lo if g == 0 else jnp.logical_not(lo)
            qs = []
            for j in range(hpg // 2 * g, hpg // 2 * (g + 1)):
                qn, _, _ = _norm_fwd(q_ref[:, LANES * j: LANES * (j + 1)], gq_ref[...], half=True)
                qn = qn * scale
                qsw = pltpu.roll(qn, 64, 1)
                qs += [jnp.where(kvmask, qn if e == g else qsw, 0.0) for e in range(2)]
            sc_st = _dot_nt(jnp.concatenate(qs, axis=0), kn)
            ps, ls = [], []
            for i, h in enumerate(heads):
                sc = sc_st[b * i: b * (i + 1)] + _alibi_slope(h) * bias
                sk = sink_ref[h]
                m = jnp.maximum(jnp.max(sc, -1, keepdims=True), sk)
                p = jnp.exp(sc - m)
                l = jnp.sum(p, -1, keepdims=True) + jnp.exp(sk - m)
                ps.append(p.astype(MXU))
                ls.append(l)
                lse_t = jnp.where(col == h, m + jnp.log(l), lse_t)
            o_st = _dot(jnp.concatenate(ps, axis=0), v)
            for j in range(hpg // 2 * g, hpg // 2 * (g + 1)):
                halves = []
                for e in range(2):
                    i = 2 * j + e - hpg * g
                    o_h = o_st[b * i: b * (i + 1)] / ls[i]
                    halves.append(o_h if e == g else pltpu.roll(o_h, 64, 1))
                y_ref[:, LANES * j: LANES * (j + 1)] = jnp.where(lo, halves[0], halves[1])
        lse_ref[...] = lse_t

    return _pcall(
        body, name="swa_fwd", grid=(s // b,),
        out_shape=[_sds((s, 1024), jnp.float32), _sds((s, SWA_Q_HEADS), jnp.float32)],
        in_specs=_swa_specs(s),
        out_specs=[pl.BlockSpec((b, 1024), lambda n: (n, 0)), pl.BlockSpec((b, SWA_Q_HEADS), lambda n: (n, 0))],
        sem=("parallel",))(proj, proj, proj, proj, proj, posc, posr, posr, gq, gk, sinks)


def _out_proj(y_a, y_b, y_m, x, w_out, g_ffn):
    s, d = x.shape
    tm = min(2 * ROW_TILE, s)

    def body(ya_ref, yb_ref, ym_ref, x_ref, w_ref, g_ref, h1_ref, fn_ref):
        y = jnp.concatenate([ya_ref[...].astype(MXU), yb_ref[...].astype(MXU), ym_ref[...].astype(MXU)], axis=1)
        h1 = x_ref[...] + _dot(y, w_ref[...])
        h1_ref[...] = h1
        fn, _, _ = _norm_fwd(h1, g_ref[...])
        fn_ref[...] = fn.astype(fn_ref.dtype)

    def row(width):
        return pl.BlockSpec((tm, width), lambda i: (i, 0))

    return _pcall(
        body, name="out_proj", grid=(s // tm,),
        out_shape=[_sds((s, d), jnp.float32), _sds((s, d), MXU)],
        in_specs=[row(1024), row(512), row(512), row(d),
                  pl.BlockSpec(w_out.shape, lambda i: (0, 0), pipeline_mode=pl.Buffered(1)),
                  pl.BlockSpec((1, d), lambda i: (0, 0))],
        out_specs=[row(d), row(d)], sem=("parallel",))(y_a, y_b, y_m, x, w_out, g_ffn)


def _ffn_gu(fn, w_gu):
    s, d = fn.shape
    f = w_gu.shape[2]
    tm = min(2 * FFN_TILE, s)

    def body(fn_ref, w_ref, gu_ref, act_ref):
        x = fn_ref[...]
        g = _dot_nt(x, w_ref[0, 0])
        u = _dot_nt(x, w_ref[0, 1])
        gu_ref[0, 0] = g
        gu_ref[0, 1] = u
        act_ref[0] = (g * jax.nn.sigmoid(g) * u).astype(act_ref.dtype)

    return _pcall(
        body, name="ffn_gate_up", grid=(N_DEV, s // tm),
        out_shape=[_sds((N_DEV, 2, s, f), jnp.float32), _sds((N_DEV, s, f), MXU)],
        in_specs=[pl.BlockSpec((tm, d), lambda j, i: (i, 0)),
                  pl.BlockSpec((1, 2, f, d), lambda j, i: (j, 0, 0, 0))],
        out_specs=[pl.BlockSpec((1, 2, tm, f), lambda j, i: (j, 0, i, 0)),
                   pl.BlockSpec((1, tm, f), lambda j, i: (j, i, 0))],
        sem=("parallel", "parallel"))(fn, w_gu)


def _ffn_down(act, w_d, h1, target):
    _, s, f = act.shape
    d = h1.shape[1]
    tm = min(FFN_TILE, s)

    def body(a_ref, w_ref, h1_ref, t_ref, dout_ref, doutb_ref, loss_ref, acc):
        i, j = pl.program_id(0), pl.program_id(1)
        part = _dot(a_ref[0], w_ref[0]) + _dot(a_ref[1], w_ref[1])

        @pl.when(j == 0)
        def _():
            acc[...] = h1_ref[...] + part

        @pl.when(j > 0)
        def _():
            acc[...] += part

        @pl.when((i == 0) & (j == 0))
        def _():
            loss_ref[...] = jnp.zeros_like(loss_ref)

        @pl.when(j == N_DEV // 2 - 1)
        def _():
            diff = acc[...] - t_ref[...]
            dout_ref[...] = diff / d
            doutb_ref[...] = (diff / d).astype(doutb_ref.dtype)
            loss_ref[...] += 0.5 * jnp.sum(jnp.sum(diff * diff, -1, keepdims=True) / d)

    row = pl.BlockSpec((tm, d), lambda i, j: (i, 0))
    return _pcall(
        body, name="ffn_down", grid=(s // tm, N_DEV // 2),
        out_shape=[_sds((s, d), jnp.float32), _sds((s, d), MXU), _sds((8, LANES), jnp.float32)],
        in_specs=[pl.BlockSpec((2, tm, f), lambda i, j: (j, i, 0)), pl.BlockSpec((2, f, d), lambda i, j: (j, 0, 0)),
                  row, row],
        out_specs=[row, row, pl.BlockSpec((8, LANES), lambda i, j: (0, 0))],
        scratch=[pltpu.VMEM((tm, d), jnp.float32)], sem=("arbitrary", "arbitrary"))(act, w_d, h1, target)


def _ffn_bwd_act(dout, w_d, gu):
    s, d = dout.shape
    f = w_d.shape[1]
    tm = min(2 * FFN_TILE, s)
    ni = s // tm

    def body(do_ref, w_ref, gu_ref, dgu_ref, dw_ref, acc):
        i = pl.program_id(1)
        do = do_ref[...]
        d_act = _dot_nt(do, w_ref[0])
        g, u = gu_ref[0, 0], gu_ref[0, 1]
        sig = jax.nn.sigmoid(g)
        silu = g * sig
        dgu_ref[0, 0] = (d_act * u * (sig * (1.0 + g * (1.0 - sig)))).astype(dgu_ref.dtype)
        dgu_ref[0, 1] = (d_act * silu).astype(dgu_ref.dtype)
        part = _dot_tn(silu * u, do)

        @pl.when(i == 0)
        def _():
            acc[...] = part

        @pl.when(i > 0)
        def _():
            acc[...] += part

        @pl.when(i == ni - 1)
        def _():
            dw_ref[0] = acc[...].astype(dw_ref.dtype)

    return _pcall(
        body, name="ffn_bwd_act", grid=(N_DEV, ni),
        out_shape=[_sds((N_DEV, 2, s, f), MXU), _sds((N_DEV, f, d), WIRE)],
        in_specs=[pl.BlockSpec((tm, d), lambda j, i: (i, 0)), pl.BlockSpec((1, f, d), lambda j, i: (j, 0, 0)),
                  pl.BlockSpec((1, 2, tm, f), lambda j, i: (j, 0, i, 0))],
        out_specs=[pl.BlockSpec((1, 2, tm, f), lambda j, i: (j, 0, i, 0)),
                   pl.BlockSpec((1, f, d), lambda j, i: (j, 0, 0))],
        scratch=[pltpu.VMEM((f, d), jnp.float32)], sem=("parallel", "arbitrary"))(dout, w_d, gu)


def _ffn_dw_gu(fn, dgu):
    s, d = fn.shape
    f = dgu.shape[-1]
    tk = min(4 * FFN_TILE, s)
    nk = s // tk

    def body(fn_ref, dgu_ref, dw_ref, acc):
        k = pl.program_id(2)
        part = _dot_tn(dgu_ref[0, 0], fn_ref[...])

        @pl.when(k == 0)
        def _():
            acc[...] = part

        @pl.when(k > 0)
        def _():
            acc[...] += part

        @pl.when(k == nk - 1)
        def _():
            dw_ref[0, 0] = acc[...].astype(dw_ref.dtype)

    return _pcall(
        body, name="ffn_dw_gate_up", grid=(N_DEV, 2, nk),
        out_shape=_sds((N_DEV, 2, f, d), WIRE),
        in_specs=[pl.BlockSpec((tk, d), lambda j, w, k: (k, 0)),
                  pl.BlockSpec((1, 1, tk, f), lambda j, w, k: (j, w, k, 0))],
        out_specs=pl.BlockSpec((1, 1, f, d), lambda j, w, k: (j, w, 0, 0)),
        scratch=[pltpu.VMEM((f, d), jnp.float32)], sem=("parallel", "parallel", "arbitrary"))(fn, dgu)


def _ffn_dfn(dgu, w_gu, after):
    _, _, s, f = dgu.shape
    d = w_gu.shape[3]
    tm = min(FFN_TILE, s)

    def body(dgu_ref, w_ref, dfn_ref):
        j = pl.program_id(1)
        part = (_dot(dgu_ref[0, 0], w_ref[0, 0]) + _dot(dgu_ref[0, 1], w_ref[0, 1])
                + _dot(dgu_ref[1, 0], w_ref[1, 0]) + _dot(dgu_ref[1, 1], w_ref[1, 1]))

        @pl.when(j == 0)
        def _():
            dfn_ref[...] = part

        @pl.when(j > 0)
        def _():
            dfn_ref[...] += part

    return _pcall(
        body, name="ffn_dfn", grid=(s // tm, N_DEV // 2),
        out_shape=_sds((s, d), jnp.float32),
        in_specs=[pl.BlockSpec((2, 2, tm, f), lambda i, j: (j, 0, i, 0)),
                  pl.BlockSpec((2, 2, f, d), lambda i, j: (j, 0, 0, 0))],
        out_specs=pl.BlockSpec((tm, d), lambda i, j: (i, 0)),
        sem=("parallel", "arbitrary"), after=after)(dgu, w_gu)


def _ffn_norm_bwd(d_fn, dout, h1, g_ffn):
    s, d = h1.shape
    tm = min(2 * ROW_TILE, s)

    def body(dfn_ref, do_ref, h1_ref, g_ref, dh1_ref, dg_ref):
        i = pl.program_id(0)

        @pl.when(i == 0)
        def _():
            dg_ref[...] = jnp.zeros_like(dg_ref)

        _, xn, r = _norm_fwd(h1_ref[...], g_ref[...])
        dx, dg = _norm_bwd(xn, r, g_ref[...], dfn_ref[...])
        dh1_ref[...] = do_ref[...] + dx
        dg_ref[...] += dg

    row = pl.BlockSpec((tm, d), lambda i: (i, 0))
    vec = pl.BlockSpec((1, d), lambda i: (0, 0))
    return _pcall(
        body, name="ffn_norm_bwd", grid=(s // tm,),
        out_shape=[_sds((s, d), jnp.float32), _sds((1, d), jnp.float32)],
        in_specs=[row, row, row, vec], out_specs=[row, vec], sem=("arbitrary",))(d_fn, dout, h1, g_ffn)


def _mem_bwd(proj, g_mq, km, vmm, d_y, y_m, lse):
    s = proj.shape[0]
    ml, hw = km.shape
    tm = min(FFN_TILE, s)
    scale = MEM_DIM ** -0.5

    def body(q_ref, g_ref, k_ref, v_ref, do_ref, y_ref, lse_ref, dq_ref, dk_ref, dv_ref, dg_ref):
        i = pl.program_id(0)

        @pl.when(i == 0)
        def _():
            dk_ref[...] = jnp.zeros_like(dk_ref)
            dv_ref[...] = jnp.zeros_like(dv_ref)
            dg_ref[...] = jnp.zeros_like(dg_ref)

        col = lax.broadcasted_iota(jnp.int32, (tm, MEM_HEADS), 1)
        lse_t = lse_ref[...]
        for h in range(MEM_HEADS):
            sl = slice(MEM_DIM * h, MEM_DIM * (h + 1))
            qn, xn, r = _norm_fwd(q_ref[:, sl], g_ref[...])
            lse_h = jnp.sum(jnp.where(col == h, lse_t, 0.0), -1, keepdims=True)
            p = jnp.exp(_dot_nt(qn, k_ref[:, sl]) * scale - lse_h)
            do = do_ref[:, sl]
            dd = jnp.sum(do * y_ref[:, sl], -1, keepdims=True)
            dp = _dot_nt(do, v_ref[:, sl])
            ds = (p * (dp - dd)).astype(MXU)
            dv_ref[:, sl] += _dot_tn(p, do)
            dk_ref[:, sl] += _dot_tn(ds, qn) * scale
            dx, dg = _norm_bwd(xn, r, g_ref[...], _dot(ds, k_ref[:, sl]) * scale)
            dq_ref[:, sl] = dx.astype(dq_ref.dtype)
            dg_ref[...] += dg

    full = pl.BlockSpec((ml, hw), lambda i: (0, 0))
    return _pcall(
        body, name="mem_bwd", grid=(s // tm,),
        out_shape=[_sds((s, hw), MXU), _sds((ml, hw), jnp.float32), _sds((ml, hw), jnp.float32),
                   _sds((1, MEM_DIM), jnp.float32)],
        in_specs=[pl.BlockSpec((tm, hw), lambda i: (i, C_QM // hw)), pl.BlockSpec((1, MEM_DIM), lambda i: (0, 0)),
                  full, full, pl.BlockSpec((tm, hw), lambda i: (i, 3)), pl.BlockSpec((tm, hw), lambda i: (i, 0)),
                  pl.BlockSpec((tm, MEM_HEADS), lambda i: (i, 0))],
        out_specs=[pl.BlockSpec((tm, hw), lambda i: (i, 0)), full, full,
                   pl.BlockSpec((1, MEM_DIM), lambda i: (0, 0))],
        sem=("arbitrary",))(proj, g_mq, km, vmm, d_y, y_m, lse)


def _memkv_bwd(mem, g_mem, w_mkv, g_mk, kv, memn, dk, dv):
    ml, d = mem.shape
    hw = MEM_HEADS * MEM_DIM

    def body(mem_ref, g_ref, w_ref, gk_ref, kv_ref, mn_ref, dk_ref, dv_ref, dw_ref, dgm_ref, dgk_ref):
        parts = []
        dgk = jnp.zeros((1, MEM_DIM), jnp.float32)
        for h in range(MEM_HEADS):
            sl = slice(MEM_DIM * h, MEM_DIM * (h + 1))
            _, xn, r = _norm_fwd(kv_ref[:, sl], gk_ref[...])
            dx, dg = _norm_bwd(xn, r, gk_ref[...], dk_ref[:, sl])
            parts.append(dx)
            dgk = dgk + dg
        dkv = jnp.concatenate(parts + [dv_ref[...]], axis=1).astype(MXU)
        dgk_ref[...] = dgk
        dw_ref[...] = _dot_tn(mn_ref[...], dkv).astype(dw_ref.dtype)
        d_mn = _dot_nt(dkv, w_ref[...])
        _, xn, _ = _norm_fwd(mem_ref[...], g_ref[...])
        dgm_ref[...] = jnp.sum(d_mn * xn, 0, keepdims=True)

    vm = pl.BlockSpec(memory_space=pltpu.VMEM)
    return _pcall(
        body, name="memkv_bwd",
        out_shape=[_sds((d, 2 * hw), WIRE), _sds((1, d), jnp.float32), _sds((1, MEM_DIM), jnp.float32)],
        in_specs=[vm] * 8, out_specs=[vm] * 3)(mem, g_mem, w_mkv, g_mk, kv, memn, dk, dv)


def _mla_bwd(qc, kc, v, d_y, y_b, lse, after):
    nh, s, _ = qc.shape
    t = min(ATT_TILE, s)
    nb = s // t
    scale = (MLA_NOPE + MLA_ROPE) ** -0.5

    def body(q_ref, k_ref, v_ref, do_ref, y_ref, lse_ref, dq_ref, dk_ref, dv_ref, dk_acc, dv_acc):
        kj, qi = pl.program_id(1), pl.program_id(2)

        @pl.when((kj == 0) & (qi == 0))
        def _():
            dq_ref[...] = jnp.zeros_like(dq_ref)

        @pl.when(qi == kj)
        def _():
            dk_acc[...] = jnp.zeros_like(dk_acc)
            dv_acc[...] = jnp.zeros_like(dv_acc)

        def step(diagonal):
            rc = t // 4 if diagonal else t
            for c in range(t // rc):
                rows = slice(rc * c, rc * (c + 1))
                keys = slice(0, rc * (c + 1))
                q, k = q_ref[0, rows, :], k_ref[0, keys, :]
                sc = _dot_nt(q, k) * (scale * LOG2E)
                if diagonal:
                    r_i = lax.broadcasted_iota(jnp.int32, sc.shape, 0) + rc * c
                    c_i = lax.broadcasted_iota(jnp.int32, sc.shape, 1)
                    sc = jnp.where(c_i <= r_i, sc, NEG_INF)
                p = jnp.exp2(sc - lse_ref[0, rows, :])
                do = do_ref[rows, :]
                dd = jnp.sum(do * y_ref[rows, :], -1, keepdims=True)
                dp = _dot_nt(do, v_ref[0, keys, :])
                ds = (p * (dp - dd) * scale).astype(MXU)
                dv_acc[keys, :] += _dot_tn(p, do)
                dk_acc[keys, :] += _dot_tn(ds, q)
                out_rows = pl.ds(pl.multiple_of(qi * t + rc * c, rc), rc)
                dq_ref[0, out_rows, :] += _dot(ds, k)

        @pl.when(qi > kj)
        def _():
            step(False)

        @pl.when(qi == kj)
        def _():
            step(True)

        @pl.when(qi == nb - 1)
        def _():
            dk_ref[0] = dk_acc[...]
            dv_ref[0] = dv_acc[...]

    qmap = lambda h, j, i: (h, jnp.maximum(i, j), 0)
    return _pcall(
        body, name="mla_bwd", grid=(nh, nb, nb),
        out_shape=[_sds((nh, s, 256), jnp.float32), _sds((nh, s, 256), jnp.float32),
                   _sds((nh, s, MLA_V), jnp.float32)],
        in_specs=[pl.BlockSpec((1, t, 256), qmap),
                  pl.BlockSpec((1, t, 256), lambda h, j, i: (h, j, 0)),
                  pl.BlockSpec((1, t, MLA_V), lambda h, j, i: (h, j, 0)),
                  pl.BlockSpec((t, MLA_V), lambda h, j, i: (jnp.maximum(i, j), 8 + h)),
                  pl.BlockSpec((t, MLA_V), lambda h, j, i: (jnp.maximum(i, j), h)),
                  pl.BlockSpec((1, t, 1), qmap)],
        out_specs=[pl.BlockSpec((1, s, 256), lambda h, j, i: (h, 0, 0)),
                   pl.BlockSpec((1, t, 256), lambda h, j, i: (h, j, 0)),
                   pl.BlockSpec((1, t, MLA_V), lambda h, j, i: (h, j, 0))],
        scratch=[pltpu.VMEM((t, 256), jnp.float32), pltpu.VMEM((t, MLA_V), jnp.float32)],
        sem=("parallel", "arbitrary", "arbitrary"), after=after)(qc, kc, v, d_y, y_b, lse)


def _mla_prep_bwd(proj, cos, sin, g_cq, g_ckv, w_uq, w_ukv, g_qn, g_qr, g_kn, g_kr,
                  qb, kvb, cqn, ckvn, dqc, dkc, dv):
    s = proj.shape[0]
    tm = min(ROW_TILE, s)
    nh = MLA_HEADS
    ni = s // tm

    def body(cq_ref, ckv_ref, kr_ref, cos_ref, sin_ref, gcq_ref, gckv_ref, wuq_ref, wukv_ref,
             gqn_ref, gqr_ref, gkn_ref, gkr_ref, qb_ref, kvb_ref, cqn_ref, ckvn_ref, dqc_ref, dkc_ref, dv_ref,
             dcq_ref, dckv_ref, dkr_ref, dwuq_ref, dwukv_ref,
             dgcq_ref, dgckv_ref, dgqn_ref, dgqr_ref, dgkn_ref, dgkr_ref, acc_uq, acc_ukv):
        i = pl.program_id(0)

        @pl.when(i == 0)
        def _():
            acc_uq[...] = jnp.zeros_like(acc_uq)
            acc_ukv[...] = jnp.zeros_like(acc_ukv)
            for ref in (dgcq_ref, dgckv_ref, dgqn_ref, dgqr_ref, dgkn_ref, dgkr_ref):
                ref[...] = jnp.zeros_like(ref)

        cos_t, sin_t = cos_ref[...], sin_ref[...]
        lo = _lo_mask((tm, LANES))
        qb_v, kvb_v = qb_ref[...], kvb_ref[...]
        dq_parts, dgqn = [], jnp.zeros((1, LANES), jnp.float32)
        for h in range(nh):
            _, xn, r = _norm_fwd(qb_v[:, MLA_NOPE * h: MLA_NOPE * (h + 1)], gqn_ref[...])
            dx, dg = _norm_bwd(xn, r, gqn_ref[...], dqc_ref[h][:, :MLA_NOPE])
            dq_parts.append(dx)
            dgqn = dgqn + dg
        dgqn_ref[...] += dgqn
        dgqr = jnp.zeros((1, LANES), jnp.float32)
        for j in range(nh // 2):
            d_rope = jnp.where(lo, dqc_ref[2 * j][:, MLA_NOPE:], dqc_ref[2 * j + 1][:, MLA_NOPE:])
            d_pre = _rope_bwd(d_rope, cos_t, sin_t)
            xr = qb_v[:, nh * MLA_NOPE + LANES * j: nh * MLA_NOPE + LANES * (j + 1)]
            _, xn, r = _norm_fwd(xr, gqr_ref[...], half=True)
            dx, dg = _norm_bwd(xn, r, gqr_ref[...], d_pre, half=True)
            dq_parts.append(dx)
            dgqr = dgqr + dg
        dgqr_ref[...] += dgqr
        dqb = jnp.concatenate(dq_parts, axis=1).astype(MXU)
        acc_uq[...] += _dot_tn(dqb, cqn_ref[...])
        _, xn, r = _norm_fwd(cq_ref[...], gcq_ref[...])
        dx, dg = _norm_bwd(xn, r, gcq_ref[...], _dot(dqb, wuq_ref[...]))
        dcq_ref[...] = dx.astype(dcq_ref.dtype)
        dgcq_ref[...] += dg
        dkv_parts, dgkn = [], jnp.zeros((1, LANES), jnp.float32)
        d_kr2 = jnp.zeros((tm, LANES), jnp.float32)
        for h in range(nh):
            _, xn, r = _norm_fwd(kvb_v[:, 256 * h: 256 * h + MLA_NOPE], gkn_ref[...])
            dx, dg = _norm_bwd(xn, r, gkn_ref[...], dkc_ref[h][:, :MLA_NOPE])
            dkv_parts += [dx, dv_ref[h]]
            dgkn = dgkn + dg
            d_kr2 = d_kr2 + dkc_ref[h][:, MLA_NOPE:]
        dgkn_ref[...] += dgkn
        dkvb = jnp.concatenate(dkv_parts, axis=1).astype(MXU)
        part_ukv = _dot_tn(ckvn_ref[...], dkvb)
        for dev in range(N_DEV):
            acc_ukv[dev] += part_ukv[:, LANES * dev: LANES * (dev + 1)]
        w_ukv_full = jnp.concatenate([wukv_ref[dev] for dev in range(N_DEV)], axis=1)
        d_ckvn = _dot_nt(dkvb, w_ukv_full)
        _, xn, r = _norm_fwd(ckv_ref[...], gckv_ref[...])
        dx, dg = _norm_bwd(xn, r, gckv_ref[...], d_ckvn)
        dckv_ref[...] = dx.astype(dckv_ref.dtype)
        dgckv_ref[...] += dg
        d_kr = jnp.where(lo, d_kr2 + pltpu.roll(d_kr2, 64, 1), 0.0)
        d_pre = _rope_bwd(d_kr, cos_t, sin_t)
        _, xn, r = _norm_fwd(kr_ref[...], gkr_ref[...], half=True)
        dx, dg = _norm_bwd(xn, r, gkr_ref[...], d_pre, half=True)
        dkr_ref[...] = jnp.where(lo, dx, 0.0).astype(dkr_ref.dtype)
        dgkr_ref[...] += jnp.where(_lo_mask((1, LANES)), dg, 0.0)

        @pl.when(i == ni - 1)
        def _():
            dwuq_ref[...] = acc_uq[...].astype(dwuq_ref.dtype)
            dwukv_ref[...] = acc_ukv[...].astype(dwukv_ref.dtype)

    def col(width, start):
        return pl.BlockSpec((tm, width), lambda i: (i, start // width))

    def full(shape):
        return pl.BlockSpec(shape, lambda i: (0,) * len(shape))

    def row(width):
        return pl.BlockSpec((tm, width), lambda i: (i, 0))

    def heads(width):
        return pl.BlockSpec((nh, tm, width), lambda i: (0, i, 0))

    vec = full((1, LANES))
    return _pcall(
        body, name="mla_prep_bwd", grid=(ni,),
        out_shape=[_sds((s, 512), MXU), _sds((s, 512), MXU), _sds((s, LANES), MXU),
                   _sds((768, 512), WIRE), _sds((N_DEV, 512, LANES), WIRE),
                   _sds((1, 512), jnp.float32), _sds((1, 512), jnp.float32)] + [_sds((1, LANES), jnp.float32)] * 4,
        in_specs=[col(512, C_CQ), col(512, C_CKV), col(LANES, C_KR), row(LANES), row(LANES),
                  full((1, 512)), full((1, 512)), full((768, 512)), full((N_DEV, 512, LANES)), vec, vec, vec, vec,
                  row(768), row(1024), row(512), row(512), heads(256), heads(256), heads(MLA_V)],
        out_specs=[row(512), row(512), row(LANES), full((768, 512)), full((N_DEV, 512, LANES)),
                   full((1, 512)), full((1, 512)), vec, vec, vec, vec],
        scratch=[pltpu.VMEM((768, 512), jnp.float32), pltpu.VMEM((N_DEV, 512, LANES), jnp.float32)],
        sem=("arbitrary",))(proj, proj, proj, cos, sin, g_cq, g_ckv, w_uq, w_ukv, g_qn, g_qr, g_kn, g_kr,
                            qb, kvb, cqn, ckvn, dqc, dkc, dv)


def _swa_bwd(proj, posc, posr, gq, gk, sinks, d_y, y_a, lse, after):
    s = proj.shape[0]
    b = SWA_BLOCK
    nb = s // b
    scale = SWA_DIM ** -0.5

    def body(q_ref, kp_ref, kc_ref, vp_ref, vc_ref, pq_ref, pkp_ref, pkc_ref, gq_ref, gk_ref, sink_ref,
             do_ref, y_ref, lse_ref, kfull_ref,
             dq_ref, dk_ref, dv_ref, dgq_ref, dgk_ref, dsink_ref, dk_acc, dv_acc):
        n = pl.program_id(0)

        @pl.when(n == 0)
        def _():
            dk_acc[...] = jnp.zeros_like(dk_acc)
            dv_acc[...] = jnp.zeros_like(dv_acc)
            dgq_ref[...] = jnp.zeros_like(dgq_ref)
            dsink_ref[...] = jnp.zeros_like(dsink_ref)

        kn, v, bias = _swa_common(n, kp_ref[...], kc_ref[...], vp_ref[...], vc_ref[...],
                                  pq_ref[...], pkp_ref[...], pkc_ref[...], gk_ref[...])
        lo = _lo_mask((b, LANES))
        col = lax.broadcasted_iota(jnp.int32, (b, SWA_Q_HEADS), 1)
        col1 = lax.broadcasted_iota(jnp.int32, (1, SWA_Q_HEADS), 1)
        lse_t = lse_ref[...]
        dk_blk = jnp.zeros((2 * b, LANES), jnp.float32)
        dv_blk = jnp.zeros((2 * b, LANES), jnp.float32)
        dgq = jnp.zeros((1, LANES), jnp.float32)
        dsink = jnp.zeros((1, SWA_Q_HEADS), jnp.float32)
        for j in range(SWA_Q_HEADS // 2):
            hk = (2 * j) // (SWA_Q_HEADS // SWA_KV_HEADS)
            kvmask = lo if hk == 0 else jnp.logical_not(lo)
            sl = slice(LANES * j, LANES * (j + 1))
            qn, xn, r = _norm_fwd(q_ref[:, sl], gq_ref[...], half=True)
            qn = qn * scale
            qsw = pltpu.roll(qn, 64, 1)
            d2 = do_ref[:, sl]
            d2sw = pltpu.roll(d2, 64, 1)
            prod = d2 * y_ref[:, sl]
            dqs = []
            for e in range(2):
                h = 2 * j + e
                half_e = lo if e == 0 else jnp.logical_not(lo)
                qm = jnp.where(kvmask, qn if e == hk else qsw, 0.0)
                dm = jnp.where(kvmask, d2 if e == hk else d2sw, 0.0)
                sc = _dot_nt(qm, kn) + _alibi_slope(h) * bias
                lse_h = jnp.sum(jnp.where(col == h, lse_t, 0.0), -1, keepdims=True)
                p = jnp.exp(sc - lse_h)
                dd = jnp.sum(jnp.where(half_e, prod, 0.0), -1, keepdims=True)
                dp = _dot_nt(dm, v)
                ds = (p * (dp - dd)).astype(MXU)
                dsink = dsink - jnp.where(col1 == h, jnp.sum(jnp.exp(sink_ref[h] - lse_h) * dd), 0.0)
                dq_m = _dot(ds, kn) * scale
                dk_blk = dk_blk + _dot_tn(ds, qm)
                dv_blk = dv_blk + _dot_tn(p, dm)
                dqs.append(dq_m if e == hk else pltpu.roll(dq_m, 64, 1))
            dx, dg = _norm_bwd(xn, r, gq_ref[...], jnp.where(lo, dqs[0], dqs[1]), half=True)
            dq_ref[:, sl] = dx.astype(dq_ref.dtype)
            dgq = dgq + dg
        dgq_ref[...] += dgq
        dsink_ref[...] += dsink
        prev = pl.ds(pl.multiple_of(jnp.maximum(n - 1, 0) * b, b), b)
        cur = pl.ds(pl.multiple_of(n * b, b), b)
        dk_acc[prev, :] += dk_blk[:b]
        dv_acc[prev, :] += dv_blk[:b]
        dk_acc[cur, :] += dk_blk[b:]
        dv_acc[cur, :] += dv_blk[b:]

        @pl.when(n == nb - 1)
        def _():
            _, kxn, kr = _norm_fwd(kfull_ref[...], gk_ref[...], half=True)
            dx, dg = _norm_bwd(kxn, kr, gk_ref[...], dk_acc[...], half=True)
            dk_ref[...] = dx.astype(dk_ref.dtype)
            dv_ref[...] = dv_acc[...].astype(dv_ref.dtype)
            dgk_ref[...] = dg

    full = pl.BlockSpec((s, LANES), lambda n: (0, 0))
    vec = pl.BlockSpec((1, LANES), lambda n: (0, 0))
    return _pcall(
        body, name="swa_bwd", grid=(nb,),
        out_shape=[_sds((s, 1024), MXU), _sds((s, LANES), MXU), _sds((s, LANES), MXU),
                   _sds((1, LANES), jnp.float32), _sds((1, LANES), jnp.float32),
                   _sds((1, SWA_Q_HEADS), jnp.float32)],
        in_specs=_swa_specs(s) + [pl.BlockSpec((b, 1024), lambda n: (n, 0)), pl.BlockSpec((b, 1024), lambda n: (n, 0)),
                                  pl.BlockSpec((b, SWA_Q_HEADS), lambda n: (n, 0)),
                                  pl.BlockSpec((s, LANES), lambda n: (0, C_KA // LANES))],
        out_specs=[pl.BlockSpec((b, 1024), lambda n: (n, 0)), full, full, vec, vec,
                   pl.BlockSpec((1, SWA_Q_HEADS), lambda n: (0, 0))],
        scratch=[pltpu.VMEM((s, LANES), jnp.float32), pltpu.VMEM((s, LANES), jnp.float32)],
        sem=("arbitrary",), after=after)(proj, proj, proj, proj, proj, posc, posr, posr, gq, gk, sinks, d_y, y_a, lse,
                                         proj)


def _dx(d_proj, w_in, x, g, d_h1, after):
    s, d = x.shape
    n = w_in.shape[0]
    tm = min(2 * ROW_TILE, s)

    n_pc = len(d_proj)

    def body(*refs):
        dp_refs, (w_ref, x_ref, g_ref, dh_ref, dx_ref, dg_ref) = refs[:n_pc], refs[n_pc:]
        i = pl.program_id(0)

        @pl.when(i == 0)
        def _():
            dg_ref[...] = jnp.zeros_like(dg_ref)

        d_hn = _dot(jnp.concatenate([r[...] for r in dp_refs], axis=1), w_ref[...])
        _, xn, r = _norm_fwd(x_ref[...], g_ref[...])
        dx, dg = _norm_bwd(xn, r, g_ref[...], d_hn)
        dx_ref[...] = dh_ref[...] + dx
        dg_ref[...] += dg

    row = pl.BlockSpec((tm, d), lambda i: (i, 0))
    vec = pl.BlockSpec((1, d), lambda i: (0, 0))
    return _pcall(
        body, name="grad_x", grid=(s // tm,),
        out_shape=[_sds((s, d), jnp.float32), _sds((1, d), jnp.float32)],
        in_specs=[pl.BlockSpec((tm, p.shape[1]), lambda i: (i, 0)) for p in d_proj] + [
                  pl.BlockSpec((n, d), lambda i: (0, 0), pipeline_mode=pl.Buffered(1)), row, vec, row],
        out_specs=[row, vec], sem=("arbitrary",), after=after)(*d_proj, w_in, x, g, d_h1)


_SMALL = ["attn_norm_g", "swa_q_norm_g", "swa_k_norm_g", "swa_sinks", "mla_cq_norm_g", "mla_ckv_norm_g",
          "mla_qn_norm_g", "mla_qr_norm_g", "mla_kn_norm_g", "mla_kr_norm_g", "mem_norm_g",
          "mem_q_norm_g", "mem_k_norm_g", "ffn_norm_g"]


def kernel(x, mem, positions, attn_norm_g, w_in, swa_q_norm_g, swa_k_norm_g, swa_sinks, mla_cq_norm_g, mla_ckv_norm_g, w_uq, w_ukv, mla_qn_norm_g, mla_qr_norm_g, mla_kn_norm_g, mla_kr_norm_g, mem_norm_g, w_mem_kv, mem_q_norm_g, mem_k_norm_g, w_out, ffn_norm_g, w_gate, w_up, w_down, loss_target, m_attn_norm_g, m_w_in, m_swa_q_norm_g, m_swa_k_norm_g, m_swa_sinks, m_mla_cq_norm_g, m_mla_ckv_norm_g, m_w_uq, m_w_ukv, m_mla_qn_norm_g, m_mla_qr_norm_g, m_mla_kn_norm_g, m_mla_kr_norm_g, m_mem_norm_g, m_w_mem_kv, m_mem_q_norm_g, m_mem_k_norm_g, m_w_out, m_ffn_norm_g, m_w_gate, m_w_up, m_w_down, v_attn_norm_g, v_w_in, v_swa_q_norm_g, v_swa_k_norm_g, v_swa_sinks, v_mla_cq_norm_g, v_mla_ckv_norm_g, v_w_uq, v_w_ukv, v_mla_qn_norm_g, v_mla_qr_norm_g, v_mla_kn_norm_g, v_mla_kr_norm_g, v_mem_norm_g, v_w_mem_kv, v_mem_q_norm_g, v_mem_k_norm_g, v_w_out, v_ffn_norm_g, v_w_gate, v_w_up, v_w_down):
    args = dict(locals())
    x2, mem2, tgt = x[0], mem[0], loss_target[0]
    s, d = x2.shape
    n_in = w_in.shape[2]
    f = w_gate.shape[2]

    (g_in,) = _all_gather([w_in[0].T.astype(WIRE)])
    mix_shards = [w_uq[0].T.astype(WIRE), w_ukv[0].astype(WIRE), w_mem_kv[0].astype(WIRE),
                  _to_wire([w_out[0]], g_in, "wire_out")[0]]
    g_uq, wkv, g_mkv, g_out = _all_gather_background(mix_shards, 5, "all_gather_mix_weights")
    ffn_shards = [_to_wire([w_gate[0].T, w_up[0].T], g_in, "wire_gate_up"),
                  _to_wire([w_down[0]], g_in, "wire_down")[0]]
    w_gu, w_d = _all_gather_background(ffn_shards, 1, "all_gather_ffn_weights")
    wi = g_in.reshape(N_DEV * n_in, d)
    wi = jnp.concatenate([wi[0:1024], wi[1280:1792], wi[1792:2304], wi[2368:2880],
                          wi[1024:1152], wi[1152:1280], wi[2304:2368],
                          jnp.zeros((IN_PAD - 2880, d), wi.dtype)], axis=0)
    wq = g_uq.reshape(768, 512)
    wq = jnp.concatenate([wq[192 * h: 192 * h + 128] for h in range(4)]
                         + [wq[192 * h + 128: 192 * (h + 1)] for h in range(4)], axis=0)
    wmkv = g_mkv.reshape(-1, g_mkv.shape[-1])
    wo = g_out.reshape(-1, d)

    pos = positions[0].astype(jnp.float32)
    inv_freq = ROPE_THETA ** (-jnp.arange(0, MLA_ROPE, 2, dtype=jnp.float32) / MLA_ROPE)
    ang = pos[:, None] * inv_freq
    cos32, sin32 = jnp.cos(ang), jnp.sin(ang)
    cos_t = jnp.tile(cos32, (1, 4))
    sin_t = jnp.tile(jnp.concatenate([-sin32, sin32], axis=1), (1, 2))
    posc, posr = pos.reshape(s, 1), pos.reshape(1, s)
    two = lambda g: jnp.tile(g, (1, 2))
    gq2, gk2, gqr2, gkr2 = two(swa_q_norm_g), two(swa_k_norm_g), two(mla_qr_norm_g), two(mla_kr_norm_g)
    sinks1 = swa_sinks[0]

    proj, hn = _in_proj(x2, attn_norm_g, wi)
    qc, kc, vb, qb, kvb, cqn, ckvn = _mla_prep(proj, cos_t, sin_t, mla_cq_norm_g, mla_ckv_norm_g, wq, wkv,
                                                mla_qn_norm_g, gqr2, mla_kn_norm_g, gkr2)
    y_b, lse_b = _mla_fwd(qc, kc, vb)
    km, vmm, kvm, memn = _memkv_prep(mem2, mem_norm_g, wmkv, mem_k_norm_g)
    y_m, lse_m = _mem_fwd(proj, mem_q_norm_g, km, vmm)
    y_a, lse_a = _swa_fwd(proj, posc, posr, gq2, gk2, sinks1)
    h1, fn = _out_proj(y_a, y_b, y_m, x2, wo, ffn_norm_g)
    gu, act = _ffn_gu(fn, w_gu)
    dout, dout_b, loss_tile = _ffn_down(act, w_d, h1, tgt)

    dgu, dw_d = _ffn_bwd_act(dout_b, w_d, gu)
    dw_gu = _ffn_dw_gu(fn, dgu)
    r_gu, r_d = _exchange_grads_background([dw_gu, dw_d], 2, "exchange_ffn_grads")
    d_h1, dg_ffn = _ffn_norm_bwd(_ffn_dfn(dgu, w_gu, dw_gu), dout, h1, ffn_norm_g)
    d_y = _mm(d_h1, wo, tb=True, out_dtype=jnp.float32, tm=FFN_TILE, tk=2048, name="d_mix")
    dw_out = jnp.concatenate([
        _mm(y_a, d_h1, ta=True, out_dtype=WIRE, tm=1024, tk=1024, name="dw_out_a"),
        _mm(y_b, d_h1, ta=True, out_dtype=WIRE, tm=1024, tk=1024, name="dw_out_b"),
        _mm(y_m, d_h1, ta=True, out_dtype=WIRE, tm=1024, tk=1024, name="dw_out_m")], axis=0)
    d_qm, dkm, dvmm, dg_mq = _mem_bwd(proj, mem_q_norm_g, km, vmm, d_y, y_m, lse_m)
    dw_mkv, dg_mem, dg_mk = _memkv_bwd(mem2, mem_norm_g, wmkv, mem_k_norm_g, kvm, memn, dkm, dvmm)
    r_mkv, r_out = _exchange_grads_background([dw_mkv.reshape(g_mkv.shape), dw_out.reshape(g_out.shape)], 3,
                                              "exchange_mix_grads")
    dqc, dkc, dvb = _mla_bwd(qc, kc, vb, d_y, y_b, lse_b, dw_mkv)
    (d_cq, d_ckv, d_kr, dw_uq, dw_ukv, dg_cq, dg_ckv, dg_qn, dg_qr, dg_kn, dg_kr) = _mla_prep_bwd(
        proj, cos_t, sin_t, mla_cq_norm_g, mla_ckv_norm_g, wq, wkv, mla_qn_norm_g, gqr2, mla_kn_norm_g, gkr2,
        qb, kvb, cqn, ckvn, dqc, dkc, dvb)
    d_qa, d_ka, d_va, dg_q, dg_k, d_sinks = _swa_bwd(proj, posc, posr, gq2, gk2, sinks1, d_y, y_a, lse_a, dw_out)
    d_proj = [d_qa, d_cq, d_ckv, d_qm, d_ka, d_va, d_kr]
    gi = _dw_in(hn, d_proj, n_in)

    gq_ = jnp.concatenate(sum([[dw_uq[128 * h: 128 * (h + 1)], dw_uq[512 + 64 * h: 512 + 64 * (h + 1)]]
                               for h in range(4)], []), axis=0)
    gq_ = gq_.reshape(N_DEV, 96, 512)
    r_in, r_uq, r_ukv = _exchange_grads_background([gi, gq_, dw_ukv], 4, "exchange_in_grads")
    grad_x, dg_attn = _dx(d_proj, wi, x2, attn_norm_g, d_h1, gi)

    big = {}
    last = [None]

    def adam(name, r, transposed=False, which=None):
        w, m, v = args[name][0], args["m_" + name][0], args["v_" + name][0]
        if transposed:
            outs = _adam_big(r, w.T, m.T, v.T, "adam_" + name, last[0], which)
            big[name] = [o.T[None] for o in outs]
        else:
            outs = _adam_big(r, w, m, v, "adam_" + name, last[0])
            big[name] = [o[None] for o in outs]
        last[0] = outs[0]

    adam("w_gate", r_gu, True, which=0)
    adam("w_up", r_gu, True, which=1)
    adam("w_down", r_d)
    adam("w_out", r_out)
    adam("w_mem_kv", r_mkv)
    adam("w_in", r_in, True)
    adam("w_uq", r_uq, True)
    adam("w_ukv", r_ukv)

    small_g = {
        "attn_norm_g": dg_attn, "swa_q_norm_g": dg_q, "swa_k_norm_g": dg_k,
        "swa_sinks": d_sinks, "mla_cq_norm_g": dg_cq, "mla_ckv_norm_g": dg_ckv, "mla_qn_norm_g": dg_qn,
        "mla_qr_norm_g": dg_qr, "mla_kn_norm_g": dg_kn, "mla_kr_norm_g": dg_kr,
        "mem_norm_g": dg_mem, "mem_q_norm_g": dg_mq, "mem_k_norm_g": dg_mk, "ffn_norm_g": dg_ffn}
    packed_g = _small_allreduce([small_g[n] for n in _SMALL], loss_tile, [args[n].shape[-1] for n in _SMALL])
    loss11, small_out = _small_adam(packed_g, [args[n] for n in _SMALL],
                                    [args["m_" + n] for n in _SMALL], [args["v_" + n] for n in _SMALL])
    small = dict(zip(_SMALL, small_out))
    loss = loss11.reshape(())

    order = ["attn_norm_g", "w_in", "swa_q_norm_g", "swa_k_norm_g", "swa_sinks", "mla_cq_norm_g", "mla_ckv_norm_g",
             "w_uq", "w_ukv", "mla_qn_norm_g", "mla_qr_norm_g", "mla_kn_norm_g", "mla_kr_norm_g", "mem_norm_g",
             "w_mem_kv", "mem_q_norm_g", "mem_k_norm_g", "w_out", "ffn_norm_g", "w_gate", "w_up", "w_down"]
    res = {n: (big[n] if n in big else list(small[n])) for n in order}
    outs = [loss, grad_x[None]]
    for kind in range(4):
        outs += [res[n][kind] for n in order]
    return tuple(outs)
```

```python
import jax
import jax.numpy as jnp
from jax import lax
from jax.experimental import pallas as pl
from jax.experimental.pallas import tpu as pltpu
from jax.experimental.pallas import tpu_sc as plsc

MXU = jnp.bfloat16
WIRE = jnp.bfloat16
EPS = 1e-6
NEG_INF = -1e30
LOG2E = 1.4426950408889634
N_DEV = 8
LANES = 128
ROW_TILE = 256
FFN_TILE = 512
ATT_TILE = 1024
SWA_BLOCK = 128
VMEM_LIMIT = 56 * 1024 * 1024

SWA_Q_HEADS, SWA_KV_HEADS, SWA_DIM = 16, 2, 64
MLA_HEADS, MLA_NOPE, MLA_ROPE, MLA_V = 4, 128, 64, 128
MEM_HEADS, MEM_DIM = 4, 128
ROPE_THETA = 10000.0
ADAM_LR, ADAM_B1, ADAM_B2, ADAM_EPS, ADAM_WD, ADAM_STEP = 0.001, 0.9, 0.999, 1e-08, 0.01, 10

C_QA, C_CQ, C_CKV, C_QM, C_KA, C_VA, C_KR, IN_PAD = 0, 1024, 1536, 2048, 2560, 2688, 2816, 2944


def _pcall(body, *, name, out_shape, in_specs, out_specs, grid=(), scratch=(), sem=None, after=None):
    params = pltpu.CompilerParams(dimension_semantics=sem, vmem_limit_bytes=VMEM_LIMIT)
    if after is not None:
        n_in, inner = len(in_specs), body

        def body(*refs):
            inner(*refs[:n_in], *refs[n_in + 1:])

        in_specs = list(in_specs) + [pl.BlockSpec(memory_space=pl.ANY)]
    call = pl.pallas_call(body, name=name, grid=grid, in_specs=in_specs, out_specs=out_specs,
                          out_shape=out_shape, scratch_shapes=list(scratch), compiler_params=params)
    return call if after is None else (lambda *ops: call(*ops, after))


def _sds(shape, dtype):
    return jax.ShapeDtypeStruct(tuple(shape), dtype)


def _dot(a, b):
    return jnp.dot(a.astype(MXU), b.astype(MXU), preferred_element_type=jnp.float32)


def _dot_nt(a, b):
    return lax.dot_general(a.astype(MXU), b.astype(MXU), (((1,), (1,)), ((), ())),
                           preferred_element_type=jnp.float32)


def _dot_tn(a, b):
    return lax.dot_general(a.astype(MXU), b.astype(MXU), (((0,), (0,)), ((), ())),
                           preferred_element_type=jnp.float32)


def _lo_mask(shape):
    return (lax.broadcasted_iota(jnp.int32, shape, len(shape) - 1) % LANES) < 64


def _norm_fwd(x, g, half=False):
    x2 = x * x
    if half:
        lo = _lo_mask(x.shape)
        s_lo = jnp.sum(jnp.where(lo, x2, 0.0), -1, keepdims=True)
        s_hi = jnp.sum(jnp.where(lo, 0.0, x2), -1, keepdims=True)
        r = jnp.where(lo, lax.rsqrt(s_lo / 64.0 + EPS), lax.rsqrt(s_hi / 64.0 + EPS))
    else:
        r = lax.rsqrt(jnp.mean(x2, -1, keepdims=True) + EPS)
    xn = x * r
    return xn * g, xn, r


def _norm_bwd(xn, r, g, dy, half=False):
    t = dy * g
    tx = t * xn
    if half:
        lo = _lo_mask(xn.shape)
        m_lo = jnp.sum(jnp.where(lo, tx, 0.0), -1, keepdims=True) / 64.0
        m_hi = jnp.sum(jnp.where(lo, 0.0, tx), -1, keepdims=True) / 64.0
        m = jnp.where(lo, m_lo, m_hi)
    else:
        m = jnp.mean(tx, -1, keepdims=True)
    dx = r * (t - xn * m)
    dg = jnp.sum(dy * xn, 0, keepdims=True)
    return dx, dg


def _swap32(x):
    lane = lax.broadcasted_iota(jnp.int32, x.shape, 1)
    return jnp.where((lane % 64) < 32, pltpu.roll(x, 96, 1), pltpu.roll(x, 32, 1))


def _rope(x, cos, sin):
    return x * cos + _swap32(x) * sin


def _rope_bwd(d, cos, sin):
    return d * cos + _swap32(d * sin)


def _my_coords():
    return lax.axis_index("x"), lax.axis_index("y"), lax.axis_index("c")


def _dev_index(px, py, pc):
    return 4 * px + 2 * py + pc


_FLIPS = [(0, 0, 1), (0, 1, 0), (0, 1, 1), (1, 0, 0), (1, 0, 1), (1, 1, 0), (1, 1, 1)]


def _flip(coords, f):
    return tuple((1 - v) if b else v for v, b in zip(coords, f))


def _all_gather(shards):
    n = len(shards)

    def body(*refs):
        ins, outs = refs[:n], refs[n:2 * n]
        send_sems, recv_sems, local_sems = refs[2 * n:]
        x, y, c = _my_coords()
        me, sibling = (x, y, c), (x, y, 1 - c)
        chips = [(1 - x, y), (x, 1 - y), (1 - x, 1 - y)]

        def copy(w, k, block, to, src=None):
            dst = outs[w].at[_dev_index(*block)]
            return pltpu.make_async_remote_copy(
                src_ref=dst if src is None else src, dst_ref=dst,
                send_sem=send_sems.at[w, k], recv_sem=recv_sems.at[w, k],
                device_id=to, device_id_type=pl.DeviceIdType.MESH)

        sends, locals_ = [], []
        for w in range(n):
            mine = pltpu.make_async_copy(ins[w], outs[w].at[_dev_index(*me)], local_sems.at[w])
            mine.start()
            locals_.append(mine)
            first = [copy(w, 0, me, sibling, src=ins[w])]
            first += [copy(w, 1 + j, me, (*chip, c), src=ins[w]) for j, chip in enumerate(chips)]
            for cp in first:
                cp.start()
            sends += first
        for w in range(n):
            for j, chip in enumerate(chips):
                copy(w, 1 + j, (*chip, c), me).wait_recv()
                fwd = copy(w, 4 + j, (*chip, c), sibling)
                fwd.start()
                sends.append(fwd)
        for w in range(n):
            copy(w, 0, sibling, me).wait_recv()
            for j, chip in enumerate(chips):
                copy(w, 4 + j, (*chip, 1 - c), me).wait_recv()
        for cp in sends:
            cp.wait_send()
        for mine in locals_:
            mine.wait()

    any_spec = pl.BlockSpec(memory_space=pl.ANY)
    return _pcall(
        body, name="all_gather_weights",
        out_shape=[_sds((N_DEV,) + s.shape, s.dtype) for s in shards],
        in_specs=[any_spec] * n, out_specs=[any_spec] * n,
        scratch=[pltpu.SemaphoreType.DMA((n, 7)), pltpu.SemaphoreType.DMA((n, 7)),
                 pltpu.SemaphoreType.DMA((n,))])(*shards)


def _wire_cost(arrays):
    nbytes = sum(a.size * a.dtype.itemsize for a in arrays)
    return pl.CostEstimate(flops=0, transcendentals=0, bytes_accessed=40 * nbytes)


def _all_gather_background(shards, collective_id, name):
    n = len(shards)
    src_refs = [jax.new_ref(s, memory_space=pltpu.MemorySpace.HBM) for s in shards]
    out_refs = [jax.empty_ref(_sds((N_DEV,) + s.shape, s.dtype), memory_space=pltpu.MemorySpace.HBM) for s in shards]

    @pl.kernel(mesh=plsc.ScalarSubcoreMesh(axis_name="seq", num_cores=1), name=name,
               scratch_types=(pltpu.SemaphoreType.DMA((n, 7)), pltpu.SemaphoreType.DMA((n, 7)),
                              pltpu.SemaphoreType.DMA((n,))),
               compiler_params=pltpu.CompilerParams(collective_id=collective_id))
    def launch(send_sems, recv_sems, local_sems):
        x, y, c = _my_coords()
        me, sibling = (x, y, c), (x, y, 1 - c)
        chips = [(1 - x, y), (x, 1 - y), (1 - x, 1 - y)]
        barrier = pltpu.get_barrier_semaphore()
        for peer in [sibling] + [(*chip, c) for chip in chips]:
            pl.semaphore_signal(barrier, inc=1, device_id=peer, device_id_type=pl.DeviceIdType.MESH)
        pl.semaphore_wait(barrier, 4)

        def copy(w, k, block, to, src=None):
            dst = out_refs[w].at[_dev_index(*block)]
            return pltpu.make_async_remote_copy(
                src_ref=dst if src is None else src, dst_ref=dst,
                send_sem=send_sems.at[w, k], recv_sem=recv_sems.at[w, k],
                device_id=to, device_id_type=pl.DeviceIdType.MESH)

        sends, locals_ = [], []
        for w in range(n):
            mine = pltpu.make_async_copy(src_refs[w], out_refs[w].at[_dev_index(*me)], local_sems.at[w])
            mine.start()
            locals_.append(mine)
            first = [copy(w, 0, me, sibling, src=src_refs[w])]
            first += [copy(w, 1 + j, me, (*chip, c), src=src_refs[w]) for j, chip in enumerate(chips)]
            for cp in first:
                cp.start()
            sends += first
        for w in range(n):
            for j, chip in enumerate(chips):
                copy(w, 1 + j, (*chip, c), me).wait_recv()
                fwd = copy(w, 4 + j, (*chip, c), sibling)
                fwd.start()
                sends.append(fwd)
        for w in range(n):
            copy(w, 0, sibling, me).wait_recv()
            for j, chip in enumerate(chips):
                copy(w, 4 + j, (*chip, 1 - c), me).wait_recv()
        for cp in sends:
            cp.wait_send()
        for mine in locals_:
            mine.wait()

    launch()
    return [r[...] for r in out_refs]


def _exchange_grads(grads):
    n = len(grads)

    def body(*refs):
        ins, outs = refs[:n], refs[n:2 * n]
        send_sems, recv_sems, local_sems = refs[2 * n:]
        me = _my_coords()
        my_idx = _dev_index(*me)
        sends, locals_ = [], []
        for w in range(n):
            mine = pltpu.make_async_copy(ins[w].at[my_idx], outs[w].at[my_idx], local_sems.at[w])
            mine.start()
            locals_.append(mine)
            for k, f in enumerate(_FLIPS):
                peer = _flip(me, f)
                cp = pltpu.make_async_remote_copy(
                    src_ref=ins[w].at[_dev_index(*peer)], dst_ref=outs[w].at[my_idx],
                    send_sem=send_sems.at[w, k], recv_sem=recv_sems.at[w, k],
                    device_id=peer, device_id_type=pl.DeviceIdType.MESH)
                cp.start()
                sends.append(cp)
        for w in range(n):
            for k, f in enumerate(_FLIPS):
                peer = _flip(me, f)
                slot = outs[w].at[_dev_index(*peer)]
                pltpu.make_async_remote_copy(
                    src_ref=slot, dst_ref=slot,
                    send_sem=send_sems.at[w, k], recv_sem=recv_sems.at[w, k],
                    device_id=peer, device_id_type=pl.DeviceIdType.MESH).wait_recv()
        for cp in sends:
            cp.wait_send()
        for mine in locals_:
            mine.wait()

    any_spec = pl.BlockSpec(memory_space=pl.ANY)
    return _pcall(
        body, name="exchange_grads",
        out_shape=[_sds(g.shape, g.dtype) for g in grads],
        in_specs=[any_spec] * n, out_specs=[any_spec] * n,
        scratch=[pltpu.SemaphoreType.DMA((n, 7)), pltpu.SemaphoreType.DMA((n, 7)),
                 pltpu.SemaphoreType.DMA((n,))])(*grads)


def _exchange_grads_background(grads, collective_id, name):
    n = len(grads)
    src_refs = [jax.new_ref(g, memory_space=pltpu.MemorySpace.HBM) for g in grads]
    out_refs = [jax.empty_ref(_sds(g.shape, g.dtype), memory_space=pltpu.MemorySpace.HBM) for g in grads]

    @pl.kernel(mesh=plsc.ScalarSubcoreMesh(axis_name="seq", num_cores=1), name=name,
               scratch_types=(pltpu.SemaphoreType.DMA((n, 7)), pltpu.SemaphoreType.DMA((n, 7)),
                              pltpu.SemaphoreType.DMA((n,))),
               cost_estimate=_wire_cost(grads),
               compiler_params=pltpu.CompilerParams(collective_id=collective_id))
    def launch(send_sems, recv_sems, local_sems):
        me = _my_coords()
        my_idx = _dev_index(*me)
        peers = [_flip(me, f) for f in _FLIPS]
        barrier = pltpu.get_barrier_semaphore()
        for peer in peers:
            pl.semaphore_signal(barrier, inc=1, device_id=peer, device_id_type=pl.DeviceIdType.MESH)
        pl.semaphore_wait(barrier, len(peers))
        sends, locals_ = [], []
        for w in range(n):
            mine = pltpu.make_async_copy(src_refs[w].at[my_idx], out_refs[w].at[my_idx], local_sems.at[w])
            mine.start()
            locals_.append(mine)
            for k, peer in enumerate(peers):
                cp = pltpu.make_async_remote_copy(
                    src_ref=src_refs[w].at[_dev_index(*peer)], dst_ref=out_refs[w].at[my_idx],
                    send_sem=send_sems.at[w, k], recv_sem=recv_sems.at[w, k],
                    device_id=peer, device_id_type=pl.DeviceIdType.MESH)
                cp.start()
                sends.append(cp)
        for w in range(n):
            for k, peer in enumerate(peers):
                slot = out_refs[w].at[_dev_index(*peer)]
                pltpu.make_async_remote_copy(
                    src_ref=slot, dst_ref=slot, send_sem=send_sems.at[w, k], recv_sem=recv_sems.at[w, k],
                    device_id=peer, device_id_type=pl.DeviceIdType.MESH).wait_recv()
        for cp in sends:
            cp.wait_send()
        for mine in locals_:
            mine.wait()

    launch()
    return [r[...] for r in out_refs]


def _to_wire(parts, after, name):
    n = len(parts)
    rows, cols = parts[0].shape
    tr = rows // 2 if rows % 32 == 0 else rows

    def body(*refs):
        for k in range(n):
            refs[n][k] = refs[k][...].astype(WIRE)

    blk = pl.BlockSpec((tr, cols), lambda i: (i, 0))
    return _pcall(
        body, name=name, grid=(rows // tr,), out_shape=_sds((n, rows, cols), WIRE),
        in_specs=[blk] * n, out_specs=pl.BlockSpec((n, tr, cols), lambda i: (0, i, 0)),
        sem=("parallel",), after=after)(*parts)


def _adam_math(w, g, m, v):
    m = ADAM_B1 * m + (1.0 - ADAM_B1) * g
    v = ADAM_B2 * v + (1.0 - ADAM_B2) * (g * g)
    m_hat = m / (1.0 - ADAM_B1 ** ADAM_STEP)
    v_hat = v / (1.0 - ADAM_B2 ** ADAM_STEP)
    delta = -ADAM_LR * (m_hat / (jnp.sqrt(v_hat) + ADAM_EPS) + ADAM_WD * w)
    return delta, m, v


def _small_layout(sizes):
    row0, r = [], 0
    for n in sizes:
        row0.append(r)
        r += -(-n // LANES)
    return row0, r, -(-(r + 1) // 8) * 8


def _small_pieces(n):
    return [(k, min(LANES, n - LANES * k)) for k in range(-(-n // LANES))]


def _small_fill(pack, slot, srcs, sizes, row0, rows):
    pack[slot] = jnp.zeros((rows, LANES), jnp.float32)
    for p, n in enumerate(sizes):
        val = srcs[p][...]
        if val.shape[-1] == LANES and n == 64:
            pack[slot, row0[p]:row0[p] + 1, :] = val + pltpu.roll(val, 64, 1)
            continue
        for k, width in _small_pieces(n):
            pack[slot, row0[p] + k:row0[p] + k + 1, 0:width] = srcs[p][:, LANES * k:LANES * k + width]


def _small_allreduce(grads, loss_tile, sizes):
    n_par = len(sizes)
    row0, loss_row, rows = _small_layout(sizes)

    def body(*refs):
        g_refs, loss_in, out_ref = refs[:n_par], refs[n_par], refs[n_par + 1]
        pack, gath, send_sems, recv_sems = refs[n_par + 2:]
        me = _my_coords()
        my_idx = _dev_index(*me)
        _small_fill(pack, 0, g_refs, sizes, row0, rows)
        pack[0, loss_row:loss_row + 1, :] = loss_in[0:1, :]
        gath[my_idx] = pack[0]
        sends = []
        for k, f in enumerate(_FLIPS):
            peer = _flip(me, f)
            cp = pltpu.make_async_remote_copy(
                src_ref=pack.at[0], dst_ref=gath.at[my_idx],
                send_sem=send_sems.at[k], recv_sem=recv_sems.at[k],
                device_id=peer, device_id_type=pl.DeviceIdType.MESH)
            cp.start()
            sends.append(cp)
        for k, f in enumerate(_FLIPS):
            peer = _flip(me, f)
            slot = gath.at[_dev_index(*peer)]
            pltpu.make_async_remote_copy(
                src_ref=slot, dst_ref=slot, send_sem=send_sems.at[k], recv_sem=recv_sems.at[k],
                device_id=peer, device_id_type=pl.DeviceIdType.MESH).wait_recv()
        for cp in sends:
            cp.wait_send()
        g = gath[0]
        for d in range(1, N_DEV):
            g = g + gath[d]
        out_ref[...] = g

    vm = pl.BlockSpec(memory_space=pltpu.VMEM)
    return _pcall(
        body, name="small_allreduce", out_shape=_sds((rows, LANES), jnp.float32),
        in_specs=[vm] * (n_par + 1), out_specs=vm,
        scratch=[pltpu.VMEM((1, rows, LANES), jnp.float32), pltpu.VMEM((N_DEV, rows, LANES), jnp.float32),
                 pltpu.SemaphoreType.DMA((7,)), pltpu.SemaphoreType.DMA((7,))])(*grads, loss_tile)


def _small_adam(packed_g, ws, ms, vs):
    sizes = [w.shape[-1] for w in ws]
    n_par = len(ws)
    row0, loss_row, rows = _small_layout(sizes)

    def body(*refs):
        g_ref = refs[0]
        w_refs, m_refs, v_refs = (refs[1 + k * n_par: 1 + (k + 1) * n_par] for k in range(3))
        loss_out = refs[3 * n_par + 1]
        out_refs = refs[3 * n_par + 2: 7 * n_par + 2]
        pack, res = refs[7 * n_par + 2:]
        for slot, srcs in enumerate((w_refs, m_refs, v_refs)):
            _small_fill(pack, slot, srcs, sizes, row0, rows)
        g = g_ref[...]
        delta, m, v = _adam_math(pack[0], g, pack[1], pack[2])
        res[0], res[1], res[2], res[3] = g, delta, m, v
        loss_out[...] = res[0, loss_row:loss_row + 1, 0:1]
        for p, n in enumerate(sizes):
            for kind in range(4):
                for k, width in _small_pieces(n):
                    out_refs[4 * p + kind][:, LANES * k:LANES * k + width] = (
                        res[kind, row0[p] + k:row0[p] + k + 1, 0:width])

    vm = pl.BlockSpec(memory_space=pltpu.VMEM)
    out_shape = [_sds((1, 1), jnp.float32)]
    for n in sizes:
        out_shape += [_sds((1, n), jnp.float32)] * 4
    outs = _pcall(
        body, name="small_adam", out_shape=out_shape,
        in_specs=[vm] * (3 * n_par + 1), out_specs=[vm] * len(out_shape),
        scratch=[pltpu.VMEM((3, rows, LANES), jnp.float32), pltpu.VMEM((4, rows, LANES), jnp.float32)])(
            packed_g, *ws, *ms, *vs)
    return outs[0], [outs[1 + 4 * p: 5 + 4 * p] for p in range(n_par)]


def _adam_big(recv, w, m, v, name, after=None, which=None):
    rows, cols = recv.shape[-2:]
    row_tiles = [t for t in range(16, rows + 1, 16) if rows % t == 0 and t * cols <= 400 * 1024]
    tr, tc = (max(row_tiles), cols) if row_tiles else (rows, 512 if cols % 512 == 0 else cols)

    def body(r_ref, w_ref, m_ref, v_ref, g_ref, d_ref, mo_ref, vo_ref):
        g = r_ref[0].astype(jnp.float32)
        for d in range(1, N_DEV):
            g = g + r_ref[d].astype(jnp.float32)
        delta, mn, vn = _adam_math(w_ref[...], g, m_ref[...], v_ref[...])
        g_ref[...] = g
        d_ref[...] = delta
        mo_ref[...] = mn
        vo_ref[...] = vn

    blk = pl.BlockSpec((tr, tc), lambda i, j: (i, j))
    if which is None:
        r_spec = pl.BlockSpec((N_DEV, tr, tc), lambda i, j: (0, i, j))
    else:
        r_spec = pl.BlockSpec((N_DEV, None, tr, tc), lambda i, j: (0, which, i, j))
    return _pcall(
        body, name=name, grid=(rows // tr, cols // tc),
        out_shape=[_sds((rows, cols), jnp.float32)] * 4,
        in_specs=[r_spec, blk, blk, blk],
        out_specs=[blk] * 4, sem=("parallel", "parallel"), after=after)(recv, w, m, v)


def _mm(a, b, *, ta=False, tb=False, out_dtype, tm, tk, name):
    (kdim, mdim) = a.shape if ta else a.shape[::-1]
    ndim = b.shape[0] if tb else b.shape[1]
    tm, tk = min(tm, mdim), min(tk, kdim)
    nk = kdim // tk

    def body(a_ref, b_ref, o_ref, acc):
        k = pl.program_id(1)
        if ta:
            part = _dot_tn(a_ref[...], b_ref[...])
        elif tb:
            part = _dot_nt(a_ref[...], b_ref[...])
        else:
            part = _dot(a_ref[...], b_ref[...])

        @pl.when(k == 0)
        def _():
            acc[...] = part

        @pl.when(k > 0)
        def _():
            acc[...] += part

        @pl.when(k == nk - 1)
        def _():
            o_ref[...] = acc[...].astype(o_ref.dtype)

    a_spec = pl.BlockSpec((tk, tm), lambda i, k: (k, i)) if ta else pl.BlockSpec((tm, tk), lambda i, k: (i, k))
    b_spec = pl.BlockSpec((ndim, tk), lambda i, k: (0, k)) if tb else pl.BlockSpec((tk, ndim), lambda i, k: (k, 0))
    return _pcall(
        body, name=name, grid=(mdim // tm, nk), out_shape=_sds((mdim, ndim), out_dtype),
        in_specs=[a_spec, b_spec], out_specs=pl.BlockSpec((tm, ndim), lambda i, k: (i, 0)),
        scratch=[pltpu.VMEM((tm, ndim), jnp.float32)], sem=("parallel", "arbitrary"))(a, b)


def _ref_col_pieces(start, stop):
    ref_starts = [0, 1024, 1152, 1280, 1792, 2304, 2368, 2880]
    perm_starts = [C_QA, C_KA, C_VA, C_CQ, C_CKV, C_KR, C_QM]
    out = []
    for p in range(7):
        lo, hi = max(start, ref_starts[p]), min(stop, ref_starts[p + 1])
        if lo < hi:
            out.append((lo - start, perm_starts[p] + lo - ref_starts[p], hi - lo))
    return out


def _dw_in(hn, d_proj, n_shard):
    s, d = hn.shape
    n = sum(p.shape[1] for p in d_proj)
    n_pc = len(d_proj)
    tm, tk = min(512, d), min(1024, s)
    nk = s // tk

    def body(a_ref, *refs):
        b_refs, (o_ref, acc) = refs[:n_pc], refs[n_pc:]
        k = pl.program_id(1)
        part = _dot_tn(a_ref[...], jnp.concatenate([r[...] for r in b_refs], axis=1))

        @pl.when(k == 0)
        def _():
            acc[...] = part

        @pl.when(k > 0)
        def _():
            acc[...] += part

        @pl.when(k == nk - 1)
        def _():
            t = acc[...].T
            for j in range(N_DEV):
                rows = [t[src:src + width] for _, src, width in _ref_col_pieces(j * n_shard, (j + 1) * n_shard)]
                o_ref[j] = jnp.concatenate(rows, axis=0).astype(o_ref.dtype)

    return _pcall(
        body, name="dw_in", grid=(d // tm, nk), out_shape=_sds((N_DEV, n_shard, d), WIRE),
        in_specs=[pl.BlockSpec((tk, tm), lambda i, k: (k, i))]
        + [pl.BlockSpec((tk, p.shape[1]), lambda i, k: (k, 0)) for p in d_proj],
        out_specs=pl.BlockSpec((N_DEV, n_shard, tm), lambda i, k: (0, 0, i)),
        scratch=[pltpu.VMEM((tm, n), jnp.float32)], sem=("parallel", "arbitrary"))(hn, *d_proj)


def _in_proj(x, g, w):
    s, d = x.shape
    n = w.shape[0]
    tm = min(2 * ROW_TILE, s)

    def body(x_ref, g_ref, w_ref, p_ref, hn_ref):
        hn, _, _ = _norm_fwd(x_ref[...], g_ref[...])
        hn_ref[...] = hn.astype(hn_ref.dtype)
        p_ref[...] = _dot_nt(hn, w_ref[...])

    return _pcall(
        body, name="in_proj", grid=(s // tm,),
        out_shape=[_sds((s, n), jnp.float32), _sds((s, d), MXU)],
        in_specs=[pl.BlockSpec((tm, d), lambda i: (i, 0)), pl.BlockSpec((1, d), lambda i: (0, 0)),
                  pl.BlockSpec((n, d), lambda i: (0, 0), pipeline_mode=pl.Buffered(1))],
        out_specs=[pl.BlockSpec((tm, n), lambda i: (i, 0)), pl.BlockSpec((tm, d), lambda i: (i, 0))],
        sem=("parallel",))(x, g, w)


def _mla_prep(proj, cos, sin, g_cq, g_ckv, w_uq, w_ukv, g_qn, g_qr, g_kn, g_kr):
    s = proj.shape[0]
    tm = min(ROW_TILE, s)
    nh = MLA_HEADS

    def body(cq_ref, ckv_ref, kr_ref, cos_ref, sin_ref, gcq_ref, gckv_ref, wuq_ref, wukv_ref,
             gqn_ref, gqr_ref, gkn_ref, gkr_ref,
             qc_ref, kc_ref, v_ref, qb_ref, kvb_ref, cqn_ref, ckvn_ref):
        cos_t, sin_t = cos_ref[...], sin_ref[...]
        lo = _lo_mask((tm, LANES))
        cqn, _, _ = _norm_fwd(cq_ref[...], gcq_ref[...])
        cqn_ref[...] = cqn.astype(cqn_ref.dtype)
        qb = _dot_nt(cqn, wuq_ref[...])
        qb_ref[...] = qb
        ckvn, _, _ = _norm_fwd(ckv_ref[...], gckv_ref[...])
        ckvn_ref[...] = ckvn.astype(ckvn_ref.dtype)
        w_ukv_full = jnp.concatenate([wukv_ref[dev] for dev in range(N_DEV)], axis=1)
        kvb = _dot(ckvn, w_ukv_full)
        kvb_ref[...] = kvb
        kr, _, _ = _norm_fwd(kr_ref[...], gkr_ref[...], half=True)
        kr = _rope(kr, cos_t, sin_t)
        kr2 = jnp.where(lo, kr, pltpu.roll(kr, 64, 1))
        ropes = []
        for j in range(nh // 2):
            xr = qb[:, nh * MLA_NOPE + LANES * j: nh * MLA_NOPE + LANES * (j + 1)]
            qr, _, _ = _norm_fwd(xr, gqr_ref[...], half=True)
            ropes.append(_rope(qr, cos_t, sin_t))
        for h in range(nh):
            qn, _, _ = _norm_fwd(qb[:, MLA_NOPE * h: MLA_NOPE * (h + 1)], gqn_ref[...])
            mask = lo if h % 2 == 0 else jnp.logical_not(lo)
            qr = jnp.where(mask, ropes[h // 2], 0.0)
            qc_ref[h] = jnp.concatenate([qn, qr], axis=1).astype(qc_ref.dtype)
            kn, _, _ = _norm_fwd(kvb[:, 256 * h: 256 * h + MLA_NOPE], gkn_ref[...])
            kc_ref[h] = jnp.concatenate([kn, kr2], axis=1).astype(kc_ref.dtype)
            v_ref[h] = kvb[:, 256 * h + MLA_NOPE: 256 * (h + 1)].astype(v_ref.dtype)

    def col(width, start):
        return pl.BlockSpec((tm, width), lambda i: (i, start // width))

    def full(shape):
        return pl.BlockSpec(shape, lambda i: (0,) * len(shape))

    def row(width):
        return pl.BlockSpec((tm, width), lambda i: (i, 0))

    def heads(width):
        return pl.BlockSpec((nh, tm, width), lambda i: (0, i, 0))

    return _pcall(
        body, name="mla_prep", grid=(s // tm,),
        out_shape=[_sds((nh, s, 256), MXU), _sds((nh, s, 256), MXU), _sds((nh, s, MLA_V), MXU),
                   _sds((s, 768), jnp.float32), _sds((s, 1024), jnp.float32),
                   _sds((s, 512), MXU), _sds((s, 512), MXU)],
        in_specs=[col(512, C_CQ), col(512, C_CKV), col(LANES, C_KR), row(LANES), row(LANES),
                  full((1, 512)), full((1, 512)), full((768, 512)), full((N_DEV, 512, LANES)),
                  full((1, LANES)), full((1, LANES)), full((1, LANES)), full((1, LANES))],
        out_specs=[heads(256), heads(256), heads(MLA_V), row(768), row(1024), row(512), row(512)],
        sem=("parallel",))(proj, proj, proj, cos, sin, g_cq, g_ckv, w_uq, w_ukv, g_qn, g_qr, g_kn, g_kr)


def _mla_fwd(qc, kc, v):
    nh, s, _ = qc.shape
    t = min(ATT_TILE, s)
    nb = s // t
    scale = (MLA_NOPE + MLA_ROPE) ** -0.5

    def body(q_ref, k_ref, v_ref, y_ref, lse_ref, m_sc, l_sc, acc):
        qi, ki = pl.program_id(1), pl.program_id(2)

        @pl.when(ki == 0)
        def _():
            m_sc[...] = jnp.full_like(m_sc, NEG_INF)
            l_sc[...] = jnp.zeros_like(l_sc)
            acc[...] = jnp.zeros_like(acc)

        def step(diagonal):
            rc = t // 4 if diagonal else t
            for c in range(t // rc):
                rows = slice(rc * c, rc * (c + 1))
                keys = slice(0, rc * (c + 1))
                sc = _dot_nt(q_ref[0, rows, :], k_ref[0, keys, :]) * (scale * LOG2E)
                if diagonal:
                    r_i = lax.broadcasted_iota(jnp.int32, sc.shape, 0) + rc * c
                    c_i = lax.broadcasted_iota(jnp.int32, sc.shape, 1)
                    sc = jnp.where(c_i <= r_i, sc, NEG_INF)
                m_old = m_sc[rows, :]
                m_new = jnp.maximum(m_old, jnp.max(sc, -1, keepdims=True))
                alpha = jnp.exp2(m_old - m_new)
                p = jnp.exp2(sc - m_new)
                l_sc[rows, :] = alpha * l_sc[rows, :] + jnp.sum(p, -1, keepdims=True)
                acc[rows, :] = alpha * acc[rows, :] + _dot(p, v_ref[0, keys, :])
                m_sc[rows, :] = m_new

        @pl.when(ki < qi)
        def _():
            step(False)

        @pl.when(ki == qi)
        def _():
            step(True)

        @pl.when(ki == qi)
        def _():
            y_ref[...] = acc[...] / l_sc[...]
            lse_ref[0] = m_sc[...] + jnp.log2(l_sc[...])

    return _pcall(
        body, name="mla_fwd", grid=(nh, nb, nb),
        out_shape=[_sds((s, nh * MLA_V), jnp.float32), _sds((nh, s, 1), jnp.float32)],
        in_specs=[pl.BlockSpec((1, t, 256), lambda h, i, k: (h, i, 0)),
                  pl.BlockSpec((1, t, 256), lambda h, i, k: (h, jnp.minimum(k, i), 0)),
                  pl.BlockSpec((1, t, MLA_V), lambda h, i, k: (h, jnp.minimum(k, i), 0))],
        out_specs=[pl.BlockSpec((t, MLA_V), lambda h, i, k: (i, h)),
                   pl.BlockSpec((1, t, 1), lambda h, i, k: (h, i, 0))],
        scratch=[pltpu.VMEM((t, 1), jnp.float32), pltpu.VMEM((t, 1), jnp.float32),
                 pltpu.VMEM((t, MLA_V), jnp.float32)],
        sem=("parallel", "parallel", "arbitrary"))(qc, kc, v)


def _memkv_prep(mem, g_mem, w_mkv, g_mk):
    ml, d = mem.shape
    hw = MEM_HEADS * MEM_DIM

    def body(mem_ref, g_ref, w_ref, gk_ref, k_ref, v_ref, kv_ref, mn_ref):
        mn, _, _ = _norm_fwd(mem_ref[...], g_ref[...])
        mn_ref[...] = mn.astype(mn_ref.dtype)
        kv = _dot(mn, w_ref[...])
        kv_ref[...] = kv
        for h in range(MEM_HEADS):
            kn, _, _ = _norm_fwd(kv[:, MEM_DIM * h: MEM_DIM * (h + 1)], gk_ref[...])
            k_ref[:, MEM_DIM * h: MEM_DIM * (h + 1)] = kn.astype(k_ref.dtype)
        v_ref[...] = kv[:, hw:].astype(v_ref.dtype)

    vm = pl.BlockSpec(memory_space=pltpu.VMEM)
    return _pcall(
        body, name="memkv_prep",
        out_shape=[_sds((ml, hw), MXU), _sds((ml, hw), MXU), _sds((ml, 2 * hw), jnp.float32), _sds((ml, d), MXU)],
        in_specs=[vm] * 4, out_specs=[vm] * 4)(mem, g_mem, w_mkv, g_mk)


def _mem_fwd(proj, g_mq, km, vmm):
    s = proj.shape[0]
    ml, hw = km.shape
    tm = min(FFN_TILE, s)
    scale = MEM_DIM ** -0.5

    def body(q_ref, g_ref, k_ref, v_ref, y_ref, lse_ref):
        col = lax.broadcasted_iota(jnp.int32, (tm, MEM_HEADS), 1)
        lse_t = jnp.zeros((tm, MEM_HEADS), jnp.float32)
        for h in range(MEM_HEADS):
            sl = slice(MEM_DIM * h, MEM_DIM * (h + 1))
            qn, _, _ = _norm_fwd(q_ref[:, sl], g_ref[...])
            sc = _dot_nt(qn, k_ref[:, sl]) * scale
            m = jnp.max(sc, -1, keepdims=True)
            p = jnp.exp(sc - m)
            l = jnp.sum(p, -1, keepdims=True)
            y_ref[:, sl] = _dot(p, v_ref[:, sl]) / l
            lse_t = jnp.where(col == h, m + jnp.log(l), lse_t)
        lse_ref[...] = lse_t

    return _pcall(
        body, name="mem_fwd", grid=(s // tm,),
        out_shape=[_sds((s, hw), jnp.float32), _sds((s, MEM_HEADS), jnp.float32)],
        in_specs=[pl.BlockSpec((tm, hw), lambda i: (i, C_QM // hw)), pl.BlockSpec((1, MEM_DIM), lambda i: (0, 0)),
                  pl.BlockSpec((ml, hw), lambda i: (0, 0)), pl.BlockSpec((ml, hw), lambda i: (0, 0))],
        out_specs=[pl.BlockSpec((tm, hw), lambda i: (i, 0)), pl.BlockSpec((tm, MEM_HEADS), lambda i: (i, 0))],
        sem=("parallel",))(proj, g_mq, km, vmm)


def _alibi_slope(h):
    return float(2.0 ** (-8.0 * (h + 1) / SWA_Q_HEADS))


def _swa_common(n, kp, kc, vp, vc, pq, pkp, pkc, gk):
    b = SWA_BLOCK
    k_raw = jnp.concatenate([kp, kc], axis=0)
    kn, kxn, kr = _norm_fwd(k_raw, gk, half=True)
    v = jnp.concatenate([vp, vc], axis=0)
    dist = jnp.abs(pq - jnp.concatenate([pkp, pkc], axis=1))
    r_i = lax.broadcasted_iota(jnp.int32, (b, 2 * b), 0)
    c_i = lax.broadcasted_iota(jnp.int32, (b, 2 * b), 1)
    valid = (c_i > r_i) & (c_i <= r_i + b) & (c_i >= jnp.where(n > 0, 0, b))
    bias = jnp.where(valid, -dist, NEG_INF)
    return kn, v, bias


def _swa_folded(n, kp, kc, pq, pkp, pkc, gk):
    b = SWA_BLOCK
    kn_p, _, _ = _norm_fwd(kp, gk, half=True)
    kn_c, _, _ = _norm_fwd(kc, gk, half=True)
    r_i = lax.broadcasted_iota(jnp.int32, (b, b), 0)
    c_i = lax.broadcasted_iota(jnp.int32, (b, b), 1)
    upper = c_i > r_i
    bias_prev = jnp.where(n > 0, 0.0, NEG_INF) - jnp.abs(pq - pkp)
    bias = jnp.where(upper, bias_prev, -jnp.abs(pq - pkc))
    return kn_p, kn_c, bias, upper


def _swa_specs(s):
    b = SWA_BLOCK
    prev = lambda n: jnp.maximum(n - 1, 0)
    return [
        pl.BlockSpec((b, 1024), lambda n: (n, C_QA // 1024)),
        pl.BlockSpec((b, LANES), lambda n: (prev(n), C_KA // LANES)),
        pl.BlockSpec((b, LANES), lambda n: (n, C_KA // LANES)),
        pl.BlockSpec((b, LANES), lambda n: (prev(n), C_VA // LANES)),
        pl.BlockSpec((b, LANES), lambda n: (n, C_VA // LANES)),
        pl.BlockSpec((b, 1), lambda n: (n, 0)),
        pl.BlockSpec((1, b), lambda n: (0, prev(n))),
        pl.BlockSpec((1, b), lambda n: (0, n)),
        pl.BlockSpec((1, LANES), lambda n: (0, 0)),
        pl.BlockSpec((1, LANES), lambda n: (0, 0)),
        pl.BlockSpec(memory_space=pltpu.SMEM),
    ]


def _swa_fwd(proj, posc, posr, gq, gk, sinks):
    s = proj.shape[0]
    b = SWA_BLOCK
    scale = SWA_DIM ** -0.5

    def body(q_ref, kp_ref, kc_ref, vp_ref, vc_ref, pq_ref, pkp_ref, pkc_ref, gq_ref, gk_ref, sink_ref,
             y_ref, lse_ref):
        n = pl.program_id(0)
        kn_p, kn_c, bias, upper = _swa_folded(n, kp_ref[...], kc_ref[...], pq_ref[...], pkp_ref[...], pkc_ref[...],
                                              gk_ref[...])
        v_p, v_c = vp_ref[...], vc_ref[...]
        lo = _lo_mask((b, LANES))
        col = lax.broadcasted_iota(jnp.int32, (b, SWA_Q_HEADS), 1)
        lse_t = jnp.zeros((b, SWA_Q_HEADS), jnp.float32)
        hpg = SWA_Q_HEADS // SWA_KV_HEADS
        for g in range(SWA_KV_HEADS):
            heads = range(hpg * g, hpg * (g + 1))
            kvmask = lo if g == 0 else jnp.logical_not(lo)
            qs = []
            for j in range(hpg // 2 * g, hpg // 2 * (g + 1)):
                qn, _, _ = _norm_fwd(q_ref[:, LANES * j: LANES * (j + 1)], gq_ref[...], half=True)
                qn = qn * scale
                qsw = pltpu.roll(qn, 64, 1)
                qs += [jnp.where(kvmask, qn if e == g else qsw, 0.0) for e in range(2)]
            q_st = jnp.concatenate(qs, axis=0).astype(MXU)
            sp_st, sc_st = _dot_nt(q_st, kn_p), _dot_nt(q_st, kn_c)
            pus, pls, ls = [], [], []
            for i, h in enumerate(heads):
                rows = slice(b * i, b * (i + 1))
                sc = jnp.where(upper, sp_st[rows], sc_st[rows]) + _alibi_slope(h) * bias
                sk = sink_ref[h]
                m = jnp.maximum(jnp.max(sc, -1, keepdims=True), sk)
                p = jnp.exp(sc - m)
                l = jnp.sum(p, -1, keepdims=True) + jnp.exp(sk - m)
                pus.append(jnp.where(upper, p, 0.0).astype(MXU))
                pls.append(jnp.where(upper, 0.0, p).astype(MXU))
                ls.append(l)
                lse_t = jnp.where(col == h, m + jnp.log(l), lse_t)
            o_st = _dot(jnp.concatenate(pus, axis=0), v_p) + _dot(jnp.concatenate(pls, axis=0), v_c)
            for j in range(hpg // 2 * g, hpg // 2 * (g + 1)):
                halves = []
                for e in range(2):
                    i = 2 * j + e - hpg * g
                    o_h = o_st[b * i: b * (i + 1)] / ls[i]
                    halves.append(o_h if e == g else pltpu.roll(o_h, 64, 1))
                y_ref[:, LANES * j: LANES * (j + 1)] = jnp.where(lo, halves[0], halves[1])
        lse_ref[...] = lse_t

    return _pcall(
        body, name="swa_fwd", grid=(s // b,),
        out_shape=[_sds((s, 1024), jnp.float32), _sds((s, SWA_Q_HEADS), jnp.float32)],
        in_specs=_swa_specs(s),
        out_specs=[pl.BlockSpec((b, 1024), lambda n: (n, 0)), pl.BlockSpec((b, SWA_Q_HEADS), lambda n: (n, 0))],
        sem=("parallel",))(proj, proj, proj, proj, proj, posc, posr, posr, gq, gk, sinks)


def _out_proj(y_a, y_b, y_m, x, w_out, g_ffn):
    s, d = x.shape
    tm = min(2 * ROW_TILE, s)

    def body(ya_ref, yb_ref, ym_ref, x_ref, w_ref, g_ref, h1_ref, fn_ref):
        y = jnp.concatenate([ya_ref[...].astype(MXU), yb_ref[...].astype(MXU), ym_ref[...].astype(MXU)], axis=1)
        h1 = x_ref[...] + _dot(y, w_ref[...])
        h1_ref[...] = h1
        fn, _, _ = _norm_fwd(h1, g_ref[...])
        fn_ref[...] = fn.astype(fn_ref.dtype)

    def row(width):
        return pl.BlockSpec((tm, width), lambda i: (i, 0))

    return _pcall(
        body, name="out_proj", grid=(s // tm,),
        out_shape=[_sds((s, d), jnp.float32), _sds((s, d), MXU)],
        in_specs=[row(1024), row(512), row(512), row(d),
                  pl.BlockSpec(w_out.shape, lambda i: (0, 0), pipeline_mode=pl.Buffered(1)),
                  pl.BlockSpec((1, d), lambda i: (0, 0))],
        out_specs=[row(d), row(d)], sem=("parallel",))(y_a, y_b, y_m, x, w_out, g_ffn)


def _ffn_gu(fn, w_gu):
    s, d = fn.shape
    f = w_gu.shape[2]
    tm = min(2 * FFN_TILE, s)

    def body(fn_ref, w_ref, gu_ref, act_ref):
        x = fn_ref[...]
        g = _dot_nt(x, w_ref[0, 0])
        u = _dot_nt(x, w_ref[0, 1])
        gu_ref[0, 0] = g
        gu_ref[0, 1] = u
        act_ref[0] = (g * jax.nn.sigmoid(g) * u).astype(act_ref.dtype)

    return _pcall(
        body, name="ffn_gate_up", grid=(N_DEV, s // tm),
        out_shape=[_sds((N_DEV, 2, s, f), jnp.float32), _sds((N_DEV, s, f), MXU)],
        in_specs=[pl.BlockSpec((tm, d), lambda j, i: (i, 0)),
                  pl.BlockSpec((1, 2, f, d), lambda j, i: (j, 0, 0, 0))],
        out_specs=[pl.BlockSpec((1, 2, tm, f), lambda j, i: (j, 0, i, 0)),
                   pl.BlockSpec((1, tm, f), lambda j, i: (j, i, 0))],
        sem=("parallel", "parallel"))(fn, w_gu)


def _ffn_down(act, w_d, h1, target):
    _, s, f = act.shape
    d = h1.shape[1]
    tm = min(FFN_TILE, s)

    def body(a_ref, w_ref, h1_ref, t_ref, dout_ref, doutb_ref, loss_ref, acc):
        i, j = pl.program_id(0), pl.program_id(1)
        part = _dot(a_ref[0], w_ref[0]) + _dot(a_ref[1], w_ref[1])

        @pl.when(j == 0)
        def _():
            acc[...] = h1_ref[...] + part

        @pl.when(j > 0)
        def _():
            acc[...] += part

        @pl.when((i == 0) & (j == 0))
        def _():
            loss_ref[...] = jnp.zeros_like(loss_ref)

        @pl.when(j == N_DEV // 2 - 1)
        def _():
            diff = acc[...] - t_ref[...]
            dout_ref[...] = diff / d
            doutb_ref[...] = (diff / d).astype(doutb_ref.dtype)
            loss_ref[...] += 0.5 * jnp.sum(jnp.sum(diff * diff, -1, keepdims=True) / d)

    row = pl.BlockSpec((tm, d), lambda i, j: (i, 0))
    return _pcall(
        body, name="ffn_down", grid=(s // tm, N_DEV // 2),
        out_shape=[_sds((s, d), jnp.float32), _sds((s, d), MXU), _sds((8, LANES), jnp.float32)],
        in_specs=[pl.BlockSpec((2, tm, f), lambda i, j: (j, i, 0)), pl.BlockSpec((2, f, d), lambda i, j: (j, 0, 0)),
                  row, row],
        out_specs=[row, row, pl.BlockSpec((8, LANES), lambda i, j: (0, 0))],
        scratch=[pltpu.VMEM((tm, d), jnp.float32)], sem=("arbitrary", "arbitrary"))(act, w_d, h1, target)


def _ffn_bwd_act(dout, w_d, gu):
    s, d = dout.shape
    f = w_d.shape[1]
    tm = min(2 * FFN_TILE, s)
    ni = s // tm

    def body(do_ref, w_ref, gu_ref, dgu_ref, dw_ref, acc):
        i = pl.program_id(1)
        do = do_ref[...]
        d_act = _dot_nt(do, w_ref[0])
        g, u = gu_ref[0, 0], gu_ref[0, 1]
        sig = jax.nn.sigmoid(g)
        silu = g * sig
        dgu_ref[0, 0] = (d_act * u * (sig * (1.0 + g * (1.0 - sig)))).astype(dgu_ref.dtype)
        dgu_ref[0, 1] = (d_act * silu).astype(dgu_ref.dtype)
        part = _dot_tn(silu * u, do)

        @pl.when(i == 0)
        def _():
            acc[...] = part

        @pl.when(i > 0)
        def _():
            acc[...] += part

        @pl.when(i == ni - 1)
        def _():
            dw_ref[0] = acc[...].astype(dw_ref.dtype)

    return _pcall(
        body, name="ffn_bwd_act", grid=(N_DEV, ni),
        out_shape=[_sds((N_DEV, 2, s, f), MXU), _sds((N_DEV, f, d), WIRE)],
        in_specs=[pl.BlockSpec((tm, d), lambda j, i: (i, 0)), pl.BlockSpec((1, f, d), lambda j, i: (j, 0, 0)),
                  pl.BlockSpec((1, 2, tm, f), lambda j, i: (j, 0, i, 0))],
        out_specs=[pl.BlockSpec((1, 2, tm, f), lambda j, i: (j, 0, i, 0)),
                   pl.BlockSpec((1, f, d), lambda j, i: (j, 0, 0))],
        scratch=[pltpu.VMEM((f, d), jnp.float32)], sem=("parallel", "arbitrary"))(dout, w_d, gu)


def _ffn_dw_gu(fn, dgu):
    s, d = fn.shape
    f = dgu.shape[-1]
    tk = min(4 * FFN_TILE, s)
    nk = s // tk

    def body(fn_ref, dgu_ref, dw_ref, acc):
        k = pl.program_id(2)
        part = _dot_tn(dgu_ref[0, 0], fn_ref[...])

        @pl.when(k == 0)
        def _():
            acc[...] = part

        @pl.when(k > 0)
        def _():
            acc[...] += part

        @pl.when(k == nk - 1)
        def _():
            dw_ref[0, 0] = acc[...].astype(dw_ref.dtype)

    return _pcall(
        body, name="ffn_dw_gate_up", grid=(N_DEV, 2, nk),
        out_shape=_sds((N_DEV, 2, f, d), WIRE),
        in_specs=[pl.BlockSpec((tk, d), lambda j, w, k: (k, 0)),
                  pl.BlockSpec((1, 1, tk, f), lambda j, w, k: (j, w, k, 0))],
        out_specs=pl.BlockSpec((1, 1, f, d), lambda j, w, k: (j, w, 0, 0)),
        scratch=[pltpu.VMEM((f, d), jnp.float32)], sem=("parallel", "parallel", "arbitrary"))(fn, dgu)


def _ffn_dfn(dgu, w_gu, after):
    _, _, s, f = dgu.shape
    d = w_gu.shape[3]
    tm = min(FFN_TILE, s)

    def body(dgu_ref, w_ref, dfn_ref):
        j = pl.program_id(1)
        part = (_dot(dgu_ref[0, 0], w_ref[0, 0]) + _dot(dgu_ref[0, 1], w_ref[0, 1])
                + _dot(dgu_ref[1, 0], w_ref[1, 0]) + _dot(dgu_ref[1, 1], w_ref[1, 1]))

        @pl.when(j == 0)
        def _():
            dfn_ref[...] = part

        @pl.when(j > 0)
        def _():
            dfn_ref[...] += part

    return _pcall(
        body, name="ffn_dfn", grid=(s // tm, N_DEV // 2),
        out_shape=_sds((s, d), jnp.float32),
        in_specs=[pl.BlockSpec((2, 2, tm, f), lambda i, j: (j, 0, i, 0)),
                  pl.BlockSpec((2, 2, f, d), lambda i, j: (j, 0, 0, 0))],
        out_specs=pl.BlockSpec((tm, d), lambda i, j: (i, 0)),
        sem=("parallel", "arbitrary"), after=after)(dgu, w_gu)


def _ffn_norm_bwd(d_fn, dout, h1, g_ffn):
    s, d = h1.shape
    tm = min(2 * ROW_TILE, s)

    def body(dfn_ref, do_ref, h1_ref, g_ref, dh1_ref, dg_ref):
        i = pl.program_id(0)

        @pl.when(i == 0)
        def _():
            dg_ref[...] = jnp.zeros_like(dg_ref)

        _, xn, r = _norm_fwd(h1_ref[...], g_ref[...])
        dx, dg = _norm_bwd(xn, r, g_ref[...], dfn_ref[...])
        dh1_ref[...] = do_ref[...] + dx
        dg_ref[...] += dg

    row = pl.BlockSpec((tm, d), lambda i: (i, 0))
    vec = pl.BlockSpec((1, d), lambda i: (0, 0))
    return _pcall(
        body, name="ffn_norm_bwd", grid=(s // tm,),
        out_shape=[_sds((s, d), jnp.float32), _sds((1, d), jnp.float32)],
        in_specs=[row, row, row, vec], out_specs=[row, vec], sem=("arbitrary",))(d_fn, dout, h1, g_ffn)


def _mem_bwd(proj, g_mq, km, vmm, d_y, y_m, lse):
    s = proj.shape[0]
    ml, hw = km.shape
    tm = min(FFN_TILE, s)
    scale = MEM_DIM ** -0.5

    def body(q_ref, g_ref, k_ref, v_ref, do_ref, y_ref, lse_ref, dq_ref, dk_ref, dv_ref, dg_ref):
        i = pl.program_id(0)

        @pl.when(i == 0)
        def _():
            dk_ref[...] = jnp.zeros_like(dk_ref)
            dv_ref[...] = jnp.zeros_like(dv_ref)
            dg_ref[...] = jnp.zeros_like(dg_ref)

        col = lax.broadcasted_iota(jnp.int32, (tm, MEM_HEADS), 1)
        lse_t = lse_ref[...]
        for h in range(MEM_HEADS):
            sl = slice(MEM_DIM * h, MEM_DIM * (h + 1))
            qn, xn, r = _norm_fwd(q_ref[:, sl], g_ref[...])
            lse_h = jnp.sum(jnp.where(col == h, lse_t, 0.0), -1, keepdims=True)
            p = jnp.exp(_dot_nt(qn, k_ref[:, sl]) * scale - lse_h)
            do = do_ref[:, sl]
            dd = jnp.sum(do * y_ref[:, sl], -1, keepdims=True)
            dp = _dot_nt(do, v_ref[:, sl])
            ds = (p * (dp - dd)).astype(MXU)
            dv_ref[:, sl] += _dot_tn(p, do)
            dk_ref[:, sl] += _dot_tn(ds, qn) * scale
            dx, dg = _norm_bwd(xn, r, g_ref[...], _dot(ds, k_ref[:, sl]) * scale)
            dq_ref[:, sl] = dx.astype(dq_ref.dtype)
            dg_ref[...] += dg

    full = pl.BlockSpec((ml, hw), lambda i: (0, 0))
    return _pcall(
        body, name="mem_bwd", grid=(s // tm,),
        out_shape=[_sds((s, hw), MXU), _sds((ml, hw), jnp.float32), _sds((ml, hw), jnp.float32),
                   _sds((1, MEM_DIM), jnp.float32)],
        in_specs=[pl.BlockSpec((tm, hw), lambda i: (i, C_QM // hw)), pl.BlockSpec((1, MEM_DIM), lambda i: (0, 0)),
                  full, full, pl.BlockSpec((tm, hw), lambda i: (i, 3)), pl.BlockSpec((tm, hw), lambda i: (i, 0)),
                  pl.BlockSpec((tm, MEM_HEADS), lambda i: (i, 0))],
        out_specs=[pl.BlockSpec((tm, hw), lambda i: (i, 0)), full, full,
                   pl.BlockSpec((1, MEM_DIM), lambda i: (0, 0))],
        sem=("arbitrary",))(proj, g_mq, km, vmm, d_y, y_m, lse)


def _memkv_bwd(mem, g_mem, w_mkv, g_mk, kv, memn, dk, dv):
    ml, d = mem.shape
    hw = MEM_HEADS * MEM_DIM

    def body(mem_ref, g_ref, w_ref, gk_ref, kv_ref, mn_ref, dk_ref, dv_ref, dw_ref, dgm_ref, dgk_ref):
        parts = []
        dgk = jnp.zeros((1, MEM_DIM), jnp.float32)
        for h in range(MEM_HEADS):
            sl = slice(MEM_DIM * h, MEM_DIM * (h + 1))
            _, xn, r = _norm_fwd(kv_ref[:, sl], gk_ref[...])
            dx, dg = _norm_bwd(xn, r, gk_ref[...], dk_ref[:, sl])
            parts.append(dx)
            dgk = dgk + dg
        dkv = jnp.concatenate(parts + [dv_ref[...]], axis=1).astype(MXU)
        dgk_ref[...] = dgk
        dw_ref[...] = _dot_tn(mn_ref[...], dkv).astype(dw_ref.dtype)
        d_mn = _dot_nt(dkv, w_ref[...])
        _, xn, _ = _norm_fwd(mem_ref[...], g_ref[...])
        dgm_ref[...] = jnp.sum(d_mn * xn, 0, keepdims=True)

    vm = pl.BlockSpec(memory_space=pltpu.VMEM)
    return _pcall(
        body, name="memkv_bwd",
        out_shape=[_sds((d, 2 * hw), WIRE), _sds((1, d), jnp.float32), _sds((1, MEM_DIM), jnp.float32)],
        in_specs=[vm] * 8, out_specs=[vm] * 3)(mem, g_mem, w_mkv, g_mk, kv, memn, dk, dv)


def _mla_bwd(qc, kc, v, d_y, y_b, lse, after):
    nh, s, _ = qc.shape
    t = min(ATT_TILE, s)
    nb = s // t
    scale = (MLA_NOPE + MLA_ROPE) ** -0.5

    def body(q_ref, k_ref, v_ref, do_ref, y_ref, lse_ref, dq_ref, dk_ref, dv_ref, dk_acc, dv_acc):
        kj, qi = pl.program_id(1), pl.program_id(2)

        @pl.when((kj == 0) & (qi == 0))
        def _():
            dq_ref[...] = jnp.zeros_like(dq_ref)

        @pl.when(qi == kj)
        def _():
            dk_acc[...] = jnp.zeros_like(dk_acc)
            dv_acc[...] = jnp.zeros_like(dv_acc)

        def step(diagonal):
            rc = t // 4 if diagonal else t
            for c in range(t // rc):
                rows = slice(rc * c, rc * (c + 1))
                keys = slice(0, rc * (c + 1))
                q, k = q_ref[0, rows, :], k_ref[0, keys, :]
                sc = _dot_nt(q, k) * (scale * LOG2E)
                if diagonal:
                    r_i = lax.broadcasted_iota(jnp.int32, sc.shape, 0) + rc * c
                    c_i = lax.broadcasted_iota(jnp.int32, sc.shape, 1)
                    sc = jnp.where(c_i <= r_i, sc, NEG_INF)
                p = jnp.exp2(sc - lse_ref[0, rows, :])
                do = do_ref[rows, :]
                dd = jnp.sum(do * y_ref[rows, :], -1, keepdims=True)
                dp = _dot_nt(do, v_ref[0, keys, :])
                ds = (p * (dp - dd) * scale).astype(MXU)
                dv_acc[keys, :] += _dot_tn(p, do)
                dk_acc[keys, :] += _dot_tn(ds, q)
                out_rows = pl.ds(pl.multiple_of(qi * t + rc * c, rc), rc)
                dq_ref[0, out_rows, :] += _dot(ds, k)

        @pl.when(qi > kj)
        def _():
            step(False)

        @pl.when(qi == kj)
        def _():
            step(True)

        @pl.when(qi == nb - 1)
        def _():
            dk_ref[0] = dk_acc[...]
            dv_ref[0] = dv_acc[...]

    qmap = lambda h, j, i: (h, jnp.maximum(i, j), 0)
    return _pcall(
        body, name="mla_bwd", grid=(nh, nb, nb),
        out_shape=[_sds((nh, s, 256), jnp.float32), _sds((nh, s, 256), jnp.float32),
                   _sds((nh, s, MLA_V), jnp.float32)],
        in_specs=[pl.BlockSpec((1, t, 256), qmap),
                  pl.BlockSpec((1, t, 256), lambda h, j, i: (h, j, 0)),
                  pl.BlockSpec((1, t, MLA_V), lambda h, j, i: (h, j, 0)),
                  pl.BlockSpec((t, MLA_V), lambda h, j, i: (jnp.maximum(i, j), 8 + h)),
                  pl.BlockSpec((t, MLA_V), lambda h, j, i: (jnp.maximum(i, j), h)),
                  pl.BlockSpec((1, t, 1), qmap)],
        out_specs=[pl.BlockSpec((1, s, 256), lambda h, j, i: (h, 0, 0)),
                   pl.BlockSpec((1, t, 256), lambda h, j, i: (h, j, 0)),
                   pl.BlockSpec((1, t, MLA_V), lambda h, j, i: (h, j, 0))],
        scratch=[pltpu.VMEM((t, 256), jnp.float32), pltpu.VMEM((t, MLA_V), jnp.float32)],
        sem=("parallel", "arbitrary", "arbitrary"), after=after)(qc, kc, v, d_y, y_b, lse)


def _mla_prep_bwd(proj, cos, sin, g_cq, g_ckv, w_uq, w_ukv, g_qn, g_qr, g_kn, g_kr,
                  qb, kvb, cqn, ckvn, dqc, dkc, dv):
    s = proj.shape[0]
    tm = min(ROW_TILE, s)
    nh = MLA_HEADS
    ni = s // tm

    def body(cq_ref, ckv_ref, kr_ref, cos_ref, sin_ref, gcq_ref, gckv_ref, wuq_ref, wukv_ref,
             gqn_ref, gqr_ref, gkn_ref, gkr_ref, qb_ref, kvb_ref, cqn_ref, ckvn_ref, dqc_ref, dkc_ref, dv_ref,
             dcq_ref, dckv_ref, dkr_ref, dwuq_ref, dwukv_ref,
             dgcq_ref, dgckv_ref, dgqn_ref, dgqr_ref, dgkn_ref, dgkr_ref, acc_uq, acc_ukv):
        i = pl.program_id(0)

        @pl.when(i == 0)
        def _():
            acc_uq[...] = jnp.zeros_like(acc_uq)
            acc_ukv[...] = jnp.zeros_like(acc_ukv)
            for ref in (dgcq_ref, dgckv_ref, dgqn_ref, dgqr_ref, dgkn_ref, dgkr_ref):
                ref[...] = jnp.zeros_like(ref)

        cos_t, sin_t = cos_ref[...], sin_ref[...]
        lo = _lo_mask((tm, LANES))
        qb_v, kvb_v = qb_ref[...], kvb_ref[...]
        dq_parts, dgqn = [], jnp.zeros((1, LANES), jnp.float32)
        for h in range(nh):
            _, xn, r = _norm_fwd(qb_v[:, MLA_NOPE * h: MLA_NOPE * (h + 1)], gqn_ref[...])
            dx, dg = _norm_bwd(xn, r, gqn_ref[...], dqc_ref[h][:, :MLA_NOPE])
            dq_parts.append(dx)
            dgqn = dgqn + dg
        dgqn_ref[...] += dgqn
        dgqr = jnp.zeros((1, LANES), jnp.float32)
        for j in range(nh // 2):
            d_rope = jnp.where(lo, dqc_ref[2 * j][:, MLA_NOPE:], dqc_ref[2 * j + 1][:, MLA_NOPE:])
            d_pre = _rope_bwd(d_rope, cos_t, sin_t)
            xr = qb_v[:, nh * MLA_NOPE + LANES * j: nh * MLA_NOPE + LANES * (j + 1)]
            _, xn, r = _norm_fwd(xr, gqr_ref[...], half=True)
            dx, dg = _norm_bwd(xn, r, gqr_ref[...], d_pre, half=True)
            dq_parts.append(dx)
            dgqr = dgqr + dg
        dgqr_ref[...] += dgqr
        dqb = jnp.concatenate(dq_parts, axis=1).astype(MXU)
        acc_uq[...] += _dot_tn(dqb, cqn_ref[...])
        _, xn, r = _norm_fwd(cq_ref[...], gcq_ref[...])
        dx, dg = _norm_bwd(xn, r, gcq_ref[...], _dot(dqb, wuq_ref[...]))
        dcq_ref[...] = dx.astype(dcq_ref.dtype)
        dgcq_ref[...] += dg
        dkv_parts, dgkn = [], jnp.zeros((1, LANES), jnp.float32)
        d_kr2 = jnp.zeros((tm, LANES), jnp.float32)
        for h in range(nh):
            _, xn, r = _norm_fwd(kvb_v[:, 256 * h: 256 * h + MLA_NOPE], gkn_ref[...])
            dx, dg = _norm_bwd(xn, r, gkn_ref[...], dkc_ref[h][:, :MLA_NOPE])
            dkv_parts += [dx, dv_ref[h]]
            dgkn = dgkn + dg
            d_kr2 = d_kr2 + dkc_ref[h][:, MLA_NOPE:]
        dgkn_ref[...] += dgkn
        dkvb = jnp.concatenate(dkv_parts, axis=1).astype(MXU)
        part_ukv = _dot_tn(ckvn_ref[...], dkvb)
        for dev in range(N_DEV):
            acc_ukv[dev] += part_ukv[:, LANES * dev: LANES * (dev + 1)]
        w_ukv_full = jnp.concatenate([wukv_ref[dev] for dev in range(N_DEV)], axis=1)
        d_ckvn = _dot_nt(dkvb, w_ukv_full)
        _, xn, r = _norm_fwd(ckv_ref[...], gckv_ref[...])
        dx, dg = _norm_bwd(xn, r, gckv_ref[...], d_ckvn)
        dckv_ref[...] = dx.astype(dckv_ref.dtype)
        dgckv_ref[...] += dg
        d_kr = jnp.where(lo, d_kr2 + pltpu.roll(d_kr2, 64, 1), 0.0)
        d_pre = _rope_bwd(d_kr, cos_t, sin_t)
        _, xn, r = _norm_fwd(kr_ref[...], gkr_ref[...], half=True)
        dx, dg = _norm_bwd(xn, r, gkr_ref[...], d_pre, half=True)
        dkr_ref[...] = jnp.where(lo, dx, 0.0).astype(dkr_ref.dtype)
        dgkr_ref[...] += jnp.where(_lo_mask((1, LANES)), dg, 0.0)

        @pl.when(i == ni - 1)
        def _():
            dwuq_ref[...] = acc_uq[...].astype(dwuq_ref.dtype)
            dwukv_ref[...] = acc_ukv[...].astype(dwukv_ref.dtype)

    def col(width, start):
        return pl.BlockSpec((tm, width), lambda i: (i, start // width))

    def full(shape):
        return pl.BlockSpec(shape, lambda i: (0,) * len(shape))

    def row(width):
        return pl.BlockSpec((tm, width), lambda i: (i, 0))

    def heads(width):
        return pl.BlockSpec((nh, tm, width), lambda i: (0, i, 0))

    vec = full((1, LANES))
    return _pcall(
        body, name="mla_prep_bwd", grid=(ni,),
        out_shape=[_sds((s, 512), MXU), _sds((s, 512), MXU), _sds((s, LANES), MXU),
                   _sds((768, 512), WIRE), _sds((N_DEV, 512, LANES), WIRE),
                   _sds((1, 512), jnp.float32), _sds((1, 512), jnp.float32)] + [_sds((1, LANES), jnp.float32)] * 4,
        in_specs=[col(512, C_CQ), col(512, C_CKV), col(LANES, C_KR), row(LANES), row(LANES),
                  full((1, 512)), full((1, 512)), full((768, 512)), full((N_DEV, 512, LANES)), vec, vec, vec, vec,
                  row(768), row(1024), row(512), row(512), heads(256), heads(256), heads(MLA_V)],
        out_specs=[row(512), row(512), row(LANES), full((768, 512)), full((N_DEV, 512, LANES)),
                   full((1, 512)), full((1, 512)), vec, vec, vec, vec],
        scratch=[pltpu.VMEM((768, 512), jnp.float32), pltpu.VMEM((N_DEV, 512, LANES), jnp.float32)],
        sem=("arbitrary",))(proj, proj, proj, cos, sin, g_cq, g_ckv, w_uq, w_ukv, g_qn, g_qr, g_kn, g_kr,
                            qb, kvb, cqn, ckvn, dqc, dkc, dv)


def _swa_bwd(proj, posc, posr, gq, gk, sinks, d_y, y_a, lse, after):
    s = proj.shape[0]
    b = SWA_BLOCK
    nb = s // b
    scale = SWA_DIM ** -0.5

    def body(q_ref, kp_ref, kc_ref, vp_ref, vc_ref, pq_ref, pkp_ref, pkc_ref, gq_ref, gk_ref, sink_ref,
             do_ref, y_ref, lse_ref, kfull_ref,
             dq_ref, dk_ref, dv_ref, dgq_ref, dgk_ref, dsink_ref, dk_acc, dv_acc):
        n = pl.program_id(0)

        @pl.when(n == 0)
        def _():
            dk_acc[...] = jnp.zeros_like(dk_acc)
            dv_acc[...] = jnp.zeros_like(dv_acc)
            dgq_ref[...] = jnp.zeros_like(dgq_ref)
            dsink_ref[...] = jnp.zeros_like(dsink_ref)

        kn, v, bias = _swa_common(n, kp_ref[...], kc_ref[...], vp_ref[...], vc_ref[...],
                                  pq_ref[...], pkp_ref[...], pkc_ref[...], gk_ref[...])
        lo = _lo_mask((b, LANES))
        col = lax.broadcasted_iota(jnp.int32, (b, SWA_Q_HEADS), 1)
        col1 = lax.broadcasted_iota(jnp.int32, (1, SWA_Q_HEADS), 1)
        lse_t = lse_ref[...]
        dk_blk = jnp.zeros((2 * b, LANES), jnp.float32)
        dv_blk = jnp.zeros((2 * b, LANES), jnp.float32)
        dgq = jnp.zeros((1, LANES), jnp.float32)
        dsink = jnp.zeros((1, SWA_Q_HEADS), jnp.float32)
        for j in range(SWA_Q_HEADS // 2):
            hk = (2 * j) // (SWA_Q_HEADS // SWA_KV_HEADS)
            kvmask = lo if hk == 0 else jnp.logical_not(lo)
            sl = slice(LANES * j, LANES * (j + 1))
            qn, xn, r = _norm_fwd(q_ref[:, sl], gq_ref[...], half=True)
            qn = qn * scale
            qsw = pltpu.roll(qn, 64, 1)
            d2 = do_ref[:, sl]
            d2sw = pltpu.roll(d2, 64, 1)
            prod = d2 * y_ref[:, sl]
            dqs = []
            for e in range(2):
                h = 2 * j + e
                half_e = lo if e == 0 else jnp.logical_not(lo)
                qm = jnp.where(kvmask, qn if e == hk else qsw, 0.0)
                dm = jnp.where(kvmask, d2 if e == hk else d2sw, 0.0)
                sc = _dot_nt(qm, kn) + _alibi_slope(h) * bias
                lse_h = jnp.sum(jnp.where(col == h, lse_t, 0.0), -1, keepdims=True)
                p = jnp.exp(sc - lse_h)
                dd = jnp.sum(jnp.where(half_e, prod, 0.0), -1, keepdims=True)
                dp = _dot_nt(dm, v)
                ds = (p * (dp - dd)).astype(MXU)
                dsink = dsink - jnp.where(col1 == h, jnp.sum(jnp.exp(sink_ref[h] - lse_h) * dd), 0.0)
                dq_m = _dot(ds, kn) * scale
                dk_blk = dk_blk + _dot_tn(ds, qm)
                dv_blk = dv_blk + _dot_tn(p, dm)
                dqs.append(dq_m if e == hk else pltpu.roll(dq_m, 64, 1))
            dx, dg = _norm_bwd(xn, r, gq_ref[...], jnp.where(lo, dqs[0], dqs[1]), half=True)
            dq_ref[:, sl] = dx.astype(dq_ref.dtype)
            dgq = dgq + dg
        dgq_ref[...] += dgq
        dsink_ref[...] += dsink
        prev = pl.ds(pl.multiple_of(jnp.maximum(n - 1, 0) * b, b), b)
        cur = pl.ds(pl.multiple_of(n * b, b), b)
        dk_acc[prev, :] += dk_blk[:b]
        dv_acc[prev, :] += dv_blk[:b]
        dk_acc[cur, :] += dk_blk[b:]
        dv_acc[cur, :] += dv_blk[b:]

        @pl.when(n == nb - 1)
        def _():
            _, kxn, kr = _norm_fwd(kfull_ref[...], gk_ref[...], half=True)
            dx, dg = _norm_bwd(kxn, kr, gk_ref[...], dk_acc[...], half=True)
            dk_ref[...] = dx.astype(dk_ref.dtype)
            dv_ref[...] = dv_acc[...].astype(dv_ref.dtype)
            dgk_ref[...] = dg

    full = pl.BlockSpec((s, LANES), lambda n: (0, 0))
    vec = pl.BlockSpec((1, LANES), lambda n: (0, 0))
    return _pcall(
        body, name="swa_bwd", grid=(nb,),
        out_shape=[_sds((s, 1024), MXU), _sds((s, LANES), MXU), _sds((s, LANES), MXU),
                   _sds((1, LANES), jnp.float32), _sds((1, LANES), jnp.float32),
                   _sds((1, SWA_Q_HEADS), jnp.float32)],
        in_specs=_swa_specs(s) + [pl.BlockSpec((b, 1024), lambda n: (n, 0)), pl.BlockSpec((b, 1024), lambda n: (n, 0)),
                                  pl.BlockSpec((b, SWA_Q_HEADS), lambda n: (n, 0)),
                                  pl.BlockSpec((s, LANES), lambda n: (0, C_KA // LANES))],
        out_specs=[pl.BlockSpec((b, 1024), lambda n: (n, 0)), full, full, vec, vec,
                   pl.BlockSpec((1, SWA_Q_HEADS), lambda n: (0, 0))],
        scratch=[pltpu.VMEM((s, LANES), jnp.float32), pltpu.VMEM((s, LANES), jnp.float32)],
        sem=("arbitrary",), after=after)(proj, proj, proj, proj, proj, posc, posr, posr, gq, gk, sinks, d_y, y_a, lse,
                                         proj)


def _dx(d_proj, w_in, x, g, d_h1, after):
    s, d = x.shape
    n = w_in.shape[0]
    tm = min(2 * ROW_TILE, s)

    n_pc = len(d_proj)

    def body(*refs):
        dp_refs, (w_ref, x_ref, g_ref, dh_ref, dx_ref, dg_ref) = refs[:n_pc], refs[n_pc:]
        i = pl.program_id(0)

        @pl.when(i == 0)
        def _():
            dg_ref[...] = jnp.zeros_like(dg_ref)

        d_hn = _dot(jnp.concatenate([r[...] for r in dp_refs], axis=1), w_ref[...])
        _, xn, r = _norm_fwd(x_ref[...], g_ref[...])
        dx, dg = _norm_bwd(xn, r, g_ref[...], d_hn)
        dx_ref[...] = dh_ref[...] + dx
        dg_ref[...] += dg

    row = pl.BlockSpec((tm, d), lambda i: (i, 0))
    vec = pl.BlockSpec((1, d), lambda i: (0, 0))
    return _pcall(
        body, name="grad_x", grid=(s // tm,),
        out_shape=[_sds((s, d), jnp.float32), _sds((1, d), jnp.float32)],
        in_specs=[pl.BlockSpec((tm, p.shape[1]), lambda i: (i, 0)) for p in d_proj] + [
                  pl.BlockSpec((n, d), lambda i: (0, 0), pipeline_mode=pl.Buffered(1)), row, vec, row],
        out_specs=[row, vec], sem=("arbitrary",), after=after)(*d_proj, w_in, x, g, d_h1)


_SMALL = ["attn_norm_g", "swa_q_norm_g", "swa_k_norm_g", "swa_sinks", "mla_cq_norm_g", "mla_ckv_norm_g",
          "mla_qn_norm_g", "mla_qr_norm_g", "mla_kn_norm_g", "mla_kr_norm_g", "mem_norm_g",
          "mem_q_norm_g", "mem_k_norm_g", "ffn_norm_g"]


def kernel(x, mem, positions, attn_norm_g, w_in, swa_q_norm_g, swa_k_norm_g, swa_sinks, mla_cq_norm_g, mla_ckv_norm_g, w_uq, w_ukv, mla_qn_norm_g, mla_qr_norm_g, mla_kn_norm_g, mla_kr_norm_g, mem_norm_g, w_mem_kv, mem_q_norm_g, mem_k_norm_g, w_out, ffn_norm_g, w_gate, w_up, w_down, loss_target, m_attn_norm_g, m_w_in, m_swa_q_norm_g, m_swa_k_norm_g, m_swa_sinks, m_mla_cq_norm_g, m_mla_ckv_norm_g, m_w_uq, m_w_ukv, m_mla_qn_norm_g, m_mla_qr_norm_g, m_mla_kn_norm_g, m_mla_kr_norm_g, m_mem_norm_g, m_w_mem_kv, m_mem_q_norm_g, m_mem_k_norm_g, m_w_out, m_ffn_norm_g, m_w_gate, m_w_up, m_w_down, v_attn_norm_g, v_w_in, v_swa_q_norm_g, v_swa_k_norm_g, v_swa_sinks, v_mla_cq_norm_g, v_mla_ckv_norm_g, v_w_uq, v_w_ukv, v_mla_qn_norm_g, v_mla_qr_norm_g, v_mla_kn_norm_g, v_mla_kr_norm_g, v_mem_norm_g, v_w_mem_kv, v_mem_q_norm_g, v_mem_k_norm_g, v_w_out, v_ffn_norm_g, v_w_gate, v_w_up, v_w_down):
    args = dict(locals())
    x2, mem2, tgt = x[0], mem[0], loss_target[0]
    s, d = x2.shape
    n_in = w_in.shape[2]
    f = w_gate.shape[2]

    (g_in,) = _all_gather([w_in[0].T.astype(WIRE)])
    mix_shards = [w_uq[0].T.astype(WIRE), w_ukv[0].astype(WIRE), w_mem_kv[0].astype(WIRE),
                  _to_wire([w_out[0]], g_in, "wire_out")[0]]
    g_uq, wkv, g_mkv, g_out = _all_gather_background(mix_shards, 5, "all_gather_mix_weights")
    ffn_shards = [_to_wire([w_gate[0].T, w_up[0].T], g_in, "wire_gate_up"),
                  _to_wire([w_down[0]], g_in, "wire_down")[0]]
    w_gu, w_d = _all_gather_background(ffn_shards, 1, "all_gather_ffn_weights")
    wi = g_in.reshape(N_DEV * n_in, d)
    wi = jnp.concatenate([wi[0:1024], wi[1280:1792], wi[1792:2304], wi[2368:2880],
                          wi[1024:1152], wi[1152:1280], wi[2304:2368],
                          jnp.zeros((IN_PAD - 2880, d), wi.dtype)], axis=0)
    wq = g_uq.reshape(768, 512)
    wq = jnp.concatenate([wq[192 * h: 192 * h + 128] for h in range(4)]
                         + [wq[192 * h + 128: 192 * (h + 1)] for h in range(4)], axis=0)
    wmkv = g_mkv.reshape(-1, g_mkv.shape[-1])
    wo = g_out.reshape(-1, d)

    pos = positions[0].astype(jnp.float32)
    inv_freq = ROPE_THETA ** (-jnp.arange(0, MLA_ROPE, 2, dtype=jnp.float32) / MLA_ROPE)
    ang = pos[:, None] * inv_freq
    cos32, sin32 = jnp.cos(ang), jnp.sin(ang)
    cos_t = jnp.tile(cos32, (1, 4))
    sin_t = jnp.tile(jnp.concatenate([-sin32, sin32], axis=1), (1, 2))
    posc, posr = pos.reshape(s, 1), pos.reshape(1, s)
    two = lambda g: jnp.tile(g, (1, 2))
    gq2, gk2, gqr2, gkr2 = two(swa_q_norm_g), two(swa_k_norm_g), two(mla_qr_norm_g), two(mla_kr_norm_g)
    sinks1 = swa_sinks[0]

    proj, hn = _in_proj(x2, attn_norm_g, wi)
    qc, kc, vb, qb, kvb, cqn, ckvn = _mla_prep(proj, cos_t, sin_t, mla_cq_norm_g, mla_ckv_norm_g, wq, wkv,
                                                mla_qn_norm_g, gqr2, mla_kn_norm_g, gkr2)
    y_b, lse_b = _mla_fwd(qc, kc, vb)
    km, vmm, kvm, memn = _memkv_prep(mem2, mem_norm_g, wmkv, mem_k_norm_g)
    y_m, lse_m = _mem_fwd(proj, mem_q_norm_g, km, vmm)
    y_a, lse_a = _swa_fwd(proj, posc, posr, gq2, gk2, sinks1)
    h1, fn = _out_proj(y_a, y_b, y_m, x2, wo, ffn_norm_g)
    gu, act = _ffn_gu(fn, w_gu)
    dout, dout_b, loss_tile = _ffn_down(act, w_d, h1, tgt)

    dgu, dw_d = _ffn_bwd_act(dout_b, w_d, gu)
    dw_gu = _ffn_dw_gu(fn, dgu)
    r_gu, r_d = _exchange_grads_background([dw_gu, dw_d], 2, "exchange_ffn_grads")
    d_h1, dg_ffn = _ffn_norm_bwd(_ffn_dfn(dgu, w_gu, dw_gu), dout, h1, ffn_norm_g)
    d_y = _mm(d_h1, wo, tb=True, out_dtype=jnp.float32, tm=FFN_TILE, tk=2048, name="d_mix")
    dw_out = jnp.concatenate([
        _mm(y_a, d_h1, ta=True, out_dtype=WIRE, tm=1024, tk=1024, name="dw_out_a"),
        _mm(y_b, d_h1, ta=True, out_dtype=WIRE, tm=1024, tk=1024, name="dw_out_b"),
        _mm(y_m, d_h1, ta=True, out_dtype=WIRE, tm=1024, tk=1024, name="dw_out_m")], axis=0)
    d_qm, dkm, dvmm, dg_mq = _mem_bwd(proj, mem_q_norm_g, km, vmm, d_y, y_m, lse_m)
    dw_mkv, dg_mem, dg_mk = _memkv_bwd(mem2, mem_norm_g, wmkv, mem_k_norm_g, kvm, memn, dkm, dvmm)
    r_mkv, r_out = _exchange_grads_background([dw_mkv.reshape(g_mkv.shape), dw_out.reshape(g_out.shape)], 3,
                                              "exchange_mix_grads")
    dqc, dkc, dvb = _mla_bwd(qc, kc, vb, d_y, y_b, lse_b, dw_mkv)
    (d_cq, d_ckv, d_kr, dw_uq, dw_ukv, dg_cq, dg_ckv, dg_qn, dg_qr, dg_kn, dg_kr) = _mla_prep_bwd(
        proj, cos_t, sin_t, mla_cq_norm_g, mla_ckv_norm_g, wq, wkv, mla_qn_norm_g, gqr2, mla_kn_norm_g, gkr2,
        qb, kvb, cqn, ckvn, dqc, dkc, dvb)
    d_qa, d_ka, d_va, dg_q, dg_k, d_sinks = _swa_bwd(proj, posc, posr, gq2, gk2, sinks1, d_y, y_a, lse_a, dw_out)
    d_proj = [d_qa, d_cq, d_ckv, d_qm, d_ka, d_va, d_kr]
    gi = _dw_in(hn, d_proj, n_in)

    gq_ = jnp.concatenate(sum([[dw_uq[128 * h: 128 * (h + 1)], dw_uq[512 + 64 * h: 512 + 64 * (h + 1)]]
                               for h in range(4)], []), axis=0)
    gq_ = gq_.reshape(N_DEV, 96, 512)
    r_in, r_uq, r_ukv = _exchange_grads_background([gi, gq_, dw_ukv], 4, "exchange_in_grads")
    grad_x, dg_attn = _dx(d_proj, wi, x2, attn_norm_g, d_h1, gi)

    big = {}
    last = [None]

    def adam(name, r, transposed=False, which=None):
        w, m, v = args[name][0], args["m_" + name][0], args["v_" + name][0]
        if transposed:
            outs = _adam_big(r, w.T, m.T, v.T, "adam_" + name, last[0], which)
            big[name] = [o.T[None] for o in outs]
        else:
            outs = _adam_big(r, w, m, v, "adam_" + name, last[0])
            big[name] = [o[None] for o in outs]
        last[0] = outs[0]

    adam("w_gate", r_gu, True, which=0)
    adam("w_up", r_gu, True, which=1)
    adam("w_down", r_d)
    adam("w_out", r_out)
    adam("w_mem_kv", r_mkv)
    adam("w_in", r_in, True)
    adam("w_uq", r_uq, True)
    adam("w_ukv", r_ukv)

    small_g = {
        "attn_norm_g": dg_attn, "swa_q_norm_g": dg_q, "swa_k_norm_g": dg_k,
        "swa_sinks": d_sinks, "mla_cq_norm_g": dg_cq, "mla_ckv_norm_g": dg_ckv, "mla_qn_norm_g": dg_qn,
        "mla_qr_norm_g": dg_qr, "mla_kn_norm_g": dg_kn, "mla_kr_norm_g": dg_kr,
        "mem_norm_g": dg_mem, "mem_q_norm_g": dg_mq, "mem_k_norm_g": dg_mk, "ffn_norm_g": dg_ffn}
    packed_g = _small_allreduce([small_g[n] for n in _SMALL], loss_tile, [args[n].shape[-1] for n in _SMALL])
    loss11, small_out = _small_adam(packed_g, [args[n] for n in _SMALL],
                                    [args["m_" + n] for n in _SMALL], [args["v_" + n] for n in _SMALL])
    small = dict(zip(_SMALL, small_out))
    loss = loss11.reshape(())

    order = ["attn_norm_g", "w_in", "swa_q_norm_g", "swa_k_norm_g", "swa_sinks", "mla_cq_norm_g", "mla_ckv_norm_g",
             "w_uq", "w_ukv", "mla_qn_norm_g", "mla_qr_norm_g", "mla_kn_norm_g", "mla_kr_norm_g", "mem_norm_g",
             "w_mem_kv", "mem_q_norm_g", "mem_k_norm_g", "w_out", "ffn_norm_g", "w_gate", "w_up", "w_down"]
    res = {n: (big[n] if n in big else list(small[n])) for n in order}
    outs = [loss, grad_x[None]]
    for kind in range(4):
        outs += [res[n][kind] for n in order]
    return tuple(outs)
```

```python
import jax
import jax.numpy as jnp
from jax import lax
from jax.experimental import pallas as pl
from jax.experimental.pallas import tpu as pltpu
from jax.experimental.pallas import tpu_sc as plsc

MXU = jnp.bfloat16
WIRE = jnp.bfloat16
EPS = 1e-6
NEG_INF = -1e30
LOG2E = 1.4426950408889634
N_DEV = 8
LANES = 128
ROW_TILE = 256
FFN_TILE = 512
ATT_TILE = 1024
SWA_BLOCK = 128
VMEM_LIMIT = 56 * 1024 * 1024

SWA_Q_HEADS, SWA_KV_HEADS, SWA_DIM = 16, 2, 64
MLA_HEADS, MLA_NOPE, MLA_ROPE, MLA_V = 4, 128, 64, 128
MEM_HEADS, MEM_DIM = 4, 128
ROPE_THETA = 10000.0
ADAM_LR, ADAM_B1, ADAM_B2, ADAM_EPS, ADAM_WD, ADAM_STEP = 0.001, 0.9, 0.999, 1e-08, 0.01, 10

C_QA, C_CQ, C_CKV, C_QM, C_KA, C_VA, C_KR, IN_PAD = 0, 1024, 1536, 2048, 2560, 2688, 2816, 2944


def _pcall(body, *, name, out_shape, in_specs, out_specs, grid=(), scratch=(), sem=None, after=None):
    params = pltpu.CompilerParams(dimension_semantics=sem, vmem_limit_bytes=VMEM_LIMIT)
    if after is not None:
        n_in, inner = len(in_specs), body

        def body(*refs):
            inner(*refs[:n_in], *refs[n_in + 1:])

        in_specs = list(in_specs) + [pl.BlockSpec(memory_space=pl.ANY)]
    call = pl.pallas_call(body, name=name, grid=grid, in_specs=in_specs, out_specs=out_specs,
                          out_shape=out_shape, scratch_shapes=list(scratch), compiler_params=params)
    return call if after is None else (lambda *ops: call(*ops, after))


def _sds(shape, dtype):
    return jax.ShapeDtypeStruct(tuple(shape), dtype)


def _dot(a, b):
    return jnp.dot(a.astype(MXU), b.astype(MXU), preferred_element_type=jnp.float32)


def _dot_nt(a, b):
    return lax.dot_general(a.astype(MXU), b.astype(MXU), (((1,), (1,)), ((), ())),
                           preferred_element_type=jnp.float32)


def _dot_tn(a, b):
    return lax.dot_general(a.astype(MXU), b.astype(MXU), (((0,), (0,)), ((), ())),
                           preferred_element_type=jnp.float32)


def _lo_mask(shape):
    return (lax.broadcasted_iota(jnp.int32, shape, len(shape) - 1) % LANES) < 64


def _norm_fwd(x, g, half=False):
    x2 = x * x
    if half:
        lo = _lo_mask(x.shape)
        s_lo = jnp.sum(jnp.where(lo, x2, 0.0), -1, keepdims=True)
        s_hi = jnp.sum(jnp.where(lo, 0.0, x2), -1, keepdims=True)
        r = jnp.where(lo, lax.rsqrt(s_lo / 64.0 + EPS), lax.rsqrt(s_hi / 64.0 + EPS))
    else:
        r = lax.rsqrt(jnp.mean(x2, -1, keepdims=True) + EPS)
    xn = x * r
    return xn * g, xn, r


def _norm_bwd(xn, r, g, dy, half=False):
    t = dy * g
    tx = t * xn
    if half:
        lo = _lo_mask(xn.shape)
        m_lo = jnp.sum(jnp.where(lo, tx, 0.0), -1, keepdims=True) / 64.0
        m_hi = jnp.sum(jnp.where(lo, 0.0, tx), -1, keepdims=True) / 64.0
        m = jnp.where(lo, m_lo, m_hi)
    else:
        m = jnp.mean(tx, -1, keepdims=True)
    dx = r * (t - xn * m)
    dg = jnp.sum(dy * xn, 0, keepdims=True)
    return dx, dg


def _swap32(x):
    lane = lax.broadcasted_iota(jnp.int32, x.shape, 1)
    return jnp.where((lane % 64) < 32, pltpu.roll(x, 96, 1), pltpu.roll(x, 32, 1))


def _rope(x, cos, sin):
    return x * cos + _swap32(x) * sin


def _rope_bwd(d, cos, sin):
    return d * cos + _swap32(d * sin)


def _my_coords():
    return lax.axis_index("x"), lax.axis_index("y"), lax.axis_index("c")


def _dev_index(px, py, pc):
    return 4 * px + 2 * py + pc


_FLIPS = [(0, 0, 1), (0, 1, 0), (0, 1, 1), (1, 0, 0), (1, 0, 1), (1, 1, 0), (1, 1, 1)]


def _flip(coords, f):
    return tuple((1 - v) if b else v for v, b in zip(coords, f))


def _all_gather(shards):
    n = len(shards)

    def body(*refs):
        ins, outs = refs[:n], refs[n:2 * n]
        send_sems, recv_sems, local_sems = refs[2 * n:]
        x, y, c = _my_coords()
        me, sibling = (x, y, c), (x, y, 1 - c)
        chips = [(1 - x, y), (x, 1 - y), (1 - x, 1 - y)]

        def copy(w, k, block, to, src=None):
            dst = outs[w].at[_dev_index(*block)]
            return pltpu.make_async_remote_copy(
                src_ref=dst if src is None else src, dst_ref=dst,
                send_sem=send_sems.at[w, k], recv_sem=recv_sems.at[w, k],
                device_id=to, device_id_type=pl.DeviceIdType.MESH)

        sends, locals_ = [], []
        for w in range(n):
            mine = pltpu.make_async_copy(ins[w], outs[w].at[_dev_index(*me)], local_sems.at[w])
            mine.start()
            locals_.append(mine)
            first = [copy(w, 0, me, sibling, src=ins[w])]
            first += [copy(w, 1 + j, me, (*chip, c), src=ins[w]) for j, chip in enumerate(chips)]
            for cp in first:
                cp.start()
            sends += first
        for w in range(n):
            for j, chip in enumerate(chips):
                copy(w, 1 + j, (*chip, c), me).wait_recv()
                fwd = copy(w, 4 + j, (*chip, c), sibling)
                fwd.start()
                sends.append(fwd)
        for w in range(n):
            copy(w, 0, sibling, me).wait_recv()
            for j, chip in enumerate(chips):
                copy(w, 4 + j, (*chip, 1 - c), me).wait_recv()
        for cp in sends:
            cp.wait_send()
        for mine in locals_:
            mine.wait()

    any_spec = pl.BlockSpec(memory_space=pl.ANY)
    return _pcall(
        body, name="all_gather_weights",
        out_shape=[_sds((N_DEV,) + s.shape, s.dtype) for s in shards],
        in_specs=[any_spec] * n, out_specs=[any_spec] * n,
        scratch=[pltpu.SemaphoreType.DMA((n, 7)), pltpu.SemaphoreType.DMA((n, 7)),
                 pltpu.SemaphoreType.DMA((n,))])(*shards)


def _wire_cost(arrays):
    nbytes = sum(a.size * a.dtype.itemsize for a in arrays)
    return pl.CostEstimate(flops=0, transcendentals=0, bytes_accessed=40 * nbytes)


def _all_gather_background(shards, collective_id, name):
    n = len(shards)
    src_refs = [jax.new_ref(s, memory_space=pltpu.MemorySpace.HBM) for s in shards]
    out_refs = [jax.empty_ref(_sds((N_DEV,) + s.shape, s.dtype), memory_space=pltpu.MemorySpace.HBM) for s in shards]

    @pl.kernel(mesh=plsc.ScalarSubcoreMesh(axis_name="seq", num_cores=1), name=name,
               scratch_types=(pltpu.SemaphoreType.DMA((n, 7)), pltpu.SemaphoreType.DMA((n, 7)),
                              pltpu.SemaphoreType.DMA((n,))),
               compiler_params=pltpu.CompilerParams(collective_id=collective_id))
    def launch(send_sems, recv_sems, local_sems):
        x, y, c = _my_coords()
        me, sibling = (x, y, c), (x, y, 1 - c)
        chips = [(1 - x, y), (x, 1 - y), (1 - x, 1 - y)]
        barrier = pltpu.get_barrier_semaphore()
        for peer in [sibling] + [(*chip, c) for chip in chips]:
            pl.semaphore_signal(barrier, inc=1, device_id=peer, device_id_type=pl.DeviceIdType.MESH)
        pl.semaphore_wait(barrier, 4)

        def copy(w, k, block, to, src=None):
            dst = out_refs[w].at[_dev_index(*block)]
            return pltpu.make_async_remote_copy(
                src_ref=dst if src is None else src, dst_ref=dst,
                send_sem=send_sems.at[w, k], recv_sem=recv_sems.at[w, k],
                device_id=to, device_id_type=pl.DeviceIdType.MESH)

        sends, locals_ = [], []
        for w in range(n):
            mine = pltpu.make_async_copy(src_refs[w], out_refs[w].at[_dev_index(*me)], local_sems.at[w])
            mine.start()
            locals_.append(mine)
            first = [copy(w, 0, me, sibling, src=src_refs[w])]
            first += [copy(w, 1 + j, me, (*chip, c), src=src_refs[w]) for j, chip in enumerate(chips)]
            for cp in first:
                cp.start()
            sends += first
        for w in range(n):
            for j, chip in enumerate(chips):
                copy(w, 1 + j, (*chip, c), me).wait_recv()
                fwd = copy(w, 4 + j, (*chip, c), sibling)
                fwd.start()
                sends.append(fwd)
        for w in range(n):
            copy(w, 0, sibling, me).wait_recv()
            for j, chip in enumerate(chips):
                copy(w, 4 + j, (*chip, 1 - c), me).wait_recv()
        for cp in sends:
            cp.wait_send()
        for mine in locals_:
            mine.wait()

    launch()
    return [r[...] for r in out_refs]


def _exchange_grads(grads):
    n = len(grads)

    def body(*refs):
        ins, outs = refs[:n], refs[n:2 * n]
        send_sems, recv_sems, local_sems = refs[2 * n:]
        me = _my_coords()
        my_idx = _dev_index(*me)
        sends, locals_ = [], []
        for w in range(n):
            mine = pltpu.make_async_copy(ins[w].at[my_idx], outs[w].at[my_idx], local_sems.at[w])
            mine.start()
            locals_.append(mine)
            for k, f in enumerate(_FLIPS):
                peer = _flip(me, f)
                cp = pltpu.make_async_remote_copy(
                    src_ref=ins[w].at[_dev_index(*peer)], dst_ref=outs[w].at[my_idx],
                    send_sem=send_sems.at[w, k], recv_sem=recv_sems.at[w, k],
                    device_id=peer, device_id_type=pl.DeviceIdType.MESH)
                cp.start()
                sends.append(cp)
        for w in range(n):
            for k, f in enumerate(_FLIPS):
                peer = _flip(me, f)
                slot = outs[w].at[_dev_index(*peer)]
                pltpu.make_async_remote_copy(
                    src_ref=slot, dst_ref=slot,
                    send_sem=send_sems.at[w, k], recv_sem=recv_sems.at[w, k],
                    device_id=peer, device_id_type=pl.DeviceIdType.MESH).wait_recv()
        for cp in sends:
            cp.wait_send()
        for mine in locals_:
            mine.wait()

    any_spec = pl.BlockSpec(memory_space=pl.ANY)
    return _pcall(
        body, name="exchange_grads",
        out_shape=[_sds(g.shape, g.dtype) for g in grads],
        in_specs=[any_spec] * n, out_specs=[any_spec] * n,
        scratch=[pltpu.SemaphoreType.DMA((n, 7)), pltpu.SemaphoreType.DMA((n, 7)),
                 pltpu.SemaphoreType.DMA((n,))])(*grads)


def _exchange_grads_background(grads, collective_id, name):
    n = len(grads)
    src_refs = [jax.new_ref(g, memory_space=pltpu.MemorySpace.HBM) for g in grads]
    out_refs = [jax.empty_ref(_sds(g.shape, g.dtype), memory_space=pltpu.MemorySpace.HBM) for g in grads]

    @pl.kernel(mesh=plsc.ScalarSubcoreMesh(axis_name="seq", num_cores=1), name=name,
               scratch_types=(pltpu.SemaphoreType.DMA((n, 7)), pltpu.SemaphoreType.DMA((n, 7)),
                              pltpu.SemaphoreType.DMA((n,))),
               cost_estimate=_wire_cost(grads),
               compiler_params=pltpu.CompilerParams(collective_id=collective_id))
    def launch(send_sems, recv_sems, local_sems):
        me = _my_coords()
        my_idx = _dev_index(*me)
        peers = [_flip(me, f) for f in _FLIPS]
        barrier = pltpu.get_barrier_semaphore()
        for peer in peers:
            pl.semaphore_signal(barrier, inc=1, device_id=peer, device_id_type=pl.DeviceIdType.MESH)
        pl.semaphore_wait(barrier, len(peers))
        sends, locals_ = [], []
        for w in range(n):
            mine = pltpu.make_async_copy(src_refs[w].at[my_idx], out_refs[w].at[my_idx], local_sems.at[w])
            mine.start()
            locals_.append(mine)
            for k, peer in enumerate(peers):
                cp = pltpu.make_async_remote_copy(
                    src_ref=src_refs[w].at[_dev_index(*peer)], dst_ref=out_refs[w].at[my_idx],
                    send_sem=send_sems.at[w, k], recv_sem=recv_sems.at[w, k],
                    device_id=peer, device_id_type=pl.DeviceIdType.MESH)
                cp.start()
                sends.append(cp)
        for w in range(n):
            for k, peer in enumerate(peers):
                slot = out_refs[w].at[_dev_index(*peer)]
                pltpu.make_async_remote_copy(
                    src_ref=slot, dst_ref=slot, send_sem=send_sems.at[w, k], recv_sem=recv_sems.at[w, k],
                    device_id=peer, device_id_type=pl.DeviceIdType.MESH).wait_recv()
        for cp in sends:
            cp.wait_send()
        for mine in locals_:
            mine.wait()

    launch()
    return [r[...] for r in out_refs]


def _to_wire(parts, after, name):
    n = len(parts)
    rows, cols = parts[0].shape
    tr = rows // 2 if rows % 32 == 0 else rows

    def body(*refs):
        for k in range(n):
            refs[n][k] = refs[k][...].astype(WIRE)

    blk = pl.BlockSpec((tr, cols), lambda i: (i, 0))
    return _pcall(
        body, name=name, grid=(rows // tr,), out_shape=_sds((n, rows, cols), WIRE),
        in_specs=[blk] * n, out_specs=pl.BlockSpec((n, tr, cols), lambda i: (0, i, 0)),
        sem=("parallel",), after=after)(*parts)


def _adam_math(w, g, m, v):
    m = ADAM_B1 * m + (1.0 - ADAM_B1) * g
    v = ADAM_B2 * v + (1.0 - ADAM_B2) * (g * g)
    m_hat = m / (1.0 - ADAM_B1 ** ADAM_STEP)
    v_hat = v / (1.0 - ADAM_B2 ** ADAM_STEP)
    delta = -ADAM_LR * (m_hat / (jnp.sqrt(v_hat) + ADAM_EPS) + ADAM_WD * w)
    return delta, m, v


def _small_layout(sizes):
    row0, r = [], 0
    for n in sizes:
        row0.append(r)
        r += -(-n // LANES)
    return row0, r, -(-(r + 1) // 8) * 8


def _small_pieces(n):
    return [(k, min(LANES, n - LANES * k)) for k in range(-(-n // LANES))]


def _small_fill(pack, slot, srcs, sizes, row0, rows):
    pack[slot] = jnp.zeros((rows, LANES), jnp.float32)
    for p, n in enumerate(sizes):
        val = srcs[p][...]
        if val.shape[-1] == LANES and n == 64:
            pack[slot, row0[p]:row0[p] + 1, :] = val + pltpu.roll(val, 64, 1)
            continue
        for k, width in _small_pieces(n):
            pack[slot, row0[p] + k:row0[p] + k + 1, 0:width] = srcs[p][:, LANES * k:LANES * k + width]


def _small_allreduce(grads, loss_tile, sizes):
    n_par = len(sizes)
    row0, loss_row, rows = _small_layout(sizes)

    def body(*refs):
        g_refs, loss_in, out_ref = refs[:n_par], refs[n_par], refs[n_par + 1]
        pack, gath, send_sems, recv_sems = refs[n_par + 2:]
        me = _my_coords()
        my_idx = _dev_index(*me)
        _small_fill(pack, 0, g_refs, sizes, row0, rows)
        pack[0, loss_row:loss_row + 1, :] = loss_in[0:1, :]
        gath[my_idx] = pack[0]
        sends = []
        for k, f in enumerate(_FLIPS):
            peer = _flip(me, f)
            cp = pltpu.make_async_remote_copy(
                src_ref=pack.at[0], dst_ref=gath.at[my_idx],
                send_sem=send_sems.at[k], recv_sem=recv_sems.at[k],
                device_id=peer, device_id_type=pl.DeviceIdType.MESH)
            cp.start()
            sends.append(cp)
        for k, f in enumerate(_FLIPS):
            peer = _flip(me, f)
            slot = gath.at[_dev_index(*peer)]
            pltpu.make_async_remote_copy(
                src_ref=slot, dst_ref=slot, send_sem=send_sems.at[k], recv_sem=recv_sems.at[k],
                device_id=peer, device_id_type=pl.DeviceIdType.MESH).wait_recv()
        for cp in sends:
            cp.wait_send()
        g = gath[0]
        for d in range(1, N_DEV):
            g = g + gath[d]
        out_ref[...] = g

    vm = pl.BlockSpec(memory_space=pltpu.VMEM)
    return _pcall(
        body, name="small_allreduce", out_shape=_sds((rows, LANES), jnp.float32),
        in_specs=[vm] * (n_par + 1), out_specs=vm,
        scratch=[pltpu.VMEM((1, rows, LANES), jnp.float32), pltpu.VMEM((N_DEV, rows, LANES), jnp.float32),
                 pltpu.SemaphoreType.DMA((7,)), pltpu.SemaphoreType.DMA((7,))])(*grads, loss_tile)


def _small_adam(packed_g, ws, ms, vs):
    sizes = [w.shape[-1] for w in ws]
    n_par = len(ws)
    row0, loss_row, rows = _small_layout(sizes)

    def body(*refs):
        g_ref = refs[0]
        w_refs, m_refs, v_refs = (refs[1 + k * n_par: 1 + (k + 1) * n_par] for k in range(3))
        loss_out = refs[3 * n_par + 1]
        out_refs = refs[3 * n_par + 2: 7 * n_par + 2]
        pack, res = refs[7 * n_par + 2:]
        for slot, srcs in enumerate((w_refs, m_refs, v_refs)):
            _small_fill(pack, slot, srcs, sizes, row0, rows)
        g = g_ref[...]
        delta, m, v = _adam_math(pack[0], g, pack[1], pack[2])
        res[0], res[1], res[2], res[3] = g, delta, m, v
        loss_out[...] = res[0, loss_row:loss_row + 1, 0:1]
        for p, n in enumerate(sizes):
            for kind in range(4):
                for k, width in _small_pieces(n):
                    out_refs[4 * p + kind][:, LANES * k:LANES * k + width] = (
                        res[kind, row0[p] + k:row0[p] + k + 1, 0:width])

    vm = pl.BlockSpec(memory_space=pltpu.VMEM)
    out_shape = [_sds((1, 1), jnp.float32)]
    for n in sizes:
        out_shape += [_sds((1, n), jnp.float32)] * 4
    outs = _pcall(
        body, name="small_adam", out_shape=out_shape,
        in_specs=[vm] * (3 * n_par + 1), out_specs=[vm] * len(out_shape),
        scratch=[pltpu.VMEM((3, rows, LANES), jnp.float32), pltpu.VMEM((4, rows, LANES), jnp.float32)])(
            packed_g, *ws, *ms, *vs)
    return outs[0], [outs[1 + 4 * p: 5 + 4 * p] for p in range(n_par)]


def _adam_big(recv, w, m, v, name, after=None, which=None):
    rows, cols = recv.shape[-2:]
    row_tiles = [t for t in range(16, rows + 1, 16) if rows % t == 0 and t * cols <= 400 * 1024]
    tr, tc = (max(row_tiles), cols) if row_tiles else (rows, 512 if cols % 512 == 0 else cols)

    def body(r_ref, w_ref, m_ref, v_ref, g_ref, d_ref, mo_ref, vo_ref):
        g = r_ref[0].astype(jnp.float32)
        for d in range(1, N_DEV):
            g = g + r_ref[d].astype(jnp.float32)
        delta, mn, vn = _adam_math(w_ref[...], g, m_ref[...], v_ref[...])
        g_ref[...] = g
        d_ref[...] = delta
        mo_ref[...] = mn
        vo_ref[...] = vn

    blk = pl.BlockSpec((tr, tc), lambda i, j: (i, j))
    if which is None:
        r_spec = pl.BlockSpec((N_DEV, tr, tc), lambda i, j: (0, i, j))
    else:
        r_spec = pl.BlockSpec((N_DEV, None, tr, tc), lambda i, j: (0, which, i, j))
    return _pcall(
        body, name=name, grid=(rows // tr, cols // tc),
        out_shape=[_sds((rows, cols), jnp.float32)] * 4,
        in_specs=[r_spec, blk, blk, blk],
        out_specs=[blk] * 4, sem=("parallel", "parallel"), after=after)(recv, w, m, v)


def _mm(a, b, *, ta=False, tb=False, out_dtype, tm, tk, name):
    (kdim, mdim) = a.shape if ta else a.shape[::-1]
    ndim = b.shape[0] if tb else b.shape[1]
    tm, tk = min(tm, mdim), min(tk, kdim)
    nk = kdim // tk

    def body(a_ref, b_ref, o_ref, acc):
        k = pl.program_id(1)
        if ta:
            part = _dot_tn(a_ref[...], b_ref[...])
        elif tb:
            part = _dot_nt(a_ref[...], b_ref[...])
        else:
            part = _dot(a_ref[...], b_ref[...])

        @pl.when(k == 0)
        def _():
            acc[...] = part

        @pl.when(k > 0)
        def _():
            acc[...] += part

        @pl.when(k == nk - 1)
        def _():
            o_ref[...] = acc[...].astype(o_ref.dtype)

    a_spec = pl.BlockSpec((tk, tm), lambda i, k: (k, i)) if ta else pl.BlockSpec((tm, tk), lambda i, k: (i, k))
    b_spec = pl.BlockSpec((ndim, tk), lambda i, k: (0, k)) if tb else pl.BlockSpec((tk, ndim), lambda i, k: (k, 0))
    return _pcall(
        body, name=name, grid=(mdim // tm, nk), out_shape=_sds((mdim, ndim), out_dtype),
        in_specs=[a_spec, b_spec], out_specs=pl.BlockSpec((tm, ndim), lambda i, k: (i, 0)),
        scratch=[pltpu.VMEM((tm, ndim), jnp.float32)], sem=("parallel", "arbitrary"))(a, b)


def _ref_col_pieces(start, stop):
    ref_starts = [0, 1024, 1152, 1280, 1792, 2304, 2368, 2880]
    perm_starts = [C_QA, C_KA, C_VA, C_CQ, C_CKV, C_KR, C_QM]
    out = []
    for p in range(7):
        lo, hi = max(start, ref_starts[p]), min(stop, ref_starts[p + 1])
        if lo < hi:
            out.append((lo - start, perm_starts[p] + lo - ref_starts[p], hi - lo))
    return out


def _dw_in(hn, d_proj, n_shard):
    s, d = hn.shape
    n = sum(p.shape[1] for p in d_proj)
    n_pc = len(d_proj)
    tm, tk = min(512, d), min(1024, s)
    nk = s // tk

    def body(a_ref, *refs):
        b_refs, (o_ref, acc) = refs[:n_pc], refs[n_pc:]
        k = pl.program_id(1)
        part = _dot_tn(a_ref[...], jnp.concatenate([r[...] for r in b_refs], axis=1))

        @pl.when(k == 0)
        def _():
            acc[...] = part

        @pl.when(k > 0)
        def _():
            acc[...] += part

        @pl.when(k == nk - 1)
        def _():
            t = acc[...].T
            for j in range(N_DEV):
                rows = [t[src:src + width] for _, src, width in _ref_col_pieces(j * n_shard, (j + 1) * n_shard)]
                o_ref[j] = jnp.concatenate(rows, axis=0).astype(o_ref.dtype)

    return _pcall(
        body, name="dw_in", grid=(d // tm, nk), out_shape=_sds((N_DEV, n_shard, d), WIRE),
        in_specs=[pl.BlockSpec((tk, tm), lambda i, k: (k, i))]
        + [pl.BlockSpec((tk, p.shape[1]), lambda i, k: (k, 0)) for p in d_proj],
        out_specs=pl.BlockSpec((N_DEV, n_shard, tm), lambda i, k: (0, 0, i)),
        scratch=[pltpu.VMEM((tm, n), jnp.float32)], sem=("parallel", "arbitrary"))(hn, *d_proj)


def _in_proj(x, g, w):
    s, d = x.shape
    n = w.shape[0]
    tm = min(2 * ROW_TILE, s)

    def body(x_ref, g_ref, w_ref, p_ref, hn_ref):
        hn, _, _ = _norm_fwd(x_ref[...], g_ref[...])
        hn_ref[...] = hn.astype(hn_ref.dtype)
        p_ref[...] = _dot_nt(hn, w_ref[...])

    return _pcall(
        body, name="in_proj", grid=(s // tm,),
        out_shape=[_sds((s, n), jnp.float32), _sds((s, d), MXU)],
        in_specs=[pl.BlockSpec((tm, d), lambda i: (i, 0)), pl.BlockSpec((1, d), lambda i: (0, 0)),
                  pl.BlockSpec((n, d), lambda i: (0, 0), pipeline_mode=pl.Buffered(1))],
        out_specs=[pl.BlockSpec((tm, n), lambda i: (i, 0)), pl.BlockSpec((tm, d), lambda i: (i, 0))],
        sem=("parallel",))(x, g, w)


def _mla_prep(proj, cos, sin, g_cq, g_ckv, w_uq, w_ukv, g_qn, g_qr, g_kn, g_kr):
    s = proj.shape[0]
    tm = min(ROW_TILE, s)
    nh = MLA_HEADS

    def body(cq_ref, ckv_ref, kr_ref, cos_ref, sin_ref, gcq_ref, gckv_ref, wuq_ref, wukv_ref,
             gqn_ref, gqr_ref, gkn_ref, gkr_ref,
             qc_ref, kc_ref, v_ref, qb_ref, kvb_ref, cqn_ref, ckvn_ref):
        cos_t, sin_t = cos_ref[...], sin_ref[...]
        lo = _lo_mask((tm, LANES))
        cqn, _, _ = _norm_fwd(cq_ref[...], gcq_ref[...])
        cqn_ref[...] = cqn.astype(cqn_ref.dtype)
        qb = _dot_nt(cqn, wuq_ref[...])
        qb_ref[...] = qb
        ckvn, _, _ = _norm_fwd(ckv_ref[...], gckv_ref[...])
        ckvn_ref[...] = ckvn.astype(ckvn_ref.dtype)
        w_ukv_full = jnp.concatenate([wukv_ref[dev] for dev in range(N_DEV)], axis=1)
        kvb = _dot(ckvn, w_ukv_full)
        kvb_ref[...] = kvb
        kr, _, _ = _norm_fwd(kr_ref[...], gkr_ref[...], half=True)
        kr = _rope(kr, cos_t, sin_t)
        kr2 = jnp.where(lo, kr, pltpu.roll(kr, 64, 1))
        ropes = []
        for j in range(nh // 2):
            xr = qb[:, nh * MLA_NOPE + LANES * j: nh * MLA_NOPE + LANES * (j + 1)]
            qr, _, _ = _norm_fwd(xr, gqr_ref[...], half=True)
            ropes.append(_rope(qr, cos_t, sin_t))
        for h in range(nh):
            qn, _, _ = _norm_fwd(qb[:, MLA_NOPE * h: MLA_NOPE * (h + 1)], gqn_ref[...])
            mask = lo if h % 2 == 0 else jnp.logical_not(lo)
            qr = jnp.where(mask, ropes[h // 2], 0.0)
            qc_ref[h] = jnp.concatenate([qn, qr], axis=1).astype(qc_ref.dtype)
            kn, _, _ = _norm_fwd(kvb[:, 256 * h: 256 * h + MLA_NOPE], gkn_ref[...])
            kc_ref[h] = jnp.concatenate([kn, kr2], axis=1).astype(kc_ref.dtype)
            v_ref[h] = kvb[:, 256 * h + MLA_NOPE: 256 * (h + 1)].astype(v_ref.dtype)

    def col(width, start):
        return pl.BlockSpec((tm, width), lambda i: (i, start // width))

    def full(shape):
        return pl.BlockSpec(shape, lambda i: (0,) * len(shape))

    def row(width):
        return pl.BlockSpec((tm, width), lambda i: (i, 0))

    def heads(width):
        return pl.BlockSpec((nh, tm, width), lambda i: (0, i, 0))

    return _pcall(
        body, name="mla_prep", grid=(s // tm,),
        out_shape=[_sds((nh, s, 256), MXU), _sds((nh, s, 256), MXU), _sds((nh, s, MLA_V), MXU),
                   _sds((s, 768), jnp.float32), _sds((s, 1024), jnp.float32),
                   _sds((s, 512), MXU), _sds((s, 512), MXU)],
        in_specs=[col(512, C_CQ), col(512, C_CKV), col(LANES, C_KR), row(LANES), row(LANES),
                  full((1, 512)), full((1, 512)), full((768, 512)), full((N_DEV, 512, LANES)),
                  full((1, LANES)), full((1, LANES)), full((1, LANES)), full((1, LANES))],
        out_specs=[heads(256), heads(256), heads(MLA_V), row(768), row(1024), row(512), row(512)],
        sem=("parallel",))(proj, proj, proj, cos, sin, g_cq, g_ckv, w_uq, w_ukv, g_qn, g_qr, g_kn, g_kr)


def _mla_fwd(qc, kc, v):
    nh, s, _ = qc.shape
    t = min(ATT_TILE, s)
    nb = s // t
    scale = (MLA_NOPE + MLA_ROPE) ** -0.5

    def body(q_ref, k_ref, v_ref, y_ref, lse_ref, m_sc, l_sc, acc):
        qi, ki = pl.program_id(1), pl.program_id(2)

        @pl.when(ki == 0)
        def _():
            m_sc[...] = jnp.full_like(m_sc, NEG_INF)
            l_sc[...] = jnp.zeros_like(l_sc)
            acc[...] = jnp.zeros_like(acc)

        def step(diagonal):
            rc = t // 4 if diagonal else t
            for c in range(t // rc):
                rows = slice(rc * c, rc * (c + 1))
                keys = slice(0, rc * (c + 1))
                sc = _dot_nt(q_ref[0, rows, :], k_ref[0, keys, :]) * (scale * LOG2E)
                if diagonal:
                    r_i = lax.broadcasted_iota(jnp.int32, sc.shape, 0) + rc * c
                    c_i = lax.broadcasted_iota(jnp.int32, sc.shape, 1)
                    sc = jnp.where(c_i <= r_i, sc, NEG_INF)
                m_old = m_sc[rows, :]
                m_new = jnp.maximum(m_old, jnp.max(sc, -1, keepdims=True))
                alpha = jnp.exp2(m_old - m_new)
                p = jnp.exp2(sc - m_new)
                l_sc[rows, :] = alpha * l_sc[rows, :] + jnp.sum(p, -1, keepdims=True)
                acc[rows, :] = alpha * acc[rows, :] + _dot(p, v_ref[0, keys, :])
                m_sc[rows, :] = m_new

        @pl.when(ki < qi)
        def _():
            step(False)

        @pl.when(ki == qi)
        def _():
            step(True)

        @pl.when(ki == qi)
        def _():
            y_ref[...] = acc[...] / l_sc[...]
            lse_ref[0] = m_sc[...] + jnp.log2(l_sc[...])

    return _pcall(
        body, name="mla_fwd", grid=(nh, nb, nb),
        out_shape=[_sds((s, nh * MLA_V), jnp.float32), _sds((nh, s, 1), jnp.float32)],
        in_specs=[pl.BlockSpec((1, t, 256), lambda h, i, k: (h, i, 0)),
                  pl.BlockSpec((1, t, 256), lambda h, i, k: (h, jnp.minimum(k, i), 0)),
                  pl.BlockSpec((1, t, MLA_V), lambda h, i, k: (h, jnp.minimum(k, i), 0))],
        out_specs=[pl.BlockSpec((t, MLA_V), lambda h, i, k: (i, h)),
                   pl.BlockSpec((1, t, 1), lambda h, i, k: (h, i, 0))],
        scratch=[pltpu.VMEM((t, 1), jnp.float32), pltpu.VMEM((t, 1), jnp.float32),
                 pltpu.VMEM((t, MLA_V), jnp.float32)],
        sem=("parallel", "parallel", "arbitrary"))(qc, kc, v)


def _memkv_prep(mem, g_mem, w_mkv, g_mk):
    ml, d = mem.shape
    hw = MEM_HEADS * MEM_DIM

    def body(mem_ref, g_ref, w_ref, gk_ref, k_ref, v_ref, kv_ref, mn_ref):
        mn, _, _ = _norm_fwd(mem_ref[...], g_ref[...])
        mn_ref[...] = mn.astype(mn_ref.dtype)
        kv = _dot(mn, w_ref[...])
        kv_ref[...] = kv
        for h in range(MEM_HEADS):
            kn, _, _ = _norm_fwd(kv[:, MEM_DIM * h: MEM_DIM * (h + 1)], gk_ref[...])
            k_ref[:, MEM_DIM * h: MEM_DIM * (h + 1)] = kn.astype(k_ref.dtype)
        v_ref[...] = kv[:, hw:].astype(v_ref.dtype)

    vm = pl.BlockSpec(memory_space=pltpu.VMEM)
    return _pcall(
        body, name="memkv_prep",
        out_shape=[_sds((ml, hw), MXU), _sds((ml, hw), MXU), _sds((ml, 2 * hw), jnp.float32), _sds((ml, d), MXU)],
        in_specs=[vm] * 4, out_specs=[vm] * 4)(mem, g_mem, w_mkv, g_mk)


def _mem_fwd(proj, g_mq, km, vmm):
    s = proj.shape[0]
    ml, hw = km.shape
    tm = min(FFN_TILE, s)
    scale = MEM_DIM ** -0.5

    def body(q_ref, g_ref, k_ref, v_ref, y_ref, lse_ref):
        col = lax.broadcasted_iota(jnp.int32, (tm, MEM_HEADS), 1)
        lse_t = jnp.zeros((tm, MEM_HEADS), jnp.float32)
        for h in range(MEM_HEADS):
            sl = slice(MEM_DIM * h, MEM_DIM * (h + 1))
            qn, _, _ = _norm_fwd(q_ref[:, sl], g_ref[...])
            sc = _dot_nt(qn, k_ref[:, sl]) * scale
            m = jnp.max(sc, -1, keepdims=True)
            p = jnp.exp(sc - m)
            l = jnp.sum(p, -1, keepdims=True)
            y_ref[:, sl] = _dot(p, v_ref[:, sl]) / l
            lse_t = jnp.where(col == h, m + jnp.log(l), lse_t)
        lse_ref[...] = lse_t

    return _pcall(
        body, name="mem_fwd", grid=(s // tm,),
        out_shape=[_sds((s, hw), jnp.float32), _sds((s, MEM_HEADS), jnp.float32)],
        in_specs=[pl.BlockSpec((tm, hw), lambda i: (i, C_QM // hw)), pl.BlockSpec((1, MEM_DIM), lambda i: (0, 0)),
                  pl.BlockSpec((ml, hw), lambda i: (0, 0)), pl.BlockSpec((ml, hw), lambda i: (0, 0))],
        out_specs=[pl.BlockSpec((tm, hw), lambda i: (i, 0)), pl.BlockSpec((tm, MEM_HEADS), lambda i: (i, 0))],
        sem=("parallel",))(proj, g_mq, km, vmm)


def _alibi_slope(h):
    return float(2.0 ** (-8.0 * (h + 1) / SWA_Q_HEADS))


def _swa_common(n, kp, kc, vp, vc, pq, pkp, pkc, gk):
    b = SWA_BLOCK
    k_raw = jnp.concatenate([kp, kc], axis=0)
    kn, kxn, kr = _norm_fwd(k_raw, gk, half=True)
    v = jnp.concatenate([vp, vc], axis=0)
    dist = jnp.abs(pq - jnp.concatenate([pkp, pkc], axis=1))
    r_i = lax.broadcasted_iota(jnp.int32, (b, 2 * b), 0)
    c_i = lax.broadcasted_iota(jnp.int32, (b, 2 * b), 1)
    valid = (c_i > r_i) & (c_i <= r_i + b) & (c_i >= jnp.where(n > 0, 0, b))
    bias = jnp.where(valid, -dist, NEG_INF)
    return kn, v, bias


def _swa_folded(n, kp, kc, pq, pkp, pkc, gk):
    b = SWA_BLOCK
    kn_p, _, _ = _norm_fwd(kp, gk, half=True)
    kn_c, _, _ = _norm_fwd(kc, gk, half=True)
    r_i = lax.broadcasted_iota(jnp.int32, (b, b), 0)
    c_i = lax.broadcasted_iota(jnp.int32, (b, b), 1)
    upper = c_i > r_i
    bias_prev = jnp.where(n > 0, 0.0, NEG_INF) - jnp.abs(pq - pkp)
    bias = jnp.where(upper, bias_prev, -jnp.abs(pq - pkc))
    return kn_p, kn_c, bias, upper


def _swa_specs(s):
    b = SWA_BLOCK
    prev = lambda n: jnp.maximum(n - 1, 0)
    return [
        pl.BlockSpec((b, 1024), lambda n: (n, C_QA // 1024)),
        pl.BlockSpec((b, LANES), lambda n: (prev(n), C_KA // LANES)),
        pl.BlockSpec((b, LANES), lambda n: (n, C_KA // LANES)),
        pl.BlockSpec((b, LANES), lambda n: (prev(n), C_VA // LANES)),
        pl.BlockSpec((b, LANES), lambda n: (n, C_VA // LANES)),
        pl.BlockSpec((b, 1), lambda n: (n, 0)),
        pl.BlockSpec((1, b), lambda n: (0, prev(n))),
        pl.BlockSpec((1, b), lambda n: (0, n)),
        pl.BlockSpec((1, LANES), lambda n: (0, 0)),
        pl.BlockSpec((1, LANES), lambda n: (0, 0)),
        pl.BlockSpec(memory_space=pltpu.SMEM),
    ]


def _swa_fwd(proj, posc, posr, gq, gk, sinks):
    s = proj.shape[0]
    b = SWA_BLOCK
    scale = SWA_DIM ** -0.5

    def body(q_ref, kp_ref, kc_ref, vp_ref, vc_ref, pq_ref, pkp_ref, pkc_ref, gq_ref, gk_ref, sink_ref,
             y_ref, lse_ref):
        n = pl.program_id(0)
        kn_p, kn_c, bias, upper = _swa_folded(n, kp_ref[...], kc_ref[...], pq_ref[...], pkp_ref[...], pkc_ref[...],
                                              gk_ref[...])
        v_p, v_c = vp_ref[...], vc_ref[...]
        lo = _lo_mask((b, LANES))
        col = lax.broadcasted_iota(jnp.int32, (b, SWA_Q_HEADS), 1)
        lse_t = jnp.zeros((b, SWA_Q_HEADS), jnp.float32)
        hpg = SWA_Q_HEADS // SWA_KV_HEADS
        for g in range(SWA_KV_HEADS):
            heads = range(hpg * g, hpg * (g + 1))
            kvmask = lo if g == 0 else jnp.logical_not(lo)
            qs = []
            for j in range(hpg // 2 * g, hpg // 2 * (g + 1)):
                qn, _, _ = _norm_fwd(q_ref[:, LANES * j: LANES * (j + 1)], gq_ref[...], half=True)
                qn = qn * scale
                qsw = pltpu.roll(qn, 64, 1)
                qs += [jnp.where(kvmask, qn if e == g else qsw, 0.0) for e in range(2)]
            q_st = jnp.concatenate(qs, axis=0).astype(MXU)
            sp_st, sc_st = _dot_nt(q_st, kn_p), _dot_nt(q_st, kn_c)
            pus, pls, ls = [], [], []
            for i, h in enumerate(heads):
                rows = slice(b * i, b * (i + 1))
                sc = jnp.where(upper, sp_st[rows], sc_st[rows]) + _alibi_slope(h) * bias
                sk = sink_ref[h]
                m = jnp.maximum(jnp.max(sc, -1, keepdims=True), sk)
                p = jnp.exp(sc - m)
                l = jnp.sum(p, -1, keepdims=True) + jnp.exp(sk - m)
                pus.append(jnp.where(upper, p, 0.0).astype(MXU))
                pls.append(jnp.where(upper, 0.0, p).astype(MXU))
                ls.append(l)
                lse_t = jnp.where(col == h, m + jnp.log(l), lse_t)
            o_st = _dot(jnp.concatenate(pus, axis=0), v_p) + _dot(jnp.concatenate(pls, axis=0), v_c)
            for j in range(hpg // 2 * g, hpg // 2 * (g + 1)):
                halves = []
                for e in range(2):
                    i = 2 * j + e - hpg * g
                    o_h = o_st[b * i: b * (i + 1)] / ls[i]
                    halves.append(o_h if e == g else pltpu.roll(o_h, 64, 1))
                y_ref[:, LANES * j: LANES * (j + 1)] = jnp.where(lo, halves[0], halves[1])
        lse_ref[...] = lse_t

    return _pcall(
        body, name="swa_fwd", grid=(s // b,),
        out_shape=[_sds((s, 1024), jnp.float32), _sds((s, SWA_Q_HEADS), jnp.float32)],
        in_specs=_swa_specs(s),
        out_specs=[pl.BlockSpec((b, 1024), lambda n: (n, 0)), pl.BlockSpec((b, SWA_Q_HEADS), lambda n: (n, 0))],
        sem=("parallel",))(proj, proj, proj, proj, proj, posc, posr, posr, gq, gk, sinks)


def _out_proj(y_a, y_b, y_m, x, w_out, g_ffn):
    s, d = x.shape
    tm = min(2 * ROW_TILE, s)

    def body(ya_ref, yb_ref, ym_ref, x_ref, w_ref, g_ref, h1_ref, fn_ref):
        y = jnp.concatenate([ya_ref[...].astype(MXU), yb_ref[...].astype(MXU), ym_ref[...].astype(MXU)], axis=1)
        h1 = x_ref[...] + _dot(y, w_ref[...])
        h1_ref[...] = h1
        fn, _, _ = _norm_fwd(h1, g_ref[...])
        fn_ref[...] = fn.astype(fn_ref.dtype)

    def row(width):
        return pl.BlockSpec((tm, width), lambda i: (i, 0))

    return _pcall(
        body, name="out_proj", grid=(s // tm,),
        out_shape=[_sds((s, d), jnp.float32), _sds((s, d), MXU)],
        in_specs=[row(1024), row(512), row(512), row(d),
                  pl.BlockSpec(w_out.shape, lambda i: (0, 0), pipeline_mode=pl.Buffered(1)),
                  pl.BlockSpec((1, d), lambda i: (0, 0))],
        out_specs=[row(d), row(d)], sem=("parallel",))(y_a, y_b, y_m, x, w_out, g_ffn)


def _ffn_gu(fn, w_gu):
    s, d = fn.shape
    f = w_gu.shape[2]
    tm = min(2 * FFN_TILE, s)

    def body(fn_ref, w_ref, gu_ref, act_ref):
        x = fn_ref[...]
        g = _dot_nt(x, w_ref[0, 0])
        u = _dot_nt(x, w_ref[0, 1])
        gu_ref[0, 0] = g
        gu_ref[0, 1] = u
        act_ref[0] = (g * jax.nn.sigmoid(g) * u).astype(act_ref.dtype)

    return _pcall(
        body, name="ffn_gate_up", grid=(N_DEV, s // tm),
        out_shape=[_sds((N_DEV, 2, s, f), jnp.float32), _sds((N_DEV, s, f), MXU)],
        in_specs=[pl.BlockSpec((tm, d), lambda j, i: (i, 0)),
                  pl.BlockSpec((1, 2, f, d), lambda j, i: (j, 0, 0, 0))],
        out_specs=[pl.BlockSpec((1, 2, tm, f), lambda j, i: (j, 0, i, 0)),
                   pl.BlockSpec((1, tm, f), lambda j, i: (j, i, 0))],
        sem=("parallel", "parallel"))(fn, w_gu)


def _ffn_down(act, w_d, h1, target):
    _, s, f = act.shape
    d = h1.shape[1]
    tm = min(FFN_TILE, s)

    def body(a_ref, w_ref, h1_ref, t_ref, dout_ref, doutb_ref, loss_ref, acc):
        i, j = pl.program_id(0), pl.program_id(1)
        part = _dot(a_ref[0], w_ref[0]) + _dot(a_ref[1], w_ref[1])

        @pl.when(j == 0)
        def _():
            acc[...] = h1_ref[...] + part

        @pl.when(j > 0)
        def _():
            acc[...] += part

        @pl.when((i == 0) & (j == 0))
        def _():
            loss_ref[...] = jnp.zeros_like(loss_ref)

        @pl.when(j == N_DEV // 2 - 1)
        def _():
            diff = acc[...] - t_ref[...]
            dout_ref[...] = diff / d
            doutb_ref[...] = (diff / d).astype(doutb_ref.dtype)
            loss_ref[...] += 0.5 * jnp.sum(jnp.sum(diff * diff, -1, keepdims=True) / d)

    row = pl.BlockSpec((tm, d), lambda i, j: (i, 0))
    return _pcall(
        body, name="ffn_down", grid=(s // tm, N_DEV // 2),
        out_shape=[_sds((s, d), jnp.float32), _sds((s, d), MXU), _sds((8, LANES), jnp.float32)],
        in_specs=[pl.BlockSpec((2, tm, f), lambda i, j: (j, i, 0)), pl.BlockSpec((2, f, d), lambda i, j: (j, 0, 0)),
                  row, row],
        out_specs=[row, row, pl.BlockSpec((8, LANES), lambda i, j: (0, 0))],
        scratch=[pltpu.VMEM((tm, d), jnp.float32)], sem=("arbitrary", "arbitrary"))(act, w_d, h1, target)


def _ffn_bwd_act(dout, w_d, gu):
    s, d = dout.shape
    f = w_d.shape[1]
    tm = min(2 * FFN_TILE, s)
    ni = s // tm

    def body(do_ref, w_ref, gu_ref, dgu_ref, dw_ref, acc):
        i = pl.program_id(1)
        do = do_ref[...]
        d_act = _dot_nt(do, w_ref[0])
        g, u = gu_ref[0, 0], gu_ref[0, 1]
        sig = jax.nn.sigmoid(g)
        silu = g * sig
        dgu_ref[0, 0] = (d_act * u * (sig * (1.0 + g * (1.0 - sig)))).astype(dgu_ref.dtype)
        dgu_ref[0, 1] = (d_act * silu).astype(dgu_ref.dtype)
        part = _dot_tn(silu * u, do)

        @pl.when(i == 0)
        def _():
            acc[...] = part

        @pl.when(i > 0)
        def _():
            acc[...] += part

        @pl.when(i == ni - 1)
        def _():
            dw_ref[0] = acc[...].astype(dw_ref.dtype)

    return _pcall(
        body, name="ffn_bwd_act", grid=(N_DEV, ni),
        out_shape=[_sds((N_DEV, 2, s, f), MXU), _sds((N_DEV, f, d), WIRE)],
        in_specs=[pl.BlockSpec((tm, d), lambda j, i: (i, 0)), pl.BlockSpec((1, f, d), lambda j, i: (j, 0, 0)),
                  pl.BlockSpec((1, 2, tm, f), lambda j, i: (j, 0, i, 0))],
        out_specs=[pl.BlockSpec((1, 2, tm, f), lambda j, i: (j, 0, i, 0)),
                   pl.BlockSpec((1, f, d), lambda j, i: (j, 0, 0))],
        scratch=[pltpu.VMEM((f, d), jnp.float32)], sem=("parallel", "arbitrary"))(dout, w_d, gu)


def _ffn_dw_gu(fn, dgu):
    s, d = fn.shape
    f = dgu.shape[-1]
    tk = min(4 * FFN_TILE, s)
    nk = s // tk

    def body(fn_ref, dgu_ref, dw_ref, acc):
        k = pl.program_id(2)
        part = _dot_tn(dgu_ref[0, 0], fn_ref[...])

        @pl.when(k == 0)
        def _():
            acc[...] = part

        @pl.when(k > 0)
        def _():
            acc[...] += part

        @pl.when(k == nk - 1)
        def _():
            dw_ref[0, 0] = acc[...].astype(dw_ref.dtype)

    return _pcall(
        body, name="ffn_dw_gate_up", grid=(N_DEV, 2, nk),
        out_shape=_sds((N_DEV, 2, f, d), WIRE),
        in_specs=[pl.BlockSpec((tk, d), lambda j, w, k: (k, 0)),
                  pl.BlockSpec((1, 1, tk, f), lambda j, w, k: (j, w, k, 0))],
        out_specs=pl.BlockSpec((1, 1, f, d), lambda j, w, k: (j, w, 0, 0)),
        scratch=[pltpu.VMEM((f, d), jnp.float32)], sem=("parallel", "parallel", "arbitrary"))(fn, dgu)


def _ffn_dfn(dgu, w_gu, after):
    _, _, s, f = dgu.shape
    d = w_gu.shape[3]
    tm = min(FFN_TILE, s)

    def body(dgu_ref, w_ref, dfn_ref):
        j = pl.program_id(1)
        part = (_dot(dgu_ref[0, 0], w_ref[0, 0]) + _dot(dgu_ref[0, 1], w_ref[0, 1])
                + _dot(dgu_ref[1, 0], w_ref[1, 0]) + _dot(dgu_ref[1, 1], w_ref[1, 1]))

        @pl.when(j == 0)
        def _():
            dfn_ref[...] = part

        @pl.when(j > 0)
        def _():
            dfn_ref[...] += part

    return _pcall(
        body, name="ffn_dfn", grid=(s // tm, N_DEV // 2),
        out_shape=_sds((s, d), jnp.float32),
        in_specs=[pl.BlockSpec((2, 2, tm, f), lambda i, j: (j, 0, i, 0)),
                  pl.BlockSpec((2, 2, f, d), lambda i, j: (j, 0, 0, 0))],
        out_specs=pl.BlockSpec((tm, d), lambda i, j: (i, 0)),
        sem=("parallel", "arbitrary"), after=after)(dgu, w_gu)


def _ffn_norm_bwd(d_fn, dout, h1, g_ffn):
    s, d = h1.shape
    tm = min(2 * ROW_TILE, s)

    def body(dfn_ref, do_ref, h1_ref, g_ref, dh1_ref, dg_ref):
        i = pl.program_id(0)

        @pl.when(i == 0)
        def _():
            dg_ref[...] = jnp.zeros_like(dg_ref)

        _, xn, r = _norm_fwd(h1_ref[...], g_ref[...])
        dx, dg = _norm_bwd(xn, r, g_ref[...], dfn_ref[...])
        dh1_ref[...] = do_ref[...] + dx
        dg_ref[...] += dg

    row = pl.BlockSpec((tm, d), lambda i: (i, 0))
    vec = pl.BlockSpec((1, d), lambda i: (0, 0))
    return _pcall(
        body, name="ffn_norm_bwd", grid=(s // tm,),
        out_shape=[_sds((s, d), jnp.float32), _sds((1, d), jnp.float32)],
        in_specs=[row, row, row, vec], out_specs=[row, vec], sem=("arbitrary",))(d_fn, dout, h1, g_ffn)


def _mem_bwd(proj, g_mq, km, vmm, d_y, y_m, lse):
    s = proj.shape[0]
    ml, hw = km.shape
    tm = min(FFN_TILE, s)
    scale = MEM_DIM ** -0.5

    def body(q_ref, g_ref, k_ref, v_ref, do_ref, y_ref, lse_ref, dq_ref, dk_ref, dv_ref, dg_ref):
        i = pl.program_id(0)

        @pl.when(i == 0)
        def _():
            dk_ref[...] = jnp.zeros_like(dk_ref)
            dv_ref[...] = jnp.zeros_like(dv_ref)
            dg_ref[...] = jnp.zeros_like(dg_ref)

        col = lax.broadcasted_iota(jnp.int32, (tm, MEM_HEADS), 1)
        lse_t = lse_ref[...]
        for h in range(MEM_HEADS):
            sl = slice(MEM_DIM * h, MEM_DIM * (h + 1))
            qn, xn, r = _norm_fwd(q_ref[:, sl], g_ref[...])
            lse_h = jnp.sum(jnp.where(col == h, lse_t, 0.0), -1, keepdims=True)
            p = jnp.exp(_dot_nt(qn, k_ref[:, sl]) * scale - lse_h)
            do = do_ref[:, sl]
            dd = jnp.sum(do * y_ref[:, sl], -1, keepdims=True)
            dp = _dot_nt(do, v_ref[:, sl])
            ds = (p * (dp - dd)).astype(MXU)
            dv_ref[:, sl] += _dot_tn(p, do)
            dk_ref[:, sl] += _dot_tn(ds, qn) * scale
            dx, dg = _norm_bwd(xn, r, g_ref[...], _dot(ds, k_ref[:, sl]) * scale)
            dq_ref[:, sl] = dx.astype(dq_ref.dtype)
            dg_ref[...] += dg

    full = pl.BlockSpec((ml, hw), lambda i: (0, 0))
    return _pcall(
        body, name="mem_bwd", grid=(s // tm,),
        out_shape=[_sds((s, hw), MXU), _sds((ml, hw), jnp.float32), _sds((ml, hw), jnp.float32),
                   _sds((1, MEM_DIM), jnp.float32)],
        in_specs=[pl.BlockSpec((tm, hw), lambda i: (i, C_QM // hw)), pl.BlockSpec((1, MEM_DIM), lambda i: (0, 0)),
                  full, full, pl.BlockSpec((tm, hw), lambda i: (i, 3)), pl.BlockSpec((tm, hw), lambda i: (i, 0)),
                  pl.BlockSpec((tm, MEM_HEADS), lambda i: (i, 0))],
        out_specs=[pl.BlockSpec((tm, hw), lambda i: (i, 0)), full, full,
                   pl.BlockSpec((1, MEM_DIM), lambda i: (0, 0))],
        sem=("arbitrary",))(proj, g_mq, km, vmm, d_y, y_m, lse)


def _memkv_bwd(mem, g_mem, w_mkv, g_mk, kv, memn, dk, dv):
    ml, d = mem.shape
    hw = MEM_HEADS * MEM_DIM

    def body(mem_ref, g_ref, w_ref, gk_ref, kv_ref, mn_ref, dk_ref, dv_ref, dw_ref, dgm_ref, dgk_ref):
        parts = []
        dgk = jnp.zeros((1, MEM_DIM), jnp.float32)
        for h in range(MEM_HEADS):
            sl = slice(MEM_DIM * h, MEM_DIM * (h + 1))
            _, xn, r = _norm_fwd(kv_ref[:, sl], gk_ref[...])
            dx, dg = _norm_bwd(xn, r, gk_ref[...], dk_ref[:, sl])
            parts.append(dx)
            dgk = dgk + dg
        dkv = jnp.concatenate(parts + [dv_ref[...]], axis=1).astype(MXU)
        dgk_ref[...] = dgk
        dw_ref[...] = _dot_tn(mn_ref[...], dkv).astype(dw_ref.dtype)
        d_mn = _dot_nt(dkv, w_ref[...])
        _, xn, _ = _norm_fwd(mem_ref[...], g_ref[...])
        dgm_ref[...] = jnp.sum(d_mn * xn, 0, keepdims=True)

    vm = pl.BlockSpec(memory_space=pltpu.VMEM)
    return _pcall(
        body, name="memkv_bwd",
        out_shape=[_sds((d, 2 * hw), WIRE), _sds((1, d), jnp.float32), _sds((1, MEM_DIM), jnp.float32)],
        in_specs=[vm] * 8, out_specs=[vm] * 3)(mem, g_mem, w_mkv, g_mk, kv, memn, dk, dv)


def _mla_bwd(qc, kc, v, d_y, y_b, lse, after):
    nh, s, _ = qc.shape
    t = min(ATT_TILE, s)
    nb = s // t
    scale = (MLA_NOPE + MLA_ROPE) ** -0.5

    def body(q_ref, k_ref, v_ref, do_ref, y_ref, lse_ref, dq_ref, dk_ref, dv_ref, dk_acc, dv_acc):
        kj, qi = pl.program_id(1), pl.program_id(2)

        @pl.when((kj == 0) & (qi == 0))
        def _():
            dq_ref[...] = jnp.zeros_like(dq_ref)

        @pl.when(qi == kj)
        def _():
            dk_acc[...] = jnp.zeros_like(dk_acc)
            dv_acc[...] = jnp.zeros_like(dv_acc)

        def step(diagonal):
            rc = t // 4 if diagonal else t
            for c in range(t // rc):
                rows = slice(rc * c, rc * (c + 1))
                keys = slice(0, rc * (c + 1))
                q, k = q_ref[0, rows, :], k_ref[0, keys, :]
                sc = _dot_nt(q, k) * (scale * LOG2E)
                if diagonal:
                    r_i = lax.broadcasted_iota(jnp.int32, sc.shape, 0) + rc * c
                    c_i = lax.broadcasted_iota(jnp.int32, sc.shape, 1)
                    sc = jnp.where(c_i <= r_i, sc, NEG_INF)
                p = jnp.exp2(sc - lse_ref[0, rows, :])
                do = do_ref[rows, :]
                dd = jnp.sum(do * y_ref[rows, :], -1, keepdims=True)
                dp = _dot_nt(do, v_ref[0, keys, :])
                ds = (p * (dp - dd) * scale).astype(MXU)
                dv_acc[keys, :] += _dot_tn(p, do)
                dk_acc[keys, :] += _dot_tn(ds, q)
                out_rows = pl.ds(pl.multiple_of(qi * t + rc * c, rc), rc)
                dq_ref[0, out_rows, :] += _dot(ds, k)

        @pl.when(qi > kj)
        def _():
            step(False)

        @pl.when(qi == kj)
        def _():
            step(True)

        @pl.when(qi == nb - 1)
        def _():
            dk_ref[0] = dk_acc[...]
            dv_ref[0] = dv_acc[...]

    qmap = lambda h, j, i: (h, jnp.maximum(i, j), 0)
    return _pcall(
        body, name="mla_bwd", grid=(nh, nb, nb),
        out_shape=[_sds((nh, s, 256), jnp.float32), _sds((nh, s, 256), jnp.float32),
                   _sds((nh, s, MLA_V), jnp.float32)],
        in_specs=[pl.BlockSpec((1, t, 256), qmap),
                  pl.BlockSpec((1, t, 256), lambda h, j, i: (h, j, 0)),
                  pl.BlockSpec((1, t, MLA_V), lambda h, j, i: (h, j, 0)),
                  pl.BlockSpec((t, MLA_V), lambda h, j, i: (jnp.maximum(i, j), 8 + h)),
                  pl.BlockSpec((t, MLA_V), lambda h, j, i: (jnp.maximum(i, j), h)),
                  pl.BlockSpec((1, t, 1), qmap)],
        out_specs=[pl.BlockSpec((1, s, 256), lambda h, j, i: (h, 0, 0)),
                   pl.BlockSpec((1, t, 256), lambda h, j, i: (h, j, 0)),
                   pl.BlockSpec((1, t, MLA_V), lambda h, j, i: (h, j, 0))],
        scratch=[pltpu.VMEM((t, 256), jnp.float32), pltpu.VMEM((t, MLA_V), jnp.float32)],
        sem=("parallel", "arbitrary", "arbitrary"), after=after)(qc, kc, v, d_y, y_b, lse)


def _mla_prep_bwd(proj, cos, sin, g_cq, g_ckv, w_uq, w_ukv, g_qn, g_qr, g_kn, g_kr,
                  qb, kvb, cqn, ckvn, dqc, dkc, dv):
    s = proj.shape[0]
    tm = min(ROW_TILE, s)
    nh = MLA_HEADS
    ni = s // tm

    def body(cq_ref, ckv_ref, kr_ref, cos_ref, sin_ref, gcq_ref, gckv_ref, wuq_ref, wukv_ref,
             gqn_ref, gqr_ref, gkn_ref, gkr_ref, qb_ref, kvb_ref, cqn_ref, ckvn_ref, dqc_ref, dkc_ref, dv_ref,
             dcq_ref, dckv_ref, dkr_ref, dwuq_ref, dwukv_ref,
             dgcq_ref, dgckv_ref, dgqn_ref, dgqr_ref, dgkn_ref, dgkr_ref, acc_uq, acc_ukv):
        i = pl.program_id(0)

        @pl.when(i == 0)
        def _():
            acc_uq[...] = jnp.zeros_like(acc_uq)
            acc_ukv[...] = jnp.zeros_like(acc_ukv)
            for ref in (dgcq_ref, dgckv_ref, dgqn_ref, dgqr_ref, dgkn_ref, dgkr_ref):
                ref[...] = jnp.zeros_like(ref)

        cos_t, sin_t = cos_ref[...], sin_ref[...]
        lo = _lo_mask((tm, LANES))
        qb_v, kvb_v = qb_ref[...], kvb_ref[...]
        dq_parts, dgqn = [], jnp.zeros((1, LANES), jnp.float32)
        for h in range(nh):
            _, xn, r = _norm_fwd(qb_v[:, MLA_NOPE * h: MLA_NOPE * (h + 1)], gqn_ref[...])
            dx, dg = _norm_bwd(xn, r, gqn_ref[...], dqc_ref[h][:, :MLA_NOPE])
            dq_parts.append(dx)
            dgqn = dgqn + dg
        dgqn_ref[...] += dgqn
        dgqr = jnp.zeros((1, LANES), jnp.float32)
        for j in range(nh // 2):
            d_rope = jnp.where(lo, dqc_ref[2 * j][:, MLA_NOPE:], dqc_ref[2 * j + 1][:, MLA_NOPE:])
            d_pre = _rope_bwd(d_rope, cos_t, sin_t)
            xr = qb_v[:, nh * MLA_NOPE + LANES * j: nh * MLA_NOPE + LANES * (j + 1)]
            _, xn, r = _norm_fwd(xr, gqr_ref[...], half=True)
            dx, dg = _norm_bwd(xn, r, gqr_ref[...], d_pre, half=True)
            dq_parts.append(dx)
            dgqr = dgqr + dg
        dgqr_ref[...] += dgqr
        dqb = jnp.concatenate(dq_parts, axis=1).astype(MXU)
        acc_uq[...] += _dot_tn(dqb, cqn_ref[...])
        _, xn, r = _norm_fwd(cq_ref[...], gcq_ref[...])
        dx, dg = _norm_bwd(xn, r, gcq_ref[...], _dot(dqb, wuq_ref[...]))
        dcq_ref[...] = dx.astype(dcq_ref.dtype)
        dgcq_ref[...] += dg
        dkv_parts, dgkn = [], jnp.zeros((1, LANES), jnp.float32)
        d_kr2 = jnp.zeros((tm, LANES), jnp.float32)
        for h in range(nh):
            _, xn, r = _norm_fwd(kvb_v[:, 256 * h: 256 * h + MLA_NOPE], gkn_ref[...])
            dx, dg = _norm_bwd(xn, r, gkn_ref[...], dkc_ref[h][:, :MLA_NOPE])
            dkv_parts += [dx, dv_ref[h]]
            dgkn = dgkn + dg
            d_kr2 = d_kr2 + dkc_ref[h][:, MLA_NOPE:]
        dgkn_ref[...] += dgkn
        dkvb = jnp.concatenate(dkv_parts, axis=1).astype(MXU)
        part_ukv = _dot_tn(ckvn_ref[...], dkvb)
        for dev in range(N_DEV):
            acc_ukv[dev] += part_ukv[:, LANES * dev: LANES * (dev + 1)]
        w_ukv_full = jnp.concatenate([wukv_ref[dev] for dev in range(N_DEV)], axis=1)
        d_ckvn = _dot_nt(dkvb, w_ukv_full)
        _, xn, r = _norm_fwd(ckv_ref[...], gckv_ref[...])
        dx, dg = _norm_bwd(xn, r, gckv_ref[...], d_ckvn)
        dckv_ref[...] = dx.astype(dckv_ref.dtype)
        dgckv_ref[...] += dg
        d_kr = jnp.where(lo, d_kr2 + pltpu.roll(d_kr2, 64, 1), 0.0)
        d_pre = _rope_bwd(d_kr, cos_t, sin_t)
        _, xn, r = _norm_fwd(kr_ref[...], gkr_ref[...], half=True)
        dx, dg = _norm_bwd(xn, r, gkr_ref[...], d_pre, half=True)
        dkr_ref[...] = jnp.where(lo, dx, 0.0).astype(dkr_ref.dtype)
        dgkr_ref[...] += jnp.where(_lo_mask((1, LANES)), dg, 0.0)

        @pl.when(i == ni - 1)
        def _():
            dwuq_ref[...] = acc_uq[...].astype(dwuq_ref.dtype)
            dwukv_ref[...] = acc_ukv[...].astype(dwukv_ref.dtype)

    def col(width, start):
        return pl.BlockSpec((tm, width), lambda i: (i, start // width))

    def full(shape):
        return pl.BlockSpec(shape, lambda i: (0,) * len(shape))

    def row(width):
        return pl.BlockSpec((tm, width), lambda i: (i, 0))

    def heads(width):
        return pl.BlockSpec((nh, tm, width), lambda i: (0, i, 0))

    vec = full((1, LANES))
    return _pcall(
        body, name="mla_prep_bwd", grid=(ni,),
        out_shape=[_sds((s, 512), MXU), _sds((s, 512), MXU), _sds((s, LANES), MXU),
                   _sds((768, 512), WIRE), _sds((N_DEV, 512, LANES), WIRE),
                   _sds((1, 512), jnp.float32), _sds((1, 512), jnp.float32)] + [_sds((1, LANES), jnp.float32)] * 4,
        in_specs=[col(512, C_CQ), col(512, C_CKV), col(LANES, C_KR), row(LANES), row(LANES),
                  full((1, 512)), full((1, 512)), full((768, 512)), full((N_DEV, 512, LANES)), vec, vec, vec, vec,
                  row(768), row(1024), row(512), row(512), heads(256), heads(256), heads(MLA_V)],
        out_specs=[row(512), row(512), row(LANES), full((768, 512)), full((N_DEV, 512, LANES)),
                   full((1, 512)), full((1, 512)), vec, vec, vec, vec],
        scratch=[pltpu.VMEM((768, 512), jnp.float32), pltpu.VMEM((N_DEV, 512, LANES), jnp.float32)],
        sem=("arbitrary",))(proj, proj, proj, cos, sin, g_cq, g_ckv, w_uq, w_ukv, g_qn, g_qr, g_kn, g_kr,
                            qb, kvb, cqn, ckvn, dqc, dkc, dv)


def _swa_bwd(proj, posc, posr, gq, gk, sinks, d_y, y_a, lse, after):
    s = proj.shape[0]
    b = SWA_BLOCK
    nb = s // b
    scale = SWA_DIM ** -0.5

    def body(q_ref, kp_ref, kc_ref, vp_ref, vc_ref, pq_ref, pkp_ref, pkc_ref, gq_ref, gk_ref, sink_ref,
             do_ref, y_ref, lse_ref, kfull_ref,
             dq_ref, dk_ref, dv_ref, dgq_ref, dgk_ref, dsink_ref, dk_acc, dv_acc):
        n = pl.program_id(0)

        @pl.when(n == 0)
        def _():
            dk_acc[...] = jnp.zeros_like(dk_acc)
            dv_acc[...] = jnp.zeros_like(dv_acc)
            dgq_ref[...] = jnp.zeros_like(dgq_ref)
            dsink_ref[...] = jnp.zeros_like(dsink_ref)

        kn, v, bias = _swa_common(n, kp_ref[...], kc_ref[...], vp_ref[...], vc_ref[...],
                                  pq_ref[...], pkp_ref[...], pkc_ref[...], gk_ref[...])
        lo = _lo_mask((b, LANES))
        col = lax.broadcasted_iota(jnp.int32, (b, SWA_Q_HEADS), 1)
        col1 = lax.broadcasted_iota(jnp.int32, (1, SWA_Q_HEADS), 1)
        lse_t = lse_ref[...]
        dk_blk = jnp.zeros((2 * b, LANES), jnp.float32)
        dv_blk = jnp.zeros((2 * b, LANES), jnp.float32)
        dgq = jnp.zeros((1, LANES), jnp.float32)
        dsink = jnp.zeros((1, SWA_Q_HEADS), jnp.float32)
        for j in range(SWA_Q_HEADS // 2):
            hk = (2 * j) // (SWA_Q_HEADS // SWA_KV_HEADS)
            kvmask = lo if hk == 0 else jnp.logical_not(lo)
            sl = slice(LANES * j, LANES * (j + 1))
            qn, xn, r = _norm_fwd(q_ref[:, sl], gq_ref[...], half=True)
            qn = qn * scale
            qsw = pltpu.roll(qn, 64, 1)
            d2 = do_ref[:, sl]
            d2sw = pltpu.roll(d2, 64, 1)
            prod = d2 * y_ref[:, sl]
            dqs = []
            for e in range(2):
                h = 2 * j + e
                half_e = lo if e == 0 else jnp.logical_not(lo)
                qm = jnp.where(kvmask, qn if e == hk else qsw, 0.0)
                dm = jnp.where(kvmask, d2 if e == hk else d2sw, 0.0)
                sc = _dot_nt(qm, kn) + _alibi_slope(h) * bias
                lse_h = jnp.sum(jnp.where(col == h, lse_t, 0.0), -1, keepdims=True)
                p = jnp.exp(sc - lse_h)
                dd = jnp.sum(jnp.where(half_e, prod, 0.0), -1, keepdims=True)
                dp = _dot_nt(dm, v)
                ds = (p * (dp - dd)).astype(MXU)
                dsink = dsink - jnp.where(col1 == h, jnp.sum(jnp.exp(sink_ref[h] - lse_h) * dd), 0.0)
                dq_m = _dot(ds, kn) * scale
                dk_blk = dk_blk + _dot_tn(ds, qm)
                dv_blk = dv_blk + _dot_tn(p, dm)
                dqs.append(dq_m if e == hk else pltpu.roll(dq_m, 64, 1))
            dx, dg = _norm_bwd(xn, r, gq_ref[...], jnp.where(lo, dqs[0], dqs[1]), half=True)
            dq_ref[:, sl] = dx.astype(dq_ref.dtype)
            dgq = dgq + dg
        dgq_ref[...] += dgq
        dsink_ref[...] += dsink
        prev = pl.ds(pl.multiple_of(jnp.maximum(n - 1, 0) * b, b), b)
        cur = pl.ds(pl.multiple_of(n * b, b), b)
        dk_acc[prev, :] += dk_blk[:b]
        dv_acc[prev, :] += dv_blk[:b]
        dk_acc[cur, :] += dk_blk[b:]
        dv_acc[cur, :] += dv_blk[b:]

        @pl.when(n == nb - 1)
        def _():
            _, kxn, kr = _norm_fwd(kfull_ref[...], gk_ref[...], half=True)
            dx, dg = _norm_bwd(kxn, kr, gk_ref[...], dk_acc[...], half=True)
            dk_ref[...] = dx.astype(dk_ref.dtype)
            dv_ref[...] = dv_acc[...].astype(dv_ref.dtype)
            dgk_ref[...] = dg

    full = pl.BlockSpec((s, LANES), lambda n: (0, 0))
    vec = pl.BlockSpec((1, LANES), lambda n: (0, 0))
    return _pcall(
        body, name="swa_bwd", grid=(nb,),
        out_shape=[_sds((s, 1024), MXU), _sds((s, LANES), MXU), _sds((s, LANES), MXU),
                   _sds((1, LANES), jnp.float32), _sds((1, LANES), jnp.float32),
                   _sds((1, SWA_Q_HEADS), jnp.float32)],
        in_specs=_swa_specs(s) + [pl.BlockSpec((b, 1024), lambda n: (n, 0)), pl.BlockSpec((b, 1024), lambda n: (n, 0)),
                                  pl.BlockSpec((b, SWA_Q_HEADS), lambda n: (n, 0)),
                                  pl.BlockSpec((s, LANES), lambda n: (0, C_KA // LANES))],
        out_specs=[pl.BlockSpec((b, 1024), lambda n: (n, 0)), full, full, vec, vec,
                   pl.BlockSpec((1, SWA_Q_HEADS), lambda n: (0, 0))],
        scratch=[pltpu.VMEM((s, LANES), jnp.float32), pltpu.VMEM((s, LANES), jnp.float32)],
        sem=("arbitrary",), after=after)(proj, proj, proj, proj, proj, posc, posr, posr, gq, gk, sinks, d_y, y_a, lse,
                                         proj)


def _dx(d_proj, w_in, x, g, d_h1, after):
    s, d = x.shape
    n = w_in.shape[0]
    tm = min(2 * ROW_TILE, s)

    n_pc = len(d_proj)

    def body(*refs):
        dp_refs, (w_ref, x_ref, g_ref, dh_ref, dx_ref, dg_ref) = refs[:n_pc], refs[n_pc:]
        i = pl.program_id(0)

        @pl.when(i == 0)
        def _():
            dg_ref[...] = jnp.zeros_like(dg_ref)

        d_hn = _dot(jnp.concatenate([r[...] for r in dp_refs], axis=1), w_ref[...])
        _, xn, r = _norm_fwd(x_ref[...], g_ref[...])
        dx, dg = _norm_bwd(xn, r, g_ref[...], d_hn)
        dx_ref[...] = dh_ref[...] + dx
        dg_ref[...] += dg

    row = pl.BlockSpec((tm, d), lambda i: (i, 0))
    vec = pl.BlockSpec((1, d), lambda i: (0, 0))
    return _pcall(
        body, name="grad_x", grid=(s // tm,),
        out_shape=[_sds((s, d), jnp.float32), _sds((1, d), jnp.float32)],
        in_specs=[pl.BlockSpec((tm, p.shape[1]), lambda i: (i, 0)) for p in d_proj] + [
                  pl.BlockSpec((n, d), lambda i: (0, 0), pipeline_mode=pl.Buffered(1)), row, vec, row],
        out_specs=[row, vec], sem=("arbitrary",), after=after)(*d_proj, w_in, x, g, d_h1)


_SMALL = ["attn_norm_g", "swa_q_norm_g", "swa_k_norm_g", "swa_sinks", "mla_cq_norm_g", "mla_ckv_norm_g",
          "mla_qn_norm_g", "mla_qr_norm_g", "mla_kn_norm_g", "mla_kr_norm_g", "mem_norm_g",
          "mem_q_norm_g", "mem_k_norm_g", "ffn_norm_g"]


def kernel(x, mem, positions, attn_norm_g, w_in, swa_q_norm_g, swa_k_norm_g, swa_sinks, mla_cq_norm_g, mla_ckv_norm_g, w_uq, w_ukv, mla_qn_norm_g, mla_qr_norm_g, mla_kn_norm_g, mla_kr_norm_g, mem_norm_g, w_mem_kv, mem_q_norm_g, mem_k_norm_g, w_out, ffn_norm_g, w_gate, w_up, w_down, loss_target, m_attn_norm_g, m_w_in, m_swa_q_norm_g, m_swa_k_norm_g, m_swa_sinks, m_mla_cq_norm_g, m_mla_ckv_norm_g, m_w_uq, m_w_ukv, m_mla_qn_norm_g, m_mla_qr_norm_g, m_mla_kn_norm_g, m_mla_kr_norm_g, m_mem_norm_g, m_w_mem_kv, m_mem_q_norm_g, m_mem_k_norm_g, m_w_out, m_ffn_norm_g, m_w_gate, m_w_up, m_w_down, v_attn_norm_g, v_w_in, v_swa_q_norm_g, v_swa_k_norm_g, v_swa_sinks, v_mla_cq_norm_g, v_mla_ckv_norm_g, v_w_uq, v_w_ukv, v_mla_qn_norm_g, v_mla_qr_norm_g, v_mla_kn_norm_g, v_mla_kr_norm_g, v_mem_norm_g, v_w_mem_kv, v_mem_q_norm_g, v_mem_k_norm_g, v_w_out, v_ffn_norm_g, v_w_gate, v_w_up, v_w_down):
    args = dict(locals())
    x2, mem2, tgt = x[0], mem[0], loss_target[0]
    s, d = x2.shape
    n_in = w_in.shape[2]
    f = w_gate.shape[2]

    (g_in,) = _all_gather([w_in[0].T.astype(WIRE)])
    mix_shards = [w_uq[0].T.astype(WIRE), w_ukv[0].astype(WIRE), w_mem_kv[0].astype(WIRE),
                  _to_wire([w_out[0]], g_in, "wire_out")[0]]
    g_uq, wkv, g_mkv, g_out = _all_gather_background(mix_shards, 5, "all_gather_mix_weights")
    ffn_shards = [_to_wire([w_gate[0].T, w_up[0].T], g_in, "wire_gate_up")]
    (w_gu,) = _all_gather_background(ffn_shards, 1, "all_gather_ffn_weights")
    down_shards = [_to_wire([w_down[0]], g_in, "wire_down")[0]]
    (w_d,) = _all_gather_background(down_shards, 6, "all_gather_down_weights")
    wi = g_in.reshape(N_DEV * n_in, d)
    wi = jnp.concatenate([wi[0:1024], wi[1280:1792], wi[1792:2304], wi[2368:2880],
                          wi[1024:1152], wi[1152:1280], wi[2304:2368],
                          jnp.zeros((IN_PAD - 2880, d), wi.dtype)], axis=0)
    wq = g_uq.reshape(768, 512)
    wq = jnp.concatenate([wq[192 * h: 192 * h + 128] for h in range(4)]
                         + [wq[192 * h + 128: 192 * (h + 1)] for h in range(4)], axis=0)
    wmkv = g_mkv.reshape(-1, g_mkv.shape[-1])
    wo = g_out.reshape(-1, d)

    pos = positions[0].astype(jnp.float32)
    inv_freq = ROPE_THETA ** (-jnp.arange(0, MLA_ROPE, 2, dtype=jnp.float32) / MLA_ROPE)
    ang = pos[:, None] * inv_freq
    cos32, sin32 = jnp.cos(ang), jnp.sin(ang)
    cos_t = jnp.tile(cos32, (1, 4))
    sin_t = jnp.tile(jnp.concatenate([-sin32, sin32], axis=1), (1, 2))
    posc, posr = pos.reshape(s, 1), pos.reshape(1, s)
    two = lambda g: jnp.tile(g, (1, 2))
    gq2, gk2, gqr2, gkr2 = two(swa_q_norm_g), two(swa_k_norm_g), two(mla_qr_norm_g), two(mla_kr_norm_g)
    sinks1 = swa_sinks[0]

    proj, hn = _in_proj(x2, attn_norm_g, wi)
    qc, kc, vb, qb, kvb, cqn, ckvn = _mla_prep(proj, cos_t, sin_t, mla_cq_norm_g, mla_ckv_norm_g, wq, wkv,
                                                mla_qn_norm_g, gqr2, mla_kn_norm_g, gkr2)
    y_b, lse_b = _mla_fwd(qc, kc, vb)
    km, vmm, kvm, memn = _memkv_prep(mem2, mem_norm_g, wmkv, mem_k_norm_g)
    y_m, lse_m = _mem_fwd(proj, mem_q_norm_g, km, vmm)
    y_a, lse_a = _swa_fwd(proj, posc, posr, gq2, gk2, sinks1)
    h1, fn = _out_proj(y_a, y_b, y_m, x2, wo, ffn_norm_g)
    gu, act = _ffn_gu(fn, w_gu)
    dout, dout_b, loss_tile = _ffn_down(act, w_d, h1, tgt)

    dgu, dw_d = _ffn_bwd_act(dout_b, w_d, gu)
    dw_gu = _ffn_dw_gu(fn, dgu)
    r_gu, r_d = _exchange_grads_background([dw_gu, dw_d], 2, "exchange_ffn_grads")
    d_h1, dg_ffn = _ffn_norm_bwd(_ffn_dfn(dgu, w_gu, dw_gu), dout, h1, ffn_norm_g)
    d_y = _mm(d_h1, wo, tb=True, out_dtype=jnp.float32, tm=FFN_TILE, tk=2048, name="d_mix")
    dw_out = jnp.concatenate([
        _mm(y_a, d_h1, ta=True, out_dtype=WIRE, tm=1024, tk=1024, name="dw_out_a"),
        _mm(y_b, d_h1, ta=True, out_dtype=WIRE, tm=1024, tk=1024, name="dw_out_b"),
        _mm(y_m, d_h1, ta=True, out_dtype=WIRE, tm=1024, tk=1024, name="dw_out_m")], axis=0)
    d_qm, dkm, dvmm, dg_mq = _mem_bwd(proj, mem_q_norm_g, km, vmm, d_y, y_m, lse_m)
    dw_mkv, dg_mem, dg_mk = _memkv_bwd(mem2, mem_norm_g, wmkv, mem_k_norm_g, kvm, memn, dkm, dvmm)
    r_mkv, r_out = _exchange_grads_background([dw_mkv.reshape(g_mkv.shape), dw_out.reshape(g_out.shape)], 3,
                                              "exchange_mix_grads")
    dqc, dkc, dvb = _mla_bwd(qc, kc, vb, d_y, y_b, lse_b, dw_mkv)
    (d_cq, d_ckv, d_kr, dw_uq, dw_ukv, dg_cq, dg_ckv, dg_qn, dg_qr, dg_kn, dg_kr) = _mla_prep_bwd(
        proj, cos_t, sin_t, mla_cq_norm_g, mla_ckv_norm_g, wq, wkv, mla_qn_norm_g, gqr2, mla_kn_norm_g, gkr2,
        qb, kvb, cqn, ckvn, dqc, dkc, dvb)
    d_qa, d_ka, d_va, dg_q, dg_k, d_sinks = _swa_bwd(proj, posc, posr, gq2, gk2, sinks1, d_y, y_a, lse_a, dw_out)
    d_proj = [d_qa, d_cq, d_ckv, d_qm, d_ka, d_va, d_kr]
    gi = _dw_in(hn, d_proj, n_in)

    gq_ = jnp.concatenate(sum([[dw_uq[128 * h: 128 * (h + 1)], dw_uq[512 + 64 * h: 512 + 64 * (h + 1)]]
                               for h in range(4)], []), axis=0)
    gq_ = gq_.reshape(N_DEV, 96, 512)
    r_in, r_uq, r_ukv = _exchange_grads_background([gi, gq_, dw_ukv], 4, "exchange_in_grads")
    grad_x, dg_attn = _dx(d_proj, wi, x2, attn_norm_g, d_h1, gi)

    big = {}
    last = [None]

    def adam(name, r, transposed=False, which=None):
        w, m, v = args[name][0], args["m_" + name][0], args["v_" + name][0]
        if transposed:
            outs = _adam_big(r, w.T, m.T, v.T, "adam_" + name, last[0], which)
            big[name] = [o.T[None] for o in outs]
        else:
            outs = _adam_big(r, w, m, v, "adam_" + name, last[0])
            big[name] = [o[None] for o in outs]
        last[0] = outs[0]

    adam("w_gate", r_gu, True, which=0)
    adam("w_up", r_gu, True, which=1)
    adam("w_down", r_d)
    adam("w_out", r_out)
    adam("w_mem_kv", r_mkv)
    adam("w_in", r_in, True)
    adam("w_uq", r_uq, True)
    adam("w_ukv", r_ukv)

    small_g = {
        "attn_norm_g": dg_attn, "swa_q_norm_g": dg_q, "swa_k_norm_g": dg_k,
        "swa_sinks": d_sinks, "mla_cq_norm_g": dg_cq, "mla_ckv_norm_g": dg_ckv, "mla_qn_norm_g": dg_qn,
        "mla_qr_norm_g": dg_qr, "mla_kn_norm_g": dg_kn, "mla_kr_norm_g": dg_kr,
        "mem_norm_g": dg_mem, "mem_q_norm_g": dg_mq, "mem_k_norm_g": dg_mk, "ffn_norm_g": dg_ffn}
    packed_g = _small_allreduce([small_g[n] for n in _SMALL], loss_tile, [args[n].shape[-1] for n in _SMALL])
    loss11, small_out = _small_adam(packed_g, [args[n] for n in _SMALL],
                                    [args["m_" + n] for n in _SMALL], [args["v_" + n] for n in _SMALL])
    small = dict(zip(_SMALL, small_out))
    loss = loss11.reshape(())

    order = ["attn_norm_g", "w_in", "swa_q_norm_g", "swa_k_norm_g", "swa_sinks", "mla_cq_norm_g", "mla_ckv_norm_g",
             "w_uq", "w_ukv", "mla_qn_norm_g", "mla_qr_norm_g", "mla_kn_norm_g", "mla_kr_norm_g", "mem_norm_g",
             "w_mem_kv", "mem_q_norm_g", "mem_k_norm_g", "w_out", "ffn_norm_g", "w_gate", "w_up", "w_down"]
    res = {n: (big[n] if n in big else list(small[n])) for n in order}
    outs = [loss, grad_x[None]]
    for kind in range(4):
        outs += [res[n][kind] for n in order]
    return tuple(outs)
```

```python
import jax
import jax.numpy as jnp
from jax import lax
from jax.experimental import pallas as pl
from jax.experimental.pallas import tpu as pltpu
from jax.experimental.pallas import tpu_sc as plsc

MXU = jnp.bfloat16
WIRE = jnp.bfloat16
EPS = 1e-6
NEG_INF = -1e30
LOG2E = 1.4426950408889634
N_DEV = 8
LANES = 128
ROW_TILE = 256
FFN_TILE = 512
ATT_TILE = 1024
SWA_BLOCK = 128
VMEM_LIMIT = 56 * 1024 * 1024

SWA_Q_HEADS, SWA_KV_HEADS, SWA_DIM = 16, 2, 64
MLA_HEADS, MLA_NOPE, MLA_ROPE, MLA_V = 4, 128, 64, 128
MEM_HEADS, MEM_DIM = 4, 128
ROPE_THETA = 10000.0
ADAM_LR, ADAM_B1, ADAM_B2, ADAM_EPS, ADAM_WD, ADAM_STEP = 0.001, 0.9, 0.999, 1e-08, 0.01, 10

C_QA, C_CQ, C_CKV, C_QM, C_KA, C_VA, C_KR, IN_PAD = 0, 1024, 1536, 2048, 2560, 2688, 2816, 2944


def _pcall(body, *, name, out_shape, in_specs, out_specs, grid=(), scratch=(), sem=None, after=None):
    params = pltpu.CompilerParams(dimension_semantics=sem, vmem_limit_bytes=VMEM_LIMIT)
    if after is not None:
        n_in, inner = len(in_specs), body

        def body(*refs):
            inner(*refs[:n_in], *refs[n_in + 1:])

        in_specs = list(in_specs) + [pl.BlockSpec(memory_space=pl.ANY)]
    call = pl.pallas_call(body, name=name, grid=grid, in_specs=in_specs, out_specs=out_specs,
                          out_shape=out_shape, scratch_shapes=list(scratch), compiler_params=params)
    return call if after is None else (lambda *ops: call(*ops, after))


def _sds(shape, dtype):
    return jax.ShapeDtypeStruct(tuple(shape), dtype)


def _dot(a, b):
    return jnp.dot(a.astype(MXU), b.astype(MXU), preferred_element_type=jnp.float32)


def _dot_nt(a, b):
    return lax.dot_general(a.astype(MXU), b.astype(MXU), (((1,), (1,)), ((), ())),
                           preferred_element_type=jnp.float32)


def _dot_tn(a, b):
    return lax.dot_general(a.astype(MXU), b.astype(MXU), (((0,), (0,)), ((), ())),
                           preferred_element_type=jnp.float32)


def _lo_mask(shape):
    return (lax.broadcasted_iota(jnp.int32, shape, len(shape) - 1) % LANES) < 64


def _norm_fwd(x, g, half=False):
    x2 = x * x
    if half:
        lo = _lo_mask(x.shape)
        s_lo = jnp.sum(jnp.where(lo, x2, 0.0), -1, keepdims=True)
        s_hi = jnp.sum(jnp.where(lo, 0.0, x2), -1, keepdims=True)
        r = jnp.where(lo, lax.rsqrt(s_lo / 64.0 + EPS), lax.rsqrt(s_hi / 64.0 + EPS))
    else:
        r = lax.rsqrt(jnp.mean(x2, -1, keepdims=True) + EPS)
    xn = x * r
    return xn * g, xn, r


def _norm_bwd(xn, r, g, dy, half=False):
    t = dy * g
    tx = t * xn
    if half:
        lo = _lo_mask(xn.shape)
        m_lo = jnp.sum(jnp.where(lo, tx, 0.0), -1, keepdims=True) / 64.0
        m_hi = jnp.sum(jnp.where(lo, 0.0, tx), -1, keepdims=True) / 64.0
        m = jnp.where(lo, m_lo, m_hi)
    else:
        m = jnp.mean(tx, -1, keepdims=True)
    dx = r * (t - xn * m)
    dg = jnp.sum(dy * xn, 0, keepdims=True)
    return dx, dg


def _swap32(x):
    lane = lax.broadcasted_iota(jnp.int32, x.shape, 1)
    return jnp.where((lane % 64) < 32, pltpu.roll(x, 96, 1), pltpu.roll(x, 32, 1))


def _rope(x, cos, sin):
    return x * cos + _swap32(x) * sin


def _rope_bwd(d, cos, sin):
    return d * cos + _swap32(d * sin)


def _my_coords():
    return lax.axis_index("x"), lax.axis_index("y"), lax.axis_index("c")


def _dev_index(px, py, pc):
    return 4 * px + 2 * py + pc


_FLIPS = [(0, 0, 1), (0, 1, 0), (0, 1, 1), (1, 0, 0), (1, 0, 1), (1, 1, 0), (1, 1, 1)]


def _flip(coords, f):
    return tuple((1 - v) if b else v for v, b in zip(coords, f))


def _all_gather(shards):
    n = len(shards)

    def body(*refs):
        ins, outs = refs[:n], refs[n:2 * n]
        send_sems, recv_sems, local_sems = refs[2 * n:]
        x, y, c = _my_coords()
        me, sibling = (x, y, c), (x, y, 1 - c)
        chips = [(1 - x, y), (x, 1 - y), (1 - x, 1 - y)]

        def copy(w, k, block, to, src=None):
            dst = outs[w].at[_dev_index(*block)]
            return pltpu.make_async_remote_copy(
                src_ref=dst if src is None else src, dst_ref=dst,
                send_sem=send_sems.at[w, k], recv_sem=recv_sems.at[w, k],
                device_id=to, device_id_type=pl.DeviceIdType.MESH)

        sends, locals_ = [], []
        for w in range(n):
            mine = pltpu.make_async_copy(ins[w], outs[w].at[_dev_index(*me)], local_sems.at[w])
            mine.start()
            locals_.append(mine)
            first = [copy(w, 0, me, sibling, src=ins[w])]
            first += [copy(w, 1 + j, me, (*chip, c), src=ins[w]) for j, chip in enumerate(chips)]
            for cp in first:
                cp.start()
            sends += first
        for w in range(n):
            for j, chip in enumerate(chips):
                copy(w, 1 + j, (*chip, c), me).wait_recv()
                fwd = copy(w, 4 + j, (*chip, c), sibling)
                fwd.start()
                sends.append(fwd)
        for w in range(n):
            copy(w, 0, sibling, me).wait_recv()
            for j, chip in enumerate(chips):
                copy(w, 4 + j, (*chip, 1 - c), me).wait_recv()
        for cp in sends:
            cp.wait_send()
        for mine in locals_:
            mine.wait()

    any_spec = pl.BlockSpec(memory_space=pl.ANY)
    return _pcall(
        body, name="all_gather_weights",
        out_shape=[_sds((N_DEV,) + s.shape, s.dtype) for s in shards],
        in_specs=[any_spec] * n, out_specs=[any_spec] * n,
        scratch=[pltpu.SemaphoreType.DMA((n, 7)), pltpu.SemaphoreType.DMA((n, 7)),
                 pltpu.SemaphoreType.DMA((n,))])(*shards)


def _wire_cost(arrays):
    nbytes = sum(a.size * a.dtype.itemsize for a in arrays)
    return pl.CostEstimate(flops=0, transcendentals=0, bytes_accessed=40 * nbytes)


def _all_gather_background(shards, collective_id, name):
    n = len(shards)
    src_refs = [jax.new_ref(s, memory_space=pltpu.MemorySpace.HBM) for s in shards]
    out_refs = [jax.empty_ref(_sds((N_DEV,) + s.shape, s.dtype), memory_space=pltpu.MemorySpace.HBM) for s in shards]

    @pl.kernel(mesh=plsc.ScalarSubcoreMesh(axis_name="seq", num_cores=1), name=name,
               scratch_types=(pltpu.SemaphoreType.DMA((n, 7)), pltpu.SemaphoreType.DMA((n, 7)),
                              pltpu.SemaphoreType.DMA((n,))),
               compiler_params=pltpu.CompilerParams(collective_id=collective_id))
    def launch(send_sems, recv_sems, local_sems):
        x, y, c = _my_coords()
        me, sibling = (x, y, c), (x, y, 1 - c)
        chips = [(1 - x, y), (x, 1 - y), (1 - x, 1 - y)]
        barrier = pltpu.get_barrier_semaphore()
        for peer in [sibling] + [(*chip, c) for chip in chips]:
            pl.semaphore_signal(barrier, inc=1, device_id=peer, device_id_type=pl.DeviceIdType.MESH)
        pl.semaphore_wait(barrier, 4)

        def copy(w, k, block, to, src=None):
            dst = out_refs[w].at[_dev_index(*block)]
            return pltpu.make_async_remote_copy(
                src_ref=dst if src is None else src, dst_ref=dst,
                send_sem=send_sems.at[w, k], recv_sem=recv_sems.at[w, k],
                device_id=to, device_id_type=pl.DeviceIdType.MESH)

        sends, locals_ = [], []
        for w in range(n):
            mine = pltpu.make_async_copy(src_refs[w], out_refs[w].at[_dev_index(*me)], local_sems.at[w])
            mine.start()
            locals_.append(mine)
            first = [copy(w, 0, me, sibling, src=src_refs[w])]
            first += [copy(w, 1 + j, me, (*chip, c), src=src_refs[w]) for j, chip in enumerate(chips)]
            for cp in first:
                cp.start()
            sends += first
        for w in range(n):
            for j, chip in enumerate(chips):
                copy(w, 1 + j, (*chip, c), me).wait_recv()
                fwd = copy(w, 4 + j, (*chip, c), sibling)
                fwd.start()
                sends.append(fwd)
        for w in range(n):
            copy(w, 0, sibling, me).wait_recv()
            for j, chip in enumerate(chips):
                copy(w, 4 + j, (*chip, 1 - c), me).wait_recv()
        for cp in sends:
            cp.wait_send()
        for mine in locals_:
            mine.wait()

    launch()
    return [r[...] for r in out_refs]


def _exchange_grads(grads):
    n = len(grads)

    def body(*refs):
        ins, outs = refs[:n], refs[n:2 * n]
        send_sems, recv_sems, local_sems = refs[2 * n:]
        me = _my_coords()
        my_idx = _dev_index(*me)
        sends, locals_ = [], []
        for w in range(n):
            mine = pltpu.make_async_copy(ins[w].at[my_idx], outs[w].at[my_idx], local_sems.at[w])
            mine.start()
            locals_.append(mine)
            for k, f in enumerate(_FLIPS):
                peer = _flip(me, f)
                cp = pltpu.make_async_remote_copy(
                    src_ref=ins[w].at[_dev_index(*peer)], dst_ref=outs[w].at[my_idx],
                    send_sem=send_sems.at[w, k], recv_sem=recv_sems.at[w, k],
                    device_id=peer, device_id_type=pl.DeviceIdType.MESH)
                cp.start()
                sends.append(cp)
        for w in range(n):
            for k, f in enumerate(_FLIPS):
                peer = _flip(me, f)
                slot = outs[w].at[_dev_index(*peer)]
                pltpu.make_async_remote_copy(
                    src_ref=slot, dst_ref=slot,
                    send_sem=send_sems.at[w, k], recv_sem=recv_sems.at[w, k],
                    device_id=peer, device_id_type=pl.DeviceIdType.MESH).wait_recv()
        for cp in sends:
            cp.wait_send()
        for mine in locals_:
            mine.wait()

    any_spec = pl.BlockSpec(memory_space=pl.ANY)
    return _pcall(
        body, name="exchange_grads",
        out_shape=[_sds(g.shape, g.dtype) for g in grads],
        in_specs=[any_spec] * n, out_specs=[any_spec] * n,
        scratch=[pltpu.SemaphoreType.DMA((n, 7)), pltpu.SemaphoreType.DMA((n, 7)),
                 pltpu.SemaphoreType.DMA((n,))])(*grads)


def _exchange_grads_background(grads, collective_id, name):
    n = len(grads)
    src_refs = [jax.new_ref(g, memory_space=pltpu.MemorySpace.HBM) for g in grads]
    out_refs = [jax.empty_ref(_sds(g.shape, g.dtype), memory_space=pltpu.MemorySpace.HBM) for g in grads]

    @pl.kernel(mesh=plsc.ScalarSubcoreMesh(axis_name="seq", num_cores=1), name=name,
               scratch_types=(pltpu.SemaphoreType.DMA((n, 7)), pltpu.SemaphoreType.DMA((n, 7)),
                              pltpu.SemaphoreType.DMA((n,))),
               cost_estimate=_wire_cost(grads),
               compiler_params=pltpu.CompilerParams(collective_id=collective_id))
    def launch(send_sems, recv_sems, local_sems):
        me = _my_coords()
        my_idx = _dev_index(*me)
        peers = [_flip(me, f) for f in _FLIPS]
        barrier = pltpu.get_barrier_semaphore()
        for peer in peers:
            pl.semaphore_signal(barrier, inc=1, device_id=peer, device_id_type=pl.DeviceIdType.MESH)
        pl.semaphore_wait(barrier, len(peers))
        sends, locals_ = [], []
        for w in range(n):
            mine = pltpu.make_async_copy(src_refs[w].at[my_idx], out_refs[w].at[my_idx], local_sems.at[w])
            mine.start()
            locals_.append(mine)
            for k, peer in enumerate(peers):
                cp = pltpu.make_async_remote_copy(
                    src_ref=src_refs[w].at[_dev_index(*peer)], dst_ref=out_refs[w].at[my_idx],
                    send_sem=send_sems.at[w, k], recv_sem=recv_sems.at[w, k],
                    device_id=peer, device_id_type=pl.DeviceIdType.MESH)
                cp.start()
                sends.append(cp)
        for w in range(n):
            for k, peer in enumerate(peers):
                slot = out_refs[w].at[_dev_index(*peer)]
                pltpu.make_async_remote_copy(
                    src_ref=slot, dst_ref=slot, send_sem=send_sems.at[w, k], recv_sem=recv_sems.at[w, k],
                    device_id=peer, device_id_type=pl.DeviceIdType.MESH).wait_recv()
        for cp in sends:
            cp.wait_send()
        for mine in locals_:
            mine.wait()

    launch()
    return [r[...] for r in out_refs]


def _to_wire(parts, after, name):
    n = len(parts)
    rows, cols = parts[0].shape
    tr = rows // 2 if rows % 32 == 0 else rows

    def body(*refs):
        for k in range(n):
            refs[n][k] = refs[k][...].astype(WIRE)

    blk = pl.BlockSpec((tr, cols), lambda i: (i, 0))
    return _pcall(
        body, name=name, grid=(rows // tr,), out_shape=_sds((n, rows, cols), WIRE),
        in_specs=[blk] * n, out_specs=pl.BlockSpec((n, tr, cols), lambda i: (0, i, 0)),
        sem=("parallel",), after=after)(*parts)


def _adam_math(w, g, m, v):
    m = ADAM_B1 * m + (1.0 - ADAM_B1) * g
    v = ADAM_B2 * v + (1.0 - ADAM_B2) * (g * g)
    m_hat = m / (1.0 - ADAM_B1 ** ADAM_STEP)
    v_hat = v / (1.0 - ADAM_B2 ** ADAM_STEP)
    delta = -ADAM_LR * (m_hat / (jnp.sqrt(v_hat) + ADAM_EPS) + ADAM_WD * w)
    return delta, m, v


def _small_layout(sizes):
    row0, r = [], 0
    for n in sizes:
        row0.append(r)
        r += -(-n // LANES)
    return row0, r, -(-(r + 1) // 8) * 8


def _small_pieces(n):
    return [(k, min(LANES, n - LANES * k)) for k in range(-(-n // LANES))]


def _small_fill(pack, slot, srcs, sizes, row0, rows):
    pack[slot] = jnp.zeros((rows, LANES), jnp.float32)
    for p, n in enumerate(sizes):
        val = srcs[p][...]
        if val.shape[-1] == LANES and n == 64:
            pack[slot, row0[p]:row0[p] + 1, :] = val + pltpu.roll(val, 64, 1)
            continue
        for k, width in _small_pieces(n):
            pack[slot, row0[p] + k:row0[p] + k + 1, 0:width] = srcs[p][:, LANES * k:LANES * k + width]


def _small_pack(grads, loss_tile, sizes):
    n_par = len(sizes)
    row0, loss_row, rows = _small_layout(sizes)

    def body(*refs):
        g_refs, loss_in, out_ref = refs[:n_par], refs[n_par], refs[n_par + 1]
        _small_fill(out_ref, 0, g_refs, sizes, row0, rows)
        out_ref[0, loss_row:loss_row + 1, :] = loss_in[0:1, :]

    vm = pl.BlockSpec(memory_space=pltpu.VMEM)
    return _pcall(
        body, name="small_pack", out_shape=_sds((1, rows, LANES), jnp.float32),
        in_specs=[vm] * (n_par + 1), out_specs=vm)(*grads, loss_tile)


def _small_adam(packs, ws, ms, vs, after):
    sizes = [w.shape[-1] for w in ws]
    n_par = len(ws)
    row0, loss_row, rows = _small_layout(sizes)

    def body(*refs):
        g_ref = refs[0]
        w_refs, m_refs, v_refs = (refs[1 + k * n_par: 1 + (k + 1) * n_par] for k in range(3))
        loss_out = refs[3 * n_par + 1]
        out_refs = refs[3 * n_par + 2: 7 * n_par + 2]
        pack, res = refs[7 * n_par + 2:]
        for slot, srcs in enumerate((w_refs, m_refs, v_refs)):
            _small_fill(pack, slot, srcs, sizes, row0, rows)
        g = g_ref[0]
        for dev in range(1, N_DEV):
            g = g + g_ref[dev]
        delta, m, v = _adam_math(pack[0], g, pack[1], pack[2])
        res[0], res[1], res[2], res[3] = g, delta, m, v
        loss_out[...] = res[0, loss_row:loss_row + 1, 0:1]
        for p, n in enumerate(sizes):
            for kind in range(4):
                for k, width in _small_pieces(n):
                    out_refs[4 * p + kind][:, LANES * k:LANES * k + width] = (
                        res[kind, row0[p] + k:row0[p] + k + 1, 0:width])

    vm = pl.BlockSpec(memory_space=pltpu.VMEM)
    out_shape = [_sds((1, 1), jnp.float32)]
    for n in sizes:
        out_shape += [_sds((1, n), jnp.float32)] * 4
    outs = _pcall(
        body, name="small_adam", out_shape=out_shape,
        in_specs=[vm] * (3 * n_par + 1), out_specs=[vm] * len(out_shape),
        scratch=[pltpu.VMEM((3, rows, LANES), jnp.float32), pltpu.VMEM((4, rows, LANES), jnp.float32)],
        after=after)(packs, *ws, *ms, *vs)
    return outs[0], [outs[1 + 4 * p: 5 + 4 * p] for p in range(n_par)]


def _adam_big(recv, w, m, v, name, after=None, which=None):
    rows, cols = recv.shape[-2:]
    row_tiles = [t for t in range(16, rows + 1, 16) if rows % t == 0 and t * cols <= 400 * 1024]
    tr, tc = (max(row_tiles), cols) if row_tiles else (rows, 512 if cols % 512 == 0 else cols)

    def body(r_ref, w_ref, m_ref, v_ref, g_ref, d_ref, mo_ref, vo_ref):
        g = r_ref[0].astype(jnp.float32)
        for d in range(1, N_DEV):
            g = g + r_ref[d].astype(jnp.float32)
        delta, mn, vn = _adam_math(w_ref[...], g, m_ref[...], v_ref[...])
        g_ref[...] = g
        d_ref[...] = delta
        mo_ref[...] = mn
        vo_ref[...] = vn

    blk = pl.BlockSpec((tr, tc), lambda i, j: (i, j))
    if which is None:
        r_spec = pl.BlockSpec((N_DEV, tr, tc), lambda i, j: (0, i, j))
    else:
        r_spec = pl.BlockSpec((N_DEV, None, tr, tc), lambda i, j: (0, which, i, j))
    return _pcall(
        body, name=name, grid=(rows // tr, cols // tc),
        out_shape=[_sds((rows, cols), jnp.float32)] * 4,
        in_specs=[r_spec, blk, blk, blk],
        out_specs=[blk] * 4, sem=("parallel", "parallel"), after=after)(recv, w, m, v)


def _mm(a, b, *, ta=False, tb=False, out_dtype, tm, tk, name):
    (kdim, mdim) = a.shape if ta else a.shape[::-1]
    ndim = b.shape[0] if tb else b.shape[1]
    tm, tk = min(tm, mdim), min(tk, kdim)
    nk = kdim // tk

    def body(a_ref, b_ref, o_ref, acc):
        k = pl.program_id(1)
        if ta:
            part = _dot_tn(a_ref[...], b_ref[...])
        elif tb:
            part = _dot_nt(a_ref[...], b_ref[...])
        else:
            part = _dot(a_ref[...], b_ref[...])

        @pl.when(k == 0)
        def _():
            acc[...] = part

        @pl.when(k > 0)
        def _():
            acc[...] += part

        @pl.when(k == nk - 1)
        def _():
            o_ref[...] = acc[...].astype(o_ref.dtype)

    a_spec = pl.BlockSpec((tk, tm), lambda i, k: (k, i)) if ta else pl.BlockSpec((tm, tk), lambda i, k: (i, k))
    b_spec = pl.BlockSpec((ndim, tk), lambda i, k: (0, k)) if tb else pl.BlockSpec((tk, ndim), lambda i, k: (k, 0))
    return _pcall(
        body, name=name, grid=(mdim // tm, nk), out_shape=_sds((mdim, ndim), out_dtype),
        in_specs=[a_spec, b_spec], out_specs=pl.BlockSpec((tm, ndim), lambda i, k: (i, 0)),
        scratch=[pltpu.VMEM((tm, ndim), jnp.float32)], sem=("parallel", "arbitrary"))(a, b)


def _ref_col_pieces(start, stop):
    ref_starts = [0, 1024, 1152, 1280, 1792, 2304, 2368, 2880]
    perm_starts = [C_QA, C_KA, C_VA, C_CQ, C_CKV, C_KR, C_QM]
    out = []
    for p in range(7):
        lo, hi = max(start, ref_starts[p]), min(stop, ref_starts[p + 1])
        if lo < hi:
            out.append((lo - start, perm_starts[p] + lo - ref_starts[p], hi - lo))
    return out


def _dw_in(hn, d_proj, n_shard):
    s, d = hn.shape
    n = sum(p.shape[1] for p in d_proj)
    n_pc = len(d_proj)
    tm, tk = min(512, d), min(1024, s)
    nk = s // tk

    def body(a_ref, *refs):
        b_refs, (o_ref, acc) = refs[:n_pc], refs[n_pc:]
        k = pl.program_id(1)
        part = _dot_tn(a_ref[...], jnp.concatenate([r[...] for r in b_refs], axis=1))

        @pl.when(k == 0)
        def _():
            acc[...] = part

        @pl.when(k > 0)
        def _():
            acc[...] += part

        @pl.when(k == nk - 1)
        def _():
            t = acc[...].T
            for j in range(N_DEV):
                rows = [t[src:src + width] for _, src, width in _ref_col_pieces(j * n_shard, (j + 1) * n_shard)]
                o_ref[j] = jnp.concatenate(rows, axis=0).astype(o_ref.dtype)

    return _pcall(
        body, name="dw_in", grid=(d // tm, nk), out_shape=_sds((N_DEV, n_shard, d), WIRE),
        in_specs=[pl.BlockSpec((tk, tm), lambda i, k: (k, i))]
        + [pl.BlockSpec((tk, p.shape[1]), lambda i, k: (k, 0)) for p in d_proj],
        out_specs=pl.BlockSpec((N_DEV, n_shard, tm), lambda i, k: (0, 0, i)),
        scratch=[pltpu.VMEM((tm, n), jnp.float32)], sem=("parallel", "arbitrary"))(hn, *d_proj)


def _in_proj(x, g, w):
    s, d = x.shape
    n = w.shape[0]
    tm = min(2 * ROW_TILE, s)

    def body(x_ref, g_ref, w_ref, p_ref, hn_ref):
        hn, _, _ = _norm_fwd(x_ref[...], g_ref[...])
        hn_ref[...] = hn.astype(hn_ref.dtype)
        p_ref[...] = _dot_nt(hn, w_ref[...])

    return _pcall(
        body, name="in_proj", grid=(s // tm,),
        out_shape=[_sds((s, n), jnp.float32), _sds((s, d), MXU)],
        in_specs=[pl.BlockSpec((tm, d), lambda i: (i, 0)), pl.BlockSpec((1, d), lambda i: (0, 0)),
                  pl.BlockSpec((n, d), lambda i: (0, 0), pipeline_mode=pl.Buffered(1))],
        out_specs=[pl.BlockSpec((tm, n), lambda i: (i, 0)), pl.BlockSpec((tm, d), lambda i: (i, 0))],
        sem=("parallel",))(x, g, w)


def _mla_prep(proj, cos, sin, g_cq, g_ckv, w_uq, w_ukv, g_qn, g_qr, g_kn, g_kr):
    s = proj.shape[0]
    tm = min(ROW_TILE, s)
    nh = MLA_HEADS

    def body(cq_ref, ckv_ref, kr_ref, cos_ref, sin_ref, gcq_ref, gckv_ref, wuq_ref, wukv_ref,
             gqn_ref, gqr_ref, gkn_ref, gkr_ref,
             qc_ref, kc_ref, v_ref, qb_ref, kvb_ref, cqn_ref, ckvn_ref):
        cos_t, sin_t = cos_ref[...], sin_ref[...]
        lo = _lo_mask((tm, LANES))
        cqn, _, _ = _norm_fwd(cq_ref[...], gcq_ref[...])
        cqn_ref[...] = cqn.astype(cqn_ref.dtype)
        qb = _dot_nt(cqn, wuq_ref[...])
        qb_ref[...] = qb
        ckvn, _, _ = _norm_fwd(ckv_ref[...], gckv_ref[...])
        ckvn_ref[...] = ckvn.astype(ckvn_ref.dtype)
        w_ukv_full = jnp.concatenate([wukv_ref[dev] for dev in range(N_DEV)], axis=1)
        kvb = _dot(ckvn, w_ukv_full)
        kvb_ref[...] = kvb
        kr, _, _ = _norm_fwd(kr_ref[...], gkr_ref[...], half=True)
        kr = _rope(kr, cos_t, sin_t)
        kr2 = jnp.where(lo, kr, pltpu.roll(kr, 64, 1))
        ropes = []
        for j in range(nh // 2):
            xr = qb[:, nh * MLA_NOPE + LANES * j: nh * MLA_NOPE + LANES * (j + 1)]
            qr, _, _ = _norm_fwd(xr, gqr_ref[...], half=True)
            ropes.append(_rope(qr, cos_t, sin_t))
        for h in range(nh):
            qn, _, _ = _norm_fwd(qb[:, MLA_NOPE * h: MLA_NOPE * (h + 1)], gqn_ref[...])
            mask = lo if h % 2 == 0 else jnp.logical_not(lo)
            qr = jnp.where(mask, ropes[h // 2], 0.0)
            qc_ref[h] = jnp.concatenate([qn, qr], axis=1).astype(qc_ref.dtype)
            kn, _, _ = _norm_fwd(kvb[:, 256 * h: 256 * h + MLA_NOPE], gkn_ref[...])
            kc_ref[h] = jnp.concatenate([kn, kr2], axis=1).astype(kc_ref.dtype)
            v_ref[h] = kvb[:, 256 * h + MLA_NOPE: 256 * (h + 1)].astype(v_ref.dtype)

    def col(width, start):
        return pl.BlockSpec((tm, width), lambda i: (i, start // width))

    def full(shape):
        return pl.BlockSpec(shape, lambda i: (0,) * len(shape))

    def row(width):
        return pl.BlockSpec((tm, width), lambda i: (i, 0))

    def heads(width):
        return pl.BlockSpec((nh, tm, width), lambda i: (0, i, 0))

    return _pcall(
        body, name="mla_prep", grid=(s // tm,),
        out_shape=[_sds((nh, s, 256), MXU), _sds((nh, s, 256), MXU), _sds((nh, s, MLA_V), MXU),
                   _sds((s, 768), jnp.float32), _sds((s, 1024), jnp.float32),
                   _sds((s, 512), MXU), _sds((s, 512), MXU)],
        in_specs=[col(512, C_CQ), col(512, C_CKV), col(LANES, C_KR), row(LANES), row(LANES),
                  full((1, 512)), full((1, 512)), full((768, 512)), full((N_DEV, 512, LANES)),
                  full((1, LANES)), full((1, LANES)), full((1, LANES)), full((1, LANES))],
        out_specs=[heads(256), heads(256), heads(MLA_V), row(768), row(1024), row(512), row(512)],
        sem=("parallel",))(proj, proj, proj, cos, sin, g_cq, g_ckv, w_uq, w_ukv, g_qn, g_qr, g_kn, g_kr)


def _mla_fwd(qc, kc, v):
    nh, s, _ = qc.shape
    t = min(ATT_TILE, s)
    nb = s // t
    scale = (MLA_NOPE + MLA_ROPE) ** -0.5

    def body(q_ref, k_ref, v_ref, y_ref, lse_ref, m_sc, l_sc, acc):
        qi, ki = pl.program_id(1), pl.program_id(2)

        @pl.when(ki == 0)
        def _():
            m_sc[...] = jnp.full_like(m_sc, NEG_INF)
            l_sc[...] = jnp.zeros_like(l_sc)
            acc[...] = jnp.zeros_like(acc)

        def step(diagonal):
            rc = t // 4 if diagonal else t
            for c in range(t // rc):
                rows = slice(rc * c, rc * (c + 1))
                keys = slice(0, rc * (c + 1))
                sc = _dot_nt(q_ref[0, rows, :], k_ref[0, keys, :]) * (scale * LOG2E)
                if diagonal:
                    r_i = lax.broadcasted_iota(jnp.int32, sc.shape, 0) + rc * c
                    c_i = lax.broadcasted_iota(jnp.int32, sc.shape, 1)
                    sc = jnp.where(c_i <= r_i, sc, NEG_INF)
                m_old = m_sc[rows, :]
                m_new = jnp.maximum(m_old, jnp.max(sc, -1, keepdims=True))
                alpha = jnp.exp2(m_old - m_new)
                p = jnp.exp2(sc - m_new)
                l_sc[rows, :] = alpha * l_sc[rows, :] + jnp.sum(p, -1, keepdims=True)
                acc[rows, :] = alpha * acc[rows, :] + _dot(p, v_ref[0, keys, :])
                m_sc[rows, :] = m_new

        @pl.when(ki < qi)
        def _():
            step(False)

        @pl.when(ki == qi)
        def _():
            step(True)

        @pl.when(ki == qi)
        def _():
            y_ref[...] = acc[...] / l_sc[...]
            lse_ref[0] = m_sc[...] + jnp.log2(l_sc[...])

    return _pcall(
        body, name="mla_fwd", grid=(nh, nb, nb),
        out_shape=[_sds((s, nh * MLA_V), jnp.float32), _sds((nh, s, 1), jnp.float32)],
        in_specs=[pl.BlockSpec((1, t, 256), lambda h, i, k: (h, i, 0)),
                  pl.BlockSpec((1, t, 256), lambda h, i, k: (h, jnp.minimum(k, i), 0)),
                  pl.BlockSpec((1, t, MLA_V), lambda h, i, k: (h, jnp.minimum(k, i), 0))],
        out_specs=[pl.BlockSpec((t, MLA_V), lambda h, i, k: (i, h)),
                   pl.BlockSpec((1, t, 1), lambda h, i, k: (h, i, 0))],
        scratch=[pltpu.VMEM((t, 1), jnp.float32), pltpu.VMEM((t, 1), jnp.float32),
                 pltpu.VMEM((t, MLA_V), jnp.float32)],
        sem=("parallel", "parallel", "arbitrary"))(qc, kc, v)


def _memkv_prep(mem, g_mem, w_mkv, g_mk):
    ml, d = mem.shape
    hw = MEM_HEADS * MEM_DIM

    def body(mem_ref, g_ref, w_ref, gk_ref, k_ref, v_ref, kv_ref, mn_ref):
        mn, _, _ = _norm_fwd(mem_ref[...], g_ref[...])
        mn_ref[...] = mn.astype(mn_ref.dtype)
        kv = _dot(mn, w_ref[...])
        kv_ref[...] = kv
        for h in range(MEM_HEADS):
            kn, _, _ = _norm_fwd(kv[:, MEM_DIM * h: MEM_DIM * (h + 1)], gk_ref[...])
            k_ref[:, MEM_DIM * h: MEM_DIM * (h + 1)] = kn.astype(k_ref.dtype)
        v_ref[...] = kv[:, hw:].astype(v_ref.dtype)

    vm = pl.BlockSpec(memory_space=pltpu.VMEM)
    return _pcall(
        body, name="memkv_prep",
        out_shape=[_sds((ml, hw), MXU), _sds((ml, hw), MXU), _sds((ml, 2 * hw), jnp.float32), _sds((ml, d), MXU)],
        in_specs=[vm] * 4, out_specs=[vm] * 4)(mem, g_mem, w_mkv, g_mk)


def _mem_fwd(proj, g_mq, km, vmm):
    s = proj.shape[0]
    ml, hw = km.shape
    tm = min(FFN_TILE, s)
    scale = MEM_DIM ** -0.5

    def body(q_ref, g_ref, k_ref, v_ref, y_ref, lse_ref):
        col = lax.broadcasted_iota(jnp.int32, (tm, MEM_HEADS), 1)
        lse_t = jnp.zeros((tm, MEM_HEADS), jnp.float32)
        for h in range(MEM_HEADS):
            sl = slice(MEM_DIM * h, MEM_DIM * (h + 1))
            qn, _, _ = _norm_fwd(q_ref[:, sl], g_ref[...])
            sc = _dot_nt(qn, k_ref[:, sl]) * scale
            m = jnp.max(sc, -1, keepdims=True)
            p = jnp.exp(sc - m)
            l = jnp.sum(p, -1, keepdims=True)
            y_ref[:, sl] = _dot(p, v_ref[:, sl]) / l
            lse_t = jnp.where(col == h, m + jnp.log(l), lse_t)
        lse_ref[...] = lse_t

    return _pcall(
        body, name="mem_fwd", grid=(s // tm,),
        out_shape=[_sds((s, hw), jnp.float32), _sds((s, MEM_HEADS), jnp.float32)],
        in_specs=[pl.BlockSpec((tm, hw), lambda i: (i, C_QM // hw)), pl.BlockSpec((1, MEM_DIM), lambda i: (0, 0)),
                  pl.BlockSpec((ml, hw), lambda i: (0, 0)), pl.BlockSpec((ml, hw), lambda i: (0, 0))],
        out_specs=[pl.BlockSpec((tm, hw), lambda i: (i, 0)), pl.BlockSpec((tm, MEM_HEADS), lambda i: (i, 0))],
        sem=("parallel",))(proj, g_mq, km, vmm)


def _alibi_slope(h):
    return float(2.0 ** (-8.0 * (h + 1) / SWA_Q_HEADS))


def _swa_common(n, kp, kc, vp, vc, pq, pkp, pkc, gk):
    b = SWA_BLOCK
    k_raw = jnp.concatenate([kp, kc], axis=0)
    kn, kxn, kr = _norm_fwd(k_raw, gk, half=True)
    v = jnp.concatenate([vp, vc], axis=0)
    dist = jnp.abs(pq - jnp.concatenate([pkp, pkc], axis=1))
    r_i = lax.broadcasted_iota(jnp.int32, (b, 2 * b), 0)
    c_i = lax.broadcasted_iota(jnp.int32, (b, 2 * b), 1)
    valid = (c_i > r_i) & (c_i <= r_i + b) & (c_i >= jnp.where(n > 0, 0, b))
    bias = jnp.where(valid, -dist, NEG_INF)
    return kn, v, bias


def _swa_folded(n, kp, kc, pq, pkp, pkc, gk):
    b = SWA_BLOCK
    kn_p, _, _ = _norm_fwd(kp, gk, half=True)
    kn_c, _, _ = _norm_fwd(kc, gk, half=True)
    r_i = lax.broadcasted_iota(jnp.int32, (b, b), 0)
    c_i = lax.broadcasted_iota(jnp.int32, (b, b), 1)
    upper = c_i > r_i
    bias_prev = jnp.where(n > 0, 0.0, NEG_INF) - jnp.abs(pq - pkp)
    bias = jnp.where(upper, bias_prev, -jnp.abs(pq - pkc))
    return kn_p, kn_c, bias, upper


def _swa_specs(s):
    b = SWA_BLOCK
    prev = lambda n: jnp.maximum(n - 1, 0)
    return [
        pl.BlockSpec((b, 1024), lambda n: (n, C_QA // 1024)),
        pl.BlockSpec((b, LANES), lambda n: (prev(n), C_KA // LANES)),
        pl.BlockSpec((b, LANES), lambda n: (n, C_KA // LANES)),
        pl.BlockSpec((b, LANES), lambda n: (prev(n), C_VA // LANES)),
        pl.BlockSpec((b, LANES), lambda n: (n, C_VA // LANES)),
        pl.BlockSpec((b, 1), lambda n: (n, 0)),
        pl.BlockSpec((1, b), lambda n: (0, prev(n))),
        pl.BlockSpec((1, b), lambda n: (0, n)),
        pl.BlockSpec((1, LANES), lambda n: (0, 0)),
        pl.BlockSpec((1, LANES), lambda n: (0, 0)),
        pl.BlockSpec(memory_space=pltpu.SMEM),
    ]


def _swa_fwd(proj, posc, posr, gq, gk, sinks):
    s = proj.shape[0]
    b = SWA_BLOCK
    scale = SWA_DIM ** -0.5

    def body(q_ref, kp_ref, kc_ref, vp_ref, vc_ref, pq_ref, pkp_ref, pkc_ref, gq_ref, gk_ref, sink_ref,
             y_ref, lse_ref):
        n = pl.program_id(0)
        kn_p, kn_c, bias, upper = _swa_folded(n, kp_ref[...], kc_ref[...], pq_ref[...], pkp_ref[...], pkc_ref[...],
                                              gk_ref[...])
        v_p, v_c = vp_ref[...], vc_ref[...]
        lo = _lo_mask((b, LANES))
        col = lax.broadcasted_iota(jnp.int32, (b, SWA_Q_HEADS), 1)
        lse_t = jnp.zeros((b, SWA_Q_HEADS), jnp.float32)
        hpg = SWA_Q_HEADS // SWA_KV_HEADS
        for g in range(SWA_KV_HEADS):
            heads = range(hpg * g, hpg * (g + 1))
            kvmask = lo if g == 0 else jnp.logical_not(lo)
            qs = []
            for j in range(hpg // 2 * g, hpg // 2 * (g + 1)):
                qn, _, _ = _norm_fwd(q_ref[:, LANES * j: LANES * (j + 1)], gq_ref[...], half=True)
                qn = qn * scale
                qsw = pltpu.roll(qn, 64, 1)
                qs += [jnp.where(kvmask, qn if e == g else qsw, 0.0) for e in range(2)]
            q_st = jnp.concatenate(qs, axis=0).astype(MXU)
            sp_st, sc_st = _dot_nt(q_st, kn_p), _dot_nt(q_st, kn_c)
            pus, pls, ls = [], [], []
            for i, h in enumerate(heads):
                rows = slice(b * i, b * (i + 1))
                sc = jnp.where(upper, sp_st[rows], sc_st[rows]) + _alibi_slope(h) * bias
                sk = sink_ref[h]
                m = jnp.maximum(jnp.max(sc, -1, keepdims=True), sk)
                p = jnp.exp(sc - m)
                l = jnp.sum(p, -1, keepdims=True) + jnp.exp(sk - m)
                pus.append(jnp.where(upper, p, 0.0).astype(MXU))
                pls.append(jnp.where(upper, 0.0, p).astype(MXU))
                ls.append(l)
                lse_t = jnp.where(col == h, m + jnp.log(l), lse_t)
            o_st = _dot(jnp.concatenate(pus, axis=0), v_p) + _dot(jnp.concatenate(pls, axis=0), v_c)
            for j in range(hpg // 2 * g, hpg // 2 * (g + 1)):
                halves = []
                for e in range(2):
                    i = 2 * j + e - hpg * g
                    o_h = o_st[b * i: b * (i + 1)] / ls[i]
                    halves.append(o_h if e == g else pltpu.roll(o_h, 64, 1))
                y_ref[:, LANES * j: LANES * (j + 1)] = jnp.where(lo, halves[0], halves[1])
        lse_ref[...] = lse_t

    return _pcall(
        body, name="swa_fwd", grid=(s // b,),
        out_shape=[_sds((s, 1024), jnp.float32), _sds((s, SWA_Q_HEADS), jnp.float32)],
        in_specs=_swa_specs(s),
        out_specs=[pl.BlockSpec((b, 1024), lambda n: (n, 0)), pl.BlockSpec((b, SWA_Q_HEADS), lambda n: (n, 0))],
        sem=("parallel",))(proj, proj, proj, proj, proj, posc, posr, posr, gq, gk, sinks)


def _out_proj(y_a, y_b, y_m, x, w_out, g_ffn):
    s, d = x.shape
    tm = min(2 * ROW_TILE, s)

    def body(ya_ref, yb_ref, ym_ref, x_ref, w_ref, g_ref, h1_ref, fn_ref):
        y = jnp.concatenate([ya_ref[...].astype(MXU), yb_ref[...].astype(MXU), ym_ref[...].astype(MXU)], axis=1)
        h1 = x_ref[...] + _dot(y, w_ref[...])
        h1_ref[...] = h1
        fn, _, _ = _norm_fwd(h1, g_ref[...])
        fn_ref[...] = fn.astype(fn_ref.dtype)

    def row(width):
        return pl.BlockSpec((tm, width), lambda i: (i, 0))

    return _pcall(
        body, name="out_proj", grid=(s // tm,),
        out_shape=[_sds((s, d), jnp.float32), _sds((s, d), MXU)],
        in_specs=[row(1024), row(512), row(512), row(d),
                  pl.BlockSpec(w_out.shape, lambda i: (0, 0), pipeline_mode=pl.Buffered(1)),
                  pl.BlockSpec((1, d), lambda i: (0, 0))],
        out_specs=[row(d), row(d)], sem=("parallel",))(y_a, y_b, y_m, x, w_out, g_ffn)


def _ffn_gu(fn, w_gu):
    s, d = fn.shape
    f = w_gu.shape[2]
    tm = min(2 * FFN_TILE, s)

    def body(fn_ref, w_ref, gu_ref, act_ref):
        x = fn_ref[...]
        g = _dot_nt(x, w_ref[0, 0])
        u = _dot_nt(x, w_ref[0, 1])
        gu_ref[0, 0] = g
        gu_ref[0, 1] = u
        act_ref[0] = (g * jax.nn.sigmoid(g) * u).astype(act_ref.dtype)

    return _pcall(
        body, name="ffn_gate_up", grid=(N_DEV, s // tm),
        out_shape=[_sds((N_DEV, 2, s, f), jnp.float32), _sds((N_DEV, s, f), MXU)],
        in_specs=[pl.BlockSpec((tm, d), lambda j, i: (i, 0)),
                  pl.BlockSpec((1, 2, f, d), lambda j, i: (j, 0, 0, 0))],
        out_specs=[pl.BlockSpec((1, 2, tm, f), lambda j, i: (j, 0, i, 0)),
                   pl.BlockSpec((1, tm, f), lambda j, i: (j, i, 0))],
        sem=("parallel", "parallel"))(fn, w_gu)


def _ffn_down(act, w_d, h1, target):
    _, s, f = act.shape
    d = h1.shape[1]
    tm = min(FFN_TILE, s)

    def body(a_ref, w_ref, h1_ref, t_ref, dout_ref, doutb_ref, loss_ref, acc):
        i, j = pl.program_id(0), pl.program_id(1)
        part = _dot(a_ref[0], w_ref[0]) + _dot(a_ref[1], w_ref[1])

        @pl.when(j == 0)
        def _():
            acc[...] = h1_ref[...] + part

        @pl.when(j > 0)
        def _():
            acc[...] += part

        @pl.when((i == 0) & (j == 0))
        def _():
            loss_ref[...] = jnp.zeros_like(loss_ref)

        @pl.when(j == N_DEV // 2 - 1)
        def _():
            diff = acc[...] - t_ref[...]
            dout_ref[...] = diff / d
            doutb_ref[...] = (diff / d).astype(doutb_ref.dtype)
            loss_ref[...] += 0.5 * jnp.sum(jnp.sum(diff * diff, -1, keepdims=True) / d)

    row = pl.BlockSpec((tm, d), lambda i, j: (i, 0))
    return _pcall(
        body, name="ffn_down", grid=(s // tm, N_DEV // 2),
        out_shape=[_sds((s, d), jnp.float32), _sds((s, d), MXU), _sds((8, LANES), jnp.float32)],
        in_specs=[pl.BlockSpec((2, tm, f), lambda i, j: (j, i, 0)), pl.BlockSpec((2, f, d), lambda i, j: (j, 0, 0)),
                  row, row],
        out_specs=[row, row, pl.BlockSpec((8, LANES), lambda i, j: (0, 0))],
        scratch=[pltpu.VMEM((tm, d), jnp.float32)], sem=("arbitrary", "arbitrary"))(act, w_d, h1, target)


def _ffn_bwd_act(dout, w_d, gu):
    s, d = dout.shape
    f = w_d.shape[1]
    tm = min(2 * FFN_TILE, s)
    ni = s // tm

    def body(do_ref, w_ref, gu_ref, dgu_ref, dw_ref, acc):
        i = pl.program_id(1)
        do = do_ref[...]
        d_act = _dot_nt(do, w_ref[0])
        g, u = gu_ref[0, 0], gu_ref[0, 1]
        sig = jax.nn.sigmoid(g)
        silu = g * sig
        dgu_ref[0, 0] = (d_act * u * (sig * (1.0 + g * (1.0 - sig)))).astype(dgu_ref.dtype)
        dgu_ref[0, 1] = (d_act * silu).astype(dgu_ref.dtype)
        part = _dot_tn(silu * u, do)

        @pl.when(i == 0)
        def _():
            acc[...] = part

        @pl.when(i > 0)
        def _():
            acc[...] += part

        @pl.when(i == ni - 1)
        def _():
            dw_ref[0] = acc[...].astype(dw_ref.dtype)

    return _pcall(
        body, name="ffn_bwd_act", grid=(N_DEV, ni),
        out_shape=[_sds((N_DEV, 2, s, f), MXU), _sds((N_DEV, f, d), WIRE)],
        in_specs=[pl.BlockSpec((tm, d), lambda j, i: (i, 0)), pl.BlockSpec((1, f, d), lambda j, i: (j, 0, 0)),
                  pl.BlockSpec((1, 2, tm, f), lambda j, i: (j, 0, i, 0))],
        out_specs=[pl.BlockSpec((1, 2, tm, f), lambda j, i: (j, 0, i, 0)),
                   pl.BlockSpec((1, f, d), lambda j, i: (j, 0, 0))],
        scratch=[pltpu.VMEM((f, d), jnp.float32)], sem=("parallel", "arbitrary"))(dout, w_d, gu)


def _ffn_dw_gu(fn, dgu):
    s, d = fn.shape
    f = dgu.shape[-1]
    tk = min(4 * FFN_TILE, s)
    nk = s // tk

    def body(fn_ref, dgu_ref, dw_ref, acc):
        k = pl.program_id(2)
        part = _dot_tn(dgu_ref[0, 0], fn_ref[...])

        @pl.when(k == 0)
        def _():
            acc[...] = part

        @pl.when(k > 0)
        def _():
            acc[...] += part

        @pl.when(k == nk - 1)
        def _():
            dw_ref[0, 0] = acc[...].astype(dw_ref.dtype)

    return _pcall(
        body, name="ffn_dw_gate_up", grid=(N_DEV, 2, nk),
        out_shape=_sds((N_DEV, 2, f, d), WIRE),
        in_specs=[pl.BlockSpec((tk, d), lambda j, w, k: (k, 0)),
                  pl.BlockSpec((1, 1, tk, f), lambda j, w, k: (j, w, k, 0))],
        out_specs=pl.BlockSpec((1, 1, f, d), lambda j, w, k: (j, w, 0, 0)),
        scratch=[pltpu.VMEM((f, d), jnp.float32)], sem=("parallel", "parallel", "arbitrary"))(fn, dgu)


def _ffn_dfn(dgu, w_gu, after):
    _, _, s, f = dgu.shape
    d = w_gu.shape[3]
    tm = min(FFN_TILE, s)

    def body(dgu_ref, w_ref, dfn_ref):
        j = pl.program_id(1)
        part = (_dot(dgu_ref[0, 0], w_ref[0, 0]) + _dot(dgu_ref[0, 1], w_ref[0, 1])
                + _dot(dgu_ref[1, 0], w_ref[1, 0]) + _dot(dgu_ref[1, 1], w_ref[1, 1]))

        @pl.when(j == 0)
        def _():
            dfn_ref[...] = part

        @pl.when(j > 0)
        def _():
            dfn_ref[...] += part

    return _pcall(
        body, name="ffn_dfn", grid=(s // tm, N_DEV // 2),
        out_shape=_sds((s, d), jnp.float32),
        in_specs=[pl.BlockSpec((2, 2, tm, f), lambda i, j: (j, 0, i, 0)),
                  pl.BlockSpec((2, 2, f, d), lambda i, j: (j, 0, 0, 0))],
        out_specs=pl.BlockSpec((tm, d), lambda i, j: (i, 0)),
        sem=("parallel", "arbitrary"), after=after)(dgu, w_gu)


def _ffn_norm_bwd(d_fn, dout, h1, g_ffn):
    s, d = h1.shape
    tm = min(2 * ROW_TILE, s)

    def body(dfn_ref, do_ref, h1_ref, g_ref, dh1_ref, dg_ref):
        i = pl.program_id(0)

        @pl.when(i == 0)
        def _():
            dg_ref[...] = jnp.zeros_like(dg_ref)

        _, xn, r = _norm_fwd(h1_ref[...], g_ref[...])
        dx, dg = _norm_bwd(xn, r, g_ref[...], dfn_ref[...])
        dh1_ref[...] = do_ref[...] + dx
        dg_ref[...] += dg

    row = pl.BlockSpec((tm, d), lambda i: (i, 0))
    vec = pl.BlockSpec((1, d), lambda i: (0, 0))
    return _pcall(
        body, name="ffn_norm_bwd", grid=(s // tm,),
        out_shape=[_sds((s, d), jnp.float32), _sds((1, d), jnp.float32)],
        in_specs=[row, row, row, vec], out_specs=[row, vec], sem=("arbitrary",))(d_fn, dout, h1, g_ffn)


def _mem_bwd(proj, g_mq, km, vmm, d_y, y_m, lse):
    s = proj.shape[0]
    ml, hw = km.shape
    tm = min(FFN_TILE, s)
    scale = MEM_DIM ** -0.5

    def body(q_ref, g_ref, k_ref, v_ref, do_ref, y_ref, lse_ref, dq_ref, dk_ref, dv_ref, dg_ref):
        i = pl.program_id(0)

        @pl.when(i == 0)
        def _():
            dk_ref[...] = jnp.zeros_like(dk_ref)
            dv_ref[...] = jnp.zeros_like(dv_ref)
            dg_ref[...] = jnp.zeros_like(dg_ref)

        col = lax.broadcasted_iota(jnp.int32, (tm, MEM_HEADS), 1)
        lse_t = lse_ref[...]
        for h in range(MEM_HEADS):
            sl = slice(MEM_DIM * h, MEM_DIM * (h + 1))
            qn, xn, r = _norm_fwd(q_ref[:, sl], g_ref[...])
            lse_h = jnp.sum(jnp.where(col == h, lse_t, 0.0), -1, keepdims=True)
            p = jnp.exp(_dot_nt(qn, k_ref[:, sl]) * scale - lse_h)
            do = do_ref[:, sl]
            dd = jnp.sum(do * y_ref[:, sl], -1, keepdims=True)
            dp = _dot_nt(do, v_ref[:, sl])
            ds = (p * (dp - dd)).astype(MXU)
            dv_ref[:, sl] += _dot_tn(p, do)
            dk_ref[:, sl] += _dot_tn(ds, qn) * scale
            dx, dg = _norm_bwd(xn, r, g_ref[...], _dot(ds, k_ref[:, sl]) * scale)
            dq_ref[:, sl] = dx.astype(dq_ref.dtype)
            dg_ref[...] += dg

    full = pl.BlockSpec((ml, hw), lambda i: (0, 0))
    return _pcall(
        body, name="mem_bwd", grid=(s // tm,),
        out_shape=[_sds((s, hw), MXU), _sds((ml, hw), jnp.float32), _sds((ml, hw), jnp.float32),
                   _sds((1, MEM_DIM), jnp.float32)],
        in_specs=[pl.BlockSpec((tm, hw), lambda i: (i, C_QM // hw)), pl.BlockSpec((1, MEM_DIM), lambda i: (0, 0)),
                  full, full, pl.BlockSpec((tm, hw), lambda i: (i, 3)), pl.BlockSpec((tm, hw), lambda i: (i, 0)),
                  pl.BlockSpec((tm, MEM_HEADS), lambda i: (i, 0))],
        out_specs=[pl.BlockSpec((tm, hw), lambda i: (i, 0)), full, full,
                   pl.BlockSpec((1, MEM_DIM), lambda i: (0, 0))],
        sem=("arbitrary",))(proj, g_mq, km, vmm, d_y, y_m, lse)


def _memkv_bwd(mem, g_mem, w_mkv, g_mk, kv, memn, dk, dv):
    ml, d = mem.shape
    hw = MEM_HEADS * MEM_DIM

    def body(mem_ref, g_ref, w_ref, gk_ref, kv_ref, mn_ref, dk_ref, dv_ref, dw_ref, dgm_ref, dgk_ref):
        parts = []
        dgk = jnp.zeros((1, MEM_DIM), jnp.float32)
        for h in range(MEM_HEADS):
            sl = slice(MEM_DIM * h, MEM_DIM * (h + 1))
            _, xn, r = _norm_fwd(kv_ref[:, sl], gk_ref[...])
            dx, dg = _norm_bwd(xn, r, gk_ref[...], dk_ref[:, sl])
            parts.append(dx)
            dgk = dgk + dg
        dkv = jnp.concatenate(parts + [dv_ref[...]], axis=1).astype(MXU)
        dgk_ref[...] = dgk
        dw_ref[...] = _dot_tn(mn_ref[...], dkv).astype(dw_ref.dtype)
        d_mn = _dot_nt(dkv, w_ref[...])
        _, xn, _ = _norm_fwd(mem_ref[...], g_ref[...])
        dgm_ref[...] = jnp.sum(d_mn * xn, 0, keepdims=True)

    vm = pl.BlockSpec(memory_space=pltpu.VMEM)
    return _pcall(
        body, name="memkv_bwd",
        out_shape=[_sds((d, 2 * hw), WIRE), _sds((1, d), jnp.float32), _sds((1, MEM_DIM), jnp.float32)],
        in_specs=[vm] * 8, out_specs=[vm] * 3)(mem, g_mem, w_mkv, g_mk, kv, memn, dk, dv)


def _mla_bwd(qc, kc, v, d_y, y_b, lse, after):
    nh, s, _ = qc.shape
    t = min(ATT_TILE, s)
    nb = s // t
    scale = (MLA_NOPE + MLA_ROPE) ** -0.5

    def body(q_ref, k_ref, v_ref, do_ref, y_ref, lse_ref, dq_ref, dk_ref, dv_ref, dk_acc, dv_acc):
        kj, qi = pl.program_id(1), pl.program_id(2)

        @pl.when((kj == 0) & (qi == 0))
        def _():
            dq_ref[...] = jnp.zeros_like(dq_ref)

        @pl.when(qi == kj)
        def _():
            dk_acc[...] = jnp.zeros_like(dk_acc)
            dv_acc[...] = jnp.zeros_like(dv_acc)

        def step(diagonal):
            rc = t // 4 if diagonal else t
            for c in range(t // rc):
                rows = slice(rc * c, rc * (c + 1))
                keys = slice(0, rc * (c + 1))
                q, k = q_ref[0, rows, :], k_ref[0, keys, :]
                sc = _dot_nt(q, k) * (scale * LOG2E)
                if diagonal:
                    r_i = lax.broadcasted_iota(jnp.int32, sc.shape, 0) + rc * c
                    c_i = lax.broadcasted_iota(jnp.int32, sc.shape, 1)
                    sc = jnp.where(c_i <= r_i, sc, NEG_INF)
                p = jnp.exp2(sc - lse_ref[0, rows, :])
                do = do_ref[rows, :]
                dd = jnp.sum(do * y_ref[rows, :], -1, keepdims=True)
                dp = _dot_nt(do, v_ref[0, keys, :])
                ds = (p * (dp - dd) * scale).astype(MXU)
                dv_acc[keys, :] += _dot_tn(p, do)
                dk_acc[keys, :] += _dot_tn(ds, q)
                out_rows = pl.ds(pl.multiple_of(qi * t + rc * c, rc), rc)
                dq_ref[0, out_rows, :] += _dot(ds, k)

        @pl.when(qi > kj)
        def _():
            step(False)

        @pl.when(qi == kj)
        def _():
            step(True)

        @pl.when(qi == nb - 1)
        def _():
            dk_ref[0] = dk_acc[...]
            dv_ref[0] = dv_acc[...]

    qmap = lambda h, j, i: (h, jnp.maximum(i, j), 0)
    return _pcall(
        body, name="mla_bwd", grid=(nh, nb, nb),
        out_shape=[_sds((nh, s, 256), jnp.float32), _sds((nh, s, 256), jnp.float32),
                   _sds((nh, s, MLA_V), jnp.float32)],
        in_specs=[pl.BlockSpec((1, t, 256), qmap),
                  pl.BlockSpec((1, t, 256), lambda h, j, i: (h, j, 0)),
                  pl.BlockSpec((1, t, MLA_V), lambda h, j, i: (h, j, 0)),
                  pl.BlockSpec((t, MLA_V), lambda h, j, i: (jnp.maximum(i, j), 8 + h)),
                  pl.BlockSpec((t, MLA_V), lambda h, j, i: (jnp.maximum(i, j), h)),
                  pl.BlockSpec((1, t, 1), qmap)],
        out_specs=[pl.BlockSpec((1, s, 256), lambda h, j, i: (h, 0, 0)),
                   pl.BlockSpec((1, t, 256), lambda h, j, i: (h, j, 0)),
                   pl.BlockSpec((1, t, MLA_V), lambda h, j, i: (h, j, 0))],
        scratch=[pltpu.VMEM((t, 256), jnp.float32), pltpu.VMEM((t, MLA_V), jnp.float32)],
        sem=("parallel", "arbitrary", "arbitrary"), after=after)(qc, kc, v, d_y, y_b, lse)


def _mla_prep_bwd(proj, cos, sin, g_cq, g_ckv, w_uq, w_ukv, g_qn, g_qr, g_kn, g_kr,
                  qb, kvb, cqn, ckvn, dqc, dkc, dv):
    s = proj.shape[0]
    tm = min(ROW_TILE, s)
    nh = MLA_HEADS
    ni = s // tm

    def body(cq_ref, ckv_ref, kr_ref, cos_ref, sin_ref, gcq_ref, gckv_ref, wuq_ref, wukv_ref,
             gqn_ref, gqr_ref, gkn_ref, gkr_ref, qb_ref, kvb_ref, cqn_ref, ckvn_ref, dqc_ref, dkc_ref, dv_ref,
             dcq_ref, dckv_ref, dkr_ref, dwuq_ref, dwukv_ref,
             dgcq_ref, dgckv_ref, dgqn_ref, dgqr_ref, dgkn_ref, dgkr_ref, acc_uq, acc_ukv):
        i = pl.program_id(0)

        @pl.when(i == 0)
        def _():
            acc_uq[...] = jnp.zeros_like(acc_uq)
            acc_ukv[...] = jnp.zeros_like(acc_ukv)
            for ref in (dgcq_ref, dgckv_ref, dgqn_ref, dgqr_ref, dgkn_ref, dgkr_ref):
                ref[...] = jnp.zeros_like(ref)

        cos_t, sin_t = cos_ref[...], sin_ref[...]
        lo = _lo_mask((tm, LANES))
        qb_v, kvb_v = qb_ref[...], kvb_ref[...]
        dq_parts, dgqn = [], jnp.zeros((1, LANES), jnp.float32)
        for h in range(nh):
            _, xn, r = _norm_fwd(qb_v[:, MLA_NOPE * h: MLA_NOPE * (h + 1)], gqn_ref[...])
            dx, dg = _norm_bwd(xn, r, gqn_ref[...], dqc_ref[h][:, :MLA_NOPE])
            dq_parts.append(dx)
            dgqn = dgqn + dg
        dgqn_ref[...] += dgqn
        dgqr = jnp.zeros((1, LANES), jnp.float32)
        for j in range(nh // 2):
            d_rope = jnp.where(lo, dqc_ref[2 * j][:, MLA_NOPE:], dqc_ref[2 * j + 1][:, MLA_NOPE:])
            d_pre = _rope_bwd(d_rope, cos_t, sin_t)
            xr = qb_v[:, nh * MLA_NOPE + LANES * j: nh * MLA_NOPE + LANES * (j + 1)]
            _, xn, r = _norm_fwd(xr, gqr_ref[...], half=True)
            dx, dg = _norm_bwd(xn, r, gqr_ref[...], d_pre, half=True)
            dq_parts.append(dx)
            dgqr = dgqr + dg
        dgqr_ref[...] += dgqr
        dqb = jnp.concatenate(dq_parts, axis=1).astype(MXU)
        acc_uq[...] += _dot_tn(dqb, cqn_ref[...])
        _, xn, r = _norm_fwd(cq_ref[...], gcq_ref[...])
        dx, dg = _norm_bwd(xn, r, gcq_ref[...], _dot(dqb, wuq_ref[...]))
        dcq_ref[...] = dx.astype(dcq_ref.dtype)
        dgcq_ref[...] += dg
        dkv_parts, dgkn = [], jnp.zeros((1, LANES), jnp.float32)
        d_kr2 = jnp.zeros((tm, LANES), jnp.float32)
        for h in range(nh):
            _, xn, r = _norm_fwd(kvb_v[:, 256 * h: 256 * h + MLA_NOPE], gkn_ref[...])
            dx, dg = _norm_bwd(xn, r, gkn_ref[...], dkc_ref[h][:, :MLA_NOPE])
            dkv_parts += [dx, dv_ref[h]]
            dgkn = dgkn + dg
            d_kr2 = d_kr2 + dkc_ref[h][:, MLA_NOPE:]
        dgkn_ref[...] += dgkn
        dkvb = jnp.concatenate(dkv_parts, axis=1).astype(MXU)
        part_ukv = _dot_tn(ckvn_ref[...], dkvb)
        for dev in range(N_DEV):
            acc_ukv[dev] += part_ukv[:, LANES * dev: LANES * (dev + 1)]
        w_ukv_full = jnp.concatenate([wukv_ref[dev] for dev in range(N_DEV)], axis=1)
        d_ckvn = _dot_nt(dkvb, w_ukv_full)
        _, xn, r = _norm_fwd(ckv_ref[...], gckv_ref[...])
        dx, dg = _norm_bwd(xn, r, gckv_ref[...], d_ckvn)
        dckv_ref[...] = dx.astype(dckv_ref.dtype)
        dgckv_ref[...] += dg
        d_kr = jnp.where(lo, d_kr2 + pltpu.roll(d_kr2, 64, 1), 0.0)
        d_pre = _rope_bwd(d_kr, cos_t, sin_t)
        _, xn, r = _norm_fwd(kr_ref[...], gkr_ref[...], half=True)
        dx, dg = _norm_bwd(xn, r, gkr_ref[...], d_pre, half=True)
        dkr_ref[...] = jnp.where(lo, dx, 0.0).astype(dkr_ref.dtype)
        dgkr_ref[...] += jnp.where(_lo_mask((1, LANES)), dg, 0.0)

        @pl.when(i == ni - 1)
        def _():
            dwuq_ref[...] = acc_uq[...].astype(dwuq_ref.dtype)
            dwukv_ref[...] = acc_ukv[...].astype(dwukv_ref.dtype)

    def col(width, start):
        return pl.BlockSpec((tm, width), lambda i: (i, start // width))

    def full(shape):
        return pl.BlockSpec(shape, lambda i: (0,) * len(shape))

    def row(width):
        return pl.BlockSpec((tm, width), lambda i: (i, 0))

    def heads(width):
        return pl.BlockSpec((nh, tm, width), lambda i: (0, i, 0))

    vec = full((1, LANES))
    return _pcall(
        body, name="mla_prep_bwd", grid=(ni,),
        out_shape=[_sds((s, 512), MXU), _sds((s, 512), MXU), _sds((s, LANES), MXU),
                   _sds((768, 512), WIRE), _sds((N_DEV, 512, LANES), WIRE),
                   _sds((1, 512), jnp.float32), _sds((1, 512), jnp.float32)] + [_sds((1, LANES), jnp.float32)] * 4,
        in_specs=[col(512, C_CQ), col(512, C_CKV), col(LANES, C_KR), row(LANES), row(LANES),
                  full((1, 512)), full((1, 512)), full((768, 512)), full((N_DEV, 512, LANES)), vec, vec, vec, vec,
                  row(768), row(1024), row(512), row(512), heads(256), heads(256), heads(MLA_V)],
        out_specs=[row(512), row(512), row(LANES), full((768, 512)), full((N_DEV, 512, LANES)),
                   full((1, 512)), full((1, 512)), vec, vec, vec, vec],
        scratch=[pltpu.VMEM((768, 512), jnp.float32), pltpu.VMEM((N_DEV, 512, LANES), jnp.float32)],
        sem=("arbitrary",))(proj, proj, proj, cos, sin, g_cq, g_ckv, w_uq, w_ukv, g_qn, g_qr, g_kn, g_kr,
                            qb, kvb, cqn, ckvn, dqc, dkc, dv)


def _swa_bwd(proj, posc, posr, gq, gk, sinks, d_y, y_a, lse, after):
    s = proj.shape[0]
    b = SWA_BLOCK
    nb = s // b
    scale = SWA_DIM ** -0.5

    def body(q_ref, kp_ref, kc_ref, vp_ref, vc_ref, pq_ref, pkp_ref, pkc_ref, gq_ref, gk_ref, sink_ref,
             do_ref, y_ref, lse_ref, kfull_ref,
             dq_ref, dk_ref, dv_ref, dgq_ref, dgk_ref, dsink_ref, dk_acc, dv_acc):
        n = pl.program_id(0)

        @pl.when(n == 0)
        def _():
            dk_acc[...] = jnp.zeros_like(dk_acc)
            dv_acc[...] = jnp.zeros_like(dv_acc)
            dgq_ref[...] = jnp.zeros_like(dgq_ref)
            dsink_ref[...] = jnp.zeros_like(dsink_ref)

        kn, v, bias = _swa_common(n, kp_ref[...], kc_ref[...], vp_ref[...], vc_ref[...],
                                  pq_ref[...], pkp_ref[...], pkc_ref[...], gk_ref[...])
        lo = _lo_mask((b, LANES))
        col = lax.broadcasted_iota(jnp.int32, (b, SWA_Q_HEADS), 1)
        col1 = lax.broadcasted_iota(jnp.int32, (1, SWA_Q_HEADS), 1)
        lse_t = lse_ref[...]
        dk_blk = jnp.zeros((2 * b, LANES), jnp.float32)
        dv_blk = jnp.zeros((2 * b, LANES), jnp.float32)
        dgq = jnp.zeros((1, LANES), jnp.float32)
        dsink = jnp.zeros((1, SWA_Q_HEADS), jnp.float32)
        for j in range(SWA_Q_HEADS // 2):
            hk = (2 * j) // (SWA_Q_HEADS // SWA_KV_HEADS)
            kvmask = lo if hk == 0 else jnp.logical_not(lo)
            sl = slice(LANES * j, LANES * (j + 1))
            qn, xn, r = _norm_fwd(q_ref[:, sl], gq_ref[...], half=True)
            qn = qn * scale
            qsw = pltpu.roll(qn, 64, 1)
            d2 = do_ref[:, sl]
            d2sw = pltpu.roll(d2, 64, 1)
            prod = d2 * y_ref[:, sl]
            dqs = []
            for e in range(2):
                h = 2 * j + e
                half_e = lo if e == 0 else jnp.logical_not(lo)
                qm = jnp.where(kvmask, qn if e == hk else qsw, 0.0)
                dm = jnp.where(kvmask, d2 if e == hk else d2sw, 0.0)
                sc = _dot_nt(qm, kn) + _alibi_slope(h) * bias
                lse_h = jnp.sum(jnp.where(col == h, lse_t, 0.0), -1, keepdims=True)
                p = jnp.exp(sc - lse_h)
                dd = jnp.sum(jnp.where(half_e, prod, 0.0), -1, keepdims=True)
                dp = _dot_nt(dm, v)
                ds = (p * (dp - dd)).astype(MXU)
                dsink = dsink - jnp.where(col1 == h, jnp.sum(jnp.exp(sink_ref[h] - lse_h) * dd), 0.0)
                dq_m = _dot(ds, kn) * scale
                dk_blk = dk_blk + _dot_tn(ds, qm)
                dv_blk = dv_blk + _dot_tn(p, dm)
                dqs.append(dq_m if e == hk else pltpu.roll(dq_m, 64, 1))
            dx, dg = _norm_bwd(xn, r, gq_ref[...], jnp.where(lo, dqs[0], dqs[1]), half=True)
            dq_ref[:, sl] = dx.astype(dq_ref.dtype)
            dgq = dgq + dg
        dgq_ref[...] += dgq
        dsink_ref[...] += dsink
        prev = pl.ds(pl.multiple_of(jnp.maximum(n - 1, 0) * b, b), b)
        cur = pl.ds(pl.multiple_of(n * b, b), b)
        dk_acc[prev, :] += dk_blk[:b]
        dv_acc[prev, :] += dv_blk[:b]
        dk_acc[cur, :] += dk_blk[b:]
        dv_acc[cur, :] += dv_blk[b:]

        @pl.when(n == nb - 1)
        def _():
            _, kxn, kr = _norm_fwd(kfull_ref[...], gk_ref[...], half=True)
            dx, dg = _norm_bwd(kxn, kr, gk_ref[...], dk_acc[...], half=True)
            dk_ref[...] = dx.astype(dk_ref.dtype)
            dv_ref[...] = dv_acc[...].astype(dv_ref.dtype)
            dgk_ref[...] = dg

    full = pl.BlockSpec((s, LANES), lambda n: (0, 0))
    vec = pl.BlockSpec((1, LANES), lambda n: (0, 0))
    return _pcall(
        body, name="swa_bwd", grid=(nb,),
        out_shape=[_sds((s, 1024), MXU), _sds((s, LANES), MXU), _sds((s, LANES), MXU),
                   _sds((1, LANES), jnp.float32), _sds((1, LANES), jnp.float32),
                   _sds((1, SWA_Q_HEADS), jnp.float32)],
        in_specs=_swa_specs(s) + [pl.BlockSpec((b, 1024), lambda n: (n, 0)), pl.BlockSpec((b, 1024), lambda n: (n, 0)),
                                  pl.BlockSpec((b, SWA_Q_HEADS), lambda n: (n, 0)),
                                  pl.BlockSpec((s, LANES), lambda n: (0, C_KA // LANES))],
        out_specs=[pl.BlockSpec((b, 1024), lambda n: (n, 0)), full, full, vec, vec,
                   pl.BlockSpec((1, SWA_Q_HEADS), lambda n: (0, 0))],
        scratch=[pltpu.VMEM((s, LANES), jnp.float32), pltpu.VMEM((s, LANES), jnp.float32)],
        sem=("arbitrary",), after=after)(proj, proj, proj, proj, proj, posc, posr, posr, gq, gk, sinks, d_y, y_a, lse,
                                         proj)


def _dx(d_proj, w_in, x, g, d_h1, after):
    s, d = x.shape
    n = w_in.shape[0]
    tm = min(2 * ROW_TILE, s)

    n_pc = len(d_proj)

    def body(*refs):
        dp_refs, (w_ref, x_ref, g_ref, dh_ref, dx_ref, dg_ref) = refs[:n_pc], refs[n_pc:]
        i = pl.program_id(0)

        @pl.when(i == 0)
        def _():
            dg_ref[...] = jnp.zeros_like(dg_ref)

        d_hn = _dot(jnp.concatenate([r[...] for r in dp_refs], axis=1), w_ref[...])
        _, xn, r = _norm_fwd(x_ref[...], g_ref[...])
        dx, dg = _norm_bwd(xn, r, g_ref[...], d_hn)
        dx_ref[...] = dh_ref[...] + dx
        dg_ref[...] += dg

    row = pl.BlockSpec((tm, d), lambda i: (i, 0))
    vec = pl.BlockSpec((1, d), lambda i: (0, 0))
    return _pcall(
        body, name="grad_x", grid=(s // tm,),
        out_shape=[_sds((s, d), jnp.float32), _sds((1, d), jnp.float32)],
        in_specs=[pl.BlockSpec((tm, p.shape[1]), lambda i: (i, 0)) for p in d_proj] + [
                  pl.BlockSpec((n, d), lambda i: (0, 0), pipeline_mode=pl.Buffered(1)), row, vec, row],
        out_specs=[row, vec], sem=("arbitrary",), after=after)(*d_proj, w_in, x, g, d_h1)


_SMALL = ["attn_norm_g", "swa_q_norm_g", "swa_k_norm_g", "swa_sinks", "mla_cq_norm_g", "mla_ckv_norm_g",
          "mla_qn_norm_g", "mla_qr_norm_g", "mla_kn_norm_g", "mla_kr_norm_g", "mem_norm_g",
          "mem_q_norm_g", "mem_k_norm_g", "ffn_norm_g"]


def kernel(x, mem, positions, attn_norm_g, w_in, swa_q_norm_g, swa_k_norm_g, swa_sinks, mla_cq_norm_g, mla_ckv_norm_g, w_uq, w_ukv, mla_qn_norm_g, mla_qr_norm_g, mla_kn_norm_g, mla_kr_norm_g, mem_norm_g, w_mem_kv, mem_q_norm_g, mem_k_norm_g, w_out, ffn_norm_g, w_gate, w_up, w_down, loss_target, m_attn_norm_g, m_w_in, m_swa_q_norm_g, m_swa_k_norm_g, m_swa_sinks, m_mla_cq_norm_g, m_mla_ckv_norm_g, m_w_uq, m_w_ukv, m_mla_qn_norm_g, m_mla_qr_norm_g, m_mla_kn_norm_g, m_mla_kr_norm_g, m_mem_norm_g, m_w_mem_kv, m_mem_q_norm_g, m_mem_k_norm_g, m_w_out, m_ffn_norm_g, m_w_gate, m_w_up, m_w_down, v_attn_norm_g, v_w_in, v_swa_q_norm_g, v_swa_k_norm_g, v_swa_sinks, v_mla_cq_norm_g, v_mla_ckv_norm_g, v_w_uq, v_w_ukv, v_mla_qn_norm_g, v_mla_qr_norm_g, v_mla_kn_norm_g, v_mla_kr_norm_g, v_mem_norm_g, v_w_mem_kv, v_mem_q_norm_g, v_mem_k_norm_g, v_w_out, v_ffn_norm_g, v_w_gate, v_w_up, v_w_down):
    args = dict(locals())
    x2, mem2, tgt = x[0], mem[0], loss_target[0]
    s, d = x2.shape
    n_in = w_in.shape[2]
    f = w_gate.shape[2]

    (g_in,) = _all_gather([w_in[0].T.astype(WIRE)])
    mix_shards = [w_uq[0].T.astype(WIRE), w_ukv[0].astype(WIRE), w_mem_kv[0].astype(WIRE),
                  _to_wire([w_out[0]], g_in, "wire_out")[0]]
    g_uq, wkv, g_mkv, g_out = _all_gather_background(mix_shards, 5, "all_gather_mix_weights")
    ffn_shards = [_to_wire([w_gate[0].T, w_up[0].T], g_in, "wire_gate_up")]
    (w_gu,) = _all_gather_background(ffn_shards, 1, "all_gather_ffn_weights")
    down_shards = [_to_wire([w_down[0]], g_in, "wire_down")[0]]
    (w_d,) = _all_gather_background(down_shards, 6, "all_gather_down_weights")
    wi = g_in.reshape(N_DEV * n_in, d)
    wi = jnp.concatenate([wi[0:1024], wi[1280:1792], wi[1792:2304], wi[2368:2880],
                          wi[1024:1152], wi[1152:1280], wi[2304:2368],
                          jnp.zeros((IN_PAD - 2880, d), wi.dtype)], axis=0)
    wq = g_uq.reshape(768, 512)
    wq = jnp.concatenate([wq[192 * h: 192 * h + 128] for h in range(4)]
                         + [wq[192 * h + 128: 192 * (h + 1)] for h in range(4)], axis=0)
    wmkv = g_mkv.reshape(-1, g_mkv.shape[-1])
    wo = g_out.reshape(-1, d)

    pos = positions[0].astype(jnp.float32)
    inv_freq = ROPE_THETA ** (-jnp.arange(0, MLA_ROPE, 2, dtype=jnp.float32) / MLA_ROPE)
    ang = pos[:, None] * inv_freq
    cos32, sin32 = jnp.cos(ang), jnp.sin(ang)
    cos_t = jnp.tile(cos32, (1, 4))
    sin_t = jnp.tile(jnp.concatenate([-sin32, sin32], axis=1), (1, 2))
    posc, posr = pos.reshape(s, 1), pos.reshape(1, s)
    two = lambda g: jnp.tile(g, (1, 2))
    gq2, gk2, gqr2, gkr2 = two(swa_q_norm_g), two(swa_k_norm_g), two(mla_qr_norm_g), two(mla_kr_norm_g)
    sinks1 = swa_sinks[0]

    proj, hn = _in_proj(x2, attn_norm_g, wi)
    qc, kc, vb, qb, kvb, cqn, ckvn = _mla_prep(proj, cos_t, sin_t, mla_cq_norm_g, mla_ckv_norm_g, wq, wkv,
                                                mla_qn_norm_g, gqr2, mla_kn_norm_g, gkr2)
    y_b, lse_b = _mla_fwd(qc, kc, vb)
    km, vmm, kvm, memn = _memkv_prep(mem2, mem_norm_g, wmkv, mem_k_norm_g)
    y_m, lse_m = _mem_fwd(proj, mem_q_norm_g, km, vmm)
    y_a, lse_a = _swa_fwd(proj, posc, posr, gq2, gk2, sinks1)
    h1, fn = _out_proj(y_a, y_b, y_m, x2, wo, ffn_norm_g)
    gu, act = _ffn_gu(fn, w_gu)
    dout, dout_b, loss_tile = _ffn_down(act, w_d, h1, tgt)

    dgu, dw_d = _ffn_bwd_act(dout_b, w_d, gu)
    dw_gu = _ffn_dw_gu(fn, dgu)
    r_gu, r_d = _exchange_grads_background([dw_gu, dw_d], 2, "exchange_ffn_grads")
    d_h1, dg_ffn = _ffn_norm_bwd(_ffn_dfn(dgu, w_gu, dw_gu), dout, h1, ffn_norm_g)
    d_y = _mm(d_h1, wo, tb=True, out_dtype=jnp.float32, tm=FFN_TILE, tk=2048, name="d_mix")
    dw_out = jnp.concatenate([
        _mm(y_a, d_h1, ta=True, out_dtype=WIRE, tm=1024, tk=1024, name="dw_out_a"),
        _mm(y_b, d_h1, ta=True, out_dtype=WIRE, tm=1024, tk=1024, name="dw_out_b"),
        _mm(y_m, d_h1, ta=True, out_dtype=WIRE, tm=1024, tk=1024, name="dw_out_m")], axis=0)
    d_qm, dkm, dvmm, dg_mq = _mem_bwd(proj, mem_q_norm_g, km, vmm, d_y, y_m, lse_m)
    dw_mkv, dg_mem, dg_mk = _memkv_bwd(mem2, mem_norm_g, wmkv, mem_k_norm_g, kvm, memn, dkm, dvmm)
    r_mkv, r_out = _exchange_grads_background([dw_mkv.reshape(g_mkv.shape), dw_out.reshape(g_out.shape)], 3,
                                              "exchange_mix_grads")
    dqc, dkc, dvb = _mla_bwd(qc, kc, vb, d_y, y_b, lse_b, dw_mkv)
    (d_cq, d_ckv, d_kr, dw_uq, dw_ukv, dg_cq, dg_ckv, dg_qn, dg_qr, dg_kn, dg_kr) = _mla_prep_bwd(
        proj, cos_t, sin_t, mla_cq_norm_g, mla_ckv_norm_g, wq, wkv, mla_qn_norm_g, gqr2, mla_kn_norm_g, gkr2,
        qb, kvb, cqn, ckvn, dqc, dkc, dvb)
    d_qa, d_ka, d_va, dg_q, dg_k, d_sinks = _swa_bwd(proj, posc, posr, gq2, gk2, sinks1, d_y, y_a, lse_a, dw_out)
    d_proj = [d_qa, d_cq, d_ckv, d_qm, d_ka, d_va, d_kr]
    gi = _dw_in(hn, d_proj, n_in)

    gq_ = jnp.concatenate(sum([[dw_uq[128 * h: 128 * (h + 1)], dw_uq[512 + 64 * h: 512 + 64 * (h + 1)]]
                               for h in range(4)], []), axis=0)
    gq_ = gq_.reshape(N_DEV, 96, 512)
    grad_x, dg_attn = _dx(d_proj, wi, x2, attn_norm_g, d_h1, gi)
    small_g = {
        "attn_norm_g": dg_attn, "swa_q_norm_g": dg_q, "swa_k_norm_g": dg_k,
        "swa_sinks": d_sinks, "mla_cq_norm_g": dg_cq, "mla_ckv_norm_g": dg_ckv, "mla_qn_norm_g": dg_qn,
        "mla_qr_norm_g": dg_qr, "mla_kn_norm_g": dg_kn, "mla_kr_norm_g": dg_kr,
        "mem_norm_g": dg_mem, "mem_q_norm_g": dg_mq, "mem_k_norm_g": dg_mk, "ffn_norm_g": dg_ffn}
    pack = _small_pack([small_g[n] for n in _SMALL], loss_tile, [args[n].shape[-1] for n in _SMALL])
    pack8 = jnp.broadcast_to(pack, (N_DEV,) + pack.shape[1:])
    r_in, r_uq, r_ukv, packs = _exchange_grads_background([gi, gq_, dw_ukv, pack8], 4, "exchange_in_grads")

    big = {}
    last = [None]

    def adam(name, r, transposed=False, which=None):
        w, m, v = args[name][0], args["m_" + name][0], args["v_" + name][0]
        if transposed:
            outs = _adam_big(r, w.T, m.T, v.T, "adam_" + name, last[0], which)
            big[name] = [o.T[None] for o in outs]
        else:
            outs = _adam_big(r, w, m, v, "adam_" + name, last[0])
            big[name] = [o[None] for o in outs]
        last[0] = outs[0]

    adam("w_gate", r_gu, True, which=0)
    adam("w_up", r_gu, True, which=1)
    adam("w_down", r_d)
    adam("w_out", r_out)
    adam("w_mem_kv", r_mkv)
    adam("w_in", r_in, True)
    adam("w_uq", r_uq, True)
    adam("w_ukv", r_ukv)

    loss11, small_out = _small_adam(packs, [args[n] for n in _SMALL], [args["m_" + n] for n in _SMALL],
                                    [args["v_" + n] for n in _SMALL], last[0])
    small = dict(zip(_SMALL, small_out))
    loss = loss11.reshape(())

    order = ["attn_norm_g", "w_in", "swa_q_norm_g", "swa_k_norm_g", "swa_sinks", "mla_cq_norm_g", "mla_ckv_norm_g",
             "w_uq", "w_ukv", "mla_qn_norm_g", "mla_qr_norm_g", "mla_kn_norm_g", "mla_kr_norm_g", "mem_norm_g",
             "w_mem_kv", "mem_q_norm_g", "mem_k_norm_g", "w_out", "ffn_norm_g", "w_gate", "w_up", "w_down"]
    res = {n: (big[n] if n in big else list(small[n])) for n in order}
    outs = [loss, grad_x[None]]
    for kind in range(4):
        outs += [res[n][kind] for n in order]
    return tuple(outs)
```

```python
import jax
import jax.numpy as jnp
from jax import lax
from jax.experimental import pallas as pl
from jax.experimental.pallas import tpu as pltpu
from jax.experimental.pallas import tpu_sc as plsc

MXU = jnp.bfloat16
WIRE = jnp.bfloat16
EPS = 1e-6
NEG_INF = -1e30
LOG2E = 1.4426950408889634
N_DEV = 8
LANES = 128
ROW_TILE = 256
FFN_TILE = 512
ATT_TILE = 1024
SWA_BLOCK = 128
VMEM_LIMIT = 56 * 1024 * 1024

SWA_Q_HEADS, SWA_KV_HEADS, SWA_DIM = 16, 2, 64
MLA_HEADS, MLA_NOPE, MLA_ROPE, MLA_V = 4, 128, 64, 128
MEM_HEADS, MEM_DIM = 4, 128
ROPE_THETA = 10000.0
ADAM_LR, ADAM_B1, ADAM_B2, ADAM_EPS, ADAM_WD, ADAM_STEP = 0.001, 0.9, 0.999, 1e-08, 0.01, 10

C_QA, C_CQ, C_CKV, C_QM, C_KA, C_VA, C_KR, IN_PAD = 0, 1024, 1536, 2048, 2560, 2688, 2816, 2944


def _pcall(body, *, name, out_shape, in_specs, out_specs, grid=(), scratch=(), sem=None, after=None):
    params = pltpu.CompilerParams(dimension_semantics=sem, vmem_limit_bytes=VMEM_LIMIT)
    if after is not None:
        n_in, inner = len(in_specs), body

        def body(*refs):
            inner(*refs[:n_in], *refs[n_in + 1:])

        in_specs = list(in_specs) + [pl.BlockSpec(memory_space=pl.ANY)]
    call = pl.pallas_call(body, name=name, grid=grid, in_specs=in_specs, out_specs=out_specs,
                          out_shape=out_shape, scratch_shapes=list(scratch), compiler_params=params)
    return call if after is None else (lambda *ops: call(*ops, after))


def _sds(shape, dtype):
    return jax.ShapeDtypeStruct(tuple(shape), dtype)


def _dot(a, b):
    return jnp.dot(a.astype(MXU), b.astype(MXU), preferred_element_type=jnp.float32)


def _dot_nt(a, b):
    return lax.dot_general(a.astype(MXU), b.astype(MXU), (((1,), (1,)), ((), ())),
                           preferred_element_type=jnp.float32)


def _dot_tn(a, b):
    return lax.dot_general(a.astype(MXU), b.astype(MXU), (((0,), (0,)), ((), ())),
                           preferred_element_type=jnp.float32)


def _lo_mask(shape):
    return (lax.broadcasted_iota(jnp.int32, shape, len(shape) - 1) % LANES) < 64


def _norm_fwd(x, g, half=False):
    x2 = x * x
    if half:
        lo = _lo_mask(x.shape)
        s_lo = jnp.sum(jnp.where(lo, x2, 0.0), -1, keepdims=True)
        s_hi = jnp.sum(jnp.where(lo, 0.0, x2), -1, keepdims=True)
        r = jnp.where(lo, lax.rsqrt(s_lo / 64.0 + EPS), lax.rsqrt(s_hi / 64.0 + EPS))
    else:
        r = lax.rsqrt(jnp.mean(x2, -1, keepdims=True) + EPS)
    xn = x * r
    return xn * g, xn, r


def _norm_bwd(xn, r, g, dy, half=False):
    t = dy * g
    tx = t * xn
    if half:
        lo = _lo_mask(xn.shape)
        m_lo = jnp.sum(jnp.where(lo, tx, 0.0), -1, keepdims=True) / 64.0
        m_hi = jnp.sum(jnp.where(lo, 0.0, tx), -1, keepdims=True) / 64.0
        m = jnp.where(lo, m_lo, m_hi)
    else:
        m = jnp.mean(tx, -1, keepdims=True)
    dx = r * (t - xn * m)
    dg = jnp.sum(dy * xn, 0, keepdims=True)
    return dx, dg


def _swap32(x):
    lane = lax.broadcasted_iota(jnp.int32, x.shape, 1)
    return jnp.where((lane % 64) < 32, pltpu.roll(x, 96, 1), pltpu.roll(x, 32, 1))


def _rope(x, cos, sin):
    return x * cos + _swap32(x) * sin


def _rope_bwd(d, cos, sin):
    return d * cos + _swap32(d * sin)


def _my_coords():
    return lax.axis_index("x"), lax.axis_index("y"), lax.axis_index("c")


def _dev_index(px, py, pc):
    return 4 * px + 2 * py + pc


_FLIPS = [(0, 0, 1), (0, 1, 0), (0, 1, 1), (1, 0, 0), (1, 0, 1), (1, 1, 0), (1, 1, 1)]


def _flip(coords, f):
    return tuple((1 - v) if b else v for v, b in zip(coords, f))


def _all_gather(shards):
    n = len(shards)

    def body(*refs):
        ins, outs = refs[:n], refs[n:2 * n]
        send_sems, recv_sems, local_sems = refs[2 * n:]
        x, y, c = _my_coords()
        me, sibling = (x, y, c), (x, y, 1 - c)
        chips = [(1 - x, y), (x, 1 - y), (1 - x, 1 - y)]

        def copy(w, k, block, to, src=None):
            dst = outs[w].at[_dev_index(*block)]
            return pltpu.make_async_remote_copy(
                src_ref=dst if src is None else src, dst_ref=dst,
                send_sem=send_sems.at[w, k], recv_sem=recv_sems.at[w, k],
                device_id=to, device_id_type=pl.DeviceIdType.MESH)

        sends, locals_ = [], []
        for w in range(n):
            mine = pltpu.make_async_copy(ins[w], outs[w].at[_dev_index(*me)], local_sems.at[w])
            mine.start()
            locals_.append(mine)
            first = [copy(w, 0, me, sibling, src=ins[w])]
            first += [copy(w, 1 + j, me, (*chip, c), src=ins[w]) for j, chip in enumerate(chips)]
            for cp in first:
                cp.start()
            sends += first
        for w in range(n):
            for j, chip in enumerate(chips):
                copy(w, 1 + j, (*chip, c), me).wait_recv()
                fwd = copy(w, 4 + j, (*chip, c), sibling)
                fwd.start()
                sends.append(fwd)
        for w in range(n):
            copy(w, 0, sibling, me).wait_recv()
            for j, chip in enumerate(chips):
                copy(w, 4 + j, (*chip, 1 - c), me).wait_recv()
        for cp in sends:
            cp.wait_send()
        for mine in locals_:
            mine.wait()

    any_spec = pl.BlockSpec(memory_space=pl.ANY)
    return _pcall(
        body, name="all_gather_weights",
        out_shape=[_sds((N_DEV,) + s.shape, s.dtype) for s in shards],
        in_specs=[any_spec] * n, out_specs=[any_spec] * n,
        scratch=[pltpu.SemaphoreType.DMA((n, 7)), pltpu.SemaphoreType.DMA((n, 7)),
                 pltpu.SemaphoreType.DMA((n,))])(*shards)


def _wire_cost(arrays):
    nbytes = sum(a.size * a.dtype.itemsize for a in arrays)
    return pl.CostEstimate(flops=0, transcendentals=0, bytes_accessed=40 * nbytes)


def _all_gather_background(shards, collective_id, name):
    n = len(shards)
    src_refs = [jax.new_ref(s, memory_space=pltpu.MemorySpace.HBM) for s in shards]
    out_refs = [jax.empty_ref(_sds((N_DEV,) + s.shape, s.dtype), memory_space=pltpu.MemorySpace.HBM) for s in shards]

    @pl.kernel(mesh=plsc.ScalarSubcoreMesh(axis_name="seq", num_cores=1), name=name,
               scratch_types=(pltpu.SemaphoreType.DMA((n, 7)), pltpu.SemaphoreType.DMA((n, 7)),
                              pltpu.SemaphoreType.DMA((n,))),
               compiler_params=pltpu.CompilerParams(collective_id=collective_id))
    def launch(send_sems, recv_sems, local_sems):
        x, y, c = _my_coords()
        me, sibling = (x, y, c), (x, y, 1 - c)
        chips = [(1 - x, y), (x, 1 - y), (1 - x, 1 - y)]
        barrier = pltpu.get_barrier_semaphore()
        for peer in [sibling] + [(*chip, c) for chip in chips]:
            pl.semaphore_signal(barrier, inc=1, device_id=peer, device_id_type=pl.DeviceIdType.MESH)
        pl.semaphore_wait(barrier, 4)

        def copy(w, k, block, to, src=None):
            dst = out_refs[w].at[_dev_index(*block)]
            return pltpu.make_async_remote_copy(
                src_ref=dst if src is None else src, dst_ref=dst,
                send_sem=send_sems.at[w, k], recv_sem=recv_sems.at[w, k],
                device_id=to, device_id_type=pl.DeviceIdType.MESH)

        sends, locals_ = [], []
        for w in range(n):
            mine = pltpu.make_async_copy(src_refs[w], out_refs[w].at[_dev_index(*me)], local_sems.at[w])
            mine.start()
            locals_.append(mine)
            first = [copy(w, 0, me, sibling, src=src_refs[w])]
            first += [copy(w, 1 + j, me, (*chip, c), src=src_refs[w]) for j, chip in enumerate(chips)]
            for cp in first:
                cp.start()
            sends += first
        for w in range(n):
            for j, chip in enumerate(chips):
                copy(w, 1 + j, (*chip, c), me).wait_recv()
                fwd = copy(w, 4 + j, (*chip, c), sibling)
                fwd.start()
                sends.append(fwd)
        for w in range(n):
            copy(w, 0, sibling, me).wait_recv()
            for j, chip in enumerate(chips):
                copy(w, 4 + j, (*chip, 1 - c), me).wait_recv()
        for cp in sends:
            cp.wait_send()
        for mine in locals_:
            mine.wait()

    launch()
    return [r[...] for r in out_refs]


def _exchange_grads(grads):
    n = len(grads)

    def body(*refs):
        ins, outs = refs[:n], refs[n:2 * n]
        send_sems, recv_sems, local_sems = refs[2 * n:]
        me = _my_coords()
        my_idx = _dev_index(*me)
        sends, locals_ = [], []
        for w in range(n):
            mine = pltpu.make_async_copy(ins[w].at[my_idx], outs[w].at[my_idx], local_sems.at[w])
            mine.start()
            locals_.append(mine)
            for k, f in enumerate(_FLIPS):
                peer = _flip(me, f)
                cp = pltpu.make_async_remote_copy(
                    src_ref=ins[w].at[_dev_index(*peer)], dst_ref=outs[w].at[my_idx],
                    send_sem=send_sems.at[w, k], recv_sem=recv_sems.at[w, k],
                    device_id=peer, device_id_type=pl.DeviceIdType.MESH)
                cp.start()
                sends.append(cp)
        for w in range(n):
            for k, f in enumerate(_FLIPS):
                peer = _flip(me, f)
                slot = outs[w].at[_dev_index(*peer)]
                pltpu.make_async_remote_copy(
                    src_ref=slot, dst_ref=slot,
                    send_sem=send_sems.at[w, k], recv_sem=recv_sems.at[w, k],
                    device_id=peer, device_id_type=pl.DeviceIdType.MESH).wait_recv()
        for cp in sends:
            cp.wait_send()
        for mine in locals_:
            mine.wait()

    any_spec = pl.BlockSpec(memory_space=pl.ANY)
    return _pcall(
        body, name="exchange_grads",
        out_shape=[_sds(g.shape, g.dtype) for g in grads],
        in_specs=[any_spec] * n, out_specs=[any_spec] * n,
        scratch=[pltpu.SemaphoreType.DMA((n, 7)), pltpu.SemaphoreType.DMA((n, 7)),
                 pltpu.SemaphoreType.DMA((n,))])(*grads)


def _exchange_grads_background(grads, collective_id, name):
    n = len(grads)
    src_refs = [jax.new_ref(g, memory_space=pltpu.MemorySpace.HBM) for g in grads]
    out_refs = [jax.empty_ref(_sds(g.shape, g.dtype), memory_space=pltpu.MemorySpace.HBM) for g in grads]

    @pl.kernel(mesh=plsc.ScalarSubcoreMesh(axis_name="seq", num_cores=1), name=name,
               scratch_types=(pltpu.SemaphoreType.DMA((n, 7)), pltpu.SemaphoreType.DMA((n, 7)),
                              pltpu.SemaphoreType.DMA((n,))),
               cost_estimate=_wire_cost(grads),
               compiler_params=pltpu.CompilerParams(collective_id=collective_id))
    def launch(send_sems, recv_sems, local_sems):
        me = _my_coords()
        my_idx = _dev_index(*me)
        peers = [_flip(me, f) for f in _FLIPS]
        barrier = pltpu.get_barrier_semaphore()
        for peer in peers:
            pl.semaphore_signal(barrier, inc=1, device_id=peer, device_id_type=pl.DeviceIdType.MESH)
        pl.semaphore_wait(barrier, len(peers))
        sends, locals_ = [], []
        for w in range(n):
            mine = pltpu.make_async_copy(src_refs[w].at[my_idx], out_refs[w].at[my_idx], local_sems.at[w])
            mine.start()
            locals_.append(mine)
            for k, peer in enumerate(peers):
                cp = pltpu.make_async_remote_copy(
                    src_ref=src_refs[w].at[_dev_index(*peer)], dst_ref=out_refs[w].at[my_idx],
                    send_sem=send_sems.at[w, k], recv_sem=recv_sems.at[w, k],
                    device_id=peer, device_id_type=pl.DeviceIdType.MESH)
                cp.start()
                sends.append(cp)
        for w in range(n):
            for k, peer in enumerate(peers):
                slot = out_refs[w].at[_dev_index(*peer)]
                pltpu.make_async_remote_copy(
                    src_ref=slot, dst_ref=slot, send_sem=send_sems.at[w, k], recv_sem=recv_sems.at[w, k],
                    device_id=peer, device_id_type=pl.DeviceIdType.MESH).wait_recv()
        for cp in sends:
            cp.wait_send()
        for mine in locals_:
            mine.wait()

    launch()
    return [r[...] for r in out_refs]


def _to_wire(parts, after, name):
    n = len(parts)
    rows, cols = parts[0].shape
    tr = rows // 2 if rows % 32 == 0 else rows

    def body(*refs):
        for k in range(n):
            refs[n][k] = refs[k][...].astype(WIRE)

    blk = pl.BlockSpec((tr, cols), lambda i: (i, 0))
    return _pcall(
        body, name=name, grid=(rows // tr,), out_shape=_sds((n, rows, cols), WIRE),
        in_specs=[blk] * n, out_specs=pl.BlockSpec((n, tr, cols), lambda i: (0, i, 0)),
        sem=("parallel",), after=after)(*parts)


def _adam_math(w, g, m, v):
    m = ADAM_B1 * m + (1.0 - ADAM_B1) * g
    v = ADAM_B2 * v + (1.0 - ADAM_B2) * (g * g)
    m_hat = m / (1.0 - ADAM_B1 ** ADAM_STEP)
    v_hat = v / (1.0 - ADAM_B2 ** ADAM_STEP)
    delta = -ADAM_LR * (m_hat / (jnp.sqrt(v_hat) + ADAM_EPS) + ADAM_WD * w)
    return delta, m, v


def _small_layout(sizes):
    row0, r = [], 0
    for n in sizes:
        row0.append(r)
        r += -(-n // LANES)
    return row0, r, -(-(r + 1) // 8) * 8


def _small_pieces(n):
    return [(k, min(LANES, n - LANES * k)) for k in range(-(-n // LANES))]


def _small_fill(pack, slot, srcs, sizes, row0, rows):
    pack[slot] = jnp.zeros((rows, LANES), jnp.float32)
    for p, n in enumerate(sizes):
        val = srcs[p][...]
        if val.shape[-1] == LANES and n == 64:
            pack[slot, row0[p]:row0[p] + 1, :] = val + pltpu.roll(val, 64, 1)
            continue
        for k, width in _small_pieces(n):
            pack[slot, row0[p] + k:row0[p] + k + 1, 0:width] = srcs[p][:, LANES * k:LANES * k + width]


def _small_allreduce(grads, loss_tile, sizes):
    n_par = len(sizes)
    row0, loss_row, rows = _small_layout(sizes)

    def body(*refs):
        g_refs, loss_in, out_ref = refs[:n_par], refs[n_par], refs[n_par + 1]
        pack, gath, send_sems, recv_sems = refs[n_par + 2:]
        me = _my_coords()
        my_idx = _dev_index(*me)
        _small_fill(pack, 0, g_refs, sizes, row0, rows)
        pack[0, loss_row:loss_row + 1, :] = loss_in[0:1, :]
        gath[my_idx] = pack[0]
        sends = []
        for k, f in enumerate(_FLIPS):
            peer = _flip(me, f)
            cp = pltpu.make_async_remote_copy(
                src_ref=pack.at[0], dst_ref=gath.at[my_idx],
                send_sem=send_sems.at[k], recv_sem=recv_sems.at[k],
                device_id=peer, device_id_type=pl.DeviceIdType.MESH)
            cp.start()
            sends.append(cp)
        for k, f in enumerate(_FLIPS):
            peer = _flip(me, f)
            slot = gath.at[_dev_index(*peer)]
            pltpu.make_async_remote_copy(
                src_ref=slot, dst_ref=slot, send_sem=send_sems.at[k], recv_sem=recv_sems.at[k],
                device_id=peer, device_id_type=pl.DeviceIdType.MESH).wait_recv()
        for cp in sends:
            cp.wait_send()
        g = gath[0]
        for d in range(1, N_DEV):
            g = g + gath[d]
        out_ref[...] = g

    vm = pl.BlockSpec(memory_space=pltpu.VMEM)
    return _pcall(
        body, name="small_allreduce", out_shape=_sds((rows, LANES), jnp.float32),
        in_specs=[vm] * (n_par + 1), out_specs=vm,
        scratch=[pltpu.VMEM((1, rows, LANES), jnp.float32), pltpu.VMEM((N_DEV, rows, LANES), jnp.float32),
                 pltpu.SemaphoreType.DMA((7,)), pltpu.SemaphoreType.DMA((7,))])(*grads, loss_tile)


def _small_adam(packed_g, ws, ms, vs):
    sizes = [w.shape[-1] for w in ws]
    n_par = len(ws)
    row0, loss_row, rows = _small_layout(sizes)

    def body(*refs):
        g_ref = refs[0]
        w_refs, m_refs, v_refs = (refs[1 + k * n_par: 1 + (k + 1) * n_par] for k in range(3))
        loss_out = refs[3 * n_par + 1]
        out_refs = refs[3 * n_par + 2: 7 * n_par + 2]
        pack, res = refs[7 * n_par + 2:]
        for slot, srcs in enumerate((w_refs, m_refs, v_refs)):
            _small_fill(pack, slot, srcs, sizes, row0, rows)
        g = g_ref[...]
        delta, m, v = _adam_math(pack[0], g, pack[1], pack[2])
        res[0], res[1], res[2], res[3] = g, delta, m, v
        loss_out[...] = res[0, loss_row:loss_row + 1, 0:1]
        for p, n in enumerate(sizes):
            for kind in range(4):
                for k, width in _small_pieces(n):
                    out_refs[4 * p + kind][:, LANES * k:LANES * k + width] = (
                        res[kind, row0[p] + k:row0[p] + k + 1, 0:width])

    vm = pl.BlockSpec(memory_space=pltpu.VMEM)
    out_shape = [_sds((1, 1), jnp.float32)]
    for n in sizes:
        out_shape += [_sds((1, n), jnp.float32)] * 4
    outs = _pcall(
        body, name="small_adam", out_shape=out_shape,
        in_specs=[vm] * (3 * n_par + 1), out_specs=[vm] * len(out_shape),
        scratch=[pltpu.VMEM((3, rows, LANES), jnp.float32), pltpu.VMEM((4, rows, LANES), jnp.float32)])(
            packed_g, *ws, *ms, *vs)
    return outs[0], [outs[1 + 4 * p: 5 + 4 * p] for p in range(n_par)]


def _adam_big(recv, w, m, v, name, after=None, which=None):
    rows, cols = recv.shape[-2:]
    row_tiles = [t for t in range(16, rows + 1, 16) if rows % t == 0 and t * cols <= 400 * 1024]
    tr, tc = (max(row_tiles), cols) if row_tiles else (rows, 512 if cols % 512 == 0 else cols)

    def body(r_ref, w_ref, m_ref, v_ref, g_ref, d_ref, mo_ref, vo_ref):
        g = r_ref[0].astype(jnp.float32)
        for d in range(1, N_DEV):
            g = g + r_ref[d].astype(jnp.float32)
        delta, mn, vn = _adam_math(w_ref[...], g, m_ref[...], v_ref[...])
        g_ref[...] = g
        d_ref[...] = delta
        mo_ref[...] = mn
        vo_ref[...] = vn

    blk = pl.BlockSpec((tr, tc), lambda i, j: (i, j))
    if which is None:
        r_spec = pl.BlockSpec((N_DEV, tr, tc), lambda i, j: (0, i, j))
    else:
        r_spec = pl.BlockSpec((N_DEV, None, tr, tc), lambda i, j: (0, which, i, j))
    return _pcall(
        body, name=name, grid=(rows // tr, cols // tc),
        out_shape=[_sds((rows, cols), jnp.float32)] * 4,
        in_specs=[r_spec, blk, blk, blk],
        out_specs=[blk] * 4, sem=("parallel", "parallel"), after=after)(recv, w, m, v)


def _mm(a, b, *, ta=False, tb=False, out_dtype, tm, tk, name):
    (kdim, mdim) = a.shape if ta else a.shape[::-1]
    ndim = b.shape[0] if tb else b.shape[1]
    tm, tk = min(tm, mdim), min(tk, kdim)
    nk = kdim // tk

    def body(a_ref, b_ref, o_ref, acc):
        k = pl.program_id(1)
        if ta:
            part = _dot_tn(a_ref[...], b_ref[...])
        elif tb:
            part = _dot_nt(a_ref[...], b_ref[...])
        else:
            part = _dot(a_ref[...], b_ref[...])

        @pl.when(k == 0)
        def _():
            acc[...] = part

        @pl.when(k > 0)
        def _():
            acc[...] += part

        @pl.when(k == nk - 1)
        def _():
            o_ref[...] = acc[...].astype(o_ref.dtype)

    a_spec = pl.BlockSpec((tk, tm), lambda i, k: (k, i)) if ta else pl.BlockSpec((tm, tk), lambda i, k: (i, k))
    b_spec = pl.BlockSpec((ndim, tk), lambda i, k: (0, k)) if tb else pl.BlockSpec((tk, ndim), lambda i, k: (k, 0))
    return _pcall(
        body, name=name, grid=(mdim // tm, nk), out_shape=_sds((mdim, ndim), out_dtype),
        in_specs=[a_spec, b_spec], out_specs=pl.BlockSpec((tm, ndim), lambda i, k: (i, 0)),
        scratch=[pltpu.VMEM((tm, ndim), jnp.float32)], sem=("parallel", "arbitrary"))(a, b)


def _ref_col_pieces(start, stop):
    ref_starts = [0, 1024, 1152, 1280, 1792, 2304, 2368, 2880]
    perm_starts = [C_QA, C_KA, C_VA, C_CQ, C_CKV, C_KR, C_QM]
    out = []
    for p in range(7):
        lo, hi = max(start, ref_starts[p]), min(stop, ref_starts[p + 1])
        if lo < hi:
            out.append((lo - start, perm_starts[p] + lo - ref_starts[p], hi - lo))
    return out


def _dw_in(hn, d_proj, n_shard):
    s, d = hn.shape
    n = sum(p.shape[1] for p in d_proj)
    n_pc = len(d_proj)
    tm, tk = min(512, d), min(1024, s)
    nk = s // tk

    def body(a_ref, *refs):
        b_refs, (o_ref, acc) = refs[:n_pc], refs[n_pc:]
        k = pl.program_id(1)
        part = _dot_tn(a_ref[...], jnp.concatenate([r[...] for r in b_refs], axis=1))

        @pl.when(k == 0)
        def _():
            acc[...] = part

        @pl.when(k > 0)
        def _():
            acc[...] += part

        @pl.when(k == nk - 1)
        def _():
            t = acc[...].T
            for j in range(N_DEV):
                rows = [t[src:src + width] for _, src, width in _ref_col_pieces(j * n_shard, (j + 1) * n_shard)]
                o_ref[j] = jnp.concatenate(rows, axis=0).astype(o_ref.dtype)

    return _pcall(
        body, name="dw_in", grid=(d // tm, nk), out_shape=_sds((N_DEV, n_shard, d), WIRE),
        in_specs=[pl.BlockSpec((tk, tm), lambda i, k: (k, i))]
        + [pl.BlockSpec((tk, p.shape[1]), lambda i, k: (k, 0)) for p in d_proj],
        out_specs=pl.BlockSpec((N_DEV, n_shard, tm), lambda i, k: (0, 0, i)),
        scratch=[pltpu.VMEM((tm, n), jnp.float32)], sem=("parallel", "arbitrary"))(hn, *d_proj)


def _in_proj(x, g, w):
    s, d = x.shape
    n = w.shape[0]
    tm = min(2 * ROW_TILE, s)

    def body(x_ref, g_ref, w_ref, p_ref, hn_ref):
        hn, _, _ = _norm_fwd(x_ref[...], g_ref[...])
        hn_ref[...] = hn.astype(hn_ref.dtype)
        p_ref[...] = _dot_nt(hn, w_ref[...])

    return _pcall(
        body, name="in_proj", grid=(s // tm,),
        out_shape=[_sds((s, n), jnp.float32), _sds((s, d), MXU)],
        in_specs=[pl.BlockSpec((tm, d), lambda i: (i, 0)), pl.BlockSpec((1, d), lambda i: (0, 0)),
                  pl.BlockSpec((n, d), lambda i: (0, 0), pipeline_mode=pl.Buffered(1))],
        out_specs=[pl.BlockSpec((tm, n), lambda i: (i, 0)), pl.BlockSpec((tm, d), lambda i: (i, 0))],
        sem=("parallel",))(x, g, w)


def _norm_rows(x, g):
    s, d = x.shape
    tm = min(FFN_TILE, s)

    def body(x_ref, g_ref, hn_ref):
        hn, _, _ = _norm_fwd(x_ref[...], g_ref[...])
        hn_ref[...] = hn.astype(hn_ref.dtype)

    row = pl.BlockSpec((tm, d), lambda i: (i, 0))
    return _pcall(body, name="norm_rows", grid=(s // tm,), out_shape=_sds((s, d), MXU),
                  in_specs=[row, pl.BlockSpec((1, d), lambda i: (0, 0))], out_specs=row, sem=("parallel",))(x, g)


def _mla_prep(proj, cos, sin, g_cq, g_ckv, w_uq, w_ukv, g_qn, g_qr, g_kn, g_kr):
    s = proj.shape[0]
    tm = min(ROW_TILE, s)
    nh = MLA_HEADS

    def body(cq_ref, ckv_ref, kr_ref, cos_ref, sin_ref, gcq_ref, gckv_ref, wuq_ref, wukv_ref,
             gqn_ref, gqr_ref, gkn_ref, gkr_ref,
             qc_ref, kc_ref, v_ref, qb_ref, kvb_ref, cqn_ref, ckvn_ref):
        cos_t, sin_t = cos_ref[...], sin_ref[...]
        lo = _lo_mask((tm, LANES))
        cqn, _, _ = _norm_fwd(cq_ref[...], gcq_ref[...])
        cqn_ref[...] = cqn.astype(cqn_ref.dtype)
        qb = _dot_nt(cqn, wuq_ref[...])
        qb_ref[...] = qb
        ckvn, _, _ = _norm_fwd(ckv_ref[...], gckv_ref[...])
        ckvn_ref[...] = ckvn.astype(ckvn_ref.dtype)
        w_ukv_full = jnp.concatenate([wukv_ref[dev] for dev in range(N_DEV)], axis=1)
        kvb = _dot(ckvn, w_ukv_full)
        kvb_ref[...] = kvb
        kr, _, _ = _norm_fwd(kr_ref[...], gkr_ref[...], half=True)
        kr = _rope(kr, cos_t, sin_t)
        kr2 = jnp.where(lo, kr, pltpu.roll(kr, 64, 1))
        ropes = []
        for j in range(nh // 2):
            xr = qb[:, nh * MLA_NOPE + LANES * j: nh * MLA_NOPE + LANES * (j + 1)]
            qr, _, _ = _norm_fwd(xr, gqr_ref[...], half=True)
            ropes.append(_rope(qr, cos_t, sin_t))
        for h in range(nh):
            qn, _, _ = _norm_fwd(qb[:, MLA_NOPE * h: MLA_NOPE * (h + 1)], gqn_ref[...])
            mask = lo if h % 2 == 0 else jnp.logical_not(lo)
            qr = jnp.where(mask, ropes[h // 2], 0.0)
            qc_ref[h] = jnp.concatenate([qn, qr], axis=1).astype(qc_ref.dtype)
            kn, _, _ = _norm_fwd(kvb[:, 256 * h: 256 * h + MLA_NOPE], gkn_ref[...])
            kc_ref[h] = jnp.concatenate([kn, kr2], axis=1).astype(kc_ref.dtype)
            v_ref[h] = kvb[:, 256 * h + MLA_NOPE: 256 * (h + 1)].astype(v_ref.dtype)

    def col(width, start):
        return pl.BlockSpec((tm, width), lambda i: (i, start // width))

    def full(shape):
        return pl.BlockSpec(shape, lambda i: (0,) * len(shape))

    def row(width):
        return pl.BlockSpec((tm, width), lambda i: (i, 0))

    def heads(width):
        return pl.BlockSpec((nh, tm, width), lambda i: (0, i, 0))

    return _pcall(
        body, name="mla_prep", grid=(s // tm,),
        out_shape=[_sds((nh, s, 256), MXU), _sds((nh, s, 256), MXU), _sds((nh, s, MLA_V), MXU),
                   _sds((s, 768), jnp.float32), _sds((s, 1024), jnp.float32),
                   _sds((s, 512), MXU), _sds((s, 512), MXU)],
        in_specs=[col(512, C_CQ), col(512, C_CKV), col(LANES, C_KR), row(LANES), row(LANES),
                  full((1, 512)), full((1, 512)), full((768, 512)), full((N_DEV, 512, LANES)),
                  full((1, LANES)), full((1, LANES)), full((1, LANES)), full((1, LANES))],
        out_specs=[heads(256), heads(256), heads(MLA_V), row(768), row(1024), row(512), row(512)],
        sem=("parallel",))(proj, proj, proj, cos, sin, g_cq, g_ckv, w_uq, w_ukv, g_qn, g_qr, g_kn, g_kr)


def _mla_fwd(qc, kc, v):
    nh, s, _ = qc.shape
    t = min(ATT_TILE, s)
    nb = s // t
    scale = (MLA_NOPE + MLA_ROPE) ** -0.5

    def body(q_ref, k_ref, v_ref, y_ref, lse_ref, m_sc, l_sc, acc):
        qi, ki = pl.program_id(1), pl.program_id(2)

        @pl.when(ki == 0)
        def _():
            m_sc[...] = jnp.full_like(m_sc, NEG_INF)
            l_sc[...] = jnp.zeros_like(l_sc)
            acc[...] = jnp.zeros_like(acc)

        def step(diagonal):
            rc = t // 4 if diagonal else t
            for c in range(t // rc):
                rows = slice(rc * c, rc * (c + 1))
                keys = slice(0, rc * (c + 1))
                sc = _dot_nt(q_ref[0, rows, :], k_ref[0, keys, :]) * (scale * LOG2E)
                if diagonal:
                    r_i = lax.broadcasted_iota(jnp.int32, sc.shape, 0) + rc * c
                    c_i = lax.broadcasted_iota(jnp.int32, sc.shape, 1)
                    sc = jnp.where(c_i <= r_i, sc, NEG_INF)
                m_old = m_sc[rows, :]
                m_new = jnp.maximum(m_old, jnp.max(sc, -1, keepdims=True))
                alpha = jnp.exp2(m_old - m_new)
                p = jnp.exp2(sc - m_new)
                l_sc[rows, :] = alpha * l_sc[rows, :] + jnp.sum(p, -1, keepdims=True)
                acc[rows, :] = alpha * acc[rows, :] + _dot(p, v_ref[0, keys, :])
                m_sc[rows, :] = m_new

        @pl.when(ki < qi)
        def _():
            step(False)

        @pl.when(ki == qi)
        def _():
            step(True)

        @pl.when(ki == qi)
        def _():
            y_ref[...] = acc[...] / l_sc[...]
            lse_ref[0] = m_sc[...] + jnp.log2(l_sc[...])

    return _pcall(
        body, name="mla_fwd", grid=(nh, nb, nb),
        out_shape=[_sds((s, nh * MLA_V), jnp.float32), _sds((nh, s, 1), jnp.float32)],
        in_specs=[pl.BlockSpec((1, t, 256), lambda h, i, k: (h, i, 0)),
                  pl.BlockSpec((1, t, 256), lambda h, i, k: (h, jnp.minimum(k, i), 0)),
                  pl.BlockSpec((1, t, MLA_V), lambda h, i, k: (h, jnp.minimum(k, i), 0))],
        out_specs=[pl.BlockSpec((t, MLA_V), lambda h, i, k: (i, h)),
                   pl.BlockSpec((1, t, 1), lambda h, i, k: (h, i, 0))],
        scratch=[pltpu.VMEM((t, 1), jnp.float32), pltpu.VMEM((t, 1), jnp.float32),
                 pltpu.VMEM((t, MLA_V), jnp.float32)],
        sem=("parallel", "parallel", "arbitrary"))(qc, kc, v)


def _memkv_prep(mem, g_mem, w_mkv, g_mk):
    ml, d = mem.shape
    hw = MEM_HEADS * MEM_DIM

    def body(mem_ref, g_ref, w_ref, gk_ref, k_ref, v_ref, kv_ref, mn_ref):
        mn, _, _ = _norm_fwd(mem_ref[...], g_ref[...])
        mn_ref[...] = mn.astype(mn_ref.dtype)
        kv = _dot(mn, w_ref[...])
        kv_ref[...] = kv
        for h in range(MEM_HEADS):
            kn, _, _ = _norm_fwd(kv[:, MEM_DIM * h: MEM_DIM * (h + 1)], gk_ref[...])
            k_ref[:, MEM_DIM * h: MEM_DIM * (h + 1)] = kn.astype(k_ref.dtype)
        v_ref[...] = kv[:, hw:].astype(v_ref.dtype)

    vm = pl.BlockSpec(memory_space=pltpu.VMEM)
    return _pcall(
        body, name="memkv_prep",
        out_shape=[_sds((ml, hw), MXU), _sds((ml, hw), MXU), _sds((ml, 2 * hw), jnp.float32), _sds((ml, d), MXU)],
        in_specs=[vm] * 4, out_specs=[vm] * 4)(mem, g_mem, w_mkv, g_mk)


def _mem_fwd(proj, g_mq, km, vmm):
    s = proj.shape[0]
    ml, hw = km.shape
    tm = min(FFN_TILE, s)
    scale = MEM_DIM ** -0.5

    def body(q_ref, g_ref, k_ref, v_ref, y_ref, lse_ref):
        col = lax.broadcasted_iota(jnp.int32, (tm, MEM_HEADS), 1)
        lse_t = jnp.zeros((tm, MEM_HEADS), jnp.float32)
        for h in range(MEM_HEADS):
            sl = slice(MEM_DIM * h, MEM_DIM * (h + 1))
            qn, _, _ = _norm_fwd(q_ref[:, sl], g_ref[...])
            sc = _dot_nt(qn, k_ref[:, sl]) * scale
            m = jnp.max(sc, -1, keepdims=True)
            p = jnp.exp(sc - m)
            l = jnp.sum(p, -1, keepdims=True)
            y_ref[:, sl] = _dot(p, v_ref[:, sl]) / l
            lse_t = jnp.where(col == h, m + jnp.log(l), lse_t)
        lse_ref[...] = lse_t

    return _pcall(
        body, name="mem_fwd", grid=(s // tm,),
        out_shape=[_sds((s, hw), jnp.float32), _sds((s, MEM_HEADS), jnp.float32)],
        in_specs=[pl.BlockSpec((tm, hw), lambda i: (i, C_QM // hw)), pl.BlockSpec((1, MEM_DIM), lambda i: (0, 0)),
                  pl.BlockSpec((ml, hw), lambda i: (0, 0)), pl.BlockSpec((ml, hw), lambda i: (0, 0))],
        out_specs=[pl.BlockSpec((tm, hw), lambda i: (i, 0)), pl.BlockSpec((tm, MEM_HEADS), lambda i: (i, 0))],
        sem=("parallel",))(proj, g_mq, km, vmm)


def _alibi_slope(h):
    return float(2.0 ** (-8.0 * (h + 1) / SWA_Q_HEADS))


def _swa_common(n, kp, kc, vp, vc, pq, pkp, pkc, gk):
    b = SWA_BLOCK
    k_raw = jnp.concatenate([kp, kc], axis=0)
    kn, kxn, kr = _norm_fwd(k_raw, gk, half=True)
    v = jnp.concatenate([vp, vc], axis=0)
    dist = jnp.abs(pq - jnp.concatenate([pkp, pkc], axis=1))
    r_i = lax.broadcasted_iota(jnp.int32, (b, 2 * b), 0)
    c_i = lax.broadcasted_iota(jnp.int32, (b, 2 * b), 1)
    valid = (c_i > r_i) & (c_i <= r_i + b) & (c_i >= jnp.where(n > 0, 0, b))
    bias = jnp.where(valid, -dist, NEG_INF)
    return kn, v, bias


def _swa_folded(n, kp, kc, pq, pkp, pkc, gk):
    b = SWA_BLOCK
    kn_p, _, _ = _norm_fwd(kp, gk, half=True)
    kn_c, _, _ = _norm_fwd(kc, gk, half=True)
    r_i = lax.broadcasted_iota(jnp.int32, (b, b), 0)
    c_i = lax.broadcasted_iota(jnp.int32, (b, b), 1)
    upper = c_i > r_i
    bias_prev = jnp.where(n > 0, 0.0, NEG_INF) - jnp.abs(pq - pkp)
    bias = jnp.where(upper, bias_prev, -jnp.abs(pq - pkc))
    return kn_p, kn_c, bias, upper


def _swa_specs(s):
    b = SWA_BLOCK
    prev = lambda n: jnp.maximum(n - 1, 0)
    return [
        pl.BlockSpec((b, 1024), lambda n: (n, C_QA // 1024)),
        pl.BlockSpec((b, LANES), lambda n: (prev(n), C_KA // LANES)),
        pl.BlockSpec((b, LANES), lambda n: (n, C_KA // LANES)),
        pl.BlockSpec((b, LANES), lambda n: (prev(n), C_VA // LANES)),
        pl.BlockSpec((b, LANES), lambda n: (n, C_VA // LANES)),
        pl.BlockSpec((b, 1), lambda n: (n, 0)),
        pl.BlockSpec((1, b), lambda n: (0, prev(n))),
        pl.BlockSpec((1, b), lambda n: (0, n)),
        pl.BlockSpec((1, LANES), lambda n: (0, 0)),
        pl.BlockSpec((1, LANES), lambda n: (0, 0)),
        pl.BlockSpec(memory_space=pltpu.SMEM),
    ]


def _swa_fwd(proj, posc, posr, gq, gk, sinks):
    s = proj.shape[0]
    b = SWA_BLOCK
    scale = SWA_DIM ** -0.5

    def body(q_ref, kp_ref, kc_ref, vp_ref, vc_ref, pq_ref, pkp_ref, pkc_ref, gq_ref, gk_ref, sink_ref,
             y_ref, lse_ref):
        n = pl.program_id(0)
        kn_p, kn_c, bias, upper = _swa_folded(n, kp_ref[...], kc_ref[...], pq_ref[...], pkp_ref[...], pkc_ref[...],
                                              gk_ref[...])
        v_p, v_c = vp_ref[...], vc_ref[...]
        lo = _lo_mask((b, LANES))
        col = lax.broadcasted_iota(jnp.int32, (b, SWA_Q_HEADS), 1)
        lse_t = jnp.zeros((b, SWA_Q_HEADS), jnp.float32)
        hpg = SWA_Q_HEADS // SWA_KV_HEADS
        for g in range(SWA_KV_HEADS):
            heads = range(hpg * g, hpg * (g + 1))
            kvmask = lo if g == 0 else jnp.logical_not(lo)
            qs = []
            for j in range(hpg // 2 * g, hpg // 2 * (g + 1)):
                qn, _, _ = _norm_fwd(q_ref[:, LANES * j: LANES * (j + 1)], gq_ref[...], half=True)
                qn = qn * scale
                qsw = pltpu.roll(qn, 64, 1)
                qs += [jnp.where(kvmask, qn if e == g else qsw, 0.0) for e in range(2)]
            q_st = jnp.concatenate(qs, axis=0).astype(MXU)
            sp_st, sc_st = _dot_nt(q_st, kn_p), _dot_nt(q_st, kn_c)
            pus, pls, ls = [], [], []
            for i, h in enumerate(heads):
                rows = slice(b * i, b * (i + 1))
                sc = jnp.where(upper, sp_st[rows], sc_st[rows]) + _alibi_slope(h) * bias
                sk = sink_ref[h]
                m = jnp.maximum(jnp.max(sc, -1, keepdims=True), sk)
                p = jnp.exp(sc - m)
                l = jnp.sum(p, -1, keepdims=True) + jnp.exp(sk - m)
                pus.append(jnp.where(upper, p, 0.0).astype(MXU))
                pls.append(jnp.where(upper, 0.0, p).astype(MXU))
                ls.append(l)
                lse_t = jnp.where(col == h, m + jnp.log(l), lse_t)
            o_st = _dot(jnp.concatenate(pus, axis=0), v_p) + _dot(jnp.concatenate(pls, axis=0), v_c)
            for j in range(hpg // 2 * g, hpg // 2 * (g + 1)):
                halves = []
                for e in range(2):
                    i = 2 * j + e - hpg * g
                    o_h = o_st[b * i: b * (i + 1)] / ls[i]
                    halves.append(o_h if e == g else pltpu.roll(o_h, 64, 1))
                y_ref[:, LANES * j: LANES * (j + 1)] = jnp.where(lo, halves[0], halves[1])
        lse_ref[...] = lse_t

    return _pcall(
        body, name="swa_fwd", grid=(s // b,),
        out_shape=[_sds((s, 1024), jnp.float32), _sds((s, SWA_Q_HEADS), jnp.float32)],
        in_specs=_swa_specs(s),
        out_specs=[pl.BlockSpec((b, 1024), lambda n: (n, 0)), pl.BlockSpec((b, SWA_Q_HEADS), lambda n: (n, 0))],
        sem=("parallel",))(proj, proj, proj, proj, proj, posc, posr, posr, gq, gk, sinks)


def _out_proj(y_a, y_b, y_m, x, w_out, g_ffn):
    s, d = x.shape
    tm = min(2 * ROW_TILE, s)

    def body(ya_ref, yb_ref, ym_ref, x_ref, w_ref, g_ref, h1_ref, fn_ref):
        y = jnp.concatenate([ya_ref[...].astype(MXU), yb_ref[...].astype(MXU), ym_ref[...].astype(MXU)], axis=1)
        h1 = x_ref[...] + _dot(y, w_ref[...])
        h1_ref[...] = h1
        fn, _, _ = _norm_fwd(h1, g_ref[...])
        fn_ref[...] = fn.astype(fn_ref.dtype)

    def row(width):
        return pl.BlockSpec((tm, width), lambda i: (i, 0))

    return _pcall(
        body, name="out_proj", grid=(s // tm,),
        out_shape=[_sds((s, d), jnp.float32), _sds((s, d), MXU)],
        in_specs=[row(1024), row(512), row(512), row(d),
                  pl.BlockSpec(w_out.shape, lambda i: (0, 0), pipeline_mode=pl.Buffered(1)),
                  pl.BlockSpec((1, d), lambda i: (0, 0))],
        out_specs=[row(d), row(d)], sem=("parallel",))(y_a, y_b, y_m, x, w_out, g_ffn)


def _ffn_gu(fn, w_gu):
    s, d = fn.shape
    f = w_gu.shape[2]
    tm = min(2 * FFN_TILE, s)

    def body(fn_ref, w_ref, gu_ref, act_ref):
        x = fn_ref[...]
        g = _dot_nt(x, w_ref[0, 0])
        u = _dot_nt(x, w_ref[0, 1])
        gu_ref[0, 0] = g
        gu_ref[0, 1] = u
        act_ref[0] = (g * jax.nn.sigmoid(g) * u).astype(act_ref.dtype)

    return _pcall(
        body, name="ffn_gate_up", grid=(N_DEV, s // tm),
        out_shape=[_sds((N_DEV, 2, s, f), jnp.float32), _sds((N_DEV, s, f), MXU)],
        in_specs=[pl.BlockSpec((tm, d), lambda j, i: (i, 0)),
                  pl.BlockSpec((1, 2, f, d), lambda j, i: (j, 0, 0, 0))],
        out_specs=[pl.BlockSpec((1, 2, tm, f), lambda j, i: (j, 0, i, 0)),
                   pl.BlockSpec((1, tm, f), lambda j, i: (j, i, 0))],
        sem=("parallel", "parallel"))(fn, w_gu)


def _ffn_down(act, w_d, h1, target):
    _, s, f = act.shape
    d = h1.shape[1]
    tm = min(FFN_TILE, s)

    def body(a_ref, w_ref, h1_ref, t_ref, dout_ref, doutb_ref, loss_ref, acc):
        i, j = pl.program_id(0), pl.program_id(1)
        part = _dot(a_ref[0], w_ref[0]) + _dot(a_ref[1], w_ref[1])

        @pl.when(j == 0)
        def _():
            acc[...] = h1_ref[...] + part

        @pl.when(j > 0)
        def _():
            acc[...] += part

        @pl.when((i == 0) & (j == 0))
        def _():
            loss_ref[...] = jnp.zeros_like(loss_ref)

        @pl.when(j == N_DEV // 2 - 1)
        def _():
            diff = acc[...] - t_ref[...]
            dout_ref[...] = diff / d
            doutb_ref[...] = (diff / d).astype(doutb_ref.dtype)
            loss_ref[...] += 0.5 * jnp.sum(jnp.sum(diff * diff, -1, keepdims=True) / d)

    row = pl.BlockSpec((tm, d), lambda i, j: (i, 0))
    return _pcall(
        body, name="ffn_down", grid=(s // tm, N_DEV // 2),
        out_shape=[_sds((s, d), jnp.float32), _sds((s, d), MXU), _sds((8, LANES), jnp.float32)],
        in_specs=[pl.BlockSpec((2, tm, f), lambda i, j: (j, i, 0)), pl.BlockSpec((2, f, d), lambda i, j: (j, 0, 0)),
                  row, row],
        out_specs=[row, row, pl.BlockSpec((8, LANES), lambda i, j: (0, 0))],
        scratch=[pltpu.VMEM((tm, d), jnp.float32)], sem=("arbitrary", "arbitrary"))(act, w_d, h1, target)


def _ffn_bwd_act(dout, w_d, gu):
    s, d = dout.shape
    f = w_d.shape[1]
    tm = min(2 * FFN_TILE, s)
    ni = s // tm

    def body(do_ref, w_ref, gu_ref, dgu_ref, dw_ref, acc):
        i = pl.program_id(1)
        do = do_ref[...]
        d_act = _dot_nt(do, w_ref[0])
        g, u = gu_ref[0, 0], gu_ref[0, 1]
        sig = jax.nn.sigmoid(g)
        silu = g * sig
        dgu_ref[0, 0] = (d_act * u * (sig * (1.0 + g * (1.0 - sig)))).astype(dgu_ref.dtype)
        dgu_ref[0, 1] = (d_act * silu).astype(dgu_ref.dtype)
        part = _dot_tn(silu * u, do)

        @pl.when(i == 0)
        def _():
            acc[...] = part

        @pl.when(i > 0)
        def _():
            acc[...] += part

        @pl.when(i == ni - 1)
        def _():
            dw_ref[0] = acc[...].astype(dw_ref.dtype)

    return _pcall(
        body, name="ffn_bwd_act", grid=(N_DEV, ni),
        out_shape=[_sds((N_DEV, 2, s, f), MXU), _sds((N_DEV, f, d), WIRE)],
        in_specs=[pl.BlockSpec((tm, d), lambda j, i: (i, 0)), pl.BlockSpec((1, f, d), lambda j, i: (j, 0, 0)),
                  pl.BlockSpec((1, 2, tm, f), lambda j, i: (j, 0, i, 0))],
        out_specs=[pl.BlockSpec((1, 2, tm, f), lambda j, i: (j, 0, i, 0)),
                   pl.BlockSpec((1, f, d), lambda j, i: (j, 0, 0))],
        scratch=[pltpu.VMEM((f, d), jnp.float32)], sem=("parallel", "arbitrary"))(dout, w_d, gu)


def _ffn_dw_gu(fn, dgu):
    s, d = fn.shape
    f = dgu.shape[-1]
    tk = min(4 * FFN_TILE, s)
    nk = s // tk

    def body(fn_ref, dgu_ref, dw_ref, acc):
        k = pl.program_id(2)
        part = _dot_tn(dgu_ref[0, 0], fn_ref[...])

        @pl.when(k == 0)
        def _():
            acc[...] = part

        @pl.when(k > 0)
        def _():
            acc[...] += part

        @pl.when(k == nk - 1)
        def _():
            dw_ref[0, 0] = acc[...].astype(dw_ref.dtype)

    return _pcall(
        body, name="ffn_dw_gate_up", grid=(N_DEV, 2, nk),
        out_shape=_sds((N_DEV, 2, f, d), WIRE),
        in_specs=[pl.BlockSpec((tk, d), lambda j, w, k: (k, 0)),
                  pl.BlockSpec((1, 1, tk, f), lambda j, w, k: (j, w, k, 0))],
        out_specs=pl.BlockSpec((1, 1, f, d), lambda j, w, k: (j, w, 0, 0)),
        scratch=[pltpu.VMEM((f, d), jnp.float32)], sem=("parallel", "parallel", "arbitrary"))(fn, dgu)


def _ffn_dfn(dgu, w_gu, after):
    _, _, s, f = dgu.shape
    d = w_gu.shape[3]
    tm = min(FFN_TILE, s)

    def body(dgu_ref, w_ref, dfn_ref):
        j = pl.program_id(1)
        part = (_dot(dgu_ref[0, 0], w_ref[0, 0]) + _dot(dgu_ref[0, 1], w_ref[0, 1])
                + _dot(dgu_ref[1, 0], w_ref[1, 0]) + _dot(dgu_ref[1, 1], w_ref[1, 1]))

        @pl.when(j == 0)
        def _():
            dfn_ref[...] = part

        @pl.when(j > 0)
        def _():
            dfn_ref[...] += part

    return _pcall(
        body, name="ffn_dfn", grid=(s // tm, N_DEV // 2),
        out_shape=_sds((s, d), jnp.float32),
        in_specs=[pl.BlockSpec((2, 2, tm, f), lambda i, j: (j, 0, i, 0)),
                  pl.BlockSpec((2, 2, f, d), lambda i, j: (j, 0, 0, 0))],
        out_specs=pl.BlockSpec((tm, d), lambda i, j: (i, 0)),
        sem=("parallel", "arbitrary"), after=after)(dgu, w_gu)


def _ffn_norm_bwd(d_fn, dout, h1, g_ffn):
    s, d = h1.shape
    tm = min(2 * ROW_TILE, s)

    def body(dfn_ref, do_ref, h1_ref, g_ref, dh1_ref, dg_ref):
        i = pl.program_id(0)

        @pl.when(i == 0)
        def _():
            dg_ref[...] = jnp.zeros_like(dg_ref)

        _, xn, r = _norm_fwd(h1_ref[...], g_ref[...])
        dx, dg = _norm_bwd(xn, r, g_ref[...], dfn_ref[...])
        dh1_ref[...] = do_ref[...] + dx
        dg_ref[...] += dg

    row = pl.BlockSpec((tm, d), lambda i: (i, 0))
    vec = pl.BlockSpec((1, d), lambda i: (0, 0))
    return _pcall(
        body, name="ffn_norm_bwd", grid=(s // tm,),
        out_shape=[_sds((s, d), jnp.float32), _sds((1, d), jnp.float32)],
        in_specs=[row, row, row, vec], out_specs=[row, vec], sem=("arbitrary",))(d_fn, dout, h1, g_ffn)


def _mem_bwd(proj, g_mq, km, vmm, d_y, y_m, lse):
    s = proj.shape[0]
    ml, hw = km.shape
    tm = min(FFN_TILE, s)
    scale = MEM_DIM ** -0.5

    def body(q_ref, g_ref, k_ref, v_ref, do_ref, y_ref, lse_ref, dq_ref, dk_ref, dv_ref, dg_ref):
        i = pl.program_id(0)

        @pl.when(i == 0)
        def _():
            dk_ref[...] = jnp.zeros_like(dk_ref)
            dv_ref[...] = jnp.zeros_like(dv_ref)
            dg_ref[...] = jnp.zeros_like(dg_ref)

        col = lax.broadcasted_iota(jnp.int32, (tm, MEM_HEADS), 1)
        lse_t = lse_ref[...]
        for h in range(MEM_HEADS):
            sl = slice(MEM_DIM * h, MEM_DIM * (h + 1))
            qn, xn, r = _norm_fwd(q_ref[:, sl], g_ref[...])
            lse_h = jnp.sum(jnp.where(col == h, lse_t, 0.0), -1, keepdims=True)
            p = jnp.exp(_dot_nt(qn, k_ref[:, sl]) * scale - lse_h)
            do = do_ref[:, sl]
            dd = jnp.sum(do * y_ref[:, sl], -1, keepdims=True)
            dp = _dot_nt(do, v_ref[:, sl])
            ds = (p * (dp - dd)).astype(MXU)
            dv_ref[:, sl] += _dot_tn(p, do)
            dk_ref[:, sl] += _dot_tn(ds, qn) * scale
            dx, dg = _norm_bwd(xn, r, g_ref[...], _dot(ds, k_ref[:, sl]) * scale)
            dq_ref[:, sl] = dx.astype(dq_ref.dtype)
            dg_ref[...] += dg

    full = pl.BlockSpec((ml, hw), lambda i: (0, 0))
    return _pcall(
        body, name="mem_bwd", grid=(s // tm,),
        out_shape=[_sds((s, hw), MXU), _sds((ml, hw), jnp.float32), _sds((ml, hw), jnp.float32),
                   _sds((1, MEM_DIM), jnp.float32)],
        in_specs=[pl.BlockSpec((tm, hw), lambda i: (i, C_QM // hw)), pl.BlockSpec((1, MEM_DIM), lambda i: (0, 0)),
                  full, full, pl.BlockSpec((tm, hw), lambda i: (i, 3)), pl.BlockSpec((tm, hw), lambda i: (i, 0)),
                  pl.BlockSpec((tm, MEM_HEADS), lambda i: (i, 0))],
        out_specs=[pl.BlockSpec((tm, hw), lambda i: (i, 0)), full, full,
                   pl.BlockSpec((1, MEM_DIM), lambda i: (0, 0))],
        sem=("arbitrary",))(proj, g_mq, km, vmm, d_y, y_m, lse)


def _memkv_bwd(mem, g_mem, w_mkv, g_mk, kv, memn, dk, dv):
    ml, d = mem.shape
    hw = MEM_HEADS * MEM_DIM

    def body(mem_ref, g_ref, w_ref, gk_ref, kv_ref, mn_ref, dk_ref, dv_ref, dw_ref, dgm_ref, dgk_ref):
        parts = []
        dgk = jnp.zeros((1, MEM_DIM), jnp.float32)
        for h in range(MEM_HEADS):
            sl = slice(MEM_DIM * h, MEM_DIM * (h + 1))
            _, xn, r = _norm_fwd(kv_ref[:, sl], gk_ref[...])
            dx, dg = _norm_bwd(xn, r, gk_ref[...], dk_ref[:, sl])
            parts.append(dx)
            dgk = dgk + dg
        dkv = jnp.concatenate(parts + [dv_ref[...]], axis=1).astype(MXU)
        dgk_ref[...] = dgk
        dw_ref[...] = _dot_tn(mn_ref[...], dkv).astype(dw_ref.dtype)
        d_mn = _dot_nt(dkv, w_ref[...])
        _, xn, _ = _norm_fwd(mem_ref[...], g_ref[...])
        dgm_ref[...] = jnp.sum(d_mn * xn, 0, keepdims=True)

    vm = pl.BlockSpec(memory_space=pltpu.VMEM)
    return _pcall(
        body, name="memkv_bwd",
        out_shape=[_sds((d, 2 * hw), WIRE), _sds((1, d), jnp.float32), _sds((1, MEM_DIM), jnp.float32)],
        in_specs=[vm] * 8, out_specs=[vm] * 3)(mem, g_mem, w_mkv, g_mk, kv, memn, dk, dv)


def _mla_bwd(qc, kc, v, d_y, y_b, lse, after):
    nh, s, _ = qc.shape
    t = min(ATT_TILE, s)
    nb = s // t
    scale = (MLA_NOPE + MLA_ROPE) ** -0.5

    def body(q_ref, k_ref, v_ref, do_ref, y_ref, lse_ref, dq_ref, dk_ref, dv_ref, dk_acc, dv_acc):
        kj, qi = pl.program_id(1), pl.program_id(2)

        @pl.when((kj == 0) & (qi == 0))
        def _():
            dq_ref[...] = jnp.zeros_like(dq_ref)

        @pl.when(qi == kj)
        def _():
            dk_acc[...] = jnp.zeros_like(dk_acc)
            dv_acc[...] = jnp.zeros_like(dv_acc)

        def step(diagonal):
            rc = t // 4 if diagonal else t
            for c in range(t // rc):
                rows = slice(rc * c, rc * (c + 1))
                keys = slice(0, rc * (c + 1))
                q, k = q_ref[0, rows, :], k_ref[0, keys, :]
                sc = _dot_nt(q, k) * (scale * LOG2E)
                if diagonal:
                    r_i = lax.broadcasted_iota(jnp.int32, sc.shape, 0) + rc * c
                    c_i = lax.broadcasted_iota(jnp.int32, sc.shape, 1)
                    sc = jnp.where(c_i <= r_i, sc, NEG_INF)
                p = jnp.exp2(sc - lse_ref[0, rows, :])
                do = do_ref[rows, :]
                dd = jnp.sum(do * y_ref[rows, :], -1, keepdims=True)
                dp = _dot_nt(do, v_ref[0, keys, :])
                ds = (p * (dp - dd) * scale).astype(MXU)
                dv_acc[keys, :] += _dot_tn(p, do)
                dk_acc[keys, :] += _dot_tn(ds, q)
                out_rows = pl.ds(pl.multiple_of(qi * t + rc * c, rc), rc)
                dq_ref[0, out_rows, :] += _dot(ds, k)

        @pl.when(qi > kj)
        def _():
            step(False)

        @pl.when(qi == kj)
        def _():
            step(True)

        @pl.when(qi == nb - 1)
        def _():
            dk_ref[0] = dk_acc[...]
            dv_ref[0] = dv_acc[...]

    qmap = lambda h, j, i: (h, jnp.maximum(i, j), 0)
    return _pcall(
        body, name="mla_bwd", grid=(nh, nb, nb),
        out_shape=[_sds((nh, s, 256), jnp.float32), _sds((nh, s, 256), jnp.float32),
                   _sds((nh, s, MLA_V), jnp.float32)],
        in_specs=[pl.BlockSpec((1, t, 256), qmap),
                  pl.BlockSpec((1, t, 256), lambda h, j, i: (h, j, 0)),
                  pl.BlockSpec((1, t, MLA_V), lambda h, j, i: (h, j, 0)),
                  pl.BlockSpec((t, MLA_V), lambda h, j, i: (jnp.maximum(i, j), 8 + h)),
                  pl.BlockSpec((t, MLA_V), lambda h, j, i: (jnp.maximum(i, j), h)),
                  pl.BlockSpec((1, t, 1), qmap)],
        out_specs=[pl.BlockSpec((1, s, 256), lambda h, j, i: (h, 0, 0)),
                   pl.BlockSpec((1, t, 256), lambda h, j, i: (h, j, 0)),
                   pl.BlockSpec((1, t, MLA_V), lambda h, j, i: (h, j, 0))],
        scratch=[pltpu.VMEM((t, 256), jnp.float32), pltpu.VMEM((t, MLA_V), jnp.float32)],
        sem=("parallel", "arbitrary", "arbitrary"), after=after)(qc, kc, v, d_y, y_b, lse)


def _mla_prep_bwd(proj, cos, sin, g_cq, g_ckv, w_uq, w_ukv, g_qn, g_qr, g_kn, g_kr,
                  qb, kvb, cqn, ckvn, dqc, dkc, dv):
    s = proj.shape[0]
    tm = min(ROW_TILE, s)
    nh = MLA_HEADS
    ni = s // tm

    def body(cq_ref, ckv_ref, kr_ref, cos_ref, sin_ref, gcq_ref, gckv_ref, wuq_ref, wukv_ref,
             gqn_ref, gqr_ref, gkn_ref, gkr_ref, qb_ref, kvb_ref, cqn_ref, ckvn_ref, dqc_ref, dkc_ref, dv_ref,
             dcq_ref, dckv_ref, dkr_ref, dwuq_ref, dwukv_ref,
             dgcq_ref, dgckv_ref, dgqn_ref, dgqr_ref, dgkn_ref, dgkr_ref, acc_uq, acc_ukv):
        i = pl.program_id(0)

        @pl.when(i == 0)
        def _():
            acc_uq[...] = jnp.zeros_like(acc_uq)
            acc_ukv[...] = jnp.zeros_like(acc_ukv)
            for ref in (dgcq_ref, dgckv_ref, dgqn_ref, dgqr_ref, dgkn_ref, dgkr_ref):
                ref[...] = jnp.zeros_like(ref)

        cos_t, sin_t = cos_ref[...], sin_ref[...]
        lo = _lo_mask((tm, LANES))
        qb_v, kvb_v = qb_ref[...], kvb_ref[...]
        dq_parts, dgqn = [], jnp.zeros((1, LANES), jnp.float32)
        for h in range(nh):
            _, xn, r = _norm_fwd(qb_v[:, MLA_NOPE * h: MLA_NOPE * (h + 1)], gqn_ref[...])
            dx, dg = _norm_bwd(xn, r, gqn_ref[...], dqc_ref[h][:, :MLA_NOPE])
            dq_parts.append(dx)
            dgqn = dgqn + dg
        dgqn_ref[...] += dgqn
        dgqr = jnp.zeros((1, LANES), jnp.float32)
        for j in range(nh // 2):
            d_rope = jnp.where(lo, dqc_ref[2 * j][:, MLA_NOPE:], dqc_ref[2 * j + 1][:, MLA_NOPE:])
            d_pre = _rope_bwd(d_rope, cos_t, sin_t)
            xr = qb_v[:, nh * MLA_NOPE + LANES * j: nh * MLA_NOPE + LANES * (j + 1)]
            _, xn, r = _norm_fwd(xr, gqr_ref[...], half=True)
            dx, dg = _norm_bwd(xn, r, gqr_ref[...], d_pre, half=True)
            dq_parts.append(dx)
            dgqr = dgqr + dg
        dgqr_ref[...] += dgqr
        dqb = jnp.concatenate(dq_parts, axis=1).astype(MXU)
        acc_uq[...] += _dot_tn(dqb, cqn_ref[...])
        _, xn, r = _norm_fwd(cq_ref[...], gcq_ref[...])
        dx, dg = _norm_bwd(xn, r, gcq_ref[...], _dot(dqb, wuq_ref[...]))
        dcq_ref[...] = dx.astype(dcq_ref.dtype)
        dgcq_ref[...] += dg
        dkv_parts, dgkn = [], jnp.zeros((1, LANES), jnp.float32)
        d_kr2 = jnp.zeros((tm, LANES), jnp.float32)
        for h in range(nh):
            _, xn, r = _norm_fwd(kvb_v[:, 256 * h: 256 * h + MLA_NOPE], gkn_ref[...])
            dx, dg = _norm_bwd(xn, r, gkn_ref[...], dkc_ref[h][:, :MLA_NOPE])
            dkv_parts += [dx, dv_ref[h]]
            dgkn = dgkn + dg
            d_kr2 = d_kr2 + dkc_ref[h][:, MLA_NOPE:]
        dgkn_ref[...] += dgkn
        dkvb = jnp.concatenate(dkv_parts, axis=1).astype(MXU)
        part_ukv = _dot_tn(ckvn_ref[...], dkvb)
        for dev in range(N_DEV):
            acc_ukv[dev] += part_ukv[:, LANES * dev: LANES * (dev + 1)]
        w_ukv_full = jnp.concatenate([wukv_ref[dev] for dev in range(N_DEV)], axis=1)
        d_ckvn = _dot_nt(dkvb, w_ukv_full)
        _, xn, r = _norm_fwd(ckv_ref[...], gckv_ref[...])
        dx, dg = _norm_bwd(xn, r, gckv_ref[...], d_ckvn)
        dckv_ref[...] = dx.astype(dckv_ref.dtype)
        dgckv_ref[...] += dg
        d_kr = jnp.where(lo, d_kr2 + pltpu.roll(d_kr2, 64, 1), 0.0)
        d_pre = _rope_bwd(d_kr, cos_t, sin_t)
        _, xn, r = _norm_fwd(kr_ref[...], gkr_ref[...], half=True)
        dx, dg = _norm_bwd(xn, r, gkr_ref[...], d_pre, half=True)
        dkr_ref[...] = jnp.where(lo, dx, 0.0).astype(dkr_ref.dtype)
        dgkr_ref[...] += jnp.where(_lo_mask((1, LANES)), dg, 0.0)

        @pl.when(i == ni - 1)
        def _():
            dwuq_ref[...] = acc_uq[...].astype(dwuq_ref.dtype)
            dwukv_ref[...] = acc_ukv[...].astype(dwukv_ref.dtype)

    def col(width, start):
        return pl.BlockSpec((tm, width), lambda i: (i, start // width))

    def full(shape):
        return pl.BlockSpec(shape, lambda i: (0,) * len(shape))

    def row(width):
        return pl.BlockSpec((tm, width), lambda i: (i, 0))

    def heads(width):
        return pl.BlockSpec((nh, tm, width), lambda i: (0, i, 0))

    vec = full((1, LANES))
    return _pcall(
        body, name="mla_prep_bwd", grid=(ni,),
        out_shape=[_sds((s, 512), MXU), _sds((s, 512), MXU), _sds((s, LANES), MXU),
                   _sds((768, 512), WIRE), _sds((N_DEV, 512, LANES), WIRE),
                   _sds((1, 512), jnp.float32), _sds((1, 512), jnp.float32)] + [_sds((1, LANES), jnp.float32)] * 4,
        in_specs=[col(512, C_CQ), col(512, C_CKV), col(LANES, C_KR), row(LANES), row(LANES),
                  full((1, 512)), full((1, 512)), full((768, 512)), full((N_DEV, 512, LANES)), vec, vec, vec, vec,
                  row(768), row(1024), row(512), row(512), heads(256), heads(256), heads(MLA_V)],
        out_specs=[row(512), row(512), row(LANES), full((768, 512)), full((N_DEV, 512, LANES)),
                   full((1, 512)), full((1, 512)), vec, vec, vec, vec],
        scratch=[pltpu.VMEM((768, 512), jnp.float32), pltpu.VMEM((N_DEV, 512, LANES), jnp.float32)],
        sem=("arbitrary",))(proj, proj, proj, cos, sin, g_cq, g_ckv, w_uq, w_ukv, g_qn, g_qr, g_kn, g_kr,
                            qb, kvb, cqn, ckvn, dqc, dkc, dv)


def _swa_bwd(proj, posc, posr, gq, gk, sinks, d_y, y_a, lse, after):
    s = proj.shape[0]
    b = SWA_BLOCK
    nb = s // b
    scale = SWA_DIM ** -0.5

    def body(q_ref, kp_ref, kc_ref, vp_ref, vc_ref, pq_ref, pkp_ref, pkc_ref, gq_ref, gk_ref, sink_ref,
             do_ref, y_ref, lse_ref, kfull_ref,
             dq_ref, dk_ref, dv_ref, dgq_ref, dgk_ref, dsink_ref, dk_acc, dv_acc):
        n = pl.program_id(0)

        @pl.when(n == 0)
        def _():
            dk_acc[...] = jnp.zeros_like(dk_acc)
            dv_acc[...] = jnp.zeros_like(dv_acc)
            dgq_ref[...] = jnp.zeros_like(dgq_ref)
            dsink_ref[...] = jnp.zeros_like(dsink_ref)

        kn, v, bias = _swa_common(n, kp_ref[...], kc_ref[...], vp_ref[...], vc_ref[...],
                                  pq_ref[...], pkp_ref[...], pkc_ref[...], gk_ref[...])
        lo = _lo_mask((b, LANES))
        col = lax.broadcasted_iota(jnp.int32, (b, SWA_Q_HEADS), 1)
        col1 = lax.broadcasted_iota(jnp.int32, (1, SWA_Q_HEADS), 1)
        lse_t = lse_ref[...]
        dk_blk = jnp.zeros((2 * b, LANES), jnp.float32)
        dv_blk = jnp.zeros((2 * b, LANES), jnp.float32)
        dgq = jnp.zeros((1, LANES), jnp.float32)
        dsink = jnp.zeros((1, SWA_Q_HEADS), jnp.float32)
        for j in range(SWA_Q_HEADS // 2):
            hk = (2 * j) // (SWA_Q_HEADS // SWA_KV_HEADS)
            kvmask = lo if hk == 0 else jnp.logical_not(lo)
            sl = slice(LANES * j, LANES * (j + 1))
            qn, xn, r = _norm_fwd(q_ref[:, sl], gq_ref[...], half=True)
            qn = qn * scale
            qsw = pltpu.roll(qn, 64, 1)
            d2 = do_ref[:, sl]
            d2sw = pltpu.roll(d2, 64, 1)
            prod = d2 * y_ref[:, sl]
            dqs = []
            for e in range(2):
                h = 2 * j + e
                half_e = lo if e == 0 else jnp.logical_not(lo)
                qm = jnp.where(kvmask, qn if e == hk else qsw, 0.0)
                dm = jnp.where(kvmask, d2 if e == hk else d2sw, 0.0)
                sc = _dot_nt(qm, kn) + _alibi_slope(h) * bias
                lse_h = jnp.sum(jnp.where(col == h, lse_t, 0.0), -1, keepdims=True)
                p = jnp.exp(sc - lse_h)
                dd = jnp.sum(jnp.where(half_e, prod, 0.0), -1, keepdims=True)
                dp = _dot_nt(dm, v)
                ds = (p * (dp - dd)).astype(MXU)
                dsink = dsink - jnp.where(col1 == h, jnp.sum(jnp.exp(sink_ref[h] - lse_h) * dd), 0.0)
                dq_m = _dot(ds, kn) * scale
                dk_blk = dk_blk + _dot_tn(ds, qm)
                dv_blk = dv_blk + _dot_tn(p, dm)
                dqs.append(dq_m if e == hk else pltpu.roll(dq_m, 64, 1))
            dx, dg = _norm_bwd(xn, r, gq_ref[...], jnp.where(lo, dqs[0], dqs[1]), half=True)
            dq_ref[:, sl] = dx.astype(dq_ref.dtype)
            dgq = dgq + dg
        dgq_ref[...] += dgq
        dsink_ref[...] += dsink
        prev = pl.ds(pl.multiple_of(jnp.maximum(n - 1, 0) * b, b), b)
        cur = pl.ds(pl.multiple_of(n * b, b), b)
        dk_acc[prev, :] += dk_blk[:b]
        dv_acc[prev, :] += dv_blk[:b]
        dk_acc[cur, :] += dk_blk[b:]
        dv_acc[cur, :] += dv_blk[b:]

        @pl.when(n == nb - 1)
        def _():
            _, kxn, kr = _norm_fwd(kfull_ref[...], gk_ref[...], half=True)
            dx, dg = _norm_bwd(kxn, kr, gk_ref[...], dk_acc[...], half=True)
            dk_ref[...] = dx.astype(dk_ref.dtype)
            dv_ref[...] = dv_acc[...].astype(dv_ref.dtype)
            dgk_ref[...] = dg

    full = pl.BlockSpec((s, LANES), lambda n: (0, 0))
    vec = pl.BlockSpec((1, LANES), lambda n: (0, 0))
    return _pcall(
        body, name="swa_bwd", grid=(nb,),
        out_shape=[_sds((s, 1024), MXU), _sds((s, LANES), MXU), _sds((s, LANES), MXU),
                   _sds((1, LANES), jnp.float32), _sds((1, LANES), jnp.float32),
                   _sds((1, SWA_Q_HEADS), jnp.float32)],
        in_specs=_swa_specs(s) + [pl.BlockSpec((b, 1024), lambda n: (n, 0)), pl.BlockSpec((b, 1024), lambda n: (n, 0)),
                                  pl.BlockSpec((b, SWA_Q_HEADS), lambda n: (n, 0)),
                                  pl.BlockSpec((s, LANES), lambda n: (0, C_KA // LANES))],
        out_specs=[pl.BlockSpec((b, 1024), lambda n: (n, 0)), full, full, vec, vec,
                   pl.BlockSpec((1, SWA_Q_HEADS), lambda n: (0, 0))],
        scratch=[pltpu.VMEM((s, LANES), jnp.float32), pltpu.VMEM((s, LANES), jnp.float32)],
        sem=("arbitrary",), after=after)(proj, proj, proj, proj, proj, posc, posr, posr, gq, gk, sinks, d_y, y_a, lse,
                                         proj)


def _dx(d_proj, w_in, x, g, d_h1, after):
    s, d = x.shape
    n = w_in.shape[0]
    tm = min(2 * ROW_TILE, s)

    n_pc = len(d_proj)

    def body(*refs):
        dp_refs, (w_ref, x_ref, g_ref, dh_ref, dx_ref, dg_ref) = refs[:n_pc], refs[n_pc:]
        i = pl.program_id(0)

        @pl.when(i == 0)
        def _():
            dg_ref[...] = jnp.zeros_like(dg_ref)

        d_hn = _dot(jnp.concatenate([r[...] for r in dp_refs], axis=1), w_ref[...])
        _, xn, r = _norm_fwd(x_ref[...], g_ref[...])
        dx, dg = _norm_bwd(xn, r, g_ref[...], d_hn)
        dx_ref[...] = dh_ref[...] + dx
        dg_ref[...] += dg

    row = pl.BlockSpec((tm, d), lambda i: (i, 0))
    vec = pl.BlockSpec((1, d), lambda i: (0, 0))
    return _pcall(
        body, name="grad_x", grid=(s // tm,),
        out_shape=[_sds((s, d), jnp.float32), _sds((1, d), jnp.float32)],
        in_specs=[pl.BlockSpec((tm, p.shape[1]), lambda i: (i, 0)) for p in d_proj] + [
                  pl.BlockSpec((n, d), lambda i: (0, 0), pipeline_mode=pl.Buffered(1)), row, vec, row],
        out_specs=[row, vec], sem=("arbitrary",), after=after)(*d_proj, w_in, x, g, d_h1)


_SMALL = ["attn_norm_g", "swa_q_norm_g", "swa_k_norm_g", "swa_sinks", "mla_cq_norm_g", "mla_ckv_norm_g",
          "mla_qn_norm_g", "mla_qr_norm_g", "mla_kn_norm_g", "mla_kr_norm_g", "mem_norm_g",
          "mem_q_norm_g", "mem_k_norm_g", "ffn_norm_g"]


def kernel(x, mem, positions, attn_norm_g, w_in, swa_q_norm_g, swa_k_norm_g, swa_sinks, mla_cq_norm_g, mla_ckv_norm_g, w_uq, w_ukv, mla_qn_norm_g, mla_qr_norm_g, mla_kn_norm_g, mla_kr_norm_g, mem_norm_g, w_mem_kv, mem_q_norm_g, mem_k_norm_g, w_out, ffn_norm_g, w_gate, w_up, w_down, loss_target, m_attn_norm_g, m_w_in, m_swa_q_norm_g, m_swa_k_norm_g, m_swa_sinks, m_mla_cq_norm_g, m_mla_ckv_norm_g, m_w_uq, m_w_ukv, m_mla_qn_norm_g, m_mla_qr_norm_g, m_mla_kn_norm_g, m_mla_kr_norm_g, m_mem_norm_g, m_w_mem_kv, m_mem_q_norm_g, m_mem_k_norm_g, m_w_out, m_ffn_norm_g, m_w_gate, m_w_up, m_w_down, v_attn_norm_g, v_w_in, v_swa_q_norm_g, v_swa_k_norm_g, v_swa_sinks, v_mla_cq_norm_g, v_mla_ckv_norm_g, v_w_uq, v_w_ukv, v_mla_qn_norm_g, v_mla_qr_norm_g, v_mla_kn_norm_g, v_mla_kr_norm_g, v_mem_norm_g, v_w_mem_kv, v_mem_q_norm_g, v_mem_k_norm_g, v_w_out, v_ffn_norm_g, v_w_gate, v_w_up, v_w_down):
    args = dict(locals())
    x2, mem2, tgt = x[0], mem[0], loss_target[0]
    s, d = x2.shape
    n_in = w_in.shape[2]
    f = w_gate.shape[2]

    in_shards = [w_in[0].T.astype(WIRE)]
    (g_in,) = _all_gather_background(in_shards, 7, "all_gather_in_weights")
    tok = in_shards[0]
    mix_shards = [w_uq[0].T.astype(WIRE), w_ukv[0].astype(WIRE), w_mem_kv[0].astype(WIRE),
                  _to_wire([w_out[0]], tok, "wire_out")[0]]
    g_uq, wkv, g_mkv, g_out = _all_gather_background(mix_shards, 5, "all_gather_mix_weights")
    ffn_shards = [_to_wire([w_gate[0].T, w_up[0].T], tok, "wire_gate_up")]
    (w_gu,) = _all_gather_background(ffn_shards, 1, "all_gather_ffn_weights")
    down_shards = [_to_wire([w_down[0]], tok, "wire_down")[0]]
    (w_d,) = _all_gather_background(down_shards, 6, "all_gather_down_weights")
    wi = g_in.reshape(N_DEV * n_in, d)
    wi = jnp.concatenate([wi[0:1024], wi[1280:1792], wi[1792:2304], wi[2368:2880],
                          wi[1024:1152], wi[1152:1280], wi[2304:2368],
                          jnp.zeros((IN_PAD - 2880, d), wi.dtype)], axis=0)
    wq = g_uq.reshape(768, 512)
    wq = jnp.concatenate([wq[192 * h: 192 * h + 128] for h in range(4)]
                         + [wq[192 * h + 128: 192 * (h + 1)] for h in range(4)], axis=0)
    wmkv = g_mkv.reshape(-1, g_mkv.shape[-1])
    wo = g_out.reshape(-1, d)

    pos = positions[0].astype(jnp.float32)
    inv_freq = ROPE_THETA ** (-jnp.arange(0, MLA_ROPE, 2, dtype=jnp.float32) / MLA_ROPE)
    ang = pos[:, None] * inv_freq
    cos32, sin32 = jnp.cos(ang), jnp.sin(ang)
    cos_t = jnp.tile(cos32, (1, 4))
    sin_t = jnp.tile(jnp.concatenate([-sin32, sin32], axis=1), (1, 2))
    posc, posr = pos.reshape(s, 1), pos.reshape(1, s)
    two = lambda g: jnp.tile(g, (1, 2))
    gq2, gk2, gqr2, gkr2 = two(swa_q_norm_g), two(swa_k_norm_g), two(mla_qr_norm_g), two(mla_kr_norm_g)
    sinks1 = swa_sinks[0]

    hn = _norm_rows(x2, attn_norm_g)
    proj = _mm(hn, wi, tb=True, out_dtype=jnp.float32, tm=FFN_TILE, tk=d, name="in_proj")
    qc, kc, vb, qb, kvb, cqn, ckvn = _mla_prep(proj, cos_t, sin_t, mla_cq_norm_g, mla_ckv_norm_g, wq, wkv,
                                                mla_qn_norm_g, gqr2, mla_kn_norm_g, gkr2)
    y_b, lse_b = _mla_fwd(qc, kc, vb)
    km, vmm, kvm, memn = _memkv_prep(mem2, mem_norm_g, wmkv, mem_k_norm_g)
    y_m, lse_m = _mem_fwd(proj, mem_q_norm_g, km, vmm)
    y_a, lse_a = _swa_fwd(proj, posc, posr, gq2, gk2, sinks1)
    h1, fn = _out_proj(y_a, y_b, y_m, x2, wo, ffn_norm_g)
    gu, act = _ffn_gu(fn, w_gu)
    dout, dout_b, loss_tile = _ffn_down(act, w_d, h1, tgt)

    dgu, dw_d = _ffn_bwd_act(dout_b, w_d, gu)
    dw_gu = _ffn_dw_gu(fn, dgu)
    r_gu, r_d = _exchange_grads_background([dw_gu, dw_d], 2, "exchange_ffn_grads")
    d_h1, dg_ffn = _ffn_norm_bwd(_ffn_dfn(dgu, w_gu, dw_gu), dout, h1, ffn_norm_g)
    d_y = _mm(d_h1, wo, tb=True, out_dtype=jnp.float32, tm=FFN_TILE, tk=2048, name="d_mix")
    dw_out = jnp.concatenate([
        _mm(y_a, d_h1, ta=True, out_dtype=WIRE, tm=1024, tk=1024, name="dw_out_a"),
        _mm(y_b, d_h1, ta=True, out_dtype=WIRE, tm=1024, tk=1024, name="dw_out_b"),
        _mm(y_m, d_h1, ta=True, out_dtype=WIRE, tm=1024, tk=1024, name="dw_out_m")], axis=0)
    d_qm, dkm, dvmm, dg_mq = _mem_bwd(proj, mem_q_norm_g, km, vmm, d_y, y_m, lse_m)
    dw_mkv, dg_mem, dg_mk = _memkv_bwd(mem2, mem_norm_g, wmkv, mem_k_norm_g, kvm, memn, dkm, dvmm)
    r_mkv, r_out = _exchange_grads_background([dw_mkv.reshape(g_mkv.shape), dw_out.reshape(g_out.shape)], 3,
                                              "exchange_mix_grads")
    dqc, dkc, dvb = _mla_bwd(qc, kc, vb, d_y, y_b, lse_b, dw_mkv)
    (d_cq, d_ckv, d_kr, dw_uq, dw_ukv, dg_cq, dg_ckv, dg_qn, dg_qr, dg_kn, dg_kr) = _mla_prep_bwd(
        proj, cos_t, sin_t, mla_cq_norm_g, mla_ckv_norm_g, wq, wkv, mla_qn_norm_g, gqr2, mla_kn_norm_g, gkr2,
        qb, kvb, cqn, ckvn, dqc, dkc, dvb)
    d_qa, d_ka, d_va, dg_q, dg_k, d_sinks = _swa_bwd(proj, posc, posr, gq2, gk2, sinks1, d_y, y_a, lse_a, dw_out)
    d_proj = [d_qa, d_cq, d_ckv, d_qm, d_ka, d_va, d_kr]
    gi = _dw_in(hn, d_proj, n_in)

    gq_ = jnp.concatenate(sum([[dw_uq[128 * h: 128 * (h + 1)], dw_uq[512 + 64 * h: 512 + 64 * (h + 1)]]
                               for h in range(4)], []), axis=0)
    gq_ = gq_.reshape(N_DEV, 96, 512)
    r_in, r_uq, r_ukv = _exchange_grads_background([gi, gq_, dw_ukv], 4, "exchange_in_grads")
    grad_x, dg_attn = _dx(d_proj, wi, x2, attn_norm_g, d_h1, gi)

    big = {}
    last = [None]

    def adam(name, r, transposed=False, which=None):
        w, m, v = args[name][0], args["m_" + name][0], args["v_" + name][0]
        if transposed:
            outs = _adam_big(r, w.T, m.T, v.T, "adam_" + name, last[0], which)
            big[name] = [o.T[None] for o in outs]
        else:
            outs = _adam_big(r, w, m, v, "adam_" + name, last[0])
            big[name] = [o[None] for o in outs]
        last[0] = outs[0]

    adam("w_gate", r_gu, True, which=0)
    adam("w_up", r_gu, True, which=1)
    adam("w_down", r_d)
    adam("w_out", r_out)
    adam("w_mem_kv", r_mkv)
    adam("w_in", r_in, True)
    adam("w_uq", r_uq, True)
    adam("w_ukv", r_ukv)

    small_g = {
        "attn_norm_g": dg_attn, "swa_q_norm_g": dg_q, "swa_k_norm_g": dg_k,
        "swa_sinks": d_sinks, "mla_cq_norm_g": dg_cq, "mla_ckv_norm_g": dg_ckv, "mla_qn_norm_g": dg_qn,
        "mla_qr_norm_g": dg_qr, "mla_kn_norm_g": dg_kn, "mla_kr_norm_g": dg_kr,
        "mem_norm_g": dg_mem, "mem_q_norm_g": dg_mq, "mem_k_norm_g": dg_mk, "ffn_norm_g": dg_ffn}
    packed_g = _small_allreduce([small_g[n] for n in _SMALL], loss_tile, [args[n].shape[-1] for n in _SMALL])
    loss11, small_out = _small_adam(packed_g, [args[n] for n in _SMALL],
                                    [args["m_" + n] for n in _SMALL], [args["v_" + n] for n in _SMALL])
    small = dict(zip(_SMALL, small_out))
    loss = loss11.reshape(())

    order = ["attn_norm_g", "w_in", "swa_q_norm_g", "swa_k_norm_g", "swa_sinks", "mla_cq_norm_g", "mla_ckv_norm_g",
             "w_uq", "w_ukv", "mla_qn_norm_g", "mla_qr_norm_g", "mla_kn_norm_g", "mla_kr_norm_g", "mem_norm_g",
             "w_mem_kv", "mem_q_norm_g", "mem_k_norm_g", "w_out", "ffn_norm_g", "w_gate", "w_up", "w_down"]
    res = {n: (big[n] if n in big else list(small[n])) for n in order}
    outs = [loss, grad_x[None]]
    for kind in range(4):
        outs += [res[n][kind] for n in order]
    return tuple(outs)
```

```python
import jax
import jax.numpy as jnp
from jax import lax
from jax.experimental import pallas as pl
from jax.experimental.pallas import tpu as pltpu
from jax.experimental.pallas import tpu_sc as plsc

MXU = jnp.bfloat16
WIRE = jnp.bfloat16
EPS = 1e-6
NEG_INF = -1e30
LOG2E = 1.4426950408889634
N_DEV = 8
LANES = 128
ROW_TILE = 256
FFN_TILE = 512
ATT_TILE = 1024
SWA_BLOCK = 128
VMEM_LIMIT = 56 * 1024 * 1024

SWA_Q_HEADS, SWA_KV_HEADS, SWA_DIM = 16, 2, 64
MLA_HEADS, MLA_NOPE, MLA_ROPE, MLA_V = 4, 128, 64, 128
MEM_HEADS, MEM_DIM = 4, 128
ROPE_THETA = 10000.0
ADAM_LR, ADAM_B1, ADAM_B2, ADAM_EPS, ADAM_WD, ADAM_STEP = 0.001, 0.9, 0.999, 1e-08, 0.01, 10

C_QA, C_CQ, C_CKV, C_QM, C_KA, C_VA, C_KR, IN_PAD = 0, 1024, 1536, 2048, 2560, 2688, 2816, 2944


def _pcall(body, *, name, out_shape, in_specs, out_specs, grid=(), scratch=(), sem=None, after=None):
    params = pltpu.CompilerParams(dimension_semantics=sem, vmem_limit_bytes=VMEM_LIMIT)
    if after is not None:
        n_in, inner = len(in_specs), body

        def body(*refs):
            inner(*refs[:n_in], *refs[n_in + 1:])

        in_specs = list(in_specs) + [pl.BlockSpec(memory_space=pl.ANY)]
    call = pl.pallas_call(body, name=name, grid=grid, in_specs=in_specs, out_specs=out_specs,
                          out_shape=out_shape, scratch_shapes=list(scratch), compiler_params=params)
    return call if after is None else (lambda *ops: call(*ops, after))


def _sds(shape, dtype):
    return jax.ShapeDtypeStruct(tuple(shape), dtype)


def _dot(a, b):
    return jnp.dot(a.astype(MXU), b.astype(MXU), preferred_element_type=jnp.float32)


def _dot_nt(a, b):
    return lax.dot_general(a.astype(MXU), b.astype(MXU), (((1,), (1,)), ((), ())),
                           preferred_element_type=jnp.float32)


def _dot_tn(a, b):
    return lax.dot_general(a.astype(MXU), b.astype(MXU), (((0,), (0,)), ((), ())),
                           preferred_element_type=jnp.float32)


def _lo_mask(shape):
    return (lax.broadcasted_iota(jnp.int32, shape, len(shape) - 1) % LANES) < 64


def _norm_fwd(x, g, half=False):
    x2 = x * x
    if half:
        lo = _lo_mask(x.shape)
        s_lo = jnp.sum(jnp.where(lo, x2, 0.0), -1, keepdims=True)
        s_hi = jnp.sum(jnp.where(lo, 0.0, x2), -1, keepdims=True)
        r = jnp.where(lo, lax.rsqrt(s_lo / 64.0 + EPS), lax.rsqrt(s_hi / 64.0 + EPS))
    else:
        r = lax.rsqrt(jnp.mean(x2, -1, keepdims=True) + EPS)
    xn = x * r
    return xn * g, xn, r


def _norm_bwd(xn, r, g, dy, half=False):
    t = dy * g
    tx = t * xn
    if half:
        lo = _lo_mask(xn.shape)
        m_lo = jnp.sum(jnp.where(lo, tx, 0.0), -1, keepdims=True) / 64.0
        m_hi = jnp.sum(jnp.where(lo, 0.0, tx), -1, keepdims=True) / 64.0
        m = jnp.where(lo, m_lo, m_hi)
    else:
        m = jnp.mean(tx, -1, keepdims=True)
    dx = r * (t - xn * m)
    dg = jnp.sum(dy * xn, 0, keepdims=True)
    return dx, dg


def _swap32(x):
    lane = lax.broadcasted_iota(jnp.int32, x.shape, 1)
    return jnp.where((lane % 64) < 32, pltpu.roll(x, 96, 1), pltpu.roll(x, 32, 1))


def _rope(x, cos, sin):
    return x * cos + _swap32(x) * sin


def _rope_bwd(d, cos, sin):
    return d * cos + _swap32(d * sin)


def _my_coords():
    return lax.axis_index("x"), lax.axis_index("y"), lax.axis_index("c")


def _dev_index(px, py, pc):
    return 4 * px + 2 * py + pc


_FLIPS = [(0, 0, 1), (0, 1, 0), (0, 1, 1), (1, 0, 0), (1, 0, 1), (1, 1, 0), (1, 1, 1)]


def _flip(coords, f):
    return tuple((1 - v) if b else v for v, b in zip(coords, f))


def _all_gather(shards):
    n = len(shards)

    def body(*refs):
        ins, outs = refs[:n], refs[n:2 * n]
        send_sems, recv_sems, local_sems = refs[2 * n:]
        x, y, c = _my_coords()
        me, sibling = (x, y, c), (x, y, 1 - c)
        chips = [(1 - x, y), (x, 1 - y), (1 - x, 1 - y)]

        def copy(w, k, block, to, src=None):
            dst = outs[w].at[_dev_index(*block)]
            return pltpu.make_async_remote_copy(
                src_ref=dst if src is None else src, dst_ref=dst,
                send_sem=send_sems.at[w, k], recv_sem=recv_sems.at[w, k],
                device_id=to, device_id_type=pl.DeviceIdType.MESH)

        sends, locals_ = [], []
        for w in range(n):
            mine = pltpu.make_async_copy(ins[w], outs[w].at[_dev_index(*me)], local_sems.at[w])
            mine.start()
            locals_.append(mine)
            first = [copy(w, 0, me, sibling, src=ins[w])]
            first += [copy(w, 1 + j, me, (*chip, c), src=ins[w]) for j, chip in enumerate(chips)]
            for cp in first:
                cp.start()
            sends += first
        for w in range(n):
            for j, chip in enumerate(chips):
                copy(w, 1 + j, (*chip, c), me).wait_recv()
                fwd = copy(w, 4 + j, (*chip, c), sibling)
                fwd.start()
                sends.append(fwd)
        for w in range(n):
            copy(w, 0, sibling, me).wait_recv()
            for j, chip in enumerate(chips):
                copy(w, 4 + j, (*chip, 1 - c), me).wait_recv()
        for cp in sends:
            cp.wait_send()
        for mine in locals_:
            mine.wait()

    any_spec = pl.BlockSpec(memory_space=pl.ANY)
    return _pcall(
        body, name="all_gather_weights",
        out_shape=[_sds((N_DEV,) + s.shape, s.dtype) for s in shards],
        in_specs=[any_spec] * n, out_specs=[any_spec] * n,
        scratch=[pltpu.SemaphoreType.DMA((n, 7)), pltpu.SemaphoreType.DMA((n, 7)),
                 pltpu.SemaphoreType.DMA((n,))])(*shards)


def _wire_cost(arrays):
    nbytes = sum(a.size * a.dtype.itemsize for a in arrays)
    return pl.CostEstimate(flops=0, transcendentals=0, bytes_accessed=40 * nbytes)


def _all_gather_background(shards, collective_id, name):
    n = len(shards)
    src_refs = [jax.new_ref(s, memory_space=pltpu.MemorySpace.HBM) for s in shards]
    out_refs = [jax.empty_ref(_sds((N_DEV,) + s.shape, s.dtype), memory_space=pltpu.MemorySpace.HBM) for s in shards]

    @pl.kernel(mesh=plsc.ScalarSubcoreMesh(axis_name="seq", num_cores=1), name=name,
               scratch_types=(pltpu.SemaphoreType.DMA((n, 7)), pltpu.SemaphoreType.DMA((n, 7)),
                              pltpu.SemaphoreType.DMA((n,))),
               compiler_params=pltpu.CompilerParams(collective_id=collective_id))
    def launch(send_sems, recv_sems, local_sems):
        x, y, c = _my_coords()
        me, sibling = (x, y, c), (x, y, 1 - c)
        chips = [(1 - x, y), (x, 1 - y), (1 - x, 1 - y)]
        barrier = pltpu.get_barrier_semaphore()
        for peer in [sibling] + [(*chip, c) for chip in chips]:
            pl.semaphore_signal(barrier, inc=1, device_id=peer, device_id_type=pl.DeviceIdType.MESH)
        pl.semaphore_wait(barrier, 4)

        def copy(w, k, block, to, src=None):
            dst = out_refs[w].at[_dev_index(*block)]
            return pltpu.make_async_remote_copy(
                src_ref=dst if src is None else src, dst_ref=dst,
                send_sem=send_sems.at[w, k], recv_sem=recv_sems.at[w, k],
                device_id=to, device_id_type=pl.DeviceIdType.MESH)

        sends, locals_ = [], []
        for w in range(n):
            mine = pltpu.make_async_copy(src_refs[w], out_refs[w].at[_dev_index(*me)], local_sems.at[w])
            mine.start()
            locals_.append(mine)
            first = [copy(w, 0, me, sibling, src=src_refs[w])]
            first += [copy(w, 1 + j, me, (*chip, c), src=src_refs[w]) for j, chip in enumerate(chips)]
            for cp in first:
                cp.start()
            sends += first
        for w in range(n):
            for j, chip in enumerate(chips):
                copy(w, 1 + j, (*chip, c), me).wait_recv()
                fwd = copy(w, 4 + j, (*chip, c), sibling)
                fwd.start()
                sends.append(fwd)
        for w in range(n):
            copy(w, 0, sibling, me).wait_recv()
            for j, chip in enumerate(chips):
                copy(w, 4 + j, (*chip, 1 - c), me).wait_recv()
        for cp in sends:
            cp.wait_send()
        for mine in locals_:
            mine.wait()

    launch()
    return [r[...] for r in out_refs]


def _exchange_grads(grads):
    n = len(grads)

    def body(*refs):
        ins, outs = refs[:n], refs[n:2 * n]
        send_sems, recv_sems, local_sems = refs[2 * n:]
        me = _my_coords()
        my_idx = _dev_index(*me)
        sends, locals_ = [], []
        for w in range(n):
            mine = pltpu.make_async_copy(ins[w].at[my_idx], outs[w].at[my_idx], local_sems.at[w])
            mine.start()
            locals_.append(mine)
            for k, f in enumerate(_FLIPS):
                peer = _flip(me, f)
                cp = pltpu.make_async_remote_copy(
                    src_ref=ins[w].at[_dev_index(*peer)], dst_ref=outs[w].at[my_idx],
                    send_sem=send_sems.at[w, k], recv_sem=recv_sems.at[w, k],
                    device_id=peer, device_id_type=pl.DeviceIdType.MESH)
                cp.start()
                sends.append(cp)
        for w in range(n):
            for k, f in enumerate(_FLIPS):
                peer = _flip(me, f)
                slot = outs[w].at[_dev_index(*peer)]
                pltpu.make_async_remote_copy(
                    src_ref=slot, dst_ref=slot,
                    send_sem=send_sems.at[w, k], recv_sem=recv_sems.at[w, k],
                    device_id=peer, device_id_type=pl.DeviceIdType.MESH).wait_recv()
        for cp in sends:
            cp.wait_send()
        for mine in locals_:
            mine.wait()

    any_spec = pl.BlockSpec(memory_space=pl.ANY)
    return _pcall(
        body, name="exchange_grads",
        out_shape=[_sds(g.shape, g.dtype) for g in grads],
        in_specs=[any_spec] * n, out_specs=[any_spec] * n,
        scratch=[pltpu.SemaphoreType.DMA((n, 7)), pltpu.SemaphoreType.DMA((n, 7)),
                 pltpu.SemaphoreType.DMA((n,))])(*grads)


def _exchange_grads_background(grads, collective_id, name):
    n = len(grads)
    src_refs = [jax.new_ref(g, memory_space=pltpu.MemorySpace.HBM) for g in grads]
    out_refs = [jax.empty_ref(_sds(g.shape, g.dtype), memory_space=pltpu.MemorySpace.HBM) for g in grads]

    @pl.kernel(mesh=plsc.ScalarSubcoreMesh(axis_name="seq", num_cores=1), name=name,
               scratch_types=(pltpu.SemaphoreType.DMA((n, 7)), pltpu.SemaphoreType.DMA((n, 7)),
                              pltpu.SemaphoreType.DMA((n,))),
               cost_estimate=_wire_cost(grads),
               compiler_params=pltpu.CompilerParams(collective_id=collective_id))
    def launch(send_sems, recv_sems, local_sems):
        me = _my_coords()
        my_idx = _dev_index(*me)
        peers = [_flip(me, f) for f in _FLIPS]
        barrier = pltpu.get_barrier_semaphore()
        for peer in peers:
            pl.semaphore_signal(barrier, inc=1, device_id=peer, device_id_type=pl.DeviceIdType.MESH)
        pl.semaphore_wait(barrier, len(peers))
        sends, locals_ = [], []
        for w in range(n):
            mine = pltpu.make_async_copy(src_refs[w].at[my_idx], out_refs[w].at[my_idx], local_sems.at[w])
            mine.start()
            locals_.append(mine)
            for k, peer in enumerate(peers):
                cp = pltpu.make_async_remote_copy(
                    src_ref=src_refs[w].at[_dev_index(*peer)], dst_ref=out_refs[w].at[my_idx],
                    send_sem=send_sems.at[w, k], recv_sem=recv_sems.at[w, k],
                    device_id=peer, device_id_type=pl.DeviceIdType.MESH)
                cp.start()
                sends.append(cp)
        for w in range(n):
            for k, peer in enumerate(peers):
                slot = out_refs[w].at[_dev_index(*peer)]
                pltpu.make_async_remote_copy(
                    src_ref=slot, dst_ref=slot, send_sem=send_sems.at[w, k], recv_sem=recv_sems.at[w, k],
                    device_id=peer, device_id_type=pl.DeviceIdType.MESH).wait_recv()
        for cp in sends:
            cp.wait_send()
        for mine in locals_:
            mine.wait()

    launch()
    return [r[...] for r in out_refs]


def _to_wire(parts, after, name):
    n = len(parts)
    rows, cols = parts[0].shape
    tr = rows // 2 if rows % 32 == 0 else rows

    def body(*refs):
        for k in range(n):
            refs[n][k] = refs[k][...].astype(WIRE)

    blk = pl.BlockSpec((tr, cols), lambda i: (i, 0))
    return _pcall(
        body, name=name, grid=(rows // tr,), out_shape=_sds((n, rows, cols), WIRE),
        in_specs=[blk] * n, out_specs=pl.BlockSpec((n, tr, cols), lambda i: (0, i, 0)),
        sem=("parallel",), after=after)(*parts)


def _adam_math(w, g, m, v):
    m = ADAM_B1 * m + (1.0 - ADAM_B1) * g
    v = ADAM_B2 * v + (1.0 - ADAM_B2) * (g * g)
    m_hat = m / (1.0 - ADAM_B1 ** ADAM_STEP)
    v_hat = v / (1.0 - ADAM_B2 ** ADAM_STEP)
    delta = -ADAM_LR * (m_hat / (jnp.sqrt(v_hat) + ADAM_EPS) + ADAM_WD * w)
    return delta, m, v


def _small_layout(sizes):
    row0, r = [], 0
    for n in sizes:
        row0.append(r)
        r += -(-n // LANES)
    return row0, r, -(-(r + 1) // 8) * 8


def _small_pieces(n):
    return [(k, min(LANES, n - LANES * k)) for k in range(-(-n // LANES))]


def _small_fill(pack, slot, srcs, sizes, row0, rows):
    pack[slot] = jnp.zeros((rows, LANES), jnp.float32)
    for p, n in enumerate(sizes):
        val = srcs[p][...]
        if val.shape[-1] == LANES and n == 64:
            pack[slot, row0[p]:row0[p] + 1, :] = val + pltpu.roll(val, 64, 1)
            continue
        for k, width in _small_pieces(n):
            pack[slot, row0[p] + k:row0[p] + k + 1, 0:width] = srcs[p][:, LANES * k:LANES * k + width]


def _small_pack(grads, loss_tile, sizes):
    n_par = len(sizes)
    row0, loss_row, rows = _small_layout(sizes)

    def body(*refs):
        g_refs, loss_in, out_ref = refs[:n_par], refs[n_par], refs[n_par + 1]
        _small_fill(out_ref, 0, g_refs, sizes, row0, rows)
        out_ref[0, loss_row:loss_row + 1, :] = loss_in[0:1, :]

    vm = pl.BlockSpec(memory_space=pltpu.VMEM)
    return _pcall(
        body, name="small_pack", out_shape=_sds((1, rows, LANES), jnp.float32),
        in_specs=[vm] * (n_par + 1), out_specs=vm)(*grads, loss_tile)


def _small_adam(packs, ws, ms, vs, after):
    sizes = [w.shape[-1] for w in ws]
    n_par = len(ws)
    row0, loss_row, rows = _small_layout(sizes)

    def body(*refs):
        g_ref = refs[0]
        w_refs, m_refs, v_refs = (refs[1 + k * n_par: 1 + (k + 1) * n_par] for k in range(3))
        loss_out = refs[3 * n_par + 1]
        out_refs = refs[3 * n_par + 2: 7 * n_par + 2]
        pack, res = refs[7 * n_par + 2:]
        for slot, srcs in enumerate((w_refs, m_refs, v_refs)):
            _small_fill(pack, slot, srcs, sizes, row0, rows)
        g = g_ref[0]
        for dev in range(1, N_DEV):
            g = g + g_ref[dev]
        delta, m, v = _adam_math(pack[0], g, pack[1], pack[2])
        res[0], res[1], res[2], res[3] = g, delta, m, v
        loss_out[...] = res[0, loss_row:loss_row + 1, 0:1]
        for p, n in enumerate(sizes):
            for kind in range(4):
                for k, width in _small_pieces(n):
                    out_refs[4 * p + kind][:, LANES * k:LANES * k + width] = (
                        res[kind, row0[p] + k:row0[p] + k + 1, 0:width])

    vm = pl.BlockSpec(memory_space=pltpu.VMEM)
    out_shape = [_sds((1, 1), jnp.float32)]
    for n in sizes:
        out_shape += [_sds((1, n), jnp.float32)] * 4
    outs = _pcall(
        body, name="small_adam", out_shape=out_shape,
        in_specs=[vm] * (3 * n_par + 1), out_specs=[vm] * len(out_shape),
        scratch=[pltpu.VMEM((3, rows, LANES), jnp.float32), pltpu.VMEM((4, rows, LANES), jnp.float32)],
        after=after)(packs, *ws, *ms, *vs)
    return outs[0], [outs[1 + 4 * p: 5 + 4 * p] for p in range(n_par)]


def _adam_big(recv, w, m, v, name, after=None, which=None):
    rows, cols = recv.shape[-2:]
    row_tiles = [t for t in range(16, rows + 1, 16) if rows % t == 0 and t * cols <= 400 * 1024]
    tr, tc = (max(row_tiles), cols) if row_tiles else (rows, 512 if cols % 512 == 0 else cols)

    def body(r_ref, w_ref, m_ref, v_ref, g_ref, d_ref, mo_ref, vo_ref):
        g = r_ref[0].astype(jnp.float32)
        for d in range(1, N_DEV):
            g = g + r_ref[d].astype(jnp.float32)
        delta, mn, vn = _adam_math(w_ref[...], g, m_ref[...], v_ref[...])
        g_ref[...] = g
        d_ref[...] = delta
        mo_ref[...] = mn
        vo_ref[...] = vn

    blk = pl.BlockSpec((tr, tc), lambda i, j: (i, j))
    if which is None:
        r_spec = pl.BlockSpec((N_DEV, tr, tc), lambda i, j: (0, i, j))
    else:
        r_spec = pl.BlockSpec((N_DEV, None, tr, tc), lambda i, j: (0, which, i, j))
    return _pcall(
        body, name=name, grid=(rows // tr, cols // tc),
        out_shape=[_sds((rows, cols), jnp.float32)] * 4,
        in_specs=[r_spec, blk, blk, blk],
        out_specs=[blk] * 4, sem=("parallel", "parallel"), after=after)(recv, w, m, v)


def _mm(a, b, *, ta=False, tb=False, out_dtype, tm, tk, name):
    (kdim, mdim) = a.shape if ta else a.shape[::-1]
    ndim = b.shape[0] if tb else b.shape[1]
    tm, tk = min(tm, mdim), min(tk, kdim)
    nk = kdim // tk

    def body(a_ref, b_ref, o_ref, acc):
        k = pl.program_id(1)
        if ta:
            part = _dot_tn(a_ref[...], b_ref[...])
        elif tb:
            part = _dot_nt(a_ref[...], b_ref[...])
        else:
            part = _dot(a_ref[...], b_ref[...])

        @pl.when(k == 0)
        def _():
            acc[...] = part

        @pl.when(k > 0)
        def _():
            acc[...] += part

        @pl.when(k == nk - 1)
        def _():
            o_ref[...] = acc[...].astype(o_ref.dtype)

    a_spec = pl.BlockSpec((tk, tm), lambda i, k: (k, i)) if ta else pl.BlockSpec((tm, tk), lambda i, k: (i, k))
    b_spec = pl.BlockSpec((ndim, tk), lambda i, k: (0, k)) if tb else pl.BlockSpec((tk, ndim), lambda i, k: (k, 0))
    return _pcall(
        body, name=name, grid=(mdim // tm, nk), out_shape=_sds((mdim, ndim), out_dtype),
        in_specs=[a_spec, b_spec], out_specs=pl.BlockSpec((tm, ndim), lambda i, k: (i, 0)),
        scratch=[pltpu.VMEM((tm, ndim), jnp.float32)], sem=("parallel", "arbitrary"))(a, b)


def _ref_col_pieces(start, stop):
    ref_starts = [0, 1024, 1152, 1280, 1792, 2304, 2368, 2880]
    perm_starts = [C_QA, C_KA, C_VA, C_CQ, C_CKV, C_KR, C_QM]
    out = []
    for p in range(7):
        lo, hi = max(start, ref_starts[p]), min(stop, ref_starts[p + 1])
        if lo < hi:
            out.append((lo - start, perm_starts[p] + lo - ref_starts[p], hi - lo))
    return out


def _dw_in(hn, d_proj, n_shard, after):
    s, d = hn.shape
    n = sum(p.shape[1] for p in d_proj)
    n_pc = len(d_proj)
    tm, tk = min(512, d), min(1024, s)
    nk = s // tk

    def body(a_ref, *refs):
        b_refs, (o_ref, acc) = refs[:n_pc], refs[n_pc:]
        k = pl.program_id(1)
        part = _dot_tn(a_ref[...], jnp.concatenate([r[...] for r in b_refs], axis=1))

        @pl.when(k == 0)
        def _():
            acc[...] = part

        @pl.when(k > 0)
        def _():
            acc[...] += part

        @pl.when(k == nk - 1)
        def _():
            t = acc[...].T
            for j in range(N_DEV):
                rows = [t[src:src + width] for _, src, width in _ref_col_pieces(j * n_shard, (j + 1) * n_shard)]
                o_ref[j] = jnp.concatenate(rows, axis=0).astype(o_ref.dtype)

    return _pcall(
        body, name="dw_in", grid=(d // tm, nk), out_shape=_sds((N_DEV, n_shard, d), WIRE),
        in_specs=[pl.BlockSpec((tk, tm), lambda i, k: (k, i))]
        + [pl.BlockSpec((tk, p.shape[1]), lambda i, k: (k, 0)) for p in d_proj],
        out_specs=pl.BlockSpec((N_DEV, n_shard, tm), lambda i, k: (0, 0, i)),
        scratch=[pltpu.VMEM((tm, n), jnp.float32)], sem=("parallel", "arbitrary"), after=after)(hn, *d_proj)


def _in_proj(x, g, w):
    s, d = x.shape
    n = w.shape[0]
    tm = min(2 * ROW_TILE, s)

    def body(x_ref, g_ref, w_ref, p_ref, hn_ref):
        hn, _, _ = _norm_fwd(x_ref[...], g_ref[...])
        hn_ref[...] = hn.astype(hn_ref.dtype)
        p_ref[...] = _dot_nt(hn, w_ref[...])

    return _pcall(
        body, name="in_proj", grid=(s // tm,),
        out_shape=[_sds((s, n), jnp.float32), _sds((s, d), MXU)],
        in_specs=[pl.BlockSpec((tm, d), lambda i: (i, 0)), pl.BlockSpec((1, d), lambda i: (0, 0)),
                  pl.BlockSpec((n, d), lambda i: (0, 0), pipeline_mode=pl.Buffered(1))],
        out_specs=[pl.BlockSpec((tm, n), lambda i: (i, 0)), pl.BlockSpec((tm, d), lambda i: (i, 0))],
        sem=("parallel",))(x, g, w)


def _norm_rows(x, g):
    s, d = x.shape
    tm = min(FFN_TILE, s)

    def body(x_ref, g_ref, hn_ref):
        hn, _, _ = _norm_fwd(x_ref[...], g_ref[...])
        hn_ref[...] = hn.astype(hn_ref.dtype)

    row = pl.BlockSpec((tm, d), lambda i: (i, 0))
    return _pcall(body, name="norm_rows", grid=(s // tm,), out_shape=_sds((s, d), MXU),
                  in_specs=[row, pl.BlockSpec((1, d), lambda i: (0, 0))], out_specs=row, sem=("parallel",))(x, g)


def _mla_prep(proj, cos, sin, g_cq, g_ckv, w_uq, w_ukv, g_qn, g_qr, g_kn, g_kr):
    s = proj.shape[0]
    tm = min(ROW_TILE, s)
    nh = MLA_HEADS

    def body(cq_ref, ckv_ref, kr_ref, cos_ref, sin_ref, gcq_ref, gckv_ref, wuq_ref, wukv_ref,
             gqn_ref, gqr_ref, gkn_ref, gkr_ref,
             qc_ref, kc_ref, v_ref, qb_ref, kvb_ref, cqn_ref, ckvn_ref):
        cos_t, sin_t = cos_ref[...], sin_ref[...]
        lo = _lo_mask((tm, LANES))
        cqn, _, _ = _norm_fwd(cq_ref[...], gcq_ref[...])
        cqn_ref[...] = cqn.astype(cqn_ref.dtype)
        qb = _dot_nt(cqn, wuq_ref[...])
        qb_ref[...] = qb
        ckvn, _, _ = _norm_fwd(ckv_ref[...], gckv_ref[...])
        ckvn_ref[...] = ckvn.astype(ckvn_ref.dtype)
        w_ukv_full = jnp.concatenate([wukv_ref[dev] for dev in range(N_DEV)], axis=1)
        kvb = _dot(ckvn, w_ukv_full)
        kvb_ref[...] = kvb
        kr, _, _ = _norm_fwd(kr_ref[...], gkr_ref[...], half=True)
        kr = _rope(kr, cos_t, sin_t)
        kr2 = jnp.where(lo, kr, pltpu.roll(kr, 64, 1))
        ropes = []
        for j in range(nh // 2):
            xr = qb[:, nh * MLA_NOPE + LANES * j: nh * MLA_NOPE + LANES * (j + 1)]
            qr, _, _ = _norm_fwd(xr, gqr_ref[...], half=True)
            ropes.append(_rope(qr, cos_t, sin_t))
        for h in range(nh):
            qn, _, _ = _norm_fwd(qb[:, MLA_NOPE * h: MLA_NOPE * (h + 1)], gqn_ref[...])
            mask = lo if h % 2 == 0 else jnp.logical_not(lo)
            qr = jnp.where(mask, ropes[h // 2], 0.0)
            qc_ref[h] = jnp.concatenate([qn, qr], axis=1).astype(qc_ref.dtype)
            kn, _, _ = _norm_fwd(kvb[:, 256 * h: 256 * h + MLA_NOPE], gkn_ref[...])
            kc_ref[h] = jnp.concatenate([kn, kr2], axis=1).astype(kc_ref.dtype)
            v_ref[h] = kvb[:, 256 * h + MLA_NOPE: 256 * (h + 1)].astype(v_ref.dtype)

    def col(width, start):
        return pl.BlockSpec((tm, width), lambda i: (i, start // width))

    def full(shape):
        return pl.BlockSpec(shape, lambda i: (0,) * len(shape))

    def row(width):
        return pl.BlockSpec((tm, width), lambda i: (i, 0))

    def heads(width):
        return pl.BlockSpec((nh, tm, width), lambda i: (0, i, 0))

    return _pcall(
        body, name="mla_prep", grid=(s // tm,),
        out_shape=[_sds((nh, s, 256), MXU), _sds((nh, s, 256), MXU), _sds((nh, s, MLA_V), MXU),
                   _sds((s, 768), jnp.float32), _sds((s, 1024), jnp.float32),
                   _sds((s, 512), MXU), _sds((s, 512), MXU)],
        in_specs=[col(512, C_CQ), col(512, C_CKV), col(LANES, C_KR), row(LANES), row(LANES),
                  full((1, 512)), full((1, 512)), full((768, 512)), full((N_DEV, 512, LANES)),
                  full((1, LANES)), full((1, LANES)), full((1, LANES)), full((1, LANES))],
        out_specs=[heads(256), heads(256), heads(MLA_V), row(768), row(1024), row(512), row(512)],
        sem=("parallel",))(proj, proj, proj, cos, sin, g_cq, g_ckv, w_uq, w_ukv, g_qn, g_qr, g_kn, g_kr)


def _mla_fwd(qc, kc, v):
    nh, s, _ = qc.shape
    t = min(ATT_TILE, s)
    nb = s // t
    scale = (MLA_NOPE + MLA_ROPE) ** -0.5

    def body(q_ref, k_ref, v_ref, y_ref, lse_ref, m_sc, l_sc, acc):
        qi, ki = pl.program_id(1), pl.program_id(2)

        @pl.when(ki == 0)
        def _():
            m_sc[...] = jnp.full_like(m_sc, NEG_INF)
            l_sc[...] = jnp.zeros_like(l_sc)
            acc[...] = jnp.zeros_like(acc)

        def step(diagonal):
            rc = t // 4 if diagonal else t
            for c in range(t // rc):
                rows = slice(rc * c, rc * (c + 1))
                keys = slice(0, rc * (c + 1))
                sc = _dot_nt(q_ref[0, rows, :], k_ref[0, keys, :]) * (scale * LOG2E)
                if diagonal:
                    r_i = lax.broadcasted_iota(jnp.int32, sc.shape, 0) + rc * c
                    c_i = lax.broadcasted_iota(jnp.int32, sc.shape, 1)
                    sc = jnp.where(c_i <= r_i, sc, NEG_INF)
                m_old = m_sc[rows, :]
                m_new = jnp.maximum(m_old, jnp.max(sc, -1, keepdims=True))
                alpha = jnp.exp2(m_old - m_new)
                p = jnp.exp2(sc - m_new)
                l_sc[rows, :] = alpha * l_sc[rows, :] + jnp.sum(p, -1, keepdims=True)
                acc[rows, :] = alpha * acc[rows, :] + _dot(p, v_ref[0, keys, :])
                m_sc[rows, :] = m_new

        @pl.when(ki < qi)
        def _():
            step(False)

        @pl.when(ki == qi)
        def _():
            step(True)

        @pl.when(ki == qi)
        def _():
            y_ref[...] = acc[...] / l_sc[...]
            lse_ref[0] = m_sc[...] + jnp.log2(l_sc[...])

    return _pcall(
        body, name="mla_fwd", grid=(nh, nb, nb),
        out_shape=[_sds((s, nh * MLA_V), jnp.float32), _sds((nh, s, 1), jnp.float32)],
        in_specs=[pl.BlockSpec((1, t, 256), lambda h, i, k: (h, i, 0)),
                  pl.BlockSpec((1, t, 256), lambda h, i, k: (h, jnp.minimum(k, i), 0)),
                  pl.BlockSpec((1, t, MLA_V), lambda h, i, k: (h, jnp.minimum(k, i), 0))],
        out_specs=[pl.BlockSpec((t, MLA_V), lambda h, i, k: (i, h)),
                   pl.BlockSpec((1, t, 1), lambda h, i, k: (h, i, 0))],
        scratch=[pltpu.VMEM((t, 1), jnp.float32), pltpu.VMEM((t, 1), jnp.float32),
                 pltpu.VMEM((t, MLA_V), jnp.float32)],
        sem=("parallel", "parallel", "arbitrary"))(qc, kc, v)


def _memkv_prep(mem, g_mem, w_mkv, g_mk):
    ml, d = mem.shape
    hw = MEM_HEADS * MEM_DIM

    def body(mem_ref, g_ref, w_ref, gk_ref, k_ref, v_ref, kv_ref, mn_ref):
        mn, _, _ = _norm_fwd(mem_ref[...], g_ref[...])
        mn_ref[...] = mn.astype(mn_ref.dtype)
        kv = _dot(mn, w_ref[...])
        kv_ref[...] = kv
        for h in range(MEM_HEADS):
            kn, _, _ = _norm_fwd(kv[:, MEM_DIM * h: MEM_DIM * (h + 1)], gk_ref[...])
            k_ref[:, MEM_DIM * h: MEM_DIM * (h + 1)] = kn.astype(k_ref.dtype)
        v_ref[...] = kv[:, hw:].astype(v_ref.dtype)

    vm = pl.BlockSpec(memory_space=pltpu.VMEM)
    return _pcall(
        body, name="memkv_prep",
        out_shape=[_sds((ml, hw), MXU), _sds((ml, hw), MXU), _sds((ml, 2 * hw), jnp.float32), _sds((ml, d), MXU)],
        in_specs=[vm] * 4, out_specs=[vm] * 4)(mem, g_mem, w_mkv, g_mk)


def _mem_fwd(proj, g_mq, km, vmm):
    s = proj.shape[0]
    ml, hw = km.shape
    tm = min(FFN_TILE, s)
    scale = MEM_DIM ** -0.5

    def body(q_ref, g_ref, k_ref, v_ref, y_ref, lse_ref):
        col = lax.broadcasted_iota(jnp.int32, (tm, MEM_HEADS), 1)
        lse_t = jnp.zeros((tm, MEM_HEADS), jnp.float32)
        for h in range(MEM_HEADS):
            sl = slice(MEM_DIM * h, MEM_DIM * (h + 1))
            qn, _, _ = _norm_fwd(q_ref[:, sl], g_ref[...])
            sc = _dot_nt(qn, k_ref[:, sl]) * scale
            m = jnp.max(sc, -1, keepdims=True)
            p = jnp.exp(sc - m)
            l = jnp.sum(p, -1, keepdims=True)
            y_ref[:, sl] = _dot(p, v_ref[:, sl]) / l
            lse_t = jnp.where(col == h, m + jnp.log(l), lse_t)
        lse_ref[...] = lse_t

    return _pcall(
        body, name="mem_fwd", grid=(s // tm,),
        out_shape=[_sds((s, hw), jnp.float32), _sds((s, MEM_HEADS), jnp.float32)],
        in_specs=[pl.BlockSpec((tm, hw), lambda i: (i, C_QM // hw)), pl.BlockSpec((1, MEM_DIM), lambda i: (0, 0)),
                  pl.BlockSpec((ml, hw), lambda i: (0, 0)), pl.BlockSpec((ml, hw), lambda i: (0, 0))],
        out_specs=[pl.BlockSpec((tm, hw), lambda i: (i, 0)), pl.BlockSpec((tm, MEM_HEADS), lambda i: (i, 0))],
        sem=("parallel",))(proj, g_mq, km, vmm)


def _alibi_slope(h):
    return float(2.0 ** (-8.0 * (h + 1) / SWA_Q_HEADS))


def _swa_common(n, kp, kc, vp, vc, pq, pkp, pkc, gk):
    b = SWA_BLOCK
    k_raw = jnp.concatenate([kp, kc], axis=0)
    kn, kxn, kr = _norm_fwd(k_raw, gk, half=True)
    v = jnp.concatenate([vp, vc], axis=0)
    dist = jnp.abs(pq - jnp.concatenate([pkp, pkc], axis=1))
    r_i = lax.broadcasted_iota(jnp.int32, (b, 2 * b), 0)
    c_i = lax.broadcasted_iota(jnp.int32, (b, 2 * b), 1)
    valid = (c_i > r_i) & (c_i <= r_i + b) & (c_i >= jnp.where(n > 0, 0, b))
    bias = jnp.where(valid, -dist, NEG_INF)
    return kn, v, bias


def _swa_folded(n, kp, kc, pq, pkp, pkc, gk):
    b = SWA_BLOCK
    kn_p, _, _ = _norm_fwd(kp, gk, half=True)
    kn_c, _, _ = _norm_fwd(kc, gk, half=True)
    r_i = lax.broadcasted_iota(jnp.int32, (b, b), 0)
    c_i = lax.broadcasted_iota(jnp.int32, (b, b), 1)
    upper = c_i > r_i
    bias_prev = jnp.where(n > 0, 0.0, NEG_INF) - jnp.abs(pq - pkp)
    bias = jnp.where(upper, bias_prev, -jnp.abs(pq - pkc))
    return kn_p, kn_c, bias, upper


def _swa_specs(s):
    b = SWA_BLOCK
    prev = lambda n: jnp.maximum(n - 1, 0)
    return [
        pl.BlockSpec((b, 1024), lambda n: (n, C_QA // 1024)),
        pl.BlockSpec((b, LANES), lambda n: (prev(n), C_KA // LANES)),
        pl.BlockSpec((b, LANES), lambda n: (n, C_KA // LANES)),
        pl.BlockSpec((b, LANES), lambda n: (prev(n), C_VA // LANES)),
        pl.BlockSpec((b, LANES), lambda n: (n, C_VA // LANES)),
        pl.BlockSpec((b, 1), lambda n: (n, 0)),
        pl.BlockSpec((1, b), lambda n: (0, prev(n))),
        pl.BlockSpec((1, b), lambda n: (0, n)),
        pl.BlockSpec((1, LANES), lambda n: (0, 0)),
        pl.BlockSpec((1, LANES), lambda n: (0, 0)),
        pl.BlockSpec(memory_space=pltpu.SMEM),
    ]


def _swa_fwd(proj, posc, posr, gq, gk, sinks):
    s = proj.shape[0]
    b = SWA_BLOCK
    scale = SWA_DIM ** -0.5

    def body(q_ref, kp_ref, kc_ref, vp_ref, vc_ref, pq_ref, pkp_ref, pkc_ref, gq_ref, gk_ref, sink_ref,
             y_ref, lse_ref):
        n = pl.program_id(0)
        kn_p, kn_c, bias, upper = _swa_folded(n, kp_ref[...], kc_ref[...], pq_ref[...], pkp_ref[...], pkc_ref[...],
                                              gk_ref[...])
        v_p, v_c = vp_ref[...], vc_ref[...]
        lo = _lo_mask((b, LANES))
        col = lax.broadcasted_iota(jnp.int32, (b, SWA_Q_HEADS), 1)
        lse_t = jnp.zeros((b, SWA_Q_HEADS), jnp.float32)
        hpg = SWA_Q_HEADS // SWA_KV_HEADS
        for g in range(SWA_KV_HEADS):
            heads = range(hpg * g, hpg * (g + 1))
            kvmask = lo if g == 0 else jnp.logical_not(lo)
            qs = []
            for j in range(hpg // 2 * g, hpg // 2 * (g + 1)):
                qn, _, _ = _norm_fwd(q_ref[:, LANES * j: LANES * (j + 1)], gq_ref[...], half=True)
                qn = qn * scale
                qsw = pltpu.roll(qn, 64, 1)
                qs += [jnp.where(kvmask, qn if e == g else qsw, 0.0) for e in range(2)]
            q_st = jnp.concatenate(qs, axis=0).astype(MXU)
            sp_st, sc_st = _dot_nt(q_st, kn_p), _dot_nt(q_st, kn_c)
            pus, pls, ls = [], [], []
            for i, h in enumerate(heads):
                rows = slice(b * i, b * (i + 1))
                sc = jnp.where(upper, sp_st[rows], sc_st[rows]) + _alibi_slope(h) * bias
                sk = sink_ref[h]
                m = jnp.maximum(jnp.max(sc, -1, keepdims=True), sk)
                p = jnp.exp(sc - m)
                l = jnp.sum(p, -1, keepdims=True) + jnp.exp(sk - m)
                pus.append(jnp.where(upper, p, 0.0).astype(MXU))
                pls.append(jnp.where(upper, 0.0, p).astype(MXU))
                ls.append(l)
                lse_t = jnp.where(col == h, m + jnp.log(l), lse_t)
            o_st = _dot(jnp.concatenate(pus, axis=0), v_p) + _dot(jnp.concatenate(pls, axis=0), v_c)
            for j in range(hpg // 2 * g, hpg // 2 * (g + 1)):
                halves = []
                for e in range(2):
                    i = 2 * j + e - hpg * g
                    o_h = o_st[b * i: b * (i + 1)] / ls[i]
                    halves.append(o_h if e == g else pltpu.roll(o_h, 64, 1))
                y_ref[:, LANES * j: LANES * (j + 1)] = jnp.where(lo, halves[0], halves[1])
        lse_ref[...] = lse_t

    return _pcall(
        body, name="swa_fwd", grid=(s // b,),
        out_shape=[_sds((s, 1024), jnp.float32), _sds((s, SWA_Q_HEADS), jnp.float32)],
        in_specs=_swa_specs(s),
        out_specs=[pl.BlockSpec((b, 1024), lambda n: (n, 0)), pl.BlockSpec((b, SWA_Q_HEADS), lambda n: (n, 0))],
        sem=("parallel",))(proj, proj, proj, proj, proj, posc, posr, posr, gq, gk, sinks)


def _out_proj(y_a, y_b, y_m, x, w_out, g_ffn):
    s, d = x.shape
    tm = min(2 * ROW_TILE, s)

    def body(ya_ref, yb_ref, ym_ref, x_ref, w_ref, g_ref, h1_ref, fn_ref):
        y = jnp.concatenate([ya_ref[...].astype(MXU), yb_ref[...].astype(MXU), ym_ref[...].astype(MXU)], axis=1)
        h1 = x_ref[...] + _dot(y, w_ref[...])
        h1_ref[...] = h1
        fn, _, _ = _norm_fwd(h1, g_ref[...])
        fn_ref[...] = fn.astype(fn_ref.dtype)

    def row(width):
        return pl.BlockSpec((tm, width), lambda i: (i, 0))

    return _pcall(
        body, name="out_proj", grid=(s // tm,),
        out_shape=[_sds((s, d), jnp.float32), _sds((s, d), MXU)],
        in_specs=[row(1024), row(512), row(512), row(d),
                  pl.BlockSpec(w_out.shape, lambda i: (0, 0), pipeline_mode=pl.Buffered(1)),
                  pl.BlockSpec((1, d), lambda i: (0, 0))],
        out_specs=[row(d), row(d)], sem=("parallel",))(y_a, y_b, y_m, x, w_out, g_ffn)


def _ffn_gu(fn, w_gu):
    s, d = fn.shape
    f = w_gu.shape[2]
    tm = min(2 * FFN_TILE, s)

    def body(fn_ref, w_ref, gu_ref, act_ref):
        x = fn_ref[...]
        g = _dot_nt(x, w_ref[0, 0])
        u = _dot_nt(x, w_ref[0, 1])
        gu_ref[0, 0] = g
        gu_ref[0, 1] = u
        act_ref[0] = (g * jax.nn.sigmoid(g) * u).astype(act_ref.dtype)

    return _pcall(
        body, name="ffn_gate_up", grid=(N_DEV, s // tm),
        out_shape=[_sds((N_DEV, 2, s, f), jnp.float32), _sds((N_DEV, s, f), MXU)],
        in_specs=[pl.BlockSpec((tm, d), lambda j, i: (i, 0)),
                  pl.BlockSpec((1, 2, f, d), lambda j, i: (j, 0, 0, 0))],
        out_specs=[pl.BlockSpec((1, 2, tm, f), lambda j, i: (j, 0, i, 0)),
                   pl.BlockSpec((1, tm, f), lambda j, i: (j, i, 0))],
        sem=("parallel", "parallel"))(fn, w_gu)


def _ffn_down(act, w_d, h1, target):
    _, s, f = act.shape
    d = h1.shape[1]
    tm = min(FFN_TILE, s)

    def body(a_ref, w_ref, h1_ref, t_ref, dout_ref, doutb_ref, loss_ref, acc):
        i, j = pl.program_id(0), pl.program_id(1)
        part = _dot(a_ref[0], w_ref[0]) + _dot(a_ref[1], w_ref[1])

        @pl.when(j == 0)
        def _():
            acc[...] = h1_ref[...] + part

        @pl.when(j > 0)
        def _():
            acc[...] += part

        @pl.when((i == 0) & (j == 0))
        def _():
            loss_ref[...] = jnp.zeros_like(loss_ref)

        @pl.when(j == N_DEV // 2 - 1)
        def _():
            diff = acc[...] - t_ref[...]
            dout_ref[...] = diff / d
            doutb_ref[...] = (diff / d).astype(doutb_ref.dtype)
            loss_ref[...] += 0.5 * jnp.sum(jnp.sum(diff * diff, -1, keepdims=True) / d)

    row = pl.BlockSpec((tm, d), lambda i, j: (i, 0))
    return _pcall(
        body, name="ffn_down", grid=(s // tm, N_DEV // 2),
        out_shape=[_sds((s, d), jnp.float32), _sds((s, d), MXU), _sds((8, LANES), jnp.float32)],
        in_specs=[pl.BlockSpec((2, tm, f), lambda i, j: (j, i, 0)), pl.BlockSpec((2, f, d), lambda i, j: (j, 0, 0)),
                  row, row],
        out_specs=[row, row, pl.BlockSpec((8, LANES), lambda i, j: (0, 0))],
        scratch=[pltpu.VMEM((tm, d), jnp.float32)], sem=("arbitrary", "arbitrary"))(act, w_d, h1, target)


def _ffn_bwd_act(dout, w_d, gu):
    s, d = dout.shape
    f = w_d.shape[1]
    tm = min(2 * FFN_TILE, s)
    ni = s // tm

    def body(do_ref, w_ref, gu_ref, dgu_ref, dw_ref, acc):
        i = pl.program_id(1)
        do = do_ref[...]
        d_act = _dot_nt(do, w_ref[0])
        g, u = gu_ref[0, 0], gu_ref[0, 1]
        sig = jax.nn.sigmoid(g)
        silu = g * sig
        dgu_ref[0, 0] = (d_act * u * (sig * (1.0 + g * (1.0 - sig)))).astype(dgu_ref.dtype)
        dgu_ref[0, 1] = (d_act * silu).astype(dgu_ref.dtype)
        part = _dot_tn(silu * u, do)

        @pl.when(i == 0)
        def _():
            acc[...] = part

        @pl.when(i > 0)
        def _():
            acc[...] += part

        @pl.when(i == ni - 1)
        def _():
            dw_ref[0] = acc[...].astype(dw_ref.dtype)

    return _pcall(
        body, name="ffn_bwd_act", grid=(N_DEV, ni),
        out_shape=[_sds((N_DEV, 2, s, f), MXU), _sds((N_DEV, f, d), WIRE)],
        in_specs=[pl.BlockSpec((tm, d), lambda j, i: (i, 0)), pl.BlockSpec((1, f, d), lambda j, i: (j, 0, 0)),
                  pl.BlockSpec((1, 2, tm, f), lambda j, i: (j, 0, i, 0))],
        out_specs=[pl.BlockSpec((1, 2, tm, f), lambda j, i: (j, 0, i, 0)),
                   pl.BlockSpec((1, f, d), lambda j, i: (j, 0, 0))],
        scratch=[pltpu.VMEM((f, d), jnp.float32)], sem=("parallel", "arbitrary"))(dout, w_d, gu)


def _ffn_dw_gu(fn, dgu):
    s, d = fn.shape
    f = dgu.shape[-1]
    tk = min(4 * FFN_TILE, s)
    nk = s // tk

    def body(fn_ref, dgu_ref, dw_ref, acc):
        k = pl.program_id(2)
        part = _dot_tn(dgu_ref[0, 0], fn_ref[...])

        @pl.when(k == 0)
        def _():
            acc[...] = part

        @pl.when(k > 0)
        def _():
            acc[...] += part

        @pl.when(k == nk - 1)
        def _():
            dw_ref[0, 0] = acc[...].astype(dw_ref.dtype)

    return _pcall(
        body, name="ffn_dw_gate_up", grid=(N_DEV, 2, nk),
        out_shape=_sds((N_DEV, 2, f, d), WIRE),
        in_specs=[pl.BlockSpec((tk, d), lambda j, w, k: (k, 0)),
                  pl.BlockSpec((1, 1, tk, f), lambda j, w, k: (j, w, k, 0))],
        out_specs=pl.BlockSpec((1, 1, f, d), lambda j, w, k: (j, w, 0, 0)),
        scratch=[pltpu.VMEM((f, d), jnp.float32)], sem=("parallel", "parallel", "arbitrary"))(fn, dgu)


def _ffn_dfn(dgu, w_gu, after):
    _, _, s, f = dgu.shape
    d = w_gu.shape[3]
    tm = min(FFN_TILE, s)

    def body(dgu_ref, w_ref, dfn_ref):
        j = pl.program_id(1)
        part = (_dot(dgu_ref[0, 0], w_ref[0, 0]) + _dot(dgu_ref[0, 1], w_ref[0, 1])
                + _dot(dgu_ref[1, 0], w_ref[1, 0]) + _dot(dgu_ref[1, 1], w_ref[1, 1]))

        @pl.when(j == 0)
        def _():
            dfn_ref[...] = part

        @pl.when(j > 0)
        def _():
            dfn_ref[...] += part

    return _pcall(
        body, name="ffn_dfn", grid=(s // tm, N_DEV // 2),
        out_shape=_sds((s, d), jnp.float32),
        in_specs=[pl.BlockSpec((2, 2, tm, f), lambda i, j: (j, 0, i, 0)),
                  pl.BlockSpec((2, 2, f, d), lambda i, j: (j, 0, 0, 0))],
        out_specs=pl.BlockSpec((tm, d), lambda i, j: (i, 0)),
        sem=("parallel", "arbitrary"), after=after)(dgu, w_gu)


def _ffn_norm_bwd(d_fn, dout, h1, g_ffn):
    s, d = h1.shape
    tm = min(2 * ROW_TILE, s)

    def body(dfn_ref, do_ref, h1_ref, g_ref, dh1_ref, dg_ref):
        i = pl.program_id(0)

        @pl.when(i == 0)
        def _():
            dg_ref[...] = jnp.zeros_like(dg_ref)

        _, xn, r = _norm_fwd(h1_ref[...], g_ref[...])
        dx, dg = _norm_bwd(xn, r, g_ref[...], dfn_ref[...])
        dh1_ref[...] = do_ref[...] + dx
        dg_ref[...] += dg

    row = pl.BlockSpec((tm, d), lambda i: (i, 0))
    vec = pl.BlockSpec((1, d), lambda i: (0, 0))
    return _pcall(
        body, name="ffn_norm_bwd", grid=(s // tm,),
        out_shape=[_sds((s, d), jnp.float32), _sds((1, d), jnp.float32)],
        in_specs=[row, row, row, vec], out_specs=[row, vec], sem=("arbitrary",))(d_fn, dout, h1, g_ffn)


def _mem_bwd(proj, g_mq, km, vmm, d_y, y_m, lse):
    s = proj.shape[0]
    ml, hw = km.shape
    tm = min(FFN_TILE, s)
    scale = MEM_DIM ** -0.5

    def body(q_ref, g_ref, k_ref, v_ref, do_ref, y_ref, lse_ref, dq_ref, dk_ref, dv_ref, dg_ref):
        i = pl.program_id(0)

        @pl.when(i == 0)
        def _():
            dk_ref[...] = jnp.zeros_like(dk_ref)
            dv_ref[...] = jnp.zeros_like(dv_ref)
            dg_ref[...] = jnp.zeros_like(dg_ref)

        col = lax.broadcasted_iota(jnp.int32, (tm, MEM_HEADS), 1)
        lse_t = lse_ref[...]
        for h in range(MEM_HEADS):
            sl = slice(MEM_DIM * h, MEM_DIM * (h + 1))
            qn, xn, r = _norm_fwd(q_ref[:, sl], g_ref[...])
            lse_h = jnp.sum(jnp.where(col == h, lse_t, 0.0), -1, keepdims=True)
            p = jnp.exp(_dot_nt(qn, k_ref[:, sl]) * scale - lse_h)
            do = do_ref[:, sl]
            dd = jnp.sum(do * y_ref[:, sl], -1, keepdims=True)
            dp = _dot_nt(do, v_ref[:, sl])
            ds = (p * (dp - dd)).astype(MXU)
            dv_ref[:, sl] += _dot_tn(p, do)
            dk_ref[:, sl] += _dot_tn(ds, qn) * scale
            dx, dg = _norm_bwd(xn, r, g_ref[...], _dot(ds, k_ref[:, sl]) * scale)
            dq_ref[:, sl] = dx.astype(dq_ref.dtype)
            dg_ref[...] += dg

    full = pl.BlockSpec((ml, hw), lambda i: (0, 0))
    return _pcall(
        body, name="mem_bwd", grid=(s // tm,),
        out_shape=[_sds((s, hw), MXU), _sds((ml, hw), jnp.float32), _sds((ml, hw), jnp.float32),
                   _sds((1, MEM_DIM), jnp.float32)],
        in_specs=[pl.BlockSpec((tm, hw), lambda i: (i, C_QM // hw)), pl.BlockSpec((1, MEM_DIM), lambda i: (0, 0)),
                  full, full, pl.BlockSpec((tm, hw), lambda i: (i, 3)), pl.BlockSpec((tm, hw), lambda i: (i, 0)),
                  pl.BlockSpec((tm, MEM_HEADS), lambda i: (i, 0))],
        out_specs=[pl.BlockSpec((tm, hw), lambda i: (i, 0)), full, full,
                   pl.BlockSpec((1, MEM_DIM), lambda i: (0, 0))],
        sem=("arbitrary",))(proj, g_mq, km, vmm, d_y, y_m, lse)


def _memkv_bwd(mem, g_mem, w_mkv, g_mk, kv, memn, dk, dv):
    ml, d = mem.shape
    hw = MEM_HEADS * MEM_DIM

    def body(mem_ref, g_ref, w_ref, gk_ref, kv_ref, mn_ref, dk_ref, dv_ref, dw_ref, dgm_ref, dgk_ref):
        parts = []
        dgk = jnp.zeros((1, MEM_DIM), jnp.float32)
        for h in range(MEM_HEADS):
            sl = slice(MEM_DIM * h, MEM_DIM * (h + 1))
            _, xn, r = _norm_fwd(kv_ref[:, sl], gk_ref[...])
            dx, dg = _norm_bwd(xn, r, gk_ref[...], dk_ref[:, sl])
            parts.append(dx)
            dgk = dgk + dg
        dkv = jnp.concatenate(parts + [dv_ref[...]], axis=1).astype(MXU)
        dgk_ref[...] = dgk
        dw_ref[...] = _dot_tn(mn_ref[...], dkv).astype(dw_ref.dtype)
        d_mn = _dot_nt(dkv, w_ref[...])
        _, xn, _ = _norm_fwd(mem_ref[...], g_ref[...])
        dgm_ref[...] = jnp.sum(d_mn * xn, 0, keepdims=True)

    vm = pl.BlockSpec(memory_space=pltpu.VMEM)
    return _pcall(
        body, name="memkv_bwd",
        out_shape=[_sds((d, 2 * hw), WIRE), _sds((1, d), jnp.float32), _sds((1, MEM_DIM), jnp.float32)],
        in_specs=[vm] * 8, out_specs=[vm] * 3)(mem, g_mem, w_mkv, g_mk, kv, memn, dk, dv)


def _mla_bwd(qc, kc, v, d_y, y_b, lse, after):
    nh, s, _ = qc.shape
    t = min(ATT_TILE, s)
    nb = s // t
    scale = (MLA_NOPE + MLA_ROPE) ** -0.5

    def body(q_ref, k_ref, v_ref, do_ref, y_ref, lse_ref, dq_ref, dk_ref, dv_ref, dk_acc, dv_acc):
        kj, qi = pl.program_id(1), pl.program_id(2)

        @pl.when((kj == 0) & (qi == 0))
        def _():
            dq_ref[...] = jnp.zeros_like(dq_ref)

        @pl.when(qi == kj)
        def _():
            dk_acc[...] = jnp.zeros_like(dk_acc)
            dv_acc[...] = jnp.zeros_like(dv_acc)

        def step(diagonal):
            rc = t // 4 if diagonal else t
            for c in range(t // rc):
                rows = slice(rc * c, rc * (c + 1))
                keys = slice(0, rc * (c + 1))
                q, k = q_ref[0, rows, :], k_ref[0, keys, :]
                sc = _dot_nt(q, k) * (scale * LOG2E)
                if diagonal:
                    r_i = lax.broadcasted_iota(jnp.int32, sc.shape, 0) + rc * c
                    c_i = lax.broadcasted_iota(jnp.int32, sc.shape, 1)
                    sc = jnp.where(c_i <= r_i, sc, NEG_INF)
                p = jnp.exp2(sc - lse_ref[0, rows, :])
                do = do_ref[rows, :]
                dd = jnp.sum(do * y_ref[rows, :], -1, keepdims=True)
                dp = _dot_nt(do, v_ref[0, keys, :])
                ds = (p * (dp - dd) * scale).astype(MXU)
                dv_acc[keys, :] += _dot_tn(p, do)
                dk_acc[keys, :] += _dot_tn(ds, q)
                out_rows = pl.ds(pl.multiple_of(qi * t + rc * c, rc), rc)
                dq_ref[0, out_rows, :] += _dot(ds, k)

        @pl.when(qi > kj)
        def _():
            step(False)

        @pl.when(qi == kj)
        def _():
            step(True)

        @pl.when(qi == nb - 1)
        def _():
            dk_ref[0] = dk_acc[...]
            dv_ref[0] = dv_acc[...]

    qmap = lambda h, j, i: (h, jnp.maximum(i, j), 0)
    return _pcall(
        body, name="mla_bwd", grid=(nh, nb, nb),
        out_shape=[_sds((nh, s, 256), jnp.float32), _sds((nh, s, 256), jnp.float32),
                   _sds((nh, s, MLA_V), jnp.float32)],
        in_specs=[pl.BlockSpec((1, t, 256), qmap),
                  pl.BlockSpec((1, t, 256), lambda h, j, i: (h, j, 0)),
                  pl.BlockSpec((1, t, MLA_V), lambda h, j, i: (h, j, 0)),
                  pl.BlockSpec((t, MLA_V), lambda h, j, i: (jnp.maximum(i, j), 8 + h)),
                  pl.BlockSpec((t, MLA_V), lambda h, j, i: (jnp.maximum(i, j), h)),
                  pl.BlockSpec((1, t, 1), qmap)],
        out_specs=[pl.BlockSpec((1, s, 256), lambda h, j, i: (h, 0, 0)),
                   pl.BlockSpec((1, t, 256), lambda h, j, i: (h, j, 0)),
                   pl.BlockSpec((1, t, MLA_V), lambda h, j, i: (h, j, 0))],
        scratch=[pltpu.VMEM((t, 256), jnp.float32), pltpu.VMEM((t, MLA_V), jnp.float32)],
        sem=("parallel", "arbitrary", "arbitrary"), after=after)(qc, kc, v, d_y, y_b, lse)


def _mla_prep_bwd(proj, cos, sin, g_cq, g_ckv, w_uq, w_ukv, g_qn, g_qr, g_kn, g_kr,
                  qb, kvb, cqn, ckvn, dqc, dkc, dv):
    s = proj.shape[0]
    tm = min(ROW_TILE, s)
    nh = MLA_HEADS
    ni = s // tm

    def body(cq_ref, ckv_ref, kr_ref, cos_ref, sin_ref, gcq_ref, gckv_ref, wuq_ref, wukv_ref,
             gqn_ref, gqr_ref, gkn_ref, gkr_ref, qb_ref, kvb_ref, cqn_ref, ckvn_ref, dqc_ref, dkc_ref, dv_ref,
             dcq_ref, dckv_ref, dkr_ref, dwuq_ref, dwukv_ref,
             dgcq_ref, dgckv_ref, dgqn_ref, dgqr_ref, dgkn_ref, dgkr_ref, acc_uq, acc_ukv):
        i = pl.program_id(0)

        @pl.when(i == 0)
        def _():
            acc_uq[...] = jnp.zeros_like(acc_uq)
            acc_ukv[...] = jnp.zeros_like(acc_ukv)
            for ref in (dgcq_ref, dgckv_ref, dgqn_ref, dgqr_ref, dgkn_ref, dgkr_ref):
                ref[...] = jnp.zeros_like(ref)

        cos_t, sin_t = cos_ref[...], sin_ref[...]
        lo = _lo_mask((tm, LANES))
        qb_v, kvb_v = qb_ref[...], kvb_ref[...]
        dq_parts, dgqn = [], jnp.zeros((1, LANES), jnp.float32)
        for h in range(nh):
            _, xn, r = _norm_fwd(qb_v[:, MLA_NOPE * h: MLA_NOPE * (h + 1)], gqn_ref[...])
            dx, dg = _norm_bwd(xn, r, gqn_ref[...], dqc_ref[h][:, :MLA_NOPE])
            dq_parts.append(dx)
            dgqn = dgqn + dg
        dgqn_ref[...] += dgqn
        dgqr = jnp.zeros((1, LANES), jnp.float32)
        for j in range(nh // 2):
            d_rope = jnp.where(lo, dqc_ref[2 * j][:, MLA_NOPE:], dqc_ref[2 * j + 1][:, MLA_NOPE:])
            d_pre = _rope_bwd(d_rope, cos_t, sin_t)
            xr = qb_v[:, nh * MLA_NOPE + LANES * j: nh * MLA_NOPE + LANES * (j + 1)]
            _, xn, r = _norm_fwd(xr, gqr_ref[...], half=True)
            dx, dg = _norm_bwd(xn, r, gqr_ref[...], d_pre, half=True)
            dq_parts.append(dx)
            dgqr = dgqr + dg
        dgqr_ref[...] += dgqr
        dqb = jnp.concatenate(dq_parts, axis=1).astype(MXU)
        acc_uq[...] += _dot_tn(dqb, cqn_ref[...])
        _, xn, r = _norm_fwd(cq_ref[...], gcq_ref[...])
        dx, dg = _norm_bwd(xn, r, gcq_ref[...], _dot(dqb, wuq_ref[...]))
        dcq_ref[...] = dx.astype(dcq_ref.dtype)
        dgcq_ref[...] += dg
        dkv_parts, dgkn = [], jnp.zeros((1, LANES), jnp.float32)
        d_kr2 = jnp.zeros((tm, LANES), jnp.float32)
        for h in range(nh):
            _, xn, r = _norm_fwd(kvb_v[:, 256 * h: 256 * h + MLA_NOPE], gkn_ref[...])
            dx, dg = _norm_bwd(xn, r, gkn_ref[...], dkc_ref[h][:, :MLA_NOPE])
            dkv_parts += [dx, dv_ref[h]]
            dgkn = dgkn + dg
            d_kr2 = d_kr2 + dkc_ref[h][:, MLA_NOPE:]
        dgkn_ref[...] += dgkn
        dkvb = jnp.concatenate(dkv_parts, axis=1).astype(MXU)
        part_ukv = _dot_tn(ckvn_ref[...], dkvb)
        for dev in range(N_DEV):
            acc_ukv[dev] += part_ukv[:, LANES * dev: LANES * (dev + 1)]
        w_ukv_full = jnp.concatenate([wukv_ref[dev] for dev in range(N_DEV)], axis=1)
        d_ckvn = _dot_nt(dkvb, w_ukv_full)
        _, xn, r = _norm_fwd(ckv_ref[...], gckv_ref[...])
        dx, dg = _norm_bwd(xn, r, gckv_ref[...], d_ckvn)
        dckv_ref[...] = dx.astype(dckv_ref.dtype)
        dgckv_ref[...] += dg
        d_kr = jnp.where(lo, d_kr2 + pltpu.roll(d_kr2, 64, 1), 0.0)
        d_pre = _rope_bwd(d_kr, cos_t, sin_t)
        _, xn, r = _norm_fwd(kr_ref[...], gkr_ref[...], half=True)
        dx, dg = _norm_bwd(xn, r, gkr_ref[...], d_pre, half=True)
        dkr_ref[...] = jnp.where(lo, dx, 0.0).astype(dkr_ref.dtype)
        dgkr_ref[...] += jnp.where(_lo_mask((1, LANES)), dg, 0.0)

        @pl.when(i == ni - 1)
        def _():
            dwuq_ref[...] = acc_uq[...].astype(dwuq_ref.dtype)
            dwukv_ref[...] = acc_ukv[...].astype(dwukv_ref.dtype)

    def col(width, start):
        return pl.BlockSpec((tm, width), lambda i: (i, start // width))

    def full(shape):
        return pl.BlockSpec(shape, lambda i: (0,) * len(shape))

    def row(width):
        return pl.BlockSpec((tm, width), lambda i: (i, 0))

    def heads(width):
        return pl.BlockSpec((nh, tm, width), lambda i: (0, i, 0))

    vec = full((1, LANES))
    return _pcall(
        body, name="mla_prep_bwd", grid=(ni,),
        out_shape=[_sds((s, 512), MXU), _sds((s, 512), MXU), _sds((s, LANES), MXU),
                   _sds((768, 512), WIRE), _sds((N_DEV, 512, LANES), WIRE),
                   _sds((1, 512), jnp.float32), _sds((1, 512), jnp.float32)] + [_sds((1, LANES), jnp.float32)] * 4,
        in_specs=[col(512, C_CQ), col(512, C_CKV), col(LANES, C_KR), row(LANES), row(LANES),
                  full((1, 512)), full((1, 512)), full((768, 512)), full((N_DEV, 512, LANES)), vec, vec, vec, vec,
                  row(768), row(1024), row(512), row(512), heads(256), heads(256), heads(MLA_V)],
        out_specs=[row(512), row(512), row(LANES), full((768, 512)), full((N_DEV, 512, LANES)),
                   full((1, 512)), full((1, 512)), vec, vec, vec, vec],
        scratch=[pltpu.VMEM((768, 512), jnp.float32), pltpu.VMEM((N_DEV, 512, LANES), jnp.float32)],
        sem=("arbitrary",))(proj, proj, proj, cos, sin, g_cq, g_ckv, w_uq, w_ukv, g_qn, g_qr, g_kn, g_kr,
                            qb, kvb, cqn, ckvn, dqc, dkc, dv)


def _swa_bwd(proj, posc, posr, gq, gk, sinks, d_y, y_a, lse, after):
    s = proj.shape[0]
    b = SWA_BLOCK
    nb = s // b
    scale = SWA_DIM ** -0.5

    def body(q_ref, kp_ref, kc_ref, vp_ref, vc_ref, pq_ref, pkp_ref, pkc_ref, gq_ref, gk_ref, sink_ref,
             do_ref, y_ref, lse_ref, kfull_ref,
             dq_ref, dk_ref, dv_ref, dgq_ref, dgk_ref, dsink_ref, dk_acc, dv_acc):
        n = pl.program_id(0)

        @pl.when(n == 0)
        def _():
            dk_acc[...] = jnp.zeros_like(dk_acc)
            dv_acc[...] = jnp.zeros_like(dv_acc)
            dgq_ref[...] = jnp.zeros_like(dgq_ref)
            dsink_ref[...] = jnp.zeros_like(dsink_ref)

        kn, v, bias = _swa_common(n, kp_ref[...], kc_ref[...], vp_ref[...], vc_ref[...],
                                  pq_ref[...], pkp_ref[...], pkc_ref[...], gk_ref[...])
        lo = _lo_mask((b, LANES))
        col = lax.broadcasted_iota(jnp.int32, (b, SWA_Q_HEADS), 1)
        col1 = lax.broadcasted_iota(jnp.int32, (1, SWA_Q_HEADS), 1)
        lse_t = lse_ref[...]
        dk_blk = jnp.zeros((2 * b, LANES), jnp.float32)
        dv_blk = jnp.zeros((2 * b, LANES), jnp.float32)
        dgq = jnp.zeros((1, LANES), jnp.float32)
        dsink = jnp.zeros((1, SWA_Q_HEADS), jnp.float32)
        for j in range(SWA_Q_HEADS // 2):
            hk = (2 * j) // (SWA_Q_HEADS // SWA_KV_HEADS)
            kvmask = lo if hk == 0 else jnp.logical_not(lo)
            sl = slice(LANES * j, LANES * (j + 1))
            qn, xn, r = _norm_fwd(q_ref[:, sl], gq_ref[...], half=True)
            qn = qn * scale
            qsw = pltpu.roll(qn, 64, 1)
            d2 = do_ref[:, sl]
            d2sw = pltpu.roll(d2, 64, 1)
            prod = d2 * y_ref[:, sl]
            dqs = []
            for e in range(2):
                h = 2 * j + e
                half_e = lo if e == 0 else jnp.logical_not(lo)
                qm = jnp.where(kvmask, qn if e == hk else qsw, 0.0)
                dm = jnp.where(kvmask, d2 if e == hk else d2sw, 0.0)
                sc = _dot_nt(qm, kn) + _alibi_slope(h) * bias
                lse_h = jnp.sum(jnp.where(col == h, lse_t, 0.0), -1, keepdims=True)
                p = jnp.exp(sc - lse_h)
                dd = jnp.sum(jnp.where(half_e, prod, 0.0), -1, keepdims=True)
                dp = _dot_nt(dm, v)
                ds = (p * (dp - dd)).astype(MXU)
                dsink = dsink - jnp.where(col1 == h, jnp.sum(jnp.exp(sink_ref[h] - lse_h) * dd), 0.0)
                dq_m = _dot(ds, kn) * scale
                dk_blk = dk_blk + _dot_tn(ds, qm)
                dv_blk = dv_blk + _dot_tn(p, dm)
                dqs.append(dq_m if e == hk else pltpu.roll(dq_m, 64, 1))
            dx, dg = _norm_bwd(xn, r, gq_ref[...], jnp.where(lo, dqs[0], dqs[1]), half=True)
            dq_ref[:, sl] = dx.astype(dq_ref.dtype)
            dgq = dgq + dg
        dgq_ref[...] += dgq
        dsink_ref[...] += dsink
        prev = pl.ds(pl.multiple_of(jnp.maximum(n - 1, 0) * b, b), b)
        cur = pl.ds(pl.multiple_of(n * b, b), b)
        dk_acc[prev, :] += dk_blk[:b]
        dv_acc[prev, :] += dv_blk[:b]
        dk_acc[cur, :] += dk_blk[b:]
        dv_acc[cur, :] += dv_blk[b:]

        @pl.when(n == nb - 1)
        def _():
            _, kxn, kr = _norm_fwd(kfull_ref[...], gk_ref[...], half=True)
            dx, dg = _norm_bwd(kxn, kr, gk_ref[...], dk_acc[...], half=True)
            dk_ref[...] = dx.astype(dk_ref.dtype)
            dv_ref[...] = dv_acc[...].astype(dv_ref.dtype)
            dgk_ref[...] = dg

    full = pl.BlockSpec((s, LANES), lambda n: (0, 0))
    vec = pl.BlockSpec((1, LANES), lambda n: (0, 0))
    return _pcall(
        body, name="swa_bwd", grid=(nb,),
        out_shape=[_sds((s, 1024), MXU), _sds((s, LANES), MXU), _sds((s, LANES), MXU),
                   _sds((1, LANES), jnp.float32), _sds((1, LANES), jnp.float32),
                   _sds((1, SWA_Q_HEADS), jnp.float32)],
        in_specs=_swa_specs(s) + [pl.BlockSpec((b, 1024), lambda n: (n, 0)), pl.BlockSpec((b, 1024), lambda n: (n, 0)),
                                  pl.BlockSpec((b, SWA_Q_HEADS), lambda n: (n, 0)),
                                  pl.BlockSpec((s, LANES), lambda n: (0, C_KA // LANES))],
        out_specs=[pl.BlockSpec((b, 1024), lambda n: (n, 0)), full, full, vec, vec,
                   pl.BlockSpec((1, SWA_Q_HEADS), lambda n: (0, 0))],
        scratch=[pltpu.VMEM((s, LANES), jnp.float32), pltpu.VMEM((s, LANES), jnp.float32)],
        sem=("arbitrary",), after=after)(proj, proj, proj, proj, proj, posc, posr, posr, gq, gk, sinks, d_y, y_a, lse,
                                         proj)


def _dx(d_proj, w_in, x, g, d_h1, after):
    s, d = x.shape
    n = w_in.shape[0]
    tm = min(2 * ROW_TILE, s)

    n_pc = len(d_proj)

    def body(*refs):
        dp_refs, (w_ref, x_ref, g_ref, dh_ref, dx_ref, dg_ref) = refs[:n_pc], refs[n_pc:]
        i = pl.program_id(0)

        @pl.when(i == 0)
        def _():
            dg_ref[...] = jnp.zeros_like(dg_ref)

        d_hn = _dot(jnp.concatenate([r[...] for r in dp_refs], axis=1), w_ref[...])
        _, xn, r = _norm_fwd(x_ref[...], g_ref[...])
        dx, dg = _norm_bwd(xn, r, g_ref[...], d_hn)
        dx_ref[...] = dh_ref[...] + dx
        dg_ref[...] += dg

    row = pl.BlockSpec((tm, d), lambda i: (i, 0))
    vec = pl.BlockSpec((1, d), lambda i: (0, 0))
    return _pcall(
        body, name="grad_x", grid=(s // tm,),
        out_shape=[_sds((s, d), jnp.float32), _sds((1, d), jnp.float32)],
        in_specs=[pl.BlockSpec((tm, p.shape[1]), lambda i: (i, 0)) for p in d_proj] + [
                  pl.BlockSpec((n, d), lambda i: (0, 0), pipeline_mode=pl.Buffered(1)), row, vec, row],
        out_specs=[row, vec], sem=("arbitrary",), after=after)(*d_proj, w_in, x, g, d_h1)


_SMALL = ["attn_norm_g", "swa_q_norm_g", "swa_k_norm_g", "swa_sinks", "mla_cq_norm_g", "mla_ckv_norm_g",
          "mla_qn_norm_g", "mla_qr_norm_g", "mla_kn_norm_g", "mla_kr_norm_g", "mem_norm_g",
          "mem_q_norm_g", "mem_k_norm_g", "ffn_norm_g"]


def kernel(x, mem, positions, attn_norm_g, w_in, swa_q_norm_g, swa_k_norm_g, swa_sinks, mla_cq_norm_g, mla_ckv_norm_g, w_uq, w_ukv, mla_qn_norm_g, mla_qr_norm_g, mla_kn_norm_g, mla_kr_norm_g, mem_norm_g, w_mem_kv, mem_q_norm_g, mem_k_norm_g, w_out, ffn_norm_g, w_gate, w_up, w_down, loss_target, m_attn_norm_g, m_w_in, m_swa_q_norm_g, m_swa_k_norm_g, m_swa_sinks, m_mla_cq_norm_g, m_mla_ckv_norm_g, m_w_uq, m_w_ukv, m_mla_qn_norm_g, m_mla_qr_norm_g, m_mla_kn_norm_g, m_mla_kr_norm_g, m_mem_norm_g, m_w_mem_kv, m_mem_q_norm_g, m_mem_k_norm_g, m_w_out, m_ffn_norm_g, m_w_gate, m_w_up, m_w_down, v_attn_norm_g, v_w_in, v_swa_q_norm_g, v_swa_k_norm_g, v_swa_sinks, v_mla_cq_norm_g, v_mla_ckv_norm_g, v_w_uq, v_w_ukv, v_mla_qn_norm_g, v_mla_qr_norm_g, v_mla_kn_norm_g, v_mla_kr_norm_g, v_mem_norm_g, v_w_mem_kv, v_mem_q_norm_g, v_mem_k_norm_g, v_w_out, v_ffn_norm_g, v_w_gate, v_w_up, v_w_down):
    args = dict(locals())
    x2, mem2, tgt = x[0], mem[0], loss_target[0]
    s, d = x2.shape
    n_in = w_in.shape[2]
    f = w_gate.shape[2]

    in_shards = [w_in[0].T.astype(WIRE)]
    (g_in,) = _all_gather_background(in_shards, 7, "all_gather_in_weights")
    tok = in_shards[0]
    mix_shards = [w_uq[0].T.astype(WIRE), w_ukv[0].astype(WIRE), w_mem_kv[0].astype(WIRE),
                  _to_wire([w_out[0]], tok, "wire_out")[0]]
    g_uq, wkv, g_mkv, g_out = _all_gather_background(mix_shards, 5, "all_gather_mix_weights")
    ffn_shards = [_to_wire([w_gate[0].T, w_up[0].T], tok, "wire_gate_up")]
    (w_gu,) = _all_gather_background(ffn_shards, 1, "all_gather_ffn_weights")
    down_shards = [_to_wire([w_down[0]], tok, "wire_down")[0]]
    (w_d,) = _all_gather_background(down_shards, 6, "all_gather_down_weights")
    wi = g_in.reshape(N_DEV * n_in, d)
    wi = jnp.concatenate([wi[0:1024], wi[1280:1792], wi[1792:2304], wi[2368:2880],
                          wi[1024:1152], wi[1152:1280], wi[2304:2368],
                          jnp.zeros((IN_PAD - 2880, d), wi.dtype)], axis=0)
    wq = g_uq.reshape(768, 512)
    wq = jnp.concatenate([wq[192 * h: 192 * h + 128] for h in range(4)]
                         + [wq[192 * h + 128: 192 * (h + 1)] for h in range(4)], axis=0)
    wmkv = g_mkv.reshape(-1, g_mkv.shape[-1])
    wo = g_out.reshape(-1, d)

    pos = positions[0].astype(jnp.float32)
    inv_freq = ROPE_THETA ** (-jnp.arange(0, MLA_ROPE, 2, dtype=jnp.float32) / MLA_ROPE)
    ang = pos[:, None] * inv_freq
    cos32, sin32 = jnp.cos(ang), jnp.sin(ang)
    cos_t = jnp.tile(cos32, (1, 4))
    sin_t = jnp.tile(jnp.concatenate([-sin32, sin32], axis=1), (1, 2))
    posc, posr = pos.reshape(s, 1), pos.reshape(1, s)
    two = lambda g: jnp.tile(g, (1, 2))
    gq2, gk2, gqr2, gkr2 = two(swa_q_norm_g), two(swa_k_norm_g), two(mla_qr_norm_g), two(mla_kr_norm_g)
    sinks1 = swa_sinks[0]

    hn = _norm_rows(x2, attn_norm_g)
    proj = _mm(hn, wi, tb=True, out_dtype=jnp.float32, tm=FFN_TILE, tk=d, name="in_proj")
    qc, kc, vb, qb, kvb, cqn, ckvn = _mla_prep(proj, cos_t, sin_t, mla_cq_norm_g, mla_ckv_norm_g, wq, wkv,
                                                mla_qn_norm_g, gqr2, mla_kn_norm_g, gkr2)
    y_b, lse_b = _mla_fwd(qc, kc, vb)
    km, vmm, kvm, memn = _memkv_prep(mem2, mem_norm_g, wmkv, mem_k_norm_g)
    y_m, lse_m = _mem_fwd(proj, mem_q_norm_g, km, vmm)
    y_a, lse_a = _swa_fwd(proj, posc, posr, gq2, gk2, sinks1)
    h1, fn = _out_proj(y_a, y_b, y_m, x2, wo, ffn_norm_g)
    gu, act = _ffn_gu(fn, w_gu)
    dout, dout_b, loss_tile = _ffn_down(act, w_d, h1, tgt)

    dgu, dw_d = _ffn_bwd_act(dout_b, w_d, gu)
    (r_d,) = _exchange_grads_background([dw_d], 8, "exchange_down_grads")
    dw_gu = _ffn_dw_gu(fn, dgu)
    (r_gu,) = _exchange_grads_background([dw_gu], 2, "exchange_ffn_grads")
    d_h1, dg_ffn = _ffn_norm_bwd(_ffn_dfn(dgu, w_gu, dw_gu), dout, h1, ffn_norm_g)
    d_y = _mm(d_h1, wo, tb=True, out_dtype=jnp.float32, tm=FFN_TILE, tk=2048, name="d_mix")
    dw_out = jnp.concatenate([
        _mm(y_a, d_h1, ta=True, out_dtype=WIRE, tm=1024, tk=1024, name="dw_out_a"),
        _mm(y_b, d_h1, ta=True, out_dtype=WIRE, tm=1024, tk=1024, name="dw_out_b"),
        _mm(y_m, d_h1, ta=True, out_dtype=WIRE, tm=1024, tk=1024, name="dw_out_m")], axis=0)
    d_qm, dkm, dvmm, dg_mq = _mem_bwd(proj, mem_q_norm_g, km, vmm, d_y, y_m, lse_m)
    dw_mkv, dg_mem, dg_mk = _memkv_bwd(mem2, mem_norm_g, wmkv, mem_k_norm_g, kvm, memn, dkm, dvmm)
    r_mkv, r_out = _exchange_grads_background([dw_mkv.reshape(g_mkv.shape), dw_out.reshape(g_out.shape)], 3,
                                              "exchange_mix_grads")
    dqc, dkc, dvb = _mla_bwd(qc, kc, vb, d_y, y_b, lse_b, dw_mkv)
    (d_cq, d_ckv, d_kr, dw_uq, dw_ukv, dg_cq, dg_ckv, dg_qn, dg_qr, dg_kn, dg_kr) = _mla_prep_bwd(
        proj, cos_t, sin_t, mla_cq_norm_g, mla_ckv_norm_g, wq, wkv, mla_qn_norm_g, gqr2, mla_kn_norm_g, gkr2,
        qb, kvb, cqn, ckvn, dqc, dkc, dvb)
    d_qa, d_ka, d_va, dg_q, dg_k, d_sinks = _swa_bwd(proj, posc, posr, gq2, gk2, sinks1, d_y, y_a, lse_a, dw_out)
    d_proj = [d_qa, d_cq, d_ckv, d_qm, d_ka, d_va, d_kr]
    grad_x, dg_attn = _dx(d_proj, wi, x2, attn_norm_g, d_h1, None)
    gi = _dw_in(hn, d_proj, n_in, grad_x)

    gq_ = jnp.concatenate(sum([[dw_uq[128 * h: 128 * (h + 1)], dw_uq[512 + 64 * h: 512 + 64 * (h + 1)]]
                               for h in range(4)], []), axis=0)
    gq_ = gq_.reshape(N_DEV, 96, 512)
    small_g = {
        "attn_norm_g": dg_attn, "swa_q_norm_g": dg_q, "swa_k_norm_g": dg_k,
        "swa_sinks": d_sinks, "mla_cq_norm_g": dg_cq, "mla_ckv_norm_g": dg_ckv, "mla_qn_norm_g": dg_qn,
        "mla_qr_norm_g": dg_qr, "mla_kn_norm_g": dg_kn, "mla_kr_norm_g": dg_kr,
        "mem_norm_g": dg_mem, "mem_q_norm_g": dg_mq, "mem_k_norm_g": dg_mk, "ffn_norm_g": dg_ffn}
    pack = _small_pack([small_g[n] for n in _SMALL], loss_tile, [args[n].shape[-1] for n in _SMALL])
    pack8 = jnp.broadcast_to(pack, (N_DEV,) + pack.shape[1:])
    r_in, r_uq, r_ukv, packs = _exchange_grads_background([gi, gq_, dw_ukv, pack8], 4, "exchange_in_grads")

    big = {}
    last = [None]

    def adam(name, r, transposed=False, which=None):
        w, m, v = args[name][0], args["m_" + name][0], args["v_" + name][0]
        if transposed:
            outs = _adam_big(r, w.T, m.T, v.T, "adam_" + name, last[0], which)
            big[name] = [o.T[None] for o in outs]
        else:
            outs = _adam_big(r, w, m, v, "adam_" + name, last[0])
            big[name] = [o[None] for o in outs]
        last[0] = outs[0]

    adam("w_down", r_d)
    adam("w_gate", r_gu, True, which=0)
    adam("w_up", r_gu, True, which=1)
    adam("w_out", r_out)
    adam("w_mem_kv", r_mkv)
    adam("w_in", r_in, True)
    adam("w_uq", r_uq, True)
    adam("w_ukv", r_ukv)

    loss11, small_out = _small_adam(packs, [args[n] for n in _SMALL], [args["m_" + n] for n in _SMALL],
                                    [args["v_" + n] for n in _SMALL], last[0])
    small = dict(zip(_SMALL, small_out))
    loss = loss11.reshape(())

    order = ["attn_norm_g", "w_in", "swa_q_norm_g", "swa_k_norm_g", "swa_sinks", "mla_cq_norm_g", "mla_ckv_norm_g",
             "w_uq", "w_ukv", "mla_qn_norm_g", "mla_qr_norm_g", "mla_kn_norm_g", "mla_kr_norm_g", "mem_norm_g",
             "w_mem_kv", "mem_q_norm_g", "mem_k_norm_g", "w_out", "ffn_norm_g", "w_gate", "w_up", "w_down"]
    res = {n: (big[n] if n in big else list(small[n])) for n in order}
    outs = [loss, grad_x[None]]
    for kind in range(4):
        outs += [res[n][kind] for n in order]
    return tuple(outs)
```

```python
import jax
import jax.numpy as jnp
from jax import lax
from jax.experimental import pallas as pl
from jax.experimental.pallas import tpu as pltpu
from jax.experimental.pallas import tpu_sc as plsc

MXU = jnp.bfloat16
WIRE = jnp.bfloat16
EPS = 1e-6
NEG_INF = -1e30
LOG2E = 1.4426950408889634
N_DEV = 8
LANES = 128
ROW_TILE = 256
FFN_TILE = 512
ATT_TILE = 1024
SWA_BLOCK = 128
VMEM_LIMIT = 56 * 1024 * 1024

SWA_Q_HEADS, SWA_KV_HEADS, SWA_DIM = 16, 2, 64
MLA_HEADS, MLA_NOPE, MLA_ROPE, MLA_V = 4, 128, 64, 128
MEM_HEADS, MEM_DIM = 4, 128
ROPE_THETA = 10000.0
ADAM_LR, ADAM_B1, ADAM_B2, ADAM_EPS, ADAM_WD, ADAM_STEP = 0.001, 0.9, 0.999, 1e-08, 0.01, 10

C_QA, C_CQ, C_CKV, C_QM, C_KA, C_VA, C_KR, IN_PAD = 0, 1024, 1536, 2048, 2560, 2688, 2816, 2944


def _pcall(body, *, name, out_shape, in_specs, out_specs, grid=(), scratch=(), sem=None, after=None):
    params = pltpu.CompilerParams(dimension_semantics=sem, vmem_limit_bytes=VMEM_LIMIT)
    if after is not None:
        n_in, inner = len(in_specs), body

        def body(*refs):
            inner(*refs[:n_in], *refs[n_in + 1:])

        in_specs = list(in_specs) + [pl.BlockSpec(memory_space=pl.ANY)]
    call = pl.pallas_call(body, name=name, grid=grid, in_specs=in_specs, out_specs=out_specs,
                          out_shape=out_shape, scratch_shapes=list(scratch), compiler_params=params)
    return call if after is None else (lambda *ops: call(*ops, after))


def _sds(shape, dtype):
    return jax.ShapeDtypeStruct(tuple(shape), dtype)


def _dot(a, b):
    return jnp.dot(a.astype(MXU), b.astype(MXU), preferred_element_type=jnp.float32)


def _dot_nt(a, b):
    return lax.dot_general(a.astype(MXU), b.astype(MXU), (((1,), (1,)), ((), ())),
                           preferred_element_type=jnp.float32)


def _dot_tn(a, b):
    return lax.dot_general(a.astype(MXU), b.astype(MXU), (((0,), (0,)), ((), ())),
                           preferred_element_type=jnp.float32)


def _lo_mask(shape):
    return (lax.broadcasted_iota(jnp.int32, shape, len(shape) - 1) % LANES) < 64


def _norm_fwd(x, g, half=False):
    x2 = x * x
    if half:
        lo = _lo_mask(x.shape)
        s_lo = jnp.sum(jnp.where(lo, x2, 0.0), -1, keepdims=True)
        s_hi = jnp.sum(jnp.where(lo, 0.0, x2), -1, keepdims=True)
        r = jnp.where(lo, lax.rsqrt(s_lo / 64.0 + EPS), lax.rsqrt(s_hi / 64.0 + EPS))
    else:
        r = lax.rsqrt(jnp.mean(x2, -1, keepdims=True) + EPS)
    xn = x * r
    return xn * g, xn, r


def _norm_bwd(xn, r, g, dy, half=False):
    t = dy * g
    tx = t * xn
    if half:
        lo = _lo_mask(xn.shape)
        m_lo = jnp.sum(jnp.where(lo, tx, 0.0), -1, keepdims=True) / 64.0
        m_hi = jnp.sum(jnp.where(lo, 0.0, tx), -1, keepdims=True) / 64.0
        m = jnp.where(lo, m_lo, m_hi)
    else:
        m = jnp.mean(tx, -1, keepdims=True)
    dx = r * (t - xn * m)
    dg = jnp.sum(dy * xn, 0, keepdims=True)
    return dx, dg


def _swap32(x):
    lane = lax.broadcasted_iota(jnp.int32, x.shape, 1)
    return jnp.where((lane % 64) < 32, pltpu.roll(x, 96, 1), pltpu.roll(x, 32, 1))


def _rope(x, cos, sin):
    return x * cos + _swap32(x) * sin


def _rope_bwd(d, cos, sin):
    return d * cos + _swap32(d * sin)


def _my_coords():
    return lax.axis_index("x"), lax.axis_index("y"), lax.axis_index("c")


def _dev_index(px, py, pc):
    return 4 * px + 2 * py + pc


_FLIPS = [(0, 0, 1), (0, 1, 0), (0, 1, 1), (1, 0, 0), (1, 0, 1), (1, 1, 0), (1, 1, 1)]


def _flip(coords, f):
    return tuple((1 - v) if b else v for v, b in zip(coords, f))


def _all_gather(shards):
    n = len(shards)

    def body(*refs):
        ins, outs = refs[:n], refs[n:2 * n]
        send_sems, recv_sems, local_sems = refs[2 * n:]
        x, y, c = _my_coords()
        me, sibling = (x, y, c), (x, y, 1 - c)
        chips = [(1 - x, y), (x, 1 - y), (1 - x, 1 - y)]

        def copy(w, k, block, to, src=None):
            dst = outs[w].at[_dev_index(*block)]
            return pltpu.make_async_remote_copy(
                src_ref=dst if src is None else src, dst_ref=dst,
                send_sem=send_sems.at[w, k], recv_sem=recv_sems.at[w, k],
                device_id=to, device_id_type=pl.DeviceIdType.MESH)

        sends, locals_ = [], []
        for w in range(n):
            mine = pltpu.make_async_copy(ins[w], outs[w].at[_dev_index(*me)], local_sems.at[w])
            mine.start()
            locals_.append(mine)
            first = [copy(w, 0, me, sibling, src=ins[w])]
            first += [copy(w, 1 + j, me, (*chip, c), src=ins[w]) for j, chip in enumerate(chips)]
            for cp in first:
                cp.start()
            sends += first
        for w in range(n):
            for j, chip in enumerate(chips):
                copy(w, 1 + j, (*chip, c), me).wait_recv()
                fwd = copy(w, 4 + j, (*chip, c), sibling)
                fwd.start()
                sends.append(fwd)
        for w in range(n):
            copy(w, 0, sibling, me).wait_recv()
            for j, chip in enumerate(chips):
                copy(w, 4 + j, (*chip, 1 - c), me).wait_recv()
        for cp in sends:
            cp.wait_send()
        for mine in locals_:
            mine.wait()

    any_spec = pl.BlockSpec(memory_space=pl.ANY)
    return _pcall(
        body, name="all_gather_weights",
        out_shape=[_sds((N_DEV,) + s.shape, s.dtype) for s in shards],
        in_specs=[any_spec] * n, out_specs=[any_spec] * n,
        scratch=[pltpu.SemaphoreType.DMA((n, 7)), pltpu.SemaphoreType.DMA((n, 7)),
                 pltpu.SemaphoreType.DMA((n,))])(*shards)


def _wire_cost(arrays):
    nbytes = sum(a.size * a.dtype.itemsize for a in arrays)
    return pl.CostEstimate(flops=0, transcendentals=0, bytes_accessed=40 * nbytes)


def _all_gather_background(shards, collective_id, name):
    n = len(shards)
    src_refs = [jax.new_ref(s, memory_space=pltpu.MemorySpace.HBM) for s in shards]
    out_refs = [jax.empty_ref(_sds((N_DEV,) + s.shape, s.dtype), memory_space=pltpu.MemorySpace.HBM) for s in shards]

    @pl.kernel(mesh=plsc.ScalarSubcoreMesh(axis_name="seq", num_cores=1), name=name,
               scratch_types=(pltpu.SemaphoreType.DMA((n, 7)), pltpu.SemaphoreType.DMA((n, 7)),
                              pltpu.SemaphoreType.DMA((n,))),
               compiler_params=pltpu.CompilerParams(collective_id=collective_id))
    def launch(send_sems, recv_sems, local_sems):
        x, y, c = _my_coords()
        me, sibling = (x, y, c), (x, y, 1 - c)
        chips = [(1 - x, y), (x, 1 - y), (1 - x, 1 - y)]
        barrier = pltpu.get_barrier_semaphore()
        for peer in [sibling] + [(*chip, c) for chip in chips]:
            pl.semaphore_signal(barrier, inc=1, device_id=peer, device_id_type=pl.DeviceIdType.MESH)
        pl.semaphore_wait(barrier, 4)

        def copy(w, k, block, to, src=None):
            dst = out_refs[w].at[_dev_index(*block)]
            return pltpu.make_async_remote_copy(
                src_ref=dst if src is None else src, dst_ref=dst,
                send_sem=send_sems.at[w, k], recv_sem=recv_sems.at[w, k],
                device_id=to, device_id_type=pl.DeviceIdType.MESH)

        sends, locals_ = [], []
        for w in range(n):
            mine = pltpu.make_async_copy(src_refs[w], out_refs[w].at[_dev_index(*me)], local_sems.at[w])
            mine.start()
            locals_.append(mine)
            first = [copy(w, 0, me, sibling, src=src_refs[w])]
            first += [copy(w, 1 + j, me, (*chip, c), src=src_refs[w]) for j, chip in enumerate(chips)]
            for cp in first:
                cp.start()
            sends += first
        for w in range(n):
            for j, chip in enumerate(chips):
                copy(w, 1 + j, (*chip, c), me).wait_recv()
                fwd = copy(w, 4 + j, (*chip, c), sibling)
                fwd.start()
                sends.append(fwd)
        for w in range(n):
            copy(w, 0, sibling, me).wait_recv()
            for j, chip in enumerate(chips):
                copy(w, 4 + j, (*chip, 1 - c), me).wait_recv()
        for cp in sends:
            cp.wait_send()
        for mine in locals_:
            mine.wait()

    launch()
    return [r[...] for r in out_refs]


def _exchange_grads(grads):
    n = len(grads)

    def body(*refs):
        ins, outs = refs[:n], refs[n:2 * n]
        send_sems, recv_sems, local_sems = refs[2 * n:]
        me = _my_coords()
        my_idx = _dev_index(*me)
        sends, locals_ = [], []
        for w in range(n):
            mine = pltpu.make_async_copy(ins[w].at[my_idx], outs[w].at[my_idx], local_sems.at[w])
            mine.start()
            locals_.append(mine)
            for k, f in enumerate(_FLIPS):
                peer = _flip(me, f)
                cp = pltpu.make_async_remote_copy(
                    src_ref=ins[w].at[_dev_index(*peer)], dst_ref=outs[w].at[my_idx],
                    send_sem=send_sems.at[w, k], recv_sem=recv_sems.at[w, k],
                    device_id=peer, device_id_type=pl.DeviceIdType.MESH)
                cp.start()
                sends.append(cp)
        for w in range(n):
            for k, f in enumerate(_FLIPS):
                peer = _flip(me, f)
                slot = outs[w].at[_dev_index(*peer)]
                pltpu.make_async_remote_copy(
                    src_ref=slot, dst_ref=slot,
                    send_sem=send_sems.at[w, k], recv_sem=recv_sems.at[w, k],
                    device_id=peer, device_id_type=pl.DeviceIdType.MESH).wait_recv()
        for cp in sends:
            cp.wait_send()
        for mine in locals_:
            mine.wait()

    any_spec = pl.BlockSpec(memory_space=pl.ANY)
    return _pcall(
        body, name="exchange_grads",
        out_shape=[_sds(g.shape, g.dtype) for g in grads],
        in_specs=[any_spec] * n, out_specs=[any_spec] * n,
        scratch=[pltpu.SemaphoreType.DMA((n, 7)), pltpu.SemaphoreType.DMA((n, 7)),
                 pltpu.SemaphoreType.DMA((n,))])(*grads)


def _exchange_grads_background(grads, collective_id, name):
    n = len(grads)
    src_refs = [jax.new_ref(g, memory_space=pltpu.MemorySpace.HBM) for g in grads]
    out_refs = [jax.empty_ref(_sds(g.shape, g.dtype), memory_space=pltpu.MemorySpace.HBM) for g in grads]

    @pl.kernel(mesh=plsc.ScalarSubcoreMesh(axis_name="seq", num_cores=1), name=name,
               scratch_types=(pltpu.SemaphoreType.DMA((n, 7)), pltpu.SemaphoreType.DMA((n, 7)),
                              pltpu.SemaphoreType.DMA((n,))),
               cost_estimate=_wire_cost(grads),
               compiler_params=pltpu.CompilerParams(collective_id=collective_id))
    def launch(send_sems, recv_sems, local_sems):
        me = _my_coords()
        my_idx = _dev_index(*me)
        peers = [_flip(me, f) for f in _FLIPS]
        barrier = pltpu.get_barrier_semaphore()
        for peer in peers:
            pl.semaphore_signal(barrier, inc=1, device_id=peer, device_id_type=pl.DeviceIdType.MESH)
        pl.semaphore_wait(barrier, len(peers))
        sends, locals_ = [], []
        for w in range(n):
            mine = pltpu.make_async_copy(src_refs[w].at[my_idx], out_refs[w].at[my_idx], local_sems.at[w])
            mine.start()
            locals_.append(mine)
            for k, peer in enumerate(peers):
                cp = pltpu.make_async_remote_copy(
                    src_ref=src_refs[w].at[_dev_index(*peer)], dst_ref=out_refs[w].at[my_idx],
                    send_sem=send_sems.at[w, k], recv_sem=recv_sems.at[w, k],
                    device_id=peer, device_id_type=pl.DeviceIdType.MESH)
                cp.start()
                sends.append(cp)
        for w in range(n):
            for k, peer in enumerate(peers):
                slot = out_refs[w].at[_dev_index(*peer)]
                pltpu.make_async_remote_copy(
                    src_ref=slot, dst_ref=slot, send_sem=send_sems.at[w, k], recv_sem=recv_sems.at[w, k],
                    device_id=peer, device_id_type=pl.DeviceIdType.MESH).wait_recv()
        for cp in sends:
            cp.wait_send()
        for mine in locals_:
            mine.wait()

    launch()
    return [r[...] for r in out_refs]


def _to_wire(parts, after, name):
    n = len(parts)
    rows, cols = parts[0].shape
    tr = rows // 2 if rows % 32 == 0 else rows

    def body(*refs):
        for k in range(n):
            refs[n][k] = refs[k][...].astype(WIRE)

    blk = pl.BlockSpec((tr, cols), lambda i: (i, 0))
    return _pcall(
        body, name=name, grid=(rows // tr,), out_shape=_sds((n, rows, cols), WIRE),
        in_specs=[blk] * n, out_specs=pl.BlockSpec((n, tr, cols), lambda i: (0, i, 0)),
        sem=("parallel",), after=after)(*parts)


def _adam_math(w, g, m, v):
    m = ADAM_B1 * m + (1.0 - ADAM_B1) * g
    v = ADAM_B2 * v + (1.0 - ADAM_B2) * (g * g)
    m_hat = m / (1.0 - ADAM_B1 ** ADAM_STEP)
    v_hat = v / (1.0 - ADAM_B2 ** ADAM_STEP)
    delta = -ADAM_LR * (m_hat / (jnp.sqrt(v_hat) + ADAM_EPS) + ADAM_WD * w)
    return delta, m, v


def _small_layout(sizes):
    row0, r = [], 0
    for n in sizes:
        row0.append(r)
        r += -(-n // LANES)
    return row0, r, -(-(r + 1) // 8) * 8


def _small_pieces(n):
    return [(k, min(LANES, n - LANES * k)) for k in range(-(-n // LANES))]


def _small_fill(pack, slot, srcs, sizes, row0, rows):
    pack[slot] = jnp.zeros((rows, LANES), jnp.float32)
    for p, n in enumerate(sizes):
        val = srcs[p][...]
        if val.shape[-1] == LANES and n == 64:
            pack[slot, row0[p]:row0[p] + 1, :] = val + pltpu.roll(val, 64, 1)
            continue
        for k, width in _small_pieces(n):
            pack[slot, row0[p] + k:row0[p] + k + 1, 0:width] = srcs[p][:, LANES * k:LANES * k + width]


def _small_pack(grads, loss_tile, sizes):
    n_par = len(sizes)
    row0, loss_row, rows = _small_layout(sizes)

    def body(*refs):
        g_refs, loss_in, out_ref = refs[:n_par], refs[n_par], refs[n_par + 1]
        _small_fill(out_ref, 0, g_refs, sizes, row0, rows)
        out_ref[0, loss_row:loss_row + 1, :] = loss_in[0:1, :]

    vm = pl.BlockSpec(memory_space=pltpu.VMEM)
    return _pcall(
        body, name="small_pack", out_shape=_sds((1, rows, LANES), jnp.float32),
        in_specs=[vm] * (n_par + 1), out_specs=vm)(*grads, loss_tile)


def _small_adam(packs, ws, ms, vs, after):
    sizes = [w.shape[-1] for w in ws]
    n_par = len(ws)
    row0, loss_row, rows = _small_layout(sizes)

    def body(*refs):
        g_ref = refs[0]
        w_refs, m_refs, v_refs = (refs[1 + k * n_par: 1 + (k + 1) * n_par] for k in range(3))
        loss_out = refs[3 * n_par + 1]
        out_refs = refs[3 * n_par + 2: 7 * n_par + 2]
        pack, res = refs[7 * n_par + 2:]
        for slot, srcs in enumerate((w_refs, m_refs, v_refs)):
            _small_fill(pack, slot, srcs, sizes, row0, rows)
        g = g_ref[0]
        for dev in range(1, N_DEV):
            g = g + g_ref[dev]
        delta, m, v = _adam_math(pack[0], g, pack[1], pack[2])
        res[0], res[1], res[2], res[3] = g, delta, m, v
        loss_out[...] = res[0, loss_row:loss_row + 1, 0:1]
        for p, n in enumerate(sizes):
            for kind in range(4):
                for k, width in _small_pieces(n):
                    out_refs[4 * p + kind][:, LANES * k:LANES * k + width] = (
                        res[kind, row0[p] + k:row0[p] + k + 1, 0:width])

    vm = pl.BlockSpec(memory_space=pltpu.VMEM)
    out_shape = [_sds((1, 1), jnp.float32)]
    for n in sizes:
        out_shape += [_sds((1, n), jnp.float32)] * 4
    outs = _pcall(
        body, name="small_adam", out_shape=out_shape,
        in_specs=[vm] * (3 * n_par + 1), out_specs=[vm] * len(out_shape),
        scratch=[pltpu.VMEM((3, rows, LANES), jnp.float32), pltpu.VMEM((4, rows, LANES), jnp.float32)],
        after=after)(packs, *ws, *ms, *vs)
    return outs[0], [outs[1 + 4 * p: 5 + 4 * p] for p in range(n_par)]


def _adam_big(recv, w, m, v, name, after=None, which=None):
    rows, cols = recv.shape[-2:]
    row_tiles = [t for t in range(16, rows + 1, 16) if rows % t == 0 and t * cols <= 400 * 1024]
    tr, tc = (max(row_tiles), cols) if row_tiles else (rows, 512 if cols % 512 == 0 else cols)

    def body(r_ref, w_ref, m_ref, v_ref, g_ref, d_ref, mo_ref, vo_ref):
        g = r_ref[0].astype(jnp.float32)
        for d in range(1, N_DEV):
            g = g + r_ref[d].astype(jnp.float32)
        delta, mn, vn = _adam_math(w_ref[...], g, m_ref[...], v_ref[...])
        g_ref[...] = g
        d_ref[...] = delta
        mo_ref[...] = mn
        vo_ref[...] = vn

    blk = pl.BlockSpec((tr, tc), lambda i, j: (i, j))
    if which is None:
        r_spec = pl.BlockSpec((N_DEV, tr, tc), lambda i, j: (0, i, j))
    else:
        r_spec = pl.BlockSpec((N_DEV, None, tr, tc), lambda i, j: (0, which, i, j))
    return _pcall(
        body, name=name, grid=(rows // tr, cols // tc),
        out_shape=[_sds((rows, cols), jnp.float32)] * 4,
        in_specs=[r_spec, blk, blk, blk],
        out_specs=[blk] * 4, sem=("parallel", "parallel"), after=after)(recv, w, m, v)


def _mm(a, b, *, ta=False, tb=False, out_dtype, tm, tk, name):
    (kdim, mdim) = a.shape if ta else a.shape[::-1]
    ndim = b.shape[0] if tb else b.shape[1]
    tm, tk = min(tm, mdim), min(tk, kdim)
    nk = kdim // tk

    def body(a_ref, b_ref, o_ref, acc):
        k = pl.program_id(1)
        if ta:
            part = _dot_tn(a_ref[...], b_ref[...])
        elif tb:
            part = _dot_nt(a_ref[...], b_ref[...])
        else:
            part = _dot(a_ref[...], b_ref[...])

        @pl.when(k == 0)
        def _():
            acc[...] = part

        @pl.when(k > 0)
        def _():
            acc[...] += part

        @pl.when(k == nk - 1)
        def _():
            o_ref[...] = acc[...].astype(o_ref.dtype)

    a_spec = pl.BlockSpec((tk, tm), lambda i, k: (k, i)) if ta else pl.BlockSpec((tm, tk), lambda i, k: (i, k))
    b_spec = pl.BlockSpec((ndim, tk), lambda i, k: (0, k)) if tb else pl.BlockSpec((tk, ndim), lambda i, k: (k, 0))
    return _pcall(
        body, name=name, grid=(mdim // tm, nk), out_shape=_sds((mdim, ndim), out_dtype),
        in_specs=[a_spec, b_spec], out_specs=pl.BlockSpec((tm, ndim), lambda i, k: (i, 0)),
        scratch=[pltpu.VMEM((tm, ndim), jnp.float32)], sem=("parallel", "arbitrary"))(a, b)


def _ref_col_pieces(start, stop):
    ref_starts = [0, 1024, 1152, 1280, 1792, 2304, 2368, 2880]
    perm_starts = [C_QA, C_KA, C_VA, C_CQ, C_CKV, C_KR, C_QM]
    out = []
    for p in range(7):
        lo, hi = max(start, ref_starts[p]), min(stop, ref_starts[p + 1])
        if lo < hi:
            out.append((lo - start, perm_starts[p] + lo - ref_starts[p], hi - lo))
    return out


def _dw_in(hn, d_proj, n_shard, after):
    s, d = hn.shape
    n = sum(p.shape[1] for p in d_proj)
    n_pc = len(d_proj)
    tm, tk = min(512, d), min(1024, s)
    nk = s // tk

    def body(a_ref, *refs):
        b_refs, (o_ref, acc) = refs[:n_pc], refs[n_pc:]
        k = pl.program_id(1)
        part = _dot_tn(a_ref[...], jnp.concatenate([r[...] for r in b_refs], axis=1))

        @pl.when(k == 0)
        def _():
            acc[...] = part

        @pl.when(k > 0)
        def _():
            acc[...] += part

        @pl.when(k == nk - 1)
        def _():
            t = acc[...].T
            for j in range(N_DEV):
                rows = [t[src:src + width] for _, src, width in _ref_col_pieces(j * n_shard, (j + 1) * n_shard)]
                o_ref[j] = jnp.concatenate(rows, axis=0).astype(o_ref.dtype)

    return _pcall(
        body, name="dw_in", grid=(d // tm, nk), out_shape=_sds((N_DEV, n_shard, d), WIRE),
        in_specs=[pl.BlockSpec((tk, tm), lambda i, k: (k, i))]
        + [pl.BlockSpec((tk, p.shape[1]), lambda i, k: (k, 0)) for p in d_proj],
        out_specs=pl.BlockSpec((N_DEV, n_shard, tm), lambda i, k: (0, 0, i)),
        scratch=[pltpu.VMEM((tm, n), jnp.float32)], sem=("parallel", "arbitrary"), after=after)(hn, *d_proj)


def _in_proj(x, g, w):
    s, d = x.shape
    n = w.shape[0]
    tm = min(2 * ROW_TILE, s)

    def body(x_ref, g_ref, w_ref, p_ref, hn_ref):
        hn, _, _ = _norm_fwd(x_ref[...], g_ref[...])
        hn_ref[...] = hn.astype(hn_ref.dtype)
        p_ref[...] = _dot_nt(hn, w_ref[...])

    return _pcall(
        body, name="in_proj", grid=(s // tm,),
        out_shape=[_sds((s, n), jnp.float32), _sds((s, d), MXU)],
        in_specs=[pl.BlockSpec((tm, d), lambda i: (i, 0)), pl.BlockSpec((1, d), lambda i: (0, 0)),
                  pl.BlockSpec((n, d), lambda i: (0, 0), pipeline_mode=pl.Buffered(1))],
        out_specs=[pl.BlockSpec((tm, n), lambda i: (i, 0)), pl.BlockSpec((tm, d), lambda i: (i, 0))],
        sem=("parallel",))(x, g, w)


def _norm_rows(x, g):
    s, d = x.shape
    tm = min(FFN_TILE, s)

    def body(x_ref, g_ref, hn_ref):
        hn, _, _ = _norm_fwd(x_ref[...], g_ref[...])
        hn_ref[...] = hn.astype(hn_ref.dtype)

    row = pl.BlockSpec((tm, d), lambda i: (i, 0))
    return _pcall(body, name="norm_rows", grid=(s // tm,), out_shape=_sds((s, d), MXU),
                  in_specs=[row, pl.BlockSpec((1, d), lambda i: (0, 0))], out_specs=row, sem=("parallel",))(x, g)


def _mla_prep(proj, cos, sin, g_cq, g_ckv, w_uq, w_ukv, g_qn, g_qr, g_kn, g_kr):
    s = proj.shape[0]
    tm = min(ROW_TILE, s)
    nh = MLA_HEADS

    def body(cq_ref, ckv_ref, kr_ref, cos_ref, sin_ref, gcq_ref, gckv_ref, wuq_ref, wukv_ref,
             gqn_ref, gqr_ref, gkn_ref, gkr_ref,
             qc_ref, kc_ref, v_ref, qb_ref, kvb_ref, cqn_ref, ckvn_ref):
        cos_t, sin_t = cos_ref[...], sin_ref[...]
        lo = _lo_mask((tm, LANES))
        cqn, _, _ = _norm_fwd(cq_ref[...], gcq_ref[...])
        cqn_ref[...] = cqn.astype(cqn_ref.dtype)
        qb = _dot_nt(cqn, wuq_ref[...])
        qb_ref[...] = qb
        ckvn, _, _ = _norm_fwd(ckv_ref[...], gckv_ref[...])
        ckvn_ref[...] = ckvn.astype(ckvn_ref.dtype)
        w_ukv_full = jnp.concatenate([wukv_ref[dev] for dev in range(N_DEV)], axis=1)
        kvb = _dot(ckvn, w_ukv_full)
        kvb_ref[...] = kvb
        kr, _, _ = _norm_fwd(kr_ref[...], gkr_ref[...], half=True)
        kr = _rope(kr, cos_t, sin_t)
        kr2 = jnp.where(lo, kr, pltpu.roll(kr, 64, 1))
        ropes = []
        for j in range(nh // 2):
            xr = qb[:, nh * MLA_NOPE + LANES * j: nh * MLA_NOPE + LANES * (j + 1)]
            qr, _, _ = _norm_fwd(xr, gqr_ref[...], half=True)
            ropes.append(_rope(qr, cos_t, sin_t))
        for h in range(nh):
            qn, _, _ = _norm_fwd(qb[:, MLA_NOPE * h: MLA_NOPE * (h + 1)], gqn_ref[...])
            mask = lo if h % 2 == 0 else jnp.logical_not(lo)
            qr = jnp.where(mask, ropes[h // 2], 0.0)
            qc_ref[h] = jnp.concatenate([qn, qr], axis=1).astype(qc_ref.dtype)
            kn, _, _ = _norm_fwd(kvb[:, 256 * h: 256 * h + MLA_NOPE], gkn_ref[...])
            kc_ref[h] = jnp.concatenate([kn, kr2], axis=1).astype(kc_ref.dtype)
            v_ref[h] = kvb[:, 256 * h + MLA_NOPE: 256 * (h + 1)].astype(v_ref.dtype)

    def col(width, start):
        return pl.BlockSpec((tm, width), lambda i: (i, start // width))

    def full(shape):
        return pl.BlockSpec(shape, lambda i: (0,) * len(shape))

    def row(width):
        return pl.BlockSpec((tm, width), lambda i: (i, 0))

    def heads(width):
        return pl.BlockSpec((nh, tm, width), lambda i: (0, i, 0))

    return _pcall(
        body, name="mla_prep", grid=(s // tm,),
        out_shape=[_sds((nh, s, 256), MXU), _sds((nh, s, 256), MXU), _sds((nh, s, MLA_V), MXU),
                   _sds((s, 768), jnp.float32), _sds((s, 1024), jnp.float32),
                   _sds((s, 512), MXU), _sds((s, 512), MXU)],
        in_specs=[col(512, C_CQ), col(512, C_CKV), col(LANES, C_KR), row(LANES), row(LANES),
                  full((1, 512)), full((1, 512)), full((768, 512)), full((N_DEV, 512, LANES)),
                  full((1, LANES)), full((1, LANES)), full((1, LANES)), full((1, LANES))],
        out_specs=[heads(256), heads(256), heads(MLA_V), row(768), row(1024), row(512), row(512)],
        sem=("parallel",))(proj, proj, proj, cos, sin, g_cq, g_ckv, w_uq, w_ukv, g_qn, g_qr, g_kn, g_kr)


def _mla_fwd(qc, kc, v):
    nh, s, _ = qc.shape
    t = min(ATT_TILE, s)
    nb = s // t
    scale = (MLA_NOPE + MLA_ROPE) ** -0.5

    def body(q_ref, k_ref, v_ref, y_ref, lse_ref, m_sc, l_sc, acc):
        qi, ki = pl.program_id(1), pl.program_id(2)

        @pl.when(ki == 0)
        def _():
            m_sc[...] = jnp.full_like(m_sc, NEG_INF)
            l_sc[...] = jnp.zeros_like(l_sc)
            acc[...] = jnp.zeros_like(acc)

        def step(diagonal):
            rc = t // 4 if diagonal else t
            for c in range(t // rc):
                rows = slice(rc * c, rc * (c + 1))
                keys = slice(0, rc * (c + 1))
                sc = _dot_nt(q_ref[0, rows, :], k_ref[0, keys, :]) * (scale * LOG2E)
                if diagonal:
                    r_i = lax.broadcasted_iota(jnp.int32, sc.shape, 0) + rc * c
                    c_i = lax.broadcasted_iota(jnp.int32, sc.shape, 1)
                    sc = jnp.where(c_i <= r_i, sc, NEG_INF)
                m_old = m_sc[rows, :]
                m_new = jnp.maximum(m_old, jnp.max(sc, -1, keepdims=True))
                alpha = jnp.exp2(m_old - m_new)
                p = jnp.exp2(sc - m_new)
                l_sc[rows, :] = alpha * l_sc[rows, :] + jnp.sum(p, -1, keepdims=True)
                acc[rows, :] = alpha * acc[rows, :] + _dot(p, v_ref[0, keys, :])
                m_sc[rows, :] = m_new

        @pl.when(ki < qi)
        def _():
            step(False)

        @pl.when(ki == qi)
        def _():
            step(True)

        @pl.when(ki == qi)
        def _():
            y_ref[...] = acc[...] / l_sc[...]
            lse_ref[0] = m_sc[...] + jnp.log2(l_sc[...])

    return _pcall(
        body, name="mla_fwd", grid=(nh, nb, nb),
        out_shape=[_sds((s, nh * MLA_V), jnp.float32), _sds((nh, s, 1), jnp.float32)],
        in_specs=[pl.BlockSpec((1, t, 256), lambda h, i, k: (h, i, 0)),
                  pl.BlockSpec((1, t, 256), lambda h, i, k: (h, jnp.minimum(k, i), 0)),
                  pl.BlockSpec((1, t, MLA_V), lambda h, i, k: (h, jnp.minimum(k, i), 0))],
        out_specs=[pl.BlockSpec((t, MLA_V), lambda h, i, k: (i, h)),
                   pl.BlockSpec((1, t, 1), lambda h, i, k: (h, i, 0))],
        scratch=[pltpu.VMEM((t, 1), jnp.float32), pltpu.VMEM((t, 1), jnp.float32),
                 pltpu.VMEM((t, MLA_V), jnp.float32)],
        sem=("parallel", "parallel", "arbitrary"))(qc, kc, v)


def _memkv_prep(mem, g_mem, w_mkv, g_mk):
    ml, d = mem.shape
    hw = MEM_HEADS * MEM_DIM

    def body(mem_ref, g_ref, w_ref, gk_ref, k_ref, v_ref, kv_ref, mn_ref):
        mn, _, _ = _norm_fwd(mem_ref[...], g_ref[...])
        mn_ref[...] = mn.astype(mn_ref.dtype)
        kv = _dot(mn, w_ref[...])
        kv_ref[...] = kv
        for h in range(MEM_HEADS):
            kn, _, _ = _norm_fwd(kv[:, MEM_DIM * h: MEM_DIM * (h + 1)], gk_ref[...])
            k_ref[:, MEM_DIM * h: MEM_DIM * (h + 1)] = kn.astype(k_ref.dtype)
        v_ref[...] = kv[:, hw:].astype(v_ref.dtype)

    vm = pl.BlockSpec(memory_space=pltpu.VMEM)
    return _pcall(
        body, name="memkv_prep",
        out_shape=[_sds((ml, hw), MXU), _sds((ml, hw), MXU), _sds((ml, 2 * hw), jnp.float32), _sds((ml, d), MXU)],
        in_specs=[vm] * 4, out_specs=[vm] * 4)(mem, g_mem, w_mkv, g_mk)


def _mem_fwd(proj, g_mq, km, vmm):
    s = proj.shape[0]
    ml, hw = km.shape
    tm = min(FFN_TILE, s)
    scale = MEM_DIM ** -0.5

    def body(q_ref, g_ref, k_ref, v_ref, y_ref, lse_ref):
        col = lax.broadcasted_iota(jnp.int32, (tm, MEM_HEADS), 1)
        lse_t = jnp.zeros((tm, MEM_HEADS), jnp.float32)
        for h in range(MEM_HEADS):
            sl = slice(MEM_DIM * h, MEM_DIM * (h + 1))
            qn, _, _ = _norm_fwd(q_ref[:, sl], g_ref[...])
            sc = _dot_nt(qn, k_ref[:, sl]) * scale
            m = jnp.max(sc, -1, keepdims=True)
            p = jnp.exp(sc - m)
            l = jnp.sum(p, -1, keepdims=True)
            y_ref[:, sl] = _dot(p, v_ref[:, sl]) / l
            lse_t = jnp.where(col == h, m + jnp.log(l), lse_t)
        lse_ref[...] = lse_t

    return _pcall(
        body, name="mem_fwd", grid=(s // tm,),
        out_shape=[_sds((s, hw), jnp.float32), _sds((s, MEM_HEADS), jnp.float32)],
        in_specs=[pl.BlockSpec((tm, hw), lambda i: (i, C_QM // hw)), pl.BlockSpec((1, MEM_DIM), lambda i: (0, 0)),
                  pl.BlockSpec((ml, hw), lambda i: (0, 0)), pl.BlockSpec((ml, hw), lambda i: (0, 0))],
        out_specs=[pl.BlockSpec((tm, hw), lambda i: (i, 0)), pl.BlockSpec((tm, MEM_HEADS), lambda i: (i, 0))],
        sem=("parallel",))(proj, g_mq, km, vmm)


def _alibi_slope(h):
    return float(2.0 ** (-8.0 * (h + 1) / SWA_Q_HEADS))


def _swa_common(n, kp, kc, vp, vc, pq, pkp, pkc, gk):
    b = SWA_BLOCK
    k_raw = jnp.concatenate([kp, kc], axis=0)
    kn, kxn, kr = _norm_fwd(k_raw, gk, half=True)
    v = jnp.concatenate([vp, vc], axis=0)
    dist = jnp.abs(pq - jnp.concatenate([pkp, pkc], axis=1))
    r_i = lax.broadcasted_iota(jnp.int32, (b, 2 * b), 0)
    c_i = lax.broadcasted_iota(jnp.int32, (b, 2 * b), 1)
    valid = (c_i > r_i) & (c_i <= r_i + b) & (c_i >= jnp.where(n > 0, 0, b))
    bias = jnp.where(valid, -dist, NEG_INF)
    return kn, v, bias


def _swa_folded(n, kp, kc, pq, pkp, pkc, gk):
    b = SWA_BLOCK
    kn_p, _, _ = _norm_fwd(kp, gk, half=True)
    kn_c, _, _ = _norm_fwd(kc, gk, half=True)
    r_i = lax.broadcasted_iota(jnp.int32, (b, b), 0)
    c_i = lax.broadcasted_iota(jnp.int32, (b, b), 1)
    upper = c_i > r_i
    bias_prev = jnp.where(n > 0, 0.0, NEG_INF) - jnp.abs(pq - pkp)
    bias = jnp.where(upper, bias_prev, -jnp.abs(pq - pkc))
    return kn_p, kn_c, bias, upper


def _swa_specs(s):
    b = SWA_BLOCK
    prev = lambda n: jnp.maximum(n - 1, 0)
    return [
        pl.BlockSpec((b, 1024), lambda n: (n, C_QA // 1024)),
        pl.BlockSpec((b, LANES), lambda n: (prev(n), C_KA // LANES)),
        pl.BlockSpec((b, LANES), lambda n: (n, C_KA // LANES)),
        pl.BlockSpec((b, LANES), lambda n: (prev(n), C_VA // LANES)),
        pl.BlockSpec((b, LANES), lambda n: (n, C_VA // LANES)),
        pl.BlockSpec((b, 1), lambda n: (n, 0)),
        pl.BlockSpec((1, b), lambda n: (0, prev(n))),
        pl.BlockSpec((1, b), lambda n: (0, n)),
        pl.BlockSpec((1, LANES), lambda n: (0, 0)),
        pl.BlockSpec((1, LANES), lambda n: (0, 0)),
        pl.BlockSpec(memory_space=pltpu.SMEM),
    ]


def _swa_fwd(proj, posc, posr, gq, gk, sinks):
    s = proj.shape[0]
    b = SWA_BLOCK
    scale = SWA_DIM ** -0.5

    def body(q_ref, kp_ref, kc_ref, vp_ref, vc_ref, pq_ref, pkp_ref, pkc_ref, gq_ref, gk_ref, sink_ref,
             y_ref, lse_ref):
        n = pl.program_id(0)
        kn_p, kn_c, bias, upper = _swa_folded(n, kp_ref[...], kc_ref[...], pq_ref[...], pkp_ref[...], pkc_ref[...],
                                              gk_ref[...])
        v_p, v_c = vp_ref[...], vc_ref[...]
        lo = _lo_mask((b, LANES))
        col = lax.broadcasted_iota(jnp.int32, (b, SWA_Q_HEADS), 1)
        lse_t = jnp.zeros((b, SWA_Q_HEADS), jnp.float32)
        hpg = SWA_Q_HEADS // SWA_KV_HEADS
        for g in range(SWA_KV_HEADS):
            heads = range(hpg * g, hpg * (g + 1))
            kvmask = lo if g == 0 else jnp.logical_not(lo)
            qs = []
            for j in range(hpg // 2 * g, hpg // 2 * (g + 1)):
                qn, _, _ = _norm_fwd(q_ref[:, LANES * j: LANES * (j + 1)], gq_ref[...], half=True)
                qn = qn * scale
                qsw = pltpu.roll(qn, 64, 1)
                qs += [jnp.where(kvmask, qn if e == g else qsw, 0.0) for e in range(2)]
            q_st = jnp.concatenate(qs, axis=0).astype(MXU)
            sp_st, sc_st = _dot_nt(q_st, kn_p), _dot_nt(q_st, kn_c)
            pus, pls, ls = [], [], []
            for i, h in enumerate(heads):
                rows = slice(b * i, b * (i + 1))
                sc = jnp.where(upper, sp_st[rows], sc_st[rows]) + _alibi_slope(h) * bias
                sk = sink_ref[h]
                m = jnp.maximum(jnp.max(sc, -1, keepdims=True), sk)
                p = jnp.exp(sc - m)
                l = jnp.sum(p, -1, keepdims=True) + jnp.exp(sk - m)
                pus.append(jnp.where(upper, p, 0.0).astype(MXU))
                pls.append(jnp.where(upper, 0.0, p).astype(MXU))
                ls.append(l)
                lse_t = jnp.where(col == h, m + jnp.log(l), lse_t)
            o_st = _dot(jnp.concatenate(pus, axis=0), v_p) + _dot(jnp.concatenate(pls, axis=0), v_c)
            for j in range(hpg // 2 * g, hpg // 2 * (g + 1)):
                halves = []
                for e in range(2):
                    i = 2 * j + e - hpg * g
                    o_h = o_st[b * i: b * (i + 1)] / ls[i]
                    halves.append(o_h if e == g else pltpu.roll(o_h, 64, 1))
                y_ref[:, LANES * j: LANES * (j + 1)] = jnp.where(lo, halves[0], halves[1])
        lse_ref[...] = lse_t

    return _pcall(
        body, name="swa_fwd", grid=(s // b,),
        out_shape=[_sds((s, 1024), jnp.float32), _sds((s, SWA_Q_HEADS), jnp.float32)],
        in_specs=_swa_specs(s),
        out_specs=[pl.BlockSpec((b, 1024), lambda n: (n, 0)), pl.BlockSpec((b, SWA_Q_HEADS), lambda n: (n, 0))],
        sem=("parallel",))(proj, proj, proj, proj, proj, posc, posr, posr, gq, gk, sinks)


def _out_proj(y_a, y_b, y_m, x, w_out, g_ffn):
    s, d = x.shape
    tm = min(2 * ROW_TILE, s)

    def body(ya_ref, yb_ref, ym_ref, x_ref, w_ref, g_ref, h1_ref, fn_ref):
        y = jnp.concatenate([ya_ref[...].astype(MXU), yb_ref[...].astype(MXU), ym_ref[...].astype(MXU)], axis=1)
        h1 = x_ref[...] + _dot(y, w_ref[...])
        h1_ref[...] = h1
        fn, _, _ = _norm_fwd(h1, g_ref[...])
        fn_ref[...] = fn.astype(fn_ref.dtype)

    def row(width):
        return pl.BlockSpec((tm, width), lambda i: (i, 0))

    return _pcall(
        body, name="out_proj", grid=(s // tm,),
        out_shape=[_sds((s, d), jnp.float32), _sds((s, d), MXU)],
        in_specs=[row(1024), row(512), row(512), row(d),
                  pl.BlockSpec(w_out.shape, lambda i: (0, 0), pipeline_mode=pl.Buffered(1)),
                  pl.BlockSpec((1, d), lambda i: (0, 0))],
        out_specs=[row(d), row(d)], sem=("parallel",))(y_a, y_b, y_m, x, w_out, g_ffn)


def _ffn_gu(fn, w_gu):
    s, d = fn.shape
    f = w_gu.shape[2]
    tm = min(2 * FFN_TILE, s)

    def body(fn_ref, w_ref, gu_ref, act_ref):
        x = fn_ref[...]
        g = _dot_nt(x, w_ref[0, 0])
        u = _dot_nt(x, w_ref[0, 1])
        gu_ref[0, 0] = g
        gu_ref[0, 1] = u
        act_ref[0] = (g * jax.nn.sigmoid(g) * u).astype(act_ref.dtype)

    return _pcall(
        body, name="ffn_gate_up", grid=(N_DEV, s // tm),
        out_shape=[_sds((N_DEV, 2, s, f), jnp.float32), _sds((N_DEV, s, f), MXU)],
        in_specs=[pl.BlockSpec((tm, d), lambda j, i: (i, 0)),
                  pl.BlockSpec((1, 2, f, d), lambda j, i: (j, 0, 0, 0))],
        out_specs=[pl.BlockSpec((1, 2, tm, f), lambda j, i: (j, 0, i, 0)),
                   pl.BlockSpec((1, tm, f), lambda j, i: (j, i, 0))],
        sem=("parallel", "parallel"))(fn, w_gu)


def _ffn_down(act, w_d, h1, target):
    _, s, f = act.shape
    d = h1.shape[1]
    tm = min(FFN_TILE, s)

    def body(a_ref, w_ref, h1_ref, t_ref, dout_ref, doutb_ref, loss_ref, acc):
        i, j = pl.program_id(0), pl.program_id(1)
        part = _dot(a_ref[0], w_ref[0]) + _dot(a_ref[1], w_ref[1])

        @pl.when(j == 0)
        def _():
            acc[...] = h1_ref[...] + part

        @pl.when(j > 0)
        def _():
            acc[...] += part

        @pl.when((i == 0) & (j == 0))
        def _():
            loss_ref[...] = jnp.zeros_like(loss_ref)

        @pl.when(j == N_DEV // 2 - 1)
        def _():
            diff = acc[...] - t_ref[...]
            dout_ref[...] = diff / d
            doutb_ref[...] = (diff / d).astype(doutb_ref.dtype)
            loss_ref[...] += 0.5 * jnp.sum(jnp.sum(diff * diff, -1, keepdims=True) / d)

    row = pl.BlockSpec((tm, d), lambda i, j: (i, 0))
    return _pcall(
        body, name="ffn_down", grid=(s // tm, N_DEV // 2),
        out_shape=[_sds((s, d), jnp.float32), _sds((s, d), MXU), _sds((8, LANES), jnp.float32)],
        in_specs=[pl.BlockSpec((2, tm, f), lambda i, j: (j, i, 0)), pl.BlockSpec((2, f, d), lambda i, j: (j, 0, 0)),
                  row, row],
        out_specs=[row, row, pl.BlockSpec((8, LANES), lambda i, j: (0, 0))],
        scratch=[pltpu.VMEM((tm, d), jnp.float32)], sem=("arbitrary", "arbitrary"))(act, w_d, h1, target)


def _ffn_bwd_act(dout, w_d, gu):
    s, d = dout.shape
    f = w_d.shape[1]
    tm = min(2 * FFN_TILE, s)
    ni = s // tm

    def body(do_ref, w_ref, gu_ref, dgu_ref, dw_ref, acc):
        i = pl.program_id(1)
        do = do_ref[...]
        d_act = _dot_nt(do, w_ref[0])
        g, u = gu_ref[0, 0], gu_ref[0, 1]
        sig = jax.nn.sigmoid(g)
        silu = g * sig
        dgu_ref[0, 0] = (d_act * u * (sig * (1.0 + g * (1.0 - sig)))).astype(dgu_ref.dtype)
        dgu_ref[0, 1] = (d_act * silu).astype(dgu_ref.dtype)
        part = _dot_tn(silu * u, do)

        @pl.when(i == 0)
        def _():
            acc[...] = part

        @pl.when(i > 0)
        def _():
            acc[...] += part

        @pl.when(i == ni - 1)
        def _():
            dw_ref[0] = acc[...].astype(dw_ref.dtype)

    return _pcall(
        body, name="ffn_bwd_act", grid=(N_DEV, ni),
        out_shape=[_sds((N_DEV, 2, s, f), MXU), _sds((N_DEV, f, d), WIRE)],
        in_specs=[pl.BlockSpec((tm, d), lambda j, i: (i, 0)), pl.BlockSpec((1, f, d), lambda j, i: (j, 0, 0)),
                  pl.BlockSpec((1, 2, tm, f), lambda j, i: (j, 0, i, 0))],
        out_specs=[pl.BlockSpec((1, 2, tm, f), lambda j, i: (j, 0, i, 0)),
                   pl.BlockSpec((1, f, d), lambda j, i: (j, 0, 0))],
        scratch=[pltpu.VMEM((f, d), jnp.float32)], sem=("parallel", "arbitrary"))(dout, w_d, gu)


def _ffn_dw_gu(fn, dgu):
    s, d = fn.shape
    f = dgu.shape[-1]
    tk = min(4 * FFN_TILE, s)
    nk = s // tk

    def body(fn_ref, dgu_ref, dw_ref, acc):
        k = pl.program_id(2)
        part = _dot_tn(dgu_ref[0, 0], fn_ref[...])

        @pl.when(k == 0)
        def _():
            acc[...] = part

        @pl.when(k > 0)
        def _():
            acc[...] += part

        @pl.when(k == nk - 1)
        def _():
            dw_ref[0, 0] = acc[...].astype(dw_ref.dtype)

    return _pcall(
        body, name="ffn_dw_gate_up", grid=(N_DEV, 2, nk),
        out_shape=_sds((N_DEV, 2, f, d), WIRE),
        in_specs=[pl.BlockSpec((tk, d), lambda j, w, k: (k, 0)),
                  pl.BlockSpec((1, 1, tk, f), lambda j, w, k: (j, w, k, 0))],
        out_specs=pl.BlockSpec((1, 1, f, d), lambda j, w, k: (j, w, 0, 0)),
        scratch=[pltpu.VMEM((f, d), jnp.float32)], sem=("parallel", "parallel", "arbitrary"))(fn, dgu)


def _ffn_dfn(dgu, w_gu, after):
    _, _, s, f = dgu.shape
    d = w_gu.shape[3]
    tm = min(FFN_TILE, s)

    def body(dgu_ref, w_ref, dfn_ref):
        j = pl.program_id(1)
        part = (_dot(dgu_ref[0, 0], w_ref[0, 0]) + _dot(dgu_ref[0, 1], w_ref[0, 1])
                + _dot(dgu_ref[1, 0], w_ref[1, 0]) + _dot(dgu_ref[1, 1], w_ref[1, 1]))

        @pl.when(j == 0)
        def _():
            dfn_ref[...] = part

        @pl.when(j > 0)
        def _():
            dfn_ref[...] += part

    return _pcall(
        body, name="ffn_dfn", grid=(s // tm, N_DEV // 2),
        out_shape=_sds((s, d), jnp.float32),
        in_specs=[pl.BlockSpec((2, 2, tm, f), lambda i, j: (j, 0, i, 0)),
                  pl.BlockSpec((2, 2, f, d), lambda i, j: (j, 0, 0, 0))],
        out_specs=pl.BlockSpec((tm, d), lambda i, j: (i, 0)),
        sem=("parallel", "arbitrary"), after=after)(dgu, w_gu)


def _ffn_norm_bwd(d_fn, dout, h1, g_ffn):
    s, d = h1.shape
    tm = min(2 * ROW_TILE, s)

    def body(dfn_ref, do_ref, h1_ref, g_ref, dh1_ref, dg_ref):
        i = pl.program_id(0)

        @pl.when(i == 0)
        def _():
            dg_ref[...] = jnp.zeros_like(dg_ref)

        _, xn, r = _norm_fwd(h1_ref[...], g_ref[...])
        dx, dg = _norm_bwd(xn, r, g_ref[...], dfn_ref[...])
        dh1_ref[...] = do_ref[...] + dx
        dg_ref[...] += dg

    row = pl.BlockSpec((tm, d), lambda i: (i, 0))
    vec = pl.BlockSpec((1, d), lambda i: (0, 0))
    return _pcall(
        body, name="ffn_norm_bwd", grid=(s // tm,),
        out_shape=[_sds((s, d), jnp.float32), _sds((1, d), jnp.float32)],
        in_specs=[row, row, row, vec], out_specs=[row, vec], sem=("arbitrary",))(d_fn, dout, h1, g_ffn)


def _mem_bwd(proj, g_mq, km, vmm, d_y, y_m, lse):
    s = proj.shape[0]
    ml, hw = km.shape
    tm = min(FFN_TILE, s)
    scale = MEM_DIM ** -0.5

    def body(q_ref, g_ref, k_ref, v_ref, do_ref, y_ref, lse_ref, dq_ref, dk_ref, dv_ref, dg_ref):
        i = pl.program_id(0)

        @pl.when(i == 0)
        def _():
            dk_ref[...] = jnp.zeros_like(dk_ref)
            dv_ref[...] = jnp.zeros_like(dv_ref)
            dg_ref[...] = jnp.zeros_like(dg_ref)

        col = lax.broadcasted_iota(jnp.int32, (tm, MEM_HEADS), 1)
        lse_t = lse_ref[...]
        for h in range(MEM_HEADS):
            sl = slice(MEM_DIM * h, MEM_DIM * (h + 1))
            qn, xn, r = _norm_fwd(q_ref[:, sl], g_ref[...])
            lse_h = jnp.sum(jnp.where(col == h, lse_t, 0.0), -1, keepdims=True)
            p = jnp.exp(_dot_nt(qn, k_ref[:, sl]) * scale - lse_h)
            do = do_ref[:, sl]
            dd = jnp.sum(do * y_ref[:, sl], -1, keepdims=True)
            dp = _dot_nt(do, v_ref[:, sl])
            ds = (p * (dp - dd)).astype(MXU)
            dv_ref[:, sl] += _dot_tn(p, do)
            dk_ref[:, sl] += _dot_tn(ds, qn) * scale
            dx, dg = _norm_bwd(xn, r, g_ref[...], _dot(ds, k_ref[:, sl]) * scale)
            dq_ref[:, sl] = dx.astype(dq_ref.dtype)
            dg_ref[...] += dg

    full = pl.BlockSpec((ml, hw), lambda i: (0, 0))
    return _pcall(
        body, name="mem_bwd", grid=(s // tm,),
        out_shape=[_sds((s, hw), MXU), _sds((ml, hw), jnp.float32), _sds((ml, hw), jnp.float32),
                   _sds((1, MEM_DIM), jnp.float32)],
        in_specs=[pl.BlockSpec((tm, hw), lambda i: (i, C_QM // hw)), pl.BlockSpec((1, MEM_DIM), lambda i: (0, 0)),
                  full, full, pl.BlockSpec((tm, hw), lambda i: (i, 3)), pl.BlockSpec((tm, hw), lambda i: (i, 0)),
                  pl.BlockSpec((tm, MEM_HEADS), lambda i: (i, 0))],
        out_specs=[pl.BlockSpec((tm, hw), lambda i: (i, 0)), full, full,
                   pl.BlockSpec((1, MEM_DIM), lambda i: (0, 0))],
        sem=("arbitrary",))(proj, g_mq, km, vmm, d_y, y_m, lse)


def _memkv_bwd(mem, g_mem, w_mkv, g_mk, kv, memn, dk, dv):
    ml, d = mem.shape
    hw = MEM_HEADS * MEM_DIM

    def body(mem_ref, g_ref, w_ref, gk_ref, kv_ref, mn_ref, dk_ref, dv_ref, dw_ref, dgm_ref, dgk_ref):
        parts = []
        dgk = jnp.zeros((1, MEM_DIM), jnp.float32)
        for h in range(MEM_HEADS):
            sl = slice(MEM_DIM * h, MEM_DIM * (h + 1))
            _, xn, r = _norm_fwd(kv_ref[:, sl], gk_ref[...])
            dx, dg = _norm_bwd(xn, r, gk_ref[...], dk_ref[:, sl])
            parts.append(dx)
            dgk = dgk + dg
        dkv = jnp.concatenate(parts + [dv_ref[...]], axis=1).astype(MXU)
        dgk_ref[...] = dgk
        dw_ref[...] = _dot_tn(mn_ref[...], dkv).astype(dw_ref.dtype)
        d_mn = _dot_nt(dkv, w_ref[...])
        _, xn, _ = _norm_fwd(mem_ref[...], g_ref[...])
        dgm_ref[...] = jnp.sum(d_mn * xn, 0, keepdims=True)

    vm = pl.BlockSpec(memory_space=pltpu.VMEM)
    return _pcall(
        body, name="memkv_bwd",
        out_shape=[_sds((d, 2 * hw), WIRE), _sds((1, d), jnp.float32), _sds((1, MEM_DIM), jnp.float32)],
        in_specs=[vm] * 8, out_specs=[vm] * 3)(mem, g_mem, w_mkv, g_mk, kv, memn, dk, dv)


def _mla_bwd(qc, kc, v, d_y, y_b, lse, after):
    nh, s, _ = qc.shape
    t = min(ATT_TILE, s)
    nb = s // t
    scale = (MLA_NOPE + MLA_ROPE) ** -0.5

    def body(q_ref, k_ref, v_ref, do_ref, y_ref, lse_ref, dq_ref, dk_ref, dv_ref, dk_acc, dv_acc):
        kj, qi = pl.program_id(1), pl.program_id(2)

        @pl.when((kj == 0) & (qi == 0))
        def _():
            dq_ref[...] = jnp.zeros_like(dq_ref)

        @pl.when(qi == kj)
        def _():
            dk_acc[...] = jnp.zeros_like(dk_acc)
            dv_acc[...] = jnp.zeros_like(dv_acc)

        def step(diagonal):
            rc = t // 4 if diagonal else t
            for c in range(t // rc):
                rows = slice(rc * c, rc * (c + 1))
                keys = slice(0, rc * (c + 1))
                q, k = q_ref[0, rows, :], k_ref[0, keys, :]
                sc = _dot_nt(q, k) * (scale * LOG2E)
                if diagonal:
                    r_i = lax.broadcasted_iota(jnp.int32, sc.shape, 0) + rc * c
                    c_i = lax.broadcasted_iota(jnp.int32, sc.shape, 1)
                    sc = jnp.where(c_i <= r_i, sc, NEG_INF)
                p = jnp.exp2(sc - lse_ref[0, rows, :])
                do = do_ref[rows, :]
                dd = jnp.sum(do * y_ref[rows, :], -1, keepdims=True)
                dp = _dot_nt(do, v_ref[0, keys, :])
                ds = (p * (dp - dd) * scale).astype(MXU)
                dv_acc[keys, :] += _dot_tn(p, do)
                dk_acc[keys, :] += _dot_tn(ds, q)
                out_rows = pl.ds(pl.multiple_of(qi * t + rc * c, rc), rc)
                dq_ref[0, out_rows, :] += _dot(ds, k)

        @pl.when(qi > kj)
        def _():
            step(False)

        @pl.when(qi == kj)
        def _():
            step(True)

        @pl.when(qi == nb - 1)
        def _():
            dk_ref[0] = dk_acc[...]
            dv_ref[0] = dv_acc[...]

    qmap = lambda h, j, i: (h, jnp.maximum(i, j), 0)
    return _pcall(
        body, name="mla_bwd", grid=(nh, nb, nb),
        out_shape=[_sds((nh, s, 256), jnp.float32), _sds((nh, s, 256), jnp.float32),
                   _sds((nh, s, MLA_V), jnp.float32)],
        in_specs=[pl.BlockSpec((1, t, 256), qmap),
                  pl.BlockSpec((1, t, 256), lambda h, j, i: (h, j, 0)),
                  pl.BlockSpec((1, t, MLA_V), lambda h, j, i: (h, j, 0)),
                  pl.BlockSpec((t, MLA_V), lambda h, j, i: (jnp.maximum(i, j), 8 + h)),
                  pl.BlockSpec((t, MLA_V), lambda h, j, i: (jnp.maximum(i, j), h)),
                  pl.BlockSpec((1, t, 1), qmap)],
        out_specs=[pl.BlockSpec((1, s, 256), lambda h, j, i: (h, 0, 0)),
                   pl.BlockSpec((1, t, 256), lambda h, j, i: (h, j, 0)),
                   pl.BlockSpec((1, t, MLA_V), lambda h, j, i: (h, j, 0))],
        scratch=[pltpu.VMEM((t, 256), jnp.float32), pltpu.VMEM((t, MLA_V), jnp.float32)],
        sem=("parallel", "arbitrary", "arbitrary"), after=after)(qc, kc, v, d_y, y_b, lse)


def _mla_prep_bwd(proj, cos, sin, g_cq, g_ckv, w_uq, w_ukv, g_qn, g_qr, g_kn, g_kr,
                  qb, kvb, cqn, ckvn, dqc, dkc, dv):
    s = proj.shape[0]
    tm = min(ROW_TILE, s)
    nh = MLA_HEADS
    ni = s // tm

    def body(cq_ref, ckv_ref, kr_ref, cos_ref, sin_ref, gcq_ref, gckv_ref, wuq_ref, wukv_ref,
             gqn_ref, gqr_ref, gkn_ref, gkr_ref, qb_ref, kvb_ref, cqn_ref, ckvn_ref, dqc_ref, dkc_ref, dv_ref,
             dcq_ref, dckv_ref, dkr_ref, dwuq_ref, dwukv_ref,
             dgcq_ref, dgckv_ref, dgqn_ref, dgqr_ref, dgkn_ref, dgkr_ref, acc_uq, acc_ukv):
        i = pl.program_id(0)

        @pl.when(i == 0)
        def _():
            acc_uq[...] = jnp.zeros_like(acc_uq)
            acc_ukv[...] = jnp.zeros_like(acc_ukv)
            for ref in (dgcq_ref, dgckv_ref, dgqn_ref, dgqr_ref, dgkn_ref, dgkr_ref):
                ref[...] = jnp.zeros_like(ref)

        cos_t, sin_t = cos_ref[...], sin_ref[...]
        lo = _lo_mask((tm, LANES))
        qb_v, kvb_v = qb_ref[...], kvb_ref[...]
        dq_parts, dgqn = [], jnp.zeros((1, LANES), jnp.float32)
        for h in range(nh):
            _, xn, r = _norm_fwd(qb_v[:, MLA_NOPE * h: MLA_NOPE * (h + 1)], gqn_ref[...])
            dx, dg = _norm_bwd(xn, r, gqn_ref[...], dqc_ref[h][:, :MLA_NOPE])
            dq_parts.append(dx)
            dgqn = dgqn + dg
        dgqn_ref[...] += dgqn
        dgqr = jnp.zeros((1, LANES), jnp.float32)
        for j in range(nh // 2):
            d_rope = jnp.where(lo, dqc_ref[2 * j][:, MLA_NOPE:], dqc_ref[2 * j + 1][:, MLA_NOPE:])
            d_pre = _rope_bwd(d_rope, cos_t, sin_t)
            xr = qb_v[:, nh * MLA_NOPE + LANES * j: nh * MLA_NOPE + LANES * (j + 1)]
            _, xn, r = _norm_fwd(xr, gqr_ref[...], half=True)
            dx, dg = _norm_bwd(xn, r, gqr_ref[...], d_pre, half=True)
            dq_parts.append(dx)
            dgqr = dgqr + dg
        dgqr_ref[...] += dgqr
        dqb = jnp.concatenate(dq_parts, axis=1).astype(MXU)
        acc_uq[...] += _dot_tn(dqb, cqn_ref[...])
        _, xn, r = _norm_fwd(cq_ref[...], gcq_ref[...])
        dx, dg = _norm_bwd(xn, r, gcq_ref[...], _dot(dqb, wuq_ref[...]))
        dcq_ref[...] = dx.astype(dcq_ref.dtype)
        dgcq_ref[...] += dg
        dkv_parts, dgkn = [], jnp.zeros((1, LANES), jnp.float32)
        d_kr2 = jnp.zeros((tm, LANES), jnp.float32)
        for h in range(nh):
            _, xn, r = _norm_fwd(kvb_v[:, 256 * h: 256 * h + MLA_NOPE], gkn_ref[...])
            dx, dg = _norm_bwd(xn, r, gkn_ref[...], dkc_ref[h][:, :MLA_NOPE])
            dkv_parts += [dx, dv_ref[h]]
            dgkn = dgkn + dg
            d_kr2 = d_kr2 + dkc_ref[h][:, MLA_NOPE:]
        dgkn_ref[...] += dgkn
        dkvb = jnp.concatenate(dkv_parts, axis=1).astype(MXU)
        part_ukv = _dot_tn(ckvn_ref[...], dkvb)
        for dev in range(N_DEV):
            acc_ukv[dev] += part_ukv[:, LANES * dev: LANES * (dev + 1)]
        w_ukv_full = jnp.concatenate([wukv_ref[dev] for dev in range(N_DEV)], axis=1)
        d_ckvn = _dot_nt(dkvb, w_ukv_full)
        _, xn, r = _norm_fwd(ckv_ref[...], gckv_ref[...])
        dx, dg = _norm_bwd(xn, r, gckv_ref[...], d_ckvn)
        dckv_ref[...] = dx.astype(dckv_ref.dtype)
        dgckv_ref[...] += dg
        d_kr = jnp.where(lo, d_kr2 + pltpu.roll(d_kr2, 64, 1), 0.0)
        d_pre = _rope_bwd(d_kr, cos_t, sin_t)
        _, xn, r = _norm_fwd(kr_ref[...], gkr_ref[...], half=True)
        dx, dg = _norm_bwd(xn, r, gkr_ref[...], d_pre, half=True)
        dkr_ref[...] = jnp.where(lo, dx, 0.0).astype(dkr_ref.dtype)
        dgkr_ref[...] += jnp.where(_lo_mask((1, LANES)), dg, 0.0)

        @pl.when(i == ni - 1)
        def _():
            dwuq_ref[...] = acc_uq[...].astype(dwuq_ref.dtype)
            dwukv_ref[...] = acc_ukv[...].astype(dwukv_ref.dtype)

    def col(width, start):
        return pl.BlockSpec((tm, width), lambda i: (i, start // width))

    def full(shape):
        return pl.BlockSpec(shape, lambda i: (0,) * len(shape))

    def row(width):
        return pl.BlockSpec((tm, width), lambda i: (i, 0))

    def heads(width):
        return pl.BlockSpec((nh, tm, width), lambda i: (0, i, 0))

    vec = full((1, LANES))
    return _pcall(
        body, name="mla_prep_bwd", grid=(ni,),
        out_shape=[_sds((s, 512), MXU), _sds((s, 512), MXU), _sds((s, LANES), MXU),
                   _sds((768, 512), WIRE), _sds((N_DEV, 512, LANES), WIRE),
                   _sds((1, 512), jnp.float32), _sds((1, 512), jnp.float32)] + [_sds((1, LANES), jnp.float32)] * 4,
        in_specs=[col(512, C_CQ), col(512, C_CKV), col(LANES, C_KR), row(LANES), row(LANES),
                  full((1, 512)), full((1, 512)), full((768, 512)), full((N_DEV, 512, LANES)), vec, vec, vec, vec,
                  row(768), row(1024), row(512), row(512), heads(256), heads(256), heads(MLA_V)],
        out_specs=[row(512), row(512), row(LANES), full((768, 512)), full((N_DEV, 512, LANES)),
                   full((1, 512)), full((1, 512)), vec, vec, vec, vec],
        scratch=[pltpu.VMEM((768, 512), jnp.float32), pltpu.VMEM((N_DEV, 512, LANES), jnp.float32)],
        sem=("arbitrary",))(proj, proj, proj, cos, sin, g_cq, g_ckv, w_uq, w_ukv, g_qn, g_qr, g_kn, g_kr,
                            qb, kvb, cqn, ckvn, dqc, dkc, dv)


def _swa_bwd(proj, posc, posr, gq, gk, sinks, d_y, y_a, lse, after):
    s = proj.shape[0]
    b = SWA_BLOCK
    nb = s // b
    scale = SWA_DIM ** -0.5

    def body(q_ref, kp_ref, kc_ref, vp_ref, vc_ref, pq_ref, pkp_ref, pkc_ref, gq_ref, gk_ref, sink_ref,
             do_ref, y_ref, lse_ref, kfull_ref,
             dq_ref, dk_ref, dv_ref, dgq_ref, dgk_ref, dsink_ref, dk_acc, dv_acc):
        n = pl.program_id(0)

        @pl.when(n == 0)
        def _():
            dk_acc[...] = jnp.zeros_like(dk_acc)
            dv_acc[...] = jnp.zeros_like(dv_acc)
            dgq_ref[...] = jnp.zeros_like(dgq_ref)
            dsink_ref[...] = jnp.zeros_like(dsink_ref)

        kn, v, bias = _swa_common(n, kp_ref[...], kc_ref[...], vp_ref[...], vc_ref[...],
                                  pq_ref[...], pkp_ref[...], pkc_ref[...], gk_ref[...])
        lo = _lo_mask((b, LANES))
        col = lax.broadcasted_iota(jnp.int32, (b, SWA_Q_HEADS), 1)
        col1 = lax.broadcasted_iota(jnp.int32, (1, SWA_Q_HEADS), 1)
        lse_t = lse_ref[...]
        dk_blk = jnp.zeros((2 * b, LANES), jnp.float32)
        dv_blk = jnp.zeros((2 * b, LANES), jnp.float32)
        dgq = jnp.zeros((1, LANES), jnp.float32)
        dsink = jnp.zeros((1, SWA_Q_HEADS), jnp.float32)
        for j in range(SWA_Q_HEADS // 2):
            hk = (2 * j) // (SWA_Q_HEADS // SWA_KV_HEADS)
            kvmask = lo if hk == 0 else jnp.logical_not(lo)
            sl = slice(LANES * j, LANES * (j + 1))
            qn, xn, r = _norm_fwd(q_ref[:, sl], gq_ref[...], half=True)
            qn = qn * scale
            qsw = pltpu.roll(qn, 64, 1)
            d2 = do_ref[:, sl]
            d2sw = pltpu.roll(d2, 64, 1)
            prod = d2 * y_ref[:, sl]
            dqs = []
            for e in range(2):
                h = 2 * j + e
                half_e = lo if e == 0 else jnp.logical_not(lo)
                qm = jnp.where(kvmask, qn if e == hk else qsw, 0.0)
                dm = jnp.where(kvmask, d2 if e == hk else d2sw, 0.0)
                sc = _dot_nt(qm, kn) + _alibi_slope(h) * bias
                lse_h = jnp.sum(jnp.where(col == h, lse_t, 0.0), -1, keepdims=True)
                p = jnp.exp(sc - lse_h)
                dd = jnp.sum(jnp.where(half_e, prod, 0.0), -1, keepdims=True)
                dp = _dot_nt(dm, v)
                ds = (p * (dp - dd)).astype(MXU)
                dsink = dsink - jnp.where(col1 == h, jnp.sum(jnp.exp(sink_ref[h] - lse_h) * dd), 0.0)
                dq_m = _dot(ds, kn) * scale
                dk_blk = dk_blk + _dot_tn(ds, qm)
                dv_blk = dv_blk + _dot_tn(p, dm)
                dqs.append(dq_m if e == hk else pltpu.roll(dq_m, 64, 1))
            dx, dg = _norm_bwd(xn, r, gq_ref[...], jnp.where(lo, dqs[0], dqs[1]), half=True)
            dq_ref[:, sl] = dx.astype(dq_ref.dtype)
            dgq = dgq + dg
        dgq_ref[...] += dgq
        dsink_ref[...] += dsink
        prev = pl.ds(pl.multiple_of(jnp.maximum(n - 1, 0) * b, b), b)
        cur = pl.ds(pl.multiple_of(n * b, b), b)
        dk_acc[prev, :] += dk_blk[:b]
        dv_acc[prev, :] += dv_blk[:b]
        dk_acc[cur, :] += dk_blk[b:]
        dv_acc[cur, :] += dv_blk[b:]

        @pl.when(n == nb - 1)
        def _():
            _, kxn, kr = _norm_fwd(kfull_ref[...], gk_ref[...], half=True)
            dx, dg = _norm_bwd(kxn, kr, gk_ref[...], dk_acc[...], half=True)
            dk_ref[...] = dx.astype(dk_ref.dtype)
            dv_ref[...] = dv_acc[...].astype(dv_ref.dtype)
            dgk_ref[...] = dg

    full = pl.BlockSpec((s, LANES), lambda n: (0, 0))
    vec = pl.BlockSpec((1, LANES), lambda n: (0, 0))
    return _pcall(
        body, name="swa_bwd", grid=(nb,),
        out_shape=[_sds((s, 1024), MXU), _sds((s, LANES), MXU), _sds((s, LANES), MXU),
                   _sds((1, LANES), jnp.float32), _sds((1, LANES), jnp.float32),
                   _sds((1, SWA_Q_HEADS), jnp.float32)],
        in_specs=_swa_specs(s) + [pl.BlockSpec((b, 1024), lambda n: (n, 0)), pl.BlockSpec((b, 1024), lambda n: (n, 0)),
                                  pl.BlockSpec((b, SWA_Q_HEADS), lambda n: (n, 0)),
                                  pl.BlockSpec((s, LANES), lambda n: (0, C_KA // LANES))],
        out_specs=[pl.BlockSpec((b, 1024), lambda n: (n, 0)), full, full, vec, vec,
                   pl.BlockSpec((1, SWA_Q_HEADS), lambda n: (0, 0))],
        scratch=[pltpu.VMEM((s, LANES), jnp.float32), pltpu.VMEM((s, LANES), jnp.float32)],
        sem=("arbitrary",), after=after)(proj, proj, proj, proj, proj, posc, posr, posr, gq, gk, sinks, d_y, y_a, lse,
                                         proj)


def _dx(d_proj, w_in, x, g, d_h1, after):
    s, d = x.shape
    n = w_in.shape[0]
    tm = min(2 * ROW_TILE, s)

    n_pc = len(d_proj)

    def body(*refs):
        dp_refs, (w_ref, x_ref, g_ref, dh_ref, dx_ref, dg_ref) = refs[:n_pc], refs[n_pc:]
        i = pl.program_id(0)

        @pl.when(i == 0)
        def _():
            dg_ref[...] = jnp.zeros_like(dg_ref)

        d_hn = _dot(jnp.concatenate([r[...] for r in dp_refs], axis=1), w_ref[...])
        _, xn, r = _norm_fwd(x_ref[...], g_ref[...])
        dx, dg = _norm_bwd(xn, r, g_ref[...], d_hn)
        dx_ref[...] = dh_ref[...] + dx
        dg_ref[...] += dg

    row = pl.BlockSpec((tm, d), lambda i: (i, 0))
    vec = pl.BlockSpec((1, d), lambda i: (0, 0))
    return _pcall(
        body, name="grad_x", grid=(s // tm,),
        out_shape=[_sds((s, d), jnp.float32), _sds((1, d), jnp.float32)],
        in_specs=[pl.BlockSpec((tm, p.shape[1]), lambda i: (i, 0)) for p in d_proj] + [
                  pl.BlockSpec((n, d), lambda i: (0, 0), pipeline_mode=pl.Buffered(1)), row, vec, row],
        out_specs=[row, vec], sem=("arbitrary",), after=after)(*d_proj, w_in, x, g, d_h1)


_SMALL = ["attn_norm_g", "swa_q_norm_g", "swa_k_norm_g", "swa_sinks", "mla_cq_norm_g", "mla_ckv_norm_g",
          "mla_qn_norm_g", "mla_qr_norm_g", "mla_kn_norm_g", "mla_kr_norm_g", "mem_norm_g",
          "mem_q_norm_g", "mem_k_norm_g", "ffn_norm_g"]


def kernel(x, mem, positions, attn_norm_g, w_in, swa_q_norm_g, swa_k_norm_g, swa_sinks, mla_cq_norm_g, mla_ckv_norm_g, w_uq, w_ukv, mla_qn_norm_g, mla_qr_norm_g, mla_kn_norm_g, mla_kr_norm_g, mem_norm_g, w_mem_kv, mem_q_norm_g, mem_k_norm_g, w_out, ffn_norm_g, w_gate, w_up, w_down, loss_target, m_attn_norm_g, m_w_in, m_swa_q_norm_g, m_swa_k_norm_g, m_swa_sinks, m_mla_cq_norm_g, m_mla_ckv_norm_g, m_w_uq, m_w_ukv, m_mla_qn_norm_g, m_mla_qr_norm_g, m_mla_kn_norm_g, m_mla_kr_norm_g, m_mem_norm_g, m_w_mem_kv, m_mem_q_norm_g, m_mem_k_norm_g, m_w_out, m_ffn_norm_g, m_w_gate, m_w_up, m_w_down, v_attn_norm_g, v_w_in, v_swa_q_norm_g, v_swa_k_norm_g, v_swa_sinks, v_mla_cq_norm_g, v_mla_ckv_norm_g, v_w_uq, v_w_ukv, v_mla_qn_norm_g, v_mla_qr_norm_g, v_mla_kn_norm_g, v_mla_kr_norm_g, v_mem_norm_g, v_w_mem_kv, v_mem_q_norm_g, v_mem_k_norm_g, v_w_out, v_ffn_norm_g, v_w_gate, v_w_up, v_w_down):
    args = dict(locals())
    x2, mem2, tgt = x[0], mem[0], loss_target[0]
    s, d = x2.shape
    n_in = w_in.shape[2]
    f = w_gate.shape[2]

    in_shards = [w_in[0].T.astype(WIRE)]
    (g_in,) = _all_gather_background(in_shards, 7, "all_gather_in_weights")
    tok = in_shards[0]
    mix_shards = [w_uq[0].T.astype(WIRE), w_ukv[0].astype(WIRE), w_mem_kv[0].astype(WIRE),
                  _to_wire([w_out[0]], tok, "wire_out")[0]]
    g_uq, wkv, g_mkv, g_out = _all_gather_background(mix_shards, 5, "all_gather_mix_weights")
    ffn_shards = [_to_wire([w_gate[0].T, w_up[0].T], tok, "wire_gate_up")]
    (w_gu,) = _all_gather_background(ffn_shards, 1, "all_gather_ffn_weights")
    down_shards = [_to_wire([w_down[0]], tok, "wire_down")[0]]
    (w_d,) = _all_gather_background(down_shards, 6, "all_gather_down_weights")
    wi = g_in.reshape(N_DEV * n_in, d)
    wi = jnp.concatenate([wi[0:1024], wi[1280:1792], wi[1792:2304], wi[2368:2880],
                          wi[1024:1152], wi[1152:1280], wi[2304:2368],
                          jnp.zeros((IN_PAD - 2880, d), wi.dtype)], axis=0)
    wq = g_uq.reshape(768, 512)
    wq = jnp.concatenate([wq[192 * h: 192 * h + 128] for h in range(4)]
                         + [wq[192 * h + 128: 192 * (h + 1)] for h in range(4)], axis=0)
    wmkv = g_mkv.reshape(-1, g_mkv.shape[-1])
    wo = g_out.reshape(-1, d)

    pos = positions[0].astype(jnp.float32)
    inv_freq = ROPE_THETA ** (-jnp.arange(0, MLA_ROPE, 2, dtype=jnp.float32) / MLA_ROPE)
    ang = pos[:, None] * inv_freq
    cos32, sin32 = jnp.cos(ang), jnp.sin(ang)
    cos_t = jnp.tile(cos32, (1, 4))
    sin_t = jnp.tile(jnp.concatenate([-sin32, sin32], axis=1), (1, 2))
    posc, posr = pos.reshape(s, 1), pos.reshape(1, s)
    two = lambda g: jnp.tile(g, (1, 2))
    gq2, gk2, gqr2, gkr2 = two(swa_q_norm_g), two(swa_k_norm_g), two(mla_qr_norm_g), two(mla_kr_norm_g)
    sinks1 = swa_sinks[0]

    hn = _norm_rows(x2, attn_norm_g)
    proj = _mm(hn, wi, tb=True, out_dtype=jnp.float32, tm=FFN_TILE, tk=d, name="in_proj")
    qc, kc, vb, qb, kvb, cqn, ckvn = _mla_prep(proj, cos_t, sin_t, mla_cq_norm_g, mla_ckv_norm_g, wq, wkv,
                                                mla_qn_norm_g, gqr2, mla_kn_norm_g, gkr2)
    y_b, lse_b = _mla_fwd(qc, kc, vb)
    km, vmm, kvm, memn = _memkv_prep(mem2, mem_norm_g, wmkv, mem_k_norm_g)
    y_m, lse_m = _mem_fwd(proj, mem_q_norm_g, km, vmm)
    y_a, lse_a = _swa_fwd(proj, posc, posr, gq2, gk2, sinks1)
    h1, fn = _out_proj(y_a, y_b, y_m, x2, wo, ffn_norm_g)
    gu, act = _ffn_gu(fn, w_gu)
    dout, dout_b, loss_tile = _ffn_down(act, w_d, h1, tgt)

    dgu, dw_d = _ffn_bwd_act(dout_b, w_d, gu)
    (r_d,) = _exchange_grads_background([dw_d], 8, "exchange_down_grads")
    dw_gu = _ffn_dw_gu(fn, dgu)
    (r_gu,) = _exchange_grads_background([dw_gu], 2, "exchange_ffn_grads")
    d_h1, dg_ffn = _ffn_norm_bwd(_ffn_dfn(dgu, w_gu, dw_gu), dout, h1, ffn_norm_g)
    d_y = _mm(d_h1, wo, tb=True, out_dtype=jnp.float32, tm=FFN_TILE, tk=2048, name="d_mix")
    dw_out = jnp.concatenate([
        _mm(y_a, d_h1, ta=True, out_dtype=WIRE, tm=1024, tk=1024, name="dw_out_a"),
        _mm(y_b, d_h1, ta=True, out_dtype=WIRE, tm=1024, tk=1024, name="dw_out_b"),
        _mm(y_m, d_h1, ta=True, out_dtype=WIRE, tm=1024, tk=1024, name="dw_out_m")], axis=0)
    d_qm, dkm, dvmm, dg_mq = _mem_bwd(proj, mem_q_norm_g, km, vmm, d_y, y_m, lse_m)
    dw_mkv, dg_mem, dg_mk = _memkv_bwd(mem2, mem_norm_g, wmkv, mem_k_norm_g, kvm, memn, dkm, dvmm)
    r_mkv, r_out = _exchange_grads_background([dw_mkv.reshape(g_mkv.shape), dw_out.reshape(g_out.shape)], 3,
                                              "exchange_mix_grads")
    dqc, dkc, dvb = _mla_bwd(qc, kc, vb, d_y, y_b, lse_b, dw_mkv)
    (d_cq, d_ckv, d_kr, dw_uq, dw_ukv, dg_cq, dg_ckv, dg_qn, dg_qr, dg_kn, dg_kr) = _mla_prep_bwd(
        proj, cos_t, sin_t, mla_cq_norm_g, mla_ckv_norm_g, wq, wkv, mla_qn_norm_g, gqr2, mla_kn_norm_g, gkr2,
        qb, kvb, cqn, ckvn, dqc, dkc, dvb)
    d_qa, d_ka, d_va, dg_q, dg_k, d_sinks = _swa_bwd(proj, posc, posr, gq2, gk2, sinks1, d_y, y_a, lse_a, dw_out)
    d_proj = [d_qa, d_cq, d_ckv, d_qm, d_ka, d_va, d_kr]
    gi = _dw_in(hn, d_proj, n_in, None)

    gq_ = jnp.concatenate(sum([[dw_uq[128 * h: 128 * (h + 1)], dw_uq[512 + 64 * h: 512 + 64 * (h + 1)]]
                               for h in range(4)], []), axis=0)
    gq_ = gq_.reshape(N_DEV, 96, 512)
    r_in, r_uq, r_ukv = _exchange_grads_background([gi, gq_, dw_ukv], 4, "exchange_in_grads")
    grad_x, dg_attn = _dx(d_proj, wi, x2, attn_norm_g, d_h1, gi)
    small_g = {
        "attn_norm_g": dg_attn, "swa_q_norm_g": dg_q, "swa_k_norm_g": dg_k,
        "swa_sinks": d_sinks, "mla_cq_norm_g": dg_cq, "mla_ckv_norm_g": dg_ckv, "mla_qn_norm_g": dg_qn,
        "mla_qr_norm_g": dg_qr, "mla_kn_norm_g": dg_kn, "mla_kr_norm_g": dg_kr,
        "mem_norm_g": dg_mem, "mem_q_norm_g": dg_mq, "mem_k_norm_g": dg_mk, "ffn_norm_g": dg_ffn}
    pack = _small_pack([small_g[n] for n in _SMALL], loss_tile, [args[n].shape[-1] for n in _SMALL])
    pack8 = jnp.broadcast_to(pack, (N_DEV,) + pack.shape[1:])
    (packs,) = _exchange_grads_background([pack8], 9, "exchange_small_grads")

    big = {}
    last = [None]

    def adam(name, r, transposed=False, which=None):
        w, m, v = args[name][0], args["m_" + name][0], args["v_" + name][0]
        if transposed:
            outs = _adam_big(r, w.T, m.T, v.T, "adam_" + name, last[0], which)
            big[name] = [o.T[None] for o in outs]
        else:
            outs = _adam_big(r, w, m, v, "adam_" + name, last[0])
            big[name] = [o[None] for o in outs]
        last[0] = outs[0]

    adam("w_down", r_d)
    adam("w_gate", r_gu, True, which=0)
    adam("w_up", r_gu, True, which=1)
    adam("w_out", r_out)
    adam("w_mem_kv", r_mkv)
    adam("w_in", r_in, True)
    adam("w_uq", r_uq, True)
    adam("w_ukv", r_ukv)

    loss11, small_out = _small_adam(packs, [args[n] for n in _SMALL], [args["m_" + n] for n in _SMALL],
                                    [args["v_" + n] for n in _SMALL], last[0])
    small = dict(zip(_SMALL, small_out))
    loss = loss11.reshape(())

    order = ["attn_norm_g", "w_in", "swa_q_norm_g", "swa_k_norm_g", "swa_sinks", "mla_cq_norm_g", "mla_ckv_norm_g",
             "w_uq", "w_ukv", "mla_qn_norm_g", "mla_qr_norm_g", "mla_kn_norm_g", "mla_kr_norm_g", "mem_norm_g",
             "w_mem_kv", "mem_q_norm_g", "mem_k_norm_g", "w_out", "ffn_norm_g", "w_gate", "w_up", "w_down"]
    res = {n: (big[n] if n in big else list(small[n])) for n in order}
    outs = [loss, grad_x[None]]
    for kind in range(4):
        outs += [res[n][kind] for n in order]
    return tuple(outs)
```

```python
import jax
import jax.numpy as jnp
from jax import lax
from jax.experimental import pallas as pl
from jax.experimental.pallas import tpu as pltpu
from jax.experimental.pallas import tpu_sc as plsc

MXU = jnp.bfloat16
WIRE = jnp.bfloat16
EPS = 1e-6
NEG_INF = -1e30
LOG2E = 1.4426950408889634
N_DEV = 8
LANES = 128
ROW_TILE = 256
FFN_TILE = 512
ATT_TILE = 1024
SWA_BLOCK = 128
VMEM_LIMIT = 56 * 1024 * 1024

SWA_Q_HEADS, SWA_KV_HEADS, SWA_DIM = 16, 2, 64
MLA_HEADS, MLA_NOPE, MLA_ROPE, MLA_V = 4, 128, 64, 128
MEM_HEADS, MEM_DIM = 4, 128
ROPE_THETA = 10000.0
ADAM_LR, ADAM_B1, ADAM_B2, ADAM_EPS, ADAM_WD, ADAM_STEP = 0.001, 0.9, 0.999, 1e-08, 0.01, 10

C_QA, C_CQ, C_CKV, C_QM, C_KA, C_VA, C_KR, IN_PAD = 0, 1024, 1536, 2048, 2560, 2688, 2816, 2944


def _pcall(body, *, name, out_shape, in_specs, out_specs, grid=(), scratch=(), sem=None, after=None):
    params = pltpu.CompilerParams(dimension_semantics=sem, vmem_limit_bytes=VMEM_LIMIT)
    if after is not None:
        n_in, inner = len(in_specs), body

        def body(*refs):
            inner(*refs[:n_in], *refs[n_in + 1:])

        in_specs = list(in_specs) + [pl.BlockSpec(memory_space=pl.ANY)]
    call = pl.pallas_call(body, name=name, grid=grid, in_specs=in_specs, out_specs=out_specs,
                          out_shape=out_shape, scratch_shapes=list(scratch), compiler_params=params)
    return call if after is None else (lambda *ops: call(*ops, after))


def _sds(shape, dtype):
    return jax.ShapeDtypeStruct(tuple(shape), dtype)


def _dot(a, b):
    return jnp.dot(a.astype(MXU), b.astype(MXU), preferred_element_type=jnp.float32)


def _dot_nt(a, b):
    return lax.dot_general(a.astype(MXU), b.astype(MXU), (((1,), (1,)), ((), ())),
                           preferred_element_type=jnp.float32)


def _dot_tn(a, b):
    return lax.dot_general(a.astype(MXU), b.astype(MXU), (((0,), (0,)), ((), ())),
                           preferred_element_type=jnp.float32)


def _lo_mask(shape):
    return (lax.broadcasted_iota(jnp.int32, shape, len(shape) - 1) % LANES) < 64


def _norm_fwd(x, g, half=False):
    x2 = x * x
    if half:
        lo = _lo_mask(x.shape)
        s_lo = jnp.sum(jnp.where(lo, x2, 0.0), -1, keepdims=True)
        s_hi = jnp.sum(jnp.where(lo, 0.0, x2), -1, keepdims=True)
        r = jnp.where(lo, lax.rsqrt(s_lo / 64.0 + EPS), lax.rsqrt(s_hi / 64.0 + EPS))
    else:
        r = lax.rsqrt(jnp.mean(x2, -1, keepdims=True) + EPS)
    xn = x * r
    return xn * g, xn, r


def _norm_bwd(xn, r, g, dy, half=False):
    t = dy * g
    tx = t * xn
    if half:
        lo = _lo_mask(xn.shape)
        m_lo = jnp.sum(jnp.where(lo, tx, 0.0), -1, keepdims=True) / 64.0
        m_hi = jnp.sum(jnp.where(lo, 0.0, tx), -1, keepdims=True) / 64.0
        m = jnp.where(lo, m_lo, m_hi)
    else:
        m = jnp.mean(tx, -1, keepdims=True)
    dx = r * (t - xn * m)
    dg = jnp.sum(dy * xn, 0, keepdims=True)
    return dx, dg


def _swap32(x):
    lane = lax.broadcasted_iota(jnp.int32, x.shape, 1)
    return jnp.where((lane % 64) < 32, pltpu.roll(x, 96, 1), pltpu.roll(x, 32, 1))


def _rope(x, cos, sin):
    return x * cos + _swap32(x) * sin


def _rope_bwd(d, cos, sin):
    return d * cos + _swap32(d * sin)


def _my_coords():
    return lax.axis_index("x"), lax.axis_index("y"), lax.axis_index("c")


def _dev_index(px, py, pc):
    return 4 * px + 2 * py + pc


_FLIPS = [(0, 0, 1), (0, 1, 0), (0, 1, 1), (1, 0, 0), (1, 0, 1), (1, 1, 0), (1, 1, 1)]


def _flip(coords, f):
    return tuple((1 - v) if b else v for v, b in zip(coords, f))


def _all_gather(shards):
    n = len(shards)

    def body(*refs):
        ins, outs = refs[:n], refs[n:2 * n]
        send_sems, recv_sems, local_sems = refs[2 * n:]
        x, y, c = _my_coords()
        me, sibling = (x, y, c), (x, y, 1 - c)
        chips = [(1 - x, y), (x, 1 - y), (1 - x, 1 - y)]

        def copy(w, k, block, to, src=None):
            dst = outs[w].at[_dev_index(*block)]
            return pltpu.make_async_remote_copy(
                src_ref=dst if src is None else src, dst_ref=dst,
                send_sem=send_sems.at[w, k], recv_sem=recv_sems.at[w, k],
                device_id=to, device_id_type=pl.DeviceIdType.MESH)

        sends, locals_ = [], []
        for w in range(n):
            mine = pltpu.make_async_copy(ins[w], outs[w].at[_dev_index(*me)], local_sems.at[w])
            mine.start()
            locals_.append(mine)
            first = [copy(w, 0, me, sibling, src=ins[w])]
            first += [copy(w, 1 + j, me, (*chip, c), src=ins[w]) for j, chip in enumerate(chips)]
            for cp in first:
                cp.start()
            sends += first
        for w in range(n):
            for j, chip in enumerate(chips):
                copy(w, 1 + j, (*chip, c), me).wait_recv()
                fwd = copy(w, 4 + j, (*chip, c), sibling)
                fwd.start()
                sends.append(fwd)
        for w in range(n):
            copy(w, 0, sibling, me).wait_recv()
            for j, chip in enumerate(chips):
                copy(w, 4 + j, (*chip, 1 - c), me).wait_recv()
        for cp in sends:
            cp.wait_send()
        for mine in locals_:
            mine.wait()

    any_spec = pl.BlockSpec(memory_space=pl.ANY)
    return _pcall(
        body, name="all_gather_weights",
        out_shape=[_sds((N_DEV,) + s.shape, s.dtype) for s in shards],
        in_specs=[any_spec] * n, out_specs=[any_spec] * n,
        scratch=[pltpu.SemaphoreType.DMA((n, 7)), pltpu.SemaphoreType.DMA((n, 7)),
                 pltpu.SemaphoreType.DMA((n,))])(*shards)


def _wire_cost(arrays):
    nbytes = sum(a.size * a.dtype.itemsize for a in arrays)
    return pl.CostEstimate(flops=0, transcendentals=0, bytes_accessed=40 * nbytes)


def _all_gather_background(shards, collective_id, name):
    n = len(shards)
    src_refs = [jax.new_ref(s, memory_space=pltpu.MemorySpace.HBM) for s in shards]
    out_refs = [jax.empty_ref(_sds((N_DEV,) + s.shape, s.dtype), memory_space=pltpu.MemorySpace.HBM) for s in shards]

    @pl.kernel(mesh=plsc.ScalarSubcoreMesh(axis_name="seq", num_cores=1), name=name,
               scratch_types=(pltpu.SemaphoreType.DMA((n, 7)), pltpu.SemaphoreType.DMA((n, 7)),
                              pltpu.SemaphoreType.DMA((n,))),
               compiler_params=pltpu.CompilerParams(collective_id=collective_id))
    def launch(send_sems, recv_sems, local_sems):
        x, y, c = _my_coords()
        me, sibling = (x, y, c), (x, y, 1 - c)
        chips = [(1 - x, y), (x, 1 - y), (1 - x, 1 - y)]
        barrier = pltpu.get_barrier_semaphore()
        for peer in [sibling] + [(*chip, c) for chip in chips]:
            pl.semaphore_signal(barrier, inc=1, device_id=peer, device_id_type=pl.DeviceIdType.MESH)
        pl.semaphore_wait(barrier, 4)

        def copy(w, k, block, to, src=None):
            dst = out_refs[w].at[_dev_index(*block)]
            return pltpu.make_async_remote_copy(
                src_ref=dst if src is None else src, dst_ref=dst,
                send_sem=send_sems.at[w, k], recv_sem=recv_sems.at[w, k],
                device_id=to, device_id_type=pl.DeviceIdType.MESH)

        sends, locals_ = [], []
        for w in range(n):
            mine = pltpu.make_async_copy(src_refs[w], out_refs[w].at[_dev_index(*me)], local_sems.at[w])
            mine.start()
            locals_.append(mine)
            first = [copy(w, 0, me, sibling, src=src_refs[w])]
            first += [copy(w, 1 + j, me, (*chip, c), src=src_refs[w]) for j, chip in enumerate(chips)]
            for cp in first:
                cp.start()
            sends += first
        for w in range(n):
            for j, chip in enumerate(chips):
                copy(w, 1 + j, (*chip, c), me).wait_recv()
                fwd = copy(w, 4 + j, (*chip, c), sibling)
                fwd.start()
                sends.append(fwd)
        for w in range(n):
            copy(w, 0, sibling, me).wait_recv()
            for j, chip in enumerate(chips):
                copy(w, 4 + j, (*chip, 1 - c), me).wait_recv()
        for cp in sends:
            cp.wait_send()
        for mine in locals_:
            mine.wait()

    launch()
    return [r[...] for r in out_refs]


def _exchange_grads(grads):
    n = len(grads)

    def body(*refs):
        ins, outs = refs[:n], refs[n:2 * n]
        send_sems, recv_sems, local_sems = refs[2 * n:]
        me = _my_coords()
        my_idx = _dev_index(*me)
        sends, locals_ = [], []
        for w in range(n):
            mine = pltpu.make_async_copy(ins[w].at[my_idx], outs[w].at[my_idx], local_sems.at[w])
            mine.start()
            locals_.append(mine)
            for k, f in enumerate(_FLIPS):
                peer = _flip(me, f)
                cp = pltpu.make_async_remote_copy(
                    src_ref=ins[w].at[_dev_index(*peer)], dst_ref=outs[w].at[my_idx],
                    send_sem=send_sems.at[w, k], recv_sem=recv_sems.at[w, k],
                    device_id=peer, device_id_type=pl.DeviceIdType.MESH)
                cp.start()
                sends.append(cp)
        for w in range(n):
            for k, f in enumerate(_FLIPS):
                peer = _flip(me, f)
                slot = outs[w].at[_dev_index(*peer)]
                pltpu.make_async_remote_copy(
                    src_ref=slot, dst_ref=slot,
                    send_sem=send_sems.at[w, k], recv_sem=recv_sems.at[w, k],
                    device_id=peer, device_id_type=pl.DeviceIdType.MESH).wait_recv()
        for cp in sends:
            cp.wait_send()
        for mine in locals_:
            mine.wait()

    any_spec = pl.BlockSpec(memory_space=pl.ANY)
    return _pcall(
        body, name="exchange_grads",
        out_shape=[_sds(g.shape, g.dtype) for g in grads],
        in_specs=[any_spec] * n, out_specs=[any_spec] * n,
        scratch=[pltpu.SemaphoreType.DMA((n, 7)), pltpu.SemaphoreType.DMA((n, 7)),
                 pltpu.SemaphoreType.DMA((n,))])(*grads)


def _exchange_grads_background(grads, collective_id, name):
    n = len(grads)
    src_refs = [jax.new_ref(g, memory_space=pltpu.MemorySpace.HBM) for g in grads]
    out_refs = [jax.empty_ref(_sds(g.shape, g.dtype), memory_space=pltpu.MemorySpace.HBM) for g in grads]

    @pl.kernel(mesh=plsc.ScalarSubcoreMesh(axis_name="seq", num_cores=1), name=name,
               scratch_types=(pltpu.SemaphoreType.DMA((n, 7)), pltpu.SemaphoreType.DMA((n, 7)),
                              pltpu.SemaphoreType.DMA((n,))),
               cost_estimate=_wire_cost(grads),
               compiler_params=pltpu.CompilerParams(collective_id=collective_id))
    def launch(send_sems, recv_sems, local_sems):
        me = _my_coords()
        my_idx = _dev_index(*me)
        peers = [_flip(me, f) for f in _FLIPS]
        barrier = pltpu.get_barrier_semaphore()
        for peer in peers:
            pl.semaphore_signal(barrier, inc=1, device_id=peer, device_id_type=pl.DeviceIdType.MESH)
        pl.semaphore_wait(barrier, len(peers))
        sends, locals_ = [], []
        for w in range(n):
            mine = pltpu.make_async_copy(src_refs[w].at[my_idx], out_refs[w].at[my_idx], local_sems.at[w])
            mine.start()
            locals_.append(mine)
            for k, peer in enumerate(peers):
                cp = pltpu.make_async_remote_copy(
                    src_ref=src_refs[w].at[_dev_index(*peer)], dst_ref=out_refs[w].at[my_idx],
                    send_sem=send_sems.at[w, k], recv_sem=recv_sems.at[w, k],
                    device_id=peer, device_id_type=pl.DeviceIdType.MESH)
                cp.start()
                sends.append(cp)
        for w in range(n):
            for k, peer in enumerate(peers):
                slot = out_refs[w].at[_dev_index(*peer)]
                pltpu.make_async_remote_copy(
                    src_ref=slot, dst_ref=slot, send_sem=send_sems.at[w, k], recv_sem=recv_sems.at[w, k],
                    device_id=peer, device_id_type=pl.DeviceIdType.MESH).wait_recv()
        for cp in sends:
            cp.wait_send()
        for mine in locals_:
            mine.wait()

    launch()
    return [r[...] for r in out_refs]


def _to_wire(parts, after, name):
    n = len(parts)
    rows, cols = parts[0].shape
    tr = rows // 2 if rows % 32 == 0 else rows

    def body(*refs):
        for k in range(n):
            refs[n][k] = refs[k][...].astype(WIRE)

    blk = pl.BlockSpec((tr, cols), lambda i: (i, 0))
    return _pcall(
        body, name=name, grid=(rows // tr,), out_shape=_sds((n, rows, cols), WIRE),
        in_specs=[blk] * n, out_specs=pl.BlockSpec((n, tr, cols), lambda i: (0, i, 0)),
        sem=("parallel",), after=after)(*parts)


def _adam_math(w, g, m, v):
    m = ADAM_B1 * m + (1.0 - ADAM_B1) * g
    v = ADAM_B2 * v + (1.0 - ADAM_B2) * (g * g)
    m_hat = m / (1.0 - ADAM_B1 ** ADAM_STEP)
    v_hat = v / (1.0 - ADAM_B2 ** ADAM_STEP)
    delta = -ADAM_LR * (m_hat / (jnp.sqrt(v_hat) + ADAM_EPS) + ADAM_WD * w)
    return delta, m, v


def _small_layout(sizes):
    row0, r = [], 0
    for n in sizes:
        row0.append(r)
        r += -(-n // LANES)
    return row0, r, -(-(r + 1) // 8) * 8


def _small_pieces(n):
    return [(k, min(LANES, n - LANES * k)) for k in range(-(-n // LANES))]


def _small_fill(pack, slot, srcs, sizes, row0, rows):
    pack[slot] = jnp.zeros((rows, LANES), jnp.float32)
    for p, n in enumerate(sizes):
        val = srcs[p][...]
        if val.shape[-1] == LANES and n == 64:
            pack[slot, row0[p]:row0[p] + 1, :] = val + pltpu.roll(val, 64, 1)
            continue
        for k, width in _small_pieces(n):
            pack[slot, row0[p] + k:row0[p] + k + 1, 0:width] = srcs[p][:, LANES * k:LANES * k + width]


def _small_pack(grads, loss_tile, sizes):
    n_par = len(sizes)
    row0, loss_row, rows = _small_layout(sizes)

    def body(*refs):
        g_refs, loss_in, out_ref = refs[:n_par], refs[n_par], refs[n_par + 1]
        _small_fill(out_ref, 0, g_refs, sizes, row0, rows)
        out_ref[0, loss_row:loss_row + 1, :] = loss_in[0:1, :]

    vm = pl.BlockSpec(memory_space=pltpu.VMEM)
    return _pcall(
        body, name="small_pack", out_shape=_sds((1, rows, LANES), jnp.float32),
        in_specs=[vm] * (n_par + 1), out_specs=vm)(*grads, loss_tile)


def _small_adam(packs, ws, ms, vs, after):
    sizes = [w.shape[-1] for w in ws]
    n_par = len(ws)
    row0, loss_row, rows = _small_layout(sizes)

    def body(*refs):
        g_ref = refs[0]
        w_refs, m_refs, v_refs = (refs[1 + k * n_par: 1 + (k + 1) * n_par] for k in range(3))
        loss_out = refs[3 * n_par + 1]
        out_refs = refs[3 * n_par + 2: 7 * n_par + 2]
        pack, res = refs[7 * n_par + 2:]
        for slot, srcs in enumerate((w_refs, m_refs, v_refs)):
            _small_fill(pack, slot, srcs, sizes, row0, rows)
        g = g_ref[0]
        for dev in range(1, N_DEV):
            g = g + g_ref[dev]
        delta, m, v = _adam_math(pack[0], g, pack[1], pack[2])
        res[0], res[1], res[2], res[3] = g, delta, m, v
        loss_out[...] = res[0, loss_row:loss_row + 1, 0:1]
        for p, n in enumerate(sizes):
            for kind in range(4):
                for k, width in _small_pieces(n):
                    out_refs[4 * p + kind][:, LANES * k:LANES * k + width] = (
                        res[kind, row0[p] + k:row0[p] + k + 1, 0:width])

    vm = pl.BlockSpec(memory_space=pltpu.VMEM)
    out_shape = [_sds((1, 1), jnp.float32)]
    for n in sizes:
        out_shape += [_sds((1, n), jnp.float32)] * 4
    outs = _pcall(
        body, name="small_adam", out_shape=out_shape,
        in_specs=[vm] * (3 * n_par + 1), out_specs=[vm] * len(out_shape),
        scratch=[pltpu.VMEM((3, rows, LANES), jnp.float32), pltpu.VMEM((4, rows, LANES), jnp.float32)],
        after=after)(packs, *ws, *ms, *vs)
    return outs[0], [outs[1 + 4 * p: 5 + 4 * p] for p in range(n_par)]


def _adam_big(recv, w, m, v, name, after=None, which=None):
    rows, cols = recv.shape[-2:]
    row_tiles = [t for t in range(16, rows + 1, 16) if rows % t == 0 and t * cols <= 400 * 1024]
    tr, tc = (max(row_tiles), cols) if row_tiles else (rows, 512 if cols % 512 == 0 else cols)

    def body(r_ref, w_ref, m_ref, v_ref, g_ref, d_ref, mo_ref, vo_ref):
        g = r_ref[0].astype(jnp.float32)
        for d in range(1, N_DEV):
            g = g + r_ref[d].astype(jnp.float32)
        delta, mn, vn = _adam_math(w_ref[...], g, m_ref[...], v_ref[...])
        g_ref[...] = g
        d_ref[...] = delta
        mo_ref[...] = mn
        vo_ref[...] = vn

    blk = pl.BlockSpec((tr, tc), lambda i, j: (i, j))
    if which is None:
        r_spec = pl.BlockSpec((N_DEV, tr, tc), lambda i, j: (0, i, j))
    else:
        r_spec = pl.BlockSpec((N_DEV, None, tr, tc), lambda i, j: (0, which, i, j))
    return _pcall(
        body, name=name, grid=(rows // tr, cols // tc),
        out_shape=[_sds((rows, cols), jnp.float32)] * 4,
        in_specs=[r_spec, blk, blk, blk],
        out_specs=[blk] * 4, sem=("parallel", "parallel"), after=after)(recv, w, m, v)


def _mm(a, b, *, ta=False, tb=False, out_dtype, tm, tk, name):
    (kdim, mdim) = a.shape if ta else a.shape[::-1]
    ndim = b.shape[0] if tb else b.shape[1]
    tm, tk = min(tm, mdim), min(tk, kdim)
    nk = kdim // tk

    def body(a_ref, b_ref, o_ref, acc):
        k = pl.program_id(1)
        if ta:
            part = _dot_tn(a_ref[...], b_ref[...])
        elif tb:
            part = _dot_nt(a_ref[...], b_ref[...])
        else:
            part = _dot(a_ref[...], b_ref[...])

        @pl.when(k == 0)
        def _():
            acc[...] = part

        @pl.when(k > 0)
        def _():
            acc[...] += part

        @pl.when(k == nk - 1)
        def _():
            o_ref[...] = acc[...].astype(o_ref.dtype)

    a_spec = pl.BlockSpec((tk, tm), lambda i, k: (k, i)) if ta else pl.BlockSpec((tm, tk), lambda i, k: (i, k))
    b_spec = pl.BlockSpec((ndim, tk), lambda i, k: (0, k)) if tb else pl.BlockSpec((tk, ndim), lambda i, k: (k, 0))
    return _pcall(
        body, name=name, grid=(mdim // tm, nk), out_shape=_sds((mdim, ndim), out_dtype),
        in_specs=[a_spec, b_spec], out_specs=pl.BlockSpec((tm, ndim), lambda i, k: (i, 0)),
        scratch=[pltpu.VMEM((tm, ndim), jnp.float32)], sem=("parallel", "arbitrary"))(a, b)


def _ref_col_pieces(start, stop):
    ref_starts = [0, 1024, 1152, 1280, 1792, 2304, 2368, 2880]
    perm_starts = [C_QA, C_KA, C_VA, C_CQ, C_CKV, C_KR, C_QM]
    out = []
    for p in range(7):
        lo, hi = max(start, ref_starts[p]), min(stop, ref_starts[p + 1])
        if lo < hi:
            out.append((lo - start, perm_starts[p] + lo - ref_starts[p], hi - lo))
    return out


def _dw_in(hn, d_proj, n_shard, after):
    s, d = hn.shape
    n = sum(p.shape[1] for p in d_proj)
    n_pc = len(d_proj)
    tm, tk = min(512, d), min(1024, s)
    nk = s // tk

    def body(a_ref, *refs):
        b_refs, (o_ref, acc) = refs[:n_pc], refs[n_pc:]
        k = pl.program_id(1)
        part = _dot_tn(a_ref[...], jnp.concatenate([r[...] for r in b_refs], axis=1))

        @pl.when(k == 0)
        def _():
            acc[...] = part

        @pl.when(k > 0)
        def _():
            acc[...] += part

        @pl.when(k == nk - 1)
        def _():
            t = acc[...].T
            for j in range(N_DEV):
                rows = [t[src:src + width] for _, src, width in _ref_col_pieces(j * n_shard, (j + 1) * n_shard)]
                o_ref[j] = jnp.concatenate(rows, axis=0).astype(o_ref.dtype)

    return _pcall(
        body, name="dw_in", grid=(d // tm, nk), out_shape=_sds((N_DEV, n_shard, d), WIRE),
        in_specs=[pl.BlockSpec((tk, tm), lambda i, k: (k, i))]
        + [pl.BlockSpec((tk, p.shape[1]), lambda i, k: (k, 0)) for p in d_proj],
        out_specs=pl.BlockSpec((N_DEV, n_shard, tm), lambda i, k: (0, 0, i)),
        scratch=[pltpu.VMEM((tm, n), jnp.float32)], sem=("parallel", "arbitrary"), after=after)(hn, *d_proj)


def _in_proj(x, g, w):
    s, d = x.shape
    n = w.shape[0]
    tm = min(2 * ROW_TILE, s)

    def body(x_ref, g_ref, w_ref, p_ref, hn_ref):
        hn, _, _ = _norm_fwd(x_ref[...], g_ref[...])
        hn_ref[...] = hn.astype(hn_ref.dtype)
        p_ref[...] = _dot_nt(hn, w_ref[...])

    return _pcall(
        body, name="in_proj", grid=(s // tm,),
        out_shape=[_sds((s, n), jnp.float32), _sds((s, d), MXU)],
        in_specs=[pl.BlockSpec((tm, d), lambda i: (i, 0)), pl.BlockSpec((1, d), lambda i: (0, 0)),
                  pl.BlockSpec((n, d), lambda i: (0, 0), pipeline_mode=pl.Buffered(1))],
        out_specs=[pl.BlockSpec((tm, n), lambda i: (i, 0)), pl.BlockSpec((tm, d), lambda i: (i, 0))],
        sem=("parallel",))(x, g, w)


def _norm_rows(x, g):
    s, d = x.shape
    tm = min(FFN_TILE, s)

    def body(x_ref, g_ref, hn_ref):
        hn, _, _ = _norm_fwd(x_ref[...], g_ref[...])
        hn_ref[...] = hn.astype(hn_ref.dtype)

    row = pl.BlockSpec((tm, d), lambda i: (i, 0))
    return _pcall(body, name="norm_rows", grid=(s // tm,), out_shape=_sds((s, d), MXU),
                  in_specs=[row, pl.BlockSpec((1, d), lambda i: (0, 0))], out_specs=row, sem=("parallel",))(x, g)


def _mla_prep(proj, cos, sin, g_cq, g_ckv, w_uq, w_ukv, g_qn, g_qr, g_kn, g_kr):
    s = proj.shape[0]
    tm = min(ROW_TILE, s)
    nh = MLA_HEADS

    def body(cq_ref, ckv_ref, kr_ref, cos_ref, sin_ref, gcq_ref, gckv_ref, wuq_ref, wukv_ref,
             gqn_ref, gqr_ref, gkn_ref, gkr_ref,
             qc_ref, kc_ref, v_ref, qb_ref, kvb_ref, cqn_ref, ckvn_ref):
        cos_t, sin_t = cos_ref[...], sin_ref[...]
        lo = _lo_mask((tm, LANES))
        cqn, _, _ = _norm_fwd(cq_ref[...], gcq_ref[...])
        cqn_ref[...] = cqn.astype(cqn_ref.dtype)
        qb = _dot_nt(cqn, wuq_ref[...])
        qb_ref[...] = qb
        ckvn, _, _ = _norm_fwd(ckv_ref[...], gckv_ref[...])
        ckvn_ref[...] = ckvn.astype(ckvn_ref.dtype)
        w_ukv_full = jnp.concatenate([wukv_ref[dev] for dev in range(N_DEV)], axis=1)
        kvb = _dot(ckvn, w_ukv_full)
        kvb_ref[...] = kvb
        kr, _, _ = _norm_fwd(kr_ref[...], gkr_ref[...], half=True)
        kr = _rope(kr, cos_t, sin_t)
        kr2 = jnp.where(lo, kr, pltpu.roll(kr, 64, 1))
        ropes = []
        for j in range(nh // 2):
            xr = qb[:, nh * MLA_NOPE + LANES * j: nh * MLA_NOPE + LANES * (j + 1)]
            qr, _, _ = _norm_fwd(xr, gqr_ref[...], half=True)
            ropes.append(_rope(qr, cos_t, sin_t))
        for h in range(nh):
            qn, _, _ = _norm_fwd(qb[:, MLA_NOPE * h: MLA_NOPE * (h + 1)], gqn_ref[...])
            mask = lo if h % 2 == 0 else jnp.logical_not(lo)
            qr = jnp.where(mask, ropes[h // 2], 0.0)
            qc_ref[h] = jnp.concatenate([qn, qr], axis=1).astype(qc_ref.dtype)
            kn, _, _ = _norm_fwd(kvb[:, 256 * h: 256 * h + MLA_NOPE], gkn_ref[...])
            kc_ref[h] = jnp.concatenate([kn, kr2], axis=1).astype(kc_ref.dtype)
            v_ref[h] = kvb[:, 256 * h + MLA_NOPE: 256 * (h + 1)].astype(v_ref.dtype)

    def col(width, start):
        return pl.BlockSpec((tm, width), lambda i: (i, start // width))

    def full(shape):
        return pl.BlockSpec(shape, lambda i: (0,) * len(shape))

    def row(width):
        return pl.BlockSpec((tm, width), lambda i: (i, 0))

    def heads(width):
        return pl.BlockSpec((nh, tm, width), lambda i: (0, i, 0))

    return _pcall(
        body, name="mla_prep", grid=(s // tm,),
        out_shape=[_sds((nh, s, 256), MXU), _sds((nh, s, 256), MXU), _sds((nh, s, MLA_V), MXU),
                   _sds((s, 768), jnp.float32), _sds((s, 1024), jnp.float32),
                   _sds((s, 512), MXU), _sds((s, 512), MXU)],
        in_specs=[col(512, C_CQ), col(512, C_CKV), col(LANES, C_KR), row(LANES), row(LANES),
                  full((1, 512)), full((1, 512)), full((768, 512)), full((N_DEV, 512, LANES)),
                  full((1, LANES)), full((1, LANES)), full((1, LANES)), full((1, LANES))],
        out_specs=[heads(256), heads(256), heads(MLA_V), row(768), row(1024), row(512), row(512)],
        sem=("parallel",))(proj, proj, proj, cos, sin, g_cq, g_ckv, w_uq, w_ukv, g_qn, g_qr, g_kn, g_kr)


def _tri_rows(p, nb):
    i = sum(jnp.where(p >= (r * (r + 1)) // 2, 1, 0) for r in range(1, nb))
    return i, p - (i * (i + 1)) // 2


def _tri_cols(p, nb):
    j = sum(jnp.where(p >= r * nb - (r * (r - 1)) // 2, 1, 0) for r in range(1, nb))
    return j, j + p - (j * nb - (j * (j - 1)) // 2)


def _mla_fwd(qc, kc, v):
    nh, s, _ = qc.shape
    t = min(ATT_TILE, s)
    nb = s // t
    scale = (MLA_NOPE + MLA_ROPE) ** -0.5

    def body(q_ref, k_ref, v_ref, y_ref, lse_ref, m_sc, l_sc, acc):
        qi, ki = _tri_rows(pl.program_id(1), nb)

        @pl.when(ki == 0)
        def _():
            m_sc[...] = jnp.full_like(m_sc, NEG_INF)
            l_sc[...] = jnp.zeros_like(l_sc)
            acc[...] = jnp.zeros_like(acc)

        def step(diagonal):
            rc = t // 4 if diagonal else t
            for c in range(t // rc):
                rows = slice(rc * c, rc * (c + 1))
                keys = slice(0, rc * (c + 1))
                sc = _dot_nt(q_ref[0, rows, :], k_ref[0, keys, :]) * (scale * LOG2E)
                if diagonal:
                    r_i = lax.broadcasted_iota(jnp.int32, sc.shape, 0) + rc * c
                    c_i = lax.broadcasted_iota(jnp.int32, sc.shape, 1)
                    sc = jnp.where(c_i <= r_i, sc, NEG_INF)
                m_old = m_sc[rows, :]
                m_new = jnp.maximum(m_old, jnp.max(sc, -1, keepdims=True))
                alpha = jnp.exp2(m_old - m_new)
                p = jnp.exp2(sc - m_new)
                l_sc[rows, :] = alpha * l_sc[rows, :] + jnp.sum(p, -1, keepdims=True)
                acc[rows, :] = alpha * acc[rows, :] + _dot(p, v_ref[0, keys, :])
                m_sc[rows, :] = m_new

        @pl.when(ki < qi)
        def _():
            step(False)

        @pl.when(ki == qi)
        def _():
            step(True)

        @pl.when(ki == qi)
        def _():
            y_ref[...] = acc[...] / l_sc[...]
            lse_ref[0] = m_sc[...] + jnp.log2(l_sc[...])

    return _pcall(
        body, name="mla_fwd", grid=(nh, (nb * (nb + 1)) // 2),
        out_shape=[_sds((s, nh * MLA_V), jnp.float32), _sds((nh, s, 1), jnp.float32)],
        in_specs=[pl.BlockSpec((1, t, 256), lambda h, p: (h, _tri_rows(p, nb)[0], 0)),
                  pl.BlockSpec((1, t, 256), lambda h, p: (h, _tri_rows(p, nb)[1], 0)),
                  pl.BlockSpec((1, t, MLA_V), lambda h, p: (h, _tri_rows(p, nb)[1], 0))],
        out_specs=[pl.BlockSpec((t, MLA_V), lambda h, p: (_tri_rows(p, nb)[0], h)),
                   pl.BlockSpec((1, t, 1), lambda h, p: (h, _tri_rows(p, nb)[0], 0))],
        scratch=[pltpu.VMEM((t, 1), jnp.float32), pltpu.VMEM((t, 1), jnp.float32),
                 pltpu.VMEM((t, MLA_V), jnp.float32)],
        sem=("parallel", "arbitrary"))(qc, kc, v)


def _memkv_prep(mem, g_mem, w_mkv, g_mk):
    ml, d = mem.shape
    hw = MEM_HEADS * MEM_DIM

    def body(mem_ref, g_ref, w_ref, gk_ref, k_ref, v_ref, kv_ref, mn_ref):
        mn, _, _ = _norm_fwd(mem_ref[...], g_ref[...])
        mn_ref[...] = mn.astype(mn_ref.dtype)
        kv = _dot(mn, w_ref[...])
        kv_ref[...] = kv
        for h in range(MEM_HEADS):
            kn, _, _ = _norm_fwd(kv[:, MEM_DIM * h: MEM_DIM * (h + 1)], gk_ref[...])
            k_ref[:, MEM_DIM * h: MEM_DIM * (h + 1)] = kn.astype(k_ref.dtype)
        v_ref[...] = kv[:, hw:].astype(v_ref.dtype)

    vm = pl.BlockSpec(memory_space=pltpu.VMEM)
    return _pcall(
        body, name="memkv_prep",
        out_shape=[_sds((ml, hw), MXU), _sds((ml, hw), MXU), _sds((ml, 2 * hw), jnp.float32), _sds((ml, d), MXU)],
        in_specs=[vm] * 4, out_specs=[vm] * 4)(mem, g_mem, w_mkv, g_mk)


def _mem_fwd(proj, g_mq, km, vmm):
    s = proj.shape[0]
    ml, hw = km.shape
    tm = min(FFN_TILE, s)
    scale = MEM_DIM ** -0.5

    def body(q_ref, g_ref, k_ref, v_ref, y_ref, lse_ref):
        col = lax.broadcasted_iota(jnp.int32, (tm, MEM_HEADS), 1)
        lse_t = jnp.zeros((tm, MEM_HEADS), jnp.float32)
        for h in range(MEM_HEADS):
            sl = slice(MEM_DIM * h, MEM_DIM * (h + 1))
            qn, _, _ = _norm_fwd(q_ref[:, sl], g_ref[...])
            sc = _dot_nt(qn, k_ref[:, sl]) * scale
            m = jnp.max(sc, -1, keepdims=True)
            p = jnp.exp(sc - m)
            l = jnp.sum(p, -1, keepdims=True)
            y_ref[:, sl] = _dot(p, v_ref[:, sl]) / l
            lse_t = jnp.where(col == h, m + jnp.log(l), lse_t)
        lse_ref[...] = lse_t

    return _pcall(
        body, name="mem_fwd", grid=(s // tm,),
        out_shape=[_sds((s, hw), jnp.float32), _sds((s, MEM_HEADS), jnp.float32)],
        in_specs=[pl.BlockSpec((tm, hw), lambda i: (i, C_QM // hw)), pl.BlockSpec((1, MEM_DIM), lambda i: (0, 0)),
                  pl.BlockSpec((ml, hw), lambda i: (0, 0)), pl.BlockSpec((ml, hw), lambda i: (0, 0))],
        out_specs=[pl.BlockSpec((tm, hw), lambda i: (i, 0)), pl.BlockSpec((tm, MEM_HEADS), lambda i: (i, 0))],
        sem=("parallel",))(proj, g_mq, km, vmm)


def _alibi_slope(h):
    return float(2.0 ** (-8.0 * (h + 1) / SWA_Q_HEADS))


def _swa_common(n, kp, kc, vp, vc, pq, pkp, pkc, gk):
    b = SWA_BLOCK
    k_raw = jnp.concatenate([kp, kc], axis=0)
    kn, kxn, kr = _norm_fwd(k_raw, gk, half=True)
    v = jnp.concatenate([vp, vc], axis=0)
    dist = jnp.abs(pq - jnp.concatenate([pkp, pkc], axis=1))
    r_i = lax.broadcasted_iota(jnp.int32, (b, 2 * b), 0)
    c_i = lax.broadcasted_iota(jnp.int32, (b, 2 * b), 1)
    valid = (c_i > r_i) & (c_i <= r_i + b) & (c_i >= jnp.where(n > 0, 0, b))
    bias = jnp.where(valid, -dist, NEG_INF)
    return kn, v, bias


def _swa_folded(n, kp, kc, pq, pkp, pkc, gk):
    b = SWA_BLOCK
    kn_p, _, _ = _norm_fwd(kp, gk, half=True)
    kn_c, _, _ = _norm_fwd(kc, gk, half=True)
    r_i = lax.broadcasted_iota(jnp.int32, (b, b), 0)
    c_i = lax.broadcasted_iota(jnp.int32, (b, b), 1)
    upper = c_i > r_i
    bias_prev = jnp.where(n > 0, 0.0, NEG_INF) - jnp.abs(pq - pkp)
    bias = jnp.where(upper, bias_prev, -jnp.abs(pq - pkc))
    return kn_p, kn_c, bias, upper


def _swa_specs(s):
    b = SWA_BLOCK
    prev = lambda n: jnp.maximum(n - 1, 0)
    return [
        pl.BlockSpec((b, 1024), lambda n: (n, C_QA // 1024)),
        pl.BlockSpec((b, LANES), lambda n: (prev(n), C_KA // LANES)),
        pl.BlockSpec((b, LANES), lambda n: (n, C_KA // LANES)),
        pl.BlockSpec((b, LANES), lambda n: (prev(n), C_VA // LANES)),
        pl.BlockSpec((b, LANES), lambda n: (n, C_VA // LANES)),
        pl.BlockSpec((b, 1), lambda n: (n, 0)),
        pl.BlockSpec((1, b), lambda n: (0, prev(n))),
        pl.BlockSpec((1, b), lambda n: (0, n)),
        pl.BlockSpec((1, LANES), lambda n: (0, 0)),
        pl.BlockSpec((1, LANES), lambda n: (0, 0)),
        pl.BlockSpec(memory_space=pltpu.SMEM),
    ]


def _swa_fwd(proj, posc, posr, gq, gk, sinks):
    s = proj.shape[0]
    b = SWA_BLOCK
    scale = SWA_DIM ** -0.5

    def body(q_ref, kp_ref, kc_ref, vp_ref, vc_ref, pq_ref, pkp_ref, pkc_ref, gq_ref, gk_ref, sink_ref,
             y_ref, lse_ref):
        n = pl.program_id(0)
        kn_p, kn_c, bias, upper = _swa_folded(n, kp_ref[...], kc_ref[...], pq_ref[...], pkp_ref[...], pkc_ref[...],
                                              gk_ref[...])
        v_p, v_c = vp_ref[...], vc_ref[...]
        lo = _lo_mask((b, LANES))
        col = lax.broadcasted_iota(jnp.int32, (b, SWA_Q_HEADS), 1)
        lse_t = jnp.zeros((b, SWA_Q_HEADS), jnp.float32)
        hpg = SWA_Q_HEADS // SWA_KV_HEADS
        for g in range(SWA_KV_HEADS):
            heads = range(hpg * g, hpg * (g + 1))
            kvmask = lo if g == 0 else jnp.logical_not(lo)
            qs = []
            for j in range(hpg // 2 * g, hpg // 2 * (g + 1)):
                qn, _, _ = _norm_fwd(q_ref[:, LANES * j: LANES * (j + 1)], gq_ref[...], half=True)
                qn = qn * scale
                qsw = pltpu.roll(qn, 64, 1)
                qs += [jnp.where(kvmask, qn if e == g else qsw, 0.0) for e in range(2)]
            q_st = jnp.concatenate(qs, axis=0).astype(MXU)
            sp_st, sc_st = _dot_nt(q_st, kn_p), _dot_nt(q_st, kn_c)
            pus, pls, ls = [], [], []
            for i, h in enumerate(heads):
                rows = slice(b * i, b * (i + 1))
                sc = jnp.where(upper, sp_st[rows], sc_st[rows]) + _alibi_slope(h) * bias
                sk = sink_ref[h]
                m = jnp.maximum(jnp.max(sc, -1, keepdims=True), sk)
                p = jnp.exp(sc - m)
                l = jnp.sum(p, -1, keepdims=True) + jnp.exp(sk - m)
                pus.append(jnp.where(upper, p, 0.0).astype(MXU))
                pls.append(jnp.where(upper, 0.0, p).astype(MXU))
                ls.append(l)
                lse_t = jnp.where(col == h, m + jnp.log(l), lse_t)
            o_st = _dot(jnp.concatenate(pus, axis=0), v_p) + _dot(jnp.concatenate(pls, axis=0), v_c)
            for j in range(hpg // 2 * g, hpg // 2 * (g + 1)):
                halves = []
                for e in range(2):
                    i = 2 * j + e - hpg * g
                    o_h = o_st[b * i: b * (i + 1)] / ls[i]
                    halves.append(o_h if e == g else pltpu.roll(o_h, 64, 1))
                y_ref[:, LANES * j: LANES * (j + 1)] = jnp.where(lo, halves[0], halves[1])
        lse_ref[...] = lse_t

    return _pcall(
        body, name="swa_fwd", grid=(s // b,),
        out_shape=[_sds((s, 1024), jnp.float32), _sds((s, SWA_Q_HEADS), jnp.float32)],
        in_specs=_swa_specs(s),
        out_specs=[pl.BlockSpec((b, 1024), lambda n: (n, 0)), pl.BlockSpec((b, SWA_Q_HEADS), lambda n: (n, 0))],
        sem=("parallel",))(proj, proj, proj, proj, proj, posc, posr, posr, gq, gk, sinks)


def _out_proj(y_a, y_b, y_m, x, w_out, g_ffn):
    s, d = x.shape
    tm = min(2 * ROW_TILE, s)

    def body(ya_ref, yb_ref, ym_ref, x_ref, w_ref, g_ref, h1_ref, fn_ref):
        y = jnp.concatenate([ya_ref[...].astype(MXU), yb_ref[...].astype(MXU), ym_ref[...].astype(MXU)], axis=1)
        h1 = x_ref[...] + _dot(y, w_ref[...])
        h1_ref[...] = h1
        fn, _, _ = _norm_fwd(h1, g_ref[...])
        fn_ref[...] = fn.astype(fn_ref.dtype)

    def row(width):
        return pl.BlockSpec((tm, width), lambda i: (i, 0))

    return _pcall(
        body, name="out_proj", grid=(s // tm,),
        out_shape=[_sds((s, d), jnp.float32), _sds((s, d), MXU)],
        in_specs=[row(1024), row(512), row(512), row(d),
                  pl.BlockSpec(w_out.shape, lambda i: (0, 0), pipeline_mode=pl.Buffered(1)),
                  pl.BlockSpec((1, d), lambda i: (0, 0))],
        out_specs=[row(d), row(d)], sem=("parallel",))(y_a, y_b, y_m, x, w_out, g_ffn)


def _ffn_gu(fn, w_gu):
    s, d = fn.shape
    f = w_gu.shape[2]
    tm = min(2 * FFN_TILE, s)

    def body(fn_ref, w_ref, gu_ref, act_ref):
        x = fn_ref[...]
        g = _dot_nt(x, w_ref[0, 0])
        u = _dot_nt(x, w_ref[0, 1])
        gu_ref[0, 0] = g
        gu_ref[0, 1] = u
        act_ref[0] = (g * jax.nn.sigmoid(g) * u).astype(act_ref.dtype)

    return _pcall(
        body, name="ffn_gate_up", grid=(N_DEV, s // tm),
        out_shape=[_sds((N_DEV, 2, s, f), jnp.float32), _sds((N_DEV, s, f), MXU)],
        in_specs=[pl.BlockSpec((tm, d), lambda j, i: (i, 0)),
                  pl.BlockSpec((1, 2, f, d), lambda j, i: (j, 0, 0, 0))],
        out_specs=[pl.BlockSpec((1, 2, tm, f), lambda j, i: (j, 0, i, 0)),
                   pl.BlockSpec((1, tm, f), lambda j, i: (j, i, 0))],
        sem=("parallel", "parallel"))(fn, w_gu)


def _ffn_down(act, w_d, h1, target):
    _, s, f = act.shape
    d = h1.shape[1]
    tm = min(FFN_TILE, s)

    def body(a_ref, w_ref, h1_ref, t_ref, dout_ref, doutb_ref, loss_ref, acc):
        i, j = pl.program_id(0), pl.program_id(1)
        part = _dot(a_ref[0], w_ref[0]) + _dot(a_ref[1], w_ref[1])

        @pl.when(j == 0)
        def _():
            acc[...] = h1_ref[...] + part

        @pl.when(j > 0)
        def _():
            acc[...] += part

        @pl.when((i == 0) & (j == 0))
        def _():
            loss_ref[...] = jnp.zeros_like(loss_ref)

        @pl.when(j == N_DEV // 2 - 1)
        def _():
            diff = acc[...] - t_ref[...]
            dout_ref[...] = diff / d
            doutb_ref[...] = (diff / d).astype(doutb_ref.dtype)
            loss_ref[...] += 0.5 * jnp.sum(jnp.sum(diff * diff, -1, keepdims=True) / d)

    row = pl.BlockSpec((tm, d), lambda i, j: (i, 0))
    return _pcall(
        body, name="ffn_down", grid=(s // tm, N_DEV // 2),
        out_shape=[_sds((s, d), jnp.float32), _sds((s, d), MXU), _sds((8, LANES), jnp.float32)],
        in_specs=[pl.BlockSpec((2, tm, f), lambda i, j: (j, i, 0)), pl.BlockSpec((2, f, d), lambda i, j: (j, 0, 0)),
                  row, row],
        out_specs=[row, row, pl.BlockSpec((8, LANES), lambda i, j: (0, 0))],
        scratch=[pltpu.VMEM((tm, d), jnp.float32)], sem=("arbitrary", "arbitrary"))(act, w_d, h1, target)


def _ffn_bwd_act(dout, w_d, gu):
    s, d = dout.shape
    f = w_d.shape[1]
    tm = min(2 * FFN_TILE, s)
    ni = s // tm

    def body(do_ref, w_ref, gu_ref, dgu_ref, dw_ref, acc):
        i = pl.program_id(1)
        do = do_ref[...]
        d_act = _dot_nt(do, w_ref[0])
        g, u = gu_ref[0, 0], gu_ref[0, 1]
        sig = jax.nn.sigmoid(g)
        silu = g * sig
        dgu_ref[0, 0] = (d_act * u * (sig * (1.0 + g * (1.0 - sig)))).astype(dgu_ref.dtype)
        dgu_ref[0, 1] = (d_act * silu).astype(dgu_ref.dtype)
        part = _dot_tn(silu * u, do)

        @pl.when(i == 0)
        def _():
            acc[...] = part

        @pl.when(i > 0)
        def _():
            acc[...] += part

        @pl.when(i == ni - 1)
        def _():
            dw_ref[0] = acc[...].astype(dw_ref.dtype)

    return _pcall(
        body, name="ffn_bwd_act", grid=(N_DEV, ni),
        out_shape=[_sds((N_DEV, 2, s, f), MXU), _sds((N_DEV, f, d), WIRE)],
        in_specs=[pl.BlockSpec((tm, d), lambda j, i: (i, 0)), pl.BlockSpec((1, f, d), lambda j, i: (j, 0, 0)),
                  pl.BlockSpec((1, 2, tm, f), lambda j, i: (j, 0, i, 0))],
        out_specs=[pl.BlockSpec((1, 2, tm, f), lambda j, i: (j, 0, i, 0)),
                   pl.BlockSpec((1, f, d), lambda j, i: (j, 0, 0))],
        scratch=[pltpu.VMEM((f, d), jnp.float32)], sem=("parallel", "arbitrary"))(dout, w_d, gu)


def _ffn_dw_gu(fn, dgu):
    s, d = fn.shape
    f = dgu.shape[-1]
    tk = min(4 * FFN_TILE, s)
    nk = s // tk

    def body(fn_ref, dgu_ref, dw_ref, acc):
        k = pl.program_id(2)
        part = _dot_tn(dgu_ref[0, 0], fn_ref[...])

        @pl.when(k == 0)
        def _():
            acc[...] = part

        @pl.when(k > 0)
        def _():
            acc[...] += part

        @pl.when(k == nk - 1)
        def _():
            dw_ref[0, 0] = acc[...].astype(dw_ref.dtype)

    return _pcall(
        body, name="ffn_dw_gate_up", grid=(N_DEV, 2, nk),
        out_shape=_sds((N_DEV, 2, f, d), WIRE),
        in_specs=[pl.BlockSpec((tk, d), lambda j, w, k: (k, 0)),
                  pl.BlockSpec((1, 1, tk, f), lambda j, w, k: (j, w, k, 0))],
        out_specs=pl.BlockSpec((1, 1, f, d), lambda j, w, k: (j, w, 0, 0)),
        scratch=[pltpu.VMEM((f, d), jnp.float32)], sem=("parallel", "parallel", "arbitrary"))(fn, dgu)


def _ffn_dfn(dgu, w_gu, after):
    _, _, s, f = dgu.shape
    d = w_gu.shape[3]
    tm = min(FFN_TILE, s)

    def body(dgu_ref, w_ref, dfn_ref):
        j = pl.program_id(1)
        part = (_dot(dgu_ref[0, 0], w_ref[0, 0]) + _dot(dgu_ref[0, 1], w_ref[0, 1])
                + _dot(dgu_ref[1, 0], w_ref[1, 0]) + _dot(dgu_ref[1, 1], w_ref[1, 1]))

        @pl.when(j == 0)
        def _():
            dfn_ref[...] = part

        @pl.when(j > 0)
        def _():
            dfn_ref[...] += part

    return _pcall(
        body, name="ffn_dfn", grid=(s // tm, N_DEV // 2),
        out_shape=_sds((s, d), jnp.float32),
        in_specs=[pl.BlockSpec((2, 2, tm, f), lambda i, j: (j, 0, i, 0)),
                  pl.BlockSpec((2, 2, f, d), lambda i, j: (j, 0, 0, 0))],
        out_specs=pl.BlockSpec((tm, d), lambda i, j: (i, 0)),
        sem=("parallel", "arbitrary"), after=after)(dgu, w_gu)


def _ffn_norm_bwd(d_fn, dout, h1, g_ffn):
    s, d = h1.shape
    tm = min(2 * ROW_TILE, s)

    def body(dfn_ref, do_ref, h1_ref, g_ref, dh1_ref, dg_ref):
        i = pl.program_id(0)

        @pl.when(i == 0)
        def _():
            dg_ref[...] = jnp.zeros_like(dg_ref)

        _, xn, r = _norm_fwd(h1_ref[...], g_ref[...])
        dx, dg = _norm_bwd(xn, r, g_ref[...], dfn_ref[...])
        dh1_ref[...] = do_ref[...] + dx
        dg_ref[...] += dg

    row = pl.BlockSpec((tm, d), lambda i: (i, 0))
    vec = pl.BlockSpec((1, d), lambda i: (0, 0))
    return _pcall(
        body, name="ffn_norm_bwd", grid=(s // tm,),
        out_shape=[_sds((s, d), jnp.float32), _sds((1, d), jnp.float32)],
        in_specs=[row, row, row, vec], out_specs=[row, vec], sem=("arbitrary",))(d_fn, dout, h1, g_ffn)


def _mem_bwd(proj, g_mq, km, vmm, d_y, y_m, lse):
    s = proj.shape[0]
    ml, hw = km.shape
    tm = min(FFN_TILE, s)
    scale = MEM_DIM ** -0.5

    def body(q_ref, g_ref, k_ref, v_ref, do_ref, y_ref, lse_ref, dq_ref, dk_ref, dv_ref, dg_ref):
        i = pl.program_id(0)

        @pl.when(i == 0)
        def _():
            dk_ref[...] = jnp.zeros_like(dk_ref)
            dv_ref[...] = jnp.zeros_like(dv_ref)
            dg_ref[...] = jnp.zeros_like(dg_ref)

        col = lax.broadcasted_iota(jnp.int32, (tm, MEM_HEADS), 1)
        lse_t = lse_ref[...]
        for h in range(MEM_HEADS):
            sl = slice(MEM_DIM * h, MEM_DIM * (h + 1))
            qn, xn, r = _norm_fwd(q_ref[:, sl], g_ref[...])
            lse_h = jnp.sum(jnp.where(col == h, lse_t, 0.0), -1, keepdims=True)
            p = jnp.exp(_dot_nt(qn, k_ref[:, sl]) * scale - lse_h)
            do = do_ref[:, sl]
            dd = jnp.sum(do * y_ref[:, sl], -1, keepdims=True)
            dp = _dot_nt(do, v_ref[:, sl])
            ds = (p * (dp - dd)).astype(MXU)
            dv_ref[:, sl] += _dot_tn(p, do)
            dk_ref[:, sl] += _dot_tn(ds, qn) * scale
            dx, dg = _norm_bwd(xn, r, g_ref[...], _dot(ds, k_ref[:, sl]) * scale)
            dq_ref[:, sl] = dx.astype(dq_ref.dtype)
            dg_ref[...] += dg

    full = pl.BlockSpec((ml, hw), lambda i: (0, 0))
    return _pcall(
        body, name="mem_bwd", grid=(s // tm,),
        out_shape=[_sds((s, hw), MXU), _sds((ml, hw), jnp.float32), _sds((ml, hw), jnp.float32),
                   _sds((1, MEM_DIM), jnp.float32)],
        in_specs=[pl.BlockSpec((tm, hw), lambda i: (i, C_QM // hw)), pl.BlockSpec((1, MEM_DIM), lambda i: (0, 0)),
                  full, full, pl.BlockSpec((tm, hw), lambda i: (i, 3)), pl.BlockSpec((tm, hw), lambda i: (i, 0)),
                  pl.BlockSpec((tm, MEM_HEADS), lambda i: (i, 0))],
        out_specs=[pl.BlockSpec((tm, hw), lambda i: (i, 0)), full, full,
                   pl.BlockSpec((1, MEM_DIM), lambda i: (0, 0))],
        sem=("arbitrary",))(proj, g_mq, km, vmm, d_y, y_m, lse)


def _memkv_bwd(mem, g_mem, w_mkv, g_mk, kv, memn, dk, dv):
    ml, d = mem.shape
    hw = MEM_HEADS * MEM_DIM

    def body(mem_ref, g_ref, w_ref, gk_ref, kv_ref, mn_ref, dk_ref, dv_ref, dw_ref, dgm_ref, dgk_ref):
        parts = []
        dgk = jnp.zeros((1, MEM_DIM), jnp.float32)
        for h in range(MEM_HEADS):
            sl = slice(MEM_DIM * h, MEM_DIM * (h + 1))
            _, xn, r = _norm_fwd(kv_ref[:, sl], gk_ref[...])
            dx, dg = _norm_bwd(xn, r, gk_ref[...], dk_ref[:, sl])
            parts.append(dx)
            dgk = dgk + dg
        dkv = jnp.concatenate(parts + [dv_ref[...]], axis=1).astype(MXU)
        dgk_ref[...] = dgk
        dw_ref[...] = _dot_tn(mn_ref[...], dkv).astype(dw_ref.dtype)
        d_mn = _dot_nt(dkv, w_ref[...])
        _, xn, _ = _norm_fwd(mem_ref[...], g_ref[...])
        dgm_ref[...] = jnp.sum(d_mn * xn, 0, keepdims=True)

    vm = pl.BlockSpec(memory_space=pltpu.VMEM)
    return _pcall(
        body, name="memkv_bwd",
        out_shape=[_sds((d, 2 * hw), WIRE), _sds((1, d), jnp.float32), _sds((1, MEM_DIM), jnp.float32)],
        in_specs=[vm] * 8, out_specs=[vm] * 3)(mem, g_mem, w_mkv, g_mk, kv, memn, dk, dv)


def _mla_bwd(qc, kc, v, d_y, y_b, lse, after):
    nh, s, _ = qc.shape
    t = min(ATT_TILE, s)
    nb = s // t
    scale = (MLA_NOPE + MLA_ROPE) ** -0.5

    def body(q_ref, k_ref, v_ref, do_ref, y_ref, lse_ref, dq_ref, dk_ref, dv_ref, dk_acc, dv_acc):
        kj, qi = _tri_cols(pl.program_id(1), nb)

        @pl.when((kj == 0) & (qi == 0))
        def _():
            dq_ref[...] = jnp.zeros_like(dq_ref)

        @pl.when(qi == kj)
        def _():
            dk_acc[...] = jnp.zeros_like(dk_acc)
            dv_acc[...] = jnp.zeros_like(dv_acc)

        def step(diagonal):
            rc = t // 4 if diagonal else t
            for c in range(t // rc):
                rows = slice(rc * c, rc * (c + 1))
                keys = slice(0, rc * (c + 1))
                q, k = q_ref[0, rows, :], k_ref[0, keys, :]
                sc = _dot_nt(q, k) * (scale * LOG2E)
                if diagonal:
                    r_i = lax.broadcasted_iota(jnp.int32, sc.shape, 0) + rc * c
                    c_i = lax.broadcasted_iota(jnp.int32, sc.shape, 1)
                    sc = jnp.where(c_i <= r_i, sc, NEG_INF)
                p = jnp.exp2(sc - lse_ref[0, rows, :])
                do = do_ref[rows, :]
                dd = jnp.sum(do * y_ref[rows, :], -1, keepdims=True)
                dp = _dot_nt(do, v_ref[0, keys, :])
                ds = (p * (dp - dd) * scale).astype(MXU)
                dv_acc[keys, :] += _dot_tn(p, do)
                dk_acc[keys, :] += _dot_tn(ds, q)
                out_rows = pl.ds(pl.multiple_of(qi * t + rc * c, rc), rc)
                dq_ref[0, out_rows, :] += _dot(ds, k)

        @pl.when(qi > kj)
        def _():
            step(False)

        @pl.when(qi == kj)
        def _():
            step(True)

        @pl.when(qi == nb - 1)
        def _():
            dk_ref[0] = dk_acc[...]
            dv_ref[0] = dv_acc[...]

    qmap = lambda h, p: (h, _tri_cols(p, nb)[1], 0)
    kmap = lambda h, p: (h, _tri_cols(p, nb)[0], 0)
    return _pcall(
        body, name="mla_bwd", grid=(nh, (nb * (nb + 1)) // 2),
        out_shape=[_sds((nh, s, 256), jnp.float32), _sds((nh, s, 256), jnp.float32),
                   _sds((nh, s, MLA_V), jnp.float32)],
        in_specs=[pl.BlockSpec((1, t, 256), qmap),
                  pl.BlockSpec((1, t, 256), kmap),
                  pl.BlockSpec((1, t, MLA_V), kmap),
                  pl.BlockSpec((t, MLA_V), lambda h, p: (_tri_cols(p, nb)[1], 8 + h)),
                  pl.BlockSpec((t, MLA_V), lambda h, p: (_tri_cols(p, nb)[1], h)),
                  pl.BlockSpec((1, t, 1), qmap)],
        out_specs=[pl.BlockSpec((1, s, 256), lambda h, p: (h, 0, 0)),
                   pl.BlockSpec((1, t, 256), kmap),
                   pl.BlockSpec((1, t, MLA_V), kmap)],
        scratch=[pltpu.VMEM((t, 256), jnp.float32), pltpu.VMEM((t, MLA_V), jnp.float32)],
        sem=("parallel", "arbitrary"), after=after)(qc, kc, v, d_y, y_b, lse)


def _mla_prep_bwd(proj, cos, sin, g_cq, g_ckv, w_uq, w_ukv, g_qn, g_qr, g_kn, g_kr,
                  qb, kvb, cqn, ckvn, dqc, dkc, dv):
    s = proj.shape[0]
    tm = min(ROW_TILE, s)
    nh = MLA_HEADS
    ni = s // tm

    def body(cq_ref, ckv_ref, kr_ref, cos_ref, sin_ref, gcq_ref, gckv_ref, wuq_ref, wukv_ref,
             gqn_ref, gqr_ref, gkn_ref, gkr_ref, qb_ref, kvb_ref, cqn_ref, ckvn_ref, dqc_ref, dkc_ref, dv_ref,
             dcq_ref, dckv_ref, dkr_ref, dwuq_ref, dwukv_ref,
             dgcq_ref, dgckv_ref, dgqn_ref, dgqr_ref, dgkn_ref, dgkr_ref, acc_uq, acc_ukv):
        i = pl.program_id(0)

        @pl.when(i == 0)
        def _():
            acc_uq[...] = jnp.zeros_like(acc_uq)
            acc_ukv[...] = jnp.zeros_like(acc_ukv)
            for ref in (dgcq_ref, dgckv_ref, dgqn_ref, dgqr_ref, dgkn_ref, dgkr_ref):
                ref[...] = jnp.zeros_like(ref)

        cos_t, sin_t = cos_ref[...], sin_ref[...]
        lo = _lo_mask((tm, LANES))
        qb_v, kvb_v = qb_ref[...], kvb_ref[...]
        dq_parts, dgqn = [], jnp.zeros((1, LANES), jnp.float32)
        for h in range(nh):
            _, xn, r = _norm_fwd(qb_v[:, MLA_NOPE * h: MLA_NOPE * (h + 1)], gqn_ref[...])
            dx, dg = _norm_bwd(xn, r, gqn_ref[...], dqc_ref[h][:, :MLA_NOPE])
            dq_parts.append(dx)
            dgqn = dgqn + dg
        dgqn_ref[...] += dgqn
        dgqr = jnp.zeros((1, LANES), jnp.float32)
        for j in range(nh // 2):
            d_rope = jnp.where(lo, dqc_ref[2 * j][:, MLA_NOPE:], dqc_ref[2 * j + 1][:, MLA_NOPE:])
            d_pre = _rope_bwd(d_rope, cos_t, sin_t)
            xr = qb_v[:, nh * MLA_NOPE + LANES * j: nh * MLA_NOPE + LANES * (j + 1)]
            _, xn, r = _norm_fwd(xr, gqr_ref[...], half=True)
            dx, dg = _norm_bwd(xn, r, gqr_ref[...], d_pre, half=True)
            dq_parts.append(dx)
            dgqr = dgqr + dg
        dgqr_ref[...] += dgqr
        dqb = jnp.concatenate(dq_parts, axis=1).astype(MXU)
        acc_uq[...] += _dot_tn(dqb, cqn_ref[...])
        _, xn, r = _norm_fwd(cq_ref[...], gcq_ref[...])
        dx, dg = _norm_bwd(xn, r, gcq_ref[...], _dot(dqb, wuq_ref[...]))
        dcq_ref[...] = dx.astype(dcq_ref.dtype)
        dgcq_ref[...] += dg
        dkv_parts, dgkn = [], jnp.zeros((1, LANES), jnp.float32)
        d_kr2 = jnp.zeros((tm, LANES), jnp.float32)
        for h in range(nh):
            _, xn, r = _norm_fwd(kvb_v[:, 256 * h: 256 * h + MLA_NOPE], gkn_ref[...])
            dx, dg = _norm_bwd(xn, r, gkn_ref[...], dkc_ref[h][:, :MLA_NOPE])
            dkv_parts += [dx, dv_ref[h]]
            dgkn = dgkn + dg
            d_kr2 = d_kr2 + dkc_ref[h][:, MLA_NOPE:]
        dgkn_ref[...] += dgkn
        dkvb = jnp.concatenate(dkv_parts, axis=1).astype(MXU)
        part_ukv = _dot_tn(ckvn_ref[...], dkvb)
        for dev in range(N_DEV):
            acc_ukv[dev] += part_ukv[:, LANES * dev: LANES * (dev + 1)]
        w_ukv_full = jnp.concatenate([wukv_ref[dev] for dev in range(N_DEV)], axis=1)
        d_ckvn = _dot_nt(dkvb, w_ukv_full)
        _, xn, r = _norm_fwd(ckv_ref[...], gckv_ref[...])
        dx, dg = _norm_bwd(xn, r, gckv_ref[...], d_ckvn)
        dckv_ref[...] = dx.astype(dckv_ref.dtype)
        dgckv_ref[...] += dg
        d_kr = jnp.where(lo, d_kr2 + pltpu.roll(d_kr2, 64, 1), 0.0)
        d_pre = _rope_bwd(d_kr, cos_t, sin_t)
        _, xn, r = _norm_fwd(kr_ref[...], gkr_ref[...], half=True)
        dx, dg = _norm_bwd(xn, r, gkr_ref[...], d_pre, half=True)
        dkr_ref[...] = jnp.where(lo, dx, 0.0).astype(dkr_ref.dtype)
        dgkr_ref[...] += jnp.where(_lo_mask((1, LANES)), dg, 0.0)

        @pl.when(i == ni - 1)
        def _():
            dwuq_ref[...] = acc_uq[...].astype(dwuq_ref.dtype)
            dwukv_ref[...] = acc_ukv[...].astype(dwukv_ref.dtype)

    def col(width, start):
        return pl.BlockSpec((tm, width), lambda i: (i, start // width))

    def full(shape):
        return pl.BlockSpec(shape, lambda i: (0,) * len(shape))

    def row(width):
        return pl.BlockSpec((tm, width), lambda i: (i, 0))

    def heads(width):
        return pl.BlockSpec((nh, tm, width), lambda i: (0, i, 0))

    vec = full((1, LANES))
    return _pcall(
        body, name="mla_prep_bwd", grid=(ni,),
        out_shape=[_sds((s, 512), MXU), _sds((s, 512), MXU), _sds((s, LANES), MXU),
                   _sds((768, 512), WIRE), _sds((N_DEV, 512, LANES), WIRE),
                   _sds((1, 512), jnp.float32), _sds((1, 512), jnp.float32)] + [_sds((1, LANES), jnp.float32)] * 4,
        in_specs=[col(512, C_CQ), col(512, C_CKV), col(LANES, C_KR), row(LANES), row(LANES),
                  full((1, 512)), full((1, 512)), full((768, 512)), full((N_DEV, 512, LANES)), vec, vec, vec, vec,
                  row(768), row(1024), row(512), row(512), heads(256), heads(256), heads(MLA_V)],
        out_specs=[row(512), row(512), row(LANES), full((768, 512)), full((N_DEV, 512, LANES)),
                   full((1, 512)), full((1, 512)), vec, vec, vec, vec],
        scratch=[pltpu.VMEM((768, 512), jnp.float32), pltpu.VMEM((N_DEV, 512, LANES), jnp.float32)],
        sem=("arbitrary",))(proj, proj, proj, cos, sin, g_cq, g_ckv, w_uq, w_ukv, g_qn, g_qr, g_kn, g_kr,
                            qb, kvb, cqn, ckvn, dqc, dkc, dv)


def _swa_bwd(proj, posc, posr, gq, gk, sinks, d_y, y_a, lse, after):
    s = proj.shape[0]
    b = SWA_BLOCK
    nb = s // b
    scale = SWA_DIM ** -0.5

    def body(q_ref, kp_ref, kc_ref, vp_ref, vc_ref, pq_ref, pkp_ref, pkc_ref, gq_ref, gk_ref, sink_ref,
             do_ref, y_ref, lse_ref, kfull_ref,
             dq_ref, dk_ref, dv_ref, dgq_ref, dgk_ref, dsink_ref, dk_acc, dv_acc):
        n = pl.program_id(0)

        @pl.when(n == 0)
        def _():
            dk_acc[...] = jnp.zeros_like(dk_acc)
            dv_acc[...] = jnp.zeros_like(dv_acc)
            dgq_ref[...] = jnp.zeros_like(dgq_ref)
            dsink_ref[...] = jnp.zeros_like(dsink_ref)

        kn, v, bias = _swa_common(n, kp_ref[...], kc_ref[...], vp_ref[...], vc_ref[...],
                                  pq_ref[...], pkp_ref[...], pkc_ref[...], gk_ref[...])
        lo = _lo_mask((b, LANES))
        col = lax.broadcasted_iota(jnp.int32, (b, SWA_Q_HEADS), 1)
        col1 = lax.broadcasted_iota(jnp.int32, (1, SWA_Q_HEADS), 1)
        lse_t = lse_ref[...]
        dk_blk = jnp.zeros((2 * b, LANES), jnp.float32)
        dv_blk = jnp.zeros((2 * b, LANES), jnp.float32)
        dgq = jnp.zeros((1, LANES), jnp.float32)
        dsink = jnp.zeros((1, SWA_Q_HEADS), jnp.float32)
        for j in range(SWA_Q_HEADS // 2):
            hk = (2 * j) // (SWA_Q_HEADS // SWA_KV_HEADS)
            kvmask = lo if hk == 0 else jnp.logical_not(lo)
            sl = slice(LANES * j, LANES * (j + 1))
            qn, xn, r = _norm_fwd(q_ref[:, sl], gq_ref[...], half=True)
            qn = qn * scale
            qsw = pltpu.roll(qn, 64, 1)
            d2 = do_ref[:, sl]
            d2sw = pltpu.roll(d2, 64, 1)
            prod = d2 * y_ref[:, sl]
            dqs = []
            for e in range(2):
                h = 2 * j + e
                half_e = lo if e == 0 else jnp.logical_not(lo)
                qm = jnp.where(kvmask, qn if e == hk else qsw, 0.0)
                dm = jnp.where(kvmask, d2 if e == hk else d2sw, 0.0)
                sc = _dot_nt(qm, kn) + _alibi_slope(h) * bias
                lse_h = jnp.sum(jnp.where(col == h, lse_t, 0.0), -1, keepdims=True)
                p = jnp.exp(sc - lse_h)
                dd = jnp.sum(jnp.where(half_e, prod, 0.0), -1, keepdims=True)
                dp = _dot_nt(dm, v)
                ds = (p * (dp - dd)).astype(MXU)
                dsink = dsink - jnp.where(col1 == h, jnp.sum(jnp.exp(sink_ref[h] - lse_h) * dd), 0.0)
                dq_m = _dot(ds, kn) * scale
                dk_blk = dk_blk + _dot_tn(ds, qm)
                dv_blk = dv_blk + _dot_tn(p, dm)
                dqs.append(dq_m if e == hk else pltpu.roll(dq_m, 64, 1))
            dx, dg = _norm_bwd(xn, r, gq_ref[...], jnp.where(lo, dqs[0], dqs[1]), half=True)
            dq_ref[:, sl] = dx.astype(dq_ref.dtype)
            dgq = dgq + dg
        dgq_ref[...] += dgq
        dsink_ref[...] += dsink
        prev = pl.ds(pl.multiple_of(jnp.maximum(n - 1, 0) * b, b), b)
        cur = pl.ds(pl.multiple_of(n * b, b), b)
        dk_acc[prev, :] += dk_blk[:b]
        dv_acc[prev, :] += dv_blk[:b]
        dk_acc[cur, :] += dk_blk[b:]
        dv_acc[cur, :] += dv_blk[b:]

        @pl.when(n == nb - 1)
        def _():
            _, kxn, kr = _norm_fwd(kfull_ref[...], gk_ref[...], half=True)
            dx, dg = _norm_bwd(kxn, kr, gk_ref[...], dk_acc[...], half=True)
            dk_ref[...] = dx.astype(dk_ref.dtype)
            dv_ref[...] = dv_acc[...].astype(dv_ref.dtype)
            dgk_ref[...] = dg

    full = pl.BlockSpec((s, LANES), lambda n: (0, 0))
    vec = pl.BlockSpec((1, LANES), lambda n: (0, 0))
    return _pcall(
        body, name="swa_bwd", grid=(nb,),
        out_shape=[_sds((s, 1024), MXU), _sds((s, LANES), MXU), _sds((s, LANES), MXU),
                   _sds((1, LANES), jnp.float32), _sds((1, LANES), jnp.float32),
                   _sds((1, SWA_Q_HEADS), jnp.float32)],
        in_specs=_swa_specs(s) + [pl.BlockSpec((b, 1024), lambda n: (n, 0)), pl.BlockSpec((b, 1024), lambda n: (n, 0)),
                                  pl.BlockSpec((b, SWA_Q_HEADS), lambda n: (n, 0)),
                                  pl.BlockSpec((s, LANES), lambda n: (0, C_KA // LANES))],
        out_specs=[pl.BlockSpec((b, 1024), lambda n: (n, 0)), full, full, vec, vec,
                   pl.BlockSpec((1, SWA_Q_HEADS), lambda n: (0, 0))],
        scratch=[pltpu.VMEM((s, LANES), jnp.float32), pltpu.VMEM((s, LANES), jnp.float32)],
        sem=("arbitrary",), after=after)(proj, proj, proj, proj, proj, posc, posr, posr, gq, gk, sinks, d_y, y_a, lse,
                                         proj)


def _dx(d_proj, w_in, x, g, d_h1, after):
    s, d = x.shape
    n = w_in.shape[0]
    tm = min(2 * ROW_TILE, s)

    n_pc = len(d_proj)

    def body(*refs):
        dp_refs, (w_ref, x_ref, g_ref, dh_ref, dx_ref, dg_ref) = refs[:n_pc], refs[n_pc:]
        i = pl.program_id(0)

        @pl.when(i == 0)
        def _():
            dg_ref[...] = jnp.zeros_like(dg_ref)

        d_hn = _dot(jnp.concatenate([r[...] for r in dp_refs], axis=1), w_ref[...])
        _, xn, r = _norm_fwd(x_ref[...], g_ref[...])
        dx, dg = _norm_bwd(xn, r, g_ref[...], d_hn)
        dx_ref[...] = dh_ref[...] + dx
        dg_ref[...] += dg

    row = pl.BlockSpec((tm, d), lambda i: (i, 0))
    vec = pl.BlockSpec((1, d), lambda i: (0, 0))
    return _pcall(
        body, name="grad_x", grid=(s // tm,),
        out_shape=[_sds((s, d), jnp.float32), _sds((1, d), jnp.float32)],
        in_specs=[pl.BlockSpec((tm, p.shape[1]), lambda i: (i, 0)) for p in d_proj] + [
                  pl.BlockSpec((n, d), lambda i: (0, 0), pipeline_mode=pl.Buffered(1)), row, vec, row],
        out_specs=[row, vec], sem=("arbitrary",), after=after)(*d_proj, w_in, x, g, d_h1)


_SMALL = ["attn_norm_g", "swa_q_norm_g", "swa_k_norm_g", "swa_sinks", "mla_cq_norm_g", "mla_ckv_norm_g",
          "mla_qn_norm_g", "mla_qr_norm_g", "mla_kn_norm_g", "mla_kr_norm_g", "mem_norm_g",
          "mem_q_norm_g", "mem_k_norm_g", "ffn_norm_g"]


def kernel(x, mem, positions, attn_norm_g, w_in, swa_q_norm_g, swa_k_norm_g, swa_sinks, mla_cq_norm_g, mla_ckv_norm_g, w_uq, w_ukv, mla_qn_norm_g, mla_qr_norm_g, mla_kn_norm_g, mla_kr_norm_g, mem_norm_g, w_mem_kv, mem_q_norm_g, mem_k_norm_g, w_out, ffn_norm_g, w_gate, w_up, w_down, loss_target, m_attn_norm_g, m_w_in, m_swa_q_norm_g, m_swa_k_norm_g, m_swa_sinks, m_mla_cq_norm_g, m_mla_ckv_norm_g, m_w_uq, m_w_ukv, m_mla_qn_norm_g, m_mla_qr_norm_g, m_mla_kn_norm_g, m_mla_kr_norm_g, m_mem_norm_g, m_w_mem_kv, m_mem_q_norm_g, m_mem_k_norm_g, m_w_out, m_ffn_norm_g, m_w_gate, m_w_up, m_w_down, v_attn_norm_g, v_w_in, v_swa_q_norm_g, v_swa_k_norm_g, v_swa_sinks, v_mla_cq_norm_g, v_mla_ckv_norm_g, v_w_uq, v_w_ukv, v_mla_qn_norm_g, v_mla_qr_norm_g, v_mla_kn_norm_g, v_mla_kr_norm_g, v_mem_norm_g, v_w_mem_kv, v_mem_q_norm_g, v_mem_k_norm_g, v_w_out, v_ffn_norm_g, v_w_gate, v_w_up, v_w_down):
    args = dict(locals())
    x2, mem2, tgt = x[0], mem[0], loss_target[0]
    s, d = x2.shape
    n_in = w_in.shape[2]
    f = w_gate.shape[2]

    in_shards = [w_in[0].T.astype(WIRE)]
    (g_in,) = _all_gather_background(in_shards, 7, "all_gather_in_weights")
    tok = in_shards[0]
    mix_shards = [w_uq[0].T.astype(WIRE), w_ukv[0].astype(WIRE), w_mem_kv[0].astype(WIRE),
                  _to_wire([w_out[0]], tok, "wire_out")[0]]
    g_uq, wkv, g_mkv, g_out = _all_gather_background(mix_shards, 5, "all_gather_mix_weights")
    ffn_shards = [_to_wire([w_gate[0].T, w_up[0].T], tok, "wire_gate_up")]
    (w_gu,) = _all_gather_background(ffn_shards, 1, "all_gather_ffn_weights")
    down_shards = [_to_wire([w_down[0]], tok, "wire_down")[0]]
    (w_d,) = _all_gather_background(down_shards, 6, "all_gather_down_weights")
    wi = g_in.reshape(N_DEV * n_in, d)
    wi = jnp.concatenate([wi[0:1024], wi[1280:1792], wi[1792:2304], wi[2368:2880],
                          wi[1024:1152], wi[1152:1280], wi[2304:2368],
                          jnp.zeros((IN_PAD - 2880, d), wi.dtype)], axis=0)
    wq = g_uq.reshape(768, 512)
    wq = jnp.concatenate([wq[192 * h: 192 * h + 128] for h in range(4)]
                         + [wq[192 * h + 128: 192 * (h + 1)] for h in range(4)], axis=0)
    wmkv = g_mkv.reshape(-1, g_mkv.shape[-1])
    wo = g_out.reshape(-1, d)

    pos = positions[0].astype(jnp.float32)
    inv_freq = ROPE_THETA ** (-jnp.arange(0, MLA_ROPE, 2, dtype=jnp.float32) / MLA_ROPE)
    ang = pos[:, None] * inv_freq
    cos32, sin32 = jnp.cos(ang), jnp.sin(ang)
    cos_t = jnp.tile(cos32, (1, 4))
    sin_t = jnp.tile(jnp.concatenate([-sin32, sin32], axis=1), (1, 2))
    posc, posr = pos.reshape(s, 1), pos.reshape(1, s)
    two = lambda g: jnp.tile(g, (1, 2))
    gq2, gk2, gqr2, gkr2 = two(swa_q_norm_g), two(swa_k_norm_g), two(mla_qr_norm_g), two(mla_kr_norm_g)
    sinks1 = swa_sinks[0]

    hn = _norm_rows(x2, attn_norm_g)
    proj = _mm(hn, wi, tb=True, out_dtype=jnp.float32, tm=FFN_TILE, tk=d, name="in_proj")
    qc, kc, vb, qb, kvb, cqn, ckvn = _mla_prep(proj, cos_t, sin_t, mla_cq_norm_g, mla_ckv_norm_g, wq, wkv,
                                                mla_qn_norm_g, gqr2, mla_kn_norm_g, gkr2)
    y_b, lse_b = _mla_fwd(qc, kc, vb)
    km, vmm, kvm, memn = _memkv_prep(mem2, mem_norm_g, wmkv, mem_k_norm_g)
    y_m, lse_m = _mem_fwd(proj, mem_q_norm_g, km, vmm)
    y_a, lse_a = _swa_fwd(proj, posc, posr, gq2, gk2, sinks1)
    h1, fn = _out_proj(y_a, y_b, y_m, x2, wo, ffn_norm_g)
    gu, act = _ffn_gu(fn, w_gu)
    dout, dout_b, loss_tile = _ffn_down(act, w_d, h1, tgt)

    dgu, dw_d = _ffn_bwd_act(dout_b, w_d, gu)
    (r_d,) = _exchange_grads_background([dw_d], 8, "exchange_down_grads")
    dw_gu = _ffn_dw_gu(fn, dgu)
    (r_gu,) = _exchange_grads_background([dw_gu], 2, "exchange_ffn_grads")
    d_h1, dg_ffn = _ffn_norm_bwd(_ffn_dfn(dgu, w_gu, dw_gu), dout, h1, ffn_norm_g)
    d_y = _mm(d_h1, wo, tb=True, out_dtype=jnp.float32, tm=FFN_TILE, tk=2048, name="d_mix")
    dw_out = jnp.concatenate([
        _mm(y_a, d_h1, ta=True, out_dtype=WIRE, tm=1024, tk=1024, name="dw_out_a"),
        _mm(y_b, d_h1, ta=True, out_dtype=WIRE, tm=1024, tk=1024, name="dw_out_b"),
        _mm(y_m, d_h1, ta=True, out_dtype=WIRE, tm=1024, tk=1024, name="dw_out_m")], axis=0)
    d_qm, dkm, dvmm, dg_mq = _mem_bwd(proj, mem_q_norm_g, km, vmm, d_y, y_m, lse_m)
    dw_mkv, dg_mem, dg_mk = _memkv_bwd(mem2, mem_norm_g, wmkv, mem_k_norm_g, kvm, memn, dkm, dvmm)
    r_mkv, r_out = _exchange_grads_background([dw_mkv.reshape(g_mkv.shape), dw_out.reshape(g_out.shape)], 3,
                                              "exchange_mix_grads")
    dqc, dkc, dvb = _mla_bwd(qc, kc, vb, d_y, y_b, lse_b, dw_mkv)
    (d_cq, d_ckv, d_kr, dw_uq, dw_ukv, dg_cq, dg_ckv, dg_qn, dg_qr, dg_kn, dg_kr) = _mla_prep_bwd(
        proj, cos_t, sin_t, mla_cq_norm_g, mla_ckv_norm_g, wq, wkv, mla_qn_norm_g, gqr2, mla_kn_norm_g, gkr2,
        qb, kvb, cqn, ckvn, dqc, dkc, dvb)
    d_qa, d_ka, d_va, dg_q, dg_k, d_sinks = _swa_bwd(proj, posc, posr, gq2, gk2, sinks1, d_y, y_a, lse_a, dw_out)
    d_proj = [d_qa, d_cq, d_ckv, d_qm, d_ka, d_va, d_kr]
    gi = _dw_in(hn, d_proj, n_in, None)

    gq_ = jnp.concatenate(sum([[dw_uq[128 * h: 128 * (h + 1)], dw_uq[512 + 64 * h: 512 + 64 * (h + 1)]]
                               for h in range(4)], []), axis=0)
    gq_ = gq_.reshape(N_DEV, 96, 512)
    r_in, r_uq, r_ukv = _exchange_grads_background([gi, gq_, dw_ukv], 4, "exchange_in_grads")
    grad_x, dg_attn = _dx(d_proj, wi, x2, attn_norm_g, d_h1, gi)
    small_g = {
        "attn_norm_g": dg_attn, "swa_q_norm_g": dg_q, "swa_k_norm_g": dg_k,
        "swa_sinks": d_sinks, "mla_cq_norm_g": dg_cq, "mla_ckv_norm_g": dg_ckv, "mla_qn_norm_g": dg_qn,
        "mla_qr_norm_g": dg_qr, "mla_kn_norm_g": dg_kn, "mla_kr_norm_g": dg_kr,
        "mem_norm_g": dg_mem, "mem_q_norm_g": dg_mq, "mem_k_norm_g": dg_mk, "ffn_norm_g": dg_ffn}
    pack = _small_pack([small_g[n] for n in _SMALL], loss_tile, [args[n].shape[-1] for n in _SMALL])
    pack8 = jnp.broadcast_to(pack, (N_DEV,) + pack.shape[1:])
    (packs,) = _exchange_grads_background([pack8], 9, "exchange_small_grads")

    big = {}
    last = [None]

    def adam(name, r, transposed=False, which=None):
        w, m, v = args[name][0], args["m_" + name][0], args["v_" + name][0]
        if transposed:
            outs = _adam_big(r, w.T, m.T, v.T, "adam_" + name, last[0], which)
            big[name] = [o.T[None] for o in outs]
        else:
            outs = _adam_big(r, w, m, v, "adam_" + name, last[0])
            big[name] = [o[None] for o in outs]
        last[0] = outs[0]

    adam("w_down", r_d)
    adam("w_gate", r_gu, True, which=0)
    adam("w_up", r_gu, True, which=1)
    adam("w_out", r_out)
    adam("w_mem_kv", r_mkv)
    adam("w_in", r_in, True)
    adam("w_uq", r_uq, True)
    adam("w_ukv", r_ukv)

    loss11, small_out = _small_adam(packs, [args[n] for n in _SMALL], [args["m_" + n] for n in _SMALL],
                                    [args["v_" + n] for n in _SMALL], last[0])
    small = dict(zip(_SMALL, small_out))
    loss = loss11.reshape(())

    order = ["attn_norm_g", "w_in", "swa_q_norm_g", "swa_k_norm_g", "swa_sinks", "mla_cq_norm_g", "mla_ckv_norm_g",
             "w_uq", "w_ukv", "mla_qn_norm_g", "mla_qr_norm_g", "mla_kn_norm_g", "mla_kr_norm_g", "mem_norm_g",
             "w_mem_kv", "mem_q_norm_g", "mem_k_norm_g", "w_out", "ffn_norm_g", "w_gate", "w_up", "w_down"]
    res = {n: (big[n] if n in big else list(small[n])) for n in order}
    outs = [loss, grad_x[None]]
    for kind in range(4):
        outs += [res[n][kind] for n in order]
    return tuple(outs)
```

```python
import jax
import jax.numpy as jnp
from jax import lax
from jax.experimental import pallas as pl
from jax.experimental.pallas import tpu as pltpu
from jax.experimental.pallas import tpu_sc as plsc

MXU = jnp.bfloat16
WIRE = jnp.bfloat16
EPS = 1e-6
NEG_INF = -1e30
LOG2E = 1.4426950408889634
N_DEV = 8
LANES = 128
ROW_TILE = 256
FFN_TILE = 512
ATT_TILE = 1024
SWA_BLOCK = 128
VMEM_LIMIT = 56 * 1024 * 1024

SWA_Q_HEADS, SWA_KV_HEADS, SWA_DIM = 16, 2, 64
MLA_HEADS, MLA_NOPE, MLA_ROPE, MLA_V = 4, 128, 64, 128
MEM_HEADS, MEM_DIM = 4, 128
ROPE_THETA = 10000.0
ADAM_LR, ADAM_B1, ADAM_B2, ADAM_EPS, ADAM_WD, ADAM_STEP = 0.001, 0.9, 0.999, 1e-08, 0.01, 10

C_QA, C_CQ, C_CKV, C_QM, C_KA, C_VA, C_KR, IN_PAD = 0, 1024, 1536, 2048, 2560, 2688, 2816, 2944


def _pcall(body, *, name, out_shape, in_specs, out_specs, grid=(), scratch=(), sem=None, after=None):
    params = pltpu.CompilerParams(dimension_semantics=sem, vmem_limit_bytes=VMEM_LIMIT)
    if after is not None:
        n_in, inner = len(in_specs), body

        def body(*refs):
            inner(*refs[:n_in], *refs[n_in + 1:])

        in_specs = list(in_specs) + [pl.BlockSpec(memory_space=pl.ANY)]
    call = pl.pallas_call(body, name=name, grid=grid, in_specs=in_specs, out_specs=out_specs,
                          out_shape=out_shape, scratch_shapes=list(scratch), compiler_params=params)
    return call if after is None else (lambda *ops: call(*ops, after))


def _sds(shape, dtype):
    return jax.ShapeDtypeStruct(tuple(shape), dtype)


def _dot(a, b):
    return jnp.dot(a.astype(MXU), b.astype(MXU), preferred_element_type=jnp.float32)


def _dot_nt(a, b):
    return lax.dot_general(a.astype(MXU), b.astype(MXU), (((1,), (1,)), ((), ())),
                           preferred_element_type=jnp.float32)


def _dot_tn(a, b):
    return lax.dot_general(a.astype(MXU), b.astype(MXU), (((0,), (0,)), ((), ())),
                           preferred_element_type=jnp.float32)


def _lo_mask(shape):
    return (lax.broadcasted_iota(jnp.int32, shape, len(shape) - 1) % LANES) < 64


def _norm_fwd(x, g, half=False):
    x2 = x * x
    if half:
        lo = _lo_mask(x.shape)
        s_lo = jnp.sum(jnp.where(lo, x2, 0.0), -1, keepdims=True)
        s_hi = jnp.sum(jnp.where(lo, 0.0, x2), -1, keepdims=True)
        r = jnp.where(lo, lax.rsqrt(s_lo / 64.0 + EPS), lax.rsqrt(s_hi / 64.0 + EPS))
    else:
        r = lax.rsqrt(jnp.mean(x2, -1, keepdims=True) + EPS)
    xn = x * r
    return xn * g, xn, r


def _norm_bwd(xn, r, g, dy, half=False):
    t = dy * g
    tx = t * xn
    if half:
        lo = _lo_mask(xn.shape)
        m_lo = jnp.sum(jnp.where(lo, tx, 0.0), -1, keepdims=True) / 64.0
        m_hi = jnp.sum(jnp.where(lo, 0.0, tx), -1, keepdims=True) / 64.0
        m = jnp.where(lo, m_lo, m_hi)
    else:
        m = jnp.mean(tx, -1, keepdims=True)
    dx = r * (t - xn * m)
    dg = jnp.sum(dy * xn, 0, keepdims=True)
    return dx, dg


def _swap32(x):
    lane = lax.broadcasted_iota(jnp.int32, x.shape, 1)
    return jnp.where((lane % 64) < 32, pltpu.roll(x, 96, 1), pltpu.roll(x, 32, 1))


def _rope(x, cos, sin):
    return x * cos + _swap32(x) * sin


def _rope_bwd(d, cos, sin):
    return d * cos + _swap32(d * sin)


def _my_coords():
    return lax.axis_index("x"), lax.axis_index("y"), lax.axis_index("c")


def _dev_index(px, py, pc):
    return 4 * px + 2 * py + pc


_FLIPS = [(0, 0, 1), (0, 1, 0), (0, 1, 1), (1, 0, 0), (1, 0, 1), (1, 1, 0), (1, 1, 1)]


def _flip(coords, f):
    return tuple((1 - v) if b else v for v, b in zip(coords, f))


def _all_gather(shards):
    n = len(shards)

    def body(*refs):
        ins, outs = refs[:n], refs[n:2 * n]
        send_sems, recv_sems, local_sems = refs[2 * n:]
        x, y, c = _my_coords()
        me, sibling = (x, y, c), (x, y, 1 - c)
        chips = [(1 - x, y), (x, 1 - y), (1 - x, 1 - y)]

        def copy(w, k, block, to, src=None):
            dst = outs[w].at[_dev_index(*block)]
            return pltpu.make_async_remote_copy(
                src_ref=dst if src is None else src, dst_ref=dst,
                send_sem=send_sems.at[w, k], recv_sem=recv_sems.at[w, k],
                device_id=to, device_id_type=pl.DeviceIdType.MESH)

        sends, locals_ = [], []
        for w in range(n):
            mine = pltpu.make_async_copy(ins[w], outs[w].at[_dev_index(*me)], local_sems.at[w])
            mine.start()
            locals_.append(mine)
            first = [copy(w, 0, me, sibling, src=ins[w])]
            first += [copy(w, 1 + j, me, (*chip, c), src=ins[w]) for j, chip in enumerate(chips)]
            for cp in first:
                cp.start()
            sends += first
        for w in range(n):
            for j, chip in enumerate(chips):
                copy(w, 1 + j, (*chip, c), me).wait_recv()
                fwd = copy(w, 4 + j, (*chip, c), sibling)
                fwd.start()
                sends.append(fwd)
        for w in range(n):
            copy(w, 0, sibling, me).wait_recv()
            for j, chip in enumerate(chips):
                copy(w, 4 + j, (*chip, 1 - c), me).wait_recv()
        for cp in sends:
            cp.wait_send()
        for mine in locals_:
            mine.wait()

    any_spec = pl.BlockSpec(memory_space=pl.ANY)
    return _pcall(
        body, name="all_gather_weights",
        out_shape=[_sds((N_DEV,) + s.shape, s.dtype) for s in shards],
        in_specs=[any_spec] * n, out_specs=[any_spec] * n,
        scratch=[pltpu.SemaphoreType.DMA((n, 7)), pltpu.SemaphoreType.DMA((n, 7)),
                 pltpu.SemaphoreType.DMA((n,))])(*shards)


def _wire_cost(arrays):
    nbytes = sum(a.size * a.dtype.itemsize for a in arrays)
    return pl.CostEstimate(flops=0, transcendentals=0, bytes_accessed=40 * nbytes)


def _all_gather_background(shards, collective_id, name):
    n = len(shards)
    src_refs = [jax.new_ref(s, memory_space=pltpu.MemorySpace.HBM) for s in shards]
    out_refs = [jax.empty_ref(_sds((N_DEV,) + s.shape, s.dtype), memory_space=pltpu.MemorySpace.HBM) for s in shards]

    @pl.kernel(mesh=plsc.ScalarSubcoreMesh(axis_name="seq", num_cores=1), name=name,
               scratch_types=(pltpu.SemaphoreType.DMA((n, 7)), pltpu.SemaphoreType.DMA((n, 7)),
                              pltpu.SemaphoreType.DMA((n,))),
               compiler_params=pltpu.CompilerParams(collective_id=collective_id))
    def launch(send_sems, recv_sems, local_sems):
        x, y, c = _my_coords()
        me, sibling = (x, y, c), (x, y, 1 - c)
        chips = [(1 - x, y), (x, 1 - y), (1 - x, 1 - y)]
        barrier = pltpu.get_barrier_semaphore()
        for peer in [sibling] + [(*chip, c) for chip in chips]:
            pl.semaphore_signal(barrier, inc=1, device_id=peer, device_id_type=pl.DeviceIdType.MESH)
        pl.semaphore_wait(barrier, 4)

        def copy(w, k, block, to, src=None):
            dst = out_refs[w].at[_dev_index(*block)]
            return pltpu.make_async_remote_copy(
                src_ref=dst if src is None else src, dst_ref=dst,
                send_sem=send_sems.at[w, k], recv_sem=recv_sems.at[w, k],
                device_id=to, device_id_type=pl.DeviceIdType.MESH)

        sends, locals_ = [], []
        for w in range(n):
            mine = pltpu.make_async_copy(src_refs[w], out_refs[w].at[_dev_index(*me)], local_sems.at[w])
            mine.start()
            locals_.append(mine)
            first = [copy(w, 0, me, sibling, src=src_refs[w])]
            first += [copy(w, 1 + j, me, (*chip, c), src=src_refs[w]) for j, chip in enumerate(chips)]
            for cp in first:
                cp.start()
            sends += first
        for w in range(n):
            for j, chip in enumerate(chips):
                copy(w, 1 + j, (*chip, c), me).wait_recv()
                fwd = copy(w, 4 + j, (*chip, c), sibling)
                fwd.start()
                sends.append(fwd)
        for w in range(n):
            copy(w, 0, sibling, me).wait_recv()
            for j, chip in enumerate(chips):
                copy(w, 4 + j, (*chip, 1 - c), me).wait_recv()
        for cp in sends:
            cp.wait_send()
        for mine in locals_:
            mine.wait()

    launch()
    return [r[...] for r in out_refs]


def _exchange_grads(grads):
    n = len(grads)

    def body(*refs):
        ins, outs = refs[:n], refs[n:2 * n]
        send_sems, recv_sems, local_sems = refs[2 * n:]
        me = _my_coords()
        my_idx = _dev_index(*me)
        sends, locals_ = [], []
        for w in range(n):
            mine = pltpu.make_async_copy(ins[w].at[my_idx], outs[w].at[my_idx], local_sems.at[w])
            mine.start()
            locals_.append(mine)
            for k, f in enumerate(_FLIPS):
                peer = _flip(me, f)
                cp = pltpu.make_async_remote_copy(
                    src_ref=ins[w].at[_dev_index(*peer)], dst_ref=outs[w].at[my_idx],
                    send_sem=send_sems.at[w, k], recv_sem=recv_sems.at[w, k],
                    device_id=peer, device_id_type=pl.DeviceIdType.MESH)
                cp.start()
                sends.append(cp)
        for w in range(n):
            for k, f in enumerate(_FLIPS):
                peer = _flip(me, f)
                slot = outs[w].at[_dev_index(*peer)]
                pltpu.make_async_remote_copy(
                    src_ref=slot, dst_ref=slot,
                    send_sem=send_sems.at[w, k], recv_sem=recv_sems.at[w, k],
                    device_id=peer, device_id_type=pl.DeviceIdType.MESH).wait_recv()
        for cp in sends:
            cp.wait_send()
        for mine in locals_:
            mine.wait()

    any_spec = pl.BlockSpec(memory_space=pl.ANY)
    return _pcall(
        body, name="exchange_grads",
        out_shape=[_sds(g.shape, g.dtype) for g in grads],
        in_specs=[any_spec] * n, out_specs=[any_spec] * n,
        scratch=[pltpu.SemaphoreType.DMA((n, 7)), pltpu.SemaphoreType.DMA((n, 7)),
                 pltpu.SemaphoreType.DMA((n,))])(*grads)


def _exchange_grads_background(grads, collective_id, name):
    n = len(grads)
    src_refs = [jax.new_ref(g, memory_space=pltpu.MemorySpace.HBM) for g in grads]
    out_refs = [jax.empty_ref(_sds(g.shape, g.dtype), memory_space=pltpu.MemorySpace.HBM) for g in grads]

    @pl.kernel(mesh=plsc.ScalarSubcoreMesh(axis_name="seq", num_cores=1), name=name,
               scratch_types=(pltpu.SemaphoreType.DMA((n, 7)), pltpu.SemaphoreType.DMA((n, 7)),
                              pltpu.SemaphoreType.DMA((n,))),
               cost_estimate=_wire_cost(grads),
               compiler_params=pltpu.CompilerParams(collective_id=collective_id))
    def launch(send_sems, recv_sems, local_sems):
        me = _my_coords()
        my_idx = _dev_index(*me)
        peers = [_flip(me, f) for f in _FLIPS]
        barrier = pltpu.get_barrier_semaphore()
        for peer in peers:
            pl.semaphore_signal(barrier, inc=1, device_id=peer, device_id_type=pl.DeviceIdType.MESH)
        pl.semaphore_wait(barrier, len(peers))
        sends, locals_ = [], []
        for w in range(n):
            mine = pltpu.make_async_copy(src_refs[w].at[my_idx], out_refs[w].at[my_idx], local_sems.at[w])
            mine.start()
            locals_.append(mine)
            for k, peer in enumerate(peers):
                cp = pltpu.make_async_remote_copy(
                    src_ref=src_refs[w].at[_dev_index(*peer)], dst_ref=out_refs[w].at[my_idx],
                    send_sem=send_sems.at[w, k], recv_sem=recv_sems.at[w, k],
                    device_id=peer, device_id_type=pl.DeviceIdType.MESH)
                cp.start()
                sends.append(cp)
        for w in range(n):
            for k, peer in enumerate(peers):
                slot = out_refs[w].at[_dev_index(*peer)]
                pltpu.make_async_remote_copy(
                    src_ref=slot, dst_ref=slot, send_sem=send_sems.at[w, k], recv_sem=recv_sems.at[w, k],
                    device_id=peer, device_id_type=pl.DeviceIdType.MESH).wait_recv()
        for cp in sends:
            cp.wait_send()
        for mine in locals_:
            mine.wait()

    launch()
    return [r[...] for r in out_refs]


def _to_wire(parts, after, name):
    n = len(parts)
    rows, cols = parts[0].shape
    tr = rows // 2 if rows % 32 == 0 else rows

    def body(*refs):
        for k in range(n):
            refs[n][k] = refs[k][...].astype(WIRE)

    blk = pl.BlockSpec((tr, cols), lambda i: (i, 0))
    return _pcall(
        body, name=name, grid=(rows // tr,), out_shape=_sds((n, rows, cols), WIRE),
        in_specs=[blk] * n, out_specs=pl.BlockSpec((n, tr, cols), lambda i: (0, i, 0)),
        sem=("parallel",), after=after)(*parts)


def _adam_math(w, g, m, v):
    m = ADAM_B1 * m + (1.0 - ADAM_B1) * g
    v = ADAM_B2 * v + (1.0 - ADAM_B2) * (g * g)
    m_hat = m / (1.0 - ADAM_B1 ** ADAM_STEP)
    v_hat = v / (1.0 - ADAM_B2 ** ADAM_STEP)
    delta = -ADAM_LR * (m_hat / (jnp.sqrt(v_hat) + ADAM_EPS) + ADAM_WD * w)
    return delta, m, v


def _small_layout(sizes):
    row0, r = [], 0
    for n in sizes:
        row0.append(r)
        r += -(-n // LANES)
    return row0, r, -(-(r + 1) // 8) * 8


def _small_pieces(n):
    return [(k, min(LANES, n - LANES * k)) for k in range(-(-n // LANES))]


def _small_fill(pack, slot, srcs, sizes, row0, rows):
    pack[slot] = jnp.zeros((rows, LANES), jnp.float32)
    for p, n in enumerate(sizes):
        val = srcs[p][...]
        if val.shape[-1] == LANES and n == 64:
            pack[slot, row0[p]:row0[p] + 1, :] = val + pltpu.roll(val, 64, 1)
            continue
        for k, width in _small_pieces(n):
            pack[slot, row0[p] + k:row0[p] + k + 1, 0:width] = srcs[p][:, LANES * k:LANES * k + width]


def _small_pack(grads, loss_tile, sizes):
    n_par = len(sizes)
    row0, loss_row, rows = _small_layout(sizes)

    def body(*refs):
        g_refs, loss_in, out_ref = refs[:n_par], refs[n_par], refs[n_par + 1]
        _small_fill(out_ref, 0, g_refs, sizes, row0, rows)
        out_ref[0, loss_row:loss_row + 1, :] = loss_in[0:1, :]

    vm = pl.BlockSpec(memory_space=pltpu.VMEM)
    return _pcall(
        body, name="small_pack", out_shape=_sds((1, rows, LANES), jnp.float32),
        in_specs=[vm] * (n_par + 1), out_specs=vm)(*grads, loss_tile)


def _small_adam(packs, ws, ms, vs, after):
    sizes = [w.shape[-1] for w in ws]
    n_par = len(ws)
    row0, loss_row, rows = _small_layout(sizes)

    def body(*refs):
        g_ref = refs[0]
        w_refs, m_refs, v_refs = (refs[1 + k * n_par: 1 + (k + 1) * n_par] for k in range(3))
        loss_out = refs[3 * n_par + 1]
        out_refs = refs[3 * n_par + 2: 7 * n_par + 2]
        pack, res = refs[7 * n_par + 2:]
        for slot, srcs in enumerate((w_refs, m_refs, v_refs)):
            _small_fill(pack, slot, srcs, sizes, row0, rows)
        g = g_ref[0]
        for dev in range(1, N_DEV):
            g = g + g_ref[dev]
        delta, m, v = _adam_math(pack[0], g, pack[1], pack[2])
        res[0], res[1], res[2], res[3] = g, delta, m, v
        loss_out[...] = res[0, loss_row:loss_row + 1, 0:1]
        for p, n in enumerate(sizes):
            for kind in range(4):
                for k, width in _small_pieces(n):
                    out_refs[4 * p + kind][:, LANES * k:LANES * k + width] = (
                        res[kind, row0[p] + k:row0[p] + k + 1, 0:width])

    vm = pl.BlockSpec(memory_space=pltpu.VMEM)
    out_shape = [_sds((1, 1), jnp.float32)]
    for n in sizes:
        out_shape += [_sds((1, n), jnp.float32)] * 4
    outs = _pcall(
        body, name="small_adam", out_shape=out_shape,
        in_specs=[vm] * (3 * n_par + 1), out_specs=[vm] * len(out_shape),
        scratch=[pltpu.VMEM((3, rows, LANES), jnp.float32), pltpu.VMEM((4, rows, LANES), jnp.float32)],
        after=after)(packs, *ws, *ms, *vs)
    return outs[0], [outs[1 + 4 * p: 5 + 4 * p] for p in range(n_par)]


def _adam_big(recv, w, m, v, name, after=None, which=None):
    rows, cols = recv.shape[-2:]
    row_tiles = [t for t in range(16, rows + 1, 16) if rows % t == 0 and t * cols <= 400 * 1024]
    tr, tc = (max(row_tiles), cols) if row_tiles else (rows, 512 if cols % 512 == 0 else cols)

    def body(r_ref, w_ref, m_ref, v_ref, g_ref, d_ref, mo_ref, vo_ref):
        g = r_ref[0].astype(jnp.float32)
        for d in range(1, N_DEV):
            g = g + r_ref[d].astype(jnp.float32)
        delta, mn, vn = _adam_math(w_ref[...], g, m_ref[...], v_ref[...])
        g_ref[...] = g
        d_ref[...] = delta
        mo_ref[...] = mn
        vo_ref[...] = vn

    blk = pl.BlockSpec((tr, tc), lambda i, j: (i, j))
    if which is None:
        r_spec = pl.BlockSpec((N_DEV, tr, tc), lambda i, j: (0, i, j))
    else:
        r_spec = pl.BlockSpec((N_DEV, None, tr, tc), lambda i, j: (0, which, i, j))
    return _pcall(
        body, name=name, grid=(rows // tr, cols // tc),
        out_shape=[_sds((rows, cols), jnp.float32)] * 4,
        in_specs=[r_spec, blk, blk, blk],
        out_specs=[blk] * 4, sem=("parallel", "parallel"), after=after)(recv, w, m, v)


def _mm(a, b, *, ta=False, tb=False, out_dtype, tm, tk, name):
    (kdim, mdim) = a.shape if ta else a.shape[::-1]
    ndim = b.shape[0] if tb else b.shape[1]
    tm, tk = min(tm, mdim), min(tk, kdim)
    nk = kdim // tk

    def body(a_ref, b_ref, o_ref, acc):
        k = pl.program_id(1)
        if ta:
            part = _dot_tn(a_ref[...], b_ref[...])
        elif tb:
            part = _dot_nt(a_ref[...], b_ref[...])
        else:
            part = _dot(a_ref[...], b_ref[...])

        @pl.when(k == 0)
        def _():
            acc[...] = part

        @pl.when(k > 0)
        def _():
            acc[...] += part

        @pl.when(k == nk - 1)
        def _():
            o_ref[...] = acc[...].astype(o_ref.dtype)

    a_spec = pl.BlockSpec((tk, tm), lambda i, k: (k, i)) if ta else pl.BlockSpec((tm, tk), lambda i, k: (i, k))
    b_spec = pl.BlockSpec((ndim, tk), lambda i, k: (0, k)) if tb else pl.BlockSpec((tk, ndim), lambda i, k: (k, 0))
    return _pcall(
        body, name=name, grid=(mdim // tm, nk), out_shape=_sds((mdim, ndim), out_dtype),
        in_specs=[a_spec, b_spec], out_specs=pl.BlockSpec((tm, ndim), lambda i, k: (i, 0)),
        scratch=[pltpu.VMEM((tm, ndim), jnp.float32)], sem=("parallel", "arbitrary"))(a, b)


def _ref_col_pieces(start, stop):
    ref_starts = [0, 1024, 1152, 1280, 1792, 2304, 2368, 2880]
    perm_starts = [C_QA, C_KA, C_VA, C_CQ, C_CKV, C_KR, C_QM]
    out = []
    for p in range(7):
        lo, hi = max(start, ref_starts[p]), min(stop, ref_starts[p + 1])
        if lo < hi:
            out.append((lo - start, perm_starts[p] + lo - ref_starts[p], hi - lo))
    return out


def _dw_in(hn, d_proj, n_shard, after):
    s, d = hn.shape
    n = sum(p.shape[1] for p in d_proj)
    n_pc = len(d_proj)
    tm, tk = min(512, d), min(1024, s)
    nk = s // tk

    def body(a_ref, *refs):
        b_refs, (o_ref, acc) = refs[:n_pc], refs[n_pc:]
        k = pl.program_id(1)
        part = _dot_tn(a_ref[...], jnp.concatenate([r[...] for r in b_refs], axis=1))

        @pl.when(k == 0)
        def _():
            acc[...] = part

        @pl.when(k > 0)
        def _():
            acc[...] += part

        @pl.when(k == nk - 1)
        def _():
            t = acc[...].T
            for j in range(N_DEV):
                rows = [t[src:src + width] for _, src, width in _ref_col_pieces(j * n_shard, (j + 1) * n_shard)]
                o_ref[j] = jnp.concatenate(rows, axis=0).astype(o_ref.dtype)

    return _pcall(
        body, name="dw_in", grid=(d // tm, nk), out_shape=_sds((N_DEV, n_shard, d), WIRE),
        in_specs=[pl.BlockSpec((tk, tm), lambda i, k: (k, i))]
        + [pl.BlockSpec((tk, p.shape[1]), lambda i, k: (k, 0)) for p in d_proj],
        out_specs=pl.BlockSpec((N_DEV, n_shard, tm), lambda i, k: (0, 0, i)),
        scratch=[pltpu.VMEM((tm, n), jnp.float32)], sem=("parallel", "arbitrary"), after=after)(hn, *d_proj)


def _in_proj(x, g, w):
    s, d = x.shape
    n = w.shape[0]
    tm = min(2 * ROW_TILE, s)

    def body(x_ref, g_ref, w_ref, p_ref, hn_ref):
        hn, _, _ = _norm_fwd(x_ref[...], g_ref[...])
        hn_ref[...] = hn.astype(hn_ref.dtype)
        p_ref[...] = _dot_nt(hn, w_ref[...])

    return _pcall(
        body, name="in_proj", grid=(s // tm,),
        out_shape=[_sds((s, n), jnp.float32), _sds((s, d), MXU)],
        in_specs=[pl.BlockSpec((tm, d), lambda i: (i, 0)), pl.BlockSpec((1, d), lambda i: (0, 0)),
                  pl.BlockSpec((n, d), lambda i: (0, 0), pipeline_mode=pl.Buffered(1))],
        out_specs=[pl.BlockSpec((tm, n), lambda i: (i, 0)), pl.BlockSpec((tm, d), lambda i: (i, 0))],
        sem=("parallel",))(x, g, w)


def _norm_rows(x, g):
    s, d = x.shape
    tm = min(FFN_TILE, s)

    def body(x_ref, g_ref, hn_ref):
        hn, _, _ = _norm_fwd(x_ref[...], g_ref[...])
        hn_ref[...] = hn.astype(hn_ref.dtype)

    row = pl.BlockSpec((tm, d), lambda i: (i, 0))
    return _pcall(body, name="norm_rows", grid=(s // tm,), out_shape=_sds((s, d), MXU),
                  in_specs=[row, pl.BlockSpec((1, d), lambda i: (0, 0))], out_specs=row, sem=("parallel",))(x, g)


def _mla_prep(proj, cos, sin, g_cq, g_ckv, w_uq, w_ukv, g_qn, g_qr, g_kn, g_kr):
    s = proj.shape[0]
    tm = min(ROW_TILE, s)
    nh = MLA_HEADS

    def body(cq_ref, ckv_ref, kr_ref, cos_ref, sin_ref, gcq_ref, gckv_ref, wuq_ref, wukv_ref,
             gqn_ref, gqr_ref, gkn_ref, gkr_ref,
             qc_ref, kc_ref, v_ref, qb_ref, kvb_ref, cqn_ref, ckvn_ref):
        cos_t, sin_t = cos_ref[...], sin_ref[...]
        lo = _lo_mask((tm, LANES))
        cqn, _, _ = _norm_fwd(cq_ref[...], gcq_ref[...])
        cqn_ref[...] = cqn.astype(cqn_ref.dtype)
        qb = _dot_nt(cqn, wuq_ref[...])
        qb_ref[...] = qb
        ckvn, _, _ = _norm_fwd(ckv_ref[...], gckv_ref[...])
        ckvn_ref[...] = ckvn.astype(ckvn_ref.dtype)
        w_ukv_full = jnp.concatenate([wukv_ref[dev] for dev in range(N_DEV)], axis=1)
        kvb = _dot(ckvn, w_ukv_full)
        kvb_ref[...] = kvb
        kr, _, _ = _norm_fwd(kr_ref[...], gkr_ref[...], half=True)
        kr = _rope(kr, cos_t, sin_t)
        kr2 = jnp.where(lo, kr, pltpu.roll(kr, 64, 1))
        ropes = []
        for j in range(nh // 2):
            xr = qb[:, nh * MLA_NOPE + LANES * j: nh * MLA_NOPE + LANES * (j + 1)]
            qr, _, _ = _norm_fwd(xr, gqr_ref[...], half=True)
            ropes.append(_rope(qr, cos_t, sin_t))
        for h in range(nh):
            qn, _, _ = _norm_fwd(qb[:, MLA_NOPE * h: MLA_NOPE * (h + 1)], gqn_ref[...])
            mask = lo if h % 2 == 0 else jnp.logical_not(lo)
            qr = jnp.where(mask, ropes[h // 2], 0.0)
            qc_ref[h] = jnp.concatenate([qn, qr], axis=1).astype(qc_ref.dtype)
            kn, _, _ = _norm_fwd(kvb[:, 256 * h: 256 * h + MLA_NOPE], gkn_ref[...])
            kc_ref[h] = jnp.concatenate([kn, kr2], axis=1).astype(kc_ref.dtype)
            v_ref[h] = kvb[:, 256 * h + MLA_NOPE: 256 * (h + 1)].astype(v_ref.dtype)

    def col(width, start):
        return pl.BlockSpec((tm, width), lambda i: (i, start // width))

    def full(shape):
        return pl.BlockSpec(shape, lambda i: (0,) * len(shape))

    def row(width):
        return pl.BlockSpec((tm, width), lambda i: (i, 0))

    def heads(width):
        return pl.BlockSpec((nh, tm, width), lambda i: (0, i, 0))

    return _pcall(
        body, name="mla_prep", grid=(s // tm,),
        out_shape=[_sds((nh, s, 256), MXU), _sds((nh, s, 256), MXU), _sds((nh, s, MLA_V), MXU),
                   _sds((s, 768), jnp.float32), _sds((s, 1024), jnp.float32),
                   _sds((s, 512), MXU), _sds((s, 512), MXU)],
        in_specs=[col(512, C_CQ), col(512, C_CKV), col(LANES, C_KR), row(LANES), row(LANES),
                  full((1, 512)), full((1, 512)), full((768, 512)), full((N_DEV, 512, LANES)),
                  full((1, LANES)), full((1, LANES)), full((1, LANES)), full((1, LANES))],
        out_specs=[heads(256), heads(256), heads(MLA_V), row(768), row(1024), row(512), row(512)],
        sem=("parallel",))(proj, proj, proj, cos, sin, g_cq, g_ckv, w_uq, w_ukv, g_qn, g_qr, g_kn, g_kr)


def _tri_rows(p, nb):
    i = sum(jnp.where(p >= (r * (r + 1)) // 2, 1, 0) for r in range(1, nb))
    return i, p - (i * (i + 1)) // 2


def _tri_cols(p, nb):
    j = sum(jnp.where(p >= r * nb - (r * (r - 1)) // 2, 1, 0) for r in range(1, nb))
    return j, j + p - (j * nb - (j * (j - 1)) // 2)


def _mla_fwd(qc, kc, v):
    nh, s, _ = qc.shape
    t = min(ATT_TILE, s)
    nb = s // t
    scale = (MLA_NOPE + MLA_ROPE) ** -0.5

    def body(q_ref, k_ref, v_ref, y_ref, lse_ref, m_sc, l_sc, acc):
        qi, ki = _tri_rows(pl.program_id(1), nb)

        @pl.when(ki == 0)
        def _():
            m_sc[...] = jnp.full_like(m_sc, NEG_INF)
            l_sc[...] = jnp.zeros_like(l_sc)
            acc[...] = jnp.zeros_like(acc)

        def step(diagonal):
            rc = t // 4 if diagonal else t
            for c in range(t // rc):
                rows = slice(rc * c, rc * (c + 1))
                keys = slice(0, rc * (c + 1))
                sc = _dot_nt(q_ref[0, rows, :], k_ref[0, keys, :]) * (scale * LOG2E)
                if diagonal:
                    r_i = lax.broadcasted_iota(jnp.int32, sc.shape, 0) + rc * c
                    c_i = lax.broadcasted_iota(jnp.int32, sc.shape, 1)
                    sc = jnp.where(c_i <= r_i, sc, NEG_INF)
                m_old = m_sc[rows, :]
                m_new = jnp.maximum(m_old, jnp.max(sc, -1, keepdims=True))
                alpha = jnp.exp2(m_old - m_new)
                p = jnp.exp2(sc - m_new)
                l_sc[rows, :] = alpha * l_sc[rows, :] + jnp.sum(p, -1, keepdims=True)
                acc[rows, :] = alpha * acc[rows, :] + _dot(p, v_ref[0, keys, :])
                m_sc[rows, :] = m_new

        @pl.when(ki < qi)
        def _():
            step(False)

        @pl.when(ki == qi)
        def _():
            step(True)

        @pl.when(ki == qi)
        def _():
            y_ref[...] = acc[...] / l_sc[...]
            lse_ref[0] = m_sc[...] + jnp.log2(l_sc[...])

    return _pcall(
        body, name="mla_fwd", grid=(nh, (nb * (nb + 1)) // 2),
        out_shape=[_sds((s, nh * MLA_V), jnp.float32), _sds((nh, s, 1), jnp.float32)],
        in_specs=[pl.BlockSpec((1, t, 256), lambda h, p: (h, _tri_rows(p, nb)[0], 0)),
                  pl.BlockSpec((1, t, 256), lambda h, p: (h, _tri_rows(p, nb)[1], 0)),
                  pl.BlockSpec((1, t, MLA_V), lambda h, p: (h, _tri_rows(p, nb)[1], 0))],
        out_specs=[pl.BlockSpec((t, MLA_V), lambda h, p: (_tri_rows(p, nb)[0], h)),
                   pl.BlockSpec((1, t, 1), lambda h, p: (h, _tri_rows(p, nb)[0], 0))],
        scratch=[pltpu.VMEM((t, 1), jnp.float32), pltpu.VMEM((t, 1), jnp.float32),
                 pltpu.VMEM((t, MLA_V), jnp.float32)],
        sem=("parallel", "arbitrary"))(qc, kc, v)


def _memkv_prep(mem, g_mem, w_mkv, g_mk):
    ml, d = mem.shape
    hw = MEM_HEADS * MEM_DIM

    def body(mem_ref, g_ref, w_ref, gk_ref, k_ref, v_ref, kv_ref, mn_ref):
        mn, _, _ = _norm_fwd(mem_ref[...], g_ref[...])
        mn_ref[...] = mn.astype(mn_ref.dtype)
        kv = _dot(mn, w_ref[...])
        kv_ref[...] = kv
        for h in range(MEM_HEADS):
            kn, _, _ = _norm_fwd(kv[:, MEM_DIM * h: MEM_DIM * (h + 1)], gk_ref[...])
            k_ref[:, MEM_DIM * h: MEM_DIM * (h + 1)] = kn.astype(k_ref.dtype)
        v_ref[...] = kv[:, hw:].astype(v_ref.dtype)

    vm = pl.BlockSpec(memory_space=pltpu.VMEM)
    return _pcall(
        body, name="memkv_prep",
        out_shape=[_sds((ml, hw), MXU), _sds((ml, hw), MXU), _sds((ml, 2 * hw), jnp.float32), _sds((ml, d), MXU)],
        in_specs=[vm] * 4, out_specs=[vm] * 4)(mem, g_mem, w_mkv, g_mk)


def _mem_fwd(proj, g_mq, km, vmm):
    s = proj.shape[0]
    ml, hw = km.shape
    tm = min(FFN_TILE, s)
    scale = MEM_DIM ** -0.5

    def body(q_ref, g_ref, k_ref, v_ref, y_ref, lse_ref):
        col = lax.broadcasted_iota(jnp.int32, (tm, MEM_HEADS), 1)
        lse_t = jnp.zeros((tm, MEM_HEADS), jnp.float32)
        for h in range(MEM_HEADS):
            sl = slice(MEM_DIM * h, MEM_DIM * (h + 1))
            qn, _, _ = _norm_fwd(q_ref[:, sl], g_ref[...])
            sc = _dot_nt(qn, k_ref[:, sl]) * scale
            m = jnp.max(sc, -1, keepdims=True)
            p = jnp.exp(sc - m)
            l = jnp.sum(p, -1, keepdims=True)
            y_ref[:, sl] = _dot(p, v_ref[:, sl]) / l
            lse_t = jnp.where(col == h, m + jnp.log(l), lse_t)
        lse_ref[...] = lse_t

    return _pcall(
        body, name="mem_fwd", grid=(s // tm,),
        out_shape=[_sds((s, hw), jnp.float32), _sds((s, MEM_HEADS), jnp.float32)],
        in_specs=[pl.BlockSpec((tm, hw), lambda i: (i, C_QM // hw)), pl.BlockSpec((1, MEM_DIM), lambda i: (0, 0)),
                  pl.BlockSpec((ml, hw), lambda i: (0, 0)), pl.BlockSpec((ml, hw), lambda i: (0, 0))],
        out_specs=[pl.BlockSpec((tm, hw), lambda i: (i, 0)), pl.BlockSpec((tm, MEM_HEADS), lambda i: (i, 0))],
        sem=("parallel",))(proj, g_mq, km, vmm)


def _alibi_slope(h):
    return float(2.0 ** (-8.0 * (h + 1) / SWA_Q_HEADS))


def _swa_common(n, kp, kc, vp, vc, pq, pkp, pkc, gk):
    b = SWA_BLOCK
    k_raw = jnp.concatenate([kp, kc], axis=0)
    kn, kxn, kr = _norm_fwd(k_raw, gk, half=True)
    v = jnp.concatenate([vp, vc], axis=0)
    dist = jnp.abs(pq - jnp.concatenate([pkp, pkc], axis=1))
    r_i = lax.broadcasted_iota(jnp.int32, (b, 2 * b), 0)
    c_i = lax.broadcasted_iota(jnp.int32, (b, 2 * b), 1)
    valid = (c_i > r_i) & (c_i <= r_i + b) & (c_i >= jnp.where(n > 0, 0, b))
    bias = jnp.where(valid, -dist, NEG_INF)
    return kn, v, bias


def _swa_folded(n, kp, kc, pq, pkp, pkc, gk):
    b = SWA_BLOCK
    kn_p, _, _ = _norm_fwd(kp, gk, half=True)
    kn_c, _, _ = _norm_fwd(kc, gk, half=True)
    r_i = lax.broadcasted_iota(jnp.int32, (b, b), 0)
    c_i = lax.broadcasted_iota(jnp.int32, (b, b), 1)
    upper = c_i > r_i
    bias_prev = jnp.where(n > 0, 0.0, NEG_INF) - jnp.abs(pq - pkp)
    bias = jnp.where(upper, bias_prev, -jnp.abs(pq - pkc))
    return kn_p, kn_c, bias, upper


def _swa_specs(s):
    b = SWA_BLOCK
    return [
        pl.BlockSpec((b, 1024), lambda n: (n, C_QA // 1024)),
        pl.BlockSpec((s, LANES), lambda n: (0, C_KA // LANES)),
        pl.BlockSpec((s, LANES), lambda n: (0, C_VA // LANES)),
        pl.BlockSpec((s, 1), lambda n: (0, 0)),
        pl.BlockSpec((1, s), lambda n: (0, 0)),
        pl.BlockSpec((1, LANES), lambda n: (0, 0)),
        pl.BlockSpec((1, LANES), lambda n: (0, 0)),
        pl.BlockSpec(memory_space=pltpu.SMEM),
    ]


def _swa_window(n, k_ref, v_ref, posc_ref, posr_ref):
    b = SWA_BLOCK
    prev = pl.ds(pl.multiple_of(jnp.maximum(n - 1, 0) * b, b), b)
    cur = pl.ds(pl.multiple_of(n * b, b), b)
    return (k_ref[prev, :], k_ref[cur, :], v_ref[prev, :], v_ref[cur, :],
            posc_ref[cur, :], posr_ref[:, prev], posr_ref[:, cur])


def _swa_fwd(proj, posc, posr, gq, gk, sinks):
    s = proj.shape[0]
    b = SWA_BLOCK
    scale = SWA_DIM ** -0.5

    def body(q_ref, k_ref, v_ref, posc_ref, posr_ref, gq_ref, gk_ref, sink_ref, y_ref, lse_ref):
        n = pl.program_id(0)
        kp, kc, v_p, v_c, pq, pkp, pkc = _swa_window(n, k_ref, v_ref, posc_ref, posr_ref)
        kn_p, kn_c, bias, upper = _swa_folded(n, kp, kc, pq, pkp, pkc, gk_ref[...])
        lo = _lo_mask((b, LANES))
        col = lax.broadcasted_iota(jnp.int32, (b, SWA_Q_HEADS), 1)
        lse_t = jnp.zeros((b, SWA_Q_HEADS), jnp.float32)
        hpg = SWA_Q_HEADS // SWA_KV_HEADS
        for g in range(SWA_KV_HEADS):
            heads = range(hpg * g, hpg * (g + 1))
            kvmask = lo if g == 0 else jnp.logical_not(lo)
            qs = []
            for j in range(hpg // 2 * g, hpg // 2 * (g + 1)):
                qn, _, _ = _norm_fwd(q_ref[:, LANES * j: LANES * (j + 1)], gq_ref[...], half=True)
                qn = qn * scale
                qsw = pltpu.roll(qn, 64, 1)
                qs += [jnp.where(kvmask, qn if e == g else qsw, 0.0) for e in range(2)]
            q_st = jnp.concatenate(qs, axis=0).astype(MXU)
            sp_st, sc_st = _dot_nt(q_st, kn_p), _dot_nt(q_st, kn_c)
            pus, pls, ls = [], [], []
            for i, h in enumerate(heads):
                rows = slice(b * i, b * (i + 1))
                sc = jnp.where(upper, sp_st[rows], sc_st[rows]) + _alibi_slope(h) * bias
                sk = sink_ref[h]
                m = jnp.maximum(jnp.max(sc, -1, keepdims=True), sk)
                p = jnp.exp(sc - m)
                l = jnp.sum(p, -1, keepdims=True) + jnp.exp(sk - m)
                pus.append(jnp.where(upper, p, 0.0).astype(MXU))
                pls.append(jnp.where(upper, 0.0, p).astype(MXU))
                ls.append(l)
                lse_t = jnp.where(col == h, m + jnp.log(l), lse_t)
            o_st = _dot(jnp.concatenate(pus, axis=0), v_p) + _dot(jnp.concatenate(pls, axis=0), v_c)
            for j in range(hpg // 2 * g, hpg // 2 * (g + 1)):
                halves = []
                for e in range(2):
                    i = 2 * j + e - hpg * g
                    o_h = o_st[b * i: b * (i + 1)] / ls[i]
                    halves.append(o_h if e == g else pltpu.roll(o_h, 64, 1))
                y_ref[:, LANES * j: LANES * (j + 1)] = jnp.where(lo, halves[0], halves[1])
        lse_ref[...] = lse_t

    return _pcall(
        body, name="swa_fwd", grid=(s // b,),
        out_shape=[_sds((s, 1024), jnp.float32), _sds((s, SWA_Q_HEADS), jnp.float32)],
        in_specs=_swa_specs(s),
        out_specs=[pl.BlockSpec((b, 1024), lambda n: (n, 0)), pl.BlockSpec((b, SWA_Q_HEADS), lambda n: (n, 0))],
        sem=("parallel",))(proj, proj, proj, posc, posr, gq, gk, sinks)


def _out_proj(y_a, y_b, y_m, x, w_out, g_ffn):
    s, d = x.shape
    tm = min(2 * ROW_TILE, s)

    def body(ya_ref, yb_ref, ym_ref, x_ref, w_ref, g_ref, h1_ref, fn_ref):
        y = jnp.concatenate([ya_ref[...].astype(MXU), yb_ref[...].astype(MXU), ym_ref[...].astype(MXU)], axis=1)
        h1 = x_ref[...] + _dot(y, w_ref[...])
        h1_ref[...] = h1
        fn, _, _ = _norm_fwd(h1, g_ref[...])
        fn_ref[...] = fn.astype(fn_ref.dtype)

    def row(width):
        return pl.BlockSpec((tm, width), lambda i: (i, 0))

    return _pcall(
        body, name="out_proj", grid=(s // tm,),
        out_shape=[_sds((s, d), jnp.float32), _sds((s, d), MXU)],
        in_specs=[row(1024), row(512), row(512), row(d),
                  pl.BlockSpec(w_out.shape, lambda i: (0, 0), pipeline_mode=pl.Buffered(1)),
                  pl.BlockSpec((1, d), lambda i: (0, 0))],
        out_specs=[row(d), row(d)], sem=("parallel",))(y_a, y_b, y_m, x, w_out, g_ffn)


def _ffn_gu(fn, w_gu):
    s, d = fn.shape
    f = w_gu.shape[2]
    tm = min(2 * FFN_TILE, s)

    def body(fn_ref, w_ref, gu_ref, act_ref):
        x = fn_ref[...]
        g = _dot_nt(x, w_ref[0, 0])
        u = _dot_nt(x, w_ref[0, 1])
        gu_ref[0, 0] = g
        gu_ref[0, 1] = u
        act_ref[0] = (g * jax.nn.sigmoid(g) * u).astype(act_ref.dtype)

    return _pcall(
        body, name="ffn_gate_up", grid=(N_DEV, s // tm),
        out_shape=[_sds((N_DEV, 2, s, f), jnp.float32), _sds((N_DEV, s, f), MXU)],
        in_specs=[pl.BlockSpec((tm, d), lambda j, i: (i, 0)),
                  pl.BlockSpec((1, 2, f, d), lambda j, i: (j, 0, 0, 0))],
        out_specs=[pl.BlockSpec((1, 2, tm, f), lambda j, i: (j, 0, i, 0)),
                   pl.BlockSpec((1, tm, f), lambda j, i: (j, i, 0))],
        sem=("parallel", "parallel"))(fn, w_gu)


def _ffn_down(act, w_d, h1, target):
    _, s, f = act.shape
    d = h1.shape[1]
    tm = min(FFN_TILE, s)

    def body(a_ref, w_ref, h1_ref, t_ref, dout_ref, doutb_ref, loss_ref, acc):
        i, j = pl.program_id(0), pl.program_id(1)
        part = _dot(a_ref[0], w_ref[0]) + _dot(a_ref[1], w_ref[1])

        @pl.when(j == 0)
        def _():
            acc[...] = h1_ref[...] + part

        @pl.when(j > 0)
        def _():
            acc[...] += part

        @pl.when((i == 0) & (j == 0))
        def _():
            loss_ref[...] = jnp.zeros_like(loss_ref)

        @pl.when(j == N_DEV // 2 - 1)
        def _():
            diff = acc[...] - t_ref[...]
            dout_ref[...] = diff / d
            doutb_ref[...] = (diff / d).astype(doutb_ref.dtype)
            loss_ref[...] += 0.5 * jnp.sum(jnp.sum(diff * diff, -1, keepdims=True) / d)

    row = pl.BlockSpec((tm, d), lambda i, j: (i, 0))
    return _pcall(
        body, name="ffn_down", grid=(s // tm, N_DEV // 2),
        out_shape=[_sds((s, d), jnp.float32), _sds((s, d), MXU), _sds((8, LANES), jnp.float32)],
        in_specs=[pl.BlockSpec((2, tm, f), lambda i, j: (j, i, 0)), pl.BlockSpec((2, f, d), lambda i, j: (j, 0, 0)),
                  row, row],
        out_specs=[row, row, pl.BlockSpec((8, LANES), lambda i, j: (0, 0))],
        scratch=[pltpu.VMEM((tm, d), jnp.float32)], sem=("arbitrary", "arbitrary"))(act, w_d, h1, target)


def _ffn_bwd_act(dout, w_d, gu):
    s, d = dout.shape
    f = w_d.shape[1]
    tm = min(2 * FFN_TILE, s)
    ni = s // tm

    def body(do_ref, w_ref, gu_ref, dgu_ref, dw_ref, acc):
        i = pl.program_id(1)
        do = do_ref[...]
        d_act = _dot_nt(do, w_ref[0])
        g, u = gu_ref[0, 0], gu_ref[0, 1]
        sig = jax.nn.sigmoid(g)
        silu = g * sig
        dgu_ref[0, 0] = (d_act * u * (sig * (1.0 + g * (1.0 - sig)))).astype(dgu_ref.dtype)
        dgu_ref[0, 1] = (d_act * silu).astype(dgu_ref.dtype)
        part = _dot_tn(silu * u, do)

        @pl.when(i == 0)
        def _():
            acc[...] = part

        @pl.when(i > 0)
        def _():
            acc[...] += part

        @pl.when(i == ni - 1)
        def _():
            dw_ref[0] = acc[...].astype(dw_ref.dtype)

    return _pcall(
        body, name="ffn_bwd_act", grid=(N_DEV, ni),
        out_shape=[_sds((N_DEV, 2, s, f), MXU), _sds((N_DEV, f, d), WIRE)],
        in_specs=[pl.BlockSpec((tm, d), lambda j, i: (i, 0)), pl.BlockSpec((1, f, d), lambda j, i: (j, 0, 0)),
                  pl.BlockSpec((1, 2, tm, f), lambda j, i: (j, 0, i, 0))],
        out_specs=[pl.BlockSpec((1, 2, tm, f), lambda j, i: (j, 0, i, 0)),
                   pl.BlockSpec((1, f, d), lambda j, i: (j, 0, 0))],
        scratch=[pltpu.VMEM((f, d), jnp.float32)], sem=("parallel", "arbitrary"))(dout, w_d, gu)


def _ffn_dw_gu(fn, dgu):
    s, d = fn.shape
    f = dgu.shape[-1]
    tk = min(4 * FFN_TILE, s)
    nk = s // tk

    def body(fn_ref, dgu_ref, dw_ref, acc):
        k = pl.program_id(2)
        part = _dot_tn(dgu_ref[0, 0], fn_ref[...])

        @pl.when(k == 0)
        def _():
            acc[...] = part

        @pl.when(k > 0)
        def _():
            acc[...] += part

        @pl.when(k == nk - 1)
        def _():
            dw_ref[0, 0] = acc[...].astype(dw_ref.dtype)

    return _pcall(
        body, name="ffn_dw_gate_up", grid=(N_DEV, 2, nk),
        out_shape=_sds((N_DEV, 2, f, d), WIRE),
        in_specs=[pl.BlockSpec((tk, d), lambda j, w, k: (k, 0)),
                  pl.BlockSpec((1, 1, tk, f), lambda j, w, k: (j, w, k, 0))],
        out_specs=pl.BlockSpec((1, 1, f, d), lambda j, w, k: (j, w, 0, 0)),
        scratch=[pltpu.VMEM((f, d), jnp.float32)], sem=("parallel", "parallel", "arbitrary"))(fn, dgu)


def _ffn_dfn(dgu, w_gu, after):
    _, _, s, f = dgu.shape
    d = w_gu.shape[3]
    tm = min(FFN_TILE, s)

    def body(dgu_ref, w_ref, dfn_ref):
        j = pl.program_id(1)
        part = (_dot(dgu_ref[0, 0], w_ref[0, 0]) + _dot(dgu_ref[0, 1], w_ref[0, 1])
                + _dot(dgu_ref[1, 0], w_ref[1, 0]) + _dot(dgu_ref[1, 1], w_ref[1, 1]))

        @pl.when(j == 0)
        def _():
            dfn_ref[...] = part

        @pl.when(j > 0)
        def _():
            dfn_ref[...] += part

    return _pcall(
        body, name="ffn_dfn", grid=(s // tm, N_DEV // 2),
        out_shape=_sds((s, d), jnp.float32),
        in_specs=[pl.BlockSpec((2, 2, tm, f), lambda i, j: (j, 0, i, 0)),
                  pl.BlockSpec((2, 2, f, d), lambda i, j: (j, 0, 0, 0))],
        out_specs=pl.BlockSpec((tm, d), lambda i, j: (i, 0)),
        sem=("parallel", "arbitrary"), after=after)(dgu, w_gu)


def _ffn_norm_bwd(d_fn, dout, h1, g_ffn):
    s, d = h1.shape
    tm = min(2 * ROW_TILE, s)

    def body(dfn_ref, do_ref, h1_ref, g_ref, dh1_ref, dg_ref):
        i = pl.program_id(0)

        @pl.when(i == 0)
        def _():
            dg_ref[...] = jnp.zeros_like(dg_ref)

        _, xn, r = _norm_fwd(h1_ref[...], g_ref[...])
        dx, dg = _norm_bwd(xn, r, g_ref[...], dfn_ref[...])
        dh1_ref[...] = do_ref[...] + dx
        dg_ref[...] += dg

    row = pl.BlockSpec((tm, d), lambda i: (i, 0))
    vec = pl.BlockSpec((1, d), lambda i: (0, 0))
    return _pcall(
        body, name="ffn_norm_bwd", grid=(s // tm,),
        out_shape=[_sds((s, d), jnp.float32), _sds((1, d), jnp.float32)],
        in_specs=[row, row, row, vec], out_specs=[row, vec], sem=("arbitrary",))(d_fn, dout, h1, g_ffn)


def _mem_bwd(proj, g_mq, km, vmm, d_y, y_m, lse):
    s = proj.shape[0]
    ml, hw = km.shape
    tm = min(FFN_TILE, s)
    scale = MEM_DIM ** -0.5

    def body(q_ref, g_ref, k_ref, v_ref, do_ref, y_ref, lse_ref, dq_ref, dk_ref, dv_ref, dg_ref):
        i = pl.program_id(0)

        @pl.when(i == 0)
        def _():
            dk_ref[...] = jnp.zeros_like(dk_ref)
            dv_ref[...] = jnp.zeros_like(dv_ref)
            dg_ref[...] = jnp.zeros_like(dg_ref)

        col = lax.broadcasted_iota(jnp.int32, (tm, MEM_HEADS), 1)
        lse_t = lse_ref[...]
        for h in range(MEM_HEADS):
            sl = slice(MEM_DIM * h, MEM_DIM * (h + 1))
            qn, xn, r = _norm_fwd(q_ref[:, sl], g_ref[...])
            lse_h = jnp.sum(jnp.where(col == h, lse_t, 0.0), -1, keepdims=True)
            p = jnp.exp(_dot_nt(qn, k_ref[:, sl]) * scale - lse_h)
            do = do_ref[:, sl]
            dd = jnp.sum(do * y_ref[:, sl], -1, keepdims=True)
            dp = _dot_nt(do, v_ref[:, sl])
            ds = (p * (dp - dd)).astype(MXU)
            dv_ref[:, sl] += _dot_tn(p, do)
            dk_ref[:, sl] += _dot_tn(ds, qn) * scale
            dx, dg = _norm_bwd(xn, r, g_ref[...], _dot(ds, k_ref[:, sl]) * scale)
            dq_ref[:, sl] = dx.astype(dq_ref.dtype)
            dg_ref[...] += dg

    full = pl.BlockSpec((ml, hw), lambda i: (0, 0))
    return _pcall(
        body, name="mem_bwd", grid=(s // tm,),
        out_shape=[_sds((s, hw), MXU), _sds((ml, hw), jnp.float32), _sds((ml, hw), jnp.float32),
                   _sds((1, MEM_DIM), jnp.float32)],
        in_specs=[pl.BlockSpec((tm, hw), lambda i: (i, C_QM // hw)), pl.BlockSpec((1, MEM_DIM), lambda i: (0, 0)),
                  full, full, pl.BlockSpec((tm, hw), lambda i: (i, 3)), pl.BlockSpec((tm, hw), lambda i: (i, 0)),
                  pl.BlockSpec((tm, MEM_HEADS), lambda i: (i, 0))],
        out_specs=[pl.BlockSpec((tm, hw), lambda i: (i, 0)), full, full,
                   pl.BlockSpec((1, MEM_DIM), lambda i: (0, 0))],
        sem=("arbitrary",))(proj, g_mq, km, vmm, d_y, y_m, lse)


def _memkv_bwd(mem, g_mem, w_mkv, g_mk, kv, memn, dk, dv):
    ml, d = mem.shape
    hw = MEM_HEADS * MEM_DIM

    def body(mem_ref, g_ref, w_ref, gk_ref, kv_ref, mn_ref, dk_ref, dv_ref, dw_ref, dgm_ref, dgk_ref):
        parts = []
        dgk = jnp.zeros((1, MEM_DIM), jnp.float32)
        for h in range(MEM_HEADS):
            sl = slice(MEM_DIM * h, MEM_DIM * (h + 1))
            _, xn, r = _norm_fwd(kv_ref[:, sl], gk_ref[...])
            dx, dg = _norm_bwd(xn, r, gk_ref[...], dk_ref[:, sl])
            parts.append(dx)
            dgk = dgk + dg
        dkv = jnp.concatenate(parts + [dv_ref[...]], axis=1).astype(MXU)
        dgk_ref[...] = dgk
        dw_ref[...] = _dot_tn(mn_ref[...], dkv).astype(dw_ref.dtype)
        d_mn = _dot_nt(dkv, w_ref[...])
        _, xn, _ = _norm_fwd(mem_ref[...], g_ref[...])
        dgm_ref[...] = jnp.sum(d_mn * xn, 0, keepdims=True)

    vm = pl.BlockSpec(memory_space=pltpu.VMEM)
    return _pcall(
        body, name="memkv_bwd",
        out_shape=[_sds((d, 2 * hw), WIRE), _sds((1, d), jnp.float32), _sds((1, MEM_DIM), jnp.float32)],
        in_specs=[vm] * 8, out_specs=[vm] * 3)(mem, g_mem, w_mkv, g_mk, kv, memn, dk, dv)


def _mla_bwd(qc, kc, v, d_y, y_b, lse, after):
    nh, s, _ = qc.shape
    t = min(ATT_TILE, s)
    nb = s // t
    scale = (MLA_NOPE + MLA_ROPE) ** -0.5

    def body(q_ref, k_ref, v_ref, do_ref, y_ref, lse_ref, dq_ref, dk_ref, dv_ref, dk_acc, dv_acc):
        kj, qi = _tri_cols(pl.program_id(1), nb)

        @pl.when((kj == 0) & (qi == 0))
        def _():
            dq_ref[...] = jnp.zeros_like(dq_ref)

        @pl.when(qi == kj)
        def _():
            dk_acc[...] = jnp.zeros_like(dk_acc)
            dv_acc[...] = jnp.zeros_like(dv_acc)

        def step(diagonal):
            rc = t // 4 if diagonal else t
            for c in range(t // rc):
                rows = slice(rc * c, rc * (c + 1))
                keys = slice(0, rc * (c + 1))
                q, k = q_ref[0, rows, :], k_ref[0, keys, :]
                sc = _dot_nt(q, k) * (scale * LOG2E)
                if diagonal:
                    r_i = lax.broadcasted_iota(jnp.int32, sc.shape, 0) + rc * c
                    c_i = lax.broadcasted_iota(jnp.int32, sc.shape, 1)
                    sc = jnp.where(c_i <= r_i, sc, NEG_INF)
                p = jnp.exp2(sc - lse_ref[0, rows, :])
                do = do_ref[rows, :]
                dd = jnp.sum(do * y_ref[rows, :], -1, keepdims=True)
                dp = _dot_nt(do, v_ref[0, keys, :])
                ds = (p * (dp - dd) * scale).astype(MXU)
                dv_acc[keys, :] += _dot_tn(p, do)
                dk_acc[keys, :] += _dot_tn(ds, q)
                out_rows = pl.ds(pl.multiple_of(qi * t + rc * c, rc), rc)
                dq_ref[0, out_rows, :] += _dot(ds, k)

        @pl.when(qi > kj)
        def _():
            step(False)

        @pl.when(qi == kj)
        def _():
            step(True)

        @pl.when(qi == nb - 1)
        def _():
            dk_ref[0] = dk_acc[...]
            dv_ref[0] = dv_acc[...]

    qmap = lambda h, p: (h, _tri_cols(p, nb)[1], 0)
    kmap = lambda h, p: (h, _tri_cols(p, nb)[0], 0)
    return _pcall(
        body, name="mla_bwd", grid=(nh, (nb * (nb + 1)) // 2),
        out_shape=[_sds((nh, s, 256), jnp.float32), _sds((nh, s, 256), jnp.float32),
                   _sds((nh, s, MLA_V), jnp.float32)],
        in_specs=[pl.BlockSpec((1, t, 256), qmap),
                  pl.BlockSpec((1, t, 256), kmap),
                  pl.BlockSpec((1, t, MLA_V), kmap),
                  pl.BlockSpec((t, MLA_V), lambda h, p: (_tri_cols(p, nb)[1], 8 + h)),
                  pl.BlockSpec((t, MLA_V), lambda h, p: (_tri_cols(p, nb)[1], h)),
                  pl.BlockSpec((1, t, 1), qmap)],
        out_specs=[pl.BlockSpec((1, s, 256), lambda h, p: (h, 0, 0)),
                   pl.BlockSpec((1, t, 256), kmap),
                   pl.BlockSpec((1, t, MLA_V), kmap)],
        scratch=[pltpu.VMEM((t, 256), jnp.float32), pltpu.VMEM((t, MLA_V), jnp.float32)],
        sem=("parallel", "arbitrary"), after=after)(qc, kc, v, d_y, y_b, lse)


def _mla_prep_bwd(proj, cos, sin, g_cq, g_ckv, w_uq, w_ukv, g_qn, g_qr, g_kn, g_kr,
                  qb, kvb, cqn, ckvn, dqc, dkc, dv):
    s = proj.shape[0]
    tm = min(ROW_TILE, s)
    nh = MLA_HEADS
    ni = s // tm

    def body(cq_ref, ckv_ref, kr_ref, cos_ref, sin_ref, gcq_ref, gckv_ref, wuq_ref, wukv_ref,
             gqn_ref, gqr_ref, gkn_ref, gkr_ref, qb_ref, kvb_ref, cqn_ref, ckvn_ref, dqc_ref, dkc_ref, dv_ref,
             dcq_ref, dckv_ref, dkr_ref, dwuq_ref, dwukv_ref,
             dgcq_ref, dgckv_ref, dgqn_ref, dgqr_ref, dgkn_ref, dgkr_ref, acc_uq, acc_ukv):
        i = pl.program_id(0)

        @pl.when(i == 0)
        def _():
            acc_uq[...] = jnp.zeros_like(acc_uq)
            acc_ukv[...] = jnp.zeros_like(acc_ukv)
            for ref in (dgcq_ref, dgckv_ref, dgqn_ref, dgqr_ref, dgkn_ref, dgkr_ref):
                ref[...] = jnp.zeros_like(ref)

        cos_t, sin_t = cos_ref[...], sin_ref[...]
        lo = _lo_mask((tm, LANES))
        qb_v, kvb_v = qb_ref[...], kvb_ref[...]
        dq_parts, dgqn = [], jnp.zeros((1, LANES), jnp.float32)
        for h in range(nh):
            _, xn, r = _norm_fwd(qb_v[:, MLA_NOPE * h: MLA_NOPE * (h + 1)], gqn_ref[...])
            dx, dg = _norm_bwd(xn, r, gqn_ref[...], dqc_ref[h][:, :MLA_NOPE])
            dq_parts.append(dx)
            dgqn = dgqn + dg
        dgqn_ref[...] += dgqn
        dgqr = jnp.zeros((1, LANES), jnp.float32)
        for j in range(nh // 2):
            d_rope = jnp.where(lo, dqc_ref[2 * j][:, MLA_NOPE:], dqc_ref[2 * j + 1][:, MLA_NOPE:])
            d_pre = _rope_bwd(d_rope, cos_t, sin_t)
            xr = qb_v[:, nh * MLA_NOPE + LANES * j: nh * MLA_NOPE + LANES * (j + 1)]
            _, xn, r = _norm_fwd(xr, gqr_ref[...], half=True)
            dx, dg = _norm_bwd(xn, r, gqr_ref[...], d_pre, half=True)
            dq_parts.append(dx)
            dgqr = dgqr + dg
        dgqr_ref[...] += dgqr
        dqb = jnp.concatenate(dq_parts, axis=1).astype(MXU)
        acc_uq[...] += _dot_tn(dqb, cqn_ref[...])
        _, xn, r = _norm_fwd(cq_ref[...], gcq_ref[...])
        dx, dg = _norm_bwd(xn, r, gcq_ref[...], _dot(dqb, wuq_ref[...]))
        dcq_ref[...] = dx.astype(dcq_ref.dtype)
        dgcq_ref[...] += dg
        dkv_parts, dgkn = [], jnp.zeros((1, LANES), jnp.float32)
        d_kr2 = jnp.zeros((tm, LANES), jnp.float32)
        for h in range(nh):
            _, xn, r = _norm_fwd(kvb_v[:, 256 * h: 256 * h + MLA_NOPE], gkn_ref[...])
            dx, dg = _norm_bwd(xn, r, gkn_ref[...], dkc_ref[h][:, :MLA_NOPE])
            dkv_parts += [dx, dv_ref[h]]
            dgkn = dgkn + dg
            d_kr2 = d_kr2 + dkc_ref[h][:, MLA_NOPE:]
        dgkn_ref[...] += dgkn
        dkvb = jnp.concatenate(dkv_parts, axis=1).astype(MXU)
        part_ukv = _dot_tn(ckvn_ref[...], dkvb)
        for dev in range(N_DEV):
            acc_ukv[dev] += part_ukv[:, LANES * dev: LANES * (dev + 1)]
        w_ukv_full = jnp.concatenate([wukv_ref[dev] for dev in range(N_DEV)], axis=1)
        d_ckvn = _dot_nt(dkvb, w_ukv_full)
        _, xn, r = _norm_fwd(ckv_ref[...], gckv_ref[...])
        dx, dg = _norm_bwd(xn, r, gckv_ref[...], d_ckvn)
        dckv_ref[...] = dx.astype(dckv_ref.dtype)
        dgckv_ref[...] += dg
        d_kr = jnp.where(lo, d_kr2 + pltpu.roll(d_kr2, 64, 1), 0.0)
        d_pre = _rope_bwd(d_kr, cos_t, sin_t)
        _, xn, r = _norm_fwd(kr_ref[...], gkr_ref[...], half=True)
        dx, dg = _norm_bwd(xn, r, gkr_ref[...], d_pre, half=True)
        dkr_ref[...] = jnp.where(lo, dx, 0.0).astype(dkr_ref.dtype)
        dgkr_ref[...] += jnp.where(_lo_mask((1, LANES)), dg, 0.0)

        @pl.when(i == ni - 1)
        def _():
            dwuq_ref[...] = acc_uq[...].astype(dwuq_ref.dtype)
            dwukv_ref[...] = acc_ukv[...].astype(dwukv_ref.dtype)

    def col(width, start):
        return pl.BlockSpec((tm, width), lambda i: (i, start // width))

    def full(shape):
        return pl.BlockSpec(shape, lambda i: (0,) * len(shape))

    def row(width):
        return pl.BlockSpec((tm, width), lambda i: (i, 0))

    def heads(width):
        return pl.BlockSpec((nh, tm, width), lambda i: (0, i, 0))

    vec = full((1, LANES))
    return _pcall(
        body, name="mla_prep_bwd", grid=(ni,),
        out_shape=[_sds((s, 512), MXU), _sds((s, 512), MXU), _sds((s, LANES), MXU),
                   _sds((768, 512), WIRE), _sds((N_DEV, 512, LANES), WIRE),
                   _sds((1, 512), jnp.float32), _sds((1, 512), jnp.float32)] + [_sds((1, LANES), jnp.float32)] * 4,
        in_specs=[col(512, C_CQ), col(512, C_CKV), col(LANES, C_KR), row(LANES), row(LANES),
                  full((1, 512)), full((1, 512)), full((768, 512)), full((N_DEV, 512, LANES)), vec, vec, vec, vec,
                  row(768), row(1024), row(512), row(512), heads(256), heads(256), heads(MLA_V)],
        out_specs=[row(512), row(512), row(LANES), full((768, 512)), full((N_DEV, 512, LANES)),
                   full((1, 512)), full((1, 512)), vec, vec, vec, vec],
        scratch=[pltpu.VMEM((768, 512), jnp.float32), pltpu.VMEM((N_DEV, 512, LANES), jnp.float32)],
        sem=("arbitrary",))(proj, proj, proj, cos, sin, g_cq, g_ckv, w_uq, w_ukv, g_qn, g_qr, g_kn, g_kr,
                            qb, kvb, cqn, ckvn, dqc, dkc, dv)


def _swa_bwd(proj, posc, posr, gq, gk, sinks, d_y, y_a, lse, after):
    s = proj.shape[0]
    b = SWA_BLOCK
    nb = s // b
    scale = SWA_DIM ** -0.5

    def body(q_ref, k_ref, v_ref, posc_ref, posr_ref, gq_ref, gk_ref, sink_ref, do_ref, y_ref, lse_ref,
             dq_ref, dk_ref, dv_ref, dgq_ref, dgk_ref, dsink_ref, dk_acc, dv_acc):
        n = pl.program_id(0)

        @pl.when(n == 0)
        def _():
            dk_acc[...] = jnp.zeros_like(dk_acc)
            dv_acc[...] = jnp.zeros_like(dv_acc)
            dgq_ref[...] = jnp.zeros_like(dgq_ref)
            dsink_ref[...] = jnp.zeros_like(dsink_ref)

        kn, v, bias = _swa_common(n, *_swa_window(n, k_ref, v_ref, posc_ref, posr_ref), gk_ref[...])
        lo = _lo_mask((b, LANES))
        col = lax.broadcasted_iota(jnp.int32, (b, SWA_Q_HEADS), 1)
        col1 = lax.broadcasted_iota(jnp.int32, (1, SWA_Q_HEADS), 1)
        lse_t = lse_ref[...]
        dk_blk = jnp.zeros((2 * b, LANES), jnp.float32)
        dv_blk = jnp.zeros((2 * b, LANES), jnp.float32)
        dgq = jnp.zeros((1, LANES), jnp.float32)
        dsink = jnp.zeros((1, SWA_Q_HEADS), jnp.float32)
        for j in range(SWA_Q_HEADS // 2):
            hk = (2 * j) // (SWA_Q_HEADS // SWA_KV_HEADS)
            kvmask = lo if hk == 0 else jnp.logical_not(lo)
            sl = slice(LANES * j, LANES * (j + 1))
            qn, xn, r = _norm_fwd(q_ref[:, sl], gq_ref[...], half=True)
            qn = qn * scale
            qsw = pltpu.roll(qn, 64, 1)
            d2 = do_ref[:, sl]
            d2sw = pltpu.roll(d2, 64, 1)
            prod = d2 * y_ref[:, sl]
            dqs = []
            for e in range(2):
                h = 2 * j + e
                half_e = lo if e == 0 else jnp.logical_not(lo)
                qm = jnp.where(kvmask, qn if e == hk else qsw, 0.0)
                dm = jnp.where(kvmask, d2 if e == hk else d2sw, 0.0)
                sc = _dot_nt(qm, kn) + _alibi_slope(h) * bias
                lse_h = jnp.sum(jnp.where(col == h, lse_t, 0.0), -1, keepdims=True)
                p = jnp.exp(sc - lse_h)
                dd = jnp.sum(jnp.where(half_e, prod, 0.0), -1, keepdims=True)
                dp = _dot_nt(dm, v)
                ds = (p * (dp - dd)).astype(MXU)
                dsink = dsink - jnp.where(col1 == h, jnp.sum(jnp.exp(sink_ref[h] - lse_h) * dd), 0.0)
                dq_m = _dot(ds, kn) * scale
                dk_blk = dk_blk + _dot_tn(ds, qm)
                dv_blk = dv_blk + _dot_tn(p, dm)
                dqs.append(dq_m if e == hk else pltpu.roll(dq_m, 64, 1))
            dx, dg = _norm_bwd(xn, r, gq_ref[...], jnp.where(lo, dqs[0], dqs[1]), half=True)
            dq_ref[:, sl] = dx.astype(dq_ref.dtype)
            dgq = dgq + dg
        dgq_ref[...] += dgq
        dsink_ref[...] += dsink
        prev = pl.ds(pl.multiple_of(jnp.maximum(n - 1, 0) * b, b), b)
        cur = pl.ds(pl.multiple_of(n * b, b), b)
        dk_acc[prev, :] += dk_blk[:b]
        dv_acc[prev, :] += dv_blk[:b]
        dk_acc[cur, :] += dk_blk[b:]
        dv_acc[cur, :] += dv_blk[b:]

        @pl.when(n == nb - 1)
        def _():
            _, kxn, kr = _norm_fwd(k_ref[...], gk_ref[...], half=True)
            dx, dg = _norm_bwd(kxn, kr, gk_ref[...], dk_acc[...], half=True)
            dk_ref[...] = dx.astype(dk_ref.dtype)
            dv_ref[...] = dv_acc[...].astype(dv_ref.dtype)
            dgk_ref[...] = dg

    full = pl.BlockSpec((s, LANES), lambda n: (0, 0))
    vec = pl.BlockSpec((1, LANES), lambda n: (0, 0))
    return _pcall(
        body, name="swa_bwd", grid=(nb,),
        out_shape=[_sds((s, 1024), MXU), _sds((s, LANES), MXU), _sds((s, LANES), MXU),
                   _sds((1, LANES), jnp.float32), _sds((1, LANES), jnp.float32),
                   _sds((1, SWA_Q_HEADS), jnp.float32)],
        in_specs=_swa_specs(s) + [pl.BlockSpec((b, 1024), lambda n: (n, 0)), pl.BlockSpec((b, 1024), lambda n: (n, 0)),
                                  pl.BlockSpec((b, SWA_Q_HEADS), lambda n: (n, 0))],
        out_specs=[pl.BlockSpec((b, 1024), lambda n: (n, 0)), full, full, vec, vec,
                   pl.BlockSpec((1, SWA_Q_HEADS), lambda n: (0, 0))],
        scratch=[pltpu.VMEM((s, LANES), jnp.float32), pltpu.VMEM((s, LANES), jnp.float32)],
        sem=("arbitrary",), after=after)(proj, proj, proj, posc, posr, gq, gk, sinks, d_y, y_a, lse)


def _dx(d_proj, w_in, x, g, d_h1, after):
    s, d = x.shape
    n = w_in.shape[0]
    tm = min(2 * ROW_TILE, s)

    n_pc = len(d_proj)

    def body(*refs):
        dp_refs, (w_ref, x_ref, g_ref, dh_ref, dx_ref, dg_ref) = refs[:n_pc], refs[n_pc:]
        i = pl.program_id(0)

        @pl.when(i == 0)
        def _():
            dg_ref[...] = jnp.zeros_like(dg_ref)

        d_hn = _dot(jnp.concatenate([r[...] for r in dp_refs], axis=1), w_ref[...])
        _, xn, r = _norm_fwd(x_ref[...], g_ref[...])
        dx, dg = _norm_bwd(xn, r, g_ref[...], d_hn)
        dx_ref[...] = dh_ref[...] + dx
        dg_ref[...] += dg

    row = pl.BlockSpec((tm, d), lambda i: (i, 0))
    vec = pl.BlockSpec((1, d), lambda i: (0, 0))
    return _pcall(
        body, name="grad_x", grid=(s // tm,),
        out_shape=[_sds((s, d), jnp.float32), _sds((1, d), jnp.float32)],
        in_specs=[pl.BlockSpec((tm, p.shape[1]), lambda i: (i, 0)) for p in d_proj] + [
                  pl.BlockSpec((n, d), lambda i: (0, 0), pipeline_mode=pl.Buffered(1)), row, vec, row],
        out_specs=[row, vec], sem=("arbitrary",), after=after)(*d_proj, w_in, x, g, d_h1)


_SMALL = ["attn_norm_g", "swa_q_norm_g", "swa_k_norm_g", "swa_sinks", "mla_cq_norm_g", "mla_ckv_norm_g",
          "mla_qn_norm_g", "mla_qr_norm_g", "mla_kn_norm_g", "mla_kr_norm_g", "mem_norm_g",
          "mem_q_norm_g", "mem_k_norm_g", "ffn_norm_g"]


def kernel(x, mem, positions, attn_norm_g, w_in, swa_q_norm_g, swa_k_norm_g, swa_sinks, mla_cq_norm_g, mla_ckv_norm_g, w_uq, w_ukv, mla_qn_norm_g, mla_qr_norm_g, mla_kn_norm_g, mla_kr_norm_g, mem_norm_g, w_mem_kv, mem_q_norm_g, mem_k_norm_g, w_out, ffn_norm_g, w_gate, w_up, w_down, loss_target, m_attn_norm_g, m_w_in, m_swa_q_norm_g, m_swa_k_norm_g, m_swa_sinks, m_mla_cq_norm_g, m_mla_ckv_norm_g, m_w_uq, m_w_ukv, m_mla_qn_norm_g, m_mla_qr_norm_g, m_mla_kn_norm_g, m_mla_kr_norm_g, m_mem_norm_g, m_w_mem_kv, m_mem_q_norm_g, m_mem_k_norm_g, m_w_out, m_ffn_norm_g, m_w_gate, m_w_up, m_w_down, v_attn_norm_g, v_w_in, v_swa_q_norm_g, v_swa_k_norm_g, v_swa_sinks, v_mla_cq_norm_g, v_mla_ckv_norm_g, v_w_uq, v_w_ukv, v_mla_qn_norm_g, v_mla_qr_norm_g, v_mla_kn_norm_g, v_mla_kr_norm_g, v_mem_norm_g, v_w_mem_kv, v_mem_q_norm_g, v_mem_k_norm_g, v_w_out, v_ffn_norm_g, v_w_gate, v_w_up, v_w_down):
    args = dict(locals())
    x2, mem2, tgt = x[0], mem[0], loss_target[0]
    s, d = x2.shape
    n_in = w_in.shape[2]
    f = w_gate.shape[2]

    in_shards = [w_in[0].T.astype(WIRE)]
    (g_in,) = _all_gather_background(in_shards, 7, "all_gather_in_weights")
    tok = in_shards[0]
    mix_shards = [w_uq[0].T.astype(WIRE), w_ukv[0].astype(WIRE), w_mem_kv[0].astype(WIRE),
                  _to_wire([w_out[0]], tok, "wire_out")[0]]
    g_uq, wkv, g_mkv, g_out = _all_gather_background(mix_shards, 5, "all_gather_mix_weights")
    ffn_shards = [_to_wire([w_gate[0].T, w_up[0].T], tok, "wire_gate_up")]
    (w_gu,) = _all_gather_background(ffn_shards, 1, "all_gather_ffn_weights")
    down_shards = [_to_wire([w_down[0]], tok, "wire_down")[0]]
    (w_d,) = _all_gather_background(down_shards, 6, "all_gather_down_weights")
    wi = g_in.reshape(N_DEV * n_in, d)
    wi = jnp.concatenate([wi[0:1024], wi[1280:1792], wi[1792:2304], wi[2368:2880],
                          wi[1024:1152], wi[1152:1280], wi[2304:2368],
                          jnp.zeros((IN_PAD - 2880, d), wi.dtype)], axis=0)
    wq = g_uq.reshape(768, 512)
    wq = jnp.concatenate([wq[192 * h: 192 * h + 128] for h in range(4)]
                         + [wq[192 * h + 128: 192 * (h + 1)] for h in range(4)], axis=0)
    wmkv = g_mkv.reshape(-1, g_mkv.shape[-1])
    wo = g_out.reshape(-1, d)

    pos = positions[0].astype(jnp.float32)
    inv_freq = ROPE_THETA ** (-jnp.arange(0, MLA_ROPE, 2, dtype=jnp.float32) / MLA_ROPE)
    ang = pos[:, None] * inv_freq
    cos32, sin32 = jnp.cos(ang), jnp.sin(ang)
    cos_t = jnp.tile(cos32, (1, 4))
    sin_t = jnp.tile(jnp.concatenate([-sin32, sin32], axis=1), (1, 2))
    posc, posr = pos.reshape(s, 1), pos.reshape(1, s)
    two = lambda g: jnp.tile(g, (1, 2))
    gq2, gk2, gqr2, gkr2 = two(swa_q_norm_g), two(swa_k_norm_g), two(mla_qr_norm_g), two(mla_kr_norm_g)
    sinks1 = swa_sinks[0]

    hn = _norm_rows(x2, attn_norm_g)
    proj = _mm(hn, wi, tb=True, out_dtype=jnp.float32, tm=FFN_TILE, tk=d, name="in_proj")
    qc, kc, vb, qb, kvb, cqn, ckvn = _mla_prep(proj, cos_t, sin_t, mla_cq_norm_g, mla_ckv_norm_g, wq, wkv,
                                                mla_qn_norm_g, gqr2, mla_kn_norm_g, gkr2)
    y_b, lse_b = _mla_fwd(qc, kc, vb)
    km, vmm, kvm, memn = _memkv_prep(mem2, mem_norm_g, wmkv, mem_k_norm_g)
    y_m, lse_m = _mem_fwd(proj, mem_q_norm_g, km, vmm)
    y_a, lse_a = _swa_fwd(proj, posc, posr, gq2, gk2, sinks1)
    h1, fn = _out_proj(y_a, y_b, y_m, x2, wo, ffn_norm_g)
    gu, act = _ffn_gu(fn, w_gu)
    dout, dout_b, loss_tile = _ffn_down(act, w_d, h1, tgt)

    dgu, dw_d = _ffn_bwd_act(dout_b, w_d, gu)
    (r_d,) = _exchange_grads_background([dw_d], 8, "exchange_down_grads")
    dw_gu = _ffn_dw_gu(fn, dgu)
    (r_gu,) = _exchange_grads_background([dw_gu], 2, "exchange_ffn_grads")
    d_h1, dg_ffn = _ffn_norm_bwd(_ffn_dfn(dgu, w_gu, dw_gu), dout, h1, ffn_norm_g)
    d_y = _mm(d_h1, wo, tb=True, out_dtype=jnp.float32, tm=FFN_TILE, tk=2048, name="d_mix")
    dw_out = jnp.concatenate([
        _mm(y_a, d_h1, ta=True, out_dtype=WIRE, tm=1024, tk=1024, name="dw_out_a"),
        _mm(y_b, d_h1, ta=True, out_dtype=WIRE, tm=1024, tk=1024, name="dw_out_b"),
        _mm(y_m, d_h1, ta=True, out_dtype=WIRE, tm=1024, tk=1024, name="dw_out_m")], axis=0)
    d_qm, dkm, dvmm, dg_mq = _mem_bwd(proj, mem_q_norm_g, km, vmm, d_y, y_m, lse_m)
    dw_mkv, dg_mem, dg_mk = _memkv_bwd(mem2, mem_norm_g, wmkv, mem_k_norm_g, kvm, memn, dkm, dvmm)
    r_mkv, r_out = _exchange_grads_background([dw_mkv.reshape(g_mkv.shape), dw_out.reshape(g_out.shape)], 3,
                                              "exchange_mix_grads")
    dqc, dkc, dvb = _mla_bwd(qc, kc, vb, d_y, y_b, lse_b, dw_mkv)
    (d_cq, d_ckv, d_kr, dw_uq, dw_ukv, dg_cq, dg_ckv, dg_qn, dg_qr, dg_kn, dg_kr) = _mla_prep_bwd(
        proj, cos_t, sin_t, mla_cq_norm_g, mla_ckv_norm_g, wq, wkv, mla_qn_norm_g, gqr2, mla_kn_norm_g, gkr2,
        qb, kvb, cqn, ckvn, dqc, dkc, dvb)
    d_qa, d_ka, d_va, dg_q, dg_k, d_sinks = _swa_bwd(proj, posc, posr, gq2, gk2, sinks1, d_y, y_a, lse_a, dw_out)
    d_proj = [d_qa, d_cq, d_ckv, d_qm, d_ka, d_va, d_kr]
    gi = _dw_in(hn, d_proj, n_in, None)

    gq_ = jnp.concatenate(sum([[dw_uq[128 * h: 128 * (h + 1)], dw_uq[512 + 64 * h: 512 + 64 * (h + 1)]]
                               for h in range(4)], []), axis=0)
    gq_ = gq_.reshape(N_DEV, 96, 512)
    r_in, r_uq, r_ukv = _exchange_grads_background([gi, gq_, dw_ukv], 4, "exchange_in_grads")
    grad_x, dg_attn = _dx(d_proj, wi, x2, attn_norm_g, d_h1, gi)
    small_g = {
        "attn_norm_g": dg_attn, "swa_q_norm_g": dg_q, "swa_k_norm_g": dg_k,
        "swa_sinks": d_sinks, "mla_cq_norm_g": dg_cq, "mla_ckv_norm_g": dg_ckv, "mla_qn_norm_g": dg_qn,
        "mla_qr_norm_g": dg_qr, "mla_kn_norm_g": dg_kn, "mla_kr_norm_g": dg_kr,
        "mem_norm_g": dg_mem, "mem_q_norm_g": dg_mq, "mem_k_norm_g": dg_mk, "ffn_norm_g": dg_ffn}
    pack = _small_pack([small_g[n] for n in _SMALL], loss_tile, [args[n].shape[-1] for n in _SMALL])
    pack8 = jnp.broadcast_to(pack, (N_DEV,) + pack.shape[1:])
    (packs,) = _exchange_grads_background([pack8], 9, "exchange_small_grads")

    big = {}
    last = [None]

    def adam(name, r, transposed=False, which=None):
        w, m, v = args[name][0], args["m_" + name][0], args["v_" + name][0]
        if transposed:
            outs = _adam_big(r, w.T, m.T, v.T, "adam_" + name, last[0], which)
            big[name] = [o.T[None] for o in outs]
        else:
            outs = _adam_big(r, w, m, v, "adam_" + name, last[0])
            big[name] = [o[None] for o in outs]
        last[0] = outs[0]

    adam("w_down", r_d)
    adam("w_gate", r_gu, True, which=0)
    adam("w_up", r_gu, True, which=1)
    adam("w_out", r_out)
    adam("w_mem_kv", r_mkv)
    adam("w_in", r_in, True)
    adam("w_uq", r_uq, True)
    adam("w_ukv", r_ukv)

    loss11, small_out = _small_adam(packs, [args[n] for n in _SMALL], [args["m_" + n] for n in _SMALL],
                                    [args["v_" + n] for n in _SMALL], last[0])
    small = dict(zip(_SMALL, small_out))
    loss = loss11.reshape(())

    order = ["attn_norm_g", "w_in", "swa_q_norm_g", "swa_k_norm_g", "swa_sinks", "mla_cq_norm_g", "mla_ckv_norm_g",
             "w_uq", "w_ukv", "mla_qn_norm_g", "mla_qr_norm_g", "mla_kn_norm_g", "mla_kr_norm_g", "mem_norm_g",
             "w_mem_kv", "mem_q_norm_g", "mem_k_norm_g", "w_out", "ffn_norm_g", "w_gate", "w_up", "w_down"]
    res = {n: (big[n] if n in big else list(small[n])) for n in order}
    outs = [loss, grad_x[None]]
    for kind in range(4):
        outs += [res[n][kind] for n in order]
    return tuple(outs)
```

```python
import jax
import jax.numpy as jnp
from jax import lax
from jax.experimental import pallas as pl
from jax.experimental.pallas import tpu as pltpu
from jax.experimental.pallas import tpu_sc as plsc

MXU = jnp.bfloat16
WIRE = jnp.bfloat16
EPS = 1e-6
NEG_INF = -1e30
LOG2E = 1.4426950408889634
N_DEV = 8
LANES = 128
ROW_TILE = 256
FFN_TILE = 512
ATT_TILE = 1024
SWA_BLOCK = 128
VMEM_LIMIT = 56 * 1024 * 1024

SWA_Q_HEADS, SWA_KV_HEADS, SWA_DIM = 16, 2, 64
MLA_HEADS, MLA_NOPE, MLA_ROPE, MLA_V = 4, 128, 64, 128
MEM_HEADS, MEM_DIM = 4, 128
ROPE_THETA = 10000.0
ADAM_LR, ADAM_B1, ADAM_B2, ADAM_EPS, ADAM_WD, ADAM_STEP = 0.001, 0.9, 0.999, 1e-08, 0.01, 10

C_QA, C_CQ, C_CKV, C_QM, C_KA, C_VA, C_KR, IN_PAD = 0, 1024, 1536, 2048, 2560, 2688, 2816, 2944


def _pcall(body, *, name, out_shape, in_specs, out_specs, grid=(), scratch=(), sem=None, after=None):
    params = pltpu.CompilerParams(dimension_semantics=sem, vmem_limit_bytes=VMEM_LIMIT)
    if after is not None:
        n_in, inner = len(in_specs), body

        def body(*refs):
            inner(*refs[:n_in], *refs[n_in + 1:])

        in_specs = list(in_specs) + [pl.BlockSpec(memory_space=pl.ANY)]
    call = pl.pallas_call(body, name=name, grid=grid, in_specs=in_specs, out_specs=out_specs,
                          out_shape=out_shape, scratch_shapes=list(scratch), compiler_params=params)
    return call if after is None else (lambda *ops: call(*ops, after))


def _sds(shape, dtype):
    return jax.ShapeDtypeStruct(tuple(shape), dtype)


def _dot(a, b):
    return jnp.dot(a.astype(MXU), b.astype(MXU), preferred_element_type=jnp.float32)


def _dot_nt(a, b):
    return lax.dot_general(a.astype(MXU), b.astype(MXU), (((1,), (1,)), ((), ())),
                           preferred_element_type=jnp.float32)


def _dot_tn(a, b):
    return lax.dot_general(a.astype(MXU), b.astype(MXU), (((0,), (0,)), ((), ())),
                           preferred_element_type=jnp.float32)


def _lo_mask(shape):
    return (lax.broadcasted_iota(jnp.int32, shape, len(shape) - 1) % LANES) < 64


def _norm_fwd(x, g, half=False):
    x2 = x * x
    if half:
        lo = _lo_mask(x.shape)
        s_lo = jnp.sum(jnp.where(lo, x2, 0.0), -1, keepdims=True)
        s_hi = jnp.sum(jnp.where(lo, 0.0, x2), -1, keepdims=True)
        r = jnp.where(lo, lax.rsqrt(s_lo / 64.0 + EPS), lax.rsqrt(s_hi / 64.0 + EPS))
    else:
        r = lax.rsqrt(jnp.mean(x2, -1, keepdims=True) + EPS)
    xn = x * r
    return xn * g, xn, r


def _norm_bwd(xn, r, g, dy, half=False):
    t = dy * g
    tx = t * xn
    if half:
        lo = _lo_mask(xn.shape)
        m_lo = jnp.sum(jnp.where(lo, tx, 0.0), -1, keepdims=True) / 64.0
        m_hi = jnp.sum(jnp.where(lo, 0.0, tx), -1, keepdims=True) / 64.0
        m = jnp.where(lo, m_lo, m_hi)
    else:
        m = jnp.mean(tx, -1, keepdims=True)
    dx = r * (t - xn * m)
    dg = jnp.sum(dy * xn, 0, keepdims=True)
    return dx, dg


def _swap32(x):
    lane = lax.broadcasted_iota(jnp.int32, x.shape, 1)
    return jnp.where((lane % 64) < 32, pltpu.roll(x, 96, 1), pltpu.roll(x, 32, 1))


def _rope(x, cos, sin):
    return x * cos + _swap32(x) * sin


def _rope_bwd(d, cos, sin):
    return d * cos + _swap32(d * sin)


def _my_coords():
    return lax.axis_index("x"), lax.axis_index("y"), lax.axis_index("c")


def _dev_index(px, py, pc):
    return 4 * px + 2 * py + pc


_FLIPS = [(0, 0, 1), (0, 1, 0), (0, 1, 1), (1, 0, 0), (1, 0, 1), (1, 1, 0), (1, 1, 1)]


def _flip(coords, f):
    return tuple((1 - v) if b else v for v, b in zip(coords, f))


def _all_gather(shards):
    n = len(shards)

    def body(*refs):
        ins, outs = refs[:n], refs[n:2 * n]
        send_sems, recv_sems, local_sems = refs[2 * n:]
        x, y, c = _my_coords()
        me, sibling = (x, y, c), (x, y, 1 - c)
        chips = [(1 - x, y), (x, 1 - y), (1 - x, 1 - y)]

        def copy(w, k, block, to, src=None):
            dst = outs[w].at[_dev_index(*block)]
            return pltpu.make_async_remote_copy(
                src_ref=dst if src is None else src, dst_ref=dst,
                send_sem=send_sems.at[w, k], recv_sem=recv_sems.at[w, k],
                device_id=to, device_id_type=pl.DeviceIdType.MESH)

        sends, locals_ = [], []
        for w in range(n):
            mine = pltpu.make_async_copy(ins[w], outs[w].at[_dev_index(*me)], local_sems.at[w])
            mine.start()
            locals_.append(mine)
            first = [copy(w, 0, me, sibling, src=ins[w])]
            first += [copy(w, 1 + j, me, (*chip, c), src=ins[w]) for j, chip in enumerate(chips)]
            for cp in first:
                cp.start()
            sends += first
        for w in range(n):
            for j, chip in enumerate(chips):
                copy(w, 1 + j, (*chip, c), me).wait_recv()
                fwd = copy(w, 4 + j, (*chip, c), sibling)
                fwd.start()
                sends.append(fwd)
        for w in range(n):
            copy(w, 0, sibling, me).wait_recv()
            for j, chip in enumerate(chips):
                copy(w, 4 + j, (*chip, 1 - c), me).wait_recv()
        for cp in sends:
            cp.wait_send()
        for mine in locals_:
            mine.wait()

    any_spec = pl.BlockSpec(memory_space=pl.ANY)
    return _pcall(
        body, name="all_gather_weights",
        out_shape=[_sds((N_DEV,) + s.shape, s.dtype) for s in shards],
        in_specs=[any_spec] * n, out_specs=[any_spec] * n,
        scratch=[pltpu.SemaphoreType.DMA((n, 7)), pltpu.SemaphoreType.DMA((n, 7)),
                 pltpu.SemaphoreType.DMA((n,))])(*shards)


def _wire_cost(arrays):
    nbytes = sum(a.size * a.dtype.itemsize for a in arrays)
    return pl.CostEstimate(flops=0, transcendentals=0, bytes_accessed=40 * nbytes)


def _all_gather_background(shards, collective_id, name):
    n = len(shards)
    src_refs = [jax.new_ref(s, memory_space=pltpu.MemorySpace.HBM) for s in shards]
    out_refs = [jax.empty_ref(_sds((N_DEV,) + s.shape, s.dtype), memory_space=pltpu.MemorySpace.HBM) for s in shards]

    @pl.kernel(mesh=plsc.ScalarSubcoreMesh(axis_name="seq", num_cores=1), name=name,
               scratch_types=(pltpu.SemaphoreType.DMA((n, 7)), pltpu.SemaphoreType.DMA((n, 7)),
                              pltpu.SemaphoreType.DMA((n,))),
               compiler_params=pltpu.CompilerParams(collective_id=collective_id))
    def launch(send_sems, recv_sems, local_sems):
        x, y, c = _my_coords()
        me, sibling = (x, y, c), (x, y, 1 - c)
        chips = [(1 - x, y), (x, 1 - y), (1 - x, 1 - y)]
        barrier = pltpu.get_barrier_semaphore()
        for peer in [sibling] + [(*chip, c) for chip in chips]:
            pl.semaphore_signal(barrier, inc=1, device_id=peer, device_id_type=pl.DeviceIdType.MESH)
        pl.semaphore_wait(barrier, 4)

        def copy(w, k, block, to, src=None):
            dst = out_refs[w].at[_dev_index(*block)]
            return pltpu.make_async_remote_copy(
                src_ref=dst if src is None else src, dst_ref=dst,
                send_sem=send_sems.at[w, k], recv_sem=recv_sems.at[w, k],
                device_id=to, device_id_type=pl.DeviceIdType.MESH)

        sends, locals_ = [], []
        for w in range(n):
            mine = pltpu.make_async_copy(src_refs[w], out_refs[w].at[_dev_index(*me)], local_sems.at[w])
            mine.start()
            locals_.append(mine)
            first = [copy(w, 0, me, sibling, src=src_refs[w])]
            first += [copy(w, 1 + j, me, (*chip, c), src=src_refs[w]) for j, chip in enumerate(chips)]
            for cp in first:
                cp.start()
            sends += first
        for w in range(n):
            for j, chip in enumerate(chips):
                copy(w, 1 + j, (*chip, c), me).wait_recv()
                fwd = copy(w, 4 + j, (*chip, c), sibling)
                fwd.start()
                sends.append(fwd)
        for w in range(n):
            copy(w, 0, sibling, me).wait_recv()
            for j, chip in enumerate(chips):
                copy(w, 4 + j, (*chip, 1 - c), me).wait_recv()
        for cp in sends:
            cp.wait_send()
        for mine in locals_:
            mine.wait()

    launch()
    return [r[...] for r in out_refs]


def _exchange_grads(grads):
    n = len(grads)

    def body(*refs):
        ins, outs = refs[:n], refs[n:2 * n]
        send_sems, recv_sems, local_sems = refs[2 * n:]
        me = _my_coords()
        my_idx = _dev_index(*me)
        sends, locals_ = [], []
        for w in range(n):
            mine = pltpu.make_async_copy(ins[w].at[my_idx], outs[w].at[my_idx], local_sems.at[w])
            mine.start()
            locals_.append(mine)
            for k, f in enumerate(_FLIPS):
                peer = _flip(me, f)
                cp = pltpu.make_async_remote_copy(
                    src_ref=ins[w].at[_dev_index(*peer)], dst_ref=outs[w].at[my_idx],
                    send_sem=send_sems.at[w, k], recv_sem=recv_sems.at[w, k],
                    device_id=peer, device_id_type=pl.DeviceIdType.MESH)
                cp.start()
                sends.append(cp)
        for w in range(n):
            for k, f in enumerate(_FLIPS):
                peer = _flip(me, f)
                slot = outs[w].at[_dev_index(*peer)]
                pltpu.make_async_remote_copy(
                    src_ref=slot, dst_ref=slot,
                    send_sem=send_sems.at[w, k], recv_sem=recv_sems.at[w, k],
                    device_id=peer, device_id_type=pl.DeviceIdType.MESH).wait_recv()
        for cp in sends:
            cp.wait_send()
        for mine in locals_:
            mine.wait()

    any_spec = pl.BlockSpec(memory_space=pl.ANY)
    return _pcall(
        body, name="exchange_grads",
        out_shape=[_sds(g.shape, g.dtype) for g in grads],
        in_specs=[any_spec] * n, out_specs=[any_spec] * n,
        scratch=[pltpu.SemaphoreType.DMA((n, 7)), pltpu.SemaphoreType.DMA((n, 7)),
                 pltpu.SemaphoreType.DMA((n,))])(*grads)


def _exchange_grads_background(grads, collective_id, name):
    n = len(grads)
    src_refs = [jax.new_ref(g, memory_space=pltpu.MemorySpace.HBM) for g in grads]
    out_refs = [jax.empty_ref(_sds(g.shape, g.dtype), memory_space=pltpu.MemorySpace.HBM) for g in grads]

    @pl.kernel(mesh=plsc.ScalarSubcoreMesh(axis_name="seq", num_cores=1), name=name,
               scratch_types=(pltpu.SemaphoreType.DMA((n, 7)), pltpu.SemaphoreType.DMA((n, 7)),
                              pltpu.SemaphoreType.DMA((n,))),
               cost_estimate=_wire_cost(grads),
               compiler_params=pltpu.CompilerParams(collective_id=collective_id))
    def launch(send_sems, recv_sems, local_sems):
        me = _my_coords()
        my_idx = _dev_index(*me)
        peers = [_flip(me, f) for f in _FLIPS]
        barrier = pltpu.get_barrier_semaphore()
        for peer in peers:
            pl.semaphore_signal(barrier, inc=1, device_id=peer, device_id_type=pl.DeviceIdType.MESH)
        pl.semaphore_wait(barrier, len(peers))
        sends, locals_ = [], []
        for w in range(n):
            mine = pltpu.make_async_copy(src_refs[w].at[my_idx], out_refs[w].at[my_idx], local_sems.at[w])
            mine.start()
            locals_.append(mine)
            for k, peer in enumerate(peers):
                cp = pltpu.make_async_remote_copy(
                    src_ref=src_refs[w].at[_dev_index(*peer)], dst_ref=out_refs[w].at[my_idx],
                    send_sem=send_sems.at[w, k], recv_sem=recv_sems.at[w, k],
                    device_id=peer, device_id_type=pl.DeviceIdType.MESH)
                cp.start()
                sends.append(cp)
        for w in range(n):
            for k, peer in enumerate(peers):
                slot = out_refs[w].at[_dev_index(*peer)]
                pltpu.make_async_remote_copy(
                    src_ref=slot, dst_ref=slot, send_sem=send_sems.at[w, k], recv_sem=recv_sems.at[w, k],
                    device_id=peer, device_id_type=pl.DeviceIdType.MESH).wait_recv()
        for cp in sends:
            cp.wait_send()
        for mine in locals_:
            mine.wait()

    launch()
    return [r[...] for r in out_refs]


def _to_wire(parts, after, name):
    n = len(parts)
    rows, cols = parts[0].shape
    tr = rows // 2 if rows % 32 == 0 else rows

    def body(*refs):
        for k in range(n):
            refs[n][k] = refs[k][...].astype(WIRE)

    blk = pl.BlockSpec((tr, cols), lambda i: (i, 0))
    return _pcall(
        body, name=name, grid=(rows // tr,), out_shape=_sds((n, rows, cols), WIRE),
        in_specs=[blk] * n, out_specs=pl.BlockSpec((n, tr, cols), lambda i: (0, i, 0)),
        sem=("parallel",), after=after)(*parts)


def _adam_math(w, g, m, v):
    m = ADAM_B1 * m + (1.0 - ADAM_B1) * g
    v = ADAM_B2 * v + (1.0 - ADAM_B2) * (g * g)
    m_hat = m / (1.0 - ADAM_B1 ** ADAM_STEP)
    v_hat = v / (1.0 - ADAM_B2 ** ADAM_STEP)
    delta = -ADAM_LR * (m_hat / (jnp.sqrt(v_hat) + ADAM_EPS) + ADAM_WD * w)
    return delta, m, v


def _small_layout(sizes):
    row0, r = [], 0
    for n in sizes:
        row0.append(r)
        r += -(-n // LANES)
    return row0, r, -(-(r + 1) // 8) * 8


def _small_pieces(n):
    return [(k, min(LANES, n - LANES * k)) for k in range(-(-n // LANES))]


def _small_fill(pack, slot, srcs, sizes, row0, rows):
    pack[slot] = jnp.zeros((rows, LANES), jnp.float32)
    for p, n in enumerate(sizes):
        val = srcs[p][...]
        if val.shape[-1] == LANES and n == 64:
            pack[slot, row0[p]:row0[p] + 1, :] = val + pltpu.roll(val, 64, 1)
            continue
        for k, width in _small_pieces(n):
            pack[slot, row0[p] + k:row0[p] + k + 1, 0:width] = srcs[p][:, LANES * k:LANES * k + width]


def _small_pack(grads, loss_tile, sizes):
    n_par = len(sizes)
    row0, loss_row, rows = _small_layout(sizes)

    def body(*refs):
        g_refs, loss_in, out_ref = refs[:n_par], refs[n_par], refs[n_par + 1]
        _small_fill(out_ref, 0, g_refs, sizes, row0, rows)
        out_ref[0, loss_row:loss_row + 1, :] = loss_in[0:1, :]

    vm = pl.BlockSpec(memory_space=pltpu.VMEM)
    return _pcall(
        body, name="small_pack", out_shape=_sds((1, rows, LANES), jnp.float32),
        in_specs=[vm] * (n_par + 1), out_specs=vm)(*grads, loss_tile)


def _small_adam(packs, ws, ms, vs, after):
    sizes = [w.shape[-1] for w in ws]
    n_par = len(ws)
    row0, loss_row, rows = _small_layout(sizes)

    def body(*refs):
        g_ref = refs[0]
        w_refs, m_refs, v_refs = (refs[1 + k * n_par: 1 + (k + 1) * n_par] for k in range(3))
        loss_out = refs[3 * n_par + 1]
        out_refs = refs[3 * n_par + 2: 7 * n_par + 2]
        pack, res = refs[7 * n_par + 2:]
        for slot, srcs in enumerate((w_refs, m_refs, v_refs)):
            _small_fill(pack, slot, srcs, sizes, row0, rows)
        g = g_ref[0]
        for dev in range(1, N_DEV):
            g = g + g_ref[dev]
        delta, m, v = _adam_math(pack[0], g, pack[1], pack[2])
        res[0], res[1], res[2], res[3] = g, delta, m, v
        loss_out[...] = res[0, loss_row:loss_row + 1, 0:1]
        for p, n in enumerate(sizes):
            for kind in range(4):
                for k, width in _small_pieces(n):
                    out_refs[4 * p + kind][:, LANES * k:LANES * k + width] = (
                        res[kind, row0[p] + k:row0[p] + k + 1, 0:width])

    vm = pl.BlockSpec(memory_space=pltpu.VMEM)
    out_shape = [_sds((1, 1), jnp.float32)]
    for n in sizes:
        out_shape += [_sds((1, n), jnp.float32)] * 4
    outs = _pcall(
        body, name="small_adam", out_shape=out_shape,
        in_specs=[vm] * (3 * n_par + 1), out_specs=[vm] * len(out_shape),
        scratch=[pltpu.VMEM((3, rows, LANES), jnp.float32), pltpu.VMEM((4, rows, LANES), jnp.float32)],
        after=after)(packs, *ws, *ms, *vs)
    return outs[0], [outs[1 + 4 * p: 5 + 4 * p] for p in range(n_par)]


def _adam_big(recv, w, m, v, name, after=None, which=None):
    rows, cols = recv.shape[-2:]
    row_tiles = [t for t in range(16, rows + 1, 16) if rows % t == 0 and t * cols <= 400 * 1024]
    tr, tc = (max(row_tiles), cols) if row_tiles else (rows, 512 if cols % 512 == 0 else cols)

    def body(r_ref, w_ref, m_ref, v_ref, g_ref, d_ref, mo_ref, vo_ref):
        g = r_ref[0].astype(jnp.float32)
        for d in range(1, N_DEV):
            g = g + r_ref[d].astype(jnp.float32)
        delta, mn, vn = _adam_math(w_ref[...], g, m_ref[...], v_ref[...])
        g_ref[...] = g
        d_ref[...] = delta
        mo_ref[...] = mn
        vo_ref[...] = vn

    blk = pl.BlockSpec((tr, tc), lambda i, j: (i, j))
    if which is None:
        r_spec = pl.BlockSpec((N_DEV, tr, tc), lambda i, j: (0, i, j))
    else:
        r_spec = pl.BlockSpec((N_DEV, None, tr, tc), lambda i, j: (0, which, i, j))
    return _pcall(
        body, name=name, grid=(rows // tr, cols // tc),
        out_shape=[_sds((rows, cols), jnp.float32)] * 4,
        in_specs=[r_spec, blk, blk, blk],
        out_specs=[blk] * 4, sem=("parallel", "parallel"), after=after)(recv, w, m, v)


def _mm(a, b, *, ta=False, tb=False, out_dtype, tm, tk, name):
    (kdim, mdim) = a.shape if ta else a.shape[::-1]
    ndim = b.shape[0] if tb else b.shape[1]
    tm, tk = min(tm, mdim), min(tk, kdim)
    nk = kdim // tk

    def body(a_ref, b_ref, o_ref, acc):
        k = pl.program_id(1)
        if ta:
            part = _dot_tn(a_ref[...], b_ref[...])
        elif tb:
            part = _dot_nt(a_ref[...], b_ref[...])
        else:
            part = _dot(a_ref[...], b_ref[...])

        @pl.when(k == 0)
        def _():
            acc[...] = part

        @pl.when(k > 0)
        def _():
            acc[...] += part

        @pl.when(k == nk - 1)
        def _():
            o_ref[...] = acc[...].astype(o_ref.dtype)

    a_spec = pl.BlockSpec((tk, tm), lambda i, k: (k, i)) if ta else pl.BlockSpec((tm, tk), lambda i, k: (i, k))
    b_spec = pl.BlockSpec((ndim, tk), lambda i, k: (0, k)) if tb else pl.BlockSpec((tk, ndim), lambda i, k: (k, 0))
    return _pcall(
        body, name=name, grid=(mdim // tm, nk), out_shape=_sds((mdim, ndim), out_dtype),
        in_specs=[a_spec, b_spec], out_specs=pl.BlockSpec((tm, ndim), lambda i, k: (i, 0)),
        scratch=[pltpu.VMEM((tm, ndim), jnp.float32)], sem=("parallel", "arbitrary"))(a, b)


def _ref_col_pieces(start, stop):
    ref_starts = [0, 1024, 1152, 1280, 1792, 2304, 2368, 2880]
    perm_starts = [C_QA, C_KA, C_VA, C_CQ, C_CKV, C_KR, C_QM]
    out = []
    for p in range(7):
        lo, hi = max(start, ref_starts[p]), min(stop, ref_starts[p + 1])
        if lo < hi:
            out.append((lo - start, perm_starts[p] + lo - ref_starts[p], hi - lo))
    return out


def _dw_in(hn, d_proj, n_shard, after):
    s, d = hn.shape
    n = sum(p.shape[1] for p in d_proj)
    n_pc = len(d_proj)
    tm, tk = min(512, d), min(1024, s)
    nk = s // tk

    def body(a_ref, *refs):
        b_refs, (o_ref, acc) = refs[:n_pc], refs[n_pc:]
        k = pl.program_id(1)
        part = _dot_tn(a_ref[...], jnp.concatenate([r[...] for r in b_refs], axis=1))

        @pl.when(k == 0)
        def _():
            acc[...] = part

        @pl.when(k > 0)
        def _():
            acc[...] += part

        @pl.when(k == nk - 1)
        def _():
            t = acc[...].T
            for j in range(N_DEV):
                rows = [t[src:src + width] for _, src, width in _ref_col_pieces(j * n_shard, (j + 1) * n_shard)]
                o_ref[j] = jnp.concatenate(rows, axis=0).astype(o_ref.dtype)

    return _pcall(
        body, name="dw_in", grid=(d // tm, nk), out_shape=_sds((N_DEV, n_shard, d), WIRE),
        in_specs=[pl.BlockSpec((tk, tm), lambda i, k: (k, i))]
        + [pl.BlockSpec((tk, p.shape[1]), lambda i, k: (k, 0)) for p in d_proj],
        out_specs=pl.BlockSpec((N_DEV, n_shard, tm), lambda i, k: (0, 0, i)),
        scratch=[pltpu.VMEM((tm, n), jnp.float32)], sem=("parallel", "arbitrary"), after=after)(hn, *d_proj)


def _in_proj(x, g, w):
    s, d = x.shape
    n = w.shape[0]
    tm = min(2 * ROW_TILE, s)

    def body(x_ref, g_ref, w_ref, p_ref, hn_ref):
        hn, _, _ = _norm_fwd(x_ref[...], g_ref[...])
        hn_ref[...] = hn.astype(hn_ref.dtype)
        p_ref[...] = _dot_nt(hn, w_ref[...])

    return _pcall(
        body, name="in_proj", grid=(s // tm,),
        out_shape=[_sds((s, n), jnp.float32), _sds((s, d), MXU)],
        in_specs=[pl.BlockSpec((tm, d), lambda i: (i, 0)), pl.BlockSpec((1, d), lambda i: (0, 0)),
                  pl.BlockSpec((n, d), lambda i: (0, 0), pipeline_mode=pl.Buffered(1))],
        out_specs=[pl.BlockSpec((tm, n), lambda i: (i, 0)), pl.BlockSpec((tm, d), lambda i: (i, 0))],
        sem=("parallel",))(x, g, w)


def _norm_rows(x, g):
    s, d = x.shape
    tm = min(FFN_TILE, s)

    def body(x_ref, g_ref, hn_ref):
        hn, _, _ = _norm_fwd(x_ref[...], g_ref[...])
        hn_ref[...] = hn.astype(hn_ref.dtype)

    row = pl.BlockSpec((tm, d), lambda i: (i, 0))
    return _pcall(body, name="norm_rows", grid=(s // tm,), out_shape=_sds((s, d), MXU),
                  in_specs=[row, pl.BlockSpec((1, d), lambda i: (0, 0))], out_specs=row, sem=("parallel",))(x, g)


def _mla_prep(proj, cos, sin, g_cq, g_ckv, w_uq, w_ukv, g_qn, g_qr, g_kn, g_kr):
    s = proj.shape[0]
    tm = min(2 * ROW_TILE, s)
    nh = MLA_HEADS

    def body(cq_ref, ckv_ref, kr_ref, cos_ref, sin_ref, gcq_ref, gckv_ref, wuq_ref, wukv_ref,
             gqn_ref, gqr_ref, gkn_ref, gkr_ref,
             qc_ref, kc_ref, v_ref, qb_ref, kvb_ref, cqn_ref, ckvn_ref):
        cos_t, sin_t = cos_ref[...], sin_ref[...]
        lo = _lo_mask((tm, LANES))
        cqn, _, _ = _norm_fwd(cq_ref[...], gcq_ref[...])
        cqn_ref[...] = cqn.astype(cqn_ref.dtype)
        qb = _dot_nt(cqn, wuq_ref[...])
        qb_ref[...] = qb
        ckvn, _, _ = _norm_fwd(ckv_ref[...], gckv_ref[...])
        ckvn_ref[...] = ckvn.astype(ckvn_ref.dtype)
        w_ukv_full = jnp.concatenate([wukv_ref[dev] for dev in range(N_DEV)], axis=1)
        kvb = _dot(ckvn, w_ukv_full)
        kvb_ref[...] = kvb
        kr, _, _ = _norm_fwd(kr_ref[...], gkr_ref[...], half=True)
        kr = _rope(kr, cos_t, sin_t)
        kr2 = jnp.where(lo, kr, pltpu.roll(kr, 64, 1))
        ropes = []
        for j in range(nh // 2):
            xr = qb[:, nh * MLA_NOPE + LANES * j: nh * MLA_NOPE + LANES * (j + 1)]
            qr, _, _ = _norm_fwd(xr, gqr_ref[...], half=True)
            ropes.append(_rope(qr, cos_t, sin_t))
        for h in range(nh):
            qn, _, _ = _norm_fwd(qb[:, MLA_NOPE * h: MLA_NOPE * (h + 1)], gqn_ref[...])
            mask = lo if h % 2 == 0 else jnp.logical_not(lo)
            qr = jnp.where(mask, ropes[h // 2], 0.0)
            qc_ref[h] = jnp.concatenate([qn, qr], axis=1).astype(qc_ref.dtype)
            kn, _, _ = _norm_fwd(kvb[:, 256 * h: 256 * h + MLA_NOPE], gkn_ref[...])
            kc_ref[h] = jnp.concatenate([kn, kr2], axis=1).astype(kc_ref.dtype)
            v_ref[h] = kvb[:, 256 * h + MLA_NOPE: 256 * (h + 1)].astype(v_ref.dtype)

    def col(width, start):
        return pl.BlockSpec((tm, width), lambda i: (i, start // width))

    def full(shape):
        return pl.BlockSpec(shape, lambda i: (0,) * len(shape))

    def row(width):
        return pl.BlockSpec((tm, width), lambda i: (i, 0))

    def heads(width):
        return pl.BlockSpec((nh, tm, width), lambda i: (0, i, 0))

    return _pcall(
        body, name="mla_prep", grid=(s // tm,),
        out_shape=[_sds((nh, s, 256), MXU), _sds((nh, s, 256), MXU), _sds((nh, s, MLA_V), MXU),
                   _sds((s, 768), jnp.float32), _sds((s, 1024), jnp.float32),
                   _sds((s, 512), MXU), _sds((s, 512), MXU)],
        in_specs=[col(512, C_CQ), col(512, C_CKV), col(LANES, C_KR), row(LANES), row(LANES),
                  full((1, 512)), full((1, 512)), full((768, 512)), full((N_DEV, 512, LANES)),
                  full((1, LANES)), full((1, LANES)), full((1, LANES)), full((1, LANES))],
        out_specs=[heads(256), heads(256), heads(MLA_V), row(768), row(1024), row(512), row(512)],
        sem=("parallel",))(proj, proj, proj, cos, sin, g_cq, g_ckv, w_uq, w_ukv, g_qn, g_qr, g_kn, g_kr)


def _tri_rows(p, nb):
    i = sum(jnp.where(p >= (r * (r + 1)) // 2, 1, 0) for r in range(1, nb))
    return i, p - (i * (i + 1)) // 2


def _tri_cols(p, nb):
    j = sum(jnp.where(p >= r * nb - (r * (r - 1)) // 2, 1, 0) for r in range(1, nb))
    return j, j + p - (j * nb - (j * (j - 1)) // 2)


def _mla_fwd(qc, kc, v):
    nh, s, _ = qc.shape
    t = min(ATT_TILE, s)
    nb = s // t
    scale = (MLA_NOPE + MLA_ROPE) ** -0.5

    def body(q_ref, k_ref, v_ref, y_ref, lse_ref, m_sc, l_sc, acc):
        qi, ki = _tri_rows(pl.program_id(1), nb)

        @pl.when(ki == 0)
        def _():
            m_sc[...] = jnp.full_like(m_sc, NEG_INF)
            l_sc[...] = jnp.zeros_like(l_sc)
            acc[...] = jnp.zeros_like(acc)

        def step(diagonal):
            rc = t // 4 if diagonal else t
            for c in range(t // rc):
                rows = slice(rc * c, rc * (c + 1))
                keys = slice(0, rc * (c + 1))
                sc = _dot_nt(q_ref[0, rows, :], k_ref[0, keys, :]) * (scale * LOG2E)
                if diagonal:
                    r_i = lax.broadcasted_iota(jnp.int32, sc.shape, 0) + rc * c
                    c_i = lax.broadcasted_iota(jnp.int32, sc.shape, 1)
                    sc = jnp.where(c_i <= r_i, sc, NEG_INF)
                m_old = m_sc[rows, :]
                m_new = jnp.maximum(m_old, jnp.max(sc, -1, keepdims=True))
                alpha = jnp.exp2(m_old - m_new)
                p = jnp.exp2(sc - m_new)
                l_sc[rows, :] = alpha * l_sc[rows, :] + jnp.sum(p, -1, keepdims=True)
                acc[rows, :] = alpha * acc[rows, :] + _dot(p, v_ref[0, keys, :])
                m_sc[rows, :] = m_new

        @pl.when(ki < qi)
        def _():
            step(False)

        @pl.when(ki == qi)
        def _():
            step(True)

        @pl.when(ki == qi)
        def _():
            y_ref[...] = acc[...] / l_sc[...]
            lse_ref[0] = m_sc[...] + jnp.log2(l_sc[...])

    return _pcall(
        body, name="mla_fwd", grid=(nh, (nb * (nb + 1)) // 2),
        out_shape=[_sds((s, nh * MLA_V), jnp.float32), _sds((nh, s, 1), jnp.float32)],
        in_specs=[pl.BlockSpec((1, t, 256), lambda h, p: (h, _tri_rows(p, nb)[0], 0)),
                  pl.BlockSpec((1, t, 256), lambda h, p: (h, _tri_rows(p, nb)[1], 0)),
                  pl.BlockSpec((1, t, MLA_V), lambda h, p: (h, _tri_rows(p, nb)[1], 0))],
        out_specs=[pl.BlockSpec((t, MLA_V), lambda h, p: (_tri_rows(p, nb)[0], h)),
                   pl.BlockSpec((1, t, 1), lambda h, p: (h, _tri_rows(p, nb)[0], 0))],
        scratch=[pltpu.VMEM((t, 1), jnp.float32), pltpu.VMEM((t, 1), jnp.float32),
                 pltpu.VMEM((t, MLA_V), jnp.float32)],
        sem=("parallel", "arbitrary"))(qc, kc, v)


def _memkv_prep(mem, g_mem, w_mkv, g_mk):
    ml, d = mem.shape
    hw = MEM_HEADS * MEM_DIM

    def body(mem_ref, g_ref, w_ref, gk_ref, k_ref, v_ref, kv_ref, mn_ref):
        mn, _, _ = _norm_fwd(mem_ref[...], g_ref[...])
        mn_ref[...] = mn.astype(mn_ref.dtype)
        kv = _dot(mn, w_ref[...])
        kv_ref[...] = kv
        for h in range(MEM_HEADS):
            kn, _, _ = _norm_fwd(kv[:, MEM_DIM * h: MEM_DIM * (h + 1)], gk_ref[...])
            k_ref[:, MEM_DIM * h: MEM_DIM * (h + 1)] = kn.astype(k_ref.dtype)
        v_ref[...] = kv[:, hw:].astype(v_ref.dtype)

    vm = pl.BlockSpec(memory_space=pltpu.VMEM)
    return _pcall(
        body, name="memkv_prep",
        out_shape=[_sds((ml, hw), MXU), _sds((ml, hw), MXU), _sds((ml, 2 * hw), jnp.float32), _sds((ml, d), MXU)],
        in_specs=[vm] * 4, out_specs=[vm] * 4)(mem, g_mem, w_mkv, g_mk)


def _mem_fwd(proj, g_mq, km, vmm):
    s = proj.shape[0]
    ml, hw = km.shape
    tm = min(FFN_TILE, s)
    scale = MEM_DIM ** -0.5

    def body(q_ref, g_ref, k_ref, v_ref, y_ref, lse_ref):
        col = lax.broadcasted_iota(jnp.int32, (tm, MEM_HEADS), 1)
        lse_t = jnp.zeros((tm, MEM_HEADS), jnp.float32)
        for h in range(MEM_HEADS):
            sl = slice(MEM_DIM * h, MEM_DIM * (h + 1))
            qn, _, _ = _norm_fwd(q_ref[:, sl], g_ref[...])
            sc = _dot_nt(qn, k_ref[:, sl]) * scale
            m = jnp.max(sc, -1, keepdims=True)
            p = jnp.exp(sc - m)
            l = jnp.sum(p, -1, keepdims=True)
            y_ref[:, sl] = _dot(p, v_ref[:, sl]) / l
            lse_t = jnp.where(col == h, m + jnp.log(l), lse_t)
        lse_ref[...] = lse_t

    return _pcall(
        body, name="mem_fwd", grid=(s // tm,),
        out_shape=[_sds((s, hw), jnp.float32), _sds((s, MEM_HEADS), jnp.float32)],
        in_specs=[pl.BlockSpec((tm, hw), lambda i: (i, C_QM // hw)), pl.BlockSpec((1, MEM_DIM), lambda i: (0, 0)),
                  pl.BlockSpec((ml, hw), lambda i: (0, 0)), pl.BlockSpec((ml, hw), lambda i: (0, 0))],
        out_specs=[pl.BlockSpec((tm, hw), lambda i: (i, 0)), pl.BlockSpec((tm, MEM_HEADS), lambda i: (i, 0))],
        sem=("parallel",))(proj, g_mq, km, vmm)


def _alibi_slope(h):
    return float(2.0 ** (-8.0 * (h + 1) / SWA_Q_HEADS))


def _swa_common(n, kp, kc, vp, vc, pq, pkp, pkc, gk):
    b = SWA_BLOCK
    k_raw = jnp.concatenate([kp, kc], axis=0)
    kn, kxn, kr = _norm_fwd(k_raw, gk, half=True)
    v = jnp.concatenate([vp, vc], axis=0)
    dist = jnp.abs(pq - jnp.concatenate([pkp, pkc], axis=1))
    r_i = lax.broadcasted_iota(jnp.int32, (b, 2 * b), 0)
    c_i = lax.broadcasted_iota(jnp.int32, (b, 2 * b), 1)
    valid = (c_i > r_i) & (c_i <= r_i + b) & (c_i >= jnp.where(n > 0, 0, b))
    bias = jnp.where(valid, -dist, NEG_INF)
    return kn, v, bias


def _swa_folded(n, kp, kc, pq, pkp, pkc, gk):
    b = SWA_BLOCK
    kn_p, _, _ = _norm_fwd(kp, gk, half=True)
    kn_c, _, _ = _norm_fwd(kc, gk, half=True)
    r_i = lax.broadcasted_iota(jnp.int32, (b, b), 0)
    c_i = lax.broadcasted_iota(jnp.int32, (b, b), 1)
    upper = c_i > r_i
    bias_prev = jnp.where(n > 0, 0.0, NEG_INF) - jnp.abs(pq - pkp)
    bias = jnp.where(upper, bias_prev, -jnp.abs(pq - pkc))
    return kn_p, kn_c, bias, upper


def _swa_specs(s):
    b = SWA_BLOCK
    prev = lambda n: jnp.maximum(n - 1, 0)
    return [
        pl.BlockSpec((b, 1024), lambda n: (n, C_QA // 1024)),
        pl.BlockSpec((b, LANES), lambda n: (prev(n), C_KA // LANES)),
        pl.BlockSpec((b, LANES), lambda n: (n, C_KA // LANES)),
        pl.BlockSpec((b, LANES), lambda n: (prev(n), C_VA // LANES)),
        pl.BlockSpec((b, LANES), lambda n: (n, C_VA // LANES)),
        pl.BlockSpec((b, 1), lambda n: (n, 0)),
        pl.BlockSpec((1, b), lambda n: (0, prev(n))),
        pl.BlockSpec((1, b), lambda n: (0, n)),
        pl.BlockSpec((1, LANES), lambda n: (0, 0)),
        pl.BlockSpec((1, LANES), lambda n: (0, 0)),
        pl.BlockSpec(memory_space=pltpu.SMEM),
    ]


def _swa_fwd(proj, posc, posr, gq, gk, sinks):
    s = proj.shape[0]
    b = SWA_BLOCK
    scale = SWA_DIM ** -0.5

    def body(q_ref, kp_ref, kc_ref, vp_ref, vc_ref, pq_ref, pkp_ref, pkc_ref, gq_ref, gk_ref, sink_ref,
             y_ref, lse_ref):
        n = pl.program_id(0)
        kn_p, kn_c, bias, upper = _swa_folded(n, kp_ref[...], kc_ref[...], pq_ref[...], pkp_ref[...], pkc_ref[...],
                                              gk_ref[...])
        v_p, v_c = vp_ref[...], vc_ref[...]
        lo = _lo_mask((b, LANES))
        col = lax.broadcasted_iota(jnp.int32, (b, SWA_Q_HEADS), 1)
        lse_t = jnp.zeros((b, SWA_Q_HEADS), jnp.float32)
        hpg = SWA_Q_HEADS // SWA_KV_HEADS
        for g in range(SWA_KV_HEADS):
            heads = range(hpg * g, hpg * (g + 1))
            kvmask = lo if g == 0 else jnp.logical_not(lo)
            qs = []
            for j in range(hpg // 2 * g, hpg // 2 * (g + 1)):
                qn, _, _ = _norm_fwd(q_ref[:, LANES * j: LANES * (j + 1)], gq_ref[...], half=True)
                qn = qn * scale
                qsw = pltpu.roll(qn, 64, 1)
                qs += [jnp.where(kvmask, qn if e == g else qsw, 0.0) for e in range(2)]
            q_st = jnp.concatenate(qs, axis=0).astype(MXU)
            sp_st, sc_st = _dot_nt(q_st, kn_p), _dot_nt(q_st, kn_c)
            pus, pls, ls = [], [], []
            for i, h in enumerate(heads):
                rows = slice(b * i, b * (i + 1))
                sc = jnp.where(upper, sp_st[rows], sc_st[rows]) + _alibi_slope(h) * bias
                sk = sink_ref[h]
                m = jnp.maximum(jnp.max(sc, -1, keepdims=True), sk)
                p = jnp.exp(sc - m)
                l = jnp.sum(p, -1, keepdims=True) + jnp.exp(sk - m)
                pus.append(jnp.where(upper, p, 0.0).astype(MXU))
                pls.append(jnp.where(upper, 0.0, p).astype(MXU))
                ls.append(l)
                lse_t = jnp.where(col == h, m + jnp.log(l), lse_t)
            o_st = _dot(jnp.concatenate(pus, axis=0), v_p) + _dot(jnp.concatenate(pls, axis=0), v_c)
            for j in range(hpg // 2 * g, hpg // 2 * (g + 1)):
                halves = []
                for e in range(2):
                    i = 2 * j + e - hpg * g
                    o_h = o_st[b * i: b * (i + 1)] / ls[i]
                    halves.append(o_h if e == g else pltpu.roll(o_h, 64, 1))
                y_ref[:, LANES * j: LANES * (j + 1)] = jnp.where(lo, halves[0], halves[1])
        lse_ref[...] = lse_t

    return _pcall(
        body, name="swa_fwd", grid=(s // b,),
        out_shape=[_sds((s, 1024), jnp.float32), _sds((s, SWA_Q_HEADS), jnp.float32)],
        in_specs=_swa_specs(s),
        out_specs=[pl.BlockSpec((b, 1024), lambda n: (n, 0)), pl.BlockSpec((b, SWA_Q_HEADS), lambda n: (n, 0))],
        sem=("parallel",))(proj, proj, proj, proj, proj, posc, posr, posr, gq, gk, sinks)


def _out_proj(y_a, y_b, y_m, x, w_out, g_ffn):
    s, d = x.shape
    tm = min(2 * ROW_TILE, s)

    def body(ya_ref, yb_ref, ym_ref, x_ref, w_ref, g_ref, h1_ref, fn_ref):
        y = jnp.concatenate([ya_ref[...].astype(MXU), yb_ref[...].astype(MXU), ym_ref[...].astype(MXU)], axis=1)
        h1 = x_ref[...] + _dot(y, w_ref[...])
        h1_ref[...] = h1
        fn, _, _ = _norm_fwd(h1, g_ref[...])
        fn_ref[...] = fn.astype(fn_ref.dtype)

    def row(width):
        return pl.BlockSpec((tm, width), lambda i: (i, 0))

    return _pcall(
        body, name="out_proj", grid=(s // tm,),
        out_shape=[_sds((s, d), jnp.float32), _sds((s, d), MXU)],
        in_specs=[row(1024), row(512), row(512), row(d),
                  pl.BlockSpec(w_out.shape, lambda i: (0, 0), pipeline_mode=pl.Buffered(1)),
                  pl.BlockSpec((1, d), lambda i: (0, 0))],
        out_specs=[row(d), row(d)], sem=("parallel",))(y_a, y_b, y_m, x, w_out, g_ffn)


def _ffn_gu(fn, w_gu):
    s, d = fn.shape
    f = w_gu.shape[2]
    tm = min(2 * FFN_TILE, s)

    def body(fn_ref, w_ref, gu_ref, act_ref):
        x = fn_ref[...]
        g = _dot_nt(x, w_ref[0, 0])
        u = _dot_nt(x, w_ref[0, 1])
        gu_ref[0, 0] = g
        gu_ref[0, 1] = u
        act_ref[0] = (g * jax.nn.sigmoid(g) * u).astype(act_ref.dtype)

    return _pcall(
        body, name="ffn_gate_up", grid=(N_DEV, s // tm),
        out_shape=[_sds((N_DEV, 2, s, f), jnp.float32), _sds((N_DEV, s, f), MXU)],
        in_specs=[pl.BlockSpec((tm, d), lambda j, i: (i, 0)),
                  pl.BlockSpec((1, 2, f, d), lambda j, i: (j, 0, 0, 0))],
        out_specs=[pl.BlockSpec((1, 2, tm, f), lambda j, i: (j, 0, i, 0)),
                   pl.BlockSpec((1, tm, f), lambda j, i: (j, i, 0))],
        sem=("parallel", "parallel"))(fn, w_gu)


def _ffn_down(act, w_d, h1, target):
    _, s, f = act.shape
    d = h1.shape[1]
    tm = min(FFN_TILE, s)

    def body(a_ref, w_ref, h1_ref, t_ref, dout_ref, doutb_ref, loss_ref, acc):
        i, j = pl.program_id(0), pl.program_id(1)
        part = _dot(a_ref[0], w_ref[0]) + _dot(a_ref[1], w_ref[1])

        @pl.when(j == 0)
        def _():
            acc[...] = h1_ref[...] + part

        @pl.when(j > 0)
        def _():
            acc[...] += part

        @pl.when((i == 0) & (j == 0))
        def _():
            loss_ref[...] = jnp.zeros_like(loss_ref)

        @pl.when(j == N_DEV // 2 - 1)
        def _():
            diff = acc[...] - t_ref[...]
            dout_ref[...] = diff / d
            doutb_ref[...] = (diff / d).astype(doutb_ref.dtype)
            loss_ref[...] += 0.5 * jnp.sum(jnp.sum(diff * diff, -1, keepdims=True) / d)

    row = pl.BlockSpec((tm, d), lambda i, j: (i, 0))
    return _pcall(
        body, name="ffn_down", grid=(s // tm, N_DEV // 2),
        out_shape=[_sds((s, d), jnp.float32), _sds((s, d), MXU), _sds((8, LANES), jnp.float32)],
        in_specs=[pl.BlockSpec((2, tm, f), lambda i, j: (j, i, 0)), pl.BlockSpec((2, f, d), lambda i, j: (j, 0, 0)),
                  row, row],
        out_specs=[row, row, pl.BlockSpec((8, LANES), lambda i, j: (0, 0))],
        scratch=[pltpu.VMEM((tm, d), jnp.float32)], sem=("arbitrary", "arbitrary"))(act, w_d, h1, target)


def _ffn_bwd_act(dout, w_d, gu):
    s, d = dout.shape
    f = w_d.shape[1]
    tm = min(2 * FFN_TILE, s)
    ni = s // tm

    def body(do_ref, w_ref, gu_ref, dgu_ref, dw_ref, acc):
        i = pl.program_id(1)
        do = do_ref[...]
        d_act = _dot_nt(do, w_ref[0])
        g, u = gu_ref[0, 0], gu_ref[0, 1]
        sig = jax.nn.sigmoid(g)
        silu = g * sig
        dgu_ref[0, 0] = (d_act * u * (sig * (1.0 + g * (1.0 - sig)))).astype(dgu_ref.dtype)
        dgu_ref[0, 1] = (d_act * silu).astype(dgu_ref.dtype)
        part = _dot_tn(silu * u, do)

        @pl.when(i == 0)
        def _():
            acc[...] = part

        @pl.when(i > 0)
        def _():
            acc[...] += part

        @pl.when(i == ni - 1)
        def _():
            dw_ref[0] = acc[...].astype(dw_ref.dtype)

    return _pcall(
        body, name="ffn_bwd_act", grid=(N_DEV, ni),
        out_shape=[_sds((N_DEV, 2, s, f), MXU), _sds((N_DEV, f, d), WIRE)],
        in_specs=[pl.BlockSpec((tm, d), lambda j, i: (i, 0)), pl.BlockSpec((1, f, d), lambda j, i: (j, 0, 0)),
                  pl.BlockSpec((1, 2, tm, f), lambda j, i: (j, 0, i, 0))],
        out_specs=[pl.BlockSpec((1, 2, tm, f), lambda j, i: (j, 0, i, 0)),
                   pl.BlockSpec((1, f, d), lambda j, i: (j, 0, 0))],
        scratch=[pltpu.VMEM((f, d), jnp.float32)], sem=("parallel", "arbitrary"))(dout, w_d, gu)


def _ffn_dw_gu(fn, dgu):
    s, d = fn.shape
    f = dgu.shape[-1]
    tk = min(4 * FFN_TILE, s)
    nk = s // tk

    def body(fn_ref, dgu_ref, dw_ref, acc):
        k = pl.program_id(2)
        part = _dot_tn(dgu_ref[0, 0], fn_ref[...])

        @pl.when(k == 0)
        def _():
            acc[...] = part

        @pl.when(k > 0)
        def _():
            acc[...] += part

        @pl.when(k == nk - 1)
        def _():
            dw_ref[0, 0] = acc[...].astype(dw_ref.dtype)

    return _pcall(
        body, name="ffn_dw_gate_up", grid=(N_DEV, 2, nk),
        out_shape=_sds((N_DEV, 2, f, d), WIRE),
        in_specs=[pl.BlockSpec((tk, d), lambda j, w, k: (k, 0)),
                  pl.BlockSpec((1, 1, tk, f), lambda j, w, k: (j, w, k, 0))],
        out_specs=pl.BlockSpec((1, 1, f, d), lambda j, w, k: (j, w, 0, 0)),
        scratch=[pltpu.VMEM((f, d), jnp.float32)], sem=("parallel", "parallel", "arbitrary"))(fn, dgu)


def _ffn_dfn(dgu, w_gu, after):
    _, _, s, f = dgu.shape
    d = w_gu.shape[3]
    tm = min(FFN_TILE, s)

    def body(dgu_ref, w_ref, dfn_ref):
        j = pl.program_id(1)
        part = (_dot(dgu_ref[0, 0], w_ref[0, 0]) + _dot(dgu_ref[0, 1], w_ref[0, 1])
                + _dot(dgu_ref[1, 0], w_ref[1, 0]) + _dot(dgu_ref[1, 1], w_ref[1, 1]))

        @pl.when(j == 0)
        def _():
            dfn_ref[...] = part

        @pl.when(j > 0)
        def _():
            dfn_ref[...] += part

    return _pcall(
        body, name="ffn_dfn", grid=(s // tm, N_DEV // 2),
        out_shape=_sds((s, d), jnp.float32),
        in_specs=[pl.BlockSpec((2, 2, tm, f), lambda i, j: (j, 0, i, 0)),
                  pl.BlockSpec((2, 2, f, d), lambda i, j: (j, 0, 0, 0))],
        out_specs=pl.BlockSpec((tm, d), lambda i, j: (i, 0)),
        sem=("parallel", "arbitrary"), after=after)(dgu, w_gu)


def _ffn_norm_bwd(d_fn, dout, h1, g_ffn):
    s, d = h1.shape
    tm = min(2 * ROW_TILE, s)

    def body(dfn_ref, do_ref, h1_ref, g_ref, dh1_ref, dg_ref):
        i = pl.program_id(0)

        @pl.when(i == 0)
        def _():
            dg_ref[...] = jnp.zeros_like(dg_ref)

        _, xn, r = _norm_fwd(h1_ref[...], g_ref[...])
        dx, dg = _norm_bwd(xn, r, g_ref[...], dfn_ref[...])
        dh1_ref[...] = do_ref[...] + dx
        dg_ref[...] += dg

    row = pl.BlockSpec((tm, d), lambda i: (i, 0))
    vec = pl.BlockSpec((1, d), lambda i: (0, 0))
    return _pcall(
        body, name="ffn_norm_bwd", grid=(s // tm,),
        out_shape=[_sds((s, d), jnp.float32), _sds((1, d), jnp.float32)],
        in_specs=[row, row, row, vec], out_specs=[row, vec], sem=("arbitrary",))(d_fn, dout, h1, g_ffn)


def _mem_bwd(proj, g_mq, km, vmm, d_y, y_m, lse):
    s = proj.shape[0]
    ml, hw = km.shape
    tm = min(FFN_TILE, s)
    scale = MEM_DIM ** -0.5

    def body(q_ref, g_ref, k_ref, v_ref, do_ref, y_ref, lse_ref, dq_ref, dk_ref, dv_ref, dg_ref):
        i = pl.program_id(0)

        @pl.when(i == 0)
        def _():
            dk_ref[...] = jnp.zeros_like(dk_ref)
            dv_ref[...] = jnp.zeros_like(dv_ref)
            dg_ref[...] = jnp.zeros_like(dg_ref)

        col = lax.broadcasted_iota(jnp.int32, (tm, MEM_HEADS), 1)
        lse_t = lse_ref[...]
        for h in range(MEM_HEADS):
            sl = slice(MEM_DIM * h, MEM_DIM * (h + 1))
            qn, xn, r = _norm_fwd(q_ref[:, sl], g_ref[...])
            lse_h = jnp.sum(jnp.where(col == h, lse_t, 0.0), -1, keepdims=True)
            p = jnp.exp(_dot_nt(qn, k_ref[:, sl]) * scale - lse_h)
            do = do_ref[:, sl]
            dd = jnp.sum(do * y_ref[:, sl], -1, keepdims=True)
            dp = _dot_nt(do, v_ref[:, sl])
            ds = (p * (dp - dd)).astype(MXU)
            dv_ref[:, sl] += _dot_tn(p, do)
            dk_ref[:, sl] += _dot_tn(ds, qn) * scale
            dx, dg = _norm_bwd(xn, r, g_ref[...], _dot(ds, k_ref[:, sl]) * scale)
            dq_ref[:, sl] = dx.astype(dq_ref.dtype)
            dg_ref[...] += dg

    full = pl.BlockSpec((ml, hw), lambda i: (0, 0))
    return _pcall(
        body, name="mem_bwd", grid=(s // tm,),
        out_shape=[_sds((s, hw), MXU), _sds((ml, hw), jnp.float32), _sds((ml, hw), jnp.float32),
                   _sds((1, MEM_DIM), jnp.float32)],
        in_specs=[pl.BlockSpec((tm, hw), lambda i: (i, C_QM // hw)), pl.BlockSpec((1, MEM_DIM), lambda i: (0, 0)),
                  full, full, pl.BlockSpec((tm, hw), lambda i: (i, 3)), pl.BlockSpec((tm, hw), lambda i: (i, 0)),
                  pl.BlockSpec((tm, MEM_HEADS), lambda i: (i, 0))],
        out_specs=[pl.BlockSpec((tm, hw), lambda i: (i, 0)), full, full,
                   pl.BlockSpec((1, MEM_DIM), lambda i: (0, 0))],
        sem=("arbitrary",))(proj, g_mq, km, vmm, d_y, y_m, lse)


def _memkv_bwd(mem, g_mem, w_mkv, g_mk, kv, memn, dk, dv):
    ml, d = mem.shape
    hw = MEM_HEADS * MEM_DIM

    def body(mem_ref, g_ref, w_ref, gk_ref, kv_ref, mn_ref, dk_ref, dv_ref, dw_ref, dgm_ref, dgk_ref):
        parts = []
        dgk = jnp.zeros((1, MEM_DIM), jnp.float32)
        for h in range(MEM_HEADS):
            sl = slice(MEM_DIM * h, MEM_DIM * (h + 1))
            _, xn, r = _norm_fwd(kv_ref[:, sl], gk_ref[...])
            dx, dg = _norm_bwd(xn, r, gk_ref[...], dk_ref[:, sl])
            parts.append(dx)
            dgk = dgk + dg
        dkv = jnp.concatenate(parts + [dv_ref[...]], axis=1).astype(MXU)
        dgk_ref[...] = dgk
        dw_ref[...] = _dot_tn(mn_ref[...], dkv).astype(dw_ref.dtype)
        d_mn = _dot_nt(dkv, w_ref[...])
        _, xn, _ = _norm_fwd(mem_ref[...], g_ref[...])
        dgm_ref[...] = jnp.sum(d_mn * xn, 0, keepdims=True)

    vm = pl.BlockSpec(memory_space=pltpu.VMEM)
    return _pcall(
        body, name="memkv_bwd",
        out_shape=[_sds((d, 2 * hw), WIRE), _sds((1, d), jnp.float32), _sds((1, MEM_DIM), jnp.float32)],
        in_specs=[vm] * 8, out_specs=[vm] * 3)(mem, g_mem, w_mkv, g_mk, kv, memn, dk, dv)


def _mla_bwd(qc, kc, v, d_y, y_b, lse, after):
    nh, s, _ = qc.shape
    t = min(ATT_TILE, s)
    nb = s // t
    scale = (MLA_NOPE + MLA_ROPE) ** -0.5

    def body(q_ref, k_ref, v_ref, do_ref, y_ref, lse_ref, dq_ref, dk_ref, dv_ref, dk_acc, dv_acc):
        kj, qi = _tri_cols(pl.program_id(1), nb)

        @pl.when((kj == 0) & (qi == 0))
        def _():
            dq_ref[...] = jnp.zeros_like(dq_ref)

        @pl.when(qi == kj)
        def _():
            dk_acc[...] = jnp.zeros_like(dk_acc)
            dv_acc[...] = jnp.zeros_like(dv_acc)

        def step(diagonal):
            rc = t // 4 if diagonal else t
            for c in range(t // rc):
                rows = slice(rc * c, rc * (c + 1))
                keys = slice(0, rc * (c + 1))
                q, k = q_ref[0, rows, :], k_ref[0, keys, :]
                sc = _dot_nt(q, k) * (scale * LOG2E)
                if diagonal:
                    r_i = lax.broadcasted_iota(jnp.int32, sc.shape, 0) + rc * c
                    c_i = lax.broadcasted_iota(jnp.int32, sc.shape, 1)
                    sc = jnp.where(c_i <= r_i, sc, NEG_INF)
                p = jnp.exp2(sc - lse_ref[0, rows, :])
                do = do_ref[rows, :]
                dd = jnp.sum(do * y_ref[rows, :], -1, keepdims=True)
                dp = _dot_nt(do, v_ref[0, keys, :])
                ds = (p * (dp - dd) * scale).astype(MXU)
                dv_acc[keys, :] += _dot_tn(p, do)
                dk_acc[keys, :] += _dot_tn(ds, q)
                out_rows = pl.ds(pl.multiple_of(qi * t + rc * c, rc), rc)
                dq_ref[0, out_rows, :] += _dot(ds, k)

        @pl.when(qi > kj)
        def _():
            step(False)

        @pl.when(qi == kj)
        def _():
            step(True)

        @pl.when(qi == nb - 1)
        def _():
            dk_ref[0] = dk_acc[...]
            dv_ref[0] = dv_acc[...]

    qmap = lambda h, p: (h, _tri_cols(p, nb)[1], 0)
    kmap = lambda h, p: (h, _tri_cols(p, nb)[0], 0)
    return _pcall(
        body, name="mla_bwd", grid=(nh, (nb * (nb + 1)) // 2),
        out_shape=[_sds((nh, s, 256), jnp.float32), _sds((nh, s, 256), jnp.float32),
                   _sds((nh, s, MLA_V), jnp.float32)],
        in_specs=[pl.BlockSpec((1, t, 256), qmap),
                  pl.BlockSpec((1, t, 256), kmap),
                  pl.BlockSpec((1, t, MLA_V), kmap),
                  pl.BlockSpec((t, MLA_V), lambda h, p: (_tri_cols(p, nb)[1], 8 + h)),
                  pl.BlockSpec((t, MLA_V), lambda h, p: (_tri_cols(p, nb)[1], h)),
                  pl.BlockSpec((1, t, 1), qmap)],
        out_specs=[pl.BlockSpec((1, s, 256), lambda h, p: (h, 0, 0)),
                   pl.BlockSpec((1, t, 256), kmap),
                   pl.BlockSpec((1, t, MLA_V), kmap)],
        scratch=[pltpu.VMEM((t, 256), jnp.float32), pltpu.VMEM((t, MLA_V), jnp.float32)],
        sem=("parallel", "arbitrary"), after=after)(qc, kc, v, d_y, y_b, lse)


def _mla_prep_bwd(proj, cos, sin, g_cq, g_ckv, w_uq, w_ukv, g_qn, g_qr, g_kn, g_kr,
                  qb, kvb, cqn, ckvn, dqc, dkc, dv):
    s = proj.shape[0]
    tm = min(2 * ROW_TILE, s)
    nh = MLA_HEADS
    ni = s // tm

    def body(cq_ref, ckv_ref, kr_ref, cos_ref, sin_ref, gcq_ref, gckv_ref, wuq_ref, wukv_ref,
             gqn_ref, gqr_ref, gkn_ref, gkr_ref, qb_ref, kvb_ref, cqn_ref, ckvn_ref, dqc_ref, dkc_ref, dv_ref,
             dcq_ref, dckv_ref, dkr_ref, dwuq_ref, dwukv_ref,
             dgcq_ref, dgckv_ref, dgqn_ref, dgqr_ref, dgkn_ref, dgkr_ref, acc_uq, acc_ukv):
        i = pl.program_id(0)

        @pl.when(i == 0)
        def _():
            acc_uq[...] = jnp.zeros_like(acc_uq)
            acc_ukv[...] = jnp.zeros_like(acc_ukv)
            for ref in (dgcq_ref, dgckv_ref, dgqn_ref, dgqr_ref, dgkn_ref, dgkr_ref):
                ref[...] = jnp.zeros_like(ref)

        cos_t, sin_t = cos_ref[...], sin_ref[...]
        lo = _lo_mask((tm, LANES))
        qb_v, kvb_v = qb_ref[...], kvb_ref[...]
        dq_parts, dgqn = [], jnp.zeros((1, LANES), jnp.float32)
        for h in range(nh):
            _, xn, r = _norm_fwd(qb_v[:, MLA_NOPE * h: MLA_NOPE * (h + 1)], gqn_ref[...])
            dx, dg = _norm_bwd(xn, r, gqn_ref[...], dqc_ref[h][:, :MLA_NOPE])
            dq_parts.append(dx)
            dgqn = dgqn + dg
        dgqn_ref[...] += dgqn
        dgqr = jnp.zeros((1, LANES), jnp.float32)
        for j in range(nh // 2):
            d_rope = jnp.where(lo, dqc_ref[2 * j][:, MLA_NOPE:], dqc_ref[2 * j + 1][:, MLA_NOPE:])
            d_pre = _rope_bwd(d_rope, cos_t, sin_t)
            xr = qb_v[:, nh * MLA_NOPE + LANES * j: nh * MLA_NOPE + LANES * (j + 1)]
            _, xn, r = _norm_fwd(xr, gqr_ref[...], half=True)
            dx, dg = _norm_bwd(xn, r, gqr_ref[...], d_pre, half=True)
            dq_parts.append(dx)
            dgqr = dgqr + dg
        dgqr_ref[...] += dgqr
        dqb = jnp.concatenate(dq_parts, axis=1).astype(MXU)
        acc_uq[...] += _dot_tn(dqb, cqn_ref[...])
        _, xn, r = _norm_fwd(cq_ref[...], gcq_ref[...])
        dx, dg = _norm_bwd(xn, r, gcq_ref[...], _dot(dqb, wuq_ref[...]))
        dcq_ref[...] = dx.astype(dcq_ref.dtype)
        dgcq_ref[...] += dg
        dkv_parts, dgkn = [], jnp.zeros((1, LANES), jnp.float32)
        d_kr2 = jnp.zeros((tm, LANES), jnp.float32)
        for h in range(nh):
            _, xn, r = _norm_fwd(kvb_v[:, 256 * h: 256 * h + MLA_NOPE], gkn_ref[...])
            dx, dg = _norm_bwd(xn, r, gkn_ref[...], dkc_ref[h][:, :MLA_NOPE])
            dkv_parts += [dx, dv_ref[h]]
            dgkn = dgkn + dg
            d_kr2 = d_kr2 + dkc_ref[h][:, MLA_NOPE:]
        dgkn_ref[...] += dgkn
        dkvb = jnp.concatenate(dkv_parts, axis=1).astype(MXU)
        part_ukv = _dot_tn(ckvn_ref[...], dkvb)
        for dev in range(N_DEV):
            acc_ukv[dev] += part_ukv[:, LANES * dev: LANES * (dev + 1)]
        w_ukv_full = jnp.concatenate([wukv_ref[dev] for dev in range(N_DEV)], axis=1)
        d_ckvn = _dot_nt(dkvb, w_ukv_full)
        _, xn, r = _norm_fwd(ckv_ref[...], gckv_ref[...])
        dx, dg = _norm_bwd(xn, r, gckv_ref[...], d_ckvn)
        dckv_ref[...] = dx.astype(dckv_ref.dtype)
        dgckv_ref[...] += dg
        d_kr = jnp.where(lo, d_kr2 + pltpu.roll(d_kr2, 64, 1), 0.0)
        d_pre = _rope_bwd(d_kr, cos_t, sin_t)
        _, xn, r = _norm_fwd(kr_ref[...], gkr_ref[...], half=True)
        dx, dg = _norm_bwd(xn, r, gkr_ref[...], d_pre, half=True)
        dkr_ref[...] = jnp.where(lo, dx, 0.0).astype(dkr_ref.dtype)
        dgkr_ref[...] += jnp.where(_lo_mask((1, LANES)), dg, 0.0)

        @pl.when(i == ni - 1)
        def _():
            dwuq_ref[...] = acc_uq[...].astype(dwuq_ref.dtype)
            dwukv_ref[...] = acc_ukv[...].astype(dwukv_ref.dtype)

    def col(width, start):
        return pl.BlockSpec((tm, width), lambda i: (i, start // width))

    def full(shape):
        return pl.BlockSpec(shape, lambda i: (0,) * len(shape))

    def row(width):
        return pl.BlockSpec((tm, width), lambda i: (i, 0))

    def heads(width):
        return pl.BlockSpec((nh, tm, width), lambda i: (0, i, 0))

    vec = full((1, LANES))
    return _pcall(
        body, name="mla_prep_bwd", grid=(ni,),
        out_shape=[_sds((s, 512), MXU), _sds((s, 512), MXU), _sds((s, LANES), MXU),
                   _sds((768, 512), WIRE), _sds((N_DEV, 512, LANES), WIRE),
                   _sds((1, 512), jnp.float32), _sds((1, 512), jnp.float32)] + [_sds((1, LANES), jnp.float32)] * 4,
        in_specs=[col(512, C_CQ), col(512, C_CKV), col(LANES, C_KR), row(LANES), row(LANES),
                  full((1, 512)), full((1, 512)), full((768, 512)), full((N_DEV, 512, LANES)), vec, vec, vec, vec,
                  row(768), row(1024), row(512), row(512), heads(256), heads(256), heads(MLA_V)],
        out_specs=[row(512), row(512), row(LANES), full((768, 512)), full((N_DEV, 512, LANES)),
                   full((1, 512)), full((1, 512)), vec, vec, vec, vec],
        scratch=[pltpu.VMEM((768, 512), jnp.float32), pltpu.VMEM((N_DEV, 512, LANES), jnp.float32)],
        sem=("arbitrary",))(proj, proj, proj, cos, sin, g_cq, g_ckv, w_uq, w_ukv, g_qn, g_qr, g_kn, g_kr,
                            qb, kvb, cqn, ckvn, dqc, dkc, dv)


def _swa_bwd(proj, posc, posr, gq, gk, sinks, d_y, y_a, lse, after):
    s = proj.shape[0]
    b = SWA_BLOCK
    nb = s // b
    scale = SWA_DIM ** -0.5

    def body(q_ref, kp_ref, kc_ref, vp_ref, vc_ref, pq_ref, pkp_ref, pkc_ref, gq_ref, gk_ref, sink_ref,
             do_ref, y_ref, lse_ref, kfull_ref,
             dq_ref, dk_ref, dv_ref, dgq_ref, dgk_ref, dsink_ref, dk_acc, dv_acc):
        n = pl.program_id(0)

        @pl.when(n == 0)
        def _():
            dk_acc[...] = jnp.zeros_like(dk_acc)
            dv_acc[...] = jnp.zeros_like(dv_acc)
            dgq_ref[...] = jnp.zeros_like(dgq_ref)
            dsink_ref[...] = jnp.zeros_like(dsink_ref)

        kn, v, bias = _swa_common(n, kp_ref[...], kc_ref[...], vp_ref[...], vc_ref[...],
                                  pq_ref[...], pkp_ref[...], pkc_ref[...], gk_ref[...])
        lo = _lo_mask((b, LANES))
        col = lax.broadcasted_iota(jnp.int32, (b, SWA_Q_HEADS), 1)
        col1 = lax.broadcasted_iota(jnp.int32, (1, SWA_Q_HEADS), 1)
        lse_t = lse_ref[...]
        dk_blk = jnp.zeros((2 * b, LANES), jnp.float32)
        dv_blk = jnp.zeros((2 * b, LANES), jnp.float32)
        dgq = jnp.zeros((1, LANES), jnp.float32)
        dsink = jnp.zeros((1, SWA_Q_HEADS), jnp.float32)
        for j in range(SWA_Q_HEADS // 2):
            hk = (2 * j) // (SWA_Q_HEADS // SWA_KV_HEADS)
            kvmask = lo if hk == 0 else jnp.logical_not(lo)
            sl = slice(LANES * j, LANES * (j + 1))
            qn, xn, r = _norm_fwd(q_ref[:, sl], gq_ref[...], half=True)
            qn = qn * scale
            qsw = pltpu.roll(qn, 64, 1)
            d2 = do_ref[:, sl]
            d2sw = pltpu.roll(d2, 64, 1)
            prod = d2 * y_ref[:, sl]
            dqs = []
            for e in range(2):
                h = 2 * j + e
                half_e = lo if e == 0 else jnp.logical_not(lo)
                qm = jnp.where(kvmask, qn if e == hk else qsw, 0.0)
                dm = jnp.where(kvmask, d2 if e == hk else d2sw, 0.0)
                sc = _dot_nt(qm, kn) + _alibi_slope(h) * bias
                lse_h = jnp.sum(jnp.where(col == h, lse_t, 0.0), -1, keepdims=True)
                p = jnp.exp(sc - lse_h)
                dd = jnp.sum(jnp.where(half_e, prod, 0.0), -1, keepdims=True)
                dp = _dot_nt(dm, v)
                ds = (p * (dp - dd)).astype(MXU)
                dsink = dsink - jnp.where(col1 == h, jnp.sum(jnp.exp(sink_ref[h] - lse_h) * dd), 0.0)
                dq_m = _dot(ds, kn) * scale
                dk_blk = dk_blk + _dot_tn(ds, qm)
                dv_blk = dv_blk + _dot_tn(p, dm)
                dqs.append(dq_m if e == hk else pltpu.roll(dq_m, 64, 1))
            dx, dg = _norm_bwd(xn, r, gq_ref[...], jnp.where(lo, dqs[0], dqs[1]), half=True)
            dq_ref[:, sl] = dx.astype(dq_ref.dtype)
            dgq = dgq + dg
        dgq_ref[...] += dgq
        dsink_ref[...] += dsink
        prev = pl.ds(pl.multiple_of(jnp.maximum(n - 1, 0) * b, b), b)
        cur = pl.ds(pl.multiple_of(n * b, b), b)
        dk_acc[prev, :] += dk_blk[:b]
        dv_acc[prev, :] += dv_blk[:b]
        dk_acc[cur, :] += dk_blk[b:]
        dv_acc[cur, :] += dv_blk[b:]

        @pl.when(n == nb - 1)
        def _():
            _, kxn, kr = _norm_fwd(kfull_ref[...], gk_ref[...], half=True)
            dx, dg = _norm_bwd(kxn, kr, gk_ref[...], dk_acc[...], half=True)
            dk_ref[...] = dx.astype(dk_ref.dtype)
            dv_ref[...] = dv_acc[...].astype(dv_ref.dtype)
            dgk_ref[...] = dg

    full = pl.BlockSpec((s, LANES), lambda n: (0, 0))
    vec = pl.BlockSpec((1, LANES), lambda n: (0, 0))
    return _pcall(
        body, name="swa_bwd", grid=(nb,),
        out_shape=[_sds((s, 1024), MXU), _sds((s, LANES), MXU), _sds((s, LANES), MXU),
                   _sds((1, LANES), jnp.float32), _sds((1, LANES), jnp.float32),
                   _sds((1, SWA_Q_HEADS), jnp.float32)],
        in_specs=_swa_specs(s) + [pl.BlockSpec((b, 1024), lambda n: (n, 0)), pl.BlockSpec((b, 1024), lambda n: (n, 0)),
                                  pl.BlockSpec((b, SWA_Q_HEADS), lambda n: (n, 0)),
                                  pl.BlockSpec((s, LANES), lambda n: (0, C_KA // LANES))],
        out_specs=[pl.BlockSpec((b, 1024), lambda n: (n, 0)), full, full, vec, vec,
                   pl.BlockSpec((1, SWA_Q_HEADS), lambda n: (0, 0))],
        scratch=[pltpu.VMEM((s, LANES), jnp.float32), pltpu.VMEM((s, LANES), jnp.float32)],
        sem=("arbitrary",), after=after)(proj, proj, proj, proj, proj, posc, posr, posr, gq, gk, sinks, d_y, y_a, lse,
                                         proj)


def _dx(d_proj, w_in, x, g, d_h1, after):
    s, d = x.shape
    n = w_in.shape[0]
    tm = min(2 * ROW_TILE, s)

    n_pc = len(d_proj)

    def body(*refs):
        dp_refs, (w_ref, x_ref, g_ref, dh_ref, dx_ref, dg_ref) = refs[:n_pc], refs[n_pc:]
        i = pl.program_id(0)

        @pl.when(i == 0)
        def _():
            dg_ref[...] = jnp.zeros_like(dg_ref)

        d_hn = _dot(jnp.concatenate([r[...] for r in dp_refs], axis=1), w_ref[...])
        _, xn, r = _norm_fwd(x_ref[...], g_ref[...])
        dx, dg = _norm_bwd(xn, r, g_ref[...], d_hn)
        dx_ref[...] = dh_ref[...] + dx
        dg_ref[...] += dg

    row = pl.BlockSpec((tm, d), lambda i: (i, 0))
    vec = pl.BlockSpec((1, d), lambda i: (0, 0))
    return _pcall(
        body, name="grad_x", grid=(s // tm,),
        out_shape=[_sds((s, d), jnp.float32), _sds((1, d), jnp.float32)],
        in_specs=[pl.BlockSpec((tm, p.shape[1]), lambda i: (i, 0)) for p in d_proj] + [
                  pl.BlockSpec((n, d), lambda i: (0, 0), pipeline_mode=pl.Buffered(1)), row, vec, row],
        out_specs=[row, vec], sem=("arbitrary",), after=after)(*d_proj, w_in, x, g, d_h1)


_SMALL = ["attn_norm_g", "swa_q_norm_g", "swa_k_norm_g", "swa_sinks", "mla_cq_norm_g", "mla_ckv_norm_g",
          "mla_qn_norm_g", "mla_qr_norm_g", "mla_kn_norm_g", "mla_kr_norm_g", "mem_norm_g",
          "mem_q_norm_g", "mem_k_norm_g", "ffn_norm_g"]


def kernel(x, mem, positions, attn_norm_g, w_in, swa_q_norm_g, swa_k_norm_g, swa_sinks, mla_cq_norm_g, mla_ckv_norm_g, w_uq, w_ukv, mla_qn_norm_g, mla_qr_norm_g, mla_kn_norm_g, mla_kr_norm_g, mem_norm_g, w_mem_kv, mem_q_norm_g, mem_k_norm_g, w_out, ffn_norm_g, w_gate, w_up, w_down, loss_target, m_attn_norm_g, m_w_in, m_swa_q_norm_g, m_swa_k_norm_g, m_swa_sinks, m_mla_cq_norm_g, m_mla_ckv_norm_g, m_w_uq, m_w_ukv, m_mla_qn_norm_g, m_mla_qr_norm_g, m_mla_kn_norm_g, m_mla_kr_norm_g, m_mem_norm_g, m_w_mem_kv, m_mem_q_norm_g, m_mem_k_norm_g, m_w_out, m_ffn_norm_g, m_w_gate, m_w_up, m_w_down, v_attn_norm_g, v_w_in, v_swa_q_norm_g, v_swa_k_norm_g, v_swa_sinks, v_mla_cq_norm_g, v_mla_ckv_norm_g, v_w_uq, v_w_ukv, v_mla_qn_norm_g, v_mla_qr_norm_g, v_mla_kn_norm_g, v_mla_kr_norm_g, v_mem_norm_g, v_w_mem_kv, v_mem_q_norm_g, v_mem_k_norm_g, v_w_out, v_ffn_norm_g, v_w_gate, v_w_up, v_w_down):
    args = dict(locals())
    x2, mem2, tgt = x[0], mem[0], loss_target[0]
    s, d = x2.shape
    n_in = w_in.shape[2]
    f = w_gate.shape[2]

    in_shards = [w_in[0].T.astype(WIRE)]
    (g_in,) = _all_gather_background(in_shards, 7, "all_gather_in_weights")
    tok = in_shards[0]
    mix_shards = [w_uq[0].T.astype(WIRE), w_ukv[0].astype(WIRE), w_mem_kv[0].astype(WIRE),
                  _to_wire([w_out[0]], tok, "wire_out")[0]]
    g_uq, wkv, g_mkv, g_out = _all_gather_background(mix_shards, 5, "all_gather_mix_weights")
    ffn_shards = [_to_wire([w_gate[0].T, w_up[0].T], tok, "wire_gate_up")]
    (w_gu,) = _all_gather_background(ffn_shards, 1, "all_gather_ffn_weights")
    down_shards = [_to_wire([w_down[0]], tok, "wire_down")[0]]
    (w_d,) = _all_gather_background(down_shards, 6, "all_gather_down_weights")
    wi = g_in.reshape(N_DEV * n_in, d)
    wi = jnp.concatenate([wi[0:1024], wi[1280:1792], wi[1792:2304], wi[2368:2880],
                          wi[1024:1152], wi[1152:1280], wi[2304:2368],
                          jnp.zeros((IN_PAD - 2880, d), wi.dtype)], axis=0)
    wq = g_uq.reshape(768, 512)
    wq = jnp.concatenate([wq[192 * h: 192 * h + 128] for h in range(4)]
                         + [wq[192 * h + 128: 192 * (h + 1)] for h in range(4)], axis=0)
    wmkv = g_mkv.reshape(-1, g_mkv.shape[-1])
    wo = g_out.reshape(-1, d)

    pos = positions[0].astype(jnp.float32)
    inv_freq = ROPE_THETA ** (-jnp.arange(0, MLA_ROPE, 2, dtype=jnp.float32) / MLA_ROPE)
    ang = pos[:, None] * inv_freq
    cos32, sin32 = jnp.cos(ang), jnp.sin(ang)
    cos_t = jnp.tile(cos32, (1, 4))
    sin_t = jnp.tile(jnp.concatenate([-sin32, sin32], axis=1), (1, 2))
    posc, posr = pos.reshape(s, 1), pos.reshape(1, s)
    two = lambda g: jnp.tile(g, (1, 2))
    gq2, gk2, gqr2, gkr2 = two(swa_q_norm_g), two(swa_k_norm_g), two(mla_qr_norm_g), two(mla_kr_norm_g)
    sinks1 = swa_sinks[0]

    hn = _norm_rows(x2, attn_norm_g)
    proj = _mm(hn, wi, tb=True, out_dtype=jnp.float32, tm=FFN_TILE, tk=d, name="in_proj")
    qc, kc, vb, qb, kvb, cqn, ckvn = _mla_prep(proj, cos_t, sin_t, mla_cq_norm_g, mla_ckv_norm_g, wq, wkv,
                                                mla_qn_norm_g, gqr2, mla_kn_norm_g, gkr2)
    y_b, lse_b = _mla_fwd(qc, kc, vb)
    km, vmm, kvm, memn = _memkv_prep(mem2, mem_norm_g, wmkv, mem_k_norm_g)
    y_m, lse_m = _mem_fwd(proj, mem_q_norm_g, km, vmm)
    y_a, lse_a = _swa_fwd(proj, posc, posr, gq2, gk2, sinks1)
    h1, fn = _out_proj(y_a, y_b, y_m, x2, wo, ffn_norm_g)
    gu, act = _ffn_gu(fn, w_gu)
    dout, dout_b, loss_tile = _ffn_down(act, w_d, h1, tgt)

    dgu, dw_d = _ffn_bwd_act(dout_b, w_d, gu)
    (r_d,) = _exchange_grads_background([dw_d], 8, "exchange_down_grads")
    dw_gu = _ffn_dw_gu(fn, dgu)
    (r_gu,) = _exchange_grads_background([dw_gu], 2, "exchange_ffn_grads")
    d_h1, dg_ffn = _ffn_norm_bwd(_ffn_dfn(dgu, w_gu, dw_gu), dout, h1, ffn_norm_g)
    d_y = _mm(d_h1, wo, tb=True, out_dtype=jnp.float32, tm=FFN_TILE, tk=2048, name="d_mix")
    dw_out = jnp.concatenate([
        _mm(y_a, d_h1, ta=True, out_dtype=WIRE, tm=1024, tk=1024, name="dw_out_a"),
        _mm(y_b, d_h1, ta=True, out_dtype=WIRE, tm=1024, tk=1024, name="dw_out_b"),
        _mm(y_m, d_h1, ta=True, out_dtype=WIRE, tm=1024, tk=1024, name="dw_out_m")], axis=0)
    d_qm, dkm, dvmm, dg_mq = _mem_bwd(proj, mem_q_norm_g, km, vmm, d_y, y_m, lse_m)
    dw_mkv, dg_mem, dg_mk = _memkv_bwd(mem2, mem_norm_g, wmkv, mem_k_norm_g, kvm, memn, dkm, dvmm)
    r_mkv, r_out = _exchange_grads_background([dw_mkv.reshape(g_mkv.shape), dw_out.reshape(g_out.shape)], 3,
                                              "exchange_mix_grads")
    dqc, dkc, dvb = _mla_bwd(qc, kc, vb, d_y, y_b, lse_b, dw_mkv)
    (d_cq, d_ckv, d_kr, dw_uq, dw_ukv, dg_cq, dg_ckv, dg_qn, dg_qr, dg_kn, dg_kr) = _mla_prep_bwd(
        proj, cos_t, sin_t, mla_cq_norm_g, mla_ckv_norm_g, wq, wkv, mla_qn_norm_g, gqr2, mla_kn_norm_g, gkr2,
        qb, kvb, cqn, ckvn, dqc, dkc, dvb)
    d_qa, d_ka, d_va, dg_q, dg_k, d_sinks = _swa_bwd(proj, posc, posr, gq2, gk2, sinks1, d_y, y_a, lse_a, dw_out)
    d_proj = [d_qa, d_cq, d_ckv, d_qm, d_ka, d_va, d_kr]
    gi = _dw_in(hn, d_proj, n_in, None)

    gq_ = jnp.concatenate(sum([[dw_uq[128 * h: 128 * (h + 1)], dw_uq[512 + 64 * h: 512 + 64 * (h + 1)]]
                               for h in range(4)], []), axis=0)
    gq_ = gq_.reshape(N_DEV, 96, 512)
    r_in, r_uq, r_ukv = _exchange_grads_background([gi, gq_, dw_ukv], 4, "exchange_in_grads")
    grad_x, dg_attn = _dx(d_proj, wi, x2, attn_norm_g, d_h1, gi)
    small_g = {
        "attn_norm_g": dg_attn, "swa_q_norm_g": dg_q, "swa_k_norm_g": dg_k,
        "swa_sinks": d_sinks, "mla_cq_norm_g": dg_cq, "mla_ckv_norm_g": dg_ckv, "mla_qn_norm_g": dg_qn,
        "mla_qr_norm_g": dg_qr, "mla_kn_norm_g": dg_kn, "mla_kr_norm_g": dg_kr,
        "mem_norm_g": dg_mem, "mem_q_norm_g": dg_mq, "mem_k_norm_g": dg_mk, "ffn_norm_g": dg_ffn}
    pack = _small_pack([small_g[n] for n in _SMALL], loss_tile, [args[n].shape[-1] for n in _SMALL])
    pack8 = jnp.broadcast_to(pack, (N_DEV,) + pack.shape[1:])
    (packs,) = _exchange_grads_background([pack8], 9, "exchange_small_grads")

    big = {}
    last = [None]

    def adam(name, r, transposed=False, which=None):
        w, m, v = args[name][0], args["m_" + name][0], args["v_" + name][0]
        if transposed:
            outs = _adam_big(r, w.T, m.T, v.T, "adam_" + name, last[0], which)
            big[name] = [o.T[None] for o in outs]
        else:
            outs = _adam_big(r, w, m, v, "adam_" + name, last[0])
            big[name] = [o[None] for o in outs]
        last[0] = outs[0]

    adam("w_down", r_d)
    adam("w_gate", r_gu, True, which=0)
    adam("w_up", r_gu, True, which=1)
    adam("w_out", r_out)
    adam("w_mem_kv", r_mkv)
    adam("w_in", r_in, True)
    adam("w_uq", r_uq, True)
    adam("w_ukv", r_ukv)

    loss11, small_out = _small_adam(packs, [args[n] for n in _SMALL], [args["m_" + n] for n in _SMALL],
                                    [args["v_" + n] for n in _SMALL], last[0])
    small = dict(zip(_SMALL, small_out))
    loss = loss11.reshape(())

    order = ["attn_norm_g", "w_in", "swa_q_norm_g", "swa_k_norm_g", "swa_sinks", "mla_cq_norm_g", "mla_ckv_norm_g",
             "w_uq", "w_ukv", "mla_qn_norm_g", "mla_qr_norm_g", "mla_kn_norm_g", "mla_kr_norm_g", "mem_norm_g",
             "w_mem_kv", "mem_q_norm_g", "mem_k_norm_g", "w_out", "ffn_norm_g", "w_gate", "w_up", "w_down"]
    res = {n: (big[n] if n in big else list(small[n])) for n in order}
    outs = [loss, grad_x[None]]
    for kind in range(4):
        outs += [res[n][kind] for n in order]
    return tuple(outs)
```
